```python
import jax, jax.numpy as jnp
from jax import lax
import numpy as np


D_MODEL = 1024
BATCH = 8
SEQ = 4096
DEPTH = 4

N_META = 16
CHUNK = 128
PAD = CHUNK - N_META
EPS = 1e-6
SSD_HEADS = 16
SSD_HEAD_DIM = 64
SSD_D_INNER = SSD_HEADS * SSD_HEAD_DIM
SSD_GROUPS = 2
SSD_HEADS_PER_GROUP = SSD_HEADS // SSD_GROUPS
SSD_STATE = 128
SSD_CONV = 4
SSD_CONV_CH = SSD_D_INNER + 2 * SSD_GROUPS * SSD_STATE
MLA_HEADS = 16
MLA_NOPE = 64
MLA_ROPE = 32
MLA_V = 64
MLA_Q_RANK = 384
MLA_KV_RANK = 256
ROPE_BASE = 10000.0
IN_SPLITS = (SSD_D_INNER, SSD_CONV_CH, SSD_HEADS, MLA_Q_RANK, MLA_KV_RANK, MLA_ROPE)
D_IN = sum(IN_SPLITS)
AB_WIDTH = SSD_D_INNER + MLA_HEADS * MLA_V
LRU_WIDTH = 1280
LRU_BLOCKS = 10
LRU_BLOCK = LRU_WIDTH // LRU_BLOCKS
LRU_CONV = 4
LRU_C = 8.0
D_FF = 4 * D_MODEL
N_EVEN = (DEPTH + 1) // 2
N_ODD = DEPTH // 2

kernel_name = 'hybrid_ssd_mla_rglru_sandwich_meta'


def rmsnorm(x, g):
    xf = x.astype(jnp.float32)
    y = xf * lax.rsqrt(jnp.mean(xf * xf, axis=-1, keepdims=True) + EPS)
    return (y * g.astype(jnp.float32)).astype(x.dtype)


def _split(x, sizes):
    offs = np.cumsum(sizes)[:-1].tolist()
    return jnp.split(x, offs, axis=-1)


def pad_front(x, n):
    return jnp.pad(x, [(0, 0), (n, 0)] + [(0, 0)] * (x.ndim - 2))


def causal_dwconv(x, w, b):
    k, t = w.shape[0], x.shape[1]
    xp = jnp.pad(x, ((0, 0), (k - 1, 0), (0, 0)))
    out = b
    for i in range(k):
        out = out + xp[:, i:i + t] * w[i]
    return out


def rope_tables(t, dim):
    inv = ROPE_BASE ** (-jnp.arange(0, dim, 2, dtype=jnp.float32) / dim)
    ang = jnp.arange(t, dtype=jnp.float32)[:, None] * inv[None, :]
    return jnp.cos(ang), jnp.sin(ang)


def apply_rope(x, cos, sin):
    half = x.shape[-1] // 2
    x1, x2 = x[..., :half], x[..., half:]
    return jnp.concatenate([x1 * cos - x2 * sin, x2 * cos + x1 * sin], axis=-1).astype(x.dtype)


def ssd_chunked_scan(xdt, da, bm, cm):
    b, tp = xdt.shape[:2]
    nc = tp // CHUNK
    g, k = SSD_GROUPS, SSD_HEADS_PER_GROUP
    x = xdt.reshape(b, nc, CHUNK, g, k, SSD_HEAD_DIM)
    a = da.reshape(b, nc, CHUNK, g, k).transpose(0, 1, 3, 4, 2)
    bc = bm.reshape(b, nc, CHUNK, g, SSD_STATE)
    cc = cm.reshape(b, nc, CHUNK, g, SSD_STATE)
    a_cs = jnp.cumsum(a, axis=-1)
    causal = jnp.tril(jnp.ones((CHUNK, CHUNK), dtype=bool))
    seg = a_cs[..., :, None] - a_cs[..., None, :]
    decay_in = jnp.exp(jnp.where(causal, seg, -jnp.inf))
    cb = jnp.einsum('bclgn,bcsgn->bcgls', cc, bc)
    y_diag = jnp.einsum('bcgkls,bcsgkp->bclgkp', cb[:, :, :, None] * decay_in, x)
    decay_to_end = jnp.exp(a_cs[..., -1:] - a_cs)
    states = jnp.einsum('bclgn,bcgkl,bclgkp->bcgkpn', bc, decay_to_end, x).astype(jnp.float32)
    chunk_decay = jnp.exp(a_cs[..., -1])

    def step(h, inp):
        dec, st = inp
        return dec[..., None, None] * h + st, h

    h0 = jnp.zeros((b, g, k, SSD_HEAD_DIM, SSD_STATE), jnp.float32)
    _, prev = lax.scan(step, h0, (jnp.moveaxis(chunk_decay, 1, 0), jnp.moveaxis(states, 1, 0)))
    prev = jnp.moveaxis(prev, 0, 1)
    y_off = jnp.einsum('bclgn,bcgkpn,bcgkl->bclgkp', cc.astype(jnp.float32), prev, jnp.exp(a_cs))
    y = y_diag.astype(jnp.float32) + y_off
    return y.reshape(b, tp, SSD_HEADS, SSD_HEAD_DIM).astype(xdt.dtype)


def ssd_branch(z, xbc, dt_raw, conv_w, conv_b, dt_bias, a_log, d_skip, norm_g):
    b, t, _ = z.shape
    xbc = jax.nn.silu(causal_dwconv(xbc, conv_w, conv_b))
    xs, bm, cm = _split(xbc, (SSD_D_INNER, SSD_GROUPS * SSD_STATE, SSD_GROUPS * SSD_STATE))
    dt = jax.nn.softplus(dt_raw.astype(jnp.float32) + dt_bias.astype(jnp.float32))
    a = -jnp.exp(a_log.astype(jnp.float32))
    xh = xs.reshape(b, t, SSD_HEADS, SSD_HEAD_DIM)
    xdt = pad_front(xh * dt[..., None].astype(xh.dtype), PAD)
    da = pad_front(dt * a, PAD)
    bm = pad_front(bm.reshape(b, t, SSD_GROUPS, SSD_STATE), PAD)
    cm = pad_front(cm.reshape(b, t, SSD_GROUPS, SSD_STATE), PAD)
    y = ssd_chunked_scan(xdt, da, bm, cm)[:, PAD:]
    y = y + d_skip[:, None] * xh
    y = y.reshape(b, t, SSD_D_INNER)
    return rmsnorm(y * jax.nn.silu(z), norm_g)


def mla_branch(cq, ckv, krope, q_norm_g, w_q_up, kv_norm_g, w_kv_up, cos, sin):
    b, t, _ = cq.shape
    q = (rmsnorm(cq, q_norm_g) @ w_q_up).reshape(b, t, MLA_HEADS, MLA_NOPE + MLA_ROPE)
    q = jnp.concatenate([q[..., :MLA_NOPE], apply_rope(q[..., MLA_NOPE:], cos[:, None], sin[:, None])], axis=-1)
    kv = (rmsnorm(ckv, kv_norm_g) @ w_kv_up).reshape(b, t, MLA_HEADS, MLA_NOPE + MLA_V)
    k_r = apply_rope(krope, cos, sin)
    k = jnp.concatenate([kv[..., :MLA_NOPE],
                         jnp.broadcast_to(k_r[:, :, None], (b, t, MLA_HEADS, MLA_ROPE))], axis=-1)
    v = kv[..., MLA_NOPE:]
    qp, kp, vp = pad_front(q, PAD), pad_front(k, PAD), pad_front(v, PAD)
    tp = t + PAD
    nb = tp // CHUNK
    scale = (MLA_NOPE + MLA_ROPE) ** -0.5
    kidx = jnp.arange(tp)

    def block(j):
        q_blk = lax.dynamic_slice_in_dim(qp, j * CHUNK, CHUNK, axis=1)
        s = jnp.einsum('bqhd,bkhd->bhqk', q_blk, kp).astype(jnp.float32) * scale
        qidx = j * CHUNK + jnp.arange(CHUNK)
        mask = (kidx[None, :] <= qidx[:, None]) & (kidx[None, :] >= PAD)
        s = jnp.where(mask, s, -1e30)
        p = jax.nn.softmax(s, axis=-1).astype(vp.dtype)
        return jnp.einsum('bhqk,bkhd->bqhd', p, vp)

    o = lax.map(block, jnp.arange(nb))
    o = o.transpose(1, 0, 2, 3, 4).reshape(b, tp, MLA_HEADS * MLA_V)
    return o[:, PAD:]


def mixer_ssd_mla(h, w_in, conv_w, conv_b, dt_bias, a_log, d_skip, ssd_norm_g,
                  q_norm_g, w_q_up, kv_norm_g, w_kv_up, w_out, cos, sin):
    z, xbc, dt_raw, cq, ckv, krope = _split(h @ w_in, IN_SPLITS)
    y_ssd = ssd_branch(z, xbc, dt_raw, conv_w, conv_b, dt_bias, a_log, d_skip, ssd_norm_g)
    y_att = mla_branch(cq, ckv, krope, q_norm_g, w_q_up, kv_norm_g, w_kv_up, cos, sin)
    return jnp.concatenate([y_ssd, y_att], axis=-1) @ w_out


def _lru_combine(c1, c2):
    a1, b1 = c1
    a2, b2 = c2
    return a1 * a2, a2 * b1 + b2


def mixer_rglru(h, w_x, w_y, conv_w, conv_b, w_a, b_a, w_i, b_i, lam, w_out):
    b, t, _ = h.shape
    gate = jax.nn.gelu(h @ w_y)
    xr = causal_dwconv(h @ w_x, conv_w, conv_b)
    xb = xr.reshape(b, t, LRU_BLOCKS, LRU_BLOCK)
    r = jax.nn.sigmoid(jnp.einsum('btni,nij->btnj', xb, w_a).reshape(b, t, LRU_WIDTH) + b_a)
    i = jax.nn.sigmoid(jnp.einsum('btni,nij->btnj', xb, w_i).reshape(b, t, LRU_WIDTH) + b_i)
    log_a = -LRU_C * r.astype(jnp.float32) * jax.nn.softplus(-lam.astype(jnp.float32))
    a = jnp.exp(log_a)
    u = jnp.sqrt(-jnp.expm1(2.0 * log_a)) * (i * xr).astype(jnp.float32)
    _, hs = lax.associative_scan(_lru_combine, (a, u), axis=1)
    return (hs.astype(h.dtype) * gate) @ w_out


def _normal(k, shape, fan_in):
    return jax.random.normal(k, shape, jnp.float32) * (fan_in ** -0.5)


def _fwd_setup_inputs(seed: int = 0) -> dict:
    key = jax.random.key(seed)
    ks = iter(jax.random.split(key, 40))
    gain = lambda k, shape: 1.0 + 0.02 * jax.random.normal(k, shape, jnp.float32)
    small = lambda k, shape: 0.02 * jax.random.normal(k, shape, jnp.float32)
    x = jax.random.normal(next(ks), (BATCH, SEQ, D_MODEL), jnp.float32)
    meta_tokens = jax.random.normal(next(ks), (N_META, D_MODEL), jnp.float32)
    mix_pre_g = gain(next(ks), (DEPTH, D_MODEL))
    mix_post_g = gain(next(ks), (DEPTH, D_MODEL))
    mlp_pre_g = gain(next(ks), (DEPTH, D_MODEL))
    mlp_post_g = gain(next(ks), (DEPTH, D_MODEL))
    w_up = _normal(next(ks), (DEPTH, D_MODEL, D_FF), D_MODEL)
    w_down = _normal(next(ks), (DEPTH, D_FF, D_MODEL), D_FF)
    w_in = _normal(next(ks), (N_EVEN, D_MODEL, D_IN), D_MODEL)
    ssd_conv_w = _normal(next(ks), (N_EVEN, SSD_CONV, SSD_CONV_CH), SSD_CONV)
    ssd_conv_b = small(next(ks), (N_EVEN, SSD_CONV_CH))
    dt = jnp.exp(jax.random.uniform(next(ks), (N_EVEN, SSD_HEADS), jnp.float32,
                                    minval=np.log(1e-3), maxval=np.log(1e-1)))
    ssd_dt_bias = dt + jnp.log(-jnp.expm1(-dt))
    ssd_a_log = jnp.log(jax.random.uniform(next(ks), (N_EVEN, SSD_HEADS), jnp.float32, minval=1.0, maxval=16.0))
    ssd_d = 1.0 + 0.1 * jax.random.normal(next(ks), (N_EVEN, SSD_HEADS), jnp.float32)
    ssd_norm_g = gain(next(ks), (N_EVEN, SSD_D_INNER))
    mla_q_norm_g = gain(next(ks), (N_EVEN, MLA_Q_RANK))
    mla_w_q_up = _normal(next(ks), (N_EVEN, MLA_Q_RANK, MLA_HEADS * (MLA_NOPE + MLA_ROPE)), MLA_Q_RANK)
    mla_kv_norm_g = gain(next(ks), (N_EVEN, MLA_KV_RANK))
    mla_w_kv_up = _normal(next(ks), (N_EVEN, MLA_KV_RANK, MLA_HEADS * (MLA_NOPE + MLA_V)), MLA_KV_RANK)
    w_out_ab = _normal(next(ks), (N_EVEN, AB_WIDTH, D_MODEL), AB_WIDTH)
    rg_w_x = _normal(next(ks), (N_ODD, D_MODEL, LRU_WIDTH), D_MODEL)
    rg_w_y = _normal(next(ks), (N_ODD, D_MODEL, LRU_WIDTH), D_MODEL)
    rg_conv_w = _normal(next(ks), (N_ODD, LRU_CONV, LRU_WIDTH), LRU_CONV)
    rg_conv_b = small(next(ks), (N_ODD, LRU_WIDTH))
    rg_w_a = _normal(next(ks), (N_ODD, LRU_BLOCKS, LRU_BLOCK, LRU_BLOCK), LRU_BLOCK)
    rg_b_a = small(next(ks), (N_ODD, LRU_WIDTH))
    rg_w_i = _normal(next(ks), (N_ODD, LRU_BLOCKS, LRU_BLOCK, LRU_BLOCK), LRU_BLOCK)
    rg_b_i = small(next(ks), (N_ODD, LRU_WIDTH))
    a8 = jax.random.uniform(next(ks), (N_ODD, LRU_WIDTH), jnp.float32, minval=0.9, maxval=0.999)
    base = a8 ** (1.0 / LRU_C)
    rg_lambda = jnp.log(base) - jnp.log1p(-base)
    rg_w_out = _normal(next(ks), (N_ODD, LRU_WIDTH, D_MODEL), LRU_WIDTH)
    return {'x': x, 'meta_tokens': meta_tokens, 'mix_pre_g': mix_pre_g, 'mix_post_g': mix_post_g,
            'mlp_pre_g': mlp_pre_g, 'mlp_post_g': mlp_post_g, 'w_up': w_up, 'w_down': w_down,
            'w_in': w_in, 'ssd_conv_w': ssd_conv_w, 'ssd_conv_b': ssd_conv_b, 'ssd_dt_bias': ssd_dt_bias,
            'ssd_a_log': ssd_a_log, 'ssd_d': ssd_d, 'ssd_norm_g': ssd_norm_g,
            'mla_q_norm_g': mla_q_norm_g, 'mla_w_q_up': mla_w_q_up, 'mla_kv_norm_g': mla_kv_norm_g,
            'mla_w_kv_up': mla_w_kv_up, 'w_out_ab': w_out_ab, 'rg_w_x': rg_w_x, 'rg_w_y': rg_w_y,
            'rg_conv_w': rg_conv_w, 'rg_conv_b': rg_conv_b, 'rg_w_a': rg_w_a, 'rg_b_a': rg_b_a,
            'rg_w_i': rg_w_i, 'rg_b_i': rg_b_i, 'rg_lambda': rg_lambda, 'rg_w_out': rg_w_out}


def _fwd_reference(x, meta_tokens, mix_pre_g, mix_post_g, mlp_pre_g, mlp_post_g, w_up, w_down,
              w_in, ssd_conv_w, ssd_conv_b, ssd_dt_bias, ssd_a_log, ssd_d, ssd_norm_g,
              mla_q_norm_g, mla_w_q_up, mla_kv_norm_g, mla_w_kv_up, w_out_ab,
              rg_w_x, rg_w_y, rg_conv_w, rg_conv_b, rg_w_a, rg_b_a, rg_w_i, rg_b_i, rg_lambda, rg_w_out):
    b = x.shape[0]
    meta = jnp.broadcast_to(meta_tokens[None].astype(x.dtype), (b, N_META, D_MODEL))
    h = jnp.concatenate([meta, x], axis=1)
    cos, sin = rope_tables(h.shape[1], MLA_ROPE)
    for layer in range(DEPTH):
        hn = rmsnorm(h, mix_pre_g[layer])
        if layer % 2 == 0:
            e = layer // 2
            m = mixer_ssd_mla(hn, w_in[e], ssd_conv_w[e], ssd_conv_b[e], ssd_dt_bias[e], ssd_a_log[e],
                              ssd_d[e], ssd_norm_g[e], mla_q_norm_g[e], mla_w_q_up[e], mla_kv_norm_g[e],
                              mla_w_kv_up[e], w_out_ab[e], cos, sin)
        else:
            o = layer // 2
            m = mixer_rglru(hn, rg_w_x[o], rg_w_y[o], rg_conv_w[o], rg_conv_b[o], rg_w_a[o], rg_b_a[o],
                            rg_w_i[o], rg_b_i[o], rg_lambda[o], rg_w_out[o])
        h = h + rmsnorm(m, mix_post_g[layer])
        hn = rmsnorm(h, mlp_pre_g[layer])
        u = jnp.square(jax.nn.relu(hn @ w_up[layer]))
        h = h + rmsnorm(u @ w_down[layer], mlp_post_g[layer])
    return h[:, N_META:]


import jax as _jax
import jax.numpy as _jnp

TWIN_FORMAT = 'train_step'
FWD_PARAMS = ['x', 'meta_tokens', 'mix_pre_g', 'mix_post_g', 'mlp_pre_g', 'mlp_post_g', 'w_up', 'w_down', 'w_in', 'ssd_conv_w', 'ssd_conv_b', 'ssd_dt_bias', 'ssd_a_log', 'ssd_d', 'ssd_norm_g', 'mla_q_norm_g', 'mla_w_q_up', 'mla_kv_norm_g', 'mla_w_kv_up', 'w_out_ab', 'rg_w_x', 'rg_w_y', 'rg_conv_w', 'rg_conv_b', 'rg_w_a', 'rg_b_a', 'rg_w_i', 'rg_b_i', 'rg_lambda', 'rg_w_out']
TWIN_WEIGHTS = ['meta_tokens', 'mix_pre_g', 'mix_post_g', 'mlp_pre_g', 'mlp_post_g', 'w_up', 'w_down', 'w_in', 'ssd_conv_w', 'ssd_conv_b', 'ssd_dt_bias', 'ssd_a_log', 'ssd_d', 'ssd_norm_g', 'mla_q_norm_g', 'mla_w_q_up', 'mla_kv_norm_g', 'mla_w_kv_up', 'w_out_ab', 'rg_w_x', 'rg_w_y', 'rg_conv_w', 'rg_conv_b', 'rg_w_a', 'rg_b_a', 'rg_w_i', 'rg_b_i', 'rg_lambda', 'rg_w_out']
TWIN_DIFF_INPUT = 'x'
TWIN_INPUTS = ['x', 'meta_tokens', 'mix_pre_g', 'mix_post_g', 'mlp_pre_g', 'mlp_post_g', 'w_up', 'w_down', 'w_in', 'ssd_conv_w', 'ssd_conv_b', 'ssd_dt_bias', 'ssd_a_log', 'ssd_d', 'ssd_norm_g', 'mla_q_norm_g', 'mla_w_q_up', 'mla_kv_norm_g', 'mla_w_kv_up', 'w_out_ab', 'rg_w_x', 'rg_w_y', 'rg_conv_w', 'rg_conv_b', 'rg_w_a', 'rg_b_a', 'rg_w_i', 'rg_b_i', 'rg_lambda', 'rg_w_out', 'loss_target', 'm_meta_tokens', 'm_mix_pre_g', 'm_mix_post_g', 'm_mlp_pre_g', 'm_mlp_post_g', 'm_w_up', 'm_w_down', 'm_w_in', 'm_ssd_conv_w', 'm_ssd_conv_b', 'm_ssd_dt_bias', 'm_ssd_a_log', 'm_ssd_d', 'm_ssd_norm_g', 'm_mla_q_norm_g', 'm_mla_w_q_up', 'm_mla_kv_norm_g', 'm_mla_w_kv_up', 'm_w_out_ab', 'm_rg_w_x', 'm_rg_w_y', 'm_rg_conv_w', 'm_rg_conv_b', 'm_rg_w_a', 'm_rg_b_a', 'm_rg_w_i', 'm_rg_b_i', 'm_rg_lambda', 'm_rg_w_out', 'v_meta_tokens', 'v_mix_pre_g', 'v_mix_post_g', 'v_mlp_pre_g', 'v_mlp_post_g', 'v_w_up', 'v_w_down', 'v_w_in', 'v_ssd_conv_w', 'v_ssd_conv_b', 'v_ssd_dt_bias', 'v_ssd_a_log', 'v_ssd_d', 'v_ssd_norm_g', 'v_mla_q_norm_g', 'v_mla_w_q_up', 'v_mla_kv_norm_g', 'v_mla_w_kv_up', 'v_w_out_ab', 'v_rg_w_x', 'v_rg_w_y', 'v_rg_conv_w', 'v_rg_conv_b', 'v_rg_w_a', 'v_rg_b_a', 'v_rg_w_i', 'v_rg_b_i', 'v_rg_lambda', 'v_rg_w_out']
TWIN_OUTPUTS = ['loss', 'grad_x', 'grad_meta_tokens', 'grad_mix_pre_g', 'grad_mix_post_g', 'grad_mlp_pre_g', 'grad_mlp_post_g', 'grad_w_up', 'grad_w_down', 'grad_w_in', 'grad_ssd_conv_w', 'grad_ssd_conv_b', 'grad_ssd_dt_bias', 'grad_ssd_a_log', 'grad_ssd_d', 'grad_ssd_norm_g', 'grad_mla_q_norm_g', 'grad_mla_w_q_up', 'grad_mla_kv_norm_g', 'grad_mla_w_kv_up', 'grad_w_out_ab', 'grad_rg_w_x', 'grad_rg_w_y', 'grad_rg_conv_w', 'grad_rg_conv_b', 'grad_rg_w_a', 'grad_rg_b_a', 'grad_rg_w_i', 'grad_rg_b_i', 'grad_rg_lambda', 'grad_rg_w_out', 'delta_meta_tokens', 'delta_mix_pre_g', 'delta_mix_post_g', 'delta_mlp_pre_g', 'delta_mlp_post_g', 'delta_w_up', 'delta_w_down', 'delta_w_in', 'delta_ssd_conv_w', 'delta_ssd_conv_b', 'delta_ssd_dt_bias', 'delta_ssd_a_log', 'delta_ssd_d', 'delta_ssd_norm_g', 'delta_mla_q_norm_g', 'delta_mla_w_q_up', 'delta_mla_kv_norm_g', 'delta_mla_w_kv_up', 'delta_w_out_ab', 'delta_rg_w_x', 'delta_rg_w_y', 'delta_rg_conv_w', 'delta_rg_conv_b', 'delta_rg_w_a', 'delta_rg_b_a', 'delta_rg_w_i', 'delta_rg_b_i', 'delta_rg_lambda', 'delta_rg_w_out', 'new_m_meta_tokens', 'new_m_mix_pre_g', 'new_m_mix_post_g', 'new_m_mlp_pre_g', 'new_m_mlp_post_g', 'new_m_w_up', 'new_m_w_down', 'new_m_w_in', 'new_m_ssd_conv_w', 'new_m_ssd_conv_b', 'new_m_ssd_dt_bias', 'new_m_ssd_a_log', 'new_m_ssd_d', 'new_m_ssd_norm_g', 'new_m_mla_q_norm_g', 'new_m_mla_w_q_up', 'new_m_mla_kv_norm_g', 'new_m_mla_w_kv_up', 'new_m_w_out_ab', 'new_m_rg_w_x', 'new_m_rg_w_y', 'new_m_rg_conv_w', 'new_m_rg_conv_b', 'new_m_rg_w_a', 'new_m_rg_b_a', 'new_m_rg_w_i', 'new_m_rg_b_i', 'new_m_rg_lambda', 'new_m_rg_w_out', 'new_v_meta_tokens', 'new_v_mix_pre_g', 'new_v_mix_post_g', 'new_v_mlp_pre_g', 'new_v_mlp_post_g', 'new_v_w_up', 'new_v_w_down', 'new_v_w_in', 'new_v_ssd_conv_w', 'new_v_ssd_conv_b', 'new_v_ssd_dt_bias', 'new_v_ssd_a_log', 'new_v_ssd_d', 'new_v_ssd_norm_g', 'new_v_mla_q_norm_g', 'new_v_mla_w_q_up', 'new_v_mla_kv_norm_g', 'new_v_mla_w_kv_up', 'new_v_w_out_ab', 'new_v_rg_w_x', 'new_v_rg_w_y', 'new_v_rg_conv_w', 'new_v_rg_conv_b', 'new_v_rg_w_a', 'new_v_rg_b_a', 'new_v_rg_w_i', 'new_v_rg_b_i', 'new_v_rg_lambda', 'new_v_rg_w_out']
TWIN_LEAF_KINDS = {'loss': 'loss', 'grad_x': 'grad_x', 'grad_meta_tokens': 'grad_w', 'grad_mix_pre_g': 'grad_w', 'grad_mix_post_g': 'grad_w', 'grad_mlp_pre_g': 'grad_w', 'grad_mlp_post_g': 'grad_w', 'grad_w_up': 'grad_w', 'grad_w_down': 'grad_w', 'grad_w_in': 'grad_w', 'grad_ssd_conv_w': 'grad_w', 'grad_ssd_conv_b': 'grad_w', 'grad_ssd_dt_bias': 'grad_w', 'grad_ssd_a_log': 'grad_w', 'grad_ssd_d': 'grad_w', 'grad_ssd_norm_g': 'grad_w', 'grad_mla_q_norm_g': 'grad_w', 'grad_mla_w_q_up': 'grad_w', 'grad_mla_kv_norm_g': 'grad_w', 'grad_mla_w_kv_up': 'grad_w', 'grad_w_out_ab': 'grad_w', 'grad_rg_w_x': 'grad_w', 'grad_rg_w_y': 'grad_w', 'grad_rg_conv_w': 'grad_w', 'grad_rg_conv_b': 'grad_w', 'grad_rg_w_a': 'grad_w', 'grad_rg_b_a': 'grad_w', 'grad_rg_w_i': 'grad_w', 'grad_rg_b_i': 'grad_w', 'grad_rg_lambda': 'grad_w', 'grad_rg_w_out': 'grad_w', 'delta_meta_tokens': 'delta_w', 'delta_mix_pre_g': 'delta_w', 'delta_mix_post_g': 'delta_w', 'delta_mlp_pre_g': 'delta_w', 'delta_mlp_post_g': 'delta_w', 'delta_w_up': 'delta_w', 'delta_w_down': 'delta_w', 'delta_w_in': 'delta_w', 'delta_ssd_conv_w': 'delta_w', 'delta_ssd_conv_b': 'delta_w', 'delta_ssd_dt_bias': 'delta_w', 'delta_ssd_a_log': 'delta_w', 'delta_ssd_d': 'delta_w', 'delta_ssd_norm_g': 'delta_w', 'delta_mla_q_norm_g': 'delta_w', 'delta_mla_w_q_up': 'delta_w', 'delta_mla_kv_norm_g': 'delta_w', 'delta_mla_w_kv_up': 'delta_w', 'delta_w_out_ab': 'delta_w', 'delta_rg_w_x': 'delta_w', 'delta_rg_w_y': 'delta_w', 'delta_rg_conv_w': 'delta_w', 'delta_rg_conv_b': 'delta_w', 'delta_rg_w_a': 'delta_w', 'delta_rg_b_a': 'delta_w', 'delta_rg_w_i': 'delta_w', 'delta_rg_b_i': 'delta_w', 'delta_rg_lambda': 'delta_w', 'delta_rg_w_out': 'delta_w', 'new_m_meta_tokens': 'new_m', 'new_m_mix_pre_g': 'new_m', 'new_m_mix_post_g': 'new_m', 'new_m_mlp_pre_g': 'new_m', 'new_m_mlp_post_g': 'new_m', 'new_m_w_up': 'new_m', 'new_m_w_down': 'new_m', 'new_m_w_in': 'new_m', 'new_m_ssd_conv_w': 'new_m', 'new_m_ssd_conv_b': 'new_m', 'new_m_ssd_dt_bias': 'new_m', 'new_m_ssd_a_log': 'new_m', 'new_m_ssd_d': 'new_m', 'new_m_ssd_norm_g': 'new_m', 'new_m_mla_q_norm_g': 'new_m', 'new_m_mla_w_q_up': 'new_m', 'new_m_mla_kv_norm_g': 'new_m', 'new_m_mla_w_kv_up': 'new_m', 'new_m_w_out_ab': 'new_m', 'new_m_rg_w_x': 'new_m', 'new_m_rg_w_y': 'new_m', 'new_m_rg_conv_w': 'new_m', 'new_m_rg_conv_b': 'new_m', 'new_m_rg_w_a': 'new_m', 'new_m_rg_b_a': 'new_m', 'new_m_rg_w_i': 'new_m', 'new_m_rg_b_i': 'new_m', 'new_m_rg_lambda': 'new_m', 'new_m_rg_w_out': 'new_m', 'new_v_meta_tokens': 'new_v', 'new_v_mix_pre_g': 'new_v', 'new_v_mix_post_g': 'new_v', 'new_v_mlp_pre_g': 'new_v', 'new_v_mlp_post_g': 'new_v', 'new_v_w_up': 'new_v', 'new_v_w_down': 'new_v', 'new_v_w_in': 'new_v', 'new_v_ssd_conv_w': 'new_v', 'new_v_ssd_conv_b': 'new_v', 'new_v_ssd_dt_bias': 'new_v', 'new_v_ssd_a_log': 'new_v', 'new_v_ssd_d': 'new_v', 'new_v_ssd_norm_g': 'new_v', 'new_v_mla_q_norm_g': 'new_v', 'new_v_mla_w_q_up': 'new_v', 'new_v_mla_kv_norm_g': 'new_v', 'new_v_mla_w_kv_up': 'new_v', 'new_v_w_out_ab': 'new_v', 'new_v_rg_w_x': 'new_v', 'new_v_rg_w_y': 'new_v', 'new_v_rg_conv_w': 'new_v', 'new_v_rg_conv_b': 'new_v', 'new_v_rg_w_a': 'new_v', 'new_v_rg_b_a': 'new_v', 'new_v_rg_w_i': 'new_v', 'new_v_rg_b_i': 'new_v', 'new_v_rg_lambda': 'new_v', 'new_v_rg_w_out': 'new_v'}


def _forward(args):
    return _fwd_reference(*[args[k] for k in FWD_PARAMS])


def _output_shape():
    out = _jax.eval_shape(lambda: _forward(_fwd_setup_inputs(0)))
    return out.shape, out.dtype

N_MICROBATCH = 1
ADAM_LR = 0.001
ADAM_B1 = 0.9
ADAM_B2 = 0.999
ADAM_EPS = 1e-08
ADAM_WD = 0.01
ADAM_STEP = 10
PER_EXAMPLE_BATCH_AXIS = {'x': 0, 'loss_target': 0}
SHARED_INPUTS = []
_WEIGHT_DTYPES = {'meta_tokens': _jnp.float32, 'mix_pre_g': _jnp.float32, 'mix_post_g': _jnp.float32, 'mlp_pre_g': _jnp.float32, 'mlp_post_g': _jnp.float32, 'w_up': _jnp.float32, 'w_down': _jnp.float32, 'w_in': _jnp.float32, 'ssd_conv_w': _jnp.float32, 'ssd_conv_b': _jnp.float32, 'ssd_dt_bias': _jnp.float32, 'ssd_a_log': _jnp.float32, 'ssd_d': _jnp.float32, 'ssd_norm_g': _jnp.float32, 'mla_q_norm_g': _jnp.float32, 'mla_w_q_up': _jnp.float32, 'mla_kv_norm_g': _jnp.float32, 'mla_w_kv_up': _jnp.float32, 'w_out_ab': _jnp.float32, 'rg_w_x': _jnp.float32, 'rg_w_y': _jnp.float32, 'rg_conv_w': _jnp.float32, 'rg_conv_b': _jnp.float32, 'rg_w_a': _jnp.float32, 'rg_b_a': _jnp.float32, 'rg_w_i': _jnp.float32, 'rg_b_i': _jnp.float32, 'rg_lambda': _jnp.float32, 'rg_w_out': _jnp.float32}
MOMENT_SCALE = {'meta_tokens': 3.912552e-01, 'mix_pre_g': 1.831223e+01, 'mix_post_g': 3.901683e+01, 'mlp_pre_g': 1.065191e+01, 'mlp_post_g': 4.238381e+01, 'w_up': 5.230004e+00, 'w_down': 2.456060e+01, 'w_in': 5.985191e+00, 'ssd_conv_w': 8.031276e+00, 'ssd_conv_b': 2.459565e+01, 'ssd_dt_bias': 1.153623e+01, 'ssd_a_log': 3.910776e+01, 'ssd_d': 5.220947e+01, 'ssd_norm_g': 1.266258e+01, 'mla_q_norm_g': 5.461663e-01, 'mla_w_q_up': 2.571975e-01, 'mla_kv_norm_g': 1.978594e+01, 'mla_w_kv_up': 6.681365e+00, 'w_out_ab': 1.679414e+01, 'rg_w_x': 1.959919e+01, 'rg_w_y': 1.041252e+01, 'rg_conv_w': 1.983788e+01, 'rg_conv_b': 6.302601e+01, 'rg_w_a': 1.785068e+00, 'rg_b_a': 2.164090e+00, 'rg_w_i': 3.970989e+00, 'rg_b_i': 7.699765e+00, 'rg_lambda': 4.949489e+00, 'rg_w_out': 2.063972e+01}


def _to_microbatches(a, axis):
    t = _jnp.moveaxis(a, axis, 0)
    t = t.reshape((N_MICROBATCH, t.shape[0] // N_MICROBATCH) + t.shape[1:])
    return _jnp.moveaxis(t, 1, axis + 1)


def setup_inputs(seed: int = 0) -> dict:
    inp = _fwd_setup_inputs(seed)
    key = _jax.random.fold_in(_jax.random.key(seed), 7919)
    shape, _ = _output_shape()
    out = dict(inp)
    out["loss_target"] = _jax.random.normal(_jax.random.fold_in(key, 0), shape, _jnp.float32)
    for i, name in enumerate(TWIN_WEIGHTS):
        w = inp[name].astype(_jnp.float32)
        if MOMENT_SCALE is None:
            s = _jnp.sqrt(_jnp.mean(_jnp.square(w)) + 1e-30)
        else:
            s = MOMENT_SCALE[name]
        km, kv = _jax.random.split(_jax.random.fold_in(key, i + 1))
        out[name] = w
        out["m_" + name] = s * _jax.random.normal(km, w.shape, _jnp.float32)
        out["v_" + name] = (s * s) * _jax.random.uniform(kv, w.shape, _jnp.float32, 0.5, 1.5)
    if N_MICROBATCH > 1:
        for name, axis in PER_EXAMPLE_BATCH_AXIS.items():
            out[name] = _to_microbatches(out[name], axis)
    return {'x': out['x'], 'meta_tokens': out['meta_tokens'], 'mix_pre_g': out['mix_pre_g'], 'mix_post_g': out['mix_post_g'], 'mlp_pre_g': out['mlp_pre_g'], 'mlp_post_g': out['mlp_post_g'], 'w_up': out['w_up'], 'w_down': out['w_down'], 'w_in': out['w_in'], 'ssd_conv_w': out['ssd_conv_w'], 'ssd_conv_b': out['ssd_conv_b'], 'ssd_dt_bias': out['ssd_dt_bias'], 'ssd_a_log': out['ssd_a_log'], 'ssd_d': out['ssd_d'], 'ssd_norm_g': out['ssd_norm_g'], 'mla_q_norm_g': out['mla_q_norm_g'], 'mla_w_q_up': out['mla_w_q_up'], 'mla_kv_norm_g': out['mla_kv_norm_g'], 'mla_w_kv_up': out['mla_w_kv_up'], 'w_out_ab': out['w_out_ab'], 'rg_w_x': out['rg_w_x'], 'rg_w_y': out['rg_w_y'], 'rg_conv_w': out['rg_conv_w'], 'rg_conv_b': out['rg_conv_b'], 'rg_w_a': out['rg_w_a'], 'rg_b_a': out['rg_b_a'], 'rg_w_i': out['rg_w_i'], 'rg_b_i': out['rg_b_i'], 'rg_lambda': out['rg_lambda'], 'rg_w_out': out['rg_w_out'], 'loss_target': out['loss_target'], 'm_meta_tokens': out['m_meta_tokens'], 'm_mix_pre_g': out['m_mix_pre_g'], 'm_mix_post_g': out['m_mix_post_g'], 'm_mlp_pre_g': out['m_mlp_pre_g'], 'm_mlp_post_g': out['m_mlp_post_g'], 'm_w_up': out['m_w_up'], 'm_w_down': out['m_w_down'], 'm_w_in': out['m_w_in'], 'm_ssd_conv_w': out['m_ssd_conv_w'], 'm_ssd_conv_b': out['m_ssd_conv_b'], 'm_ssd_dt_bias': out['m_ssd_dt_bias'], 'm_ssd_a_log': out['m_ssd_a_log'], 'm_ssd_d': out['m_ssd_d'], 'm_ssd_norm_g': out['m_ssd_norm_g'], 'm_mla_q_norm_g': out['m_mla_q_norm_g'], 'm_mla_w_q_up': out['m_mla_w_q_up'], 'm_mla_kv_norm_g': out['m_mla_kv_norm_g'], 'm_mla_w_kv_up': out['m_mla_w_kv_up'], 'm_w_out_ab': out['m_w_out_ab'], 'm_rg_w_x': out['m_rg_w_x'], 'm_rg_w_y': out['m_rg_w_y'], 'm_rg_conv_w': out['m_rg_conv_w'], 'm_rg_conv_b': out['m_rg_conv_b'], 'm_rg_w_a': out['m_rg_w_a'], 'm_rg_b_a': out['m_rg_b_a'], 'm_rg_w_i': out['m_rg_w_i'], 'm_rg_b_i': out['m_rg_b_i'], 'm_rg_lambda': out['m_rg_lambda'], 'm_rg_w_out': out['m_rg_w_out'], 'v_meta_tokens': out['v_meta_tokens'], 'v_mix_pre_g': out['v_mix_pre_g'], 'v_mix_post_g': out['v_mix_post_g'], 'v_mlp_pre_g': out['v_mlp_pre_g'], 'v_mlp_post_g': out['v_mlp_post_g'], 'v_w_up': out['v_w_up'], 'v_w_down': out['v_w_down'], 'v_w_in': out['v_w_in'], 'v_ssd_conv_w': out['v_ssd_conv_w'], 'v_ssd_conv_b': out['v_ssd_conv_b'], 'v_ssd_dt_bias': out['v_ssd_dt_bias'], 'v_ssd_a_log': out['v_ssd_a_log'], 'v_ssd_d': out['v_ssd_d'], 'v_ssd_norm_g': out['v_ssd_norm_g'], 'v_mla_q_norm_g': out['v_mla_q_norm_g'], 'v_mla_w_q_up': out['v_mla_w_q_up'], 'v_mla_kv_norm_g': out['v_mla_kv_norm_g'], 'v_mla_w_kv_up': out['v_mla_w_kv_up'], 'v_w_out_ab': out['v_w_out_ab'], 'v_rg_w_x': out['v_rg_w_x'], 'v_rg_w_y': out['v_rg_w_y'], 'v_rg_conv_w': out['v_rg_conv_w'], 'v_rg_conv_b': out['v_rg_conv_b'], 'v_rg_w_a': out['v_rg_w_a'], 'v_rg_b_a': out['v_rg_b_a'], 'v_rg_w_i': out['v_rg_w_i'], 'v_rg_b_i': out['v_rg_b_i'], 'v_rg_lambda': out['v_rg_lambda'], 'v_rg_w_out': out['v_rg_w_out']}


def _loss(weights, diff, rest, loss_target):
    with _jax.named_scope("forward"):
        args = {**rest, TWIN_DIFF_INPUT: diff, **{k: w.astype(_WEIGHT_DTYPES[k]) for k, w in weights.items()}}
        y = _forward(args)
    with _jax.named_scope("loss_head"):
        err = _jnp.square(y.astype(_jnp.float32) - loss_target)
        return 0.5 * _jnp.sum(_jnp.mean(err, axis=-1)) if err.ndim else 0.5 * err


def _adamw(w, g, m, v):
    m = ADAM_B1 * m + (1.0 - ADAM_B1) * g
    v = ADAM_B2 * v + (1.0 - ADAM_B2) * _jnp.square(g)
    m_hat = m / (1.0 - ADAM_B1 ** ADAM_STEP)
    v_hat = v / (1.0 - ADAM_B2 ** ADAM_STEP)
    delta = -ADAM_LR * (m_hat / (_jnp.sqrt(v_hat) + ADAM_EPS) + ADAM_WD * w)
    return delta, m, v


def reference(x, meta_tokens, mix_pre_g, mix_post_g, mlp_pre_g, mlp_post_g, w_up, w_down, w_in, ssd_conv_w, ssd_conv_b, ssd_dt_bias, ssd_a_log, ssd_d, ssd_norm_g, mla_q_norm_g, mla_w_q_up, mla_kv_norm_g, mla_w_kv_up, w_out_ab, rg_w_x, rg_w_y, rg_conv_w, rg_conv_b, rg_w_a, rg_b_a, rg_w_i, rg_b_i, rg_lambda, rg_w_out, loss_target, m_meta_tokens, m_mix_pre_g, m_mix_post_g, m_mlp_pre_g, m_mlp_post_g, m_w_up, m_w_down, m_w_in, m_ssd_conv_w, m_ssd_conv_b, m_ssd_dt_bias, m_ssd_a_log, m_ssd_d, m_ssd_norm_g, m_mla_q_norm_g, m_mla_w_q_up, m_mla_kv_norm_g, m_mla_w_kv_up, m_w_out_ab, m_rg_w_x, m_rg_w_y, m_rg_conv_w, m_rg_conv_b, m_rg_w_a, m_rg_b_a, m_rg_w_i, m_rg_b_i, m_rg_lambda, m_rg_w_out, v_meta_tokens, v_mix_pre_g, v_mix_post_g, v_mlp_pre_g, v_mlp_post_g, v_w_up, v_w_down, v_w_in, v_ssd_conv_w, v_ssd_conv_b, v_ssd_dt_bias, v_ssd_a_log, v_ssd_d, v_ssd_norm_g, v_mla_q_norm_g, v_mla_w_q_up, v_mla_kv_norm_g, v_mla_w_kv_up, v_w_out_ab, v_rg_w_x, v_rg_w_y, v_rg_conv_w, v_rg_conv_b, v_rg_w_a, v_rg_b_a, v_rg_w_i, v_rg_b_i, v_rg_lambda, v_rg_w_out):
    given = dict(x=x, meta_tokens=meta_tokens, mix_pre_g=mix_pre_g, mix_post_g=mix_post_g, mlp_pre_g=mlp_pre_g, mlp_post_g=mlp_post_g, w_up=w_up, w_down=w_down, w_in=w_in, ssd_conv_w=ssd_conv_w, ssd_conv_b=ssd_conv_b, ssd_dt_bias=ssd_dt_bias, ssd_a_log=ssd_a_log, ssd_d=ssd_d, ssd_norm_g=ssd_norm_g, mla_q_norm_g=mla_q_norm_g, mla_w_q_up=mla_w_q_up, mla_kv_norm_g=mla_kv_norm_g, mla_w_kv_up=mla_w_kv_up, w_out_ab=w_out_ab, rg_w_x=rg_w_x, rg_w_y=rg_w_y, rg_conv_w=rg_conv_w, rg_conv_b=rg_conv_b, rg_w_a=rg_w_a, rg_b_a=rg_b_a, rg_w_i=rg_w_i, rg_b_i=rg_b_i, rg_lambda=rg_lambda, rg_w_out=rg_w_out, loss_target=loss_target, m_meta_tokens=m_meta_tokens, m_mix_pre_g=m_mix_pre_g, m_mix_post_g=m_mix_post_g, m_mlp_pre_g=m_mlp_pre_g, m_mlp_post_g=m_mlp_post_g, m_w_up=m_w_up, m_w_down=m_w_down, m_w_in=m_w_in, m_ssd_conv_w=m_ssd_conv_w, m_ssd_conv_b=m_ssd_conv_b, m_ssd_dt_bias=m_ssd_dt_bias, m_ssd_a_log=m_ssd_a_log, m_ssd_d=m_ssd_d, m_ssd_norm_g=m_ssd_norm_g, m_mla_q_norm_g=m_mla_q_norm_g, m_mla_w_q_up=m_mla_w_q_up, m_mla_kv_norm_g=m_mla_kv_norm_g, m_mla_w_kv_up=m_mla_w_kv_up, m_w_out_ab=m_w_out_ab, m_rg_w_x=m_rg_w_x, m_rg_w_y=m_rg_w_y, m_rg_conv_w=m_rg_conv_w, m_rg_conv_b=m_rg_conv_b, m_rg_w_a=m_rg_w_a, m_rg_b_a=m_rg_b_a, m_rg_w_i=m_rg_w_i, m_rg_b_i=m_rg_b_i, m_rg_lambda=m_rg_lambda, m_rg_w_out=m_rg_w_out, v_meta_tokens=v_meta_tokens, v_mix_pre_g=v_mix_pre_g, v_mix_post_g=v_mix_post_g, v_mlp_pre_g=v_mlp_pre_g, v_mlp_post_g=v_mlp_post_g, v_w_up=v_w_up, v_w_down=v_w_down, v_w_in=v_w_in, v_ssd_conv_w=v_ssd_conv_w, v_ssd_conv_b=v_ssd_conv_b, v_ssd_dt_bias=v_ssd_dt_bias, v_ssd_a_log=v_ssd_a_log, v_ssd_d=v_ssd_d, v_ssd_norm_g=v_ssd_norm_g, v_mla_q_norm_g=v_mla_q_norm_g, v_mla_w_q_up=v_mla_w_q_up, v_mla_kv_norm_g=v_mla_kv_norm_g, v_mla_w_kv_up=v_mla_w_kv_up, v_w_out_ab=v_w_out_ab, v_rg_w_x=v_rg_w_x, v_rg_w_y=v_rg_w_y, v_rg_conv_w=v_rg_conv_w, v_rg_conv_b=v_rg_conv_b, v_rg_w_a=v_rg_w_a, v_rg_b_a=v_rg_b_a, v_rg_w_i=v_rg_w_i, v_rg_b_i=v_rg_b_i, v_rg_lambda=v_rg_lambda, v_rg_w_out=v_rg_w_out)
    weights = {n: given[n] for n in TWIN_WEIGHTS}
    shared = {n: given[n] for n in SHARED_INPUTS}
    per_example = {n: given[n] for n in ['x']}
    grad_fn = _jax.value_and_grad(_loss, argnums=(0, 1))

    def one_microbatch(ex, loss_target):
        ex = dict(ex)
        diff = ex.pop(TWIN_DIFF_INPUT)
        return grad_fn(weights, diff, {**shared, **ex}, loss_target)

    if N_MICROBATCH == 1:
        loss, (grad_w, grad_x) = one_microbatch(per_example, given["loss_target"])
    else:
        def body(carry, xs):
            loss_sum, grad_sum = carry
            l_k, (gw_k, gx_k) = one_microbatch(xs[0], xs[1])
            with _jax.named_scope("update"):
                return (loss_sum + l_k, _jax.tree.map(_jnp.add, grad_sum, gw_k)), gx_k

        init = (_jnp.zeros((), _jnp.float32), _jax.tree.map(_jnp.zeros_like, weights))
        (loss, grad_w), grad_x = _jax.lax.scan(body, init, (per_example, given["loss_target"]))
    with _jax.named_scope("update"):
        delta_w, new_m, new_v = {}, {}, {}
        for n in TWIN_WEIGHTS:
            delta_w[n], new_m[n], new_v[n] = _adamw(weights[n], grad_w[n], given["m_" + n], given["v_" + n])
    return (loss, grad_x, *[grad_w[n] for n in TWIN_WEIGHTS], *[delta_w[n] for n in TWIN_WEIGHTS],
            *[new_m[n] for n in TWIN_WEIGHTS], *[new_v[n] for n in TWIN_WEIGHTS])
```

```python
import functools
import math

import numpy as np
import jax
import jax.numpy as jnp
from jax import lax
from jax.experimental import pallas as pl
from jax.experimental.pallas import tpu as pltpu

F32 = jnp.float32
BF = jnp.bfloat16
HI = lax.Precision.HIGHEST

D_MODEL = 1024
DEPTH = 4
N_META = 16
CHUNK = 128
PAD = CHUNK - N_META
EPS = 1e-6
SSD_HEADS = 16
SSD_HEAD_DIM = 64
SSD_D_INNER = 1024
SSD_STATE = 128
SSD_CONV_CH = 1536
MLA_HEADS = 16
MLA_NOPE = 64
MLA_ROPE = 32
MLA_V = 64
MLA_Q_RANK = 384
MLA_KV_RANK = 256
ROPE_BASE = 10000.0
LRU_WIDTH = 1280
LRU_BLOCKS = 10
LRU_C = 8.0
D_FF = 4096
N_DEV = 8
LANE = 128
IN_W = 3456
OFF_Z, OFF_XBC, OFF_CKV, OFF_DT, OFF_KR, OFF_CQ = 0, 1024, 2560, 2816, 2944, 3072

ADAM_LR = 0.001
ADAM_B1 = 0.9
ADAM_B2 = 0.999
ADAM_EPS = 1e-08
ADAM_WD = 0.01
ADAM_STEP = 10

VMEM_LIMIT = 56 * 1024 * 1024
NEG = -1e30


def _pick(n, cands):
    for c in cands:
        if n % c == 0:
            return c
    return n


def _cp(sem=None):
    return pltpu.CompilerParams(dimension_semantics=sem, vmem_limit_bytes=VMEM_LIMIT)


def _sds(shape, dtype):
    return jax.ShapeDtypeStruct(tuple(shape), dtype)


def _silu(x):
    return x * jax.nn.sigmoid(x)


def _softplus(x):
    return jnp.maximum(x, 0.0) + jnp.log(1.0 + jnp.exp(-jnp.abs(x)))


def _gelu(x):
    c = math.sqrt(2.0 / math.pi)
    return 0.5 * x * (1.0 + jnp.tanh(c * (x + 0.044715 * (x * x * x))))


def _row_mask(i, tr, shape, first_valid=PAD):
    row = i * tr + lax.broadcasted_iota(jnp.int32, shape, 0)
    return row >= first_valid


def matmul(a, b, mode, out_dtypes=(F32,), epi=None, extras=(), name="mm", tm=None, tn=None):
    if mode == "nn":
        (m, k), (k2, n) = a.shape, b.shape
    elif mode == "nt":
        (m, k), (n, k2) = a.shape, b.shape
    else:
        (k, m), (k2, n) = a.shape, b.shape
    assert k == k2, (a.shape, b.shape, mode)
    tm = tm or _pick(m, (1056, 1024, 768, 640, 512, 384, 256, 128))
    tn = tn or _pick(n, (512, 640, 384, 256, 128))
    if mode == "tn":
        tm = _pick(m, (512, 384, 256, 128))
    dims = {"nn": (((1,), (0,)), ((), ())), "nt": (((1,), (1,)), ((), ())), "tn": (((0,), (0,)), ((), ()))}[mode]
    n_ex = len(extras)

    def body(a_ref, b_ref, *rest):
        ex_refs, out_refs = rest[:n_ex], rest[n_ex:]
        acc = lax.dot_general(a_ref[...].astype(BF), b_ref[...].astype(BF), dims, preferred_element_type=F32)
        outs = (acc,) if epi is None else epi(acc, *[r[...] for r in ex_refs])
        for r, o in zip(out_refs, outs):
            r[...] = o.astype(r.dtype)

    a_spec = pl.BlockSpec((k, tm), lambda i, j: (0, i)) if mode == "tn" else pl.BlockSpec((tm, k), lambda i, j: (i, 0))
    b_spec = pl.BlockSpec((tn, k), lambda i, j: (j, 0)) if mode == "nt" else pl.BlockSpec((k, tn), lambda i, j: (0, j))
    o_spec = pl.BlockSpec((tm, tn), lambda i, j: (i, j))
    outs = pl.pallas_call(
        body,
        out_shape=tuple(_sds((m, n), dt) for dt in out_dtypes),
        grid=(m // tm, n // tn),
        in_specs=[a_spec, b_spec] + [o_spec] * n_ex,
        out_specs=tuple(o_spec for _ in out_dtypes),
        compiler_params=_cp(("parallel", "parallel")),
        name=name,
    )(a, b, *extras)
    return outs[0] if len(out_dtypes) == 1 else outs


def _rt(t):
    return _pick(t, (384, 256, 128))


def norm_fwd(x, g, out_dtype, col_blk=0, width=None, name="norm_fwd"):
    t = x.shape[0]
    w = width or x.shape[1]
    tr = _rt(t)

    def body(x_ref, g_ref, o_ref):
        xv = x_ref[...]
        r = lax.rsqrt(jnp.mean(xv * xv, axis=-1, keepdims=True) + EPS)
        o_ref[...] = (xv * r * g_ref[...]).astype(o_ref.dtype)

    return pl.pallas_call(
        body,
        out_shape=_sds((t, w), out_dtype),
        grid=(t // tr,),
        in_specs=[pl.BlockSpec((tr, w), lambda i: (i, col_blk)), pl.BlockSpec((1, w), lambda i: (0, 0))],
        out_specs=pl.BlockSpec((tr, w), lambda i: (i, 0)),
        compiler_params=_cp(("parallel",)),
        name=name,
    )(x, g.reshape(1, w))


def norm_bwd(x, g, dy, dres=None, mask_pad=False, out_dtype=F32, col_blk=0, width=None, dy_col_blk=0, name="norm_bwd"):
    t = x.shape[0]
    w = width or x.shape[1]
    tr = _rt(t)
    has_res = dres is not None

    def body(x_ref, g_ref, dy_ref, *rest):
        if has_res:
            res_ref, dx_ref, dg_ref = rest
        else:
            dx_ref, dg_ref = rest
        i = pl.program_id(0)
        xv = x_ref[...]
        dyv = dy_ref[...].astype(F32)
        if mask_pad:
            dyv = jnp.where(_row_mask(i, tr, dyv.shape), dyv, 0.0)
        r = lax.rsqrt(jnp.mean(xv * xv, axis=-1, keepdims=True) + EPS)
        xh = xv * r
        dyg = dyv * g_ref[...]
        dx = r * (dyg - xh * jnp.mean(dyg * xh, axis=-1, keepdims=True))
        if has_res:
            dx = dx + res_ref[...]
        dx_ref[...] = dx.astype(dx_ref.dtype)

        @pl.when(i == 0)
        def _():
            dg_ref[...] = jnp.zeros_like(dg_ref)

        dg_ref[...] += jnp.sum(dyv * xh, axis=0, keepdims=True)

    in_specs = [pl.BlockSpec((tr, w), lambda i: (i, col_blk)), pl.BlockSpec((1, w), lambda i: (0, 0)),
                pl.BlockSpec((tr, w), lambda i: (i, dy_col_blk))]
    args = [x, g.reshape(1, w), dy]
    if has_res:
        in_specs.append(pl.BlockSpec((tr, w), lambda i: (i, 0)))
        args.append(dres)
    dx, dg = pl.pallas_call(
        body,
        out_shape=(_sds((t, w), out_dtype), _sds((1, w), F32)),
        grid=(t // tr,),
        in_specs=in_specs,
        out_specs=(pl.BlockSpec((tr, w), lambda i: (i, 0)), pl.BlockSpec((1, w), lambda i: (0, 0))),
        compiler_params=_cp(("arbitrary",)),
        name=name,
    )(*args)
    return dx, dg.reshape(w)


def resadd_fwd(h, m, g, name="resadd"):
    t, w = h.shape
    tr = _rt(t)

    def body(h_ref, m_ref, g_ref, o_ref):
        mv = m_ref[...]
        r = lax.rsqrt(jnp.mean(mv * mv, axis=-1, keepdims=True) + EPS)
        y = mv * r * g_ref[...]
        o_ref[...] = h_ref[...] + jnp.where(_row_mask(pl.program_id(0), tr, y.shape), y, 0.0)

    return pl.pallas_call(
        body,
        out_shape=_sds((t, w), F32),
        grid=(t // tr,),
        in_specs=[pl.BlockSpec((tr, w), lambda i: (i, 0)), pl.BlockSpec((tr, w), lambda i: (i, 0)),
                  pl.BlockSpec((1, w), lambda i: (0, 0))],
        out_specs=pl.BlockSpec((tr, w), lambda i: (i, 0)),
        compiler_params=_cp(("parallel",)),
        name=name,
    )(h, m, g.reshape(1, w))


def loss_fwd_bwd(h, target):
    t, w = h.shape
    tr = _rt(t)

    def body(h_ref, t_ref, s_ref, dh_ref):
        i = pl.program_id(0)
        err = h_ref[...] - t_ref[...]
        err = jnp.where(_row_mask(i, tr, err.shape, PAD + N_META), err, 0.0)
        dh_ref[...] = err * (1.0 / w)

        @pl.when(i == 0)
        def _():
            s_ref[...] = jnp.zeros_like(s_ref)

        s_ref[...] += jnp.sum(err * err).reshape(1, 1)

    s, dh = pl.pallas_call(
        body,
        out_shape=(_sds((1, LANE), F32), _sds((t, w), F32)),
        grid=(t // tr,),
        in_specs=[pl.BlockSpec((tr, w), lambda i: (i, 0)), pl.BlockSpec((tr, w), lambda i: (i, 0))],
        out_specs=(pl.BlockSpec((1, LANE), lambda i: (0, 0)), pl.BlockSpec((tr, w), lambda i: (i, 0))),
        compiler_params=_cp(("arbitrary",)),
        name="loss",
    )(h, target)
    return 0.5 * s[0, 0] / w, dh


def _shift_down(ext, k, n):
    return pltpu.roll(ext, k, 0)[8:]


def _conv_pre(ext, x, w_ref, n):
    return (w_ref[4:5, :] + w_ref[3:4, :] * x + w_ref[2:3, :] * _shift_down(ext, 1, n)
            + w_ref[1:2, :] * _shift_down(ext, 2, n) + w_ref[0:1, :] * _shift_down(ext, 3, n))


def _conv_bwd_parts(dpre, dnext, x, ext, w_ref, n):
    extd = jnp.concatenate([dpre, dnext], axis=0)
    ln = n + 8
    dx = (w_ref[3:4, :] * dpre + w_ref[2:3, :] * pltpu.roll(extd, ln - 1, 0)[:n]
          + w_ref[1:2, :] * pltpu.roll(extd, ln - 2, 0)[:n] + w_ref[0:1, :] * pltpu.roll(extd, ln - 3, 0)[:n])
    sums = [jnp.sum(dpre * _shift_down(ext, 3, n), axis=0, keepdims=True),
            jnp.sum(dpre * _shift_down(ext, 2, n), axis=0, keepdims=True),
            jnp.sum(dpre * _shift_down(ext, 1, n), axis=0, keepdims=True),
            jnp.sum(dpre * x, axis=0, keepdims=True),
            jnp.sum(dpre, axis=0, keepdims=True)]
    return dx, sums


def _rows_block(sums):
    row = lax.broadcasted_iota(jnp.int32, (8, LANE), 0)
    out = jnp.zeros((8, LANE), F32)
    for k, s in enumerate(sums):
        out = jnp.where(row == k, s, out)
    return out


def conv_silu_fwd(x, col0_blk, nblk, wb, name="conv_fwd"):
    t = x.shape[0]
    c = nblk * LANE
    tr = _rt(t)

    def body(x_ref, w_ref, o_ref, prev):
        ti = pl.program_id(1)

        @pl.when(ti == 0)
        def _():
            prev[...] = jnp.zeros_like(prev)

        xv = x_ref[...]
        ext = jnp.concatenate([prev[...], xv], axis=0)
        o_ref[...] = _silu(_conv_pre(ext, xv, w_ref, tr))
        prev[...] = xv[tr - 8:, :]

    return pl.pallas_call(
        body,
        out_shape=_sds((t, c), F32),
        grid=(nblk, t // tr),
        in_specs=[pl.BlockSpec((tr, LANE), lambda cb, ti: (ti, col0_blk + cb)),
                  pl.BlockSpec((8, LANE), lambda cb, ti: (0, cb))],
        out_specs=pl.BlockSpec((tr, LANE), lambda cb, ti: (ti, cb)),
        scratch_shapes=[pltpu.VMEM((8, LANE), F32)],
        compiler_params=_cp(("parallel", "arbitrary")),
        name=name,
    )(x, wb)


def conv_silu_bwd(x, col0_blk, nblk, wb, dout, name="conv_bwd"):
    t = x.shape[0]
    c = nblk * LANE
    tr = _rt(t)
    nt = t // tr
    r8 = tr // 8

    def body(x_ref, xp_ref, w_ref, do_ref, dx_ref, dwb_ref, dnext):
        ti = pl.program_id(1)
        tt = nt - 1 - ti

        @pl.when(ti == 0)
        def _():
            dnext[...] = jnp.zeros_like(dnext)
            dwb_ref[...] = jnp.zeros_like(dwb_ref)

        xv = x_ref[...]
        halo = jnp.where(tt > 0, xp_ref[...], 0.0)
        ext = jnp.concatenate([halo, xv], axis=0)
        pre = _conv_pre(ext, xv, w_ref, tr)
        s = jax.nn.sigmoid(pre)
        dpre = do_ref[...] * (s + pre * s * (1.0 - s))
        dx, sums = _conv_bwd_parts(dpre, dnext[...], xv, ext, w_ref, tr)
        dx_ref[...] = dx
        dwb_ref[...] += _rows_block(sums)
        dnext[...] = dpre[:8, :]

    return pl.pallas_call(
        body,
        out_shape=(_sds((t, c), F32), _sds((8, c), F32)),
        grid=(nblk, nt),
        in_specs=[pl.BlockSpec((tr, LANE), lambda cb, ti: (nt - 1 - ti, col0_blk + cb)),
                  pl.BlockSpec((8, LANE), lambda cb, ti: (jnp.maximum((nt - 1 - ti) * r8 - 1, 0), col0_blk + cb)),
                  pl.BlockSpec((8, LANE), lambda cb, ti: (0, cb)),
                  pl.BlockSpec((tr, LANE), lambda cb, ti: (nt - 1 - ti, cb))],
        out_specs=(pl.BlockSpec((tr, LANE), lambda cb, ti: (nt - 1 - ti, cb)),
                   pl.BlockSpec((8, LANE), lambda cb, ti: (0, cb))),
        scratch_shapes=[pltpu.VMEM((8, LANE), F32)],
        compiler_params=_cp(("parallel", "arbitrary")),
        name=name,
    )(x, x, wb, dout)


def gated_norm_fwd(y, proj, g, name="gnorm_fwd"):
    t, w = y.shape
    tr = _rt(t)

    def body(y_ref, z_ref, g_ref, o_ref):
        v = y_ref[...] * _silu(z_ref[...])
        r = lax.rsqrt(jnp.mean(v * v, axis=-1, keepdims=True) + EPS)
        o_ref[...] = (v * r * g_ref[...]).astype(o_ref.dtype)

    return pl.pallas_call(
        body,
        out_shape=_sds((t, w), BF),
        grid=(t // tr,),
        in_specs=[pl.BlockSpec((tr, w), lambda i: (i, 0)), pl.BlockSpec((tr, w), lambda i: (i, OFF_Z // w)),
                  pl.BlockSpec((1, w), lambda i: (0, 0))],
        out_specs=pl.BlockSpec((tr, w), lambda i: (i, 0)),
        compiler_params=_cp(("parallel",)),
        name=name,
    )(y, proj, g.reshape(1, w))


def gated_norm_bwd(y, proj, g, dyab, name="gnorm_bwd"):
    t, w = y.shape
    tr = _rt(t)

    def body(y_ref, z_ref, g_ref, do_ref, dy_ref, dz_ref, dg_ref):
        i = pl.program_id(0)
        yv, zv, dov = y_ref[...], z_ref[...], do_ref[...]
        s = jax.nn.sigmoid(zv)
        sz = zv * s
        v = yv * sz
        r = lax.rsqrt(jnp.mean(v * v, axis=-1, keepdims=True) + EPS)
        vh = v * r
        dvg = dov * g_ref[...]
        dv = r * (dvg - vh * jnp.mean(dvg * vh, axis=-1, keepdims=True))
        dy_ref[...] = dv * sz
        dz_ref[...] = dv * yv * (s + sz * (1.0 - s))

        @pl.when(i == 0)
        def _():
            dg_ref[...] = jnp.zeros_like(dg_ref)

        dg_ref[...] += jnp.sum(dov * vh, axis=0, keepdims=True)

    dy, dz, dg = pl.pallas_call(
        body,
        out_shape=(_sds((t, w), F32), _sds((t, w), F32), _sds((1, w), F32)),
        grid=(t // tr,),
        in_specs=[pl.BlockSpec((tr, w), lambda i: (i, 0)), pl.BlockSpec((tr, w), lambda i: (i, OFF_Z // w)),
                  pl.BlockSpec((1, w), lambda i: (0, 0)), pl.BlockSpec((tr, w), lambda i: (i, 0))],
        out_specs=(pl.BlockSpec((tr, w), lambda i: (i, 0)), pl.BlockSpec((tr, w), lambda i: (i, 0)),
                   pl.BlockSpec((1, w), lambda i: (0, 0))),
        compiler_params=_cp(("arbitrary",)),
        name=name,
    )(y, proj, g.reshape(1, w), dyab)
    return dy, dz, dg.reshape(w)


def rope_tables(t):
    inv = ROPE_BASE ** (-jnp.arange(0, MLA_ROPE, 2, dtype=F32) / MLA_ROPE)
    pos = (jnp.arange(t, dtype=F32) - PAD)[:, None]
    ang = pos * inv[None, :]
    cos, sin = jnp.cos(ang), jnp.sin(ang)
    z16 = jnp.zeros((t, 16), F32)
    z32 = jnp.zeros((t, 32), F32)
    c = jnp.concatenate([jnp.ones((t, 64), F32), cos, cos, z32], axis=1)
    s1 = jnp.concatenate([jnp.zeros((t, 64), F32), z16, sin, z32], axis=1)
    s2 = jnp.concatenate([jnp.zeros((t, 64), F32), -sin, z16, z32], axis=1)
    return c, s1, s2


def _rope(x, c, s1, s2):
    return x * c + pltpu.roll(x, 16, 1) * s1 + pltpu.roll(x, LANE - 16, 1) * s2


def _rope_t(d, c, s1, s2):
    return d * c + pltpu.roll(d * s1, LANE - 16, 1) + pltpu.roll(d * s2, 16, 1)


def rope_fwd(q_raw, kv_raw, proj, tabs):
    t = q_raw.shape[0]
    tr = _rt(t)
    hw = MLA_HEADS * LANE

    def body(q_ref, k_ref, v_ref, kr_ref, c_ref, s1_ref, s2_ref, qo_ref, ko_ref, vo_ref):
        c, s1, s2 = c_ref[...], s1_ref[...], s2_ref[...]
        kr = _rope(kr_ref[...], c, s1, s2)
        for h in range(MLA_HEADS):
            sl = slice(h * LANE, (h + 1) * LANE)
            qo_ref[:, sl] = _rope(q_ref[:, sl], c, s1, s2).astype(BF)
            ko_ref[:, sl] = (k_ref[:, sl] + kr).astype(BF)
        vo_ref[...] = v_ref[...].astype(BF)

    tab_spec = pl.BlockSpec((tr, LANE), lambda i: (i, 0))
    return pl.pallas_call(
        body,
        out_shape=(_sds((t, hw), BF), _sds((t, hw), BF), _sds((t, 1024), BF)),
        grid=(t // tr,),
        in_specs=[pl.BlockSpec((tr, hw), lambda i: (i, 0)), pl.BlockSpec((tr, hw), lambda i: (i, 0)),
                  pl.BlockSpec((tr, 1024), lambda i: (i, 2)), pl.BlockSpec((tr, LANE), lambda i: (i, OFF_KR // LANE)),
                  tab_spec, tab_spec, tab_spec],
        out_specs=(pl.BlockSpec((tr, hw), lambda i: (i, 0)), pl.BlockSpec((tr, hw), lambda i: (i, 0)),
                   pl.BlockSpec((tr, 1024), lambda i: (i, 0))),
        compiler_params=_cp(("parallel",)),
        name="rope_fwd",
    )(q_raw, kv_raw, kv_raw, proj, *tabs)


def rope_bwd(dq_cat, dk_cat, tabs):
    t = dq_cat.shape[0]
    tr = _rt(t)
    hw = MLA_HEADS * LANE

    def body(dq_ref, dk_ref, c_ref, s1_ref, s2_ref, dqo_ref, dkr_ref):
        c, s1, s2 = c_ref[...], s1_ref[...], s2_ref[...]
        acc = jnp.zeros((tr, LANE), F32)
        for h in range(MLA_HEADS):
            sl = slice(h * LANE, (h + 1) * LANE)
            dqo_ref[:, sl] = _rope_t(dq_ref[:, sl], c, s1, s2).astype(BF)
            acc = acc + dk_ref[:, sl]
        lane = lax.broadcasted_iota(jnp.int32, (tr, LANE), 1)
        dkr_ref[...] = jnp.where((lane >= 64) & (lane < 96), _rope_t(acc, c, s1, s2), 0.0)

    tab_spec = pl.BlockSpec((tr, LANE), lambda i: (i, 0))
    return pl.pallas_call(
        body,
        out_shape=(_sds((t, hw), BF), _sds((t, LANE), F32)),
        grid=(t // tr,),
        in_specs=[pl.BlockSpec((tr, hw), lambda i: (i, 0)), pl.BlockSpec((tr, hw), lambda i: (i, 0)),
                  tab_spec, tab_spec, tab_spec],
        out_specs=(pl.BlockSpec((tr, hw), lambda i: (i, 0)), pl.BlockSpec((tr, LANE), lambda i: (i, 0))),
        compiler_params=_cp(("parallel",)),
        name="rope_bwd",
    )(dq_cat, dk_cat, *tabs)


ATT_SCALE = (MLA_NOPE + MLA_ROPE) ** -0.5
NT_DIMS = (((1,), (1,)), ((), ()))
TN_DIMS = (((0,), (0,)), ((), ()))


def _att_mask(qi, ki, tq, tk):
    qpos = qi * tq + lax.broadcasted_iota(jnp.int32, (tq, tk), 0)
    kpos = ki * tk + lax.broadcasted_iota(jnp.int32, (tq, tk), 1)
    return (kpos <= qpos) & (kpos >= PAD)


def _half_masks(n):
    lane = lax.broadcasted_iota(jnp.int32, (n, LANE), 1)
    return lane < 64, lane >= 64


def _att_tile(t):
    return _pick(t, (384, 256, 128))


def attn_fwd(q_cat, k_cat, v):
    t = q_cat.shape[0]
    tq = tk = _att_tile(t)
    nq = t // tq

    def body(q_ref, k_ref, v_ref, o_ref, lse_ref, m_s, l_s, acc_s):
        qi, ki = pl.program_id(1), pl.program_id(2)

        @pl.when(ki == 0)
        def _():
            m_s[...] = jnp.full_like(m_s, NEG)
            l_s[...] = jnp.zeros_like(l_s)
            acc_s[...] = jnp.zeros_like(acc_s)

        @pl.when(ki <= qi)
        def _():
            valid = _att_mask(qi, ki, tq, tk)
            lo, hi = _half_masks(tk)
            vv = v_ref[...]
            lo_q, _ = _half_masks(tq)
            alphas, pvs = [], []
            for hh in range(2):
                sl = slice(hh * LANE, (hh + 1) * LANE)
                s = lax.dot_general(q_ref[:, sl], k_ref[:, sl], NT_DIMS, preferred_element_type=F32) * ATT_SCALE
                s = jnp.where(valid, s, NEG)
                m_old = m_s[hh]
                m_new = jnp.maximum(m_old, jnp.max(s, axis=-1, keepdims=True))
                p = jnp.exp(s - m_new[:, 0:1])
                alpha = jnp.exp(m_old - m_new)
                l_s[hh] = alpha * l_s[hh] + jnp.sum(p, axis=-1, keepdims=True)
                m_s[hh] = m_new
                vm = jnp.where(lo if hh == 0 else hi, vv, jnp.zeros_like(vv))
                pvs.append(jnp.dot(p.astype(BF), vm, preferred_element_type=F32))
                alphas.append(alpha)
            acc_s[...] = acc_s[...] * jnp.where(lo_q, alphas[0], alphas[1]) + pvs[0] + pvs[1]

        @pl.when(ki == qi)
        def _():
            lo_q, _ = _half_masks(tq)
            l = jnp.where(lo_q, l_s[0], l_s[1])
            m = jnp.where(lo_q, m_s[0], m_s[1])
            o_ref[...] = (acc_s[...] / l).astype(o_ref.dtype)
            lse_ref[...] = m + jnp.log(l)

    return pl.pallas_call(
        body,
        out_shape=(_sds((t, 1024), BF), _sds((t, 1024), F32)),
        grid=(MLA_HEADS // 2, nq, nq),
        in_specs=[pl.BlockSpec((tq, 2 * LANE), lambda p, qi, ki: (qi, p)),
                  pl.BlockSpec((tk, 2 * LANE), lambda p, qi, ki: (jnp.minimum(ki, qi), p)),
                  pl.BlockSpec((tk, LANE), lambda p, qi, ki: (jnp.minimum(ki, qi), p))],
        out_specs=(pl.BlockSpec((tq, LANE), lambda p, qi, ki: (qi, p)),
                   pl.BlockSpec((tq, LANE), lambda p, qi, ki: (qi, p))),
        scratch_shapes=[pltpu.VMEM((2, tq, LANE), F32), pltpu.VMEM((2, tq, LANE), F32), pltpu.VMEM((tq, LANE), F32)],
        compiler_params=_cp(("parallel", "parallel", "arbitrary")),
        name="attn_fwd",
    )(q_cat, k_cat, v)


def _att_bwd_head(hh, q_ref, k_ref, vv, dov, ov, lse, valid):
    sl = slice(hh * LANE, (hh + 1) * LANE)
    half = _half_masks(dov.shape[0])[hh]
    s = lax.dot_general(q_ref[:, sl], k_ref[:, sl], NT_DIMS, preferred_element_type=F32) * ATT_SCALE
    s = jnp.where(valid, s, NEG)
    col = 0 if hh == 0 else 64
    p = jnp.exp(s - lse[:, col:col + 1])
    dom = jnp.where(half, dov, 0.0)
    delta = jnp.sum(dom * ov, axis=-1, keepdims=True)
    dp = lax.dot_general(dom.astype(BF), vv, NT_DIMS, preferred_element_type=F32)
    ds = p * (dp - delta) * ATT_SCALE
    return p, ds, dom


def attn_bwd(q_cat, k_cat, v, o, lse, dyab):
    t = q_cat.shape[0]
    tq = tk = _att_tile(t)
    nq = t // tq

    def dq_body(q_ref, k_ref, v_ref, o_ref, lse_ref, do_ref, dq_ref, dq_s):
        qi, ki = pl.program_id(1), pl.program_id(2)

        @pl.when(ki == 0)
        def _():
            dq_s[...] = jnp.zeros_like(dq_s)

        @pl.when(ki <= qi)
        def _():
            valid = _att_mask(qi, ki, tq, tk)
            vv, dov, ov, lse = v_ref[...], do_ref[...], o_ref[...].astype(F32), lse_ref[...]
            for hh in range(2):
                sl = slice(hh * LANE, (hh + 1) * LANE)
                _, ds, _ = _att_bwd_head(hh, q_ref, k_ref, vv, dov, ov, lse, valid)
                dq_s[:, sl] += jnp.dot(ds.astype(BF), k_ref[:, sl], preferred_element_type=F32)

        @pl.when(ki == qi)
        def _():
            dq_ref[...] = dq_s[...]

    def dkv_body(q_ref, k_ref, v_ref, o_ref, lse_ref, do_ref, dk_ref, dv_ref, dk_s, dv_s):
        ki, qj = pl.program_id(1), pl.program_id(2)
        qi = jnp.maximum(qj, ki)

        @pl.when(qj == 0)
        def _():
            dk_s[...] = jnp.zeros_like(dk_s)
            dv_s[...] = jnp.zeros_like(dv_s)

        @pl.when(qj >= ki)
        def _():
            valid = _att_mask(qi, ki, tq, tk)
            vv, dov, ov, lse = v_ref[...], do_ref[...], o_ref[...].astype(F32), lse_ref[...]
            for hh in range(2):
                sl = slice(hh * LANE, (hh + 1) * LANE)
                p, ds, dom = _att_bwd_head(hh, q_ref, k_ref, vv, dov, ov, lse, valid)
                dv_s[...] += lax.dot_general(p.astype(BF), dom.astype(BF), TN_DIMS, preferred_element_type=F32)
                dk_s[:, sl] += lax.dot_general(ds.astype(BF), q_ref[:, sl], TN_DIMS, preferred_element_type=F32)

        @pl.when(qj == nq - 1)
        def _():
            dk_ref[...] = dk_s[...]
            dv_ref[...] = dv_s[...]

    qspec = lambda f: pl.BlockSpec((tq, 2 * LANE), lambda p, a, b: (f(a, b), p))
    hspec = lambda f, off=0: pl.BlockSpec((tq, LANE), lambda p, a, b: (f(a, b), p + off))
    q_of = lambda qi, ki: qi
    k_of = lambda qi, ki: jnp.minimum(ki, qi)
    dq = pl.pallas_call(
        dq_body,
        out_shape=_sds((t, 2048), F32),
        grid=(MLA_HEADS // 2, nq, nq),
        in_specs=[qspec(q_of), qspec(k_of), hspec(k_of), hspec(q_of), hspec(q_of), hspec(q_of, 8)],
        out_specs=qspec(q_of),
        scratch_shapes=[pltpu.VMEM((tq, 2 * LANE), F32)],
        compiler_params=_cp(("parallel", "parallel", "arbitrary")),
        name="attn_bwd_dq",
    )(q_cat, k_cat, v, o, lse, dyab)
    q_of2 = lambda ki, qj: jnp.maximum(qj, ki)
    k_of2 = lambda ki, qj: ki
    dk, dv = pl.pallas_call(
        dkv_body,
        out_shape=(_sds((t, 2048), F32), _sds((t, 1024), F32)),
        grid=(MLA_HEADS // 2, nq, nq),
        in_specs=[qspec(q_of2), qspec(k_of2), hspec(k_of2), hspec(q_of2), hspec(q_of2), hspec(q_of2, 8)],
        out_specs=(qspec(k_of2), hspec(k_of2)),
        scratch_shapes=[pltpu.VMEM((tk, 2 * LANE), F32), pltpu.VMEM((tk, LANE), F32)],
        compiler_params=_cp(("parallel", "parallel", "arbitrary")),
        name="attn_bwd_dkv",
    )(q_cat, k_cat, v, o, lse, dyab)
    return dq, dk, dv


N_PAIR = SSD_HEADS // 2


def _hdot(a, b):
    return jnp.dot(a, b, precision=HI, preferred_element_type=F32)


def _ssd_chunk(xs, bg, cg, dtraw, hin, dt_bias, a_log, dskip, rowmask):
    ln = CHUNK
    row = lax.broadcasted_iota(jnp.int32, (ln, ln), 0)
    col = lax.broadcasted_iota(jnp.int32, (ln, ln), 1)
    causal = row >= col
    ltri = causal.astype(F32)
    ones = jnp.ones((ln, ln), F32)
    k16 = lax.broadcasted_iota(jnp.int32, (SSD_HEADS, LANE), 0)
    upper = (lax.broadcasted_iota(jnp.int32, (SSD_HEADS, LANE), 1) >= 64).astype(jnp.int32)
    lane = lax.broadcasted_iota(jnp.int32, (ln, LANE), 1)
    halves = (lane < 64, lane >= 64)

    dt = _softplus(dtraw + dt_bias) * rowmask
    da = dt * (-jnp.exp(a_log))
    acs = _hdot(ltri, da)
    tot = _hdot(ones, da)
    bm = [b * rowmask for b in bg]
    cm = [c * rowmask for c in cg]
    cb = [lax.dot_general(cm[g].astype(BF), bm[g].astype(BF), NT_DIMS, preferred_element_type=F32) for g in range(2)]
    ys, hout = [], []
    for p in range(N_PAIR):
        g = p // (N_PAIR // 2)
        e_p = (k16 == 2 * p + upper).astype(F32)
        xdt = xs[p] * _hdot(dt, e_p)
        y = jnp.zeros((ln, LANE), F32)
        snew = jnp.zeros((ln, LANE), F32)
        for hh in range(2):
            f_h = (k16 == 2 * p + hh).astype(F32)
            m = _hdot(acs, f_h)
            mt = _hdot(tot, f_h)
            dec = jnp.exp(jnp.where(causal, m - m.T, NEG))
            xm = jnp.where(halves[hh], xdt, 0.0).astype(BF)
            y = y + jnp.dot((cb[g] * dec).astype(BF), xm, preferred_element_type=F32)
            bd = bm[g] * jnp.exp(mt - m)
            snew = snew + lax.dot_general(bd.astype(BF), xm, TN_DIMS, preferred_element_type=F32)
        y_off = jnp.dot(cm[g].astype(BF), hin[p].astype(BF), preferred_element_type=F32) * jnp.exp(_hdot(acs, e_p))
        ys.append(y + y_off + _hdot(dskip, e_p) * xs[p])
        hout.append(jnp.exp(_hdot(tot, e_p)) * hin[p] + snew)
    return ys, hout


def _ssd_load(x_ref, dt_ref):
    xs = [x_ref[:, p * LANE:(p + 1) * LANE] for p in range(N_PAIR)]
    bg = [x_ref[:, SSD_D_INNER + g * LANE:SSD_D_INNER + (g + 1) * LANE] for g in range(2)]
    cg = [x_ref[:, SSD_D_INNER + (2 + g) * LANE:SSD_D_INNER + (3 + g) * LANE] for g in range(2)]
    return xs, bg, cg, dt_ref[:, 0:SSD_HEADS]


def _chunk_rowmask(c):
    return ((c * CHUNK + lax.broadcasted_iota(jnp.int32, (CHUNK, 1), 0)) >= PAD).astype(F32)


def ssd_fwd(xbc_c, proj, dt_bias, a_log, dskip):
    t = xbc_c.shape[0]
    nc = t // CHUNK

    def body(x_ref, dt_ref, dtb_ref, al_ref, d_ref, y_ref, hs_ref, h_s):
        c = pl.program_id(0)

        @pl.when(c == 0)
        def _():
            h_s[...] = jnp.zeros_like(h_s)

        xs, bg, cg, dtraw = _ssd_load(x_ref, dt_ref)
        hin = [h_s[p] for p in range(N_PAIR)]
        hs_ref[0] = h_s[...]
        ys, hout = _ssd_chunk(xs, bg, cg, dtraw, hin, dtb_ref[...], al_ref[...], d_ref[...], _chunk_rowmask(c))
        for p in range(N_PAIR):
            y_ref[:, p * LANE:(p + 1) * LANE] = ys[p]
            h_s[p] = hout[p]

    par = pl.BlockSpec((1, SSD_HEADS), lambda c: (0, 0))
    return pl.pallas_call(
        body,
        out_shape=(_sds((t, SSD_D_INNER), F32), _sds((nc, N_PAIR, CHUNK, LANE), F32)),
        grid=(nc,),
        in_specs=[pl.BlockSpec((CHUNK, SSD_CONV_CH), lambda c: (c, 0)),
                  pl.BlockSpec((CHUNK, LANE), lambda c: (c, OFF_DT // LANE)), par, par, par],
        out_specs=(pl.BlockSpec((CHUNK, SSD_D_INNER), lambda c: (c, 0)),
                   pl.BlockSpec((1, N_PAIR, CHUNK, LANE), lambda c: (c, 0, 0, 0))),
        scratch_shapes=[pltpu.VMEM((N_PAIR, CHUNK, LANE), F32)],
        compiler_params=_cp(("arbitrary",)),
        name="ssd_fwd",
    )(xbc_c, proj, dt_bias.reshape(1, -1), a_log.reshape(1, -1), dskip.reshape(1, -1))


def ssd_bwd(xbc_c, proj, dt_bias, a_log, dskip, hs, dy):
    t = xbc_c.shape[0]
    nc = t // CHUNK

    def body(x_ref, dt_ref, dtb_ref, al_ref, d_ref, hs_ref, dy_ref, dx_ref, ddt_ref, dpar_ref, dh_s):
        ci = pl.program_id(0)
        c = nc - 1 - ci

        @pl.when(ci == 0)
        def _():
            dh_s[...] = jnp.zeros_like(dh_s)
            dpar_ref[...] = jnp.zeros_like(dpar_ref)

        xs, bg, cg, dtraw = _ssd_load(x_ref, dt_ref)
        hin = [hs_ref[0, p] for p in range(N_PAIR)]
        rowmask = _chunk_rowmask(c)
        fn = lambda xs_, bg_, cg_, dtraw_, hin_, dtb_, al_, d_: _ssd_chunk(xs_, bg_, cg_, dtraw_, hin_, dtb_, al_, d_, rowmask)
        _, vjp = jax.vjp(fn, xs, bg, cg, dtraw, hin, dtb_ref[...], al_ref[...], d_ref[...])
        dys = [dy_ref[:, p * LANE:(p + 1) * LANE] for p in range(N_PAIR)]
        dhs = [dh_s[p] for p in range(N_PAIR)]
        dxs, dbg, dcg, ddtraw, dhin, ddtb, dal, dd = vjp((dys, dhs))
        for p in range(N_PAIR):
            dx_ref[:, p * LANE:(p + 1) * LANE] = dxs[p]
            dh_s[p] = dhin[p]
        for g in range(2):
            dx_ref[:, SSD_D_INNER + g * LANE:SSD_D_INNER + (g + 1) * LANE] = dbg[g]
            dx_ref[:, SSD_D_INNER + (2 + g) * LANE:SSD_D_INNER + (3 + g) * LANE] = dcg[g]
        ddt_ref[...] = jnp.zeros_like(ddt_ref)
        ddt_ref[:, 0:SSD_HEADS] = ddtraw
        dpar_ref[0:1, 0:SSD_HEADS] += ddtb
        dpar_ref[1:2, 0:SSD_HEADS] += dal
        dpar_ref[2:3, 0:SSD_HEADS] += dd

    par = pl.BlockSpec((1, SSD_HEADS), lambda ci: (0, 0))
    return pl.pallas_call(
        body,
        out_shape=(_sds((t, SSD_CONV_CH), F32), _sds((t, LANE), F32), _sds((8, LANE), F32)),
        grid=(nc,),
        in_specs=[pl.BlockSpec((CHUNK, SSD_CONV_CH), lambda ci: (nc - 1 - ci, 0)),
                  pl.BlockSpec((CHUNK, LANE), lambda ci: (nc - 1 - ci, OFF_DT // LANE)), par, par, par,
                  pl.BlockSpec((1, N_PAIR, CHUNK, LANE), lambda ci: (nc - 1 - ci, 0, 0, 0)),
                  pl.BlockSpec((CHUNK, SSD_D_INNER), lambda ci: (nc - 1 - ci, 0))],
        out_specs=(pl.BlockSpec((CHUNK, SSD_CONV_CH), lambda ci: (nc - 1 - ci, 0)),
                   pl.BlockSpec((CHUNK, LANE), lambda ci: (nc - 1 - ci, 0)),
                   pl.BlockSpec((8, LANE), lambda ci: (0, 0))),
        scratch_shapes=[pltpu.VMEM((N_PAIR, CHUNK, LANE), F32)],
        compiler_params=_cp(("arbitrary",)),
        name="ssd_bwd",
    )(xbc_c, proj, dt_bias.reshape(1, -1), a_log.reshape(1, -1), dskip.reshape(1, -1), hs, dy)


def _neg_expm1(y):
    series = -(y * (1.0 + y * (0.5 + y * (1.0 / 6.0 + y * (1.0 / 24.0 + y * (1.0 / 120.0))))))
    return jnp.where(y > -0.1, series, 1.0 - jnp.exp(y))


def _rg_pw(xr, wa, ba, wi, bi, lam, rowmask):
    xb = xr.astype(BF)
    r = jax.nn.sigmoid(jnp.dot(xb, wa.astype(BF), preferred_element_type=F32) + ba)
    i = jax.nn.sigmoid(jnp.dot(xb, wi.astype(BF), preferred_element_type=F32) + bi)
    log_a = -LRU_C * r * _softplus(-lam)
    a = jnp.exp(log_a)
    u = jnp.sqrt(_neg_expm1(2.0 * log_a)) * (i * xr) * rowmask
    return a, u


def _gelu_grad(x):
    c = math.sqrt(2.0 / math.pi)
    th = jnp.tanh(c * (x + 0.044715 * (x * x * x)))
    return 0.5 * (1.0 + th) + 0.5 * x * (1.0 - th * th) * c * (1.0 + 3.0 * 0.044715 * x * x)


def _scan_fwd(a, u):
    n = a.shape[0]
    row = lax.broadcasted_iota(jnp.int32, a.shape, 0)
    s = 1
    while s < n:
        a_s = jnp.where(row >= s, pltpu.roll(a, s, 0), 1.0)
        u_s = jnp.where(row >= s, pltpu.roll(u, s, 0), 0.0)
        u = u + a * u_s
        a = a * a_s
        s *= 2
    return a, u


def _scan_bwd(b, d):
    n = b.shape[0]
    row = lax.broadcasted_iota(jnp.int32, b.shape, 0)
    s = 1
    while s < n:
        b_s = jnp.where(row < n - s, pltpu.roll(b, n - s, 0), 1.0)
        d_s = jnp.where(row < n - s, pltpu.roll(d, n - s, 0), 0.0)
        d = d + b * d_s
        b = b * b_s
        s *= 2
    return d


def rg_fwd(xg, rgp, w_a, w_i):
    t = xg.shape[0]
    tr = _rt(t)

    def body(x_ref, g_ref, p_ref, wa_ref, wi_ref, hg_ref, hs_ref, prev, hcar):
        ti = pl.program_id(1)

        @pl.when(ti == 0)
        def _():
            prev[...] = jnp.zeros_like(prev)
            hcar[...] = jnp.zeros_like(hcar)

        xv = x_ref[...]
        ext = jnp.concatenate([prev[...], xv], axis=0)
        xr = _conv_pre(ext, xv, p_ref, tr)
        rowmask = _row_mask(ti, tr, (tr, 1)).astype(F32)
        a, u = _rg_pw(xr, wa_ref[0], p_ref[5:6, :], wi_ref[0], p_ref[6:7, :], p_ref[7:8, :], rowmask)
        a_cum, h_loc = _scan_fwd(a, u)
        hs = h_loc + a_cum * hcar[0:1, :]
        hs_ref[...] = hs
        hg_ref[...] = (hs * _gelu(g_ref[...])).astype(hg_ref.dtype)
        hcar[...] = jnp.broadcast_to(hs[tr - 1:tr, :], (8, LANE))
        prev[...] = xv[tr - 8:, :]

    return pl.pallas_call(
        body,
        out_shape=(_sds((t, LRU_WIDTH), BF), _sds((t, LRU_WIDTH), F32)),
        grid=(LRU_BLOCKS, t // tr),
        in_specs=[pl.BlockSpec((tr, LANE), lambda n, ti: (ti, n)),
                  pl.BlockSpec((tr, LANE), lambda n, ti: (ti, LRU_BLOCKS + n)),
                  pl.BlockSpec((8, LANE), lambda n, ti: (0, n)),
                  pl.BlockSpec((1, LANE, LANE), lambda n, ti: (n, 0, 0)),
                  pl.BlockSpec((1, LANE, LANE), lambda n, ti: (n, 0, 0))],
        out_specs=(pl.BlockSpec((tr, LANE), lambda n, ti: (ti, n)), pl.BlockSpec((tr, LANE), lambda n, ti: (ti, n))),
        scratch_shapes=[pltpu.VMEM((8, LANE), F32), pltpu.VMEM((8, LANE), F32)],
        compiler_params=_cp(("parallel", "arbitrary")),
        name="rg_fwd",
    )(xg, xg, rgp, w_a, w_i)


def rg_bwd(xg, rgp, w_a, w_i, hs, dhg):
    t = xg.shape[0]
    tr = _rt(t)
    nt = t // tr
    r8 = tr // 8

    def body(x_ref, xp_ref, g_ref, p_ref, wa_ref, wi_ref, hs_ref, hp_ref, dhg_ref,
             dx_ref, dg_ref, dp_ref, dwa_ref, dwi_ref, gcar, dnext):
        ti = pl.program_id(1)
        tt = nt - 1 - ti

        @pl.when(ti == 0)
        def _():
            gcar[...] = jnp.zeros_like(gcar)
            dnext[...] = jnp.zeros_like(dnext)
            dp_ref[...] = jnp.zeros_like(dp_ref)
            dwa_ref[...] = jnp.zeros_like(dwa_ref)
            dwi_ref[...] = jnp.zeros_like(dwi_ref)

        xv = x_ref[...]
        halo = jnp.where(tt > 0, xp_ref[...], 0.0)
        ext = jnp.concatenate([halo, xv], axis=0)
        xr = _conv_pre(ext, xv, p_ref, tr)
        rowmask = _row_mask(tt, tr, (tr, 1)).astype(F32)
        fn = lambda xr_, wa_, ba_, wi_, bi_, lam_: _rg_pw(xr_, wa_, ba_, wi_, bi_, lam_, rowmask)
        (a, _), vjp = jax.vjp(fn, xr, wa_ref[0], p_ref[5:6, :], wi_ref[0], p_ref[6:7, :], p_ref[7:8, :])
        gpre = g_ref[...]
        hsv = hs_ref[...]
        dhg_v = dhg_ref[...]
        dg_ref[...] = (dhg_v * hsv * _gelu_grad(gpre)).astype(dg_ref.dtype)
        row = lax.broadcasted_iota(jnp.int32, (tr, LANE), 0)
        d = dhg_v * _gelu(gpre) + jnp.where(row == tr - 1, gcar[0:1, :], 0.0)
        b = jnp.where(row < tr - 1, pltpu.roll(a, tr - 1, 0), 0.0)
        g = _scan_bwd(b, d)
        gcar[...] = jnp.broadcast_to(a[0:1, :] * g[0:1, :], (8, LANE))
        hlast = jnp.where(tt > 0, hp_ref[7:8, :], 0.0)
        hprev = jnp.where(row == 0, hlast, pltpu.roll(hsv, 1, 0))
        dxr, dwa, dba, dwi, dbi, dlam = vjp((g * hprev, g))
        dx, sums = _conv_bwd_parts(dxr, dnext[...], xv, ext, p_ref, tr)
        dx_ref[...] = dx.astype(dx_ref.dtype)
        dnext[...] = dxr[:8, :]
        dp_ref[...] += _rows_block(sums + [dba, dbi, dlam])
        dwa_ref[0] += dwa
        dwi_ref[0] += dwi

    tile = lambda off=0: pl.BlockSpec((tr, LANE), lambda n, ti: (nt - 1 - ti, off + n))
    halo = lambda off=0: pl.BlockSpec((8, LANE), lambda n, ti: (jnp.maximum((nt - 1 - ti) * r8 - 1, 0), off + n))
    par = pl.BlockSpec((8, LANE), lambda n, ti: (0, n))
    wspec = pl.BlockSpec((1, LANE, LANE), lambda n, ti: (n, 0, 0))
    return pl.pallas_call(
        body,
        out_shape=(_sds((t, LRU_WIDTH), BF), _sds((t, LRU_WIDTH), BF), _sds((8, LRU_WIDTH), F32),
                   _sds((LRU_BLOCKS, LANE, LANE), F32), _sds((LRU_BLOCKS, LANE, LANE), F32)),
        grid=(LRU_BLOCKS, nt),
        in_specs=[tile(), halo(), tile(LRU_BLOCKS), par, wspec, wspec, tile(), halo(), tile()],
        out_specs=(tile(), tile(), par, wspec, wspec),
        scratch_shapes=[pltpu.VMEM((8, LANE), F32), pltpu.VMEM((8, LANE), F32)],
        compiler_params=_cp(("parallel", "arbitrary")),
        name="rg_bwd",
    )(xg, xg, xg, rgp, w_a, w_i, hs, hs, dhg)


PACK_W = 1024
MESH_ID = pl.DeviceIdType.MESH
ANY = pl.BlockSpec(memory_space=pl.ANY)


def _my_place():
    x, y, c = lax.axis_index("x"), lax.axis_index("y"), lax.axis_index("c")
    return x, y, c


def _lin(px, py, pc):
    return 4 * px + 2 * py + pc


def all_gather(arrs, name):
    k = len(arrs)

    def body(*refs):
        ins, outs = refs[:k], refs[k:2 * k]
        send_sems, recv_sems, local_sems = refs[2 * k:]
        x, y, c = _my_place()
        me, sibling = (x, y, c), (x, y, 1 - c)
        chips = [(1 - x, y), (x, 1 - y), (1 - x, 1 - y)]

        def copy(a, sem, block, to, from_input=False):
            slab = outs[a].at[_lin(*block)]
            return pltpu.make_async_remote_copy(
                src_ref=ins[a] if from_input else slab, dst_ref=slab,
                send_sem=send_sems.at[a, sem], recv_sem=recv_sems.at[a, sem],
                device_id=to, device_id_type=MESH_ID)

        mine = [pltpu.make_async_copy(ins[a], outs[a].at[_lin(*me)], local_sems.at[a]) for a in range(k)]
        for cp in mine:
            cp.start()
        first = []
        for a in range(k):
            first.append(copy(a, 0, me, sibling, True))
            first += [copy(a, 1 + j, me, (*chip, c), True) for j, chip in enumerate(chips)]
        for cp in first:
            cp.start()
        passed = []
        for j, chip in enumerate(chips):
            for a in range(k):
                copy(a, 1 + j, (*chip, c), me).wait_recv()
                fwd = copy(a, 4 + j, (*chip, c), sibling)
                fwd.start()
                passed.append(fwd)
        for a in range(k):
            copy(a, 0, sibling, me).wait_recv()
            for j, chip in enumerate(chips):
                copy(a, 4 + j, (*chip, 1 - c), me).wait_recv()
        for cp in first + passed:
            cp.wait_send()
        for cp in mine:
            cp.wait()

    return pl.pallas_call(
        body,
        out_shape=tuple(_sds((N_DEV,) + a.shape, a.dtype) for a in arrs),
        in_specs=[ANY] * k,
        out_specs=tuple(ANY for _ in arrs),
        scratch_shapes=[pltpu.SemaphoreType.DMA((k, 7)), pltpu.SemaphoreType.DMA((k, 7)), pltpu.SemaphoreType.DMA((k,))],
        name=name,
    )(*arrs)


def all_to_all(arrs, name):
    k = len(arrs)

    def body(*refs):
        ins, outs = refs[:k], refs[k:2 * k]
        send_sems, recv_sems, local_sems = refs[2 * k:]
        x, y, c = _my_place()
        me = _lin(x, y, c)
        peers = []
        for r in range(1, N_DEV):
            peers.append(((1 - x) if r & 4 else x, (1 - y) if r & 2 else y, (1 - c) if r & 1 else c))

        def copy(a, r, src_slab, dst_slab, to):
            return pltpu.make_async_remote_copy(
                src_ref=ins[a].at[src_slab], dst_ref=outs[a].at[dst_slab],
                send_sem=send_sems.at[a, r], recv_sem=recv_sems.at[a, r],
                device_id=to, device_id_type=MESH_ID)

        mine = [pltpu.make_async_copy(ins[a].at[me], outs[a].at[me], local_sems.at[a]) for a in range(k)]
        for cp in mine:
            cp.start()
        sends = [copy(a, r, _lin(*peer), me, peer) for r, peer in enumerate(peers) for a in range(k)]
        for cp in sends:
            cp.start()
        for r, peer in enumerate(peers):
            for a in range(k):
                copy(a, r, me, _lin(*peer), peer).wait_recv()
        for cp in sends:
            cp.wait_send()
        for cp in mine:
            cp.wait()

    return pl.pallas_call(
        body,
        out_shape=tuple(_sds(a.shape, a.dtype) for a in arrs),
        in_specs=[ANY] * k,
        out_specs=tuple(ANY for _ in arrs),
        scratch_shapes=[pltpu.SemaphoreType.DMA((k, 7)), pltpu.SemaphoreType.DMA((k, 7)), pltpu.SemaphoreType.DMA((k,))],
        name=name,
    )(*arrs)


def slab_sum(a, name):
    _, r, w = a.shape
    tr = _pick(r, (256, 128, 64, 32, 16, 8))

    def body(a_ref, o_ref):
        acc = a_ref[0].astype(F32)
        for d in range(1, N_DEV):
            acc = acc + a_ref[d].astype(F32)
        o_ref[...] = acc

    return pl.pallas_call(
        body,
        out_shape=_sds((r, w), F32),
        grid=(r // tr,),
        in_specs=[pl.BlockSpec((N_DEV, tr, w), lambda i: (0, i, 0))],
        out_specs=pl.BlockSpec((tr, w), lambda i: (i, 0)),
        compiler_params=_cp(("parallel",)),
        name=name,
    )(a)


def adamw(w, g, m, v, name):
    r, c = w.shape
    tr = _pick(r, (256, 160, 128, 64, 32, 16, 8))

    def body(w_ref, g_ref, m_ref, v_ref, d_ref, nm_ref, nv_ref):
        gv = g_ref[...]
        nm = ADAM_B1 * m_ref[...] + (1.0 - ADAM_B1) * gv
        nv = ADAM_B2 * v_ref[...] + (1.0 - ADAM_B2) * (gv * gv)
        m_hat = nm / (1.0 - ADAM_B1 ** ADAM_STEP)
        v_hat = nv / (1.0 - ADAM_B2 ** ADAM_STEP)
        d_ref[...] = -ADAM_LR * (m_hat / (jnp.sqrt(v_hat) + ADAM_EPS) + ADAM_WD * w_ref[...])
        nm_ref[...] = nm
        nv_ref[...] = nv

    spec = pl.BlockSpec((tr, c), lambda i: (i, 0))
    return pl.pallas_call(
        body,
        out_shape=tuple(_sds((r, c), F32) for _ in range(3)),
        grid=(r // tr,),
        in_specs=[spec] * 4,
        out_specs=(spec, spec, spec),
        compiler_params=_cp(("parallel",)),
        name=name,
    )(w, g, m, v)


def _relu2_epi(acc):
    r = jnp.maximum(acc, 0.0)
    return r * r, r


def _drelu2_epi(acc, r):
    return (acc * (2.0 * r.astype(F32)),)


def mlp_fwd(h, g_pre, g_post, w_up, w_down):
    hn = norm_fwd(h, g_pre, BF, name="mlp_norm")
    u, r = matmul(hn, w_up, "nn", (BF, BF), epi=_relu2_epi, name="mlp_up")
    d = matmul(u, w_down, "nn", name="mlp_down")
    return resadd_fwd(h, d, g_post, name="mlp_res"), (h, hn, u, r, d)


def mlp_bwd(res, dh2, g_pre, g_post, w_up, w_down):
    h, hn, u, r, d = res
    dd, dg_post = norm_bwd(d, g_post, dh2, mask_pad=True, out_dtype=BF, name="mlp_post_bwd")
    dw_down = matmul(u, dd, "tn", name="mlp_dwdown")
    dp = matmul(dd, w_down, "nt", (BF,), epi=_drelu2_epi, extras=(r,), name="mlp_du")
    dw_up = matmul(hn, dp, "tn", name="mlp_dwup")
    dhn = matmul(dp, w_up, "nt", name="mlp_dhn")
    dh, dg_pre = norm_bwd(h, g_pre, dhn, dres=dh2, name="mlp_pre_bwd")
    return dh, dict(mlp_pre_g=dg_pre, mlp_post_g=dg_post, w_up=dw_up, w_down=dw_down)


def rg_layer_fwd(h, g_pre, g_post, w_xy, rgp, w_a, w_i, w_out):
    hn = norm_fwd(h, g_pre, BF, name="rg_norm")
    xg = matmul(hn, w_xy, "nn", name="rg_in")
    hg, hs = rg_fwd(xg, rgp, w_a, w_i)
    m = matmul(hg, w_out, "nn", name="rg_out")
    return resadd_fwd(h, m, g_post, name="rg_res"), (h, hn, xg, hg, hs, m)


def rg_layer_bwd(res, dh2, g_pre, g_post, w_xy, rgp, w_a, w_i, w_out):
    h, hn, xg, hg, hs, m = res
    dm, dg_post = norm_bwd(m, g_post, dh2, mask_pad=True, out_dtype=BF, name="rg_post_bwd")
    dw_out = matmul(hg, dm, "tn", name="rg_dwout")
    dhg = matmul(dm, w_out, "nt", name="rg_dhg")
    dxr, dgp, drgp, dwa, dwi = rg_bwd(xg, rgp, w_a, w_i, hs, dhg)
    dxg = jnp.concatenate([dxr, dgp], axis=1)
    dw_xy = matmul(hn, dxg, "tn", name="rg_dwin")
    dhn = matmul(dxg, w_xy, "nt", name="rg_dhn")
    dh, dg_pre = norm_bwd(h, g_pre, dhn, dres=dh2, name="rg_pre_bwd")
    return dh, dict(mix_pre_g=dg_pre, mix_post_g=dg_post, rg_w_x=dw_xy[:, :LRU_WIDTH], rg_w_y=dw_xy[:, LRU_WIDTH:],
                    rg_conv_w=drgp[0:4], rg_conv_b=drgp[4], rg_b_a=drgp[5], rg_b_i=drgp[6], rg_lambda=drgp[7],
                    rg_w_a=dwa, rg_w_i=dwi, rg_w_out=dw_out)


def sm_layer_fwd(h, g_pre, g_post, w_in_p, convp, dt_bias, a_log, dskip, ssd_g, q_g, w_q_p, kv_g, w_kv_p, w_out, tabs):
    hn = norm_fwd(h, g_pre, BF, name="sm_norm")
    proj = matmul(hn, w_in_p, "nn", name="sm_in")
    xbc_c = conv_silu_fwd(proj, OFF_XBC // LANE, SSD_CONV_CH // LANE, convp, name="ssd_conv")
    y, hst = ssd_fwd(xbc_c, proj, dt_bias, a_log, dskip)
    y_ssd = gated_norm_fwd(y, proj, ssd_g)
    cqn = norm_fwd(proj, q_g, BF, col_blk=OFF_CQ // MLA_Q_RANK, width=MLA_Q_RANK, name="q_norm")
    q_raw = matmul(cqn, w_q_p, "nn", name="q_up")
    ckvn = norm_fwd(proj, kv_g, BF, col_blk=OFF_CKV // MLA_KV_RANK, width=MLA_KV_RANK, name="kv_norm")
    kv_raw = matmul(ckvn, w_kv_p, "nn", name="kv_up")
    q_cat, k_cat, v = rope_fwd(q_raw, kv_raw, proj, tabs)
    o, lse = attn_fwd(q_cat, k_cat, v)
    yab = jnp.concatenate([y_ssd, o], axis=1)
    m = matmul(yab, w_out, "nn", name="sm_out")
    return resadd_fwd(h, m, g_post, name="sm_res"), (h, hn, proj, xbc_c, y, hst, cqn, ckvn, q_cat, k_cat, v, o, lse, yab, m)


def sm_layer_bwd(res, dh2, g_pre, g_post, w_in_p, convp, dt_bias, a_log, dskip, ssd_g, q_g, w_q_p, kv_g, w_kv_p, w_out, tabs):
    h, hn, proj, xbc_c, y, hst, cqn, ckvn, q_cat, k_cat, v, o, lse, yab, m = res
    dm, dg_post = norm_bwd(m, g_post, dh2, mask_pad=True, out_dtype=BF, name="sm_post_bwd")
    dw_out = matmul(yab, dm, "tn", name="sm_dwout")
    dyab = matmul(dm, w_out, "nt", name="sm_dyab")
    dq_cat, dk_cat, dv = attn_bwd(q_cat, k_cat, v, o, lse, dyab)
    dq_raw, dkr = rope_bwd(dq_cat, dk_cat, tabs)
    dkv_raw = jnp.concatenate([dk_cat, dv], axis=1).astype(BF)
    dw_kv_p = matmul(ckvn, dkv_raw, "tn", name="kv_dw")
    dckvn = matmul(dkv_raw, w_kv_p, "nt", name="kv_dx")
    dckv, dg_kv = norm_bwd(proj, kv_g, dckvn, col_blk=OFF_CKV // MLA_KV_RANK, width=MLA_KV_RANK, name="kv_norm_bwd")
    dw_q_p = matmul(cqn, dq_raw, "tn", name="q_dw")
    dcqn = matmul(dq_raw, w_q_p, "nt", name="q_dx")
    dcq, dg_q = norm_bwd(proj, q_g, dcqn, col_blk=OFF_CQ // MLA_Q_RANK, width=MLA_Q_RANK, name="q_norm_bwd")
    dy, dz, dg_ssd = gated_norm_bwd(y, proj, ssd_g, dyab)
    dxbc_c, ddt, dpar = ssd_bwd(xbc_c, proj, dt_bias, a_log, dskip, hst, dy)
    dxbc, dconvp = conv_silu_bwd(proj, OFF_XBC // LANE, SSD_CONV_CH // LANE, convp, dxbc_c, name="ssd_conv_bwd")
    dproj = jnp.concatenate([dz, dxbc, dckv, ddt, dkr, dcq], axis=1).astype(BF)
    dw_in_p = matmul(hn, dproj, "tn", name="sm_dwin")
    dhn = matmul(dproj, w_in_p, "nt", name="sm_dhn")
    dh, dg_pre = norm_bwd(h, g_pre, dhn, dres=dh2, name="sm_pre_bwd")
    return dh, dict(mix_pre_g=dg_pre, mix_post_g=dg_post, w_in=_unpack_w_in(dw_in_p), ssd_conv_w=dconvp[0:4],
                    ssd_conv_b=dconvp[4], ssd_dt_bias=dpar[0, :SSD_HEADS], ssd_a_log=dpar[1, :SSD_HEADS],
                    ssd_d=dpar[2, :SSD_HEADS], ssd_norm_g=dg_ssd, mla_q_norm_g=dg_q, mla_w_q_up=_unpack_w_q(dw_q_p),
                    mla_kv_norm_g=dg_kv, mla_w_kv_up=_unpack_w_kv(dw_kv_p), w_out_ab=dw_out)


def _pack_w_in(w):
    z_xbc, dt, cq, ckv, kr = w[:, :2560], w[:, 2560:2576], w[:, 2576:2960], w[:, 2960:3216], w[:, 3216:3248]
    zeros = lambda n: jnp.zeros((w.shape[0], n), w.dtype)
    return jnp.concatenate([z_xbc, ckv, dt, zeros(112), zeros(64), kr, zeros(32), cq], axis=1)


def _unpack_w_in(p):
    return jnp.concatenate([p[:, :2560], p[:, OFF_DT:OFF_DT + 16], p[:, OFF_CQ:OFF_CQ + 384],
                            p[:, OFF_CKV:OFF_CKV + 256], p[:, OFF_KR + 64:OFF_KR + 96]], axis=1)


def _pack_w_q(w):
    w3 = w.reshape(w.shape[0], MLA_HEADS, MLA_NOPE + MLA_ROPE)
    return jnp.pad(w3, ((0, 0), (0, 0), (0, LANE - MLA_NOPE - MLA_ROPE))).reshape(w.shape[0], MLA_HEADS * LANE)


def _unpack_w_q(p):
    return p.reshape(p.shape[0], MLA_HEADS, LANE)[:, :, :MLA_NOPE + MLA_ROPE].reshape(p.shape[0], -1)


def _pack_w_kv(w):
    w3 = w.reshape(w.shape[0], MLA_HEADS, MLA_NOPE + MLA_V)
    k = jnp.pad(w3[:, :, :MLA_NOPE], ((0, 0), (0, 0), (0, LANE - MLA_NOPE))).reshape(w.shape[0], MLA_HEADS * LANE)
    return jnp.concatenate([k, w3[:, :, MLA_NOPE:].reshape(w.shape[0], MLA_HEADS * MLA_V)], axis=1)


def _unpack_w_kv(p):
    k = p[:, :MLA_HEADS * LANE].reshape(p.shape[0], MLA_HEADS, LANE)[:, :, :MLA_NOPE]
    v = p[:, MLA_HEADS * LANE:].reshape(p.shape[0], MLA_HEADS, MLA_V)
    return jnp.concatenate([k, v], axis=2).reshape(p.shape[0], -1)


def _rows8(rows, width):
    a = jnp.concatenate([r.reshape(-1, width) for r in rows], axis=0)
    return jnp.pad(a, ((0, 8 - a.shape[0]), (0, 0)))


SLAB_ROWS = 256


def _to_slab(flat_list, lead=()):
    cat = jnp.concatenate(flat_list, axis=-1)
    n = cat.shape[-1]
    unit = SLAB_ROWS * PACK_W
    total = -(-n // unit) * unit
    cat = jnp.pad(cat, [(0, 0)] * len(lead) + [(0, total - n)])
    return cat.reshape(lead + (total // PACK_W, PACK_W))


def _from_flat(flat, shapes):
    out, off = [], 0
    for s in shapes:
        n = int(np.prod(s))
        out.append(flat[off:off + n].reshape(s))
        off += n
    return out


def _gathered_full(g8, axis):
    moved = jnp.moveaxis(g8, 0, axis)
    shp = moved.shape
    return moved.reshape(shp[:axis] + (shp[axis] * shp[axis + 1],) + shp[axis + 2:])


def _per_device(full, axis):
    shp = full.shape
    split = full.reshape(shp[:axis] + (N_DEV, shp[axis] // N_DEV) + shp[axis + 1:])
    return jnp.moveaxis(split, axis, 0)


ARG_NAMES = ['x', 'meta_tokens', 'mix_pre_g', 'mix_post_g', 'mlp_pre_g', 'mlp_post_g', 'w_up', 'w_down', 'w_in',
             'ssd_conv_w', 'ssd_conv_b', 'ssd_dt_bias', 'ssd_a_log', 'ssd_d', 'ssd_norm_g', 'mla_q_norm_g',
             'mla_w_q_up', 'mla_kv_norm_g', 'mla_w_kv_up', 'w_out_ab', 'rg_w_x', 'rg_w_y', 'rg_conv_w', 'rg_conv_b',
             'rg_w_a', 'rg_b_a', 'rg_w_i', 'rg_b_i', 'rg_lambda', 'rg_w_out']
WEIGHTS = ARG_NAMES[1:]
BIG = {'w_up': 2, 'w_down': 1, 'w_in': 2, 'mla_w_q_up': 2, 'mla_w_kv_up': 2, 'w_out_ab': 1, 'rg_w_x': 2,
       'rg_w_y': 2, 'rg_w_out': 1}
SMALL = {'meta_tokens': 1, 'ssd_conv_w': 2, 'rg_conv_w': 2, 'rg_conv_b': 1, 'rg_b_a': 1, 'rg_b_i': 1, 'rg_lambda': 1}
REPL = [n for n in WEIGHTS if n not in BIG and n not in SMALL]


def _step(p, moments):
    big_slab = _to_slab([p[n].astype(BF).reshape(-1) for n in BIG])
    small_slab = _to_slab([p[n].reshape(-1) for n in SMALL])
    big8, small8 = all_gather([big_slab, small_slab], name="gather_weights")
    full = {}
    for names, g8 in ((BIG, big8), (SMALL, small8)):
        pieces = _from_flat_rows(g8, [p[n].shape for n in names])
        for n, piece in zip(names, pieces):
            full[n] = _gathered_full(piece, names[n])
    for n in REPL:
        full[n] = p[n]
    loss_local, grad_x, gfull = _local_step(full, p['x'][0], p['loss_target'][0])

    rep_flat = jnp.concatenate([gfull[n].reshape(-1) for n in REPL])
    rep_n = rep_flat.shape[0]
    rep_chunk = -(-rep_n // (N_DEV * PACK_W * 8)) * PACK_W * 8
    rep8 = jnp.pad(rep_flat, (0, N_DEV * rep_chunk - rep_n)).reshape(N_DEV, rep_chunk)
    gbig = _to_slab([_per_device(gfull[n], BIG[n]).reshape(N_DEV, -1).astype(BF) for n in BIG], lead=(N_DEV,))
    gsmall = _to_slab([_per_device(gfull[n], SMALL[n]).reshape(N_DEV, -1) for n in SMALL] + [rep8], lead=(N_DEV,))
    rbig, rsmall = all_to_all([gbig, gsmall], name="scatter_grads")
    sbig = slab_sum(rbig, name="sum_big").reshape(-1)
    ssmall = slab_sum(rsmall, name="sum_small").reshape(-1)
    g_loc = dict(zip(BIG, _from_flat(sbig, [p[n].shape for n in BIG])))
    small_n = sum(int(np.prod(p[n].shape)) for n in SMALL)
    g_loc.update(zip(SMALL, _from_flat(ssmall, [p[n].shape for n in SMALL])))
    rep_mine = ssmall[small_n:small_n + rep_chunk].reshape(-1, PACK_W)
    (rep_all,) = all_gather([rep_mine], name="gather_replicated")
    g_loc.update(zip(REPL, _from_flat(rep_all.reshape(-1), [p[n].shape for n in REPL])))

    out = {'loss': lax.psum(loss_local, ("x", "y", "c")), 'grad_x': grad_x[None]}
    small_names = list(SMALL) + REPL
    for n in BIG:
        shp = p[n].shape
        v2 = lambda a: a.reshape(-1, shp[-1])
        d, nm, nv = adamw(v2(p[n]), v2(g_loc[n]), v2(moments['m_' + n]), v2(moments['v_' + n]), name="adamw_" + n)
        out['delta_' + n], out['new_m_' + n], out['new_v_' + n] = d.reshape(shp), nm.reshape(shp), nv.reshape(shp)
    slab = lambda src: _to_slab([src(n).reshape(-1) for n in small_names])
    d, nm, nv = adamw(slab(lambda n: p[n]), slab(lambda n: g_loc[n]), slab(lambda n: moments['m_' + n]),
                      slab(lambda n: moments['v_' + n]), name="adamw_small")
    shapes = [p[n].shape for n in small_names]
    for key, flat in (('delta_', d), ('new_m_', nm), ('new_v_', nv)):
        for n, a in zip(small_names, _from_flat(flat.reshape(-1), shapes)):
            out[key + n] = a
    for n in WEIGHTS:
        out['grad_' + n] = g_loc[n]
    return out


def _local_step(full, x, target_rows):
    t = PAD + N_META + x.shape[0]
    h = jnp.concatenate([jnp.zeros((PAD, D_MODEL), F32), full['meta_tokens'], x], axis=0)
    target = jnp.concatenate([jnp.zeros((PAD + N_META, D_MODEL), F32), target_rows], axis=0)
    tabs = rope_tables(t)

    def layer_args(layer):
        i = layer // 2
        if layer % 2 == 0:
            convp = _rows8([full['ssd_conv_w'][i], full['ssd_conv_b'][i]], SSD_CONV_CH)
            return (full['mix_pre_g'][layer], full['mix_post_g'][layer], _pack_w_in(full['w_in'][i]), convp,
                    full['ssd_dt_bias'][i], full['ssd_a_log'][i], full['ssd_d'][i], full['ssd_norm_g'][i],
                    full['mla_q_norm_g'][i], _pack_w_q(full['mla_w_q_up'][i]), full['mla_kv_norm_g'][i],
                    _pack_w_kv(full['mla_w_kv_up'][i]), full['w_out_ab'][i], tabs)
        rgp = _rows8([full['rg_conv_w'][i], full['rg_conv_b'][i], full['rg_b_a'][i], full['rg_b_i'][i],
                      full['rg_lambda'][i]], LRU_WIDTH)
        w_xy = jnp.concatenate([full['rg_w_x'][i], full['rg_w_y'][i]], axis=1)
        return (full['mix_pre_g'][layer], full['mix_post_g'][layer], w_xy, rgp, full['rg_w_a'][i], full['rg_w_i'][i],
                full['rg_w_out'][i])

    def mlp_args(layer):
        return (full['mlp_pre_g'][layer], full['mlp_post_g'][layer], full['w_up'][layer], full['w_down'][layer])

    saved = []
    for layer in range(DEPTH):
        la, ma = layer_args(layer), mlp_args(layer)
        h, res_mix = (sm_layer_fwd if layer % 2 == 0 else rg_layer_fwd)(h, *la)
        h, res_mlp = mlp_fwd(h, *ma)
        saved.append((la, ma, res_mix, res_mlp))
    loss_local, dh = loss_fwd_bwd(h, target)

    grads = {n: [None] * full[n].shape[0] for n in WEIGHTS if n != 'meta_tokens'}
    for layer in reversed(range(DEPTH)):
        la, ma, res_mix, res_mlp = saved[layer]
        dh, gm = mlp_bwd(res_mlp, dh, *ma)
        dh, gx = (sm_layer_bwd if layer % 2 == 0 else rg_layer_bwd)(res_mix, dh, *la)
        for n, g in gm.items():
            grads[n][layer] = g
        for n, g in gx.items():
            grads[n][layer if n in ('mix_pre_g', 'mix_post_g') else layer // 2] = g
    gfull = {n: jnp.stack(v, axis=0) for n, v in grads.items()}
    gfull['meta_tokens'] = dh[PAD:PAD + N_META]
    return loss_local, dh[PAD + N_META:], gfull


def _from_flat_rows(g8, shapes):
    flat = g8.reshape(N_DEV, -1)
    out, off = [], 0
    for s in shapes:
        n = int(np.prod(s))
        out.append(flat[:, off:off + n].reshape((N_DEV,) + tuple(s)))
        off += n
    return out


def kernel(x, meta_tokens, mix_pre_g, mix_post_g, mlp_pre_g, mlp_post_g, w_up, w_down, w_in, ssd_conv_w, ssd_conv_b, ssd_dt_bias, ssd_a_log, ssd_d, ssd_norm_g, mla_q_norm_g, mla_w_q_up, mla_kv_norm_g, mla_w_kv_up, w_out_ab, rg_w_x, rg_w_y, rg_conv_w, rg_conv_b, rg_w_a, rg_b_a, rg_w_i, rg_b_i, rg_lambda, rg_w_out, loss_target, m_meta_tokens, m_mix_pre_g, m_mix_post_g, m_mlp_pre_g, m_mlp_post_g, m_w_up, m_w_down, m_w_in, m_ssd_conv_w, m_ssd_conv_b, m_ssd_dt_bias, m_ssd_a_log, m_ssd_d, m_ssd_norm_g, m_mla_q_norm_g, m_mla_w_q_up, m_mla_kv_norm_g, m_mla_w_kv_up, m_w_out_ab, m_rg_w_x, m_rg_w_y, m_rg_conv_w, m_rg_conv_b, m_rg_w_a, m_rg_b_a, m_rg_w_i, m_rg_b_i, m_rg_lambda, m_rg_w_out, v_meta_tokens, v_mix_pre_g, v_mix_post_g, v_mlp_pre_g, v_mlp_post_g, v_w_up, v_w_down, v_w_in, v_ssd_conv_w, v_ssd_conv_b, v_ssd_dt_bias, v_ssd_a_log, v_ssd_d, v_ssd_norm_g, v_mla_q_norm_g, v_mla_w_q_up, v_mla_kv_norm_g, v_mla_w_kv_up, v_w_out_ab, v_rg_w_x, v_rg_w_y, v_rg_conv_w, v_rg_conv_b, v_rg_w_a, v_rg_b_a, v_rg_w_i, v_rg_b_i, v_rg_lambda, v_rg_w_out):
    args = (x, meta_tokens, mix_pre_g, mix_post_g, mlp_pre_g, mlp_post_g, w_up, w_down, w_in, ssd_conv_w, ssd_conv_b, ssd_dt_bias, ssd_a_log, ssd_d, ssd_norm_g, mla_q_norm_g, mla_w_q_up, mla_kv_norm_g, mla_w_kv_up, w_out_ab, rg_w_x, rg_w_y, rg_conv_w, rg_conv_b, rg_w_a, rg_b_a, rg_w_i, rg_b_i, rg_lambda, rg_w_out, loss_target, m_meta_tokens, m_mix_pre_g, m_mix_post_g, m_mlp_pre_g, m_mlp_post_g, m_w_up, m_w_down, m_w_in, m_ssd_conv_w, m_ssd_conv_b, m_ssd_dt_bias, m_ssd_a_log, m_ssd_d, m_ssd_norm_g, m_mla_q_norm_g, m_mla_w_q_up, m_mla_kv_norm_g, m_mla_w_kv_up, m_w_out_ab, m_rg_w_x, m_rg_w_y, m_rg_conv_w, m_rg_conv_b, m_rg_w_a, m_rg_b_a, m_rg_w_i, m_rg_b_i, m_rg_lambda, m_rg_w_out, v_meta_tokens, v_mix_pre_g, v_mix_post_g, v_mlp_pre_g, v_mlp_post_g, v_w_up, v_w_down, v_w_in, v_ssd_conv_w, v_ssd_conv_b, v_ssd_dt_bias, v_ssd_a_log, v_ssd_d, v_ssd_norm_g, v_mla_q_norm_g, v_mla_w_q_up, v_mla_kv_norm_g, v_mla_w_kv_up, v_w_out_ab, v_rg_w_x, v_rg_w_y, v_rg_conv_w, v_rg_conv_b, v_rg_w_a, v_rg_b_a, v_rg_w_i, v_rg_b_i, v_rg_lambda, v_rg_w_out,)
    n_w = len(ARG_NAMES)
    p = dict(zip(ARG_NAMES, args[:n_w]))
    p['loss_target'] = args[n_w]
    moments = {}
    for i, n in enumerate(WEIGHTS):
        moments['m_' + n] = args[n_w + 1 + i]
        moments['v_' + n] = args[n_w + 1 + len(WEIGHTS) + i]
    out = _step(p, moments)
    res = [out['loss'], out['grad_x']]
    for prefix in ('grad_', 'delta_', 'new_m_', 'new_v_'):
        res += [out[prefix + n] for n in WEIGHTS]
    return tuple(res)
```

```python
import functools
import math

import numpy as np
import jax
import jax.numpy as jnp
from jax import lax
from jax.experimental import pallas as pl
from jax.experimental.pallas import tpu as pltpu

F32 = jnp.float32
BF = jnp.bfloat16
HI = lax.Precision.HIGHEST

D_MODEL = 1024
DEPTH = 4
N_META = 16
CHUNK = 128
PAD = CHUNK - N_META
EPS = 1e-6
SSD_HEADS = 16
SSD_HEAD_DIM = 64
SSD_D_INNER = 1024
SSD_STATE = 128
SSD_CONV_CH = 1536
MLA_HEADS = 16
MLA_NOPE = 64
MLA_ROPE = 32
MLA_V = 64
MLA_Q_RANK = 384
MLA_KV_RANK = 256
ROPE_BASE = 10000.0
LRU_WIDTH = 1280
LRU_BLOCKS = 10
LRU_C = 8.0
D_FF = 4096
N_DEV = 8
LANE = 128
IN_W = 3456
OFF_Z, OFF_XBC, OFF_CKV, OFF_DT, OFF_KR, OFF_CQ = 0, 1024, 2560, 2816, 2944, 3072

ADAM_LR = 0.001
ADAM_B1 = 0.9
ADAM_B2 = 0.999
ADAM_EPS = 1e-08
ADAM_WD = 0.01
ADAM_STEP = 10

VMEM_LIMIT = 56 * 1024 * 1024
NEG = -1e30


def _pick(n, cands):
    for c in cands:
        if n % c == 0:
            return c
    return n


def _cp(sem=None):
    return pltpu.CompilerParams(dimension_semantics=sem, vmem_limit_bytes=VMEM_LIMIT)


def _sds(shape, dtype):
    return jax.ShapeDtypeStruct(tuple(shape), dtype)


def _silu(x):
    return x * jax.nn.sigmoid(x)


def _softplus(x):
    return jnp.maximum(x, 0.0) + jnp.log(1.0 + jnp.exp(-jnp.abs(x)))


def _gelu(x):
    c = math.sqrt(2.0 / math.pi)
    return 0.5 * x * (1.0 + jnp.tanh(c * (x + 0.044715 * (x * x * x))))


def _row_mask(i, tr, shape, first_valid=PAD):
    row = i * tr + lax.broadcasted_iota(jnp.int32, shape, 0)
    return row >= first_valid


def matmul(a, b, mode, out_dtypes=(F32,), epi=None, extras=(), name="mm", tm=None, tn=None):
    if mode == "nn":
        (m, k), (k2, n) = a.shape, b.shape
    elif mode == "nt":
        (m, k), (n, k2) = a.shape, b.shape
    else:
        (k, m), (k2, n) = a.shape, b.shape
    assert k == k2, (a.shape, b.shape, mode)
    tm = tm or _pick(m, (1056, 1024, 768, 640, 512, 384, 256, 128))
    tn = tn or _pick(n, (512, 640, 384, 256, 128))
    if mode == "tn":
        tm = _pick(m, (512, 384, 256, 128))
    dims = {"nn": (((1,), (0,)), ((), ())), "nt": (((1,), (1,)), ((), ())), "tn": (((0,), (0,)), ((), ()))}[mode]
    n_ex = len(extras)

    def body(a_ref, b_ref, *rest):
        ex_refs, out_refs = rest[:n_ex], rest[n_ex:]
        acc = lax.dot_general(a_ref[...].astype(BF), b_ref[...].astype(BF), dims, preferred_element_type=F32)
        outs = (acc,) if epi is None else epi(acc, *[r[...] for r in ex_refs])
        for r, o in zip(out_refs, outs):
            r[...] = o.astype(r.dtype)

    a_spec = pl.BlockSpec((k, tm), lambda i, j: (0, i)) if mode == "tn" else pl.BlockSpec((tm, k), lambda i, j: (i, 0))
    b_spec = pl.BlockSpec((tn, k), lambda i, j: (j, 0)) if mode == "nt" else pl.BlockSpec((k, tn), lambda i, j: (0, j))
    o_spec = pl.BlockSpec((tm, tn), lambda i, j: (i, j))
    outs = pl.pallas_call(
        body,
        out_shape=tuple(_sds((m, n), dt) for dt in out_dtypes),
        grid=(m // tm, n // tn),
        in_specs=[a_spec, b_spec] + [o_spec] * n_ex,
        out_specs=tuple(o_spec for _ in out_dtypes),
        compiler_params=_cp(("parallel", "parallel")),
        name=name,
    )(a, b, *extras)
    return outs[0] if len(out_dtypes) == 1 else outs


def _rt(t):
    return _pick(t, (384, 256, 128))


def norm_fwd(x, g, out_dtype, col_blk=0, width=None, name="norm_fwd"):
    t = x.shape[0]
    w = width or x.shape[1]
    tr = _rt(t)

    def body(x_ref, g_ref, o_ref):
        xv = x_ref[...]
        r = lax.rsqrt(jnp.mean(xv * xv, axis=-1, keepdims=True) + EPS)
        o_ref[...] = (xv * r * g_ref[...]).astype(o_ref.dtype)

    return pl.pallas_call(
        body,
        out_shape=_sds((t, w), out_dtype),
        grid=(t // tr,),
        in_specs=[pl.BlockSpec((tr, w), lambda i: (i, col_blk)), pl.BlockSpec((1, w), lambda i: (0, 0))],
        out_specs=pl.BlockSpec((tr, w), lambda i: (i, 0)),
        compiler_params=_cp(("parallel",)),
        name=name,
    )(x, g.reshape(1, w))


def norm_bwd(x, g, dy, dres=None, mask_pad=False, out_dtype=F32, col_blk=0, width=None, dy_col_blk=0, name="norm_bwd"):
    t = x.shape[0]
    w = width or x.shape[1]
    tr = _rt(t)
    has_res = dres is not None

    def body(x_ref, g_ref, dy_ref, *rest):
        if has_res:
            res_ref, dx_ref, dg_ref = rest
        else:
            dx_ref, dg_ref = rest
        i = pl.program_id(0)
        xv = x_ref[...]
        dyv = dy_ref[...].astype(F32)
        if mask_pad:
            dyv = jnp.where(_row_mask(i, tr, dyv.shape), dyv, 0.0)
        r = lax.rsqrt(jnp.mean(xv * xv, axis=-1, keepdims=True) + EPS)
        xh = xv * r
        dyg = dyv * g_ref[...]
        dx = r * (dyg - xh * jnp.mean(dyg * xh, axis=-1, keepdims=True))
        if has_res:
            dx = dx + res_ref[...]
        dx_ref[...] = dx.astype(dx_ref.dtype)

        @pl.when(i == 0)
        def _():
            dg_ref[...] = jnp.zeros_like(dg_ref)

        dg_ref[...] += jnp.sum(dyv * xh, axis=0, keepdims=True)

    in_specs = [pl.BlockSpec((tr, w), lambda i: (i, col_blk)), pl.BlockSpec((1, w), lambda i: (0, 0)),
                pl.BlockSpec((tr, w), lambda i: (i, dy_col_blk))]
    args = [x, g.reshape(1, w), dy]
    if has_res:
        in_specs.append(pl.BlockSpec((tr, w), lambda i: (i, 0)))
        args.append(dres)
    dx, dg = pl.pallas_call(
        body,
        out_shape=(_sds((t, w), out_dtype), _sds((1, w), F32)),
        grid=(t // tr,),
        in_specs=in_specs,
        out_specs=(pl.BlockSpec((tr, w), lambda i: (i, 0)), pl.BlockSpec((1, w), lambda i: (0, 0))),
        compiler_params=_cp(("arbitrary",)),
        name=name,
    )(*args)
    return dx, dg.reshape(w)


def resadd_fwd(h, m, g, name="resadd"):
    t, w = h.shape
    tr = _rt(t)

    def body(h_ref, m_ref, g_ref, o_ref):
        mv = m_ref[...]
        r = lax.rsqrt(jnp.mean(mv * mv, axis=-1, keepdims=True) + EPS)
        y = mv * r * g_ref[...]
        o_ref[...] = h_ref[...] + jnp.where(_row_mask(pl.program_id(0), tr, y.shape), y, 0.0)

    return pl.pallas_call(
        body,
        out_shape=_sds((t, w), F32),
        grid=(t // tr,),
        in_specs=[pl.BlockSpec((tr, w), lambda i: (i, 0)), pl.BlockSpec((tr, w), lambda i: (i, 0)),
                  pl.BlockSpec((1, w), lambda i: (0, 0))],
        out_specs=pl.BlockSpec((tr, w), lambda i: (i, 0)),
        compiler_params=_cp(("parallel",)),
        name=name,
    )(h, m, g.reshape(1, w))


def loss_fwd_bwd(h, target):
    t, w = h.shape
    tr = _rt(t)

    def body(h_ref, t_ref, s_ref, dh_ref):
        i = pl.program_id(0)
        err = h_ref[...] - t_ref[...]
        err = jnp.where(_row_mask(i, tr, err.shape, PAD + N_META), err, 0.0)
        dh_ref[...] = err * (1.0 / w)

        @pl.when(i == 0)
        def _():
            s_ref[...] = jnp.zeros_like(s_ref)

        s_ref[...] += jnp.sum(err * err).reshape(1, 1)

    s, dh = pl.pallas_call(
        body,
        out_shape=(_sds((1, LANE), F32), _sds((t, w), F32)),
        grid=(t // tr,),
        in_specs=[pl.BlockSpec((tr, w), lambda i: (i, 0)), pl.BlockSpec((tr, w), lambda i: (i, 0))],
        out_specs=(pl.BlockSpec((1, LANE), lambda i: (0, 0)), pl.BlockSpec((tr, w), lambda i: (i, 0))),
        compiler_params=_cp(("arbitrary",)),
        name="loss",
    )(h, target)
    return 0.5 * s[0, 0] / w, dh


def _shift_down(ext, k, n):
    return pltpu.roll(ext, k, 0)[8:]


def _conv_pre(ext, x, w_ref, n):
    return (w_ref[4:5, :] + w_ref[3:4, :] * x + w_ref[2:3, :] * _shift_down(ext, 1, n)
            + w_ref[1:2, :] * _shift_down(ext, 2, n) + w_ref[0:1, :] * _shift_down(ext, 3, n))


def _conv_bwd_parts(dpre, dnext, x, ext, w_ref, n):
    extd = jnp.concatenate([dpre, dnext], axis=0)
    ln = n + 8
    dx = (w_ref[3:4, :] * dpre + w_ref[2:3, :] * pltpu.roll(extd, ln - 1, 0)[:n]
          + w_ref[1:2, :] * pltpu.roll(extd, ln - 2, 0)[:n] + w_ref[0:1, :] * pltpu.roll(extd, ln - 3, 0)[:n])
    sums = [jnp.sum(dpre * _shift_down(ext, 3, n), axis=0, keepdims=True),
            jnp.sum(dpre * _shift_down(ext, 2, n), axis=0, keepdims=True),
            jnp.sum(dpre * _shift_down(ext, 1, n), axis=0, keepdims=True),
            jnp.sum(dpre * x, axis=0, keepdims=True),
            jnp.sum(dpre, axis=0, keepdims=True)]
    return dx, sums


def _rows_block(sums):
    row = lax.broadcasted_iota(jnp.int32, (8, LANE), 0)
    out = jnp.zeros((8, LANE), F32)
    for k, s in enumerate(sums):
        out = jnp.where(row == k, s, out)
    return out


def conv_silu_fwd(x, col0_blk, nblk, wb, name="conv_fwd"):
    t = x.shape[0]
    c = nblk * LANE
    tr = _rt(t)

    def body(x_ref, w_ref, o_ref, prev):
        ti = pl.program_id(1)

        @pl.when(ti == 0)
        def _():
            prev[...] = jnp.zeros_like(prev)

        xv = x_ref[...]
        ext = jnp.concatenate([prev[...], xv], axis=0)
        o_ref[...] = _silu(_conv_pre(ext, xv, w_ref, tr))
        prev[...] = xv[tr - 8:, :]

    return pl.pallas_call(
        body,
        out_shape=_sds((t, c), F32),
        grid=(nblk, t // tr),
        in_specs=[pl.BlockSpec((tr, LANE), lambda cb, ti: (ti, col0_blk + cb)),
                  pl.BlockSpec((8, LANE), lambda cb, ti: (0, cb))],
        out_specs=pl.BlockSpec((tr, LANE), lambda cb, ti: (ti, cb)),
        scratch_shapes=[pltpu.VMEM((8, LANE), F32)],
        compiler_params=_cp(("parallel", "arbitrary")),
        name=name,
    )(x, wb)


def conv_silu_bwd(x, col0_blk, nblk, wb, dout, name="conv_bwd"):
    t = x.shape[0]
    c = nblk * LANE
    tr = _rt(t)
    nt = t // tr
    r8 = tr // 8

    def body(x_ref, xp_ref, w_ref, do_ref, dx_ref, dwb_ref, dnext):
        ti = pl.program_id(1)
        tt = nt - 1 - ti

        @pl.when(ti == 0)
        def _():
            dnext[...] = jnp.zeros_like(dnext)
            dwb_ref[...] = jnp.zeros_like(dwb_ref)

        xv = x_ref[...]
        halo = jnp.where(tt > 0, xp_ref[...], 0.0)
        ext = jnp.concatenate([halo, xv], axis=0)
        pre = _conv_pre(ext, xv, w_ref, tr)
        s = jax.nn.sigmoid(pre)
        dpre = do_ref[...] * (s + pre * s * (1.0 - s))
        dx, sums = _conv_bwd_parts(dpre, dnext[...], xv, ext, w_ref, tr)
        dx_ref[...] = dx
        dwb_ref[...] += _rows_block(sums)
        dnext[...] = dpre[:8, :]

    return pl.pallas_call(
        body,
        out_shape=(_sds((t, c), F32), _sds((8, c), F32)),
        grid=(nblk, nt),
        in_specs=[pl.BlockSpec((tr, LANE), lambda cb, ti: (nt - 1 - ti, col0_blk + cb)),
                  pl.BlockSpec((8, LANE), lambda cb, ti: (jnp.maximum((nt - 1 - ti) * r8 - 1, 0), col0_blk + cb)),
                  pl.BlockSpec((8, LANE), lambda cb, ti: (0, cb)),
                  pl.BlockSpec((tr, LANE), lambda cb, ti: (nt - 1 - ti, cb))],
        out_specs=(pl.BlockSpec((tr, LANE), lambda cb, ti: (nt - 1 - ti, cb)),
                   pl.BlockSpec((8, LANE), lambda cb, ti: (0, cb))),
        scratch_shapes=[pltpu.VMEM((8, LANE), F32)],
        compiler_params=_cp(("parallel", "arbitrary")),
        name=name,
    )(x, x, wb, dout)


def gated_norm_fwd(y, proj, g, name="gnorm_fwd"):
    t, w = y.shape
    tr = _rt(t)

    def body(y_ref, z_ref, g_ref, o_ref):
        v = y_ref[...] * _silu(z_ref[...])
        r = lax.rsqrt(jnp.mean(v * v, axis=-1, keepdims=True) + EPS)
        o_ref[...] = (v * r * g_ref[...]).astype(o_ref.dtype)

    return pl.pallas_call(
        body,
        out_shape=_sds((t, w), BF),
        grid=(t // tr,),
        in_specs=[pl.BlockSpec((tr, w), lambda i: (i, 0)), pl.BlockSpec((tr, w), lambda i: (i, OFF_Z // w)),
                  pl.BlockSpec((1, w), lambda i: (0, 0))],
        out_specs=pl.BlockSpec((tr, w), lambda i: (i, 0)),
        compiler_params=_cp(("parallel",)),
        name=name,
    )(y, proj, g.reshape(1, w))


def gated_norm_bwd(y, proj, g, dyab, name="gnorm_bwd"):
    t, w = y.shape
    tr = _rt(t)

    def body(y_ref, z_ref, g_ref, do_ref, dy_ref, dz_ref, dg_ref):
        i = pl.program_id(0)
        yv, zv, dov = y_ref[...], z_ref[...], do_ref[...]
        s = jax.nn.sigmoid(zv)
        sz = zv * s
        v = yv * sz
        r = lax.rsqrt(jnp.mean(v * v, axis=-1, keepdims=True) + EPS)
        vh = v * r
        dvg = dov * g_ref[...]
        dv = r * (dvg - vh * jnp.mean(dvg * vh, axis=-1, keepdims=True))
        dy_ref[...] = dv * sz
        dz_ref[...] = dv * yv * (s + sz * (1.0 - s))

        @pl.when(i == 0)
        def _():
            dg_ref[...] = jnp.zeros_like(dg_ref)

        dg_ref[...] += jnp.sum(dov * vh, axis=0, keepdims=True)

    dy, dz, dg = pl.pallas_call(
        body,
        out_shape=(_sds((t, w), F32), _sds((t, w), F32), _sds((1, w), F32)),
        grid=(t // tr,),
        in_specs=[pl.BlockSpec((tr, w), lambda i: (i, 0)), pl.BlockSpec((tr, w), lambda i: (i, OFF_Z // w)),
                  pl.BlockSpec((1, w), lambda i: (0, 0)), pl.BlockSpec((tr, w), lambda i: (i, 0))],
        out_specs=(pl.BlockSpec((tr, w), lambda i: (i, 0)), pl.BlockSpec((tr, w), lambda i: (i, 0)),
                   pl.BlockSpec((1, w), lambda i: (0, 0))),
        compiler_params=_cp(("arbitrary",)),
        name=name,
    )(y, proj, g.reshape(1, w), dyab)
    return dy, dz, dg.reshape(w)


def rope_tables(t):
    inv = ROPE_BASE ** (-jnp.arange(0, MLA_ROPE, 2, dtype=F32) / MLA_ROPE)
    pos = (jnp.arange(t, dtype=F32) - PAD)[:, None]
    ang = pos * inv[None, :]
    cos, sin = jnp.cos(ang), jnp.sin(ang)
    z16 = jnp.zeros((t, 16), F32)
    z32 = jnp.zeros((t, 32), F32)
    c = jnp.concatenate([jnp.ones((t, 64), F32), cos, cos, z32], axis=1)
    s1 = jnp.concatenate([jnp.zeros((t, 64), F32), z16, sin, z32], axis=1)
    s2 = jnp.concatenate([jnp.zeros((t, 64), F32), -sin, z16, z32], axis=1)
    return c, s1, s2


def _rope(x, c, s1, s2):
    return x * c + pltpu.roll(x, 16, 1) * s1 + pltpu.roll(x, LANE - 16, 1) * s2


def _rope_t(d, c, s1, s2):
    return d * c + pltpu.roll(d * s1, LANE - 16, 1) + pltpu.roll(d * s2, 16, 1)


def rope_fwd(q_raw, kv_raw, proj, tabs):
    t = q_raw.shape[0]
    tr = _rt(t)
    hw = MLA_HEADS * LANE

    def body(q_ref, k_ref, v_ref, kr_ref, c_ref, s1_ref, s2_ref, qo_ref, ko_ref, vo_ref):
        c, s1, s2 = c_ref[...], s1_ref[...], s2_ref[...]
        kr = _rope(kr_ref[...], c, s1, s2)
        for h in range(MLA_HEADS):
            sl = slice(h * LANE, (h + 1) * LANE)
            qo_ref[:, sl] = _rope(q_ref[:, sl], c, s1, s2).astype(BF)
            ko_ref[:, sl] = (k_ref[:, sl] + kr).astype(BF)
        vo_ref[...] = v_ref[...].astype(BF)

    tab_spec = pl.BlockSpec((tr, LANE), lambda i: (i, 0))
    return pl.pallas_call(
        body,
        out_shape=(_sds((t, hw), BF), _sds((t, hw), BF), _sds((t, 1024), BF)),
        grid=(t // tr,),
        in_specs=[pl.BlockSpec((tr, hw), lambda i: (i, 0)), pl.BlockSpec((tr, hw), lambda i: (i, 0)),
                  pl.BlockSpec((tr, 1024), lambda i: (i, 2)), pl.BlockSpec((tr, LANE), lambda i: (i, OFF_KR // LANE)),
                  tab_spec, tab_spec, tab_spec],
        out_specs=(pl.BlockSpec((tr, hw), lambda i: (i, 0)), pl.BlockSpec((tr, hw), lambda i: (i, 0)),
                   pl.BlockSpec((tr, 1024), lambda i: (i, 0))),
        compiler_params=_cp(("parallel",)),
        name="rope_fwd",
    )(q_raw, kv_raw, kv_raw, proj, *tabs)


def rope_bwd(dq_cat, dk_cat, tabs):
    t = dq_cat.shape[0]
    tr = _rt(t)
    hw = MLA_HEADS * LANE

    def body(dq_ref, dk_ref, c_ref, s1_ref, s2_ref, dqo_ref, dkr_ref):
        c, s1, s2 = c_ref[...], s1_ref[...], s2_ref[...]
        acc = jnp.zeros((tr, LANE), F32)
        for h in range(MLA_HEADS):
            sl = slice(h * LANE, (h + 1) * LANE)
            dqo_ref[:, sl] = _rope_t(dq_ref[:, sl], c, s1, s2).astype(BF)
            acc = acc + dk_ref[:, sl]
        lane = lax.broadcasted_iota(jnp.int32, (tr, LANE), 1)
        dkr_ref[...] = jnp.where((lane >= 64) & (lane < 96), _rope_t(acc, c, s1, s2), 0.0)

    tab_spec = pl.BlockSpec((tr, LANE), lambda i: (i, 0))
    return pl.pallas_call(
        body,
        out_shape=(_sds((t, hw), BF), _sds((t, LANE), F32)),
        grid=(t // tr,),
        in_specs=[pl.BlockSpec((tr, hw), lambda i: (i, 0)), pl.BlockSpec((tr, hw), lambda i: (i, 0)),
                  tab_spec, tab_spec, tab_spec],
        out_specs=(pl.BlockSpec((tr, hw), lambda i: (i, 0)), pl.BlockSpec((tr, LANE), lambda i: (i, 0))),
        compiler_params=_cp(("parallel",)),
        name="rope_bwd",
    )(dq_cat, dk_cat, *tabs)


ATT_SCALE = (MLA_NOPE + MLA_ROPE) ** -0.5
NT_DIMS = (((1,), (1,)), ((), ()))
TN_DIMS = (((0,), (0,)), ((), ()))


def _att_mask(qi, ki, tq, tk):
    qpos = qi * tq + lax.broadcasted_iota(jnp.int32, (tq, tk), 0)
    kpos = ki * tk + lax.broadcasted_iota(jnp.int32, (tq, tk), 1)
    return (kpos <= qpos) & (kpos >= PAD)


def _half_masks(n):
    lane = lax.broadcasted_iota(jnp.int32, (n, LANE), 1)
    return lane < 64, lane >= 64


def _att_tile(t):
    return _pick(t, (384, 256, 128))


def _ds(i, n):
    return pl.ds(i * n, n) if isinstance(i, int) else pl.ds(pl.multiple_of(i * n, n), n)


def attn_fwd(q_cat, k_cat, v):
    t = q_cat.shape[0]
    tq = tk = _att_tile(t)
    nq = t // tq

    def body(q_ref, k_ref, v_ref, o_ref, lse_ref, m_s, l_s, acc_s):
        qi = pl.program_id(1)
        m_s[...] = jnp.full_like(m_s, NEG)
        l_s[...] = jnp.zeros_like(l_s)
        acc_s[...] = jnp.zeros_like(acc_s)
        lo_q, _ = _half_masks(tq)
        halves = _half_masks(tk)

        def step(ki, masked):
            rows = _ds(ki, tk)
            vv = v_ref[rows, :]
            alphas, pvs = [], []
            for hh in range(2):
                sl = slice(hh * LANE, (hh + 1) * LANE)
                s = lax.dot_general(q_ref[:, sl], k_ref[rows, sl], NT_DIMS, preferred_element_type=F32) * ATT_SCALE
                if masked:
                    s = jnp.where(_att_mask(qi, ki, tq, tk), s, NEG)
                m_old = m_s[hh]
                m_new = jnp.maximum(m_old, jnp.max(s, axis=-1, keepdims=True))
                p = jnp.exp(s - m_new[:, 0:1])
                alpha = jnp.exp(m_old - m_new)
                l_s[hh] = alpha * l_s[hh] + jnp.sum(p, axis=-1, keepdims=True)
                m_s[hh] = m_new
                vm = jnp.where(halves[hh], vv, jnp.zeros_like(vv))
                pvs.append(jnp.dot(p.astype(BF), vm, preferred_element_type=F32))
                alphas.append(alpha)
            acc_s[...] = acc_s[...] * jnp.where(lo_q, alphas[0], alphas[1]) + pvs[0] + pvs[1]

        step(0, True)

        def inner(ki, carry):
            step(ki, False)
            return carry

        lax.fori_loop(1, qi, inner, 0)

        @pl.when(qi > 0)
        def _():
            step(qi, True)

        l = jnp.where(lo_q, l_s[0], l_s[1])
        m = jnp.where(lo_q, m_s[0], m_s[1])
        o_ref[...] = (acc_s[...] / l).astype(o_ref.dtype)
        lse_ref[...] = m + jnp.log(l)

    return pl.pallas_call(
        body,
        out_shape=(_sds((t, 1024), BF), _sds((t, 1024), F32)),
        grid=(MLA_HEADS // 2, nq),
        in_specs=[pl.BlockSpec((tq, 2 * LANE), lambda p, qi: (qi, p)),
                  pl.BlockSpec((t, 2 * LANE), lambda p, qi: (0, p)),
                  pl.BlockSpec((t, LANE), lambda p, qi: (0, p))],
        out_specs=(pl.BlockSpec((tq, LANE), lambda p, qi: (qi, p)),
                   pl.BlockSpec((tq, LANE), lambda p, qi: (qi, p))),
        scratch_shapes=[pltpu.VMEM((2, tq, LANE), F32), pltpu.VMEM((2, tq, LANE), F32), pltpu.VMEM((tq, LANE), F32)],
        compiler_params=_cp(("parallel", "parallel")),
        name="attn_fwd",
    )(q_cat, k_cat, v)


def attn_bwd(q_cat, k_cat, v, o, lse, dyab):
    t = q_cat.shape[0]
    tq = tk = _att_tile(t)
    nq = t // tq

    def body(q_ref, k_ref, v_ref, o_ref, lse_ref, do_ref, dq_ref, dk_ref, dv_ref):
        ki = pl.program_id(1)

        @pl.when(ki == 0)
        def _():
            dq_ref[...] = jnp.zeros_like(dq_ref)

        dk_ref[...] = jnp.zeros_like(dk_ref)
        dv_ref[...] = jnp.zeros_like(dv_ref)
        halves = _half_masks(tq)
        vv = v_ref[...]

        def step(qi, masked):
            rows = _ds(qi, tq)
            dov, ov, lse_v = do_ref[rows, :], o_ref[rows, :].astype(F32), lse_ref[rows, :]
            for hh in range(2):
                sl = slice(hh * LANE, (hh + 1) * LANE)
                qh = q_ref[rows, sl]
                s = lax.dot_general(qh, k_ref[:, sl], NT_DIMS, preferred_element_type=F32) * ATT_SCALE
                if masked:
                    s = jnp.where(_att_mask(qi, ki, tq, tk), s, NEG)
                p = jnp.exp(s - lse_v[:, 64 * hh:64 * hh + 1])
                dom = jnp.where(halves[hh], dov, 0.0)
                delta = jnp.sum(dom * ov, axis=-1, keepdims=True)
                dp = lax.dot_general(dom.astype(BF), vv, NT_DIMS, preferred_element_type=F32)
                ds = (p * (dp - delta) * ATT_SCALE).astype(BF)
                dv_ref[...] += lax.dot_general(p.astype(BF), dom.astype(BF), TN_DIMS, preferred_element_type=F32)
                dk_ref[:, sl] += lax.dot_general(ds, qh, TN_DIMS, preferred_element_type=F32)
                dq_ref[rows, sl] += jnp.dot(ds, k_ref[:, sl], preferred_element_type=F32)

        def masked_step(qi, carry):
            step(qi, True)
            return carry

        def plain_step(qi, carry):
            step(qi, False)
            return carry

        step(ki, True)
        lax.fori_loop(ki + 1, jnp.where(ki == 0, nq, ki + 1), masked_step, 0)
        lax.fori_loop(ki + 1, jnp.where(ki == 0, ki + 1, nq), plain_step, 0)

    full = lambda w, off=0: pl.BlockSpec((t, w), lambda p, ki: (0, p + off))
    blk = lambda w: pl.BlockSpec((tk, w), lambda p, ki: (ki, p))
    return pl.pallas_call(
        body,
        out_shape=(_sds((t, 2048), F32), _sds((t, 2048), F32), _sds((t, 1024), F32)),
        grid=(MLA_HEADS // 2, nq),
        in_specs=[full(2 * LANE), blk(2 * LANE), blk(LANE), full(LANE), full(LANE), full(LANE, 8)],
        out_specs=(full(2 * LANE), blk(2 * LANE), blk(LANE)),
        compiler_params=_cp(("parallel", "arbitrary")),
        name="attn_bwd",
    )(q_cat, k_cat, v, o, lse, dyab)


N_PAIR = SSD_HEADS // 2


def _hdot(a, b):
    return jnp.dot(a, b, precision=HI, preferred_element_type=F32)


def _ssd_chunk(xs, bg, cg, dtraw, hin, dt_bias, a_log, dskip, rowmask):
    ln = CHUNK
    row = lax.broadcasted_iota(jnp.int32, (ln, ln), 0)
    col = lax.broadcasted_iota(jnp.int32, (ln, ln), 1)
    causal = row >= col
    ltri = causal.astype(F32)
    ones = jnp.ones((ln, ln), F32)
    k16 = lax.broadcasted_iota(jnp.int32, (SSD_HEADS, LANE), 0)
    upper = (lax.broadcasted_iota(jnp.int32, (SSD_HEADS, LANE), 1) >= 64).astype(jnp.int32)
    lane = lax.broadcasted_iota(jnp.int32, (ln, LANE), 1)
    halves = (lane < 64, lane >= 64)

    dt = _softplus(dtraw + dt_bias) * rowmask
    da = dt * (-jnp.exp(a_log))
    acs = _hdot(ltri, da)
    tot = _hdot(ones, da)
    bm = [b * rowmask for b in bg]
    cm = [c * rowmask for c in cg]
    cb = [lax.dot_general(cm[g].astype(BF), bm[g].astype(BF), NT_DIMS, preferred_element_type=F32) for g in range(2)]
    ys, hout = [], []
    for p in range(N_PAIR):
        g = p // (N_PAIR // 2)
        e_p = (k16 == 2 * p + upper).astype(F32)
        xdt = xs[p] * _hdot(dt, e_p)
        y = jnp.zeros((ln, LANE), F32)
        snew = jnp.zeros((ln, LANE), F32)
        for hh in range(2):
            f_h = (k16 == 2 * p + hh).astype(F32)
            m = _hdot(acs, f_h)
            mt = _hdot(tot, f_h)
            dec = jnp.exp(jnp.where(causal, m - m.T, NEG))
            xm = jnp.where(halves[hh], xdt, 0.0).astype(BF)
            y = y + jnp.dot((cb[g] * dec).astype(BF), xm, preferred_element_type=F32)
            bd = bm[g] * jnp.exp(mt - m)
            snew = snew + lax.dot_general(bd.astype(BF), xm, TN_DIMS, preferred_element_type=F32)
        y_off = jnp.dot(cm[g].astype(BF), hin[p].astype(BF), preferred_element_type=F32) * jnp.exp(_hdot(acs, e_p))
        ys.append(y + y_off + _hdot(dskip, e_p) * xs[p])
        hout.append(jnp.exp(_hdot(tot, e_p)) * hin[p] + snew)
    return ys, hout


def _ssd_load(x_ref, dt_ref):
    xs = [x_ref[:, p * LANE:(p + 1) * LANE] for p in range(N_PAIR)]
    bg = [x_ref[:, SSD_D_INNER + g * LANE:SSD_D_INNER + (g + 1) * LANE] for g in range(2)]
    cg = [x_ref[:, SSD_D_INNER + (2 + g) * LANE:SSD_D_INNER + (3 + g) * LANE] for g in range(2)]
    return xs, bg, cg, dt_ref[:, 0:SSD_HEADS]


def _chunk_rowmask(c):
    return ((c * CHUNK + lax.broadcasted_iota(jnp.int32, (CHUNK, 1), 0)) >= PAD).astype(F32)


def ssd_fwd(xbc_c, proj, dt_bias, a_log, dskip):
    t = xbc_c.shape[0]
    nc = t // CHUNK

    def body(x_ref, dt_ref, dtb_ref, al_ref, d_ref, y_ref, hs_ref, h_s):
        c = pl.program_id(0)

        @pl.when(c == 0)
        def _():
            h_s[...] = jnp.zeros_like(h_s)

        xs, bg, cg, dtraw = _ssd_load(x_ref, dt_ref)
        hin = [h_s[p] for p in range(N_PAIR)]
        hs_ref[0] = h_s[...]
        ys, hout = _ssd_chunk(xs, bg, cg, dtraw, hin, dtb_ref[...], al_ref[...], d_ref[...], _chunk_rowmask(c))
        for p in range(N_PAIR):
            y_ref[:, p * LANE:(p + 1) * LANE] = ys[p]
            h_s[p] = hout[p]

    par = pl.BlockSpec((1, SSD_HEADS), lambda c: (0, 0))
    return pl.pallas_call(
        body,
        out_shape=(_sds((t, SSD_D_INNER), F32), _sds((nc, N_PAIR, CHUNK, LANE), F32)),
        grid=(nc,),
        in_specs=[pl.BlockSpec((CHUNK, SSD_CONV_CH), lambda c: (c, 0)),
                  pl.BlockSpec((CHUNK, LANE), lambda c: (c, OFF_DT // LANE)), par, par, par],
        out_specs=(pl.BlockSpec((CHUNK, SSD_D_INNER), lambda c: (c, 0)),
                   pl.BlockSpec((1, N_PAIR, CHUNK, LANE), lambda c: (c, 0, 0, 0))),
        scratch_shapes=[pltpu.VMEM((N_PAIR, CHUNK, LANE), F32)],
        compiler_params=_cp(("arbitrary",)),
        name="ssd_fwd",
    )(xbc_c, proj, dt_bias.reshape(1, -1), a_log.reshape(1, -1), dskip.reshape(1, -1))


def ssd_bwd(xbc_c, proj, dt_bias, a_log, dskip, hs, dy):
    t = xbc_c.shape[0]
    nc = t // CHUNK

    def body(x_ref, dt_ref, dtb_ref, al_ref, d_ref, hs_ref, dy_ref, dx_ref, ddt_ref, dpar_ref, dh_s):
        ci = pl.program_id(0)
        c = nc - 1 - ci

        @pl.when(ci == 0)
        def _():
            dh_s[...] = jnp.zeros_like(dh_s)
            dpar_ref[...] = jnp.zeros_like(dpar_ref)

        xs, bg, cg, dtraw = _ssd_load(x_ref, dt_ref)
        hin = [hs_ref[0, p] for p in range(N_PAIR)]
        rowmask = _chunk_rowmask(c)
        fn = lambda xs_, bg_, cg_, dtraw_, hin_, dtb_, al_, d_: _ssd_chunk(xs_, bg_, cg_, dtraw_, hin_, dtb_, al_, d_, rowmask)
        _, vjp = jax.vjp(fn, xs, bg, cg, dtraw, hin, dtb_ref[...], al_ref[...], d_ref[...])
        dys = [dy_ref[:, p * LANE:(p + 1) * LANE] for p in range(N_PAIR)]
        dhs = [dh_s[p] for p in range(N_PAIR)]
        dxs, dbg, dcg, ddtraw, dhin, ddtb, dal, dd = vjp((dys, dhs))
        for p in range(N_PAIR):
            dx_ref[:, p * LANE:(p + 1) * LANE] = dxs[p]
            dh_s[p] = dhin[p]
        for g in range(2):
            dx_ref[:, SSD_D_INNER + g * LANE:SSD_D_INNER + (g + 1) * LANE] = dbg[g]
            dx_ref[:, SSD_D_INNER + (2 + g) * LANE:SSD_D_INNER + (3 + g) * LANE] = dcg[g]
        ddt_ref[...] = jnp.zeros_like(ddt_ref)
        ddt_ref[:, 0:SSD_HEADS] = ddtraw
        dpar_ref[0:1, 0:SSD_HEADS] += ddtb
        dpar_ref[1:2, 0:SSD_HEADS] += dal
        dpar_ref[2:3, 0:SSD_HEADS] += dd

    par = pl.BlockSpec((1, SSD_HEADS), lambda ci: (0, 0))
    return pl.pallas_call(
        body,
        out_shape=(_sds((t, SSD_CONV_CH), F32), _sds((t, LANE), F32), _sds((8, LANE), F32)),
        grid=(nc,),
        in_specs=[pl.BlockSpec((CHUNK, SSD_CONV_CH), lambda ci: (nc - 1 - ci, 0)),
                  pl.BlockSpec((CHUNK, LANE), lambda ci: (nc - 1 - ci, OFF_DT // LANE)), par, par, par,
                  pl.BlockSpec((1, N_PAIR, CHUNK, LANE), lambda ci: (nc - 1 - ci, 0, 0, 0)),
                  pl.BlockSpec((CHUNK, SSD_D_INNER), lambda ci: (nc - 1 - ci, 0))],
        out_specs=(pl.BlockSpec((CHUNK, SSD_CONV_CH), lambda ci: (nc - 1 - ci, 0)),
                   pl.BlockSpec((CHUNK, LANE), lambda ci: (nc - 1 - ci, 0)),
                   pl.BlockSpec((8, LANE), lambda ci: (0, 0))),
        scratch_shapes=[pltpu.VMEM((N_PAIR, CHUNK, LANE), F32)],
        compiler_params=_cp(("arbitrary",)),
        name="ssd_bwd",
    )(xbc_c, proj, dt_bias.reshape(1, -1), a_log.reshape(1, -1), dskip.reshape(1, -1), hs, dy)


def _neg_expm1(y):
    series = -(y * (1.0 + y * (0.5 + y * (1.0 / 6.0 + y * (1.0 / 24.0 + y * (1.0 / 120.0))))))
    return jnp.where(y > -0.1, series, 1.0 - jnp.exp(y))


def _rg_pw(xr, wa, ba, wi, bi, lam, rowmask):
    xb = xr.astype(BF)
    r = jax.nn.sigmoid(jnp.dot(xb, wa.astype(BF), preferred_element_type=F32) + ba)
    i = jax.nn.sigmoid(jnp.dot(xb, wi.astype(BF), preferred_element_type=F32) + bi)
    log_a = -LRU_C * r * _softplus(-lam)
    a = jnp.exp(log_a)
    u = jnp.sqrt(_neg_expm1(2.0 * log_a)) * (i * xr) * rowmask
    return a, u


def _gelu_grad(x):
    c = math.sqrt(2.0 / math.pi)
    th = jnp.tanh(c * (x + 0.044715 * (x * x * x)))
    return 0.5 * (1.0 + th) + 0.5 * x * (1.0 - th * th) * c * (1.0 + 3.0 * 0.044715 * x * x)


def _scan_fwd(a, u):
    n = a.shape[0]
    row = lax.broadcasted_iota(jnp.int32, a.shape, 0)
    s = 1
    while s < n:
        a_s = jnp.where(row >= s, pltpu.roll(a, s, 0), 1.0)
        u_s = jnp.where(row >= s, pltpu.roll(u, s, 0), 0.0)
        u = u + a * u_s
        a = a * a_s
        s *= 2
    return a, u


def _scan_bwd(b, d):
    n = b.shape[0]
    row = lax.broadcasted_iota(jnp.int32, b.shape, 0)
    s = 1
    while s < n:
        b_s = jnp.where(row < n - s, pltpu.roll(b, n - s, 0), 1.0)
        d_s = jnp.where(row < n - s, pltpu.roll(d, n - s, 0), 0.0)
        d = d + b * d_s
        b = b * b_s
        s *= 2
    return d


def rg_fwd(xg, rgp, w_a, w_i):
    t = xg.shape[0]
    tr = _rt(t)

    def body(x_ref, g_ref, p_ref, wa_ref, wi_ref, hg_ref, hs_ref, prev, hcar):
        ti = pl.program_id(1)

        @pl.when(ti == 0)
        def _():
            prev[...] = jnp.zeros_like(prev)
            hcar[...] = jnp.zeros_like(hcar)

        xv = x_ref[...]
        ext = jnp.concatenate([prev[...], xv], axis=0)
        xr = _conv_pre(ext, xv, p_ref, tr)
        rowmask = _row_mask(ti, tr, (tr, 1)).astype(F32)
        a, u = _rg_pw(xr, wa_ref[0], p_ref[5:6, :], wi_ref[0], p_ref[6:7, :], p_ref[7:8, :], rowmask)
        a_cum, h_loc = _scan_fwd(a, u)
        hs = h_loc + a_cum * hcar[0:1, :]
        hs_ref[...] = hs
        hg_ref[...] = (hs * _gelu(g_ref[...])).astype(hg_ref.dtype)
        hcar[...] = jnp.broadcast_to(hs[tr - 1:tr, :], (8, LANE))
        prev[...] = xv[tr - 8:, :]

    return pl.pallas_call(
        body,
        out_shape=(_sds((t, LRU_WIDTH), BF), _sds((t, LRU_WIDTH), F32)),
        grid=(LRU_BLOCKS, t // tr),
        in_specs=[pl.BlockSpec((tr, LANE), lambda n, ti: (ti, n)),
                  pl.BlockSpec((tr, LANE), lambda n, ti: (ti, LRU_BLOCKS + n)),
                  pl.BlockSpec((8, LANE), lambda n, ti: (0, n)),
                  pl.BlockSpec((1, LANE, LANE), lambda n, ti: (n, 0, 0)),
                  pl.BlockSpec((1, LANE, LANE), lambda n, ti: (n, 0, 0))],
        out_specs=(pl.BlockSpec((tr, LANE), lambda n, ti: (ti, n)), pl.BlockSpec((tr, LANE), lambda n, ti: (ti, n))),
        scratch_shapes=[pltpu.VMEM((8, LANE), F32), pltpu.VMEM((8, LANE), F32)],
        compiler_params=_cp(("parallel", "arbitrary")),
        name="rg_fwd",
    )(xg, xg, rgp, w_a, w_i)


def rg_bwd(xg, rgp, w_a, w_i, hs, dhg):
    t = xg.shape[0]
    tr = _rt(t)
    nt = t // tr
    r8 = tr // 8

    def body(x_ref, xp_ref, g_ref, p_ref, wa_ref, wi_ref, hs_ref, hp_ref, dhg_ref,
             dx_ref, dg_ref, dp_ref, dwa_ref, dwi_ref, gcar, dnext):
        ti = pl.program_id(1)
        tt = nt - 1 - ti

        @pl.when(ti == 0)
        def _():
            gcar[...] = jnp.zeros_like(gcar)
            dnext[...] = jnp.zeros_like(dnext)
            dp_ref[...] = jnp.zeros_like(dp_ref)
            dwa_ref[...] = jnp.zeros_like(dwa_ref)
            dwi_ref[...] = jnp.zeros_like(dwi_ref)

        xv = x_ref[...]
        halo = jnp.where(tt > 0, xp_ref[...], 0.0)
        ext = jnp.concatenate([halo, xv], axis=0)
        xr = _conv_pre(ext, xv, p_ref, tr)
        rowmask = _row_mask(tt, tr, (tr, 1)).astype(F32)
        fn = lambda xr_, wa_, ba_, wi_, bi_, lam_: _rg_pw(xr_, wa_, ba_, wi_, bi_, lam_, rowmask)
        (a, _), vjp = jax.vjp(fn, xr, wa_ref[0], p_ref[5:6, :], wi_ref[0], p_ref[6:7, :], p_ref[7:8, :])
        gpre = g_ref[...]
        hsv = hs_ref[...]
        dhg_v = dhg_ref[...]
        dg_ref[...] = (dhg_v * hsv * _gelu_grad(gpre)).astype(dg_ref.dtype)
        row = lax.broadcasted_iota(jnp.int32, (tr, LANE), 0)
        d = dhg_v * _gelu(gpre) + jnp.where(row == tr - 1, gcar[0:1, :], 0.0)
        b = jnp.where(row < tr - 1, pltpu.roll(a, tr - 1, 0), 0.0)
        g = _scan_bwd(b, d)
        gcar[...] = jnp.broadcast_to(a[0:1, :] * g[0:1, :], (8, LANE))
        hlast = jnp.where(tt > 0, hp_ref[7:8, :], 0.0)
        hprev = jnp.where(row == 0, hlast, pltpu.roll(hsv, 1, 0))
        dxr, dwa, dba, dwi, dbi, dlam = vjp((g * hprev, g))
        dx, sums = _conv_bwd_parts(dxr, dnext[...], xv, ext, p_ref, tr)
        dx_ref[...] = dx.astype(dx_ref.dtype)
        dnext[...] = dxr[:8, :]
        dp_ref[...] += _rows_block(sums + [dba, dbi, dlam])
        dwa_ref[0] += dwa
        dwi_ref[0] += dwi

    tile = lambda off=0: pl.BlockSpec((tr, LANE), lambda n, ti: (nt - 1 - ti, off + n))
    halo = lambda off=0: pl.BlockSpec((8, LANE), lambda n, ti: (jnp.maximum((nt - 1 - ti) * r8 - 1, 0), off + n))
    par = pl.BlockSpec((8, LANE), lambda n, ti: (0, n))
    wspec = pl.BlockSpec((1, LANE, LANE), lambda n, ti: (n, 0, 0))
    return pl.pallas_call(
        body,
        out_shape=(_sds((t, LRU_WIDTH), BF), _sds((t, LRU_WIDTH), BF), _sds((8, LRU_WIDTH), F32),
                   _sds((LRU_BLOCKS, LANE, LANE), F32), _sds((LRU_BLOCKS, LANE, LANE), F32)),
        grid=(LRU_BLOCKS, nt),
        in_specs=[tile(), halo(), tile(LRU_BLOCKS), par, wspec, wspec, tile(), halo(), tile()],
        out_specs=(tile(), tile(), par, wspec, wspec),
        scratch_shapes=[pltpu.VMEM((8, LANE), F32), pltpu.VMEM((8, LANE), F32)],
        compiler_params=_cp(("parallel", "arbitrary")),
        name="rg_bwd",
    )(xg, xg, xg, rgp, w_a, w_i, hs, hs, dhg)


PACK_W = 1024
MESH_ID = pl.DeviceIdType.MESH
ANY = pl.BlockSpec(memory_space=pl.ANY)


def _my_place():
    x, y, c = lax.axis_index("x"), lax.axis_index("y"), lax.axis_index("c")
    return x, y, c


def _lin(px, py, pc):
    return 4 * px + 2 * py + pc


def all_gather(arrs, name):
    k = len(arrs)

    def body(*refs):
        ins, outs = refs[:k], refs[k:2 * k]
        send_sems, recv_sems, local_sems = refs[2 * k:]
        x, y, c = _my_place()
        me, sibling = (x, y, c), (x, y, 1 - c)
        chips = [(1 - x, y), (x, 1 - y), (1 - x, 1 - y)]

        def copy(a, sem, block, to, from_input=False):
            slab = outs[a].at[_lin(*block)]
            return pltpu.make_async_remote_copy(
                src_ref=ins[a] if from_input else slab, dst_ref=slab,
                send_sem=send_sems.at[a, sem], recv_sem=recv_sems.at[a, sem],
                device_id=to, device_id_type=MESH_ID)

        mine = [pltpu.make_async_copy(ins[a], outs[a].at[_lin(*me)], local_sems.at[a]) for a in range(k)]
        for cp in mine:
            cp.start()
        first = []
        for a in range(k):
            first.append(copy(a, 0, me, sibling, True))
            first += [copy(a, 1 + j, me, (*chip, c), True) for j, chip in enumerate(chips)]
        for cp in first:
            cp.start()
        passed = []
        for j, chip in enumerate(chips):
            for a in range(k):
                copy(a, 1 + j, (*chip, c), me).wait_recv()
                fwd = copy(a, 4 + j, (*chip, c), sibling)
                fwd.start()
                passed.append(fwd)
        for a in range(k):
            copy(a, 0, sibling, me).wait_recv()
            for j, chip in enumerate(chips):
                copy(a, 4 + j, (*chip, 1 - c), me).wait_recv()
        for cp in first + passed:
            cp.wait_send()
        for cp in mine:
            cp.wait()

    return pl.pallas_call(
        body,
        out_shape=tuple(_sds((N_DEV,) + a.shape, a.dtype) for a in arrs),
        in_specs=[ANY] * k,
        out_specs=tuple(ANY for _ in arrs),
        scratch_shapes=[pltpu.SemaphoreType.DMA((k, 7)), pltpu.SemaphoreType.DMA((k, 7)), pltpu.SemaphoreType.DMA((k,))],
        name=name,
    )(*arrs)


def all_to_all(arrs, name):
    k = len(arrs)

    def body(*refs):
        ins, outs = refs[:k], refs[k:2 * k]
        send_sems, recv_sems, local_sems = refs[2 * k:]
        x, y, c = _my_place()
        me = _lin(x, y, c)
        peers = []
        for r in range(1, N_DEV):
            peers.append(((1 - x) if r & 4 else x, (1 - y) if r & 2 else y, (1 - c) if r & 1 else c))

        def copy(a, r, src_slab, dst_slab, to):
            return pltpu.make_async_remote_copy(
                src_ref=ins[a].at[src_slab], dst_ref=outs[a].at[dst_slab],
                send_sem=send_sems.at[a, r], recv_sem=recv_sems.at[a, r],
                device_id=to, device_id_type=MESH_ID)

        mine = [pltpu.make_async_copy(ins[a].at[me], outs[a].at[me], local_sems.at[a]) for a in range(k)]
        for cp in mine:
            cp.start()
        sends = [copy(a, r, _lin(*peer), me, peer) for r, peer in enumerate(peers) for a in range(k)]
        for cp in sends:
            cp.start()
        for r, peer in enumerate(peers):
            for a in range(k):
                copy(a, r, me, _lin(*peer), peer).wait_recv()
        for cp in sends:
            cp.wait_send()
        for cp in mine:
            cp.wait()

    return pl.pallas_call(
        body,
        out_shape=tuple(_sds(a.shape, a.dtype) for a in arrs),
        in_specs=[ANY] * k,
        out_specs=tuple(ANY for _ in arrs),
        scratch_shapes=[pltpu.SemaphoreType.DMA((k, 7)), pltpu.SemaphoreType.DMA((k, 7)), pltpu.SemaphoreType.DMA((k,))],
        name=name,
    )(*arrs)


def slab_sum(a, name):
    _, r, w = a.shape
    tr = _pick(r, (256, 128, 64, 32, 16, 8))

    def body(a_ref, o_ref):
        acc = a_ref[0].astype(F32)
        for d in range(1, N_DEV):
            acc = acc + a_ref[d].astype(F32)
        o_ref[...] = acc

    return pl.pallas_call(
        body,
        out_shape=_sds((r, w), F32),
        grid=(r // tr,),
        in_specs=[pl.BlockSpec((N_DEV, tr, w), lambda i: (0, i, 0))],
        out_specs=pl.BlockSpec((tr, w), lambda i: (i, 0)),
        compiler_params=_cp(("parallel",)),
        name=name,
    )(a)


def adamw(w, g, m, v, name):
    r, c = w.shape
    tr = _pick(r, (256, 160, 128, 64, 32, 16, 8))

    def body(w_ref, g_ref, m_ref, v_ref, d_ref, nm_ref, nv_ref):
        gv = g_ref[...]
        nm = ADAM_B1 * m_ref[...] + (1.0 - ADAM_B1) * gv
        nv = ADAM_B2 * v_ref[...] + (1.0 - ADAM_B2) * (gv * gv)
        m_hat = nm / (1.0 - ADAM_B1 ** ADAM_STEP)
        v_hat = nv / (1.0 - ADAM_B2 ** ADAM_STEP)
        d_ref[...] = -ADAM_LR * (m_hat / (jnp.sqrt(v_hat) + ADAM_EPS) + ADAM_WD * w_ref[...])
        nm_ref[...] = nm
        nv_ref[...] = nv

    spec = pl.BlockSpec((tr, c), lambda i: (i, 0))
    return pl.pallas_call(
        body,
        out_shape=tuple(_sds((r, c), F32) for _ in range(3)),
        grid=(r // tr,),
        in_specs=[spec] * 4,
        out_specs=(spec, spec, spec),
        compiler_params=_cp(("parallel",)),
        name=name,
    )(w, g, m, v)


def _relu2_epi(acc):
    r = jnp.maximum(acc, 0.0)
    return r * r, r


def _drelu2_epi(acc, r):
    return (acc * (2.0 * r.astype(F32)),)


def mlp_fwd(h, g_pre, g_post, w_up, w_down):
    hn = norm_fwd(h, g_pre, BF, name="mlp_norm")
    u, r = matmul(hn, w_up, "nn", (BF, BF), epi=_relu2_epi, name="mlp_up")
    d = matmul(u, w_down, "nn", name="mlp_down")
    return resadd_fwd(h, d, g_post, name="mlp_res"), (h, hn, u, r, d)


def mlp_bwd(res, dh2, g_pre, g_post, w_up, w_down):
    h, hn, u, r, d = res
    dd, dg_post = norm_bwd(d, g_post, dh2, mask_pad=True, out_dtype=BF, name="mlp_post_bwd")
    dw_down = matmul(u, dd, "tn", name="mlp_dwdown")
    dp = matmul(dd, w_down, "nt", (BF,), epi=_drelu2_epi, extras=(r,), name="mlp_du")
    dw_up = matmul(hn, dp, "tn", name="mlp_dwup")
    dhn = matmul(dp, w_up, "nt", name="mlp_dhn")
    dh, dg_pre = norm_bwd(h, g_pre, dhn, dres=dh2, name="mlp_pre_bwd")
    return dh, dict(mlp_pre_g=dg_pre, mlp_post_g=dg_post, w_up=dw_up, w_down=dw_down)


def rg_layer_fwd(h, g_pre, g_post, w_xy, rgp, w_a, w_i, w_out):
    hn = norm_fwd(h, g_pre, BF, name="rg_norm")
    xg = matmul(hn, w_xy, "nn", name="rg_in")
    hg, hs = rg_fwd(xg, rgp, w_a, w_i)
    m = matmul(hg, w_out, "nn", name="rg_out")
    return resadd_fwd(h, m, g_post, name="rg_res"), (h, hn, xg, hg, hs, m)


def rg_layer_bwd(res, dh2, g_pre, g_post, w_xy, rgp, w_a, w_i, w_out):
    h, hn, xg, hg, hs, m = res
    dm, dg_post = norm_bwd(m, g_post, dh2, mask_pad=True, out_dtype=BF, name="rg_post_bwd")
    dw_out = matmul(hg, dm, "tn", name="rg_dwout")
    dhg = matmul(dm, w_out, "nt", name="rg_dhg")
    dxr, dgp, drgp, dwa, dwi = rg_bwd(xg, rgp, w_a, w_i, hs, dhg)
    dxg = jnp.concatenate([dxr, dgp], axis=1)
    dw_xy = matmul(hn, dxg, "tn", name="rg_dwin")
    dhn = matmul(dxg, w_xy, "nt", name="rg_dhn")
    dh, dg_pre = norm_bwd(h, g_pre, dhn, dres=dh2, name="rg_pre_bwd")
    return dh, dict(mix_pre_g=dg_pre, mix_post_g=dg_post, rg_w_x=dw_xy[:, :LRU_WIDTH], rg_w_y=dw_xy[:, LRU_WIDTH:],
                    rg_conv_w=drgp[0:4], rg_conv_b=drgp[4], rg_b_a=drgp[5], rg_b_i=drgp[6], rg_lambda=drgp[7],
                    rg_w_a=dwa, rg_w_i=dwi, rg_w_out=dw_out)


def sm_layer_fwd(h, g_pre, g_post, w_in_p, convp, dt_bias, a_log, dskip, ssd_g, q_g, w_q_p, kv_g, w_kv_p, w_out, tabs):
    hn = norm_fwd(h, g_pre, BF, name="sm_norm")
    proj = matmul(hn, w_in_p, "nn", name="sm_in")
    xbc_c = conv_silu_fwd(proj, OFF_XBC // LANE, SSD_CONV_CH // LANE, convp, name="ssd_conv")
    y, hst = ssd_fwd(xbc_c, proj, dt_bias, a_log, dskip)
    y_ssd = gated_norm_fwd(y, proj, ssd_g)
    cqn = norm_fwd(proj, q_g, BF, col_blk=OFF_CQ // MLA_Q_RANK, width=MLA_Q_RANK, name="q_norm")
    q_raw = matmul(cqn, w_q_p, "nn", name="q_up")
    ckvn = norm_fwd(proj, kv_g, BF, col_blk=OFF_CKV // MLA_KV_RANK, width=MLA_KV_RANK, name="kv_norm")
    kv_raw = matmul(ckvn, w_kv_p, "nn", name="kv_up")
    q_cat, k_cat, v = rope_fwd(q_raw, kv_raw, proj, tabs)
    o, lse = attn_fwd(q_cat, k_cat, v)
    yab = jnp.concatenate([y_ssd, o], axis=1)
    m = matmul(yab, w_out, "nn", name="sm_out")
    return resadd_fwd(h, m, g_post, name="sm_res"), (h, hn, proj, xbc_c, y, hst, cqn, ckvn, q_cat, k_cat, v, o, lse, yab, m)


def sm_layer_bwd(res, dh2, g_pre, g_post, w_in_p, convp, dt_bias, a_log, dskip, ssd_g, q_g, w_q_p, kv_g, w_kv_p, w_out, tabs):
    h, hn, proj, xbc_c, y, hst, cqn, ckvn, q_cat, k_cat, v, o, lse, yab, m = res
    dm, dg_post = norm_bwd(m, g_post, dh2, mask_pad=True, out_dtype=BF, name="sm_post_bwd")
    dw_out = matmul(yab, dm, "tn", name="sm_dwout")
    dyab = matmul(dm, w_out, "nt", name="sm_dyab")
    dq_cat, dk_cat, dv = attn_bwd(q_cat, k_cat, v, o, lse, dyab)
    dq_raw, dkr = rope_bwd(dq_cat, dk_cat, tabs)
    dkv_raw = jnp.concatenate([dk_cat, dv], axis=1).astype(BF)
    dw_kv_p = matmul(ckvn, dkv_raw, "tn", name="kv_dw")
    dckvn = matmul(dkv_raw, w_kv_p, "nt", name="kv_dx")
    dckv, dg_kv = norm_bwd(proj, kv_g, dckvn, col_blk=OFF_CKV // MLA_KV_RANK, width=MLA_KV_RANK, name="kv_norm_bwd")
    dw_q_p = matmul(cqn, dq_raw, "tn", name="q_dw")
    dcqn = matmul(dq_raw, w_q_p, "nt", name="q_dx")
    dcq, dg_q = norm_bwd(proj, q_g, dcqn, col_blk=OFF_CQ // MLA_Q_RANK, width=MLA_Q_RANK, name="q_norm_bwd")
    dy, dz, dg_ssd = gated_norm_bwd(y, proj, ssd_g, dyab)
    dxbc_c, ddt, dpar = ssd_bwd(xbc_c, proj, dt_bias, a_log, dskip, hst, dy)
    dxbc, dconvp = conv_silu_bwd(proj, OFF_XBC // LANE, SSD_CONV_CH // LANE, convp, dxbc_c, name="ssd_conv_bwd")
    dproj = jnp.concatenate([dz, dxbc, dckv, ddt, dkr, dcq], axis=1).astype(BF)
    dw_in_p = matmul(hn, dproj, "tn", name="sm_dwin")
    dhn = matmul(dproj, w_in_p, "nt", name="sm_dhn")
    dh, dg_pre = norm_bwd(h, g_pre, dhn, dres=dh2, name="sm_pre_bwd")
    return dh, dict(mix_pre_g=dg_pre, mix_post_g=dg_post, w_in=_unpack_w_in(dw_in_p), ssd_conv_w=dconvp[0:4],
                    ssd_conv_b=dconvp[4], ssd_dt_bias=dpar[0, :SSD_HEADS], ssd_a_log=dpar[1, :SSD_HEADS],
                    ssd_d=dpar[2, :SSD_HEADS], ssd_norm_g=dg_ssd, mla_q_norm_g=dg_q, mla_w_q_up=_unpack_w_q(dw_q_p),
                    mla_kv_norm_g=dg_kv, mla_w_kv_up=_unpack_w_kv(dw_kv_p), w_out_ab=dw_out)


def _pack_w_in(w):
    z_xbc, dt, cq, ckv, kr = w[:, :2560], w[:, 2560:2576], w[:, 2576:2960], w[:, 2960:3216], w[:, 3216:3248]
    zeros = lambda n: jnp.zeros((w.shape[0], n), w.dtype)
    return jnp.concatenate([z_xbc, ckv, dt, zeros(112), zeros(64), kr, zeros(32), cq], axis=1)


def _unpack_w_in(p):
    return jnp.concatenate([p[:, :2560], p[:, OFF_DT:OFF_DT + 16], p[:, OFF_CQ:OFF_CQ + 384],
                            p[:, OFF_CKV:OFF_CKV + 256], p[:, OFF_KR + 64:OFF_KR + 96]], axis=1)


def _pack_w_q(w):
    w3 = w.reshape(w.shape[0], MLA_HEADS, MLA_NOPE + MLA_ROPE)
    return jnp.pad(w3, ((0, 0), (0, 0), (0, LANE - MLA_NOPE - MLA_ROPE))).reshape(w.shape[0], MLA_HEADS * LANE)


def _unpack_w_q(p):
    return p.reshape(p.shape[0], MLA_HEADS, LANE)[:, :, :MLA_NOPE + MLA_ROPE].reshape(p.shape[0], -1)


def _pack_w_kv(w):
    w3 = w.reshape(w.shape[0], MLA_HEADS, MLA_NOPE + MLA_V)
    k = jnp.pad(w3[:, :, :MLA_NOPE], ((0, 0), (0, 0), (0, LANE - MLA_NOPE))).reshape(w.shape[0], MLA_HEADS * LANE)
    return jnp.concatenate([k, w3[:, :, MLA_NOPE:].reshape(w.shape[0], MLA_HEADS * MLA_V)], axis=1)


def _unpack_w_kv(p):
    k = p[:, :MLA_HEADS * LANE].reshape(p.shape[0], MLA_HEADS, LANE)[:, :, :MLA_NOPE]
    v = p[:, MLA_HEADS * LANE:].reshape(p.shape[0], MLA_HEADS, MLA_V)
    return jnp.concatenate([k, v], axis=2).reshape(p.shape[0], -1)


def _rows8(rows, width):
    a = jnp.concatenate([r.reshape(-1, width) for r in rows], axis=0)
    return jnp.pad(a, ((0, 8 - a.shape[0]), (0, 0)))


SLAB_ROWS = 256


def _to_slab(flat_list, lead=()):
    cat = jnp.concatenate(flat_list, axis=-1)
    n = cat.shape[-1]
    unit = SLAB_ROWS * PACK_W
    total = -(-n // unit) * unit
    cat = jnp.pad(cat, [(0, 0)] * len(lead) + [(0, total - n)])
    return cat.reshape(lead + (total // PACK_W, PACK_W))


def _from_flat(flat, shapes):
    out, off = [], 0
    for s in shapes:
        n = int(np.prod(s))
        out.append(flat[off:off + n].reshape(s))
        off += n
    return out


def _gathered_full(g8, axis):
    moved = jnp.moveaxis(g8, 0, axis)
    shp = moved.shape
    return moved.reshape(shp[:axis] + (shp[axis] * shp[axis + 1],) + shp[axis + 2:])


def _per_device(full, axis):
    shp = full.shape
    split = full.reshape(shp[:axis] + (N_DEV, shp[axis] // N_DEV) + shp[axis + 1:])
    return jnp.moveaxis(split, axis, 0)


ARG_NAMES = ['x', 'meta_tokens', 'mix_pre_g', 'mix_post_g', 'mlp_pre_g', 'mlp_post_g', 'w_up', 'w_down', 'w_in',
             'ssd_conv_w', 'ssd_conv_b', 'ssd_dt_bias', 'ssd_a_log', 'ssd_d', 'ssd_norm_g', 'mla_q_norm_g',
             'mla_w_q_up', 'mla_kv_norm_g', 'mla_w_kv_up', 'w_out_ab', 'rg_w_x', 'rg_w_y', 'rg_conv_w', 'rg_conv_b',
             'rg_w_a', 'rg_b_a', 'rg_w_i', 'rg_b_i', 'rg_lambda', 'rg_w_out']
WEIGHTS = ARG_NAMES[1:]
BIG = {'w_up': 2, 'w_down': 1, 'w_in': 2, 'mla_w_q_up': 2, 'mla_w_kv_up': 2, 'w_out_ab': 1, 'rg_w_x': 2,
       'rg_w_y': 2, 'rg_w_out': 1}
SMALL = {'meta_tokens': 1, 'ssd_conv_w': 2, 'rg_conv_w': 2, 'rg_conv_b': 1, 'rg_b_a': 1, 'rg_b_i': 1, 'rg_lambda': 1}
REPL = [n for n in WEIGHTS if n not in BIG and n not in SMALL]


def _step(p, moments):
    big_slab = _to_slab([p[n].astype(BF).reshape(-1) for n in BIG])
    small_slab = _to_slab([p[n].reshape(-1) for n in SMALL])
    big8, small8 = all_gather([big_slab, small_slab], name="gather_weights")
    full = {}
    for names, g8 in ((BIG, big8), (SMALL, small8)):
        pieces = _from_flat_rows(g8, [p[n].shape for n in names])
        for n, piece in zip(names, pieces):
            full[n] = _gathered_full(piece, names[n])
    for n in REPL:
        full[n] = p[n]
    loss_local, grad_x, gfull = _local_step(full, p['x'][0], p['loss_target'][0])

    rep_flat = jnp.concatenate([gfull[n].reshape(-1) for n in REPL])
    rep_n = rep_flat.shape[0]
    rep_chunk = -(-rep_n // (N_DEV * PACK_W * 8)) * PACK_W * 8
    rep8 = jnp.pad(rep_flat, (0, N_DEV * rep_chunk - rep_n)).reshape(N_DEV, rep_chunk)
    gbig = _to_slab([_per_device(gfull[n], BIG[n]).reshape(N_DEV, -1).astype(BF) for n in BIG], lead=(N_DEV,))
    gsmall = _to_slab([_per_device(gfull[n], SMALL[n]).reshape(N_DEV, -1) for n in SMALL] + [rep8], lead=(N_DEV,))
    rbig, rsmall = all_to_all([gbig, gsmall], name="scatter_grads")
    sbig = slab_sum(rbig, name="sum_big").reshape(-1)
    ssmall = slab_sum(rsmall, name="sum_small").reshape(-1)
    g_loc = dict(zip(BIG, _from_flat(sbig, [p[n].shape for n in BIG])))
    small_n = sum(int(np.prod(p[n].shape)) for n in SMALL)
    g_loc.update(zip(SMALL, _from_flat(ssmall, [p[n].shape for n in SMALL])))
    rep_mine = ssmall[small_n:small_n + rep_chunk].reshape(-1, PACK_W)
    (rep_all,) = all_gather([rep_mine], name="gather_replicated")
    g_loc.update(zip(REPL, _from_flat(rep_all.reshape(-1), [p[n].shape for n in REPL])))

    out = {'loss': lax.psum(loss_local, ("x", "y", "c")), 'grad_x': grad_x[None]}
    small_names = list(SMALL) + REPL
    for n in BIG:
        shp = p[n].shape
        v2 = lambda a: a.reshape(-1, shp[-1])
        d, nm, nv = adamw(v2(p[n]), v2(g_loc[n]), v2(moments['m_' + n]), v2(moments['v_' + n]), name="adamw_" + n)
        out['delta_' + n], out['new_m_' + n], out['new_v_' + n] = d.reshape(shp), nm.reshape(shp), nv.reshape(shp)
    slab = lambda src: _to_slab([src(n).reshape(-1) for n in small_names])
    d, nm, nv = adamw(slab(lambda n: p[n]), slab(lambda n: g_loc[n]), slab(lambda n: moments['m_' + n]),
                      slab(lambda n: moments['v_' + n]), name="adamw_small")
    shapes = [p[n].shape for n in small_names]
    for key, flat in (('delta_', d), ('new_m_', nm), ('new_v_', nv)):
        for n, a in zip(small_names, _from_flat(flat.reshape(-1), shapes)):
            out[key + n] = a
    for n in WEIGHTS:
        out['grad_' + n] = g_loc[n]
    return out


def _local_step(full, x, target_rows):
    t = PAD + N_META + x.shape[0]
    h = jnp.concatenate([jnp.zeros((PAD, D_MODEL), F32), full['meta_tokens'], x], axis=0)
    target = jnp.concatenate([jnp.zeros((PAD + N_META, D_MODEL), F32), target_rows], axis=0)
    tabs = rope_tables(t)

    def layer_args(layer):
        i = layer // 2
        if layer % 2 == 0:
            convp = _rows8([full['ssd_conv_w'][i], full['ssd_conv_b'][i]], SSD_CONV_CH)
            return (full['mix_pre_g'][layer], full['mix_post_g'][layer], _pack_w_in(full['w_in'][i]), convp,
                    full['ssd_dt_bias'][i], full['ssd_a_log'][i], full['ssd_d'][i], full['ssd_norm_g'][i],
                    full['mla_q_norm_g'][i], _pack_w_q(full['mla_w_q_up'][i]), full['mla_kv_norm_g'][i],
                    _pack_w_kv(full['mla_w_kv_up'][i]), full['w_out_ab'][i], tabs)
        rgp = _rows8([full['rg_conv_w'][i], full['rg_conv_b'][i], full['rg_b_a'][i], full['rg_b_i'][i],
                      full['rg_lambda'][i]], LRU_WIDTH)
        w_xy = jnp.concatenate([full['rg_w_x'][i], full['rg_w_y'][i]], axis=1)
        return (full['mix_pre_g'][layer], full['mix_post_g'][layer], w_xy, rgp, full['rg_w_a'][i], full['rg_w_i'][i],
                full['rg_w_out'][i])

    def mlp_args(layer):
        return (full['mlp_pre_g'][layer], full['mlp_post_g'][layer], full['w_up'][layer], full['w_down'][layer])

    saved = []
    for layer in range(DEPTH):
        la, ma = layer_args(layer), mlp_args(layer)
        h, res_mix = (sm_layer_fwd if layer % 2 == 0 else rg_layer_fwd)(h, *la)
        h, res_mlp = mlp_fwd(h, *ma)
        saved.append((la, ma, res_mix, res_mlp))
    loss_local, dh = loss_fwd_bwd(h, target)

    grads = {n: [None] * full[n].shape[0] for n in WEIGHTS if n != 'meta_tokens'}
    for layer in reversed(range(DEPTH)):
        la, ma, res_mix, res_mlp = saved[layer]
        dh, gm = mlp_bwd(res_mlp, dh, *ma)
        dh, gx = (sm_layer_bwd if layer % 2 == 0 else rg_layer_bwd)(res_mix, dh, *la)
        for n, g in gm.items():
            grads[n][layer] = g
        for n, g in gx.items():
            grads[n][layer if n in ('mix_pre_g', 'mix_post_g') else layer // 2] = g
    gfull = {n: jnp.stack(v, axis=0) for n, v in grads.items()}
    gfull['meta_tokens'] = dh[PAD:PAD + N_META]
    return loss_local, dh[PAD + N_META:], gfull


def _from_flat_rows(g8, shapes):
    flat = g8.reshape(N_DEV, -1)
    out, off = [], 0
    for s in shapes:
        n = int(np.prod(s))
        out.append(flat[:, off:off + n].reshape((N_DEV,) + tuple(s)))
        off += n
    return out


def kernel(x, meta_tokens, mix_pre_g, mix_post_g, mlp_pre_g, mlp_post_g, w_up, w_down, w_in, ssd_conv_w, ssd_conv_b, ssd_dt_bias, ssd_a_log, ssd_d, ssd_norm_g, mla_q_norm_g, mla_w_q_up, mla_kv_norm_g, mla_w_kv_up, w_out_ab, rg_w_x, rg_w_y, rg_conv_w, rg_conv_b, rg_w_a, rg_b_a, rg_w_i, rg_b_i, rg_lambda, rg_w_out, loss_target, m_meta_tokens, m_mix_pre_g, m_mix_post_g, m_mlp_pre_g, m_mlp_post_g, m_w_up, m_w_down, m_w_in, m_ssd_conv_w, m_ssd_conv_b, m_ssd_dt_bias, m_ssd_a_log, m_ssd_d, m_ssd_norm_g, m_mla_q_norm_g, m_mla_w_q_up, m_mla_kv_norm_g, m_mla_w_kv_up, m_w_out_ab, m_rg_w_x, m_rg_w_y, m_rg_conv_w, m_rg_conv_b, m_rg_w_a, m_rg_b_a, m_rg_w_i, m_rg_b_i, m_rg_lambda, m_rg_w_out, v_meta_tokens, v_mix_pre_g, v_mix_post_g, v_mlp_pre_g, v_mlp_post_g, v_w_up, v_w_down, v_w_in, v_ssd_conv_w, v_ssd_conv_b, v_ssd_dt_bias, v_ssd_a_log, v_ssd_d, v_ssd_norm_g, v_mla_q_norm_g, v_mla_w_q_up, v_mla_kv_norm_g, v_mla_w_kv_up, v_w_out_ab, v_rg_w_x, v_rg_w_y, v_rg_conv_w, v_rg_conv_b, v_rg_w_a, v_rg_b_a, v_rg_w_i, v_rg_b_i, v_rg_lambda, v_rg_w_out):
    args = (x, meta_tokens, mix_pre_g, mix_post_g, mlp_pre_g, mlp_post_g, w_up, w_down, w_in, ssd_conv_w, ssd_conv_b, ssd_dt_bias, ssd_a_log, ssd_d, ssd_norm_g, mla_q_norm_g, mla_w_q_up, mla_kv_norm_g, mla_w_kv_up, w_out_ab, rg_w_x, rg_w_y, rg_conv_w, rg_conv_b, rg_w_a, rg_b_a, rg_w_i, rg_b_i, rg_lambda, rg_w_out, loss_target, m_meta_tokens, m_mix_pre_g, m_mix_post_g, m_mlp_pre_g, m_mlp_post_g, m_w_up, m_w_down, m_w_in, m_ssd_conv_w, m_ssd_conv_b, m_ssd_dt_bias, m_ssd_a_log, m_ssd_d, m_ssd_norm_g, m_mla_q_norm_g, m_mla_w_q_up, m_mla_kv_norm_g, m_mla_w_kv_up, m_w_out_ab, m_rg_w_x, m_rg_w_y, m_rg_conv_w, m_rg_conv_b, m_rg_w_a, m_rg_b_a, m_rg_w_i, m_rg_b_i, m_rg_lambda, m_rg_w_out, v_meta_tokens, v_mix_pre_g, v_mix_post_g, v_mlp_pre_g, v_mlp_post_g, v_w_up, v_w_down, v_w_in, v_ssd_conv_w, v_ssd_conv_b, v_ssd_dt_bias, v_ssd_a_log, v_ssd_d, v_ssd_norm_g, v_mla_q_norm_g, v_mla_w_q_up, v_mla_kv_norm_g, v_mla_w_kv_up, v_w_out_ab, v_rg_w_x, v_rg_w_y, v_rg_conv_w, v_rg_conv_b, v_rg_w_a, v_rg_b_a, v_rg_w_i, v_rg_b_i, v_rg_lambda, v_rg_w_out,)
    n_w = len(ARG_NAMES)
    p = dict(zip(ARG_NAMES, args[:n_w]))
    p['loss_target'] = args[n_w]
    moments = {}
    for i, n in enumerate(WEIGHTS):
        moments['m_' + n] = args[n_w + 1 + i]
        moments['v_' + n] = args[n_w + 1 + len(WEIGHTS) + i]
    out = _step(p, moments)
    res = [out['loss'], out['grad_x']]
    for prefix in ('grad_', 'delta_', 'new_m_', 'new_v_'):
        res += [out[prefix + n] for n in WEIGHTS]
    return tuple(res)
```

```python
import functools
import math

import numpy as np
import jax
import jax.numpy as jnp
from jax import lax
from jax.experimental import pallas as pl
from jax.experimental.pallas import tpu as pltpu

F32 = jnp.float32
BF = jnp.bfloat16
HI = lax.Precision.HIGHEST

D_MODEL = 1024
DEPTH = 4
N_META = 16
CHUNK = 128
PAD = CHUNK - N_META
EPS = 1e-6
SSD_HEADS = 16
SSD_HEAD_DIM = 64
SSD_D_INNER = 1024
SSD_STATE = 128
SSD_CONV_CH = 1536
MLA_HEADS = 16
MLA_NOPE = 64
MLA_ROPE = 32
MLA_V = 64
MLA_Q_RANK = 384
MLA_KV_RANK = 256
ROPE_BASE = 10000.0
LRU_WIDTH = 1280
LRU_BLOCKS = 10
LRU_C = 8.0
D_FF = 4096
N_DEV = 8
LANE = 128
IN_W = 3456
OFF_Z, OFF_XBC, OFF_CKV, OFF_DT, OFF_KR, OFF_CQ = 0, 1024, 2560, 2816, 2944, 3072

ADAM_LR = 0.001
ADAM_B1 = 0.9
ADAM_B2 = 0.999
ADAM_EPS = 1e-08
ADAM_WD = 0.01
ADAM_STEP = 10

VMEM_LIMIT = 56 * 1024 * 1024
NEG = -1e30


def _pick(n, cands):
    for c in cands:
        if n % c == 0:
            return c
    return n


def _cp(sem=None):
    return pltpu.CompilerParams(dimension_semantics=sem, vmem_limit_bytes=VMEM_LIMIT)


def _sds(shape, dtype):
    return jax.ShapeDtypeStruct(tuple(shape), dtype)


def _silu(x):
    return x * jax.nn.sigmoid(x)


def _softplus(x):
    return jnp.maximum(x, 0.0) + jnp.log(1.0 + jnp.exp(-jnp.abs(x)))


def _gelu(x):
    c = math.sqrt(2.0 / math.pi)
    return 0.5 * x * (1.0 + jnp.tanh(c * (x + 0.044715 * (x * x * x))))


def _row_mask(i, tr, shape, first_valid=PAD):
    row = i * tr + lax.broadcasted_iota(jnp.int32, shape, 0)
    return row >= first_valid


def matmul(a, b, mode, out_dtypes=(F32,), epi=None, extras=(), name="mm", tm=None, tn=None):
    if mode == "nn":
        (m, k), (k2, n) = a.shape, b.shape
    elif mode == "nt":
        (m, k), (n, k2) = a.shape, b.shape
    else:
        (k, m), (k2, n) = a.shape, b.shape
    assert k == k2, (a.shape, b.shape, mode)
    tm = tm or _pick(m, (1056, 1024, 768, 640, 512, 384, 256, 128))
    tn = tn or _pick(n, (512, 640, 384, 256, 128))
    if mode == "tn":
        tm = _pick(m, (512, 384, 256, 128))
    dims = {"nn": (((1,), (0,)), ((), ())), "nt": (((1,), (1,)), ((), ())), "tn": (((0,), (0,)), ((), ()))}[mode]
    n_ex = len(extras)

    def body(a_ref, b_ref, *rest):
        ex_refs, out_refs = rest[:n_ex], rest[n_ex:]
        acc = lax.dot_general(a_ref[...].astype(BF), b_ref[...].astype(BF), dims, preferred_element_type=F32)
        outs = (acc,) if epi is None else epi(acc, *[r[...] for r in ex_refs])
        for r, o in zip(out_refs, outs):
            r[...] = o.astype(r.dtype)

    a_spec = pl.BlockSpec((k, tm), lambda i, j: (0, i)) if mode == "tn" else pl.BlockSpec((tm, k), lambda i, j: (i, 0))
    b_spec = pl.BlockSpec((tn, k), lambda i, j: (j, 0)) if mode == "nt" else pl.BlockSpec((k, tn), lambda i, j: (0, j))
    o_spec = pl.BlockSpec((tm, tn), lambda i, j: (i, j))
    outs = pl.pallas_call(
        body,
        out_shape=tuple(_sds((m, n), dt) for dt in out_dtypes),
        grid=(m // tm, n // tn),
        in_specs=[a_spec, b_spec] + [o_spec] * n_ex,
        out_specs=tuple(o_spec for _ in out_dtypes),
        compiler_params=_cp(("parallel", "parallel")),
        name=name,
    )(a, b, *extras)
    return outs[0] if len(out_dtypes) == 1 else outs


def _rt(t):
    return _pick(t, (384, 256, 128))


def norm_fwd(x, g, out_dtype, col_blk=0, width=None, name="norm_fwd"):
    t = x.shape[0]
    w = width or x.shape[1]
    tr = _rt(t)

    def body(x_ref, g_ref, o_ref):
        xv = x_ref[...]
        r = lax.rsqrt(jnp.mean(xv * xv, axis=-1, keepdims=True) + EPS)
        o_ref[...] = (xv * r * g_ref[...]).astype(o_ref.dtype)

    return pl.pallas_call(
        body,
        out_shape=_sds((t, w), out_dtype),
        grid=(t // tr,),
        in_specs=[pl.BlockSpec((tr, w), lambda i: (i, col_blk)), pl.BlockSpec((1, w), lambda i: (0, 0))],
        out_specs=pl.BlockSpec((tr, w), lambda i: (i, 0)),
        compiler_params=_cp(("parallel",)),
        name=name,
    )(x, g.reshape(1, w))


def norm_bwd(x, g, dy, dres=None, mask_pad=False, out_dtype=F32, col_blk=0, width=None, dy_col_blk=0, name="norm_bwd"):
    t = x.shape[0]
    w = width or x.shape[1]
    tr = _rt(t)
    has_res = dres is not None

    def body(x_ref, g_ref, dy_ref, *rest):
        if has_res:
            res_ref, dx_ref, dg_ref = rest
        else:
            dx_ref, dg_ref = rest
        i = pl.program_id(0)
        xv = x_ref[...]
        dyv = dy_ref[...].astype(F32)
        if mask_pad:
            dyv = jnp.where(_row_mask(i, tr, dyv.shape), dyv, 0.0)
        r = lax.rsqrt(jnp.mean(xv * xv, axis=-1, keepdims=True) + EPS)
        xh = xv * r
        dyg = dyv * g_ref[...]
        dx = r * (dyg - xh * jnp.mean(dyg * xh, axis=-1, keepdims=True))
        if has_res:
            dx = dx + res_ref[...]
        dx_ref[...] = dx.astype(dx_ref.dtype)

        @pl.when(i == 0)
        def _():
            dg_ref[...] = jnp.zeros_like(dg_ref)

        dg_ref[...] += jnp.sum(dyv * xh, axis=0, keepdims=True)

    in_specs = [pl.BlockSpec((tr, w), lambda i: (i, col_blk)), pl.BlockSpec((1, w), lambda i: (0, 0)),
                pl.BlockSpec((tr, w), lambda i: (i, dy_col_blk))]
    args = [x, g.reshape(1, w), dy]
    if has_res:
        in_specs.append(pl.BlockSpec((tr, w), lambda i: (i, 0)))
        args.append(dres)
    dx, dg = pl.pallas_call(
        body,
        out_shape=(_sds((t, w), out_dtype), _sds((1, w), F32)),
        grid=(t // tr,),
        in_specs=in_specs,
        out_specs=(pl.BlockSpec((tr, w), lambda i: (i, 0)), pl.BlockSpec((1, w), lambda i: (0, 0))),
        compiler_params=_cp(("arbitrary",)),
        name=name,
    )(*args)
    return dx, dg.reshape(w)


def resadd_fwd(h, m, g, name="resadd"):
    t, w = h.shape
    tr = _rt(t)

    def body(h_ref, m_ref, g_ref, o_ref):
        mv = m_ref[...]
        r = lax.rsqrt(jnp.mean(mv * mv, axis=-1, keepdims=True) + EPS)
        y = mv * r * g_ref[...]
        o_ref[...] = h_ref[...] + jnp.where(_row_mask(pl.program_id(0), tr, y.shape), y, 0.0)

    return pl.pallas_call(
        body,
        out_shape=_sds((t, w), F32),
        grid=(t // tr,),
        in_specs=[pl.BlockSpec((tr, w), lambda i: (i, 0)), pl.BlockSpec((tr, w), lambda i: (i, 0)),
                  pl.BlockSpec((1, w), lambda i: (0, 0))],
        out_specs=pl.BlockSpec((tr, w), lambda i: (i, 0)),
        compiler_params=_cp(("parallel",)),
        name=name,
    )(h, m, g.reshape(1, w))


def loss_fwd_bwd(h, target):
    t, w = h.shape
    tr = _rt(t)

    def body(h_ref, t_ref, s_ref, dh_ref):
        i = pl.program_id(0)
        err = h_ref[...] - t_ref[...]
        err = jnp.where(_row_mask(i, tr, err.shape, PAD + N_META), err, 0.0)
        dh_ref[...] = err * (1.0 / w)

        @pl.when(i == 0)
        def _():
            s_ref[...] = jnp.zeros_like(s_ref)

        s_ref[...] += jnp.sum(err * err).reshape(1, 1)

    s, dh = pl.pallas_call(
        body,
        out_shape=(_sds((1, LANE), F32), _sds((t, w), F32)),
        grid=(t // tr,),
        in_specs=[pl.BlockSpec((tr, w), lambda i: (i, 0)), pl.BlockSpec((tr, w), lambda i: (i, 0))],
        out_specs=(pl.BlockSpec((1, LANE), lambda i: (0, 0)), pl.BlockSpec((tr, w), lambda i: (i, 0))),
        compiler_params=_cp(("arbitrary",)),
        name="loss",
    )(h, target)
    return 0.5 * s[0, 0] / w, dh


def _shift_down(ext, k, n):
    return pltpu.roll(ext, k, 0)[8:]


def _conv_pre(ext, x, w_ref, n):
    return (w_ref[4:5, :] + w_ref[3:4, :] * x + w_ref[2:3, :] * _shift_down(ext, 1, n)
            + w_ref[1:2, :] * _shift_down(ext, 2, n) + w_ref[0:1, :] * _shift_down(ext, 3, n))


def _conv_bwd_parts(dpre, dnext, x, ext, w_ref, n):
    extd = jnp.concatenate([dpre, dnext], axis=0)
    ln = n + 8
    dx = (w_ref[3:4, :] * dpre + w_ref[2:3, :] * pltpu.roll(extd, ln - 1, 0)[:n]
          + w_ref[1:2, :] * pltpu.roll(extd, ln - 2, 0)[:n] + w_ref[0:1, :] * pltpu.roll(extd, ln - 3, 0)[:n])
    sums = [jnp.sum(dpre * _shift_down(ext, 3, n), axis=0, keepdims=True),
            jnp.sum(dpre * _shift_down(ext, 2, n), axis=0, keepdims=True),
            jnp.sum(dpre * _shift_down(ext, 1, n), axis=0, keepdims=True),
            jnp.sum(dpre * x, axis=0, keepdims=True),
            jnp.sum(dpre, axis=0, keepdims=True)]
    return dx, sums


def _rows_block(sums):
    row = lax.broadcasted_iota(jnp.int32, (8, LANE), 0)
    out = jnp.zeros((8, LANE), F32)
    for k, s in enumerate(sums):
        out = jnp.where(row == k, s, out)
    return out


def conv_silu_fwd(x, col0_blk, nblk, wb, name="conv_fwd"):
    t = x.shape[0]
    c = nblk * LANE
    tr = _rt(t)

    def body(x_ref, w_ref, o_ref, prev):
        ti = pl.program_id(1)

        @pl.when(ti == 0)
        def _():
            prev[...] = jnp.zeros_like(prev)

        xv = x_ref[...]
        ext = jnp.concatenate([prev[...], xv], axis=0)
        o_ref[...] = _silu(_conv_pre(ext, xv, w_ref, tr))
        prev[...] = xv[tr - 8:, :]

    return pl.pallas_call(
        body,
        out_shape=_sds((t, c), F32),
        grid=(nblk, t // tr),
        in_specs=[pl.BlockSpec((tr, LANE), lambda cb, ti: (ti, col0_blk + cb)),
                  pl.BlockSpec((8, LANE), lambda cb, ti: (0, cb))],
        out_specs=pl.BlockSpec((tr, LANE), lambda cb, ti: (ti, cb)),
        scratch_shapes=[pltpu.VMEM((8, LANE), F32)],
        compiler_params=_cp(("parallel", "arbitrary")),
        name=name,
    )(x, wb)


def conv_silu_bwd(x, col0_blk, nblk, wb, dout, name="conv_bwd"):
    t = x.shape[0]
    c = nblk * LANE
    tr = _rt(t)
    nt = t // tr
    r8 = tr // 8

    def body(x_ref, xp_ref, w_ref, do_ref, dx_ref, dwb_ref, dnext):
        ti = pl.program_id(1)
        tt = nt - 1 - ti

        @pl.when(ti == 0)
        def _():
            dnext[...] = jnp.zeros_like(dnext)
            dwb_ref[...] = jnp.zeros_like(dwb_ref)

        xv = x_ref[...]
        halo = jnp.where(tt > 0, xp_ref[...], 0.0)
        ext = jnp.concatenate([halo, xv], axis=0)
        pre = _conv_pre(ext, xv, w_ref, tr)
        s = jax.nn.sigmoid(pre)
        dpre = do_ref[...] * (s + pre * s * (1.0 - s))
        dx, sums = _conv_bwd_parts(dpre, dnext[...], xv, ext, w_ref, tr)
        dx_ref[...] = dx
        dwb_ref[...] += _rows_block(sums)
        dnext[...] = dpre[:8, :]

    return pl.pallas_call(
        body,
        out_shape=(_sds((t, c), F32), _sds((8, c), F32)),
        grid=(nblk, nt),
        in_specs=[pl.BlockSpec((tr, LANE), lambda cb, ti: (nt - 1 - ti, col0_blk + cb)),
                  pl.BlockSpec((8, LANE), lambda cb, ti: (jnp.maximum((nt - 1 - ti) * r8 - 1, 0), col0_blk + cb)),
                  pl.BlockSpec((8, LANE), lambda cb, ti: (0, cb)),
                  pl.BlockSpec((tr, LANE), lambda cb, ti: (nt - 1 - ti, cb))],
        out_specs=(pl.BlockSpec((tr, LANE), lambda cb, ti: (nt - 1 - ti, cb)),
                   pl.BlockSpec((8, LANE), lambda cb, ti: (0, cb))),
        scratch_shapes=[pltpu.VMEM((8, LANE), F32)],
        compiler_params=_cp(("parallel", "arbitrary")),
        name=name,
    )(x, x, wb, dout)


def gated_norm_fwd(y, proj, g, name="gnorm_fwd"):
    t, w = y.shape
    tr = _rt(t)

    def body(y_ref, z_ref, g_ref, o_ref):
        v = y_ref[...] * _silu(z_ref[...])
        r = lax.rsqrt(jnp.mean(v * v, axis=-1, keepdims=True) + EPS)
        o_ref[...] = (v * r * g_ref[...]).astype(o_ref.dtype)

    return pl.pallas_call(
        body,
        out_shape=_sds((t, w), BF),
        grid=(t // tr,),
        in_specs=[pl.BlockSpec((tr, w), lambda i: (i, 0)), pl.BlockSpec((tr, w), lambda i: (i, OFF_Z // w)),
                  pl.BlockSpec((1, w), lambda i: (0, 0))],
        out_specs=pl.BlockSpec((tr, w), lambda i: (i, 0)),
        compiler_params=_cp(("parallel",)),
        name=name,
    )(y, proj, g.reshape(1, w))


def gated_norm_bwd(y, proj, g, dyab, name="gnorm_bwd"):
    t, w = y.shape
    tr = _rt(t)

    def body(y_ref, z_ref, g_ref, do_ref, dy_ref, dz_ref, dg_ref):
        i = pl.program_id(0)
        yv, zv, dov = y_ref[...], z_ref[...], do_ref[...]
        s = jax.nn.sigmoid(zv)
        sz = zv * s
        v = yv * sz
        r = lax.rsqrt(jnp.mean(v * v, axis=-1, keepdims=True) + EPS)
        vh = v * r
        dvg = dov * g_ref[...]
        dv = r * (dvg - vh * jnp.mean(dvg * vh, axis=-1, keepdims=True))
        dy_ref[...] = dv * sz
        dz_ref[...] = dv * yv * (s + sz * (1.0 - s))

        @pl.when(i == 0)
        def _():
            dg_ref[...] = jnp.zeros_like(dg_ref)

        dg_ref[...] += jnp.sum(dov * vh, axis=0, keepdims=True)

    dy, dz, dg = pl.pallas_call(
        body,
        out_shape=(_sds((t, w), F32), _sds((t, w), F32), _sds((1, w), F32)),
        grid=(t // tr,),
        in_specs=[pl.BlockSpec((tr, w), lambda i: (i, 0)), pl.BlockSpec((tr, w), lambda i: (i, OFF_Z // w)),
                  pl.BlockSpec((1, w), lambda i: (0, 0)), pl.BlockSpec((tr, w), lambda i: (i, 0))],
        out_specs=(pl.BlockSpec((tr, w), lambda i: (i, 0)), pl.BlockSpec((tr, w), lambda i: (i, 0)),
                   pl.BlockSpec((1, w), lambda i: (0, 0))),
        compiler_params=_cp(("arbitrary",)),
        name=name,
    )(y, proj, g.reshape(1, w), dyab)
    return dy, dz, dg.reshape(w)


def rope_tables(t):
    inv = ROPE_BASE ** (-jnp.arange(0, MLA_ROPE, 2, dtype=F32) / MLA_ROPE)
    pos = (jnp.arange(t, dtype=F32) - PAD)[:, None]
    ang = pos * inv[None, :]
    cos, sin = jnp.cos(ang), jnp.sin(ang)
    z16 = jnp.zeros((t, 16), F32)
    z32 = jnp.zeros((t, 32), F32)
    c = jnp.concatenate([jnp.ones((t, 64), F32), cos, cos, z32], axis=1)
    s1 = jnp.concatenate([jnp.zeros((t, 64), F32), z16, sin, z32], axis=1)
    s2 = jnp.concatenate([jnp.zeros((t, 64), F32), -sin, z16, z32], axis=1)
    return c, s1, s2


def _rope(x, c, s1, s2):
    return x * c + pltpu.roll(x, 16, 1) * s1 + pltpu.roll(x, LANE - 16, 1) * s2


def _rope_t(d, c, s1, s2):
    return d * c + pltpu.roll(d * s1, LANE - 16, 1) + pltpu.roll(d * s2, 16, 1)


def rope_fwd(q_raw, kv_raw, proj, tabs):
    t = q_raw.shape[0]
    tr = _rt(t)
    hw = MLA_HEADS * LANE

    def body(q_ref, k_ref, v_ref, kr_ref, c_ref, s1_ref, s2_ref, qo_ref, ko_ref, vo_ref):
        c, s1, s2 = c_ref[...], s1_ref[...], s2_ref[...]
        kr = _rope(kr_ref[...], c, s1, s2)
        for h in range(MLA_HEADS):
            sl = slice(h * LANE, (h + 1) * LANE)
            qo_ref[:, sl] = _rope(q_ref[:, sl], c, s1, s2).astype(BF)
            ko_ref[:, sl] = (k_ref[:, sl] + kr).astype(BF)
        vo_ref[...] = v_ref[...].astype(BF)

    tab_spec = pl.BlockSpec((tr, LANE), lambda i: (i, 0))
    return pl.pallas_call(
        body,
        out_shape=(_sds((t, hw), BF), _sds((t, hw), BF), _sds((t, 1024), BF)),
        grid=(t // tr,),
        in_specs=[pl.BlockSpec((tr, hw), lambda i: (i, 0)), pl.BlockSpec((tr, hw), lambda i: (i, 0)),
                  pl.BlockSpec((tr, 1024), lambda i: (i, 2)), pl.BlockSpec((tr, LANE), lambda i: (i, OFF_KR // LANE)),
                  tab_spec, tab_spec, tab_spec],
        out_specs=(pl.BlockSpec((tr, hw), lambda i: (i, 0)), pl.BlockSpec((tr, hw), lambda i: (i, 0)),
                   pl.BlockSpec((tr, 1024), lambda i: (i, 0))),
        compiler_params=_cp(("parallel",)),
        name="rope_fwd",
    )(q_raw, kv_raw, kv_raw, proj, *tabs)


def rope_bwd(dq_cat, dk_cat, tabs):
    t = dq_cat.shape[0]
    tr = _rt(t)
    hw = MLA_HEADS * LANE

    def body(dq_ref, dk_ref, c_ref, s1_ref, s2_ref, dqo_ref, dkr_ref):
        c, s1, s2 = c_ref[...], s1_ref[...], s2_ref[...]
        acc = jnp.zeros((tr, LANE), F32)
        for h in range(MLA_HEADS):
            sl = slice(h * LANE, (h + 1) * LANE)
            dqo_ref[:, sl] = _rope_t(dq_ref[:, sl], c, s1, s2).astype(BF)
            acc = acc + dk_ref[:, sl]
        lane = lax.broadcasted_iota(jnp.int32, (tr, LANE), 1)
        dkr_ref[...] = jnp.where((lane >= 64) & (lane < 96), _rope_t(acc, c, s1, s2), 0.0)

    tab_spec = pl.BlockSpec((tr, LANE), lambda i: (i, 0))
    return pl.pallas_call(
        body,
        out_shape=(_sds((t, hw), BF), _sds((t, LANE), F32)),
        grid=(t // tr,),
        in_specs=[pl.BlockSpec((tr, hw), lambda i: (i, 0)), pl.BlockSpec((tr, hw), lambda i: (i, 0)),
                  tab_spec, tab_spec, tab_spec],
        out_specs=(pl.BlockSpec((tr, hw), lambda i: (i, 0)), pl.BlockSpec((tr, LANE), lambda i: (i, 0))),
        compiler_params=_cp(("parallel",)),
        name="rope_bwd",
    )(dq_cat, dk_cat, *tabs)


ATT_SCALE = (MLA_NOPE + MLA_ROPE) ** -0.5
NT_DIMS = (((1,), (1,)), ((), ()))
TN_DIMS = (((0,), (0,)), ((), ()))


def _att_mask(qi, ki, tq, tk):
    qpos = qi * tq + lax.broadcasted_iota(jnp.int32, (tq, tk), 0)
    kpos = ki * tk + lax.broadcasted_iota(jnp.int32, (tq, tk), 1)
    return (kpos <= qpos) & (kpos >= PAD)


def _half_masks(n):
    lane = lax.broadcasted_iota(jnp.int32, (n, LANE), 1)
    return lane < 64, lane >= 64


def _att_tile(t):
    return _pick(t, (384, 256, 128))


def _ds(i, n):
    return pl.ds(i * n, n) if isinstance(i, int) else pl.ds(pl.multiple_of(i * n, n), n)


def attn_fwd(q_cat, k_cat, v, carry=None):
    t = q_cat.shape[0]
    tq = tk = _att_tile(t)
    nq = t // tq
    n_pair = MLA_HEADS // 2
    nx = carry.k if carry else 0

    def body(*refs):
        q_ref, k_ref, v_ref = refs[:3]
        o_ref, lse_ref = refs[3 + nx:5 + nx]
        m_s, l_s, acc_s = refs[5 + 2 * nx:8 + 2 * nx]
        qi = pl.program_id(1)
        if carry:
            start, middle, finish = carry.phases(refs[3:3 + nx], refs[5 + nx:5 + 2 * nx], refs[8 + 2 * nx:])
            pair = pl.program_id(0)
            pl.when((pair == 0) & (qi == 0))(start)
            pl.when((pair == n_pair // 2) & (qi == 0))(middle)
        m_s[...] = jnp.full_like(m_s, NEG)
        l_s[...] = jnp.zeros_like(l_s)
        acc_s[...] = jnp.zeros_like(acc_s)
        lo_q, _ = _half_masks(tq)
        halves = _half_masks(tk)

        def step(ki, masked):
            rows = _ds(ki, tk)
            vv = v_ref[rows, :]
            alphas, pvs = [], []
            for hh in range(2):
                sl = slice(hh * LANE, (hh + 1) * LANE)
                s = lax.dot_general(q_ref[:, sl], k_ref[rows, sl], NT_DIMS, preferred_element_type=F32) * ATT_SCALE
                if masked:
                    s = jnp.where(_att_mask(qi, ki, tq, tk), s, NEG)
                m_old = m_s[hh]
                m_new = jnp.maximum(m_old, jnp.max(s, axis=-1, keepdims=True))
                p = jnp.exp(s - m_new[:, 0:1])
                alpha = jnp.exp(m_old - m_new)
                l_s[hh] = alpha * l_s[hh] + jnp.sum(p, axis=-1, keepdims=True)
                m_s[hh] = m_new
                vm = jnp.where(halves[hh], vv, jnp.zeros_like(vv))
                pvs.append(jnp.dot(p.astype(BF), vm, preferred_element_type=F32))
                alphas.append(alpha)
            acc_s[...] = acc_s[...] * jnp.where(lo_q, alphas[0], alphas[1]) + pvs[0] + pvs[1]

        step(0, True)

        def inner(ki, carry):
            step(ki, False)
            return carry

        lax.fori_loop(1, qi, inner, 0)

        @pl.when(qi > 0)
        def _():
            step(qi, True)

        l = jnp.where(lo_q, l_s[0], l_s[1])
        m = jnp.where(lo_q, m_s[0], m_s[1])
        o_ref[...] = (acc_s[...] / l).astype(o_ref.dtype)
        lse_ref[...] = m + jnp.log(l)
        if carry:
            pl.when((pair == n_pair - 1) & (qi == nq - 1))(finish)

    outs = pl.pallas_call(
        body,
        out_shape=(_sds((t, 1024), BF), _sds((t, 1024), F32)) + tuple(carry.out_shapes() if carry else ()),
        grid=(n_pair, nq),
        in_specs=[pl.BlockSpec((tq, 2 * LANE), lambda p, qi: (qi, p)),
                  pl.BlockSpec((t, 2 * LANE), lambda p, qi: (0, p)),
                  pl.BlockSpec((t, LANE), lambda p, qi: (0, p))] + [ANY] * nx,
        out_specs=(pl.BlockSpec((tq, LANE), lambda p, qi: (qi, p)),
                   pl.BlockSpec((tq, LANE), lambda p, qi: (qi, p))) + (ANY,) * nx,
        scratch_shapes=[pltpu.VMEM((2, tq, LANE), F32), pltpu.VMEM((2, tq, LANE), F32), pltpu.VMEM((tq, LANE), F32)]
        + (carry.scratch() if carry else []),
        compiler_params=_cp(("arbitrary", "arbitrary") if carry else ("parallel", "parallel")),
        name="attn_fwd_carrying" if carry else "attn_fwd",
    )(q_cat, k_cat, v, *(carry.arrs if carry else ()))
    return outs[0], outs[1], list(outs[2:])


def attn_bwd(q_cat, k_cat, v, o, lse, dyab, carry=None):
    t = q_cat.shape[0]
    tq = tk = _att_tile(t)
    nq = t // tq
    n_pair = MLA_HEADS // 2
    nx = carry.k if carry else 0

    def body(*refs):
        q_ref, k_ref, v_ref, o_ref, lse_ref, do_ref = refs[:6]
        dq_ref, dk_ref, dv_ref = refs[6 + nx:9 + nx]
        ki = pl.program_id(1)
        if carry:
            start, middle, finish = carry.phases(refs[6:6 + nx], refs[9 + nx:9 + 2 * nx], refs[9 + 2 * nx:])
            pair = pl.program_id(0)
            pl.when((pair == 0) & (ki == 0))(start)
            pl.when((pair == n_pair // 2) & (ki == 0))(middle)

        @pl.when(ki == 0)
        def _():
            dq_ref[...] = jnp.zeros_like(dq_ref)

        dk_ref[...] = jnp.zeros_like(dk_ref)
        dv_ref[...] = jnp.zeros_like(dv_ref)
        halves = _half_masks(tq)
        vv = v_ref[...]

        def step(qi, masked):
            rows = _ds(qi, tq)
            dov, ov, lse_v = do_ref[rows, :], o_ref[rows, :].astype(F32), lse_ref[rows, :]
            for hh in range(2):
                sl = slice(hh * LANE, (hh + 1) * LANE)
                qh = q_ref[rows, sl]
                s = lax.dot_general(qh, k_ref[:, sl], NT_DIMS, preferred_element_type=F32) * ATT_SCALE
                if masked:
                    s = jnp.where(_att_mask(qi, ki, tq, tk), s, NEG)
                p = jnp.exp(s - lse_v[:, 64 * hh:64 * hh + 1])
                dom = jnp.where(halves[hh], dov, 0.0)
                delta = jnp.sum(dom * ov, axis=-1, keepdims=True)
                dp = lax.dot_general(dom.astype(BF), vv, NT_DIMS, preferred_element_type=F32)
                ds = (p * (dp - delta) * ATT_SCALE).astype(BF)
                dv_ref[...] += lax.dot_general(p.astype(BF), dom.astype(BF), TN_DIMS, preferred_element_type=F32)
                dk_ref[:, sl] += lax.dot_general(ds, qh, TN_DIMS, preferred_element_type=F32)
                dq_ref[rows, sl] += jnp.dot(ds, k_ref[:, sl], preferred_element_type=F32)

        def masked_step(qi, carry):
            step(qi, True)
            return carry

        def plain_step(qi, carry):
            step(qi, False)
            return carry

        step(ki, True)
        lax.fori_loop(ki + 1, jnp.where(ki == 0, nq, ki + 1), masked_step, 0)
        lax.fori_loop(ki + 1, jnp.where(ki == 0, ki + 1, nq), plain_step, 0)
        if carry:
            pl.when((pair == n_pair - 1) & (ki == nq - 1))(finish)

    full = lambda w, off=0: pl.BlockSpec((t, w), lambda p, ki: (0, p + off))
    blk = lambda w: pl.BlockSpec((tk, w), lambda p, ki: (ki, p))
    outs = pl.pallas_call(
        body,
        out_shape=(_sds((t, 2048), F32), _sds((t, 2048), F32), _sds((t, 1024), F32))
        + tuple(carry.out_shapes() if carry else ()),
        grid=(n_pair, nq),
        in_specs=[full(2 * LANE), blk(2 * LANE), blk(LANE), full(LANE), full(LANE), full(LANE, 8)] + [ANY] * nx,
        out_specs=(full(2 * LANE), blk(2 * LANE), blk(LANE)) + (ANY,) * nx,
        scratch_shapes=carry.scratch() if carry else [],
        compiler_params=_cp(("arbitrary", "arbitrary") if carry else ("parallel", "arbitrary")),
        name="attn_bwd_carrying" if carry else "attn_bwd",
    )(q_cat, k_cat, v, o, lse, dyab, *(carry.arrs if carry else ()))
    return outs[0], outs[1], outs[2], list(outs[3:])


N_PAIR = SSD_HEADS // 2


def _hdot(a, b):
    return jnp.dot(a, b, precision=HI, preferred_element_type=F32)


def _ssd_chunk(xs, bg, cg, dtraw, hin, dt_bias, a_log, dskip, rowmask):
    ln = CHUNK
    row = lax.broadcasted_iota(jnp.int32, (ln, ln), 0)
    col = lax.broadcasted_iota(jnp.int32, (ln, ln), 1)
    causal = row >= col
    ltri = causal.astype(F32)
    ones = jnp.ones((ln, ln), F32)
    k16 = lax.broadcasted_iota(jnp.int32, (SSD_HEADS, LANE), 0)
    upper = (lax.broadcasted_iota(jnp.int32, (SSD_HEADS, LANE), 1) >= 64).astype(jnp.int32)
    lane = lax.broadcasted_iota(jnp.int32, (ln, LANE), 1)
    halves = (lane < 64, lane >= 64)

    dt = _softplus(dtraw + dt_bias) * rowmask
    da = dt * (-jnp.exp(a_log))
    acs = _hdot(ltri, da)
    tot = _hdot(ones, da)
    bm = [b * rowmask for b in bg]
    cm = [c * rowmask for c in cg]
    cb = [lax.dot_general(cm[g].astype(BF), bm[g].astype(BF), NT_DIMS, preferred_element_type=F32) for g in range(2)]
    ys, hout = [], []
    for p in range(N_PAIR):
        g = p // (N_PAIR // 2)
        e_p = (k16 == 2 * p + upper).astype(F32)
        xdt = xs[p] * _hdot(dt, e_p)
        y = jnp.zeros((ln, LANE), F32)
        snew = jnp.zeros((ln, LANE), F32)
        for hh in range(2):
            f_h = (k16 == 2 * p + hh).astype(F32)
            m = _hdot(acs, f_h)
            mt = _hdot(tot, f_h)
            dec = jnp.exp(jnp.where(causal, m - m.T, NEG))
            xm = jnp.where(halves[hh], xdt, 0.0).astype(BF)
            y = y + jnp.dot((cb[g] * dec).astype(BF), xm, preferred_element_type=F32)
            bd = bm[g] * jnp.exp(mt - m)
            snew = snew + lax.dot_general(bd.astype(BF), xm, TN_DIMS, preferred_element_type=F32)
        y_off = jnp.dot(cm[g].astype(BF), hin[p].astype(BF), preferred_element_type=F32) * jnp.exp(_hdot(acs, e_p))
        ys.append(y + y_off + _hdot(dskip, e_p) * xs[p])
        hout.append(jnp.exp(_hdot(tot, e_p)) * hin[p] + snew)
    return ys, hout


def _ssd_load(x_ref, dt_ref):
    xs = [x_ref[:, p * LANE:(p + 1) * LANE] for p in range(N_PAIR)]
    bg = [x_ref[:, SSD_D_INNER + g * LANE:SSD_D_INNER + (g + 1) * LANE] for g in range(2)]
    cg = [x_ref[:, SSD_D_INNER + (2 + g) * LANE:SSD_D_INNER + (3 + g) * LANE] for g in range(2)]
    return xs, bg, cg, dt_ref[:, 0:SSD_HEADS]


def _chunk_rowmask(c):
    return ((c * CHUNK + lax.broadcasted_iota(jnp.int32, (CHUNK, 1), 0)) >= PAD).astype(F32)


def ssd_fwd(xbc_c, proj, dt_bias, a_log, dskip):
    t = xbc_c.shape[0]
    nc = t // CHUNK

    def body(x_ref, dt_ref, dtb_ref, al_ref, d_ref, y_ref, hs_ref, h_s):
        c = pl.program_id(0)

        @pl.when(c == 0)
        def _():
            h_s[...] = jnp.zeros_like(h_s)

        xs, bg, cg, dtraw = _ssd_load(x_ref, dt_ref)
        hin = [h_s[p] for p in range(N_PAIR)]
        hs_ref[0] = h_s[...]
        ys, hout = _ssd_chunk(xs, bg, cg, dtraw, hin, dtb_ref[...], al_ref[...], d_ref[...], _chunk_rowmask(c))
        for p in range(N_PAIR):
            y_ref[:, p * LANE:(p + 1) * LANE] = ys[p]
            h_s[p] = hout[p]

    par = pl.BlockSpec((1, SSD_HEADS), lambda c: (0, 0))
    return pl.pallas_call(
        body,
        out_shape=(_sds((t, SSD_D_INNER), F32), _sds((nc, N_PAIR, CHUNK, LANE), F32)),
        grid=(nc,),
        in_specs=[pl.BlockSpec((CHUNK, SSD_CONV_CH), lambda c: (c, 0)),
                  pl.BlockSpec((CHUNK, LANE), lambda c: (c, OFF_DT // LANE)), par, par, par],
        out_specs=(pl.BlockSpec((CHUNK, SSD_D_INNER), lambda c: (c, 0)),
                   pl.BlockSpec((1, N_PAIR, CHUNK, LANE), lambda c: (c, 0, 0, 0))),
        scratch_shapes=[pltpu.VMEM((N_PAIR, CHUNK, LANE), F32)],
        compiler_params=_cp(("arbitrary",)),
        name="ssd_fwd",
    )(xbc_c, proj, dt_bias.reshape(1, -1), a_log.reshape(1, -1), dskip.reshape(1, -1))


def ssd_bwd(xbc_c, proj, dt_bias, a_log, dskip, hs, dy):
    t = xbc_c.shape[0]
    nc = t // CHUNK

    def body(x_ref, dt_ref, dtb_ref, al_ref, d_ref, hs_ref, dy_ref, dx_ref, ddt_ref, dpar_ref, dh_s):
        ci = pl.program_id(0)
        c = nc - 1 - ci

        @pl.when(ci == 0)
        def _():
            dh_s[...] = jnp.zeros_like(dh_s)
            dpar_ref[...] = jnp.zeros_like(dpar_ref)

        xs, bg, cg, dtraw = _ssd_load(x_ref, dt_ref)
        hin = [hs_ref[0, p] for p in range(N_PAIR)]
        rowmask = _chunk_rowmask(c)
        fn = lambda xs_, bg_, cg_, dtraw_, hin_, dtb_, al_, d_: _ssd_chunk(xs_, bg_, cg_, dtraw_, hin_, dtb_, al_, d_, rowmask)
        _, vjp = jax.vjp(fn, xs, bg, cg, dtraw, hin, dtb_ref[...], al_ref[...], d_ref[...])
        dys = [dy_ref[:, p * LANE:(p + 1) * LANE] for p in range(N_PAIR)]
        dhs = [dh_s[p] for p in range(N_PAIR)]
        dxs, dbg, dcg, ddtraw, dhin, ddtb, dal, dd = vjp((dys, dhs))
        for p in range(N_PAIR):
            dx_ref[:, p * LANE:(p + 1) * LANE] = dxs[p]
            dh_s[p] = dhin[p]
        for g in range(2):
            dx_ref[:, SSD_D_INNER + g * LANE:SSD_D_INNER + (g + 1) * LANE] = dbg[g]
            dx_ref[:, SSD_D_INNER + (2 + g) * LANE:SSD_D_INNER + (3 + g) * LANE] = dcg[g]
        ddt_ref[...] = jnp.zeros_like(ddt_ref)
        ddt_ref[:, 0:SSD_HEADS] = ddtraw
        dpar_ref[0:1, 0:SSD_HEADS] += ddtb
        dpar_ref[1:2, 0:SSD_HEADS] += dal
        dpar_ref[2:3, 0:SSD_HEADS] += dd

    par = pl.BlockSpec((1, SSD_HEADS), lambda ci: (0, 0))
    return pl.pallas_call(
        body,
        out_shape=(_sds((t, SSD_CONV_CH), F32), _sds((t, LANE), F32), _sds((8, LANE), F32)),
        grid=(nc,),
        in_specs=[pl.BlockSpec((CHUNK, SSD_CONV_CH), lambda ci: (nc - 1 - ci, 0)),
                  pl.BlockSpec((CHUNK, LANE), lambda ci: (nc - 1 - ci, OFF_DT // LANE)), par, par, par,
                  pl.BlockSpec((1, N_PAIR, CHUNK, LANE), lambda ci: (nc - 1 - ci, 0, 0, 0)),
                  pl.BlockSpec((CHUNK, SSD_D_INNER), lambda ci: (nc - 1 - ci, 0))],
        out_specs=(pl.BlockSpec((CHUNK, SSD_CONV_CH), lambda ci: (nc - 1 - ci, 0)),
                   pl.BlockSpec((CHUNK, LANE), lambda ci: (nc - 1 - ci, 0)),
                   pl.BlockSpec((8, LANE), lambda ci: (0, 0))),
        scratch_shapes=[pltpu.VMEM((N_PAIR, CHUNK, LANE), F32)],
        compiler_params=_cp(("arbitrary",)),
        name="ssd_bwd",
    )(xbc_c, proj, dt_bias.reshape(1, -1), a_log.reshape(1, -1), dskip.reshape(1, -1), hs, dy)


def _neg_expm1(y):
    series = -(y * (1.0 + y * (0.5 + y * (1.0 / 6.0 + y * (1.0 / 24.0 + y * (1.0 / 120.0))))))
    return jnp.where(y > -0.1, series, 1.0 - jnp.exp(y))


def _rg_pw(xr, wa, ba, wi, bi, lam, rowmask):
    xb = xr.astype(BF)
    r = jax.nn.sigmoid(jnp.dot(xb, wa.astype(BF), preferred_element_type=F32) + ba)
    i = jax.nn.sigmoid(jnp.dot(xb, wi.astype(BF), preferred_element_type=F32) + bi)
    log_a = -LRU_C * r * _softplus(-lam)
    a = jnp.exp(log_a)
    u = jnp.sqrt(_neg_expm1(2.0 * log_a)) * (i * xr) * rowmask
    return a, u


def _gelu_grad(x):
    c = math.sqrt(2.0 / math.pi)
    th = jnp.tanh(c * (x + 0.044715 * (x * x * x)))
    return 0.5 * (1.0 + th) + 0.5 * x * (1.0 - th * th) * c * (1.0 + 3.0 * 0.044715 * x * x)


def _scan_fwd(a, u):
    n = a.shape[0]
    row = lax.broadcasted_iota(jnp.int32, a.shape, 0)
    s = 1
    while s < n:
        a_s = jnp.where(row >= s, pltpu.roll(a, s, 0), 1.0)
        u_s = jnp.where(row >= s, pltpu.roll(u, s, 0), 0.0)
        u = u + a * u_s
        a = a * a_s
        s *= 2
    return a, u


def _scan_bwd(b, d):
    n = b.shape[0]
    row = lax.broadcasted_iota(jnp.int32, b.shape, 0)
    s = 1
    while s < n:
        b_s = jnp.where(row < n - s, pltpu.roll(b, n - s, 0), 1.0)
        d_s = jnp.where(row < n - s, pltpu.roll(d, n - s, 0), 0.0)
        d = d + b * d_s
        b = b * b_s
        s *= 2
    return d


def rg_fwd(xg, rgp, w_a, w_i):
    t = xg.shape[0]
    tr = _rt(t)

    def body(x_ref, g_ref, p_ref, wa_ref, wi_ref, hg_ref, hs_ref, prev, hcar):
        ti = pl.program_id(1)

        @pl.when(ti == 0)
        def _():
            prev[...] = jnp.zeros_like(prev)
            hcar[...] = jnp.zeros_like(hcar)

        xv = x_ref[...]
        ext = jnp.concatenate([prev[...], xv], axis=0)
        xr = _conv_pre(ext, xv, p_ref, tr)
        rowmask = _row_mask(ti, tr, (tr, 1)).astype(F32)
        a, u = _rg_pw(xr, wa_ref[0], p_ref[5:6, :], wi_ref[0], p_ref[6:7, :], p_ref[7:8, :], rowmask)
        a_cum, h_loc = _scan_fwd(a, u)
        hs = h_loc + a_cum * hcar[0:1, :]
        hs_ref[...] = hs
        hg_ref[...] = (hs * _gelu(g_ref[...])).astype(hg_ref.dtype)
        hcar[...] = jnp.broadcast_to(hs[tr - 1:tr, :], (8, LANE))
        prev[...] = xv[tr - 8:, :]

    return pl.pallas_call(
        body,
        out_shape=(_sds((t, LRU_WIDTH), BF), _sds((t, LRU_WIDTH), F32)),
        grid=(LRU_BLOCKS, t // tr),
        in_specs=[pl.BlockSpec((tr, LANE), lambda n, ti: (ti, n)),
                  pl.BlockSpec((tr, LANE), lambda n, ti: (ti, LRU_BLOCKS + n)),
                  pl.BlockSpec((8, LANE), lambda n, ti: (0, n)),
                  pl.BlockSpec((1, LANE, LANE), lambda n, ti: (n, 0, 0)),
                  pl.BlockSpec((1, LANE, LANE), lambda n, ti: (n, 0, 0))],
        out_specs=(pl.BlockSpec((tr, LANE), lambda n, ti: (ti, n)), pl.BlockSpec((tr, LANE), lambda n, ti: (ti, n))),
        scratch_shapes=[pltpu.VMEM((8, LANE), F32), pltpu.VMEM((8, LANE), F32)],
        compiler_params=_cp(("parallel", "arbitrary")),
        name="rg_fwd",
    )(xg, xg, rgp, w_a, w_i)


def rg_bwd(xg, rgp, w_a, w_i, hs, dhg):
    t = xg.shape[0]
    tr = _rt(t)
    nt = t // tr
    r8 = tr // 8

    def body(x_ref, xp_ref, g_ref, p_ref, wa_ref, wi_ref, hs_ref, hp_ref, dhg_ref,
             dx_ref, dg_ref, dp_ref, dwa_ref, dwi_ref, gcar, dnext):
        ti = pl.program_id(1)
        tt = nt - 1 - ti

        @pl.when(ti == 0)
        def _():
            gcar[...] = jnp.zeros_like(gcar)
            dnext[...] = jnp.zeros_like(dnext)
            dp_ref[...] = jnp.zeros_like(dp_ref)
            dwa_ref[...] = jnp.zeros_like(dwa_ref)
            dwi_ref[...] = jnp.zeros_like(dwi_ref)

        xv = x_ref[...]
        halo = jnp.where(tt > 0, xp_ref[...], 0.0)
        ext = jnp.concatenate([halo, xv], axis=0)
        xr = _conv_pre(ext, xv, p_ref, tr)
        rowmask = _row_mask(tt, tr, (tr, 1)).astype(F32)
        fn = lambda xr_, wa_, ba_, wi_, bi_, lam_: _rg_pw(xr_, wa_, ba_, wi_, bi_, lam_, rowmask)
        (a, _), vjp = jax.vjp(fn, xr, wa_ref[0], p_ref[5:6, :], wi_ref[0], p_ref[6:7, :], p_ref[7:8, :])
        gpre = g_ref[...]
        hsv = hs_ref[...]
        dhg_v = dhg_ref[...]
        dg_ref[...] = (dhg_v * hsv * _gelu_grad(gpre)).astype(dg_ref.dtype)
        row = lax.broadcasted_iota(jnp.int32, (tr, LANE), 0)
        d = dhg_v * _gelu(gpre) + jnp.where(row == tr - 1, gcar[0:1, :], 0.0)
        b = jnp.where(row < tr - 1, pltpu.roll(a, tr - 1, 0), 0.0)
        g = _scan_bwd(b, d)
        gcar[...] = jnp.broadcast_to(a[0:1, :] * g[0:1, :], (8, LANE))
        hlast = jnp.where(tt > 0, hp_ref[7:8, :], 0.0)
        hprev = jnp.where(row == 0, hlast, pltpu.roll(hsv, 1, 0))
        dxr, dwa, dba, dwi, dbi, dlam = vjp((g * hprev, g))
        dx, sums = _conv_bwd_parts(dxr, dnext[...], xv, ext, p_ref, tr)
        dx_ref[...] = dx.astype(dx_ref.dtype)
        dnext[...] = dxr[:8, :]
        dp_ref[...] += _rows_block(sums + [dba, dbi, dlam])
        dwa_ref[0] += dwa
        dwi_ref[0] += dwi

    tile = lambda off=0: pl.BlockSpec((tr, LANE), lambda n, ti: (nt - 1 - ti, off + n))
    halo = lambda off=0: pl.BlockSpec((8, LANE), lambda n, ti: (jnp.maximum((nt - 1 - ti) * r8 - 1, 0), off + n))
    par = pl.BlockSpec((8, LANE), lambda n, ti: (0, n))
    wspec = pl.BlockSpec((1, LANE, LANE), lambda n, ti: (n, 0, 0))
    return pl.pallas_call(
        body,
        out_shape=(_sds((t, LRU_WIDTH), BF), _sds((t, LRU_WIDTH), BF), _sds((8, LRU_WIDTH), F32),
                   _sds((LRU_BLOCKS, LANE, LANE), F32), _sds((LRU_BLOCKS, LANE, LANE), F32)),
        grid=(LRU_BLOCKS, nt),
        in_specs=[tile(), halo(), tile(LRU_BLOCKS), par, wspec, wspec, tile(), halo(), tile()],
        out_specs=(tile(), tile(), par, wspec, wspec),
        scratch_shapes=[pltpu.VMEM((8, LANE), F32), pltpu.VMEM((8, LANE), F32)],
        compiler_params=_cp(("parallel", "arbitrary")),
        name="rg_bwd",
    )(xg, xg, xg, rgp, w_a, w_i, hs, hs, dhg)


PACK_W = 1024
MESH_ID = pl.DeviceIdType.MESH
ANY = pl.BlockSpec(memory_space=pl.ANY)


def _my_place():
    x, y, c = lax.axis_index("x"), lax.axis_index("y"), lax.axis_index("c")
    return x, y, c


def _lin(px, py, pc):
    return 4 * px + 2 * py + pc


class Exchange:
    def __init__(self, kind, arrs):
        self.kind, self.arrs, self.k = kind, list(arrs), len(arrs)

    def out_shapes(self):
        if self.kind == "gather":
            return [_sds((N_DEV,) + a.shape, a.dtype) for a in self.arrs]
        return [_sds(a.shape, a.dtype) for a in self.arrs]

    def scratch(self):
        k = self.k
        return [pltpu.SemaphoreType.DMA((k, 7)), pltpu.SemaphoreType.DMA((k, 7)), pltpu.SemaphoreType.DMA((k,))]

    def phases(self, ins, outs, sems):
        return (self._gather if self.kind == "gather" else self._scatter)(ins, outs, *sems)

    def _gather(self, ins, outs, send_sems, recv_sems, local_sems):
        k = self.k
        x, y, c = _my_place()
        me, sibling = (x, y, c), (x, y, 1 - c)
        chips = [(1 - x, y), (x, 1 - y), (1 - x, 1 - y)]

        def copy(a, sem, block, to, from_input=False):
            slab = outs[a].at[_lin(*block)]
            return pltpu.make_async_remote_copy(
                src_ref=ins[a] if from_input else slab, dst_ref=slab,
                send_sem=send_sems.at[a, sem], recv_sem=recv_sems.at[a, sem],
                device_id=to, device_id_type=MESH_ID)

        def mine():
            return [pltpu.make_async_copy(ins[a], outs[a].at[_lin(*me)], local_sems.at[a]) for a in range(k)]

        def first():
            out = []
            for a in range(k):
                out.append(copy(a, 0, me, sibling, True))
                out += [copy(a, 1 + j, me, (*chip, c), True) for j, chip in enumerate(chips)]
            return out

        def passed():
            return [copy(a, 4 + j, (*chip, c), sibling) for j, chip in enumerate(chips) for a in range(k)]

        def start():
            for cp in mine() + first():
                cp.start()

        def middle():
            onward = passed()
            for j, chip in enumerate(chips):
                for a in range(k):
                    copy(a, 1 + j, (*chip, c), me).wait_recv()
                    onward[j * k + a].start()

        def finish():
            for a in range(k):
                copy(a, 0, sibling, me).wait_recv()
                for j, chip in enumerate(chips):
                    copy(a, 4 + j, (*chip, 1 - c), me).wait_recv()
            for cp in first() + passed():
                cp.wait_send()
            for cp in mine():
                cp.wait()

        return start, middle, finish

    def _scatter(self, ins, outs, send_sems, recv_sems, local_sems):
        k = self.k
        x, y, c = _my_place()
        me = _lin(x, y, c)
        peers = [((1 - x) if r & 4 else x, (1 - y) if r & 2 else y, (1 - c) if r & 1 else c) for r in range(1, N_DEV)]

        def copy(a, r, src_slab, dst_slab, to):
            return pltpu.make_async_remote_copy(
                src_ref=ins[a].at[src_slab], dst_ref=outs[a].at[dst_slab],
                send_sem=send_sems.at[a, r], recv_sem=recv_sems.at[a, r],
                device_id=to, device_id_type=MESH_ID)

        def mine():
            return [pltpu.make_async_copy(ins[a].at[me], outs[a].at[me], local_sems.at[a]) for a in range(k)]

        def sends():
            return [copy(a, r, _lin(*peer), me, peer) for r, peer in enumerate(peers) for a in range(k)]

        def start():
            for cp in mine() + sends():
                cp.start()

        def middle():
            pass

        def finish():
            for r, peer in enumerate(peers):
                for a in range(k):
                    copy(a, r, me, _lin(*peer), peer).wait_recv()
            for cp in sends():
                cp.wait_send()
            for cp in mine():
                cp.wait()

        return start, middle, finish

    def run(self, name):
        k = self.k

        def body(*refs):
            start, middle, finish = self.phases(refs[:k], refs[k:2 * k], refs[2 * k:])
            start()
            middle()
            finish()

        return pl.pallas_call(
            body,
            out_shape=tuple(self.out_shapes()),
            in_specs=[ANY] * k,
            out_specs=tuple(ANY for _ in range(k)),
            scratch_shapes=self.scratch(),
            name=name,
        )(*self.arrs)


def all_gather(arrs, name):
    return Exchange("gather", arrs).run(name)


def all_to_all(arrs, name):
    return Exchange("scatter", arrs).run(name)


def slab_sum(a, name):
    _, r, w = a.shape
    tr = _pick(r, (256, 128, 64, 32, 16, 8))

    def body(a_ref, o_ref):
        acc = a_ref[0].astype(F32)
        for d in range(1, N_DEV):
            acc = acc + a_ref[d].astype(F32)
        o_ref[...] = acc

    return pl.pallas_call(
        body,
        out_shape=_sds((r, w), F32),
        grid=(r // tr,),
        in_specs=[pl.BlockSpec((N_DEV, tr, w), lambda i: (0, i, 0))],
        out_specs=pl.BlockSpec((tr, w), lambda i: (i, 0)),
        compiler_params=_cp(("parallel",)),
        name=name,
    )(a)


def adamw(w, g, m, v, name):
    r, c = w.shape
    tr = _pick(r, (256, 160, 128, 64, 32, 16, 8))

    def body(w_ref, g_ref, m_ref, v_ref, d_ref, nm_ref, nv_ref):
        gv = g_ref[...]
        nm = ADAM_B1 * m_ref[...] + (1.0 - ADAM_B1) * gv
        nv = ADAM_B2 * v_ref[...] + (1.0 - ADAM_B2) * (gv * gv)
        m_hat = nm / (1.0 - ADAM_B1 ** ADAM_STEP)
        v_hat = nv / (1.0 - ADAM_B2 ** ADAM_STEP)
        d_ref[...] = -ADAM_LR * (m_hat / (jnp.sqrt(v_hat) + ADAM_EPS) + ADAM_WD * w_ref[...])
        nm_ref[...] = nm
        nv_ref[...] = nv

    spec = pl.BlockSpec((tr, c), lambda i: (i, 0))
    return pl.pallas_call(
        body,
        out_shape=tuple(_sds((r, c), F32) for _ in range(3)),
        grid=(r // tr,),
        in_specs=[spec] * 4,
        out_specs=(spec, spec, spec),
        compiler_params=_cp(("parallel",)),
        name=name,
    )(w, g, m, v)


def _relu2_epi(acc):
    r = jnp.maximum(acc, 0.0)
    return r * r, r


def _drelu2_epi(acc, r):
    return (acc * (2.0 * r.astype(F32)),)


def mlp_fwd(h, g_pre, g_post, w_up, w_down):
    hn = norm_fwd(h, g_pre, BF, name="mlp_norm")
    u, r = matmul(hn, w_up, "nn", (BF, BF), epi=_relu2_epi, name="mlp_up")
    d = matmul(u, w_down, "nn", name="mlp_down")
    return resadd_fwd(h, d, g_post, name="mlp_res"), (h, hn, u, r, d)


def mlp_bwd(res, dh2, g_pre, g_post, w_up, w_down):
    h, hn, u, r, d = res
    dd, dg_post = norm_bwd(d, g_post, dh2, mask_pad=True, out_dtype=BF, name="mlp_post_bwd")
    dw_down = matmul(u, dd, "tn", name="mlp_dwdown")
    dp = matmul(dd, w_down, "nt", (BF,), epi=_drelu2_epi, extras=(r,), name="mlp_du")
    dw_up = matmul(hn, dp, "tn", name="mlp_dwup")
    dhn = matmul(dp, w_up, "nt", name="mlp_dhn")
    dh, dg_pre = norm_bwd(h, g_pre, dhn, dres=dh2, name="mlp_pre_bwd")
    return dh, dict(mlp_pre_g=dg_pre, mlp_post_g=dg_post, w_up=dw_up, w_down=dw_down)


def rg_layer_fwd(h, g_pre, g_post, w_xy, rgp, w_a, w_i, w_out):
    hn = norm_fwd(h, g_pre, BF, name="rg_norm")
    xg = matmul(hn, w_xy, "nn", name="rg_in")
    hg, hs = rg_fwd(xg, rgp, w_a, w_i)
    m = matmul(hg, w_out, "nn", name="rg_out")
    return resadd_fwd(h, m, g_post, name="rg_res"), (h, hn, xg, hg, hs, m)


def rg_layer_bwd(res, dh2, g_pre, g_post, w_xy, rgp, w_a, w_i, w_out):
    h, hn, xg, hg, hs, m = res
    dm, dg_post = norm_bwd(m, g_post, dh2, mask_pad=True, out_dtype=BF, name="rg_post_bwd")
    dw_out = matmul(hg, dm, "tn", name="rg_dwout")
    dhg = matmul(dm, w_out, "nt", name="rg_dhg")
    dxr, dgp, drgp, dwa, dwi = rg_bwd(xg, rgp, w_a, w_i, hs, dhg)
    dxg = jnp.concatenate([dxr, dgp], axis=1)
    dw_xy = matmul(hn, dxg, "tn", name="rg_dwin")
    dhn = matmul(dxg, w_xy, "nt", name="rg_dhn")
    dh, dg_pre = norm_bwd(h, g_pre, dhn, dres=dh2, name="rg_pre_bwd")
    return dh, dict(mix_pre_g=dg_pre, mix_post_g=dg_post, rg_w_x=dw_xy[:, :LRU_WIDTH], rg_w_y=dw_xy[:, LRU_WIDTH:],
                    rg_conv_w=drgp[0:4], rg_conv_b=drgp[4], rg_b_a=drgp[5], rg_b_i=drgp[6], rg_lambda=drgp[7],
                    rg_w_a=dwa, rg_w_i=dwi, rg_w_out=dw_out)


def sm_layer_fwd(h, g_pre, g_post, w_in_p, convp, dt_bias, a_log, dskip, ssd_g, q_g, w_q_p, kv_g, w_kv_p, w_out, tabs,
                 carry=None):
    hn = norm_fwd(h, g_pre, BF, name="sm_norm")
    proj = matmul(hn, w_in_p, "nn", name="sm_in")
    xbc_c = conv_silu_fwd(proj, OFF_XBC // LANE, SSD_CONV_CH // LANE, convp, name="ssd_conv")
    y, hst = ssd_fwd(xbc_c, proj, dt_bias, a_log, dskip)
    y_ssd = gated_norm_fwd(y, proj, ssd_g)
    cqn = norm_fwd(proj, q_g, BF, col_blk=OFF_CQ // MLA_Q_RANK, width=MLA_Q_RANK, name="q_norm")
    q_raw = matmul(cqn, w_q_p, "nn", name="q_up")
    ckvn = norm_fwd(proj, kv_g, BF, col_blk=OFF_CKV // MLA_KV_RANK, width=MLA_KV_RANK, name="kv_norm")
    kv_raw = matmul(ckvn, w_kv_p, "nn", name="kv_up")
    q_cat, k_cat, v = rope_fwd(q_raw, kv_raw, proj, tabs)
    o, lse, carried = attn_fwd(q_cat, k_cat, v, carry)
    yab = jnp.concatenate([y_ssd, o], axis=1)
    m = matmul(yab, w_out, "nn", name="sm_out")
    res = (h, hn, proj, xbc_c, y, hst, cqn, ckvn, q_cat, k_cat, v, o, lse, yab, m)
    return resadd_fwd(h, m, g_post, name="sm_res"), res, carried


def sm_layer_bwd(res, dh2, g_pre, g_post, w_in_p, convp, dt_bias, a_log, dskip, ssd_g, q_g, w_q_p, kv_g, w_kv_p, w_out, tabs,
                 carry=None):
    h, hn, proj, xbc_c, y, hst, cqn, ckvn, q_cat, k_cat, v, o, lse, yab, m = res
    dm, dg_post = norm_bwd(m, g_post, dh2, mask_pad=True, out_dtype=BF, name="sm_post_bwd")
    dw_out = matmul(yab, dm, "tn", name="sm_dwout")
    dyab = matmul(dm, w_out, "nt", name="sm_dyab")
    dq_cat, dk_cat, dv, carried = attn_bwd(q_cat, k_cat, v, o, lse, dyab, carry)
    dq_raw, dkr = rope_bwd(dq_cat, dk_cat, tabs)
    dkv_raw = jnp.concatenate([dk_cat, dv], axis=1).astype(BF)
    dw_kv_p = matmul(ckvn, dkv_raw, "tn", name="kv_dw")
    dckvn = matmul(dkv_raw, w_kv_p, "nt", name="kv_dx")
    dckv, dg_kv = norm_bwd(proj, kv_g, dckvn, col_blk=OFF_CKV // MLA_KV_RANK, width=MLA_KV_RANK, name="kv_norm_bwd")
    dw_q_p = matmul(cqn, dq_raw, "tn", name="q_dw")
    dcqn = matmul(dq_raw, w_q_p, "nt", name="q_dx")
    dcq, dg_q = norm_bwd(proj, q_g, dcqn, col_blk=OFF_CQ // MLA_Q_RANK, width=MLA_Q_RANK, name="q_norm_bwd")
    dy, dz, dg_ssd = gated_norm_bwd(y, proj, ssd_g, dyab)
    dxbc_c, ddt, dpar = ssd_bwd(xbc_c, proj, dt_bias, a_log, dskip, hst, dy)
    dxbc, dconvp = conv_silu_bwd(proj, OFF_XBC // LANE, SSD_CONV_CH // LANE, convp, dxbc_c, name="ssd_conv_bwd")
    dproj = jnp.concatenate([dz, dxbc, dckv, ddt, dkr, dcq], axis=1).astype(BF)
    dw_in_p = matmul(hn, dproj, "tn", (BF,), name="sm_dwin")
    dhn = matmul(dproj, w_in_p, "nt", name="sm_dhn")
    dh, dg_pre = norm_bwd(h, g_pre, dhn, dres=dh2, name="sm_pre_bwd")
    grads = dict(mix_pre_g=dg_pre, mix_post_g=dg_post, w_in=w_in_cols_to_blocks(dw_in_p), ssd_conv_w=dconvp[0:4],
                 ssd_conv_b=dconvp[4], ssd_dt_bias=dpar[0, :SSD_HEADS], ssd_a_log=dpar[1, :SSD_HEADS],
                 ssd_d=dpar[2, :SSD_HEADS], ssd_norm_g=dg_ssd, mla_q_norm_g=dg_q, mla_w_q_up=_unpack_w_q(dw_q_p),
                 mla_kv_norm_g=dg_kv, mla_w_kv_up=_unpack_w_kv(dw_kv_p), w_out_ab=dw_out)
    return dh, grads, carried


W_IN_COLS = 3248
W_IN_SHARD = W_IN_COLS // N_DEV
W_IN_WIRE = 512


def _w_in_tables():
    src = np.full((IN_W,), -1, np.int64)
    src[0:2560] = np.arange(2560)
    src[OFF_CKV:OFF_CKV + 256] = 2960 + np.arange(256)
    src[OFF_DT:OFF_DT + 16] = 2560 + np.arange(16)
    src[OFF_KR + 64:OFF_KR + 96] = 3216 + np.arange(32)
    src[OFF_CQ:OFF_CQ + 384] = 2576 + np.arange(384)
    dev = np.where(src >= 0, src // W_IN_SHARD, -1).astype(np.int32).reshape(1, IN_W)
    col = np.where(src >= 0, src % W_IN_SHARD, 0).astype(np.int32).reshape(1, IN_W)
    return jnp.asarray(dev), jnp.asarray(col)


def w_in_blocks_to_cols(g8):
    _, k, wp = g8.shape
    tn = 384
    dev, col = _w_in_tables()

    def body(g_ref, dev_ref, col_ref, o_ref):
        row = lax.broadcasted_iota(jnp.int32, (wp, tn), 0)
        hit = row == col_ref[...]
        acc = jnp.zeros((k, tn), F32)
        for j in range(N_DEV):
            sel = (hit & (dev_ref[...] == j)).astype(BF)
            acc = acc + jnp.dot(g_ref[j], sel, preferred_element_type=F32)
        o_ref[...] = acc.astype(o_ref.dtype)

    return pl.pallas_call(
        body,
        out_shape=_sds((k, IN_W), BF),
        grid=(IN_W // tn,),
        in_specs=[pl.BlockSpec((N_DEV, k, wp), lambda i: (0, 0, 0)), pl.BlockSpec((1, tn), lambda i: (0, i)),
                  pl.BlockSpec((1, tn), lambda i: (0, i))],
        out_specs=pl.BlockSpec((k, tn), lambda i: (0, i)),
        compiler_params=_cp(("parallel",)),
        name="w_in_cols",
    )(g8, dev, col)


def w_in_cols_to_blocks(dw):
    k = dw.shape[0]
    dev, col = _w_in_tables()

    def body(dw_ref, dev_ref, col_ref, o_ref):
        j = pl.program_id(0)
        row = lax.broadcasted_iota(jnp.int32, (W_IN_WIRE, IN_W), 0)
        sel = ((row == col_ref[...]) & (dev_ref[...] == j)).astype(BF)
        o_ref[0] = lax.dot_general(dw_ref[...], sel, NT_DIMS, preferred_element_type=F32).astype(o_ref.dtype)

    return pl.pallas_call(
        body,
        out_shape=_sds((N_DEV, k, W_IN_WIRE), BF),
        grid=(N_DEV,),
        in_specs=[pl.BlockSpec((k, IN_W), lambda j: (0, 0)), pl.BlockSpec((1, IN_W), lambda j: (0, 0)),
                  pl.BlockSpec((1, IN_W), lambda j: (0, 0))],
        out_specs=pl.BlockSpec((1, k, W_IN_WIRE), lambda j: (j, 0, 0)),
        compiler_params=_cp(("parallel",)),
        name="w_in_blocks",
    )(dw, dev, col)


def _pack_w_q(w):
    w3 = w.reshape(w.shape[0], MLA_HEADS, MLA_NOPE + MLA_ROPE)
    return jnp.pad(w3, ((0, 0), (0, 0), (0, LANE - MLA_NOPE - MLA_ROPE))).reshape(w.shape[0], MLA_HEADS * LANE)


def _unpack_w_q(p):
    return p.reshape(p.shape[0], MLA_HEADS, LANE)[:, :, :MLA_NOPE + MLA_ROPE].reshape(p.shape[0], -1)


def _pack_w_kv(w):
    w3 = w.reshape(w.shape[0], MLA_HEADS, MLA_NOPE + MLA_V)
    k = jnp.pad(w3[:, :, :MLA_NOPE], ((0, 0), (0, 0), (0, LANE - MLA_NOPE))).reshape(w.shape[0], MLA_HEADS * LANE)
    return jnp.concatenate([k, w3[:, :, MLA_NOPE:].reshape(w.shape[0], MLA_HEADS * MLA_V)], axis=1)


def _unpack_w_kv(p):
    k = p[:, :MLA_HEADS * LANE].reshape(p.shape[0], MLA_HEADS, LANE)[:, :, :MLA_NOPE]
    v = p[:, MLA_HEADS * LANE:].reshape(p.shape[0], MLA_HEADS, MLA_V)
    return jnp.concatenate([k, v], axis=2).reshape(p.shape[0], -1)


def _rows8(rows, width):
    a = jnp.concatenate([r.reshape(-1, width) for r in rows], axis=0)
    return jnp.pad(a, ((0, 8 - a.shape[0]), (0, 0)))


SLAB_ROWS = 256


def _to_slab(flat_list, lead=()):
    cat = jnp.concatenate(flat_list, axis=-1)
    n = cat.shape[-1]
    unit = SLAB_ROWS * PACK_W
    total = -(-n // unit) * unit
    cat = jnp.pad(cat, [(0, 0)] * len(lead) + [(0, total - n)])
    return cat.reshape(lead + (total // PACK_W, PACK_W))


def _from_flat(flat, shapes):
    out, off = [], 0
    for s in shapes:
        n = int(np.prod(s))
        out.append(flat[off:off + n].reshape(s))
        off += n
    return out


def _gathered_full(g8, axis):
    moved = jnp.moveaxis(g8, 0, axis)
    shp = moved.shape
    return moved.reshape(shp[:axis] + (shp[axis] * shp[axis + 1],) + shp[axis + 2:])


def _per_device(full, axis):
    shp = full.shape
    split = full.reshape(shp[:axis] + (N_DEV, shp[axis] // N_DEV) + shp[axis + 1:])
    return jnp.moveaxis(split, axis, 0)


ARG_NAMES = ['x', 'meta_tokens', 'mix_pre_g', 'mix_post_g', 'mlp_pre_g', 'mlp_post_g', 'w_up', 'w_down', 'w_in',
             'ssd_conv_w', 'ssd_conv_b', 'ssd_dt_bias', 'ssd_a_log', 'ssd_d', 'ssd_norm_g', 'mla_q_norm_g',
             'mla_w_q_up', 'mla_kv_norm_g', 'mla_w_kv_up', 'w_out_ab', 'rg_w_x', 'rg_w_y', 'rg_conv_w', 'rg_conv_b',
             'rg_w_a', 'rg_b_a', 'rg_w_i', 'rg_b_i', 'rg_lambda', 'rg_w_out']
WEIGHTS = ARG_NAMES[1:]
BIG = {'w_up': 2, 'w_down': 1, 'w_in': 2, 'mla_w_q_up': 2, 'mla_w_kv_up': 2, 'w_out_ab': 1, 'rg_w_x': 2,
       'rg_w_y': 2, 'rg_w_out': 1}
SMALL = {'meta_tokens': 1, 'ssd_conv_w': 2, 'rg_conv_w': 2, 'rg_conv_b': 1, 'rg_b_a': 1, 'rg_b_i': 1, 'rg_lambda': 1}
REPL = [n for n in WEIGHTS if n not in BIG and n not in SMALL]


def _piece_axes():
    axes = {}
    for n, ax in BIG.items():
        for i in range(DEPTH if n in ('w_up', 'w_down') else DEPTH // 2):
            axes[(n, i)] = ax - 1
    return axes


PIECE_AXIS = _piece_axes()
_SM = lambda i: [(n, i) for n in ('w_in', 'mla_w_q_up', 'mla_w_kv_up', 'w_out_ab')]
_RG = lambda i: [(n, i) for n in ('rg_w_x', 'rg_w_y', 'rg_w_out')]
_MLP = lambda l: [('w_up', l), ('w_down', l)]
GATHER_FIRST = _SM(0)
GATHER_REST = [k for k in PIECE_AXIS if k not in GATHER_FIRST]
SCATTER_AT = {2: _MLP(3) + _RG(1) + _MLP(2), 0: _SM(1) + _MLP(1) + _RG(0) + _MLP(0)}
SCATTER_LAST = _SM(0)


def _wire_block(p, key):
    n, i = key
    blk = p[n][i]
    if n == 'w_in':
        blk = jnp.pad(blk, ((0, 0), (0, W_IN_WIRE - blk.shape[1])))
    return blk


def _step(p, moments):
    assert DEPTH == 4
    wire_shape = {k: _wire_block(p, k).shape for k in PIECE_AXIS}
    full = {n: [None] * p[n].shape[0] for n in BIG}
    full['w_in_g'] = [None] * p['w_in'].shape[0]

    def weight_slab(group):
        return _to_slab([_wire_block(p, k).astype(BF).reshape(-1) for k in group])

    def take_weights(group, g8):
        for (n, i), piece in zip(group, _from_flat_rows(g8, [wire_shape[k] for k in group])):
            if n == 'w_in':
                full['w_in_g'][i] = piece
            else:
                full[n][i] = _gathered_full(piece, PIECE_AXIS[(n, i)])

    def grad_slab(group, gw):
        parts = []
        for k in group:
            g = gw[k] if k[0] == 'w_in' else _per_device(gw[k], PIECE_AXIS[k]).astype(BF)
            parts.append(g.reshape(N_DEV, -1))
        return _to_slab(parts, lead=(N_DEV,))

    g_blocks = {}

    def take_grads(group, flat):
        for k, blk in zip(group, _from_flat(flat, [wire_shape[k] for k in group])):
            g_blocks[k] = blk[:, :W_IN_SHARD] if k[0] == 'w_in' else blk

    small_slab = _to_slab([p[n].reshape(-1) for n in SMALL])
    first8, small8 = all_gather([weight_slab(GATHER_FIRST), small_slab], name="gather_first")
    take_weights(GATHER_FIRST, first8)
    for n, piece in zip(SMALL, _from_flat_rows(small8, [p[n].shape for n in SMALL])):
        full[n] = _gathered_full(piece, SMALL[n])
    for n in REPL:
        full[n] = p[n]
    loss_local, grad_x, gw, gsmall_full, carried = _local_step(
        full, p['x'][0], p['loss_target'][0],
        fwd_carry=Exchange("gather", [weight_slab(GATHER_REST)]),
        on_fwd_carried=lambda got: take_weights(GATHER_REST, got[0]),
        bwd_carry=lambda layer, gw_now: Exchange("scatter", [grad_slab(SCATTER_AT[layer], gw_now)]))

    for layer, group in SCATTER_AT.items():
        take_grads(group, slab_sum(carried[layer][0], name="sum_l%d" % layer).reshape(-1))
    rep_flat = jnp.concatenate([gsmall_full[n].reshape(-1) for n in REPL])
    rep_n = rep_flat.shape[0]
    rep_chunk = -(-rep_n // (N_DEV * PACK_W * 8)) * PACK_W * 8
    rep8 = jnp.pad(rep_flat, (0, N_DEV * rep_chunk - rep_n)).reshape(N_DEV, rep_chunk)
    gsmall = _to_slab([_per_device(gsmall_full[n], SMALL[n]).reshape(N_DEV, -1) for n in SMALL] + [rep8], lead=(N_DEV,))
    rlast, rsmall = all_to_all([grad_slab(SCATTER_LAST, gw), gsmall], name="scatter_last")
    take_grads(SCATTER_LAST, slab_sum(rlast, name="sum_last").reshape(-1))
    ssmall = slab_sum(rsmall, name="sum_small").reshape(-1)
    g_loc = {n: jnp.stack([g_blocks[(n, i)] for i in range(p[n].shape[0])], axis=0) for n in BIG}
    small_n = sum(int(np.prod(p[n].shape)) for n in SMALL)
    g_loc.update(zip(SMALL, _from_flat(ssmall, [p[n].shape for n in SMALL])))
    rep_mine = ssmall[small_n:small_n + rep_chunk].reshape(-1, PACK_W)
    (rep_all,) = all_gather([rep_mine], name="gather_replicated")
    g_loc.update(zip(REPL, _from_flat(rep_all.reshape(-1), [p[n].shape for n in REPL])))

    out = {'loss': lax.psum(loss_local, ("x", "y", "c")), 'grad_x': grad_x[None]}
    small_names = list(SMALL) + REPL
    for n in BIG:
        shp = p[n].shape
        v2 = lambda a: a.reshape(-1, shp[-1])
        d, nm, nv = adamw(v2(p[n]), v2(g_loc[n]), v2(moments['m_' + n]), v2(moments['v_' + n]), name="adamw_" + n)
        out['delta_' + n], out['new_m_' + n], out['new_v_' + n] = d.reshape(shp), nm.reshape(shp), nv.reshape(shp)
    slab = lambda src: _to_slab([src(n).reshape(-1) for n in small_names])
    d, nm, nv = adamw(slab(lambda n: p[n]), slab(lambda n: g_loc[n]), slab(lambda n: moments['m_' + n]),
                      slab(lambda n: moments['v_' + n]), name="adamw_small")
    shapes = [p[n].shape for n in small_names]
    for key, flat in (('delta_', d), ('new_m_', nm), ('new_v_', nv)):
        for n, a in zip(small_names, _from_flat(flat.reshape(-1), shapes)):
            out[key + n] = a
    for n in WEIGHTS:
        out['grad_' + n] = g_loc[n]
    return out


def _local_step(full, x, target_rows, fwd_carry=None, on_fwd_carried=None, bwd_carry=None):
    t = PAD + N_META + x.shape[0]
    h = jnp.concatenate([jnp.zeros((PAD, D_MODEL), F32), full['meta_tokens'], x], axis=0)
    target = jnp.concatenate([jnp.zeros((PAD + N_META, D_MODEL), F32), target_rows], axis=0)
    tabs = rope_tables(t)

    def layer_args(layer):
        i = layer // 2
        if layer % 2 == 0:
            convp = _rows8([full['ssd_conv_w'][i], full['ssd_conv_b'][i]], SSD_CONV_CH)
            return (full['mix_pre_g'][layer], full['mix_post_g'][layer], w_in_blocks_to_cols(full['w_in_g'][i]), convp,
                    full['ssd_dt_bias'][i], full['ssd_a_log'][i], full['ssd_d'][i], full['ssd_norm_g'][i],
                    full['mla_q_norm_g'][i], _pack_w_q(full['mla_w_q_up'][i]), full['mla_kv_norm_g'][i],
                    _pack_w_kv(full['mla_w_kv_up'][i]), full['w_out_ab'][i], tabs)
        rgp = _rows8([full['rg_conv_w'][i], full['rg_conv_b'][i], full['rg_b_a'][i], full['rg_b_i'][i],
                      full['rg_lambda'][i]], LRU_WIDTH)
        w_xy = jnp.concatenate([full['rg_w_x'][i], full['rg_w_y'][i]], axis=1)
        return (full['mix_pre_g'][layer], full['mix_post_g'][layer], w_xy, rgp, full['rg_w_a'][i], full['rg_w_i'][i],
                full['rg_w_out'][i])

    def mlp_args(layer):
        return (full['mlp_pre_g'][layer], full['mlp_post_g'][layer], full['w_up'][layer], full['w_down'][layer])

    saved = []
    for layer in range(DEPTH):
        la = layer_args(layer)
        if layer % 2 == 0:
            h, res_mix, got = sm_layer_fwd(h, *la, carry=fwd_carry if layer == 0 else None)
            if layer == 0 and fwd_carry is not None:
                on_fwd_carried(got)
        else:
            h, res_mix = rg_layer_fwd(h, *la)
        ma = mlp_args(layer)
        h, res_mlp = mlp_fwd(h, *ma)
        saved.append((la, ma, res_mix, res_mlp))
    loss_local, dh = loss_fwd_bwd(h, target)

    others = {n: [None] * len(full[n]) for n in WEIGHTS if n not in BIG and n != 'meta_tokens'}
    gw, carried = {}, {}
    for layer in reversed(range(DEPTH)):
        la, ma, res_mix, res_mlp = saved[layer]
        dh, gm = mlp_bwd(res_mlp, dh, *ma)
        if layer % 2 == 0:
            for n in ('w_up', 'w_down'):
                gw[(n, layer)] = gm[n]
            carry = bwd_carry(layer, gw) if bwd_carry is not None else None
            dh, gx, carried[layer] = sm_layer_bwd(res_mix, dh, *la, carry=carry)
        else:
            dh, gx = rg_layer_bwd(res_mix, dh, *la)
        for n, g in list(gm.items()) + list(gx.items()):
            i = layer if n in ('mix_pre_g', 'mix_post_g', 'mlp_pre_g', 'mlp_post_g', 'w_up', 'w_down') else layer // 2
            if n in BIG:
                gw[(n, i)] = g
            else:
                others[n][i] = g
    gothers = {n: jnp.stack(v, axis=0) for n, v in others.items()}
    gothers['meta_tokens'] = dh[PAD:PAD + N_META]
    return loss_local, dh[PAD + N_META:], gw, gothers, carried


def _from_flat_rows(g8, shapes):
    flat = g8.reshape(N_DEV, -1)
    out, off = [], 0
    for s in shapes:
        n = int(np.prod(s))
        out.append(flat[:, off:off + n].reshape((N_DEV,) + tuple(s)))
        off += n
    return out


def kernel(x, meta_tokens, mix_pre_g, mix_post_g, mlp_pre_g, mlp_post_g, w_up, w_down, w_in, ssd_conv_w, ssd_conv_b, ssd_dt_bias, ssd_a_log, ssd_d, ssd_norm_g, mla_q_norm_g, mla_w_q_up, mla_kv_norm_g, mla_w_kv_up, w_out_ab, rg_w_x, rg_w_y, rg_conv_w, rg_conv_b, rg_w_a, rg_b_a, rg_w_i, rg_b_i, rg_lambda, rg_w_out, loss_target, m_meta_tokens, m_mix_pre_g, m_mix_post_g, m_mlp_pre_g, m_mlp_post_g, m_w_up, m_w_down, m_w_in, m_ssd_conv_w, m_ssd_conv_b, m_ssd_dt_bias, m_ssd_a_log, m_ssd_d, m_ssd_norm_g, m_mla_q_norm_g, m_mla_w_q_up, m_mla_kv_norm_g, m_mla_w_kv_up, m_w_out_ab, m_rg_w_x, m_rg_w_y, m_rg_conv_w, m_rg_conv_b, m_rg_w_a, m_rg_b_a, m_rg_w_i, m_rg_b_i, m_rg_lambda, m_rg_w_out, v_meta_tokens, v_mix_pre_g, v_mix_post_g, v_mlp_pre_g, v_mlp_post_g, v_w_up, v_w_down, v_w_in, v_ssd_conv_w, v_ssd_conv_b, v_ssd_dt_bias, v_ssd_a_log, v_ssd_d, v_ssd_norm_g, v_mla_q_norm_g, v_mla_w_q_up, v_mla_kv_norm_g, v_mla_w_kv_up, v_w_out_ab, v_rg_w_x, v_rg_w_y, v_rg_conv_w, v_rg_conv_b, v_rg_w_a, v_rg_b_a, v_rg_w_i, v_rg_b_i, v_rg_lambda, v_rg_w_out):
    args = (x, meta_tokens, mix_pre_g, mix_post_g, mlp_pre_g, mlp_post_g, w_up, w_down, w_in, ssd_conv_w, ssd_conv_b, ssd_dt_bias, ssd_a_log, ssd_d, ssd_norm_g, mla_q_norm_g, mla_w_q_up, mla_kv_norm_g, mla_w_kv_up, w_out_ab, rg_w_x, rg_w_y, rg_conv_w, rg_conv_b, rg_w_a, rg_b_a, rg_w_i, rg_b_i, rg_lambda, rg_w_out, loss_target, m_meta_tokens, m_mix_pre_g, m_mix_post_g, m_mlp_pre_g, m_mlp_post_g, m_w_up, m_w_down, m_w_in, m_ssd_conv_w, m_ssd_conv_b, m_ssd_dt_bias, m_ssd_a_log, m_ssd_d, m_ssd_norm_g, m_mla_q_norm_g, m_mla_w_q_up, m_mla_kv_norm_g, m_mla_w_kv_up, m_w_out_ab, m_rg_w_x, m_rg_w_y, m_rg_conv_w, m_rg_conv_b, m_rg_w_a, m_rg_b_a, m_rg_w_i, m_rg_b_i, m_rg_lambda, m_rg_w_out, v_meta_tokens, v_mix_pre_g, v_mix_post_g, v_mlp_pre_g, v_mlp_post_g, v_w_up, v_w_down, v_w_in, v_ssd_conv_w, v_ssd_conv_b, v_ssd_dt_bias, v_ssd_a_log, v_ssd_d, v_ssd_norm_g, v_mla_q_norm_g, v_mla_w_q_up, v_mla_kv_norm_g, v_mla_w_kv_up, v_w_out_ab, v_rg_w_x, v_rg_w_y, v_rg_conv_w, v_rg_conv_b, v_rg_w_a, v_rg_b_a, v_rg_w_i, v_rg_b_i, v_rg_lambda, v_rg_w_out,)
    n_w = len(ARG_NAMES)
    p = dict(zip(ARG_NAMES, args[:n_w]))
    p['loss_target'] = args[n_w]
    moments = {}
    for i, n in enumerate(WEIGHTS):
        moments['m_' + n] = args[n_w + 1 + i]
        moments['v_' + n] = args[n_w + 1 + len(WEIGHTS) + i]
    out = _step(p, moments)
    res = [out['loss'], out['grad_x']]
    for prefix in ('grad_', 'delta_', 'new_m_', 'new_v_'):
        res += [out[prefix + n] for n in WEIGHTS]
    return tuple(res)
```

```python
import functools
import math

import numpy as np
import jax
import jax.numpy as jnp
from jax import lax
from jax.experimental import pallas as pl
from jax.experimental.pallas import tpu as pltpu

F32 = jnp.float32
BF = jnp.bfloat16
HI = lax.Precision.HIGHEST

D_MODEL = 1024
DEPTH = 4
N_META = 16
CHUNK = 128
PAD = CHUNK - N_META
EPS = 1e-6
SSD_HEADS = 16
SSD_HEAD_DIM = 64
SSD_D_INNER = 1024
SSD_STATE = 128
SSD_CONV_CH = 1536
MLA_HEADS = 16
MLA_NOPE = 64
MLA_ROPE = 32
MLA_V = 64
MLA_Q_RANK = 384
MLA_KV_RANK = 256
ROPE_BASE = 10000.0
LRU_WIDTH = 1280
LRU_BLOCKS = 10
LRU_C = 8.0
D_FF = 4096
N_DEV = 8
LANE = 128
IN_W = 3456
OFF_Z, OFF_XBC, OFF_CKV, OFF_DT, OFF_KR, OFF_CQ = 0, 1024, 2560, 2816, 2944, 3072

ADAM_LR = 0.001
ADAM_B1 = 0.9
ADAM_B2 = 0.999
ADAM_EPS = 1e-08
ADAM_WD = 0.01
ADAM_STEP = 10

VMEM_LIMIT = 56 * 1024 * 1024
NEG = -1e30


def _pick(n, cands):
    for c in cands:
        if n % c == 0:
            return c
    return n


def _cp(sem=None):
    return pltpu.CompilerParams(dimension_semantics=sem, vmem_limit_bytes=VMEM_LIMIT)


def _sds(shape, dtype):
    return jax.ShapeDtypeStruct(tuple(shape), dtype)


def _silu(x):
    return x * jax.nn.sigmoid(x)


def _softplus(x):
    return jnp.maximum(x, 0.0) + jnp.log(1.0 + jnp.exp(-jnp.abs(x)))


def _gelu(x):
    c = math.sqrt(2.0 / math.pi)
    return 0.5 * x * (1.0 + jnp.tanh(c * (x + 0.044715 * (x * x * x))))


def _row_mask(i, tr, shape, first_valid=PAD):
    row = i * tr + lax.broadcasted_iota(jnp.int32, shape, 0)
    return row >= first_valid


def matmul(a, b, mode, out_dtypes=(F32,), epi=None, extras=(), name="mm", tm=None, tn=None):
    if mode == "nn":
        (m, k), (k2, n) = a.shape, b.shape
    elif mode == "nt":
        (m, k), (n, k2) = a.shape, b.shape
    else:
        (k, m), (k2, n) = a.shape, b.shape
    assert k == k2, (a.shape, b.shape, mode)
    tm = tm or _pick(m, (1056, 1024, 768, 640, 512, 384, 256, 128))
    tn = tn or _pick(n, (512, 640, 384, 256, 128))
    if mode == "tn":
        tm = _pick(m, (512, 384, 256, 128))
    dims = {"nn": (((1,), (0,)), ((), ())), "nt": (((1,), (1,)), ((), ())), "tn": (((0,), (0,)), ((), ()))}[mode]
    n_ex = len(extras)

    def body(a_ref, b_ref, *rest):
        ex_refs, out_refs = rest[:n_ex], rest[n_ex:]
        acc = lax.dot_general(a_ref[...].astype(BF), b_ref[...].astype(BF), dims, preferred_element_type=F32)
        outs = (acc,) if epi is None else epi(acc, *[r[...] for r in ex_refs])
        for r, o in zip(out_refs, outs):
            r[...] = o.astype(r.dtype)

    a_spec = pl.BlockSpec((k, tm), lambda i, j: (0, i)) if mode == "tn" else pl.BlockSpec((tm, k), lambda i, j: (i, 0))
    b_spec = pl.BlockSpec((tn, k), lambda i, j: (j, 0)) if mode == "nt" else pl.BlockSpec((k, tn), lambda i, j: (0, j))
    o_spec = pl.BlockSpec((tm, tn), lambda i, j: (i, j))
    outs = pl.pallas_call(
        body,
        out_shape=tuple(_sds((m, n), dt) for dt in out_dtypes),
        grid=(m // tm, n // tn),
        in_specs=[a_spec, b_spec] + [o_spec] * n_ex,
        out_specs=tuple(o_spec for _ in out_dtypes),
        compiler_params=_cp(("parallel", "parallel")),
        name=name,
    )(a, b, *extras)
    return outs[0] if len(out_dtypes) == 1 else outs


def _rt(t):
    return _pick(t, (384, 256, 128))


def norm_fwd(x, g, out_dtype, col_blk=0, width=None, name="norm_fwd"):
    t = x.shape[0]
    w = width or x.shape[1]
    tr = _rt(t)

    def body(x_ref, g_ref, o_ref):
        xv = x_ref[...]
        r = lax.rsqrt(jnp.mean(xv * xv, axis=-1, keepdims=True) + EPS)
        o_ref[...] = (xv * r * g_ref[...]).astype(o_ref.dtype)

    return pl.pallas_call(
        body,
        out_shape=_sds((t, w), out_dtype),
        grid=(t // tr,),
        in_specs=[pl.BlockSpec((tr, w), lambda i: (i, col_blk)), pl.BlockSpec((1, w), lambda i: (0, 0))],
        out_specs=pl.BlockSpec((tr, w), lambda i: (i, 0)),
        compiler_params=_cp(("parallel",)),
        name=name,
    )(x, g.reshape(1, w))


def norm_bwd(x, g, dy, dres=None, mask_pad=False, out_dtype=F32, col_blk=0, width=None, dy_col_blk=0, name="norm_bwd"):
    t = x.shape[0]
    w = width or x.shape[1]
    tr = _rt(t)
    has_res = dres is not None

    def body(x_ref, g_ref, dy_ref, *rest):
        if has_res:
            res_ref, dx_ref, dg_ref = rest
        else:
            dx_ref, dg_ref = rest
        i = pl.program_id(0)
        xv = x_ref[...]
        dyv = dy_ref[...].astype(F32)
        if mask_pad:
            dyv = jnp.where(_row_mask(i, tr, dyv.shape), dyv, 0.0)
        r = lax.rsqrt(jnp.mean(xv * xv, axis=-1, keepdims=True) + EPS)
        xh = xv * r
        dyg = dyv * g_ref[...]
        dx = r * (dyg - xh * jnp.mean(dyg * xh, axis=-1, keepdims=True))
        if has_res:
            dx = dx + res_ref[...]
        dx_ref[...] = dx.astype(dx_ref.dtype)

        @pl.when(i == 0)
        def _():
            dg_ref[...] = jnp.zeros_like(dg_ref)

        dg_ref[...] += jnp.sum(dyv * xh, axis=0, keepdims=True)

    in_specs = [pl.BlockSpec((tr, w), lambda i: (i, col_blk)), pl.BlockSpec((1, w), lambda i: (0, 0)),
                pl.BlockSpec((tr, w), lambda i: (i, dy_col_blk))]
    args = [x, g.reshape(1, w), dy]
    if has_res:
        in_specs.append(pl.BlockSpec((tr, w), lambda i: (i, 0)))
        args.append(dres)
    dx, dg = pl.pallas_call(
        body,
        out_shape=(_sds((t, w), out_dtype), _sds((1, w), F32)),
        grid=(t // tr,),
        in_specs=in_specs,
        out_specs=(pl.BlockSpec((tr, w), lambda i: (i, 0)), pl.BlockSpec((1, w), lambda i: (0, 0))),
        compiler_params=_cp(("arbitrary",)),
        name=name,
    )(*args)
    return dx, dg.reshape(w)


def resadd_fwd(h, m, g, name="resadd"):
    t, w = h.shape
    tr = _rt(t)

    def body(h_ref, m_ref, g_ref, o_ref):
        mv = m_ref[...]
        r = lax.rsqrt(jnp.mean(mv * mv, axis=-1, keepdims=True) + EPS)
        y = mv * r * g_ref[...]
        o_ref[...] = h_ref[...] + jnp.where(_row_mask(pl.program_id(0), tr, y.shape), y, 0.0)

    return pl.pallas_call(
        body,
        out_shape=_sds((t, w), F32),
        grid=(t // tr,),
        in_specs=[pl.BlockSpec((tr, w), lambda i: (i, 0)), pl.BlockSpec((tr, w), lambda i: (i, 0)),
                  pl.BlockSpec((1, w), lambda i: (0, 0))],
        out_specs=pl.BlockSpec((tr, w), lambda i: (i, 0)),
        compiler_params=_cp(("parallel",)),
        name=name,
    )(h, m, g.reshape(1, w))


def loss_fwd_bwd(h, target):
    t, w = h.shape
    tr = _rt(t)

    def body(h_ref, t_ref, s_ref, dh_ref):
        i = pl.program_id(0)
        err = h_ref[...] - t_ref[...]
        err = jnp.where(_row_mask(i, tr, err.shape, PAD + N_META), err, 0.0)
        dh_ref[...] = err * (1.0 / w)

        @pl.when(i == 0)
        def _():
            s_ref[...] = jnp.zeros_like(s_ref)

        s_ref[...] += jnp.sum(err * err).reshape(1, 1)

    s, dh = pl.pallas_call(
        body,
        out_shape=(_sds((1, LANE), F32), _sds((t, w), F32)),
        grid=(t // tr,),
        in_specs=[pl.BlockSpec((tr, w), lambda i: (i, 0)), pl.BlockSpec((tr, w), lambda i: (i, 0))],
        out_specs=(pl.BlockSpec((1, LANE), lambda i: (0, 0)), pl.BlockSpec((tr, w), lambda i: (i, 0))),
        compiler_params=_cp(("arbitrary",)),
        name="loss",
    )(h, target)
    return 0.5 * s[0, 0] / w, dh


def _shift_down(ext, k, n):
    return pltpu.roll(ext, k, 0)[8:]


def _conv_pre(ext, x, w_ref, n):
    return (w_ref[4:5, :] + w_ref[3:4, :] * x + w_ref[2:3, :] * _shift_down(ext, 1, n)
            + w_ref[1:2, :] * _shift_down(ext, 2, n) + w_ref[0:1, :] * _shift_down(ext, 3, n))


def _conv_bwd_parts(dpre, dnext, x, ext, w_ref, n):
    extd = jnp.concatenate([dpre, dnext], axis=0)
    ln = n + 8
    dx = (w_ref[3:4, :] * dpre + w_ref[2:3, :] * pltpu.roll(extd, ln - 1, 0)[:n]
          + w_ref[1:2, :] * pltpu.roll(extd, ln - 2, 0)[:n] + w_ref[0:1, :] * pltpu.roll(extd, ln - 3, 0)[:n])
    sums = [jnp.sum(dpre * _shift_down(ext, 3, n), axis=0, keepdims=True),
            jnp.sum(dpre * _shift_down(ext, 2, n), axis=0, keepdims=True),
            jnp.sum(dpre * _shift_down(ext, 1, n), axis=0, keepdims=True),
            jnp.sum(dpre * x, axis=0, keepdims=True),
            jnp.sum(dpre, axis=0, keepdims=True)]
    return dx, sums


def _rows_block(sums):
    row = lax.broadcasted_iota(jnp.int32, (8, LANE), 0)
    out = jnp.zeros((8, LANE), F32)
    for k, s in enumerate(sums):
        out = jnp.where(row == k, s, out)
    return out


def conv_silu_fwd(x, col0_blk, nblk, wb, name="conv_fwd"):
    t = x.shape[0]
    c = nblk * LANE
    tr = _rt(t)

    def body(x_ref, w_ref, o_ref, prev):
        ti = pl.program_id(1)

        @pl.when(ti == 0)
        def _():
            prev[...] = jnp.zeros_like(prev)

        xv = x_ref[...]
        ext = jnp.concatenate([prev[...], xv], axis=0)
        o_ref[...] = _silu(_conv_pre(ext, xv, w_ref, tr))
        prev[...] = xv[tr - 8:, :]

    return pl.pallas_call(
        body,
        out_shape=_sds((t, c), F32),
        grid=(nblk, t // tr),
        in_specs=[pl.BlockSpec((tr, LANE), lambda cb, ti: (ti, col0_blk + cb)),
                  pl.BlockSpec((8, LANE), lambda cb, ti: (0, cb))],
        out_specs=pl.BlockSpec((tr, LANE), lambda cb, ti: (ti, cb)),
        scratch_shapes=[pltpu.VMEM((8, LANE), F32)],
        compiler_params=_cp(("parallel", "arbitrary")),
        name=name,
    )(x, wb)


def conv_silu_bwd(x, col0_blk, nblk, wb, dout, name="conv_bwd"):
    t = x.shape[0]
    c = nblk * LANE
    tr = _rt(t)
    nt = t // tr
    r8 = tr // 8

    def body(x_ref, xp_ref, w_ref, do_ref, dx_ref, dwb_ref, dnext):
        ti = pl.program_id(1)
        tt = nt - 1 - ti

        @pl.when(ti == 0)
        def _():
            dnext[...] = jnp.zeros_like(dnext)
            dwb_ref[...] = jnp.zeros_like(dwb_ref)

        xv = x_ref[...]
        halo = jnp.where(tt > 0, xp_ref[...], 0.0)
        ext = jnp.concatenate([halo, xv], axis=0)
        pre = _conv_pre(ext, xv, w_ref, tr)
        s = jax.nn.sigmoid(pre)
        dpre = do_ref[...] * (s + pre * s * (1.0 - s))
        dx, sums = _conv_bwd_parts(dpre, dnext[...], xv, ext, w_ref, tr)
        dx_ref[...] = dx
        dwb_ref[...] += _rows_block(sums)
        dnext[...] = dpre[:8, :]

    return pl.pallas_call(
        body,
        out_shape=(_sds((t, c), F32), _sds((8, c), F32)),
        grid=(nblk, nt),
        in_specs=[pl.BlockSpec((tr, LANE), lambda cb, ti: (nt - 1 - ti, col0_blk + cb)),
                  pl.BlockSpec((8, LANE), lambda cb, ti: (jnp.maximum((nt - 1 - ti) * r8 - 1, 0), col0_blk + cb)),
                  pl.BlockSpec((8, LANE), lambda cb, ti: (0, cb)),
                  pl.BlockSpec((tr, LANE), lambda cb, ti: (nt - 1 - ti, cb))],
        out_specs=(pl.BlockSpec((tr, LANE), lambda cb, ti: (nt - 1 - ti, cb)),
                   pl.BlockSpec((8, LANE), lambda cb, ti: (0, cb))),
        scratch_shapes=[pltpu.VMEM((8, LANE), F32)],
        compiler_params=_cp(("parallel", "arbitrary")),
        name=name,
    )(x, x, wb, dout)


def gated_norm_fwd(y, proj, g, name="gnorm_fwd"):
    t, w = y.shape
    tr = _rt(t)

    def body(y_ref, z_ref, g_ref, o_ref):
        v = y_ref[...] * _silu(z_ref[...])
        r = lax.rsqrt(jnp.mean(v * v, axis=-1, keepdims=True) + EPS)
        o_ref[...] = (v * r * g_ref[...]).astype(o_ref.dtype)

    return pl.pallas_call(
        body,
        out_shape=_sds((t, w), BF),
        grid=(t // tr,),
        in_specs=[pl.BlockSpec((tr, w), lambda i: (i, 0)), pl.BlockSpec((tr, w), lambda i: (i, OFF_Z // w)),
                  pl.BlockSpec((1, w), lambda i: (0, 0))],
        out_specs=pl.BlockSpec((tr, w), lambda i: (i, 0)),
        compiler_params=_cp(("parallel",)),
        name=name,
    )(y, proj, g.reshape(1, w))


def gated_norm_bwd(y, proj, g, dyab, name="gnorm_bwd"):
    t, w = y.shape
    tr = _rt(t)

    def body(y_ref, z_ref, g_ref, do_ref, dy_ref, dz_ref, dg_ref):
        i = pl.program_id(0)
        yv, zv, dov = y_ref[...], z_ref[...], do_ref[...]
        s = jax.nn.sigmoid(zv)
        sz = zv * s
        v = yv * sz
        r = lax.rsqrt(jnp.mean(v * v, axis=-1, keepdims=True) + EPS)
        vh = v * r
        dvg = dov * g_ref[...]
        dv = r * (dvg - vh * jnp.mean(dvg * vh, axis=-1, keepdims=True))
        dy_ref[...] = dv * sz
        dz_ref[...] = dv * yv * (s + sz * (1.0 - s))

        @pl.when(i == 0)
        def _():
            dg_ref[...] = jnp.zeros_like(dg_ref)

        dg_ref[...] += jnp.sum(dov * vh, axis=0, keepdims=True)

    dy, dz, dg = pl.pallas_call(
        body,
        out_shape=(_sds((t, w), F32), _sds((t, w), F32), _sds((1, w), F32)),
        grid=(t // tr,),
        in_specs=[pl.BlockSpec((tr, w), lambda i: (i, 0)), pl.BlockSpec((tr, w), lambda i: (i, OFF_Z // w)),
                  pl.BlockSpec((1, w), lambda i: (0, 0)), pl.BlockSpec((tr, w), lambda i: (i, 0))],
        out_specs=(pl.BlockSpec((tr, w), lambda i: (i, 0)), pl.BlockSpec((tr, w), lambda i: (i, 0)),
                   pl.BlockSpec((1, w), lambda i: (0, 0))),
        compiler_params=_cp(("arbitrary",)),
        name=name,
    )(y, proj, g.reshape(1, w), dyab)
    return dy, dz, dg.reshape(w)


def rope_tables(t):
    inv = ROPE_BASE ** (-jnp.arange(0, MLA_ROPE, 2, dtype=F32) / MLA_ROPE)
    pos = (jnp.arange(t, dtype=F32) - PAD)[:, None]
    ang = pos * inv[None, :]
    cos, sin = jnp.cos(ang), jnp.sin(ang)
    z16 = jnp.zeros((t, 16), F32)
    z32 = jnp.zeros((t, 32), F32)
    c = jnp.concatenate([jnp.ones((t, 64), F32), cos, cos, z32], axis=1)
    s1 = jnp.concatenate([jnp.zeros((t, 64), F32), z16, sin, z32], axis=1)
    s2 = jnp.concatenate([jnp.zeros((t, 64), F32), -sin, z16, z32], axis=1)
    return c, s1, s2


def _rope(x, c, s1, s2):
    return x * c + pltpu.roll(x, 16, 1) * s1 + pltpu.roll(x, LANE - 16, 1) * s2


def _rope_t(d, c, s1, s2):
    return d * c + pltpu.roll(d * s1, LANE - 16, 1) + pltpu.roll(d * s2, 16, 1)


def rope_fwd(q_raw, kv_raw, proj, tabs):
    t = q_raw.shape[0]
    tr = _rt(t)
    hw = MLA_HEADS * LANE

    def body(q_ref, k_ref, v_ref, kr_ref, c_ref, s1_ref, s2_ref, qo_ref, ko_ref, vo_ref):
        c, s1, s2 = c_ref[...], s1_ref[...], s2_ref[...]
        kr = _rope(kr_ref[...], c, s1, s2)
        for h in range(MLA_HEADS):
            sl = slice(h * LANE, (h + 1) * LANE)
            qo_ref[:, sl] = (_rope(q_ref[:, sl], c, s1, s2) * Q_PRESCALE).astype(BF)
            ko_ref[:, sl] = (k_ref[:, sl] + kr).astype(BF)
        vo_ref[...] = v_ref[...].astype(BF)

    tab_spec = pl.BlockSpec((tr, LANE), lambda i: (i, 0))
    return pl.pallas_call(
        body,
        out_shape=(_sds((t, hw), BF), _sds((t, hw), BF), _sds((t, 1024), BF)),
        grid=(t // tr,),
        in_specs=[pl.BlockSpec((tr, hw), lambda i: (i, 0)), pl.BlockSpec((tr, hw), lambda i: (i, 0)),
                  pl.BlockSpec((tr, 1024), lambda i: (i, 2)), pl.BlockSpec((tr, LANE), lambda i: (i, OFF_KR // LANE)),
                  tab_spec, tab_spec, tab_spec],
        out_specs=(pl.BlockSpec((tr, hw), lambda i: (i, 0)), pl.BlockSpec((tr, hw), lambda i: (i, 0)),
                   pl.BlockSpec((tr, 1024), lambda i: (i, 0))),
        compiler_params=_cp(("parallel",)),
        name="rope_fwd",
    )(q_raw, kv_raw, kv_raw, proj, *tabs)


def rope_bwd(dq_cat, dk_cat, tabs):
    t = dq_cat.shape[0]
    tr = _rt(t)
    hw = MLA_HEADS * LANE

    def body(dq_ref, dk_ref, c_ref, s1_ref, s2_ref, dqo_ref, dkr_ref):
        c, s1, s2 = c_ref[...], s1_ref[...], s2_ref[...]
        acc = jnp.zeros((tr, LANE), F32)
        for h in range(MLA_HEADS):
            sl = slice(h * LANE, (h + 1) * LANE)
            dqo_ref[:, sl] = _rope_t(dq_ref[:, sl] * ATT_SCALE, c, s1, s2).astype(BF)
            acc = acc + dk_ref[:, sl]
        lane = lax.broadcasted_iota(jnp.int32, (tr, LANE), 1)
        dkr_ref[...] = jnp.where((lane >= 64) & (lane < 96), _rope_t(acc, c, s1, s2), 0.0)

    tab_spec = pl.BlockSpec((tr, LANE), lambda i: (i, 0))
    return pl.pallas_call(
        body,
        out_shape=(_sds((t, hw), BF), _sds((t, LANE), F32)),
        grid=(t // tr,),
        in_specs=[pl.BlockSpec((tr, hw), lambda i: (i, 0)), pl.BlockSpec((tr, hw), lambda i: (i, 0)),
                  tab_spec, tab_spec, tab_spec],
        out_specs=(pl.BlockSpec((tr, hw), lambda i: (i, 0)), pl.BlockSpec((tr, LANE), lambda i: (i, 0))),
        compiler_params=_cp(("parallel",)),
        name="rope_bwd",
    )(dq_cat, dk_cat, *tabs)


ATT_SCALE = (MLA_NOPE + MLA_ROPE) ** -0.5
LOG2E = math.log2(math.e)
Q_PRESCALE = ATT_SCALE * LOG2E
NT_DIMS = (((1,), (1,)), ((), ()))
TN_DIMS = (((0,), (0,)), ((), ()))


def _att_mask(qi, ki, tq, tk):
    qpos = qi * tq + lax.broadcasted_iota(jnp.int32, (tq, tk), 0)
    kpos = ki * tk + lax.broadcasted_iota(jnp.int32, (tq, tk), 1)
    return (kpos <= qpos) & (kpos >= PAD)


def _half_masks(n):
    lane = lax.broadcasted_iota(jnp.int32, (n, LANE), 1)
    return lane < 64, lane >= 64


def _att_tile(t):
    return _pick(t, (384, 256, 128))


def _ds(i, n):
    return pl.ds(i * n, n) if isinstance(i, int) else pl.ds(pl.multiple_of(i * n, n), n)


def attn_fwd(q_cat, k_cat, v, carry=None):
    t = q_cat.shape[0]
    tq = tk = _att_tile(t)
    nq = t // tq
    n_pair = MLA_HEADS // 2
    nx = carry.k if carry else 0

    def body(*refs):
        q_ref, k_ref, v_ref = refs[:3]
        o_ref, lse_ref = refs[3 + nx:5 + nx]
        qi = pl.program_id(1)
        if carry:
            start, middle, finish = carry.phases(refs[3:3 + nx], refs[5 + nx:5 + 2 * nx], refs[5 + 2 * nx:])
            pair = pl.program_id(0)
            pl.when((pair == 0) & (qi == 0))(start)
            pl.when((pair == n_pair // 2) & (qi == 0))(middle)
        lo_q, _ = _half_masks(tq)
        halves = _half_masks(tk)

        def step(ki, state, masked):
            m_old, l_old, acc = state[0:2], state[2:4], state[4]
            rows = _ds(ki, tk)
            vv = v_ref[rows, :]
            ss = [lax.dot_general(q_ref[:, hh * LANE:(hh + 1) * LANE], k_ref[rows, hh * LANE:(hh + 1) * LANE], NT_DIMS,
                                  preferred_element_type=F32) for hh in range(2)]
            if masked:
                valid = _att_mask(qi, ki, tq, tk)
                ss = [jnp.where(valid, s, NEG) for s in ss]
            m_new = [jnp.maximum(m_old[hh], jnp.max(ss[hh], axis=-1, keepdims=True)) for hh in range(2)]
            ps = [jnp.exp2(ss[hh] - m_new[hh]) for hh in range(2)]
            alpha = [jnp.exp2(m_old[hh] - m_new[hh]) for hh in range(2)]
            l_new = [alpha[hh] * l_old[hh] + jnp.sum(ps[hh], axis=-1, keepdims=True) for hh in range(2)]
            pv = [jnp.dot(ps[hh].astype(BF), jnp.where(halves[hh], vv, jnp.zeros_like(vv)), preferred_element_type=F32)
                  for hh in range(2)]
            acc = acc * jnp.where(lo_q, alpha[0], alpha[1]) + pv[0] + pv[1]
            return m_new[0], m_new[1], l_new[0], l_new[1], acc

        neg, zero = jnp.full((tq, 1), NEG, F32), jnp.zeros((tq, 1), F32)
        state = step(0, (neg, neg, zero, zero, jnp.zeros((tq, LANE), F32)), True)
        state = lax.fori_loop(1, qi, lambda ki, st: step(ki, st, False), state)
        state = lax.cond(qi > 0, lambda st: step(qi, st, True), lambda st: st, state)
        m0, m1, l0, l1, acc = state
        l = jnp.where(lo_q, l0, l1)
        o_ref[...] = (acc / l).astype(o_ref.dtype)
        lse_ref[...] = jnp.where(lo_q, m0, m1) + jnp.log2(l)
        if carry:
            pl.when((pair == n_pair - 1) & (qi == nq - 1))(finish)

    outs = pl.pallas_call(
        body,
        out_shape=(_sds((t, 1024), BF), _sds((t, 1024), F32)) + tuple(carry.out_shapes() if carry else ()),
        grid=(n_pair, nq),
        in_specs=[pl.BlockSpec((tq, 2 * LANE), lambda p, qi: (qi, p)),
                  pl.BlockSpec((t, 2 * LANE), lambda p, qi: (0, p)),
                  pl.BlockSpec((t, LANE), lambda p, qi: (0, p))] + [ANY] * nx,
        out_specs=(pl.BlockSpec((tq, LANE), lambda p, qi: (qi, p)),
                   pl.BlockSpec((tq, LANE), lambda p, qi: (qi, p))) + (ANY,) * nx,
        scratch_shapes=carry.scratch() if carry else [],
        compiler_params=_cp(("arbitrary", "arbitrary") if carry else ("parallel", "parallel")),
        name="attn_fwd_carrying" if carry else "attn_fwd",
    )(q_cat, k_cat, v, *(carry.arrs if carry else ()))
    return outs[0], outs[1], list(outs[2:])


def attn_bwd(q_cat, k_cat, v, o, lse, dyab, carry=None):
    t = q_cat.shape[0]
    tq = tk = _att_tile(t)
    nq = t // tq
    n_pair = MLA_HEADS // 2
    nx = carry.k if carry else 0

    def body(*refs):
        q_ref, k_ref, v_ref, o_ref, lse_ref, do_ref = refs[:6]
        dq_ref, dk_ref, dv_ref = refs[6 + nx:9 + nx]
        ki = pl.program_id(1)
        if carry:
            start, middle, finish = carry.phases(refs[6:6 + nx], refs[9 + nx:9 + 2 * nx], refs[9 + 2 * nx:])
            pair = pl.program_id(0)
            pl.when((pair == 0) & (ki == 0))(start)
            pl.when((pair == n_pair // 2) & (ki == 0))(middle)

        @pl.when(ki == 0)
        def _():
            dq_ref[...] = jnp.zeros_like(dq_ref)

        halves = _half_masks(tq)
        vv = v_ref[...]
        kk = [k_ref[:, hh * LANE:(hh + 1) * LANE] for hh in range(2)]

        def step(qi, acc, masked):
            rows = _ds(qi, tq)
            dov, ov, lse_v = do_ref[rows, :], o_ref[rows, :].astype(F32), lse_ref[rows, :]
            qh = [q_ref[rows, hh * LANE:(hh + 1) * LANE] for hh in range(2)]
            ss = [lax.dot_general(qh[hh], kk[hh], NT_DIMS, preferred_element_type=F32) for hh in range(2)]
            if masked:
                valid = _att_mask(qi, ki, tq, tk)
                ss = [jnp.where(valid, s, NEG) for s in ss]
            ps = [jnp.exp2(ss[hh] - lse_v[:, 64 * hh:64 * hh + 1]) for hh in range(2)]
            dom = [jnp.where(halves[hh], dov, 0.0) for hh in range(2)]
            delta = [jnp.sum(dom[hh] * ov, axis=-1, keepdims=True) for hh in range(2)]
            dom = [d.astype(BF) for d in dom]
            dp = [lax.dot_general(dom[hh], vv, NT_DIMS, preferred_element_type=F32) for hh in range(2)]
            ds = [(ps[hh] * (dp[hh] - delta[hh])).astype(BF) for hh in range(2)]
            pb = [p.astype(BF) for p in ps]
            dv = (acc[2] + lax.dot_general(pb[0], dom[0], TN_DIMS, preferred_element_type=F32)
                  + lax.dot_general(pb[1], dom[1], TN_DIMS, preferred_element_type=F32))
            dk = [acc[hh] + lax.dot_general(ds[hh], qh[hh], TN_DIMS, preferred_element_type=F32) for hh in range(2)]
            for hh in range(2):
                dq_ref[rows, hh * LANE:(hh + 1) * LANE] += jnp.dot(ds[hh], kk[hh], preferred_element_type=F32)
            return dk[0], dk[1], dv

        zero = jnp.zeros((tk, LANE), F32)
        acc = step(ki, (zero, zero, zero), True)
        acc = lax.fori_loop(ki + 1, jnp.where(ki == 0, nq, ki + 1), lambda qi, a: step(qi, a, True), acc)
        acc = lax.fori_loop(ki + 1, jnp.where(ki == 0, ki + 1, nq), lambda qi, a: step(qi, a, False), acc)
        dk_ref[:, 0:LANE] = acc[0] * (1.0 / LOG2E)
        dk_ref[:, LANE:2 * LANE] = acc[1] * (1.0 / LOG2E)
        dv_ref[...] = acc[2]
        if carry:
            pl.when((pair == n_pair - 1) & (ki == nq - 1))(finish)

    full = lambda w, off=0: pl.BlockSpec((t, w), lambda p, ki: (0, p + off))
    blk = lambda w: pl.BlockSpec((tk, w), lambda p, ki: (ki, p))
    outs = pl.pallas_call(
        body,
        out_shape=(_sds((t, 2048), F32), _sds((t, 2048), F32), _sds((t, 1024), F32))
        + tuple(carry.out_shapes() if carry else ()),
        grid=(n_pair, nq),
        in_specs=[full(2 * LANE), blk(2 * LANE), blk(LANE), full(LANE), full(LANE), full(LANE, 8)] + [ANY] * nx,
        out_specs=(full(2 * LANE), blk(2 * LANE), blk(LANE)) + (ANY,) * nx,
        scratch_shapes=carry.scratch() if carry else [],
        compiler_params=_cp(("arbitrary", "arbitrary") if carry else ("parallel", "arbitrary")),
        name="attn_bwd_carrying" if carry else "attn_bwd",
    )(q_cat, k_cat, v, o, lse, dyab, *(carry.arrs if carry else ()))
    return outs[0], outs[1], outs[2], list(outs[3:])


N_PAIR = SSD_HEADS // 2


def _hdot(a, b):
    return jnp.dot(a, b, precision=HI, preferred_element_type=F32)


def _ssd_chunk(xs, bg, cg, dtraw, hin, dt_bias, a_log, dskip, rowmask):
    ln = CHUNK
    causal = lax.broadcasted_iota(jnp.int32, (ln, ln), 0) >= lax.broadcasted_iota(jnp.int32, (ln, ln), 1)
    ltri = causal.astype(F32)
    lane = lax.broadcasted_iota(jnp.int32, (ln, LANE), 1)
    halves = (lane < 64, lane >= 64)
    low_row = lax.broadcasted_iota(jnp.int32, (1, LANE), 1) < 64
    head_lane = lax.broadcasted_iota(jnp.int32, (1, SSD_HEADS), 1)
    head_row = lax.broadcasted_iota(jnp.int32, (SSD_HEADS, 1), 0)

    def col(a, h):
        return jnp.sum(jnp.where(head_lane == h, a, 0.0), axis=1, keepdims=True)

    dt = _softplus(dtraw + dt_bias) * rowmask
    da = dt * (-jnp.exp(a_log))
    acs = _hdot(ltri, da)
    acs_t = lax.dot_general(da, ltri, (((0,), (1,)), ((), ())), precision=HI, preferred_element_type=F32)
    tot = jnp.sum(da, axis=0, keepdims=True)
    bm = [b * rowmask for b in bg]
    cm = [c * rowmask for c in cg]
    cb = [lax.dot_general(cm[g].astype(BF), bm[g].astype(BF), NT_DIMS, preferred_element_type=F32) for g in range(2)]
    ys, hout = [], []
    for p in range(N_PAIR):
        g = p // (N_PAIR // 2)
        h0, h1 = 2 * p, 2 * p + 1
        xdt = xs[p] * jnp.where(halves[0], col(dt, h0), col(dt, h1))
        y = jnp.zeros((ln, LANE), F32)
        snew = jnp.zeros((ln, LANE), F32)
        for hh in range(2):
            a_col = col(acs, h0 + hh)
            a_row = jnp.sum(jnp.where(head_row == h0 + hh, acs_t, 0.0), axis=0, keepdims=True)
            dec = jnp.exp(jnp.where(causal, a_col - a_row, NEG))
            xm = jnp.where(halves[hh], xdt, 0.0).astype(BF)
            y = y + jnp.dot((cb[g] * dec).astype(BF), xm, preferred_element_type=F32)
            bd = bm[g] * jnp.exp(col(tot, h0 + hh) - a_col)
            snew = snew + lax.dot_general(bd.astype(BF), xm, TN_DIMS, preferred_element_type=F32)
        y_off = (jnp.dot(cm[g].astype(BF), hin[p].astype(BF), preferred_element_type=F32)
                 * jnp.exp(jnp.where(halves[0], col(acs, h0), col(acs, h1))))
        ys.append(y + y_off + jnp.where(low_row, col(dskip, h0), col(dskip, h1)) * xs[p])
        hout.append(jnp.exp(jnp.where(low_row, col(tot, h0), col(tot, h1))) * hin[p] + snew)
    return ys, hout


def _ssd_load(x_ref, dt_ref):
    xs = [x_ref[:, p * LANE:(p + 1) * LANE] for p in range(N_PAIR)]
    bg = [x_ref[:, SSD_D_INNER + g * LANE:SSD_D_INNER + (g + 1) * LANE] for g in range(2)]
    cg = [x_ref[:, SSD_D_INNER + (2 + g) * LANE:SSD_D_INNER + (3 + g) * LANE] for g in range(2)]
    return xs, bg, cg, dt_ref[:, 0:SSD_HEADS]


def _chunk_rowmask(c):
    return ((c * CHUNK + lax.broadcasted_iota(jnp.int32, (CHUNK, 1), 0)) >= PAD).astype(F32)


def ssd_fwd(xbc_c, proj, dt_bias, a_log, dskip):
    t = xbc_c.shape[0]
    nc = t // CHUNK

    def body(x_ref, dt_ref, dtb_ref, al_ref, d_ref, y_ref, hs_ref, h_s):
        c = pl.program_id(0)

        @pl.when(c == 0)
        def _():
            h_s[...] = jnp.zeros_like(h_s)

        xs, bg, cg, dtraw = _ssd_load(x_ref, dt_ref)
        hin = [h_s[p] for p in range(N_PAIR)]
        hs_ref[0] = h_s[...]
        ys, hout = _ssd_chunk(xs, bg, cg, dtraw, hin, dtb_ref[...], al_ref[...], d_ref[...], _chunk_rowmask(c))
        for p in range(N_PAIR):
            y_ref[:, p * LANE:(p + 1) * LANE] = ys[p]
            h_s[p] = hout[p]

    par = pl.BlockSpec((1, SSD_HEADS), lambda c: (0, 0))
    return pl.pallas_call(
        body,
        out_shape=(_sds((t, SSD_D_INNER), F32), _sds((nc, N_PAIR, CHUNK, LANE), F32)),
        grid=(nc,),
        in_specs=[pl.BlockSpec((CHUNK, SSD_CONV_CH), lambda c: (c, 0)),
                  pl.BlockSpec((CHUNK, LANE), lambda c: (c, OFF_DT // LANE)), par, par, par],
        out_specs=(pl.BlockSpec((CHUNK, SSD_D_INNER), lambda c: (c, 0)),
                   pl.BlockSpec((1, N_PAIR, CHUNK, LANE), lambda c: (c, 0, 0, 0))),
        scratch_shapes=[pltpu.VMEM((N_PAIR, CHUNK, LANE), F32)],
        compiler_params=_cp(("arbitrary",)),
        name="ssd_fwd",
    )(xbc_c, proj, dt_bias.reshape(1, -1), a_log.reshape(1, -1), dskip.reshape(1, -1))


def ssd_bwd(xbc_c, proj, dt_bias, a_log, dskip, hs, dy):
    t = xbc_c.shape[0]
    nc = t // CHUNK

    def body(x_ref, dt_ref, dtb_ref, al_ref, d_ref, hs_ref, dy_ref, dx_ref, ddt_ref, dpar_ref, dh_s):
        ci = pl.program_id(0)
        c = nc - 1 - ci

        @pl.when(ci == 0)
        def _():
            dh_s[...] = jnp.zeros_like(dh_s)
            dpar_ref[...] = jnp.zeros_like(dpar_ref)

        xs, bg, cg, dtraw = _ssd_load(x_ref, dt_ref)
        hin = [hs_ref[0, p] for p in range(N_PAIR)]
        rowmask = _chunk_rowmask(c)
        fn = lambda xs_, bg_, cg_, dtraw_, hin_, dtb_, al_, d_: _ssd_chunk(xs_, bg_, cg_, dtraw_, hin_, dtb_, al_, d_, rowmask)
        _, vjp = jax.vjp(fn, xs, bg, cg, dtraw, hin, dtb_ref[...], al_ref[...], d_ref[...])
        dys = [dy_ref[:, p * LANE:(p + 1) * LANE] for p in range(N_PAIR)]
        dhs = [dh_s[p] for p in range(N_PAIR)]
        dxs, dbg, dcg, ddtraw, dhin, ddtb, dal, dd = vjp((dys, dhs))
        for p in range(N_PAIR):
            dx_ref[:, p * LANE:(p + 1) * LANE] = dxs[p]
            dh_s[p] = dhin[p]
        for g in range(2):
            dx_ref[:, SSD_D_INNER + g * LANE:SSD_D_INNER + (g + 1) * LANE] = dbg[g]
            dx_ref[:, SSD_D_INNER + (2 + g) * LANE:SSD_D_INNER + (3 + g) * LANE] = dcg[g]
        ddt_ref[...] = jnp.zeros_like(ddt_ref)
        ddt_ref[:, 0:SSD_HEADS] = ddtraw
        dpar_ref[0:1, 0:SSD_HEADS] += ddtb
        dpar_ref[1:2, 0:SSD_HEADS] += dal
        dpar_ref[2:3, 0:SSD_HEADS] += dd

    par = pl.BlockSpec((1, SSD_HEADS), lambda ci: (0, 0))
    return pl.pallas_call(
        body,
        out_shape=(_sds((t, SSD_CONV_CH), F32), _sds((t, LANE), F32), _sds((8, LANE), F32)),
        grid=(nc,),
        in_specs=[pl.BlockSpec((CHUNK, SSD_CONV_CH), lambda ci: (nc - 1 - ci, 0)),
                  pl.BlockSpec((CHUNK, LANE), lambda ci: (nc - 1 - ci, OFF_DT // LANE)), par, par, par,
                  pl.BlockSpec((1, N_PAIR, CHUNK, LANE), lambda ci: (nc - 1 - ci, 0, 0, 0)),
                  pl.BlockSpec((CHUNK, SSD_D_INNER), lambda ci: (nc - 1 - ci, 0))],
        out_specs=(pl.BlockSpec((CHUNK, SSD_CONV_CH), lambda ci: (nc - 1 - ci, 0)),
                   pl.BlockSpec((CHUNK, LANE), lambda ci: (nc - 1 - ci, 0)),
                   pl.BlockSpec((8, LANE), lambda ci: (0, 0))),
        scratch_shapes=[pltpu.VMEM((N_PAIR, CHUNK, LANE), F32)],
        compiler_params=_cp(("arbitrary",)),
        name="ssd_bwd",
    )(xbc_c, proj, dt_bias.reshape(1, -1), a_log.reshape(1, -1), dskip.reshape(1, -1), hs, dy)


def _neg_expm1(y):
    series = -(y * (1.0 + y * (0.5 + y * (1.0 / 6.0 + y * (1.0 / 24.0 + y * (1.0 / 120.0))))))
    return jnp.where(y > -0.1, series, 1.0 - jnp.exp(y))


def _rg_pw(xr, wa, ba, wi, bi, lam, rowmask):
    xb = xr.astype(BF)
    r = jax.nn.sigmoid(jnp.dot(xb, wa.astype(BF), preferred_element_type=F32) + ba)
    i = jax.nn.sigmoid(jnp.dot(xb, wi.astype(BF), preferred_element_type=F32) + bi)
    log_a = -LRU_C * r * _softplus(-lam)
    a = jnp.exp(log_a)
    u = jnp.sqrt(_neg_expm1(2.0 * log_a)) * (i * xr) * rowmask
    return a, u


def _gelu_grad(x):
    c = math.sqrt(2.0 / math.pi)
    th = jnp.tanh(c * (x + 0.044715 * (x * x * x)))
    return 0.5 * (1.0 + th) + 0.5 * x * (1.0 - th * th) * c * (1.0 + 3.0 * 0.044715 * x * x)


def _scan_fwd(a, u):
    n = a.shape[0]
    row = lax.broadcasted_iota(jnp.int32, a.shape, 0)
    s = 1
    while s < n:
        a_s = jnp.where(row >= s, pltpu.roll(a, s, 0), 1.0)
        u_s = jnp.where(row >= s, pltpu.roll(u, s, 0), 0.0)
        u = u + a * u_s
        a = a * a_s
        s *= 2
    return a, u


def _scan_bwd(b, d):
    n = b.shape[0]
    row = lax.broadcasted_iota(jnp.int32, b.shape, 0)
    s = 1
    while s < n:
        b_s = jnp.where(row < n - s, pltpu.roll(b, n - s, 0), 1.0)
        d_s = jnp.where(row < n - s, pltpu.roll(d, n - s, 0), 0.0)
        d = d + b * d_s
        b = b * b_s
        s *= 2
    return d


def rg_fwd(xg, rgp, w_a, w_i):
    t = xg.shape[0]
    tr = _rt(t)

    def body(x_ref, g_ref, p_ref, wa_ref, wi_ref, hg_ref, hs_ref, prev, hcar):
        ti = pl.program_id(1)

        @pl.when(ti == 0)
        def _():
            prev[...] = jnp.zeros_like(prev)
            hcar[...] = jnp.zeros_like(hcar)

        xv = x_ref[...]
        ext = jnp.concatenate([prev[...], xv], axis=0)
        xr = _conv_pre(ext, xv, p_ref, tr)
        rowmask = _row_mask(ti, tr, (tr, 1)).astype(F32)
        a, u = _rg_pw(xr, wa_ref[0], p_ref[5:6, :], wi_ref[0], p_ref[6:7, :], p_ref[7:8, :], rowmask)
        a_cum, h_loc = _scan_fwd(a, u)
        hs = h_loc + a_cum * hcar[0:1, :]
        hs_ref[...] = hs
        hg_ref[...] = (hs * _gelu(g_ref[...])).astype(hg_ref.dtype)
        hcar[...] = jnp.broadcast_to(hs[tr - 1:tr, :], (8, LANE))
        prev[...] = xv[tr - 8:, :]

    return pl.pallas_call(
        body,
        out_shape=(_sds((t, LRU_WIDTH), BF), _sds((t, LRU_WIDTH), F32)),
        grid=(LRU_BLOCKS, t // tr),
        in_specs=[pl.BlockSpec((tr, LANE), lambda n, ti: (ti, n)),
                  pl.BlockSpec((tr, LANE), lambda n, ti: (ti, LRU_BLOCKS + n)),
                  pl.BlockSpec((8, LANE), lambda n, ti: (0, n)),
                  pl.BlockSpec((1, LANE, LANE), lambda n, ti: (n, 0, 0)),
                  pl.BlockSpec((1, LANE, LANE), lambda n, ti: (n, 0, 0))],
        out_specs=(pl.BlockSpec((tr, LANE), lambda n, ti: (ti, n)), pl.BlockSpec((tr, LANE), lambda n, ti: (ti, n))),
        scratch_shapes=[pltpu.VMEM((8, LANE), F32), pltpu.VMEM((8, LANE), F32)],
        compiler_params=_cp(("parallel", "arbitrary")),
        name="rg_fwd",
    )(xg, xg, rgp, w_a, w_i)


def rg_bwd(xg, rgp, w_a, w_i, hs, dhg):
    t = xg.shape[0]
    tr = _rt(t)
    nt = t // tr
    r8 = tr // 8

    def body(x_ref, xp_ref, g_ref, p_ref, wa_ref, wi_ref, hs_ref, hp_ref, dhg_ref,
             dx_ref, dg_ref, dp_ref, dwa_ref, dwi_ref, gcar, dnext):
        ti = pl.program_id(1)
        tt = nt - 1 - ti

        @pl.when(ti == 0)
        def _():
            gcar[...] = jnp.zeros_like(gcar)
            dnext[...] = jnp.zeros_like(dnext)
            dp_ref[...] = jnp.zeros_like(dp_ref)
            dwa_ref[...] = jnp.zeros_like(dwa_ref)
            dwi_ref[...] = jnp.zeros_like(dwi_ref)

        xv = x_ref[...]
        halo = jnp.where(tt > 0, xp_ref[...], 0.0)
        ext = jnp.concatenate([halo, xv], axis=0)
        xr = _conv_pre(ext, xv, p_ref, tr)
        rowmask = _row_mask(tt, tr, (tr, 1)).astype(F32)
        fn = lambda xr_, wa_, ba_, wi_, bi_, lam_: _rg_pw(xr_, wa_, ba_, wi_, bi_, lam_, rowmask)
        (a, _), vjp = jax.vjp(fn, xr, wa_ref[0], p_ref[5:6, :], wi_ref[0], p_ref[6:7, :], p_ref[7:8, :])
        gpre = g_ref[...]
        hsv = hs_ref[...]
        dhg_v = dhg_ref[...]
        dg_ref[...] = (dhg_v * hsv * _gelu_grad(gpre)).astype(dg_ref.dtype)
        row = lax.broadcasted_iota(jnp.int32, (tr, LANE), 0)
        d = dhg_v * _gelu(gpre) + jnp.where(row == tr - 1, gcar[0:1, :], 0.0)
        b = jnp.where(row < tr - 1, pltpu.roll(a, tr - 1, 0), 0.0)
        g = _scan_bwd(b, d)
        gcar[...] = jnp.broadcast_to(a[0:1, :] * g[0:1, :], (8, LANE))
        hlast = jnp.where(tt > 0, hp_ref[7:8, :], 0.0)
        hprev = jnp.where(row == 0, hlast, pltpu.roll(hsv, 1, 0))
        dxr, dwa, dba, dwi, dbi, dlam = vjp((g * hprev, g))
        dx, sums = _conv_bwd_parts(dxr, dnext[...], xv, ext, p_ref, tr)
        dx_ref[...] = dx.astype(dx_ref.dtype)
        dnext[...] = dxr[:8, :]
        dp_ref[...] += _rows_block(sums + [dba, dbi, dlam])
        dwa_ref[0] += dwa
        dwi_ref[0] += dwi

    tile = lambda off=0: pl.BlockSpec((tr, LANE), lambda n, ti: (nt - 1 - ti, off + n))
    halo = lambda off=0: pl.BlockSpec((8, LANE), lambda n, ti: (jnp.maximum((nt - 1 - ti) * r8 - 1, 0), off + n))
    par = pl.BlockSpec((8, LANE), lambda n, ti: (0, n))
    wspec = pl.BlockSpec((1, LANE, LANE), lambda n, ti: (n, 0, 0))
    return pl.pallas_call(
        body,
        out_shape=(_sds((t, LRU_WIDTH), BF), _sds((t, LRU_WIDTH), BF), _sds((8, LRU_WIDTH), F32),
                   _sds((LRU_BLOCKS, LANE, LANE), F32), _sds((LRU_BLOCKS, LANE, LANE), F32)),
        grid=(LRU_BLOCKS, nt),
        in_specs=[tile(), halo(), tile(LRU_BLOCKS), par, wspec, wspec, tile(), halo(), tile()],
        out_specs=(tile(), tile(), par, wspec, wspec),
        scratch_shapes=[pltpu.VMEM((8, LANE), F32), pltpu.VMEM((8, LANE), F32)],
        compiler_params=_cp(("parallel", "arbitrary")),
        name="rg_bwd",
    )(xg, xg, xg, rgp, w_a, w_i, hs, hs, dhg)


PACK_W = 1024
MESH_ID = pl.DeviceIdType.MESH
ANY = pl.BlockSpec(memory_space=pl.ANY)


def _my_place():
    x, y, c = lax.axis_index("x"), lax.axis_index("y"), lax.axis_index("c")
    return x, y, c


def _lin(px, py, pc):
    return 4 * px + 2 * py + pc


class Exchange:
    def __init__(self, kind, arrs):
        self.kind, self.arrs, self.k = kind, list(arrs), len(arrs)

    def out_shapes(self):
        if self.kind == "gather":
            return [_sds((N_DEV,) + a.shape, a.dtype) for a in self.arrs]
        return [_sds(a.shape, a.dtype) for a in self.arrs]

    def scratch(self):
        k = self.k
        return [pltpu.SemaphoreType.DMA((k, 7)), pltpu.SemaphoreType.DMA((k, 7)), pltpu.SemaphoreType.DMA((k,))]

    def phases(self, ins, outs, sems):
        return (self._gather if self.kind == "gather" else self._scatter)(ins, outs, *sems)

    def _gather(self, ins, outs, send_sems, recv_sems, local_sems):
        k = self.k
        x, y, c = _my_place()
        me, sibling = (x, y, c), (x, y, 1 - c)
        chips = [(1 - x, y), (x, 1 - y), (1 - x, 1 - y)]

        def copy(a, sem, block, to, from_input=False):
            slab = outs[a].at[_lin(*block)]
            return pltpu.make_async_remote_copy(
                src_ref=ins[a] if from_input else slab, dst_ref=slab,
                send_sem=send_sems.at[a, sem], recv_sem=recv_sems.at[a, sem],
                device_id=to, device_id_type=MESH_ID)

        def mine():
            return [pltpu.make_async_copy(ins[a], outs[a].at[_lin(*me)], local_sems.at[a]) for a in range(k)]

        def first():
            out = []
            for a in range(k):
                out.append(copy(a, 0, me, sibling, True))
                out += [copy(a, 1 + j, me, (*chip, c), True) for j, chip in enumerate(chips)]
            return out

        def passed():
            return [copy(a, 4 + j, (*chip, c), sibling) for j, chip in enumerate(chips) for a in range(k)]

        def start():
            for cp in mine() + first():
                cp.start()

        def middle():
            onward = passed()
            for j, chip in enumerate(chips):
                for a in range(k):
                    copy(a, 1 + j, (*chip, c), me).wait_recv()
                    onward[j * k + a].start()

        def finish():
            for a in range(k):
                copy(a, 0, sibling, me).wait_recv()
                for j, chip in enumerate(chips):
                    copy(a, 4 + j, (*chip, 1 - c), me).wait_recv()
            for cp in first() + passed():
                cp.wait_send()
            for cp in mine():
                cp.wait()

        return start, middle, finish

    def _scatter(self, ins, outs, send_sems, recv_sems, local_sems):
        k = self.k
        x, y, c = _my_place()
        me = _lin(x, y, c)
        peers = [((1 - x) if r & 4 else x, (1 - y) if r & 2 else y, (1 - c) if r & 1 else c) for r in range(1, N_DEV)]

        def copy(a, r, src_slab, dst_slab, to):
            return pltpu.make_async_remote_copy(
                src_ref=ins[a].at[src_slab], dst_ref=outs[a].at[dst_slab],
                send_sem=send_sems.at[a, r], recv_sem=recv_sems.at[a, r],
                device_id=to, device_id_type=MESH_ID)

        def mine():
            return [pltpu.make_async_copy(ins[a].at[me], outs[a].at[me], local_sems.at[a]) for a in range(k)]

        def sends():
            return [copy(a, r, _lin(*peer), me, peer) for r, peer in enumerate(peers) for a in range(k)]

        def start():
            for cp in mine() + sends():
                cp.start()

        def middle():
            pass

        def finish():
            for r, peer in enumerate(peers):
                for a in range(k):
                    copy(a, r, me, _lin(*peer), peer).wait_recv()
            for cp in sends():
                cp.wait_send()
            for cp in mine():
                cp.wait()

        return start, middle, finish

    def run(self, name):
        k = self.k

        def body(*refs):
            start, middle, finish = self.phases(refs[:k], refs[k:2 * k], refs[2 * k:])
            start()
            middle()
            finish()

        return pl.pallas_call(
            body,
            out_shape=tuple(self.out_shapes()),
            in_specs=[ANY] * k,
            out_specs=tuple(ANY for _ in range(k)),
            scratch_shapes=self.scratch(),
            name=name,
        )(*self.arrs)


def all_gather(arrs, name):
    return Exchange("gather", arrs).run(name)


def all_to_all(arrs, name):
    return Exchange("scatter", arrs).run(name)


def slab_sum(a, name):
    _, r, w = a.shape
    tr = _pick(r, (256, 128, 64, 32, 16, 8))

    def body(a_ref, o_ref):
        acc = a_ref[0].astype(F32)
        for d in range(1, N_DEV):
            acc = acc + a_ref[d].astype(F32)
        o_ref[...] = acc

    return pl.pallas_call(
        body,
        out_shape=_sds((r, w), F32),
        grid=(r // tr,),
        in_specs=[pl.BlockSpec((N_DEV, tr, w), lambda i: (0, i, 0))],
        out_specs=pl.BlockSpec((tr, w), lambda i: (i, 0)),
        compiler_params=_cp(("parallel",)),
        name=name,
    )(a)


def adamw(w, g, m, v, name):
    r, c = w.shape
    tr = _pick(r, (256, 160, 128, 64, 32, 16, 8))

    def body(w_ref, g_ref, m_ref, v_ref, d_ref, nm_ref, nv_ref):
        gv = g_ref[...]
        nm = ADAM_B1 * m_ref[...] + (1.0 - ADAM_B1) * gv
        nv = ADAM_B2 * v_ref[...] + (1.0 - ADAM_B2) * (gv * gv)
        m_hat = nm / (1.0 - ADAM_B1 ** ADAM_STEP)
        v_hat = nv / (1.0 - ADAM_B2 ** ADAM_STEP)
        d_ref[...] = -ADAM_LR * (m_hat / (jnp.sqrt(v_hat) + ADAM_EPS) + ADAM_WD * w_ref[...])
        nm_ref[...] = nm
        nv_ref[...] = nv

    spec = pl.BlockSpec((tr, c), lambda i: (i, 0))
    return pl.pallas_call(
        body,
        out_shape=tuple(_sds((r, c), F32) for _ in range(3)),
        grid=(r // tr,),
        in_specs=[spec] * 4,
        out_specs=(spec, spec, spec),
        compiler_params=_cp(("parallel",)),
        name=name,
    )(w, g, m, v)


def _relu2_epi(acc):
    r = jnp.maximum(acc, 0.0)
    return r * r, r


def _drelu2_epi(acc, r):
    return (acc * (2.0 * r.astype(F32)),)


def mlp_fwd(h, g_pre, g_post, w_up, w_down):
    hn = norm_fwd(h, g_pre, BF, name="mlp_norm")
    u, r = matmul(hn, w_up, "nn", (BF, BF), epi=_relu2_epi, name="mlp_up")
    d = matmul(u, w_down, "nn", name="mlp_down")
    return resadd_fwd(h, d, g_post, name="mlp_res"), (h, hn, u, r, d)


def mlp_bwd(res, dh2, g_pre, g_post, w_up, w_down):
    h, hn, u, r, d = res
    dd, dg_post = norm_bwd(d, g_post, dh2, mask_pad=True, out_dtype=BF, name="mlp_post_bwd")
    dw_down = matmul(u, dd, "tn", name="mlp_dwdown")
    dp = matmul(dd, w_down, "nt", (BF,), epi=_drelu2_epi, extras=(r,), name="mlp_du")
    dw_up = matmul(hn, dp, "tn", name="mlp_dwup")
    dhn = matmul(dp, w_up, "nt", name="mlp_dhn")
    dh, dg_pre = norm_bwd(h, g_pre, dhn, dres=dh2, name="mlp_pre_bwd")
    return dh, dict(mlp_pre_g=dg_pre, mlp_post_g=dg_post, w_up=dw_up, w_down=dw_down)


def rg_layer_fwd(h, g_pre, g_post, w_xy, rgp, w_a, w_i, w_out):
    hn = norm_fwd(h, g_pre, BF, name="rg_norm")
    xg = matmul(hn, w_xy, "nn", name="rg_in")
    hg, hs = rg_fwd(xg, rgp, w_a, w_i)
    m = matmul(hg, w_out, "nn", name="rg_out")
    return resadd_fwd(h, m, g_post, name="rg_res"), (h, hn, xg, hg, hs, m)


def rg_layer_bwd(res, dh2, g_pre, g_post, w_xy, rgp, w_a, w_i, w_out):
    h, hn, xg, hg, hs, m = res
    dm, dg_post = norm_bwd(m, g_post, dh2, mask_pad=True, out_dtype=BF, name="rg_post_bwd")
    dw_out = matmul(hg, dm, "tn", name="rg_dwout")
    dhg = matmul(dm, w_out, "nt", name="rg_dhg")
    dxr, dgp, drgp, dwa, dwi = rg_bwd(xg, rgp, w_a, w_i, hs, dhg)
    dxg = jnp.concatenate([dxr, dgp], axis=1)
    dw_xy = matmul(hn, dxg, "tn", name="rg_dwin")
    dhn = matmul(dxg, w_xy, "nt", name="rg_dhn")
    dh, dg_pre = norm_bwd(h, g_pre, dhn, dres=dh2, name="rg_pre_bwd")
    return dh, dict(mix_pre_g=dg_pre, mix_post_g=dg_post, rg_w_x=dw_xy[:, :LRU_WIDTH], rg_w_y=dw_xy[:, LRU_WIDTH:],
                    rg_conv_w=drgp[0:4], rg_conv_b=drgp[4], rg_b_a=drgp[5], rg_b_i=drgp[6], rg_lambda=drgp[7],
                    rg_w_a=dwa, rg_w_i=dwi, rg_w_out=dw_out)


def sm_layer_fwd(h, g_pre, g_post, w_in_p, convp, dt_bias, a_log, dskip, ssd_g, q_g, w_q_p, kv_g, w_kv_p, w_out, tabs,
                 carry=None):
    hn = norm_fwd(h, g_pre, BF, name="sm_norm")
    proj = matmul(hn, w_in_p, "nn", name="sm_in")
    xbc_c = conv_silu_fwd(proj, OFF_XBC // LANE, SSD_CONV_CH // LANE, convp, name="ssd_conv")
    y, hst = ssd_fwd(xbc_c, proj, dt_bias, a_log, dskip)
    y_ssd = gated_norm_fwd(y, proj, ssd_g)
    cqn = norm_fwd(proj, q_g, BF, col_blk=OFF_CQ // MLA_Q_RANK, width=MLA_Q_RANK, name="q_norm")
    q_raw = matmul(cqn, w_q_p, "nn", name="q_up")
    ckvn = norm_fwd(proj, kv_g, BF, col_blk=OFF_CKV // MLA_KV_RANK, width=MLA_KV_RANK, name="kv_norm")
    kv_raw = matmul(ckvn, w_kv_p, "nn", name="kv_up")
    q_cat, k_cat, v = rope_fwd(q_raw, kv_raw, proj, tabs)
    o, lse, carried = attn_fwd(q_cat, k_cat, v, carry)
    yab = jnp.concatenate([y_ssd, o], axis=1)
    m = matmul(yab, w_out, "nn", name="sm_out")
    res = (h, hn, proj, xbc_c, y, hst, cqn, ckvn, q_cat, k_cat, v, o, lse, yab, m)
    return resadd_fwd(h, m, g_post, name="sm_res"), res, carried


def sm_layer_bwd(res, dh2, g_pre, g_post, w_in_p, convp, dt_bias, a_log, dskip, ssd_g, q_g, w_q_p, kv_g, w_kv_p, w_out, tabs,
                 carry=None):
    h, hn, proj, xbc_c, y, hst, cqn, ckvn, q_cat, k_cat, v, o, lse, yab, m = res
    dm, dg_post = norm_bwd(m, g_post, dh2, mask_pad=True, out_dtype=BF, name="sm_post_bwd")
    dw_out = matmul(yab, dm, "tn", name="sm_dwout")
    dyab = matmul(dm, w_out, "nt", name="sm_dyab")
    dq_cat, dk_cat, dv, carried = attn_bwd(q_cat, k_cat, v, o, lse, dyab, carry)
    dq_raw, dkr = rope_bwd(dq_cat, dk_cat, tabs)
    dkv_raw = jnp.concatenate([dk_cat, dv], axis=1).astype(BF)
    dw_kv_p = matmul(ckvn, dkv_raw, "tn", name="kv_dw")
    dckvn = matmul(dkv_raw, w_kv_p, "nt", name="kv_dx")
    dckv, dg_kv = norm_bwd(proj, kv_g, dckvn, col_blk=OFF_CKV // MLA_KV_RANK, width=MLA_KV_RANK, name="kv_norm_bwd")
    dw_q_p = matmul(cqn, dq_raw, "tn", name="q_dw")
    dcqn = matmul(dq_raw, w_q_p, "nt", name="q_dx")
    dcq, dg_q = norm_bwd(proj, q_g, dcqn, col_blk=OFF_CQ // MLA_Q_RANK, width=MLA_Q_RANK, name="q_norm_bwd")
    dy, dz, dg_ssd = gated_norm_bwd(y, proj, ssd_g, dyab)
    dxbc_c, ddt, dpar = ssd_bwd(xbc_c, proj, dt_bias, a_log, dskip, hst, dy)
    dxbc, dconvp = conv_silu_bwd(proj, OFF_XBC // LANE, SSD_CONV_CH // LANE, convp, dxbc_c, name="ssd_conv_bwd")
    dproj = jnp.concatenate([dz, dxbc, dckv, ddt, dkr, dcq], axis=1).astype(BF)
    dw_in_p = matmul(hn, dproj, "tn", (BF,), name="sm_dwin")
    dhn = matmul(dproj, w_in_p, "nt", name="sm_dhn")
    dh, dg_pre = norm_bwd(h, g_pre, dhn, dres=dh2, name="sm_pre_bwd")
    grads = dict(mix_pre_g=dg_pre, mix_post_g=dg_post, w_in=w_in_cols_to_blocks(dw_in_p), ssd_conv_w=dconvp[0:4],
                 ssd_conv_b=dconvp[4], ssd_dt_bias=dpar[0, :SSD_HEADS], ssd_a_log=dpar[1, :SSD_HEADS],
                 ssd_d=dpar[2, :SSD_HEADS], ssd_norm_g=dg_ssd, mla_q_norm_g=dg_q, mla_w_q_up=_unpack_w_q(dw_q_p),
                 mla_kv_norm_g=dg_kv, mla_w_kv_up=_unpack_w_kv(dw_kv_p), w_out_ab=dw_out)
    return dh, grads, carried


W_IN_COLS = 3248
W_IN_SHARD = W_IN_COLS // N_DEV
W_IN_WIRE = 512


def _w_in_tables():
    src = np.full((IN_W,), -1, np.int64)
    src[0:2560] = np.arange(2560)
    src[OFF_CKV:OFF_CKV + 256] = 2960 + np.arange(256)
    src[OFF_DT:OFF_DT + 16] = 2560 + np.arange(16)
    src[OFF_KR + 64:OFF_KR + 96] = 3216 + np.arange(32)
    src[OFF_CQ:OFF_CQ + 384] = 2576 + np.arange(384)
    dev = np.where(src >= 0, src // W_IN_SHARD, -1).astype(np.int32).reshape(1, IN_W)
    col = np.where(src >= 0, src % W_IN_SHARD, 0).astype(np.int32).reshape(1, IN_W)
    return jnp.asarray(dev), jnp.asarray(col)


def w_in_blocks_to_cols(g8):
    _, k, wp = g8.shape
    tn = 384
    dev, col = _w_in_tables()

    def body(g_ref, dev_ref, col_ref, o_ref):
        row = lax.broadcasted_iota(jnp.int32, (wp, tn), 0)
        hit = row == col_ref[...]
        acc = jnp.zeros((k, tn), F32)
        for j in range(N_DEV):
            sel = (hit & (dev_ref[...] == j)).astype(BF)
            acc = acc + jnp.dot(g_ref[j], sel, preferred_element_type=F32)
        o_ref[...] = acc.astype(o_ref.dtype)

    return pl.pallas_call(
        body,
        out_shape=_sds((k, IN_W), BF),
        grid=(IN_W // tn,),
        in_specs=[pl.BlockSpec((N_DEV, k, wp), lambda i: (0, 0, 0)), pl.BlockSpec((1, tn), lambda i: (0, i)),
                  pl.BlockSpec((1, tn), lambda i: (0, i))],
        out_specs=pl.BlockSpec((k, tn), lambda i: (0, i)),
        compiler_params=_cp(("parallel",)),
        name="w_in_cols",
    )(g8, dev, col)


def w_in_cols_to_blocks(dw):
    k = dw.shape[0]
    dev, col = _w_in_tables()

    def body(dw_ref, dev_ref, col_ref, o_ref):
        j = pl.program_id(0)
        row = lax.broadcasted_iota(jnp.int32, (W_IN_WIRE, IN_W), 0)
        sel = ((row == col_ref[...]) & (dev_ref[...] == j)).astype(BF)
        o_ref[0] = lax.dot_general(dw_ref[...], sel, NT_DIMS, preferred_element_type=F32).astype(o_ref.dtype)

    return pl.pallas_call(
        body,
        out_shape=_sds((N_DEV, k, W_IN_WIRE), BF),
        grid=(N_DEV,),
        in_specs=[pl.BlockSpec((k, IN_W), lambda j: (0, 0)), pl.BlockSpec((1, IN_W), lambda j: (0, 0)),
                  pl.BlockSpec((1, IN_W), lambda j: (0, 0))],
        out_specs=pl.BlockSpec((1, k, W_IN_WIRE), lambda j: (j, 0, 0)),
        compiler_params=_cp(("parallel",)),
        name="w_in_blocks",
    )(dw, dev, col)


def _pack_w_q(w):
    w3 = w.reshape(w.shape[0], MLA_HEADS, MLA_NOPE + MLA_ROPE)
    return jnp.pad(w3, ((0, 0), (0, 0), (0, LANE - MLA_NOPE - MLA_ROPE))).reshape(w.shape[0], MLA_HEADS * LANE)


def _unpack_w_q(p):
    return p.reshape(p.shape[0], MLA_HEADS, LANE)[:, :, :MLA_NOPE + MLA_ROPE].reshape(p.shape[0], -1)


def _pack_w_kv(w):
    w3 = w.reshape(w.shape[0], MLA_HEADS, MLA_NOPE + MLA_V)
    k = jnp.pad(w3[:, :, :MLA_NOPE], ((0, 0), (0, 0), (0, LANE - MLA_NOPE))).reshape(w.shape[0], MLA_HEADS * LANE)
    return jnp.concatenate([k, w3[:, :, MLA_NOPE:].reshape(w.shape[0], MLA_HEADS * MLA_V)], axis=1)


def _unpack_w_kv(p):
    k = p[:, :MLA_HEADS * LANE].reshape(p.shape[0], MLA_HEADS, LANE)[:, :, :MLA_NOPE]
    v = p[:, MLA_HEADS * LANE:].reshape(p.shape[0], MLA_HEADS, MLA_V)
    return jnp.concatenate([k, v], axis=2).reshape(p.shape[0], -1)


def _rows8(rows, width):
    a = jnp.concatenate([r.reshape(-1, width) for r in rows], axis=0)
    return jnp.pad(a, ((0, 8 - a.shape[0]), (0, 0)))


SLAB_ROWS = 256


def _to_slab(flat_list, lead=()):
    cat = jnp.concatenate(flat_list, axis=-1)
    n = cat.shape[-1]
    unit = SLAB_ROWS * PACK_W
    total = -(-n // unit) * unit
    cat = jnp.pad(cat, [(0, 0)] * len(lead) + [(0, total - n)])
    return cat.reshape(lead + (total // PACK_W, PACK_W))


def _from_flat(flat, shapes):
    out, off = [], 0
    for s in shapes:
        n = int(np.prod(s))
        out.append(flat[off:off + n].reshape(s))
        off += n
    return out


def _gathered_full(g8, axis):
    moved = jnp.moveaxis(g8, 0, axis)
    shp = moved.shape
    return moved.reshape(shp[:axis] + (shp[axis] * shp[axis + 1],) + shp[axis + 2:])


def _per_device(full, axis):
    shp = full.shape
    split = full.reshape(shp[:axis] + (N_DEV, shp[axis] // N_DEV) + shp[axis + 1:])
    return jnp.moveaxis(split, axis, 0)


ARG_NAMES = ['x', 'meta_tokens', 'mix_pre_g', 'mix_post_g', 'mlp_pre_g', 'mlp_post_g', 'w_up', 'w_down', 'w_in',
             'ssd_conv_w', 'ssd_conv_b', 'ssd_dt_bias', 'ssd_a_log', 'ssd_d', 'ssd_norm_g', 'mla_q_norm_g',
             'mla_w_q_up', 'mla_kv_norm_g', 'mla_w_kv_up', 'w_out_ab', 'rg_w_x', 'rg_w_y', 'rg_conv_w', 'rg_conv_b',
             'rg_w_a', 'rg_b_a', 'rg_w_i', 'rg_b_i', 'rg_lambda', 'rg_w_out']
WEIGHTS = ARG_NAMES[1:]
BIG = {'w_up': 2, 'w_down': 1, 'w_in': 2, 'mla_w_q_up': 2, 'mla_w_kv_up': 2, 'w_out_ab': 1, 'rg_w_x': 2,
       'rg_w_y': 2, 'rg_w_out': 1}
SMALL = {'meta_tokens': 1, 'ssd_conv_w': 2, 'rg_conv_w': 2, 'rg_conv_b': 1, 'rg_b_a': 1, 'rg_b_i': 1, 'rg_lambda': 1}
REPL = [n for n in WEIGHTS if n not in BIG and n not in SMALL]


def _piece_axes():
    axes = {}
    for n, ax in BIG.items():
        for i in range(DEPTH if n in ('w_up', 'w_down') else DEPTH // 2):
            axes[(n, i)] = ax - 1
    return axes


PIECE_AXIS = _piece_axes()
_SM = lambda i: [(n, i) for n in ('w_in', 'mla_w_q_up', 'mla_w_kv_up', 'w_out_ab')]
_RG = lambda i: [(n, i) for n in ('rg_w_x', 'rg_w_y', 'rg_w_out')]
_MLP = lambda l: [('w_up', l), ('w_down', l)]
GATHER_FIRST = _SM(0)
GATHER_REST = [k for k in PIECE_AXIS if k not in GATHER_FIRST]
SCATTER_AT = {2: _MLP(3) + _RG(1) + _MLP(2), 0: _SM(1) + _MLP(1) + _RG(0) + _MLP(0)}
SCATTER_LAST = _SM(0)


def _wire_block(p, key):
    n, i = key
    blk = p[n][i]
    if n == 'w_in':
        blk = jnp.pad(blk, ((0, 0), (0, W_IN_WIRE - blk.shape[1])))
    return blk


def _step(p, moments):
    assert DEPTH == 4
    wire_shape = {k: _wire_block(p, k).shape for k in PIECE_AXIS}
    full = {n: [None] * p[n].shape[0] for n in BIG}
    full['w_in_g'] = [None] * p['w_in'].shape[0]

    def weight_slab(group):
        return _to_slab([_wire_block(p, k).astype(BF).reshape(-1) for k in group])

    def take_weights(group, g8):
        for (n, i), piece in zip(group, _from_flat_rows(g8, [wire_shape[k] for k in group])):
            if n == 'w_in':
                full['w_in_g'][i] = piece
            else:
                full[n][i] = _gathered_full(piece, PIECE_AXIS[(n, i)])

    def grad_slab(group, gw):
        parts = []
        for k in group:
            g = gw[k] if k[0] == 'w_in' else _per_device(gw[k], PIECE_AXIS[k]).astype(BF)
            parts.append(g.reshape(N_DEV, -1))
        return _to_slab(parts, lead=(N_DEV,))

    g_blocks = {}

    def take_grads(group, flat):
        for k, blk in zip(group, _from_flat(flat, [wire_shape[k] for k in group])):
            g_blocks[k] = blk[:, :W_IN_SHARD] if k[0] == 'w_in' else blk

    small_slab = _to_slab([p[n].reshape(-1) for n in SMALL])
    first8, small8 = all_gather([weight_slab(GATHER_FIRST), small_slab], name="gather_first")
    take_weights(GATHER_FIRST, first8)
    for n, piece in zip(SMALL, _from_flat_rows(small8, [p[n].shape for n in SMALL])):
        full[n] = _gathered_full(piece, SMALL[n])
    for n in REPL:
        full[n] = p[n]
    loss_local, grad_x, gw, gsmall_full, carried = _local_step(
        full, p['x'][0], p['loss_target'][0],
        fwd_carry=Exchange("gather", [weight_slab(GATHER_REST)]),
        on_fwd_carried=lambda got: take_weights(GATHER_REST, got[0]),
        bwd_carry=lambda layer, gw_now: Exchange("scatter", [grad_slab(SCATTER_AT[layer], gw_now)]))

    for layer, group in SCATTER_AT.items():
        take_grads(group, slab_sum(carried[layer][0], name="sum_l%d" % layer).reshape(-1))
    rep_flat = jnp.concatenate([gsmall_full[n].reshape(-1) for n in REPL])
    rep_n = rep_flat.shape[0]
    rep_chunk = -(-rep_n // (N_DEV * PACK_W * 8)) * PACK_W * 8
    rep8 = jnp.pad(rep_flat, (0, N_DEV * rep_chunk - rep_n)).reshape(N_DEV, rep_chunk)
    gsmall = _to_slab([_per_device(gsmall_full[n], SMALL[n]).reshape(N_DEV, -1) for n in SMALL] + [rep8], lead=(N_DEV,))
    rlast, rsmall = all_to_all([grad_slab(SCATTER_LAST, gw), gsmall], name="scatter_last")
    take_grads(SCATTER_LAST, slab_sum(rlast, name="sum_last").reshape(-1))
    ssmall = slab_sum(rsmall, name="sum_small").reshape(-1)
    g_loc = {n: jnp.stack([g_blocks[(n, i)] for i in range(p[n].shape[0])], axis=0) for n in BIG}
    small_n = sum(int(np.prod(p[n].shape)) for n in SMALL)
    g_loc.update(zip(SMALL, _from_flat(ssmall, [p[n].shape for n in SMALL])))
    rep_mine = ssmall[small_n:small_n + rep_chunk].reshape(-1, PACK_W)
    (rep_all,) = all_gather([rep_mine], name="gather_replicated")
    g_loc.update(zip(REPL, _from_flat(rep_all.reshape(-1), [p[n].shape for n in REPL])))

    out = {'loss': lax.psum(loss_local, ("x", "y", "c")), 'grad_x': grad_x[None]}
    small_names = list(SMALL) + REPL
    for n in BIG:
        shp = p[n].shape
        v2 = lambda a: a.reshape(-1, shp[-1])
        d, nm, nv = adamw(v2(p[n]), v2(g_loc[n]), v2(moments['m_' + n]), v2(moments['v_' + n]), name="adamw_" + n)
        out['delta_' + n], out['new_m_' + n], out['new_v_' + n] = d.reshape(shp), nm.reshape(shp), nv.reshape(shp)
    slab = lambda src: _to_slab([src(n).reshape(-1) for n in small_names])
    d, nm, nv = adamw(slab(lambda n: p[n]), slab(lambda n: g_loc[n]), slab(lambda n: moments['m_' + n]),
                      slab(lambda n: moments['v_' + n]), name="adamw_small")
    shapes = [p[n].shape for n in small_names]
    for key, flat in (('delta_', d), ('new_m_', nm), ('new_v_', nv)):
        for n, a in zip(small_names, _from_flat(flat.reshape(-1), shapes)):
            out[key + n] = a
    for n in WEIGHTS:
        out['grad_' + n] = g_loc[n]
    return out


def _local_step(full, x, target_rows, fwd_carry=None, on_fwd_carried=None, bwd_carry=None):
    t = PAD + N_META + x.shape[0]
    h = jnp.concatenate([jnp.zeros((PAD, D_MODEL), F32), full['meta_tokens'], x], axis=0)
    target = jnp.concatenate([jnp.zeros((PAD + N_META, D_MODEL), F32), target_rows], axis=0)
    tabs = rope_tables(t)

    def layer_args(layer):
        i = layer // 2
        if layer % 2 == 0:
            convp = _rows8([full['ssd_conv_w'][i], full['ssd_conv_b'][i]], SSD_CONV_CH)
            return (full['mix_pre_g'][layer], full['mix_post_g'][layer], w_in_blocks_to_cols(full['w_in_g'][i]), convp,
                    full['ssd_dt_bias'][i], full['ssd_a_log'][i], full['ssd_d'][i], full['ssd_norm_g'][i],
                    full['mla_q_norm_g'][i], _pack_w_q(full['mla_w_q_up'][i]), full['mla_kv_norm_g'][i],
                    _pack_w_kv(full['mla_w_kv_up'][i]), full['w_out_ab'][i], tabs)
        rgp = _rows8([full['rg_conv_w'][i], full['rg_conv_b'][i], full['rg_b_a'][i], full['rg_b_i'][i],
                      full['rg_lambda'][i]], LRU_WIDTH)
        w_xy = jnp.concatenate([full['rg_w_x'][i], full['rg_w_y'][i]], axis=1)
        return (full['mix_pre_g'][layer], full['mix_post_g'][layer], w_xy, rgp, full['rg_w_a'][i], full['rg_w_i'][i],
                full['rg_w_out'][i])

    def mlp_args(layer):
        return (full['mlp_pre_g'][layer], full['mlp_post_g'][layer], full['w_up'][layer], full['w_down'][layer])

    saved = []
    for layer in range(DEPTH):
        la = layer_args(layer)
        if layer % 2 == 0:
            h, res_mix, got = sm_layer_fwd(h, *la, carry=fwd_carry if layer == 0 else None)
            if layer == 0 and fwd_carry is not None:
                on_fwd_carried(got)
        else:
            h, res_mix = rg_layer_fwd(h, *la)
        ma = mlp_args(layer)
        h, res_mlp = mlp_fwd(h, *ma)
        saved.append((la, ma, res_mix, res_mlp))
    loss_local, dh = loss_fwd_bwd(h, target)

    others = {n: [None] * len(full[n]) for n in WEIGHTS if n not in BIG and n != 'meta_tokens'}
    gw, carried = {}, {}
    for layer in reversed(range(DEPTH)):
        la, ma, res_mix, res_mlp = saved[layer]
        dh, gm = mlp_bwd(res_mlp, dh, *ma)
        if layer % 2 == 0:
            for n in ('w_up', 'w_down'):
                gw[(n, layer)] = gm[n]
            carry = bwd_carry(layer, gw) if bwd_carry is not None else None
            dh, gx, carried[layer] = sm_layer_bwd(res_mix, dh, *la, carry=carry)
        else:
            dh, gx = rg_layer_bwd(res_mix, dh, *la)
        for n, g in list(gm.items()) + list(gx.items()):
            i = layer if n in ('mix_pre_g', 'mix_post_g', 'mlp_pre_g', 'mlp_post_g', 'w_up', 'w_down') else layer // 2
            if n in BIG:
                gw[(n, i)] = g
            else:
                others[n][i] = g
    gothers = {n: jnp.stack(v, axis=0) for n, v in others.items()}
    gothers['meta_tokens'] = dh[PAD:PAD + N_META]
    return loss_local, dh[PAD + N_META:], gw, gothers, carried


def _from_flat_rows(g8, shapes):
    flat = g8.reshape(N_DEV, -1)
    out, off = [], 0
    for s in shapes:
        n = int(np.prod(s))
        out.append(flat[:, off:off + n].reshape((N_DEV,) + tuple(s)))
        off += n
    return out


def kernel(x, meta_tokens, mix_pre_g, mix_post_g, mlp_pre_g, mlp_post_g, w_up, w_down, w_in, ssd_conv_w, ssd_conv_b, ssd_dt_bias, ssd_a_log, ssd_d, ssd_norm_g, mla_q_norm_g, mla_w_q_up, mla_kv_norm_g, mla_w_kv_up, w_out_ab, rg_w_x, rg_w_y, rg_conv_w, rg_conv_b, rg_w_a, rg_b_a, rg_w_i, rg_b_i, rg_lambda, rg_w_out, loss_target, m_meta_tokens, m_mix_pre_g, m_mix_post_g, m_mlp_pre_g, m_mlp_post_g, m_w_up, m_w_down, m_w_in, m_ssd_conv_w, m_ssd_conv_b, m_ssd_dt_bias, m_ssd_a_log, m_ssd_d, m_ssd_norm_g, m_mla_q_norm_g, m_mla_w_q_up, m_mla_kv_norm_g, m_mla_w_kv_up, m_w_out_ab, m_rg_w_x, m_rg_w_y, m_rg_conv_w, m_rg_conv_b, m_rg_w_a, m_rg_b_a, m_rg_w_i, m_rg_b_i, m_rg_lambda, m_rg_w_out, v_meta_tokens, v_mix_pre_g, v_mix_post_g, v_mlp_pre_g, v_mlp_post_g, v_w_up, v_w_down, v_w_in, v_ssd_conv_w, v_ssd_conv_b, v_ssd_dt_bias, v_ssd_a_log, v_ssd_d, v_ssd_norm_g, v_mla_q_norm_g, v_mla_w_q_up, v_mla_kv_norm_g, v_mla_w_kv_up, v_w_out_ab, v_rg_w_x, v_rg_w_y, v_rg_conv_w, v_rg_conv_b, v_rg_w_a, v_rg_b_a, v_rg_w_i, v_rg_b_i, v_rg_lambda, v_rg_w_out):
    args = (x, meta_tokens, mix_pre_g, mix_post_g, mlp_pre_g, mlp_post_g, w_up, w_down, w_in, ssd_conv_w, ssd_conv_b, ssd_dt_bias, ssd_a_log, ssd_d, ssd_norm_g, mla_q_norm_g, mla_w_q_up, mla_kv_norm_g, mla_w_kv_up, w_out_ab, rg_w_x, rg_w_y, rg_conv_w, rg_conv_b, rg_w_a, rg_b_a, rg_w_i, rg_b_i, rg_lambda, rg_w_out, loss_target, m_meta_tokens, m_mix_pre_g, m_mix_post_g, m_mlp_pre_g, m_mlp_post_g, m_w_up, m_w_down, m_w_in, m_ssd_conv_w, m_ssd_conv_b, m_ssd_dt_bias, m_ssd_a_log, m_ssd_d, m_ssd_norm_g, m_mla_q_norm_g, m_mla_w_q_up, m_mla_kv_norm_g, m_mla_w_kv_up, m_w_out_ab, m_rg_w_x, m_rg_w_y, m_rg_conv_w, m_rg_conv_b, m_rg_w_a, m_rg_b_a, m_rg_w_i, m_rg_b_i, m_rg_lambda, m_rg_w_out, v_meta_tokens, v_mix_pre_g, v_mix_post_g, v_mlp_pre_g, v_mlp_post_g, v_w_up, v_w_down, v_w_in, v_ssd_conv_w, v_ssd_conv_b, v_ssd_dt_bias, v_ssd_a_log, v_ssd_d, v_ssd_norm_g, v_mla_q_norm_g, v_mla_w_q_up, v_mla_kv_norm_g, v_mla_w_kv_up, v_w_out_ab, v_rg_w_x, v_rg_w_y, v_rg_conv_w, v_rg_conv_b, v_rg_w_a, v_rg_b_a, v_rg_w_i, v_rg_b_i, v_rg_lambda, v_rg_w_out,)
    n_w = len(ARG_NAMES)
    p = dict(zip(ARG_NAMES, args[:n_w]))
    p['loss_target'] = args[n_w]
    moments = {}
    for i, n in enumerate(WEIGHTS):
        moments['m_' + n] = args[n_w + 1 + i]
        moments['v_' + n] = args[n_w + 1 + len(WEIGHTS) + i]
    out = _step(p, moments)
    res = [out['loss'], out['grad_x']]
    for prefix in ('grad_', 'delta_', 'new_m_', 'new_v_'):
        res += [out[prefix + n] for n in WEIGHTS]
    return tuple(res)
```

```python
import functools
import math

import numpy as np
import jax
import jax.numpy as jnp
from jax import lax
from jax.experimental import pallas as pl
from jax.experimental.pallas import tpu as pltpu

F32 = jnp.float32
BF = jnp.bfloat16
HI = lax.Precision.HIGHEST

D_MODEL = 1024
DEPTH = 4
N_META = 16
CHUNK = 128
PAD = CHUNK - N_META
EPS = 1e-6
SSD_HEADS = 16
SSD_HEAD_DIM = 64
SSD_D_INNER = 1024
SSD_STATE = 128
SSD_CONV_CH = 1536
MLA_HEADS = 16
MLA_NOPE = 64
MLA_ROPE = 32
MLA_V = 64
MLA_Q_RANK = 384
MLA_KV_RANK = 256
ROPE_BASE = 10000.0
LRU_WIDTH = 1280
LRU_BLOCKS = 10
LRU_C = 8.0
D_FF = 4096
N_DEV = 8
LANE = 128
IN_W = 3456
OFF_Z, OFF_XBC, OFF_CKV, OFF_DT, OFF_KR, OFF_CQ = 0, 1024, 2560, 2816, 2944, 3072

ADAM_LR = 0.001
ADAM_B1 = 0.9
ADAM_B2 = 0.999
ADAM_EPS = 1e-08
ADAM_WD = 0.01
ADAM_STEP = 10

VMEM_LIMIT = 56 * 1024 * 1024
NEG = -1e30


def _pick(n, cands):
    for c in cands:
        if n % c == 0:
            return c
    return n


def _cp(sem=None):
    return pltpu.CompilerParams(dimension_semantics=sem, vmem_limit_bytes=VMEM_LIMIT)


def _sds(shape, dtype):
    return jax.ShapeDtypeStruct(tuple(shape), dtype)


def _silu(x):
    return x * jax.nn.sigmoid(x)


def _softplus(x):
    return jnp.maximum(x, 0.0) + jnp.log(1.0 + jnp.exp(-jnp.abs(x)))


def _gelu(x):
    c = math.sqrt(2.0 / math.pi)
    return 0.5 * x * (1.0 + jnp.tanh(c * (x + 0.044715 * (x * x * x))))


def _row_mask(i, tr, shape, first_valid=PAD):
    row = i * tr + lax.broadcasted_iota(jnp.int32, shape, 0)
    return row >= first_valid


class KBlock:
    def __init__(self, arr, width, blk):
        self.arr, self.width, self.blk = arr, width, blk


class DevBlocks:
    def __init__(self, g8, axis):
        self.g8, self.axis = g8, axis
        _, r, c = g8.shape
        self.shape = (N_DEV * r, c) if axis == 0 else (r, N_DEV * c)


NN_DIMS = (((1,), (0,)), ((), ()))


def matmul(a, b, mode, out_dtypes=(F32,), epi=None, extras=(), name="mm", tm=None, tn=None, out_blocks=False):
    a_terms = a if isinstance(a, (list, tuple)) else [a]
    b_terms = b if isinstance(b, (list, tuple)) else [b]
    assert len(a_terms) == len(b_terms) and (mode != "tn" or len(a_terms) == 1)
    arr_of = lambda t: t.arr if isinstance(t, KBlock) else t
    if mode == "tn":
        m, n = a_terms[0].shape[1], b_terms[0].shape[1]
    else:
        m = arr_of(a_terms[0]).shape[0]
        b0 = b_terms[0]
        n = (b0.shape if isinstance(b0, DevBlocks) else arr_of(b0).shape)[1 if mode == "nn" else 0]
    tm = tm or _pick(m, (1056, 1024, 768, 640, 512, 384, 256, 128))
    tn = tn or _pick(n, (512, 640, 384, 256, 128))
    if mode == "tn":
        tm = _pick(m, (512, 384, 256, 128))
    dims = {"nn": NN_DIMS, "nt": NT_DIMS, "tn": TN_DIMS}[mode]

    in_specs, args, plan = [], [], []
    for ta, tb in zip(a_terms, b_terms):
        if mode == "tn":
            k = ta.shape[0]
            in_specs += [pl.BlockSpec((k, tm), lambda i, j: (0, i)), pl.BlockSpec((k, tn), lambda i, j: (0, j))]
            args += [ta, tb]
            plan.append(None)
            continue
        if isinstance(ta, KBlock):
            kw, ka = ta.width, ta.blk
            in_specs.append(pl.BlockSpec((tm, kw), lambda i, j, ka=ka: (i, ka)))
        else:
            kw = ta.shape[1]
            in_specs.append(pl.BlockSpec((tm, kw), lambda i, j: (i, 0)))
        args.append(arr_of(ta))
        if isinstance(tb, DevBlocks):
            _, r, c = tb.g8.shape
            split_k = tb.axis == (0 if mode == "nn" else 1)
            if split_k:
                kd = r if mode == "nn" else c
                assert kw == N_DEV * kd
                blk = (N_DEV, kd, tn) if mode == "nn" else (N_DEV, tn, kd)
                in_specs.append(pl.BlockSpec(blk, (lambda i, j: (0, 0, j)) if mode == "nn" else (lambda i, j: (0, j, 0))))
                plan.append(kd)
            else:
                per = (c if mode == "nn" else r) // tn
                blk = (None, kw, tn) if mode == "nn" else (None, tn, kw)
                in_specs.append(pl.BlockSpec(blk, (lambda i, j, per=per: (j // per, 0, j % per)) if mode == "nn"
                                             else (lambda i, j, per=per: (j // per, j % per, 0))))
                plan.append(None)
            args.append(tb.g8)
        else:
            kb = tb.blk if isinstance(tb, KBlock) else 0
            assert (tb.width if isinstance(tb, KBlock) else tb.shape[0 if mode == "nn" else 1]) == kw
            in_specs.append(pl.BlockSpec((kw, tn), lambda i, j, kb=kb: (kb, j)) if mode == "nn"
                            else pl.BlockSpec((tn, kw), lambda i, j, kb=kb: (j, kb)))
            args.append(arr_of(tb))
            plan.append(None)
    n_terms, n_ex = len(plan), len(extras)

    def body(*refs):
        ex_refs, out_refs = refs[2 * n_terms:2 * n_terms + n_ex], refs[2 * n_terms + n_ex:]
        acc = None
        for t, kd in enumerate(plan):
            a_ref, b_ref = refs[2 * t], refs[2 * t + 1]
            if kd is None:
                parts = [lax.dot_general(a_ref[...].astype(BF), b_ref[...].astype(BF), dims, preferred_element_type=F32)]
            else:
                parts = [lax.dot_general(a_ref[:, d * kd:(d + 1) * kd].astype(BF), b_ref[d].astype(BF), dims,
                                         preferred_element_type=F32) for d in range(N_DEV)]
            for part in parts:
                acc = part if acc is None else acc + part
        outs = (acc,) if epi is None else epi(acc, *[r[...] for r in ex_refs])
        for r, o in zip(out_refs, outs):
            r[...] = o.astype(r.dtype)

    o_spec = pl.BlockSpec((tm, tn), lambda i, j: (i, j))
    if out_blocks:
        per = n // N_DEV // tn
        out_shape = tuple(_sds((N_DEV, m, n // N_DEV), dt) for dt in out_dtypes)
        out_specs = tuple(pl.BlockSpec((None, tm, tn), lambda i, j: (j // per, i, j % per)) for _ in out_dtypes)
    else:
        out_shape = tuple(_sds((m, n), dt) for dt in out_dtypes)
        out_specs = tuple(o_spec for _ in out_dtypes)
    outs = pl.pallas_call(
        body,
        out_shape=out_shape,
        grid=(m // tm, n // tn),
        in_specs=in_specs + [o_spec] * n_ex,
        out_specs=out_specs,
        compiler_params=_cp(("parallel", "parallel")),
        name=name,
    )(*args, *extras)
    return outs[0] if len(out_dtypes) == 1 else outs


def _rt(t):
    return _pick(t, (384, 256, 128))


def norm_fwd(x, g, out_dtype, col_blk=0, width=None, name="norm_fwd"):
    t = x.shape[0]
    w = width or x.shape[1]
    tr = _rt(t)

    def body(x_ref, g_ref, o_ref):
        xv = x_ref[...]
        r = lax.rsqrt(jnp.mean(xv * xv, axis=-1, keepdims=True) + EPS)
        o_ref[...] = (xv * r * g_ref[...]).astype(o_ref.dtype)

    return pl.pallas_call(
        body,
        out_shape=_sds((t, w), out_dtype),
        grid=(t // tr,),
        in_specs=[pl.BlockSpec((tr, w), lambda i: (i, col_blk)), pl.BlockSpec((1, w), lambda i: (0, 0))],
        out_specs=pl.BlockSpec((tr, w), lambda i: (i, 0)),
        compiler_params=_cp(("parallel",)),
        name=name,
    )(x, g.reshape(1, w))


def norm_bwd(x, g, dy, dres=None, mask_pad=False, out_dtype=F32, col_blk=0, width=None, dy_col_blk=0, name="norm_bwd"):
    t = x.shape[0]
    w = width or x.shape[1]
    tr = _rt(t)
    has_res = dres is not None

    def body(x_ref, g_ref, dy_ref, *rest):
        if has_res:
            res_ref, dx_ref, dg_ref = rest
        else:
            dx_ref, dg_ref = rest
        i = pl.program_id(0)
        xv = x_ref[...]
        dyv = dy_ref[...].astype(F32)
        if mask_pad:
            dyv = jnp.where(_row_mask(i, tr, dyv.shape), dyv, 0.0)
        r = lax.rsqrt(jnp.mean(xv * xv, axis=-1, keepdims=True) + EPS)
        xh = xv * r
        dyg = dyv * g_ref[...]
        dx = r * (dyg - xh * jnp.mean(dyg * xh, axis=-1, keepdims=True))
        if has_res:
            dx = dx + res_ref[...]
        dx_ref[...] = dx.astype(dx_ref.dtype)

        @pl.when(i == 0)
        def _():
            dg_ref[...] = jnp.zeros_like(dg_ref)

        dg_ref[...] += jnp.sum(dyv * xh, axis=0, keepdims=True)

    in_specs = [pl.BlockSpec((tr, w), lambda i: (i, col_blk)), pl.BlockSpec((1, w), lambda i: (0, 0)),
                pl.BlockSpec((tr, w), lambda i: (i, dy_col_blk))]
    args = [x, g.reshape(1, w), dy]
    if has_res:
        in_specs.append(pl.BlockSpec((tr, w), lambda i: (i, 0)))
        args.append(dres)
    dx, dg = pl.pallas_call(
        body,
        out_shape=(_sds((t, w), out_dtype), _sds((1, w), F32)),
        grid=(t // tr,),
        in_specs=in_specs,
        out_specs=(pl.BlockSpec((tr, w), lambda i: (i, 0)), pl.BlockSpec((1, w), lambda i: (0, 0))),
        compiler_params=_cp(("arbitrary",)),
        name=name,
    )(*args)
    return dx, dg.reshape(w)


def resadd_fwd(h, m, g, name="resadd"):
    t, w = h.shape
    tr = _rt(t)

    def body(h_ref, m_ref, g_ref, o_ref):
        mv = m_ref[...]
        r = lax.rsqrt(jnp.mean(mv * mv, axis=-1, keepdims=True) + EPS)
        y = mv * r * g_ref[...]
        o_ref[...] = h_ref[...] + jnp.where(_row_mask(pl.program_id(0), tr, y.shape), y, 0.0)

    return pl.pallas_call(
        body,
        out_shape=_sds((t, w), F32),
        grid=(t // tr,),
        in_specs=[pl.BlockSpec((tr, w), lambda i: (i, 0)), pl.BlockSpec((tr, w), lambda i: (i, 0)),
                  pl.BlockSpec((1, w), lambda i: (0, 0))],
        out_specs=pl.BlockSpec((tr, w), lambda i: (i, 0)),
        compiler_params=_cp(("parallel",)),
        name=name,
    )(h, m, g.reshape(1, w))


def loss_fwd_bwd(h, target):
    t, w = h.shape
    tr = _rt(t)

    def body(h_ref, t_ref, s_ref, dh_ref):
        i = pl.program_id(0)
        err = h_ref[...] - t_ref[...]
        err = jnp.where(_row_mask(i, tr, err.shape, PAD + N_META), err, 0.0)
        dh_ref[...] = err * (1.0 / w)

        @pl.when(i == 0)
        def _():
            s_ref[...] = jnp.zeros_like(s_ref)

        s_ref[...] += jnp.sum(err * err).reshape(1, 1)

    s, dh = pl.pallas_call(
        body,
        out_shape=(_sds((1, LANE), F32), _sds((t, w), F32)),
        grid=(t // tr,),
        in_specs=[pl.BlockSpec((tr, w), lambda i: (i, 0)), pl.BlockSpec((tr, w), lambda i: (i, 0))],
        out_specs=(pl.BlockSpec((1, LANE), lambda i: (0, 0)), pl.BlockSpec((tr, w), lambda i: (i, 0))),
        compiler_params=_cp(("arbitrary",)),
        name="loss",
    )(h, target)
    return 0.5 * s[0, 0] / w, dh


def _shift_down(ext, k, n):
    return pltpu.roll(ext, k, 0)[8:]


def _conv_pre(ext, x, w_ref, n):
    return (w_ref[4:5, :] + w_ref[3:4, :] * x + w_ref[2:3, :] * _shift_down(ext, 1, n)
            + w_ref[1:2, :] * _shift_down(ext, 2, n) + w_ref[0:1, :] * _shift_down(ext, 3, n))


def _conv_bwd_parts(dpre, dnext, x, ext, w_ref, n):
    extd = jnp.concatenate([dpre, dnext], axis=0)
    ln = n + 8
    dx = (w_ref[3:4, :] * dpre + w_ref[2:3, :] * pltpu.roll(extd, ln - 1, 0)[:n]
          + w_ref[1:2, :] * pltpu.roll(extd, ln - 2, 0)[:n] + w_ref[0:1, :] * pltpu.roll(extd, ln - 3, 0)[:n])
    sums = [jnp.sum(dpre * _shift_down(ext, 3, n), axis=0, keepdims=True),
            jnp.sum(dpre * _shift_down(ext, 2, n), axis=0, keepdims=True),
            jnp.sum(dpre * _shift_down(ext, 1, n), axis=0, keepdims=True),
            jnp.sum(dpre * x, axis=0, keepdims=True),
            jnp.sum(dpre, axis=0, keepdims=True)]
    return dx, sums


def _rows_block(sums):
    row = lax.broadcasted_iota(jnp.int32, (8, LANE), 0)
    out = jnp.zeros((8, LANE), F32)
    for k, s in enumerate(sums):
        out = jnp.where(row == k, s, out)
    return out


def conv_silu_fwd(x, col0_blk, nblk, wb, name="conv_fwd"):
    t = x.shape[0]
    c = nblk * LANE
    tr = _rt(t)

    def body(x_ref, w_ref, o_ref, prev):
        ti = pl.program_id(1)

        @pl.when(ti == 0)
        def _():
            prev[...] = jnp.zeros_like(prev)

        xv = x_ref[...]
        ext = jnp.concatenate([prev[...], xv], axis=0)
        o_ref[...] = _silu(_conv_pre(ext, xv, w_ref, tr))
        prev[...] = xv[tr - 8:, :]

    return pl.pallas_call(
        body,
        out_shape=_sds((t, c), F32),
        grid=(nblk, t // tr),
        in_specs=[pl.BlockSpec((tr, LANE), lambda cb, ti: (ti, col0_blk + cb)),
                  pl.BlockSpec((8, LANE), lambda cb, ti: (0, cb))],
        out_specs=pl.BlockSpec((tr, LANE), lambda cb, ti: (ti, cb)),
        scratch_shapes=[pltpu.VMEM((8, LANE), F32)],
        compiler_params=_cp(("parallel", "arbitrary")),
        name=name,
    )(x, wb)


def conv_silu_bwd(x, col0_blk, nblk, wb, dout, name="conv_bwd"):
    t = x.shape[0]
    c = nblk * LANE
    tr = _rt(t)
    nt = t // tr
    r8 = tr // 8

    def body(x_ref, xp_ref, w_ref, do_ref, dx_ref, dwb_ref, dnext):
        ti = pl.program_id(1)
        tt = nt - 1 - ti

        @pl.when(ti == 0)
        def _():
            dnext[...] = jnp.zeros_like(dnext)
            dwb_ref[...] = jnp.zeros_like(dwb_ref)

        xv = x_ref[...]
        halo = jnp.where(tt > 0, xp_ref[...], 0.0)
        ext = jnp.concatenate([halo, xv], axis=0)
        pre = _conv_pre(ext, xv, w_ref, tr)
        s = jax.nn.sigmoid(pre)
        dpre = do_ref[...] * (s + pre * s * (1.0 - s))
        dx, sums = _conv_bwd_parts(dpre, dnext[...], xv, ext, w_ref, tr)
        dx_ref[...] = dx.astype(dx_ref.dtype)
        dwb_ref[...] += _rows_block(sums)
        dnext[...] = dpre[:8, :]

    return pl.pallas_call(
        body,
        out_shape=(_sds((t, c), BF), _sds((8, c), F32)),
        grid=(nblk, nt),
        in_specs=[pl.BlockSpec((tr, LANE), lambda cb, ti: (nt - 1 - ti, col0_blk + cb)),
                  pl.BlockSpec((8, LANE), lambda cb, ti: (jnp.maximum((nt - 1 - ti) * r8 - 1, 0), col0_blk + cb)),
                  pl.BlockSpec((8, LANE), lambda cb, ti: (0, cb)),
                  pl.BlockSpec((tr, LANE), lambda cb, ti: (nt - 1 - ti, cb))],
        out_specs=(pl.BlockSpec((tr, LANE), lambda cb, ti: (nt - 1 - ti, cb)),
                   pl.BlockSpec((8, LANE), lambda cb, ti: (0, cb))),
        scratch_shapes=[pltpu.VMEM((8, LANE), F32)],
        compiler_params=_cp(("parallel", "arbitrary")),
        name=name,
    )(x, x, wb, dout)


def gated_norm_fwd(y, proj, g, name="gnorm_fwd"):
    t, w = y.shape
    tr = _rt(t)

    def body(y_ref, z_ref, g_ref, o_ref):
        v = y_ref[...] * _silu(z_ref[...])
        r = lax.rsqrt(jnp.mean(v * v, axis=-1, keepdims=True) + EPS)
        o_ref[...] = (v * r * g_ref[...]).astype(o_ref.dtype)

    return pl.pallas_call(
        body,
        out_shape=_sds((t, w), BF),
        grid=(t // tr,),
        in_specs=[pl.BlockSpec((tr, w), lambda i: (i, 0)), pl.BlockSpec((tr, w), lambda i: (i, OFF_Z // w)),
                  pl.BlockSpec((1, w), lambda i: (0, 0))],
        out_specs=pl.BlockSpec((tr, w), lambda i: (i, 0)),
        compiler_params=_cp(("parallel",)),
        name=name,
    )(y, proj, g.reshape(1, w))


def gated_norm_bwd(y, proj, g, dyab, name="gnorm_bwd"):
    t, w = y.shape
    tr = _rt(t)

    def body(y_ref, z_ref, g_ref, do_ref, dy_ref, dz_ref, dg_ref):
        i = pl.program_id(0)
        yv, zv, dov = y_ref[...], z_ref[...], do_ref[...]
        s = jax.nn.sigmoid(zv)
        sz = zv * s
        v = yv * sz
        r = lax.rsqrt(jnp.mean(v * v, axis=-1, keepdims=True) + EPS)
        vh = v * r
        dvg = dov * g_ref[...]
        dv = r * (dvg - vh * jnp.mean(dvg * vh, axis=-1, keepdims=True))
        dy_ref[...] = dv * sz
        dz_ref[...] = (dv * yv * (s + sz * (1.0 - s))).astype(dz_ref.dtype)

        @pl.when(i == 0)
        def _():
            dg_ref[...] = jnp.zeros_like(dg_ref)

        dg_ref[...] += jnp.sum(dov * vh, axis=0, keepdims=True)

    dy, dz, dg = pl.pallas_call(
        body,
        out_shape=(_sds((t, w), F32), _sds((t, w), BF), _sds((1, w), F32)),
        grid=(t // tr,),
        in_specs=[pl.BlockSpec((tr, w), lambda i: (i, 0)), pl.BlockSpec((tr, w), lambda i: (i, OFF_Z // w)),
                  pl.BlockSpec((1, w), lambda i: (0, 0)), pl.BlockSpec((tr, w), lambda i: (i, 0))],
        out_specs=(pl.BlockSpec((tr, w), lambda i: (i, 0)), pl.BlockSpec((tr, w), lambda i: (i, 0)),
                   pl.BlockSpec((1, w), lambda i: (0, 0))),
        compiler_params=_cp(("arbitrary",)),
        name=name,
    )(y, proj, g.reshape(1, w), dyab)
    return dy, dz, dg.reshape(w)


def rope_tables(t):
    inv = ROPE_BASE ** (-jnp.arange(0, MLA_ROPE, 2, dtype=F32) / MLA_ROPE)
    pos = (jnp.arange(t, dtype=F32) - PAD)[:, None]
    ang = pos * inv[None, :]
    cos, sin = jnp.cos(ang), jnp.sin(ang)
    z16 = jnp.zeros((t, 16), F32)
    z32 = jnp.zeros((t, 32), F32)
    c = jnp.concatenate([jnp.ones((t, 64), F32), cos, cos, z32], axis=1)
    s1 = jnp.concatenate([jnp.zeros((t, 64), F32), z16, sin, z32], axis=1)
    s2 = jnp.concatenate([jnp.zeros((t, 64), F32), -sin, z16, z32], axis=1)
    return c, s1, s2


def _rope(x, c, s1, s2):
    return x * c + pltpu.roll(x, 16, 1) * s1 + pltpu.roll(x, LANE - 16, 1) * s2


def _rope_t(d, c, s1, s2):
    return d * c + pltpu.roll(d * s1, LANE - 16, 1) + pltpu.roll(d * s2, 16, 1)


def rope_fwd(q_raw, kv_raw, proj, tabs):
    t = q_raw.shape[0]
    tr = _rt(t)
    hw = MLA_HEADS * LANE

    def body(q_ref, k_ref, v_ref, kr_ref, c_ref, s1_ref, s2_ref, qo_ref, ko_ref, vo_ref):
        c, s1, s2 = c_ref[...], s1_ref[...], s2_ref[...]
        kr = _rope(kr_ref[...], c, s1, s2)
        for h in range(MLA_HEADS):
            sl = slice(h * LANE, (h + 1) * LANE)
            qo_ref[:, sl] = (_rope(q_ref[:, sl], c, s1, s2) * Q_PRESCALE).astype(BF)
            ko_ref[:, sl] = (k_ref[:, sl] + kr).astype(BF)
        vo_ref[...] = v_ref[...].astype(BF)

    tab_spec = pl.BlockSpec((tr, LANE), lambda i: (i, 0))
    return pl.pallas_call(
        body,
        out_shape=(_sds((t, hw), BF), _sds((t, hw), BF), _sds((t, 1024), BF)),
        grid=(t // tr,),
        in_specs=[pl.BlockSpec((tr, hw), lambda i: (i, 0)), pl.BlockSpec((tr, hw), lambda i: (i, 0)),
                  pl.BlockSpec((tr, 1024), lambda i: (i, 2)), pl.BlockSpec((tr, LANE), lambda i: (i, OFF_KR // LANE)),
                  tab_spec, tab_spec, tab_spec],
        out_specs=(pl.BlockSpec((tr, hw), lambda i: (i, 0)), pl.BlockSpec((tr, hw), lambda i: (i, 0)),
                   pl.BlockSpec((tr, 1024), lambda i: (i, 0))),
        compiler_params=_cp(("parallel",)),
        name="rope_fwd",
    )(q_raw, kv_raw, kv_raw, proj, *tabs)


def rope_bwd(dq_cat, dk_cat, tabs):
    t = dq_cat.shape[0]
    tr = _rt(t)
    hw = MLA_HEADS * LANE

    def body(dq_ref, dk_ref, c_ref, s1_ref, s2_ref, dqo_ref, dkr_ref):
        c, s1, s2 = c_ref[...], s1_ref[...], s2_ref[...]
        acc = jnp.zeros((tr, LANE), F32)
        for h in range(MLA_HEADS):
            sl = slice(h * LANE, (h + 1) * LANE)
            dqo_ref[:, sl] = _rope_t(dq_ref[:, sl] * ATT_SCALE, c, s1, s2).astype(BF)
            acc = acc + dk_ref[:, sl]
        lane = lax.broadcasted_iota(jnp.int32, (tr, LANE), 1)
        dkr_ref[...] = jnp.where((lane >= 64) & (lane < 96), _rope_t(acc, c, s1, s2), 0.0)

    tab_spec = pl.BlockSpec((tr, LANE), lambda i: (i, 0))
    return pl.pallas_call(
        body,
        out_shape=(_sds((t, hw), BF), _sds((t, LANE), F32)),
        grid=(t // tr,),
        in_specs=[pl.BlockSpec((tr, hw), lambda i: (i, 0)), pl.BlockSpec((tr, hw), lambda i: (i, 0)),
                  tab_spec, tab_spec, tab_spec],
        out_specs=(pl.BlockSpec((tr, hw), lambda i: (i, 0)), pl.BlockSpec((tr, LANE), lambda i: (i, 0))),
        compiler_params=_cp(("parallel",)),
        name="rope_bwd",
    )(dq_cat, dk_cat, *tabs)


ATT_SCALE = (MLA_NOPE + MLA_ROPE) ** -0.5
LOG2E = math.log2(math.e)
Q_PRESCALE = ATT_SCALE * LOG2E
NT_DIMS = (((1,), (1,)), ((), ()))
TN_DIMS = (((0,), (0,)), ((), ()))


def _att_mask(qi, ki, tq, tk):
    qpos = qi * tq + lax.broadcasted_iota(jnp.int32, (tq, tk), 0)
    kpos = ki * tk + lax.broadcasted_iota(jnp.int32, (tq, tk), 1)
    return (kpos <= qpos) & (kpos >= PAD)


def _half_masks(n):
    lane = lax.broadcasted_iota(jnp.int32, (n, LANE), 1)
    return lane < 64, lane >= 64


def _att_tile(t):
    return _pick(t, (384, 256, 128))


def _ds(i, n):
    return pl.ds(i * n, n) if isinstance(i, int) else pl.ds(pl.multiple_of(i * n, n), n)


def attn_fwd(q_cat, k_cat, v, carry=None):
    t = q_cat.shape[0]
    tq = tk = _att_tile(t)
    nq = t // tq
    n_pair = MLA_HEADS // 2
    nx = carry.k if carry else 0

    def body(*refs):
        q_ref, k_ref, v_ref = refs[:3]
        o_ref, lse_ref = refs[3 + nx:5 + nx]
        qi = pl.program_id(1)
        if carry:
            start, middle, finish = carry.phases(refs[3:3 + nx], refs[5 + nx:5 + 2 * nx], refs[5 + 2 * nx:])
            pair = pl.program_id(0)
            pl.when((pair == 0) & (qi == 0))(start)
            pl.when((pair == n_pair // 2) & (qi == 0))(middle)
        lo_q, _ = _half_masks(tq)
        halves = _half_masks(tk)

        def step(ki, state, masked):
            m_old, l_old, acc = state[0:2], state[2:4], state[4]
            rows = _ds(ki, tk)
            vv = v_ref[rows, :]
            ss = [lax.dot_general(q_ref[:, hh * LANE:(hh + 1) * LANE], k_ref[rows, hh * LANE:(hh + 1) * LANE], NT_DIMS,
                                  preferred_element_type=F32) for hh in range(2)]
            if masked:
                valid = _att_mask(qi, ki, tq, tk)
                ss = [jnp.where(valid, s, NEG) for s in ss]
            m_new = [jnp.maximum(m_old[hh], jnp.max(ss[hh], axis=-1, keepdims=True)) for hh in range(2)]
            ps = [jnp.exp2(ss[hh] - m_new[hh]) for hh in range(2)]
            alpha = [jnp.exp2(m_old[hh] - m_new[hh]) for hh in range(2)]
            l_new = [alpha[hh] * l_old[hh] + jnp.sum(ps[hh], axis=-1, keepdims=True) for hh in range(2)]
            pv = [jnp.dot(ps[hh].astype(BF), jnp.where(halves[hh], vv, jnp.zeros_like(vv)), preferred_element_type=F32)
                  for hh in range(2)]
            acc = acc * jnp.where(lo_q, alpha[0], alpha[1]) + pv[0] + pv[1]
            return m_new[0], m_new[1], l_new[0], l_new[1], acc

        neg, zero = jnp.full((tq, 1), NEG, F32), jnp.zeros((tq, 1), F32)
        state = step(0, (neg, neg, zero, zero, jnp.zeros((tq, LANE), F32)), True)
        state = lax.fori_loop(1, qi, lambda ki, st: step(ki, st, False), state)
        state = lax.cond(qi > 0, lambda st: step(qi, st, True), lambda st: st, state)
        m0, m1, l0, l1, acc = state
        l = jnp.where(lo_q, l0, l1)
        o_ref[...] = (acc / l).astype(o_ref.dtype)
        lse_ref[...] = jnp.where(lo_q, m0, m1) + jnp.log2(l)
        if carry:
            pl.when((pair == n_pair - 1) & (qi == nq - 1))(finish)

    outs = pl.pallas_call(
        body,
        out_shape=(_sds((t, 1024), BF), _sds((t, 1024), F32)) + tuple(carry.out_shapes() if carry else ()),
        grid=(n_pair, nq),
        in_specs=[pl.BlockSpec((tq, 2 * LANE), lambda p, qi: (qi, p)),
                  pl.BlockSpec((t, 2 * LANE), lambda p, qi: (0, p)),
                  pl.BlockSpec((t, LANE), lambda p, qi: (0, p))] + [ANY] * nx,
        out_specs=(pl.BlockSpec((tq, LANE), lambda p, qi: (qi, p)),
                   pl.BlockSpec((tq, LANE), lambda p, qi: (qi, p))) + (ANY,) * nx,
        scratch_shapes=carry.scratch() if carry else [],
        compiler_params=_cp(("arbitrary", "arbitrary") if carry else ("parallel", "parallel")),
        name="attn_fwd_carrying" if carry else "attn_fwd",
    )(q_cat, k_cat, v, *(carry.arrs if carry else ()))
    return outs[0], outs[1], list(outs[2:])


def attn_bwd(q_cat, k_cat, v, o, lse, dyab, carry=None):
    t = q_cat.shape[0]
    tq = tk = _att_tile(t)
    nq = t // tq
    n_pair = MLA_HEADS // 2
    nx = carry.k if carry else 0

    def body(*refs):
        q_ref, k_ref, v_ref, o_ref, lse_ref, do_ref = refs[:6]
        dq_ref, dk_ref, dv_ref = refs[6 + nx:9 + nx]
        ki = pl.program_id(1)
        if carry:
            start, middle, finish = carry.phases(refs[6:6 + nx], refs[9 + nx:9 + 2 * nx], refs[9 + 2 * nx:])
            pair = pl.program_id(0)
            pl.when((pair == 0) & (ki == 0))(start)
            pl.when((pair == n_pair // 2) & (ki == 0))(middle)

        @pl.when(ki == 0)
        def _():
            dq_ref[...] = jnp.zeros_like(dq_ref)

        halves = _half_masks(tq)
        vv = v_ref[...]
        kk = [k_ref[:, hh * LANE:(hh + 1) * LANE] for hh in range(2)]

        def step(qi, acc, masked):
            rows = _ds(qi, tq)
            dov, ov, lse_v = do_ref[rows, :], o_ref[rows, :].astype(F32), lse_ref[rows, :]
            qh = [q_ref[rows, hh * LANE:(hh + 1) * LANE] for hh in range(2)]
            ss = [lax.dot_general(qh[hh], kk[hh], NT_DIMS, preferred_element_type=F32) for hh in range(2)]
            if masked:
                valid = _att_mask(qi, ki, tq, tk)
                ss = [jnp.where(valid, s, NEG) for s in ss]
            ps = [jnp.exp2(ss[hh] - lse_v[:, 64 * hh:64 * hh + 1]) for hh in range(2)]
            dom = [jnp.where(halves[hh], dov, 0.0) for hh in range(2)]
            delta = [jnp.sum(dom[hh] * ov, axis=-1, keepdims=True) for hh in range(2)]
            dom = [d.astype(BF) for d in dom]
            dp = [lax.dot_general(dom[hh], vv, NT_DIMS, preferred_element_type=F32) for hh in range(2)]
            ds = [(ps[hh] * (dp[hh] - delta[hh])).astype(BF) for hh in range(2)]
            pb = [p.astype(BF) for p in ps]
            dv = (acc[2] + lax.dot_general(pb[0], dom[0], TN_DIMS, preferred_element_type=F32)
                  + lax.dot_general(pb[1], dom[1], TN_DIMS, preferred_element_type=F32))
            dk = [acc[hh] + lax.dot_general(ds[hh], qh[hh], TN_DIMS, preferred_element_type=F32) for hh in range(2)]
            for hh in range(2):
                dq_ref[rows, hh * LANE:(hh + 1) * LANE] += jnp.dot(ds[hh], kk[hh], preferred_element_type=F32)
            return dk[0], dk[1], dv

        zero = jnp.zeros((tk, LANE), F32)
        acc = step(ki, (zero, zero, zero), True)
        acc = lax.fori_loop(ki + 1, jnp.where(ki == 0, nq, ki + 1), lambda qi, a: step(qi, a, True), acc)
        acc = lax.fori_loop(ki + 1, jnp.where(ki == 0, ki + 1, nq), lambda qi, a: step(qi, a, False), acc)
        dk_ref[:, 0:LANE] = acc[0] * (1.0 / LOG2E)
        dk_ref[:, LANE:2 * LANE] = acc[1] * (1.0 / LOG2E)
        dv_ref[...] = acc[2]
        if carry:
            pl.when((pair == n_pair - 1) & (ki == nq - 1))(finish)

    full = lambda w, off=0: pl.BlockSpec((t, w), lambda p, ki: (0, p + off))
    blk = lambda w: pl.BlockSpec((tk, w), lambda p, ki: (ki, p))
    outs = pl.pallas_call(
        body,
        out_shape=(_sds((t, 2048), F32), _sds((t, 2048), F32), _sds((t, 1024), F32))
        + tuple(carry.out_shapes() if carry else ()),
        grid=(n_pair, nq),
        in_specs=[full(2 * LANE), blk(2 * LANE), blk(LANE), full(LANE), full(LANE), full(LANE, 8)] + [ANY] * nx,
        out_specs=(full(2 * LANE), blk(2 * LANE), blk(LANE)) + (ANY,) * nx,
        scratch_shapes=carry.scratch() if carry else [],
        compiler_params=_cp(("arbitrary", "arbitrary") if carry else ("parallel", "arbitrary")),
        name="attn_bwd_carrying" if carry else "attn_bwd",
    )(q_cat, k_cat, v, o, lse, dyab, *(carry.arrs if carry else ()))
    return outs[0], outs[1], outs[2], list(outs[3:])


N_PAIR = SSD_HEADS // 2


def _hdot(a, b):
    return jnp.dot(a, b, precision=HI, preferred_element_type=F32)


def _ssd_chunk(xs, bg, cg, dtraw, hin, dt_bias, a_log, dskip, rowmask):
    ln = CHUNK
    causal = lax.broadcasted_iota(jnp.int32, (ln, ln), 0) >= lax.broadcasted_iota(jnp.int32, (ln, ln), 1)
    ltri = causal.astype(F32)
    lane = lax.broadcasted_iota(jnp.int32, (ln, LANE), 1)
    halves = (lane < 64, lane >= 64)
    low_row = lax.broadcasted_iota(jnp.int32, (1, LANE), 1) < 64
    head_lane = lax.broadcasted_iota(jnp.int32, (1, SSD_HEADS), 1)
    head_row = lax.broadcasted_iota(jnp.int32, (SSD_HEADS, 1), 0)

    def col(a, h):
        return jnp.sum(jnp.where(head_lane == h, a, 0.0), axis=1, keepdims=True)

    dt = _softplus(dtraw + dt_bias) * rowmask
    da = dt * (-jnp.exp(a_log))
    acs = _hdot(ltri, da)
    acs_t = lax.dot_general(da, ltri, (((0,), (1,)), ((), ())), precision=HI, preferred_element_type=F32)
    tot = jnp.sum(da, axis=0, keepdims=True)
    bm = [b * rowmask for b in bg]
    cm = [c * rowmask for c in cg]
    cb = [lax.dot_general(cm[g].astype(BF), bm[g].astype(BF), NT_DIMS, preferred_element_type=F32) for g in range(2)]
    ys, hout = [], []
    for p in range(N_PAIR):
        g = p // (N_PAIR // 2)
        h0, h1 = 2 * p, 2 * p + 1
        xdt = xs[p] * jnp.where(halves[0], col(dt, h0), col(dt, h1))
        y = jnp.zeros((ln, LANE), F32)
        snew = jnp.zeros((ln, LANE), F32)
        for hh in range(2):
            a_col = col(acs, h0 + hh)
            a_row = jnp.sum(jnp.where(head_row == h0 + hh, acs_t, 0.0), axis=0, keepdims=True)
            dec = jnp.exp(jnp.where(causal, a_col - a_row, NEG))
            xm = jnp.where(halves[hh], xdt, 0.0).astype(BF)
            y = y + jnp.dot((cb[g] * dec).astype(BF), xm, preferred_element_type=F32)
            bd = bm[g] * jnp.exp(col(tot, h0 + hh) - a_col)
            snew = snew + lax.dot_general(bd.astype(BF), xm, TN_DIMS, preferred_element_type=F32)
        y_off = (jnp.dot(cm[g].astype(BF), hin[p].astype(BF), preferred_element_type=F32)
                 * jnp.exp(jnp.where(halves[0], col(acs, h0), col(acs, h1))))
        ys.append(y + y_off + jnp.where(low_row, col(dskip, h0), col(dskip, h1)) * xs[p])
        hout.append(jnp.exp(jnp.where(low_row, col(tot, h0), col(tot, h1))) * hin[p] + snew)
    return ys, hout


def _ssd_load(x_ref, dt_ref):
    xs = [x_ref[:, p * LANE:(p + 1) * LANE] for p in range(N_PAIR)]
    bg = [x_ref[:, SSD_D_INNER + g * LANE:SSD_D_INNER + (g + 1) * LANE] for g in range(2)]
    cg = [x_ref[:, SSD_D_INNER + (2 + g) * LANE:SSD_D_INNER + (3 + g) * LANE] for g in range(2)]
    return xs, bg, cg, dt_ref[:, 0:SSD_HEADS]


def _chunk_rowmask(c):
    return ((c * CHUNK + lax.broadcasted_iota(jnp.int32, (CHUNK, 1), 0)) >= PAD).astype(F32)


def ssd_fwd(xbc_c, proj, dt_bias, a_log, dskip):
    t = xbc_c.shape[0]
    nc = t // CHUNK

    def body(x_ref, dt_ref, dtb_ref, al_ref, d_ref, y_ref, hs_ref, h_s):
        c = pl.program_id(0)

        @pl.when(c == 0)
        def _():
            h_s[...] = jnp.zeros_like(h_s)

        xs, bg, cg, dtraw = _ssd_load(x_ref, dt_ref)
        hin = [h_s[p] for p in range(N_PAIR)]
        hs_ref[0] = h_s[...]
        ys, hout = _ssd_chunk(xs, bg, cg, dtraw, hin, dtb_ref[...], al_ref[...], d_ref[...], _chunk_rowmask(c))
        for p in range(N_PAIR):
            y_ref[:, p * LANE:(p + 1) * LANE] = ys[p]
            h_s[p] = hout[p]

    par = pl.BlockSpec((1, SSD_HEADS), lambda c: (0, 0))
    return pl.pallas_call(
        body,
        out_shape=(_sds((t, SSD_D_INNER), F32), _sds((nc, N_PAIR, CHUNK, LANE), F32)),
        grid=(nc,),
        in_specs=[pl.BlockSpec((CHUNK, SSD_CONV_CH), lambda c: (c, 0)),
                  pl.BlockSpec((CHUNK, LANE), lambda c: (c, OFF_DT // LANE)), par, par, par],
        out_specs=(pl.BlockSpec((CHUNK, SSD_D_INNER), lambda c: (c, 0)),
                   pl.BlockSpec((1, N_PAIR, CHUNK, LANE), lambda c: (c, 0, 0, 0))),
        scratch_shapes=[pltpu.VMEM((N_PAIR, CHUNK, LANE), F32)],
        compiler_params=_cp(("arbitrary",)),
        name="ssd_fwd",
    )(xbc_c, proj, dt_bias.reshape(1, -1), a_log.reshape(1, -1), dskip.reshape(1, -1))


def ssd_bwd(xbc_c, proj, dt_bias, a_log, dskip, hs, dy):
    t = xbc_c.shape[0]
    nc = t // CHUNK

    def body(x_ref, dt_ref, dtb_ref, al_ref, d_ref, hs_ref, dy_ref, dx_ref, ddt_ref, dpar_ref, dh_s):
        ci = pl.program_id(0)
        c = nc - 1 - ci

        @pl.when(ci == 0)
        def _():
            dh_s[...] = jnp.zeros_like(dh_s)
            dpar_ref[...] = jnp.zeros_like(dpar_ref)

        xs, bg, cg, dtraw = _ssd_load(x_ref, dt_ref)
        hin = [hs_ref[0, p] for p in range(N_PAIR)]
        rowmask = _chunk_rowmask(c)
        fn = lambda xs_, bg_, cg_, dtraw_, hin_, dtb_, al_, d_: _ssd_chunk(xs_, bg_, cg_, dtraw_, hin_, dtb_, al_, d_, rowmask)
        _, vjp = jax.vjp(fn, xs, bg, cg, dtraw, hin, dtb_ref[...], al_ref[...], d_ref[...])
        dys = [dy_ref[:, p * LANE:(p + 1) * LANE] for p in range(N_PAIR)]
        dhs = [dh_s[p] for p in range(N_PAIR)]
        dxs, dbg, dcg, ddtraw, dhin, ddtb, dal, dd = vjp((dys, dhs))
        for p in range(N_PAIR):
            dx_ref[:, p * LANE:(p + 1) * LANE] = dxs[p]
            dh_s[p] = dhin[p]
        for g in range(2):
            dx_ref[:, SSD_D_INNER + g * LANE:SSD_D_INNER + (g + 1) * LANE] = dbg[g]
            dx_ref[:, SSD_D_INNER + (2 + g) * LANE:SSD_D_INNER + (3 + g) * LANE] = dcg[g]
        ddt_ref[...] = jnp.zeros_like(ddt_ref)
        ddt_ref[:, 0:SSD_HEADS] = ddtraw
        dpar_ref[0:1, 0:SSD_HEADS] += ddtb
        dpar_ref[1:2, 0:SSD_HEADS] += dal
        dpar_ref[2:3, 0:SSD_HEADS] += dd

    par = pl.BlockSpec((1, SSD_HEADS), lambda ci: (0, 0))
    return pl.pallas_call(
        body,
        out_shape=(_sds((t, SSD_CONV_CH), F32), _sds((t, LANE), F32), _sds((8, LANE), F32)),
        grid=(nc,),
        in_specs=[pl.BlockSpec((CHUNK, SSD_CONV_CH), lambda ci: (nc - 1 - ci, 0)),
                  pl.BlockSpec((CHUNK, LANE), lambda ci: (nc - 1 - ci, OFF_DT // LANE)), par, par, par,
                  pl.BlockSpec((1, N_PAIR, CHUNK, LANE), lambda ci: (nc - 1 - ci, 0, 0, 0)),
                  pl.BlockSpec((CHUNK, SSD_D_INNER), lambda ci: (nc - 1 - ci, 0))],
        out_specs=(pl.BlockSpec((CHUNK, SSD_CONV_CH), lambda ci: (nc - 1 - ci, 0)),
                   pl.BlockSpec((CHUNK, LANE), lambda ci: (nc - 1 - ci, 0)),
                   pl.BlockSpec((8, LANE), lambda ci: (0, 0))),
        scratch_shapes=[pltpu.VMEM((N_PAIR, CHUNK, LANE), F32)],
        compiler_params=_cp(("arbitrary",)),
        name="ssd_bwd",
    )(xbc_c, proj, dt_bias.reshape(1, -1), a_log.reshape(1, -1), dskip.reshape(1, -1), hs, dy)


def _neg_expm1(y):
    series = -(y * (1.0 + y * (0.5 + y * (1.0 / 6.0 + y * (1.0 / 24.0 + y * (1.0 / 120.0))))))
    return jnp.where(y > -0.1, series, 1.0 - jnp.exp(y))


def _rg_pw(xr, wa, ba, wi, bi, lam, rowmask):
    xb = xr.astype(BF)
    r = jax.nn.sigmoid(jnp.dot(xb, wa.astype(BF), preferred_element_type=F32) + ba)
    i = jax.nn.sigmoid(jnp.dot(xb, wi.astype(BF), preferred_element_type=F32) + bi)
    log_a = -LRU_C * r * _softplus(-lam)
    a = jnp.exp(log_a)
    u = jnp.sqrt(_neg_expm1(2.0 * log_a)) * (i * xr) * rowmask
    return a, u


def _gelu_grad(x):
    c = math.sqrt(2.0 / math.pi)
    th = jnp.tanh(c * (x + 0.044715 * (x * x * x)))
    return 0.5 * (1.0 + th) + 0.5 * x * (1.0 - th * th) * c * (1.0 + 3.0 * 0.044715 * x * x)


def _scan_fwd(a, u):
    n = a.shape[0]
    row = lax.broadcasted_iota(jnp.int32, a.shape, 0)
    s = 1
    while s < n:
        a_s = jnp.where(row >= s, pltpu.roll(a, s, 0), 1.0)
        u_s = jnp.where(row >= s, pltpu.roll(u, s, 0), 0.0)
        u = u + a * u_s
        a = a * a_s
        s *= 2
    return a, u


def _scan_bwd(b, d):
    n = b.shape[0]
    row = lax.broadcasted_iota(jnp.int32, b.shape, 0)
    s = 1
    while s < n:
        b_s = jnp.where(row < n - s, pltpu.roll(b, n - s, 0), 1.0)
        d_s = jnp.where(row < n - s, pltpu.roll(d, n - s, 0), 0.0)
        d = d + b * d_s
        b = b * b_s
        s *= 2
    return d


def rg_fwd(xr_pre, gate_pre, rgp, w_a, w_i):
    t = xr_pre.shape[0]
    tr = _rt(t)

    def body(x_ref, g_ref, p_ref, wa_ref, wi_ref, hg_ref, hs_ref, prev, hcar):
        ti = pl.program_id(1)

        @pl.when(ti == 0)
        def _():
            prev[...] = jnp.zeros_like(prev)
            hcar[...] = jnp.zeros_like(hcar)

        xv = x_ref[...]
        ext = jnp.concatenate([prev[...], xv], axis=0)
        xr = _conv_pre(ext, xv, p_ref, tr)
        rowmask = _row_mask(ti, tr, (tr, 1)).astype(F32)
        a, u = _rg_pw(xr, wa_ref[0], p_ref[5:6, :], wi_ref[0], p_ref[6:7, :], p_ref[7:8, :], rowmask)
        a_cum, h_loc = _scan_fwd(a, u)
        hs = h_loc + a_cum * hcar[0:1, :]
        hs_ref[...] = hs
        hg_ref[...] = (hs * _gelu(g_ref[...])).astype(hg_ref.dtype)
        hcar[...] = jnp.broadcast_to(hs[tr - 1:tr, :], (8, LANE))
        prev[...] = xv[tr - 8:, :]

    return pl.pallas_call(
        body,
        out_shape=(_sds((t, LRU_WIDTH), BF), _sds((t, LRU_WIDTH), F32)),
        grid=(LRU_BLOCKS, t // tr),
        in_specs=[pl.BlockSpec((tr, LANE), lambda n, ti: (ti, n)),
                  pl.BlockSpec((tr, LANE), lambda n, ti: (ti, n)),
                  pl.BlockSpec((8, LANE), lambda n, ti: (0, n)),
                  pl.BlockSpec((1, LANE, LANE), lambda n, ti: (n, 0, 0)),
                  pl.BlockSpec((1, LANE, LANE), lambda n, ti: (n, 0, 0))],
        out_specs=(pl.BlockSpec((tr, LANE), lambda n, ti: (ti, n)), pl.BlockSpec((tr, LANE), lambda n, ti: (ti, n))),
        scratch_shapes=[pltpu.VMEM((8, LANE), F32), pltpu.VMEM((8, LANE), F32)],
        compiler_params=_cp(("parallel", "arbitrary")),
        name="rg_fwd",
    )(xr_pre, gate_pre, rgp, w_a, w_i)


def rg_bwd(xr_pre, gate_pre, rgp, w_a, w_i, hs, dhg):
    t = xr_pre.shape[0]
    tr = _rt(t)
    nt = t // tr
    r8 = tr // 8

    def body(x_ref, xp_ref, g_ref, p_ref, wa_ref, wi_ref, hs_ref, hp_ref, dhg_ref,
             dx_ref, dg_ref, dp_ref, dwa_ref, dwi_ref, gcar, dnext):
        ti = pl.program_id(1)
        tt = nt - 1 - ti

        @pl.when(ti == 0)
        def _():
            gcar[...] = jnp.zeros_like(gcar)
            dnext[...] = jnp.zeros_like(dnext)
            dp_ref[...] = jnp.zeros_like(dp_ref)
            dwa_ref[...] = jnp.zeros_like(dwa_ref)
            dwi_ref[...] = jnp.zeros_like(dwi_ref)

        xv = x_ref[...]
        halo = jnp.where(tt > 0, xp_ref[...], 0.0)
        ext = jnp.concatenate([halo, xv], axis=0)
        xr = _conv_pre(ext, xv, p_ref, tr)
        rowmask = _row_mask(tt, tr, (tr, 1)).astype(F32)
        fn = lambda xr_, wa_, ba_, wi_, bi_, lam_: _rg_pw(xr_, wa_, ba_, wi_, bi_, lam_, rowmask)
        (a, _), vjp = jax.vjp(fn, xr, wa_ref[0], p_ref[5:6, :], wi_ref[0], p_ref[6:7, :], p_ref[7:8, :])
        gpre = g_ref[...]
        hsv = hs_ref[...]
        dhg_v = dhg_ref[...]
        dg_ref[...] = (dhg_v * hsv * _gelu_grad(gpre)).astype(dg_ref.dtype)
        row = lax.broadcasted_iota(jnp.int32, (tr, LANE), 0)
        d = dhg_v * _gelu(gpre) + jnp.where(row == tr - 1, gcar[0:1, :], 0.0)
        b = jnp.where(row < tr - 1, pltpu.roll(a, tr - 1, 0), 0.0)
        g = _scan_bwd(b, d)
        gcar[...] = jnp.broadcast_to(a[0:1, :] * g[0:1, :], (8, LANE))
        hlast = jnp.where(tt > 0, hp_ref[7:8, :], 0.0)
        hprev = jnp.where(row == 0, hlast, pltpu.roll(hsv, 1, 0))
        dxr, dwa, dba, dwi, dbi, dlam = vjp((g * hprev, g))
        dx, sums = _conv_bwd_parts(dxr, dnext[...], xv, ext, p_ref, tr)
        dx_ref[...] = dx.astype(dx_ref.dtype)
        dnext[...] = dxr[:8, :]
        dp_ref[...] += _rows_block(sums + [dba, dbi, dlam])
        dwa_ref[0] += dwa
        dwi_ref[0] += dwi

    tile = lambda off=0: pl.BlockSpec((tr, LANE), lambda n, ti: (nt - 1 - ti, off + n))
    halo = lambda off=0: pl.BlockSpec((8, LANE), lambda n, ti: (jnp.maximum((nt - 1 - ti) * r8 - 1, 0), off + n))
    par = pl.BlockSpec((8, LANE), lambda n, ti: (0, n))
    wspec = pl.BlockSpec((1, LANE, LANE), lambda n, ti: (n, 0, 0))
    return pl.pallas_call(
        body,
        out_shape=(_sds((t, LRU_WIDTH), BF), _sds((t, LRU_WIDTH), BF), _sds((8, LRU_WIDTH), F32),
                   _sds((LRU_BLOCKS, LANE, LANE), F32), _sds((LRU_BLOCKS, LANE, LANE), F32)),
        grid=(LRU_BLOCKS, nt),
        in_specs=[tile(), halo(), tile(), par, wspec, wspec, tile(), halo(), tile()],
        out_specs=(tile(), tile(), par, wspec, wspec),
        scratch_shapes=[pltpu.VMEM((8, LANE), F32), pltpu.VMEM((8, LANE), F32)],
        compiler_params=_cp(("parallel", "arbitrary")),
        name="rg_bwd",
    )(xr_pre, xr_pre, gate_pre, rgp, w_a, w_i, hs, hs, dhg)


PACK_W = 1024
MESH_ID = pl.DeviceIdType.MESH
ANY = pl.BlockSpec(memory_space=pl.ANY)


def _my_place():
    x, y, c = lax.axis_index("x"), lax.axis_index("y"), lax.axis_index("c")
    return x, y, c


def _lin(px, py, pc):
    return 4 * px + 2 * py + pc


class Exchange:
    def __init__(self, kind, arrs):
        self.kind, self.arrs, self.k = kind, list(arrs), len(arrs)

    def out_shapes(self):
        if self.kind == "gather":
            return [_sds((N_DEV,) + a.shape, a.dtype) for a in self.arrs]
        return [_sds(a.shape, a.dtype) for a in self.arrs]

    def scratch(self):
        k = self.k
        return [pltpu.SemaphoreType.DMA((k, 7)), pltpu.SemaphoreType.DMA((k, 7)), pltpu.SemaphoreType.DMA((k,))]

    def phases(self, ins, outs, sems):
        return (self._gather if self.kind == "gather" else self._scatter)(ins, outs, *sems)

    def _gather(self, ins, outs, send_sems, recv_sems, local_sems):
        k = self.k
        x, y, c = _my_place()
        me, sibling = (x, y, c), (x, y, 1 - c)
        chips = [(1 - x, y), (x, 1 - y), (1 - x, 1 - y)]

        def copy(a, sem, block, to, from_input=False):
            slab = outs[a].at[_lin(*block)]
            return pltpu.make_async_remote_copy(
                src_ref=ins[a] if from_input else slab, dst_ref=slab,
                send_sem=send_sems.at[a, sem], recv_sem=recv_sems.at[a, sem],
                device_id=to, device_id_type=MESH_ID)

        def mine():
            return [pltpu.make_async_copy(ins[a], outs[a].at[_lin(*me)], local_sems.at[a]) for a in range(k)]

        def first():
            out = []
            for a in range(k):
                out.append(copy(a, 0, me, sibling, True))
                out += [copy(a, 1 + j, me, (*chip, c), True) for j, chip in enumerate(chips)]
            return out

        def passed():
            return [copy(a, 4 + j, (*chip, c), sibling) for j, chip in enumerate(chips) for a in range(k)]

        def start():
            for cp in mine() + first():
                cp.start()

        def middle():
            onward = passed()
            for j, chip in enumerate(chips):
                for a in range(k):
                    copy(a, 1 + j, (*chip, c), me).wait_recv()
                    onward[j * k + a].start()

        def finish():
            for a in range(k):
                copy(a, 0, sibling, me).wait_recv()
                for j, chip in enumerate(chips):
                    copy(a, 4 + j, (*chip, 1 - c), me).wait_recv()
            for cp in first() + passed():
                cp.wait_send()
            for cp in mine():
                cp.wait()

        return start, middle, finish

    def _scatter(self, ins, outs, send_sems, recv_sems, local_sems):
        k = self.k
        x, y, c = _my_place()
        me = _lin(x, y, c)
        peers = [((1 - x) if r & 4 else x, (1 - y) if r & 2 else y, (1 - c) if r & 1 else c) for r in range(1, N_DEV)]

        def copy(a, r, src_slab, dst_slab, to):
            return pltpu.make_async_remote_copy(
                src_ref=ins[a].at[src_slab], dst_ref=outs[a].at[dst_slab],
                send_sem=send_sems.at[a, r], recv_sem=recv_sems.at[a, r],
                device_id=to, device_id_type=MESH_ID)

        def mine():
            return [pltpu.make_async_copy(ins[a].at[me], outs[a].at[me], local_sems.at[a]) for a in range(k)]

        def sends():
            return [copy(a, r, _lin(*peer), me, peer) for r, peer in enumerate(peers) for a in range(k)]

        def start():
            for cp in mine() + sends():
                cp.start()

        def middle():
            pass

        def finish():
            for r, peer in enumerate(peers):
                for a in range(k):
                    copy(a, r, me, _lin(*peer), peer).wait_recv()
            for cp in sends():
                cp.wait_send()
            for cp in mine():
                cp.wait()

        return start, middle, finish

    def run(self, name):
        k = self.k

        def body(*refs):
            start, middle, finish = self.phases(refs[:k], refs[k:2 * k], refs[2 * k:])
            start()
            middle()
            finish()

        return pl.pallas_call(
            body,
            out_shape=tuple(self.out_shapes()),
            in_specs=[ANY] * k,
            out_specs=tuple(ANY for _ in range(k)),
            scratch_shapes=self.scratch(),
            name=name,
        )(*self.arrs)


def all_gather(arrs, name):
    return Exchange("gather", arrs).run(name)


def all_to_all(arrs, name):
    return Exchange("scatter", arrs).run(name)


def slab_sum(a, name):
    _, r, w = a.shape
    tr = _pick(r, (256, 128, 64, 32, 16, 8))

    def body(a_ref, o_ref):
        acc = a_ref[0].astype(F32)
        for d in range(1, N_DEV):
            acc = acc + a_ref[d].astype(F32)
        o_ref[...] = acc

    return pl.pallas_call(
        body,
        out_shape=_sds((r, w), F32),
        grid=(r // tr,),
        in_specs=[pl.BlockSpec((N_DEV, tr, w), lambda i: (0, i, 0))],
        out_specs=pl.BlockSpec((tr, w), lambda i: (i, 0)),
        compiler_params=_cp(("parallel",)),
        name=name,
    )(a)


def adamw(w, g, m, v, name):
    r, c = w.shape
    tr = _pick(r, (256, 160, 128, 64, 32, 16, 8))

    def body(w_ref, g_ref, m_ref, v_ref, d_ref, nm_ref, nv_ref):
        gv = g_ref[...]
        nm = ADAM_B1 * m_ref[...] + (1.0 - ADAM_B1) * gv
        nv = ADAM_B2 * v_ref[...] + (1.0 - ADAM_B2) * (gv * gv)
        m_hat = nm / (1.0 - ADAM_B1 ** ADAM_STEP)
        v_hat = nv / (1.0 - ADAM_B2 ** ADAM_STEP)
        d_ref[...] = -ADAM_LR * (m_hat / (jnp.sqrt(v_hat) + ADAM_EPS) + ADAM_WD * w_ref[...])
        nm_ref[...] = nm
        nv_ref[...] = nv

    spec = pl.BlockSpec((tr, c), lambda i: (i, 0))
    return pl.pallas_call(
        body,
        out_shape=tuple(_sds((r, c), F32) for _ in range(3)),
        grid=(r // tr,),
        in_specs=[spec] * 4,
        out_specs=(spec, spec, spec),
        compiler_params=_cp(("parallel",)),
        name=name,
    )(w, g, m, v)


def _relu2_epi(acc):
    r = jnp.maximum(acc, 0.0)
    return r * r, r


def _drelu2_epi(acc, r):
    return (acc * (2.0 * r.astype(F32)),)


def mlp_fwd(h, g_pre, g_post, w_up, w_down):
    hn = norm_fwd(h, g_pre, BF, name="mlp_norm")
    u, r = matmul(hn, w_up, "nn", (BF, BF), epi=_relu2_epi, name="mlp_up")
    d = matmul(u, w_down, "nn", name="mlp_down")
    return resadd_fwd(h, d, g_post, name="mlp_res"), (h, hn, u, r, d)


def mlp_bwd(res, dh2, g_pre, g_post, w_up, w_down):
    h, hn, u, r, d = res
    dd, dg_post = norm_bwd(d, g_post, dh2, mask_pad=True, out_dtype=BF, name="mlp_post_bwd")
    dw_down = matmul(u, dd, "tn", (BF,), name="mlp_dwdown").reshape(w_down.g8.shape)
    dp = matmul(dd, w_down, "nt", (BF,), epi=_drelu2_epi, extras=(r,), name="mlp_du")
    dw_up = matmul(hn, dp, "tn", (BF,), out_blocks=True, name="mlp_dwup")
    dhn = matmul(dp, w_up, "nt", name="mlp_dhn")
    dh, dg_pre = norm_bwd(h, g_pre, dhn, dres=dh2, name="mlp_pre_bwd")
    return dh, dict(mlp_pre_g=dg_pre, mlp_post_g=dg_post, w_up=dw_up, w_down=dw_down)


def rg_layer_fwd(h, g_pre, g_post, w_x, w_y, rgp, w_a, w_i, w_out):
    hn = norm_fwd(h, g_pre, BF, name="rg_norm")
    xr = matmul(hn, w_x, "nn", name="rg_in_x")
    gp = matmul(hn, w_y, "nn", name="rg_in_y")
    hg, hs = rg_fwd(xr, gp, rgp, w_a, w_i)
    m = matmul(hg, w_out, "nn", name="rg_out")
    return resadd_fwd(h, m, g_post, name="rg_res"), (h, hn, xr, gp, hg, hs, m)


def rg_layer_bwd(res, dh2, g_pre, g_post, w_x, w_y, rgp, w_a, w_i, w_out):
    h, hn, xr, gp, hg, hs, m = res
    dm, dg_post = norm_bwd(m, g_post, dh2, mask_pad=True, out_dtype=BF, name="rg_post_bwd")
    dw_out = matmul(hg, dm, "tn", name="rg_dwout")
    dhg = matmul(dm, w_out, "nt", name="rg_dhg")
    dxr, dgp, drgp, dwa, dwi = rg_bwd(xr, gp, rgp, w_a, w_i, hs, dhg)
    dw_x = matmul(hn, dxr, "tn", name="rg_dwx")
    dw_y = matmul(hn, dgp, "tn", name="rg_dwy")
    dhn = matmul([dxr, dgp], [w_x, w_y], "nt", name="rg_dhn")
    dh, dg_pre = norm_bwd(h, g_pre, dhn, dres=dh2, name="rg_pre_bwd")
    return dh, dict(mix_pre_g=dg_pre, mix_post_g=dg_post, rg_w_x=dw_x, rg_w_y=dw_y,
                    rg_conv_w=drgp[0:4], rg_conv_b=drgp[4], rg_b_a=drgp[5], rg_b_i=drgp[6], rg_lambda=drgp[7],
                    rg_w_a=dwa, rg_w_i=dwi, rg_w_out=dw_out)


def sm_layer_fwd(h, g_pre, g_post, w_in_p, convp, dt_bias, a_log, dskip, ssd_g, q_g, w_q_p, kv_g, w_kv_p, w_out, tabs,
                 carry=None):
    hn = norm_fwd(h, g_pre, BF, name="sm_norm")
    proj = matmul(hn, w_in_p, "nn", name="sm_in")
    xbc_c = conv_silu_fwd(proj, OFF_XBC // LANE, SSD_CONV_CH // LANE, convp, name="ssd_conv")
    y, hst = ssd_fwd(xbc_c, proj, dt_bias, a_log, dskip)
    y_ssd = gated_norm_fwd(y, proj, ssd_g)
    cqn = norm_fwd(proj, q_g, BF, col_blk=OFF_CQ // MLA_Q_RANK, width=MLA_Q_RANK, name="q_norm")
    q_raw = matmul(cqn, w_q_p, "nn", name="q_up")
    ckvn = norm_fwd(proj, kv_g, BF, col_blk=OFF_CKV // MLA_KV_RANK, width=MLA_KV_RANK, name="kv_norm")
    kv_raw = matmul(ckvn, w_kv_p, "nn", name="kv_up")
    q_cat, k_cat, v = rope_fwd(q_raw, kv_raw, proj, tabs)
    o, lse, carried = attn_fwd(q_cat, k_cat, v, carry)
    half = w_out.shape[0] // 2
    m = matmul([y_ssd, o], [KBlock(w_out, half, 0), KBlock(w_out, half, 1)], "nn", name="sm_out")
    res = (h, hn, proj, xbc_c, y, hst, cqn, ckvn, q_cat, k_cat, v, o, lse, y_ssd, m)
    return resadd_fwd(h, m, g_post, name="sm_res"), res, carried


def sm_layer_bwd(res, dh2, g_pre, g_post, w_in_p, convp, dt_bias, a_log, dskip, ssd_g, q_g, w_q_p, kv_g, w_kv_p, w_out, tabs,
                 carry=None):
    h, hn, proj, xbc_c, y, hst, cqn, ckvn, q_cat, k_cat, v, o, lse, y_ssd, m = res
    dm, dg_post = norm_bwd(m, g_post, dh2, mask_pad=True, out_dtype=BF, name="sm_post_bwd")
    dw_out = jnp.concatenate([matmul(y_ssd, dm, "tn", name="sm_dwout_ssd"), matmul(o, dm, "tn", name="sm_dwout_att")], axis=0)
    dyab = matmul(dm, w_out, "nt", name="sm_dyab")
    dq_cat, dk_cat, dv, carried = attn_bwd(q_cat, k_cat, v, o, lse, dyab, carry)
    dq_raw, dkr = rope_bwd(dq_cat, dk_cat, tabs)
    kw = MLA_HEADS * LANE
    dw_kv_p = jnp.concatenate([matmul(ckvn, dk_cat, "tn", name="kv_dw_k"), matmul(ckvn, dv, "tn", name="kv_dw_v")], axis=1)
    dckvn = matmul([dk_cat, dv], [KBlock(w_kv_p, kw, 0), KBlock(w_kv_p, kw // 2, 2)], "nt", name="kv_dx")
    dckv, dg_kv = norm_bwd(proj, kv_g, dckvn, out_dtype=BF, col_blk=OFF_CKV // MLA_KV_RANK, width=MLA_KV_RANK,
                           name="kv_norm_bwd")
    dw_q_p = matmul(cqn, dq_raw, "tn", name="q_dw")
    dcqn = matmul(dq_raw, w_q_p, "nt", name="q_dx")
    dcq, dg_q = norm_bwd(proj, q_g, dcqn, out_dtype=BF, col_blk=OFF_CQ // MLA_Q_RANK, width=MLA_Q_RANK, name="q_norm_bwd")
    dy, dz, dg_ssd = gated_norm_bwd(y, proj, ssd_g, dyab)
    dxbc_c, ddt, dpar = ssd_bwd(xbc_c, proj, dt_bias, a_log, dskip, hst, dy)
    dxbc, dconvp = conv_silu_bwd(proj, OFF_XBC // LANE, SSD_CONV_CH // LANE, convp, dxbc_c, name="ssd_conv_bwd")
    pieces = [dz, dxbc, dckv, ddt, dkr, dcq]
    dw_in_p = jnp.concatenate([matmul(hn, pc, "tn", (BF,), name="sm_dwin_%d" % i) for i, pc in enumerate(pieces)], axis=1)
    third = SSD_CONV_CH // 3
    a_terms = [dz] + [KBlock(dxbc, third, i) for i in range(3)] + [dckv, ddt, dkr, dcq]
    b_terms = ([KBlock(w_in_p, SSD_D_INNER, 0)] + [KBlock(w_in_p, third, OFF_XBC // third + i) for i in range(3)]
               + [KBlock(w_in_p, MLA_KV_RANK, OFF_CKV // MLA_KV_RANK), KBlock(w_in_p, LANE, OFF_DT // LANE),
                  KBlock(w_in_p, LANE, OFF_KR // LANE), KBlock(w_in_p, MLA_Q_RANK, OFF_CQ // MLA_Q_RANK)])
    dhn = matmul(a_terms, b_terms, "nt", name="sm_dhn")
    dh, dg_pre = norm_bwd(h, g_pre, dhn, dres=dh2, name="sm_pre_bwd")
    grads = dict(mix_pre_g=dg_pre, mix_post_g=dg_post, w_in=w_in_cols_to_blocks(dw_in_p), ssd_conv_w=dconvp[0:4],
                 ssd_conv_b=dconvp[4], ssd_dt_bias=dpar[0, :SSD_HEADS], ssd_a_log=dpar[1, :SSD_HEADS],
                 ssd_d=dpar[2, :SSD_HEADS], ssd_norm_g=dg_ssd, mla_q_norm_g=dg_q, mla_w_q_up=_unpack_w_q(dw_q_p),
                 mla_kv_norm_g=dg_kv, mla_w_kv_up=_unpack_w_kv(dw_kv_p), w_out_ab=dw_out)
    return dh, grads, carried


W_IN_COLS = 3248
W_IN_SHARD = W_IN_COLS // N_DEV
W_IN_WIRE = 512


def _w_in_tables():
    src = np.full((IN_W,), -1, np.int64)
    src[0:2560] = np.arange(2560)
    src[OFF_CKV:OFF_CKV + 256] = 2960 + np.arange(256)
    src[OFF_DT:OFF_DT + 16] = 2560 + np.arange(16)
    src[OFF_KR + 64:OFF_KR + 96] = 3216 + np.arange(32)
    src[OFF_CQ:OFF_CQ + 384] = 2576 + np.arange(384)
    dev = np.where(src >= 0, src // W_IN_SHARD, -1).astype(np.int32).reshape(1, IN_W)
    col = np.where(src >= 0, src % W_IN_SHARD, 0).astype(np.int32).reshape(1, IN_W)
    return jnp.asarray(dev), jnp.asarray(col)


def w_in_blocks_to_cols(g8):
    _, k, wp = g8.shape
    tn = 384
    dev, col = _w_in_tables()

    def body(g_ref, dev_ref, col_ref, o_ref):
        row = lax.broadcasted_iota(jnp.int32, (wp, tn), 0)
        hit = row == col_ref[...]
        acc = jnp.zeros((k, tn), F32)
        for j in range(N_DEV):
            sel = (hit & (dev_ref[...] == j)).astype(BF)
            acc = acc + jnp.dot(g_ref[j], sel, preferred_element_type=F32)
        o_ref[...] = acc.astype(o_ref.dtype)

    return pl.pallas_call(
        body,
        out_shape=_sds((k, IN_W), BF),
        grid=(IN_W // tn,),
        in_specs=[pl.BlockSpec((N_DEV, k, wp), lambda i: (0, 0, 0)), pl.BlockSpec((1, tn), lambda i: (0, i)),
                  pl.BlockSpec((1, tn), lambda i: (0, i))],
        out_specs=pl.BlockSpec((k, tn), lambda i: (0, i)),
        compiler_params=_cp(("parallel",)),
        name="w_in_cols",
    )(g8, dev, col)


def w_in_cols_to_blocks(dw):
    k = dw.shape[0]
    dev, col = _w_in_tables()

    def body(dw_ref, dev_ref, col_ref, o_ref):
        j = pl.program_id(0)
        row = lax.broadcasted_iota(jnp.int32, (W_IN_WIRE, IN_W), 0)
        sel = ((row == col_ref[...]) & (dev_ref[...] == j)).astype(BF)
        o_ref[0] = lax.dot_general(dw_ref[...], sel, NT_DIMS, preferred_element_type=F32).astype(o_ref.dtype)

    return pl.pallas_call(
        body,
        out_shape=_sds((N_DEV, k, W_IN_WIRE), BF),
        grid=(N_DEV,),
        in_specs=[pl.BlockSpec((k, IN_W), lambda j: (0, 0)), pl.BlockSpec((1, IN_W), lambda j: (0, 0)),
                  pl.BlockSpec((1, IN_W), lambda j: (0, 0))],
        out_specs=pl.BlockSpec((1, k, W_IN_WIRE), lambda j: (j, 0, 0)),
        compiler_params=_cp(("parallel",)),
        name="w_in_blocks",
    )(dw, dev, col)


def _pack_w_q(w):
    w3 = w.reshape(w.shape[0], MLA_HEADS, MLA_NOPE + MLA_ROPE)
    return jnp.pad(w3, ((0, 0), (0, 0), (0, LANE - MLA_NOPE - MLA_ROPE))).reshape(w.shape[0], MLA_HEADS * LANE)


def _unpack_w_q(p):
    return p.reshape(p.shape[0], MLA_HEADS, LANE)[:, :, :MLA_NOPE + MLA_ROPE].reshape(p.shape[0], -1)


def _pack_w_kv(w):
    w3 = w.reshape(w.shape[0], MLA_HEADS, MLA_NOPE + MLA_V)
    k = jnp.pad(w3[:, :, :MLA_NOPE], ((0, 0), (0, 0), (0, LANE - MLA_NOPE))).reshape(w.shape[0], MLA_HEADS * LANE)
    return jnp.concatenate([k, w3[:, :, MLA_NOPE:].reshape(w.shape[0], MLA_HEADS * MLA_V)], axis=1)


def _unpack_w_kv(p):
    k = p[:, :MLA_HEADS * LANE].reshape(p.shape[0], MLA_HEADS, LANE)[:, :, :MLA_NOPE]
    v = p[:, MLA_HEADS * LANE:].reshape(p.shape[0], MLA_HEADS, MLA_V)
    return jnp.concatenate([k, v], axis=2).reshape(p.shape[0], -1)


def _rows8(rows, width):
    a = jnp.concatenate([r.reshape(-1, width) for r in rows], axis=0)
    return jnp.pad(a, ((0, 8 - a.shape[0]), (0, 0)))


SLAB_ROWS = 256


def _to_slab(flat_list, lead=()):
    cat = jnp.concatenate(flat_list, axis=-1)
    n = cat.shape[-1]
    unit = SLAB_ROWS * PACK_W
    total = -(-n // unit) * unit
    cat = jnp.pad(cat, [(0, 0)] * len(lead) + [(0, total - n)])
    return cat.reshape(lead + (total // PACK_W, PACK_W))


def _from_flat(flat, shapes):
    out, off = [], 0
    for s in shapes:
        n = int(np.prod(s))
        out.append(flat[off:off + n].reshape(s))
        off += n
    return out


def _gathered_full(g8, axis):
    moved = jnp.moveaxis(g8, 0, axis)
    shp = moved.shape
    return moved.reshape(shp[:axis] + (shp[axis] * shp[axis + 1],) + shp[axis + 2:])


def _per_device(full, axis):
    shp = full.shape
    split = full.reshape(shp[:axis] + (N_DEV, shp[axis] // N_DEV) + shp[axis + 1:])
    return jnp.moveaxis(split, axis, 0)


ARG_NAMES = ['x', 'meta_tokens', 'mix_pre_g', 'mix_post_g', 'mlp_pre_g', 'mlp_post_g', 'w_up', 'w_down', 'w_in',
             'ssd_conv_w', 'ssd_conv_b', 'ssd_dt_bias', 'ssd_a_log', 'ssd_d', 'ssd_norm_g', 'mla_q_norm_g',
             'mla_w_q_up', 'mla_kv_norm_g', 'mla_w_kv_up', 'w_out_ab', 'rg_w_x', 'rg_w_y', 'rg_conv_w', 'rg_conv_b',
             'rg_w_a', 'rg_b_a', 'rg_w_i', 'rg_b_i', 'rg_lambda', 'rg_w_out']
WEIGHTS = ARG_NAMES[1:]
BIG = {'w_up': 2, 'w_down': 1, 'w_in': 2, 'mla_w_q_up': 2, 'mla_w_kv_up': 2, 'w_out_ab': 1, 'rg_w_x': 2,
       'rg_w_y': 2, 'rg_w_out': 1}
SMALL = {'meta_tokens': 1, 'ssd_conv_w': 2, 'rg_conv_w': 2, 'rg_conv_b': 1, 'rg_b_a': 1, 'rg_b_i': 1, 'rg_lambda': 1}
REPL = [n for n in WEIGHTS if n not in BIG and n not in SMALL]


def _piece_axes():
    axes = {}
    for n, ax in BIG.items():
        for i in range(DEPTH if n in ('w_up', 'w_down') else DEPTH // 2):
            axes[(n, i)] = ax - 1
    return axes


PIECE_AXIS = _piece_axes()
AS_BLOCKS = ('w_up', 'w_down')
_SM = lambda i: [(n, i) for n in ('w_in', 'mla_w_q_up', 'mla_w_kv_up', 'w_out_ab')]
_RG = lambda i: [(n, i) for n in ('rg_w_x', 'rg_w_y', 'rg_w_out')]
_MLP = lambda l: [('w_up', l), ('w_down', l)]
GATHER_FIRST = _SM(0)
GATHER_AT = {0: _MLP(0) + _RG(0) + _MLP(1) + _SM(1), 2: _MLP(2) + _RG(1) + _MLP(3)}
SCATTER_AT = {2: _MLP(3) + _RG(1) + _MLP(2), 0: _SM(1) + _MLP(1) + _RG(0) + _MLP(0)}
SCATTER_LAST = _SM(0)


def _wire_block(p, key):
    n, i = key
    blk = p[n][i]
    if n == 'w_in':
        blk = jnp.pad(blk, ((0, 0), (0, W_IN_WIRE - blk.shape[1])))
    return blk


def _step(p, moments):
    assert DEPTH == 4
    full = {n: [None] * p[n].shape[0] for n in BIG}
    full['w_in_g'] = [None] * p['w_in'].shape[0]

    def weight_blocks(group):
        return [_wire_block(p, k).astype(BF) for k in group]

    def take_weights(group, gathered):
        for (n, i), piece in zip(group, gathered):
            if n == 'w_in':
                full['w_in_g'][i] = piece
            elif n in AS_BLOCKS:
                full[n][i] = DevBlocks(piece, PIECE_AXIS[(n, i)])
            else:
                full[n][i] = _gathered_full(piece, PIECE_AXIS[(n, i)])

    def grad_blocks(group, gw):
        return [gw[k] if k[0] in AS_BLOCKS or k[0] == 'w_in' else _per_device(gw[k], PIECE_AXIS[k]).astype(BF)
                for k in group]

    g_blocks = {}

    def take_grads(group, received):
        for k, r8 in zip(group, received):
            blk = slab_sum(r8, name="sum_%s_%d" % k)
            g_blocks[k] = blk[:, :W_IN_SHARD] if k[0] == 'w_in' else blk

    small_slab = _to_slab([p[n].reshape(-1) for n in SMALL])
    *first, small8 = all_gather(weight_blocks(GATHER_FIRST) + [small_slab], name="gather_first")
    take_weights(GATHER_FIRST, first)
    for n, piece in zip(SMALL, _from_flat_rows(small8, [p[n].shape for n in SMALL])):
        full[n] = _gathered_full(piece, SMALL[n])
    for n in REPL:
        full[n] = p[n]
    loss_local, grad_x, gw, gsmall_full, carried = _local_step(
        full, p['x'][0], p['loss_target'][0],
        fwd_carry=lambda layer: Exchange("gather", weight_blocks(GATHER_AT[layer])),
        on_fwd_carried=lambda layer, got: take_weights(GATHER_AT[layer], got),
        bwd_carry=lambda layer, gw_now: Exchange("scatter", grad_blocks(SCATTER_AT[layer], gw_now)))

    for layer, group in SCATTER_AT.items():
        take_grads(group, carried[layer])
    rep_flat = jnp.concatenate([gsmall_full[n].reshape(-1) for n in REPL])
    rep_n = rep_flat.shape[0]
    rep_chunk = -(-rep_n // (N_DEV * PACK_W * 8)) * PACK_W * 8
    rep8 = jnp.pad(rep_flat, (0, N_DEV * rep_chunk - rep_n)).reshape(N_DEV, rep_chunk)
    gsmall = _to_slab([_per_device(gsmall_full[n], SMALL[n]).reshape(N_DEV, -1) for n in SMALL] + [rep8], lead=(N_DEV,))
    *rlast, rsmall = all_to_all(grad_blocks(SCATTER_LAST, gw) + [gsmall], name="scatter_last")
    take_grads(SCATTER_LAST, rlast)
    ssmall = slab_sum(rsmall, name="sum_small").reshape(-1)
    g_loc = {n: jnp.stack([g_blocks[(n, i)] for i in range(p[n].shape[0])], axis=0) for n in BIG}
    small_n = sum(int(np.prod(p[n].shape)) for n in SMALL)
    g_loc.update(zip(SMALL, _from_flat(ssmall, [p[n].shape for n in SMALL])))
    rep_mine = ssmall[small_n:small_n + rep_chunk].reshape(-1, PACK_W)
    (rep_all,) = all_gather([rep_mine], name="gather_replicated")
    g_loc.update(zip(REPL, _from_flat(rep_all.reshape(-1), [p[n].shape for n in REPL])))

    out = {'loss': lax.psum(loss_local, ("x", "y", "c")), 'grad_x': grad_x[None]}
    small_names = list(SMALL) + REPL
    for n in BIG:
        shp = p[n].shape
        v2 = lambda a: a.reshape(-1, shp[-1])
        d, nm, nv = adamw(v2(p[n]), v2(g_loc[n]), v2(moments['m_' + n]), v2(moments['v_' + n]), name="adamw_" + n)
        out['delta_' + n], out['new_m_' + n], out['new_v_' + n] = d.reshape(shp), nm.reshape(shp), nv.reshape(shp)
    slab = lambda src: _to_slab([src(n).reshape(-1) for n in small_names])
    d, nm, nv = adamw(slab(lambda n: p[n]), slab(lambda n: g_loc[n]), slab(lambda n: moments['m_' + n]),
                      slab(lambda n: moments['v_' + n]), name="adamw_small")
    shapes = [p[n].shape for n in small_names]
    for key, flat in (('delta_', d), ('new_m_', nm), ('new_v_', nv)):
        for n, a in zip(small_names, _from_flat(flat.reshape(-1), shapes)):
            out[key + n] = a
    for n in WEIGHTS:
        out['grad_' + n] = g_loc[n]
    return out


def _local_step(full, x, target_rows, fwd_carry=None, on_fwd_carried=None, bwd_carry=None):
    t = PAD + N_META + x.shape[0]
    h = jnp.concatenate([jnp.zeros((PAD, D_MODEL), F32), full['meta_tokens'], x], axis=0)
    target = jnp.concatenate([jnp.zeros((PAD + N_META, D_MODEL), F32), target_rows], axis=0)
    tabs = rope_tables(t)

    def layer_args(layer):
        i = layer // 2
        if layer % 2 == 0:
            convp = _rows8([full['ssd_conv_w'][i], full['ssd_conv_b'][i]], SSD_CONV_CH)
            return (full['mix_pre_g'][layer], full['mix_post_g'][layer], w_in_blocks_to_cols(full['w_in_g'][i]), convp,
                    full['ssd_dt_bias'][i], full['ssd_a_log'][i], full['ssd_d'][i], full['ssd_norm_g'][i],
                    full['mla_q_norm_g'][i], _pack_w_q(full['mla_w_q_up'][i]), full['mla_kv_norm_g'][i],
                    _pack_w_kv(full['mla_w_kv_up'][i]), full['w_out_ab'][i], tabs)
        rgp = _rows8([full['rg_conv_w'][i], full['rg_conv_b'][i], full['rg_b_a'][i], full['rg_b_i'][i],
                      full['rg_lambda'][i]], LRU_WIDTH)
        return (full['mix_pre_g'][layer], full['mix_post_g'][layer], full['rg_w_x'][i], full['rg_w_y'][i], rgp,
                full['rg_w_a'][i], full['rg_w_i'][i], full['rg_w_out'][i])

    def mlp_args(layer):
        return (full['mlp_pre_g'][layer], full['mlp_post_g'][layer], full['w_up'][layer], full['w_down'][layer])

    saved = []
    for layer in range(DEPTH):
        la = layer_args(layer)
        if layer % 2 == 0:
            h, res_mix, got = sm_layer_fwd(h, *la, carry=fwd_carry(layer) if fwd_carry is not None else None)
            if fwd_carry is not None:
                on_fwd_carried(layer, got)
        else:
            h, res_mix = rg_layer_fwd(h, *la)
        ma = mlp_args(layer)
        h, res_mlp = mlp_fwd(h, *ma)
        saved.append((la, ma, res_mix, res_mlp))
    loss_local, dh = loss_fwd_bwd(h, target)

    others = {n: [None] * len(full[n]) for n in WEIGHTS if n not in BIG and n != 'meta_tokens'}
    gw, carried = {}, {}
    for layer in reversed(range(DEPTH)):
        la, ma, res_mix, res_mlp = saved[layer]
        dh, gm = mlp_bwd(res_mlp, dh, *ma)
        if layer % 2 == 0:
            for n in ('w_up', 'w_down'):
                gw[(n, layer)] = gm[n]
            carry = bwd_carry(layer, gw) if bwd_carry is not None else None
            dh, gx, carried[layer] = sm_layer_bwd(res_mix, dh, *la, carry=carry)
        else:
            dh, gx = rg_layer_bwd(res_mix, dh, *la)
        for n, g in list(gm.items()) + list(gx.items()):
            i = layer if n in ('mix_pre_g', 'mix_post_g', 'mlp_pre_g', 'mlp_post_g', 'w_up', 'w_down') else layer // 2
            if n in BIG:
                gw[(n, i)] = g
            else:
                others[n][i] = g
    gothers = {n: jnp.stack(v, axis=0) for n, v in others.items()}
    gothers['meta_tokens'] = dh[PAD:PAD + N_META]
    return loss_local, dh[PAD + N_META:], gw, gothers, carried


def _from_flat_rows(g8, shapes):
    flat = g8.reshape(N_DEV, -1)
    out, off = [], 0
    for s in shapes:
        n = int(np.prod(s))
        out.append(flat[:, off:off + n].reshape((N_DEV,) + tuple(s)))
        off += n
    return out


def kernel(x, meta_tokens, mix_pre_g, mix_post_g, mlp_pre_g, mlp_post_g, w_up, w_down, w_in, ssd_conv_w, ssd_conv_b, ssd_dt_bias, ssd_a_log, ssd_d, ssd_norm_g, mla_q_norm_g, mla_w_q_up, mla_kv_norm_g, mla_w_kv_up, w_out_ab, rg_w_x, rg_w_y, rg_conv_w, rg_conv_b, rg_w_a, rg_b_a, rg_w_i, rg_b_i, rg_lambda, rg_w_out, loss_target, m_meta_tokens, m_mix_pre_g, m_mix_post_g, m_mlp_pre_g, m_mlp_post_g, m_w_up, m_w_down, m_w_in, m_ssd_conv_w, m_ssd_conv_b, m_ssd_dt_bias, m_ssd_a_log, m_ssd_d, m_ssd_norm_g, m_mla_q_norm_g, m_mla_w_q_up, m_mla_kv_norm_g, m_mla_w_kv_up, m_w_out_ab, m_rg_w_x, m_rg_w_y, m_rg_conv_w, m_rg_conv_b, m_rg_w_a, m_rg_b_a, m_rg_w_i, m_rg_b_i, m_rg_lambda, m_rg_w_out, v_meta_tokens, v_mix_pre_g, v_mix_post_g, v_mlp_pre_g, v_mlp_post_g, v_w_up, v_w_down, v_w_in, v_ssd_conv_w, v_ssd_conv_b, v_ssd_dt_bias, v_ssd_a_log, v_ssd_d, v_ssd_norm_g, v_mla_q_norm_g, v_mla_w_q_up, v_mla_kv_norm_g, v_mla_w_kv_up, v_w_out_ab, v_rg_w_x, v_rg_w_y, v_rg_conv_w, v_rg_conv_b, v_rg_w_a, v_rg_b_a, v_rg_w_i, v_rg_b_i, v_rg_lambda, v_rg_w_out):
    args = (x, meta_tokens, mix_pre_g, mix_post_g, mlp_pre_g, mlp_post_g, w_up, w_down, w_in, ssd_conv_w, ssd_conv_b, ssd_dt_bias, ssd_a_log, ssd_d, ssd_norm_g, mla_q_norm_g, mla_w_q_up, mla_kv_norm_g, mla_w_kv_up, w_out_ab, rg_w_x, rg_w_y, rg_conv_w, rg_conv_b, rg_w_a, rg_b_a, rg_w_i, rg_b_i, rg_lambda, rg_w_out, loss_target, m_meta_tokens, m_mix_pre_g, m_mix_post_g, m_mlp_pre_g, m_mlp_post_g, m_w_up, m_w_down, m_w_in, m_ssd_conv_w, m_ssd_conv_b, m_ssd_dt_bias, m_ssd_a_log, m_ssd_d, m_ssd_norm_g, m_mla_q_norm_g, m_mla_w_q_up, m_mla_kv_norm_g, m_mla_w_kv_up, m_w_out_ab, m_rg_w_x, m_rg_w_y, m_rg_conv_w, m_rg_conv_b, m_rg_w_a, m_rg_b_a, m_rg_w_i, m_rg_b_i, m_rg_lambda, m_rg_w_out, v_meta_tokens, v_mix_pre_g, v_mix_post_g, v_mlp_pre_g, v_mlp_post_g, v_w_up, v_w_down, v_w_in, v_ssd_conv_w, v_ssd_conv_b, v_ssd_dt_bias, v_ssd_a_log, v_ssd_d, v_ssd_norm_g, v_mla_q_norm_g, v_mla_w_q_up, v_mla_kv_norm_g, v_mla_w_kv_up, v_w_out_ab, v_rg_w_x, v_rg_w_y, v_rg_conv_w, v_rg_conv_b, v_rg_w_a, v_rg_b_a, v_rg_w_i, v_rg_b_i, v_rg_lambda, v_rg_w_out,)
    n_w = len(ARG_NAMES)
    p = dict(zip(ARG_NAMES, args[:n_w]))
    p['loss_target'] = args[n_w]
    moments = {}
    for i, n in enumerate(WEIGHTS):
        moments['m_' + n] = args[n_w + 1 + i]
        moments['v_' + n] = args[n_w + 1 + len(WEIGHTS) + i]
    out = _step(p, moments)
    res = [out['loss'], out['grad_x']]
    for prefix in ('grad_', 'delta_', 'new_m_', 'new_v_'):
        res += [out[prefix + n] for n in WEIGHTS]
    return tuple(res)
```

```python
import functools
import math

import numpy as np
import jax
import jax.numpy as jnp
from jax import lax
from jax.experimental import pallas as pl
from jax.experimental.pallas import tpu as pltpu

F32 = jnp.float32
BF = jnp.bfloat16
HI = lax.Precision.HIGHEST

D_MODEL = 1024
DEPTH = 4
N_META = 16
CHUNK = 128
PAD = CHUNK - N_META
EPS = 1e-6
SSD_HEADS = 16
SSD_HEAD_DIM = 64
SSD_D_INNER = 1024
SSD_STATE = 128
SSD_CONV_CH = 1536
MLA_HEADS = 16
MLA_NOPE = 64
MLA_ROPE = 32
MLA_V = 64
MLA_Q_RANK = 384
MLA_KV_RANK = 256
ROPE_BASE = 10000.0
LRU_WIDTH = 1280
LRU_BLOCKS = 10
LRU_C = 8.0
D_FF = 4096
N_DEV = 8
LANE = 128
IN_W = 3456
OFF_Z, OFF_XBC, OFF_CKV, OFF_DT, OFF_KR, OFF_CQ = 0, 1024, 2560, 2816, 2944, 3072

ADAM_LR = 0.001
ADAM_B1 = 0.9
ADAM_B2 = 0.999
ADAM_EPS = 1e-08
ADAM_WD = 0.01
ADAM_STEP = 10

VMEM_LIMIT = 56 * 1024 * 1024
NEG = -1e30


def _pick(n, cands):
    for c in cands:
        if n % c == 0:
            return c
    return n


def _cp(sem=None):
    return pltpu.CompilerParams(dimension_semantics=sem, vmem_limit_bytes=VMEM_LIMIT)


def _sds(shape, dtype):
    return jax.ShapeDtypeStruct(tuple(shape), dtype)


def _silu(x):
    return x * jax.nn.sigmoid(x)


def _softplus(x):
    return jnp.maximum(x, 0.0) + jnp.log(1.0 + jnp.exp(-jnp.abs(x)))


def _gelu(x):
    c = math.sqrt(2.0 / math.pi)
    return 0.5 * x * (1.0 + jnp.tanh(c * (x + 0.044715 * (x * x * x))))


def _row_mask(i, tr, shape, first_valid=PAD):
    row = i * tr + lax.broadcasted_iota(jnp.int32, shape, 0)
    return row >= first_valid


class KBlock:
    def __init__(self, arr, width, blk):
        self.arr, self.width, self.blk = arr, width, blk


class DevBlocks:
    def __init__(self, g8, axis):
        self.g8, self.axis = g8, axis
        _, r, c = g8.shape
        self.shape = (N_DEV * r, c) if axis == 0 else (r, N_DEV * c)


NN_DIMS = (((1,), (0,)), ((), ()))


def matmul(a, b, mode, out_dtypes=(F32,), epi=None, extras=(), name="mm", tm=None, tn=None, out_blocks=False):
    a_terms = a if isinstance(a, (list, tuple)) else [a]
    b_terms = b if isinstance(b, (list, tuple)) else [b]
    assert len(a_terms) == len(b_terms) and (mode != "tn" or len(a_terms) == 1)
    arr_of = lambda t: t.arr if isinstance(t, KBlock) else t
    if mode == "tn":
        m, n = a_terms[0].shape[1], b_terms[0].shape[1]
    else:
        m = arr_of(a_terms[0]).shape[0]
        b0 = b_terms[0]
        n = (b0.shape if isinstance(b0, DevBlocks) else arr_of(b0).shape)[1 if mode == "nn" else 0]
    tm = tm or _pick(m, (1056, 1024, 768, 640, 512, 384, 256, 128))
    tn = tn or _pick(n, (512, 640, 384, 256, 128))
    if mode == "tn":
        tm = _pick(m, (512, 384, 256, 128))
    dims = {"nn": NN_DIMS, "nt": NT_DIMS, "tn": TN_DIMS}[mode]

    in_specs, args, plan = [], [], []
    for ta, tb in zip(a_terms, b_terms):
        if mode == "tn":
            k = ta.shape[0]
            in_specs += [pl.BlockSpec((k, tm), lambda i, j: (0, i)), pl.BlockSpec((k, tn), lambda i, j: (0, j))]
            args += [ta, tb]
            plan.append(None)
            continue
        if isinstance(ta, KBlock):
            kw, ka = ta.width, ta.blk
            in_specs.append(pl.BlockSpec((tm, kw), lambda i, j, ka=ka: (i, ka)))
        else:
            kw = ta.shape[1]
            in_specs.append(pl.BlockSpec((tm, kw), lambda i, j: (i, 0)))
        args.append(arr_of(ta))
        if isinstance(tb, DevBlocks):
            _, r, c = tb.g8.shape
            split_k = tb.axis == (0 if mode == "nn" else 1)
            if split_k:
                kd = r if mode == "nn" else c
                assert kw == N_DEV * kd
                blk = (N_DEV, kd, tn) if mode == "nn" else (N_DEV, tn, kd)
                in_specs.append(pl.BlockSpec(blk, (lambda i, j: (0, 0, j)) if mode == "nn" else (lambda i, j: (0, j, 0))))
                plan.append(kd)
            else:
                per = (c if mode == "nn" else r) // tn
                blk = (None, kw, tn) if mode == "nn" else (None, tn, kw)
                in_specs.append(pl.BlockSpec(blk, (lambda i, j, per=per: (j // per, 0, j % per)) if mode == "nn"
                                             else (lambda i, j, per=per: (j // per, j % per, 0))))
                plan.append(None)
            args.append(tb.g8)
        else:
            kb = tb.blk if isinstance(tb, KBlock) else 0
            assert (tb.width if isinstance(tb, KBlock) else tb.shape[0 if mode == "nn" else 1]) == kw
            in_specs.append(pl.BlockSpec((kw, tn), lambda i, j, kb=kb: (kb, j)) if mode == "nn"
                            else pl.BlockSpec((tn, kw), lambda i, j, kb=kb: (j, kb)))
            args.append(arr_of(tb))
            plan.append(None)
    n_terms, n_ex = len(plan), len(extras)

    def body(*refs):
        ex_refs, out_refs = refs[2 * n_terms:2 * n_terms + n_ex], refs[2 * n_terms + n_ex:]
        acc = None
        for t, kd in enumerate(plan):
            a_ref, b_ref = refs[2 * t], refs[2 * t + 1]
            if kd is None:
                parts = [lax.dot_general(a_ref[...].astype(BF), b_ref[...].astype(BF), dims, preferred_element_type=F32)]
            else:
                parts = [lax.dot_general(a_ref[:, d * kd:(d + 1) * kd].astype(BF), b_ref[d].astype(BF), dims,
                                         preferred_element_type=F32) for d in range(N_DEV)]
            for part in parts:
                acc = part if acc is None else acc + part
        outs = (acc,) if epi is None else epi(acc, *[r[...] for r in ex_refs])
        for r, o in zip(out_refs, outs):
            r[...] = o.astype(r.dtype)

    o_spec = pl.BlockSpec((tm, tn), lambda i, j: (i, j))
    if out_blocks:
        per = n // N_DEV // tn
        out_shape = tuple(_sds((N_DEV, m, n // N_DEV), dt) for dt in out_dtypes)
        out_specs = tuple(pl.BlockSpec((None, tm, tn), lambda i, j: (j // per, i, j % per)) for _ in out_dtypes)
    else:
        out_shape = tuple(_sds((m, n), dt) for dt in out_dtypes)
        out_specs = tuple(o_spec for _ in out_dtypes)
    outs = pl.pallas_call(
        body,
        out_shape=out_shape,
        grid=(m // tm, n // tn),
        in_specs=in_specs + [o_spec] * n_ex,
        out_specs=out_specs,
        compiler_params=_cp(("parallel", "parallel")),
        name=name,
    )(*args, *extras)
    return outs[0] if len(out_dtypes) == 1 else outs


def _rt(t):
    return _pick(t, (384, 256, 128))


def norm_fwd(x, g, out_dtype, col_blk=0, width=None, name="norm_fwd"):
    t = x.shape[0]
    w = width or x.shape[1]
    tr = _rt(t)

    def body(x_ref, g_ref, o_ref):
        xv = x_ref[...]
        r = lax.rsqrt(jnp.mean(xv * xv, axis=-1, keepdims=True) + EPS)
        o_ref[...] = (xv * r * g_ref[...]).astype(o_ref.dtype)

    return pl.pallas_call(
        body,
        out_shape=_sds((t, w), out_dtype),
        grid=(t // tr,),
        in_specs=[pl.BlockSpec((tr, w), lambda i: (i, col_blk)), pl.BlockSpec((1, w), lambda i: (0, 0))],
        out_specs=pl.BlockSpec((tr, w), lambda i: (i, 0)),
        compiler_params=_cp(("parallel",)),
        name=name,
    )(x, g.reshape(1, w))


def norm_bwd(x, g, dy, dres=None, mask_pad=False, out_dtype=F32, col_blk=0, width=None, dy_col_blk=0, name="norm_bwd"):
    t = x.shape[0]
    w = width or x.shape[1]
    tr = _rt(t)
    has_res = dres is not None

    def body(x_ref, g_ref, dy_ref, *rest):
        if has_res:
            res_ref, dx_ref, dg_ref = rest
        else:
            dx_ref, dg_ref = rest
        i = pl.program_id(0)
        xv = x_ref[...]
        dyv = dy_ref[...].astype(F32)
        if mask_pad:
            dyv = jnp.where(_row_mask(i, tr, dyv.shape), dyv, 0.0)
        r = lax.rsqrt(jnp.mean(xv * xv, axis=-1, keepdims=True) + EPS)
        xh = xv * r
        dyg = dyv * g_ref[...]
        dx = r * (dyg - xh * jnp.mean(dyg * xh, axis=-1, keepdims=True))
        if has_res:
            dx = dx + res_ref[...]
        dx_ref[...] = dx.astype(dx_ref.dtype)

        @pl.when(i == 0)
        def _():
            dg_ref[...] = jnp.zeros_like(dg_ref)

        dg_ref[...] += jnp.sum(dyv * xh, axis=0, keepdims=True)

    in_specs = [pl.BlockSpec((tr, w), lambda i: (i, col_blk)), pl.BlockSpec((1, w), lambda i: (0, 0)),
                pl.BlockSpec((tr, w), lambda i: (i, dy_col_blk))]
    args = [x, g.reshape(1, w), dy]
    if has_res:
        in_specs.append(pl.BlockSpec((tr, w), lambda i: (i, 0)))
        args.append(dres)
    dx, dg = pl.pallas_call(
        body,
        out_shape=(_sds((t, w), out_dtype), _sds((1, w), F32)),
        grid=(t // tr,),
        in_specs=in_specs,
        out_specs=(pl.BlockSpec((tr, w), lambda i: (i, 0)), pl.BlockSpec((1, w), lambda i: (0, 0))),
        compiler_params=_cp(("arbitrary",)),
        name=name,
    )(*args)
    return dx, dg.reshape(w)


def resadd_fwd(h, m, g, name="resadd"):
    t, w = h.shape
    tr = _rt(t)

    def body(h_ref, m_ref, g_ref, o_ref):
        mv = m_ref[...]
        r = lax.rsqrt(jnp.mean(mv * mv, axis=-1, keepdims=True) + EPS)
        y = mv * r * g_ref[...]
        o_ref[...] = h_ref[...] + jnp.where(_row_mask(pl.program_id(0), tr, y.shape), y, 0.0)

    return pl.pallas_call(
        body,
        out_shape=_sds((t, w), F32),
        grid=(t // tr,),
        in_specs=[pl.BlockSpec((tr, w), lambda i: (i, 0)), pl.BlockSpec((tr, w), lambda i: (i, 0)),
                  pl.BlockSpec((1, w), lambda i: (0, 0))],
        out_specs=pl.BlockSpec((tr, w), lambda i: (i, 0)),
        compiler_params=_cp(("parallel",)),
        name=name,
    )(h, m, g.reshape(1, w))


def loss_fwd_bwd(h, target):
    t, w = h.shape
    tr = _rt(t)

    def body(h_ref, t_ref, s_ref, dh_ref):
        i = pl.program_id(0)
        err = h_ref[...] - t_ref[...]
        err = jnp.where(_row_mask(i, tr, err.shape, PAD + N_META), err, 0.0)
        dh_ref[...] = err * (1.0 / w)

        @pl.when(i == 0)
        def _():
            s_ref[...] = jnp.zeros_like(s_ref)

        s_ref[...] += jnp.sum(err * err).reshape(1, 1)

    s, dh = pl.pallas_call(
        body,
        out_shape=(_sds((1, LANE), F32), _sds((t, w), F32)),
        grid=(t // tr,),
        in_specs=[pl.BlockSpec((tr, w), lambda i: (i, 0)), pl.BlockSpec((tr, w), lambda i: (i, 0))],
        out_specs=(pl.BlockSpec((1, LANE), lambda i: (0, 0)), pl.BlockSpec((tr, w), lambda i: (i, 0))),
        compiler_params=_cp(("arbitrary",)),
        name="loss",
    )(h, target)
    return 0.5 * s[0, 0] / w, dh


def _shift_down(ext, k, n):
    return pltpu.roll(ext, k, 0)[8:]


def _conv_pre(ext, x, w_ref, n):
    return (w_ref[4:5, :] + w_ref[3:4, :] * x + w_ref[2:3, :] * _shift_down(ext, 1, n)
            + w_ref[1:2, :] * _shift_down(ext, 2, n) + w_ref[0:1, :] * _shift_down(ext, 3, n))


def _conv_bwd_parts(dpre, dnext, x, ext, w_ref, n):
    extd = jnp.concatenate([dpre, dnext], axis=0)
    ln = n + 8
    dx = (w_ref[3:4, :] * dpre + w_ref[2:3, :] * pltpu.roll(extd, ln - 1, 0)[:n]
          + w_ref[1:2, :] * pltpu.roll(extd, ln - 2, 0)[:n] + w_ref[0:1, :] * pltpu.roll(extd, ln - 3, 0)[:n])
    sums = [jnp.sum(dpre * _shift_down(ext, 3, n), axis=0, keepdims=True),
            jnp.sum(dpre * _shift_down(ext, 2, n), axis=0, keepdims=True),
            jnp.sum(dpre * _shift_down(ext, 1, n), axis=0, keepdims=True),
            jnp.sum(dpre * x, axis=0, keepdims=True),
            jnp.sum(dpre, axis=0, keepdims=True)]
    return dx, sums


def _rows_block(sums):
    w = sums[0].shape[1]
    row = lax.broadcasted_iota(jnp.int32, (8, w), 0)
    out = jnp.zeros((8, w), F32)
    for k, s in enumerate(sums):
        out = jnp.where(row == k, s, out)
    return out


CONV_BLOCK = 512


def conv_silu_fwd(x, col0, c, wb, name="conv_fwd"):
    t = x.shape[0]
    cw = _pick(c, (CONV_BLOCK, LANE))
    nblk, col0_blk = c // cw, col0 // cw
    assert col0 % cw == 0
    tr = _rt(t)

    def body(x_ref, w_ref, o_ref, prev):
        ti = pl.program_id(1)

        @pl.when(ti == 0)
        def _():
            prev[...] = jnp.zeros_like(prev)

        xv = x_ref[...]
        ext = jnp.concatenate([prev[...], xv], axis=0)
        o_ref[...] = _silu(_conv_pre(ext, xv, w_ref, tr))
        prev[...] = xv[tr - 8:, :]

    return pl.pallas_call(
        body,
        out_shape=_sds((t, c), F32),
        grid=(nblk, t // tr),
        in_specs=[pl.BlockSpec((tr, cw), lambda cb, ti: (ti, col0_blk + cb)),
                  pl.BlockSpec((8, cw), lambda cb, ti: (0, cb))],
        out_specs=pl.BlockSpec((tr, cw), lambda cb, ti: (ti, cb)),
        scratch_shapes=[pltpu.VMEM((8, cw), F32)],
        compiler_params=_cp(("parallel", "arbitrary")),
        name=name,
    )(x, wb)


def conv_silu_bwd(x, col0, c, wb, dout, name="conv_bwd"):
    t = x.shape[0]
    cw = _pick(c, (CONV_BLOCK, LANE))
    nblk, col0_blk = c // cw, col0 // cw
    assert col0 % cw == 0
    tr = _rt(t)
    nt = t // tr
    r8 = tr // 8

    def body(x_ref, xp_ref, w_ref, do_ref, dx_ref, dwb_ref, dnext):
        ti = pl.program_id(1)
        tt = nt - 1 - ti

        @pl.when(ti == 0)
        def _():
            dnext[...] = jnp.zeros_like(dnext)
            dwb_ref[...] = jnp.zeros_like(dwb_ref)

        xv = x_ref[...]
        halo = jnp.where(tt > 0, xp_ref[...], 0.0)
        ext = jnp.concatenate([halo, xv], axis=0)
        pre = _conv_pre(ext, xv, w_ref, tr)
        s = jax.nn.sigmoid(pre)
        dpre = do_ref[...] * (s + pre * s * (1.0 - s))
        dx, sums = _conv_bwd_parts(dpre, dnext[...], xv, ext, w_ref, tr)
        dx_ref[...] = dx.astype(dx_ref.dtype)
        dwb_ref[...] += _rows_block(sums)
        dnext[...] = dpre[:8, :]

    return pl.pallas_call(
        body,
        out_shape=(_sds((t, c), BF), _sds((8, c), F32)),
        grid=(nblk, nt),
        in_specs=[pl.BlockSpec((tr, cw), lambda cb, ti: (nt - 1 - ti, col0_blk + cb)),
                  pl.BlockSpec((8, cw), lambda cb, ti: (jnp.maximum((nt - 1 - ti) * r8 - 1, 0), col0_blk + cb)),
                  pl.BlockSpec((8, cw), lambda cb, ti: (0, cb)),
                  pl.BlockSpec((tr, cw), lambda cb, ti: (nt - 1 - ti, cb))],
        out_specs=(pl.BlockSpec((tr, cw), lambda cb, ti: (nt - 1 - ti, cb)),
                   pl.BlockSpec((8, cw), lambda cb, ti: (0, cb))),
        scratch_shapes=[pltpu.VMEM((8, cw), F32)],
        compiler_params=_cp(("parallel", "arbitrary")),
        name=name,
    )(x, x, wb, dout)


def gated_norm_fwd(y, proj, g, name="gnorm_fwd"):
    t, w = y.shape
    tr = _rt(t)

    def body(y_ref, z_ref, g_ref, o_ref):
        v = y_ref[...] * _silu(z_ref[...])
        r = lax.rsqrt(jnp.mean(v * v, axis=-1, keepdims=True) + EPS)
        o_ref[...] = (v * r * g_ref[...]).astype(o_ref.dtype)

    return pl.pallas_call(
        body,
        out_shape=_sds((t, w), BF),
        grid=(t // tr,),
        in_specs=[pl.BlockSpec((tr, w), lambda i: (i, 0)), pl.BlockSpec((tr, w), lambda i: (i, OFF_Z // w)),
                  pl.BlockSpec((1, w), lambda i: (0, 0))],
        out_specs=pl.BlockSpec((tr, w), lambda i: (i, 0)),
        compiler_params=_cp(("parallel",)),
        name=name,
    )(y, proj, g.reshape(1, w))


def gated_norm_bwd(y, proj, g, dyab, name="gnorm_bwd"):
    t, w = y.shape
    tr = _rt(t)

    def body(y_ref, z_ref, g_ref, do_ref, dy_ref, dz_ref, dg_ref):
        i = pl.program_id(0)
        yv, zv, dov = y_ref[...], z_ref[...], do_ref[...]
        s = jax.nn.sigmoid(zv)
        sz = zv * s
        v = yv * sz
        r = lax.rsqrt(jnp.mean(v * v, axis=-1, keepdims=True) + EPS)
        vh = v * r
        dvg = dov * g_ref[...]
        dv = r * (dvg - vh * jnp.mean(dvg * vh, axis=-1, keepdims=True))
        dy_ref[...] = dv * sz
        dz_ref[...] = (dv * yv * (s + sz * (1.0 - s))).astype(dz_ref.dtype)

        @pl.when(i == 0)
        def _():
            dg_ref[...] = jnp.zeros_like(dg_ref)

        dg_ref[...] += jnp.sum(dov * vh, axis=0, keepdims=True)

    dy, dz, dg = pl.pallas_call(
        body,
        out_shape=(_sds((t, w), F32), _sds((t, w), BF), _sds((1, w), F32)),
        grid=(t // tr,),
        in_specs=[pl.BlockSpec((tr, w), lambda i: (i, 0)), pl.BlockSpec((tr, w), lambda i: (i, OFF_Z // w)),
                  pl.BlockSpec((1, w), lambda i: (0, 0)), pl.BlockSpec((tr, w), lambda i: (i, 0))],
        out_specs=(pl.BlockSpec((tr, w), lambda i: (i, 0)), pl.BlockSpec((tr, w), lambda i: (i, 0)),
                   pl.BlockSpec((1, w), lambda i: (0, 0))),
        compiler_params=_cp(("arbitrary",)),
        name=name,
    )(y, proj, g.reshape(1, w), dyab)
    return dy, dz, dg.reshape(w)


def rope_tables(t):
    inv = ROPE_BASE ** (-jnp.arange(0, MLA_ROPE, 2, dtype=F32) / MLA_ROPE)
    pos = (jnp.arange(t, dtype=F32) - PAD)[:, None]
    ang = pos * inv[None, :]
    cos, sin = jnp.cos(ang), jnp.sin(ang)
    z16 = jnp.zeros((t, 16), F32)
    z32 = jnp.zeros((t, 32), F32)
    c = jnp.concatenate([jnp.ones((t, 64), F32), cos, cos, z32], axis=1)
    s1 = jnp.concatenate([jnp.zeros((t, 64), F32), z16, sin, z32], axis=1)
    s2 = jnp.concatenate([jnp.zeros((t, 64), F32), -sin, z16, z32], axis=1)
    return c, s1, s2


def _rope(x, c, s1, s2):
    return x * c + pltpu.roll(x, 16, 1) * s1 + pltpu.roll(x, LANE - 16, 1) * s2


def _rope_t(d, c, s1, s2):
    return d * c + pltpu.roll(d * s1, LANE - 16, 1) + pltpu.roll(d * s2, 16, 1)


def rope_fwd(q_raw, kv_raw, proj, tabs):
    t = q_raw.shape[0]
    tr = _rt(t)
    hw = MLA_HEADS * LANE

    def body(q_ref, k_ref, v_ref, kr_ref, c_ref, s1_ref, s2_ref, qo_ref, ko_ref, vo_ref):
        c, s1, s2 = c_ref[...], s1_ref[...], s2_ref[...]
        kr = _rope(kr_ref[...], c, s1, s2)
        for h in range(MLA_HEADS):
            sl = slice(h * LANE, (h + 1) * LANE)
            qo_ref[:, sl] = (_rope(q_ref[:, sl], c, s1, s2) * Q_PRESCALE).astype(BF)
            ko_ref[:, sl] = (k_ref[:, sl] + kr).astype(BF)
        vo_ref[...] = v_ref[...].astype(BF)

    tab_spec = pl.BlockSpec((tr, LANE), lambda i: (i, 0))
    return pl.pallas_call(
        body,
        out_shape=(_sds((t, hw), BF), _sds((t, hw), BF), _sds((t, 1024), BF)),
        grid=(t // tr,),
        in_specs=[pl.BlockSpec((tr, hw), lambda i: (i, 0)), pl.BlockSpec((tr, hw), lambda i: (i, 0)),
                  pl.BlockSpec((tr, 1024), lambda i: (i, 2)), pl.BlockSpec((tr, LANE), lambda i: (i, OFF_KR // LANE)),
                  tab_spec, tab_spec, tab_spec],
        out_specs=(pl.BlockSpec((tr, hw), lambda i: (i, 0)), pl.BlockSpec((tr, hw), lambda i: (i, 0)),
                   pl.BlockSpec((tr, 1024), lambda i: (i, 0))),
        compiler_params=_cp(("parallel",)),
        name="rope_fwd",
    )(q_raw, kv_raw, kv_raw, proj, *tabs)


def rope_bwd(dq_cat, dk_cat, tabs):
    t = dq_cat.shape[0]
    tr = _rt(t)
    hw = MLA_HEADS * LANE

    def body(dq_ref, dk_ref, c_ref, s1_ref, s2_ref, dqo_ref, dkr_ref):
        c, s1, s2 = c_ref[...], s1_ref[...], s2_ref[...]
        acc = jnp.zeros((tr, LANE), F32)
        for h in range(MLA_HEADS):
            sl = slice(h * LANE, (h + 1) * LANE)
            dqo_ref[:, sl] = _rope_t(dq_ref[:, sl] * ATT_SCALE, c, s1, s2).astype(BF)
            acc = acc + dk_ref[:, sl]
        lane = lax.broadcasted_iota(jnp.int32, (tr, LANE), 1)
        dkr_ref[...] = jnp.where((lane >= 64) & (lane < 96), _rope_t(acc, c, s1, s2), 0.0)

    tab_spec = pl.BlockSpec((tr, LANE), lambda i: (i, 0))
    return pl.pallas_call(
        body,
        out_shape=(_sds((t, hw), BF), _sds((t, LANE), F32)),
        grid=(t // tr,),
        in_specs=[pl.BlockSpec((tr, hw), lambda i: (i, 0)), pl.BlockSpec((tr, hw), lambda i: (i, 0)),
                  tab_spec, tab_spec, tab_spec],
        out_specs=(pl.BlockSpec((tr, hw), lambda i: (i, 0)), pl.BlockSpec((tr, LANE), lambda i: (i, 0))),
        compiler_params=_cp(("parallel",)),
        name="rope_bwd",
    )(dq_cat, dk_cat, *tabs)


ATT_SCALE = (MLA_NOPE + MLA_ROPE) ** -0.5
LOG2E = math.log2(math.e)
Q_PRESCALE = ATT_SCALE * LOG2E
NT_DIMS = (((1,), (1,)), ((), ()))
TN_DIMS = (((0,), (0,)), ((), ()))


def _att_mask(qi, ki, tq, tk):
    qpos = qi * tq + lax.broadcasted_iota(jnp.int32, (tq, tk), 0)
    kpos = ki * tk + lax.broadcasted_iota(jnp.int32, (tq, tk), 1)
    return (kpos <= qpos) & (kpos >= PAD)


def _half_masks(n):
    lane = lax.broadcasted_iota(jnp.int32, (n, LANE), 1)
    return lane < 64, lane >= 64


def _att_tile(t):
    return _pick(t, (384, 256, 128))


def _ds(i, n):
    return pl.ds(i * n, n) if isinstance(i, int) else pl.ds(pl.multiple_of(i * n, n), n)


def attn_fwd(q_cat, k_cat, v, carry=None):
    t = q_cat.shape[0]
    tq = tk = _att_tile(t)
    nq = t // tq
    n_pair = MLA_HEADS // 2
    nx = carry.k if carry else 0

    def body(*refs):
        q_ref, k_ref, v_ref = refs[:3]
        o_ref, lse_ref = refs[3 + nx:5 + nx]
        qi = pl.program_id(1)
        if carry:
            start, middle, finish = carry.phases(refs[3:3 + nx], refs[5 + nx:5 + 2 * nx], refs[5 + 2 * nx:])
            pair = pl.program_id(0)
            pl.when((pair == 0) & (qi == 0))(start)
            pl.when((pair == n_pair // 2) & (qi == 0))(middle)
        lo_q, _ = _half_masks(tq)
        halves = _half_masks(tk)

        def step(ki, state, masked):
            m_old, l_old, acc = state[0:2], state[2:4], state[4]
            rows = _ds(ki, tk)
            vv = v_ref[rows, :]
            ss = [lax.dot_general(q_ref[:, hh * LANE:(hh + 1) * LANE], k_ref[rows, hh * LANE:(hh + 1) * LANE], NT_DIMS,
                                  preferred_element_type=F32) for hh in range(2)]
            if masked:
                valid = _att_mask(qi, ki, tq, tk)
                ss = [jnp.where(valid, s, NEG) for s in ss]
            m_new = [jnp.maximum(m_old[hh], jnp.max(ss[hh], axis=-1, keepdims=True)) for hh in range(2)]
            ps = [jnp.exp2(ss[hh] - m_new[hh]) for hh in range(2)]
            alpha = [jnp.exp2(m_old[hh] - m_new[hh]) for hh in range(2)]
            l_new = [alpha[hh] * l_old[hh] + jnp.sum(ps[hh], axis=-1, keepdims=True) for hh in range(2)]
            pv = [jnp.dot(ps[hh].astype(BF), jnp.where(halves[hh], vv, jnp.zeros_like(vv)), preferred_element_type=F32)
                  for hh in range(2)]
            acc = acc * jnp.where(lo_q, alpha[0], alpha[1]) + pv[0] + pv[1]
            return m_new[0], m_new[1], l_new[0], l_new[1], acc

        neg, zero = jnp.full((tq, 1), NEG, F32), jnp.zeros((tq, 1), F32)
        state = step(0, (neg, neg, zero, zero, jnp.zeros((tq, LANE), F32)), True)
        state = lax.fori_loop(1, qi, lambda ki, st: step(ki, st, False), state)
        state = lax.cond(qi > 0, lambda st: step(qi, st, True), lambda st: st, state)
        m0, m1, l0, l1, acc = state
        l = jnp.where(lo_q, l0, l1)
        o_ref[...] = (acc / l).astype(o_ref.dtype)
        lse_ref[...] = jnp.where(lo_q, m0, m1) + jnp.log2(l)
        if carry:
            pl.when((pair == n_pair - 1) & (qi == nq - 1))(finish)

    outs = pl.pallas_call(
        body,
        out_shape=(_sds((t, 1024), BF), _sds((t, 1024), F32)) + tuple(carry.out_shapes() if carry else ()),
        grid=(n_pair, nq),
        in_specs=[pl.BlockSpec((tq, 2 * LANE), lambda p, qi: (qi, p)),
                  pl.BlockSpec((t, 2 * LANE), lambda p, qi: (0, p)),
                  pl.BlockSpec((t, LANE), lambda p, qi: (0, p))] + [ANY] * nx,
        out_specs=(pl.BlockSpec((tq, LANE), lambda p, qi: (qi, p)),
                   pl.BlockSpec((tq, LANE), lambda p, qi: (qi, p))) + (ANY,) * nx,
        scratch_shapes=carry.scratch() if carry else [],
        compiler_params=_cp(("arbitrary", "arbitrary") if carry else ("parallel", "parallel")),
        name="attn_fwd_carrying" if carry else "attn_fwd",
    )(q_cat, k_cat, v, *(carry.arrs if carry else ()))
    return outs[0], outs[1], list(outs[2:])


def attn_bwd(q_cat, k_cat, v, o, lse, dyab, carry=None):
    t = q_cat.shape[0]
    tq = tk = _att_tile(t)
    nq = t // tq
    n_pair = MLA_HEADS // 2
    nx = carry.k if carry else 0

    def body(*refs):
        q_ref, k_ref, v_ref, o_ref, lse_ref, do_ref = refs[:6]
        dq_ref, dk_ref, dv_ref = refs[6 + nx:9 + nx]
        ki = pl.program_id(1)
        if carry:
            start, middle, finish = carry.phases(refs[6:6 + nx], refs[9 + nx:9 + 2 * nx], refs[9 + 2 * nx:])
            pair = pl.program_id(0)
            pl.when((pair == 0) & (ki == 0))(start)
            pl.when((pair == n_pair // 2) & (ki == 0))(middle)

        @pl.when(ki == 0)
        def _():
            dq_ref[...] = jnp.zeros_like(dq_ref)

        halves = _half_masks(tq)
        vv = v_ref[...]
        kk = [k_ref[:, hh * LANE:(hh + 1) * LANE] for hh in range(2)]

        def step(qi, acc, masked):
            rows = _ds(qi, tq)
            dov, ov, lse_v = do_ref[rows, :], o_ref[rows, :].astype(F32), lse_ref[rows, :]
            qh = [q_ref[rows, hh * LANE:(hh + 1) * LANE] for hh in range(2)]
            ss = [lax.dot_general(qh[hh], kk[hh], NT_DIMS, preferred_element_type=F32) for hh in range(2)]
            if masked:
                valid = _att_mask(qi, ki, tq, tk)
                ss = [jnp.where(valid, s, NEG) for s in ss]
            ps = [jnp.exp2(ss[hh] - lse_v[:, 64 * hh:64 * hh + 1]) for hh in range(2)]
            dom = [jnp.where(halves[hh], dov, 0.0) for hh in range(2)]
            delta = [jnp.sum(dom[hh] * ov, axis=-1, keepdims=True) for hh in range(2)]
            dom = [d.astype(BF) for d in dom]
            dp = [lax.dot_general(dom[hh], vv, NT_DIMS, preferred_element_type=F32) for hh in range(2)]
            ds = [(ps[hh] * (dp[hh] - delta[hh])).astype(BF) for hh in range(2)]
            pb = [p.astype(BF) for p in ps]
            dv = (acc[2] + lax.dot_general(pb[0], dom[0], TN_DIMS, preferred_element_type=F32)
                  + lax.dot_general(pb[1], dom[1], TN_DIMS, preferred_element_type=F32))
            dk = [acc[hh] + lax.dot_general(ds[hh], qh[hh], TN_DIMS, preferred_element_type=F32) for hh in range(2)]
            for hh in range(2):
                dq_ref[rows, hh * LANE:(hh + 1) * LANE] += jnp.dot(ds[hh], kk[hh], preferred_element_type=F32)
            return dk[0], dk[1], dv

        zero = jnp.zeros((tk, LANE), F32)
        acc = step(ki, (zero, zero, zero), True)
        acc = lax.fori_loop(ki + 1, jnp.where(ki == 0, nq, ki + 1), lambda qi, a: step(qi, a, True), acc)
        acc = lax.fori_loop(ki + 1, jnp.where(ki == 0, ki + 1, nq), lambda qi, a: step(qi, a, False), acc)
        dk_ref[:, 0:LANE] = acc[0] * (1.0 / LOG2E)
        dk_ref[:, LANE:2 * LANE] = acc[1] * (1.0 / LOG2E)
        dv_ref[...] = acc[2]
        if carry:
            pl.when((pair == n_pair - 1) & (ki == nq - 1))(finish)

    full = lambda w, off=0: pl.BlockSpec((t, w), lambda p, ki: (0, p + off))
    blk = lambda w: pl.BlockSpec((tk, w), lambda p, ki: (ki, p))
    outs = pl.pallas_call(
        body,
        out_shape=(_sds((t, 2048), F32), _sds((t, 2048), F32), _sds((t, 1024), F32))
        + tuple(carry.out_shapes() if carry else ()),
        grid=(n_pair, nq),
        in_specs=[full(2 * LANE), blk(2 * LANE), blk(LANE), full(LANE), full(LANE), full(LANE, 8)] + [ANY] * nx,
        out_specs=(full(2 * LANE), blk(2 * LANE), blk(LANE)) + (ANY,) * nx,
        scratch_shapes=carry.scratch() if carry else [],
        compiler_params=_cp(("arbitrary", "arbitrary") if carry else ("parallel", "arbitrary")),
        name="attn_bwd_carrying" if carry else "attn_bwd",
    )(q_cat, k_cat, v, o, lse, dyab, *(carry.arrs if carry else ()))
    return outs[0], outs[1], outs[2], list(outs[3:])


N_PAIR = SSD_HEADS // 2


def _hdot(a, b):
    return jnp.dot(a, b, precision=HI, preferred_element_type=F32)


def _ssd_chunk(xs, bg, cg, dtraw, hin, dt_bias, a_log, dskip, rowmask):
    ln = CHUNK
    causal = lax.broadcasted_iota(jnp.int32, (ln, ln), 0) >= lax.broadcasted_iota(jnp.int32, (ln, ln), 1)
    ltri = causal.astype(F32)
    lane = lax.broadcasted_iota(jnp.int32, (ln, LANE), 1)
    halves = (lane < 64, lane >= 64)
    low_row = lax.broadcasted_iota(jnp.int32, (1, LANE), 1) < 64
    head_lane = lax.broadcasted_iota(jnp.int32, (1, SSD_HEADS), 1)
    head_row = lax.broadcasted_iota(jnp.int32, (SSD_HEADS, 1), 0)

    def col(a, h):
        return jnp.sum(jnp.where(head_lane == h, a, 0.0), axis=1, keepdims=True)

    dt = _softplus(dtraw + dt_bias) * rowmask
    da = dt * (-jnp.exp(a_log))
    acs = _hdot(ltri, da)
    acs_t = lax.dot_general(da, ltri, (((0,), (1,)), ((), ())), precision=HI, preferred_element_type=F32)
    tot = jnp.sum(da, axis=0, keepdims=True)
    bm = [b * rowmask for b in bg]
    cm = [c * rowmask for c in cg]
    cb = [lax.dot_general(cm[g].astype(BF), bm[g].astype(BF), NT_DIMS, preferred_element_type=F32) for g in range(2)]
    ys, hout = [], []
    for p in range(N_PAIR):
        g = p // (N_PAIR // 2)
        h0, h1 = 2 * p, 2 * p + 1
        xdt = xs[p] * jnp.where(halves[0], col(dt, h0), col(dt, h1))
        y = jnp.zeros((ln, LANE), F32)
        snew = jnp.zeros((ln, LANE), F32)
        for hh in range(2):
            a_col = col(acs, h0 + hh)
            a_row = jnp.sum(jnp.where(head_row == h0 + hh, acs_t, 0.0), axis=0, keepdims=True)
            dec = jnp.exp(jnp.where(causal, a_col - a_row, NEG))
            xm = jnp.where(halves[hh], xdt, 0.0).astype(BF)
            y = y + jnp.dot((cb[g] * dec).astype(BF), xm, preferred_element_type=F32)
            bd = bm[g] * jnp.exp(col(tot, h0 + hh) - a_col)
            snew = snew + lax.dot_general(bd.astype(BF), xm, TN_DIMS, preferred_element_type=F32)
        y_off = (jnp.dot(cm[g].astype(BF), hin[p].astype(BF), preferred_element_type=F32)
                 * jnp.exp(jnp.where(halves[0], col(acs, h0), col(acs, h1))))
        ys.append(y + y_off + jnp.where(low_row, col(dskip, h0), col(dskip, h1)) * xs[p])
        hout.append(jnp.exp(jnp.where(low_row, col(tot, h0), col(tot, h1))) * hin[p] + snew)
    return ys, hout


def _ssd_load(x_ref, dt_ref):
    xs = [x_ref[:, p * LANE:(p + 1) * LANE] for p in range(N_PAIR)]
    bg = [x_ref[:, SSD_D_INNER + g * LANE:SSD_D_INNER + (g + 1) * LANE] for g in range(2)]
    cg = [x_ref[:, SSD_D_INNER + (2 + g) * LANE:SSD_D_INNER + (3 + g) * LANE] for g in range(2)]
    return xs, bg, cg, dt_ref[:, 0:SSD_HEADS]


def _chunk_rowmask(c):
    return ((c * CHUNK + lax.broadcasted_iota(jnp.int32, (CHUNK, 1), 0)) >= PAD).astype(F32)


def ssd_fwd(xbc_c, proj, dt_bias, a_log, dskip):
    t = xbc_c.shape[0]
    nc = t // CHUNK

    def body(x_ref, dt_ref, dtb_ref, al_ref, d_ref, y_ref, hs_ref, h_s):
        c = pl.program_id(0)

        @pl.when(c == 0)
        def _():
            h_s[...] = jnp.zeros_like(h_s)

        xs, bg, cg, dtraw = _ssd_load(x_ref, dt_ref)
        hin = [h_s[p] for p in range(N_PAIR)]
        hs_ref[0] = h_s[...]
        ys, hout = _ssd_chunk(xs, bg, cg, dtraw, hin, dtb_ref[...], al_ref[...], d_ref[...], _chunk_rowmask(c))
        for p in range(N_PAIR):
            y_ref[:, p * LANE:(p + 1) * LANE] = ys[p]
            h_s[p] = hout[p]

    par = pl.BlockSpec((1, SSD_HEADS), lambda c: (0, 0))
    return pl.pallas_call(
        body,
        out_shape=(_sds((t, SSD_D_INNER), F32), _sds((nc, N_PAIR, CHUNK, LANE), F32)),
        grid=(nc,),
        in_specs=[pl.BlockSpec((CHUNK, SSD_CONV_CH), lambda c: (c, 0)),
                  pl.BlockSpec((CHUNK, LANE), lambda c: (c, OFF_DT // LANE)), par, par, par],
        out_specs=(pl.BlockSpec((CHUNK, SSD_D_INNER), lambda c: (c, 0)),
                   pl.BlockSpec((1, N_PAIR, CHUNK, LANE), lambda c: (c, 0, 0, 0))),
        scratch_shapes=[pltpu.VMEM((N_PAIR, CHUNK, LANE), F32)],
        compiler_params=_cp(("arbitrary",)),
        name="ssd_fwd",
    )(xbc_c, proj, dt_bias.reshape(1, -1), a_log.reshape(1, -1), dskip.reshape(1, -1))


def ssd_bwd(xbc_c, proj, dt_bias, a_log, dskip, hs, dy):
    t = xbc_c.shape[0]
    nc = t // CHUNK

    def body(x_ref, dt_ref, dtb_ref, al_ref, d_ref, hs_ref, dy_ref, dx_ref, ddt_ref, dpar_ref, dh_s):
        ci = pl.program_id(0)
        c = nc - 1 - ci

        @pl.when(ci == 0)
        def _():
            dh_s[...] = jnp.zeros_like(dh_s)
            dpar_ref[...] = jnp.zeros_like(dpar_ref)

        xs, bg, cg, dtraw = _ssd_load(x_ref, dt_ref)
        hin = [hs_ref[0, p] for p in range(N_PAIR)]
        rowmask = _chunk_rowmask(c)
        fn = lambda xs_, bg_, cg_, dtraw_, hin_, dtb_, al_, d_: _ssd_chunk(xs_, bg_, cg_, dtraw_, hin_, dtb_, al_, d_, rowmask)
        _, vjp = jax.vjp(fn, xs, bg, cg, dtraw, hin, dtb_ref[...], al_ref[...], d_ref[...])
        dys = [dy_ref[:, p * LANE:(p + 1) * LANE] for p in range(N_PAIR)]
        dhs = [dh_s[p] for p in range(N_PAIR)]
        dxs, dbg, dcg, ddtraw, dhin, ddtb, dal, dd = vjp((dys, dhs))
        for p in range(N_PAIR):
            dx_ref[:, p * LANE:(p + 1) * LANE] = dxs[p]
            dh_s[p] = dhin[p]
        for g in range(2):
            dx_ref[:, SSD_D_INNER + g * LANE:SSD_D_INNER + (g + 1) * LANE] = dbg[g]
            dx_ref[:, SSD_D_INNER + (2 + g) * LANE:SSD_D_INNER + (3 + g) * LANE] = dcg[g]
        ddt_ref[...] = jnp.zeros_like(ddt_ref)
        ddt_ref[:, 0:SSD_HEADS] = ddtraw
        dpar_ref[0:1, 0:SSD_HEADS] += ddtb
        dpar_ref[1:2, 0:SSD_HEADS] += dal
        dpar_ref[2:3, 0:SSD_HEADS] += dd

    par = pl.BlockSpec((1, SSD_HEADS), lambda ci: (0, 0))
    return pl.pallas_call(
        body,
        out_shape=(_sds((t, SSD_CONV_CH), F32), _sds((t, LANE), F32), _sds((8, LANE), F32)),
        grid=(nc,),
        in_specs=[pl.BlockSpec((CHUNK, SSD_CONV_CH), lambda ci: (nc - 1 - ci, 0)),
                  pl.BlockSpec((CHUNK, LANE), lambda ci: (nc - 1 - ci, OFF_DT // LANE)), par, par, par,
                  pl.BlockSpec((1, N_PAIR, CHUNK, LANE), lambda ci: (nc - 1 - ci, 0, 0, 0)),
                  pl.BlockSpec((CHUNK, SSD_D_INNER), lambda ci: (nc - 1 - ci, 0))],
        out_specs=(pl.BlockSpec((CHUNK, SSD_CONV_CH), lambda ci: (nc - 1 - ci, 0)),
                   pl.BlockSpec((CHUNK, LANE), lambda ci: (nc - 1 - ci, 0)),
                   pl.BlockSpec((8, LANE), lambda ci: (0, 0))),
        scratch_shapes=[pltpu.VMEM((N_PAIR, CHUNK, LANE), F32)],
        compiler_params=_cp(("arbitrary",)),
        name="ssd_bwd",
    )(xbc_c, proj, dt_bias.reshape(1, -1), a_log.reshape(1, -1), dskip.reshape(1, -1), hs, dy)


def _neg_expm1(y):
    series = -(y * (1.0 + y * (0.5 + y * (1.0 / 6.0 + y * (1.0 / 24.0 + y * (1.0 / 120.0))))))
    return jnp.where(y > -0.1, series, 1.0 - jnp.exp(y))


def _rg_pw(xr, wa, ba, wi, bi, lam, rowmask):
    xb = xr.astype(BF)
    r = jax.nn.sigmoid(jnp.dot(xb, wa.astype(BF), preferred_element_type=F32) + ba)
    i = jax.nn.sigmoid(jnp.dot(xb, wi.astype(BF), preferred_element_type=F32) + bi)
    log_a = -LRU_C * r * _softplus(-lam)
    a = jnp.exp(log_a)
    u = jnp.sqrt(_neg_expm1(2.0 * log_a)) * (i * xr) * rowmask
    return a, u


def _gelu_grad(x):
    c = math.sqrt(2.0 / math.pi)
    th = jnp.tanh(c * (x + 0.044715 * (x * x * x)))
    return 0.5 * (1.0 + th) + 0.5 * x * (1.0 - th * th) * c * (1.0 + 3.0 * 0.044715 * x * x)


def _scan_fwd(a, u):
    n = a.shape[0]
    row = lax.broadcasted_iota(jnp.int32, a.shape, 0)
    s = 1
    while s < n:
        a_s = jnp.where(row >= s, pltpu.roll(a, s, 0), 1.0)
        u_s = jnp.where(row >= s, pltpu.roll(u, s, 0), 0.0)
        u = u + a * u_s
        a = a * a_s
        s *= 2
    return a, u


def _scan_bwd(b, d):
    n = b.shape[0]
    row = lax.broadcasted_iota(jnp.int32, b.shape, 0)
    s = 1
    while s < n:
        b_s = jnp.where(row < n - s, pltpu.roll(b, n - s, 0), 1.0)
        d_s = jnp.where(row < n - s, pltpu.roll(d, n - s, 0), 0.0)
        d = d + b * d_s
        b = b * b_s
        s *= 2
    return d


def rg_fwd(xr_pre, gate_pre, rgp, w_a, w_i):
    t = xr_pre.shape[0]
    tr = _rt(t)

    def body(x_ref, g_ref, p_ref, wa_ref, wi_ref, hg_ref, hs_ref, prev, hcar):
        ti = pl.program_id(1)

        @pl.when(ti == 0)
        def _():
            prev[...] = jnp.zeros_like(prev)
            hcar[...] = jnp.zeros_like(hcar)

        xv = x_ref[...]
        ext = jnp.concatenate([prev[...], xv], axis=0)
        xr = _conv_pre(ext, xv, p_ref, tr)
        rowmask = _row_mask(ti, tr, (tr, 1)).astype(F32)
        a, u = _rg_pw(xr, wa_ref[0], p_ref[5:6, :], wi_ref[0], p_ref[6:7, :], p_ref[7:8, :], rowmask)
        a_cum, h_loc = _scan_fwd(a, u)
        hs = h_loc + a_cum * hcar[0:1, :]
        hs_ref[...] = hs
        hg_ref[...] = (hs * _gelu(g_ref[...])).astype(hg_ref.dtype)
        hcar[...] = jnp.broadcast_to(hs[tr - 1:tr, :], (8, LANE))
        prev[...] = xv[tr - 8:, :]

    return pl.pallas_call(
        body,
        out_shape=(_sds((t, LRU_WIDTH), BF), _sds((t, LRU_WIDTH), F32)),
        grid=(LRU_BLOCKS, t // tr),
        in_specs=[pl.BlockSpec((tr, LANE), lambda n, ti: (ti, n)),
                  pl.BlockSpec((tr, LANE), lambda n, ti: (ti, n)),
                  pl.BlockSpec((8, LANE), lambda n, ti: (0, n)),
                  pl.BlockSpec((1, LANE, LANE), lambda n, ti: (n, 0, 0)),
                  pl.BlockSpec((1, LANE, LANE), lambda n, ti: (n, 0, 0))],
        out_specs=(pl.BlockSpec((tr, LANE), lambda n, ti: (ti, n)), pl.BlockSpec((tr, LANE), lambda n, ti: (ti, n))),
        scratch_shapes=[pltpu.VMEM((8, LANE), F32), pltpu.VMEM((8, LANE), F32)],
        compiler_params=_cp(("parallel", "arbitrary")),
        name="rg_fwd",
    )(xr_pre, gate_pre, rgp, w_a, w_i)


def rg_bwd(xr_pre, gate_pre, rgp, w_a, w_i, hs, dhg):
    t = xr_pre.shape[0]
    tr = _rt(t)
    nt = t // tr
    r8 = tr // 8

    def body(x_ref, xp_ref, g_ref, p_ref, wa_ref, wi_ref, hs_ref, hp_ref, dhg_ref,
             dx_ref, dg_ref, dp_ref, dwa_ref, dwi_ref, gcar, dnext):
        ti = pl.program_id(1)
        tt = nt - 1 - ti

        @pl.when(ti == 0)
        def _():
            gcar[...] = jnp.zeros_like(gcar)
            dnext[...] = jnp.zeros_like(dnext)
            dp_ref[...] = jnp.zeros_like(dp_ref)
            dwa_ref[...] = jnp.zeros_like(dwa_ref)
            dwi_ref[...] = jnp.zeros_like(dwi_ref)

        xv = x_ref[...]
        halo = jnp.where(tt > 0, xp_ref[...], 0.0)
        ext = jnp.concatenate([halo, xv], axis=0)
        xr = _conv_pre(ext, xv, p_ref, tr)
        rowmask = _row_mask(tt, tr, (tr, 1)).astype(F32)
        fn = lambda xr_, wa_, ba_, wi_, bi_, lam_: _rg_pw(xr_, wa_, ba_, wi_, bi_, lam_, rowmask)
        (a, _), vjp = jax.vjp(fn, xr, wa_ref[0], p_ref[5:6, :], wi_ref[0], p_ref[6:7, :], p_ref[7:8, :])
        gpre = g_ref[...]
        hsv = hs_ref[...]
        dhg_v = dhg_ref[...]
        dg_ref[...] = (dhg_v * hsv * _gelu_grad(gpre)).astype(dg_ref.dtype)
        row = lax.broadcasted_iota(jnp.int32, (tr, LANE), 0)
        d = dhg_v * _gelu(gpre) + jnp.where(row == tr - 1, gcar[0:1, :], 0.0)
        b = jnp.where(row < tr - 1, pltpu.roll(a, tr - 1, 0), 0.0)
        g = _scan_bwd(b, d)
        gcar[...] = jnp.broadcast_to(a[0:1, :] * g[0:1, :], (8, LANE))
        hlast = jnp.where(tt > 0, hp_ref[7:8, :], 0.0)
        hprev = jnp.where(row == 0, hlast, pltpu.roll(hsv, 1, 0))
        dxr, dwa, dba, dwi, dbi, dlam = vjp((g * hprev, g))
        dx, sums = _conv_bwd_parts(dxr, dnext[...], xv, ext, p_ref, tr)
        dx_ref[...] = dx.astype(dx_ref.dtype)
        dnext[...] = dxr[:8, :]
        dp_ref[...] += _rows_block(sums + [dba, dbi, dlam])
        dwa_ref[0] += dwa
        dwi_ref[0] += dwi

    tile = lambda off=0: pl.BlockSpec((tr, LANE), lambda n, ti: (nt - 1 - ti, off + n))
    halo = lambda off=0: pl.BlockSpec((8, LANE), lambda n, ti: (jnp.maximum((nt - 1 - ti) * r8 - 1, 0), off + n))
    par = pl.BlockSpec((8, LANE), lambda n, ti: (0, n))
    wspec = pl.BlockSpec((1, LANE, LANE), lambda n, ti: (n, 0, 0))
    return pl.pallas_call(
        body,
        out_shape=(_sds((t, LRU_WIDTH), BF), _sds((t, LRU_WIDTH), BF), _sds((8, LRU_WIDTH), F32),
                   _sds((LRU_BLOCKS, LANE, LANE), F32), _sds((LRU_BLOCKS, LANE, LANE), F32)),
        grid=(LRU_BLOCKS, nt),
        in_specs=[tile(), halo(), tile(), par, wspec, wspec, tile(), halo(), tile()],
        out_specs=(tile(), tile(), par, wspec, wspec),
        scratch_shapes=[pltpu.VMEM((8, LANE), F32), pltpu.VMEM((8, LANE), F32)],
        compiler_params=_cp(("parallel", "arbitrary")),
        name="rg_bwd",
    )(xr_pre, xr_pre, gate_pre, rgp, w_a, w_i, hs, hs, dhg)


PACK_W = 1024
MESH_ID = pl.DeviceIdType.MESH
ANY = pl.BlockSpec(memory_space=pl.ANY)


def _my_place():
    x, y, c = lax.axis_index("x"), lax.axis_index("y"), lax.axis_index("c")
    return x, y, c


def _lin(px, py, pc):
    return 4 * px + 2 * py + pc


class Exchange:
    def __init__(self, kind, arrs):
        self.kind, self.arrs, self.k = kind, list(arrs), len(arrs)

    def out_shapes(self):
        if self.kind == "gather":
            return [_sds((N_DEV,) + a.shape, a.dtype) for a in self.arrs]
        return [_sds(a.shape, a.dtype) for a in self.arrs]

    def scratch(self):
        k = self.k
        return [pltpu.SemaphoreType.DMA((k, 7)), pltpu.SemaphoreType.DMA((k, 7)), pltpu.SemaphoreType.DMA((k,))]

    def phases(self, ins, outs, sems):
        return (self._gather if self.kind == "gather" else self._scatter)(ins, outs, *sems)

    def _gather(self, ins, outs, send_sems, recv_sems, local_sems):
        k = self.k
        x, y, c = _my_place()
        me, sibling = (x, y, c), (x, y, 1 - c)
        chips = [(1 - x, y), (x, 1 - y), (1 - x, 1 - y)]

        def copy(a, sem, block, to, from_input=False):
            slab = outs[a].at[_lin(*block)]
            return pltpu.make_async_remote_copy(
                src_ref=ins[a] if from_input else slab, dst_ref=slab,
                send_sem=send_sems.at[a, sem], recv_sem=recv_sems.at[a, sem],
                device_id=to, device_id_type=MESH_ID)

        def mine():
            return [pltpu.make_async_copy(ins[a], outs[a].at[_lin(*me)], local_sems.at[a]) for a in range(k)]

        def first():
            out = []
            for a in range(k):
                out.append(copy(a, 0, me, sibling, True))
                out += [copy(a, 1 + j, me, (*chip, c), True) for j, chip in enumerate(chips)]
            return out

        def passed():
            return [copy(a, 4 + j, (*chip, c), sibling) for j, chip in enumerate(chips) for a in range(k)]

        def start():
            for cp in mine() + first():
                cp.start()

        def middle():
            onward = passed()
            for j, chip in enumerate(chips):
                for a in range(k):
                    copy(a, 1 + j, (*chip, c), me).wait_recv()
                    onward[j * k + a].start()

        def finish():
            for a in range(k):
                copy(a, 0, sibling, me).wait_recv()
                for j, chip in enumerate(chips):
                    copy(a, 4 + j, (*chip, 1 - c), me).wait_recv()
            for cp in first() + passed():
                cp.wait_send()
            for cp in mine():
                cp.wait()

        return start, middle, finish

    def _scatter(self, ins, outs, send_sems, recv_sems, local_sems):
        k = self.k
        x, y, c = _my_place()
        me = _lin(x, y, c)
        peers = [((1 - x) if r & 4 else x, (1 - y) if r & 2 else y, (1 - c) if r & 1 else c) for r in range(1, N_DEV)]

        def copy(a, r, src_slab, dst_slab, to):
            return pltpu.make_async_remote_copy(
                src_ref=ins[a].at[src_slab], dst_ref=outs[a].at[dst_slab],
                send_sem=send_sems.at[a, r], recv_sem=recv_sems.at[a, r],
                device_id=to, device_id_type=MESH_ID)

        def mine():
            return [pltpu.make_async_copy(ins[a].at[me], outs[a].at[me], local_sems.at[a]) for a in range(k)]

        def sends():
            return [copy(a, r, _lin(*peer), me, peer) for r, peer in enumerate(peers) for a in range(k)]

        def start():
            for cp in mine() + sends():
                cp.start()

        def middle():
            pass

        def finish():
            for r, peer in enumerate(peers):
                for a in range(k):
                    copy(a, r, me, _lin(*peer), peer).wait_recv()
            for cp in sends():
                cp.wait_send()
            for cp in mine():
                cp.wait()

        return start, middle, finish

    def run(self, name):
        k = self.k

        def body(*refs):
            start, middle, finish = self.phases(refs[:k], refs[k:2 * k], refs[2 * k:])
            start()
            middle()
            finish()

        return pl.pallas_call(
            body,
            out_shape=tuple(self.out_shapes()),
            in_specs=[ANY] * k,
            out_specs=tuple(ANY for _ in range(k)),
            scratch_shapes=self.scratch(),
            name=name,
        )(*self.arrs)


def all_gather(arrs, name):
    return Exchange("gather", arrs).run(name)


def all_to_all(arrs, name):
    return Exchange("scatter", arrs).run(name)


def slab_sum(a, name):
    _, r, w = a.shape
    tr = _pick(r, (256, 128, 64, 32, 16, 8))

    def body(a_ref, o_ref):
        acc = a_ref[0].astype(F32)
        for d in range(1, N_DEV):
            acc = acc + a_ref[d].astype(F32)
        o_ref[...] = acc

    return pl.pallas_call(
        body,
        out_shape=_sds((r, w), F32),
        grid=(r // tr,),
        in_specs=[pl.BlockSpec((N_DEV, tr, w), lambda i: (0, i, 0))],
        out_specs=pl.BlockSpec((tr, w), lambda i: (i, 0)),
        compiler_params=_cp(("parallel",)),
        name=name,
    )(a)


def _adam_update(w, g, m, v):
    nm = ADAM_B1 * m + (1.0 - ADAM_B1) * g
    nv = ADAM_B2 * v + (1.0 - ADAM_B2) * (g * g)
    m_hat = nm / (1.0 - ADAM_B1 ** ADAM_STEP)
    v_hat = nv / (1.0 - ADAM_B2 ** ADAM_STEP)
    return -ADAM_LR * (m_hat / (jnp.sqrt(v_hat) + ADAM_EPS) + ADAM_WD * w), nm, nv


def adamw_blocks(w, m, v, parts, name):
    nl, r, c = w.shape
    tr = next(t for t in (256, 160, 128, 64, 32, 16) if r % t == 0 and N_DEV * t * c * 2 <= 2 * 1024 * 1024)

    def body(w_ref, m_ref, v_ref, *rest):
        part_refs, (g_ref, d_ref, nm_ref, nv_ref) = rest[:nl], rest[nl:]
        layer = pl.program_id(0)
        for idx in range(nl):
            @pl.when(layer == idx)
            def _(idx=idx):
                g = part_refs[idx][0].astype(F32)
                for dev in range(1, N_DEV):
                    g = g + part_refs[idx][dev].astype(F32)
                g_ref[...] = g
                d_ref[...], nm_ref[...], nv_ref[...] = _adam_update(w_ref[...], g, m_ref[...], v_ref[...])

    spec = pl.BlockSpec((None, tr, c), lambda l, i: (l, i, 0))
    part_spec = lambda idx: pl.BlockSpec((N_DEV, tr, c), lambda l, i: (0, jnp.where(l == idx, i, 0), 0))
    return pl.pallas_call(
        body,
        out_shape=tuple(_sds((nl, r, c), F32) for _ in range(4)),
        grid=(nl, r // tr),
        in_specs=[spec] * 3 + [part_spec(idx) for idx in range(nl)],
        out_specs=(spec,) * 4,
        compiler_params=_cp(("arbitrary", "arbitrary")),
        name=name,
    )(w, m, v, *parts)


def adamw(w, g, m, v, name):
    r, c = w.shape
    tr = _pick(r, (256, 160, 128, 64, 32, 16, 8))

    def body(w_ref, g_ref, m_ref, v_ref, d_ref, nm_ref, nv_ref):
        d_ref[...], nm_ref[...], nv_ref[...] = _adam_update(w_ref[...], g_ref[...], m_ref[...], v_ref[...])

    spec = pl.BlockSpec((tr, c), lambda i: (i, 0))
    return pl.pallas_call(
        body,
        out_shape=tuple(_sds((r, c), F32) for _ in range(3)),
        grid=(r // tr,),
        in_specs=[spec] * 4,
        out_specs=(spec, spec, spec),
        compiler_params=_cp(("parallel",)),
        name=name,
    )(w, g, m, v)


def _relu2_epi(acc):
    r = jnp.maximum(acc, 0.0)
    return r * r, r


def _drelu2_epi(acc, r):
    return (acc * (2.0 * r.astype(F32)),)


def mlp_fwd(h, g_pre, g_post, w_up, w_down):
    hn = norm_fwd(h, g_pre, BF, name="mlp_norm")
    u, r = matmul(hn, w_up, "nn", (BF, BF), epi=_relu2_epi, name="mlp_up")
    d = matmul(u, w_down, "nn", name="mlp_down")
    return resadd_fwd(h, d, g_post, name="mlp_res"), (h, hn, u, r, d)


def mlp_bwd(res, dh2, g_pre, g_post, w_up, w_down):
    h, hn, u, r, d = res
    dd, dg_post = norm_bwd(d, g_post, dh2, mask_pad=True, out_dtype=BF, name="mlp_post_bwd")
    dw_down = matmul(u, dd, "tn", (BF,), name="mlp_dwdown").reshape(w_down.g8.shape)
    dp = matmul(dd, w_down, "nt", (BF,), epi=_drelu2_epi, extras=(r,), name="mlp_du")
    dw_up = matmul(hn, dp, "tn", (BF,), out_blocks=True, name="mlp_dwup")
    dhn = matmul(dp, w_up, "nt", name="mlp_dhn")
    dh, dg_pre = norm_bwd(h, g_pre, dhn, dres=dh2, name="mlp_pre_bwd")
    return dh, dict(mlp_pre_g=dg_pre, mlp_post_g=dg_post, w_up=dw_up, w_down=dw_down)


def rg_layer_fwd(h, g_pre, g_post, w_x, w_y, rgp, w_a, w_i, w_out):
    hn = norm_fwd(h, g_pre, BF, name="rg_norm")
    xr = matmul(hn, w_x, "nn", name="rg_in_x")
    gp = matmul(hn, w_y, "nn", name="rg_in_y")
    hg, hs = rg_fwd(xr, gp, rgp, w_a, w_i)
    m = matmul(hg, w_out, "nn", name="rg_out")
    return resadd_fwd(h, m, g_post, name="rg_res"), (h, hn, xr, gp, hg, hs, m)


def rg_layer_bwd(res, dh2, g_pre, g_post, w_x, w_y, rgp, w_a, w_i, w_out):
    h, hn, xr, gp, hg, hs, m = res
    dm, dg_post = norm_bwd(m, g_post, dh2, mask_pad=True, out_dtype=BF, name="rg_post_bwd")
    dw_out = matmul(hg, dm, "tn", name="rg_dwout")
    dhg = matmul(dm, w_out, "nt", name="rg_dhg")
    dxr, dgp, drgp, dwa, dwi = rg_bwd(xr, gp, rgp, w_a, w_i, hs, dhg)
    dw_x = matmul(hn, dxr, "tn", name="rg_dwx")
    dw_y = matmul(hn, dgp, "tn", name="rg_dwy")
    dhn = matmul([dxr, dgp], [w_x, w_y], "nt", name="rg_dhn")
    dh, dg_pre = norm_bwd(h, g_pre, dhn, dres=dh2, name="rg_pre_bwd")
    return dh, dict(mix_pre_g=dg_pre, mix_post_g=dg_post, rg_w_x=dw_x, rg_w_y=dw_y,
                    rg_conv_w=drgp[0:4], rg_conv_b=drgp[4], rg_b_a=drgp[5], rg_b_i=drgp[6], rg_lambda=drgp[7],
                    rg_w_a=dwa, rg_w_i=dwi, rg_w_out=dw_out)


def sm_layer_fwd(h, g_pre, g_post, w_in_p, convp, dt_bias, a_log, dskip, ssd_g, q_g, w_q_p, kv_g, w_kv_p, w_out, tabs,
                 carry=None, on_carried=None):
    hn = norm_fwd(h, g_pre, BF, name="sm_norm")
    proj = matmul(hn, w_in_p, "nn", name="sm_in")
    xbc_c = conv_silu_fwd(proj, OFF_XBC, SSD_CONV_CH, convp, name="ssd_conv")
    y, hst = ssd_fwd(xbc_c, proj, dt_bias, a_log, dskip)
    y_ssd = gated_norm_fwd(y, proj, ssd_g)
    cqn = norm_fwd(proj, q_g, BF, col_blk=OFF_CQ // MLA_Q_RANK, width=MLA_Q_RANK, name="q_norm")
    q_raw = matmul(cqn, w_q_p, "nn", name="q_up")
    ckvn = norm_fwd(proj, kv_g, BF, col_blk=OFF_CKV // MLA_KV_RANK, width=MLA_KV_RANK, name="kv_norm")
    kv_raw = matmul(ckvn, w_kv_p, "nn", name="kv_up")
    q_cat, k_cat, v = rope_fwd(q_raw, kv_raw, proj, tabs)
    o, lse, carried = attn_fwd(q_cat, k_cat, v, carry)
    if on_carried is not None:
        on_carried(carried)
    w_out = w_out()
    half = w_out.shape[0] // 2
    m = matmul([y_ssd, o], [KBlock(w_out, half, 0), KBlock(w_out, half, 1)], "nn", name="sm_out")
    res = (h, hn, proj, xbc_c, y, hst, cqn, ckvn, q_cat, k_cat, v, o, lse, y_ssd, m)
    return resadd_fwd(h, m, g_post, name="sm_res"), res


def sm_layer_bwd(res, dh2, g_pre, g_post, w_in_p, convp, dt_bias, a_log, dskip, ssd_g, q_g, w_q_p, kv_g, w_kv_p, w_out, tabs,
                 carry=None):
    h, hn, proj, xbc_c, y, hst, cqn, ckvn, q_cat, k_cat, v, o, lse, y_ssd, m = res
    w_out = w_out()
    dm, dg_post = norm_bwd(m, g_post, dh2, mask_pad=True, out_dtype=BF, name="sm_post_bwd")
    dw_out = jnp.concatenate([matmul(y_ssd, dm, "tn", name="sm_dwout_ssd"), matmul(o, dm, "tn", name="sm_dwout_att")], axis=0)
    dyab = matmul(dm, w_out, "nt", name="sm_dyab")
    dq_cat, dk_cat, dv, carried = attn_bwd(q_cat, k_cat, v, o, lse, dyab, carry(dw_out) if carry is not None else None)
    dq_raw, dkr = rope_bwd(dq_cat, dk_cat, tabs)
    kw = MLA_HEADS * LANE
    dw_kv_p = jnp.concatenate([matmul(ckvn, dk_cat, "tn", name="kv_dw_k"), matmul(ckvn, dv, "tn", name="kv_dw_v")], axis=1)
    dckvn = matmul([dk_cat, dv], [KBlock(w_kv_p, kw, 0), KBlock(w_kv_p, kw // 2, 2)], "nt", name="kv_dx")
    dckv, dg_kv = norm_bwd(proj, kv_g, dckvn, out_dtype=BF, col_blk=OFF_CKV // MLA_KV_RANK, width=MLA_KV_RANK,
                           name="kv_norm_bwd")
    dw_q_p = matmul(cqn, dq_raw, "tn", name="q_dw")
    dcqn = matmul(dq_raw, w_q_p, "nt", name="q_dx")
    dcq, dg_q = norm_bwd(proj, q_g, dcqn, out_dtype=BF, col_blk=OFF_CQ // MLA_Q_RANK, width=MLA_Q_RANK, name="q_norm_bwd")
    dy, dz, dg_ssd = gated_norm_bwd(y, proj, ssd_g, dyab)
    dxbc_c, ddt, dpar = ssd_bwd(xbc_c, proj, dt_bias, a_log, dskip, hst, dy)
    dxbc, dconvp = conv_silu_bwd(proj, OFF_XBC, SSD_CONV_CH, convp, dxbc_c, name="ssd_conv_bwd")
    pieces = [dz, dxbc, dckv, ddt, dkr, dcq]
    dw_in_p = jnp.concatenate([matmul(hn, pc, "tn", (BF,), name="sm_dwin_%d" % i) for i, pc in enumerate(pieces)], axis=1)
    third = SSD_CONV_CH // 3
    a_terms = [dz] + [KBlock(dxbc, third, i) for i in range(3)] + [dckv, ddt, dkr, dcq]
    b_terms = ([KBlock(w_in_p, SSD_D_INNER, 0)] + [KBlock(w_in_p, third, OFF_XBC // third + i) for i in range(3)]
               + [KBlock(w_in_p, MLA_KV_RANK, OFF_CKV // MLA_KV_RANK), KBlock(w_in_p, LANE, OFF_DT // LANE),
                  KBlock(w_in_p, LANE, OFF_KR // LANE), KBlock(w_in_p, MLA_Q_RANK, OFF_CQ // MLA_Q_RANK)])
    dhn = matmul(a_terms, b_terms, "nt", name="sm_dhn")
    dh, dg_pre = norm_bwd(h, g_pre, dhn, dres=dh2, name="sm_pre_bwd")
    grads = dict(mix_pre_g=dg_pre, mix_post_g=dg_post, w_in=w_in_cols_to_blocks(dw_in_p), ssd_conv_w=dconvp[0:4],
                 ssd_conv_b=dconvp[4], ssd_dt_bias=dpar[0, :SSD_HEADS], ssd_a_log=dpar[1, :SSD_HEADS],
                 ssd_d=dpar[2, :SSD_HEADS], ssd_norm_g=dg_ssd, mla_q_norm_g=dg_q, mla_w_q_up=_unpack_w_q(dw_q_p),
                 mla_kv_norm_g=dg_kv, mla_w_kv_up=_unpack_w_kv(dw_kv_p), w_out_ab=dw_out)
    return dh, grads, carried


W_IN_COLS = 3248
W_IN_SHARD = W_IN_COLS // N_DEV
W_IN_WIRE = 512


def _w_in_tables():
    src = np.full((IN_W,), -1, np.int64)
    src[0:2560] = np.arange(2560)
    src[OFF_CKV:OFF_CKV + 256] = 2960 + np.arange(256)
    src[OFF_DT:OFF_DT + 16] = 2560 + np.arange(16)
    src[OFF_KR + 64:OFF_KR + 96] = 3216 + np.arange(32)
    src[OFF_CQ:OFF_CQ + 384] = 2576 + np.arange(384)
    dev = np.where(src >= 0, src // W_IN_SHARD, -1).astype(np.int32).reshape(1, IN_W)
    col = np.where(src >= 0, src % W_IN_SHARD, 0).astype(np.int32).reshape(1, IN_W)
    return dev, col


W_IN_TILE = 384


def _w_in_devices_of_tile(dev):
    return [sorted(set(dev[0, t * W_IN_TILE:(t + 1) * W_IN_TILE].tolist()) - {-1}) for t in range(IN_W // W_IN_TILE)]


def _any_of(index, values):
    cond = index == values[0]
    for v in values[1:]:
        cond = cond | (index == v)
    return cond


def w_in_blocks_to_cols(g8):
    _, k, wp = g8.shape
    tn = W_IN_TILE
    dev, col = _w_in_tables()
    holders = _w_in_devices_of_tile(dev)

    def body(g_ref, dev_ref, col_ref, o_ref):
        i = pl.program_id(0)
        row = lax.broadcasted_iota(jnp.int32, (wp, tn), 0)
        o_ref[...] = jnp.zeros_like(o_ref)
        for j in range(N_DEV):
            tiles = [t for t, devs in enumerate(holders) if j in devs]
            if tiles:
                @pl.when(_any_of(i, tiles))
                def _(j=j):
                    sel = ((row == col_ref[...]) & (dev_ref[...] == j)).astype(BF)
                    o_ref[...] += jnp.dot(g_ref[j], sel, preferred_element_type=F32).astype(o_ref.dtype)

    dev, col = jnp.asarray(dev), jnp.asarray(col)
    return pl.pallas_call(
        body,
        out_shape=_sds((k, IN_W), BF),
        grid=(IN_W // tn,),
        in_specs=[pl.BlockSpec((N_DEV, k, wp), lambda i: (0, 0, 0)), pl.BlockSpec((1, tn), lambda i: (0, i)),
                  pl.BlockSpec((1, tn), lambda i: (0, i))],
        out_specs=pl.BlockSpec((k, tn), lambda i: (0, i)),
        compiler_params=_cp(("parallel",)),
        name="w_in_cols",
    )(g8, dev, col)


def w_in_cols_to_blocks(dw):
    k = dw.shape[0]
    tn = W_IN_TILE
    dev, col = _w_in_tables()
    holders = _w_in_devices_of_tile(dev)

    def body(dw_ref, dev_ref, col_ref, o_ref):
        j = pl.program_id(0)
        row = lax.broadcasted_iota(jnp.int32, (W_IN_WIRE, tn), 0)
        o_ref[...] = jnp.zeros_like(o_ref)
        for t, devs in enumerate(holders):
            if devs:
                @pl.when(_any_of(j, devs))
                def _(t=t):
                    cols = slice(t * tn, (t + 1) * tn)
                    sel = ((row == col_ref[:, cols]) & (dev_ref[:, cols] == j)).astype(BF)
                    o_ref[0] += lax.dot_general(dw_ref[:, cols], sel, NT_DIMS,
                                                preferred_element_type=F32).astype(o_ref.dtype)

    dev, col = jnp.asarray(dev), jnp.asarray(col)
    return pl.pallas_call(
        body,
        out_shape=_sds((N_DEV, k, W_IN_WIRE), BF),
        grid=(N_DEV,),
        in_specs=[pl.BlockSpec((k, IN_W), lambda j: (0, 0)), pl.BlockSpec((1, IN_W), lambda j: (0, 0)),
                  pl.BlockSpec((1, IN_W), lambda j: (0, 0))],
        out_specs=pl.BlockSpec((1, k, W_IN_WIRE), lambda j: (j, 0, 0)),
        compiler_params=_cp(("parallel",)),
        name="w_in_blocks",
    )(dw, dev, col)


def _pack_w_q(w):
    w3 = w.reshape(w.shape[0], MLA_HEADS, MLA_NOPE + MLA_ROPE)
    return jnp.pad(w3, ((0, 0), (0, 0), (0, LANE - MLA_NOPE - MLA_ROPE))).reshape(w.shape[0], MLA_HEADS * LANE)


def _unpack_w_q(p):
    return p.reshape(p.shape[0], MLA_HEADS, LANE)[:, :, :MLA_NOPE + MLA_ROPE].reshape(p.shape[0], -1)


def _pack_w_kv(w):
    w3 = w.reshape(w.shape[0], MLA_HEADS, MLA_NOPE + MLA_V)
    k = jnp.pad(w3[:, :, :MLA_NOPE], ((0, 0), (0, 0), (0, LANE - MLA_NOPE))).reshape(w.shape[0], MLA_HEADS * LANE)
    return jnp.concatenate([k, w3[:, :, MLA_NOPE:].reshape(w.shape[0], MLA_HEADS * MLA_V)], axis=1)


def _unpack_w_kv(p):
    k = p[:, :MLA_HEADS * LANE].reshape(p.shape[0], MLA_HEADS, LANE)[:, :, :MLA_NOPE]
    v = p[:, MLA_HEADS * LANE:].reshape(p.shape[0], MLA_HEADS, MLA_V)
    return jnp.concatenate([k, v], axis=2).reshape(p.shape[0], -1)


def _rows8(rows, width):
    a = jnp.concatenate([r.reshape(-1, width) for r in rows], axis=0)
    return jnp.pad(a, ((0, 8 - a.shape[0]), (0, 0)))


SLAB_ROWS = 256


def _to_slab(flat_list, lead=()):
    cat = jnp.concatenate(flat_list, axis=-1)
    n = cat.shape[-1]
    unit = SLAB_ROWS * PACK_W
    total = -(-n // unit) * unit
    cat = jnp.pad(cat, [(0, 0)] * len(lead) + [(0, total - n)])
    return cat.reshape(lead + (total // PACK_W, PACK_W))


def _from_flat(flat, shapes):
    out, off = [], 0
    for s in shapes:
        n = int(np.prod(s))
        out.append(flat[off:off + n].reshape(s))
        off += n
    return out


def _gathered_full(g8, axis):
    moved = jnp.moveaxis(g8, 0, axis)
    shp = moved.shape
    return moved.reshape(shp[:axis] + (shp[axis] * shp[axis + 1],) + shp[axis + 2:])


def _per_device(full, axis):
    shp = full.shape
    split = full.reshape(shp[:axis] + (N_DEV, shp[axis] // N_DEV) + shp[axis + 1:])
    return jnp.moveaxis(split, axis, 0)


ARG_NAMES = ['x', 'meta_tokens', 'mix_pre_g', 'mix_post_g', 'mlp_pre_g', 'mlp_post_g', 'w_up', 'w_down', 'w_in',
             'ssd_conv_w', 'ssd_conv_b', 'ssd_dt_bias', 'ssd_a_log', 'ssd_d', 'ssd_norm_g', 'mla_q_norm_g',
             'mla_w_q_up', 'mla_kv_norm_g', 'mla_w_kv_up', 'w_out_ab', 'rg_w_x', 'rg_w_y', 'rg_conv_w', 'rg_conv_b',
             'rg_w_a', 'rg_b_a', 'rg_w_i', 'rg_b_i', 'rg_lambda', 'rg_w_out']
WEIGHTS = ARG_NAMES[1:]
BIG = {'w_up': 2, 'w_down': 1, 'w_in': 2, 'mla_w_q_up': 2, 'mla_w_kv_up': 2, 'w_out_ab': 1, 'rg_w_x': 2,
       'rg_w_y': 2, 'rg_w_out': 1}
SMALL = {'meta_tokens': 1, 'ssd_conv_w': 2, 'rg_conv_w': 2, 'rg_conv_b': 1, 'rg_b_a': 1, 'rg_b_i': 1, 'rg_lambda': 1}
REPL = [n for n in WEIGHTS if n not in BIG and n not in SMALL]


def _piece_axes():
    axes = {}
    for n, ax in BIG.items():
        for i in range(DEPTH if n in ('w_up', 'w_down') else DEPTH // 2):
            axes[(n, i)] = ax - 1
    return axes


PIECE_AXIS = _piece_axes()
AS_BLOCKS = ('w_up', 'w_down')
_SM = lambda i: [(n, i) for n in ('w_in', 'mla_w_q_up', 'mla_w_kv_up', 'w_out_ab')]
_RG = lambda i: [(n, i) for n in ('rg_w_x', 'rg_w_y', 'rg_w_out')]
_MLP = lambda l: [('w_up', l), ('w_down', l)]
_SM_IN = lambda i: [(n, i) for n in ('w_in', 'mla_w_q_up', 'mla_w_kv_up')]
GATHER_FIRST = _SM_IN(0)
GATHER_AT = {0: [('w_out_ab', 0)] + _MLP(0) + _RG(0) + _MLP(1) + _SM(1), 2: _MLP(2) + _RG(1) + _MLP(3)}
SCATTER_AT = {2: _MLP(3) + _RG(1) + _MLP(2) + [('w_out_ab', 1)],
              0: _SM_IN(1) + _MLP(1) + _RG(0) + _MLP(0) + [('w_out_ab', 0)]}
SCATTER_LAST = _SM_IN(0)


def _wire_block(p, key):
    n, i = key
    blk = p[n][i]
    if n == 'w_in':
        blk = jnp.pad(blk, ((0, 0), (0, W_IN_WIRE - blk.shape[1])))
    return blk


def _step(p, moments):
    assert DEPTH == 4
    full = {n: [None] * p[n].shape[0] for n in BIG}
    full['w_in_g'] = [None] * p['w_in'].shape[0]

    def weight_blocks(group):
        return [_wire_block(p, k).astype(BF) for k in group]

    def take_weights(group, gathered):
        for (n, i), piece in zip(group, gathered):
            if n == 'w_in':
                full['w_in_g'][i] = piece
            elif n in AS_BLOCKS:
                full[n][i] = DevBlocks(piece, PIECE_AXIS[(n, i)])
            else:
                full[n][i] = _gathered_full(piece, PIECE_AXIS[(n, i)])

    def grad_blocks(group, gw):
        return [gw[k] if k[0] in AS_BLOCKS or k[0] == 'w_in' else _per_device(gw[k], PIECE_AXIS[k]).astype(BF)
                for k in group]

    parts = {}

    small_slab = _to_slab([p[n].reshape(-1) for n in SMALL])
    *first, small8 = all_gather(weight_blocks(GATHER_FIRST) + [small_slab], name="gather_first")
    take_weights(GATHER_FIRST, first)
    for n, piece in zip(SMALL, _from_flat_rows(small8, [p[n].shape for n in SMALL])):
        full[n] = _gathered_full(piece, SMALL[n])
    for n in REPL:
        full[n] = p[n]
    loss_local, grad_x, gw, gsmall_full, carried = _local_step(
        full, p['x'][0], p['loss_target'][0],
        fwd_carry=lambda layer: Exchange("gather", weight_blocks(GATHER_AT[layer])),
        on_fwd_carried=lambda layer, got: take_weights(GATHER_AT[layer], got),
        bwd_carry=lambda layer, gw_now: Exchange("scatter", grad_blocks(SCATTER_AT[layer], gw_now)))

    for layer, group in SCATTER_AT.items():
        parts.update(zip(group, carried[layer]))
    rep_flat = jnp.concatenate([gsmall_full[n].reshape(-1) for n in REPL])
    rep_n = rep_flat.shape[0]
    rep_chunk = -(-rep_n // (N_DEV * PACK_W * 8)) * PACK_W * 8
    rep8 = jnp.pad(rep_flat, (0, N_DEV * rep_chunk - rep_n)).reshape(N_DEV, rep_chunk)
    gsmall = _to_slab([_per_device(gsmall_full[n], SMALL[n]).reshape(N_DEV, -1) for n in SMALL] + [rep8], lead=(N_DEV,))
    *rlast, rsmall = all_to_all(grad_blocks(SCATTER_LAST, gw) + [gsmall], name="scatter_last")
    parts.update(zip(SCATTER_LAST, rlast))
    ssmall = slab_sum(rsmall, name="sum_small").reshape(-1)
    g_loc = {'w_in': jnp.stack([slab_sum(parts[('w_in', i)], name="sum_w_in_%d" % i)[:, :W_IN_SHARD]
                                for i in range(p['w_in'].shape[0])], axis=0)}
    small_n = sum(int(np.prod(p[n].shape)) for n in SMALL)
    g_loc.update(zip(SMALL, _from_flat(ssmall, [p[n].shape for n in SMALL])))
    rep_mine = ssmall[small_n:small_n + rep_chunk].reshape(-1, PACK_W)
    (rep_all,) = all_gather([rep_mine], name="gather_replicated")
    g_loc.update(zip(REPL, _from_flat(rep_all.reshape(-1), [p[n].shape for n in REPL])))

    out = {'loss': lax.psum(loss_local, ("x", "y", "c")), 'grad_x': grad_x[None]}
    small_names = list(SMALL) + REPL
    for n in BIG:
        shp = p[n].shape
        if n == 'w_in':
            v2 = lambda a: a.reshape(-1, shp[-1])
            d, nm, nv = adamw(v2(p[n]), v2(g_loc[n]), v2(moments['m_' + n]), v2(moments['v_' + n]), name="adamw_" + n)
            d, nm, nv = d.reshape(shp), nm.reshape(shp), nv.reshape(shp)
        else:
            g_loc[n], d, nm, nv = adamw_blocks(p[n], moments['m_' + n], moments['v_' + n],
                                               [parts[(n, i)] for i in range(shp[0])], name="adamw_" + n)
        out['delta_' + n], out['new_m_' + n], out['new_v_' + n] = d, nm, nv
    slab = lambda src: _to_slab([src(n).reshape(-1) for n in small_names])
    d, nm, nv = adamw(slab(lambda n: p[n]), slab(lambda n: g_loc[n]), slab(lambda n: moments['m_' + n]),
                      slab(lambda n: moments['v_' + n]), name="adamw_small")
    shapes = [p[n].shape for n in small_names]
    for key, flat in (('delta_', d), ('new_m_', nm), ('new_v_', nv)):
        for n, a in zip(small_names, _from_flat(flat.reshape(-1), shapes)):
            out[key + n] = a
    for n in WEIGHTS:
        out['grad_' + n] = g_loc[n]
    return out


def _local_step(full, x, target_rows, fwd_carry=None, on_fwd_carried=None, bwd_carry=None):
    t = PAD + N_META + x.shape[0]
    h = jnp.concatenate([jnp.zeros((PAD, D_MODEL), F32), full['meta_tokens'], x], axis=0)
    target = jnp.concatenate([jnp.zeros((PAD + N_META, D_MODEL), F32), target_rows], axis=0)
    tabs = rope_tables(t)

    def layer_args(layer):
        i = layer // 2
        if layer % 2 == 0:
            convp = _rows8([full['ssd_conv_w'][i], full['ssd_conv_b'][i]], SSD_CONV_CH)
            return (full['mix_pre_g'][layer], full['mix_post_g'][layer], w_in_blocks_to_cols(full['w_in_g'][i]), convp,
                    full['ssd_dt_bias'][i], full['ssd_a_log'][i], full['ssd_d'][i], full['ssd_norm_g'][i],
                    full['mla_q_norm_g'][i], _pack_w_q(full['mla_w_q_up'][i]), full['mla_kv_norm_g'][i],
                    _pack_w_kv(full['mla_w_kv_up'][i]), lambda: full['w_out_ab'][i], tabs)
        rgp = _rows8([full['rg_conv_w'][i], full['rg_conv_b'][i], full['rg_b_a'][i], full['rg_b_i'][i],
                      full['rg_lambda'][i]], LRU_WIDTH)
        return (full['mix_pre_g'][layer], full['mix_post_g'][layer], full['rg_w_x'][i], full['rg_w_y'][i], rgp,
                full['rg_w_a'][i], full['rg_w_i'][i], full['rg_w_out'][i])

    def mlp_args(layer):
        return (full['mlp_pre_g'][layer], full['mlp_post_g'][layer], full['w_up'][layer], full['w_down'][layer])

    saved = []
    for layer in range(DEPTH):
        la = layer_args(layer)
        if layer % 2 == 0:
            if fwd_carry is not None:
                h, res_mix = sm_layer_fwd(h, *la, carry=fwd_carry(layer),
                                          on_carried=lambda got, layer=layer: on_fwd_carried(layer, got))
            else:
                h, res_mix = sm_layer_fwd(h, *la)
        else:
            h, res_mix = rg_layer_fwd(h, *la)
        ma = mlp_args(layer)
        h, res_mlp = mlp_fwd(h, *ma)
        saved.append((la, ma, res_mix, res_mlp))
    loss_local, dh = loss_fwd_bwd(h, target)

    others = {n: [None] * len(full[n]) for n in WEIGHTS if n not in BIG and n != 'meta_tokens'}
    gw, carried = {}, {}
    for layer in reversed(range(DEPTH)):
        la, ma, res_mix, res_mlp = saved[layer]
        dh, gm = mlp_bwd(res_mlp, dh, *ma)
        if layer % 2 == 0:
            for n in ('w_up', 'w_down'):
                gw[(n, layer)] = gm[n]
            carry = None
            if bwd_carry is not None:
                carry = lambda dw_out, layer=layer: bwd_carry(layer, {**gw, ('w_out_ab', layer // 2): dw_out})
            dh, gx, carried[layer] = sm_layer_bwd(res_mix, dh, *la, carry=carry)
        else:
            dh, gx = rg_layer_bwd(res_mix, dh, *la)
        for n, g in list(gm.items()) + list(gx.items()):
            i = layer if n in ('mix_pre_g', 'mix_post_g', 'mlp_pre_g', 'mlp_post_g', 'w_up', 'w_down') else layer // 2
            if n in BIG:
                gw[(n, i)] = g
            else:
                others[n][i] = g
    gothers = {n: jnp.stack(v, axis=0) for n, v in others.items()}
    gothers['meta_tokens'] = dh[PAD:PAD + N_META]
    return loss_local, dh[PAD + N_META:], gw, gothers, carried


def _from_flat_rows(g8, shapes):
    flat = g8.reshape(N_DEV, -1)
    out, off = [], 0
    for s in shapes:
        n = int(np.prod(s))
        out.append(flat[:, off:off + n].reshape((N_DEV,) + tuple(s)))
        off += n
    return out


def kernel(x, meta_tokens, mix_pre_g, mix_post_g, mlp_pre_g, mlp_post_g, w_up, w_down, w_in, ssd_conv_w, ssd_conv_b, ssd_dt_bias, ssd_a_log, ssd_d, ssd_norm_g, mla_q_norm_g, mla_w_q_up, mla_kv_norm_g, mla_w_kv_up, w_out_ab, rg_w_x, rg_w_y, rg_conv_w, rg_conv_b, rg_w_a, rg_b_a, rg_w_i, rg_b_i, rg_lambda, rg_w_out, loss_target, m_meta_tokens, m_mix_pre_g, m_mix_post_g, m_mlp_pre_g, m_mlp_post_g, m_w_up, m_w_down, m_w_in, m_ssd_conv_w, m_ssd_conv_b, m_ssd_dt_bias, m_ssd_a_log, m_ssd_d, m_ssd_norm_g, m_mla_q_norm_g, m_mla_w_q_up, m_mla_kv_norm_g, m_mla_w_kv_up, m_w_out_ab, m_rg_w_x, m_rg_w_y, m_rg_conv_w, m_rg_conv_b, m_rg_w_a, m_rg_b_a, m_rg_w_i, m_rg_b_i, m_rg_lambda, m_rg_w_out, v_meta_tokens, v_mix_pre_g, v_mix_post_g, v_mlp_pre_g, v_mlp_post_g, v_w_up, v_w_down, v_w_in, v_ssd_conv_w, v_ssd_conv_b, v_ssd_dt_bias, v_ssd_a_log, v_ssd_d, v_ssd_norm_g, v_mla_q_norm_g, v_mla_w_q_up, v_mla_kv_norm_g, v_mla_w_kv_up, v_w_out_ab, v_rg_w_x, v_rg_w_y, v_rg_conv_w, v_rg_conv_b, v_rg_w_a, v_rg_b_a, v_rg_w_i, v_rg_b_i, v_rg_lambda, v_rg_w_out):
    args = (x, meta_tokens, mix_pre_g, mix_post_g, mlp_pre_g, mlp_post_g, w_up, w_down, w_in, ssd_conv_w, ssd_conv_b, ssd_dt_bias, ssd_a_log, ssd_d, ssd_norm_g, mla_q_norm_g, mla_w_q_up, mla_kv_norm_g, mla_w_kv_up, w_out_ab, rg_w_x, rg_w_y, rg_conv_w, rg_conv_b, rg_w_a, rg_b_a, rg_w_i, rg_b_i, rg_lambda, rg_w_out, loss_target, m_meta_tokens, m_mix_pre_g, m_mix_post_g, m_mlp_pre_g, m_mlp_post_g, m_w_up, m_w_down, m_w_in, m_ssd_conv_w, m_ssd_conv_b, m_ssd_dt_bias, m_ssd_a_log, m_ssd_d, m_ssd_norm_g, m_mla_q_norm_g, m_mla_w_q_up, m_mla_kv_norm_g, m_mla_w_kv_up, m_w_out_ab, m_rg_w_x, m_rg_w_y, m_rg_conv_w, m_rg_conv_b, m_rg_w_a, m_rg_b_a, m_rg_w_i, m_rg_b_i, m_rg_lambda, m_rg_w_out, v_meta_tokens, v_mix_pre_g, v_mix_post_g, v_mlp_pre_g, v_mlp_post_g, v_w_up, v_w_down, v_w_in, v_ssd_conv_w, v_ssd_conv_b, v_ssd_dt_bias, v_ssd_a_log, v_ssd_d, v_ssd_norm_g, v_mla_q_norm_g, v_mla_w_q_up, v_mla_kv_norm_g, v_mla_w_kv_up, v_w_out_ab, v_rg_w_x, v_rg_w_y, v_rg_conv_w, v_rg_conv_b, v_rg_w_a, v_rg_b_a, v_rg_w_i, v_rg_b_i, v_rg_lambda, v_rg_w_out,)
    n_w = len(ARG_NAMES)
    p = dict(zip(ARG_NAMES, args[:n_w]))
    p['loss_target'] = args[n_w]
    moments = {}
    for i, n in enumerate(WEIGHTS):
        moments['m_' + n] = args[n_w + 1 + i]
        moments['v_' + n] = args[n_w + 1 + len(WEIGHTS) + i]
    out = _step(p, moments)
    res = [out['loss'], out['grad_x']]
    for prefix in ('grad_', 'delta_', 'new_m_', 'new_v_'):
        res += [out[prefix + n] for n in WEIGHTS]
    return tuple(res)
```

```python
import functools
import math

import numpy as np
import jax
import jax.numpy as jnp
from jax import lax
from jax.experimental import pallas as pl
from jax.experimental.pallas import tpu as pltpu

F32 = jnp.float32
BF = jnp.bfloat16
HI = lax.Precision.HIGHEST

D_MODEL = 1024
DEPTH = 4
N_META = 16
CHUNK = 128
PAD = CHUNK - N_META
EPS = 1e-6
SSD_HEADS = 16
SSD_HEAD_DIM = 64
SSD_D_INNER = 1024
SSD_STATE = 128
SSD_CONV_CH = 1536
MLA_HEADS = 16
MLA_NOPE = 64
MLA_ROPE = 32
MLA_V = 64
MLA_Q_RANK = 384
MLA_KV_RANK = 256
ROPE_BASE = 10000.0
LRU_WIDTH = 1280
LRU_BLOCKS = 10
LRU_C = 8.0
D_FF = 4096
N_DEV = 8
LANE = 128
IN_W = 3456
OFF_Z, OFF_XBC, OFF_CKV, OFF_DT, OFF_KR, OFF_CQ = 0, 1024, 2560, 2816, 2944, 3072

ADAM_LR = 0.001
ADAM_B1 = 0.9
ADAM_B2 = 0.999
ADAM_EPS = 1e-08
ADAM_WD = 0.01
ADAM_STEP = 10

VMEM_LIMIT = 56 * 1024 * 1024
NEG = -1e30


def _pick(n, cands):
    for c in cands:
        if n % c == 0:
            return c
    return n


def _cp(sem=None):
    return pltpu.CompilerParams(dimension_semantics=sem, vmem_limit_bytes=VMEM_LIMIT)


def _sds(shape, dtype):
    return jax.ShapeDtypeStruct(tuple(shape), dtype)


def _silu(x):
    return x * jax.nn.sigmoid(x)


def _softplus(x):
    return jnp.maximum(x, 0.0) + jnp.log(1.0 + jnp.exp(-jnp.abs(x)))


def _gelu(x):
    c = math.sqrt(2.0 / math.pi)
    return 0.5 * x * (1.0 + jnp.tanh(c * (x + 0.044715 * (x * x * x))))


def _row_mask(i, tr, shape, first_valid=PAD):
    row = i * tr + lax.broadcasted_iota(jnp.int32, shape, 0)
    return row >= first_valid


class KBlock:
    def __init__(self, arr, width, blk):
        self.arr, self.width, self.blk = arr, width, blk


class DevBlocks:
    def __init__(self, g8, axis):
        self.g8, self.axis = g8, axis
        _, r, c = g8.shape
        self.shape = (N_DEV * r, c) if axis == 0 else (r, N_DEV * c)


NN_DIMS = (((1,), (0,)), ((), ()))
MM_TALL_K = 1536


def matmul(a, b, mode, out_dtypes=(F32,), epi=None, extras=(), name="mm", tm=None, tn=None, out_blocks=False):
    a_terms = a if isinstance(a, (list, tuple)) else [a]
    b_terms = b if isinstance(b, (list, tuple)) else [b]
    assert len(a_terms) == len(b_terms) and (mode != "tn" or len(a_terms) == 1)
    arr_of = lambda t: t.arr if isinstance(t, KBlock) else t
    if mode == "tn":
        m, n = a_terms[0].shape[1], b_terms[0].shape[1]
    else:
        m = arr_of(a_terms[0]).shape[0]
        b0 = b_terms[0]
        n = (b0.shape if isinstance(b0, DevBlocks) else arr_of(b0).shape)[1 if mode == "nn" else 0]
    if mode == "tn":
        tm = _pick(m, (512, 384, 256, 128))
    else:
        k_all = sum(t.width if isinstance(t, KBlock) else t.shape[1] for t in a_terms)
        tall = (2112,) if k_all <= MM_TALL_K else ()
        tm = tm or _pick(m, tall + (1056, 1024, 768, 640, 512, 384, 256, 128))
    tn = tn or _pick(n, (512, 640, 384, 256, 128))
    dims = {"nn": NN_DIMS, "nt": NT_DIMS, "tn": TN_DIMS}[mode]

    in_specs, args, plan = [], [], []
    for ta, tb in zip(a_terms, b_terms):
        if mode == "tn":
            k = ta.shape[0]
            in_specs += [pl.BlockSpec((k, tm), lambda i, j: (0, i)), pl.BlockSpec((k, tn), lambda i, j: (0, j))]
            args += [ta, tb]
            plan.append(None)
            continue
        if isinstance(ta, KBlock):
            kw, ka = ta.width, ta.blk
            in_specs.append(pl.BlockSpec((tm, kw), lambda i, j, ka=ka: (i, ka)))
        else:
            kw = ta.shape[1]
            in_specs.append(pl.BlockSpec((tm, kw), lambda i, j: (i, 0)))
        args.append(arr_of(ta))
        if isinstance(tb, DevBlocks):
            _, r, c = tb.g8.shape
            split_k = tb.axis == (0 if mode == "nn" else 1)
            if split_k:
                kd = r if mode == "nn" else c
                assert kw == N_DEV * kd
                blk = (N_DEV, kd, tn) if mode == "nn" else (N_DEV, tn, kd)
                in_specs.append(pl.BlockSpec(blk, (lambda i, j: (0, 0, j)) if mode == "nn" else (lambda i, j: (0, j, 0))))
                plan.append(kd)
            else:
                per = (c if mode == "nn" else r) // tn
                blk = (None, kw, tn) if mode == "nn" else (None, tn, kw)
                in_specs.append(pl.BlockSpec(blk, (lambda i, j, per=per: (j // per, 0, j % per)) if mode == "nn"
                                             else (lambda i, j, per=per: (j // per, j % per, 0))))
                plan.append(None)
            args.append(tb.g8)
        else:
            kb = tb.blk if isinstance(tb, KBlock) else 0
            assert (tb.width if isinstance(tb, KBlock) else tb.shape[0 if mode == "nn" else 1]) == kw
            in_specs.append(pl.BlockSpec((kw, tn), lambda i, j, kb=kb: (kb, j)) if mode == "nn"
                            else pl.BlockSpec((tn, kw), lambda i, j, kb=kb: (j, kb)))
            args.append(arr_of(tb))
            plan.append(None)
    n_terms, n_ex = len(plan), len(extras)

    def body(*refs):
        ex_refs, out_refs = refs[2 * n_terms:2 * n_terms + n_ex], refs[2 * n_terms + n_ex:]
        acc = None
        for t, kd in enumerate(plan):
            a_ref, b_ref = refs[2 * t], refs[2 * t + 1]
            if kd is None:
                parts = [lax.dot_general(a_ref[...].astype(BF), b_ref[...].astype(BF), dims, preferred_element_type=F32)]
            else:
                parts = [lax.dot_general(a_ref[:, d * kd:(d + 1) * kd].astype(BF), b_ref[d].astype(BF), dims,
                                         preferred_element_type=F32) for d in range(N_DEV)]
            for part in parts:
                acc = part if acc is None else acc + part
        outs = (acc,) if epi is None else epi(acc, *[r[...] for r in ex_refs])
        for r, o in zip(out_refs, outs):
            r[...] = o.astype(r.dtype)

    o_spec = pl.BlockSpec((tm, tn), lambda i, j: (i, j))
    if out_blocks:
        per = n // N_DEV // tn
        out_shape = tuple(_sds((N_DEV, m, n // N_DEV), dt) for dt in out_dtypes)
        out_specs = tuple(pl.BlockSpec((None, tm, tn), lambda i, j: (j // per, i, j % per)) for _ in out_dtypes)
    else:
        out_shape = tuple(_sds((m, n), dt) for dt in out_dtypes)
        out_specs = tuple(o_spec for _ in out_dtypes)
    outs = pl.pallas_call(
        body,
        out_shape=out_shape,
        grid=(m // tm, n // tn),
        in_specs=in_specs + [o_spec] * n_ex,
        out_specs=out_specs,
        compiler_params=_cp(("parallel", "parallel")),
        name=name,
    )(*args, *extras)
    return outs[0] if len(out_dtypes) == 1 else outs


def _rt(t):
    return _pick(t, (384, 256, 128))


def norm_fwd(x, g, out_dtype, col_blk=0, width=None, name="norm_fwd"):
    t = x.shape[0]
    w = width or x.shape[1]
    tr = _rt(t)

    def body(x_ref, g_ref, o_ref):
        xv = x_ref[...]
        r = lax.rsqrt(jnp.mean(xv * xv, axis=-1, keepdims=True) + EPS)
        o_ref[...] = (xv * r * g_ref[...]).astype(o_ref.dtype)

    return pl.pallas_call(
        body,
        out_shape=_sds((t, w), out_dtype),
        grid=(t // tr,),
        in_specs=[pl.BlockSpec((tr, w), lambda i: (i, col_blk)), pl.BlockSpec((1, w), lambda i: (0, 0))],
        out_specs=pl.BlockSpec((tr, w), lambda i: (i, 0)),
        compiler_params=_cp(("parallel",)),
        name=name,
    )(x, g.reshape(1, w))


def norm_bwd(x, g, dy, dres=None, mask_pad=False, out_dtype=F32, col_blk=0, width=None, dy_col_blk=0, name="norm_bwd"):
    t = x.shape[0]
    w = width or x.shape[1]
    tr = _rt(t)
    has_res = dres is not None

    def body(x_ref, g_ref, dy_ref, *rest):
        if has_res:
            res_ref, dx_ref, dg_ref = rest
        else:
            dx_ref, dg_ref = rest
        i = pl.program_id(0)
        xv = x_ref[...]
        dyv = dy_ref[...].astype(F32)
        if mask_pad:
            dyv = jnp.where(_row_mask(i, tr, dyv.shape), dyv, 0.0)
        r = lax.rsqrt(jnp.mean(xv * xv, axis=-1, keepdims=True) + EPS)
        xh = xv * r
        dyg = dyv * g_ref[...]
        dx = r * (dyg - xh * jnp.mean(dyg * xh, axis=-1, keepdims=True))
        if has_res:
            dx = dx + res_ref[...]
        dx_ref[...] = dx.astype(dx_ref.dtype)

        @pl.when(i == 0)
        def _():
            dg_ref[...] = jnp.zeros_like(dg_ref)

        dg_ref[...] += jnp.sum(dyv * xh, axis=0, keepdims=True)

    in_specs = [pl.BlockSpec((tr, w), lambda i: (i, col_blk)), pl.BlockSpec((1, w), lambda i: (0, 0)),
                pl.BlockSpec((tr, w), lambda i: (i, dy_col_blk))]
    args = [x, g.reshape(1, w), dy]
    if has_res:
        in_specs.append(pl.BlockSpec((tr, w), lambda i: (i, 0)))
        args.append(dres)
    dx, dg = pl.pallas_call(
        body,
        out_shape=(_sds((t, w), out_dtype), _sds((1, w), F32)),
        grid=(t // tr,),
        in_specs=in_specs,
        out_specs=(pl.BlockSpec((tr, w), lambda i: (i, 0)), pl.BlockSpec((1, w), lambda i: (0, 0))),
        compiler_params=_cp(("arbitrary",)),
        name=name,
    )(*args)
    return dx, dg.reshape(w)


def resadd_fwd(h, m, g, name="resadd"):
    t, w = h.shape
    tr = _rt(t)

    def body(h_ref, m_ref, g_ref, o_ref):
        mv = m_ref[...]
        r = lax.rsqrt(jnp.mean(mv * mv, axis=-1, keepdims=True) + EPS)
        y = mv * r * g_ref[...]
        o_ref[...] = h_ref[...] + jnp.where(_row_mask(pl.program_id(0), tr, y.shape), y, 0.0)

    return pl.pallas_call(
        body,
        out_shape=_sds((t, w), F32),
        grid=(t // tr,),
        in_specs=[pl.BlockSpec((tr, w), lambda i: (i, 0)), pl.BlockSpec((tr, w), lambda i: (i, 0)),
                  pl.BlockSpec((1, w), lambda i: (0, 0))],
        out_specs=pl.BlockSpec((tr, w), lambda i: (i, 0)),
        compiler_params=_cp(("parallel",)),
        name=name,
    )(h, m, g.reshape(1, w))


def loss_fwd_bwd(h, target):
    t, w = h.shape
    tr = _rt(t)

    def body(h_ref, t_ref, s_ref, dh_ref):
        i = pl.program_id(0)
        err = h_ref[...] - t_ref[...]
        err = jnp.where(_row_mask(i, tr, err.shape, PAD + N_META), err, 0.0)
        dh_ref[...] = err * (1.0 / w)

        @pl.when(i == 0)
        def _():
            s_ref[...] = jnp.zeros_like(s_ref)

        s_ref[...] += jnp.sum(err * err).reshape(1, 1)

    s, dh = pl.pallas_call(
        body,
        out_shape=(_sds((1, LANE), F32), _sds((t, w), F32)),
        grid=(t // tr,),
        in_specs=[pl.BlockSpec((tr, w), lambda i: (i, 0)), pl.BlockSpec((tr, w), lambda i: (i, 0))],
        out_specs=(pl.BlockSpec((1, LANE), lambda i: (0, 0)), pl.BlockSpec((tr, w), lambda i: (i, 0))),
        compiler_params=_cp(("arbitrary",)),
        name="loss",
    )(h, target)
    return 0.5 * s[0, 0] / w, dh


def _shift_down(ext, k, n):
    return pltpu.roll(ext, k, 0)[8:]


def _conv_pre(ext, x, w_ref, n):
    return (w_ref[4:5, :] + w_ref[3:4, :] * x + w_ref[2:3, :] * _shift_down(ext, 1, n)
            + w_ref[1:2, :] * _shift_down(ext, 2, n) + w_ref[0:1, :] * _shift_down(ext, 3, n))


def _conv_bwd_parts(dpre, dnext, x, ext, w_ref, n):
    extd = jnp.concatenate([dpre, dnext], axis=0)
    ln = n + 8
    dx = (w_ref[3:4, :] * dpre + w_ref[2:3, :] * pltpu.roll(extd, ln - 1, 0)[:n]
          + w_ref[1:2, :] * pltpu.roll(extd, ln - 2, 0)[:n] + w_ref[0:1, :] * pltpu.roll(extd, ln - 3, 0)[:n])
    sums = [jnp.sum(dpre * _shift_down(ext, 3, n), axis=0, keepdims=True),
            jnp.sum(dpre * _shift_down(ext, 2, n), axis=0, keepdims=True),
            jnp.sum(dpre * _shift_down(ext, 1, n), axis=0, keepdims=True),
            jnp.sum(dpre * x, axis=0, keepdims=True),
            jnp.sum(dpre, axis=0, keepdims=True)]
    return dx, sums


def _rows_block(sums):
    w = sums[0].shape[1]
    row = lax.broadcasted_iota(jnp.int32, (8, w), 0)
    out = jnp.zeros((8, w), F32)
    for k, s in enumerate(sums):
        out = jnp.where(row == k, s, out)
    return out


CONV_BLOCK = 512


def conv_silu_fwd(x, col0, c, wb, name="conv_fwd"):
    t = x.shape[0]
    cw = _pick(c, (CONV_BLOCK, LANE))
    nblk, col0_blk = c // cw, col0 // cw
    assert col0 % cw == 0
    tr = _rt(t)

    def body(x_ref, w_ref, o_ref, prev):
        ti = pl.program_id(1)

        @pl.when(ti == 0)
        def _():
            prev[...] = jnp.zeros_like(prev)

        xv = x_ref[...]
        ext = jnp.concatenate([prev[...], xv], axis=0)
        o_ref[...] = _silu(_conv_pre(ext, xv, w_ref, tr))
        prev[...] = xv[tr - 8:, :]

    return pl.pallas_call(
        body,
        out_shape=_sds((t, c), F32),
        grid=(nblk, t // tr),
        in_specs=[pl.BlockSpec((tr, cw), lambda cb, ti: (ti, col0_blk + cb)),
                  pl.BlockSpec((8, cw), lambda cb, ti: (0, cb))],
        out_specs=pl.BlockSpec((tr, cw), lambda cb, ti: (ti, cb)),
        scratch_shapes=[pltpu.VMEM((8, cw), F32)],
        compiler_params=_cp(("parallel", "arbitrary")),
        name=name,
    )(x, wb)


def conv_silu_bwd(x, col0, c, wb, dout, name="conv_bwd"):
    t = x.shape[0]
    cw = _pick(c, (CONV_BLOCK, LANE))
    nblk, col0_blk = c // cw, col0 // cw
    assert col0 % cw == 0
    tr = _rt(t)
    nt = t // tr
    r8 = tr // 8

    def body(x_ref, xp_ref, w_ref, do_ref, dx_ref, dwb_ref, dnext):
        ti = pl.program_id(1)
        tt = nt - 1 - ti

        @pl.when(ti == 0)
        def _():
            dnext[...] = jnp.zeros_like(dnext)
            dwb_ref[...] = jnp.zeros_like(dwb_ref)

        xv = x_ref[...]
        halo = jnp.where(tt > 0, xp_ref[...], 0.0)
        ext = jnp.concatenate([halo, xv], axis=0)
        pre = _conv_pre(ext, xv, w_ref, tr)
        s = jax.nn.sigmoid(pre)
        dpre = do_ref[...] * (s + pre * s * (1.0 - s))
        dx, sums = _conv_bwd_parts(dpre, dnext[...], xv, ext, w_ref, tr)
        dx_ref[...] = dx.astype(dx_ref.dtype)
        dwb_ref[...] += _rows_block(sums)
        dnext[...] = dpre[:8, :]

    return pl.pallas_call(
        body,
        out_shape=(_sds((t, c), BF), _sds((8, c), F32)),
        grid=(nblk, nt),
        in_specs=[pl.BlockSpec((tr, cw), lambda cb, ti: (nt - 1 - ti, col0_blk + cb)),
                  pl.BlockSpec((8, cw), lambda cb, ti: (jnp.maximum((nt - 1 - ti) * r8 - 1, 0), col0_blk + cb)),
                  pl.BlockSpec((8, cw), lambda cb, ti: (0, cb)),
                  pl.BlockSpec((tr, cw), lambda cb, ti: (nt - 1 - ti, cb))],
        out_specs=(pl.BlockSpec((tr, cw), lambda cb, ti: (nt - 1 - ti, cb)),
                   pl.BlockSpec((8, cw), lambda cb, ti: (0, cb))),
        scratch_shapes=[pltpu.VMEM((8, cw), F32)],
        compiler_params=_cp(("parallel", "arbitrary")),
        name=name,
    )(x, x, wb, dout)


def gated_norm_fwd(y, proj, g, name="gnorm_fwd"):
    t, w = y.shape
    tr = _rt(t)

    def body(y_ref, z_ref, g_ref, o_ref):
        v = y_ref[...] * _silu(z_ref[...])
        r = lax.rsqrt(jnp.mean(v * v, axis=-1, keepdims=True) + EPS)
        o_ref[...] = (v * r * g_ref[...]).astype(o_ref.dtype)

    return pl.pallas_call(
        body,
        out_shape=_sds((t, w), BF),
        grid=(t // tr,),
        in_specs=[pl.BlockSpec((tr, w), lambda i: (i, 0)), pl.BlockSpec((tr, w), lambda i: (i, OFF_Z // w)),
                  pl.BlockSpec((1, w), lambda i: (0, 0))],
        out_specs=pl.BlockSpec((tr, w), lambda i: (i, 0)),
        compiler_params=_cp(("parallel",)),
        name=name,
    )(y, proj, g.reshape(1, w))


def gated_norm_bwd(y, proj, g, dyab, name="gnorm_bwd"):
    t, w = y.shape
    tr = _rt(t)

    def body(y_ref, z_ref, g_ref, do_ref, dy_ref, dz_ref, dg_ref):
        i = pl.program_id(0)
        yv, zv, dov = y_ref[...], z_ref[...], do_ref[...]
        s = jax.nn.sigmoid(zv)
        sz = zv * s
        v = yv * sz
        r = lax.rsqrt(jnp.mean(v * v, axis=-1, keepdims=True) + EPS)
        vh = v * r
        dvg = dov * g_ref[...]
        dv = r * (dvg - vh * jnp.mean(dvg * vh, axis=-1, keepdims=True))
        dy_ref[...] = dv * sz
        dz_ref[...] = (dv * yv * (s + sz * (1.0 - s))).astype(dz_ref.dtype)

        @pl.when(i == 0)
        def _():
            dg_ref[...] = jnp.zeros_like(dg_ref)

        dg_ref[...] += jnp.sum(dov * vh, axis=0, keepdims=True)

    dy, dz, dg = pl.pallas_call(
        body,
        out_shape=(_sds((t, w), F32), _sds((t, w), BF), _sds((1, w), F32)),
        grid=(t // tr,),
        in_specs=[pl.BlockSpec((tr, w), lambda i: (i, 0)), pl.BlockSpec((tr, w), lambda i: (i, OFF_Z // w)),
                  pl.BlockSpec((1, w), lambda i: (0, 0)), pl.BlockSpec((tr, w), lambda i: (i, 0))],
        out_specs=(pl.BlockSpec((tr, w), lambda i: (i, 0)), pl.BlockSpec((tr, w), lambda i: (i, 0)),
                   pl.BlockSpec((1, w), lambda i: (0, 0))),
        compiler_params=_cp(("arbitrary",)),
        name=name,
    )(y, proj, g.reshape(1, w), dyab)
    return dy, dz, dg.reshape(w)


def rope_tables(t):
    inv = ROPE_BASE ** (-jnp.arange(0, MLA_ROPE, 2, dtype=F32) / MLA_ROPE)
    pos = (jnp.arange(t, dtype=F32) - PAD)[:, None]
    ang = pos * inv[None, :]
    cos, sin = jnp.cos(ang), jnp.sin(ang)
    z16 = jnp.zeros((t, 16), F32)
    z32 = jnp.zeros((t, 32), F32)
    c = jnp.concatenate([jnp.ones((t, 64), F32), cos, cos, z32], axis=1)
    s1 = jnp.concatenate([jnp.zeros((t, 64), F32), z16, sin, z32], axis=1)
    s2 = jnp.concatenate([jnp.zeros((t, 64), F32), -sin, z16, z32], axis=1)
    return c, s1, s2


def _rope(x, c, s1, s2):
    return x * c + pltpu.roll(x, 16, 1) * s1 + pltpu.roll(x, LANE - 16, 1) * s2


def _rope_t(d, c, s1, s2):
    return d * c + pltpu.roll(d * s1, LANE - 16, 1) + pltpu.roll(d * s2, 16, 1)


def rope_fwd(q_raw, kv_raw, proj, tabs):
    t = q_raw.shape[0]
    tr = _rt(t)
    hw = MLA_HEADS * LANE

    def body(q_ref, k_ref, v_ref, kr_ref, c_ref, s1_ref, s2_ref, qo_ref, ko_ref, vo_ref):
        c, s1, s2 = c_ref[...], s1_ref[...], s2_ref[...]
        kr = _rope(kr_ref[...], c, s1, s2)
        for h in range(MLA_HEADS):
            sl = slice(h * LANE, (h + 1) * LANE)
            qo_ref[:, sl] = (_rope(q_ref[:, sl], c, s1, s2) * Q_PRESCALE).astype(BF)
            ko_ref[:, sl] = (k_ref[:, sl] + kr).astype(BF)
        vo_ref[...] = v_ref[...].astype(BF)

    tab_spec = pl.BlockSpec((tr, LANE), lambda i: (i, 0))
    return pl.pallas_call(
        body,
        out_shape=(_sds((t, hw), BF), _sds((t, hw), BF), _sds((t, 1024), BF)),
        grid=(t // tr,),
        in_specs=[pl.BlockSpec((tr, hw), lambda i: (i, 0)), pl.BlockSpec((tr, hw), lambda i: (i, 0)),
                  pl.BlockSpec((tr, 1024), lambda i: (i, 2)), pl.BlockSpec((tr, LANE), lambda i: (i, OFF_KR // LANE)),
                  tab_spec, tab_spec, tab_spec],
        out_specs=(pl.BlockSpec((tr, hw), lambda i: (i, 0)), pl.BlockSpec((tr, hw), lambda i: (i, 0)),
                   pl.BlockSpec((tr, 1024), lambda i: (i, 0))),
        compiler_params=_cp(("parallel",)),
        name="rope_fwd",
    )(q_raw, kv_raw, kv_raw, proj, *tabs)


def rope_bwd(dq_cat, dk_cat, tabs):
    t = dq_cat.shape[0]
    tr = _rt(t)
    hw = MLA_HEADS * LANE

    def body(dq_ref, dk_ref, c_ref, s1_ref, s2_ref, dqo_ref, dkr_ref):
        c, s1, s2 = c_ref[...], s1_ref[...], s2_ref[...]
        acc = jnp.zeros((tr, LANE), F32)
        for h in range(MLA_HEADS):
            sl = slice(h * LANE, (h + 1) * LANE)
            dqo_ref[:, sl] = _rope_t(dq_ref[:, sl] * ATT_SCALE, c, s1, s2).astype(BF)
            acc = acc + dk_ref[:, sl]
        lane = lax.broadcasted_iota(jnp.int32, (tr, LANE), 1)
        dkr_ref[...] = jnp.where((lane >= 64) & (lane < 96), _rope_t(acc, c, s1, s2), 0.0)

    tab_spec = pl.BlockSpec((tr, LANE), lambda i: (i, 0))
    return pl.pallas_call(
        body,
        out_shape=(_sds((t, hw), BF), _sds((t, LANE), F32)),
        grid=(t // tr,),
        in_specs=[pl.BlockSpec((tr, hw), lambda i: (i, 0)), pl.BlockSpec((tr, hw), lambda i: (i, 0)),
                  tab_spec, tab_spec, tab_spec],
        out_specs=(pl.BlockSpec((tr, hw), lambda i: (i, 0)), pl.BlockSpec((tr, LANE), lambda i: (i, 0))),
        compiler_params=_cp(("parallel",)),
        name="rope_bwd",
    )(dq_cat, dk_cat, *tabs)


ATT_SCALE = (MLA_NOPE + MLA_ROPE) ** -0.5
LOG2E = math.log2(math.e)
Q_PRESCALE = ATT_SCALE * LOG2E
CARRY_MIDDLE_PAIR = 6
NT_DIMS = (((1,), (1,)), ((), ()))
TN_DIMS = (((0,), (0,)), ((), ()))


def _att_mask(qi, ki, tq, tk):
    qpos = qi * tq + lax.broadcasted_iota(jnp.int32, (tq, tk), 0)
    kpos = ki * tk + lax.broadcasted_iota(jnp.int32, (tq, tk), 1)
    return (kpos <= qpos) & (kpos >= PAD)


def _half_masks(n):
    lane = lax.broadcasted_iota(jnp.int32, (n, LANE), 1)
    return lane < 64, lane >= 64


def _att_tile(t):
    return _pick(t, (384, 256, 128))


def _ds(i, n):
    return pl.ds(i * n, n) if isinstance(i, int) else pl.ds(pl.multiple_of(i * n, n), n)


def attn_fwd(q_cat, k_cat, v, carry=None):
    t = q_cat.shape[0]
    tq = tk = _att_tile(t)
    nq = t // tq
    n_pair = MLA_HEADS // 2
    nx = carry.k if carry else 0

    def body(*refs):
        q_ref, k_ref, v_ref = refs[:3]
        o_ref, lse_ref = refs[3 + nx:5 + nx]
        qi = pl.program_id(1)
        if carry:
            start, middle, finish = carry.phases(refs[3:3 + nx], refs[5 + nx:5 + 2 * nx], refs[5 + 2 * nx:])
            pair = pl.program_id(0)
            pl.when((pair == 0) & (qi == 0))(start)
            pl.when((pair == CARRY_MIDDLE_PAIR) & (qi == 0))(middle)
        lo_q, _ = _half_masks(tq)
        halves = _half_masks(tk)

        def step(ki, state, masked):
            m_old, l_old, acc = state[0:2], state[2:4], state[4]
            rows = _ds(ki, tk)
            vv = v_ref[rows, :]
            ss = [lax.dot_general(q_ref[:, hh * LANE:(hh + 1) * LANE], k_ref[rows, hh * LANE:(hh + 1) * LANE], NT_DIMS,
                                  preferred_element_type=F32) for hh in range(2)]
            if masked:
                valid = _att_mask(qi, ki, tq, tk)
                ss = [jnp.where(valid, s, NEG) for s in ss]
            m_new = [jnp.maximum(m_old[hh], jnp.max(ss[hh], axis=-1, keepdims=True)) for hh in range(2)]
            ps = [jnp.exp2(ss[hh] - m_new[hh]) for hh in range(2)]
            alpha = [jnp.exp2(m_old[hh] - m_new[hh]) for hh in range(2)]
            l_new = [alpha[hh] * l_old[hh] + jnp.sum(ps[hh], axis=-1, keepdims=True) for hh in range(2)]
            pv = [jnp.dot(ps[hh].astype(BF), jnp.where(halves[hh], vv, jnp.zeros_like(vv)), preferred_element_type=F32)
                  for hh in range(2)]
            acc = acc * jnp.where(lo_q, alpha[0], alpha[1]) + pv[0] + pv[1]
            return m_new[0], m_new[1], l_new[0], l_new[1], acc

        neg, zero = jnp.full((tq, 1), NEG, F32), jnp.zeros((tq, 1), F32)
        state = step(0, (neg, neg, zero, zero, jnp.zeros((tq, LANE), F32)), True)
        state = lax.fori_loop(1, qi, lambda ki, st: step(ki, st, False), state)
        state = lax.cond(qi > 0, lambda st: step(qi, st, True), lambda st: st, state)
        m0, m1, l0, l1, acc = state
        l = jnp.where(lo_q, l0, l1)
        o_ref[...] = (acc / l).astype(o_ref.dtype)
        lse_ref[...] = jnp.where(lo_q, m0, m1) + jnp.log2(l)
        if carry:
            pl.when((pair == n_pair - 1) & (qi == nq - 1))(finish)

    outs = pl.pallas_call(
        body,
        out_shape=(_sds((t, 1024), BF), _sds((t, 1024), F32)) + tuple(carry.out_shapes() if carry else ()),
        grid=(n_pair, nq),
        in_specs=[pl.BlockSpec((tq, 2 * LANE), lambda p, qi: (qi, p)),
                  pl.BlockSpec((t, 2 * LANE), lambda p, qi: (0, p)),
                  pl.BlockSpec((t, LANE), lambda p, qi: (0, p))] + [ANY] * nx,
        out_specs=(pl.BlockSpec((tq, LANE), lambda p, qi: (qi, p)),
                   pl.BlockSpec((tq, LANE), lambda p, qi: (qi, p))) + (ANY,) * nx,
        scratch_shapes=carry.scratch() if carry else [],
        compiler_params=_cp(("arbitrary", "arbitrary") if carry else ("parallel", "parallel")),
        name="attn_fwd_carrying" if carry else "attn_fwd",
    )(q_cat, k_cat, v, *(carry.arrs if carry else ()))
    return outs[0], outs[1], list(outs[2:])


def attn_bwd(q_cat, k_cat, v, o, lse, dyab, carry=None):
    t = q_cat.shape[0]
    tq = tk = _att_tile(t)
    nq = t // tq
    n_pair = MLA_HEADS // 2
    nx = carry.k if carry else 0

    def body(*refs):
        q_ref, k_ref, v_ref, o_ref, lse_ref, do_ref = refs[:6]
        dq_ref, dk_ref, dv_ref = refs[6 + nx:9 + nx]
        ki = pl.program_id(1)
        if carry:
            start, middle, finish = carry.phases(refs[6:6 + nx], refs[9 + nx:9 + 2 * nx], refs[9 + 2 * nx:])
            pair = pl.program_id(0)
            pl.when((pair == 0) & (ki == 0))(start)
            pl.when((pair == CARRY_MIDDLE_PAIR) & (ki == 0))(middle)

        @pl.when(ki == 0)
        def _():
            dq_ref[...] = jnp.zeros_like(dq_ref)

        halves = _half_masks(tq)
        vv = v_ref[...]
        kk = [k_ref[:, hh * LANE:(hh + 1) * LANE] for hh in range(2)]

        def step(qi, acc, masked):
            rows = _ds(qi, tq)
            dov, ov, lse_v = do_ref[rows, :], o_ref[rows, :].astype(F32), lse_ref[rows, :]
            qh = [q_ref[rows, hh * LANE:(hh + 1) * LANE] for hh in range(2)]
            ss = [lax.dot_general(qh[hh], kk[hh], NT_DIMS, preferred_element_type=F32) for hh in range(2)]
            if masked:
                valid = _att_mask(qi, ki, tq, tk)
                ss = [jnp.where(valid, s, NEG) for s in ss]
            ps = [jnp.exp2(ss[hh] - lse_v[:, 64 * hh:64 * hh + 1]) for hh in range(2)]
            dom = [jnp.where(halves[hh], dov, 0.0) for hh in range(2)]
            delta = [jnp.sum(dom[hh] * ov, axis=-1, keepdims=True) for hh in range(2)]
            dom = [d.astype(BF) for d in dom]
            dp = [lax.dot_general(dom[hh], vv, NT_DIMS, preferred_element_type=F32) for hh in range(2)]
            ds = [(ps[hh] * (dp[hh] - delta[hh])).astype(BF) for hh in range(2)]
            pb = [p.astype(BF) for p in ps]
            dv = (acc[2] + lax.dot_general(pb[0], dom[0], TN_DIMS, preferred_element_type=F32)
                  + lax.dot_general(pb[1], dom[1], TN_DIMS, preferred_element_type=F32))
            dk = [acc[hh] + lax.dot_general(ds[hh], qh[hh], TN_DIMS, preferred_element_type=F32) for hh in range(2)]
            for hh in range(2):
                dq_ref[rows, hh * LANE:(hh + 1) * LANE] += jnp.dot(ds[hh], kk[hh], preferred_element_type=F32)
            return dk[0], dk[1], dv

        zero = jnp.zeros((tk, LANE), F32)
        acc = step(ki, (zero, zero, zero), True)
        acc = lax.fori_loop(ki + 1, jnp.where(ki == 0, nq, ki + 1), lambda qi, a: step(qi, a, True), acc)
        acc = lax.fori_loop(ki + 1, jnp.where(ki == 0, ki + 1, nq), lambda qi, a: step(qi, a, False), acc)
        dk_ref[:, 0:LANE] = acc[0] * (1.0 / LOG2E)
        dk_ref[:, LANE:2 * LANE] = acc[1] * (1.0 / LOG2E)
        dv_ref[...] = acc[2]
        if carry:
            pl.when((pair == n_pair - 1) & (ki == nq - 1))(finish)

    full = lambda w, off=0: pl.BlockSpec((t, w), lambda p, ki: (0, p + off))
    blk = lambda w: pl.BlockSpec((tk, w), lambda p, ki: (ki, p))
    outs = pl.pallas_call(
        body,
        out_shape=(_sds((t, 2048), F32), _sds((t, 2048), F32), _sds((t, 1024), F32))
        + tuple(carry.out_shapes() if carry else ()),
        grid=(n_pair, nq),
        in_specs=[full(2 * LANE), blk(2 * LANE), blk(LANE), full(LANE), full(LANE), full(LANE, 8)] + [ANY] * nx,
        out_specs=(full(2 * LANE), blk(2 * LANE), blk(LANE)) + (ANY,) * nx,
        scratch_shapes=carry.scratch() if carry else [],
        compiler_params=_cp(("arbitrary", "arbitrary") if carry else ("parallel", "arbitrary")),
        name="attn_bwd_carrying" if carry else "attn_bwd",
    )(q_cat, k_cat, v, o, lse, dyab, *(carry.arrs if carry else ()))
    return outs[0], outs[1], outs[2], list(outs[3:])


N_PAIR = SSD_HEADS // 2


def _hdot(a, b):
    return jnp.dot(a, b, precision=HI, preferred_element_type=F32)


def _ssd_chunk(xs, bg, cg, dtraw, hin, dt_bias, a_log, dskip, rowmask):
    ln = CHUNK
    causal = lax.broadcasted_iota(jnp.int32, (ln, ln), 0) >= lax.broadcasted_iota(jnp.int32, (ln, ln), 1)
    ltri = causal.astype(F32)
    lane = lax.broadcasted_iota(jnp.int32, (ln, LANE), 1)
    halves = (lane < 64, lane >= 64)
    low_row = lax.broadcasted_iota(jnp.int32, (1, LANE), 1) < 64
    head_lane = lax.broadcasted_iota(jnp.int32, (1, SSD_HEADS), 1)
    head_row = lax.broadcasted_iota(jnp.int32, (SSD_HEADS, 1), 0)

    def col(a, h):
        return jnp.sum(jnp.where(head_lane == h, a, 0.0), axis=1, keepdims=True)

    dt = _softplus(dtraw + dt_bias) * rowmask
    da = dt * (-jnp.exp(a_log))
    acs = _hdot(ltri, da)
    acs_t = lax.dot_general(da, ltri, (((0,), (1,)), ((), ())), precision=HI, preferred_element_type=F32)
    tot = jnp.sum(da, axis=0, keepdims=True)
    bm = [b * rowmask for b in bg]
    cm = [c * rowmask for c in cg]
    cb = [lax.dot_general(cm[g].astype(BF), bm[g].astype(BF), NT_DIMS, preferred_element_type=F32) for g in range(2)]
    ys, hout = [], []
    for p in range(N_PAIR):
        g = p // (N_PAIR // 2)
        h0, h1 = 2 * p, 2 * p + 1
        xdt = xs[p] * jnp.where(halves[0], col(dt, h0), col(dt, h1))
        y = jnp.zeros((ln, LANE), F32)
        snew = jnp.zeros((ln, LANE), F32)
        for hh in range(2):
            a_col = col(acs, h0 + hh)
            a_row = jnp.sum(jnp.where(head_row == h0 + hh, acs_t, 0.0), axis=0, keepdims=True)
            dec = jnp.exp(jnp.where(causal, a_col - a_row, NEG))
            xm = jnp.where(halves[hh], xdt, 0.0).astype(BF)
            y = y + jnp.dot((cb[g] * dec).astype(BF), xm, preferred_element_type=F32)
            bd = bm[g] * jnp.exp(col(tot, h0 + hh) - a_col)
            snew = snew + lax.dot_general(bd.astype(BF), xm, TN_DIMS, preferred_element_type=F32)
        y_off = (jnp.dot(cm[g].astype(BF), hin[p].astype(BF), preferred_element_type=F32)
                 * jnp.exp(jnp.where(halves[0], col(acs, h0), col(acs, h1))))
        ys.append(y + y_off + jnp.where(low_row, col(dskip, h0), col(dskip, h1)) * xs[p])
        hout.append(jnp.exp(jnp.where(low_row, col(tot, h0), col(tot, h1))) * hin[p] + snew)
    return ys, hout


def _ssd_load(x_ref, dt_ref):
    xs = [x_ref[:, p * LANE:(p + 1) * LANE] for p in range(N_PAIR)]
    bg = [x_ref[:, SSD_D_INNER + g * LANE:SSD_D_INNER + (g + 1) * LANE] for g in range(2)]
    cg = [x_ref[:, SSD_D_INNER + (2 + g) * LANE:SSD_D_INNER + (3 + g) * LANE] for g in range(2)]
    return xs, bg, cg, dt_ref[:, 0:SSD_HEADS]


def _chunk_rowmask(c):
    return ((c * CHUNK + lax.broadcasted_iota(jnp.int32, (CHUNK, 1), 0)) >= PAD).astype(F32)


def ssd_fwd(xbc_c, proj, dt_bias, a_log, dskip):
    t = xbc_c.shape[0]
    nc = t // CHUNK

    def body(x_ref, dt_ref, dtb_ref, al_ref, d_ref, y_ref, hs_ref, h_s):
        c = pl.program_id(0)

        @pl.when(c == 0)
        def _():
            h_s[...] = jnp.zeros_like(h_s)

        xs, bg, cg, dtraw = _ssd_load(x_ref, dt_ref)
        hin = [h_s[p] for p in range(N_PAIR)]
        hs_ref[0] = h_s[...]
        ys, hout = _ssd_chunk(xs, bg, cg, dtraw, hin, dtb_ref[...], al_ref[...], d_ref[...], _chunk_rowmask(c))
        for p in range(N_PAIR):
            y_ref[:, p * LANE:(p + 1) * LANE] = ys[p]
            h_s[p] = hout[p]

    par = pl.BlockSpec((1, SSD_HEADS), lambda c: (0, 0))
    return pl.pallas_call(
        body,
        out_shape=(_sds((t, SSD_D_INNER), F32), _sds((nc, N_PAIR, CHUNK, LANE), F32)),
        grid=(nc,),
        in_specs=[pl.BlockSpec((CHUNK, SSD_CONV_CH), lambda c: (c, 0)),
                  pl.BlockSpec((CHUNK, LANE), lambda c: (c, OFF_DT // LANE)), par, par, par],
        out_specs=(pl.BlockSpec((CHUNK, SSD_D_INNER), lambda c: (c, 0)),
                   pl.BlockSpec((1, N_PAIR, CHUNK, LANE), lambda c: (c, 0, 0, 0))),
        scratch_shapes=[pltpu.VMEM((N_PAIR, CHUNK, LANE), F32)],
        compiler_params=_cp(("arbitrary",)),
        name="ssd_fwd",
    )(xbc_c, proj, dt_bias.reshape(1, -1), a_log.reshape(1, -1), dskip.reshape(1, -1))


def ssd_bwd(xbc_c, proj, dt_bias, a_log, dskip, hs, dy):
    t = xbc_c.shape[0]
    nc = t // CHUNK

    def body(x_ref, dt_ref, dtb_ref, al_ref, d_ref, hs_ref, dy_ref, dx_ref, ddt_ref, dpar_ref, dh_s):
        ci = pl.program_id(0)
        c = nc - 1 - ci

        @pl.when(ci == 0)
        def _():
            dh_s[...] = jnp.zeros_like(dh_s)
            dpar_ref[...] = jnp.zeros_like(dpar_ref)

        xs, bg, cg, dtraw = _ssd_load(x_ref, dt_ref)
        hin = [hs_ref[0, p] for p in range(N_PAIR)]
        rowmask = _chunk_rowmask(c)
        fn = lambda xs_, bg_, cg_, dtraw_, hin_, dtb_, al_, d_: _ssd_chunk(xs_, bg_, cg_, dtraw_, hin_, dtb_, al_, d_, rowmask)
        _, vjp = jax.vjp(fn, xs, bg, cg, dtraw, hin, dtb_ref[...], al_ref[...], d_ref[...])
        dys = [dy_ref[:, p * LANE:(p + 1) * LANE] for p in range(N_PAIR)]
        dhs = [dh_s[p] for p in range(N_PAIR)]
        dxs, dbg, dcg, ddtraw, dhin, ddtb, dal, dd = vjp((dys, dhs))
        for p in range(N_PAIR):
            dx_ref[:, p * LANE:(p + 1) * LANE] = dxs[p]
            dh_s[p] = dhin[p]
        for g in range(2):
            dx_ref[:, SSD_D_INNER + g * LANE:SSD_D_INNER + (g + 1) * LANE] = dbg[g]
            dx_ref[:, SSD_D_INNER + (2 + g) * LANE:SSD_D_INNER + (3 + g) * LANE] = dcg[g]
        ddt_ref[...] = jnp.zeros_like(ddt_ref)
        ddt_ref[:, 0:SSD_HEADS] = ddtraw
        dpar_ref[0:1, 0:SSD_HEADS] += ddtb
        dpar_ref[1:2, 0:SSD_HEADS] += dal
        dpar_ref[2:3, 0:SSD_HEADS] += dd

    par = pl.BlockSpec((1, SSD_HEADS), lambda ci: (0, 0))
    return pl.pallas_call(
        body,
        out_shape=(_sds((t, SSD_CONV_CH), F32), _sds((t, LANE), F32), _sds((8, LANE), F32)),
        grid=(nc,),
        in_specs=[pl.BlockSpec((CHUNK, SSD_CONV_CH), lambda ci: (nc - 1 - ci, 0)),
                  pl.BlockSpec((CHUNK, LANE), lambda ci: (nc - 1 - ci, OFF_DT // LANE)), par, par, par,
                  pl.BlockSpec((1, N_PAIR, CHUNK, LANE), lambda ci: (nc - 1 - ci, 0, 0, 0)),
                  pl.BlockSpec((CHUNK, SSD_D_INNER), lambda ci: (nc - 1 - ci, 0))],
        out_specs=(pl.BlockSpec((CHUNK, SSD_CONV_CH), lambda ci: (nc - 1 - ci, 0)),
                   pl.BlockSpec((CHUNK, LANE), lambda ci: (nc - 1 - ci, 0)),
                   pl.BlockSpec((8, LANE), lambda ci: (0, 0))),
        scratch_shapes=[pltpu.VMEM((N_PAIR, CHUNK, LANE), F32)],
        compiler_params=_cp(("arbitrary",)),
        name="ssd_bwd",
    )(xbc_c, proj, dt_bias.reshape(1, -1), a_log.reshape(1, -1), dskip.reshape(1, -1), hs, dy)


def _neg_expm1(y):
    series = -(y * (1.0 + y * (0.5 + y * (1.0 / 6.0 + y * (1.0 / 24.0 + y * (1.0 / 120.0))))))
    return jnp.where(y > -0.1, series, 1.0 - jnp.exp(y))


def _rg_pw(xr, wa, ba, wi, bi, lam, rowmask):
    xb = xr.astype(BF)
    r = jax.nn.sigmoid(jnp.dot(xb, wa.astype(BF), preferred_element_type=F32) + ba)
    i = jax.nn.sigmoid(jnp.dot(xb, wi.astype(BF), preferred_element_type=F32) + bi)
    log_a = -LRU_C * r * _softplus(-lam)
    a = jnp.exp(log_a)
    u = jnp.sqrt(_neg_expm1(2.0 * log_a)) * (i * xr) * rowmask
    return a, u


def _gelu_grad(x):
    c = math.sqrt(2.0 / math.pi)
    th = jnp.tanh(c * (x + 0.044715 * (x * x * x)))
    return 0.5 * (1.0 + th) + 0.5 * x * (1.0 - th * th) * c * (1.0 + 3.0 * 0.044715 * x * x)


def _scan_fwd(a, u):
    n = a.shape[0]
    row = lax.broadcasted_iota(jnp.int32, a.shape, 0)
    s = 1
    while s < n:
        a_s = jnp.where(row >= s, pltpu.roll(a, s, 0), 1.0)
        u_s = jnp.where(row >= s, pltpu.roll(u, s, 0), 0.0)
        u = u + a * u_s
        a = a * a_s
        s *= 2
    return a, u


def _scan_bwd(b, d):
    n = b.shape[0]
    row = lax.broadcasted_iota(jnp.int32, b.shape, 0)
    s = 1
    while s < n:
        b_s = jnp.where(row < n - s, pltpu.roll(b, n - s, 0), 1.0)
        d_s = jnp.where(row < n - s, pltpu.roll(d, n - s, 0), 0.0)
        d = d + b * d_s
        b = b * b_s
        s *= 2
    return d


def rg_fwd(xr_pre, gate_pre, rgp, w_a, w_i):
    t = xr_pre.shape[0]
    tr = _rt(t)

    def body(x_ref, g_ref, p_ref, wa_ref, wi_ref, hg_ref, hs_ref, prev, hcar):
        ti = pl.program_id(1)

        @pl.when(ti == 0)
        def _():
            prev[...] = jnp.zeros_like(prev)
            hcar[...] = jnp.zeros_like(hcar)

        xv = x_ref[...]
        ext = jnp.concatenate([prev[...], xv], axis=0)
        xr = _conv_pre(ext, xv, p_ref, tr)
        rowmask = _row_mask(ti, tr, (tr, 1)).astype(F32)
        a, u = _rg_pw(xr, wa_ref[0], p_ref[5:6, :], wi_ref[0], p_ref[6:7, :], p_ref[7:8, :], rowmask)
        a_cum, h_loc = _scan_fwd(a, u)
        hs = h_loc + a_cum * hcar[0:1, :]
        hs_ref[...] = hs
        hg_ref[...] = (hs * _gelu(g_ref[...])).astype(hg_ref.dtype)
        hcar[...] = jnp.broadcast_to(hs[tr - 1:tr, :], (8, LANE))
        prev[...] = xv[tr - 8:, :]

    return pl.pallas_call(
        body,
        out_shape=(_sds((t, LRU_WIDTH), BF), _sds((t, LRU_WIDTH), F32)),
        grid=(LRU_BLOCKS, t // tr),
        in_specs=[pl.BlockSpec((tr, LANE), lambda n, ti: (ti, n)),
                  pl.BlockSpec((tr, LANE), lambda n, ti: (ti, n)),
                  pl.BlockSpec((8, LANE), lambda n, ti: (0, n)),
                  pl.BlockSpec((1, LANE, LANE), lambda n, ti: (n, 0, 0)),
                  pl.BlockSpec((1, LANE, LANE), lambda n, ti: (n, 0, 0))],
        out_specs=(pl.BlockSpec((tr, LANE), lambda n, ti: (ti, n)), pl.BlockSpec((tr, LANE), lambda n, ti: (ti, n))),
        scratch_shapes=[pltpu.VMEM((8, LANE), F32), pltpu.VMEM((8, LANE), F32)],
        compiler_params=_cp(("parallel", "arbitrary")),
        name="rg_fwd",
    )(xr_pre, gate_pre, rgp, w_a, w_i)


def rg_bwd(xr_pre, gate_pre, rgp, w_a, w_i, hs, dhg):
    t = xr_pre.shape[0]
    tr = _rt(t)
    nt = t // tr
    r8 = tr // 8

    def body(x_ref, xp_ref, g_ref, p_ref, wa_ref, wi_ref, hs_ref, hp_ref, dhg_ref,
             dx_ref, dg_ref, dp_ref, dwa_ref, dwi_ref, gcar, dnext):
        ti = pl.program_id(1)
        tt = nt - 1 - ti

        @pl.when(ti == 0)
        def _():
            gcar[...] = jnp.zeros_like(gcar)
            dnext[...] = jnp.zeros_like(dnext)
            dp_ref[...] = jnp.zeros_like(dp_ref)
            dwa_ref[...] = jnp.zeros_like(dwa_ref)
            dwi_ref[...] = jnp.zeros_like(dwi_ref)

        xv = x_ref[...]
        halo = jnp.where(tt > 0, xp_ref[...], 0.0)
        ext = jnp.concatenate([halo, xv], axis=0)
        xr = _conv_pre(ext, xv, p_ref, tr)
        rowmask = _row_mask(tt, tr, (tr, 1)).astype(F32)
        fn = lambda xr_, wa_, ba_, wi_, bi_, lam_: _rg_pw(xr_, wa_, ba_, wi_, bi_, lam_, rowmask)
        (a, _), vjp = jax.vjp(fn, xr, wa_ref[0], p_ref[5:6, :], wi_ref[0], p_ref[6:7, :], p_ref[7:8, :])
        gpre = g_ref[...]
        hsv = hs_ref[...]
        dhg_v = dhg_ref[...]
        dg_ref[...] = (dhg_v * hsv * _gelu_grad(gpre)).astype(dg_ref.dtype)
        row = lax.broadcasted_iota(jnp.int32, (tr, LANE), 0)
        d = dhg_v * _gelu(gpre) + jnp.where(row == tr - 1, gcar[0:1, :], 0.0)
        b = jnp.where(row < tr - 1, pltpu.roll(a, tr - 1, 0), 0.0)
        g = _scan_bwd(b, d)
        gcar[...] = jnp.broadcast_to(a[0:1, :] * g[0:1, :], (8, LANE))
        hlast = jnp.where(tt > 0, hp_ref[7:8, :], 0.0)
        hprev = jnp.where(row == 0, hlast, pltpu.roll(hsv, 1, 0))
        dxr, dwa, dba, dwi, dbi, dlam = vjp((g * hprev, g))
        dx, sums = _conv_bwd_parts(dxr, dnext[...], xv, ext, p_ref, tr)
        dx_ref[...] = dx.astype(dx_ref.dtype)
        dnext[...] = dxr[:8, :]
        dp_ref[...] += _rows_block(sums + [dba, dbi, dlam])
        dwa_ref[0] += dwa
        dwi_ref[0] += dwi

    tile = lambda off=0: pl.BlockSpec((tr, LANE), lambda n, ti: (nt - 1 - ti, off + n))
    halo = lambda off=0: pl.BlockSpec((8, LANE), lambda n, ti: (jnp.maximum((nt - 1 - ti) * r8 - 1, 0), off + n))
    par = pl.BlockSpec((8, LANE), lambda n, ti: (0, n))
    wspec = pl.BlockSpec((1, LANE, LANE), lambda n, ti: (n, 0, 0))
    return pl.pallas_call(
        body,
        out_shape=(_sds((t, LRU_WIDTH), BF), _sds((t, LRU_WIDTH), BF), _sds((8, LRU_WIDTH), F32),
                   _sds((LRU_BLOCKS, LANE, LANE), F32), _sds((LRU_BLOCKS, LANE, LANE), F32)),
        grid=(LRU_BLOCKS, nt),
        in_specs=[tile(), halo(), tile(), par, wspec, wspec, tile(), halo(), tile()],
        out_specs=(tile(), tile(), par, wspec, wspec),
        scratch_shapes=[pltpu.VMEM((8, LANE), F32), pltpu.VMEM((8, LANE), F32)],
        compiler_params=_cp(("parallel", "arbitrary")),
        name="rg_bwd",
    )(xr_pre, xr_pre, gate_pre, rgp, w_a, w_i, hs, hs, dhg)


PACK_W = 1024
MESH_ID = pl.DeviceIdType.MESH
ANY = pl.BlockSpec(memory_space=pl.ANY)


def _my_place():
    x, y, c = lax.axis_index("x"), lax.axis_index("y"), lax.axis_index("c")
    return x, y, c


def _lin(px, py, pc):
    return 4 * px + 2 * py + pc


class Exchange:
    def __init__(self, kind, arrs):
        self.kind, self.arrs, self.k = kind, list(arrs), len(arrs)

    def out_shapes(self):
        if self.kind == "gather":
            return [_sds((N_DEV,) + a.shape, a.dtype) for a in self.arrs]
        return [_sds(a.shape, a.dtype) for a in self.arrs]

    def scratch(self):
        k = self.k
        return [pltpu.SemaphoreType.DMA((k, 7)), pltpu.SemaphoreType.DMA((k, 7)), pltpu.SemaphoreType.DMA((k,))]

    def phases(self, ins, outs, sems):
        return (self._gather if self.kind == "gather" else self._scatter)(ins, outs, *sems)

    def _gather(self, ins, outs, send_sems, recv_sems, local_sems):
        k = self.k
        x, y, c = _my_place()
        me, sibling = (x, y, c), (x, y, 1 - c)
        chips = [(1 - x, y), (x, 1 - y), (1 - x, 1 - y)]

        def copy(a, sem, block, to, from_input=False):
            slab = outs[a].at[_lin(*block)]
            return pltpu.make_async_remote_copy(
                src_ref=ins[a] if from_input else slab, dst_ref=slab,
                send_sem=send_sems.at[a, sem], recv_sem=recv_sems.at[a, sem],
                device_id=to, device_id_type=MESH_ID)

        def mine():
            return [pltpu.make_async_copy(ins[a], outs[a].at[_lin(*me)], local_sems.at[a]) for a in range(k)]

        def first():
            out = []
            for a in range(k):
                out.append(copy(a, 0, me, sibling, True))
                out += [copy(a, 1 + j, me, (*chip, c), True) for j, chip in enumerate(chips)]
            return out

        def passed():
            return [copy(a, 4 + j, (*chip, c), sibling) for j, chip in enumerate(chips) for a in range(k)]

        def start():
            for cp in mine() + first():
                cp.start()

        def middle():
            onward = passed()
            for j, chip in enumerate(chips):
                for a in range(k):
                    copy(a, 1 + j, (*chip, c), me).wait_recv()
                    onward[j * k + a].start()

        def finish():
            for a in range(k):
                copy(a, 0, sibling, me).wait_recv()
                for j, chip in enumerate(chips):
                    copy(a, 4 + j, (*chip, 1 - c), me).wait_recv()
            for cp in first() + passed():
                cp.wait_send()
            for cp in mine():
                cp.wait()

        return start, middle, finish

    def _scatter(self, ins, outs, send_sems, recv_sems, local_sems):
        k = self.k
        x, y, c = _my_place()
        me = _lin(x, y, c)
        peers = [((1 - x) if r & 4 else x, (1 - y) if r & 2 else y, (1 - c) if r & 1 else c) for r in range(1, N_DEV)]

        def copy(a, r, src_slab, dst_slab, to):
            return pltpu.make_async_remote_copy(
                src_ref=ins[a].at[src_slab], dst_ref=outs[a].at[dst_slab],
                send_sem=send_sems.at[a, r], recv_sem=recv_sems.at[a, r],
                device_id=to, device_id_type=MESH_ID)

        def mine():
            return [pltpu.make_async_copy(ins[a].at[me], outs[a].at[me], local_sems.at[a]) for a in range(k)]

        def sends():
            return [copy(a, r, _lin(*peer), me, peer) for r, peer in enumerate(peers) for a in range(k)]

        def start():
            for cp in mine() + sends():
                cp.start()

        def middle():
            pass

        def finish():
            for r, peer in enumerate(peers):
                for a in range(k):
                    copy(a, r, me, _lin(*peer), peer).wait_recv()
            for cp in sends():
                cp.wait_send()
            for cp in mine():
                cp.wait()

        return start, middle, finish

    def run(self, name):
        k = self.k

        def body(*refs):
            start, middle, finish = self.phases(refs[:k], refs[k:2 * k], refs[2 * k:])
            start()
            middle()
            finish()

        return pl.pallas_call(
            body,
            out_shape=tuple(self.out_shapes()),
            in_specs=[ANY] * k,
            out_specs=tuple(ANY for _ in range(k)),
            scratch_shapes=self.scratch(),
            name=name,
        )(*self.arrs)


def all_gather(arrs, name):
    return Exchange("gather", arrs).run(name)


def all_to_all(arrs, name):
    return Exchange("scatter", arrs).run(name)


def slab_sum(a, name):
    _, r, w = a.shape
    tr = _pick(r, (256, 128, 64, 32, 16, 8))

    def body(a_ref, o_ref):
        acc = a_ref[0].astype(F32)
        for d in range(1, N_DEV):
            acc = acc + a_ref[d].astype(F32)
        o_ref[...] = acc

    return pl.pallas_call(
        body,
        out_shape=_sds((r, w), F32),
        grid=(r // tr,),
        in_specs=[pl.BlockSpec((N_DEV, tr, w), lambda i: (0, i, 0))],
        out_specs=pl.BlockSpec((tr, w), lambda i: (i, 0)),
        compiler_params=_cp(("parallel",)),
        name=name,
    )(a)


def _adam_update(w, g, m, v):
    nm = ADAM_B1 * m + (1.0 - ADAM_B1) * g
    nv = ADAM_B2 * v + (1.0 - ADAM_B2) * (g * g)
    m_hat = nm / (1.0 - ADAM_B1 ** ADAM_STEP)
    v_hat = nv / (1.0 - ADAM_B2 ** ADAM_STEP)
    return -ADAM_LR * (m_hat / (jnp.sqrt(v_hat) + ADAM_EPS) + ADAM_WD * w), nm, nv


def adamw_blocks(w, m, v, parts, name):
    nl, r, c = w.shape
    tr = next(t for t in (256, 160, 128, 64, 32, 16) if r % t == 0 and N_DEV * t * c * 2 <= 2 * 1024 * 1024)

    def body(w_ref, m_ref, v_ref, *rest):
        part_refs, (g_ref, d_ref, nm_ref, nv_ref) = rest[:nl], rest[nl:]
        layer = pl.program_id(0)
        for idx in range(nl):
            @pl.when(layer == idx)
            def _(idx=idx):
                g = part_refs[idx][0].astype(F32)
                for dev in range(1, N_DEV):
                    g = g + part_refs[idx][dev].astype(F32)
                g_ref[...] = g
                d_ref[...], nm_ref[...], nv_ref[...] = _adam_update(w_ref[...], g, m_ref[...], v_ref[...])

    spec = pl.BlockSpec((None, tr, c), lambda l, i: (l, i, 0))
    part_spec = lambda idx: pl.BlockSpec((N_DEV, tr, c), lambda l, i: (0, jnp.where(l == idx, i, 0), 0))
    return pl.pallas_call(
        body,
        out_shape=tuple(_sds((nl, r, c), F32) for _ in range(4)),
        grid=(nl, r // tr),
        in_specs=[spec] * 3 + [part_spec(idx) for idx in range(nl)],
        out_specs=(spec,) * 4,
        compiler_params=_cp(("arbitrary", "arbitrary")),
        name=name,
    )(w, m, v, *parts)


def adamw(w, g, m, v, name):
    r, c = w.shape
    tr = _pick(r, (256, 160, 128, 64, 32, 16, 8))

    def body(w_ref, g_ref, m_ref, v_ref, d_ref, nm_ref, nv_ref):
        d_ref[...], nm_ref[...], nv_ref[...] = _adam_update(w_ref[...], g_ref[...], m_ref[...], v_ref[...])

    spec = pl.BlockSpec((tr, c), lambda i: (i, 0))
    return pl.pallas_call(
        body,
        out_shape=tuple(_sds((r, c), F32) for _ in range(3)),
        grid=(r // tr,),
        in_specs=[spec] * 4,
        out_specs=(spec, spec, spec),
        compiler_params=_cp(("parallel",)),
        name=name,
    )(w, g, m, v)


def _relu2_epi(acc):
    r = jnp.maximum(acc, 0.0)
    return r * r, r


def _drelu2_epi(acc, r):
    return (acc * (2.0 * r.astype(F32)),)


def mlp_fwd(h, g_pre, g_post, w_up, w_down):
    hn = norm_fwd(h, g_pre, BF, name="mlp_norm")
    u, r = matmul(hn, w_up, "nn", (BF, BF), epi=_relu2_epi, name="mlp_up")
    d = matmul(u, w_down, "nn", name="mlp_down")
    return resadd_fwd(h, d, g_post, name="mlp_res"), (h, hn, u, r, d)


def mlp_bwd(res, dh2, g_pre, g_post, w_up, w_down):
    h, hn, u, r, d = res
    dd, dg_post = norm_bwd(d, g_post, dh2, mask_pad=True, out_dtype=BF, name="mlp_post_bwd")
    dw_down = matmul(u, dd, "tn", (BF,), name="mlp_dwdown").reshape(w_down.g8.shape)
    dp = matmul(dd, w_down, "nt", (BF,), epi=_drelu2_epi, extras=(r,), name="mlp_du")
    dw_up = matmul(hn, dp, "tn", (BF,), out_blocks=True, name="mlp_dwup")
    dhn = matmul(dp, w_up, "nt", name="mlp_dhn")
    dh, dg_pre = norm_bwd(h, g_pre, dhn, dres=dh2, name="mlp_pre_bwd")
    return dh, dict(mlp_pre_g=dg_pre, mlp_post_g=dg_post, w_up=dw_up, w_down=dw_down)


def rg_layer_fwd(h, g_pre, g_post, w_x, w_y, rgp, w_a, w_i, w_out):
    hn = norm_fwd(h, g_pre, BF, name="rg_norm")
    xr = matmul(hn, w_x, "nn", name="rg_in_x")
    gp = matmul(hn, w_y, "nn", name="rg_in_y")
    hg, hs = rg_fwd(xr, gp, rgp, w_a, w_i)
    m = matmul(hg, w_out, "nn", name="rg_out")
    return resadd_fwd(h, m, g_post, name="rg_res"), (h, hn, xr, gp, hg, hs, m)


def rg_layer_bwd(res, dh2, g_pre, g_post, w_x, w_y, rgp, w_a, w_i, w_out):
    h, hn, xr, gp, hg, hs, m = res
    dm, dg_post = norm_bwd(m, g_post, dh2, mask_pad=True, out_dtype=BF, name="rg_post_bwd")
    dw_out = matmul(hg, dm, "tn", name="rg_dwout")
    dhg = matmul(dm, w_out, "nt", name="rg_dhg")
    dxr, dgp, drgp, dwa, dwi = rg_bwd(xr, gp, rgp, w_a, w_i, hs, dhg)
    dw_x = matmul(hn, dxr, "tn", name="rg_dwx")
    dw_y = matmul(hn, dgp, "tn", name="rg_dwy")
    dhn = matmul([dxr, dgp], [w_x, w_y], "nt", name="rg_dhn")
    dh, dg_pre = norm_bwd(h, g_pre, dhn, dres=dh2, name="rg_pre_bwd")
    return dh, dict(mix_pre_g=dg_pre, mix_post_g=dg_post, rg_w_x=dw_x, rg_w_y=dw_y,
                    rg_conv_w=drgp[0:4], rg_conv_b=drgp[4], rg_b_a=drgp[5], rg_b_i=drgp[6], rg_lambda=drgp[7],
                    rg_w_a=dwa, rg_w_i=dwi, rg_w_out=dw_out)


def sm_layer_fwd(h, g_pre, g_post, w_in_p, convp, dt_bias, a_log, dskip, ssd_g, q_g, w_q_p, kv_g, w_kv_p, w_out, tabs,
                 carry=None, on_carried=None):
    hn = norm_fwd(h, g_pre, BF, name="sm_norm")
    proj = matmul(hn, w_in_p, "nn", name="sm_in")
    xbc_c = conv_silu_fwd(proj, OFF_XBC, SSD_CONV_CH, convp, name="ssd_conv")
    y, hst = ssd_fwd(xbc_c, proj, dt_bias, a_log, dskip)
    y_ssd = gated_norm_fwd(y, proj, ssd_g)
    cqn = norm_fwd(proj, q_g, BF, col_blk=OFF_CQ // MLA_Q_RANK, width=MLA_Q_RANK, name="q_norm")
    q_raw = matmul(cqn, w_q_p, "nn", name="q_up")
    ckvn = norm_fwd(proj, kv_g, BF, col_blk=OFF_CKV // MLA_KV_RANK, width=MLA_KV_RANK, name="kv_norm")
    kv_raw = matmul(ckvn, w_kv_p, "nn", name="kv_up")
    q_cat, k_cat, v = rope_fwd(q_raw, kv_raw, proj, tabs)
    o, lse, carried = attn_fwd(q_cat, k_cat, v, carry)
    if on_carried is not None:
        on_carried(carried)
    w_out = w_out()
    half = w_out.shape[0] // 2
    m = matmul([y_ssd, o], [KBlock(w_out, half, 0), KBlock(w_out, half, 1)], "nn", name="sm_out")
    res = (h, hn, proj, xbc_c, y, hst, cqn, ckvn, q_cat, k_cat, v, o, lse, y_ssd, m)
    return resadd_fwd(h, m, g_post, name="sm_res"), res


def sm_layer_bwd(res, dh2, g_pre, g_post, w_in_p, convp, dt_bias, a_log, dskip, ssd_g, q_g, w_q_p, kv_g, w_kv_p, w_out, tabs,
                 carry=None):
    h, hn, proj, xbc_c, y, hst, cqn, ckvn, q_cat, k_cat, v, o, lse, y_ssd, m = res
    w_out = w_out()
    dm, dg_post = norm_bwd(m, g_post, dh2, mask_pad=True, out_dtype=BF, name="sm_post_bwd")
    dw_out = jnp.concatenate([matmul(y_ssd, dm, "tn", name="sm_dwout_ssd"), matmul(o, dm, "tn", name="sm_dwout_att")], axis=0)
    dyab = matmul(dm, w_out, "nt", name="sm_dyab")
    dq_cat, dk_cat, dv, carried = attn_bwd(q_cat, k_cat, v, o, lse, dyab, carry(dw_out) if carry is not None else None)
    dq_raw, dkr = rope_bwd(dq_cat, dk_cat, tabs)
    kw = MLA_HEADS * LANE
    dw_kv_p = jnp.concatenate([matmul(ckvn, dk_cat, "tn", name="kv_dw_k"), matmul(ckvn, dv, "tn", name="kv_dw_v")], axis=1)
    dckvn = matmul([dk_cat, dv], [KBlock(w_kv_p, kw, 0), KBlock(w_kv_p, kw // 2, 2)], "nt", name="kv_dx")
    dckv, dg_kv = norm_bwd(proj, kv_g, dckvn, out_dtype=BF, col_blk=OFF_CKV // MLA_KV_RANK, width=MLA_KV_RANK,
                           name="kv_norm_bwd")
    dw_q_p = matmul(cqn, dq_raw, "tn", name="q_dw")
    dcqn = matmul(dq_raw, w_q_p, "nt", name="q_dx")
    dcq, dg_q = norm_bwd(proj, q_g, dcqn, out_dtype=BF, col_blk=OFF_CQ // MLA_Q_RANK, width=MLA_Q_RANK, name="q_norm_bwd")
    dy, dz, dg_ssd = gated_norm_bwd(y, proj, ssd_g, dyab)
    dxbc_c, ddt, dpar = ssd_bwd(xbc_c, proj, dt_bias, a_log, dskip, hst, dy)
    dxbc, dconvp = conv_silu_bwd(proj, OFF_XBC, SSD_CONV_CH, convp, dxbc_c, name="ssd_conv_bwd")
    pieces = [dz, dxbc, dckv, ddt, dkr, dcq]
    dw_in_p = jnp.concatenate([matmul(hn, pc, "tn", (BF,), name="sm_dwin_%d" % i) for i, pc in enumerate(pieces)], axis=1)
    third = SSD_CONV_CH // 3
    a_terms = [dz] + [KBlock(dxbc, third, i) for i in range(3)] + [dckv, ddt, dkr, dcq]
    b_terms = ([KBlock(w_in_p, SSD_D_INNER, 0)] + [KBlock(w_in_p, third, OFF_XBC // third + i) for i in range(3)]
               + [KBlock(w_in_p, MLA_KV_RANK, OFF_CKV // MLA_KV_RANK), KBlock(w_in_p, LANE, OFF_DT // LANE),
                  KBlock(w_in_p, LANE, OFF_KR // LANE), KBlock(w_in_p, MLA_Q_RANK, OFF_CQ // MLA_Q_RANK)])
    dhn = matmul(a_terms, b_terms, "nt", name="sm_dhn")
    dh, dg_pre = norm_bwd(h, g_pre, dhn, dres=dh2, name="sm_pre_bwd")
    grads = dict(mix_pre_g=dg_pre, mix_post_g=dg_post, w_in=w_in_cols_to_blocks(dw_in_p), ssd_conv_w=dconvp[0:4],
                 ssd_conv_b=dconvp[4], ssd_dt_bias=dpar[0, :SSD_HEADS], ssd_a_log=dpar[1, :SSD_HEADS],
                 ssd_d=dpar[2, :SSD_HEADS], ssd_norm_g=dg_ssd, mla_q_norm_g=dg_q, mla_w_q_up=_unpack_w_q(dw_q_p),
                 mla_kv_norm_g=dg_kv, mla_w_kv_up=_unpack_w_kv(dw_kv_p), w_out_ab=dw_out)
    return dh, grads, carried


W_IN_COLS = 3248
W_IN_SHARD = W_IN_COLS // N_DEV
W_IN_WIRE = 512


def _w_in_tables():
    src = np.full((IN_W,), -1, np.int64)
    src[0:2560] = np.arange(2560)
    src[OFF_CKV:OFF_CKV + 256] = 2960 + np.arange(256)
    src[OFF_DT:OFF_DT + 16] = 2560 + np.arange(16)
    src[OFF_KR + 64:OFF_KR + 96] = 3216 + np.arange(32)
    src[OFF_CQ:OFF_CQ + 384] = 2576 + np.arange(384)
    dev = np.where(src >= 0, src // W_IN_SHARD, -1).astype(np.int32).reshape(1, IN_W)
    col = np.where(src >= 0, src % W_IN_SHARD, 0).astype(np.int32).reshape(1, IN_W)
    return dev, col


W_IN_TILE = 384


def _w_in_devices_of_tile(dev):
    return [sorted(set(dev[0, t * W_IN_TILE:(t + 1) * W_IN_TILE].tolist()) - {-1}) for t in range(IN_W // W_IN_TILE)]


def _any_of(index, values):
    cond = index == values[0]
    for v in values[1:]:
        cond = cond | (index == v)
    return cond


def w_in_blocks_to_cols(g8):
    _, k, wp = g8.shape
    tn = W_IN_TILE
    dev, col = _w_in_tables()
    holders = _w_in_devices_of_tile(dev)

    def body(g_ref, dev_ref, col_ref, o_ref):
        i = pl.program_id(0)
        row = lax.broadcasted_iota(jnp.int32, (wp, tn), 0)
        o_ref[...] = jnp.zeros_like(o_ref)
        for j in range(N_DEV):
            tiles = [t for t, devs in enumerate(holders) if j in devs]
            if tiles:
                @pl.when(_any_of(i, tiles))
                def _(j=j):
                    sel = ((row == col_ref[...]) & (dev_ref[...] == j)).astype(BF)
                    o_ref[...] += jnp.dot(g_ref[j], sel, preferred_element_type=F32).astype(o_ref.dtype)

    dev, col = jnp.asarray(dev), jnp.asarray(col)
    return pl.pallas_call(
        body,
        out_shape=_sds((k, IN_W), BF),
        grid=(IN_W // tn,),
        in_specs=[pl.BlockSpec((N_DEV, k, wp), lambda i: (0, 0, 0)), pl.BlockSpec((1, tn), lambda i: (0, i)),
                  pl.BlockSpec((1, tn), lambda i: (0, i))],
        out_specs=pl.BlockSpec((k, tn), lambda i: (0, i)),
        compiler_params=_cp(("parallel",)),
        name="w_in_cols",
    )(g8, dev, col)


def w_in_cols_to_blocks(dw):
    k = dw.shape[0]
    tn = W_IN_TILE
    dev, col = _w_in_tables()
    holders = _w_in_devices_of_tile(dev)

    def body(dw_ref, dev_ref, col_ref, o_ref):
        j = pl.program_id(0)
        row = lax.broadcasted_iota(jnp.int32, (W_IN_WIRE, tn), 0)
        o_ref[...] = jnp.zeros_like(o_ref)
        for t, devs in enumerate(holders):
            if devs:
                @pl.when(_any_of(j, devs))
                def _(t=t):
                    cols = slice(t * tn, (t + 1) * tn)
                    sel = ((row == col_ref[:, cols]) & (dev_ref[:, cols] == j)).astype(BF)
                    o_ref[0] += lax.dot_general(dw_ref[:, cols], sel, NT_DIMS,
                                                preferred_element_type=F32).astype(o_ref.dtype)

    dev, col = jnp.asarray(dev), jnp.asarray(col)
    return pl.pallas_call(
        body,
        out_shape=_sds((N_DEV, k, W_IN_WIRE), BF),
        grid=(N_DEV,),
        in_specs=[pl.BlockSpec((k, IN_W), lambda j: (0, 0)), pl.BlockSpec((1, IN_W), lambda j: (0, 0)),
                  pl.BlockSpec((1, IN_W), lambda j: (0, 0))],
        out_specs=pl.BlockSpec((1, k, W_IN_WIRE), lambda j: (j, 0, 0)),
        compiler_params=_cp(("parallel",)),
        name="w_in_blocks",
    )(dw, dev, col)


def _pack_w_q(w):
    w3 = w.reshape(w.shape[0], MLA_HEADS, MLA_NOPE + MLA_ROPE)
    return jnp.pad(w3, ((0, 0), (0, 0), (0, LANE - MLA_NOPE - MLA_ROPE))).reshape(w.shape[0], MLA_HEADS * LANE)


def _unpack_w_q(p):
    return p.reshape(p.shape[0], MLA_HEADS, LANE)[:, :, :MLA_NOPE + MLA_ROPE].reshape(p.shape[0], -1)


def _pack_w_kv(w):
    w3 = w.reshape(w.shape[0], MLA_HEADS, MLA_NOPE + MLA_V)
    k = jnp.pad(w3[:, :, :MLA_NOPE], ((0, 0), (0, 0), (0, LANE - MLA_NOPE))).reshape(w.shape[0], MLA_HEADS * LANE)
    return jnp.concatenate([k, w3[:, :, MLA_NOPE:].reshape(w.shape[0], MLA_HEADS * MLA_V)], axis=1)


def _unpack_w_kv(p):
    k = p[:, :MLA_HEADS * LANE].reshape(p.shape[0], MLA_HEADS, LANE)[:, :, :MLA_NOPE]
    v = p[:, MLA_HEADS * LANE:].reshape(p.shape[0], MLA_HEADS, MLA_V)
    return jnp.concatenate([k, v], axis=2).reshape(p.shape[0], -1)


def _rows8(rows, width):
    a = jnp.concatenate([r.reshape(-1, width) for r in rows], axis=0)
    return jnp.pad(a, ((0, 8 - a.shape[0]), (0, 0)))


SLAB_ROWS = 16


def _to_slab(flat_list, lead=()):
    cat = jnp.concatenate(flat_list, axis=-1)
    n = cat.shape[-1]
    unit = SLAB_ROWS * PACK_W
    total = -(-n // unit) * unit
    cat = jnp.pad(cat, [(0, 0)] * len(lead) + [(0, total - n)])
    return cat.reshape(lead + (total // PACK_W, PACK_W))


def _from_flat(flat, shapes):
    out, off = [], 0
    for s in shapes:
        n = int(np.prod(s))
        out.append(flat[off:off + n].reshape(s))
        off += n
    return out


def _gathered_full(g8, axis):
    moved = jnp.moveaxis(g8, 0, axis)
    shp = moved.shape
    return moved.reshape(shp[:axis] + (shp[axis] * shp[axis + 1],) + shp[axis + 2:])


def _per_device(full, axis):
    shp = full.shape
    split = full.reshape(shp[:axis] + (N_DEV, shp[axis] // N_DEV) + shp[axis + 1:])
    return jnp.moveaxis(split, axis, 0)


ARG_NAMES = ['x', 'meta_tokens', 'mix_pre_g', 'mix_post_g', 'mlp_pre_g', 'mlp_post_g', 'w_up', 'w_down', 'w_in',
             'ssd_conv_w', 'ssd_conv_b', 'ssd_dt_bias', 'ssd_a_log', 'ssd_d', 'ssd_norm_g', 'mla_q_norm_g',
             'mla_w_q_up', 'mla_kv_norm_g', 'mla_w_kv_up', 'w_out_ab', 'rg_w_x', 'rg_w_y', 'rg_conv_w', 'rg_conv_b',
             'rg_w_a', 'rg_b_a', 'rg_w_i', 'rg_b_i', 'rg_lambda', 'rg_w_out']
WEIGHTS = ARG_NAMES[1:]
BIG = {'w_up': 2, 'w_down': 1, 'w_in': 2, 'mla_w_q_up': 2, 'mla_w_kv_up': 2, 'w_out_ab': 1, 'rg_w_x': 2,
       'rg_w_y': 2, 'rg_w_out': 1}
SMALL = {'meta_tokens': 1, 'ssd_conv_w': 2, 'rg_conv_w': 2, 'rg_conv_b': 1, 'rg_b_a': 1, 'rg_b_i': 1, 'rg_lambda': 1}
REPL = [n for n in WEIGHTS if n not in BIG and n not in SMALL]
REPL_MEDIUM = ['rg_w_a', 'rg_w_i']
REPL_TINY = [n for n in REPL if n not in REPL_MEDIUM]


def _piece_axes():
    axes = {}
    for n, ax in BIG.items():
        for i in range(DEPTH if n in ('w_up', 'w_down') else DEPTH // 2):
            axes[(n, i)] = ax - 1
    return axes


PIECE_AXIS = _piece_axes()
AS_BLOCKS = ('w_up', 'w_down')
_SM = lambda i: [(n, i) for n in ('w_in', 'mla_w_q_up', 'mla_w_kv_up', 'w_out_ab')]
_RG = lambda i: [(n, i) for n in ('rg_w_x', 'rg_w_y', 'rg_w_out')]
_MLP = lambda l: [('w_up', l), ('w_down', l)]
_SM_IN = lambda i: [(n, i) for n in ('w_in', 'mla_w_q_up', 'mla_w_kv_up')]
GATHER_FIRST = _SM_IN(0)
GATHER_AT = {0: [('w_out_ab', 0)] + _MLP(0) + _RG(0) + _MLP(1) + _SM(1), 2: _MLP(2) + _RG(1) + _MLP(3)}
SCATTER_AT = {2: _MLP(3) + _RG(1) + _MLP(2) + [('w_out_ab', 1)],
              0: _SM_IN(1) + _MLP(1) + _RG(0) + _MLP(0) + [('w_out_ab', 0)]}
SCATTER_LAST = _SM_IN(0)


def _wire_block(p, key):
    n, i = key
    blk = p[n][i]
    if n == 'w_in':
        blk = jnp.pad(blk, ((0, 0), (0, W_IN_WIRE - blk.shape[1])))
    return blk


def _step(p, moments):
    assert DEPTH == 4
    full = {n: [None] * p[n].shape[0] for n in BIG}
    full['w_in_g'] = [None] * p['w_in'].shape[0]

    def weight_blocks(group):
        return [_wire_block(p, k).astype(BF) for k in group]

    def take_weights(group, gathered):
        for (n, i), piece in zip(group, gathered):
            if n == 'w_in':
                full['w_in_g'][i] = piece
            elif n in AS_BLOCKS:
                full[n][i] = DevBlocks(piece, PIECE_AXIS[(n, i)])
            else:
                full[n][i] = _gathered_full(piece, PIECE_AXIS[(n, i)])

    def grad_blocks(group, gw):
        return [gw[k] if k[0] in AS_BLOCKS or k[0] == 'w_in' else _per_device(gw[k], PIECE_AXIS[k]).astype(BF)
                for k in group]

    parts = {}

    small_slab = _to_slab([p[n].reshape(-1) for n in SMALL])
    *first, small8 = all_gather(weight_blocks(GATHER_FIRST) + [small_slab], name="gather_first")
    take_weights(GATHER_FIRST, first)
    for n, piece in zip(SMALL, _from_flat_rows(small8, [p[n].shape for n in SMALL])):
        full[n] = _gathered_full(piece, SMALL[n])
    for n in REPL:
        full[n] = p[n]
    loss_local, grad_x, gw, gsmall_full, carried = _local_step(
        full, p['x'][0], p['loss_target'][0],
        fwd_carry=lambda layer: Exchange("gather", weight_blocks(GATHER_AT[layer])),
        on_fwd_carried=lambda layer, got: take_weights(GATHER_AT[layer], got),
        bwd_carry=lambda layer, gw_now: Exchange("scatter", grad_blocks(SCATTER_AT[layer], gw_now)))

    for layer, group in SCATTER_AT.items():
        parts.update(zip(group, carried[layer]))
    rep_flat = jnp.concatenate([gsmall_full[n].reshape(-1) for n in REPL_TINY])
    rep_n = rep_flat.shape[0]
    rep_chunk = -(-rep_n // (N_DEV * PACK_W * 8)) * PACK_W * 8
    rep8 = jnp.pad(rep_flat, (0, N_DEV * rep_chunk - rep_n)).reshape(N_DEV, rep_chunk)
    gsmall = _to_slab([_per_device(gsmall_full[n], SMALL[n]).reshape(N_DEV, -1) for n in SMALL] + [rep8], lead=(N_DEV,))
    medium8 = [gsmall_full[n].reshape(N_DEV, -1, LANE) for n in REPL_MEDIUM]
    received = all_to_all(grad_blocks(SCATTER_LAST, gw) + [gsmall] + medium8, name="scatter_last")
    n_last = len(SCATTER_LAST)
    parts.update(zip(SCATTER_LAST, received[:n_last]))
    ssmall = slab_sum(received[n_last], name="sum_small").reshape(-1)
    medium_mine = [slab_sum(r8, name="sum_" + n) for n, r8 in zip(REPL_MEDIUM, received[n_last + 1:])]
    g_loc = {'w_in': jnp.stack([slab_sum(parts[('w_in', i)], name="sum_w_in_%d" % i)[:, :W_IN_SHARD]
                                for i in range(p['w_in'].shape[0])], axis=0)}
    small_n = sum(int(np.prod(p[n].shape)) for n in SMALL)
    g_loc.update(zip(SMALL, _from_flat(ssmall, [p[n].shape for n in SMALL])))
    rep_mine = ssmall[small_n:small_n + rep_chunk].reshape(-1, PACK_W)
    rep_all, *medium_all = all_gather([rep_mine] + medium_mine, name="gather_replicated")
    g_loc.update(zip(REPL_TINY, _from_flat(rep_all.reshape(-1), [p[n].shape for n in REPL_TINY])))
    g_loc.update({n: g.reshape(p[n].shape) for n, g in zip(REPL_MEDIUM, medium_all)})

    out = {'loss': lax.psum(loss_local, ("x", "y", "c")), 'grad_x': grad_x[None]}
    small_names = list(SMALL) + REPL_TINY
    for n in list(BIG) + REPL_MEDIUM:
        shp = p[n].shape
        if n == 'w_in' or n in REPL_MEDIUM:
            v2 = lambda a: a.reshape(-1, shp[-1])
            d, nm, nv = adamw(v2(p[n]), v2(g_loc[n]), v2(moments['m_' + n]), v2(moments['v_' + n]), name="adamw_" + n)
            d, nm, nv = d.reshape(shp), nm.reshape(shp), nv.reshape(shp)
        else:
            g_loc[n], d, nm, nv = adamw_blocks(p[n], moments['m_' + n], moments['v_' + n],
                                               [parts[(n, i)] for i in range(shp[0])], name="adamw_" + n)
        out['delta_' + n], out['new_m_' + n], out['new_v_' + n] = d, nm, nv
    slab = lambda src: _to_slab([src(n).reshape(-1) for n in small_names])
    d, nm, nv = adamw(slab(lambda n: p[n]), slab(lambda n: g_loc[n]), slab(lambda n: moments['m_' + n]),
                      slab(lambda n: moments['v_' + n]), name="adamw_small")
    shapes = [p[n].shape for n in small_names]
    for key, flat in (('delta_', d), ('new_m_', nm), ('new_v_', nv)):
        for n, a in zip(small_names, _from_flat(flat.reshape(-1), shapes)):
            out[key + n] = a
    for n in WEIGHTS:
        out['grad_' + n] = g_loc[n]
    return out


def _local_step(full, x, target_rows, fwd_carry=None, on_fwd_carried=None, bwd_carry=None):
    t = PAD + N_META + x.shape[0]
    h = jnp.concatenate([jnp.zeros((PAD, D_MODEL), F32), full['meta_tokens'], x], axis=0)
    target = jnp.concatenate([jnp.zeros((PAD + N_META, D_MODEL), F32), target_rows], axis=0)
    tabs = rope_tables(t)

    def layer_args(layer):
        i = layer // 2
        if layer % 2 == 0:
            convp = _rows8([full['ssd_conv_w'][i], full['ssd_conv_b'][i]], SSD_CONV_CH)
            return (full['mix_pre_g'][layer], full['mix_post_g'][layer], w_in_blocks_to_cols(full['w_in_g'][i]), convp,
                    full['ssd_dt_bias'][i], full['ssd_a_log'][i], full['ssd_d'][i], full['ssd_norm_g'][i],
                    full['mla_q_norm_g'][i], _pack_w_q(full['mla_w_q_up'][i]), full['mla_kv_norm_g'][i],
                    _pack_w_kv(full['mla_w_kv_up'][i]), lambda: full['w_out_ab'][i], tabs)
        rgp = _rows8([full['rg_conv_w'][i], full['rg_conv_b'][i], full['rg_b_a'][i], full['rg_b_i'][i],
                      full['rg_lambda'][i]], LRU_WIDTH)
        return (full['mix_pre_g'][layer], full['mix_post_g'][layer], full['rg_w_x'][i], full['rg_w_y'][i], rgp,
                full['rg_w_a'][i], full['rg_w_i'][i], full['rg_w_out'][i])

    def mlp_args(layer):
        return (full['mlp_pre_g'][layer], full['mlp_post_g'][layer], full['w_up'][layer], full['w_down'][layer])

    saved = []
    for layer in range(DEPTH):
        la = layer_args(layer)
        if layer % 2 == 0:
            if fwd_carry is not None:
                h, res_mix = sm_layer_fwd(h, *la, carry=fwd_carry(layer),
                                          on_carried=lambda got, layer=layer: on_fwd_carried(layer, got))
            else:
                h, res_mix = sm_layer_fwd(h, *la)
        else:
            h, res_mix = rg_layer_fwd(h, *la)
        ma = mlp_args(layer)
        h, res_mlp = mlp_fwd(h, *ma)
        saved.append((la, ma, res_mix, res_mlp))
    loss_local, dh = loss_fwd_bwd(h, target)

    others = {n: [None] * len(full[n]) for n in WEIGHTS if n not in BIG and n != 'meta_tokens'}
    gw, carried = {}, {}
    for layer in reversed(range(DEPTH)):
        la, ma, res_mix, res_mlp = saved[layer]
        dh, gm = mlp_bwd(res_mlp, dh, *ma)
        if layer % 2 == 0:
            for n in ('w_up', 'w_down'):
                gw[(n, layer)] = gm[n]
            carry = None
            if bwd_carry is not None:
                carry = lambda dw_out, layer=layer: bwd_carry(layer, {**gw, ('w_out_ab', layer // 2): dw_out})
            dh, gx, carried[layer] = sm_layer_bwd(res_mix, dh, *la, carry=carry)
        else:
            dh, gx = rg_layer_bwd(res_mix, dh, *la)
        for n, g in list(gm.items()) + list(gx.items()):
            i = layer if n in ('mix_pre_g', 'mix_post_g', 'mlp_pre_g', 'mlp_post_g', 'w_up', 'w_down') else layer // 2
            if n in BIG:
                gw[(n, i)] = g
            else:
                others[n][i] = g
    gothers = {n: jnp.stack(v, axis=0) for n, v in others.items()}
    gothers['meta_tokens'] = dh[PAD:PAD + N_META]
    return loss_local, dh[PAD + N_META:], gw, gothers, carried


def _from_flat_rows(g8, shapes):
    flat = g8.reshape(N_DEV, -1)
    out, off = [], 0
    for s in shapes:
        n = int(np.prod(s))
        out.append(flat[:, off:off + n].reshape((N_DEV,) + tuple(s)))
        off += n
    return out


def kernel(x, meta_tokens, mix_pre_g, mix_post_g, mlp_pre_g, mlp_post_g, w_up, w_down, w_in, ssd_conv_w, ssd_conv_b, ssd_dt_bias, ssd_a_log, ssd_d, ssd_norm_g, mla_q_norm_g, mla_w_q_up, mla_kv_norm_g, mla_w_kv_up, w_out_ab, rg_w_x, rg_w_y, rg_conv_w, rg_conv_b, rg_w_a, rg_b_a, rg_w_i, rg_b_i, rg_lambda, rg_w_out, loss_target, m_meta_tokens, m_mix_pre_g, m_mix_post_g, m_mlp_pre_g, m_mlp_post_g, m_w_up, m_w_down, m_w_in, m_ssd_conv_w, m_ssd_conv_b, m_ssd_dt_bias, m_ssd_a_log, m_ssd_d, m_ssd_norm_g, m_mla_q_norm_g, m_mla_w_q_up, m_mla_kv_norm_g, m_mla_w_kv_up, m_w_out_ab, m_rg_w_x, m_rg_w_y, m_rg_conv_w, m_rg_conv_b, m_rg_w_a, m_rg_b_a, m_rg_w_i, m_rg_b_i, m_rg_lambda, m_rg_w_out, v_meta_tokens, v_mix_pre_g, v_mix_post_g, v_mlp_pre_g, v_mlp_post_g, v_w_up, v_w_down, v_w_in, v_ssd_conv_w, v_ssd_conv_b, v_ssd_dt_bias, v_ssd_a_log, v_ssd_d, v_ssd_norm_g, v_mla_q_norm_g, v_mla_w_q_up, v_mla_kv_norm_g, v_mla_w_kv_up, v_w_out_ab, v_rg_w_x, v_rg_w_y, v_rg_conv_w, v_rg_conv_b, v_rg_w_a, v_rg_b_a, v_rg_w_i, v_rg_b_i, v_rg_lambda, v_rg_w_out):
    args = (x, meta_tokens, mix_pre_g, mix_post_g, mlp_pre_g, mlp_post_g, w_up, w_down, w_in, ssd_conv_w, ssd_conv_b, ssd_dt_bias, ssd_a_log, ssd_d, ssd_norm_g, mla_q_norm_g, mla_w_q_up, mla_kv_norm_g, mla_w_kv_up, w_out_ab, rg_w_x, rg_w_y, rg_conv_w, rg_conv_b, rg_w_a, rg_b_a, rg_w_i, rg_b_i, rg_lambda, rg_w_out, loss_target, m_meta_tokens, m_mix_pre_g, m_mix_post_g, m_mlp_pre_g, m_mlp_post_g, m_w_up, m_w_down, m_w_in, m_ssd_conv_w, m_ssd_conv_b, m_ssd_dt_bias, m_ssd_a_log, m_ssd_d, m_ssd_norm_g, m_mla_q_norm_g, m_mla_w_q_up, m_mla_kv_norm_g, m_mla_w_kv_up, m_w_out_ab, m_rg_w_x, m_rg_w_y, m_rg_conv_w, m_rg_conv_b, m_rg_w_a, m_rg_b_a, m_rg_w_i, m_rg_b_i, m_rg_lambda, m_rg_w_out, v_meta_tokens, v_mix_pre_g, v_mix_post_g, v_mlp_pre_g, v_mlp_post_g, v_w_up, v_w_down, v_w_in, v_ssd_conv_w, v_ssd_conv_b, v_ssd_dt_bias, v_ssd_a_log, v_ssd_d, v_ssd_norm_g, v_mla_q_norm_g, v_mla_w_q_up, v_mla_kv_norm_g, v_mla_w_kv_up, v_w_out_ab, v_rg_w_x, v_rg_w_y, v_rg_conv_w, v_rg_conv_b, v_rg_w_a, v_rg_b_a, v_rg_w_i, v_rg_b_i, v_rg_lambda, v_rg_w_out,)
    n_w = len(ARG_NAMES)
    p = dict(zip(ARG_NAMES, args[:n_w]))
    p['loss_target'] = args[n_w]
    moments = {}
    for i, n in enumerate(WEIGHTS):
        moments['m_' + n] = args[n_w + 1 + i]
        moments['v_' + n] = args[n_w + 1 + len(WEIGHTS) + i]
    out = _step(p, moments)
    res = [out['loss'], out['grad_x']]
    for prefix in ('grad_', 'delta_', 'new_m_', 'new_v_'):
        res += [out[prefix + n] for n in WEIGHTS]
    return tuple(res)
```

```python
import functools
import math

import numpy as np
import jax
import jax.numpy as jnp
from jax import lax
from jax.experimental import pallas as pl
from jax.experimental.pallas import tpu as pltpu

F32 = jnp.float32
BF = jnp.bfloat16
HI = lax.Precision.HIGHEST

D_MODEL = 1024
DEPTH = 4
N_META = 16
CHUNK = 128
PAD = CHUNK - N_META
EPS = 1e-6
SSD_HEADS = 16
SSD_HEAD_DIM = 64
SSD_D_INNER = 1024
SSD_STATE = 128
SSD_CONV_CH = 1536
MLA_HEADS = 16
MLA_NOPE = 64
MLA_ROPE = 32
MLA_V = 64
MLA_Q_RANK = 384
MLA_KV_RANK = 256
ROPE_BASE = 10000.0
LRU_WIDTH = 1280
LRU_BLOCKS = 10
LRU_C = 8.0
D_FF = 4096
N_DEV = 8
LANE = 128
IN_W = 3456
OFF_Z, OFF_XBC, OFF_CKV, OFF_DT, OFF_KR, OFF_CQ = 0, 1024, 2560, 2816, 2944, 3072

ADAM_LR = 0.001
ADAM_B1 = 0.9
ADAM_B2 = 0.999
ADAM_EPS = 1e-08
ADAM_WD = 0.01
ADAM_STEP = 10

VMEM_LIMIT = 56 * 1024 * 1024
NEG = -1e30


def _pick(n, cands):
    for c in cands:
        if n % c == 0:
            return c
    return n


def _cp(sem=None):
    return pltpu.CompilerParams(dimension_semantics=sem, vmem_limit_bytes=VMEM_LIMIT)


def _sds(shape, dtype):
    return jax.ShapeDtypeStruct(tuple(shape), dtype)


def _silu(x):
    return x * jax.nn.sigmoid(x)


def _softplus(x):
    return jnp.maximum(x, 0.0) + jnp.log(1.0 + jnp.exp(-jnp.abs(x)))


def _gelu(x):
    c = math.sqrt(2.0 / math.pi)
    return 0.5 * x * (1.0 + jnp.tanh(c * (x + 0.044715 * (x * x * x))))


def _row_mask(i, tr, shape, first_valid=PAD):
    row = i * tr + lax.broadcasted_iota(jnp.int32, shape, 0)
    return row >= first_valid


class KBlock:
    def __init__(self, arr, width, blk):
        self.arr, self.width, self.blk = arr, width, blk


class DevBlocks:
    def __init__(self, g8, axis):
        self.g8, self.axis = g8, axis
        _, r, c = g8.shape
        self.shape = (N_DEV * r, c) if axis == 0 else (r, N_DEV * c)


NN_DIMS = (((1,), (0,)), ((), ()))
MM_TALL_K = 1536


def matmul(a, b, mode, out_dtypes=(F32,), epi=None, extras=(), name="mm", tm=None, tn=None, out_blocks=False):
    a_terms = a if isinstance(a, (list, tuple)) else [a]
    b_terms = b if isinstance(b, (list, tuple)) else [b]
    assert len(a_terms) == len(b_terms) and (mode != "tn" or len(a_terms) == 1)
    arr_of = lambda t: t.arr if isinstance(t, KBlock) else t
    if mode == "tn":
        m, n = a_terms[0].shape[1], b_terms[0].shape[1]
    else:
        m = arr_of(a_terms[0]).shape[0]
        b0 = b_terms[0]
        n = (b0.shape if isinstance(b0, DevBlocks) else arr_of(b0).shape)[1 if mode == "nn" else 0]
    if mode == "tn":
        tm = _pick(m, (1024, 512, 384, 256, 128))
    else:
        k_all = sum(t.width if isinstance(t, KBlock) else t.shape[1] for t in a_terms)
        tall = (2112,) if k_all <= MM_TALL_K else ()
        tm = tm or _pick(m, tall + (1056, 1024, 768, 640, 512, 384, 256, 128))
    tn = tn or _pick(n, (512, 640, 384, 256, 128))
    dims = {"nn": NN_DIMS, "nt": NT_DIMS, "tn": TN_DIMS}[mode]

    in_specs, args, plan = [], [], []
    for ta, tb in zip(a_terms, b_terms):
        if mode == "tn":
            k = ta.shape[0]
            in_specs += [pl.BlockSpec((k, tm), lambda i, j: (0, i)), pl.BlockSpec((k, tn), lambda i, j: (0, j))]
            args += [ta, tb]
            plan.append(None)
            continue
        if isinstance(ta, KBlock):
            kw, ka = ta.width, ta.blk
            in_specs.append(pl.BlockSpec((tm, kw), lambda i, j, ka=ka: (i, ka)))
        else:
            kw = ta.shape[1]
            in_specs.append(pl.BlockSpec((tm, kw), lambda i, j: (i, 0)))
        args.append(arr_of(ta))
        if isinstance(tb, DevBlocks):
            _, r, c = tb.g8.shape
            split_k = tb.axis == (0 if mode == "nn" else 1)
            if split_k:
                kd = r if mode == "nn" else c
                assert kw == N_DEV * kd
                blk = (N_DEV, kd, tn) if mode == "nn" else (N_DEV, tn, kd)
                in_specs.append(pl.BlockSpec(blk, (lambda i, j: (0, 0, j)) if mode == "nn" else (lambda i, j: (0, j, 0))))
                plan.append(kd)
            else:
                per = (c if mode == "nn" else r) // tn
                blk = (None, kw, tn) if mode == "nn" else (None, tn, kw)
                in_specs.append(pl.BlockSpec(blk, (lambda i, j, per=per: (j // per, 0, j % per)) if mode == "nn"
                                             else (lambda i, j, per=per: (j // per, j % per, 0))))
                plan.append(None)
            args.append(tb.g8)
        else:
            kb = tb.blk if isinstance(tb, KBlock) else 0
            assert (tb.width if isinstance(tb, KBlock) else tb.shape[0 if mode == "nn" else 1]) == kw
            in_specs.append(pl.BlockSpec((kw, tn), lambda i, j, kb=kb: (kb, j)) if mode == "nn"
                            else pl.BlockSpec((tn, kw), lambda i, j, kb=kb: (j, kb)))
            args.append(arr_of(tb))
            plan.append(None)
    n_terms, n_ex = len(plan), len(extras)

    def body(*refs):
        ex_refs, out_refs = refs[2 * n_terms:2 * n_terms + n_ex], refs[2 * n_terms + n_ex:]
        acc = None
        for t, kd in enumerate(plan):
            a_ref, b_ref = refs[2 * t], refs[2 * t + 1]
            if kd is None:
                parts = [lax.dot_general(a_ref[...].astype(BF), b_ref[...].astype(BF), dims, preferred_element_type=F32)]
            else:
                parts = [lax.dot_general(a_ref[:, d * kd:(d + 1) * kd].astype(BF), b_ref[d].astype(BF), dims,
                                         preferred_element_type=F32) for d in range(N_DEV)]
            for part in parts:
                acc = part if acc is None else acc + part
        outs = (acc,) if epi is None else epi(acc, *[r[...] for r in ex_refs])
        for r, o in zip(out_refs, outs):
            r[...] = o.astype(r.dtype)

    o_spec = pl.BlockSpec((tm, tn), lambda i, j: (i, j))
    if out_blocks:
        per = n // N_DEV // tn
        out_shape = tuple(_sds((N_DEV, m, n // N_DEV), dt) for dt in out_dtypes)
        out_specs = tuple(pl.BlockSpec((None, tm, tn), lambda i, j: (j // per, i, j % per)) for _ in out_dtypes)
    else:
        out_shape = tuple(_sds((m, n), dt) for dt in out_dtypes)
        out_specs = tuple(o_spec for _ in out_dtypes)
    outs = pl.pallas_call(
        body,
        out_shape=out_shape,
        grid=(m // tm, n // tn),
        in_specs=in_specs + [o_spec] * n_ex,
        out_specs=out_specs,
        compiler_params=_cp(("parallel", "parallel")),
        name=name,
    )(*args, *extras)
    return outs[0] if len(out_dtypes) == 1 else outs


def _rt(t):
    return _pick(t, (384, 256, 128))


def norm_fwd(x, g, out_dtype, col_blk=0, width=None, name="norm_fwd"):
    t = x.shape[0]
    w = width or x.shape[1]
    tr = _rt(t)

    def body(x_ref, g_ref, o_ref):
        xv = x_ref[...]
        r = lax.rsqrt(jnp.mean(xv * xv, axis=-1, keepdims=True) + EPS)
        o_ref[...] = (xv * r * g_ref[...]).astype(o_ref.dtype)

    return pl.pallas_call(
        body,
        out_shape=_sds((t, w), out_dtype),
        grid=(t // tr,),
        in_specs=[pl.BlockSpec((tr, w), lambda i: (i, col_blk)), pl.BlockSpec((1, w), lambda i: (0, 0))],
        out_specs=pl.BlockSpec((tr, w), lambda i: (i, 0)),
        compiler_params=_cp(("parallel",)),
        name=name,
    )(x, g.reshape(1, w))


def norm_bwd(x, g, dy, dres=None, mask_pad=False, out_dtype=F32, col_blk=0, width=None, dy_col_blk=0, name="norm_bwd"):
    t = x.shape[0]
    w = width or x.shape[1]
    tr = _rt(t)
    has_res = dres is not None

    def body(x_ref, g_ref, dy_ref, *rest):
        if has_res:
            res_ref, dx_ref, dg_ref = rest
        else:
            dx_ref, dg_ref = rest
        i = pl.program_id(0)
        xv = x_ref[...]
        dyv = dy_ref[...].astype(F32)
        if mask_pad:
            dyv = jnp.where(_row_mask(i, tr, dyv.shape), dyv, 0.0)
        r = lax.rsqrt(jnp.mean(xv * xv, axis=-1, keepdims=True) + EPS)
        xh = xv * r
        dyg = dyv * g_ref[...]
        dx = r * (dyg - xh * jnp.mean(dyg * xh, axis=-1, keepdims=True))
        if has_res:
            dx = dx + res_ref[...]
        dx_ref[...] = dx.astype(dx_ref.dtype)

        @pl.when(i == 0)
        def _():
            dg_ref[...] = jnp.zeros_like(dg_ref)

        dg_ref[...] += jnp.sum(dyv * xh, axis=0, keepdims=True)

    in_specs = [pl.BlockSpec((tr, w), lambda i: (i, col_blk)), pl.BlockSpec((1, w), lambda i: (0, 0)),
                pl.BlockSpec((tr, w), lambda i: (i, dy_col_blk))]
    args = [x, g.reshape(1, w), dy]
    if has_res:
        in_specs.append(pl.BlockSpec((tr, w), lambda i: (i, 0)))
        args.append(dres)
    dx, dg = pl.pallas_call(
        body,
        out_shape=(_sds((t, w), out_dtype), _sds((1, w), F32)),
        grid=(t // tr,),
        in_specs=in_specs,
        out_specs=(pl.BlockSpec((tr, w), lambda i: (i, 0)), pl.BlockSpec((1, w), lambda i: (0, 0))),
        compiler_params=_cp(("arbitrary",)),
        name=name,
    )(*args)
    return dx, dg.reshape(w)


def resadd_fwd(h, m, g, name="resadd"):
    t, w = h.shape
    tr = _rt(t)

    def body(h_ref, m_ref, g_ref, o_ref):
        mv = m_ref[...]
        r = lax.rsqrt(jnp.mean(mv * mv, axis=-1, keepdims=True) + EPS)
        y = mv * r * g_ref[...]
        o_ref[...] = h_ref[...] + jnp.where(_row_mask(pl.program_id(0), tr, y.shape), y, 0.0)

    return pl.pallas_call(
        body,
        out_shape=_sds((t, w), F32),
        grid=(t // tr,),
        in_specs=[pl.BlockSpec((tr, w), lambda i: (i, 0)), pl.BlockSpec((tr, w), lambda i: (i, 0)),
                  pl.BlockSpec((1, w), lambda i: (0, 0))],
        out_specs=pl.BlockSpec((tr, w), lambda i: (i, 0)),
        compiler_params=_cp(("parallel",)),
        name=name,
    )(h, m, g.reshape(1, w))


def loss_fwd_bwd(h, target):
    t, w = h.shape
    tr = _rt(t)

    def body(h_ref, t_ref, s_ref, dh_ref):
        i = pl.program_id(0)
        err = h_ref[...] - t_ref[...]
        err = jnp.where(_row_mask(i, tr, err.shape, PAD + N_META), err, 0.0)
        dh_ref[...] = err * (1.0 / w)

        @pl.when(i == 0)
        def _():
            s_ref[...] = jnp.zeros_like(s_ref)

        s_ref[...] += jnp.sum(err * err).reshape(1, 1)

    s, dh = pl.pallas_call(
        body,
        out_shape=(_sds((1, LANE), F32), _sds((t, w), F32)),
        grid=(t // tr,),
        in_specs=[pl.BlockSpec((tr, w), lambda i: (i, 0)), pl.BlockSpec((tr, w), lambda i: (i, 0))],
        out_specs=(pl.BlockSpec((1, LANE), lambda i: (0, 0)), pl.BlockSpec((tr, w), lambda i: (i, 0))),
        compiler_params=_cp(("arbitrary",)),
        name="loss",
    )(h, target)
    return 0.5 * s[0, 0] / w, dh


def _shift_down(ext, k, n):
    return pltpu.roll(ext, k, 0)[8:]


def _conv_pre(ext, x, w_ref, n):
    return (w_ref[4:5, :] + w_ref[3:4, :] * x + w_ref[2:3, :] * _shift_down(ext, 1, n)
            + w_ref[1:2, :] * _shift_down(ext, 2, n) + w_ref[0:1, :] * _shift_down(ext, 3, n))


def _conv_bwd_parts(dpre, dnext, x, ext, w_ref, n):
    extd = jnp.concatenate([dpre, dnext], axis=0)
    ln = n + 8
    dx = (w_ref[3:4, :] * dpre + w_ref[2:3, :] * pltpu.roll(extd, ln - 1, 0)[:n]
          + w_ref[1:2, :] * pltpu.roll(extd, ln - 2, 0)[:n] + w_ref[0:1, :] * pltpu.roll(extd, ln - 3, 0)[:n])
    sums = [jnp.sum(dpre * _shift_down(ext, 3, n), axis=0, keepdims=True),
            jnp.sum(dpre * _shift_down(ext, 2, n), axis=0, keepdims=True),
            jnp.sum(dpre * _shift_down(ext, 1, n), axis=0, keepdims=True),
            jnp.sum(dpre * x, axis=0, keepdims=True),
            jnp.sum(dpre, axis=0, keepdims=True)]
    return dx, sums


def _rows_block(sums):
    w = sums[0].shape[1]
    row = lax.broadcasted_iota(jnp.int32, (8, w), 0)
    out = jnp.zeros((8, w), F32)
    for k, s in enumerate(sums):
        out = jnp.where(row == k, s, out)
    return out


CONV_BLOCK = 512


def conv_silu_fwd(x, col0, c, wb, name="conv_fwd"):
    t = x.shape[0]
    cw = _pick(c, (CONV_BLOCK, LANE))
    nblk, col0_blk = c // cw, col0 // cw
    assert col0 % cw == 0
    tr = _rt(t)

    def body(x_ref, w_ref, o_ref, prev):
        ti = pl.program_id(1)

        @pl.when(ti == 0)
        def _():
            prev[...] = jnp.zeros_like(prev)

        xv = x_ref[...]
        ext = jnp.concatenate([prev[...], xv], axis=0)
        o_ref[...] = _silu(_conv_pre(ext, xv, w_ref, tr))
        prev[...] = xv[tr - 8:, :]

    return pl.pallas_call(
        body,
        out_shape=_sds((t, c), F32),
        grid=(nblk, t // tr),
        in_specs=[pl.BlockSpec((tr, cw), lambda cb, ti: (ti, col0_blk + cb)),
                  pl.BlockSpec((8, cw), lambda cb, ti: (0, cb))],
        out_specs=pl.BlockSpec((tr, cw), lambda cb, ti: (ti, cb)),
        scratch_shapes=[pltpu.VMEM((8, cw), F32)],
        compiler_params=_cp(("parallel", "arbitrary")),
        name=name,
    )(x, wb)


def conv_silu_bwd(x, col0, c, wb, dout, name="conv_bwd"):
    t = x.shape[0]
    cw = _pick(c, (CONV_BLOCK, LANE))
    nblk, col0_blk = c // cw, col0 // cw
    assert col0 % cw == 0
    tr = _rt(t)
    nt = t // tr
    r8 = tr // 8

    def body(x_ref, xp_ref, w_ref, do_ref, dx_ref, dwb_ref, dnext):
        ti = pl.program_id(1)
        tt = nt - 1 - ti

        @pl.when(ti == 0)
        def _():
            dnext[...] = jnp.zeros_like(dnext)
            dwb_ref[...] = jnp.zeros_like(dwb_ref)

        xv = x_ref[...]
        halo = jnp.where(tt > 0, xp_ref[...], 0.0)
        ext = jnp.concatenate([halo, xv], axis=0)
        pre = _conv_pre(ext, xv, w_ref, tr)
        s = jax.nn.sigmoid(pre)
        dpre = do_ref[...] * (s + pre * s * (1.0 - s))
        dx, sums = _conv_bwd_parts(dpre, dnext[...], xv, ext, w_ref, tr)
        dx_ref[...] = dx.astype(dx_ref.dtype)
        dwb_ref[...] += _rows_block(sums)
        dnext[...] = dpre[:8, :]

    return pl.pallas_call(
        body,
        out_shape=(_sds((t, c), BF), _sds((8, c), F32)),
        grid=(nblk, nt),
        in_specs=[pl.BlockSpec((tr, cw), lambda cb, ti: (nt - 1 - ti, col0_blk + cb)),
                  pl.BlockSpec((8, cw), lambda cb, ti: (jnp.maximum((nt - 1 - ti) * r8 - 1, 0), col0_blk + cb)),
                  pl.BlockSpec((8, cw), lambda cb, ti: (0, cb)),
                  pl.BlockSpec((tr, cw), lambda cb, ti: (nt - 1 - ti, cb))],
        out_specs=(pl.BlockSpec((tr, cw), lambda cb, ti: (nt - 1 - ti, cb)),
                   pl.BlockSpec((8, cw), lambda cb, ti: (0, cb))),
        scratch_shapes=[pltpu.VMEM((8, cw), F32)],
        compiler_params=_cp(("parallel", "arbitrary")),
        name=name,
    )(x, x, wb, dout)


def gated_norm_fwd(y, proj, g, name="gnorm_fwd"):
    t, w = y.shape
    tr = _rt(t)

    def body(y_ref, z_ref, g_ref, o_ref):
        v = y_ref[...] * _silu(z_ref[...])
        r = lax.rsqrt(jnp.mean(v * v, axis=-1, keepdims=True) + EPS)
        o_ref[...] = (v * r * g_ref[...]).astype(o_ref.dtype)

    return pl.pallas_call(
        body,
        out_shape=_sds((t, w), BF),
        grid=(t // tr,),
        in_specs=[pl.BlockSpec((tr, w), lambda i: (i, 0)), pl.BlockSpec((tr, w), lambda i: (i, OFF_Z // w)),
                  pl.BlockSpec((1, w), lambda i: (0, 0))],
        out_specs=pl.BlockSpec((tr, w), lambda i: (i, 0)),
        compiler_params=_cp(("parallel",)),
        name=name,
    )(y, proj, g.reshape(1, w))


def gated_norm_bwd(y, proj, g, dyab, name="gnorm_bwd"):
    t, w = y.shape
    tr = _rt(t)

    def body(y_ref, z_ref, g_ref, do_ref, dy_ref, dz_ref, dg_ref):
        i = pl.program_id(0)
        yv, zv, dov = y_ref[...], z_ref[...], do_ref[...]
        s = jax.nn.sigmoid(zv)
        sz = zv * s
        v = yv * sz
        r = lax.rsqrt(jnp.mean(v * v, axis=-1, keepdims=True) + EPS)
        vh = v * r
        dvg = dov * g_ref[...]
        dv = r * (dvg - vh * jnp.mean(dvg * vh, axis=-1, keepdims=True))
        dy_ref[...] = dv * sz
        dz_ref[...] = (dv * yv * (s + sz * (1.0 - s))).astype(dz_ref.dtype)

        @pl.when(i == 0)
        def _():
            dg_ref[...] = jnp.zeros_like(dg_ref)

        dg_ref[...] += jnp.sum(dov * vh, axis=0, keepdims=True)

    dy, dz, dg = pl.pallas_call(
        body,
        out_shape=(_sds((t, w), F32), _sds((t, w), BF), _sds((1, w), F32)),
        grid=(t // tr,),
        in_specs=[pl.BlockSpec((tr, w), lambda i: (i, 0)), pl.BlockSpec((tr, w), lambda i: (i, OFF_Z // w)),
                  pl.BlockSpec((1, w), lambda i: (0, 0)), pl.BlockSpec((tr, w), lambda i: (i, 0))],
        out_specs=(pl.BlockSpec((tr, w), lambda i: (i, 0)), pl.BlockSpec((tr, w), lambda i: (i, 0)),
                   pl.BlockSpec((1, w), lambda i: (0, 0))),
        compiler_params=_cp(("arbitrary",)),
        name=name,
    )(y, proj, g.reshape(1, w), dyab)
    return dy, dz, dg.reshape(w)


def rope_tables(t):
    inv = ROPE_BASE ** (-jnp.arange(0, MLA_ROPE, 2, dtype=F32) / MLA_ROPE)
    pos = (jnp.arange(t, dtype=F32) - PAD)[:, None]
    ang = pos * inv[None, :]
    cos, sin = jnp.cos(ang), jnp.sin(ang)
    z16 = jnp.zeros((t, 16), F32)
    z32 = jnp.zeros((t, 32), F32)
    c = jnp.concatenate([jnp.ones((t, 64), F32), cos, cos, z32], axis=1)
    s1 = jnp.concatenate([jnp.zeros((t, 64), F32), z16, sin, z32], axis=1)
    s2 = jnp.concatenate([jnp.zeros((t, 64), F32), -sin, z16, z32], axis=1)
    return c, s1, s2


def _rope(x, c, s1, s2):
    return x * c + pltpu.roll(x, 16, 1) * s1 + pltpu.roll(x, LANE - 16, 1) * s2


def _rope_t(d, c, s1, s2):
    return d * c + pltpu.roll(d * s1, LANE - 16, 1) + pltpu.roll(d * s2, 16, 1)


def rope_fwd(q_raw, kv_raw, proj, tabs):
    t = q_raw.shape[0]
    tr = _rt(t)
    hw = MLA_HEADS * LANE

    def body(q_ref, k_ref, v_ref, kr_ref, c_ref, s1_ref, s2_ref, qo_ref, ko_ref, vo_ref):
        c, s1, s2 = c_ref[...], s1_ref[...], s2_ref[...]
        kr = _rope(kr_ref[...], c, s1, s2)
        for h in range(MLA_HEADS):
            sl = slice(h * LANE, (h + 1) * LANE)
            qo_ref[:, sl] = (_rope(q_ref[:, sl], c, s1, s2) * Q_PRESCALE).astype(BF)
            ko_ref[:, sl] = (k_ref[:, sl] + kr).astype(BF)
        vo_ref[...] = v_ref[...].astype(BF)

    tab_spec = pl.BlockSpec((tr, LANE), lambda i: (i, 0))
    return pl.pallas_call(
        body,
        out_shape=(_sds((t, hw), BF), _sds((t, hw), BF), _sds((t, 1024), BF)),
        grid=(t // tr,),
        in_specs=[pl.BlockSpec((tr, hw), lambda i: (i, 0)), pl.BlockSpec((tr, hw), lambda i: (i, 0)),
                  pl.BlockSpec((tr, 1024), lambda i: (i, 2)), pl.BlockSpec((tr, LANE), lambda i: (i, OFF_KR // LANE)),
                  tab_spec, tab_spec, tab_spec],
        out_specs=(pl.BlockSpec((tr, hw), lambda i: (i, 0)), pl.BlockSpec((tr, hw), lambda i: (i, 0)),
                   pl.BlockSpec((tr, 1024), lambda i: (i, 0))),
        compiler_params=_cp(("parallel",)),
        name="rope_fwd",
    )(q_raw, kv_raw, kv_raw, proj, *tabs)


def rope_bwd(dq_cat, dk_cat, tabs):
    t = dq_cat.shape[0]
    tr = _rt(t)
    hw = MLA_HEADS * LANE

    def body(dq_ref, dk_ref, c_ref, s1_ref, s2_ref, dqo_ref, dkr_ref):
        c, s1, s2 = c_ref[...], s1_ref[...], s2_ref[...]
        acc = jnp.zeros((tr, LANE), F32)
        for h in range(MLA_HEADS):
            sl = slice(h * LANE, (h + 1) * LANE)
            dqo_ref[:, sl] = _rope_t(dq_ref[:, sl] * ATT_SCALE, c, s1, s2).astype(BF)
            acc = acc + dk_ref[:, sl]
        lane = lax.broadcasted_iota(jnp.int32, (tr, LANE), 1)
        dkr_ref[...] = jnp.where((lane >= 64) & (lane < 96), _rope_t(acc, c, s1, s2), 0.0)

    tab_spec = pl.BlockSpec((tr, LANE), lambda i: (i, 0))
    return pl.pallas_call(
        body,
        out_shape=(_sds((t, hw), BF), _sds((t, LANE), F32)),
        grid=(t // tr,),
        in_specs=[pl.BlockSpec((tr, hw), lambda i: (i, 0)), pl.BlockSpec((tr, hw), lambda i: (i, 0)),
                  tab_spec, tab_spec, tab_spec],
        out_specs=(pl.BlockSpec((tr, hw), lambda i: (i, 0)), pl.BlockSpec((tr, LANE), lambda i: (i, 0))),
        compiler_params=_cp(("parallel",)),
        name="rope_bwd",
    )(dq_cat, dk_cat, *tabs)


ATT_SCALE = (MLA_NOPE + MLA_ROPE) ** -0.5
LOG2E = math.log2(math.e)
Q_PRESCALE = ATT_SCALE * LOG2E
CARRY_MIDDLE_PAIR = 6
NT_DIMS = (((1,), (1,)), ((), ()))
TN_DIMS = (((0,), (0,)), ((), ()))


def _att_mask(qi, ki, tq, tk):
    qpos = qi * tq + lax.broadcasted_iota(jnp.int32, (tq, tk), 0)
    kpos = ki * tk + lax.broadcasted_iota(jnp.int32, (tq, tk), 1)
    return (kpos <= qpos) & (kpos >= PAD)


def _half_masks(n):
    lane = lax.broadcasted_iota(jnp.int32, (n, LANE), 1)
    return lane < 64, lane >= 64


def _att_tile(t):
    return _pick(t, (384, 256, 128))


def _ds(i, n):
    return pl.ds(i * n, n) if isinstance(i, int) else pl.ds(pl.multiple_of(i * n, n), n)


FWD_PAIRS = 2


def attn_fwd(q_cat, k_cat, v, carry=None):
    t = q_cat.shape[0]
    tq = tk = _att_tile(t)
    nq = t // tq
    npair, nh = FWD_PAIRS, 2 * FWD_PAIRS
    n_grp = MLA_HEADS // nh
    nx = carry.k if carry else 0

    def body(*refs):
        q_ref, k_ref, v_ref = refs[:3]
        o_ref, lse_ref = refs[3 + nx:5 + nx]
        qi = pl.program_id(1)
        if carry:
            start, middle, finish = carry.phases(refs[3:3 + nx], refs[5 + nx:5 + 2 * nx], refs[5 + 2 * nx:])
            grp = pl.program_id(0)
            pl.when((grp == 0) & (qi == 0))(start)
            pl.when((grp == CARRY_MIDDLE_PAIR // npair) & (qi == 0))(middle)
        lo_q, _ = _half_masks(tq)
        halves = _half_masks(tk)

        def step(ki, state, masked):
            m_old, l_old, accs = state[0:nh], state[nh:2 * nh], state[2 * nh:]
            rows = _ds(ki, tk)
            ss = [lax.dot_general(q_ref[:, h * LANE:(h + 1) * LANE], k_ref[rows, h * LANE:(h + 1) * LANE], NT_DIMS,
                                  preferred_element_type=F32) for h in range(nh)]
            if masked:
                valid = _att_mask(qi, ki, tq, tk)
                ss = [jnp.where(valid, s, NEG) for s in ss]
            m_new = [jnp.maximum(m_old[h], jnp.max(ss[h], axis=-1, keepdims=True)) for h in range(nh)]
            ps = [jnp.exp2(ss[h] - m_new[h]) for h in range(nh)]
            alpha = [jnp.exp2(m_old[h] - m_new[h]) for h in range(nh)]
            l_new = [alpha[h] * l_old[h] + jnp.sum(ps[h], axis=-1, keepdims=True) for h in range(nh)]
            new_accs = []
            for pp in range(npair):
                vv = v_ref[rows, pp * LANE:(pp + 1) * LANE]
                pv = [jnp.dot(ps[2 * pp + hh].astype(BF), jnp.where(halves[hh], vv, jnp.zeros_like(vv)),
                              preferred_element_type=F32) for hh in range(2)]
                new_accs.append(accs[pp] * jnp.where(lo_q, alpha[2 * pp], alpha[2 * pp + 1]) + pv[0] + pv[1])
            return tuple(m_new) + tuple(l_new) + tuple(new_accs)

        neg, zero = jnp.full((tq, 1), NEG, F32), jnp.zeros((tq, 1), F32)
        state = step(0, (neg,) * nh + (zero,) * nh + (jnp.zeros((tq, LANE), F32),) * npair, True)
        state = lax.fori_loop(1, qi, lambda ki, st: step(ki, st, False), state)
        state = lax.cond(qi > 0, lambda st: step(qi, st, True), lambda st: st, state)
        for pp in range(npair):
            l = jnp.where(lo_q, state[nh + 2 * pp], state[nh + 2 * pp + 1])
            o_ref[:, pp * LANE:(pp + 1) * LANE] = (state[2 * nh + pp] / l).astype(o_ref.dtype)
            lse_ref[:, pp * LANE:(pp + 1) * LANE] = jnp.where(lo_q, state[2 * pp], state[2 * pp + 1]) + jnp.log2(l)
        if carry:
            pl.when((grp == n_grp - 1) & (qi == nq - 1))(finish)

    outs = pl.pallas_call(
        body,
        out_shape=(_sds((t, 1024), BF), _sds((t, 1024), F32)) + tuple(carry.out_shapes() if carry else ()),
        grid=(n_grp, nq),
        in_specs=[pl.BlockSpec((tq, nh * LANE), lambda g, qi: (qi, g)),
                  pl.BlockSpec((t, nh * LANE), lambda g, qi: (0, g)),
                  pl.BlockSpec((t, npair * LANE), lambda g, qi: (0, g))] + [ANY] * nx,
        out_specs=(pl.BlockSpec((tq, npair * LANE), lambda g, qi: (qi, g)),
                   pl.BlockSpec((tq, npair * LANE), lambda g, qi: (qi, g))) + (ANY,) * nx,
        scratch_shapes=carry.scratch() if carry else [],
        compiler_params=_cp(("arbitrary", "arbitrary") if carry else ("parallel", "parallel")),
        name="attn_fwd_carrying" if carry else "attn_fwd",
    )(q_cat, k_cat, v, *(carry.arrs if carry else ()))
    return outs[0], outs[1], list(outs[2:])


def attn_bwd(q_cat, k_cat, v, o, lse, dyab, carry=None):
    t = q_cat.shape[0]
    tq = tk = _att_tile(t)
    nq = t // tq
    n_pair = MLA_HEADS // 2
    nx = carry.k if carry else 0

    def body(*refs):
        q_ref, k_ref, v_ref, o_ref, lse_ref, do_ref = refs[:6]
        dq_ref, dk_ref, dv_ref = refs[6 + nx:9 + nx]
        ki = pl.program_id(1)
        if carry:
            start, middle, finish = carry.phases(refs[6:6 + nx], refs[9 + nx:9 + 2 * nx], refs[9 + 2 * nx:])
            pair = pl.program_id(0)
            pl.when((pair == 0) & (ki == 0))(start)
            pl.when((pair == CARRY_MIDDLE_PAIR) & (ki == 0))(middle)

        @pl.when(ki == 0)
        def _():
            dq_ref[...] = jnp.zeros_like(dq_ref)

        halves = _half_masks(tq)
        vv = v_ref[...]
        kk = [k_ref[:, hh * LANE:(hh + 1) * LANE] for hh in range(2)]

        def step(qi, acc, masked):
            rows = _ds(qi, tq)
            dov, ov, lse_v = do_ref[rows, :], o_ref[rows, :].astype(F32), lse_ref[rows, :]
            qh = [q_ref[rows, hh * LANE:(hh + 1) * LANE] for hh in range(2)]
            ss = [lax.dot_general(qh[hh], kk[hh], NT_DIMS, preferred_element_type=F32) for hh in range(2)]
            if masked:
                valid = _att_mask(qi, ki, tq, tk)
                ss = [jnp.where(valid, s, NEG) for s in ss]
            ps = [jnp.exp2(ss[hh] - lse_v[:, 64 * hh:64 * hh + 1]) for hh in range(2)]
            dom = [jnp.where(halves[hh], dov, 0.0) for hh in range(2)]
            delta = [jnp.sum(dom[hh] * ov, axis=-1, keepdims=True) for hh in range(2)]
            dom = [d.astype(BF) for d in dom]
            dp = [lax.dot_general(dom[hh], vv, NT_DIMS, preferred_element_type=F32) for hh in range(2)]
            ds = [(ps[hh] * (dp[hh] - delta[hh])).astype(BF) for hh in range(2)]
            pb = [p.astype(BF) for p in ps]
            dv = (acc[2] + lax.dot_general(pb[0], dom[0], TN_DIMS, preferred_element_type=F32)
                  + lax.dot_general(pb[1], dom[1], TN_DIMS, preferred_element_type=F32))
            dk = [acc[hh] + lax.dot_general(ds[hh], qh[hh], TN_DIMS, preferred_element_type=F32) for hh in range(2)]
            for hh in range(2):
                dq_ref[rows, hh * LANE:(hh + 1) * LANE] += jnp.dot(ds[hh], kk[hh], preferred_element_type=F32)
            return dk[0], dk[1], dv

        zero = jnp.zeros((tk, LANE), F32)
        acc = step(ki, (zero, zero, zero), True)
        acc = lax.fori_loop(ki + 1, jnp.where(ki == 0, nq, ki + 1), lambda qi, a: step(qi, a, True), acc)
        acc = lax.fori_loop(ki + 1, jnp.where(ki == 0, ki + 1, nq), lambda qi, a: step(qi, a, False), acc)
        dk_ref[:, 0:LANE] = acc[0] * (1.0 / LOG2E)
        dk_ref[:, LANE:2 * LANE] = acc[1] * (1.0 / LOG2E)
        dv_ref[...] = acc[2]
        if carry:
            pl.when((pair == n_pair - 1) & (ki == nq - 1))(finish)

    full = lambda w, off=0: pl.BlockSpec((t, w), lambda p, ki: (0, p + off))
    blk = lambda w: pl.BlockSpec((tk, w), lambda p, ki: (ki, p))
    outs = pl.pallas_call(
        body,
        out_shape=(_sds((t, 2048), F32), _sds((t, 2048), F32), _sds((t, 1024), F32))
        + tuple(carry.out_shapes() if carry else ()),
        grid=(n_pair, nq),
        in_specs=[full(2 * LANE), blk(2 * LANE), blk(LANE), full(LANE), full(LANE), full(LANE, 8)] + [ANY] * nx,
        out_specs=(full(2 * LANE), blk(2 * LANE), blk(LANE)) + (ANY,) * nx,
        scratch_shapes=carry.scratch() if carry else [],
        compiler_params=_cp(("arbitrary", "arbitrary") if carry else ("parallel", "arbitrary")),
        name="attn_bwd_carrying" if carry else "attn_bwd",
    )(q_cat, k_cat, v, o, lse, dyab, *(carry.arrs if carry else ()))
    return outs[0], outs[1], outs[2], list(outs[3:])


N_PAIR = SSD_HEADS // 2


def _hdot(a, b):
    return jnp.dot(a, b, precision=HI, preferred_element_type=F32)


def _ssd_chunk(xs, bg, cg, dtraw, hin, dt_bias, a_log, dskip, rowmask):
    ln = CHUNK
    causal = lax.broadcasted_iota(jnp.int32, (ln, ln), 0) >= lax.broadcasted_iota(jnp.int32, (ln, ln), 1)
    ltri = causal.astype(F32)
    lane = lax.broadcasted_iota(jnp.int32, (ln, LANE), 1)
    halves = (lane < 64, lane >= 64)
    low_row = lax.broadcasted_iota(jnp.int32, (1, LANE), 1) < 64
    head_lane = lax.broadcasted_iota(jnp.int32, (1, SSD_HEADS), 1)
    head_row = lax.broadcasted_iota(jnp.int32, (SSD_HEADS, 1), 0)

    def col(a, h):
        return jnp.sum(jnp.where(head_lane == h, a, 0.0), axis=1, keepdims=True)

    dt = _softplus(dtraw + dt_bias) * rowmask
    da = dt * (-jnp.exp(a_log))
    acs = _hdot(ltri, da)
    acs_t = lax.dot_general(da, ltri, (((0,), (1,)), ((), ())), precision=HI, preferred_element_type=F32)
    tot = jnp.sum(da, axis=0, keepdims=True)
    bm = [b * rowmask for b in bg]
    cm = [c * rowmask for c in cg]
    cb = [lax.dot_general(cm[g].astype(BF), bm[g].astype(BF), NT_DIMS, preferred_element_type=F32) for g in range(2)]
    ys, hout = [], []
    for p in range(N_PAIR):
        g = p // (N_PAIR // 2)
        h0, h1 = 2 * p, 2 * p + 1
        xdt = xs[p] * jnp.where(halves[0], col(dt, h0), col(dt, h1))
        y = jnp.zeros((ln, LANE), F32)
        snew = jnp.zeros((ln, LANE), F32)
        for hh in range(2):
            a_col = col(acs, h0 + hh)
            a_row = jnp.sum(jnp.where(head_row == h0 + hh, acs_t, 0.0), axis=0, keepdims=True)
            dec = jnp.exp(jnp.where(causal, a_col - a_row, NEG))
            xm = jnp.where(halves[hh], xdt, 0.0).astype(BF)
            y = y + jnp.dot((cb[g] * dec).astype(BF), xm, preferred_element_type=F32)
            bd = bm[g] * jnp.exp(col(tot, h0 + hh) - a_col)
            snew = snew + lax.dot_general(bd.astype(BF), xm, TN_DIMS, preferred_element_type=F32)
        y_off = (jnp.dot(cm[g].astype(BF), hin[p].astype(BF), preferred_element_type=F32)
                 * jnp.exp(jnp.where(halves[0], col(acs, h0), col(acs, h1))))
        ys.append(y + y_off + jnp.where(low_row, col(dskip, h0), col(dskip, h1)) * xs[p])
        hout.append(jnp.exp(jnp.where(low_row, col(tot, h0), col(tot, h1))) * hin[p] + snew)
    return ys, hout


def _ssd_load(x_ref, dt_ref):
    xs = [x_ref[:, p * LANE:(p + 1) * LANE] for p in range(N_PAIR)]
    bg = [x_ref[:, SSD_D_INNER + g * LANE:SSD_D_INNER + (g + 1) * LANE] for g in range(2)]
    cg = [x_ref[:, SSD_D_INNER + (2 + g) * LANE:SSD_D_INNER + (3 + g) * LANE] for g in range(2)]
    return xs, bg, cg, dt_ref[:, 0:SSD_HEADS]


def _chunk_rowmask(c):
    return ((c * CHUNK + lax.broadcasted_iota(jnp.int32, (CHUNK, 1), 0)) >= PAD).astype(F32)


def ssd_fwd(xbc_c, proj, dt_bias, a_log, dskip):
    t = xbc_c.shape[0]
    nc = t // CHUNK

    def body(x_ref, dt_ref, dtb_ref, al_ref, d_ref, y_ref, hs_ref, h_s):
        c = pl.program_id(0)

        @pl.when(c == 0)
        def _():
            h_s[...] = jnp.zeros_like(h_s)

        xs, bg, cg, dtraw = _ssd_load(x_ref, dt_ref)
        hin = [h_s[p] for p in range(N_PAIR)]
        hs_ref[0] = h_s[...]
        ys, hout = _ssd_chunk(xs, bg, cg, dtraw, hin, dtb_ref[...], al_ref[...], d_ref[...], _chunk_rowmask(c))
        for p in range(N_PAIR):
            y_ref[:, p * LANE:(p + 1) * LANE] = ys[p]
            h_s[p] = hout[p]

    par = pl.BlockSpec((1, SSD_HEADS), lambda c: (0, 0))
    return pl.pallas_call(
        body,
        out_shape=(_sds((t, SSD_D_INNER), F32), _sds((nc, N_PAIR, CHUNK, LANE), F32)),
        grid=(nc,),
        in_specs=[pl.BlockSpec((CHUNK, SSD_CONV_CH), lambda c: (c, 0)),
                  pl.BlockSpec((CHUNK, LANE), lambda c: (c, OFF_DT // LANE)), par, par, par],
        out_specs=(pl.BlockSpec((CHUNK, SSD_D_INNER), lambda c: (c, 0)),
                   pl.BlockSpec((1, N_PAIR, CHUNK, LANE), lambda c: (c, 0, 0, 0))),
        scratch_shapes=[pltpu.VMEM((N_PAIR, CHUNK, LANE), F32)],
        compiler_params=_cp(("arbitrary",)),
        name="ssd_fwd",
    )(xbc_c, proj, dt_bias.reshape(1, -1), a_log.reshape(1, -1), dskip.reshape(1, -1))


def ssd_bwd(xbc_c, proj, dt_bias, a_log, dskip, hs, dy):
    t = xbc_c.shape[0]
    nc = t // CHUNK

    def body(x_ref, dt_ref, dtb_ref, al_ref, d_ref, hs_ref, dy_ref, dx_ref, ddt_ref, dpar_ref, dh_s):
        ci = pl.program_id(0)
        c = nc - 1 - ci

        @pl.when(ci == 0)
        def _():
            dh_s[...] = jnp.zeros_like(dh_s)
            dpar_ref[...] = jnp.zeros_like(dpar_ref)

        xs, bg, cg, dtraw = _ssd_load(x_ref, dt_ref)
        hin = [hs_ref[0, p] for p in range(N_PAIR)]
        rowmask = _chunk_rowmask(c)
        fn = lambda xs_, bg_, cg_, dtraw_, hin_, dtb_, al_, d_: _ssd_chunk(xs_, bg_, cg_, dtraw_, hin_, dtb_, al_, d_, rowmask)
        _, vjp = jax.vjp(fn, xs, bg, cg, dtraw, hin, dtb_ref[...], al_ref[...], d_ref[...])
        dys = [dy_ref[:, p * LANE:(p + 1) * LANE] for p in range(N_PAIR)]
        dhs = [dh_s[p] for p in range(N_PAIR)]
        dxs, dbg, dcg, ddtraw, dhin, ddtb, dal, dd = vjp((dys, dhs))
        for p in range(N_PAIR):
            dx_ref[:, p * LANE:(p + 1) * LANE] = dxs[p]
            dh_s[p] = dhin[p]
        for g in range(2):
            dx_ref[:, SSD_D_INNER + g * LANE:SSD_D_INNER + (g + 1) * LANE] = dbg[g]
            dx_ref[:, SSD_D_INNER + (2 + g) * LANE:SSD_D_INNER + (3 + g) * LANE] = dcg[g]
        ddt_ref[...] = jnp.zeros_like(ddt_ref)
        ddt_ref[:, 0:SSD_HEADS] = ddtraw
        dpar_ref[0:1, 0:SSD_HEADS] += ddtb
        dpar_ref[1:2, 0:SSD_HEADS] += dal
        dpar_ref[2:3, 0:SSD_HEADS] += dd

    par = pl.BlockSpec((1, SSD_HEADS), lambda ci: (0, 0))
    return pl.pallas_call(
        body,
        out_shape=(_sds((t, SSD_CONV_CH), F32), _sds((t, LANE), F32), _sds((8, LANE), F32)),
        grid=(nc,),
        in_specs=[pl.BlockSpec((CHUNK, SSD_CONV_CH), lambda ci: (nc - 1 - ci, 0)),
                  pl.BlockSpec((CHUNK, LANE), lambda ci: (nc - 1 - ci, OFF_DT // LANE)), par, par, par,
                  pl.BlockSpec((1, N_PAIR, CHUNK, LANE), lambda ci: (nc - 1 - ci, 0, 0, 0)),
                  pl.BlockSpec((CHUNK, SSD_D_INNER), lambda ci: (nc - 1 - ci, 0))],
        out_specs=(pl.BlockSpec((CHUNK, SSD_CONV_CH), lambda ci: (nc - 1 - ci, 0)),
                   pl.BlockSpec((CHUNK, LANE), lambda ci: (nc - 1 - ci, 0)),
                   pl.BlockSpec((8, LANE), lambda ci: (0, 0))),
        scratch_shapes=[pltpu.VMEM((N_PAIR, CHUNK, LANE), F32)],
        compiler_params=_cp(("arbitrary",)),
        name="ssd_bwd",
    )(xbc_c, proj, dt_bias.reshape(1, -1), a_log.reshape(1, -1), dskip.reshape(1, -1), hs, dy)


def _neg_expm1(y):
    series = -(y * (1.0 + y * (0.5 + y * (1.0 / 6.0 + y * (1.0 / 24.0 + y * (1.0 / 120.0))))))
    return jnp.where(y > -0.1, series, 1.0 - jnp.exp(y))


def _rg_pw(xr, wa, ba, wi, bi, lam, rowmask):
    xb = xr.astype(BF)
    r = jax.nn.sigmoid(jnp.dot(xb, wa.astype(BF), preferred_element_type=F32) + ba)
    i = jax.nn.sigmoid(jnp.dot(xb, wi.astype(BF), preferred_element_type=F32) + bi)
    log_a = -LRU_C * r * _softplus(-lam)
    a = jnp.exp(log_a)
    u = jnp.sqrt(_neg_expm1(2.0 * log_a)) * (i * xr) * rowmask
    return a, u


def _gelu_grad(x):
    c = math.sqrt(2.0 / math.pi)
    th = jnp.tanh(c * (x + 0.044715 * (x * x * x)))
    return 0.5 * (1.0 + th) + 0.5 * x * (1.0 - th * th) * c * (1.0 + 3.0 * 0.044715 * x * x)


def _scan_fwd(a, u):
    n = a.shape[0]
    row = lax.broadcasted_iota(jnp.int32, a.shape, 0)
    s = 1
    while s < n:
        a_s = jnp.where(row >= s, pltpu.roll(a, s, 0), 1.0)
        u_s = jnp.where(row >= s, pltpu.roll(u, s, 0), 0.0)
        u = u + a * u_s
        a = a * a_s
        s *= 2
    return a, u


def _scan_bwd(b, d):
    n = b.shape[0]
    row = lax.broadcasted_iota(jnp.int32, b.shape, 0)
    s = 1
    while s < n:
        b_s = jnp.where(row < n - s, pltpu.roll(b, n - s, 0), 1.0)
        d_s = jnp.where(row < n - s, pltpu.roll(d, n - s, 0), 0.0)
        d = d + b * d_s
        b = b * b_s
        s *= 2
    return d


def rg_fwd(xr_pre, gate_pre, rgp, w_a, w_i):
    t = xr_pre.shape[0]
    tr = _rt(t)

    def body(x_ref, g_ref, p_ref, wa_ref, wi_ref, hg_ref, hs_ref, prev, hcar):
        ti = pl.program_id(1)

        @pl.when(ti == 0)
        def _():
            prev[...] = jnp.zeros_like(prev)
            hcar[...] = jnp.zeros_like(hcar)

        xv = x_ref[...]
        ext = jnp.concatenate([prev[...], xv], axis=0)
        xr = _conv_pre(ext, xv, p_ref, tr)
        rowmask = _row_mask(ti, tr, (tr, 1)).astype(F32)
        a, u = _rg_pw(xr, wa_ref[0], p_ref[5:6, :], wi_ref[0], p_ref[6:7, :], p_ref[7:8, :], rowmask)
        a_cum, h_loc = _scan_fwd(a, u)
        hs = h_loc + a_cum * hcar[0:1, :]
        hs_ref[...] = hs
        hg_ref[...] = (hs * _gelu(g_ref[...])).astype(hg_ref.dtype)
        hcar[...] = jnp.broadcast_to(hs[tr - 1:tr, :], (8, LANE))
        prev[...] = xv[tr - 8:, :]

    return pl.pallas_call(
        body,
        out_shape=(_sds((t, LRU_WIDTH), BF), _sds((t, LRU_WIDTH), F32)),
        grid=(LRU_BLOCKS, t // tr),
        in_specs=[pl.BlockSpec((tr, LANE), lambda n, ti: (ti, n)),
                  pl.BlockSpec((tr, LANE), lambda n, ti: (ti, n)),
                  pl.BlockSpec((8, LANE), lambda n, ti: (0, n)),
                  pl.BlockSpec((1, LANE, LANE), lambda n, ti: (n, 0, 0)),
                  pl.BlockSpec((1, LANE, LANE), lambda n, ti: (n, 0, 0))],
        out_specs=(pl.BlockSpec((tr, LANE), lambda n, ti: (ti, n)), pl.BlockSpec((tr, LANE), lambda n, ti: (ti, n))),
        scratch_shapes=[pltpu.VMEM((8, LANE), F32), pltpu.VMEM((8, LANE), F32)],
        compiler_params=_cp(("parallel", "arbitrary")),
        name="rg_fwd",
    )(xr_pre, gate_pre, rgp, w_a, w_i)


def rg_bwd(xr_pre, gate_pre, rgp, w_a, w_i, hs, dhg):
    t = xr_pre.shape[0]
    tr = _rt(t)
    nt = t // tr
    r8 = tr // 8

    def body(x_ref, xp_ref, g_ref, p_ref, wa_ref, wi_ref, hs_ref, hp_ref, dhg_ref,
             dx_ref, dg_ref, dp_ref, dwa_ref, dwi_ref, gcar, dnext):
        ti = pl.program_id(1)
        tt = nt - 1 - ti

        @pl.when(ti == 0)
        def _():
            gcar[...] = jnp.zeros_like(gcar)
            dnext[...] = jnp.zeros_like(dnext)
            dp_ref[...] = jnp.zeros_like(dp_ref)
            dwa_ref[...] = jnp.zeros_like(dwa_ref)
            dwi_ref[...] = jnp.zeros_like(dwi_ref)

        xv = x_ref[...]
        halo = jnp.where(tt > 0, xp_ref[...], 0.0)
        ext = jnp.concatenate([halo, xv], axis=0)
        xr = _conv_pre(ext, xv, p_ref, tr)
        rowmask = _row_mask(tt, tr, (tr, 1)).astype(F32)
        fn = lambda xr_, wa_, ba_, wi_, bi_, lam_: _rg_pw(xr_, wa_, ba_, wi_, bi_, lam_, rowmask)
        (a, _), vjp = jax.vjp(fn, xr, wa_ref[0], p_ref[5:6, :], wi_ref[0], p_ref[6:7, :], p_ref[7:8, :])
        gpre = g_ref[...]
        hsv = hs_ref[...]
        dhg_v = dhg_ref[...]
        dg_ref[...] = (dhg_v * hsv * _gelu_grad(gpre)).astype(dg_ref.dtype)
        row = lax.broadcasted_iota(jnp.int32, (tr, LANE), 0)
        d = dhg_v * _gelu(gpre) + jnp.where(row == tr - 1, gcar[0:1, :], 0.0)
        b = jnp.where(row < tr - 1, pltpu.roll(a, tr - 1, 0), 0.0)
        g = _scan_bwd(b, d)
        gcar[...] = jnp.broadcast_to(a[0:1, :] * g[0:1, :], (8, LANE))
        hlast = jnp.where(tt > 0, hp_ref[7:8, :], 0.0)
        hprev = jnp.where(row == 0, hlast, pltpu.roll(hsv, 1, 0))
        dxr, dwa, dba, dwi, dbi, dlam = vjp((g * hprev, g))
        dx, sums = _conv_bwd_parts(dxr, dnext[...], xv, ext, p_ref, tr)
        dx_ref[...] = dx.astype(dx_ref.dtype)
        dnext[...] = dxr[:8, :]
        dp_ref[...] += _rows_block(sums + [dba, dbi, dlam])
        dwa_ref[0] += dwa
        dwi_ref[0] += dwi

    tile = lambda off=0: pl.BlockSpec((tr, LANE), lambda n, ti: (nt - 1 - ti, off + n))
    halo = lambda off=0: pl.BlockSpec((8, LANE), lambda n, ti: (jnp.maximum((nt - 1 - ti) * r8 - 1, 0), off + n))
    par = pl.BlockSpec((8, LANE), lambda n, ti: (0, n))
    wspec = pl.BlockSpec((1, LANE, LANE), lambda n, ti: (n, 0, 0))
    return pl.pallas_call(
        body,
        out_shape=(_sds((t, LRU_WIDTH), BF), _sds((t, LRU_WIDTH), BF), _sds((8, LRU_WIDTH), F32),
                   _sds((LRU_BLOCKS, LANE, LANE), F32), _sds((LRU_BLOCKS, LANE, LANE), F32)),
        grid=(LRU_BLOCKS, nt),
        in_specs=[tile(), halo(), tile(), par, wspec, wspec, tile(), halo(), tile()],
        out_specs=(tile(), tile(), par, wspec, wspec),
        scratch_shapes=[pltpu.VMEM((8, LANE), F32), pltpu.VMEM((8, LANE), F32)],
        compiler_params=_cp(("parallel", "arbitrary")),
        name="rg_bwd",
    )(xr_pre, xr_pre, gate_pre, rgp, w_a, w_i, hs, hs, dhg)


PACK_W = 1024
MESH_ID = pl.DeviceIdType.MESH
ANY = pl.BlockSpec(memory_space=pl.ANY)


def _my_place():
    x, y, c = lax.axis_index("x"), lax.axis_index("y"), lax.axis_index("c")
    return x, y, c


def _lin(px, py, pc):
    return 4 * px + 2 * py + pc


class Exchange:
    def __init__(self, kind, arrs):
        self.kind, self.arrs, self.k = kind, list(arrs), len(arrs)

    def out_shapes(self):
        if self.kind == "gather":
            return [_sds((N_DEV,) + a.shape, a.dtype) for a in self.arrs]
        return [_sds(a.shape, a.dtype) for a in self.arrs]

    def scratch(self):
        k = self.k
        return [pltpu.SemaphoreType.DMA((k, 7)), pltpu.SemaphoreType.DMA((k, 7)), pltpu.SemaphoreType.DMA((k,))]

    def phases(self, ins, outs, sems):
        return (self._gather if self.kind == "gather" else self._scatter)(ins, outs, *sems)

    def _gather(self, ins, outs, send_sems, recv_sems, local_sems):
        k = self.k
        x, y, c = _my_place()
        me, sibling = (x, y, c), (x, y, 1 - c)
        chips = [(1 - x, y), (x, 1 - y), (1 - x, 1 - y)]

        def copy(a, sem, block, to, from_input=False):
            slab = outs[a].at[_lin(*block)]
            return pltpu.make_async_remote_copy(
                src_ref=ins[a] if from_input else slab, dst_ref=slab,
                send_sem=send_sems.at[a, sem], recv_sem=recv_sems.at[a, sem],
                device_id=to, device_id_type=MESH_ID)

        def mine():
            return [pltpu.make_async_copy(ins[a], outs[a].at[_lin(*me)], local_sems.at[a]) for a in range(k)]

        def first():
            out = []
            for a in range(k):
                out.append(copy(a, 0, me, sibling, True))
                out += [copy(a, 1 + j, me, (*chip, c), True) for j, chip in enumerate(chips)]
            return out

        def passed():
            return [copy(a, 4 + j, (*chip, c), sibling) for j, chip in enumerate(chips) for a in range(k)]

        def start():
            for cp in mine() + first():
                cp.start()

        def middle():
            onward = passed()
            for j, chip in enumerate(chips):
                for a in range(k):
                    copy(a, 1 + j, (*chip, c), me).wait_recv()
                    onward[j * k + a].start()

        def finish():
            for a in range(k):
                copy(a, 0, sibling, me).wait_recv()
                for j, chip in enumerate(chips):
                    copy(a, 4 + j, (*chip, 1 - c), me).wait_recv()
            for cp in first() + passed():
                cp.wait_send()
            for cp in mine():
                cp.wait()

        return start, middle, finish

    def _scatter(self, ins, outs, send_sems, recv_sems, local_sems):
        k = self.k
        x, y, c = _my_place()
        me = _lin(x, y, c)
        peers = [((1 - x) if r & 4 else x, (1 - y) if r & 2 else y, (1 - c) if r & 1 else c) for r in range(1, N_DEV)]

        def copy(a, r, src_slab, dst_slab, to):
            return pltpu.make_async_remote_copy(
                src_ref=ins[a].at[src_slab], dst_ref=outs[a].at[dst_slab],
                send_sem=send_sems.at[a, r], recv_sem=recv_sems.at[a, r],
                device_id=to, device_id_type=MESH_ID)

        def mine():
            return [pltpu.make_async_copy(ins[a].at[me], outs[a].at[me], local_sems.at[a]) for a in range(k)]

        def sends():
            return [copy(a, r, _lin(*peer), me, peer) for r, peer in enumerate(peers) for a in range(k)]

        def start():
            for cp in mine() + sends():
                cp.start()

        def middle():
            pass

        def finish():
            for r, peer in enumerate(peers):
                for a in range(k):
                    copy(a, r, me, _lin(*peer), peer).wait_recv()
            for cp in sends():
                cp.wait_send()
            for cp in mine():
                cp.wait()

        return start, middle, finish

    def run(self, name):
        k = self.k

        def body(*refs):
            start, middle, finish = self.phases(refs[:k], refs[k:2 * k], refs[2 * k:])
            start()
            middle()
            finish()

        return pl.pallas_call(
            body,
            out_shape=tuple(self.out_shapes()),
            in_specs=[ANY] * k,
            out_specs=tuple(ANY for _ in range(k)),
            scratch_shapes=self.scratch(),
            name=name,
        )(*self.arrs)


def all_gather(arrs, name):
    return Exchange("gather", arrs).run(name)


def all_to_all(arrs, name):
    return Exchange("scatter", arrs).run(name)


def slab_sum(a, name):
    _, r, w = a.shape
    tr = _pick(r, (256, 128, 64, 32, 16, 8))

    def body(a_ref, o_ref):
        acc = a_ref[0].astype(F32)
        for d in range(1, N_DEV):
            acc = acc + a_ref[d].astype(F32)
        o_ref[...] = acc

    return pl.pallas_call(
        body,
        out_shape=_sds((r, w), F32),
        grid=(r // tr,),
        in_specs=[pl.BlockSpec((N_DEV, tr, w), lambda i: (0, i, 0))],
        out_specs=pl.BlockSpec((tr, w), lambda i: (i, 0)),
        compiler_params=_cp(("parallel",)),
        name=name,
    )(a)


def _adam_update(w, g, m, v):
    nm = ADAM_B1 * m + (1.0 - ADAM_B1) * g
    nv = ADAM_B2 * v + (1.0 - ADAM_B2) * (g * g)
    m_hat = nm / (1.0 - ADAM_B1 ** ADAM_STEP)
    v_hat = nv / (1.0 - ADAM_B2 ** ADAM_STEP)
    return -ADAM_LR * (m_hat / (jnp.sqrt(v_hat) + ADAM_EPS) + ADAM_WD * w), nm, nv


def adamw_blocks(w, m, v, parts, name):
    nl, r, c = w.shape
    tr = next(t for t in (256, 160, 128, 64, 32, 16) if r % t == 0 and N_DEV * t * c * 2 <= 2 * 1024 * 1024)

    def body(w_ref, m_ref, v_ref, *rest):
        part_refs, (g_ref, d_ref, nm_ref, nv_ref) = rest[:nl], rest[nl:]
        layer = pl.program_id(0)
        for idx in range(nl):
            @pl.when(layer == idx)
            def _(idx=idx):
                g = part_refs[idx][0].astype(F32)
                for dev in range(1, N_DEV):
                    g = g + part_refs[idx][dev].astype(F32)
                g_ref[...] = g
                d_ref[...], nm_ref[...], nv_ref[...] = _adam_update(w_ref[...], g, m_ref[...], v_ref[...])

    spec = pl.BlockSpec((None, tr, c), lambda l, i: (l, i, 0))
    part_spec = lambda idx: pl.BlockSpec((N_DEV, tr, c), lambda l, i: (0, jnp.where(l == idx, i, 0), 0))
    return pl.pallas_call(
        body,
        out_shape=tuple(_sds((nl, r, c), F32) for _ in range(4)),
        grid=(nl, r // tr),
        in_specs=[spec] * 3 + [part_spec(idx) for idx in range(nl)],
        out_specs=(spec,) * 4,
        compiler_params=_cp(("arbitrary", "arbitrary")),
        name=name,
    )(w, m, v, *parts)


def adamw(w, g, m, v, name):
    r, c = w.shape
    tr = _pick(r, (256, 160, 128, 64, 32, 16, 8))

    def body(w_ref, g_ref, m_ref, v_ref, d_ref, nm_ref, nv_ref):
        d_ref[...], nm_ref[...], nv_ref[...] = _adam_update(w_ref[...], g_ref[...], m_ref[...], v_ref[...])

    spec = pl.BlockSpec((tr, c), lambda i: (i, 0))
    return pl.pallas_call(
        body,
        out_shape=tuple(_sds((r, c), F32) for _ in range(3)),
        grid=(r // tr,),
        in_specs=[spec] * 4,
        out_specs=(spec, spec, spec),
        compiler_params=_cp(("parallel",)),
        name=name,
    )(w, g, m, v)


def _relu2_epi(acc):
    r = jnp.maximum(acc, 0.0)
    return r * r, r


def _drelu2_epi(acc, r):
    return (acc * (2.0 * r.astype(F32)),)


def mlp_fwd(h, g_pre, g_post, w_up, w_down):
    hn = norm_fwd(h, g_pre, BF, name="mlp_norm")
    u, r = matmul(hn, w_up, "nn", (BF, BF), epi=_relu2_epi, name="mlp_up")
    d = matmul(u, w_down, "nn", name="mlp_down")
    return resadd_fwd(h, d, g_post, name="mlp_res"), (h, hn, u, r, d)


def mlp_bwd(res, dh2, g_pre, g_post, w_up, w_down):
    h, hn, u, r, d = res
    dd, dg_post = norm_bwd(d, g_post, dh2, mask_pad=True, out_dtype=BF, name="mlp_post_bwd")
    dw_down = matmul(u, dd, "tn", (BF,), name="mlp_dwdown").reshape(w_down.g8.shape)
    dp = matmul(dd, w_down, "nt", (BF,), epi=_drelu2_epi, extras=(r,), name="mlp_du")
    dw_up = matmul(hn, dp, "tn", (BF,), out_blocks=True, name="mlp_dwup")
    dhn = matmul(dp, w_up, "nt", name="mlp_dhn")
    dh, dg_pre = norm_bwd(h, g_pre, dhn, dres=dh2, name="mlp_pre_bwd")
    return dh, dict(mlp_pre_g=dg_pre, mlp_post_g=dg_post, w_up=dw_up, w_down=dw_down)


def rg_layer_fwd(h, g_pre, g_post, w_x, w_y, rgp, w_a, w_i, w_out):
    hn = norm_fwd(h, g_pre, BF, name="rg_norm")
    xr = matmul(hn, w_x, "nn", name="rg_in_x")
    gp = matmul(hn, w_y, "nn", name="rg_in_y")
    hg, hs = rg_fwd(xr, gp, rgp, w_a, w_i)
    m = matmul(hg, w_out, "nn", name="rg_out")
    return resadd_fwd(h, m, g_post, name="rg_res"), (h, hn, xr, gp, hg, hs, m)


def rg_layer_bwd(res, dh2, g_pre, g_post, w_x, w_y, rgp, w_a, w_i, w_out):
    h, hn, xr, gp, hg, hs, m = res
    dm, dg_post = norm_bwd(m, g_post, dh2, mask_pad=True, out_dtype=BF, name="rg_post_bwd")
    dw_out = matmul(hg, dm, "tn", name="rg_dwout")
    dhg = matmul(dm, w_out, "nt", name="rg_dhg")
    dxr, dgp, drgp, dwa, dwi = rg_bwd(xr, gp, rgp, w_a, w_i, hs, dhg)
    dw_x = matmul(hn, dxr, "tn", name="rg_dwx")
    dw_y = matmul(hn, dgp, "tn", name="rg_dwy")
    dhn = matmul([dxr, dgp], [w_x, w_y], "nt", name="rg_dhn")
    dh, dg_pre = norm_bwd(h, g_pre, dhn, dres=dh2, name="rg_pre_bwd")
    return dh, dict(mix_pre_g=dg_pre, mix_post_g=dg_post, rg_w_x=dw_x, rg_w_y=dw_y,
                    rg_conv_w=drgp[0:4], rg_conv_b=drgp[4], rg_b_a=drgp[5], rg_b_i=drgp[6], rg_lambda=drgp[7],
                    rg_w_a=dwa, rg_w_i=dwi, rg_w_out=dw_out)


def sm_layer_fwd(h, g_pre, g_post, w_in_p, convp, dt_bias, a_log, dskip, ssd_g, q_g, w_q_p, kv_g, w_kv_p, w_out, tabs,
                 carry=None, on_carried=None):
    hn = norm_fwd(h, g_pre, BF, name="sm_norm")
    proj = matmul(hn, w_in_p, "nn", name="sm_in")
    xbc_c = conv_silu_fwd(proj, OFF_XBC, SSD_CONV_CH, convp, name="ssd_conv")
    y, hst = ssd_fwd(xbc_c, proj, dt_bias, a_log, dskip)
    y_ssd = gated_norm_fwd(y, proj, ssd_g)
    cqn = norm_fwd(proj, q_g, BF, col_blk=OFF_CQ // MLA_Q_RANK, width=MLA_Q_RANK, name="q_norm")
    q_raw = matmul(cqn, w_q_p, "nn", name="q_up")
    ckvn = norm_fwd(proj, kv_g, BF, col_blk=OFF_CKV // MLA_KV_RANK, width=MLA_KV_RANK, name="kv_norm")
    kv_raw = matmul(ckvn, w_kv_p, "nn", name="kv_up")
    q_cat, k_cat, v = rope_fwd(q_raw, kv_raw, proj, tabs)
    o, lse, carried = attn_fwd(q_cat, k_cat, v, carry)
    if on_carried is not None:
        on_carried(carried)
    w_out = w_out()
    half = w_out.shape[0] // 2
    m = matmul([y_ssd, o], [KBlock(w_out, half, 0), KBlock(w_out, half, 1)], "nn", name="sm_out")
    res = (h, hn, proj, xbc_c, y, hst, cqn, ckvn, q_cat, k_cat, v, o, lse, y_ssd, m)
    return resadd_fwd(h, m, g_post, name="sm_res"), res


def sm_layer_bwd(res, dh2, g_pre, g_post, w_in_p, convp, dt_bias, a_log, dskip, ssd_g, q_g, w_q_p, kv_g, w_kv_p, w_out, tabs,
                 carry=None):
    h, hn, proj, xbc_c, y, hst, cqn, ckvn, q_cat, k_cat, v, o, lse, y_ssd, m = res
    w_out = w_out()
    dm, dg_post = norm_bwd(m, g_post, dh2, mask_pad=True, out_dtype=BF, name="sm_post_bwd")
    dw_out = jnp.concatenate([matmul(y_ssd, dm, "tn", name="sm_dwout_ssd"), matmul(o, dm, "tn", name="sm_dwout_att")], axis=0)
    dyab = matmul(dm, w_out, "nt", name="sm_dyab")
    dq_cat, dk_cat, dv, carried = attn_bwd(q_cat, k_cat, v, o, lse, dyab, carry(dw_out) if carry is not None else None)
    dq_raw, dkr = rope_bwd(dq_cat, dk_cat, tabs)
    kw = MLA_HEADS * LANE
    dw_kv_p = jnp.concatenate([matmul(ckvn, dk_cat, "tn", name="kv_dw_k"), matmul(ckvn, dv, "tn", name="kv_dw_v")], axis=1)
    dckvn = matmul([dk_cat, dv], [KBlock(w_kv_p, kw, 0), KBlock(w_kv_p, kw // 2, 2)], "nt", name="kv_dx")
    dckv, dg_kv = norm_bwd(proj, kv_g, dckvn, out_dtype=BF, col_blk=OFF_CKV // MLA_KV_RANK, width=MLA_KV_RANK,
                           name="kv_norm_bwd")
    dw_q_p = matmul(cqn, dq_raw, "tn", name="q_dw")
    dcqn = matmul(dq_raw, w_q_p, "nt", name="q_dx")
    dcq, dg_q = norm_bwd(proj, q_g, dcqn, out_dtype=BF, col_blk=OFF_CQ // MLA_Q_RANK, width=MLA_Q_RANK, name="q_norm_bwd")
    dy, dz, dg_ssd = gated_norm_bwd(y, proj, ssd_g, dyab)
    dxbc_c, ddt, dpar = ssd_bwd(xbc_c, proj, dt_bias, a_log, dskip, hst, dy)
    dxbc, dconvp = conv_silu_bwd(proj, OFF_XBC, SSD_CONV_CH, convp, dxbc_c, name="ssd_conv_bwd")
    pieces = [dz, dxbc, dckv, ddt, dkr, dcq]
    dw_in_p = jnp.concatenate([matmul(hn, pc, "tn", (BF,), name="sm_dwin_%d" % i) for i, pc in enumerate(pieces)], axis=1)
    third = SSD_CONV_CH // 3
    a_terms = [dz] + [KBlock(dxbc, third, i) for i in range(3)] + [dckv, ddt, dkr, dcq]
    b_terms = ([KBlock(w_in_p, SSD_D_INNER, 0)] + [KBlock(w_in_p, third, OFF_XBC // third + i) for i in range(3)]
               + [KBlock(w_in_p, MLA_KV_RANK, OFF_CKV // MLA_KV_RANK), KBlock(w_in_p, LANE, OFF_DT // LANE),
                  KBlock(w_in_p, LANE, OFF_KR // LANE), KBlock(w_in_p, MLA_Q_RANK, OFF_CQ // MLA_Q_RANK)])
    dhn = matmul(a_terms, b_terms, "nt", name="sm_dhn")
    dh, dg_pre = norm_bwd(h, g_pre, dhn, dres=dh2, name="sm_pre_bwd")
    grads = dict(mix_pre_g=dg_pre, mix_post_g=dg_post, w_in=w_in_cols_to_blocks(dw_in_p), ssd_conv_w=dconvp[0:4],
                 ssd_conv_b=dconvp[4], ssd_dt_bias=dpar[0, :SSD_HEADS], ssd_a_log=dpar[1, :SSD_HEADS],
                 ssd_d=dpar[2, :SSD_HEADS], ssd_norm_g=dg_ssd, mla_q_norm_g=dg_q, mla_w_q_up=_unpack_w_q(dw_q_p),
                 mla_kv_norm_g=dg_kv, mla_w_kv_up=_unpack_w_kv(dw_kv_p), w_out_ab=dw_out)
    return dh, grads, carried


W_IN_COLS = 3248
W_IN_SHARD = W_IN_COLS // N_DEV
W_IN_WIRE = 512


def _w_in_tables():
    src = np.full((IN_W,), -1, np.int64)
    src[0:2560] = np.arange(2560)
    src[OFF_CKV:OFF_CKV + 256] = 2960 + np.arange(256)
    src[OFF_DT:OFF_DT + 16] = 2560 + np.arange(16)
    src[OFF_KR + 64:OFF_KR + 96] = 3216 + np.arange(32)
    src[OFF_CQ:OFF_CQ + 384] = 2576 + np.arange(384)
    dev = np.where(src >= 0, src // W_IN_SHARD, -1).astype(np.int32).reshape(1, IN_W)
    col = np.where(src >= 0, src % W_IN_SHARD, 0).astype(np.int32).reshape(1, IN_W)
    return dev, col


W_IN_TILE = 384


def _w_in_devices_of_tile(dev):
    return [sorted(set(dev[0, t * W_IN_TILE:(t + 1) * W_IN_TILE].tolist()) - {-1}) for t in range(IN_W // W_IN_TILE)]


def _any_of(index, values):
    cond = index == values[0]
    for v in values[1:]:
        cond = cond | (index == v)
    return cond


def w_in_blocks_to_cols(g8):
    _, k, wp = g8.shape
    tn = W_IN_TILE
    dev, col = _w_in_tables()
    holders = _w_in_devices_of_tile(dev)

    def body(g_ref, dev_ref, col_ref, o_ref):
        i = pl.program_id(0)
        row = lax.broadcasted_iota(jnp.int32, (wp, tn), 0)
        o_ref[...] = jnp.zeros_like(o_ref)
        for j in range(N_DEV):
            tiles = [t for t, devs in enumerate(holders) if j in devs]
            if tiles:
                @pl.when(_any_of(i, tiles))
                def _(j=j):
                    sel = ((row == col_ref[...]) & (dev_ref[...] == j)).astype(BF)
                    o_ref[...] += jnp.dot(g_ref[j], sel, preferred_element_type=F32).astype(o_ref.dtype)

    dev, col = jnp.asarray(dev), jnp.asarray(col)
    return pl.pallas_call(
        body,
        out_shape=_sds((k, IN_W), BF),
        grid=(IN_W // tn,),
        in_specs=[pl.BlockSpec((N_DEV, k, wp), lambda i: (0, 0, 0)), pl.BlockSpec((1, tn), lambda i: (0, i)),
                  pl.BlockSpec((1, tn), lambda i: (0, i))],
        out_specs=pl.BlockSpec((k, tn), lambda i: (0, i)),
        compiler_params=_cp(("parallel",)),
        name="w_in_cols",
    )(g8, dev, col)


def w_in_cols_to_blocks(dw):
    k = dw.shape[0]
    tn = W_IN_TILE
    dev, col = _w_in_tables()
    holders = _w_in_devices_of_tile(dev)

    def body(dw_ref, dev_ref, col_ref, o_ref):
        j = pl.program_id(0)
        row = lax.broadcasted_iota(jnp.int32, (W_IN_WIRE, tn), 0)
        o_ref[...] = jnp.zeros_like(o_ref)
        for t, devs in enumerate(holders):
            if devs:
                @pl.when(_any_of(j, devs))
                def _(t=t):
                    cols = slice(t * tn, (t + 1) * tn)
                    sel = ((row == col_ref[:, cols]) & (dev_ref[:, cols] == j)).astype(BF)
                    o_ref[0] += lax.dot_general(dw_ref[:, cols], sel, NT_DIMS,
                                                preferred_element_type=F32).astype(o_ref.dtype)

    dev, col = jnp.asarray(dev), jnp.asarray(col)
    return pl.pallas_call(
        body,
        out_shape=_sds((N_DEV, k, W_IN_WIRE), BF),
        grid=(N_DEV,),
        in_specs=[pl.BlockSpec((k, IN_W), lambda j: (0, 0)), pl.BlockSpec((1, IN_W), lambda j: (0, 0)),
                  pl.BlockSpec((1, IN_W), lambda j: (0, 0))],
        out_specs=pl.BlockSpec((1, k, W_IN_WIRE), lambda j: (j, 0, 0)),
        compiler_params=_cp(("parallel",)),
        name="w_in_blocks",
    )(dw, dev, col)


def _pack_w_q(w):
    w3 = w.reshape(w.shape[0], MLA_HEADS, MLA_NOPE + MLA_ROPE)
    return jnp.pad(w3, ((0, 0), (0, 0), (0, LANE - MLA_NOPE - MLA_ROPE))).reshape(w.shape[0], MLA_HEADS * LANE)


def _unpack_w_q(p):
    return p.reshape(p.shape[0], MLA_HEADS, LANE)[:, :, :MLA_NOPE + MLA_ROPE].reshape(p.shape[0], -1)


def _pack_w_kv(w):
    w3 = w.reshape(w.shape[0], MLA_HEADS, MLA_NOPE + MLA_V)
    k = jnp.pad(w3[:, :, :MLA_NOPE], ((0, 0), (0, 0), (0, LANE - MLA_NOPE))).reshape(w.shape[0], MLA_HEADS * LANE)
    return jnp.concatenate([k, w3[:, :, MLA_NOPE:].reshape(w.shape[0], MLA_HEADS * MLA_V)], axis=1)


def _unpack_w_kv(p):
    k = p[:, :MLA_HEADS * LANE].reshape(p.shape[0], MLA_HEADS, LANE)[:, :, :MLA_NOPE]
    v = p[:, MLA_HEADS * LANE:].reshape(p.shape[0], MLA_HEADS, MLA_V)
    return jnp.concatenate([k, v], axis=2).reshape(p.shape[0], -1)


def _rows8(rows, width):
    a = jnp.concatenate([r.reshape(-1, width) for r in rows], axis=0)
    return jnp.pad(a, ((0, 8 - a.shape[0]), (0, 0)))


SLAB_ROWS = 16


def _to_slab(flat_list, lead=()):
    cat = jnp.concatenate(flat_list, axis=-1)
    n = cat.shape[-1]
    unit = SLAB_ROWS * PACK_W
    total = -(-n // unit) * unit
    cat = jnp.pad(cat, [(0, 0)] * len(lead) + [(0, total - n)])
    return cat.reshape(lead + (total // PACK_W, PACK_W))


def _from_flat(flat, shapes):
    out, off = [], 0
    for s in shapes:
        n = int(np.prod(s))
        out.append(flat[off:off + n].reshape(s))
        off += n
    return out


def _gathered_full(g8, axis):
    moved = jnp.moveaxis(g8, 0, axis)
    shp = moved.shape
    return moved.reshape(shp[:axis] + (shp[axis] * shp[axis + 1],) + shp[axis + 2:])


def _per_device(full, axis):
    shp = full.shape
    split = full.reshape(shp[:axis] + (N_DEV, shp[axis] // N_DEV) + shp[axis + 1:])
    return jnp.moveaxis(split, axis, 0)


ARG_NAMES = ['x', 'meta_tokens', 'mix_pre_g', 'mix_post_g', 'mlp_pre_g', 'mlp_post_g', 'w_up', 'w_down', 'w_in',
             'ssd_conv_w', 'ssd_conv_b', 'ssd_dt_bias', 'ssd_a_log', 'ssd_d', 'ssd_norm_g', 'mla_q_norm_g',
             'mla_w_q_up', 'mla_kv_norm_g', 'mla_w_kv_up', 'w_out_ab', 'rg_w_x', 'rg_w_y', 'rg_conv_w', 'rg_conv_b',
             'rg_w_a', 'rg_b_a', 'rg_w_i', 'rg_b_i', 'rg_lambda', 'rg_w_out']
WEIGHTS = ARG_NAMES[1:]
BIG = {'w_up': 2, 'w_down': 1, 'w_in': 2, 'mla_w_q_up': 2, 'mla_w_kv_up': 2, 'w_out_ab': 1, 'rg_w_x': 2,
       'rg_w_y': 2, 'rg_w_out': 1}
SMALL = {'meta_tokens': 1, 'ssd_conv_w': 2, 'rg_conv_w': 2, 'rg_conv_b': 1, 'rg_b_a': 1, 'rg_b_i': 1, 'rg_lambda': 1}
REPL = [n for n in WEIGHTS if n not in BIG and n not in SMALL]
REPL_MEDIUM = ['rg_w_a', 'rg_w_i']
REPL_TINY = [n for n in REPL if n not in REPL_MEDIUM]


def _piece_axes():
    axes = {}
    for n, ax in BIG.items():
        for i in range(DEPTH if n in ('w_up', 'w_down') else DEPTH // 2):
            axes[(n, i)] = ax - 1
    return axes


PIECE_AXIS = _piece_axes()
AS_BLOCKS = ('w_up', 'w_down')
_SM = lambda i: [(n, i) for n in ('w_in', 'mla_w_q_up', 'mla_w_kv_up', 'w_out_ab')]
_RG = lambda i: [(n, i) for n in ('rg_w_x', 'rg_w_y', 'rg_w_out')]
_MLP = lambda l: [('w_up', l), ('w_down', l)]
_SM_IN = lambda i: [(n, i) for n in ('w_in', 'mla_w_q_up', 'mla_w_kv_up')]
GATHER_FIRST = _SM_IN(0)
GATHER_AT = {0: [('w_out_ab', 0)] + _MLP(0) + _RG(0) + _MLP(1) + _SM(1), 2: _MLP(2) + _RG(1) + _MLP(3)}
SCATTER_AT = {2: _MLP(3) + _RG(1) + _MLP(2) + [('w_out_ab', 1)],
              0: _SM_IN(1) + _MLP(1) + _RG(0) + _MLP(0) + [('w_out_ab', 0)]}
SCATTER_LAST = _SM_IN(0)


def _wire_block(p, key):
    n, i = key
    blk = p[n][i]
    if n == 'w_in':
        blk = jnp.pad(blk, ((0, 0), (0, W_IN_WIRE - blk.shape[1])))
    return blk


def _step(p, moments):
    assert DEPTH == 4
    full = {n: [None] * p[n].shape[0] for n in BIG}
    full['w_in_g'] = [None] * p['w_in'].shape[0]

    def weight_blocks(group):
        return [_wire_block(p, k).astype(BF) for k in group]

    def take_weights(group, gathered):
        for (n, i), piece in zip(group, gathered):
            if n == 'w_in':
                full['w_in_g'][i] = piece
            elif n in AS_BLOCKS:
                full[n][i] = DevBlocks(piece, PIECE_AXIS[(n, i)])
            else:
                full[n][i] = _gathered_full(piece, PIECE_AXIS[(n, i)])

    def grad_blocks(group, gw):
        return [gw[k] if k[0] in AS_BLOCKS or k[0] == 'w_in' else _per_device(gw[k], PIECE_AXIS[k]).astype(BF)
                for k in group]

    parts = {}

    small_slab = _to_slab([p[n].reshape(-1) for n in SMALL])
    *first, small8 = all_gather(weight_blocks(GATHER_FIRST) + [small_slab], name="gather_first")
    take_weights(GATHER_FIRST, first)
    for n, piece in zip(SMALL, _from_flat_rows(small8, [p[n].shape for n in SMALL])):
        full[n] = _gathered_full(piece, SMALL[n])
    for n in REPL:
        full[n] = p[n]
    loss_local, grad_x, gw, gsmall_full, carried = _local_step(
        full, p['x'][0], p['loss_target'][0],
        fwd_carry=lambda layer: Exchange("gather", weight_blocks(GATHER_AT[layer])),
        on_fwd_carried=lambda layer, got: take_weights(GATHER_AT[layer], got),
        bwd_carry=lambda layer, gw_now: Exchange("scatter", grad_blocks(SCATTER_AT[layer], gw_now)))

    for layer, group in SCATTER_AT.items():
        parts.update(zip(group, carried[layer]))
    rep_flat = jnp.concatenate([gsmall_full[n].reshape(-1) for n in REPL_TINY])
    rep_n = rep_flat.shape[0]
    rep_chunk = -(-rep_n // (N_DEV * PACK_W * 8)) * PACK_W * 8
    rep8 = jnp.pad(rep_flat, (0, N_DEV * rep_chunk - rep_n)).reshape(N_DEV, rep_chunk)
    gsmall = _to_slab([_per_device(gsmall_full[n], SMALL[n]).reshape(N_DEV, -1) for n in SMALL] + [rep8], lead=(N_DEV,))
    medium8 = [gsmall_full[n].reshape(N_DEV, -1, LANE) for n in REPL_MEDIUM]
    received = all_to_all(grad_blocks(SCATTER_LAST, gw) + [gsmall] + medium8, name="scatter_last")
    n_last = len(SCATTER_LAST)
    parts.update(zip(SCATTER_LAST, received[:n_last]))
    ssmall = slab_sum(received[n_last], name="sum_small").reshape(-1)
    medium_mine = [slab_sum(r8, name="sum_" + n) for n, r8 in zip(REPL_MEDIUM, received[n_last + 1:])]
    g_loc = {'w_in': jnp.stack([slab_sum(parts[('w_in', i)], name="sum_w_in_%d" % i)[:, :W_IN_SHARD]
                                for i in range(p['w_in'].shape[0])], axis=0)}
    small_n = sum(int(np.prod(p[n].shape)) for n in SMALL)
    g_loc.update(zip(SMALL, _from_flat(ssmall, [p[n].shape for n in SMALL])))
    rep_mine = ssmall[small_n:small_n + rep_chunk].reshape(-1, PACK_W)
    rep_all, *medium_all = all_gather([rep_mine] + medium_mine, name="gather_replicated")
    g_loc.update(zip(REPL_TINY, _from_flat(rep_all.reshape(-1), [p[n].shape for n in REPL_TINY])))
    g_loc.update({n: g.reshape(p[n].shape) for n, g in zip(REPL_MEDIUM, medium_all)})

    out = {'loss': lax.psum(loss_local, ("x", "y", "c")), 'grad_x': grad_x[None]}
    small_names = list(SMALL) + REPL_TINY
    for n in list(BIG) + REPL_MEDIUM:
        shp = p[n].shape
        if n == 'w_in' or n in REPL_MEDIUM:
            v2 = lambda a: a.reshape(-1, shp[-1])
            d, nm, nv = adamw(v2(p[n]), v2(g_loc[n]), v2(moments['m_' + n]), v2(moments['v_' + n]), name="adamw_" + n)
            d, nm, nv = d.reshape(shp), nm.reshape(shp), nv.reshape(shp)
        else:
            g_loc[n], d, nm, nv = adamw_blocks(p[n], moments['m_' + n], moments['v_' + n],
                                               [parts[(n, i)] for i in range(shp[0])], name="adamw_" + n)
        out['delta_' + n], out['new_m_' + n], out['new_v_' + n] = d, nm, nv
    slab = lambda src: _to_slab([src(n).reshape(-1) for n in small_names])
    d, nm, nv = adamw(slab(lambda n: p[n]), slab(lambda n: g_loc[n]), slab(lambda n: moments['m_' + n]),
                      slab(lambda n: moments['v_' + n]), name="adamw_small")
    shapes = [p[n].shape for n in small_names]
    for key, flat in (('delta_', d), ('new_m_', nm), ('new_v_', nv)):
        for n, a in zip(small_names, _from_flat(flat.reshape(-1), shapes)):
            out[key + n] = a
    for n in WEIGHTS:
        out['grad_' + n] = g_loc[n]
    return out


def _local_step(full, x, target_rows, fwd_carry=None, on_fwd_carried=None, bwd_carry=None):
    t = PAD + N_META + x.shape[0]
    h = jnp.concatenate([jnp.zeros((PAD, D_MODEL), F32), full['meta_tokens'], x], axis=0)
    target = jnp.concatenate([jnp.zeros((PAD + N_META, D_MODEL), F32), target_rows], axis=0)
    tabs = rope_tables(t)

    def layer_args(layer):
        i = layer // 2
        if layer % 2 == 0:
            convp = _rows8([full['ssd_conv_w'][i], full['ssd_conv_b'][i]], SSD_CONV_CH)
            return (full['mix_pre_g'][layer], full['mix_post_g'][layer], w_in_blocks_to_cols(full['w_in_g'][i]), convp,
                    full['ssd_dt_bias'][i], full['ssd_a_log'][i], full['ssd_d'][i], full['ssd_norm_g'][i],
                    full['mla_q_norm_g'][i], _pack_w_q(full['mla_w_q_up'][i]), full['mla_kv_norm_g'][i],
                    _pack_w_kv(full['mla_w_kv_up'][i]), lambda: full['w_out_ab'][i], tabs)
        rgp = _rows8([full['rg_conv_w'][i], full['rg_conv_b'][i], full['rg_b_a'][i], full['rg_b_i'][i],
                      full['rg_lambda'][i]], LRU_WIDTH)
        return (full['mix_pre_g'][layer], full['mix_post_g'][layer], full['rg_w_x'][i], full['rg_w_y'][i], rgp,
                full['rg_w_a'][i], full['rg_w_i'][i], full['rg_w_out'][i])

    def mlp_args(layer):
        return (full['mlp_pre_g'][layer], full['mlp_post_g'][layer], full['w_up'][layer], full['w_down'][layer])

    saved = []
    for layer in range(DEPTH):
        la = layer_args(layer)
        if layer % 2 == 0:
            if fwd_carry is not None:
                h, res_mix = sm_layer_fwd(h, *la, carry=fwd_carry(layer),
                                          on_carried=lambda got, layer=layer: on_fwd_carried(layer, got))
            else:
                h, res_mix = sm_layer_fwd(h, *la)
        else:
            h, res_mix = rg_layer_fwd(h, *la)
        ma = mlp_args(layer)
        h, res_mlp = mlp_fwd(h, *ma)
        saved.append((la, ma, res_mix, res_mlp))
    loss_local, dh = loss_fwd_bwd(h, target)

    others = {n: [None] * len(full[n]) for n in WEIGHTS if n not in BIG and n != 'meta_tokens'}
    gw, carried = {}, {}
    for layer in reversed(range(DEPTH)):
        la, ma, res_mix, res_mlp = saved[layer]
        dh, gm = mlp_bwd(res_mlp, dh, *ma)
        if layer % 2 == 0:
            for n in ('w_up', 'w_down'):
                gw[(n, layer)] = gm[n]
            carry = None
            if bwd_carry is not None:
                carry = lambda dw_out, layer=layer: bwd_carry(layer, {**gw, ('w_out_ab', layer // 2): dw_out})
            dh, gx, carried[layer] = sm_layer_bwd(res_mix, dh, *la, carry=carry)
        else:
            dh, gx = rg_layer_bwd(res_mix, dh, *la)
        for n, g in list(gm.items()) + list(gx.items()):
            i = layer if n in ('mix_pre_g', 'mix_post_g', 'mlp_pre_g', 'mlp_post_g', 'w_up', 'w_down') else layer // 2
            if n in BIG:
                gw[(n, i)] = g
            else:
                others[n][i] = g
    gothers = {n: jnp.stack(v, axis=0) for n, v in others.items()}
    gothers['meta_tokens'] = dh[PAD:PAD + N_META]
    return loss_local, dh[PAD + N_META:], gw, gothers, carried


def _from_flat_rows(g8, shapes):
    flat = g8.reshape(N_DEV, -1)
    out, off = [], 0
    for s in shapes:
        n = int(np.prod(s))
        out.append(flat[:, off:off + n].reshape((N_DEV,) + tuple(s)))
        off += n
    return out


def kernel(x, meta_tokens, mix_pre_g, mix_post_g, mlp_pre_g, mlp_post_g, w_up, w_down, w_in, ssd_conv_w, ssd_conv_b, ssd_dt_bias, ssd_a_log, ssd_d, ssd_norm_g, mla_q_norm_g, mla_w_q_up, mla_kv_norm_g, mla_w_kv_up, w_out_ab, rg_w_x, rg_w_y, rg_conv_w, rg_conv_b, rg_w_a, rg_b_a, rg_w_i, rg_b_i, rg_lambda, rg_w_out, loss_target, m_meta_tokens, m_mix_pre_g, m_mix_post_g, m_mlp_pre_g, m_mlp_post_g, m_w_up, m_w_down, m_w_in, m_ssd_conv_w, m_ssd_conv_b, m_ssd_dt_bias, m_ssd_a_log, m_ssd_d, m_ssd_norm_g, m_mla_q_norm_g, m_mla_w_q_up, m_mla_kv_norm_g, m_mla_w_kv_up, m_w_out_ab, m_rg_w_x, m_rg_w_y, m_rg_conv_w, m_rg_conv_b, m_rg_w_a, m_rg_b_a, m_rg_w_i, m_rg_b_i, m_rg_lambda, m_rg_w_out, v_meta_tokens, v_mix_pre_g, v_mix_post_g, v_mlp_pre_g, v_mlp_post_g, v_w_up, v_w_down, v_w_in, v_ssd_conv_w, v_ssd_conv_b, v_ssd_dt_bias, v_ssd_a_log, v_ssd_d, v_ssd_norm_g, v_mla_q_norm_g, v_mla_w_q_up, v_mla_kv_norm_g, v_mla_w_kv_up, v_w_out_ab, v_rg_w_x, v_rg_w_y, v_rg_conv_w, v_rg_conv_b, v_rg_w_a, v_rg_b_a, v_rg_w_i, v_rg_b_i, v_rg_lambda, v_rg_w_out):
    args = (x, meta_tokens, mix_pre_g, mix_post_g, mlp_pre_g, mlp_post_g, w_up, w_down, w_in, ssd_conv_w, ssd_conv_b, ssd_dt_bias, ssd_a_log, ssd_d, ssd_norm_g, mla_q_norm_g, mla_w_q_up, mla_kv_norm_g, mla_w_kv_up, w_out_ab, rg_w_x, rg_w_y, rg_conv_w, rg_conv_b, rg_w_a, rg_b_a, rg_w_i, rg_b_i, rg_lambda, rg_w_out, loss_target, m_meta_tokens, m_mix_pre_g, m_mix_post_g, m_mlp_pre_g, m_mlp_post_g, m_w_up, m_w_down, m_w_in, m_ssd_conv_w, m_ssd_conv_b, m_ssd_dt_bias, m_ssd_a_log, m_ssd_d, m_ssd_norm_g, m_mla_q_norm_g, m_mla_w_q_up, m_mla_kv_norm_g, m_mla_w_kv_up, m_w_out_ab, m_rg_w_x, m_rg_w_y, m_rg_conv_w, m_rg_conv_b, m_rg_w_a, m_rg_b_a, m_rg_w_i, m_rg_b_i, m_rg_lambda, m_rg_w_out, v_meta_tokens, v_mix_pre_g, v_mix_post_g, v_mlp_pre_g, v_mlp_post_g, v_w_up, v_w_down, v_w_in, v_ssd_conv_w, v_ssd_conv_b, v_ssd_dt_bias, v_ssd_a_log, v_ssd_d, v_ssd_norm_g, v_mla_q_norm_g, v_mla_w_q_up, v_mla_kv_norm_g, v_mla_w_kv_up, v_w_out_ab, v_rg_w_x, v_rg_w_y, v_rg_conv_w, v_rg_conv_b, v_rg_w_a, v_rg_b_a, v_rg_w_i, v_rg_b_i, v_rg_lambda, v_rg_w_out,)
    n_w = len(ARG_NAMES)
    p = dict(zip(ARG_NAMES, args[:n_w]))
    p['loss_target'] = args[n_w]
    moments = {}
    for i, n in enumerate(WEIGHTS):
        moments['m_' + n] = args[n_w + 1 + i]
        moments['v_' + n] = args[n_w + 1 + len(WEIGHTS) + i]
    out = _step(p, moments)
    res = [out['loss'], out['grad_x']]
    for prefix in ('grad_', 'delta_', 'new_m_', 'new_v_'):
        res += [out[prefix + n] for n in WEIGHTS]
    return tuple(res)
```

```python
import functools
import math

import numpy as np
import jax
import jax.numpy as jnp
from jax import lax
from jax.experimental import pallas as pl
from jax.experimental.pallas import tpu as pltpu

F32 = jnp.float32
BF = jnp.bfloat16
HI = lax.Precision.HIGHEST

D_MODEL = 1024
DEPTH = 4
N_META = 16
CHUNK = 128
PAD = CHUNK - N_META
EPS = 1e-6
SSD_HEADS = 16
SSD_HEAD_DIM = 64
SSD_D_INNER = 1024
SSD_STATE = 128
SSD_CONV_CH = 1536
MLA_HEADS = 16
MLA_NOPE = 64
MLA_ROPE = 32
MLA_V = 64
MLA_Q_RANK = 384
MLA_KV_RANK = 256
ROPE_BASE = 10000.0
LRU_WIDTH = 1280
LRU_BLOCKS = 10
LRU_C = 8.0
D_FF = 4096
N_DEV = 8
LANE = 128
IN_W = 3456
OFF_Z, OFF_XBC, OFF_CKV, OFF_DT, OFF_KR, OFF_CQ = 0, 1024, 2560, 2816, 2944, 3072

ADAM_LR = 0.001
ADAM_B1 = 0.9
ADAM_B2 = 0.999
ADAM_EPS = 1e-08
ADAM_WD = 0.01
ADAM_STEP = 10

VMEM_LIMIT = 56 * 1024 * 1024
NEG = -1e30


def _pick(n, cands):
    for c in cands:
        if n % c == 0:
            return c
    return n


def _cp(sem=None):
    return pltpu.CompilerParams(dimension_semantics=sem, vmem_limit_bytes=VMEM_LIMIT)


def _sds(shape, dtype):
    return jax.ShapeDtypeStruct(tuple(shape), dtype)


def _silu(x):
    return x * jax.nn.sigmoid(x)


def _softplus(x):
    return jnp.maximum(x, 0.0) + jnp.log(1.0 + jnp.exp(-jnp.abs(x)))


def _gelu(x):
    c = math.sqrt(2.0 / math.pi)
    return 0.5 * x * (1.0 + jnp.tanh(c * (x + 0.044715 * (x * x * x))))


def _row_mask(i, tr, shape, first_valid=PAD):
    row = i * tr + lax.broadcasted_iota(jnp.int32, shape, 0)
    return row >= first_valid


class KBlock:
    def __init__(self, arr, width, blk):
        self.arr, self.width, self.blk = arr, width, blk


class DevBlocks:
    def __init__(self, g8, axis):
        self.g8, self.axis = g8, axis
        _, r, c = g8.shape
        self.shape = (N_DEV * r, c) if axis == 0 else (r, N_DEV * c)


NN_DIMS = (((1,), (0,)), ((), ()))
MM_TALL_K = 1536


def matmul(a, b, mode, out_dtypes=(F32,), epi=None, extras=(), name="mm", tm=None, tn=None, out_blocks=False):
    a_terms = a if isinstance(a, (list, tuple)) else [a]
    b_terms = b if isinstance(b, (list, tuple)) else [b]
    assert len(a_terms) == len(b_terms) and (mode != "tn" or len(a_terms) == 1)
    arr_of = lambda t: t.arr if isinstance(t, KBlock) else t
    if mode == "tn":
        m, n = a_terms[0].shape[1], b_terms[0].shape[1]
    else:
        m = arr_of(a_terms[0]).shape[0]
        b0 = b_terms[0]
        n = (b0.shape if isinstance(b0, DevBlocks) else arr_of(b0).shape)[1 if mode == "nn" else 0]
    if mode == "tn":
        tm = _pick(m, (1024, 512, 384, 256, 128))
    else:
        k_all = sum(t.width if isinstance(t, KBlock) else t.shape[1] for t in a_terms)
        tall = (2112,) if k_all <= MM_TALL_K else ()
        tm = tm or _pick(m, tall + (1056, 1024, 768, 640, 512, 384, 256, 128))
    tn = tn or _pick(n, (512, 640, 384, 256, 128))
    dims = {"nn": NN_DIMS, "nt": NT_DIMS, "tn": TN_DIMS}[mode]

    in_specs, args, plan = [], [], []
    for ta, tb in zip(a_terms, b_terms):
        if mode == "tn":
            k = ta.shape[0]
            in_specs += [pl.BlockSpec((k, tm), lambda i, j: (0, i)), pl.BlockSpec((k, tn), lambda i, j: (0, j))]
            args += [ta, tb]
            plan.append(None)
            continue
        if isinstance(ta, KBlock):
            kw, ka = ta.width, ta.blk
            in_specs.append(pl.BlockSpec((tm, kw), lambda i, j, ka=ka: (i, ka)))
        else:
            kw = ta.shape[1]
            in_specs.append(pl.BlockSpec((tm, kw), lambda i, j: (i, 0)))
        args.append(arr_of(ta))
        if isinstance(tb, DevBlocks):
            _, r, c = tb.g8.shape
            split_k = tb.axis == (0 if mode == "nn" else 1)
            if split_k:
                kd = r if mode == "nn" else c
                assert kw == N_DEV * kd
                blk = (N_DEV, kd, tn) if mode == "nn" else (N_DEV, tn, kd)
                in_specs.append(pl.BlockSpec(blk, (lambda i, j: (0, 0, j)) if mode == "nn" else (lambda i, j: (0, j, 0))))
                plan.append(kd)
            else:
                per = (c if mode == "nn" else r) // tn
                blk = (None, kw, tn) if mode == "nn" else (None, tn, kw)
                in_specs.append(pl.BlockSpec(blk, (lambda i, j, per=per: (j // per, 0, j % per)) if mode == "nn"
                                             else (lambda i, j, per=per: (j // per, j % per, 0))))
                plan.append(None)
            args.append(tb.g8)
        else:
            kb = tb.blk if isinstance(tb, KBlock) else 0
            assert (tb.width if isinstance(tb, KBlock) else tb.shape[0 if mode == "nn" else 1]) == kw
            in_specs.append(pl.BlockSpec((kw, tn), lambda i, j, kb=kb: (kb, j)) if mode == "nn"
                            else pl.BlockSpec((tn, kw), lambda i, j, kb=kb: (j, kb)))
            args.append(arr_of(tb))
            plan.append(None)
    n_terms, n_ex = len(plan), len(extras)

    def body(*refs):
        ex_refs, out_refs = refs[2 * n_terms:2 * n_terms + n_ex], refs[2 * n_terms + n_ex:]
        acc = None
        for t, kd in enumerate(plan):
            a_ref, b_ref = refs[2 * t], refs[2 * t + 1]
            if kd is None:
                parts = [lax.dot_general(a_ref[...].astype(BF), b_ref[...].astype(BF), dims, preferred_element_type=F32)]
            else:
                parts = [lax.dot_general(a_ref[:, d * kd:(d + 1) * kd].astype(BF), b_ref[d].astype(BF), dims,
                                         preferred_element_type=F32) for d in range(N_DEV)]
            for part in parts:
                acc = part if acc is None else acc + part
        outs = (acc,) if epi is None else epi(acc, *[r[...] for r in ex_refs])
        for r, o in zip(out_refs, outs):
            r[...] = o.astype(r.dtype)

    o_spec = pl.BlockSpec((tm, tn), lambda i, j: (i, j))
    if out_blocks:
        per = n // N_DEV // tn
        out_shape = tuple(_sds((N_DEV, m, n // N_DEV), dt) for dt in out_dtypes)
        out_specs = tuple(pl.BlockSpec((None, tm, tn), lambda i, j: (j // per, i, j % per)) for _ in out_dtypes)
    else:
        out_shape = tuple(_sds((m, n), dt) for dt in out_dtypes)
        out_specs = tuple(o_spec for _ in out_dtypes)
    outs = pl.pallas_call(
        body,
        out_shape=out_shape,
        grid=(m // tm, n // tn),
        in_specs=in_specs + [o_spec] * n_ex,
        out_specs=out_specs,
        compiler_params=_cp(("parallel", "parallel")),
        name=name,
    )(*args, *extras)
    return outs[0] if len(out_dtypes) == 1 else outs


def _rt(t):
    return _pick(t, (384, 256, 128))


def norm_fwd(x, g, out_dtype, col_blk=0, width=None, name="norm_fwd"):
    t = x.shape[0]
    w = width or x.shape[1]
    tr = _rt(t)

    def body(x_ref, g_ref, o_ref):
        xv = x_ref[...]
        r = lax.rsqrt(jnp.mean(xv * xv, axis=-1, keepdims=True) + EPS)
        o_ref[...] = (xv * r * g_ref[...]).astype(o_ref.dtype)

    return pl.pallas_call(
        body,
        out_shape=_sds((t, w), out_dtype),
        grid=(t // tr,),
        in_specs=[pl.BlockSpec((tr, w), lambda i: (i, col_blk)), pl.BlockSpec((1, w), lambda i: (0, 0))],
        out_specs=pl.BlockSpec((tr, w), lambda i: (i, 0)),
        compiler_params=_cp(("parallel",)),
        name=name,
    )(x, g.reshape(1, w))


def norm_bwd(x, g, dy, dres=None, mask_pad=False, out_dtype=F32, col_blk=0, width=None, dy_col_blk=0, name="norm_bwd"):
    t = x.shape[0]
    w = width or x.shape[1]
    tr = _rt(t)
    has_res = dres is not None

    def body(x_ref, g_ref, dy_ref, *rest):
        if has_res:
            res_ref, dx_ref, dg_ref = rest
        else:
            dx_ref, dg_ref = rest
        i = pl.program_id(0)
        xv = x_ref[...]
        dyv = dy_ref[...].astype(F32)
        if mask_pad:
            dyv = jnp.where(_row_mask(i, tr, dyv.shape), dyv, 0.0)
        r = lax.rsqrt(jnp.mean(xv * xv, axis=-1, keepdims=True) + EPS)
        xh = xv * r
        dyg = dyv * g_ref[...]
        dx = r * (dyg - xh * jnp.mean(dyg * xh, axis=-1, keepdims=True))
        if has_res:
            dx = dx + res_ref[...]
        dx_ref[...] = dx.astype(dx_ref.dtype)

        @pl.when(i == 0)
        def _():
            dg_ref[...] = jnp.zeros_like(dg_ref)

        dg_ref[...] += jnp.sum(dyv * xh, axis=0, keepdims=True)

    in_specs = [pl.BlockSpec((tr, w), lambda i: (i, col_blk)), pl.BlockSpec((1, w), lambda i: (0, 0)),
                pl.BlockSpec((tr, w), lambda i: (i, dy_col_blk))]
    args = [x, g.reshape(1, w), dy]
    if has_res:
        in_specs.append(pl.BlockSpec((tr, w), lambda i: (i, 0)))
        args.append(dres)
    dx, dg = pl.pallas_call(
        body,
        out_shape=(_sds((t, w), out_dtype), _sds((1, w), F32)),
        grid=(t // tr,),
        in_specs=in_specs,
        out_specs=(pl.BlockSpec((tr, w), lambda i: (i, 0)), pl.BlockSpec((1, w), lambda i: (0, 0))),
        compiler_params=_cp(("arbitrary",)),
        name=name,
    )(*args)
    return dx, dg.reshape(w)


def resadd_fwd(h, m, g, name="resadd"):
    t, w = h.shape
    tr = _rt(t)

    def body(h_ref, m_ref, g_ref, o_ref):
        mv = m_ref[...]
        r = lax.rsqrt(jnp.mean(mv * mv, axis=-1, keepdims=True) + EPS)
        y = mv * r * g_ref[...]
        o_ref[...] = h_ref[...] + jnp.where(_row_mask(pl.program_id(0), tr, y.shape), y, 0.0)

    return pl.pallas_call(
        body,
        out_shape=_sds((t, w), F32),
        grid=(t // tr,),
        in_specs=[pl.BlockSpec((tr, w), lambda i: (i, 0)), pl.BlockSpec((tr, w), lambda i: (i, 0)),
                  pl.BlockSpec((1, w), lambda i: (0, 0))],
        out_specs=pl.BlockSpec((tr, w), lambda i: (i, 0)),
        compiler_params=_cp(("parallel",)),
        name=name,
    )(h, m, g.reshape(1, w))


def loss_fwd_bwd(h, target):
    t, w = h.shape
    tr = _rt(t)

    def body(h_ref, t_ref, s_ref, dh_ref):
        i = pl.program_id(0)
        err = h_ref[...] - t_ref[...]
        err = jnp.where(_row_mask(i, tr, err.shape, PAD + N_META), err, 0.0)
        dh_ref[...] = err * (1.0 / w)

        @pl.when(i == 0)
        def _():
            s_ref[...] = jnp.zeros_like(s_ref)

        s_ref[...] += jnp.sum(err * err).reshape(1, 1)

    s, dh = pl.pallas_call(
        body,
        out_shape=(_sds((1, LANE), F32), _sds((t, w), F32)),
        grid=(t // tr,),
        in_specs=[pl.BlockSpec((tr, w), lambda i: (i, 0)), pl.BlockSpec((tr, w), lambda i: (i, 0))],
        out_specs=(pl.BlockSpec((1, LANE), lambda i: (0, 0)), pl.BlockSpec((tr, w), lambda i: (i, 0))),
        compiler_params=_cp(("arbitrary",)),
        name="loss",
    )(h, target)
    return 0.5 * s[0, 0] / w, dh


def _shift_down(ext, k, n):
    return pltpu.roll(ext, k, 0)[8:]


def _conv_pre(ext, x, w_ref, n):
    return (w_ref[4:5, :] + w_ref[3:4, :] * x + w_ref[2:3, :] * _shift_down(ext, 1, n)
            + w_ref[1:2, :] * _shift_down(ext, 2, n) + w_ref[0:1, :] * _shift_down(ext, 3, n))


def _conv_bwd_parts(dpre, dnext, x, ext, w_ref, n):
    extd = jnp.concatenate([dpre, dnext], axis=0)
    ln = n + 8
    dx = (w_ref[3:4, :] * dpre + w_ref[2:3, :] * pltpu.roll(extd, ln - 1, 0)[:n]
          + w_ref[1:2, :] * pltpu.roll(extd, ln - 2, 0)[:n] + w_ref[0:1, :] * pltpu.roll(extd, ln - 3, 0)[:n])
    sums = [jnp.sum(dpre * _shift_down(ext, 3, n), axis=0, keepdims=True),
            jnp.sum(dpre * _shift_down(ext, 2, n), axis=0, keepdims=True),
            jnp.sum(dpre * _shift_down(ext, 1, n), axis=0, keepdims=True),
            jnp.sum(dpre * x, axis=0, keepdims=True),
            jnp.sum(dpre, axis=0, keepdims=True)]
    return dx, sums


def _rows_block(sums):
    w = sums[0].shape[1]
    row = lax.broadcasted_iota(jnp.int32, (8, w), 0)
    out = jnp.zeros((8, w), F32)
    for k, s in enumerate(sums):
        out = jnp.where(row == k, s, out)
    return out


CONV_BLOCK = 512


def conv_silu_fwd(x, col0, c, wb, name="conv_fwd"):
    t = x.shape[0]
    cw = _pick(c, (CONV_BLOCK, LANE))
    nblk, col0_blk = c // cw, col0 // cw
    assert col0 % cw == 0
    tr = _rt(t)

    def body(x_ref, w_ref, o_ref, prev):
        ti = pl.program_id(1)

        @pl.when(ti == 0)
        def _():
            prev[...] = jnp.zeros_like(prev)

        xv = x_ref[...]
        ext = jnp.concatenate([prev[...], xv], axis=0)
        o_ref[...] = _silu(_conv_pre(ext, xv, w_ref, tr))
        prev[...] = xv[tr - 8:, :]

    return pl.pallas_call(
        body,
        out_shape=_sds((t, c), F32),
        grid=(nblk, t // tr),
        in_specs=[pl.BlockSpec((tr, cw), lambda cb, ti: (ti, col0_blk + cb)),
                  pl.BlockSpec((8, cw), lambda cb, ti: (0, cb))],
        out_specs=pl.BlockSpec((tr, cw), lambda cb, ti: (ti, cb)),
        scratch_shapes=[pltpu.VMEM((8, cw), F32)],
        compiler_params=_cp(("parallel", "arbitrary")),
        name=name,
    )(x, wb)


def conv_silu_bwd(x, col0, c, wb, dout, name="conv_bwd"):
    t = x.shape[0]
    cw = _pick(c, (CONV_BLOCK, LANE))
    nblk, col0_blk = c // cw, col0 // cw
    assert col0 % cw == 0
    tr = _rt(t)
    nt = t // tr
    r8 = tr // 8

    def body(x_ref, xp_ref, w_ref, do_ref, dx_ref, dwb_ref, dnext):
        ti = pl.program_id(1)
        tt = nt - 1 - ti

        @pl.when(ti == 0)
        def _():
            dnext[...] = jnp.zeros_like(dnext)
            dwb_ref[...] = jnp.zeros_like(dwb_ref)

        xv = x_ref[...]
        halo = jnp.where(tt > 0, xp_ref[...], 0.0)
        ext = jnp.concatenate([halo, xv], axis=0)
        pre = _conv_pre(ext, xv, w_ref, tr)
        s = jax.nn.sigmoid(pre)
        dpre = do_ref[...] * (s + pre * s * (1.0 - s))
        dx, sums = _conv_bwd_parts(dpre, dnext[...], xv, ext, w_ref, tr)
        dx_ref[...] = dx.astype(dx_ref.dtype)
        dwb_ref[...] += _rows_block(sums)
        dnext[...] = dpre[:8, :]

    return pl.pallas_call(
        body,
        out_shape=(_sds((t, c), BF), _sds((8, c), F32)),
        grid=(nblk, nt),
        in_specs=[pl.BlockSpec((tr, cw), lambda cb, ti: (nt - 1 - ti, col0_blk + cb)),
                  pl.BlockSpec((8, cw), lambda cb, ti: (jnp.maximum((nt - 1 - ti) * r8 - 1, 0), col0_blk + cb)),
                  pl.BlockSpec((8, cw), lambda cb, ti: (0, cb)),
                  pl.BlockSpec((tr, cw), lambda cb, ti: (nt - 1 - ti, cb))],
        out_specs=(pl.BlockSpec((tr, cw), lambda cb, ti: (nt - 1 - ti, cb)),
                   pl.BlockSpec((8, cw), lambda cb, ti: (0, cb))),
        scratch_shapes=[pltpu.VMEM((8, cw), F32)],
        compiler_params=_cp(("parallel", "arbitrary")),
        name=name,
    )(x, x, wb, dout)


def gated_norm_fwd(y, proj, g, name="gnorm_fwd"):
    t, w = y.shape
    tr = _rt(t)

    def body(y_ref, z_ref, g_ref, o_ref):
        v = y_ref[...] * _silu(z_ref[...])
        r = lax.rsqrt(jnp.mean(v * v, axis=-1, keepdims=True) + EPS)
        o_ref[...] = (v * r * g_ref[...]).astype(o_ref.dtype)

    return pl.pallas_call(
        body,
        out_shape=_sds((t, w), BF),
        grid=(t // tr,),
        in_specs=[pl.BlockSpec((tr, w), lambda i: (i, 0)), pl.BlockSpec((tr, w), lambda i: (i, OFF_Z // w)),
                  pl.BlockSpec((1, w), lambda i: (0, 0))],
        out_specs=pl.BlockSpec((tr, w), lambda i: (i, 0)),
        compiler_params=_cp(("parallel",)),
        name=name,
    )(y, proj, g.reshape(1, w))


def gated_norm_bwd(y, proj, g, dyab, name="gnorm_bwd"):
    t, w = y.shape
    tr = _rt(t)

    def body(y_ref, z_ref, g_ref, do_ref, dy_ref, dz_ref, dg_ref):
        i = pl.program_id(0)
        yv, zv, dov = y_ref[...], z_ref[...], do_ref[...]
        s = jax.nn.sigmoid(zv)
        sz = zv * s
        v = yv * sz
        r = lax.rsqrt(jnp.mean(v * v, axis=-1, keepdims=True) + EPS)
        vh = v * r
        dvg = dov * g_ref[...]
        dv = r * (dvg - vh * jnp.mean(dvg * vh, axis=-1, keepdims=True))
        dy_ref[...] = dv * sz
        dz_ref[...] = (dv * yv * (s + sz * (1.0 - s))).astype(dz_ref.dtype)

        @pl.when(i == 0)
        def _():
            dg_ref[...] = jnp.zeros_like(dg_ref)

        dg_ref[...] += jnp.sum(dov * vh, axis=0, keepdims=True)

    dy, dz, dg = pl.pallas_call(
        body,
        out_shape=(_sds((t, w), F32), _sds((t, w), BF), _sds((1, w), F32)),
        grid=(t // tr,),
        in_specs=[pl.BlockSpec((tr, w), lambda i: (i, 0)), pl.BlockSpec((tr, w), lambda i: (i, OFF_Z // w)),
                  pl.BlockSpec((1, w), lambda i: (0, 0)), pl.BlockSpec((tr, w), lambda i: (i, 0))],
        out_specs=(pl.BlockSpec((tr, w), lambda i: (i, 0)), pl.BlockSpec((tr, w), lambda i: (i, 0)),
                   pl.BlockSpec((1, w), lambda i: (0, 0))),
        compiler_params=_cp(("arbitrary",)),
        name=name,
    )(y, proj, g.reshape(1, w), dyab)
    return dy, dz, dg.reshape(w)


def rope_tables(t):
    inv = ROPE_BASE ** (-jnp.arange(0, MLA_ROPE, 2, dtype=F32) / MLA_ROPE)
    pos = (jnp.arange(t, dtype=F32) - PAD)[:, None]
    ang = pos * inv[None, :]
    cos, sin = jnp.cos(ang), jnp.sin(ang)
    z16 = jnp.zeros((t, 16), F32)
    z32 = jnp.zeros((t, 32), F32)
    c = jnp.concatenate([jnp.ones((t, 64), F32), cos, cos, z32], axis=1)
    s1 = jnp.concatenate([jnp.zeros((t, 64), F32), z16, sin, z32], axis=1)
    s2 = jnp.concatenate([jnp.zeros((t, 64), F32), -sin, z16, z32], axis=1)
    return c, s1, s2


def _rope(x, c, s1, s2):
    return x * c + pltpu.roll(x, 16, 1) * s1 + pltpu.roll(x, LANE - 16, 1) * s2


def _rope_t(d, c, s1, s2):
    return d * c + pltpu.roll(d * s1, LANE - 16, 1) + pltpu.roll(d * s2, 16, 1)


def rope_fwd(q_raw, kv_raw, proj, tabs):
    t = q_raw.shape[0]
    tr = _rt(t)
    hw = MLA_HEADS * LANE

    def body(q_ref, k_ref, v_ref, kr_ref, c_ref, s1_ref, s2_ref, qo_ref, ko_ref, vo_ref):
        c, s1, s2 = c_ref[...], s1_ref[...], s2_ref[...]
        kr = _rope(kr_ref[...], c, s1, s2)
        for h in range(MLA_HEADS):
            sl = slice(h * LANE, (h + 1) * LANE)
            qo_ref[:, sl] = (_rope(q_ref[:, sl], c, s1, s2) * Q_PRESCALE).astype(BF)
            ko_ref[:, sl] = (k_ref[:, sl] + kr).astype(BF)
        vo_ref[...] = v_ref[...].astype(BF)

    tab_spec = pl.BlockSpec((tr, LANE), lambda i: (i, 0))
    return pl.pallas_call(
        body,
        out_shape=(_sds((t, hw), BF), _sds((t, hw), BF), _sds((t, 1024), BF)),
        grid=(t // tr,),
        in_specs=[pl.BlockSpec((tr, hw), lambda i: (i, 0)), pl.BlockSpec((tr, hw), lambda i: (i, 0)),
                  pl.BlockSpec((tr, 1024), lambda i: (i, 2)), pl.BlockSpec((tr, LANE), lambda i: (i, OFF_KR // LANE)),
                  tab_spec, tab_spec, tab_spec],
        out_specs=(pl.BlockSpec((tr, hw), lambda i: (i, 0)), pl.BlockSpec((tr, hw), lambda i: (i, 0)),
                   pl.BlockSpec((tr, 1024), lambda i: (i, 0))),
        compiler_params=_cp(("parallel",)),
        name="rope_fwd",
    )(q_raw, kv_raw, kv_raw, proj, *tabs)


def rope_bwd(dq_cat, dk_cat, tabs):
    t = dq_cat.shape[0]
    tr = _rt(t)
    hw = MLA_HEADS * LANE

    def body(dq_ref, dk_ref, c_ref, s1_ref, s2_ref, dqo_ref, dkr_ref):
        c, s1, s2 = c_ref[...], s1_ref[...], s2_ref[...]
        acc = jnp.zeros((tr, LANE), F32)
        for h in range(MLA_HEADS):
            sl = slice(h * LANE, (h + 1) * LANE)
            dqo_ref[:, sl] = _rope_t(dq_ref[:, sl] * ATT_SCALE, c, s1, s2).astype(BF)
            acc = acc + dk_ref[:, sl]
        lane = lax.broadcasted_iota(jnp.int32, (tr, LANE), 1)
        dkr_ref[...] = jnp.where((lane >= 64) & (lane < 96), _rope_t(acc, c, s1, s2), 0.0)

    tab_spec = pl.BlockSpec((tr, LANE), lambda i: (i, 0))
    return pl.pallas_call(
        body,
        out_shape=(_sds((t, hw), BF), _sds((t, LANE), F32)),
        grid=(t // tr,),
        in_specs=[pl.BlockSpec((tr, hw), lambda i: (i, 0)), pl.BlockSpec((tr, hw), lambda i: (i, 0)),
                  tab_spec, tab_spec, tab_spec],
        out_specs=(pl.BlockSpec((tr, hw), lambda i: (i, 0)), pl.BlockSpec((tr, LANE), lambda i: (i, 0))),
        compiler_params=_cp(("parallel",)),
        name="rope_bwd",
    )(dq_cat, dk_cat, *tabs)


ATT_SCALE = (MLA_NOPE + MLA_ROPE) ** -0.5
LOG2E = math.log2(math.e)
Q_PRESCALE = ATT_SCALE * LOG2E
CARRY_MIDDLE_PAIR = 6
NT_DIMS = (((1,), (1,)), ((), ()))
TN_DIMS = (((0,), (0,)), ((), ()))


def _att_mask(qi, ki, tq, tk):
    qpos = qi * tq + lax.broadcasted_iota(jnp.int32, (tq, tk), 0)
    kpos = ki * tk + lax.broadcasted_iota(jnp.int32, (tq, tk), 1)
    return (kpos <= qpos) & (kpos >= PAD)


def _half_masks(n):
    lane = lax.broadcasted_iota(jnp.int32, (n, LANE), 1)
    return lane < 64, lane >= 64


def _att_tile(t):
    return _pick(t, (384, 256, 128))


def _ds(i, n):
    return pl.ds(i * n, n) if isinstance(i, int) else pl.ds(pl.multiple_of(i * n, n), n)


FWD_PAIRS = 2


def attn_fwd(q_cat, k_cat, v, carry=None):
    t = q_cat.shape[0]
    tq = tk = _att_tile(t)
    nq = t // tq
    npair, nh = FWD_PAIRS, 2 * FWD_PAIRS
    n_grp = MLA_HEADS // nh
    nx = carry.k if carry else 0

    def body(*refs):
        q_ref, k_ref, v_ref = refs[:3]
        o_ref, lse_ref = refs[3 + nx:5 + nx]
        qi = pl.program_id(1)
        if carry:
            start, middle, finish = carry.phases(refs[3:3 + nx], refs[5 + nx:5 + 2 * nx], refs[5 + 2 * nx:])
            grp = pl.program_id(0)
            pl.when((grp == 0) & (qi == 0))(start)
            pl.when((grp == CARRY_MIDDLE_PAIR // npair) & (qi == 0))(middle)
        lo_q, _ = _half_masks(tq)
        halves = _half_masks(tk)

        def step(ki, state, masked):
            m_old, l_old, accs = state[0:nh], state[nh:2 * nh], state[2 * nh:]
            rows = _ds(ki, tk)
            ss = [lax.dot_general(q_ref[:, h * LANE:(h + 1) * LANE], k_ref[rows, h * LANE:(h + 1) * LANE], NT_DIMS,
                                  preferred_element_type=F32) for h in range(nh)]
            if masked:
                valid = _att_mask(qi, ki, tq, tk)
                ss = [jnp.where(valid, s, NEG) for s in ss]
            m_new = [jnp.maximum(m_old[h], jnp.max(ss[h], axis=-1, keepdims=True)) for h in range(nh)]
            ps = [jnp.exp2(ss[h] - m_new[h]) for h in range(nh)]
            alpha = [jnp.exp2(m_old[h] - m_new[h]) for h in range(nh)]
            l_new = [alpha[h] * l_old[h] + jnp.sum(ps[h], axis=-1, keepdims=True) for h in range(nh)]
            new_accs = []
            for pp in range(npair):
                vv = v_ref[rows, pp * LANE:(pp + 1) * LANE]
                pv = [jnp.dot(ps[2 * pp + hh].astype(BF), jnp.where(halves[hh], vv, jnp.zeros_like(vv)),
                              preferred_element_type=F32) for hh in range(2)]
                new_accs.append(accs[pp] * jnp.where(lo_q, alpha[2 * pp], alpha[2 * pp + 1]) + pv[0] + pv[1])
            return tuple(m_new) + tuple(l_new) + tuple(new_accs)

        neg, zero = jnp.full((tq, 1), NEG, F32), jnp.zeros((tq, 1), F32)
        state = step(0, (neg,) * nh + (zero,) * nh + (jnp.zeros((tq, LANE), F32),) * npair, True)
        state = lax.fori_loop(1, qi, lambda ki, st: step(ki, st, False), state)
        state = lax.cond(qi > 0, lambda st: step(qi, st, True), lambda st: st, state)
        for pp in range(npair):
            l = jnp.where(lo_q, state[nh + 2 * pp], state[nh + 2 * pp + 1])
            o_ref[:, pp * LANE:(pp + 1) * LANE] = (state[2 * nh + pp] / l).astype(o_ref.dtype)
            lse_ref[:, pp * LANE:(pp + 1) * LANE] = jnp.where(lo_q, state[2 * pp], state[2 * pp + 1]) + jnp.log2(l)
        if carry:
            pl.when((grp == n_grp - 1) & (qi == nq - 1))(finish)

    outs = pl.pallas_call(
        body,
        out_shape=(_sds((t, 1024), BF), _sds((t, 1024), F32)) + tuple(carry.out_shapes() if carry else ()),
        grid=(n_grp, nq),
        in_specs=[pl.BlockSpec((tq, nh * LANE), lambda g, qi: (qi, g)),
                  pl.BlockSpec((t, nh * LANE), lambda g, qi: (0, g)),
                  pl.BlockSpec((t, npair * LANE), lambda g, qi: (0, g))] + [ANY] * nx,
        out_specs=(pl.BlockSpec((tq, npair * LANE), lambda g, qi: (qi, g)),
                   pl.BlockSpec((tq, npair * LANE), lambda g, qi: (qi, g))) + (ANY,) * nx,
        scratch_shapes=carry.scratch() if carry else [],
        compiler_params=_cp(("arbitrary", "arbitrary") if carry else ("parallel", "parallel")),
        name="attn_fwd_carrying" if carry else "attn_fwd",
    )(q_cat, k_cat, v, *(carry.arrs if carry else ()))
    return outs[0], outs[1], list(outs[2:])


def attn_bwd(q_cat, k_cat, v, o, lse, dyab, carry=None):
    t = q_cat.shape[0]
    tq = tk = _att_tile(t)
    nq = t // tq
    n_pair = MLA_HEADS // 2
    nx = carry.k if carry else 0

    def body(*refs):
        q_ref, k_ref, v_ref, o_ref, lse_ref, do_ref = refs[:6]
        dq_ref, dk_ref, dv_ref = refs[6 + nx:9 + nx]
        ki = pl.program_id(1)
        if carry:
            start, middle, finish = carry.phases(refs[6:6 + nx], refs[9 + nx:9 + 2 * nx], refs[9 + 2 * nx:])
            pair = pl.program_id(0)
            pl.when((pair == 0) & (ki == 0))(start)
            pl.when((pair == CARRY_MIDDLE_PAIR) & (ki == 0))(middle)

        @pl.when(ki == 0)
        def _():
            dq_ref[...] = jnp.zeros_like(dq_ref)

        halves = _half_masks(tq)
        vv = v_ref[...]
        kk = [k_ref[:, hh * LANE:(hh + 1) * LANE] for hh in range(2)]

        def step(qi, acc, masked):
            rows = _ds(qi, tq)
            dov, ov, lse_v = do_ref[rows, :], o_ref[rows, :].astype(F32), lse_ref[rows, :]
            qh = [q_ref[rows, hh * LANE:(hh + 1) * LANE] for hh in range(2)]
            ss = [lax.dot_general(qh[hh], kk[hh], NT_DIMS, preferred_element_type=F32) for hh in range(2)]
            if masked:
                valid = _att_mask(qi, ki, tq, tk)
                ss = [jnp.where(valid, s, NEG) for s in ss]
            ps = [jnp.exp2(ss[hh] - lse_v[:, 64 * hh:64 * hh + 1]) for hh in range(2)]
            dom = [jnp.where(halves[hh], dov, 0.0) for hh in range(2)]
            delta = [jnp.sum(dom[hh] * ov, axis=-1, keepdims=True) for hh in range(2)]
            dom = [d.astype(BF) for d in dom]
            dp = [lax.dot_general(dom[hh], vv, NT_DIMS, preferred_element_type=F32) for hh in range(2)]
            ds = [(ps[hh] * (dp[hh] - delta[hh])).astype(BF) for hh in range(2)]
            pb = [p.astype(BF) for p in ps]
            dv = (acc[2] + lax.dot_general(pb[0], dom[0], TN_DIMS, preferred_element_type=F32)
                  + lax.dot_general(pb[1], dom[1], TN_DIMS, preferred_element_type=F32))
            dk = [acc[hh] + lax.dot_general(ds[hh], qh[hh], TN_DIMS, preferred_element_type=F32) for hh in range(2)]
            for hh in range(2):
                dq_ref[rows, hh * LANE:(hh + 1) * LANE] += jnp.dot(ds[hh], kk[hh], preferred_element_type=F32)
            return dk[0], dk[1], dv

        zero = jnp.zeros((tk, LANE), F32)
        acc = step(ki, (zero, zero, zero), True)
        acc = lax.fori_loop(ki + 1, jnp.where(ki == 0, nq, ki + 1), lambda qi, a: step(qi, a, True), acc)
        acc = lax.fori_loop(ki + 1, jnp.where(ki == 0, ki + 1, nq), lambda qi, a: step(qi, a, False), acc)
        dk_ref[:, 0:LANE] = acc[0] * (1.0 / LOG2E)
        dk_ref[:, LANE:2 * LANE] = acc[1] * (1.0 / LOG2E)
        dv_ref[...] = acc[2]
        if carry:
            pl.when((pair == n_pair - 1) & (ki == nq - 1))(finish)

    full = lambda w, off=0: pl.BlockSpec((t, w), lambda p, ki: (0, p + off))
    blk = lambda w: pl.BlockSpec((tk, w), lambda p, ki: (ki, p))
    outs = pl.pallas_call(
        body,
        out_shape=(_sds((t, 2048), F32), _sds((t, 2048), F32), _sds((t, 1024), F32))
        + tuple(carry.out_shapes() if carry else ()),
        grid=(n_pair, nq),
        in_specs=[full(2 * LANE), blk(2 * LANE), blk(LANE), full(LANE), full(LANE), full(LANE, 8)] + [ANY] * nx,
        out_specs=(full(2 * LANE), blk(2 * LANE), blk(LANE)) + (ANY,) * nx,
        scratch_shapes=carry.scratch() if carry else [],
        compiler_params=_cp(("arbitrary", "arbitrary") if carry else ("parallel", "arbitrary")),
        name="attn_bwd_carrying" if carry else "attn_bwd",
    )(q_cat, k_cat, v, o, lse, dyab, *(carry.arrs if carry else ()))
    return outs[0], outs[1], outs[2], list(outs[3:])


N_PAIR = SSD_HEADS // 2


def _hdot(a, b):
    return jnp.dot(a, b, precision=HI, preferred_element_type=F32)


def _ssd_chunk(xs, bg, cg, dtraw, hin, dt_bias, a_log, dskip, rowmask):
    ln = CHUNK
    causal = lax.broadcasted_iota(jnp.int32, (ln, ln), 0) >= lax.broadcasted_iota(jnp.int32, (ln, ln), 1)
    ltri = causal.astype(F32)
    lane = lax.broadcasted_iota(jnp.int32, (ln, LANE), 1)
    halves = (lane < 64, lane >= 64)
    low_row = lax.broadcasted_iota(jnp.int32, (1, LANE), 1) < 64
    head_lane = lax.broadcasted_iota(jnp.int32, (1, SSD_HEADS), 1)
    head_row = lax.broadcasted_iota(jnp.int32, (SSD_HEADS, 1), 0)

    def col(a, h):
        return jnp.sum(jnp.where(head_lane == h, a, 0.0), axis=1, keepdims=True)

    dt = _softplus(dtraw + dt_bias) * rowmask
    da = dt * (-jnp.exp(a_log))
    acs = _hdot(ltri, da)
    acs_t = lax.dot_general(da, ltri, (((0,), (1,)), ((), ())), precision=HI, preferred_element_type=F32)
    tot = jnp.sum(da, axis=0, keepdims=True)
    bm = [b * rowmask for b in bg]
    cm = [c * rowmask for c in cg]
    cb = [lax.dot_general(cm[g].astype(BF), bm[g].astype(BF), NT_DIMS, preferred_element_type=F32) for g in range(2)]
    ys, hout = [], []
    for p in range(N_PAIR):
        g = p // (N_PAIR // 2)
        h0, h1 = 2 * p, 2 * p + 1
        xdt = xs[p] * jnp.where(halves[0], col(dt, h0), col(dt, h1))
        a_cols = [col(acs, h0), col(acs, h1)]
        tot_cols = [col(tot, h0), col(tot, h1)]
        y = jnp.zeros((ln, LANE), F32)
        snew = jnp.zeros((ln, LANE), F32)
        for hh in range(2):
            a_row = jnp.sum(jnp.where(head_row == h0 + hh, acs_t, 0.0), axis=0, keepdims=True)
            dec = jnp.exp(jnp.where(causal, a_cols[hh] - a_row, NEG))
            xm = jnp.where(halves[hh], xdt, 0.0).astype(BF)
            y = y + jnp.dot((cb[g] * dec).astype(BF), xm, preferred_element_type=F32)
            bd = bm[g] * jnp.exp(tot_cols[hh] - a_cols[hh])
            snew = snew + lax.dot_general(bd.astype(BF), xm, TN_DIMS, preferred_element_type=F32)
        y_off = (jnp.dot(cm[g].astype(BF), hin[p].astype(BF), preferred_element_type=F32)
                 * jnp.where(halves[0], jnp.exp(a_cols[0]), jnp.exp(a_cols[1])))
        ys.append(y + y_off + jnp.where(low_row, col(dskip, h0), col(dskip, h1)) * xs[p])
        hout.append(jnp.where(low_row, jnp.exp(tot_cols[0]), jnp.exp(tot_cols[1])) * hin[p] + snew)
    return ys, hout


def _ssd_load(x_ref, dt_ref):
    xs = [x_ref[:, p * LANE:(p + 1) * LANE] for p in range(N_PAIR)]
    bg = [x_ref[:, SSD_D_INNER + g * LANE:SSD_D_INNER + (g + 1) * LANE] for g in range(2)]
    cg = [x_ref[:, SSD_D_INNER + (2 + g) * LANE:SSD_D_INNER + (3 + g) * LANE] for g in range(2)]
    return xs, bg, cg, dt_ref[:, 0:SSD_HEADS]


def _chunk_rowmask(c):
    return ((c * CHUNK + lax.broadcasted_iota(jnp.int32, (CHUNK, 1), 0)) >= PAD).astype(F32)


def ssd_fwd(xbc_c, proj, dt_bias, a_log, dskip):
    t = xbc_c.shape[0]
    nc = t // CHUNK

    def body(x_ref, dt_ref, dtb_ref, al_ref, d_ref, y_ref, hs_ref, h_s):
        c = pl.program_id(0)

        @pl.when(c == 0)
        def _():
            h_s[...] = jnp.zeros_like(h_s)

        xs, bg, cg, dtraw = _ssd_load(x_ref, dt_ref)
        hin = [h_s[p] for p in range(N_PAIR)]
        hs_ref[0] = h_s[...]
        ys, hout = _ssd_chunk(xs, bg, cg, dtraw, hin, dtb_ref[...], al_ref[...], d_ref[...], _chunk_rowmask(c))
        for p in range(N_PAIR):
            y_ref[:, p * LANE:(p + 1) * LANE] = ys[p]
            h_s[p] = hout[p]

    par = pl.BlockSpec((1, SSD_HEADS), lambda c: (0, 0))
    return pl.pallas_call(
        body,
        out_shape=(_sds((t, SSD_D_INNER), F32), _sds((nc, N_PAIR, CHUNK, LANE), F32)),
        grid=(nc,),
        in_specs=[pl.BlockSpec((CHUNK, SSD_CONV_CH), lambda c: (c, 0)),
                  pl.BlockSpec((CHUNK, LANE), lambda c: (c, OFF_DT // LANE)), par, par, par],
        out_specs=(pl.BlockSpec((CHUNK, SSD_D_INNER), lambda c: (c, 0)),
                   pl.BlockSpec((1, N_PAIR, CHUNK, LANE), lambda c: (c, 0, 0, 0))),
        scratch_shapes=[pltpu.VMEM((N_PAIR, CHUNK, LANE), F32)],
        compiler_params=_cp(("arbitrary",)),
        name="ssd_fwd",
    )(xbc_c, proj, dt_bias.reshape(1, -1), a_log.reshape(1, -1), dskip.reshape(1, -1))


def ssd_bwd(xbc_c, proj, dt_bias, a_log, dskip, hs, dy):
    t = xbc_c.shape[0]
    nc = t // CHUNK

    def body(x_ref, dt_ref, dtb_ref, al_ref, d_ref, hs_ref, dy_ref, dx_ref, ddt_ref, dpar_ref, dh_s):
        ci = pl.program_id(0)
        c = nc - 1 - ci

        @pl.when(ci == 0)
        def _():
            dh_s[...] = jnp.zeros_like(dh_s)
            dpar_ref[...] = jnp.zeros_like(dpar_ref)

        xs, bg, cg, dtraw = _ssd_load(x_ref, dt_ref)
        hin = [hs_ref[0, p] for p in range(N_PAIR)]
        rowmask = _chunk_rowmask(c)
        fn = lambda xs_, bg_, cg_, dtraw_, hin_, dtb_, al_, d_: _ssd_chunk(xs_, bg_, cg_, dtraw_, hin_, dtb_, al_, d_, rowmask)
        _, vjp = jax.vjp(fn, xs, bg, cg, dtraw, hin, dtb_ref[...], al_ref[...], d_ref[...])
        dys = [dy_ref[:, p * LANE:(p + 1) * LANE] for p in range(N_PAIR)]
        dhs = [dh_s[p] for p in range(N_PAIR)]
        dxs, dbg, dcg, ddtraw, dhin, ddtb, dal, dd = vjp((dys, dhs))
        for p in range(N_PAIR):
            dx_ref[:, p * LANE:(p + 1) * LANE] = dxs[p]
            dh_s[p] = dhin[p]
        for g in range(2):
            dx_ref[:, SSD_D_INNER + g * LANE:SSD_D_INNER + (g + 1) * LANE] = dbg[g]
            dx_ref[:, SSD_D_INNER + (2 + g) * LANE:SSD_D_INNER + (3 + g) * LANE] = dcg[g]
        ddt_ref[...] = jnp.zeros_like(ddt_ref)
        ddt_ref[:, 0:SSD_HEADS] = ddtraw
        dpar_ref[0:1, 0:SSD_HEADS] += ddtb
        dpar_ref[1:2, 0:SSD_HEADS] += dal
        dpar_ref[2:3, 0:SSD_HEADS] += dd

    par = pl.BlockSpec((1, SSD_HEADS), lambda ci: (0, 0))
    return pl.pallas_call(
        body,
        out_shape=(_sds((t, SSD_CONV_CH), F32), _sds((t, LANE), F32), _sds((8, LANE), F32)),
        grid=(nc,),
        in_specs=[pl.BlockSpec((CHUNK, SSD_CONV_CH), lambda ci: (nc - 1 - ci, 0)),
                  pl.BlockSpec((CHUNK, LANE), lambda ci: (nc - 1 - ci, OFF_DT // LANE)), par, par, par,
                  pl.BlockSpec((1, N_PAIR, CHUNK, LANE), lambda ci: (nc - 1 - ci, 0, 0, 0)),
                  pl.BlockSpec((CHUNK, SSD_D_INNER), lambda ci: (nc - 1 - ci, 0))],
        out_specs=(pl.BlockSpec((CHUNK, SSD_CONV_CH), lambda ci: (nc - 1 - ci, 0)),
                   pl.BlockSpec((CHUNK, LANE), lambda ci: (nc - 1 - ci, 0)),
                   pl.BlockSpec((8, LANE), lambda ci: (0, 0))),
        scratch_shapes=[pltpu.VMEM((N_PAIR, CHUNK, LANE), F32)],
        compiler_params=_cp(("arbitrary",)),
        name="ssd_bwd",
    )(xbc_c, proj, dt_bias.reshape(1, -1), a_log.reshape(1, -1), dskip.reshape(1, -1), hs, dy)


def _neg_expm1(y):
    series = -(y * (1.0 + y * (0.5 + y * (1.0 / 6.0 + y * (1.0 / 24.0 + y * (1.0 / 120.0))))))
    return jnp.where(y > -0.1, series, 1.0 - jnp.exp(y))


def _rg_pw(xr, wa, ba, wi, bi, lam, rowmask):
    xb = xr.astype(BF)
    r = jax.nn.sigmoid(jnp.dot(xb, wa.astype(BF), preferred_element_type=F32) + ba)
    i = jax.nn.sigmoid(jnp.dot(xb, wi.astype(BF), preferred_element_type=F32) + bi)
    log_a = -LRU_C * r * _softplus(-lam)
    a = jnp.exp(log_a)
    u = jnp.sqrt(_neg_expm1(2.0 * log_a)) * (i * xr) * rowmask
    return a, u


def _gelu_grad(x):
    c = math.sqrt(2.0 / math.pi)
    th = jnp.tanh(c * (x + 0.044715 * (x * x * x)))
    return 0.5 * (1.0 + th) + 0.5 * x * (1.0 - th * th) * c * (1.0 + 3.0 * 0.044715 * x * x)


def _scan_fwd(a, u):
    n = a.shape[0]
    row = lax.broadcasted_iota(jnp.int32, a.shape, 0)
    s = 1
    while s < n:
        a_s = jnp.where(row >= s, pltpu.roll(a, s, 0), 1.0)
        u_s = jnp.where(row >= s, pltpu.roll(u, s, 0), 0.0)
        u = u + a * u_s
        a = a * a_s
        s *= 2
    return a, u


def _scan_bwd(b, d):
    n = b.shape[0]
    row = lax.broadcasted_iota(jnp.int32, b.shape, 0)
    s = 1
    while s < n:
        b_s = jnp.where(row < n - s, pltpu.roll(b, n - s, 0), 1.0)
        d_s = jnp.where(row < n - s, pltpu.roll(d, n - s, 0), 0.0)
        d = d + b * d_s
        b = b * b_s
        s *= 2
    return d


def rg_fwd(xr_pre, gate_pre, rgp, w_a, w_i):
    t = xr_pre.shape[0]
    tr = _rt(t)

    def body(x_ref, g_ref, p_ref, wa_ref, wi_ref, hg_ref, hs_ref, prev, hcar):
        ti = pl.program_id(1)

        @pl.when(ti == 0)
        def _():
            prev[...] = jnp.zeros_like(prev)
            hcar[...] = jnp.zeros_like(hcar)

        xv = x_ref[...]
        ext = jnp.concatenate([prev[...], xv], axis=0)
        xr = _conv_pre(ext, xv, p_ref, tr)
        rowmask = _row_mask(ti, tr, (tr, 1)).astype(F32)
        a, u = _rg_pw(xr, wa_ref[0], p_ref[5:6, :], wi_ref[0], p_ref[6:7, :], p_ref[7:8, :], rowmask)
        a_cum, h_loc = _scan_fwd(a, u)
        hs = h_loc + a_cum * hcar[0:1, :]
        hs_ref[...] = hs
        hg_ref[...] = (hs * _gelu(g_ref[...])).astype(hg_ref.dtype)
        hcar[...] = jnp.broadcast_to(hs[tr - 1:tr, :], (8, LANE))
        prev[...] = xv[tr - 8:, :]

    return pl.pallas_call(
        body,
        out_shape=(_sds((t, LRU_WIDTH), BF), _sds((t, LRU_WIDTH), F32)),
        grid=(LRU_BLOCKS, t // tr),
        in_specs=[pl.BlockSpec((tr, LANE), lambda n, ti: (ti, n)),
                  pl.BlockSpec((tr, LANE), lambda n, ti: (ti, n)),
                  pl.BlockSpec((8, LANE), lambda n, ti: (0, n)),
                  pl.BlockSpec((1, LANE, LANE), lambda n, ti: (n, 0, 0)),
                  pl.BlockSpec((1, LANE, LANE), lambda n, ti: (n, 0, 0))],
        out_specs=(pl.BlockSpec((tr, LANE), lambda n, ti: (ti, n)), pl.BlockSpec((tr, LANE), lambda n, ti: (ti, n))),
        scratch_shapes=[pltpu.VMEM((8, LANE), F32), pltpu.VMEM((8, LANE), F32)],
        compiler_params=_cp(("parallel", "arbitrary")),
        name="rg_fwd",
    )(xr_pre, gate_pre, rgp, w_a, w_i)


def rg_bwd(xr_pre, gate_pre, rgp, w_a, w_i, hs, dhg):
    t = xr_pre.shape[0]
    tr = _rt(t)
    nt = t // tr
    r8 = tr // 8

    def body(x_ref, xp_ref, g_ref, p_ref, wa_ref, wi_ref, hs_ref, hp_ref, dhg_ref,
             dx_ref, dg_ref, dp_ref, dwa_ref, dwi_ref, gcar, dnext):
        ti = pl.program_id(1)
        tt = nt - 1 - ti

        @pl.when(ti == 0)
        def _():
            gcar[...] = jnp.zeros_like(gcar)
            dnext[...] = jnp.zeros_like(dnext)
            dp_ref[...] = jnp.zeros_like(dp_ref)
            dwa_ref[...] = jnp.zeros_like(dwa_ref)
            dwi_ref[...] = jnp.zeros_like(dwi_ref)

        xv = x_ref[...]
        halo = jnp.where(tt > 0, xp_ref[...], 0.0)
        ext = jnp.concatenate([halo, xv], axis=0)
        xr = _conv_pre(ext, xv, p_ref, tr)
        rowmask = _row_mask(tt, tr, (tr, 1)).astype(F32)
        fn = lambda xr_, wa_, ba_, wi_, bi_, lam_: _rg_pw(xr_, wa_, ba_, wi_, bi_, lam_, rowmask)
        (a, _), vjp = jax.vjp(fn, xr, wa_ref[0], p_ref[5:6, :], wi_ref[0], p_ref[6:7, :], p_ref[7:8, :])
        gpre = g_ref[...]
        hsv = hs_ref[...]
        dhg_v = dhg_ref[...]
        dg_ref[...] = (dhg_v * hsv * _gelu_grad(gpre)).astype(dg_ref.dtype)
        row = lax.broadcasted_iota(jnp.int32, (tr, LANE), 0)
        d = dhg_v * _gelu(gpre) + jnp.where(row == tr - 1, gcar[0:1, :], 0.0)
        b = jnp.where(row < tr - 1, pltpu.roll(a, tr - 1, 0), 0.0)
        g = _scan_bwd(b, d)
        gcar[...] = jnp.broadcast_to(a[0:1, :] * g[0:1, :], (8, LANE))
        hlast = jnp.where(tt > 0, hp_ref[7:8, :], 0.0)
        hprev = jnp.where(row == 0, hlast, pltpu.roll(hsv, 1, 0))
        dxr, dwa, dba, dwi, dbi, dlam = vjp((g * hprev, g))
        dx, sums = _conv_bwd_parts(dxr, dnext[...], xv, ext, p_ref, tr)
        dx_ref[...] = dx.astype(dx_ref.dtype)
        dnext[...] = dxr[:8, :]
        dp_ref[...] += _rows_block(sums + [dba, dbi, dlam])
        dwa_ref[0] += dwa
        dwi_ref[0] += dwi

    tile = lambda off=0: pl.BlockSpec((tr, LANE), lambda n, ti: (nt - 1 - ti, off + n))
    halo = lambda off=0: pl.BlockSpec((8, LANE), lambda n, ti: (jnp.maximum((nt - 1 - ti) * r8 - 1, 0), off + n))
    par = pl.BlockSpec((8, LANE), lambda n, ti: (0, n))
    wspec = pl.BlockSpec((1, LANE, LANE), lambda n, ti: (n, 0, 0))
    return pl.pallas_call(
        body,
        out_shape=(_sds((t, LRU_WIDTH), BF), _sds((t, LRU_WIDTH), BF), _sds((8, LRU_WIDTH), F32),
                   _sds((LRU_BLOCKS, LANE, LANE), F32), _sds((LRU_BLOCKS, LANE, LANE), F32)),
        grid=(LRU_BLOCKS, nt),
        in_specs=[tile(), halo(), tile(), par, wspec, wspec, tile(), halo(), tile()],
        out_specs=(tile(), tile(), par, wspec, wspec),
        scratch_shapes=[pltpu.VMEM((8, LANE), F32), pltpu.VMEM((8, LANE), F32)],
        compiler_params=_cp(("parallel", "arbitrary")),
        name="rg_bwd",
    )(xr_pre, xr_pre, gate_pre, rgp, w_a, w_i, hs, hs, dhg)


PACK_W = 1024
MESH_ID = pl.DeviceIdType.MESH
ANY = pl.BlockSpec(memory_space=pl.ANY)


def _my_place():
    x, y, c = lax.axis_index("x"), lax.axis_index("y"), lax.axis_index("c")
    return x, y, c


def _lin(px, py, pc):
    return 4 * px + 2 * py + pc


class Exchange:
    def __init__(self, kind, arrs):
        self.kind, self.arrs, self.k = kind, list(arrs), len(arrs)

    def out_shapes(self):
        if self.kind == "gather":
            return [_sds((N_DEV,) + a.shape, a.dtype) for a in self.arrs]
        return [_sds(a.shape, a.dtype) for a in self.arrs]

    def scratch(self):
        k = self.k
        return [pltpu.SemaphoreType.DMA((k, 7)), pltpu.SemaphoreType.DMA((k, 7)), pltpu.SemaphoreType.DMA((k,))]

    def phases(self, ins, outs, sems):
        return (self._gather if self.kind == "gather" else self._scatter)(ins, outs, *sems)

    def _gather(self, ins, outs, send_sems, recv_sems, local_sems):
        k = self.k
        x, y, c = _my_place()
        me, sibling = (x, y, c), (x, y, 1 - c)
        chips = [(1 - x, y), (x, 1 - y), (1 - x, 1 - y)]

        def copy(a, sem, block, to, from_input=False):
            slab = outs[a].at[_lin(*block)]
            return pltpu.make_async_remote_copy(
                src_ref=ins[a] if from_input else slab, dst_ref=slab,
                send_sem=send_sems.at[a, sem], recv_sem=recv_sems.at[a, sem],
                device_id=to, device_id_type=MESH_ID)

        def mine():
            return [pltpu.make_async_copy(ins[a], outs[a].at[_lin(*me)], local_sems.at[a]) for a in range(k)]

        def first():
            out = []
            for a in range(k):
                out.append(copy(a, 0, me, sibling, True))
                out += [copy(a, 1 + j, me, (*chip, c), True) for j, chip in enumerate(chips)]
            return out

        def passed():
            return [copy(a, 4 + j, (*chip, c), sibling) for j, chip in enumerate(chips) for a in range(k)]

        def start():
            for cp in mine() + first():
                cp.start()

        def middle():
            onward = passed()
            for j, chip in enumerate(chips):
                for a in range(k):
                    copy(a, 1 + j, (*chip, c), me).wait_recv()
                    onward[j * k + a].start()

        def finish():
            for a in range(k):
                copy(a, 0, sibling, me).wait_recv()
                for j, chip in enumerate(chips):
                    copy(a, 4 + j, (*chip, 1 - c), me).wait_recv()
            for cp in first() + passed():
                cp.wait_send()
            for cp in mine():
                cp.wait()

        return start, middle, finish

    def _scatter(self, ins, outs, send_sems, recv_sems, local_sems):
        k = self.k
        x, y, c = _my_place()
        me = _lin(x, y, c)
        peers = [((1 - x) if r & 4 else x, (1 - y) if r & 2 else y, (1 - c) if r & 1 else c) for r in range(1, N_DEV)]

        def copy(a, r, src_slab, dst_slab, to):
            return pltpu.make_async_remote_copy(
                src_ref=ins[a].at[src_slab], dst_ref=outs[a].at[dst_slab],
                send_sem=send_sems.at[a, r], recv_sem=recv_sems.at[a, r],
                device_id=to, device_id_type=MESH_ID)

        def mine():
            return [pltpu.make_async_copy(ins[a].at[me], outs[a].at[me], local_sems.at[a]) for a in range(k)]

        def sends():
            return [copy(a, r, _lin(*peer), me, peer) for r, peer in enumerate(peers) for a in range(k)]

        def start():
            for cp in mine() + sends():
                cp.start()

        def middle():
            pass

        def finish():
            for r, peer in enumerate(peers):
                for a in range(k):
                    copy(a, r, me, _lin(*peer), peer).wait_recv()
            for cp in sends():
                cp.wait_send()
            for cp in mine():
                cp.wait()

        return start, middle, finish

    def run(self, name):
        k = self.k

        def body(*refs):
            start, middle, finish = self.phases(refs[:k], refs[k:2 * k], refs[2 * k:])
            start()
            middle()
            finish()

        return pl.pallas_call(
            body,
            out_shape=tuple(self.out_shapes()),
            in_specs=[ANY] * k,
            out_specs=tuple(ANY for _ in range(k)),
            scratch_shapes=self.scratch(),
            name=name,
        )(*self.arrs)


def all_gather(arrs, name):
    return Exchange("gather", arrs).run(name)


def all_to_all(arrs, name):
    return Exchange("scatter", arrs).run(name)


def slab_sum(a, name):
    _, r, w = a.shape
    tr = _pick(r, (256, 128, 64, 32, 16, 8))

    def body(a_ref, o_ref):
        acc = a_ref[0].astype(F32)
        for d in range(1, N_DEV):
            acc = acc + a_ref[d].astype(F32)
        o_ref[...] = acc

    return pl.pallas_call(
        body,
        out_shape=_sds((r, w), F32),
        grid=(r // tr,),
        in_specs=[pl.BlockSpec((N_DEV, tr, w), lambda i: (0, i, 0))],
        out_specs=pl.BlockSpec((tr, w), lambda i: (i, 0)),
        compiler_params=_cp(("parallel",)),
        name=name,
    )(a)


def _adam_update(w, g, m, v):
    nm = ADAM_B1 * m + (1.0 - ADAM_B1) * g
    nv = ADAM_B2 * v + (1.0 - ADAM_B2) * (g * g)
    m_hat = nm / (1.0 - ADAM_B1 ** ADAM_STEP)
    v_hat = nv / (1.0 - ADAM_B2 ** ADAM_STEP)
    return -ADAM_LR * (m_hat / (jnp.sqrt(v_hat) + ADAM_EPS) + ADAM_WD * w), nm, nv


def adamw_blocks(w, m, v, parts, name):
    nl, r, c = w.shape
    tr = next(t for t in (256, 160, 128, 64, 32, 16) if r % t == 0 and N_DEV * t * c * 2 <= 2 * 1024 * 1024)

    def body(w_ref, m_ref, v_ref, *rest):
        part_refs, (g_ref, d_ref, nm_ref, nv_ref) = rest[:nl], rest[nl:]
        layer = pl.program_id(0)
        for idx in range(nl):
            @pl.when(layer == idx)
            def _(idx=idx):
                g = part_refs[idx][0].astype(F32)
                for dev in range(1, N_DEV):
                    g = g + part_refs[idx][dev].astype(F32)
                g_ref[...] = g
                d_ref[...], nm_ref[...], nv_ref[...] = _adam_update(w_ref[...], g, m_ref[...], v_ref[...])

    spec = pl.BlockSpec((None, tr, c), lambda l, i: (l, i, 0))
    part_spec = lambda idx: pl.BlockSpec((N_DEV, tr, c), lambda l, i: (0, jnp.where(l == idx, i, 0), 0))
    return pl.pallas_call(
        body,
        out_shape=tuple(_sds((nl, r, c), F32) for _ in range(4)),
        grid=(nl, r // tr),
        in_specs=[spec] * 3 + [part_spec(idx) for idx in range(nl)],
        out_specs=(spec,) * 4,
        compiler_params=_cp(("arbitrary", "arbitrary")),
        name=name,
    )(w, m, v, *parts)


def adamw(w, g, m, v, name):
    r, c = w.shape
    tr = _pick(r, (256, 160, 128, 64, 32, 16, 8))

    def body(w_ref, g_ref, m_ref, v_ref, d_ref, nm_ref, nv_ref):
        d_ref[...], nm_ref[...], nv_ref[...] = _adam_update(w_ref[...], g_ref[...], m_ref[...], v_ref[...])

    spec = pl.BlockSpec((tr, c), lambda i: (i, 0))
    return pl.pallas_call(
        body,
        out_shape=tuple(_sds((r, c), F32) for _ in range(3)),
        grid=(r // tr,),
        in_specs=[spec] * 4,
        out_specs=(spec, spec, spec),
        compiler_params=_cp(("parallel",)),
        name=name,
    )(w, g, m, v)


def _relu2_epi(acc):
    r = jnp.maximum(acc, 0.0)
    return r * r, r


def _drelu2_epi(acc, r):
    return (acc * (2.0 * r.astype(F32)),)


def mlp_fwd(h, g_pre, g_post, w_up, w_down):
    hn = norm_fwd(h, g_pre, BF, name="mlp_norm")
    u, r = matmul(hn, w_up, "nn", (BF, BF), epi=_relu2_epi, name="mlp_up")
    d = matmul(u, w_down, "nn", name="mlp_down")
    return resadd_fwd(h, d, g_post, name="mlp_res"), (h, hn, u, r, d)


def mlp_bwd(res, dh2, g_pre, g_post, w_up, w_down):
    h, hn, u, r, d = res
    dd, dg_post = norm_bwd(d, g_post, dh2, mask_pad=True, out_dtype=BF, name="mlp_post_bwd")
    dw_down = matmul(u, dd, "tn", (BF,), name="mlp_dwdown").reshape(w_down.g8.shape)
    dp = matmul(dd, w_down, "nt", (BF,), epi=_drelu2_epi, extras=(r,), name="mlp_du")
    dw_up = matmul(hn, dp, "tn", (BF,), out_blocks=True, name="mlp_dwup")
    dhn = matmul(dp, w_up, "nt", name="mlp_dhn")
    dh, dg_pre = norm_bwd(h, g_pre, dhn, dres=dh2, name="mlp_pre_bwd")
    return dh, dict(mlp_pre_g=dg_pre, mlp_post_g=dg_post, w_up=dw_up, w_down=dw_down)


def rg_layer_fwd(h, g_pre, g_post, w_x, w_y, rgp, w_a, w_i, w_out):
    hn = norm_fwd(h, g_pre, BF, name="rg_norm")
    xr = matmul(hn, w_x, "nn", name="rg_in_x")
    gp = matmul(hn, w_y, "nn", name="rg_in_y")
    hg, hs = rg_fwd(xr, gp, rgp, w_a, w_i)
    m = matmul(hg, w_out, "nn", name="rg_out")
    return resadd_fwd(h, m, g_post, name="rg_res"), (h, hn, xr, gp, hg, hs, m)


def rg_layer_bwd(res, dh2, g_pre, g_post, w_x, w_y, rgp, w_a, w_i, w_out):
    h, hn, xr, gp, hg, hs, m = res
    dm, dg_post = norm_bwd(m, g_post, dh2, mask_pad=True, out_dtype=BF, name="rg_post_bwd")
    dw_out = matmul(hg, dm, "tn", name="rg_dwout")
    dhg = matmul(dm, w_out, "nt", name="rg_dhg")
    dxr, dgp, drgp, dwa, dwi = rg_bwd(xr, gp, rgp, w_a, w_i, hs, dhg)
    dw_x = matmul(hn, dxr, "tn", name="rg_dwx")
    dw_y = matmul(hn, dgp, "tn", name="rg_dwy")
    dhn = matmul([dxr, dgp], [w_x, w_y], "nt", name="rg_dhn")
    dh, dg_pre = norm_bwd(h, g_pre, dhn, dres=dh2, name="rg_pre_bwd")
    return dh, dict(mix_pre_g=dg_pre, mix_post_g=dg_post, rg_w_x=dw_x, rg_w_y=dw_y,
                    rg_conv_w=drgp[0:4], rg_conv_b=drgp[4], rg_b_a=drgp[5], rg_b_i=drgp[6], rg_lambda=drgp[7],
                    rg_w_a=dwa, rg_w_i=dwi, rg_w_out=dw_out)


def sm_layer_fwd(h, g_pre, g_post, w_in_p, convp, dt_bias, a_log, dskip, ssd_g, q_g, w_q_p, kv_g, w_kv_p, w_out, tabs,
                 carry=None, on_carried=None):
    hn = norm_fwd(h, g_pre, BF, name="sm_norm")
    proj = matmul(hn, w_in_p, "nn", name="sm_in")
    xbc_c = conv_silu_fwd(proj, OFF_XBC, SSD_CONV_CH, convp, name="ssd_conv")
    y, hst = ssd_fwd(xbc_c, proj, dt_bias, a_log, dskip)
    y_ssd = gated_norm_fwd(y, proj, ssd_g)
    cqn = norm_fwd(proj, q_g, BF, col_blk=OFF_CQ // MLA_Q_RANK, width=MLA_Q_RANK, name="q_norm")
    q_raw = matmul(cqn, w_q_p, "nn", name="q_up")
    ckvn = norm_fwd(proj, kv_g, BF, col_blk=OFF_CKV // MLA_KV_RANK, width=MLA_KV_RANK, name="kv_norm")
    kv_raw = matmul(ckvn, w_kv_p, "nn", name="kv_up")
    q_cat, k_cat, v = rope_fwd(q_raw, kv_raw, proj, tabs)
    o, lse, carried = attn_fwd(q_cat, k_cat, v, carry)
    if on_carried is not None:
        on_carried(carried)
    w_out = w_out()
    half = w_out.shape[0] // 2
    m = matmul([y_ssd, o], [KBlock(w_out, half, 0), KBlock(w_out, half, 1)], "nn", name="sm_out")
    res = (h, hn, proj, xbc_c, y, hst, cqn, ckvn, q_cat, k_cat, v, o, lse, y_ssd, m)
    return resadd_fwd(h, m, g_post, name="sm_res"), res


def sm_layer_bwd(res, dh2, g_pre, g_post, w_in_p, convp, dt_bias, a_log, dskip, ssd_g, q_g, w_q_p, kv_g, w_kv_p, w_out, tabs,
                 carry=None):
    h, hn, proj, xbc_c, y, hst, cqn, ckvn, q_cat, k_cat, v, o, lse, y_ssd, m = res
    w_out = w_out()
    dm, dg_post = norm_bwd(m, g_post, dh2, mask_pad=True, out_dtype=BF, name="sm_post_bwd")
    dw_out = jnp.concatenate([matmul(y_ssd, dm, "tn", name="sm_dwout_ssd"), matmul(o, dm, "tn", name="sm_dwout_att")], axis=0)
    dyab = matmul(dm, w_out, "nt", name="sm_dyab")
    dq_cat, dk_cat, dv, carried = attn_bwd(q_cat, k_cat, v, o, lse, dyab, carry(dw_out) if carry is not None else None)
    dq_raw, dkr = rope_bwd(dq_cat, dk_cat, tabs)
    kw = MLA_HEADS * LANE
    dw_kv_p = jnp.concatenate([matmul(ckvn, dk_cat, "tn", name="kv_dw_k"), matmul(ckvn, dv, "tn", name="kv_dw_v")], axis=1)
    dckvn = matmul([dk_cat, dv], [KBlock(w_kv_p, kw, 0), KBlock(w_kv_p, kw // 2, 2)], "nt", name="kv_dx")
    dckv, dg_kv = norm_bwd(proj, kv_g, dckvn, out_dtype=BF, col_blk=OFF_CKV // MLA_KV_RANK, width=MLA_KV_RANK,
                           name="kv_norm_bwd")
    dw_q_p = matmul(cqn, dq_raw, "tn", name="q_dw")
    dcqn = matmul(dq_raw, w_q_p, "nt", name="q_dx")
    dcq, dg_q = norm_bwd(proj, q_g, dcqn, out_dtype=BF, col_blk=OFF_CQ // MLA_Q_RANK, width=MLA_Q_RANK, name="q_norm_bwd")
    dy, dz, dg_ssd = gated_norm_bwd(y, proj, ssd_g, dyab)
    dxbc_c, ddt, dpar = ssd_bwd(xbc_c, proj, dt_bias, a_log, dskip, hst, dy)
    dxbc, dconvp = conv_silu_bwd(proj, OFF_XBC, SSD_CONV_CH, convp, dxbc_c, name="ssd_conv_bwd")
    pieces = [dz, dxbc, dckv, ddt, dkr, dcq]
    dw_in_p = jnp.concatenate([matmul(hn, pc, "tn", (BF,), name="sm_dwin_%d" % i) for i, pc in enumerate(pieces)], axis=1)
    third = SSD_CONV_CH // 3
    a_terms = [dz] + [KBlock(dxbc, third, i) for i in range(3)] + [dckv, ddt, dkr, dcq]
    b_terms = ([KBlock(w_in_p, SSD_D_INNER, 0)] + [KBlock(w_in_p, third, OFF_XBC // third + i) for i in range(3)]
               + [KBlock(w_in_p, MLA_KV_RANK, OFF_CKV // MLA_KV_RANK), KBlock(w_in_p, LANE, OFF_DT // LANE),
                  KBlock(w_in_p, LANE, OFF_KR // LANE), KBlock(w_in_p, MLA_Q_RANK, OFF_CQ // MLA_Q_RANK)])
    dhn = matmul(a_terms, b_terms, "nt", name="sm_dhn")
    dh, dg_pre = norm_bwd(h, g_pre, dhn, dres=dh2, name="sm_pre_bwd")
    grads = dict(mix_pre_g=dg_pre, mix_post_g=dg_post, w_in=w_in_cols_to_blocks(dw_in_p), ssd_conv_w=dconvp[0:4],
                 ssd_conv_b=dconvp[4], ssd_dt_bias=dpar[0, :SSD_HEADS], ssd_a_log=dpar[1, :SSD_HEADS],
                 ssd_d=dpar[2, :SSD_HEADS], ssd_norm_g=dg_ssd, mla_q_norm_g=dg_q, mla_w_q_up=_unpack_w_q(dw_q_p),
                 mla_kv_norm_g=dg_kv, mla_w_kv_up=_unpack_w_kv(dw_kv_p), w_out_ab=dw_out)
    return dh, grads, carried


W_IN_COLS = 3248
W_IN_SHARD = W_IN_COLS // N_DEV
W_IN_WIRE = 512


def _w_in_tables():
    src = np.full((IN_W,), -1, np.int64)
    src[0:2560] = np.arange(2560)
    src[OFF_CKV:OFF_CKV + 256] = 2960 + np.arange(256)
    src[OFF_DT:OFF_DT + 16] = 2560 + np.arange(16)
    src[OFF_KR + 64:OFF_KR + 96] = 3216 + np.arange(32)
    src[OFF_CQ:OFF_CQ + 384] = 2576 + np.arange(384)
    dev = np.where(src >= 0, src // W_IN_SHARD, -1).astype(np.int32).reshape(1, IN_W)
    col = np.where(src >= 0, src % W_IN_SHARD, 0).astype(np.int32).reshape(1, IN_W)
    return dev, col


W_IN_TILE = 384


def _w_in_devices_of_tile(dev):
    return [sorted(set(dev[0, t * W_IN_TILE:(t + 1) * W_IN_TILE].tolist()) - {-1}) for t in range(IN_W // W_IN_TILE)]


def _any_of(index, values):
    cond = index == values[0]
    for v in values[1:]:
        cond = cond | (index == v)
    return cond


def w_in_blocks_to_cols(g8):
    _, k, wp = g8.shape
    tn = W_IN_TILE
    dev, col = _w_in_tables()
    holders = _w_in_devices_of_tile(dev)

    def body(g_ref, dev_ref, col_ref, o_ref):
        i = pl.program_id(0)
        row = lax.broadcasted_iota(jnp.int32, (wp, tn), 0)
        o_ref[...] = jnp.zeros_like(o_ref)
        for j in range(N_DEV):
            tiles = [t for t, devs in enumerate(holders) if j in devs]
            if tiles:
                @pl.when(_any_of(i, tiles))
                def _(j=j):
                    sel = ((row == col_ref[...]) & (dev_ref[...] == j)).astype(BF)
                    o_ref[...] += jnp.dot(g_ref[j], sel, preferred_element_type=F32).astype(o_ref.dtype)

    dev, col = jnp.asarray(dev), jnp.asarray(col)
    return pl.pallas_call(
        body,
        out_shape=_sds((k, IN_W), BF),
        grid=(IN_W // tn,),
        in_specs=[pl.BlockSpec((N_DEV, k, wp), lambda i: (0, 0, 0)), pl.BlockSpec((1, tn), lambda i: (0, i)),
                  pl.BlockSpec((1, tn), lambda i: (0, i))],
        out_specs=pl.BlockSpec((k, tn), lambda i: (0, i)),
        compiler_params=_cp(("parallel",)),
        name="w_in_cols",
    )(g8, dev, col)


def w_in_cols_to_blocks(dw):
    k = dw.shape[0]
    tn = W_IN_TILE
    dev, col = _w_in_tables()
    holders = _w_in_devices_of_tile(dev)

    def body(dw_ref, dev_ref, col_ref, o_ref):
        j = pl.program_id(0)
        row = lax.broadcasted_iota(jnp.int32, (W_IN_WIRE, tn), 0)
        o_ref[...] = jnp.zeros_like(o_ref)
        for t, devs in enumerate(holders):
            if devs:
                @pl.when(_any_of(j, devs))
                def _(t=t):
                    cols = slice(t * tn, (t + 1) * tn)
                    sel = ((row == col_ref[:, cols]) & (dev_ref[:, cols] == j)).astype(BF)
                    o_ref[0] += lax.dot_general(dw_ref[:, cols], sel, NT_DIMS,
                                                preferred_element_type=F32).astype(o_ref.dtype)

    dev, col = jnp.asarray(dev), jnp.asarray(col)
    return pl.pallas_call(
        body,
        out_shape=_sds((N_DEV, k, W_IN_WIRE), BF),
        grid=(N_DEV,),
        in_specs=[pl.BlockSpec((k, IN_W), lambda j: (0, 0)), pl.BlockSpec((1, IN_W), lambda j: (0, 0)),
                  pl.BlockSpec((1, IN_W), lambda j: (0, 0))],
        out_specs=pl.BlockSpec((1, k, W_IN_WIRE), lambda j: (j, 0, 0)),
        compiler_params=_cp(("parallel",)),
        name="w_in_blocks",
    )(dw, dev, col)


def _pack_w_q(w):
    w3 = w.reshape(w.shape[0], MLA_HEADS, MLA_NOPE + MLA_ROPE)
    return jnp.pad(w3, ((0, 0), (0, 0), (0, LANE - MLA_NOPE - MLA_ROPE))).reshape(w.shape[0], MLA_HEADS * LANE)


def _unpack_w_q(p):
    return p.reshape(p.shape[0], MLA_HEADS, LANE)[:, :, :MLA_NOPE + MLA_ROPE].reshape(p.shape[0], -1)


def _pack_w_kv(w):
    w3 = w.reshape(w.shape[0], MLA_HEADS, MLA_NOPE + MLA_V)
    k = jnp.pad(w3[:, :, :MLA_NOPE], ((0, 0), (0, 0), (0, LANE - MLA_NOPE))).reshape(w.shape[0], MLA_HEADS * LANE)
    return jnp.concatenate([k, w3[:, :, MLA_NOPE:].reshape(w.shape[0], MLA_HEADS * MLA_V)], axis=1)


def _unpack_w_kv(p):
    k = p[:, :MLA_HEADS * LANE].reshape(p.shape[0], MLA_HEADS, LANE)[:, :, :MLA_NOPE]
    v = p[:, MLA_HEADS * LANE:].reshape(p.shape[0], MLA_HEADS, MLA_V)
    return jnp.concatenate([k, v], axis=2).reshape(p.shape[0], -1)


def _rows8(rows, width):
    a = jnp.concatenate([r.reshape(-1, width) for r in rows], axis=0)
    return jnp.pad(a, ((0, 8 - a.shape[0]), (0, 0)))


SLAB_ROWS = 16


def _to_slab(flat_list, lead=()):
    cat = jnp.concatenate(flat_list, axis=-1)
    n = cat.shape[-1]
    unit = SLAB_ROWS * PACK_W
    total = -(-n // unit) * unit
    cat = jnp.pad(cat, [(0, 0)] * len(lead) + [(0, total - n)])
    return cat.reshape(lead + (total // PACK_W, PACK_W))


def _from_flat(flat, shapes):
    out, off = [], 0
    for s in shapes:
        n = int(np.prod(s))
        out.append(flat[off:off + n].reshape(s))
        off += n
    return out


def _gathered_full(g8, axis):
    moved = jnp.moveaxis(g8, 0, axis)
    shp = moved.shape
    return moved.reshape(shp[:axis] + (shp[axis] * shp[axis + 1],) + shp[axis + 2:])


def _per_device(full, axis):
    shp = full.shape
    split = full.reshape(shp[:axis] + (N_DEV, shp[axis] // N_DEV) + shp[axis + 1:])
    return jnp.moveaxis(split, axis, 0)


ARG_NAMES = ['x', 'meta_tokens', 'mix_pre_g', 'mix_post_g', 'mlp_pre_g', 'mlp_post_g', 'w_up', 'w_down', 'w_in',
             'ssd_conv_w', 'ssd_conv_b', 'ssd_dt_bias', 'ssd_a_log', 'ssd_d', 'ssd_norm_g', 'mla_q_norm_g',
             'mla_w_q_up', 'mla_kv_norm_g', 'mla_w_kv_up', 'w_out_ab', 'rg_w_x', 'rg_w_y', 'rg_conv_w', 'rg_conv_b',
             'rg_w_a', 'rg_b_a', 'rg_w_i', 'rg_b_i', 'rg_lambda', 'rg_w_out']
WEIGHTS = ARG_NAMES[1:]
BIG = {'w_up': 2, 'w_down': 1, 'w_in': 2, 'mla_w_q_up': 2, 'mla_w_kv_up': 2, 'w_out_ab': 1, 'rg_w_x': 2,
       'rg_w_y': 2, 'rg_w_out': 1}
SMALL = {'meta_tokens': 1, 'ssd_conv_w': 2, 'rg_conv_w': 2, 'rg_conv_b': 1, 'rg_b_a': 1, 'rg_b_i': 1, 'rg_lambda': 1}
REPL = [n for n in WEIGHTS if n not in BIG and n not in SMALL]
REPL_MEDIUM = ['rg_w_a', 'rg_w_i']
REPL_TINY = [n for n in REPL if n not in REPL_MEDIUM]


def _piece_axes():
    axes = {}
    for n, ax in BIG.items():
        for i in range(DEPTH if n in ('w_up', 'w_down') else DEPTH // 2):
            axes[(n, i)] = ax - 1
    return axes


PIECE_AXIS = _piece_axes()
AS_BLOCKS = ('w_up', 'w_down')
_SM = lambda i: [(n, i) for n in ('w_in', 'mla_w_q_up', 'mla_w_kv_up', 'w_out_ab')]
_RG = lambda i: [(n, i) for n in ('rg_w_x', 'rg_w_y', 'rg_w_out')]
_MLP = lambda l: [('w_up', l), ('w_down', l)]
_SM_IN = lambda i: [(n, i) for n in ('w_in', 'mla_w_q_up', 'mla_w_kv_up')]
GATHER_FIRST = _SM_IN(0)
GATHER_AT = {0: [('w_out_ab', 0)] + _MLP(0) + _RG(0) + _MLP(1) + _SM(1), 2: _MLP(2) + _RG(1) + _MLP(3)}
SCATTER_AT = {2: _MLP(3) + _RG(1) + _MLP(2) + [('w_out_ab', 1)],
              0: _SM_IN(1) + _MLP(1) + _RG(0) + _MLP(0) + [('w_out_ab', 0)]}
SCATTER_LAST = _SM_IN(0)


def _wire_block(p, key):
    n, i = key
    blk = p[n][i]
    if n == 'w_in':
        blk = jnp.pad(blk, ((0, 0), (0, W_IN_WIRE - blk.shape[1])))
    return blk


def _step(p, moments):
    assert DEPTH == 4
    full = {n: [None] * p[n].shape[0] for n in BIG}
    full['w_in_g'] = [None] * p['w_in'].shape[0]

    def weight_blocks(group):
        return [_wire_block(p, k).astype(BF) for k in group]

    def take_weights(group, gathered):
        for (n, i), piece in zip(group, gathered):
            if n == 'w_in':
                full['w_in_g'][i] = piece
            elif n in AS_BLOCKS:
                full[n][i] = DevBlocks(piece, PIECE_AXIS[(n, i)])
            else:
                full[n][i] = _gathered_full(piece, PIECE_AXIS[(n, i)])

    def grad_blocks(group, gw):
        return [gw[k] if k[0] in AS_BLOCKS or k[0] == 'w_in' else _per_device(gw[k], PIECE_AXIS[k]).astype(BF)
                for k in group]

    parts = {}

    small_slab = _to_slab([p[n].reshape(-1) for n in SMALL])
    *first, small8 = all_gather(weight_blocks(GATHER_FIRST) + [small_slab], name="gather_first")
    take_weights(GATHER_FIRST, first)
    for n, piece in zip(SMALL, _from_flat_rows(small8, [p[n].shape for n in SMALL])):
        full[n] = _gathered_full(piece, SMALL[n])
    for n in REPL:
        full[n] = p[n]
    loss_local, grad_x, gw, gsmall_full, carried = _local_step(
        full, p['x'][0], p['loss_target'][0],
        fwd_carry=lambda layer: Exchange("gather", weight_blocks(GATHER_AT[layer])),
        on_fwd_carried=lambda layer, got: take_weights(GATHER_AT[layer], got),
        bwd_carry=lambda layer, gw_now, others: Exchange(
            "scatter", grad_blocks(SCATTER_AT[layer], gw_now)
            + ([jnp.stack(others[n], axis=0).reshape(N_DEV, -1, LANE) for n in REPL_MEDIUM] if layer == 0 else [])))

    for layer, group in SCATTER_AT.items():
        parts.update(zip(group, carried[layer]))
    rep_flat = jnp.concatenate([gsmall_full[n].reshape(-1) for n in REPL_TINY])
    rep_n = rep_flat.shape[0]
    rep_chunk = -(-rep_n // (N_DEV * PACK_W * 8)) * PACK_W * 8
    rep8 = jnp.pad(rep_flat, (0, N_DEV * rep_chunk - rep_n)).reshape(N_DEV, rep_chunk)
    gsmall = _to_slab([_per_device(gsmall_full[n], SMALL[n]).reshape(N_DEV, -1) for n in SMALL] + [rep8], lead=(N_DEV,))
    received = all_to_all(grad_blocks(SCATTER_LAST, gw) + [gsmall], name="scatter_last")
    n_last = len(SCATTER_LAST)
    parts.update(zip(SCATTER_LAST, received[:n_last]))
    ssmall = slab_sum(received[n_last], name="sum_small").reshape(-1)
    medium_mine = [slab_sum(r8, name="sum_" + n) for n, r8 in zip(REPL_MEDIUM, carried[0][len(SCATTER_AT[0]):])]
    g_loc = {'w_in': jnp.stack([slab_sum(parts[('w_in', i)], name="sum_w_in_%d" % i)[:, :W_IN_SHARD]
                                for i in range(p['w_in'].shape[0])], axis=0)}
    small_n = sum(int(np.prod(p[n].shape)) for n in SMALL)
    g_loc.update(zip(SMALL, _from_flat(ssmall, [p[n].shape for n in SMALL])))
    rep_mine = ssmall[small_n:small_n + rep_chunk].reshape(-1, PACK_W)
    rep_all, *medium_all = all_gather([rep_mine] + medium_mine, name="gather_replicated")
    g_loc.update(zip(REPL_TINY, _from_flat(rep_all.reshape(-1), [p[n].shape for n in REPL_TINY])))
    g_loc.update({n: g.reshape(p[n].shape) for n, g in zip(REPL_MEDIUM, medium_all)})

    out = {'loss': lax.psum(loss_local, ("x", "y", "c")), 'grad_x': grad_x[None]}
    small_names = list(SMALL) + REPL_TINY
    for n in list(BIG) + REPL_MEDIUM:
        shp = p[n].shape
        if n == 'w_in' or n in REPL_MEDIUM:
            v2 = lambda a: a.reshape(-1, shp[-1])
            d, nm, nv = adamw(v2(p[n]), v2(g_loc[n]), v2(moments['m_' + n]), v2(moments['v_' + n]), name="adamw_" + n)
            d, nm, nv = d.reshape(shp), nm.reshape(shp), nv.reshape(shp)
        else:
            g_loc[n], d, nm, nv = adamw_blocks(p[n], moments['m_' + n], moments['v_' + n],
                                               [parts[(n, i)] for i in range(shp[0])], name="adamw_" + n)
        out['delta_' + n], out['new_m_' + n], out['new_v_' + n] = d, nm, nv
    slab = lambda src: _to_slab([src(n).reshape(-1) for n in small_names])
    d, nm, nv = adamw(slab(lambda n: p[n]), slab(lambda n: g_loc[n]), slab(lambda n: moments['m_' + n]),
                      slab(lambda n: moments['v_' + n]), name="adamw_small")
    shapes = [p[n].shape for n in small_names]
    for key, flat in (('delta_', d), ('new_m_', nm), ('new_v_', nv)):
        for n, a in zip(small_names, _from_flat(flat.reshape(-1), shapes)):
            out[key + n] = a
    for n in WEIGHTS:
        out['grad_' + n] = g_loc[n]
    return out


def _local_step(full, x, target_rows, fwd_carry=None, on_fwd_carried=None, bwd_carry=None):
    t = PAD + N_META + x.shape[0]
    h = jnp.concatenate([jnp.zeros((PAD, D_MODEL), F32), full['meta_tokens'], x], axis=0)
    target = jnp.concatenate([jnp.zeros((PAD + N_META, D_MODEL), F32), target_rows], axis=0)
    tabs = rope_tables(t)

    def layer_args(layer):
        i = layer // 2
        if layer % 2 == 0:
            convp = _rows8([full['ssd_conv_w'][i], full['ssd_conv_b'][i]], SSD_CONV_CH)
            return (full['mix_pre_g'][layer], full['mix_post_g'][layer], w_in_blocks_to_cols(full['w_in_g'][i]), convp,
                    full['ssd_dt_bias'][i], full['ssd_a_log'][i], full['ssd_d'][i], full['ssd_norm_g'][i],
                    full['mla_q_norm_g'][i], _pack_w_q(full['mla_w_q_up'][i]), full['mla_kv_norm_g'][i],
                    _pack_w_kv(full['mla_w_kv_up'][i]), lambda: full['w_out_ab'][i], tabs)
        rgp = _rows8([full['rg_conv_w'][i], full['rg_conv_b'][i], full['rg_b_a'][i], full['rg_b_i'][i],
                      full['rg_lambda'][i]], LRU_WIDTH)
        return (full['mix_pre_g'][layer], full['mix_post_g'][layer], full['rg_w_x'][i], full['rg_w_y'][i], rgp,
                full['rg_w_a'][i], full['rg_w_i'][i], full['rg_w_out'][i])

    def mlp_args(layer):
        return (full['mlp_pre_g'][layer], full['mlp_post_g'][layer], full['w_up'][layer], full['w_down'][layer])

    saved = []
    for layer in range(DEPTH):
        la = layer_args(layer)
        if layer % 2 == 0:
            if fwd_carry is not None:
                h, res_mix = sm_layer_fwd(h, *la, carry=fwd_carry(layer),
                                          on_carried=lambda got, layer=layer: on_fwd_carried(layer, got))
            else:
                h, res_mix = sm_layer_fwd(h, *la)
        else:
            h, res_mix = rg_layer_fwd(h, *la)
        ma = mlp_args(layer)
        h, res_mlp = mlp_fwd(h, *ma)
        saved.append((la, ma, res_mix, res_mlp))
    loss_local, dh = loss_fwd_bwd(h, target)

    others = {n: [None] * len(full[n]) for n in WEIGHTS if n not in BIG and n != 'meta_tokens'}
    gw, carried = {}, {}
    for layer in reversed(range(DEPTH)):
        la, ma, res_mix, res_mlp = saved[layer]
        dh, gm = mlp_bwd(res_mlp, dh, *ma)
        if layer % 2 == 0:
            for n in ('w_up', 'w_down'):
                gw[(n, layer)] = gm[n]
            carry = None
            if bwd_carry is not None:
                carry = lambda dw_out, layer=layer: bwd_carry(layer, {**gw, ('w_out_ab', layer // 2): dw_out}, others)
            dh, gx, carried[layer] = sm_layer_bwd(res_mix, dh, *la, carry=carry)
        else:
            dh, gx = rg_layer_bwd(res_mix, dh, *la)
        for n, g in list(gm.items()) + list(gx.items()):
            i = layer if n in ('mix_pre_g', 'mix_post_g', 'mlp_pre_g', 'mlp_post_g', 'w_up', 'w_down') else layer // 2
            if n in BIG:
                gw[(n, i)] = g
            else:
                others[n][i] = g
    gothers = {n: jnp.stack(v, axis=0) for n, v in others.items()}
    gothers['meta_tokens'] = dh[PAD:PAD + N_META]
    return loss_local, dh[PAD + N_META:], gw, gothers, carried


def _from_flat_rows(g8, shapes):
    flat = g8.reshape(N_DEV, -1)
    out, off = [], 0
    for s in shapes:
        n = int(np.prod(s))
        out.append(flat[:, off:off + n].reshape((N_DEV,) + tuple(s)))
        off += n
    return out


def kernel(x, meta_tokens, mix_pre_g, mix_post_g, mlp_pre_g, mlp_post_g, w_up, w_down, w_in, ssd_conv_w, ssd_conv_b, ssd_dt_bias, ssd_a_log, ssd_d, ssd_norm_g, mla_q_norm_g, mla_w_q_up, mla_kv_norm_g, mla_w_kv_up, w_out_ab, rg_w_x, rg_w_y, rg_conv_w, rg_conv_b, rg_w_a, rg_b_a, rg_w_i, rg_b_i, rg_lambda, rg_w_out, loss_target, m_meta_tokens, m_mix_pre_g, m_mix_post_g, m_mlp_pre_g, m_mlp_post_g, m_w_up, m_w_down, m_w_in, m_ssd_conv_w, m_ssd_conv_b, m_ssd_dt_bias, m_ssd_a_log, m_ssd_d, m_ssd_norm_g, m_mla_q_norm_g, m_mla_w_q_up, m_mla_kv_norm_g, m_mla_w_kv_up, m_w_out_ab, m_rg_w_x, m_rg_w_y, m_rg_conv_w, m_rg_conv_b, m_rg_w_a, m_rg_b_a, m_rg_w_i, m_rg_b_i, m_rg_lambda, m_rg_w_out, v_meta_tokens, v_mix_pre_g, v_mix_post_g, v_mlp_pre_g, v_mlp_post_g, v_w_up, v_w_down, v_w_in, v_ssd_conv_w, v_ssd_conv_b, v_ssd_dt_bias, v_ssd_a_log, v_ssd_d, v_ssd_norm_g, v_mla_q_norm_g, v_mla_w_q_up, v_mla_kv_norm_g, v_mla_w_kv_up, v_w_out_ab, v_rg_w_x, v_rg_w_y, v_rg_conv_w, v_rg_conv_b, v_rg_w_a, v_rg_b_a, v_rg_w_i, v_rg_b_i, v_rg_lambda, v_rg_w_out):
    args = (x, meta_tokens, mix_pre_g, mix_post_g, mlp_pre_g, mlp_post_g, w_up, w_down, w_in, ssd_conv_w, ssd_conv_b, ssd_dt_bias, ssd_a_log, ssd_d, ssd_norm_g, mla_q_norm_g, mla_w_q_up, mla_kv_norm_g, mla_w_kv_up, w_out_ab, rg_w_x, rg_w_y, rg_conv_w, rg_conv_b, rg_w_a, rg_b_a, rg_w_i, rg_b_i, rg_lambda, rg_w_out, loss_target, m_meta_tokens, m_mix_pre_g, m_mix_post_g, m_mlp_pre_g, m_mlp_post_g, m_w_up, m_w_down, m_w_in, m_ssd_conv_w, m_ssd_conv_b, m_ssd_dt_bias, m_ssd_a_log, m_ssd_d, m_ssd_norm_g, m_mla_q_norm_g, m_mla_w_q_up, m_mla_kv_norm_g, m_mla_w_kv_up, m_w_out_ab, m_rg_w_x, m_rg_w_y, m_rg_conv_w, m_rg_conv_b, m_rg_w_a, m_rg_b_a, m_rg_w_i, m_rg_b_i, m_rg_lambda, m_rg_w_out, v_meta_tokens, v_mix_pre_g, v_mix_post_g, v_mlp_pre_g, v_mlp_post_g, v_w_up, v_w_down, v_w_in, v_ssd_conv_w, v_ssd_conv_b, v_ssd_dt_bias, v_ssd_a_log, v_ssd_d, v_ssd_norm_g, v_mla_q_norm_g, v_mla_w_q_up, v_mla_kv_norm_g, v_mla_w_kv_up, v_w_out_ab, v_rg_w_x, v_rg_w_y, v_rg_conv_w, v_rg_conv_b, v_rg_w_a, v_rg_b_a, v_rg_w_i, v_rg_b_i, v_rg_lambda, v_rg_w_out,)
    n_w = len(ARG_NAMES)
    p = dict(zip(ARG_NAMES, args[:n_w]))
    p['loss_target'] = args[n_w]
    moments = {}
    for i, n in enumerate(WEIGHTS):
        moments['m_' + n] = args[n_w + 1 + i]
        moments['v_' + n] = args[n_w + 1 + len(WEIGHTS) + i]
    out = _step(p, moments)
    res = [out['loss'], out['grad_x']]
    for prefix in ('grad_', 'delta_', 'new_m_', 'new_v_'):
        res += [out[prefix + n] for n in WEIGHTS]
    return tuple(res)
```

```python
import functools
import math

import numpy as np
import jax
import jax.numpy as jnp
from jax import lax
from jax.experimental import pallas as pl
from jax.experimental.pallas import tpu as pltpu

F32 = jnp.float32
BF = jnp.bfloat16
HI = lax.Precision.HIGHEST

D_MODEL = 1024
DEPTH = 4
N_META = 16
CHUNK = 128
PAD = CHUNK - N_META
EPS = 1e-6
SSD_HEADS = 16
SSD_HEAD_DIM = 64
SSD_D_INNER = 1024
SSD_STATE = 128
SSD_CONV_CH = 1536
MLA_HEADS = 16
MLA_NOPE = 64
MLA_ROPE = 32
MLA_V = 64
MLA_Q_RANK = 384
MLA_KV_RANK = 256
ROPE_BASE = 10000.0
LRU_WIDTH = 1280
LRU_BLOCKS = 10
LRU_C = 8.0
D_FF = 4096
N_DEV = 8
LANE = 128
IN_W = 3456
OFF_Z, OFF_XBC, OFF_CKV, OFF_DT, OFF_KR, OFF_CQ = 0, 1024, 2560, 2816, 2944, 3072

ADAM_LR = 0.001
ADAM_B1 = 0.9
ADAM_B2 = 0.999
ADAM_EPS = 1e-08
ADAM_WD = 0.01
ADAM_STEP = 10

VMEM_LIMIT = 56 * 1024 * 1024
NEG = -1e30


def _pick(n, cands):
    for c in cands:
        if n % c == 0:
            return c
    return n


def _cp(sem=None):
    return pltpu.CompilerParams(dimension_semantics=sem, vmem_limit_bytes=VMEM_LIMIT)


def _sds(shape, dtype):
    return jax.ShapeDtypeStruct(tuple(shape), dtype)


def _silu(x):
    return x * jax.nn.sigmoid(x)


def _softplus(x):
    return jnp.maximum(x, 0.0) + jnp.log(1.0 + jnp.exp(-jnp.abs(x)))


def _gelu(x):
    c = math.sqrt(2.0 / math.pi)
    return 0.5 * x * (1.0 + jnp.tanh(c * (x + 0.044715 * (x * x * x))))


def _row_mask(i, tr, shape, first_valid=PAD):
    row = i * tr + lax.broadcasted_iota(jnp.int32, shape, 0)
    return row >= first_valid


class KBlock:
    def __init__(self, arr, width, blk):
        self.arr, self.width, self.blk = arr, width, blk


class DevBlocks:
    def __init__(self, g8, axis):
        self.g8, self.axis = g8, axis
        _, r, c = g8.shape
        self.shape = (N_DEV * r, c) if axis == 0 else (r, N_DEV * c)


NN_DIMS = (((1,), (0,)), ((), ()))
MM_TALL_K = 1536


def matmul(a, b, mode, out_dtypes=(F32,), epi=None, extras=(), name="mm", tm=None, tn=None, out_blocks=False):
    a_terms = a if isinstance(a, (list, tuple)) else [a]
    b_terms = b if isinstance(b, (list, tuple)) else [b]
    assert len(a_terms) == len(b_terms) and (mode != "tn" or len(a_terms) == 1)
    arr_of = lambda t: t.arr if isinstance(t, KBlock) else t
    if mode == "tn":
        m, n = a_terms[0].shape[1], b_terms[0].shape[1]
    else:
        m = arr_of(a_terms[0]).shape[0]
        b0 = b_terms[0]
        n = (b0.shape if isinstance(b0, DevBlocks) else arr_of(b0).shape)[1 if mode == "nn" else 0]
    if mode == "tn":
        tm = _pick(m, (1024, 512, 384, 256, 128))
    else:
        k_all = sum(t.width if isinstance(t, KBlock) else t.shape[1] for t in a_terms)
        tall = (2112,) if k_all <= MM_TALL_K else ()
        tm = tm or _pick(m, tall + (1056, 1024, 768, 640, 512, 384, 256, 128))
    tn = tn or _pick(n, (512, 640, 384, 256, 128))
    dims = {"nn": NN_DIMS, "nt": NT_DIMS, "tn": TN_DIMS}[mode]

    in_specs, args, plan = [], [], []
    for ta, tb in zip(a_terms, b_terms):
        if mode == "tn":
            k = ta.shape[0]
            in_specs += [pl.BlockSpec((k, tm), lambda i, j: (0, i)), pl.BlockSpec((k, tn), lambda i, j: (0, j))]
            args += [ta, tb]
            plan.append(None)
            continue
        if isinstance(ta, KBlock):
            kw, ka = ta.width, ta.blk
            in_specs.append(pl.BlockSpec((tm, kw), lambda i, j, ka=ka: (i, ka)))
        else:
            kw = ta.shape[1]
            in_specs.append(pl.BlockSpec((tm, kw), lambda i, j: (i, 0)))
        args.append(arr_of(ta))
        if isinstance(tb, DevBlocks):
            _, r, c = tb.g8.shape
            split_k = tb.axis == (0 if mode == "nn" else 1)
            if split_k:
                kd = r if mode == "nn" else c
                assert kw == N_DEV * kd
                blk = (N_DEV, kd, tn) if mode == "nn" else (N_DEV, tn, kd)
                in_specs.append(pl.BlockSpec(blk, (lambda i, j: (0, 0, j)) if mode == "nn" else (lambda i, j: (0, j, 0))))
                plan.append(kd)
            else:
                per = (c if mode == "nn" else r) // tn
                blk = (None, kw, tn) if mode == "nn" else (None, tn, kw)
                in_specs.append(pl.BlockSpec(blk, (lambda i, j, per=per: (j // per, 0, j % per)) if mode == "nn"
                                             else (lambda i, j, per=per: (j // per, j % per, 0))))
                plan.append(None)
            args.append(tb.g8)
        else:
            kb = tb.blk if isinstance(tb, KBlock) else 0
            assert (tb.width if isinstance(tb, KBlock) else tb.shape[0 if mode == "nn" else 1]) == kw
            in_specs.append(pl.BlockSpec((kw, tn), lambda i, j, kb=kb: (kb, j)) if mode == "nn"
                            else pl.BlockSpec((tn, kw), lambda i, j, kb=kb: (j, kb)))
            args.append(arr_of(tb))
            plan.append(None)
    n_terms, n_ex = len(plan), len(extras)

    def body(*refs):
        ex_refs, out_refs = refs[2 * n_terms:2 * n_terms + n_ex], refs[2 * n_terms + n_ex:]
        acc = None
        for t, kd in enumerate(plan):
            a_ref, b_ref = refs[2 * t], refs[2 * t + 1]
            if kd is None:
                parts = [lax.dot_general(a_ref[...].astype(BF), b_ref[...].astype(BF), dims, preferred_element_type=F32)]
            else:
                parts = [lax.dot_general(a_ref[:, d * kd:(d + 1) * kd].astype(BF), b_ref[d].astype(BF), dims,
                                         preferred_element_type=F32) for d in range(N_DEV)]
            for part in parts:
                acc = part if acc is None else acc + part
        outs = (acc,) if epi is None else epi(acc, *[r[...] for r in ex_refs])
        for r, o in zip(out_refs, outs):
            r[...] = o.astype(r.dtype)

    o_spec = pl.BlockSpec((tm, tn), lambda i, j: (i, j))
    if out_blocks:
        per = n // N_DEV // tn
        out_shape = tuple(_sds((N_DEV, m, n // N_DEV), dt) for dt in out_dtypes)
        out_specs = tuple(pl.BlockSpec((None, tm, tn), lambda i, j: (j // per, i, j % per)) for _ in out_dtypes)
    else:
        out_shape = tuple(_sds((m, n), dt) for dt in out_dtypes)
        out_specs = tuple(o_spec for _ in out_dtypes)
    outs = pl.pallas_call(
        body,
        out_shape=out_shape,
        grid=(m // tm, n // tn),
        in_specs=in_specs + [o_spec] * n_ex,
        out_specs=out_specs,
        compiler_params=_cp(("parallel", "parallel")),
        name=name,
    )(*args, *extras)
    return outs[0] if len(out_dtypes) == 1 else outs


def _rt(t):
    return _pick(t, (384, 256, 128))


def norm_fwd(x, g, out_dtype, col_blk=0, width=None, name="norm_fwd"):
    t = x.shape[0]
    w = width or x.shape[1]
    tr = _rt(t)

    def body(x_ref, g_ref, o_ref):
        xv = x_ref[...]
        r = lax.rsqrt(jnp.mean(xv * xv, axis=-1, keepdims=True) + EPS)
        o_ref[...] = (xv * r * g_ref[...]).astype(o_ref.dtype)

    return pl.pallas_call(
        body,
        out_shape=_sds((t, w), out_dtype),
        grid=(t // tr,),
        in_specs=[pl.BlockSpec((tr, w), lambda i: (i, col_blk)), pl.BlockSpec((1, w), lambda i: (0, 0))],
        out_specs=pl.BlockSpec((tr, w), lambda i: (i, 0)),
        compiler_params=_cp(("parallel",)),
        name=name,
    )(x, g.reshape(1, w))


def _rms_bwd(xv, gv, dyv):
    r = lax.rsqrt(jnp.mean(xv * xv, axis=-1, keepdims=True) + EPS)
    xh = xv * r
    dyg = dyv * gv
    dx = r * (dyg - xh * jnp.mean(dyg * xh, axis=-1, keepdims=True))
    return dx, jnp.sum(dyv * xh, axis=0, keepdims=True)


def norm_bwd(x, g, dy, dres=None, mask_pad=False, out_dtype=F32, col_blk=0, width=None, dy_col_blk=0, then=None,
             name="norm_bwd"):
    t = x.shape[0]
    w = width or x.shape[1]
    tr = _rt(t)
    has_res, has_then = dres is not None, then is not None

    def body(*refs):
        x_ref, g_ref, dy_ref = refs[:3]
        n_in = 3 + has_res + 2 * has_then
        dx_ref, dg_ref = refs[n_in:n_in + 2]
        i = pl.program_id(0)
        dyv = dy_ref[...].astype(F32)
        if mask_pad:
            dyv = jnp.where(_row_mask(i, tr, dyv.shape), dyv, 0.0)
        dx, dg = _rms_bwd(x_ref[...], g_ref[...], dyv)
        if has_res:
            dx = dx + refs[3][...]
        dx_ref[...] = dx.astype(dx_ref.dtype)

        @pl.when(i == 0)
        def _():
            for r in refs[n_in + 1::2]:
                r[...] = jnp.zeros_like(r)

        dg_ref[...] += dg
        if has_then:
            x2_ref, g2_ref = refs[3 + has_res:5 + has_res]
            dx2_ref, dg2_ref = refs[n_in + 2:]
            dx2, dg2 = _rms_bwd(x2_ref[...], g2_ref[...], jnp.where(_row_mask(i, tr, dx.shape), dx, 0.0))
            dx2_ref[...] = dx2.astype(dx2_ref.dtype)
            dg2_ref[...] += dg2

    row = pl.BlockSpec((tr, w), lambda i: (i, 0))
    vec = pl.BlockSpec((1, w), lambda i: (0, 0))
    in_specs = [pl.BlockSpec((tr, w), lambda i: (i, col_blk)), vec, pl.BlockSpec((tr, w), lambda i: (i, dy_col_blk))]
    args = [x, g.reshape(1, w), dy]
    out_shape, out_specs = [_sds((t, w), out_dtype), _sds((1, w), F32)], [row, vec]
    if has_res:
        in_specs.append(row)
        args.append(dres)
    if has_then:
        in_specs += [row, vec]
        args += [then[0], then[1].reshape(1, w)]
        out_shape += [_sds((t, w), BF), _sds((1, w), F32)]
        out_specs += [row, vec]
    outs = pl.pallas_call(
        body,
        out_shape=tuple(out_shape),
        grid=(t // tr,),
        in_specs=in_specs,
        out_specs=tuple(out_specs),
        compiler_params=_cp(("arbitrary",)),
        name=name,
    )(*args)
    if has_then:
        return outs[0], outs[1].reshape(w), outs[2], outs[3].reshape(w)
    return outs[0], outs[1].reshape(w)


def resadd_fwd(h, m, g, g_next=None, name="resadd"):
    t, w = h.shape
    tr = _rt(t)
    with_next = g_next is not None

    def body(h_ref, m_ref, g_ref, *rest):
        mv = m_ref[...]
        r = lax.rsqrt(jnp.mean(mv * mv, axis=-1, keepdims=True) + EPS)
        y = mv * r * g_ref[...]
        h2 = h_ref[...] + jnp.where(_row_mask(pl.program_id(0), tr, y.shape), y, 0.0)
        if with_next:
            gn_ref, o_ref, hn_ref = rest
            r2 = lax.rsqrt(jnp.mean(h2 * h2, axis=-1, keepdims=True) + EPS)
            hn_ref[...] = (h2 * r2 * gn_ref[...]).astype(hn_ref.dtype)
        else:
            (o_ref,) = rest
        o_ref[...] = h2

    row = pl.BlockSpec((tr, w), lambda i: (i, 0))
    vec = pl.BlockSpec((1, w), lambda i: (0, 0))
    outs = pl.pallas_call(
        body,
        out_shape=(_sds((t, w), F32),) + ((_sds((t, w), BF),) if with_next else ()),
        grid=(t // tr,),
        in_specs=[row, row, vec] + ([vec] if with_next else []),
        out_specs=(row,) + ((row,) if with_next else ()),
        compiler_params=_cp(("parallel",)),
        name=name,
    )(h, m, g.reshape(1, w), *((g_next.reshape(1, w),) if with_next else ()))
    return outs[0], (outs[1] if with_next else None)


def loss_fwd_bwd(h, target):
    t, w = h.shape
    tr = _rt(t)

    def body(h_ref, t_ref, s_ref, dh_ref):
        i = pl.program_id(0)
        err = h_ref[...] - t_ref[...]
        err = jnp.where(_row_mask(i, tr, err.shape, PAD + N_META), err, 0.0)
        dh_ref[...] = err * (1.0 / w)

        @pl.when(i == 0)
        def _():
            s_ref[...] = jnp.zeros_like(s_ref)

        s_ref[...] += jnp.sum(err * err).reshape(1, 1)

    s, dh = pl.pallas_call(
        body,
        out_shape=(_sds((1, LANE), F32), _sds((t, w), F32)),
        grid=(t // tr,),
        in_specs=[pl.BlockSpec((tr, w), lambda i: (i, 0)), pl.BlockSpec((tr, w), lambda i: (i, 0))],
        out_specs=(pl.BlockSpec((1, LANE), lambda i: (0, 0)), pl.BlockSpec((tr, w), lambda i: (i, 0))),
        compiler_params=_cp(("arbitrary",)),
        name="loss",
    )(h, target)
    return 0.5 * s[0, 0] / w, dh


def _shift_down(ext, k, n):
    return pltpu.roll(ext, k, 0)[8:]


def _conv_pre(ext, x, w_ref, n):
    return (w_ref[4:5, :] + w_ref[3:4, :] * x + w_ref[2:3, :] * _shift_down(ext, 1, n)
            + w_ref[1:2, :] * _shift_down(ext, 2, n) + w_ref[0:1, :] * _shift_down(ext, 3, n))


def _conv_bwd_parts(dpre, dnext, x, ext, w_ref, n):
    extd = jnp.concatenate([dpre, dnext], axis=0)
    ln = n + 8
    dx = (w_ref[3:4, :] * dpre + w_ref[2:3, :] * pltpu.roll(extd, ln - 1, 0)[:n]
          + w_ref[1:2, :] * pltpu.roll(extd, ln - 2, 0)[:n] + w_ref[0:1, :] * pltpu.roll(extd, ln - 3, 0)[:n])
    sums = [jnp.sum(dpre * _shift_down(ext, 3, n), axis=0, keepdims=True),
            jnp.sum(dpre * _shift_down(ext, 2, n), axis=0, keepdims=True),
            jnp.sum(dpre * _shift_down(ext, 1, n), axis=0, keepdims=True),
            jnp.sum(dpre * x, axis=0, keepdims=True),
            jnp.sum(dpre, axis=0, keepdims=True)]
    return dx, sums


def _rows_block(sums):
    w = sums[0].shape[1]
    row = lax.broadcasted_iota(jnp.int32, (8, w), 0)
    out = jnp.zeros((8, w), F32)
    for k, s in enumerate(sums):
        out = jnp.where(row == k, s, out)
    return out


CONV_BLOCK = 512


def conv_silu_fwd(x, col0, c, wb, name="conv_fwd"):
    t = x.shape[0]
    cw = _pick(c, (CONV_BLOCK, LANE))
    nblk, col0_blk = c // cw, col0 // cw
    assert col0 % cw == 0
    tr = _rt(t)

    def body(x_ref, w_ref, o_ref, prev):
        ti = pl.program_id(1)

        @pl.when(ti == 0)
        def _():
            prev[...] = jnp.zeros_like(prev)

        xv = x_ref[...]
        ext = jnp.concatenate([prev[...], xv], axis=0)
        o_ref[...] = _silu(_conv_pre(ext, xv, w_ref, tr))
        prev[...] = xv[tr - 8:, :]

    return pl.pallas_call(
        body,
        out_shape=_sds((t, c), F32),
        grid=(nblk, t // tr),
        in_specs=[pl.BlockSpec((tr, cw), lambda cb, ti: (ti, col0_blk + cb)),
                  pl.BlockSpec((8, cw), lambda cb, ti: (0, cb))],
        out_specs=pl.BlockSpec((tr, cw), lambda cb, ti: (ti, cb)),
        scratch_shapes=[pltpu.VMEM((8, cw), F32)],
        compiler_params=_cp(("parallel", "arbitrary")),
        name=name,
    )(x, wb)


def conv_silu_bwd(x, col0, c, wb, dout, name="conv_bwd"):
    t = x.shape[0]
    cw = _pick(c, (CONV_BLOCK, LANE))
    nblk, col0_blk = c // cw, col0 // cw
    assert col0 % cw == 0
    tr = _rt(t)
    nt = t // tr
    r8 = tr // 8

    def body(x_ref, xp_ref, w_ref, do_ref, dx_ref, dwb_ref, dnext):
        ti = pl.program_id(1)
        tt = nt - 1 - ti

        @pl.when(ti == 0)
        def _():
            dnext[...] = jnp.zeros_like(dnext)
            dwb_ref[...] = jnp.zeros_like(dwb_ref)

        xv = x_ref[...]
        halo = jnp.where(tt > 0, xp_ref[...], 0.0)
        ext = jnp.concatenate([halo, xv], axis=0)
        pre = _conv_pre(ext, xv, w_ref, tr)
        s = jax.nn.sigmoid(pre)
        dpre = do_ref[...] * (s + pre * s * (1.0 - s))
        dx, sums = _conv_bwd_parts(dpre, dnext[...], xv, ext, w_ref, tr)
        dx_ref[...] = dx.astype(dx_ref.dtype)
        dwb_ref[...] += _rows_block(sums)
        dnext[...] = dpre[:8, :]

    return pl.pallas_call(
        body,
        out_shape=(_sds((t, c), BF), _sds((8, c), F32)),
        grid=(nblk, nt),
        in_specs=[pl.BlockSpec((tr, cw), lambda cb, ti: (nt - 1 - ti, col0_blk + cb)),
                  pl.BlockSpec((8, cw), lambda cb, ti: (jnp.maximum((nt - 1 - ti) * r8 - 1, 0), col0_blk + cb)),
                  pl.BlockSpec((8, cw), lambda cb, ti: (0, cb)),
                  pl.BlockSpec((tr, cw), lambda cb, ti: (nt - 1 - ti, cb))],
        out_specs=(pl.BlockSpec((tr, cw), lambda cb, ti: (nt - 1 - ti, cb)),
                   pl.BlockSpec((8, cw), lambda cb, ti: (0, cb))),
        scratch_shapes=[pltpu.VMEM((8, cw), F32)],
        compiler_params=_cp(("parallel", "arbitrary")),
        name=name,
    )(x, x, wb, dout)


def gated_norm_fwd(y, proj, g, name="gnorm_fwd"):
    t, w = y.shape
    tr = _rt(t)

    def body(y_ref, z_ref, g_ref, o_ref):
        v = y_ref[...] * _silu(z_ref[...])
        r = lax.rsqrt(jnp.mean(v * v, axis=-1, keepdims=True) + EPS)
        o_ref[...] = (v * r * g_ref[...]).astype(o_ref.dtype)

    return pl.pallas_call(
        body,
        out_shape=_sds((t, w), BF),
        grid=(t // tr,),
        in_specs=[pl.BlockSpec((tr, w), lambda i: (i, 0)), pl.BlockSpec((tr, w), lambda i: (i, OFF_Z // w)),
                  pl.BlockSpec((1, w), lambda i: (0, 0))],
        out_specs=pl.BlockSpec((tr, w), lambda i: (i, 0)),
        compiler_params=_cp(("parallel",)),
        name=name,
    )(y, proj, g.reshape(1, w))


def gated_norm_bwd(y, proj, g, dyab, name="gnorm_bwd"):
    t, w = y.shape
    tr = _rt(t)

    def body(y_ref, z_ref, g_ref, do_ref, dy_ref, dz_ref, dg_ref):
        i = pl.program_id(0)
        yv, zv, dov = y_ref[...], z_ref[...], do_ref[...]
        s = jax.nn.sigmoid(zv)
        sz = zv * s
        v = yv * sz
        r = lax.rsqrt(jnp.mean(v * v, axis=-1, keepdims=True) + EPS)
        vh = v * r
        dvg = dov * g_ref[...]
        dv = r * (dvg - vh * jnp.mean(dvg * vh, axis=-1, keepdims=True))
        dy_ref[...] = dv * sz
        dz_ref[...] = (dv * yv * (s + sz * (1.0 - s))).astype(dz_ref.dtype)

        @pl.when(i == 0)
        def _():
            dg_ref[...] = jnp.zeros_like(dg_ref)

        dg_ref[...] += jnp.sum(dov * vh, axis=0, keepdims=True)

    dy, dz, dg = pl.pallas_call(
        body,
        out_shape=(_sds((t, w), F32), _sds((t, w), BF), _sds((1, w), F32)),
        grid=(t // tr,),
        in_specs=[pl.BlockSpec((tr, w), lambda i: (i, 0)), pl.BlockSpec((tr, w), lambda i: (i, OFF_Z // w)),
                  pl.BlockSpec((1, w), lambda i: (0, 0)), pl.BlockSpec((tr, w), lambda i: (i, 0))],
        out_specs=(pl.BlockSpec((tr, w), lambda i: (i, 0)), pl.BlockSpec((tr, w), lambda i: (i, 0)),
                   pl.BlockSpec((1, w), lambda i: (0, 0))),
        compiler_params=_cp(("arbitrary",)),
        name=name,
    )(y, proj, g.reshape(1, w), dyab)
    return dy, dz, dg.reshape(w)


def rope_tables(t):
    inv = ROPE_BASE ** (-jnp.arange(0, MLA_ROPE, 2, dtype=F32) / MLA_ROPE)
    pos = (jnp.arange(t, dtype=F32) - PAD)[:, None]
    ang = pos * inv[None, :]
    cos, sin = jnp.cos(ang), jnp.sin(ang)
    z16 = jnp.zeros((t, 16), F32)
    z32 = jnp.zeros((t, 32), F32)
    c = jnp.concatenate([jnp.ones((t, 64), F32), cos, cos, z32], axis=1)
    s1 = jnp.concatenate([jnp.zeros((t, 64), F32), z16, sin, z32], axis=1)
    s2 = jnp.concatenate([jnp.zeros((t, 64), F32), -sin, z16, z32], axis=1)
    return c, s1, s2


def _rope(x, c, s1, s2):
    return x * c + pltpu.roll(x, 16, 1) * s1 + pltpu.roll(x, LANE - 16, 1) * s2


def _rope_t(d, c, s1, s2):
    return d * c + pltpu.roll(d * s1, LANE - 16, 1) + pltpu.roll(d * s2, 16, 1)


def rope_fwd(q_raw, kv_raw, proj, tabs):
    t = q_raw.shape[0]
    tr = _rt(t)
    hw = MLA_HEADS * LANE

    def body(q_ref, k_ref, v_ref, kr_ref, c_ref, s1_ref, s2_ref, qo_ref, ko_ref, vo_ref):
        c, s1, s2 = c_ref[...], s1_ref[...], s2_ref[...]
        kr = _rope(kr_ref[...], c, s1, s2)
        for h in range(MLA_HEADS):
            sl = slice(h * LANE, (h + 1) * LANE)
            qo_ref[:, sl] = (_rope(q_ref[:, sl], c, s1, s2) * Q_PRESCALE).astype(BF)
            ko_ref[:, sl] = (k_ref[:, sl] + kr).astype(BF)
        vo_ref[...] = v_ref[...].astype(BF)

    tab_spec = pl.BlockSpec((tr, LANE), lambda i: (i, 0))
    return pl.pallas_call(
        body,
        out_shape=(_sds((t, hw), BF), _sds((t, hw), BF), _sds((t, 1024), BF)),
        grid=(t // tr,),
        in_specs=[pl.BlockSpec((tr, hw), lambda i: (i, 0)), pl.BlockSpec((tr, hw), lambda i: (i, 0)),
                  pl.BlockSpec((tr, 1024), lambda i: (i, 2)), pl.BlockSpec((tr, LANE), lambda i: (i, OFF_KR // LANE)),
                  tab_spec, tab_spec, tab_spec],
        out_specs=(pl.BlockSpec((tr, hw), lambda i: (i, 0)), pl.BlockSpec((tr, hw), lambda i: (i, 0)),
                   pl.BlockSpec((tr, 1024), lambda i: (i, 0))),
        compiler_params=_cp(("parallel",)),
        name="rope_fwd",
    )(q_raw, kv_raw, kv_raw, proj, *tabs)


def rope_bwd(dq_cat, dk_cat, tabs):
    t = dq_cat.shape[0]
    tr = _rt(t)
    hw = MLA_HEADS * LANE

    def body(dq_ref, dk_ref, c_ref, s1_ref, s2_ref, dqo_ref, dkr_ref):
        c, s1, s2 = c_ref[...], s1_ref[...], s2_ref[...]
        acc = jnp.zeros((tr, LANE), F32)
        for h in range(MLA_HEADS):
            sl = slice(h * LANE, (h + 1) * LANE)
            dqo_ref[:, sl] = _rope_t(dq_ref[:, sl] * ATT_SCALE, c, s1, s2).astype(BF)
            acc = acc + dk_ref[:, sl]
        lane = lax.broadcasted_iota(jnp.int32, (tr, LANE), 1)
        dkr_ref[...] = jnp.where((lane >= 64) & (lane < 96), _rope_t(acc, c, s1, s2), 0.0)

    tab_spec = pl.BlockSpec((tr, LANE), lambda i: (i, 0))
    return pl.pallas_call(
        body,
        out_shape=(_sds((t, hw), BF), _sds((t, LANE), F32)),
        grid=(t // tr,),
        in_specs=[pl.BlockSpec((tr, hw), lambda i: (i, 0)), pl.BlockSpec((tr, hw), lambda i: (i, 0)),
                  tab_spec, tab_spec, tab_spec],
        out_specs=(pl.BlockSpec((tr, hw), lambda i: (i, 0)), pl.BlockSpec((tr, LANE), lambda i: (i, 0))),
        compiler_params=_cp(("parallel",)),
        name="rope_bwd",
    )(dq_cat, dk_cat, *tabs)


ATT_SCALE = (MLA_NOPE + MLA_ROPE) ** -0.5
LOG2E = math.log2(math.e)
Q_PRESCALE = ATT_SCALE * LOG2E
CARRY_MIDDLE_PAIR = 6
NT_DIMS = (((1,), (1,)), ((), ()))
TN_DIMS = (((0,), (0,)), ((), ()))


def _att_mask(qi, ki, tq, tk):
    qpos = qi * tq + lax.broadcasted_iota(jnp.int32, (tq, tk), 0)
    kpos = ki * tk + lax.broadcasted_iota(jnp.int32, (tq, tk), 1)
    return (kpos <= qpos) & (kpos >= PAD)


def _half_masks(n):
    lane = lax.broadcasted_iota(jnp.int32, (n, LANE), 1)
    return lane < 64, lane >= 64


def _att_tile(t):
    return _pick(t, (384, 256, 128))


def _ds(i, n):
    return pl.ds(i * n, n) if isinstance(i, int) else pl.ds(pl.multiple_of(i * n, n), n)


FWD_PAIRS = 2


def attn_fwd(q_cat, k_cat, v, carry=None):
    t = q_cat.shape[0]
    tq = tk = _att_tile(t)
    nq = t // tq
    npair, nh = FWD_PAIRS, 2 * FWD_PAIRS
    n_grp = MLA_HEADS // nh
    nx = carry.k if carry else 0

    def body(*refs):
        q_ref, k_ref, v_ref = refs[:3]
        o_ref, lse_ref = refs[3 + nx:5 + nx]
        qi = pl.program_id(1)
        if carry:
            start, middle, finish = carry.phases(refs[3:3 + nx], refs[5 + nx:5 + 2 * nx], refs[5 + 2 * nx:])
            grp = pl.program_id(0)
            pl.when((grp == 0) & (qi == 0))(start)
            pl.when((grp == CARRY_MIDDLE_PAIR // npair) & (qi == 0))(middle)
        lo_q, _ = _half_masks(tq)
        halves = _half_masks(tk)

        def step(ki, state, masked):
            m_old, l_old, accs = state[0:nh], state[nh:2 * nh], state[2 * nh:]
            rows = _ds(ki, tk)
            ss = [lax.dot_general(q_ref[:, h * LANE:(h + 1) * LANE], k_ref[rows, h * LANE:(h + 1) * LANE], NT_DIMS,
                                  preferred_element_type=F32) for h in range(nh)]
            if masked:
                valid = _att_mask(qi, ki, tq, tk)
                ss = [jnp.where(valid, s, NEG) for s in ss]
            m_new = [jnp.maximum(m_old[h], jnp.max(ss[h], axis=-1, keepdims=True)) for h in range(nh)]
            ps = [jnp.exp2(ss[h] - m_new[h]) for h in range(nh)]
            alpha = [jnp.exp2(m_old[h] - m_new[h]) for h in range(nh)]
            l_new = [alpha[h] * l_old[h] + jnp.sum(ps[h], axis=-1, keepdims=True) for h in range(nh)]
            new_accs = []
            for pp in range(npair):
                vv = v_ref[rows, pp * LANE:(pp + 1) * LANE]
                pv = [jnp.dot(ps[2 * pp + hh].astype(BF), jnp.where(halves[hh], vv, jnp.zeros_like(vv)),
                              preferred_element_type=F32) for hh in range(2)]
                new_accs.append(accs[pp] * jnp.where(lo_q, alpha[2 * pp], alpha[2 * pp + 1]) + pv[0] + pv[1])
            return tuple(m_new) + tuple(l_new) + tuple(new_accs)

        neg, zero = jnp.full((tq, 1), NEG, F32), jnp.zeros((tq, 1), F32)
        state = step(0, (neg,) * nh + (zero,) * nh + (jnp.zeros((tq, LANE), F32),) * npair, True)
        state = lax.fori_loop(1, qi, lambda ki, st: step(ki, st, False), state)
        state = lax.cond(qi > 0, lambda st: step(qi, st, True), lambda st: st, state)
        for pp in range(npair):
            l = jnp.where(lo_q, state[nh + 2 * pp], state[nh + 2 * pp + 1])
            o_ref[:, pp * LANE:(pp + 1) * LANE] = (state[2 * nh + pp] / l).astype(o_ref.dtype)
            lse_ref[:, pp * LANE:(pp + 1) * LANE] = jnp.where(lo_q, state[2 * pp], state[2 * pp + 1]) + jnp.log2(l)
        if carry:
            pl.when((grp == n_grp - 1) & (qi == nq - 1))(finish)

    outs = pl.pallas_call(
        body,
        out_shape=(_sds((t, 1024), BF), _sds((t, 1024), F32)) + tuple(carry.out_shapes() if carry else ()),
        grid=(n_grp, nq),
        in_specs=[pl.BlockSpec((tq, nh * LANE), lambda g, qi: (qi, g)),
                  pl.BlockSpec((t, nh * LANE), lambda g, qi: (0, g)),
                  pl.BlockSpec((t, npair * LANE), lambda g, qi: (0, g))] + [ANY] * nx,
        out_specs=(pl.BlockSpec((tq, npair * LANE), lambda g, qi: (qi, g)),
                   pl.BlockSpec((tq, npair * LANE), lambda g, qi: (qi, g))) + (ANY,) * nx,
        scratch_shapes=carry.scratch() if carry else [],
        compiler_params=_cp(("arbitrary", "arbitrary") if carry else ("parallel", "parallel")),
        name="attn_fwd_carrying" if carry else "attn_fwd",
    )(q_cat, k_cat, v, *(carry.arrs if carry else ()))
    return outs[0], outs[1], list(outs[2:])


def attn_bwd(q_cat, k_cat, v, o, lse, dyab, carry=None):
    t = q_cat.shape[0]
    tq = tk = _att_tile(t)
    nq = t // tq
    n_pair = MLA_HEADS // 2
    nx = carry.k if carry else 0

    def body(*refs):
        q_ref, k_ref, v_ref, o_ref, lse_ref, do_ref = refs[:6]
        dq_ref, dk_ref, dv_ref = refs[6 + nx:9 + nx]
        ki = pl.program_id(1)
        if carry:
            start, middle, finish = carry.phases(refs[6:6 + nx], refs[9 + nx:9 + 2 * nx], refs[9 + 2 * nx:])
            pair = pl.program_id(0)
            pl.when((pair == 0) & (ki == 0))(start)
            pl.when((pair == CARRY_MIDDLE_PAIR) & (ki == 0))(middle)

        @pl.when(ki == 0)
        def _():
            dq_ref[...] = jnp.zeros_like(dq_ref)

        halves = _half_masks(tq)
        vv = v_ref[...]
        kk = [k_ref[:, hh * LANE:(hh + 1) * LANE] for hh in range(2)]

        def step(qi, acc, masked):
            rows = _ds(qi, tq)
            dov, ov, lse_v = do_ref[rows, :], o_ref[rows, :].astype(F32), lse_ref[rows, :]
            qh = [q_ref[rows, hh * LANE:(hh + 1) * LANE] for hh in range(2)]
            ss = [lax.dot_general(qh[hh], kk[hh], NT_DIMS, preferred_element_type=F32) for hh in range(2)]
            if masked:
                valid = _att_mask(qi, ki, tq, tk)
                ss = [jnp.where(valid, s, NEG) for s in ss]
            ps = [jnp.exp2(ss[hh] - lse_v[:, 64 * hh:64 * hh + 1]) for hh in range(2)]
            dom = [jnp.where(halves[hh], dov, 0.0) for hh in range(2)]
            delta = [jnp.sum(dom[hh] * ov, axis=-1, keepdims=True) for hh in range(2)]
            dom = [d.astype(BF) for d in dom]
            dp = [lax.dot_general(dom[hh], vv, NT_DIMS, preferred_element_type=F32) for hh in range(2)]
            ds = [(ps[hh] * (dp[hh] - delta[hh])).astype(BF) for hh in range(2)]
            pb = [p.astype(BF) for p in ps]
            dv = (acc[2] + lax.dot_general(pb[0], dom[0], TN_DIMS, preferred_element_type=F32)
                  + lax.dot_general(pb[1], dom[1], TN_DIMS, preferred_element_type=F32))
            dk = [acc[hh] + lax.dot_general(ds[hh], qh[hh], TN_DIMS, preferred_element_type=F32) for hh in range(2)]
            for hh in range(2):
                dq_ref[rows, hh * LANE:(hh + 1) * LANE] += jnp.dot(ds[hh], kk[hh], preferred_element_type=F32)
            return dk[0], dk[1], dv

        zero = jnp.zeros((tk, LANE), F32)
        acc = step(ki, (zero, zero, zero), True)
        acc = lax.fori_loop(ki + 1, jnp.where(ki == 0, nq, ki + 1), lambda qi, a: step(qi, a, True), acc)
        acc = lax.fori_loop(ki + 1, jnp.where(ki == 0, ki + 1, nq), lambda qi, a: step(qi, a, False), acc)
        dk_ref[:, 0:LANE] = acc[0] * (1.0 / LOG2E)
        dk_ref[:, LANE:2 * LANE] = acc[1] * (1.0 / LOG2E)
        dv_ref[...] = acc[2]
        if carry:
            pl.when((pair == n_pair - 1) & (ki == nq - 1))(finish)

    full = lambda w, off=0: pl.BlockSpec((t, w), lambda p, ki: (0, p + off))
    blk = lambda w: pl.BlockSpec((tk, w), lambda p, ki: (ki, p))
    outs = pl.pallas_call(
        body,
        out_shape=(_sds((t, 2048), F32), _sds((t, 2048), F32), _sds((t, 1024), F32))
        + tuple(carry.out_shapes() if carry else ()),
        grid=(n_pair, nq),
        in_specs=[full(2 * LANE), blk(2 * LANE), blk(LANE), full(LANE), full(LANE), full(LANE, 8)] + [ANY] * nx,
        out_specs=(full(2 * LANE), blk(2 * LANE), blk(LANE)) + (ANY,) * nx,
        scratch_shapes=carry.scratch() if carry else [],
        compiler_params=_cp(("arbitrary", "arbitrary") if carry else ("parallel", "arbitrary")),
        name="attn_bwd_carrying" if carry else "attn_bwd",
    )(q_cat, k_cat, v, o, lse, dyab, *(carry.arrs if carry else ()))
    return outs[0], outs[1], outs[2], list(outs[3:])


N_PAIR = SSD_HEADS // 2


def _hdot(a, b):
    return jnp.dot(a, b, precision=HI, preferred_element_type=F32)


def _ssd_chunk(xs, bg, cg, dtraw, hin, dt_bias, a_log, dskip, rowmask):
    ln = CHUNK
    causal = lax.broadcasted_iota(jnp.int32, (ln, ln), 0) >= lax.broadcasted_iota(jnp.int32, (ln, ln), 1)
    ltri = causal.astype(F32)
    lane = lax.broadcasted_iota(jnp.int32, (ln, LANE), 1)
    halves = (lane < 64, lane >= 64)
    low_row = lax.broadcasted_iota(jnp.int32, (1, LANE), 1) < 64
    head_lane = lax.broadcasted_iota(jnp.int32, (1, SSD_HEADS), 1)
    head_row = lax.broadcasted_iota(jnp.int32, (SSD_HEADS, 1), 0)

    def col(a, h):
        return jnp.sum(jnp.where(head_lane == h, a, 0.0), axis=1, keepdims=True)

    dt = _softplus(dtraw + dt_bias) * rowmask
    da = dt * (-jnp.exp(a_log))
    acs = _hdot(ltri, da)
    acs_t = lax.dot_general(da, ltri, (((0,), (1,)), ((), ())), precision=HI, preferred_element_type=F32)
    tot = jnp.sum(da, axis=0, keepdims=True)
    bm = [b * rowmask for b in bg]
    cm = [c * rowmask for c in cg]
    cb = [lax.dot_general(cm[g].astype(BF), bm[g].astype(BF), NT_DIMS, preferred_element_type=F32) for g in range(2)]
    ys, hout = [], []
    for p in range(N_PAIR):
        g = p // (N_PAIR // 2)
        h0, h1 = 2 * p, 2 * p + 1
        xdt = xs[p] * jnp.where(halves[0], col(dt, h0), col(dt, h1))
        a_cols = [col(acs, h0), col(acs, h1)]
        tot_cols = [col(tot, h0), col(tot, h1)]
        y = jnp.zeros((ln, LANE), F32)
        snew = jnp.zeros((ln, LANE), F32)
        for hh in range(2):
            a_row = jnp.sum(jnp.where(head_row == h0 + hh, acs_t, 0.0), axis=0, keepdims=True)
            dec = jnp.exp(jnp.where(causal, a_cols[hh] - a_row, NEG))
            xm = jnp.where(halves[hh], xdt, 0.0).astype(BF)
            y = y + jnp.dot((cb[g] * dec).astype(BF), xm, preferred_element_type=F32)
            bd = bm[g] * jnp.exp(tot_cols[hh] - a_cols[hh])
            snew = snew + lax.dot_general(bd.astype(BF), xm, TN_DIMS, preferred_element_type=F32)
        y_off = (jnp.dot(cm[g].astype(BF), hin[p].astype(BF), preferred_element_type=F32)
                 * jnp.where(halves[0], jnp.exp(a_cols[0]), jnp.exp(a_cols[1])))
        ys.append(y + y_off + jnp.where(low_row, col(dskip, h0), col(dskip, h1)) * xs[p])
        hout.append(jnp.where(low_row, jnp.exp(tot_cols[0]), jnp.exp(tot_cols[1])) * hin[p] + snew)
    return ys, hout


def _ssd_load(x_ref, dt_ref):
    xs = [x_ref[:, p * LANE:(p + 1) * LANE] for p in range(N_PAIR)]
    bg = [x_ref[:, SSD_D_INNER + g * LANE:SSD_D_INNER + (g + 1) * LANE] for g in range(2)]
    cg = [x_ref[:, SSD_D_INNER + (2 + g) * LANE:SSD_D_INNER + (3 + g) * LANE] for g in range(2)]
    return xs, bg, cg, dt_ref[:, 0:SSD_HEADS]


def _chunk_rowmask(c):
    return ((c * CHUNK + lax.broadcasted_iota(jnp.int32, (CHUNK, 1), 0)) >= PAD).astype(F32)


def ssd_fwd(xbc_c, proj, dt_bias, a_log, dskip):
    t = xbc_c.shape[0]
    nc = t // CHUNK

    def body(x_ref, dt_ref, dtb_ref, al_ref, d_ref, y_ref, hs_ref, h_s):
        c = pl.program_id(0)

        @pl.when(c == 0)
        def _():
            h_s[...] = jnp.zeros_like(h_s)

        xs, bg, cg, dtraw = _ssd_load(x_ref, dt_ref)
        hin = [h_s[p] for p in range(N_PAIR)]
        hs_ref[0] = h_s[...]
        ys, hout = _ssd_chunk(xs, bg, cg, dtraw, hin, dtb_ref[...], al_ref[...], d_ref[...], _chunk_rowmask(c))
        for p in range(N_PAIR):
            y_ref[:, p * LANE:(p + 1) * LANE] = ys[p]
            h_s[p] = hout[p]

    par = pl.BlockSpec((1, SSD_HEADS), lambda c: (0, 0))
    return pl.pallas_call(
        body,
        out_shape=(_sds((t, SSD_D_INNER), F32), _sds((nc, N_PAIR, CHUNK, LANE), F32)),
        grid=(nc,),
        in_specs=[pl.BlockSpec((CHUNK, SSD_CONV_CH), lambda c: (c, 0)),
                  pl.BlockSpec((CHUNK, LANE), lambda c: (c, OFF_DT // LANE)), par, par, par],
        out_specs=(pl.BlockSpec((CHUNK, SSD_D_INNER), lambda c: (c, 0)),
                   pl.BlockSpec((1, N_PAIR, CHUNK, LANE), lambda c: (c, 0, 0, 0))),
        scratch_shapes=[pltpu.VMEM((N_PAIR, CHUNK, LANE), F32)],
        compiler_params=_cp(("arbitrary",)),
        name="ssd_fwd",
    )(xbc_c, proj, dt_bias.reshape(1, -1), a_log.reshape(1, -1), dskip.reshape(1, -1))


def ssd_bwd(xbc_c, proj, dt_bias, a_log, dskip, hs, dy):
    t = xbc_c.shape[0]
    nc = t // CHUNK

    def body(x_ref, dt_ref, dtb_ref, al_ref, d_ref, hs_ref, dy_ref, dx_ref, ddt_ref, dpar_ref, dh_s):
        ci = pl.program_id(0)
        c = nc - 1 - ci

        @pl.when(ci == 0)
        def _():
            dh_s[...] = jnp.zeros_like(dh_s)
            dpar_ref[...] = jnp.zeros_like(dpar_ref)

        xs, bg, cg, dtraw = _ssd_load(x_ref, dt_ref)
        hin = [hs_ref[0, p] for p in range(N_PAIR)]
        rowmask = _chunk_rowmask(c)
        fn = lambda xs_, bg_, cg_, dtraw_, hin_, dtb_, al_, d_: _ssd_chunk(xs_, bg_, cg_, dtraw_, hin_, dtb_, al_, d_, rowmask)
        _, vjp = jax.vjp(fn, xs, bg, cg, dtraw, hin, dtb_ref[...], al_ref[...], d_ref[...])
        dys = [dy_ref[:, p * LANE:(p + 1) * LANE] for p in range(N_PAIR)]
        dhs = [dh_s[p] for p in range(N_PAIR)]
        dxs, dbg, dcg, ddtraw, dhin, ddtb, dal, dd = vjp((dys, dhs))
        for p in range(N_PAIR):
            dx_ref[:, p * LANE:(p + 1) * LANE] = dxs[p]
            dh_s[p] = dhin[p]
        for g in range(2):
            dx_ref[:, SSD_D_INNER + g * LANE:SSD_D_INNER + (g + 1) * LANE] = dbg[g]
            dx_ref[:, SSD_D_INNER + (2 + g) * LANE:SSD_D_INNER + (3 + g) * LANE] = dcg[g]
        ddt_ref[...] = jnp.zeros_like(ddt_ref)
        ddt_ref[:, 0:SSD_HEADS] = ddtraw
        dpar_ref[0:1, 0:SSD_HEADS] += ddtb
        dpar_ref[1:2, 0:SSD_HEADS] += dal
        dpar_ref[2:3, 0:SSD_HEADS] += dd

    par = pl.BlockSpec((1, SSD_HEADS), lambda ci: (0, 0))
    return pl.pallas_call(
        body,
        out_shape=(_sds((t, SSD_CONV_CH), F32), _sds((t, LANE), F32), _sds((8, LANE), F32)),
        grid=(nc,),
        in_specs=[pl.BlockSpec((CHUNK, SSD_CONV_CH), lambda ci: (nc - 1 - ci, 0)),
                  pl.BlockSpec((CHUNK, LANE), lambda ci: (nc - 1 - ci, OFF_DT // LANE)), par, par, par,
                  pl.BlockSpec((1, N_PAIR, CHUNK, LANE), lambda ci: (nc - 1 - ci, 0, 0, 0)),
                  pl.BlockSpec((CHUNK, SSD_D_INNER), lambda ci: (nc - 1 - ci, 0))],
        out_specs=(pl.BlockSpec((CHUNK, SSD_CONV_CH), lambda ci: (nc - 1 - ci, 0)),
                   pl.BlockSpec((CHUNK, LANE), lambda ci: (nc - 1 - ci, 0)),
                   pl.BlockSpec((8, LANE), lambda ci: (0, 0))),
        scratch_shapes=[pltpu.VMEM((N_PAIR, CHUNK, LANE), F32)],
        compiler_params=_cp(("arbitrary",)),
        name="ssd_bwd",
    )(xbc_c, proj, dt_bias.reshape(1, -1), a_log.reshape(1, -1), dskip.reshape(1, -1), hs, dy)


def _neg_expm1(y):
    series = -(y * (1.0 + y * (0.5 + y * (1.0 / 6.0 + y * (1.0 / 24.0 + y * (1.0 / 120.0))))))
    return jnp.where(y > -0.1, series, 1.0 - jnp.exp(y))


def _rg_pw(xr, wa, ba, wi, bi, lam, rowmask):
    xb = xr.astype(BF)
    r = jax.nn.sigmoid(jnp.dot(xb, wa.astype(BF), preferred_element_type=F32) + ba)
    i = jax.nn.sigmoid(jnp.dot(xb, wi.astype(BF), preferred_element_type=F32) + bi)
    log_a = -LRU_C * r * _softplus(-lam)
    a = jnp.exp(log_a)
    u = jnp.sqrt(_neg_expm1(2.0 * log_a)) * (i * xr) * rowmask
    return a, u


def _gelu_grad(x):
    c = math.sqrt(2.0 / math.pi)
    th = jnp.tanh(c * (x + 0.044715 * (x * x * x)))
    return 0.5 * (1.0 + th) + 0.5 * x * (1.0 - th * th) * c * (1.0 + 3.0 * 0.044715 * x * x)


def _scan_fwd(a, u):
    n = a.shape[0]
    row = lax.broadcasted_iota(jnp.int32, a.shape, 0)
    s = 1
    while s < n:
        a_s = jnp.where(row >= s, pltpu.roll(a, s, 0), 1.0)
        u_s = jnp.where(row >= s, pltpu.roll(u, s, 0), 0.0)
        u = u + a * u_s
        a = a * a_s
        s *= 2
    return a, u


def _scan_bwd(b, d):
    n = b.shape[0]
    row = lax.broadcasted_iota(jnp.int32, b.shape, 0)
    s = 1
    while s < n:
        b_s = jnp.where(row < n - s, pltpu.roll(b, n - s, 0), 1.0)
        d_s = jnp.where(row < n - s, pltpu.roll(d, n - s, 0), 0.0)
        d = d + b * d_s
        b = b * b_s
        s *= 2
    return d


def rg_fwd(xr_pre, gate_pre, rgp, w_a, w_i):
    t = xr_pre.shape[0]
    tr = _rt(t)

    def body(x_ref, g_ref, p_ref, wa_ref, wi_ref, hg_ref, hs_ref, prev, hcar):
        ti = pl.program_id(1)

        @pl.when(ti == 0)
        def _():
            prev[...] = jnp.zeros_like(prev)
            hcar[...] = jnp.zeros_like(hcar)

        xv = x_ref[...]
        ext = jnp.concatenate([prev[...], xv], axis=0)
        xr = _conv_pre(ext, xv, p_ref, tr)
        rowmask = _row_mask(ti, tr, (tr, 1)).astype(F32)
        a, u = _rg_pw(xr, wa_ref[0], p_ref[5:6, :], wi_ref[0], p_ref[6:7, :], p_ref[7:8, :], rowmask)
        a_cum, h_loc = _scan_fwd(a, u)
        hs = h_loc + a_cum * hcar[0:1, :]
        hs_ref[...] = hs
        hg_ref[...] = (hs * _gelu(g_ref[...])).astype(hg_ref.dtype)
        hcar[...] = jnp.broadcast_to(hs[tr - 1:tr, :], (8, LANE))
        prev[...] = xv[tr - 8:, :]

    return pl.pallas_call(
        body,
        out_shape=(_sds((t, LRU_WIDTH), BF), _sds((t, LRU_WIDTH), F32)),
        grid=(LRU_BLOCKS, t // tr),
        in_specs=[pl.BlockSpec((tr, LANE), lambda n, ti: (ti, n)),
                  pl.BlockSpec((tr, LANE), lambda n, ti: (ti, n)),
                  pl.BlockSpec((8, LANE), lambda n, ti: (0, n)),
                  pl.BlockSpec((1, LANE, LANE), lambda n, ti: (n, 0, 0)),
                  pl.BlockSpec((1, LANE, LANE), lambda n, ti: (n, 0, 0))],
        out_specs=(pl.BlockSpec((tr, LANE), lambda n, ti: (ti, n)), pl.BlockSpec((tr, LANE), lambda n, ti: (ti, n))),
        scratch_shapes=[pltpu.VMEM((8, LANE), F32), pltpu.VMEM((8, LANE), F32)],
        compiler_params=_cp(("parallel", "arbitrary")),
        name="rg_fwd",
    )(xr_pre, gate_pre, rgp, w_a, w_i)


def rg_bwd(xr_pre, gate_pre, rgp, w_a, w_i, hs, dhg):
    t = xr_pre.shape[0]
    tr = _rt(t)
    nt = t // tr
    r8 = tr // 8

    def body(x_ref, xp_ref, g_ref, p_ref, wa_ref, wi_ref, hs_ref, hp_ref, dhg_ref,
             dx_ref, dg_ref, dp_ref, dwa_ref, dwi_ref, gcar, dnext):
        ti = pl.program_id(1)
        tt = nt - 1 - ti

        @pl.when(ti == 0)
        def _():
            gcar[...] = jnp.zeros_like(gcar)
            dnext[...] = jnp.zeros_like(dnext)
            dp_ref[...] = jnp.zeros_like(dp_ref)
            dwa_ref[...] = jnp.zeros_like(dwa_ref)
            dwi_ref[...] = jnp.zeros_like(dwi_ref)

        xv = x_ref[...]
        halo = jnp.where(tt > 0, xp_ref[...], 0.0)
        ext = jnp.concatenate([halo, xv], axis=0)
        xr = _conv_pre(ext, xv, p_ref, tr)
        rowmask = _row_mask(tt, tr, (tr, 1)).astype(F32)
        fn = lambda xr_, wa_, ba_, wi_, bi_, lam_: _rg_pw(xr_, wa_, ba_, wi_, bi_, lam_, rowmask)
        (a, _), vjp = jax.vjp(fn, xr, wa_ref[0], p_ref[5:6, :], wi_ref[0], p_ref[6:7, :], p_ref[7:8, :])
        gpre = g_ref[...]
        hsv = hs_ref[...]
        dhg_v = dhg_ref[...]
        dg_ref[...] = (dhg_v * hsv * _gelu_grad(gpre)).astype(dg_ref.dtype)
        row = lax.broadcasted_iota(jnp.int32, (tr, LANE), 0)
        d = dhg_v * _gelu(gpre) + jnp.where(row == tr - 1, gcar[0:1, :], 0.0)
        b = jnp.where(row < tr - 1, pltpu.roll(a, tr - 1, 0), 0.0)
        g = _scan_bwd(b, d)
        gcar[...] = jnp.broadcast_to(a[0:1, :] * g[0:1, :], (8, LANE))
        hlast = jnp.where(tt > 0, hp_ref[7:8, :], 0.0)
        hprev = jnp.where(row == 0, hlast, pltpu.roll(hsv, 1, 0))
        dxr, dwa, dba, dwi, dbi, dlam = vjp((g * hprev, g))
        dx, sums = _conv_bwd_parts(dxr, dnext[...], xv, ext, p_ref, tr)
        dx_ref[...] = dx.astype(dx_ref.dtype)
        dnext[...] = dxr[:8, :]
        dp_ref[...] += _rows_block(sums + [dba, dbi, dlam])
        dwa_ref[0] += dwa
        dwi_ref[0] += dwi

    tile = lambda off=0: pl.BlockSpec((tr, LANE), lambda n, ti: (nt - 1 - ti, off + n))
    halo = lambda off=0: pl.BlockSpec((8, LANE), lambda n, ti: (jnp.maximum((nt - 1 - ti) * r8 - 1, 0), off + n))
    par = pl.BlockSpec((8, LANE), lambda n, ti: (0, n))
    wspec = pl.BlockSpec((1, LANE, LANE), lambda n, ti: (n, 0, 0))
    return pl.pallas_call(
        body,
        out_shape=(_sds((t, LRU_WIDTH), BF), _sds((t, LRU_WIDTH), BF), _sds((8, LRU_WIDTH), F32),
                   _sds((LRU_BLOCKS, LANE, LANE), F32), _sds((LRU_BLOCKS, LANE, LANE), F32)),
        grid=(LRU_BLOCKS, nt),
        in_specs=[tile(), halo(), tile(), par, wspec, wspec, tile(), halo(), tile()],
        out_specs=(tile(), tile(), par, wspec, wspec),
        scratch_shapes=[pltpu.VMEM((8, LANE), F32), pltpu.VMEM((8, LANE), F32)],
        compiler_params=_cp(("parallel", "arbitrary")),
        name="rg_bwd",
    )(xr_pre, xr_pre, gate_pre, rgp, w_a, w_i, hs, hs, dhg)


PACK_W = 1024
MESH_ID = pl.DeviceIdType.MESH
ANY = pl.BlockSpec(memory_space=pl.ANY)


def _my_place():
    x, y, c = lax.axis_index("x"), lax.axis_index("y"), lax.axis_index("c")
    return x, y, c


def _lin(px, py, pc):
    return 4 * px + 2 * py + pc


class Exchange:
    def __init__(self, kind, arrs):
        self.kind, self.arrs, self.k = kind, list(arrs), len(arrs)

    def out_shapes(self):
        if self.kind == "gather":
            return [_sds((N_DEV,) + a.shape, a.dtype) for a in self.arrs]
        return [_sds(a.shape, a.dtype) for a in self.arrs]

    def scratch(self):
        k = self.k
        return [pltpu.SemaphoreType.DMA((k, 7)), pltpu.SemaphoreType.DMA((k, 7)), pltpu.SemaphoreType.DMA((k,))]

    def phases(self, ins, outs, sems):
        return (self._gather if self.kind == "gather" else self._scatter)(ins, outs, *sems)

    def _gather(self, ins, outs, send_sems, recv_sems, local_sems):
        k = self.k
        x, y, c = _my_place()
        me, sibling = (x, y, c), (x, y, 1 - c)
        chips = [(1 - x, y), (x, 1 - y), (1 - x, 1 - y)]

        def copy(a, sem, block, to, from_input=False):
            slab = outs[a].at[_lin(*block)]
            return pltpu.make_async_remote_copy(
                src_ref=ins[a] if from_input else slab, dst_ref=slab,
                send_sem=send_sems.at[a, sem], recv_sem=recv_sems.at[a, sem],
                device_id=to, device_id_type=MESH_ID)

        def mine():
            return [pltpu.make_async_copy(ins[a], outs[a].at[_lin(*me)], local_sems.at[a]) for a in range(k)]

        def first():
            out = []
            for a in range(k):
                out.append(copy(a, 0, me, sibling, True))
                out += [copy(a, 1 + j, me, (*chip, c), True) for j, chip in enumerate(chips)]
            return out

        def passed():
            return [copy(a, 4 + j, (*chip, c), sibling) for j, chip in enumerate(chips) for a in range(k)]

        def start():
            for cp in mine() + first():
                cp.start()

        def middle():
            onward = passed()
            for j, chip in enumerate(chips):
                for a in range(k):
                    copy(a, 1 + j, (*chip, c), me).wait_recv()
                    onward[j * k + a].start()

        def finish():
            for a in range(k):
                copy(a, 0, sibling, me).wait_recv()
                for j, chip in enumerate(chips):
                    copy(a, 4 + j, (*chip, 1 - c), me).wait_recv()
            for cp in first() + passed():
                cp.wait_send()
            for cp in mine():
                cp.wait()

        return start, middle, finish

    def _scatter(self, ins, outs, send_sems, recv_sems, local_sems):
        k = self.k
        x, y, c = _my_place()
        me = _lin(x, y, c)
        peers = [((1 - x) if r & 4 else x, (1 - y) if r & 2 else y, (1 - c) if r & 1 else c) for r in range(1, N_DEV)]

        def copy(a, r, src_slab, dst_slab, to):
            return pltpu.make_async_remote_copy(
                src_ref=ins[a].at[src_slab], dst_ref=outs[a].at[dst_slab],
                send_sem=send_sems.at[a, r], recv_sem=recv_sems.at[a, r],
                device_id=to, device_id_type=MESH_ID)

        def mine():
            return [pltpu.make_async_copy(ins[a].at[me], outs[a].at[me], local_sems.at[a]) for a in range(k)]

        def sends():
            return [copy(a, r, _lin(*peer), me, peer) for r, peer in enumerate(peers) for a in range(k)]

        def start():
            for cp in mine() + sends():
                cp.start()

        def middle():
            pass

        def finish():
            for r, peer in enumerate(peers):
                for a in range(k):
                    copy(a, r, me, _lin(*peer), peer).wait_recv()
            for cp in sends():
                cp.wait_send()
            for cp in mine():
                cp.wait()

        return start, middle, finish

    def run(self, name):
        k = self.k

        def body(*refs):
            start, middle, finish = self.phases(refs[:k], refs[k:2 * k], refs[2 * k:])
            start()
            middle()
            finish()

        return pl.pallas_call(
            body,
            out_shape=tuple(self.out_shapes()),
            in_specs=[ANY] * k,
            out_specs=tuple(ANY for _ in range(k)),
            scratch_shapes=self.scratch(),
            name=name,
        )(*self.arrs)


def all_gather(arrs, name):
    return Exchange("gather", arrs).run(name)


def all_to_all(arrs, name):
    return Exchange("scatter", arrs).run(name)


def slab_sum(a, name):
    _, r, w = a.shape
    tr = _pick(r, (256, 128, 64, 32, 16, 8))

    def body(a_ref, o_ref):
        acc = a_ref[0].astype(F32)
        for d in range(1, N_DEV):
            acc = acc + a_ref[d].astype(F32)
        o_ref[...] = acc

    return pl.pallas_call(
        body,
        out_shape=_sds((r, w), F32),
        grid=(r // tr,),
        in_specs=[pl.BlockSpec((N_DEV, tr, w), lambda i: (0, i, 0))],
        out_specs=pl.BlockSpec((tr, w), lambda i: (i, 0)),
        compiler_params=_cp(("parallel",)),
        name=name,
    )(a)


def _adam_update(w, g, m, v):
    nm = ADAM_B1 * m + (1.0 - ADAM_B1) * g
    nv = ADAM_B2 * v + (1.0 - ADAM_B2) * (g * g)
    m_hat = nm / (1.0 - ADAM_B1 ** ADAM_STEP)
    v_hat = nv / (1.0 - ADAM_B2 ** ADAM_STEP)
    return -ADAM_LR * (m_hat / (jnp.sqrt(v_hat) + ADAM_EPS) + ADAM_WD * w), nm, nv


def adamw_blocks(w, m, v, parts, name):
    nl, r, c = w.shape
    tr = next(t for t in (256, 160, 128, 64, 32, 16) if r % t == 0 and N_DEV * t * c * 2 <= 2 * 1024 * 1024)

    def body(w_ref, m_ref, v_ref, *rest):
        part_refs, (g_ref, d_ref, nm_ref, nv_ref) = rest[:nl], rest[nl:]
        layer = pl.program_id(0)
        for idx in range(nl):
            @pl.when(layer == idx)
            def _(idx=idx):
                g = part_refs[idx][0].astype(F32)
                for dev in range(1, N_DEV):
                    g = g + part_refs[idx][dev].astype(F32)
                g_ref[...] = g
                d_ref[...], nm_ref[...], nv_ref[...] = _adam_update(w_ref[...], g, m_ref[...], v_ref[...])

    spec = pl.BlockSpec((None, tr, c), lambda l, i: (l, i, 0))
    part_spec = lambda idx: pl.BlockSpec((N_DEV, tr, c), lambda l, i: (0, jnp.where(l == idx, i, 0), 0))
    return pl.pallas_call(
        body,
        out_shape=tuple(_sds((nl, r, c), F32) for _ in range(4)),
        grid=(nl, r // tr),
        in_specs=[spec] * 3 + [part_spec(idx) for idx in range(nl)],
        out_specs=(spec,) * 4,
        compiler_params=_cp(("arbitrary", "arbitrary")),
        name=name,
    )(w, m, v, *parts)


def adamw(w, g, m, v, name):
    r, c = w.shape
    tr = _pick(r, (256, 160, 128, 64, 32, 16, 8))

    def body(w_ref, g_ref, m_ref, v_ref, d_ref, nm_ref, nv_ref):
        d_ref[...], nm_ref[...], nv_ref[...] = _adam_update(w_ref[...], g_ref[...], m_ref[...], v_ref[...])

    spec = pl.BlockSpec((tr, c), lambda i: (i, 0))
    return pl.pallas_call(
        body,
        out_shape=tuple(_sds((r, c), F32) for _ in range(3)),
        grid=(r // tr,),
        in_specs=[spec] * 4,
        out_specs=(spec, spec, spec),
        compiler_params=_cp(("parallel",)),
        name=name,
    )(w, g, m, v)


def _relu2_epi(acc):
    r = jnp.maximum(acc, 0.0)
    return r * r, r


def _drelu2_epi(acc, r):
    return (acc * (2.0 * r.astype(F32)),)


def mlp_fwd(h, g_pre, g_post, w_up, w_down, hn=None, g_next=None):
    if hn is None:
        hn = norm_fwd(h, g_pre, BF, name="mlp_norm")
    u, r = matmul(hn, w_up, "nn", (BF, BF), epi=_relu2_epi, name="mlp_up")
    d = matmul(u, w_down, "nn", name="mlp_down")
    h2, hn_next = resadd_fwd(h, d, g_post, g_next, name="mlp_res")
    return h2, (h, hn, u, r, d), hn_next


def mlp_bwd(res, dh2, g_pre, g_post, w_up, w_down, post=None, then=None):
    h, hn, u, r, d = res
    dd, dg_post = post if post is not None else norm_bwd(d, g_post, dh2, mask_pad=True, out_dtype=BF, name="mlp_post_bwd")
    dw_down = matmul(u, dd, "tn", (BF,), name="mlp_dwdown").reshape(w_down.g8.shape)
    dp = matmul(dd, w_down, "nt", (BF,), epi=_drelu2_epi, extras=(r,), name="mlp_du")
    dw_up = matmul(hn, dp, "tn", (BF,), out_blocks=True, name="mlp_dwup")
    dhn = matmul(dp, w_up, "nt", name="mlp_dhn")
    dh, dg_pre, *below = norm_bwd(h, g_pre, dhn, dres=dh2, then=then, name="mlp_pre_bwd")
    return dh, dict(mlp_pre_g=dg_pre, mlp_post_g=dg_post, w_up=dw_up, w_down=dw_down), (tuple(below) or None)


def rg_layer_fwd(h, g_pre, g_post, w_x, w_y, rgp, w_a, w_i, w_out, hn=None, g_next=None):
    if hn is None:
        hn = norm_fwd(h, g_pre, BF, name="rg_norm")
    xr = matmul(hn, w_x, "nn", name="rg_in_x")
    gp = matmul(hn, w_y, "nn", name="rg_in_y")
    hg, hs = rg_fwd(xr, gp, rgp, w_a, w_i)
    m = matmul(hg, w_out, "nn", name="rg_out")
    h2, hn_next = resadd_fwd(h, m, g_post, g_next, name="rg_res")
    return h2, (h, hn, xr, gp, hg, hs, m), hn_next


def rg_layer_bwd(res, dh2, g_pre, g_post, w_x, w_y, rgp, w_a, w_i, w_out, post=None, then=None):
    h, hn, xr, gp, hg, hs, m = res
    dm, dg_post = post if post is not None else norm_bwd(m, g_post, dh2, mask_pad=True, out_dtype=BF, name="rg_post_bwd")
    dw_out = matmul(hg, dm, "tn", name="rg_dwout")
    dhg = matmul(dm, w_out, "nt", name="rg_dhg")
    dxr, dgp, drgp, dwa, dwi = rg_bwd(xr, gp, rgp, w_a, w_i, hs, dhg)
    dw_x = matmul(hn, dxr, "tn", name="rg_dwx")
    dw_y = matmul(hn, dgp, "tn", name="rg_dwy")
    dhn = matmul([dxr, dgp], [w_x, w_y], "nt", name="rg_dhn")
    dh, dg_pre, *below = norm_bwd(h, g_pre, dhn, dres=dh2, then=then, name="rg_pre_bwd")
    grads = dict(mix_pre_g=dg_pre, mix_post_g=dg_post, rg_w_x=dw_x, rg_w_y=dw_y,
                 rg_conv_w=drgp[0:4], rg_conv_b=drgp[4], rg_b_a=drgp[5], rg_b_i=drgp[6], rg_lambda=drgp[7],
                 rg_w_a=dwa, rg_w_i=dwi, rg_w_out=dw_out)
    return dh, grads, (tuple(below) or None)


def sm_layer_fwd(h, g_pre, g_post, w_in_p, convp, dt_bias, a_log, dskip, ssd_g, q_g, w_q_p, kv_g, w_kv_p, w_out, tabs,
                 carry=None, on_carried=None, hn=None, g_next=None):
    if hn is None:
        hn = norm_fwd(h, g_pre, BF, name="sm_norm")
    proj = matmul(hn, w_in_p, "nn", name="sm_in")
    xbc_c = conv_silu_fwd(proj, OFF_XBC, SSD_CONV_CH, convp, name="ssd_conv")
    y, hst = ssd_fwd(xbc_c, proj, dt_bias, a_log, dskip)
    y_ssd = gated_norm_fwd(y, proj, ssd_g)
    cqn = norm_fwd(proj, q_g, BF, col_blk=OFF_CQ // MLA_Q_RANK, width=MLA_Q_RANK, name="q_norm")
    q_raw = matmul(cqn, w_q_p, "nn", name="q_up")
    ckvn = norm_fwd(proj, kv_g, BF, col_blk=OFF_CKV // MLA_KV_RANK, width=MLA_KV_RANK, name="kv_norm")
    kv_raw = matmul(ckvn, w_kv_p, "nn", name="kv_up")
    q_cat, k_cat, v = rope_fwd(q_raw, kv_raw, proj, tabs)
    o, lse, carried = attn_fwd(q_cat, k_cat, v, carry)
    if on_carried is not None:
        on_carried(carried)
    w_out = w_out()
    half = w_out.shape[0] // 2
    m = matmul([y_ssd, o], [KBlock(w_out, half, 0), KBlock(w_out, half, 1)], "nn", name="sm_out")
    res = (h, hn, proj, xbc_c, y, hst, cqn, ckvn, q_cat, k_cat, v, o, lse, y_ssd, m)
    h2, hn_next = resadd_fwd(h, m, g_post, g_next, name="sm_res")
    return h2, res, hn_next


def sm_layer_bwd(res, dh2, g_pre, g_post, w_in_p, convp, dt_bias, a_log, dskip, ssd_g, q_g, w_q_p, kv_g, w_kv_p, w_out, tabs,
                 carry=None, post=None, then=None):
    h, hn, proj, xbc_c, y, hst, cqn, ckvn, q_cat, k_cat, v, o, lse, y_ssd, m = res
    w_out = w_out()
    dm, dg_post = post if post is not None else norm_bwd(m, g_post, dh2, mask_pad=True, out_dtype=BF, name="sm_post_bwd")
    dw_out = jnp.concatenate([matmul(y_ssd, dm, "tn", name="sm_dwout_ssd"), matmul(o, dm, "tn", name="sm_dwout_att")], axis=0)
    dyab = matmul(dm, w_out, "nt", name="sm_dyab")
    dq_cat, dk_cat, dv, carried = attn_bwd(q_cat, k_cat, v, o, lse, dyab, carry(dw_out) if carry is not None else None)
    dq_raw, dkr = rope_bwd(dq_cat, dk_cat, tabs)
    kw = MLA_HEADS * LANE
    dw_kv_p = jnp.concatenate([matmul(ckvn, dk_cat, "tn", name="kv_dw_k"), matmul(ckvn, dv, "tn", name="kv_dw_v")], axis=1)
    dckvn = matmul([dk_cat, dv], [KBlock(w_kv_p, kw, 0), KBlock(w_kv_p, kw // 2, 2)], "nt", name="kv_dx")
    dckv, dg_kv = norm_bwd(proj, kv_g, dckvn, out_dtype=BF, col_blk=OFF_CKV // MLA_KV_RANK, width=MLA_KV_RANK,
                           name="kv_norm_bwd")
    dw_q_p = matmul(cqn, dq_raw, "tn", name="q_dw")
    dcqn = matmul(dq_raw, w_q_p, "nt", name="q_dx")
    dcq, dg_q = norm_bwd(proj, q_g, dcqn, out_dtype=BF, col_blk=OFF_CQ // MLA_Q_RANK, width=MLA_Q_RANK, name="q_norm_bwd")
    dy, dz, dg_ssd = gated_norm_bwd(y, proj, ssd_g, dyab)
    dxbc_c, ddt, dpar = ssd_bwd(xbc_c, proj, dt_bias, a_log, dskip, hst, dy)
    dxbc, dconvp = conv_silu_bwd(proj, OFF_XBC, SSD_CONV_CH, convp, dxbc_c, name="ssd_conv_bwd")
    pieces = [dz, dxbc, dckv, ddt, dkr, dcq]
    dw_in_p = jnp.concatenate([matmul(hn, pc, "tn", (BF,), name="sm_dwin_%d" % i) for i, pc in enumerate(pieces)], axis=1)
    third = SSD_CONV_CH // 3
    a_terms = [dz] + [KBlock(dxbc, third, i) for i in range(3)] + [dckv, ddt, dkr, dcq]
    b_terms = ([KBlock(w_in_p, SSD_D_INNER, 0)] + [KBlock(w_in_p, third, OFF_XBC // third + i) for i in range(3)]
               + [KBlock(w_in_p, MLA_KV_RANK, OFF_CKV // MLA_KV_RANK), KBlock(w_in_p, LANE, OFF_DT // LANE),
                  KBlock(w_in_p, LANE, OFF_KR // LANE), KBlock(w_in_p, MLA_Q_RANK, OFF_CQ // MLA_Q_RANK)])
    dhn = matmul(a_terms, b_terms, "nt", name="sm_dhn")
    dh, dg_pre, *below = norm_bwd(h, g_pre, dhn, dres=dh2, then=then, name="sm_pre_bwd")
    grads = dict(mix_pre_g=dg_pre, mix_post_g=dg_post, w_in=w_in_cols_to_blocks(dw_in_p), ssd_conv_w=dconvp[0:4],
                 ssd_conv_b=dconvp[4], ssd_dt_bias=dpar[0, :SSD_HEADS], ssd_a_log=dpar[1, :SSD_HEADS],
                 ssd_d=dpar[2, :SSD_HEADS], ssd_norm_g=dg_ssd, mla_q_norm_g=dg_q, mla_w_q_up=_unpack_w_q(dw_q_p),
                 mla_kv_norm_g=dg_kv, mla_w_kv_up=_unpack_w_kv(dw_kv_p), w_out_ab=dw_out)
    return dh, grads, carried, (tuple(below) or None)


W_IN_COLS = 3248
W_IN_SHARD = W_IN_COLS // N_DEV
W_IN_WIRE = 512


def _w_in_tables():
    src = np.full((IN_W,), -1, np.int64)
    src[0:2560] = np.arange(2560)
    src[OFF_CKV:OFF_CKV + 256] = 2960 + np.arange(256)
    src[OFF_DT:OFF_DT + 16] = 2560 + np.arange(16)
    src[OFF_KR + 64:OFF_KR + 96] = 3216 + np.arange(32)
    src[OFF_CQ:OFF_CQ + 384] = 2576 + np.arange(384)
    dev = np.where(src >= 0, src // W_IN_SHARD, -1).astype(np.int32).reshape(1, IN_W)
    col = np.where(src >= 0, src % W_IN_SHARD, 0).astype(np.int32).reshape(1, IN_W)
    return dev, col


W_IN_TILE = 384


def _w_in_devices_of_tile(dev):
    return [sorted(set(dev[0, t * W_IN_TILE:(t + 1) * W_IN_TILE].tolist()) - {-1}) for t in range(IN_W // W_IN_TILE)]


def _any_of(index, values):
    cond = index == values[0]
    for v in values[1:]:
        cond = cond | (index == v)
    return cond


def w_in_blocks_to_cols(g8):
    _, k, wp = g8.shape
    tn = W_IN_TILE
    dev, col = _w_in_tables()
    holders = _w_in_devices_of_tile(dev)

    def body(g_ref, dev_ref, col_ref, o_ref):
        i = pl.program_id(0)
        row = lax.broadcasted_iota(jnp.int32, (wp, tn), 0)
        o_ref[...] = jnp.zeros_like(o_ref)
        for j in range(N_DEV):
            tiles = [t for t, devs in enumerate(holders) if j in devs]
            if tiles:
                @pl.when(_any_of(i, tiles))
                def _(j=j):
                    sel = ((row == col_ref[...]) & (dev_ref[...] == j)).astype(BF)
                    o_ref[...] += jnp.dot(g_ref[j], sel, preferred_element_type=F32).astype(o_ref.dtype)

    dev, col = jnp.asarray(dev), jnp.asarray(col)
    return pl.pallas_call(
        body,
        out_shape=_sds((k, IN_W), BF),
        grid=(IN_W // tn,),
        in_specs=[pl.BlockSpec((N_DEV, k, wp), lambda i: (0, 0, 0)), pl.BlockSpec((1, tn), lambda i: (0, i)),
                  pl.BlockSpec((1, tn), lambda i: (0, i))],
        out_specs=pl.BlockSpec((k, tn), lambda i: (0, i)),
        compiler_params=_cp(("parallel",)),
        name="w_in_cols",
    )(g8, dev, col)


def w_in_cols_to_blocks(dw):
    k = dw.shape[0]
    tn = W_IN_TILE
    dev, col = _w_in_tables()
    holders = _w_in_devices_of_tile(dev)

    def body(dw_ref, dev_ref, col_ref, o_ref):
        j = pl.program_id(0)
        row = lax.broadcasted_iota(jnp.int32, (W_IN_WIRE, tn), 0)
        o_ref[...] = jnp.zeros_like(o_ref)
        for t, devs in enumerate(holders):
            if devs:
                @pl.when(_any_of(j, devs))
                def _(t=t):
                    cols = slice(t * tn, (t + 1) * tn)
                    sel = ((row == col_ref[:, cols]) & (dev_ref[:, cols] == j)).astype(BF)
                    o_ref[0] += lax.dot_general(dw_ref[:, cols], sel, NT_DIMS,
                                                preferred_element_type=F32).astype(o_ref.dtype)

    dev, col = jnp.asarray(dev), jnp.asarray(col)
    return pl.pallas_call(
        body,
        out_shape=_sds((N_DEV, k, W_IN_WIRE), BF),
        grid=(N_DEV,),
        in_specs=[pl.BlockSpec((k, IN_W), lambda j: (0, 0)), pl.BlockSpec((1, IN_W), lambda j: (0, 0)),
                  pl.BlockSpec((1, IN_W), lambda j: (0, 0))],
        out_specs=pl.BlockSpec((1, k, W_IN_WIRE), lambda j: (j, 0, 0)),
        compiler_params=_cp(("parallel",)),
        name="w_in_blocks",
    )(dw, dev, col)


def _pack_w_q(w):
    w3 = w.reshape(w.shape[0], MLA_HEADS, MLA_NOPE + MLA_ROPE)
    return jnp.pad(w3, ((0, 0), (0, 0), (0, LANE - MLA_NOPE - MLA_ROPE))).reshape(w.shape[0], MLA_HEADS * LANE)


def _unpack_w_q(p):
    return p.reshape(p.shape[0], MLA_HEADS, LANE)[:, :, :MLA_NOPE + MLA_ROPE].reshape(p.shape[0], -1)


def _pack_w_kv(w):
    w3 = w.reshape(w.shape[0], MLA_HEADS, MLA_NOPE + MLA_V)
    k = jnp.pad(w3[:, :, :MLA_NOPE], ((0, 0), (0, 0), (0, LANE - MLA_NOPE))).reshape(w.shape[0], MLA_HEADS * LANE)
    return jnp.concatenate([k, w3[:, :, MLA_NOPE:].reshape(w.shape[0], MLA_HEADS * MLA_V)], axis=1)


def _unpack_w_kv(p):
    k = p[:, :MLA_HEADS * LANE].reshape(p.shape[0], MLA_HEADS, LANE)[:, :, :MLA_NOPE]
    v = p[:, MLA_HEADS * LANE:].reshape(p.shape[0], MLA_HEADS, MLA_V)
    return jnp.concatenate([k, v], axis=2).reshape(p.shape[0], -1)


def _rows8(rows, width):
    a = jnp.concatenate([r.reshape(-1, width) for r in rows], axis=0)
    return jnp.pad(a, ((0, 8 - a.shape[0]), (0, 0)))


SLAB_ROWS = 16


def _to_slab(flat_list, lead=()):
    cat = jnp.concatenate(flat_list, axis=-1)
    n = cat.shape[-1]
    unit = SLAB_ROWS * PACK_W
    total = -(-n // unit) * unit
    cat = jnp.pad(cat, [(0, 0)] * len(lead) + [(0, total - n)])
    return cat.reshape(lead + (total // PACK_W, PACK_W))


def _from_flat(flat, shapes):
    out, off = [], 0
    for s in shapes:
        n = int(np.prod(s))
        out.append(flat[off:off + n].reshape(s))
        off += n
    return out


def _gathered_full(g8, axis):
    moved = jnp.moveaxis(g8, 0, axis)
    shp = moved.shape
    return moved.reshape(shp[:axis] + (shp[axis] * shp[axis + 1],) + shp[axis + 2:])


def _per_device(full, axis):
    shp = full.shape
    split = full.reshape(shp[:axis] + (N_DEV, shp[axis] // N_DEV) + shp[axis + 1:])
    return jnp.moveaxis(split, axis, 0)


ARG_NAMES = ['x', 'meta_tokens', 'mix_pre_g', 'mix_post_g', 'mlp_pre_g', 'mlp_post_g', 'w_up', 'w_down', 'w_in',
             'ssd_conv_w', 'ssd_conv_b', 'ssd_dt_bias', 'ssd_a_log', 'ssd_d', 'ssd_norm_g', 'mla_q_norm_g',
             'mla_w_q_up', 'mla_kv_norm_g', 'mla_w_kv_up', 'w_out_ab', 'rg_w_x', 'rg_w_y', 'rg_conv_w', 'rg_conv_b',
             'rg_w_a', 'rg_b_a', 'rg_w_i', 'rg_b_i', 'rg_lambda', 'rg_w_out']
WEIGHTS = ARG_NAMES[1:]
BIG = {'w_up': 2, 'w_down': 1, 'w_in': 2, 'mla_w_q_up': 2, 'mla_w_kv_up': 2, 'w_out_ab': 1, 'rg_w_x': 2,
       'rg_w_y': 2, 'rg_w_out': 1}
SMALL = {'meta_tokens': 1, 'ssd_conv_w': 2, 'rg_conv_w': 2, 'rg_conv_b': 1, 'rg_b_a': 1, 'rg_b_i': 1, 'rg_lambda': 1}
REPL = [n for n in WEIGHTS if n not in BIG and n not in SMALL]
REPL_MEDIUM = ['rg_w_a', 'rg_w_i']
REPL_TINY = [n for n in REPL if n not in REPL_MEDIUM]


def _piece_axes():
    axes = {}
    for n, ax in BIG.items():
        for i in range(DEPTH if n in ('w_up', 'w_down') else DEPTH // 2):
            axes[(n, i)] = ax - 1
    return axes


PIECE_AXIS = _piece_axes()
AS_BLOCKS = ('w_up', 'w_down')
_SM = lambda i: [(n, i) for n in ('w_in', 'mla_w_q_up', 'mla_w_kv_up', 'w_out_ab')]
_RG = lambda i: [(n, i) for n in ('rg_w_x', 'rg_w_y', 'rg_w_out')]
_MLP = lambda l: [('w_up', l), ('w_down', l)]
_SM_IN = lambda i: [(n, i) for n in ('w_in', 'mla_w_q_up', 'mla_w_kv_up')]
GATHER_FIRST = _SM_IN(0)
GATHER_AT = {0: [('w_out_ab', 0)] + _MLP(0) + _RG(0) + _MLP(1) + _SM(1), 2: _MLP(2) + _RG(1) + _MLP(3)}
SCATTER_AT = {2: _MLP(3) + _RG(1) + _MLP(2) + [('w_out_ab', 1)],
              0: _SM_IN(1) + _MLP(1) + _RG(0) + _MLP(0) + [('w_out_ab', 0)]}
SCATTER_LAST = _SM_IN(0)


def _wire_block(p, key):
    n, i = key
    blk = p[n][i]
    if n == 'w_in':
        blk = jnp.pad(blk, ((0, 0), (0, W_IN_WIRE - blk.shape[1])))
    return blk


def _step(p, moments):
    assert DEPTH == 4
    full = {n: [None] * p[n].shape[0] for n in BIG}
    full['w_in_g'] = [None] * p['w_in'].shape[0]

    def weight_blocks(group):
        return [_wire_block(p, k).astype(BF) for k in group]

    def take_weights(group, gathered):
        for (n, i), piece in zip(group, gathered):
            if n == 'w_in':
                full['w_in_g'][i] = piece
            elif n in AS_BLOCKS:
                full[n][i] = DevBlocks(piece, PIECE_AXIS[(n, i)])
            else:
                full[n][i] = _gathered_full(piece, PIECE_AXIS[(n, i)])

    def grad_blocks(group, gw):
        return [gw[k] if k[0] in AS_BLOCKS or k[0] == 'w_in' else _per_device(gw[k], PIECE_AXIS[k]).astype(BF)
                for k in group]

    parts = {}

    small_slab = _to_slab([p[n].reshape(-1) for n in SMALL])
    *first, small8 = all_gather(weight_blocks(GATHER_FIRST) + [small_slab], name="gather_first")
    take_weights(GATHER_FIRST, first)
    for n, piece in zip(SMALL, _from_flat_rows(small8, [p[n].shape for n in SMALL])):
        full[n] = _gathered_full(piece, SMALL[n])
    for n in REPL:
        full[n] = p[n]
    loss_local, grad_x, gw, gsmall_full, carried = _local_step(
        full, p['x'][0], p['loss_target'][0],
        fwd_carry=lambda layer: Exchange("gather", weight_blocks(GATHER_AT[layer])),
        on_fwd_carried=lambda layer, got: take_weights(GATHER_AT[layer], got),
        bwd_carry=lambda layer, gw_now, others: Exchange(
            "scatter", grad_blocks(SCATTER_AT[layer], gw_now)
            + ([jnp.stack(others[n], axis=0).reshape(N_DEV, -1, LANE) for n in REPL_MEDIUM] if layer == 0 else [])))

    for layer, group in SCATTER_AT.items():
        parts.update(zip(group, carried[layer]))
    rep_flat = jnp.concatenate([gsmall_full[n].reshape(-1) for n in REPL_TINY])
    rep_n = rep_flat.shape[0]
    rep_chunk = -(-rep_n // (N_DEV * PACK_W * 8)) * PACK_W * 8
    rep8 = jnp.pad(rep_flat, (0, N_DEV * rep_chunk - rep_n)).reshape(N_DEV, rep_chunk)
    gsmall = _to_slab([_per_device(gsmall_full[n], SMALL[n]).reshape(N_DEV, -1) for n in SMALL] + [rep8], lead=(N_DEV,))
    received = all_to_all(grad_blocks(SCATTER_LAST, gw) + [gsmall], name="scatter_last")
    n_last = len(SCATTER_LAST)
    parts.update(zip(SCATTER_LAST, received[:n_last]))
    ssmall = slab_sum(received[n_last], name="sum_small").reshape(-1)
    medium_mine = [slab_sum(r8, name="sum_" + n) for n, r8 in zip(REPL_MEDIUM, carried[0][len(SCATTER_AT[0]):])]
    g_loc = {'w_in': jnp.stack([slab_sum(parts[('w_in', i)], name="sum_w_in_%d" % i)[:, :W_IN_SHARD]
                                for i in range(p['w_in'].shape[0])], axis=0)}
    small_n = sum(int(np.prod(p[n].shape)) for n in SMALL)
    g_loc.update(zip(SMALL, _from_flat(ssmall, [p[n].shape for n in SMALL])))
    rep_mine = ssmall[small_n:small_n + rep_chunk].reshape(-1, PACK_W)
    rep_all, *medium_all = all_gather([rep_mine] + medium_mine, name="gather_replicated")
    g_loc.update(zip(REPL_TINY, _from_flat(rep_all.reshape(-1), [p[n].shape for n in REPL_TINY])))
    g_loc.update({n: g.reshape(p[n].shape) for n, g in zip(REPL_MEDIUM, medium_all)})

    out = {'loss': lax.psum(loss_local, ("x", "y", "c")), 'grad_x': grad_x[None]}
    small_names = list(SMALL) + REPL_TINY
    for n in list(BIG) + REPL_MEDIUM:
        shp = p[n].shape
        if n == 'w_in' or n in REPL_MEDIUM:
            v2 = lambda a: a.reshape(-1, shp[-1])
            d, nm, nv = adamw(v2(p[n]), v2(g_loc[n]), v2(moments['m_' + n]), v2(moments['v_' + n]), name="adamw_" + n)
            d, nm, nv = d.reshape(shp), nm.reshape(shp), nv.reshape(shp)
        else:
            g_loc[n], d, nm, nv = adamw_blocks(p[n], moments['m_' + n], moments['v_' + n],
                                               [parts[(n, i)] for i in range(shp[0])], name="adamw_" + n)
        out['delta_' + n], out['new_m_' + n], out['new_v_' + n] = d, nm, nv
    slab = lambda src: _to_slab([src(n).reshape(-1) for n in small_names])
    d, nm, nv = adamw(slab(lambda n: p[n]), slab(lambda n: g_loc[n]), slab(lambda n: moments['m_' + n]),
                      slab(lambda n: moments['v_' + n]), name="adamw_small")
    shapes = [p[n].shape for n in small_names]
    for key, flat in (('delta_', d), ('new_m_', nm), ('new_v_', nv)):
        for n, a in zip(small_names, _from_flat(flat.reshape(-1), shapes)):
            out[key + n] = a
    for n in WEIGHTS:
        out['grad_' + n] = g_loc[n]
    return out


def _local_step(full, x, target_rows, fwd_carry=None, on_fwd_carried=None, bwd_carry=None):
    t = PAD + N_META + x.shape[0]
    h = jnp.concatenate([jnp.zeros((PAD, D_MODEL), F32), full['meta_tokens'], x], axis=0)
    target = jnp.concatenate([jnp.zeros((PAD + N_META, D_MODEL), F32), target_rows], axis=0)
    tabs = rope_tables(t)

    def layer_args(layer):
        i = layer // 2
        if layer % 2 == 0:
            convp = _rows8([full['ssd_conv_w'][i], full['ssd_conv_b'][i]], SSD_CONV_CH)
            return (full['mix_pre_g'][layer], full['mix_post_g'][layer], w_in_blocks_to_cols(full['w_in_g'][i]), convp,
                    full['ssd_dt_bias'][i], full['ssd_a_log'][i], full['ssd_d'][i], full['ssd_norm_g'][i],
                    full['mla_q_norm_g'][i], _pack_w_q(full['mla_w_q_up'][i]), full['mla_kv_norm_g'][i],
                    _pack_w_kv(full['mla_w_kv_up'][i]), lambda: full['w_out_ab'][i], tabs)
        rgp = _rows8([full['rg_conv_w'][i], full['rg_conv_b'][i], full['rg_b_a'][i], full['rg_b_i'][i],
                      full['rg_lambda'][i]], LRU_WIDTH)
        return (full['mix_pre_g'][layer], full['mix_post_g'][layer], full['rg_w_x'][i], full['rg_w_y'][i], rgp,
                full['rg_w_a'][i], full['rg_w_i'][i], full['rg_w_out'][i])

    def mlp_args(layer):
        return (full['mlp_pre_g'][layer], full['mlp_post_g'][layer], full['w_up'][layer], full['w_down'][layer])

    saved = []
    hn = None
    for layer in range(DEPTH):
        la = layer_args(layer)
        to_mlp = dict(hn=hn, g_next=full['mlp_pre_g'][layer])
        if layer % 2 == 0:
            if fwd_carry is not None:
                h, res_mix, hn = sm_layer_fwd(h, *la, carry=fwd_carry(layer),
                                              on_carried=lambda got, layer=layer: on_fwd_carried(layer, got), **to_mlp)
            else:
                h, res_mix, hn = sm_layer_fwd(h, *la, **to_mlp)
        else:
            h, res_mix, hn = rg_layer_fwd(h, *la, **to_mlp)
        ma = mlp_args(layer)
        h, res_mlp, hn = mlp_fwd(h, *ma, hn=hn, g_next=full['mix_pre_g'][layer + 1] if layer + 1 < DEPTH else None)
        saved.append((la, ma, res_mix, res_mlp))
    loss_local, dh = loss_fwd_bwd(h, target)

    others = {n: [None] * len(full[n]) for n in WEIGHTS if n not in BIG and n != 'meta_tokens'}
    gw, carried = {}, {}
    post = None
    for layer in reversed(range(DEPTH)):
        la, ma, res_mix, res_mlp = saved[layer]
        dh, gm, post = mlp_bwd(res_mlp, dh, *ma, post=post, then=(res_mix[-1], la[1]))
        below = (saved[layer - 1][3][-1], saved[layer - 1][1][1]) if layer > 0 else None
        if layer % 2 == 0:
            for n in ('w_up', 'w_down'):
                gw[(n, layer)] = gm[n]
            carry = None
            if bwd_carry is not None:
                carry = lambda dw_out, layer=layer: bwd_carry(layer, {**gw, ('w_out_ab', layer // 2): dw_out}, others)
            dh, gx, carried[layer], post = sm_layer_bwd(res_mix, dh, *la, carry=carry, post=post, then=below)
        else:
            dh, gx, post = rg_layer_bwd(res_mix, dh, *la, post=post, then=below)
        for n, g in list(gm.items()) + list(gx.items()):
            i = layer if n in ('mix_pre_g', 'mix_post_g', 'mlp_pre_g', 'mlp_post_g', 'w_up', 'w_down') else layer // 2
            if n in BIG:
                gw[(n, i)] = g
            else:
                others[n][i] = g
    gothers = {n: jnp.stack(v, axis=0) for n, v in others.items()}
    gothers['meta_tokens'] = dh[PAD:PAD + N_META]
    return loss_local, dh[PAD + N_META:], gw, gothers, carried


def _from_flat_rows(g8, shapes):
    flat = g8.reshape(N_DEV, -1)
    out, off = [], 0
    for s in shapes:
        n = int(np.prod(s))
        out.append(flat[:, off:off + n].reshape((N_DEV,) + tuple(s)))
        off += n
    return out


def kernel(x, meta_tokens, mix_pre_g, mix_post_g, mlp_pre_g, mlp_post_g, w_up, w_down, w_in, ssd_conv_w, ssd_conv_b, ssd_dt_bias, ssd_a_log, ssd_d, ssd_norm_g, mla_q_norm_g, mla_w_q_up, mla_kv_norm_g, mla_w_kv_up, w_out_ab, rg_w_x, rg_w_y, rg_conv_w, rg_conv_b, rg_w_a, rg_b_a, rg_w_i, rg_b_i, rg_lambda, rg_w_out, loss_target, m_meta_tokens, m_mix_pre_g, m_mix_post_g, m_mlp_pre_g, m_mlp_post_g, m_w_up, m_w_down, m_w_in, m_ssd_conv_w, m_ssd_conv_b, m_ssd_dt_bias, m_ssd_a_log, m_ssd_d, m_ssd_norm_g, m_mla_q_norm_g, m_mla_w_q_up, m_mla_kv_norm_g, m_mla_w_kv_up, m_w_out_ab, m_rg_w_x, m_rg_w_y, m_rg_conv_w, m_rg_conv_b, m_rg_w_a, m_rg_b_a, m_rg_w_i, m_rg_b_i, m_rg_lambda, m_rg_w_out, v_meta_tokens, v_mix_pre_g, v_mix_post_g, v_mlp_pre_g, v_mlp_post_g, v_w_up, v_w_down, v_w_in, v_ssd_conv_w, v_ssd_conv_b, v_ssd_dt_bias, v_ssd_a_log, v_ssd_d, v_ssd_norm_g, v_mla_q_norm_g, v_mla_w_q_up, v_mla_kv_norm_g, v_mla_w_kv_up, v_w_out_ab, v_rg_w_x, v_rg_w_y, v_rg_conv_w, v_rg_conv_b, v_rg_w_a, v_rg_b_a, v_rg_w_i, v_rg_b_i, v_rg_lambda, v_rg_w_out):
    args = (x, meta_tokens, mix_pre_g, mix_post_g, mlp_pre_g, mlp_post_g, w_up, w_down, w_in, ssd_conv_w, ssd_conv_b, ssd_dt_bias, ssd_a_log, ssd_d, ssd_norm_g, mla_q_norm_g, mla_w_q_up, mla_kv_norm_g, mla_w_kv_up, w_out_ab, rg_w_x, rg_w_y, rg_conv_w, rg_conv_b, rg_w_a, rg_b_a, rg_w_i, rg_b_i, rg_lambda, rg_w_out, loss_target, m_meta_tokens, m_mix_pre_g, m_mix_post_g, m_mlp_pre_g, m_mlp_post_g, m_w_up, m_w_down, m_w_in, m_ssd_conv_w, m_ssd_conv_b, m_ssd_dt_bias, m_ssd_a_log, m_ssd_d, m_ssd_norm_g, m_mla_q_norm_g, m_mla_w_q_up, m_mla_kv_norm_g, m_mla_w_kv_up, m_w_out_ab, m_rg_w_x, m_rg_w_y, m_rg_conv_w, m_rg_conv_b, m_rg_w_a, m_rg_b_a, m_rg_w_i, m_rg_b_i, m_rg_lambda, m_rg_w_out, v_meta_tokens, v_mix_pre_g, v_mix_post_g, v_mlp_pre_g, v_mlp_post_g, v_w_up, v_w_down, v_w_in, v_ssd_conv_w, v_ssd_conv_b, v_ssd_dt_bias, v_ssd_a_log, v_ssd_d, v_ssd_norm_g, v_mla_q_norm_g, v_mla_w_q_up, v_mla_kv_norm_g, v_mla_w_kv_up, v_w_out_ab, v_rg_w_x, v_rg_w_y, v_rg_conv_w, v_rg_conv_b, v_rg_w_a, v_rg_b_a, v_rg_w_i, v_rg_b_i, v_rg_lambda, v_rg_w_out,)
    n_w = len(ARG_NAMES)
    p = dict(zip(ARG_NAMES, args[:n_w]))
    p['loss_target'] = args[n_w]
    moments = {}
    for i, n in enumerate(WEIGHTS):
        moments['m_' + n] = args[n_w + 1 + i]
        moments['v_' + n] = args[n_w + 1 + len(WEIGHTS) + i]
    out = _step(p, moments)
    res = [out['loss'], out['grad_x']]
    for prefix in ('grad_', 'delta_', 'new_m_', 'new_v_'):
        res += [out[prefix + n] for n in WEIGHTS]
    return tuple(res)
```

```python
import math

import numpy as np
import jax
import jax.numpy as jnp
from jax import lax
from jax.experimental import pallas as pl
from jax.experimental.pallas import tpu as pltpu

F32 = jnp.float32
BF = jnp.bfloat16
HI = lax.Precision.HIGHEST

D_MODEL = 1024
DEPTH = 4
N_META = 16
CHUNK = 128
PAD = CHUNK - N_META
EPS = 1e-6
SSD_HEADS = 16
SSD_HEAD_DIM = 64
SSD_D_INNER = 1024
SSD_STATE = 128
SSD_CONV_CH = 1536
MLA_HEADS = 16
MLA_NOPE = 64
MLA_ROPE = 32
MLA_V = 64
MLA_Q_RANK = 384
MLA_KV_RANK = 256
ROPE_BASE = 10000.0
LRU_WIDTH = 1280
LRU_BLOCKS = 10
LRU_C = 8.0
D_FF = 4096
N_DEV = 8
LANE = 128
IN_W = 3456
OFF_Z, OFF_XBC, OFF_CKV, OFF_DT, OFF_KR, OFF_CQ = 0, 1024, 2560, 2816, 2944, 3072

ADAM_LR = 0.001
ADAM_B1 = 0.9
ADAM_B2 = 0.999
ADAM_EPS = 1e-08
ADAM_WD = 0.01
ADAM_STEP = 10

VMEM_LIMIT = 56 * 1024 * 1024
NEG = -1e30


def _pick(n, cands):
    for c in cands:
        if n % c == 0:
            return c
    return n


def _cp(sem=None):
    return pltpu.CompilerParams(dimension_semantics=sem, vmem_limit_bytes=VMEM_LIMIT)


def _sds(shape, dtype):
    return jax.ShapeDtypeStruct(tuple(shape), dtype)


def _silu(x):
    return x * jax.nn.sigmoid(x)


def _softplus(x):
    return jnp.maximum(x, 0.0) + jnp.log(1.0 + jnp.exp(-jnp.abs(x)))


def _gelu(x):
    c = math.sqrt(2.0 / math.pi)
    return 0.5 * x * (1.0 + jnp.tanh(c * (x + 0.044715 * (x * x * x))))


def _row_mask(i, tr, shape, first_valid=PAD):
    row = i * tr + lax.broadcasted_iota(jnp.int32, shape, 0)
    return row >= first_valid


class KBlock:
    def __init__(self, arr, width, blk):
        self.arr, self.width, self.blk = arr, width, blk


class DevBlocks:
    def __init__(self, g8, axis):
        self.g8, self.axis = g8, axis
        _, r, c = g8.shape
        self.shape = (N_DEV * r, c) if axis == 0 else (r, N_DEV * c)


NN_DIMS = (((1,), (0,)), ((), ()))
MM_TALL_K = 1536


def matmul(a, b, mode, out_dtypes=(F32,), epi=None, extras=(), name="mm", tm=None, tn=None, out_blocks=False):
    a_terms = a if isinstance(a, (list, tuple)) else [a]
    b_terms = b if isinstance(b, (list, tuple)) else [b]
    assert len(a_terms) == len(b_terms) and (mode != "tn" or len(a_terms) == 1)
    arr_of = lambda t: t.arr if isinstance(t, KBlock) else t
    if mode == "tn":
        m, n = a_terms[0].shape[1], b_terms[0].shape[1]
    else:
        m = arr_of(a_terms[0]).shape[0]
        b0 = b_terms[0]
        n = (b0.shape if isinstance(b0, DevBlocks) else arr_of(b0).shape)[1 if mode == "nn" else 0]
    if mode == "tn":
        tm = _pick(m, (1024, 512, 384, 256, 128))
    else:
        k_all = sum(t.width if isinstance(t, KBlock) else t.shape[1] for t in a_terms)
        tall = (2112,) if k_all <= MM_TALL_K else ()
        tm = tm or _pick(m, tall + (1056, 1024, 768, 640, 512, 384, 256, 128))
    tn = tn or _pick(n, (512, 640, 384, 256, 128))
    dims = {"nn": NN_DIMS, "nt": NT_DIMS, "tn": TN_DIMS}[mode]

    in_specs, args, plan = [], [], []
    for ta, tb in zip(a_terms, b_terms):
        if mode == "tn":
            k = ta.shape[0]
            in_specs += [pl.BlockSpec((k, tm), lambda i, j: (0, i)), pl.BlockSpec((k, tn), lambda i, j: (0, j))]
            args += [ta, tb]
            plan.append(None)
            continue
        if isinstance(ta, KBlock):
            kw, ka = ta.width, ta.blk
            in_specs.append(pl.BlockSpec((tm, kw), lambda i, j, ka=ka: (i, ka)))
        else:
            kw = ta.shape[1]
            in_specs.append(pl.BlockSpec((tm, kw), lambda i, j: (i, 0)))
        args.append(arr_of(ta))
        if isinstance(tb, DevBlocks):
            _, r, c = tb.g8.shape
            split_k = tb.axis == (0 if mode == "nn" else 1)
            if split_k:
                kd = r if mode == "nn" else c
                assert kw == N_DEV * kd
                blk = (N_DEV, kd, tn) if mode == "nn" else (N_DEV, tn, kd)
                in_specs.append(pl.BlockSpec(blk, (lambda i, j: (0, 0, j)) if mode == "nn" else (lambda i, j: (0, j, 0))))
                plan.append(kd)
            else:
                per = (c if mode == "nn" else r) // tn
                blk = (None, kw, tn) if mode == "nn" else (None, tn, kw)
                in_specs.append(pl.BlockSpec(blk, (lambda i, j, per=per: (j // per, 0, j % per)) if mode == "nn"
                                             else (lambda i, j, per=per: (j // per, j % per, 0))))
                plan.append(None)
            args.append(tb.g8)
        else:
            kb = tb.blk if isinstance(tb, KBlock) else 0
            assert (tb.width if isinstance(tb, KBlock) else tb.shape[0 if mode == "nn" else 1]) == kw
            in_specs.append(pl.BlockSpec((kw, tn), lambda i, j, kb=kb: (kb, j)) if mode == "nn"
                            else pl.BlockSpec((tn, kw), lambda i, j, kb=kb: (j, kb)))
            args.append(arr_of(tb))
            plan.append(None)
    n_terms, n_ex = len(plan), len(extras)

    def body(*refs):
        ex_refs, out_refs = refs[2 * n_terms:2 * n_terms + n_ex], refs[2 * n_terms + n_ex:]
        acc = None
        for t, kd in enumerate(plan):
            a_ref, b_ref = refs[2 * t], refs[2 * t + 1]
            if kd is None:
                parts = [lax.dot_general(a_ref[...].astype(BF), b_ref[...].astype(BF), dims, preferred_element_type=F32)]
            else:
                parts = [lax.dot_general(a_ref[:, d * kd:(d + 1) * kd].astype(BF), b_ref[d].astype(BF), dims,
                                         preferred_element_type=F32) for d in range(N_DEV)]
            for part in parts:
                acc = part if acc is None else acc + part
        outs = (acc,) if epi is None else epi(acc, *[r[...] for r in ex_refs])
        for r, o in zip(out_refs, outs):
            r[...] = o.astype(r.dtype)

    o_spec = pl.BlockSpec((tm, tn), lambda i, j: (i, j))
    if out_blocks:
        per = n // N_DEV // tn
        out_shape = tuple(_sds((N_DEV, m, n // N_DEV), dt) for dt in out_dtypes)
        out_specs = tuple(pl.BlockSpec((None, tm, tn), lambda i, j: (j // per, i, j % per)) for _ in out_dtypes)
    else:
        out_shape = tuple(_sds((m, n), dt) for dt in out_dtypes)
        out_specs = tuple(o_spec for _ in out_dtypes)
    outs = pl.pallas_call(
        body,
        out_shape=out_shape,
        grid=(m // tm, n // tn),
        in_specs=in_specs + [o_spec] * n_ex,
        out_specs=out_specs,
        compiler_params=_cp(("parallel", "parallel")),
        name=name,
    )(*args, *extras)
    return outs[0] if len(out_dtypes) == 1 else outs


def _rt(t):
    return _pick(t, (384, 256, 128))


def norm_fwd(x, g, out_dtype, col_blk=0, width=None, name="norm_fwd"):
    t = x.shape[0]
    w = width or x.shape[1]
    tr = _rt(t)

    def body(x_ref, g_ref, o_ref):
        xv = x_ref[...]
        r = lax.rsqrt(jnp.mean(xv * xv, axis=-1, keepdims=True) + EPS)
        o_ref[...] = (xv * r * g_ref[...]).astype(o_ref.dtype)

    return pl.pallas_call(
        body,
        out_shape=_sds((t, w), out_dtype),
        grid=(t // tr,),
        in_specs=[pl.BlockSpec((tr, w), lambda i: (i, col_blk)), pl.BlockSpec((1, w), lambda i: (0, 0))],
        out_specs=pl.BlockSpec((tr, w), lambda i: (i, 0)),
        compiler_params=_cp(("parallel",)),
        name=name,
    )(x, g.reshape(1, w))


def _rms_bwd(xv, gv, dyv):
    r = lax.rsqrt(jnp.mean(xv * xv, axis=-1, keepdims=True) + EPS)
    xh = xv * r
    dyg = dyv * gv
    dx = r * (dyg - xh * jnp.mean(dyg * xh, axis=-1, keepdims=True))
    return dx, jnp.sum(dyv * xh, axis=0, keepdims=True)


def norm_bwd(x, g, dy, dres=None, mask_pad=False, out_dtype=F32, col_blk=0, width=None, dy_col_blk=0, then=None,
             name="norm_bwd"):
    t = x.shape[0]
    w = width or x.shape[1]
    tr = _rt(t)
    has_res, has_then = dres is not None, then is not None

    def body(*refs):
        x_ref, g_ref, dy_ref = refs[:3]
        n_in = 3 + has_res + 2 * has_then
        dx_ref, dg_ref = refs[n_in:n_in + 2]
        i = pl.program_id(0)
        dyv = dy_ref[...].astype(F32)
        if mask_pad:
            dyv = jnp.where(_row_mask(i, tr, dyv.shape), dyv, 0.0)
        dx, dg = _rms_bwd(x_ref[...], g_ref[...], dyv)
        if has_res:
            dx = dx + refs[3][...]
        dx_ref[...] = dx.astype(dx_ref.dtype)

        @pl.when(i == 0)
        def _():
            for r in refs[n_in + 1::2]:
                r[...] = jnp.zeros_like(r)

        dg_ref[...] += dg
        if has_then:
            x2_ref, g2_ref = refs[3 + has_res:5 + has_res]
            dx2_ref, dg2_ref = refs[n_in + 2:]
            dx2, dg2 = _rms_bwd(x2_ref[...], g2_ref[...], jnp.where(_row_mask(i, tr, dx.shape), dx, 0.0))
            dx2_ref[...] = dx2.astype(dx2_ref.dtype)
            dg2_ref[...] += dg2

    row = pl.BlockSpec((tr, w), lambda i: (i, 0))
    vec = pl.BlockSpec((1, w), lambda i: (0, 0))
    in_specs = [pl.BlockSpec((tr, w), lambda i: (i, col_blk)), vec, pl.BlockSpec((tr, w), lambda i: (i, dy_col_blk))]
    args = [x, g.reshape(1, w), dy]
    out_shape, out_specs = [_sds((t, w), out_dtype), _sds((1, w), F32)], [row, vec]
    if has_res:
        in_specs.append(row)
        args.append(dres)
    if has_then:
        in_specs += [row, vec]
        args += [then[0], then[1].reshape(1, w)]
        out_shape += [_sds((t, w), BF), _sds((1, w), F32)]
        out_specs += [row, vec]
    outs = pl.pallas_call(
        body,
        out_shape=tuple(out_shape),
        grid=(t // tr,),
        in_specs=in_specs,
        out_specs=tuple(out_specs),
        compiler_params=_cp(("arbitrary",)),
        name=name,
    )(*args)
    if has_then:
        return outs[0], outs[1].reshape(w), outs[2], outs[3].reshape(w)
    return outs[0], outs[1].reshape(w)


def resadd_fwd(h, m, g, g_next=None, name="resadd"):
    t, w = h.shape
    tr = _rt(t)
    with_next = g_next is not None

    def body(h_ref, m_ref, g_ref, *rest):
        mv = m_ref[...]
        r = lax.rsqrt(jnp.mean(mv * mv, axis=-1, keepdims=True) + EPS)
        y = mv * r * g_ref[...]
        h2 = h_ref[...] + jnp.where(_row_mask(pl.program_id(0), tr, y.shape), y, 0.0)
        if with_next:
            gn_ref, o_ref, hn_ref = rest
            r2 = lax.rsqrt(jnp.mean(h2 * h2, axis=-1, keepdims=True) + EPS)
            hn_ref[...] = (h2 * r2 * gn_ref[...]).astype(hn_ref.dtype)
        else:
            (o_ref,) = rest
        o_ref[...] = h2

    row = pl.BlockSpec((tr, w), lambda i: (i, 0))
    vec = pl.BlockSpec((1, w), lambda i: (0, 0))
    outs = pl.pallas_call(
        body,
        out_shape=(_sds((t, w), F32),) + ((_sds((t, w), BF),) if with_next else ()),
        grid=(t // tr,),
        in_specs=[row, row, vec] + ([vec] if with_next else []),
        out_specs=(row,) + ((row,) if with_next else ()),
        compiler_params=_cp(("parallel",)),
        name=name,
    )(h, m, g.reshape(1, w), *((g_next.reshape(1, w),) if with_next else ()))
    return outs[0], (outs[1] if with_next else None)


def loss_fwd_bwd(h, target):
    t, w = h.shape
    tr = _rt(t)

    def body(h_ref, t_ref, s_ref, dh_ref):
        i = pl.program_id(0)
        err = h_ref[...] - t_ref[...]
        err = jnp.where(_row_mask(i, tr, err.shape, PAD + N_META), err, 0.0)
        dh_ref[...] = err * (1.0 / w)

        @pl.when(i == 0)
        def _():
            s_ref[...] = jnp.zeros_like(s_ref)

        s_ref[...] += jnp.sum(err * err).reshape(1, 1)

    s, dh = pl.pallas_call(
        body,
        out_shape=(_sds((1, LANE), F32), _sds((t, w), F32)),
        grid=(t // tr,),
        in_specs=[pl.BlockSpec((tr, w), lambda i: (i, 0)), pl.BlockSpec((tr, w), lambda i: (i, 0))],
        out_specs=(pl.BlockSpec((1, LANE), lambda i: (0, 0)), pl.BlockSpec((tr, w), lambda i: (i, 0))),
        compiler_params=_cp(("arbitrary",)),
        name="loss",
    )(h, target)
    return 0.5 * s[0, 0] / w, dh


def _shift_down(ext, k, n):
    return pltpu.roll(ext, k, 0)[8:]


def _conv_pre(ext, x, w_ref, n):
    return (w_ref[4:5, :] + w_ref[3:4, :] * x + w_ref[2:3, :] * _shift_down(ext, 1, n)
            + w_ref[1:2, :] * _shift_down(ext, 2, n) + w_ref[0:1, :] * _shift_down(ext, 3, n))


def _conv_bwd_parts(dpre, dnext, x, ext, w_ref, n):
    extd = jnp.concatenate([dpre, dnext], axis=0)
    ln = n + 8
    dx = (w_ref[3:4, :] * dpre + w_ref[2:3, :] * pltpu.roll(extd, ln - 1, 0)[:n]
          + w_ref[1:2, :] * pltpu.roll(extd, ln - 2, 0)[:n] + w_ref[0:1, :] * pltpu.roll(extd, ln - 3, 0)[:n])
    sums = [jnp.sum(dpre * _shift_down(ext, 3, n), axis=0, keepdims=True),
            jnp.sum(dpre * _shift_down(ext, 2, n), axis=0, keepdims=True),
            jnp.sum(dpre * _shift_down(ext, 1, n), axis=0, keepdims=True),
            jnp.sum(dpre * x, axis=0, keepdims=True),
            jnp.sum(dpre, axis=0, keepdims=True)]
    return dx, sums


def _rows_block(sums):
    w = sums[0].shape[1]
    row = lax.broadcasted_iota(jnp.int32, (8, w), 0)
    out = jnp.zeros((8, w), F32)
    for k, s in enumerate(sums):
        out = jnp.where(row == k, s, out)
    return out


CONV_BLOCK = 512


def conv_silu_fwd(x, col0, c, wb, name="conv_fwd"):
    t = x.shape[0]
    cw = _pick(c, (CONV_BLOCK, LANE))
    nblk, col0_blk = c // cw, col0 // cw
    assert col0 % cw == 0
    tr = _rt(t)

    def body(x_ref, w_ref, o_ref, prev):
        ti = pl.program_id(1)

        @pl.when(ti == 0)
        def _():
            prev[...] = jnp.zeros_like(prev)

        xv = x_ref[...]
        ext = jnp.concatenate([prev[...], xv], axis=0)
        o_ref[...] = _silu(_conv_pre(ext, xv, w_ref, tr))
        prev[...] = xv[tr - 8:, :]

    return pl.pallas_call(
        body,
        out_shape=_sds((t, c), F32),
        grid=(nblk, t // tr),
        in_specs=[pl.BlockSpec((tr, cw), lambda cb, ti: (ti, col0_blk + cb)),
                  pl.BlockSpec((8, cw), lambda cb, ti: (0, cb))],
        out_specs=pl.BlockSpec((tr, cw), lambda cb, ti: (ti, cb)),
        scratch_shapes=[pltpu.VMEM((8, cw), F32)],
        compiler_params=_cp(("parallel", "arbitrary")),
        name=name,
    )(x, wb)


def conv_silu_bwd(x, col0, c, wb, dout, name="conv_bwd"):
    t = x.shape[0]
    cw = _pick(c, (CONV_BLOCK, LANE))
    nblk, col0_blk = c // cw, col0 // cw
    assert col0 % cw == 0
    tr = _rt(t)
    nt = t // tr
    r8 = tr // 8

    def body(x_ref, xp_ref, w_ref, do_ref, dx_ref, dwb_ref, dnext):
        ti = pl.program_id(1)
        tt = nt - 1 - ti

        @pl.when(ti == 0)
        def _():
            dnext[...] = jnp.zeros_like(dnext)
            dwb_ref[...] = jnp.zeros_like(dwb_ref)

        xv = x_ref[...]
        halo = jnp.where(tt > 0, xp_ref[...], 0.0)
        ext = jnp.concatenate([halo, xv], axis=0)
        pre = _conv_pre(ext, xv, w_ref, tr)
        s = jax.nn.sigmoid(pre)
        dpre = do_ref[...] * (s + pre * s * (1.0 - s))
        dx, sums = _conv_bwd_parts(dpre, dnext[...], xv, ext, w_ref, tr)
        dx_ref[...] = dx.astype(dx_ref.dtype)
        dwb_ref[...] += _rows_block(sums)
        dnext[...] = dpre[:8, :]

    return pl.pallas_call(
        body,
        out_shape=(_sds((t, c), BF), _sds((8, c), F32)),
        grid=(nblk, nt),
        in_specs=[pl.BlockSpec((tr, cw), lambda cb, ti: (nt - 1 - ti, col0_blk + cb)),
                  pl.BlockSpec((8, cw), lambda cb, ti: (jnp.maximum((nt - 1 - ti) * r8 - 1, 0), col0_blk + cb)),
                  pl.BlockSpec((8, cw), lambda cb, ti: (0, cb)),
                  pl.BlockSpec((tr, cw), lambda cb, ti: (nt - 1 - ti, cb))],
        out_specs=(pl.BlockSpec((tr, cw), lambda cb, ti: (nt - 1 - ti, cb)),
                   pl.BlockSpec((8, cw), lambda cb, ti: (0, cb))),
        scratch_shapes=[pltpu.VMEM((8, cw), F32)],
        compiler_params=_cp(("parallel", "arbitrary")),
        name=name,
    )(x, x, wb, dout)


def gated_norm_fwd(y, proj, g, name="gnorm_fwd"):
    t, w = y.shape
    tr = _rt(t)

    def body(y_ref, z_ref, g_ref, o_ref):
        v = y_ref[...] * _silu(z_ref[...])
        r = lax.rsqrt(jnp.mean(v * v, axis=-1, keepdims=True) + EPS)
        o_ref[...] = (v * r * g_ref[...]).astype(o_ref.dtype)

    return pl.pallas_call(
        body,
        out_shape=_sds((t, w), BF),
        grid=(t // tr,),
        in_specs=[pl.BlockSpec((tr, w), lambda i: (i, 0)), pl.BlockSpec((tr, w), lambda i: (i, OFF_Z // w)),
                  pl.BlockSpec((1, w), lambda i: (0, 0))],
        out_specs=pl.BlockSpec((tr, w), lambda i: (i, 0)),
        compiler_params=_cp(("parallel",)),
        name=name,
    )(y, proj, g.reshape(1, w))


def gated_norm_bwd(y, proj, g, dyab, name="gnorm_bwd"):
    t, w = y.shape
    tr = _rt(t)

    def body(y_ref, z_ref, g_ref, do_ref, dy_ref, dz_ref, dg_ref):
        i = pl.program_id(0)
        yv, zv, dov = y_ref[...], z_ref[...], do_ref[...]
        s = jax.nn.sigmoid(zv)
        sz = zv * s
        v = yv * sz
        r = lax.rsqrt(jnp.mean(v * v, axis=-1, keepdims=True) + EPS)
        vh = v * r
        dvg = dov * g_ref[...]
        dv = r * (dvg - vh * jnp.mean(dvg * vh, axis=-1, keepdims=True))
        dy_ref[...] = dv * sz
        dz_ref[...] = (dv * yv * (s + sz * (1.0 - s))).astype(dz_ref.dtype)

        @pl.when(i == 0)
        def _():
            dg_ref[...] = jnp.zeros_like(dg_ref)

        dg_ref[...] += jnp.sum(dov * vh, axis=0, keepdims=True)

    dy, dz, dg = pl.pallas_call(
        body,
        out_shape=(_sds((t, w), F32), _sds((t, w), BF), _sds((1, w), F32)),
        grid=(t // tr,),
        in_specs=[pl.BlockSpec((tr, w), lambda i: (i, 0)), pl.BlockSpec((tr, w), lambda i: (i, OFF_Z // w)),
                  pl.BlockSpec((1, w), lambda i: (0, 0)), pl.BlockSpec((tr, w), lambda i: (i, 0))],
        out_specs=(pl.BlockSpec((tr, w), lambda i: (i, 0)), pl.BlockSpec((tr, w), lambda i: (i, 0)),
                   pl.BlockSpec((1, w), lambda i: (0, 0))),
        compiler_params=_cp(("arbitrary",)),
        name=name,
    )(y, proj, g.reshape(1, w), dyab)
    return dy, dz, dg.reshape(w)


def rope_tables(t):
    inv = ROPE_BASE ** (-jnp.arange(0, MLA_ROPE, 2, dtype=F32) / MLA_ROPE)
    pos = (jnp.arange(t, dtype=F32) - PAD)[:, None]
    ang = pos * inv[None, :]
    cos, sin = jnp.cos(ang), jnp.sin(ang)
    z16 = jnp.zeros((t, 16), F32)
    z32 = jnp.zeros((t, 32), F32)
    c = jnp.concatenate([jnp.ones((t, 64), F32), cos, cos, z32], axis=1)
    s1 = jnp.concatenate([jnp.zeros((t, 64), F32), z16, sin, z32], axis=1)
    s2 = jnp.concatenate([jnp.zeros((t, 64), F32), -sin, z16, z32], axis=1)
    return c, s1, s2


def _rope(x, c, s1, s2):
    return x * c + pltpu.roll(x, 16, 1) * s1 + pltpu.roll(x, LANE - 16, 1) * s2


def _rope_t(d, c, s1, s2):
    return d * c + pltpu.roll(d * s1, LANE - 16, 1) + pltpu.roll(d * s2, 16, 1)


def rope_fwd(q_raw, kv_raw, proj, tabs):
    t = q_raw.shape[0]
    tr = _rt(t)
    hw = MLA_HEADS * LANE

    def body(q_ref, k_ref, v_ref, kr_ref, c_ref, s1_ref, s2_ref, qo_ref, ko_ref, vo_ref):
        c, s1, s2 = c_ref[...], s1_ref[...], s2_ref[...]
        kr = _rope(kr_ref[...], c, s1, s2)
        for h in range(MLA_HEADS):
            sl = slice(h * LANE, (h + 1) * LANE)
            qo_ref[:, sl] = (_rope(q_ref[:, sl], c, s1, s2) * Q_PRESCALE).astype(BF)
            ko_ref[:, sl] = (k_ref[:, sl] + kr).astype(BF)
        vo_ref[...] = v_ref[...].astype(BF)

    tab_spec = pl.BlockSpec((tr, LANE), lambda i: (i, 0))
    return pl.pallas_call(
        body,
        out_shape=(_sds((t, hw), BF), _sds((t, hw), BF), _sds((t, 1024), BF)),
        grid=(t // tr,),
        in_specs=[pl.BlockSpec((tr, hw), lambda i: (i, 0)), pl.BlockSpec((tr, hw), lambda i: (i, 0)),
                  pl.BlockSpec((tr, 1024), lambda i: (i, 2)), pl.BlockSpec((tr, LANE), lambda i: (i, OFF_KR // LANE)),
                  tab_spec, tab_spec, tab_spec],
        out_specs=(pl.BlockSpec((tr, hw), lambda i: (i, 0)), pl.BlockSpec((tr, hw), lambda i: (i, 0)),
                   pl.BlockSpec((tr, 1024), lambda i: (i, 0))),
        compiler_params=_cp(("parallel",)),
        name="rope_fwd",
    )(q_raw, kv_raw, kv_raw, proj, *tabs)


def rope_bwd(dq_cat, dk_cat, tabs):
    t = dq_cat.shape[0]
    tr = _rt(t)
    hw = MLA_HEADS * LANE

    def body(dq_ref, dk_ref, c_ref, s1_ref, s2_ref, dqo_ref, dkr_ref):
        c, s1, s2 = c_ref[...], s1_ref[...], s2_ref[...]
        acc = jnp.zeros((tr, LANE), F32)
        for h in range(MLA_HEADS):
            sl = slice(h * LANE, (h + 1) * LANE)
            dqo_ref[:, sl] = _rope_t(dq_ref[:, sl] * ATT_SCALE, c, s1, s2).astype(BF)
            acc = acc + dk_ref[:, sl]
        lane = lax.broadcasted_iota(jnp.int32, (tr, LANE), 1)
        dkr_ref[...] = jnp.where((lane >= 64) & (lane < 96), _rope_t(acc, c, s1, s2), 0.0)

    tab_spec = pl.BlockSpec((tr, LANE), lambda i: (i, 0))
    return pl.pallas_call(
        body,
        out_shape=(_sds((t, hw), BF), _sds((t, LANE), F32)),
        grid=(t // tr,),
        in_specs=[pl.BlockSpec((tr, hw), lambda i: (i, 0)), pl.BlockSpec((tr, hw), lambda i: (i, 0)),
                  tab_spec, tab_spec, tab_spec],
        out_specs=(pl.BlockSpec((tr, hw), lambda i: (i, 0)), pl.BlockSpec((tr, LANE), lambda i: (i, 0))),
        compiler_params=_cp(("parallel",)),
        name="rope_bwd",
    )(dq_cat, dk_cat, *tabs)


ATT_SCALE = (MLA_NOPE + MLA_ROPE) ** -0.5
LOG2E = math.log2(math.e)
Q_PRESCALE = ATT_SCALE * LOG2E
CARRY_MIDDLE_PAIR = 6
NT_DIMS = (((1,), (1,)), ((), ()))
TN_DIMS = (((0,), (0,)), ((), ()))


def _att_mask(qi, ki, tq, tk):
    qpos = qi * tq + lax.broadcasted_iota(jnp.int32, (tq, tk), 0)
    kpos = ki * tk + lax.broadcasted_iota(jnp.int32, (tq, tk), 1)
    return (kpos <= qpos) & (kpos >= PAD)


def _half_masks(n):
    lane = lax.broadcasted_iota(jnp.int32, (n, LANE), 1)
    return lane < 64, lane >= 64


def _att_tile(t):
    return _pick(t, (384, 256, 128))


def _ds(i, n):
    return pl.ds(i * n, n) if isinstance(i, int) else pl.ds(pl.multiple_of(i * n, n), n)


FWD_PAIRS = 2


def attn_fwd(q_cat, k_cat, v, carry=None):
    t = q_cat.shape[0]
    tq = tk = _att_tile(t)
    nq = t // tq
    npair, nh = FWD_PAIRS, 2 * FWD_PAIRS
    n_grp = MLA_HEADS // nh
    nx = carry.k if carry else 0

    def body(*refs):
        q_ref, k_ref, v_ref = refs[:3]
        o_ref, lse_ref = refs[3 + nx:5 + nx]
        qi = pl.program_id(1)
        if carry:
            start, middle, finish = carry.phases(refs[3:3 + nx], refs[5 + nx:5 + 2 * nx], refs[5 + 2 * nx:])
            grp = pl.program_id(0)
            pl.when((grp == 0) & (qi == 0))(start)
            pl.when((grp == CARRY_MIDDLE_PAIR // npair) & (qi == 0))(middle)
        lo_q, _ = _half_masks(tq)
        halves = _half_masks(tk)

        def step(ki, state, masked):
            m_old, l_old, accs = state[0:nh], state[nh:2 * nh], state[2 * nh:]
            rows = _ds(ki, tk)
            ss = [lax.dot_general(q_ref[:, h * LANE:(h + 1) * LANE], k_ref[rows, h * LANE:(h + 1) * LANE], NT_DIMS,
                                  preferred_element_type=F32) for h in range(nh)]
            if masked:
                valid = _att_mask(qi, ki, tq, tk)
                ss = [jnp.where(valid, s, NEG) for s in ss]
            m_new = [jnp.maximum(m_old[h], jnp.max(ss[h], axis=-1, keepdims=True)) for h in range(nh)]
            ps = [jnp.exp2(ss[h] - m_new[h]) for h in range(nh)]
            alpha = [jnp.exp2(m_old[h] - m_new[h]) for h in range(nh)]
            l_new = [alpha[h] * l_old[h] + jnp.sum(ps[h], axis=-1, keepdims=True) for h in range(nh)]
            new_accs = []
            for pp in range(npair):
                vv = v_ref[rows, pp * LANE:(pp + 1) * LANE]
                pv = [jnp.dot(ps[2 * pp + hh].astype(BF), jnp.where(halves[hh], vv, jnp.zeros_like(vv)),
                              preferred_element_type=F32) for hh in range(2)]
                new_accs.append(accs[pp] * jnp.where(lo_q, alpha[2 * pp], alpha[2 * pp + 1]) + pv[0] + pv[1])
            return tuple(m_new) + tuple(l_new) + tuple(new_accs)

        neg, zero = jnp.full((tq, 1), NEG, F32), jnp.zeros((tq, 1), F32)
        state = step(0, (neg,) * nh + (zero,) * nh + (jnp.zeros((tq, LANE), F32),) * npair, True)
        state = lax.fori_loop(1, qi, lambda ki, st: step(ki, st, False), state)
        state = lax.cond(qi > 0, lambda st: step(qi, st, True), lambda st: st, state)
        for pp in range(npair):
            l = jnp.where(lo_q, state[nh + 2 * pp], state[nh + 2 * pp + 1])
            o_ref[:, pp * LANE:(pp + 1) * LANE] = (state[2 * nh + pp] / l).astype(o_ref.dtype)
            lse_ref[:, pp * LANE:(pp + 1) * LANE] = jnp.where(lo_q, state[2 * pp], state[2 * pp + 1]) + jnp.log2(l)
        if carry:
            pl.when((grp == n_grp - 1) & (qi == nq - 1))(finish)

    outs = pl.pallas_call(
        body,
        out_shape=(_sds((t, 1024), BF), _sds((t, 1024), F32)) + tuple(carry.out_shapes() if carry else ()),
        grid=(n_grp, nq),
        in_specs=[pl.BlockSpec((tq, nh * LANE), lambda g, qi: (qi, g)),
                  pl.BlockSpec((t, nh * LANE), lambda g, qi: (0, g)),
                  pl.BlockSpec((t, npair * LANE), lambda g, qi: (0, g))] + [ANY] * nx,
        out_specs=(pl.BlockSpec((tq, npair * LANE), lambda g, qi: (qi, g)),
                   pl.BlockSpec((tq, npair * LANE), lambda g, qi: (qi, g))) + (ANY,) * nx,
        scratch_shapes=carry.scratch() if carry else [],
        compiler_params=_cp(("arbitrary", "arbitrary") if carry else ("parallel", "parallel")),
        name="attn_fwd_carrying" if carry else "attn_fwd",
    )(q_cat, k_cat, v, *(carry.arrs if carry else ()))
    return outs[0], outs[1], list(outs[2:])


def attn_bwd(q_cat, k_cat, v, o, lse, dyab, carry=None):
    t = q_cat.shape[0]
    tq = tk = _att_tile(t)
    nq = t // tq
    n_pair = MLA_HEADS // 2
    nx = carry.k if carry else 0

    def body(*refs):
        q_ref, k_ref, v_ref, o_ref, lse_ref, do_ref = refs[:6]
        dq_ref, dk_ref, dv_ref = refs[6 + nx:9 + nx]
        ki = pl.program_id(1)
        if carry:
            start, middle, finish = carry.phases(refs[6:6 + nx], refs[9 + nx:9 + 2 * nx], refs[9 + 2 * nx:])
            pair = pl.program_id(0)
            pl.when((pair == 0) & (ki == 0))(start)
            pl.when((pair == CARRY_MIDDLE_PAIR) & (ki == 0))(middle)

        @pl.when(ki == 0)
        def _():
            dq_ref[...] = jnp.zeros_like(dq_ref)

        halves = _half_masks(tq)
        vv = v_ref[...]
        kk = [k_ref[:, hh * LANE:(hh + 1) * LANE] for hh in range(2)]

        def step(qi, acc, masked):
            rows = _ds(qi, tq)
            dov, ov, lse_v = do_ref[rows, :], o_ref[rows, :].astype(F32), lse_ref[rows, :]
            qh = [q_ref[rows, hh * LANE:(hh + 1) * LANE] for hh in range(2)]
            ss = [lax.dot_general(qh[hh], kk[hh], NT_DIMS, preferred_element_type=F32) for hh in range(2)]
            if masked:
                valid = _att_mask(qi, ki, tq, tk)
                ss = [jnp.where(valid, s, NEG) for s in ss]
            ps = [jnp.exp2(ss[hh] - lse_v[:, 64 * hh:64 * hh + 1]) for hh in range(2)]
            dom = [jnp.where(halves[hh], dov, 0.0) for hh in range(2)]
            delta = [jnp.sum(dom[hh] * ov, axis=-1, keepdims=True) for hh in range(2)]
            dom = [d.astype(BF) for d in dom]
            dp = [lax.dot_general(dom[hh], vv, NT_DIMS, preferred_element_type=F32) for hh in range(2)]
            ds = [(ps[hh] * (dp[hh] - delta[hh])).astype(BF) for hh in range(2)]
            pb = [p.astype(BF) for p in ps]
            dv = (acc[2] + lax.dot_general(pb[0], dom[0], TN_DIMS, preferred_element_type=F32)
                  + lax.dot_general(pb[1], dom[1], TN_DIMS, preferred_element_type=F32))
            dk = [acc[hh] + lax.dot_general(ds[hh], qh[hh], TN_DIMS, preferred_element_type=F32) for hh in range(2)]
            for hh in range(2):
                dq_ref[rows, hh * LANE:(hh + 1) * LANE] += jnp.dot(ds[hh], kk[hh], preferred_element_type=F32)
            return dk[0], dk[1], dv

        zero = jnp.zeros((tk, LANE), F32)
        acc = step(ki, (zero, zero, zero), True)
        acc = lax.fori_loop(ki + 1, jnp.where(ki == 0, nq, ki + 1), lambda qi, a: step(qi, a, True), acc)
        acc = lax.fori_loop(ki + 1, jnp.where(ki == 0, ki + 1, nq), lambda qi, a: step(qi, a, False), acc)
        dk_ref[:, 0:LANE] = acc[0] * (1.0 / LOG2E)
        dk_ref[:, LANE:2 * LANE] = acc[1] * (1.0 / LOG2E)
        dv_ref[...] = acc[2]
        if carry:
            pl.when((pair == n_pair - 1) & (ki == nq - 1))(finish)

    full = lambda w, off=0: pl.BlockSpec((t, w), lambda p, ki: (0, p + off))
    blk = lambda w: pl.BlockSpec((tk, w), lambda p, ki: (ki, p))
    outs = pl.pallas_call(
        body,
        out_shape=(_sds((t, 2048), F32), _sds((t, 2048), F32), _sds((t, 1024), F32))
        + tuple(carry.out_shapes() if carry else ()),
        grid=(n_pair, nq),
        in_specs=[full(2 * LANE), blk(2 * LANE), blk(LANE), full(LANE), full(LANE), full(LANE, 8)] + [ANY] * nx,
        out_specs=(full(2 * LANE), blk(2 * LANE), blk(LANE)) + (ANY,) * nx,
        scratch_shapes=carry.scratch() if carry else [],
        compiler_params=_cp(("arbitrary", "arbitrary") if carry else ("parallel", "arbitrary")),
        name="attn_bwd_carrying" if carry else "attn_bwd",
    )(q_cat, k_cat, v, o, lse, dyab, *(carry.arrs if carry else ()))
    return outs[0], outs[1], outs[2], list(outs[3:])


N_PAIR = SSD_HEADS // 2


def _hdot(a, b):
    return jnp.dot(a, b, precision=HI, preferred_element_type=F32)


def _ssd_chunk(xs, bg, cg, dtraw, hin, dt_bias, a_log, dskip, rowmask):
    ln = CHUNK
    causal = lax.broadcasted_iota(jnp.int32, (ln, ln), 0) >= lax.broadcasted_iota(jnp.int32, (ln, ln), 1)
    ltri = causal.astype(F32)
    lane = lax.broadcasted_iota(jnp.int32, (ln, LANE), 1)
    halves = (lane < 64, lane >= 64)
    low_row = lax.broadcasted_iota(jnp.int32, (1, LANE), 1) < 64
    head_lane = lax.broadcasted_iota(jnp.int32, (1, SSD_HEADS), 1)
    head_row = lax.broadcasted_iota(jnp.int32, (SSD_HEADS, 1), 0)

    def col(a, h):
        return jnp.sum(jnp.where(head_lane == h, a, 0.0), axis=1, keepdims=True)

    dt = _softplus(dtraw + dt_bias) * rowmask
    da = dt * (-jnp.exp(a_log))
    acs = _hdot(ltri, da)
    acs_t = lax.dot_general(da, ltri, (((0,), (1,)), ((), ())), precision=HI, preferred_element_type=F32)
    tot = jnp.sum(da, axis=0, keepdims=True)
    bm = [b * rowmask for b in bg]
    cm = [c * rowmask for c in cg]
    cb = [lax.dot_general(cm[g].astype(BF), bm[g].astype(BF), NT_DIMS, preferred_element_type=F32) for g in range(2)]
    ys, hout = [], []
    for p in range(N_PAIR):
        g = p // (N_PAIR // 2)
        h0, h1 = 2 * p, 2 * p + 1
        xdt = xs[p] * jnp.where(halves[0], col(dt, h0), col(dt, h1))
        a_cols = [col(acs, h0), col(acs, h1)]
        tot_cols = [col(tot, h0), col(tot, h1)]
        y = jnp.zeros((ln, LANE), F32)
        snew = jnp.zeros((ln, LANE), F32)
        for hh in range(2):
            a_row = jnp.sum(jnp.where(head_row == h0 + hh, acs_t, 0.0), axis=0, keepdims=True)
            dec = jnp.exp(jnp.where(causal, a_cols[hh] - a_row, NEG))
            xm = jnp.where(halves[hh], xdt, 0.0).astype(BF)
            y = y + jnp.dot((cb[g] * dec).astype(BF), xm, preferred_element_type=F32)
            bd = bm[g] * jnp.exp(tot_cols[hh] - a_cols[hh])
            snew = snew + lax.dot_general(bd.astype(BF), xm, TN_DIMS, preferred_element_type=F32)
        y_off = (jnp.dot(cm[g].astype(BF), hin[p].astype(BF), preferred_element_type=F32)
                 * jnp.where(halves[0], jnp.exp(a_cols[0]), jnp.exp(a_cols[1])))
        ys.append(y + y_off + jnp.where(low_row, col(dskip, h0), col(dskip, h1)) * xs[p])
        hout.append(jnp.where(low_row, jnp.exp(tot_cols[0]), jnp.exp(tot_cols[1])) * hin[p] + snew)
    return ys, hout


def _ssd_load(x_ref, dt_ref):
    xs = [x_ref[:, p * LANE:(p + 1) * LANE] for p in range(N_PAIR)]
    bg = [x_ref[:, SSD_D_INNER + g * LANE:SSD_D_INNER + (g + 1) * LANE] for g in range(2)]
    cg = [x_ref[:, SSD_D_INNER + (2 + g) * LANE:SSD_D_INNER + (3 + g) * LANE] for g in range(2)]
    return xs, bg, cg, dt_ref[:, 0:SSD_HEADS]


def _chunk_rowmask(c):
    return ((c * CHUNK + lax.broadcasted_iota(jnp.int32, (CHUNK, 1), 0)) >= PAD).astype(F32)


def ssd_fwd(xbc_c, proj, dt_bias, a_log, dskip):
    t = xbc_c.shape[0]
    nc = t // CHUNK

    def body(x_ref, dt_ref, dtb_ref, al_ref, d_ref, y_ref, hs_ref, h_s):
        c = pl.program_id(0)

        @pl.when(c == 0)
        def _():
            h_s[...] = jnp.zeros_like(h_s)

        xs, bg, cg, dtraw = _ssd_load(x_ref, dt_ref)
        hin = [h_s[p] for p in range(N_PAIR)]
        hs_ref[0] = h_s[...]
        ys, hout = _ssd_chunk(xs, bg, cg, dtraw, hin, dtb_ref[...], al_ref[...], d_ref[...], _chunk_rowmask(c))
        for p in range(N_PAIR):
            y_ref[:, p * LANE:(p + 1) * LANE] = ys[p]
            h_s[p] = hout[p]

    par = pl.BlockSpec((1, SSD_HEADS), lambda c: (0, 0))
    return pl.pallas_call(
        body,
        out_shape=(_sds((t, SSD_D_INNER), F32), _sds((nc, N_PAIR, CHUNK, LANE), F32)),
        grid=(nc,),
        in_specs=[pl.BlockSpec((CHUNK, SSD_CONV_CH), lambda c: (c, 0)),
                  pl.BlockSpec((CHUNK, LANE), lambda c: (c, OFF_DT // LANE)), par, par, par],
        out_specs=(pl.BlockSpec((CHUNK, SSD_D_INNER), lambda c: (c, 0)),
                   pl.BlockSpec((1, N_PAIR, CHUNK, LANE), lambda c: (c, 0, 0, 0))),
        scratch_shapes=[pltpu.VMEM((N_PAIR, CHUNK, LANE), F32)],
        compiler_params=_cp(("arbitrary",)),
        name="ssd_fwd",
    )(xbc_c, proj, dt_bias.reshape(1, -1), a_log.reshape(1, -1), dskip.reshape(1, -1))


def ssd_bwd(xbc_c, proj, dt_bias, a_log, dskip, hs, dy):
    t = xbc_c.shape[0]
    nc = t // CHUNK

    def body(x_ref, dt_ref, dtb_ref, al_ref, d_ref, hs_ref, dy_ref, dx_ref, ddt_ref, dpar_ref, dh_s):
        ci = pl.program_id(0)
        c = nc - 1 - ci

        @pl.when(ci == 0)
        def _():
            dh_s[...] = jnp.zeros_like(dh_s)
            dpar_ref[...] = jnp.zeros_like(dpar_ref)

        xs, bg, cg, dtraw = _ssd_load(x_ref, dt_ref)
        hin = [hs_ref[0, p] for p in range(N_PAIR)]
        rowmask = _chunk_rowmask(c)
        fn = lambda xs_, bg_, cg_, dtraw_, hin_, dtb_, al_, d_: _ssd_chunk(xs_, bg_, cg_, dtraw_, hin_, dtb_, al_, d_, rowmask)
        _, vjp = jax.vjp(fn, xs, bg, cg, dtraw, hin, dtb_ref[...], al_ref[...], d_ref[...])
        dys = [dy_ref[:, p * LANE:(p + 1) * LANE] for p in range(N_PAIR)]
        dhs = [dh_s[p] for p in range(N_PAIR)]
        dxs, dbg, dcg, ddtraw, dhin, ddtb, dal, dd = vjp((dys, dhs))
        for p in range(N_PAIR):
            dx_ref[:, p * LANE:(p + 1) * LANE] = dxs[p]
            dh_s[p] = dhin[p]
        for g in range(2):
            dx_ref[:, SSD_D_INNER + g * LANE:SSD_D_INNER + (g + 1) * LANE] = dbg[g]
            dx_ref[:, SSD_D_INNER + (2 + g) * LANE:SSD_D_INNER + (3 + g) * LANE] = dcg[g]
        ddt_ref[...] = jnp.zeros_like(ddt_ref)
        ddt_ref[:, 0:SSD_HEADS] = ddtraw
        dpar_ref[0:1, 0:SSD_HEADS] += ddtb
        dpar_ref[1:2, 0:SSD_HEADS] += dal
        dpar_ref[2:3, 0:SSD_HEADS] += dd

    par = pl.BlockSpec((1, SSD_HEADS), lambda ci: (0, 0))
    return pl.pallas_call(
        body,
        out_shape=(_sds((t, SSD_CONV_CH), F32), _sds((t, LANE), F32), _sds((8, LANE), F32)),
        grid=(nc,),
        in_specs=[pl.BlockSpec((CHUNK, SSD_CONV_CH), lambda ci: (nc - 1 - ci, 0)),
                  pl.BlockSpec((CHUNK, LANE), lambda ci: (nc - 1 - ci, OFF_DT // LANE)), par, par, par,
                  pl.BlockSpec((1, N_PAIR, CHUNK, LANE), lambda ci: (nc - 1 - ci, 0, 0, 0)),
                  pl.BlockSpec((CHUNK, SSD_D_INNER), lambda ci: (nc - 1 - ci, 0))],
        out_specs=(pl.BlockSpec((CHUNK, SSD_CONV_CH), lambda ci: (nc - 1 - ci, 0)),
                   pl.BlockSpec((CHUNK, LANE), lambda ci: (nc - 1 - ci, 0)),
                   pl.BlockSpec((8, LANE), lambda ci: (0, 0))),
        scratch_shapes=[pltpu.VMEM((N_PAIR, CHUNK, LANE), F32)],
        compiler_params=_cp(("arbitrary",)),
        name="ssd_bwd",
    )(xbc_c, proj, dt_bias.reshape(1, -1), a_log.reshape(1, -1), dskip.reshape(1, -1), hs, dy)


def _neg_expm1(y):
    series = -(y * (1.0 + y * (0.5 + y * (1.0 / 6.0 + y * (1.0 / 24.0 + y * (1.0 / 120.0))))))
    return jnp.where(y > -0.1, series, 1.0 - jnp.exp(y))


def _rg_pw(xr, wa, ba, wi, bi, lam, rowmask):
    xb = xr.astype(BF)
    r = jax.nn.sigmoid(jnp.dot(xb, wa.astype(BF), preferred_element_type=F32) + ba)
    i = jax.nn.sigmoid(jnp.dot(xb, wi.astype(BF), preferred_element_type=F32) + bi)
    log_a = -LRU_C * r * _softplus(-lam)
    a = jnp.exp(log_a)
    u = jnp.sqrt(_neg_expm1(2.0 * log_a)) * (i * xr) * rowmask
    return a, u


def _gelu_grad(x):
    c = math.sqrt(2.0 / math.pi)
    th = jnp.tanh(c * (x + 0.044715 * (x * x * x)))
    return 0.5 * (1.0 + th) + 0.5 * x * (1.0 - th * th) * c * (1.0 + 3.0 * 0.044715 * x * x)


def _scan_fwd(a, u):
    n = a.shape[0]
    row = lax.broadcasted_iota(jnp.int32, a.shape, 0)
    s = 1
    while s < n:
        a_s = jnp.where(row >= s, pltpu.roll(a, s, 0), 1.0)
        u_s = jnp.where(row >= s, pltpu.roll(u, s, 0), 0.0)
        u = u + a * u_s
        a = a * a_s
        s *= 2
    return a, u


def _scan_bwd(b, d):
    n = b.shape[0]
    row = lax.broadcasted_iota(jnp.int32, b.shape, 0)
    s = 1
    while s < n:
        b_s = jnp.where(row < n - s, pltpu.roll(b, n - s, 0), 1.0)
        d_s = jnp.where(row < n - s, pltpu.roll(d, n - s, 0), 0.0)
        d = d + b * d_s
        b = b * b_s
        s *= 2
    return d


def rg_fwd(xr_pre, gate_pre, rgp, w_a, w_i):
    t = xr_pre.shape[0]
    tr = _rt(t)

    def body(x_ref, g_ref, p_ref, wa_ref, wi_ref, hg_ref, hs_ref, prev, hcar):
        ti = pl.program_id(1)

        @pl.when(ti == 0)
        def _():
            prev[...] = jnp.zeros_like(prev)
            hcar[...] = jnp.zeros_like(hcar)

        xv = x_ref[...]
        ext = jnp.concatenate([prev[...], xv], axis=0)
        xr = _conv_pre(ext, xv, p_ref, tr)
        rowmask = _row_mask(ti, tr, (tr, 1)).astype(F32)
        a, u = _rg_pw(xr, wa_ref[0], p_ref[5:6, :], wi_ref[0], p_ref[6:7, :], p_ref[7:8, :], rowmask)
        a_cum, h_loc = _scan_fwd(a, u)
        hs = h_loc + a_cum * hcar[0:1, :]
        hs_ref[...] = hs
        hg_ref[...] = (hs * _gelu(g_ref[...])).astype(hg_ref.dtype)
        hcar[...] = jnp.broadcast_to(hs[tr - 1:tr, :], (8, LANE))
        prev[...] = xv[tr - 8:, :]

    return pl.pallas_call(
        body,
        out_shape=(_sds((t, LRU_WIDTH), BF), _sds((t, LRU_WIDTH), F32)),
        grid=(LRU_BLOCKS, t // tr),
        in_specs=[pl.BlockSpec((tr, LANE), lambda n, ti: (ti, n)),
                  pl.BlockSpec((tr, LANE), lambda n, ti: (ti, n)),
                  pl.BlockSpec((8, LANE), lambda n, ti: (0, n)),
                  pl.BlockSpec((1, LANE, LANE), lambda n, ti: (n, 0, 0)),
                  pl.BlockSpec((1, LANE, LANE), lambda n, ti: (n, 0, 0))],
        out_specs=(pl.BlockSpec((tr, LANE), lambda n, ti: (ti, n)), pl.BlockSpec((tr, LANE), lambda n, ti: (ti, n))),
        scratch_shapes=[pltpu.VMEM((8, LANE), F32), pltpu.VMEM((8, LANE), F32)],
        compiler_params=_cp(("parallel", "arbitrary")),
        name="rg_fwd",
    )(xr_pre, gate_pre, rgp, w_a, w_i)


def rg_bwd(xr_pre, gate_pre, rgp, w_a, w_i, hs, dhg):
    t = xr_pre.shape[0]
    tr = _rt(t)
    nt = t // tr
    r8 = tr // 8

    def body(x_ref, xp_ref, g_ref, p_ref, wa_ref, wi_ref, hs_ref, hp_ref, dhg_ref,
             dx_ref, dg_ref, dp_ref, dwa_ref, dwi_ref, gcar, dnext):
        ti = pl.program_id(1)
        tt = nt - 1 - ti

        @pl.when(ti == 0)
        def _():
            gcar[...] = jnp.zeros_like(gcar)
            dnext[...] = jnp.zeros_like(dnext)
            dp_ref[...] = jnp.zeros_like(dp_ref)
            dwa_ref[...] = jnp.zeros_like(dwa_ref)
            dwi_ref[...] = jnp.zeros_like(dwi_ref)

        xv = x_ref[...]
        halo = jnp.where(tt > 0, xp_ref[...], 0.0)
        ext = jnp.concatenate([halo, xv], axis=0)
        xr = _conv_pre(ext, xv, p_ref, tr)
        rowmask = _row_mask(tt, tr, (tr, 1)).astype(F32)
        fn = lambda xr_, wa_, ba_, wi_, bi_, lam_: _rg_pw(xr_, wa_, ba_, wi_, bi_, lam_, rowmask)
        (a, _), vjp = jax.vjp(fn, xr, wa_ref[0], p_ref[5:6, :], wi_ref[0], p_ref[6:7, :], p_ref[7:8, :])
        gpre = g_ref[...]
        hsv = hs_ref[...]
        dhg_v = dhg_ref[...]
        dg_ref[...] = (dhg_v * hsv * _gelu_grad(gpre)).astype(dg_ref.dtype)
        row = lax.broadcasted_iota(jnp.int32, (tr, LANE), 0)
        d = dhg_v * _gelu(gpre) + jnp.where(row == tr - 1, gcar[0:1, :], 0.0)
        b = jnp.where(row < tr - 1, pltpu.roll(a, tr - 1, 0), 0.0)
        g = _scan_bwd(b, d)
        gcar[...] = jnp.broadcast_to(a[0:1, :] * g[0:1, :], (8, LANE))
        hlast = jnp.where(tt > 0, hp_ref[7:8, :], 0.0)
        hprev = jnp.where(row == 0, hlast, pltpu.roll(hsv, 1, 0))
        dxr, dwa, dba, dwi, dbi, dlam = vjp((g * hprev, g))
        dx, sums = _conv_bwd_parts(dxr, dnext[...], xv, ext, p_ref, tr)
        dx_ref[...] = dx.astype(dx_ref.dtype)
        dnext[...] = dxr[:8, :]
        dp_ref[...] += _rows_block(sums + [dba, dbi, dlam])
        dwa_ref[0] += dwa
        dwi_ref[0] += dwi

    tile = lambda off=0: pl.BlockSpec((tr, LANE), lambda n, ti: (nt - 1 - ti, off + n))
    halo = lambda off=0: pl.BlockSpec((8, LANE), lambda n, ti: (jnp.maximum((nt - 1 - ti) * r8 - 1, 0), off + n))
    par = pl.BlockSpec((8, LANE), lambda n, ti: (0, n))
    wspec = pl.BlockSpec((1, LANE, LANE), lambda n, ti: (n, 0, 0))
    return pl.pallas_call(
        body,
        out_shape=(_sds((t, LRU_WIDTH), BF), _sds((t, LRU_WIDTH), BF), _sds((8, LRU_WIDTH), F32),
                   _sds((LRU_BLOCKS, LANE, LANE), F32), _sds((LRU_BLOCKS, LANE, LANE), F32)),
        grid=(LRU_BLOCKS, nt),
        in_specs=[tile(), halo(), tile(), par, wspec, wspec, tile(), halo(), tile()],
        out_specs=(tile(), tile(), par, wspec, wspec),
        scratch_shapes=[pltpu.VMEM((8, LANE), F32), pltpu.VMEM((8, LANE), F32)],
        compiler_params=_cp(("parallel", "arbitrary")),
        name="rg_bwd",
    )(xr_pre, xr_pre, gate_pre, rgp, w_a, w_i, hs, hs, dhg)


PACK_W = 1024
MESH_ID = pl.DeviceIdType.MESH
ANY = pl.BlockSpec(memory_space=pl.ANY)


def _my_place():
    x, y, c = lax.axis_index("x"), lax.axis_index("y"), lax.axis_index("c")
    return x, y, c


def _lin(px, py, pc):
    return 4 * px + 2 * py + pc


class Exchange:
    def __init__(self, kind, arrs):
        self.kind, self.arrs, self.k = kind, list(arrs), len(arrs)

    def out_shapes(self):
        if self.kind == "gather":
            return [_sds((N_DEV,) + a.shape, a.dtype) for a in self.arrs]
        return [_sds(a.shape, a.dtype) for a in self.arrs]

    def scratch(self):
        k = self.k
        return [pltpu.SemaphoreType.DMA((k, 7)), pltpu.SemaphoreType.DMA((k, 7)), pltpu.SemaphoreType.DMA((k,))]

    def phases(self, ins, outs, sems):
        return (self._gather if self.kind == "gather" else self._scatter)(ins, outs, *sems)

    def _gather(self, ins, outs, send_sems, recv_sems, local_sems):
        k = self.k
        x, y, c = _my_place()
        me, sibling = (x, y, c), (x, y, 1 - c)
        chips = [(1 - x, y), (x, 1 - y), (1 - x, 1 - y)]

        def copy(a, sem, block, to, from_input=False):
            slab = outs[a].at[_lin(*block)]
            return pltpu.make_async_remote_copy(
                src_ref=ins[a] if from_input else slab, dst_ref=slab,
                send_sem=send_sems.at[a, sem], recv_sem=recv_sems.at[a, sem],
                device_id=to, device_id_type=MESH_ID)

        def mine():
            return [pltpu.make_async_copy(ins[a], outs[a].at[_lin(*me)], local_sems.at[a]) for a in range(k)]

        def first():
            out = []
            for a in range(k):
                out.append(copy(a, 0, me, sibling, True))
                out += [copy(a, 1 + j, me, (*chip, c), True) for j, chip in enumerate(chips)]
            return out

        def passed():
            return [copy(a, 4 + j, (*chip, c), sibling) for j, chip in enumerate(chips) for a in range(k)]

        def start():
            for cp in mine() + first():
                cp.start()

        def middle():
            onward = passed()
            for j, chip in enumerate(chips):
                for a in range(k):
                    copy(a, 1 + j, (*chip, c), me).wait_recv()
                    onward[j * k + a].start()

        def finish():
            for a in range(k):
                copy(a, 0, sibling, me).wait_recv()
                for j, chip in enumerate(chips):
                    copy(a, 4 + j, (*chip, 1 - c), me).wait_recv()
            for cp in first() + passed():
                cp.wait_send()
            for cp in mine():
                cp.wait()

        return start, middle, finish

    def _scatter(self, ins, outs, send_sems, recv_sems, local_sems):
        k = self.k
        x, y, c = _my_place()
        me = _lin(x, y, c)
        peers = [((1 - x) if r & 4 else x, (1 - y) if r & 2 else y, (1 - c) if r & 1 else c) for r in range(1, N_DEV)]

        def copy(a, r, src_slab, dst_slab, to):
            return pltpu.make_async_remote_copy(
                src_ref=ins[a].at[src_slab], dst_ref=outs[a].at[dst_slab],
                send_sem=send_sems.at[a, r], recv_sem=recv_sems.at[a, r],
                device_id=to, device_id_type=MESH_ID)

        def mine():
            return [pltpu.make_async_copy(ins[a].at[me], outs[a].at[me], local_sems.at[a]) for a in range(k)]

        def sends():
            return [copy(a, r, _lin(*peer), me, peer) for r, peer in enumerate(peers) for a in range(k)]

        def start():
            for cp in mine() + sends():
                cp.start()

        def middle():
            pass

        def finish():
            for r, peer in enumerate(peers):
                for a in range(k):
                    copy(a, r, me, _lin(*peer), peer).wait_recv()
            for cp in sends():
                cp.wait_send()
            for cp in mine():
                cp.wait()

        return start, middle, finish

    def run(self, name):
        k = self.k

        def body(*refs):
            start, middle, finish = self.phases(refs[:k], refs[k:2 * k], refs[2 * k:])
            start()
            middle()
            finish()

        return pl.pallas_call(
            body,
            out_shape=tuple(self.out_shapes()),
            in_specs=[ANY] * k,
            out_specs=tuple(ANY for _ in range(k)),
            scratch_shapes=self.scratch(),
            name=name,
        )(*self.arrs)


def all_gather(arrs, name):
    return Exchange("gather", arrs).run(name)


def all_to_all(arrs, name):
    return Exchange("scatter", arrs).run(name)


def slab_sum(a, name):
    _, r, w = a.shape
    tr = _pick(r, (256, 128, 64, 32, 16, 8))

    def body(a_ref, o_ref):
        acc = a_ref[0].astype(F32)
        for d in range(1, N_DEV):
            acc = acc + a_ref[d].astype(F32)
        o_ref[...] = acc

    return pl.pallas_call(
        body,
        out_shape=_sds((r, w), F32),
        grid=(r // tr,),
        in_specs=[pl.BlockSpec((N_DEV, tr, w), lambda i: (0, i, 0))],
        out_specs=pl.BlockSpec((tr, w), lambda i: (i, 0)),
        compiler_params=_cp(("parallel",)),
        name=name,
    )(a)


def _adam_update(w, g, m, v):
    nm = ADAM_B1 * m + (1.0 - ADAM_B1) * g
    nv = ADAM_B2 * v + (1.0 - ADAM_B2) * (g * g)
    m_hat = nm / (1.0 - ADAM_B1 ** ADAM_STEP)
    v_hat = nv / (1.0 - ADAM_B2 ** ADAM_STEP)
    return -ADAM_LR * (m_hat / (jnp.sqrt(v_hat) + ADAM_EPS) + ADAM_WD * w), nm, nv


def adamw_blocks(w, m, v, parts, name):
    nl, r, c = w.shape
    tr = next(t for t in (256, 160, 128, 64, 32, 16) if r % t == 0 and N_DEV * t * c * 2 <= 2 * 1024 * 1024)

    def body(w_ref, m_ref, v_ref, *rest):
        part_refs, (g_ref, d_ref, nm_ref, nv_ref) = rest[:nl], rest[nl:]
        layer = pl.program_id(0)
        for idx in range(nl):
            @pl.when(layer == idx)
            def _(idx=idx):
                g = part_refs[idx][0].astype(F32)
                for dev in range(1, N_DEV):
                    g = g + part_refs[idx][dev].astype(F32)
                g_ref[...] = g
                d_ref[...], nm_ref[...], nv_ref[...] = _adam_update(w_ref[...], g, m_ref[...], v_ref[...])

    spec = pl.BlockSpec((None, tr, c), lambda l, i: (l, i, 0))
    part_spec = lambda idx: pl.BlockSpec((N_DEV, tr, c), lambda l, i: (0, jnp.where(l == idx, i, 0), 0))
    return pl.pallas_call(
        body,
        out_shape=tuple(_sds((nl, r, c), F32) for _ in range(4)),
        grid=(nl, r // tr),
        in_specs=[spec] * 3 + [part_spec(idx) for idx in range(nl)],
        out_specs=(spec,) * 4,
        compiler_params=_cp(("arbitrary", "arbitrary")),
        name=name,
    )(w, m, v, *parts)


def adamw(w, g, m, v, name):
    r, c = w.shape
    tr = _pick(r, (256, 160, 128, 64, 32, 16, 8))

    def body(w_ref, g_ref, m_ref, v_ref, d_ref, nm_ref, nv_ref):
        d_ref[...], nm_ref[...], nv_ref[...] = _adam_update(w_ref[...], g_ref[...], m_ref[...], v_ref[...])

    spec = pl.BlockSpec((tr, c), lambda i: (i, 0))
    return pl.pallas_call(
        body,
        out_shape=tuple(_sds((r, c), F32) for _ in range(3)),
        grid=(r // tr,),
        in_specs=[spec] * 4,
        out_specs=(spec, spec, spec),
        compiler_params=_cp(("parallel",)),
        name=name,
    )(w, g, m, v)


def _relu2_epi(acc):
    r = jnp.maximum(acc, 0.0)
    return r * r, r


def _drelu2_epi(acc, r):
    return (acc * (2.0 * r.astype(F32)),)


def mlp_fwd(h, g_pre, g_post, w_up, w_down, hn=None, g_next=None):
    if hn is None:
        hn = norm_fwd(h, g_pre, BF, name="mlp_norm")
    u, r = matmul(hn, w_up, "nn", (BF, BF), epi=_relu2_epi, name="mlp_up")
    d = matmul(u, w_down, "nn", name="mlp_down")
    h2, hn_next = resadd_fwd(h, d, g_post, g_next, name="mlp_res")
    return h2, (h, hn, u, r, d), hn_next


def mlp_bwd(res, dh2, g_pre, g_post, w_up, w_down, post=None, then=None):
    h, hn, u, r, d = res
    dd, dg_post = post if post is not None else norm_bwd(d, g_post, dh2, mask_pad=True, out_dtype=BF, name="mlp_post_bwd")
    dw_down = matmul(u, dd, "tn", (BF,), name="mlp_dwdown").reshape(w_down.g8.shape)
    dp = matmul(dd, w_down, "nt", (BF,), epi=_drelu2_epi, extras=(r,), name="mlp_du")
    dw_up = matmul(hn, dp, "tn", (BF,), out_blocks=True, name="mlp_dwup")
    dhn = matmul(dp, w_up, "nt", name="mlp_dhn")
    dh, dg_pre, *below = norm_bwd(h, g_pre, dhn, dres=dh2, then=then, name="mlp_pre_bwd")
    return dh, dict(mlp_pre_g=dg_pre, mlp_post_g=dg_post, w_up=dw_up, w_down=dw_down), (tuple(below) or None)


def rg_layer_fwd(h, g_pre, g_post, w_x, w_y, rgp, w_a, w_i, w_out, hn=None, g_next=None):
    if hn is None:
        hn = norm_fwd(h, g_pre, BF, name="rg_norm")
    xr = matmul(hn, w_x, "nn", name="rg_in_x")
    gp = matmul(hn, w_y, "nn", name="rg_in_y")
    hg, hs = rg_fwd(xr, gp, rgp, w_a, w_i)
    m = matmul(hg, w_out, "nn", name="rg_out")
    h2, hn_next = resadd_fwd(h, m, g_post, g_next, name="rg_res")
    return h2, (h, hn, xr, gp, hg, hs, m), hn_next


def rg_layer_bwd(res, dh2, g_pre, g_post, w_x, w_y, rgp, w_a, w_i, w_out, post=None, then=None):
    h, hn, xr, gp, hg, hs, m = res
    dm, dg_post = post if post is not None else norm_bwd(m, g_post, dh2, mask_pad=True, out_dtype=BF, name="rg_post_bwd")
    dw_out = matmul(hg, dm, "tn", name="rg_dwout")
    dhg = matmul(dm, w_out, "nt", name="rg_dhg")
    dxr, dgp, drgp, dwa, dwi = rg_bwd(xr, gp, rgp, w_a, w_i, hs, dhg)
    dw_x = matmul(hn, dxr, "tn", name="rg_dwx")
    dw_y = matmul(hn, dgp, "tn", name="rg_dwy")
    dhn = matmul([dxr, dgp], [w_x, w_y], "nt", name="rg_dhn")
    dh, dg_pre, *below = norm_bwd(h, g_pre, dhn, dres=dh2, then=then, name="rg_pre_bwd")
    grads = dict(mix_pre_g=dg_pre, mix_post_g=dg_post, rg_w_x=dw_x, rg_w_y=dw_y,
                 rg_conv_w=drgp[0:4], rg_conv_b=drgp[4], rg_b_a=drgp[5], rg_b_i=drgp[6], rg_lambda=drgp[7],
                 rg_w_a=dwa, rg_w_i=dwi, rg_w_out=dw_out)
    return dh, grads, (tuple(below) or None)


def sm_layer_fwd(h, g_pre, g_post, w_in_p, convp, dt_bias, a_log, dskip, ssd_g, q_g, w_q_p, kv_g, w_kv_p, w_out, tabs,
                 carry=None, on_carried=None, hn=None, g_next=None):
    if hn is None:
        hn = norm_fwd(h, g_pre, BF, name="sm_norm")
    proj = matmul(hn, w_in_p, "nn", name="sm_in")
    xbc_c = conv_silu_fwd(proj, OFF_XBC, SSD_CONV_CH, convp, name="ssd_conv")
    y, hst = ssd_fwd(xbc_c, proj, dt_bias, a_log, dskip)
    y_ssd = gated_norm_fwd(y, proj, ssd_g)
    cqn = norm_fwd(proj, q_g, BF, col_blk=OFF_CQ // MLA_Q_RANK, width=MLA_Q_RANK, name="q_norm")
    q_raw = matmul(cqn, w_q_p, "nn", name="q_up")
    ckvn = norm_fwd(proj, kv_g, BF, col_blk=OFF_CKV // MLA_KV_RANK, width=MLA_KV_RANK, name="kv_norm")
    kv_raw = matmul(ckvn, w_kv_p, "nn", name="kv_up")
    q_cat, k_cat, v = rope_fwd(q_raw, kv_raw, proj, tabs)
    o, lse, carried = attn_fwd(q_cat, k_cat, v, carry)
    if on_carried is not None:
        on_carried(carried)
    w_out = w_out()
    half = w_out.shape[0] // 2
    m = matmul([y_ssd, o], [KBlock(w_out, half, 0), KBlock(w_out, half, 1)], "nn", name="sm_out")
    res = (h, hn, proj, xbc_c, y, hst, cqn, ckvn, q_cat, k_cat, v, o, lse, y_ssd, m)
    h2, hn_next = resadd_fwd(h, m, g_post, g_next, name="sm_res")
    return h2, res, hn_next


def sm_layer_bwd(res, dh2, g_pre, g_post, w_in_p, convp, dt_bias, a_log, dskip, ssd_g, q_g, w_q_p, kv_g, w_kv_p, w_out, tabs,
                 carry=None, post=None, then=None):
    h, hn, proj, xbc_c, y, hst, cqn, ckvn, q_cat, k_cat, v, o, lse, y_ssd, m = res
    w_out = w_out()
    dm, dg_post = post if post is not None else norm_bwd(m, g_post, dh2, mask_pad=True, out_dtype=BF, name="sm_post_bwd")
    dw_out = jnp.concatenate([matmul(y_ssd, dm, "tn", name="sm_dwout_ssd"), matmul(o, dm, "tn", name="sm_dwout_att")], axis=0)
    dyab = matmul(dm, w_out, "nt", name="sm_dyab")
    dq_cat, dk_cat, dv, carried = attn_bwd(q_cat, k_cat, v, o, lse, dyab, carry(dw_out) if carry is not None else None)
    dq_raw, dkr = rope_bwd(dq_cat, dk_cat, tabs)
    kw = MLA_HEADS * LANE
    dw_kv_p = jnp.concatenate([matmul(ckvn, dk_cat, "tn", name="kv_dw_k"), matmul(ckvn, dv, "tn", name="kv_dw_v")], axis=1)
    dckvn = matmul([dk_cat, dv], [KBlock(w_kv_p, kw, 0), KBlock(w_kv_p, kw // 2, 2)], "nt", name="kv_dx")
    dckv, dg_kv = norm_bwd(proj, kv_g, dckvn, out_dtype=BF, col_blk=OFF_CKV // MLA_KV_RANK, width=MLA_KV_RANK,
                           name="kv_norm_bwd")
    dw_q_p = matmul(cqn, dq_raw, "tn", name="q_dw")
    dcqn = matmul(dq_raw, w_q_p, "nt", name="q_dx")
    dcq, dg_q = norm_bwd(proj, q_g, dcqn, out_dtype=BF, col_blk=OFF_CQ // MLA_Q_RANK, width=MLA_Q_RANK, name="q_norm_bwd")
    dy, dz, dg_ssd = gated_norm_bwd(y, proj, ssd_g, dyab)
    dxbc_c, ddt, dpar = ssd_bwd(xbc_c, proj, dt_bias, a_log, dskip, hst, dy)
    dxbc, dconvp = conv_silu_bwd(proj, OFF_XBC, SSD_CONV_CH, convp, dxbc_c, name="ssd_conv_bwd")
    pieces = [dz, dxbc, dckv, ddt, dkr, dcq]
    dw_in_p = jnp.concatenate([matmul(hn, pc, "tn", (BF,), name="sm_dwin_%d" % i) for i, pc in enumerate(pieces)], axis=1)
    third = SSD_CONV_CH // 3
    a_terms = [dz] + [KBlock(dxbc, third, i) for i in range(3)] + [dckv, ddt, dkr, dcq]
    b_terms = ([KBlock(w_in_p, SSD_D_INNER, 0)] + [KBlock(w_in_p, third, OFF_XBC // third + i) for i in range(3)]
               + [KBlock(w_in_p, MLA_KV_RANK, OFF_CKV // MLA_KV_RANK), KBlock(w_in_p, LANE, OFF_DT // LANE),
                  KBlock(w_in_p, LANE, OFF_KR // LANE), KBlock(w_in_p, MLA_Q_RANK, OFF_CQ // MLA_Q_RANK)])
    dhn = matmul(a_terms, b_terms, "nt", name="sm_dhn")
    dh, dg_pre, *below = norm_bwd(h, g_pre, dhn, dres=dh2, then=then, name="sm_pre_bwd")
    grads = dict(mix_pre_g=dg_pre, mix_post_g=dg_post, w_in=w_in_cols_to_blocks(dw_in_p), ssd_conv_w=dconvp[0:4],
                 ssd_conv_b=dconvp[4], ssd_dt_bias=dpar[0, :SSD_HEADS], ssd_a_log=dpar[1, :SSD_HEADS],
                 ssd_d=dpar[2, :SSD_HEADS], ssd_norm_g=dg_ssd, mla_q_norm_g=dg_q, mla_w_q_up=_unpack_w_q(dw_q_p),
                 mla_kv_norm_g=dg_kv, mla_w_kv_up=_unpack_w_kv(dw_kv_p), w_out_ab=dw_out)
    return dh, grads, carried, (tuple(below) or None)


W_IN_COLS = 3248
W_IN_SHARD = W_IN_COLS // N_DEV
W_IN_WIRE = 512


def _w_in_tables():
    src = np.full((IN_W,), -1, np.int64)
    src[0:2560] = np.arange(2560)
    src[OFF_CKV:OFF_CKV + 256] = 2960 + np.arange(256)
    src[OFF_DT:OFF_DT + 16] = 2560 + np.arange(16)
    src[OFF_KR + 64:OFF_KR + 96] = 3216 + np.arange(32)
    src[OFF_CQ:OFF_CQ + 384] = 2576 + np.arange(384)
    dev = np.where(src >= 0, src // W_IN_SHARD, -1).astype(np.int32).reshape(1, IN_W)
    col = np.where(src >= 0, src % W_IN_SHARD, 0).astype(np.int32).reshape(1, IN_W)
    return dev, col


W_IN_TILE = 384


def _w_in_devices_of_tile(dev):
    return [sorted(set(dev[0, t * W_IN_TILE:(t + 1) * W_IN_TILE].tolist()) - {-1}) for t in range(IN_W // W_IN_TILE)]


def _any_of(index, values):
    cond = index == values[0]
    for v in values[1:]:
        cond = cond | (index == v)
    return cond


def w_in_blocks_to_cols(g8):
    _, k, wp = g8.shape
    tn = W_IN_TILE
    dev, col = _w_in_tables()
    holders = _w_in_devices_of_tile(dev)

    def body(g_ref, dev_ref, col_ref, o_ref):
        i = pl.program_id(0)
        row = lax.broadcasted_iota(jnp.int32, (wp, tn), 0)
        o_ref[...] = jnp.zeros_like(o_ref)
        for j in range(N_DEV):
            tiles = [t for t, devs in enumerate(holders) if j in devs]
            if tiles:
                @pl.when(_any_of(i, tiles))
                def _(j=j):
                    sel = ((row == col_ref[...]) & (dev_ref[...] == j)).astype(BF)
                    o_ref[...] += jnp.dot(g_ref[j], sel, preferred_element_type=F32).astype(o_ref.dtype)

    dev, col = jnp.asarray(dev), jnp.asarray(col)
    return pl.pallas_call(
        body,
        out_shape=_sds((k, IN_W), BF),
        grid=(IN_W // tn,),
        in_specs=[pl.BlockSpec((N_DEV, k, wp), lambda i: (0, 0, 0)), pl.BlockSpec((1, tn), lambda i: (0, i)),
                  pl.BlockSpec((1, tn), lambda i: (0, i))],
        out_specs=pl.BlockSpec((k, tn), lambda i: (0, i)),
        compiler_params=_cp(("parallel",)),
        name="w_in_cols",
    )(g8, dev, col)


def w_in_cols_to_blocks(dw):
    k = dw.shape[0]
    tn = W_IN_TILE
    dev, col = _w_in_tables()
    holders = _w_in_devices_of_tile(dev)

    def body(dw_ref, dev_ref, col_ref, o_ref):
        j = pl.program_id(0)
        row = lax.broadcasted_iota(jnp.int32, (W_IN_WIRE, tn), 0)
        o_ref[...] = jnp.zeros_like(o_ref)
        for t, devs in enumerate(holders):
            if devs:
                @pl.when(_any_of(j, devs))
                def _(t=t):
                    cols = slice(t * tn, (t + 1) * tn)
                    sel = ((row == col_ref[:, cols]) & (dev_ref[:, cols] == j)).astype(BF)
                    o_ref[0] += lax.dot_general(dw_ref[:, cols], sel, NT_DIMS,
                                                preferred_element_type=F32).astype(o_ref.dtype)

    dev, col = jnp.asarray(dev), jnp.asarray(col)
    return pl.pallas_call(
        body,
        out_shape=_sds((N_DEV, k, W_IN_WIRE), BF),
        grid=(N_DEV,),
        in_specs=[pl.BlockSpec((k, IN_W), lambda j: (0, 0)), pl.BlockSpec((1, IN_W), lambda j: (0, 0)),
                  pl.BlockSpec((1, IN_W), lambda j: (0, 0))],
        out_specs=pl.BlockSpec((1, k, W_IN_WIRE), lambda j: (j, 0, 0)),
        compiler_params=_cp(("parallel",)),
        name="w_in_blocks",
    )(dw, dev, col)


def _pack_w_q(w):
    w3 = w.reshape(w.shape[0], MLA_HEADS, MLA_NOPE + MLA_ROPE)
    return jnp.pad(w3, ((0, 0), (0, 0), (0, LANE - MLA_NOPE - MLA_ROPE))).reshape(w.shape[0], MLA_HEADS * LANE)


def _unpack_w_q(p):
    return p.reshape(p.shape[0], MLA_HEADS, LANE)[:, :, :MLA_NOPE + MLA_ROPE].reshape(p.shape[0], -1)


def _pack_w_kv(w):
    w3 = w.reshape(w.shape[0], MLA_HEADS, MLA_NOPE + MLA_V)
    k = jnp.pad(w3[:, :, :MLA_NOPE], ((0, 0), (0, 0), (0, LANE - MLA_NOPE))).reshape(w.shape[0], MLA_HEADS * LANE)
    return jnp.concatenate([k, w3[:, :, MLA_NOPE:].reshape(w.shape[0], MLA_HEADS * MLA_V)], axis=1)


def _unpack_w_kv(p):
    k = p[:, :MLA_HEADS * LANE].reshape(p.shape[0], MLA_HEADS, LANE)[:, :, :MLA_NOPE]
    v = p[:, MLA_HEADS * LANE:].reshape(p.shape[0], MLA_HEADS, MLA_V)
    return jnp.concatenate([k, v], axis=2).reshape(p.shape[0], -1)


def _rows8(rows, width):
    a = jnp.concatenate([r.reshape(-1, width) for r in rows], axis=0)
    return jnp.pad(a, ((0, 8 - a.shape[0]), (0, 0)))


SLAB_ROWS = 16


def _to_slab(flat_list, lead=()):
    cat = jnp.concatenate(flat_list, axis=-1)
    n = cat.shape[-1]
    unit = SLAB_ROWS * PACK_W
    total = -(-n // unit) * unit
    cat = jnp.pad(cat, [(0, 0)] * len(lead) + [(0, total - n)])
    return cat.reshape(lead + (total // PACK_W, PACK_W))


def _from_flat(flat, shapes):
    out, off = [], 0
    for s in shapes:
        n = int(np.prod(s))
        out.append(flat[off:off + n].reshape(s))
        off += n
    return out


def _gathered_full(g8, axis):
    moved = jnp.moveaxis(g8, 0, axis)
    shp = moved.shape
    return moved.reshape(shp[:axis] + (shp[axis] * shp[axis + 1],) + shp[axis + 2:])


def _per_device(full, axis):
    shp = full.shape
    split = full.reshape(shp[:axis] + (N_DEV, shp[axis] // N_DEV) + shp[axis + 1:])
    return jnp.moveaxis(split, axis, 0)


ARG_NAMES = ['x', 'meta_tokens', 'mix_pre_g', 'mix_post_g', 'mlp_pre_g', 'mlp_post_g', 'w_up', 'w_down', 'w_in',
             'ssd_conv_w', 'ssd_conv_b', 'ssd_dt_bias', 'ssd_a_log', 'ssd_d', 'ssd_norm_g', 'mla_q_norm_g',
             'mla_w_q_up', 'mla_kv_norm_g', 'mla_w_kv_up', 'w_out_ab', 'rg_w_x', 'rg_w_y', 'rg_conv_w', 'rg_conv_b',
             'rg_w_a', 'rg_b_a', 'rg_w_i', 'rg_b_i', 'rg_lambda', 'rg_w_out']
WEIGHTS = ARG_NAMES[1:]
BIG = {'w_up': 2, 'w_down': 1, 'w_in': 2, 'mla_w_q_up': 2, 'mla_w_kv_up': 2, 'w_out_ab': 1, 'rg_w_x': 2,
       'rg_w_y': 2, 'rg_w_out': 1}
SMALL = {'meta_tokens': 1, 'ssd_conv_w': 2, 'rg_conv_w': 2, 'rg_conv_b': 1, 'rg_b_a': 1, 'rg_b_i': 1, 'rg_lambda': 1}
REPL = [n for n in WEIGHTS if n not in BIG and n not in SMALL]
REPL_MEDIUM = ['rg_w_a', 'rg_w_i']
REPL_TINY = [n for n in REPL if n not in REPL_MEDIUM]


def _piece_axes():
    axes = {}
    for n, ax in BIG.items():
        for i in range(DEPTH if n in ('w_up', 'w_down') else DEPTH // 2):
            axes[(n, i)] = ax - 1
    return axes


PIECE_AXIS = _piece_axes()
AS_BLOCKS = ('w_up', 'w_down')
_RG = lambda i: [(n, i) for n in ('rg_w_x', 'rg_w_y', 'rg_w_out')]
_MLP = lambda l: [('w_up', l), ('w_down', l)]
_SM_IN = lambda i: [(n, i) for n in ('w_in', 'mla_w_q_up', 'mla_w_kv_up')]
GATHER_FIRST = _SM_IN(0)
GATHER_AT = {0: [('w_out_ab', 0)] + _MLP(0) + _RG(0) + _MLP(1) + _SM_IN(1), 2: [('w_out_ab', 1)] + _MLP(2) + _RG(1) + _MLP(3)}
SCATTER_AT = {2: _MLP(3) + _RG(1) + _MLP(2) + [('w_out_ab', 1)],
              0: _SM_IN(1) + _MLP(1) + _RG(0) + _MLP(0) + [('w_out_ab', 0)]}
SCATTER_LAST = _SM_IN(0)


def _wire_block(p, key):
    n, i = key
    blk = p[n][i]
    if n == 'w_in':
        blk = jnp.pad(blk, ((0, 0), (0, W_IN_WIRE - blk.shape[1])))
    return blk


def _step(p, moments):
    assert DEPTH == 4
    full = {n: [None] * p[n].shape[0] for n in BIG}
    full['w_in_g'] = [None] * p['w_in'].shape[0]

    def weight_blocks(group):
        return [_wire_block(p, k).astype(BF) for k in group]

    def take_weights(group, gathered):
        for (n, i), piece in zip(group, gathered):
            if n == 'w_in':
                full['w_in_g'][i] = piece
            elif n in AS_BLOCKS:
                full[n][i] = DevBlocks(piece, PIECE_AXIS[(n, i)])
            else:
                full[n][i] = _gathered_full(piece, PIECE_AXIS[(n, i)])

    def grad_blocks(group, gw):
        return [gw[k] if k[0] in AS_BLOCKS or k[0] == 'w_in' else _per_device(gw[k], PIECE_AXIS[k]).astype(BF)
                for k in group]

    parts = {}

    small_slab = _to_slab([p[n].reshape(-1) for n in SMALL])
    *first, small8 = all_gather(weight_blocks(GATHER_FIRST) + [small_slab], name="gather_first")
    take_weights(GATHER_FIRST, first)
    for n, piece in zip(SMALL, _from_flat_rows(small8, [p[n].shape for n in SMALL])):
        full[n] = _gathered_full(piece, SMALL[n])
    for n in REPL:
        full[n] = p[n]
    loss_local, grad_x, gw, gsmall_full, carried = _local_step(
        full, p['x'][0], p['loss_target'][0],
        fwd_carry=lambda layer: Exchange("gather", weight_blocks(GATHER_AT[layer])),
        on_fwd_carried=lambda layer, got: take_weights(GATHER_AT[layer], got),
        bwd_carry=lambda layer, gw_now, others: Exchange(
            "scatter", grad_blocks(SCATTER_AT[layer], gw_now)
            + ([jnp.stack(others[n], axis=0).reshape(N_DEV, -1, LANE) for n in REPL_MEDIUM] if layer == 0 else [])))

    for layer, group in SCATTER_AT.items():
        parts.update(zip(group, carried[layer]))
    rep_flat = jnp.concatenate([gsmall_full[n].reshape(-1) for n in REPL_TINY])
    rep_n = rep_flat.shape[0]
    rep_chunk = -(-rep_n // (N_DEV * PACK_W * 8)) * PACK_W * 8
    rep8 = jnp.pad(rep_flat, (0, N_DEV * rep_chunk - rep_n)).reshape(N_DEV, rep_chunk)
    gsmall = _to_slab([_per_device(gsmall_full[n], SMALL[n]).reshape(N_DEV, -1) for n in SMALL] + [rep8], lead=(N_DEV,))
    received = all_to_all(grad_blocks(SCATTER_LAST, gw) + [gsmall], name="scatter_last")
    n_last = len(SCATTER_LAST)
    parts.update(zip(SCATTER_LAST, received[:n_last]))
    ssmall = slab_sum(received[n_last], name="sum_small").reshape(-1)
    medium_mine = [slab_sum(r8, name="sum_" + n) for n, r8 in zip(REPL_MEDIUM, carried[0][len(SCATTER_AT[0]):])]
    g_loc = {'w_in': jnp.stack([slab_sum(parts[('w_in', i)], name="sum_w_in_%d" % i)[:, :W_IN_SHARD]
                                for i in range(p['w_in'].shape[0])], axis=0)}
    small_n = sum(int(np.prod(p[n].shape)) for n in SMALL)
    g_loc.update(zip(SMALL, _from_flat(ssmall, [p[n].shape for n in SMALL])))
    rep_mine = ssmall[small_n:small_n + rep_chunk].reshape(-1, PACK_W)
    rep_all, *medium_all = all_gather([rep_mine] + medium_mine, name="gather_replicated")
    g_loc.update(zip(REPL_TINY, _from_flat(rep_all.reshape(-1), [p[n].shape for n in REPL_TINY])))
    g_loc.update({n: g.reshape(p[n].shape) for n, g in zip(REPL_MEDIUM, medium_all)})

    out = {'loss': lax.psum(loss_local, ("x", "y", "c")), 'grad_x': grad_x[None]}
    small_names = list(SMALL) + REPL_TINY
    for n in list(BIG) + REPL_MEDIUM:
        shp = p[n].shape
        if n == 'w_in' or n in REPL_MEDIUM:
            v2 = lambda a: a.reshape(-1, shp[-1])
            d, nm, nv = adamw(v2(p[n]), v2(g_loc[n]), v2(moments['m_' + n]), v2(moments['v_' + n]), name="adamw_" + n)
            d, nm, nv = d.reshape(shp), nm.reshape(shp), nv.reshape(shp)
        else:
            g_loc[n], d, nm, nv = adamw_blocks(p[n], moments['m_' + n], moments['v_' + n],
                                               [parts[(n, i)] for i in range(shp[0])], name="adamw_" + n)
        out['delta_' + n], out['new_m_' + n], out['new_v_' + n] = d, nm, nv
    slab = lambda src: _to_slab([src(n).reshape(-1) for n in small_names])
    d, nm, nv = adamw(slab(lambda n: p[n]), slab(lambda n: g_loc[n]), slab(lambda n: moments['m_' + n]),
                      slab(lambda n: moments['v_' + n]), name="adamw_small")
    shapes = [p[n].shape for n in small_names]
    for key, flat in (('delta_', d), ('new_m_', nm), ('new_v_', nv)):
        for n, a in zip(small_names, _from_flat(flat.reshape(-1), shapes)):
            out[key + n] = a
    for n in WEIGHTS:
        out['grad_' + n] = g_loc[n]
    return out


def _local_step(full, x, target_rows, fwd_carry=None, on_fwd_carried=None, bwd_carry=None):
    t = PAD + N_META + x.shape[0]
    h = jnp.concatenate([jnp.zeros((PAD, D_MODEL), F32), full['meta_tokens'], x], axis=0)
    target = jnp.concatenate([jnp.zeros((PAD + N_META, D_MODEL), F32), target_rows], axis=0)
    tabs = rope_tables(t)

    def layer_args(layer):
        i = layer // 2
        if layer % 2 == 0:
            convp = _rows8([full['ssd_conv_w'][i], full['ssd_conv_b'][i]], SSD_CONV_CH)
            return (full['mix_pre_g'][layer], full['mix_post_g'][layer], w_in_blocks_to_cols(full['w_in_g'][i]), convp,
                    full['ssd_dt_bias'][i], full['ssd_a_log'][i], full['ssd_d'][i], full['ssd_norm_g'][i],
                    full['mla_q_norm_g'][i], _pack_w_q(full['mla_w_q_up'][i]), full['mla_kv_norm_g'][i],
                    _pack_w_kv(full['mla_w_kv_up'][i]), lambda: full['w_out_ab'][i], tabs)
        rgp = _rows8([full['rg_conv_w'][i], full['rg_conv_b'][i], full['rg_b_a'][i], full['rg_b_i'][i],
                      full['rg_lambda'][i]], LRU_WIDTH)
        return (full['mix_pre_g'][layer], full['mix_post_g'][layer], full['rg_w_x'][i], full['rg_w_y'][i], rgp,
                full['rg_w_a'][i], full['rg_w_i'][i], full['rg_w_out'][i])

    def mlp_args(layer):
        return (full['mlp_pre_g'][layer], full['mlp_post_g'][layer], full['w_up'][layer], full['w_down'][layer])

    saved = []
    hn = None
    for layer in range(DEPTH):
        la = layer_args(layer)
        to_mlp = dict(hn=hn, g_next=full['mlp_pre_g'][layer])
        if layer % 2 == 0:
            if fwd_carry is not None:
                h, res_mix, hn = sm_layer_fwd(h, *la, carry=fwd_carry(layer),
                                              on_carried=lambda got, layer=layer: on_fwd_carried(layer, got), **to_mlp)
            else:
                h, res_mix, hn = sm_layer_fwd(h, *la, **to_mlp)
        else:
            h, res_mix, hn = rg_layer_fwd(h, *la, **to_mlp)
        ma = mlp_args(layer)
        h, res_mlp, hn = mlp_fwd(h, *ma, hn=hn, g_next=full['mix_pre_g'][layer + 1] if layer + 1 < DEPTH else None)
        saved.append((la, ma, res_mix, res_mlp))
    loss_local, dh = loss_fwd_bwd(h, target)

    others = {n: [None] * len(full[n]) for n in WEIGHTS if n not in BIG and n != 'meta_tokens'}
    gw, carried = {}, {}
    post = None
    for layer in reversed(range(DEPTH)):
        la, ma, res_mix, res_mlp = saved[layer]
        dh, gm, post = mlp_bwd(res_mlp, dh, *ma, post=post, then=(res_mix[-1], la[1]))
        below = (saved[layer - 1][3][-1], saved[layer - 1][1][1]) if layer > 0 else None
        if layer % 2 == 0:
            for n in ('w_up', 'w_down'):
                gw[(n, layer)] = gm[n]
            carry = None
            if bwd_carry is not None:
                carry = lambda dw_out, layer=layer: bwd_carry(layer, {**gw, ('w_out_ab', layer // 2): dw_out}, others)
            dh, gx, carried[layer], post = sm_layer_bwd(res_mix, dh, *la, carry=carry, post=post, then=below)
        else:
            dh, gx, post = rg_layer_bwd(res_mix, dh, *la, post=post, then=below)
        for n, g in list(gm.items()) + list(gx.items()):
            i = layer if n in ('mix_pre_g', 'mix_post_g', 'mlp_pre_g', 'mlp_post_g', 'w_up', 'w_down') else layer // 2
            if n in BIG:
                gw[(n, i)] = g
            else:
                others[n][i] = g
    gothers = {n: jnp.stack(v, axis=0) for n, v in others.items()}
    gothers['meta_tokens'] = dh[PAD:PAD + N_META]
    return loss_local, dh[PAD + N_META:], gw, gothers, carried


def _from_flat_rows(g8, shapes):
    flat = g8.reshape(N_DEV, -1)
    out, off = [], 0
    for s in shapes:
        n = int(np.prod(s))
        out.append(flat[:, off:off + n].reshape((N_DEV,) + tuple(s)))
        off += n
    return out


def kernel(x, meta_tokens, mix_pre_g, mix_post_g, mlp_pre_g, mlp_post_g, w_up, w_down, w_in, ssd_conv_w, ssd_conv_b, ssd_dt_bias, ssd_a_log, ssd_d, ssd_norm_g, mla_q_norm_g, mla_w_q_up, mla_kv_norm_g, mla_w_kv_up, w_out_ab, rg_w_x, rg_w_y, rg_conv_w, rg_conv_b, rg_w_a, rg_b_a, rg_w_i, rg_b_i, rg_lambda, rg_w_out, loss_target, m_meta_tokens, m_mix_pre_g, m_mix_post_g, m_mlp_pre_g, m_mlp_post_g, m_w_up, m_w_down, m_w_in, m_ssd_conv_w, m_ssd_conv_b, m_ssd_dt_bias, m_ssd_a_log, m_ssd_d, m_ssd_norm_g, m_mla_q_norm_g, m_mla_w_q_up, m_mla_kv_norm_g, m_mla_w_kv_up, m_w_out_ab, m_rg_w_x, m_rg_w_y, m_rg_conv_w, m_rg_conv_b, m_rg_w_a, m_rg_b_a, m_rg_w_i, m_rg_b_i, m_rg_lambda, m_rg_w_out, v_meta_tokens, v_mix_pre_g, v_mix_post_g, v_mlp_pre_g, v_mlp_post_g, v_w_up, v_w_down, v_w_in, v_ssd_conv_w, v_ssd_conv_b, v_ssd_dt_bias, v_ssd_a_log, v_ssd_d, v_ssd_norm_g, v_mla_q_norm_g, v_mla_w_q_up, v_mla_kv_norm_g, v_mla_w_kv_up, v_w_out_ab, v_rg_w_x, v_rg_w_y, v_rg_conv_w, v_rg_conv_b, v_rg_w_a, v_rg_b_a, v_rg_w_i, v_rg_b_i, v_rg_lambda, v_rg_w_out):
    args = (x, meta_tokens, mix_pre_g, mix_post_g, mlp_pre_g, mlp_post_g, w_up, w_down, w_in, ssd_conv_w, ssd_conv_b, ssd_dt_bias, ssd_a_log, ssd_d, ssd_norm_g, mla_q_norm_g, mla_w_q_up, mla_kv_norm_g, mla_w_kv_up, w_out_ab, rg_w_x, rg_w_y, rg_conv_w, rg_conv_b, rg_w_a, rg_b_a, rg_w_i, rg_b_i, rg_lambda, rg_w_out, loss_target, m_meta_tokens, m_mix_pre_g, m_mix_post_g, m_mlp_pre_g, m_mlp_post_g, m_w_up, m_w_down, m_w_in, m_ssd_conv_w, m_ssd_conv_b, m_ssd_dt_bias, m_ssd_a_log, m_ssd_d, m_ssd_norm_g, m_mla_q_norm_g, m_mla_w_q_up, m_mla_kv_norm_g, m_mla_w_kv_up, m_w_out_ab, m_rg_w_x, m_rg_w_y, m_rg_conv_w, m_rg_conv_b, m_rg_w_a, m_rg_b_a, m_rg_w_i, m_rg_b_i, m_rg_lambda, m_rg_w_out, v_meta_tokens, v_mix_pre_g, v_mix_post_g, v_mlp_pre_g, v_mlp_post_g, v_w_up, v_w_down, v_w_in, v_ssd_conv_w, v_ssd_conv_b, v_ssd_dt_bias, v_ssd_a_log, v_ssd_d, v_ssd_norm_g, v_mla_q_norm_g, v_mla_w_q_up, v_mla_kv_norm_g, v_mla_w_kv_up, v_w_out_ab, v_rg_w_x, v_rg_w_y, v_rg_conv_w, v_rg_conv_b, v_rg_w_a, v_rg_b_a, v_rg_w_i, v_rg_b_i, v_rg_lambda, v_rg_w_out,)
    n_w = len(ARG_NAMES)
    p = dict(zip(ARG_NAMES, args[:n_w]))
    p['loss_target'] = args[n_w]
    moments = {}
    for i, n in enumerate(WEIGHTS):
        moments['m_' + n] = args[n_w + 1 + i]
        moments['v_' + n] = args[n_w + 1 + len(WEIGHTS) + i]
    out = _step(p, moments)
    res = [out['loss'], out['grad_x']]
    for prefix in ('grad_', 'delta_', 'new_m_', 'new_v_'):
        res += [out[prefix + n] for n in WEIGHTS]
    return tuple(res)
```

```python
import math

import numpy as np
import jax
import jax.numpy as jnp
from jax import lax
from jax.experimental import pallas as pl
from jax.experimental.pallas import tpu as pltpu

F32 = jnp.float32
BF = jnp.bfloat16
HI = lax.Precision.HIGHEST

D_MODEL = 1024
DEPTH = 4
N_META = 16
CHUNK = 128
PAD = CHUNK - N_META
EPS = 1e-6
SSD_HEADS = 16
SSD_HEAD_DIM = 64
SSD_D_INNER = 1024
SSD_STATE = 128
SSD_CONV_CH = 1536
MLA_HEADS = 16
MLA_NOPE = 64
MLA_ROPE = 32
MLA_V = 64
MLA_Q_RANK = 384
MLA_KV_RANK = 256
ROPE_BASE = 10000.0
LRU_WIDTH = 1280
LRU_BLOCKS = 10
LRU_C = 8.0
D_FF = 4096
N_DEV = 8
LANE = 128
IN_W = 3456
OFF_Z, OFF_XBC, OFF_CKV, OFF_DT, OFF_KR, OFF_CQ = 0, 1024, 2560, 2816, 2944, 3072

ADAM_LR = 0.001
ADAM_B1 = 0.9
ADAM_B2 = 0.999
ADAM_EPS = 1e-08
ADAM_WD = 0.01
ADAM_STEP = 10

VMEM_LIMIT = 56 * 1024 * 1024
NEG = -1e30


def _pick(n, cands):
    for c in cands:
        if n % c == 0:
            return c
    return n


def _cp(sem=None):
    return pltpu.CompilerParams(dimension_semantics=sem, vmem_limit_bytes=VMEM_LIMIT)


def _sds(shape, dtype):
    return jax.ShapeDtypeStruct(tuple(shape), dtype)


def _silu(x):
    return x * jax.nn.sigmoid(x)


def _softplus(x):
    return jnp.maximum(x, 0.0) + jnp.log(1.0 + jnp.exp(-jnp.abs(x)))


def _gelu(x):
    c = math.sqrt(2.0 / math.pi)
    return 0.5 * x * (1.0 + jnp.tanh(c * (x + 0.044715 * (x * x * x))))


def _row_mask(i, tr, shape, first_valid=PAD):
    row = i * tr + lax.broadcasted_iota(jnp.int32, shape, 0)
    return row >= first_valid


class KBlock:
    def __init__(self, arr, width, blk):
        self.arr, self.width, self.blk = arr, width, blk


class DevBlocks:
    def __init__(self, g8, axis):
        self.g8, self.axis = g8, axis
        _, r, c = g8.shape
        self.shape = (N_DEV * r, c) if axis == 0 else (r, N_DEV * c)


NN_DIMS = (((1,), (0,)), ((), ()))
MM_TALL_K = 1536


def matmul(a, b, mode, out_dtypes=(F32,), epi=None, extras=(), name="mm", tm=None, tn=None, out_blocks=False):
    a_terms = a if isinstance(a, (list, tuple)) else [a]
    b_terms = b if isinstance(b, (list, tuple)) else [b]
    assert len(a_terms) == len(b_terms) and (mode != "tn" or len(a_terms) == 1)
    arr_of = lambda t: t.arr if isinstance(t, KBlock) else t
    if mode == "tn":
        m, n = a_terms[0].shape[1], b_terms[0].shape[1]
    else:
        m = arr_of(a_terms[0]).shape[0]
        b0 = b_terms[0]
        n = (b0.shape if isinstance(b0, DevBlocks) else arr_of(b0).shape)[1 if mode == "nn" else 0]
    if mode == "tn":
        tm = _pick(m, (1024, 512, 384, 256, 128))
    else:
        k_all = sum(t.width if isinstance(t, KBlock) else t.shape[1] for t in a_terms)
        tall = (2112,) if k_all <= MM_TALL_K else ()
        tm = tm or _pick(m, tall + (1056, 1024, 768, 640, 512, 384, 256, 128))
    tn = tn or _pick(n, (512, 640, 384, 256, 128))
    dims = {"nn": NN_DIMS, "nt": NT_DIMS, "tn": TN_DIMS}[mode]

    in_specs, args, plan = [], [], []
    for ta, tb in zip(a_terms, b_terms):
        if mode == "tn":
            k = ta.shape[0]
            in_specs += [pl.BlockSpec((k, tm), lambda i, j: (0, i)), pl.BlockSpec((k, tn), lambda i, j: (0, j))]
            args += [ta, tb]
            plan.append(None)
            continue
        if isinstance(ta, KBlock):
            kw, ka = ta.width, ta.blk
            in_specs.append(pl.BlockSpec((tm, kw), lambda i, j, ka=ka: (i, ka)))
        else:
            kw = ta.shape[1]
            in_specs.append(pl.BlockSpec((tm, kw), lambda i, j: (i, 0)))
        args.append(arr_of(ta))
        if isinstance(tb, DevBlocks):
            _, r, c = tb.g8.shape
            split_k = tb.axis == (0 if mode == "nn" else 1)
            if split_k:
                kd = r if mode == "nn" else c
                assert kw == N_DEV * kd
                blk = (N_DEV, kd, tn) if mode == "nn" else (N_DEV, tn, kd)
                in_specs.append(pl.BlockSpec(blk, (lambda i, j: (0, 0, j)) if mode == "nn" else (lambda i, j: (0, j, 0))))
                plan.append(kd)
            else:
                per = (c if mode == "nn" else r) // tn
                blk = (None, kw, tn) if mode == "nn" else (None, tn, kw)
                in_specs.append(pl.BlockSpec(blk, (lambda i, j, per=per: (j // per, 0, j % per)) if mode == "nn"
                                             else (lambda i, j, per=per: (j // per, j % per, 0))))
                plan.append(None)
            args.append(tb.g8)
        else:
            kb = tb.blk if isinstance(tb, KBlock) else 0
            assert (tb.width if isinstance(tb, KBlock) else tb.shape[0 if mode == "nn" else 1]) == kw
            in_specs.append(pl.BlockSpec((kw, tn), lambda i, j, kb=kb: (kb, j)) if mode == "nn"
                            else pl.BlockSpec((tn, kw), lambda i, j, kb=kb: (j, kb)))
            args.append(arr_of(tb))
            plan.append(None)
    n_terms, n_ex = len(plan), len(extras)

    def body(*refs):
        ex_refs, out_refs = refs[2 * n_terms:2 * n_terms + n_ex], refs[2 * n_terms + n_ex:]
        acc = None
        for t, kd in enumerate(plan):
            a_ref, b_ref = refs[2 * t], refs[2 * t + 1]
            if kd is None:
                parts = [lax.dot_general(a_ref[...].astype(BF), b_ref[...].astype(BF), dims, preferred_element_type=F32)]
            else:
                parts = [lax.dot_general(a_ref[:, d * kd:(d + 1) * kd].astype(BF), b_ref[d].astype(BF), dims,
                                         preferred_element_type=F32) for d in range(N_DEV)]
            for part in parts:
                acc = part if acc is None else acc + part
        outs = (acc,) if epi is None else epi(acc, *[r[...] for r in ex_refs])
        for r, o in zip(out_refs, outs):
            r[...] = o.astype(r.dtype)

    o_spec = pl.BlockSpec((tm, tn), lambda i, j: (i, j))
    if out_blocks:
        per = n // N_DEV // tn
        out_shape = tuple(_sds((N_DEV, m, n // N_DEV), dt) for dt in out_dtypes)
        out_specs = tuple(pl.BlockSpec((None, tm, tn), lambda i, j: (j // per, i, j % per)) for _ in out_dtypes)
    else:
        out_shape = tuple(_sds((m, n), dt) for dt in out_dtypes)
        out_specs = tuple(o_spec for _ in out_dtypes)
    outs = pl.pallas_call(
        body,
        out_shape=out_shape,
        grid=(m // tm, n // tn),
        in_specs=in_specs + [o_spec] * n_ex,
        out_specs=out_specs,
        compiler_params=_cp(("parallel", "parallel")),
        name=name,
    )(*args, *extras)
    return outs[0] if len(out_dtypes) == 1 else outs


def _rt(t):
    return _pick(t, (384, 256, 128))


def norm_fwd(x, g, out_dtype, col_blk=0, width=None, name="norm_fwd"):
    t = x.shape[0]
    w = width or x.shape[1]
    tr = _rt(t)

    def body(x_ref, g_ref, o_ref):
        xv = x_ref[...]
        r = lax.rsqrt(jnp.mean(xv * xv, axis=-1, keepdims=True) + EPS)
        o_ref[...] = (xv * r * g_ref[...]).astype(o_ref.dtype)

    return pl.pallas_call(
        body,
        out_shape=_sds((t, w), out_dtype),
        grid=(t // tr,),
        in_specs=[pl.BlockSpec((tr, w), lambda i: (i, col_blk)), pl.BlockSpec((1, w), lambda i: (0, 0))],
        out_specs=pl.BlockSpec((tr, w), lambda i: (i, 0)),
        compiler_params=_cp(("parallel",)),
        name=name,
    )(x, g.reshape(1, w))


def _rms_bwd(xv, gv, dyv):
    r = lax.rsqrt(jnp.mean(xv * xv, axis=-1, keepdims=True) + EPS)
    xh = xv * r
    dyg = dyv * gv
    dx = r * (dyg - xh * jnp.mean(dyg * xh, axis=-1, keepdims=True))
    return dx, jnp.sum(dyv * xh, axis=0, keepdims=True)


def norm_bwd(x, g, dy, dres=None, mask_pad=False, out_dtype=F32, col_blk=0, width=None, dy_col_blk=0, then=None,
             name="norm_bwd"):
    t = x.shape[0]
    w = width or x.shape[1]
    tr = _rt(t)
    has_res, has_then = dres is not None, then is not None

    def body(*refs):
        x_ref, g_ref, dy_ref = refs[:3]
        n_in = 3 + has_res + 2 * has_then
        dx_ref, dg_ref = refs[n_in:n_in + 2]
        i = pl.program_id(0)
        dyv = dy_ref[...].astype(F32)
        if mask_pad:
            dyv = jnp.where(_row_mask(i, tr, dyv.shape), dyv, 0.0)
        dx, dg = _rms_bwd(x_ref[...], g_ref[...], dyv)
        if has_res:
            dx = dx + refs[3][...]
        dx_ref[...] = dx.astype(dx_ref.dtype)

        @pl.when(i == 0)
        def _():
            for r in refs[n_in + 1::2]:
                r[...] = jnp.zeros_like(r)

        dg_ref[...] += dg
        if has_then:
            x2_ref, g2_ref = refs[3 + has_res:5 + has_res]
            dx2_ref, dg2_ref = refs[n_in + 2:]
            dx2, dg2 = _rms_bwd(x2_ref[...], g2_ref[...], jnp.where(_row_mask(i, tr, dx.shape), dx, 0.0))
            dx2_ref[...] = dx2.astype(dx2_ref.dtype)
            dg2_ref[...] += dg2

    row = pl.BlockSpec((tr, w), lambda i: (i, 0))
    vec = pl.BlockSpec((1, w), lambda i: (0, 0))
    in_specs = [pl.BlockSpec((tr, w), lambda i: (i, col_blk)), vec, pl.BlockSpec((tr, w), lambda i: (i, dy_col_blk))]
    args = [x, g.reshape(1, w), dy]
    out_shape, out_specs = [_sds((t, w), out_dtype), _sds((1, w), F32)], [row, vec]
    if has_res:
        in_specs.append(row)
        args.append(dres)
    if has_then:
        in_specs += [row, vec]
        args += [then[0], then[1].reshape(1, w)]
        out_shape += [_sds((t, w), BF), _sds((1, w), F32)]
        out_specs += [row, vec]
    outs = pl.pallas_call(
        body,
        out_shape=tuple(out_shape),
        grid=(t // tr,),
        in_specs=in_specs,
        out_specs=tuple(out_specs),
        compiler_params=_cp(("arbitrary",)),
        name=name,
    )(*args)
    if has_then:
        return outs[0], outs[1].reshape(w), outs[2], outs[3].reshape(w)
    return outs[0], outs[1].reshape(w)


def resadd_fwd(h, m, g, g_next=None, name="resadd"):
    t, w = h.shape
    tr = _rt(t)
    with_next = g_next is not None

    def body(h_ref, m_ref, g_ref, *rest):
        mv = m_ref[...]
        r = lax.rsqrt(jnp.mean(mv * mv, axis=-1, keepdims=True) + EPS)
        y = mv * r * g_ref[...]
        h2 = h_ref[...] + jnp.where(_row_mask(pl.program_id(0), tr, y.shape), y, 0.0)
        if with_next:
            gn_ref, o_ref, hn_ref = rest
            r2 = lax.rsqrt(jnp.mean(h2 * h2, axis=-1, keepdims=True) + EPS)
            hn_ref[...] = (h2 * r2 * gn_ref[...]).astype(hn_ref.dtype)
        else:
            (o_ref,) = rest
        o_ref[...] = h2

    row = pl.BlockSpec((tr, w), lambda i: (i, 0))
    vec = pl.BlockSpec((1, w), lambda i: (0, 0))
    outs = pl.pallas_call(
        body,
        out_shape=(_sds((t, w), F32),) + ((_sds((t, w), BF),) if with_next else ()),
        grid=(t // tr,),
        in_specs=[row, row, vec] + ([vec] if with_next else []),
        out_specs=(row,) + ((row,) if with_next else ()),
        compiler_params=_cp(("parallel",)),
        name=name,
    )(h, m, g.reshape(1, w), *((g_next.reshape(1, w),) if with_next else ()))
    return outs[0], (outs[1] if with_next else None)


def loss_fwd_bwd(h, target):
    t, w = h.shape
    tr = _rt(t)

    def body(h_ref, t_ref, s_ref, dh_ref):
        i = pl.program_id(0)
        err = h_ref[...] - t_ref[...]
        err = jnp.where(_row_mask(i, tr, err.shape, PAD + N_META), err, 0.0)
        dh_ref[...] = err * (1.0 / w)

        @pl.when(i == 0)
        def _():
            s_ref[...] = jnp.zeros_like(s_ref)

        s_ref[...] += jnp.sum(err * err).reshape(1, 1)

    s, dh = pl.pallas_call(
        body,
        out_shape=(_sds((1, LANE), F32), _sds((t, w), F32)),
        grid=(t // tr,),
        in_specs=[pl.BlockSpec((tr, w), lambda i: (i, 0)), pl.BlockSpec((tr, w), lambda i: (i, 0))],
        out_specs=(pl.BlockSpec((1, LANE), lambda i: (0, 0)), pl.BlockSpec((tr, w), lambda i: (i, 0))),
        compiler_params=_cp(("arbitrary",)),
        name="loss",
    )(h, target)
    return 0.5 * s[0, 0] / w, dh


def _shift_down(ext, k, n):
    return pltpu.roll(ext, k, 0)[8:]


def _conv_pre(ext, x, w_ref, n):
    return (w_ref[4:5, :] + w_ref[3:4, :] * x + w_ref[2:3, :] * _shift_down(ext, 1, n)
            + w_ref[1:2, :] * _shift_down(ext, 2, n) + w_ref[0:1, :] * _shift_down(ext, 3, n))


def _conv_bwd_parts(dpre, dnext, x, ext, w_ref, n):
    extd = jnp.concatenate([dpre, dnext], axis=0)
    ln = n + 8
    dx = (w_ref[3:4, :] * dpre + w_ref[2:3, :] * pltpu.roll(extd, ln - 1, 0)[:n]
          + w_ref[1:2, :] * pltpu.roll(extd, ln - 2, 0)[:n] + w_ref[0:1, :] * pltpu.roll(extd, ln - 3, 0)[:n])
    sums = [jnp.sum(dpre * _shift_down(ext, 3, n), axis=0, keepdims=True),
            jnp.sum(dpre * _shift_down(ext, 2, n), axis=0, keepdims=True),
            jnp.sum(dpre * _shift_down(ext, 1, n), axis=0, keepdims=True),
            jnp.sum(dpre * x, axis=0, keepdims=True),
            jnp.sum(dpre, axis=0, keepdims=True)]
    return dx, sums


def _rows_block(sums):
    w = sums[0].shape[1]
    row = lax.broadcasted_iota(jnp.int32, (8, w), 0)
    out = jnp.zeros((8, w), F32)
    for k, s in enumerate(sums):
        out = jnp.where(row == k, s, out)
    return out


CONV_BLOCK = 512


def conv_silu_fwd(x, col0, c, wb, name="conv_fwd"):
    t = x.shape[0]
    cw = _pick(c, (CONV_BLOCK, LANE))
    nblk, col0_blk = c // cw, col0 // cw
    assert col0 % cw == 0
    tr = _rt(t)

    def body(x_ref, w_ref, o_ref, prev):
        ti = pl.program_id(1)

        @pl.when(ti == 0)
        def _():
            prev[...] = jnp.zeros_like(prev)

        xv = x_ref[...]
        ext = jnp.concatenate([prev[...], xv], axis=0)
        o_ref[...] = _silu(_conv_pre(ext, xv, w_ref, tr))
        prev[...] = xv[tr - 8:, :]

    return pl.pallas_call(
        body,
        out_shape=_sds((t, c), F32),
        grid=(nblk, t // tr),
        in_specs=[pl.BlockSpec((tr, cw), lambda cb, ti: (ti, col0_blk + cb)),
                  pl.BlockSpec((8, cw), lambda cb, ti: (0, cb))],
        out_specs=pl.BlockSpec((tr, cw), lambda cb, ti: (ti, cb)),
        scratch_shapes=[pltpu.VMEM((8, cw), F32)],
        compiler_params=_cp(("parallel", "arbitrary")),
        name=name,
    )(x, wb)


def conv_silu_bwd(x, col0, c, wb, dout, name="conv_bwd"):
    t = x.shape[0]
    cw = _pick(c, (CONV_BLOCK, LANE))
    nblk, col0_blk = c // cw, col0 // cw
    assert col0 % cw == 0
    tr = _rt(t)
    nt = t // tr
    r8 = tr // 8

    def body(x_ref, xp_ref, w_ref, do_ref, dx_ref, dwb_ref, dnext):
        ti = pl.program_id(1)
        tt = nt - 1 - ti

        @pl.when(ti == 0)
        def _():
            dnext[...] = jnp.zeros_like(dnext)
            dwb_ref[...] = jnp.zeros_like(dwb_ref)

        xv = x_ref[...]
        halo = jnp.where(tt > 0, xp_ref[...], 0.0)
        ext = jnp.concatenate([halo, xv], axis=0)
        pre = _conv_pre(ext, xv, w_ref, tr)
        s = jax.nn.sigmoid(pre)
        dpre = do_ref[...] * (s + pre * s * (1.0 - s))
        dx, sums = _conv_bwd_parts(dpre, dnext[...], xv, ext, w_ref, tr)
        dx_ref[...] = dx.astype(dx_ref.dtype)
        dwb_ref[...] += _rows_block(sums)
        dnext[...] = dpre[:8, :]

    return pl.pallas_call(
        body,
        out_shape=(_sds((t, c), BF), _sds((8, c), F32)),
        grid=(nblk, nt),
        in_specs=[pl.BlockSpec((tr, cw), lambda cb, ti: (nt - 1 - ti, col0_blk + cb)),
                  pl.BlockSpec((8, cw), lambda cb, ti: (jnp.maximum((nt - 1 - ti) * r8 - 1, 0), col0_blk + cb)),
                  pl.BlockSpec((8, cw), lambda cb, ti: (0, cb)),
                  pl.BlockSpec((tr, cw), lambda cb, ti: (nt - 1 - ti, cb))],
        out_specs=(pl.BlockSpec((tr, cw), lambda cb, ti: (nt - 1 - ti, cb)),
                   pl.BlockSpec((8, cw), lambda cb, ti: (0, cb))),
        scratch_shapes=[pltpu.VMEM((8, cw), F32)],
        compiler_params=_cp(("parallel", "arbitrary")),
        name=name,
    )(x, x, wb, dout)


def gated_norm_fwd(y, proj, g, name="gnorm_fwd"):
    t, w = y.shape
    tr = _rt(t)

    def body(y_ref, z_ref, g_ref, o_ref):
        v = y_ref[...] * _silu(z_ref[...])
        r = lax.rsqrt(jnp.mean(v * v, axis=-1, keepdims=True) + EPS)
        o_ref[...] = (v * r * g_ref[...]).astype(o_ref.dtype)

    return pl.pallas_call(
        body,
        out_shape=_sds((t, w), BF),
        grid=(t // tr,),
        in_specs=[pl.BlockSpec((tr, w), lambda i: (i, 0)), pl.BlockSpec((tr, w), lambda i: (i, OFF_Z // w)),
                  pl.BlockSpec((1, w), lambda i: (0, 0))],
        out_specs=pl.BlockSpec((tr, w), lambda i: (i, 0)),
        compiler_params=_cp(("parallel",)),
        name=name,
    )(y, proj, g.reshape(1, w))


def gated_norm_bwd(y, proj, g, dyab, name="gnorm_bwd"):
    t, w = y.shape
    tr = _rt(t)

    def body(y_ref, z_ref, g_ref, do_ref, dy_ref, dz_ref, dg_ref):
        i = pl.program_id(0)
        yv, zv, dov = y_ref[...], z_ref[...], do_ref[...]
        s = jax.nn.sigmoid(zv)
        sz = zv * s
        v = yv * sz
        r = lax.rsqrt(jnp.mean(v * v, axis=-1, keepdims=True) + EPS)
        vh = v * r
        dvg = dov * g_ref[...]
        dv = r * (dvg - vh * jnp.mean(dvg * vh, axis=-1, keepdims=True))
        dy_ref[...] = dv * sz
        dz_ref[...] = (dv * yv * (s + sz * (1.0 - s))).astype(dz_ref.dtype)

        @pl.when(i == 0)
        def _():
            dg_ref[...] = jnp.zeros_like(dg_ref)

        dg_ref[...] += jnp.sum(dov * vh, axis=0, keepdims=True)

    dy, dz, dg = pl.pallas_call(
        body,
        out_shape=(_sds((t, w), F32), _sds((t, w), BF), _sds((1, w), F32)),
        grid=(t // tr,),
        in_specs=[pl.BlockSpec((tr, w), lambda i: (i, 0)), pl.BlockSpec((tr, w), lambda i: (i, OFF_Z // w)),
                  pl.BlockSpec((1, w), lambda i: (0, 0)), pl.BlockSpec((tr, w), lambda i: (i, 0))],
        out_specs=(pl.BlockSpec((tr, w), lambda i: (i, 0)), pl.BlockSpec((tr, w), lambda i: (i, 0)),
                   pl.BlockSpec((1, w), lambda i: (0, 0))),
        compiler_params=_cp(("arbitrary",)),
        name=name,
    )(y, proj, g.reshape(1, w), dyab)
    return dy, dz, dg.reshape(w)


def rope_tables(t):
    inv = ROPE_BASE ** (-jnp.arange(0, MLA_ROPE, 2, dtype=F32) / MLA_ROPE)
    pos = (jnp.arange(t, dtype=F32) - PAD)[:, None]
    ang = pos * inv[None, :]
    cos, sin = jnp.cos(ang), jnp.sin(ang)
    z16 = jnp.zeros((t, 16), F32)
    z32 = jnp.zeros((t, 32), F32)
    c = jnp.concatenate([jnp.ones((t, 64), F32), cos, cos, z32], axis=1)
    s1 = jnp.concatenate([jnp.zeros((t, 64), F32), z16, sin, z32], axis=1)
    s2 = jnp.concatenate([jnp.zeros((t, 64), F32), -sin, z16, z32], axis=1)
    return c, s1, s2


def _rope(x, c, s1, s2):
    return x * c + pltpu.roll(x, 16, 1) * s1 + pltpu.roll(x, LANE - 16, 1) * s2


def _rope_t(d, c, s1, s2):
    return d * c + pltpu.roll(d * s1, LANE - 16, 1) + pltpu.roll(d * s2, 16, 1)


def rope_fwd(q_raw, kv_raw, proj, tabs):
    t = q_raw.shape[0]
    tr = _rt(t)
    hw = MLA_HEADS * LANE

    def body(q_ref, k_ref, v_ref, kr_ref, c_ref, s1_ref, s2_ref, qo_ref, ko_ref, vo_ref):
        c, s1, s2 = c_ref[...], s1_ref[...], s2_ref[...]
        kr = _rope(kr_ref[...], c, s1, s2)
        for h in range(MLA_HEADS):
            sl = slice(h * LANE, (h + 1) * LANE)
            qo_ref[:, sl] = (_rope(q_ref[:, sl], c, s1, s2) * Q_PRESCALE).astype(BF)
            ko_ref[:, sl] = (k_ref[:, sl] + kr).astype(BF)
        vo_ref[...] = v_ref[...].astype(BF)

    tab_spec = pl.BlockSpec((tr, LANE), lambda i: (i, 0))
    return pl.pallas_call(
        body,
        out_shape=(_sds((t, hw), BF), _sds((t, hw), BF), _sds((t, 1024), BF)),
        grid=(t // tr,),
        in_specs=[pl.BlockSpec((tr, hw), lambda i: (i, 0)), pl.BlockSpec((tr, hw), lambda i: (i, 0)),
                  pl.BlockSpec((tr, 1024), lambda i: (i, 2)), pl.BlockSpec((tr, LANE), lambda i: (i, OFF_KR // LANE)),
                  tab_spec, tab_spec, tab_spec],
        out_specs=(pl.BlockSpec((tr, hw), lambda i: (i, 0)), pl.BlockSpec((tr, hw), lambda i: (i, 0)),
                   pl.BlockSpec((tr, 1024), lambda i: (i, 0))),
        compiler_params=_cp(("parallel",)),
        name="rope_fwd",
    )(q_raw, kv_raw, kv_raw, proj, *tabs)


def rope_bwd(dq_cat, dk_cat, tabs):
    t = dq_cat.shape[0]
    tr = _rt(t)
    hw = MLA_HEADS * LANE

    def body(dq_ref, dk_ref, c_ref, s1_ref, s2_ref, dqo_ref, dkr_ref):
        c, s1, s2 = c_ref[...], s1_ref[...], s2_ref[...]
        acc = jnp.zeros((tr, LANE), F32)
        for h in range(MLA_HEADS):
            sl = slice(h * LANE, (h + 1) * LANE)
            dqo_ref[:, sl] = _rope_t(dq_ref[:, sl] * ATT_SCALE, c, s1, s2).astype(BF)
            acc = acc + dk_ref[:, sl]
        lane = lax.broadcasted_iota(jnp.int32, (tr, LANE), 1)
        dkr_ref[...] = jnp.where((lane >= 64) & (lane < 96), _rope_t(acc, c, s1, s2), 0.0)

    tab_spec = pl.BlockSpec((tr, LANE), lambda i: (i, 0))
    return pl.pallas_call(
        body,
        out_shape=(_sds((t, hw), BF), _sds((t, LANE), F32)),
        grid=(t // tr,),
        in_specs=[pl.BlockSpec((tr, hw), lambda i: (i, 0)), pl.BlockSpec((tr, hw), lambda i: (i, 0)),
                  tab_spec, tab_spec, tab_spec],
        out_specs=(pl.BlockSpec((tr, hw), lambda i: (i, 0)), pl.BlockSpec((tr, LANE), lambda i: (i, 0))),
        compiler_params=_cp(("parallel",)),
        name="rope_bwd",
    )(dq_cat, dk_cat, *tabs)


ATT_SCALE = (MLA_NOPE + MLA_ROPE) ** -0.5
LOG2E = math.log2(math.e)
Q_PRESCALE = ATT_SCALE * LOG2E
CARRY_MIDDLE_PAIR = 6
NT_DIMS = (((1,), (1,)), ((), ()))
TN_DIMS = (((0,), (0,)), ((), ()))


def _att_mask(qi, ki, tq, tk):
    qpos = qi * tq + lax.broadcasted_iota(jnp.int32, (tq, tk), 0)
    kpos = ki * tk + lax.broadcasted_iota(jnp.int32, (tq, tk), 1)
    return (kpos <= qpos) & (kpos >= PAD)


def _half_masks(n):
    lane = lax.broadcasted_iota(jnp.int32, (n, LANE), 1)
    return lane < 64, lane >= 64


def _att_tile(t):
    return _pick(t, (384, 256, 128))


def _ds(i, n):
    return pl.ds(i * n, n) if isinstance(i, int) else pl.ds(pl.multiple_of(i * n, n), n)


FWD_PAIRS = 2


def attn_fwd(q_cat, k_cat, v, carry=None):
    t = q_cat.shape[0]
    tq = tk = _att_tile(t)
    nq = t // tq
    npair, nh = FWD_PAIRS, 2 * FWD_PAIRS
    n_grp = MLA_HEADS // nh
    nx = carry.k if carry else 0

    def body(*refs):
        q_ref, k_ref, v_ref = refs[:3]
        o_ref, lse_ref = refs[3 + nx:5 + nx]
        qi = pl.program_id(1)
        if carry:
            start, middle, finish = carry.phases(refs[3:3 + nx], refs[5 + nx:5 + 2 * nx], refs[5 + 2 * nx:])
            grp = pl.program_id(0)
            pl.when((grp == 0) & (qi == 0))(start)
            pl.when((grp == CARRY_MIDDLE_PAIR // npair) & (qi == 0))(middle)
        lo_q, _ = _half_masks(tq)
        halves = _half_masks(tk)

        def step(ki, state, masked):
            m_old, l_old, accs = state[0:nh], state[nh:2 * nh], state[2 * nh:]
            rows = _ds(ki, tk)
            ss = [lax.dot_general(q_ref[:, h * LANE:(h + 1) * LANE], k_ref[rows, h * LANE:(h + 1) * LANE], NT_DIMS,
                                  preferred_element_type=F32) for h in range(nh)]
            if masked:
                valid = _att_mask(qi, ki, tq, tk)
                ss = [jnp.where(valid, s, NEG) for s in ss]
            m_new = [jnp.maximum(m_old[h], jnp.max(ss[h], axis=-1, keepdims=True)) for h in range(nh)]
            ps = [jnp.exp2(ss[h] - m_new[h]) for h in range(nh)]
            alpha = [jnp.exp2(m_old[h] - m_new[h]) for h in range(nh)]
            l_new = [alpha[h] * l_old[h] + jnp.sum(ps[h], axis=-1, keepdims=True) for h in range(nh)]
            new_accs = []
            for pp in range(npair):
                vv = v_ref[rows, pp * LANE:(pp + 1) * LANE]
                pv = [jnp.dot(ps[2 * pp + hh].astype(BF), jnp.where(halves[hh], vv, jnp.zeros_like(vv)),
                              preferred_element_type=F32) for hh in range(2)]
                new_accs.append(accs[pp] * jnp.where(lo_q, alpha[2 * pp], alpha[2 * pp + 1]) + pv[0] + pv[1])
            return tuple(m_new) + tuple(l_new) + tuple(new_accs)

        neg, zero = jnp.full((tq, 1), NEG, F32), jnp.zeros((tq, 1), F32)
        state = step(0, (neg,) * nh + (zero,) * nh + (jnp.zeros((tq, LANE), F32),) * npair, True)
        state = lax.fori_loop(1, qi, lambda ki, st: step(ki, st, False), state)
        state = lax.cond(qi > 0, lambda st: step(qi, st, True), lambda st: st, state)
        for pp in range(npair):
            l = jnp.where(lo_q, state[nh + 2 * pp], state[nh + 2 * pp + 1])
            o_ref[:, pp * LANE:(pp + 1) * LANE] = (state[2 * nh + pp] / l).astype(o_ref.dtype)
            lse_ref[:, pp * LANE:(pp + 1) * LANE] = jnp.where(lo_q, state[2 * pp], state[2 * pp + 1]) + jnp.log2(l)
        if carry:
            pl.when((grp == n_grp - 1) & (qi == nq - 1))(finish)

    outs = pl.pallas_call(
        body,
        out_shape=(_sds((t, 1024), BF), _sds((t, 1024), F32)) + tuple(carry.out_shapes() if carry else ()),
        grid=(n_grp, nq),
        in_specs=[pl.BlockSpec((tq, nh * LANE), lambda g, qi: (qi, g)),
                  pl.BlockSpec((t, nh * LANE), lambda g, qi: (0, g)),
                  pl.BlockSpec((t, npair * LANE), lambda g, qi: (0, g))] + [ANY] * nx,
        out_specs=(pl.BlockSpec((tq, npair * LANE), lambda g, qi: (qi, g)),
                   pl.BlockSpec((tq, npair * LANE), lambda g, qi: (qi, g))) + (ANY,) * nx,
        scratch_shapes=carry.scratch() if carry else [],
        compiler_params=_cp(("arbitrary", "arbitrary") if carry else ("parallel", "parallel")),
        name="attn_fwd_carrying" if carry else "attn_fwd",
    )(q_cat, k_cat, v, *(carry.arrs if carry else ()))
    return outs[0], outs[1], list(outs[2:])


def attn_bwd(q_cat, k_cat, v, o, lse, dyab, carry=None):
    t = q_cat.shape[0]
    tq = tk = _att_tile(t)
    nq = t // tq
    n_pair = MLA_HEADS // 2
    nx = carry.k if carry else 0

    def body(*refs):
        q_ref, k_ref, v_ref, o_ref, lse_ref, do_ref = refs[:6]
        dq_ref, dk_ref, dv_ref = refs[6 + nx:9 + nx]
        ki = pl.program_id(1)
        if carry:
            start, middle, finish = carry.phases(refs[6:6 + nx], refs[9 + nx:9 + 2 * nx], refs[9 + 2 * nx:])
            pair = pl.program_id(0)
            pl.when((pair == 0) & (ki == 0))(start)
            pl.when((pair == CARRY_MIDDLE_PAIR) & (ki == 0))(middle)

        @pl.when(ki == 0)
        def _():
            dq_ref[...] = jnp.zeros_like(dq_ref)

        halves = _half_masks(tq)
        vv = v_ref[...]
        kk = [k_ref[:, hh * LANE:(hh + 1) * LANE] for hh in range(2)]

        def step(qi, acc, masked):
            rows = _ds(qi, tq)
            dov, ov, lse_v = do_ref[rows, :], o_ref[rows, :].astype(F32), lse_ref[rows, :]
            qh = [q_ref[rows, hh * LANE:(hh + 1) * LANE] for hh in range(2)]
            ss = [lax.dot_general(qh[hh], kk[hh], NT_DIMS, preferred_element_type=F32) for hh in range(2)]
            if masked:
                valid = _att_mask(qi, ki, tq, tk)
                ss = [jnp.where(valid, s, NEG) for s in ss]
            ps = [jnp.exp2(ss[hh] - lse_v[:, 64 * hh:64 * hh + 1]) for hh in range(2)]
            dom = [jnp.where(halves[hh], dov, 0.0) for hh in range(2)]
            delta = [jnp.sum(dom[hh] * ov, axis=-1, keepdims=True) for hh in range(2)]
            dom = [d.astype(BF) for d in dom]
            dp = [lax.dot_general(dom[hh], vv, NT_DIMS, preferred_element_type=F32) for hh in range(2)]
            ds = [(ps[hh] * (dp[hh] - delta[hh])).astype(BF) for hh in range(2)]
            pb = [p.astype(BF) for p in ps]
            dv = (acc[2] + lax.dot_general(pb[0], dom[0], TN_DIMS, preferred_element_type=F32)
                  + lax.dot_general(pb[1], dom[1], TN_DIMS, preferred_element_type=F32))
            dk = [acc[hh] + lax.dot_general(ds[hh], qh[hh], TN_DIMS, preferred_element_type=F32) for hh in range(2)]
            for hh in range(2):
                dq_ref[rows, hh * LANE:(hh + 1) * LANE] += jnp.dot(ds[hh], kk[hh], preferred_element_type=F32)
            return dk[0], dk[1], dv

        zero = jnp.zeros((tk, LANE), F32)
        acc = step(ki, (zero, zero, zero), True)
        acc = lax.fori_loop(ki + 1, jnp.where(ki == 0, nq, ki + 1), lambda qi, a: step(qi, a, True), acc)
        acc = lax.fori_loop(ki + 1, jnp.where(ki == 0, ki + 1, nq), lambda qi, a: step(qi, a, False), acc)
        dk_ref[:, 0:LANE] = acc[0] * (1.0 / LOG2E)
        dk_ref[:, LANE:2 * LANE] = acc[1] * (1.0 / LOG2E)
        dv_ref[...] = acc[2]
        if carry:
            pl.when((pair == n_pair - 1) & (ki == nq - 1))(finish)

    full = lambda w, off=0: pl.BlockSpec((t, w), lambda p, ki: (0, p + off))
    blk = lambda w: pl.BlockSpec((tk, w), lambda p, ki: (ki, p))
    outs = pl.pallas_call(
        body,
        out_shape=(_sds((t, 2048), F32), _sds((t, 2048), F32), _sds((t, 1024), F32))
        + tuple(carry.out_shapes() if carry else ()),
        grid=(n_pair, nq),
        in_specs=[full(2 * LANE), blk(2 * LANE), blk(LANE), full(LANE), full(LANE), full(LANE, 8)] + [ANY] * nx,
        out_specs=(full(2 * LANE), blk(2 * LANE), blk(LANE)) + (ANY,) * nx,
        scratch_shapes=carry.scratch() if carry else [],
        compiler_params=_cp(("arbitrary", "arbitrary") if carry else ("parallel", "arbitrary")),
        name="attn_bwd_carrying" if carry else "attn_bwd",
    )(q_cat, k_cat, v, o, lse, dyab, *(carry.arrs if carry else ()))
    return outs[0], outs[1], outs[2], list(outs[3:])


N_PAIR = SSD_HEADS // 2


def _hdot(a, b):
    return jnp.dot(a, b, precision=HI, preferred_element_type=F32)


def _ssd_chunk(xs, bg, cg, dtraw, hin, dt_bias, a_log, dskip, rowmask):
    ln = CHUNK
    causal = lax.broadcasted_iota(jnp.int32, (ln, ln), 0) >= lax.broadcasted_iota(jnp.int32, (ln, ln), 1)
    ltri = causal.astype(F32)
    lane = lax.broadcasted_iota(jnp.int32, (ln, LANE), 1)
    halves = (lane < 64, lane >= 64)
    low_row = lax.broadcasted_iota(jnp.int32, (1, LANE), 1) < 64
    head_lane = lax.broadcasted_iota(jnp.int32, (1, SSD_HEADS), 1)
    head_row = lax.broadcasted_iota(jnp.int32, (SSD_HEADS, 1), 0)

    def col(a, h):
        return jnp.sum(jnp.where(head_lane == h, a, 0.0), axis=1, keepdims=True)

    dt = _softplus(dtraw + dt_bias) * rowmask
    da = dt * (-jnp.exp(a_log))
    acs = _hdot(ltri, da)
    acs_t = lax.dot_general(da, ltri, (((0,), (1,)), ((), ())), precision=HI, preferred_element_type=F32)
    tot = jnp.sum(da, axis=0, keepdims=True)
    bm = [b * rowmask for b in bg]
    cm = [c * rowmask for c in cg]
    cb = [lax.dot_general(cm[g].astype(BF), bm[g].astype(BF), NT_DIMS, preferred_element_type=F32) for g in range(2)]
    ys, hout = [], []
    for p in range(N_PAIR):
        g = p // (N_PAIR // 2)
        h0, h1 = 2 * p, 2 * p + 1
        xdt = xs[p] * jnp.where(halves[0], col(dt, h0), col(dt, h1))
        a_cols = [col(acs, h0), col(acs, h1)]
        tot_cols = [col(tot, h0), col(tot, h1)]
        y = jnp.zeros((ln, LANE), F32)
        snew = jnp.zeros((ln, LANE), F32)
        for hh in range(2):
            a_row = jnp.sum(jnp.where(head_row == h0 + hh, acs_t, 0.0), axis=0, keepdims=True)
            dec = jnp.exp(jnp.where(causal, a_cols[hh] - a_row, NEG))
            xm = jnp.where(halves[hh], xdt, 0.0).astype(BF)
            y = y + jnp.dot((cb[g] * dec).astype(BF), xm, preferred_element_type=F32)
            bd = bm[g] * jnp.exp(tot_cols[hh] - a_cols[hh])
            snew = snew + lax.dot_general(bd.astype(BF), xm, TN_DIMS, preferred_element_type=F32)
        y_off = (jnp.dot(cm[g].astype(BF), hin[p].astype(BF), preferred_element_type=F32)
                 * jnp.where(halves[0], jnp.exp(a_cols[0]), jnp.exp(a_cols[1])))
        ys.append(y + y_off + jnp.where(low_row, col(dskip, h0), col(dskip, h1)) * xs[p])
        hout.append(jnp.where(low_row, jnp.exp(tot_cols[0]), jnp.exp(tot_cols[1])) * hin[p] + snew)
    return ys, hout


def _ssd_load(x_ref, dt_ref):
    xs = [x_ref[:, p * LANE:(p + 1) * LANE] for p in range(N_PAIR)]
    bg = [x_ref[:, SSD_D_INNER + g * LANE:SSD_D_INNER + (g + 1) * LANE] for g in range(2)]
    cg = [x_ref[:, SSD_D_INNER + (2 + g) * LANE:SSD_D_INNER + (3 + g) * LANE] for g in range(2)]
    return xs, bg, cg, dt_ref[:, 0:SSD_HEADS]


def _chunk_rowmask(c):
    return ((c * CHUNK + lax.broadcasted_iota(jnp.int32, (CHUNK, 1), 0)) >= PAD).astype(F32)


def ssd_fwd(xbc_c, proj, dt_bias, a_log, dskip):
    t = xbc_c.shape[0]
    nc = t // CHUNK

    def body(x_ref, dt_ref, dtb_ref, al_ref, d_ref, y_ref, hs_ref, h_s):
        c = pl.program_id(0)

        @pl.when(c == 0)
        def _():
            h_s[...] = jnp.zeros_like(h_s)

        xs, bg, cg, dtraw = _ssd_load(x_ref, dt_ref)
        hin = [h_s[p] for p in range(N_PAIR)]
        hs_ref[0] = h_s[...]
        ys, hout = _ssd_chunk(xs, bg, cg, dtraw, hin, dtb_ref[...], al_ref[...], d_ref[...], _chunk_rowmask(c))
        for p in range(N_PAIR):
            y_ref[:, p * LANE:(p + 1) * LANE] = ys[p]
            h_s[p] = hout[p]

    par = pl.BlockSpec((1, SSD_HEADS), lambda c: (0, 0))
    return pl.pallas_call(
        body,
        out_shape=(_sds((t, SSD_D_INNER), F32), _sds((nc, N_PAIR, CHUNK, LANE), F32)),
        grid=(nc,),
        in_specs=[pl.BlockSpec((CHUNK, SSD_CONV_CH), lambda c: (c, 0)),
                  pl.BlockSpec((CHUNK, LANE), lambda c: (c, OFF_DT // LANE)), par, par, par],
        out_specs=(pl.BlockSpec((CHUNK, SSD_D_INNER), lambda c: (c, 0)),
                   pl.BlockSpec((1, N_PAIR, CHUNK, LANE), lambda c: (c, 0, 0, 0))),
        scratch_shapes=[pltpu.VMEM((N_PAIR, CHUNK, LANE), F32)],
        compiler_params=_cp(("arbitrary",)),
        name="ssd_fwd",
    )(xbc_c, proj, dt_bias.reshape(1, -1), a_log.reshape(1, -1), dskip.reshape(1, -1))


def ssd_bwd(xbc_c, proj, dt_bias, a_log, dskip, hs, dy):
    t = xbc_c.shape[0]
    nc = t // CHUNK

    def body(x_ref, dt_ref, dtb_ref, al_ref, d_ref, hs_ref, dy_ref, dx_ref, ddt_ref, dpar_ref, dh_s):
        ci = pl.program_id(0)
        c = nc - 1 - ci

        @pl.when(ci == 0)
        def _():
            dh_s[...] = jnp.zeros_like(dh_s)
            dpar_ref[...] = jnp.zeros_like(dpar_ref)

        xs, bg, cg, dtraw = _ssd_load(x_ref, dt_ref)
        hin = [hs_ref[0, p] for p in range(N_PAIR)]
        rowmask = _chunk_rowmask(c)
        fn = lambda xs_, bg_, cg_, dtraw_, hin_, dtb_, al_, d_: _ssd_chunk(xs_, bg_, cg_, dtraw_, hin_, dtb_, al_, d_, rowmask)
        _, vjp = jax.vjp(fn, xs, bg, cg, dtraw, hin, dtb_ref[...], al_ref[...], d_ref[...])
        dys = [dy_ref[:, p * LANE:(p + 1) * LANE] for p in range(N_PAIR)]
        dhs = [dh_s[p] for p in range(N_PAIR)]
        dxs, dbg, dcg, ddtraw, dhin, ddtb, dal, dd = vjp((dys, dhs))
        for p in range(N_PAIR):
            dx_ref[:, p * LANE:(p + 1) * LANE] = dxs[p]
            dh_s[p] = dhin[p]
        for g in range(2):
            dx_ref[:, SSD_D_INNER + g * LANE:SSD_D_INNER + (g + 1) * LANE] = dbg[g]
            dx_ref[:, SSD_D_INNER + (2 + g) * LANE:SSD_D_INNER + (3 + g) * LANE] = dcg[g]
        ddt_ref[...] = jnp.zeros_like(ddt_ref)
        ddt_ref[:, 0:SSD_HEADS] = ddtraw
        dpar_ref[0:1, 0:SSD_HEADS] += ddtb
        dpar_ref[1:2, 0:SSD_HEADS] += dal
        dpar_ref[2:3, 0:SSD_HEADS] += dd

    par = pl.BlockSpec((1, SSD_HEADS), lambda ci: (0, 0))
    return pl.pallas_call(
        body,
        out_shape=(_sds((t, SSD_CONV_CH), F32), _sds((t, LANE), F32), _sds((8, LANE), F32)),
        grid=(nc,),
        in_specs=[pl.BlockSpec((CHUNK, SSD_CONV_CH), lambda ci: (nc - 1 - ci, 0)),
                  pl.BlockSpec((CHUNK, LANE), lambda ci: (nc - 1 - ci, OFF_DT // LANE)), par, par, par,
                  pl.BlockSpec((1, N_PAIR, CHUNK, LANE), lambda ci: (nc - 1 - ci, 0, 0, 0)),
                  pl.BlockSpec((CHUNK, SSD_D_INNER), lambda ci: (nc - 1 - ci, 0))],
        out_specs=(pl.BlockSpec((CHUNK, SSD_CONV_CH), lambda ci: (nc - 1 - ci, 0)),
                   pl.BlockSpec((CHUNK, LANE), lambda ci: (nc - 1 - ci, 0)),
                   pl.BlockSpec((8, LANE), lambda ci: (0, 0))),
        scratch_shapes=[pltpu.VMEM((N_PAIR, CHUNK, LANE), F32)],
        compiler_params=_cp(("arbitrary",)),
        name="ssd_bwd",
    )(xbc_c, proj, dt_bias.reshape(1, -1), a_log.reshape(1, -1), dskip.reshape(1, -1), hs, dy)


def _neg_expm1(y):
    series = -(y * (1.0 + y * (0.5 + y * (1.0 / 6.0 + y * (1.0 / 24.0 + y * (1.0 / 120.0))))))
    return jnp.where(y > -0.1, series, 1.0 - jnp.exp(y))


def _rg_pw(xr, wa, ba, wi, bi, lam, rowmask):
    xb = xr.astype(BF)
    r = jax.nn.sigmoid(jnp.dot(xb, wa.astype(BF), preferred_element_type=F32) + ba)
    i = jax.nn.sigmoid(jnp.dot(xb, wi.astype(BF), preferred_element_type=F32) + bi)
    log_a = -LRU_C * r * _softplus(-lam)
    a = jnp.exp(log_a)
    u = jnp.sqrt(_neg_expm1(2.0 * log_a)) * (i * xr) * rowmask
    return a, u


def _gelu_grad(x):
    c = math.sqrt(2.0 / math.pi)
    th = jnp.tanh(c * (x + 0.044715 * (x * x * x)))
    return 0.5 * (1.0 + th) + 0.5 * x * (1.0 - th * th) * c * (1.0 + 3.0 * 0.044715 * x * x)


def _scan_fwd(a, u):
    n = a.shape[0]
    row = lax.broadcasted_iota(jnp.int32, a.shape, 0)
    s = 1
    while s < n:
        a_s = jnp.where(row >= s, pltpu.roll(a, s, 0), 1.0)
        u_s = jnp.where(row >= s, pltpu.roll(u, s, 0), 0.0)
        u = u + a * u_s
        a = a * a_s
        s *= 2
    return a, u


def _scan_bwd(b, d):
    n = b.shape[0]
    row = lax.broadcasted_iota(jnp.int32, b.shape, 0)
    s = 1
    while s < n:
        b_s = jnp.where(row < n - s, pltpu.roll(b, n - s, 0), 1.0)
        d_s = jnp.where(row < n - s, pltpu.roll(d, n - s, 0), 0.0)
        d = d + b * d_s
        b = b * b_s
        s *= 2
    return d


def rg_fwd(xr_pre, gate_pre, rgp, w_a, w_i):
    t = xr_pre.shape[0]
    tr = _rt(t)

    def body(x_ref, g_ref, p_ref, wa_ref, wi_ref, hg_ref, hs_ref, prev, hcar):
        ti = pl.program_id(1)

        @pl.when(ti == 0)
        def _():
            prev[...] = jnp.zeros_like(prev)
            hcar[...] = jnp.zeros_like(hcar)

        xv = x_ref[...]
        ext = jnp.concatenate([prev[...], xv], axis=0)
        xr = _conv_pre(ext, xv, p_ref, tr)
        rowmask = _row_mask(ti, tr, (tr, 1)).astype(F32)
        a, u = _rg_pw(xr, wa_ref[0], p_ref[5:6, :], wi_ref[0], p_ref[6:7, :], p_ref[7:8, :], rowmask)
        a_cum, h_loc = _scan_fwd(a, u)
        hs = h_loc + a_cum * hcar[0:1, :]
        hs_ref[...] = hs
        hg_ref[...] = (hs * _gelu(g_ref[...])).astype(hg_ref.dtype)
        hcar[...] = jnp.broadcast_to(hs[tr - 1:tr, :], (8, LANE))
        prev[...] = xv[tr - 8:, :]

    return pl.pallas_call(
        body,
        out_shape=(_sds((t, LRU_WIDTH), BF), _sds((t, LRU_WIDTH), F32)),
        grid=(LRU_BLOCKS, t // tr),
        in_specs=[pl.BlockSpec((tr, LANE), lambda n, ti: (ti, n)),
                  pl.BlockSpec((tr, LANE), lambda n, ti: (ti, n)),
                  pl.BlockSpec((8, LANE), lambda n, ti: (0, n)),
                  pl.BlockSpec((1, LANE, LANE), lambda n, ti: (n, 0, 0)),
                  pl.BlockSpec((1, LANE, LANE), lambda n, ti: (n, 0, 0))],
        out_specs=(pl.BlockSpec((tr, LANE), lambda n, ti: (ti, n)), pl.BlockSpec((tr, LANE), lambda n, ti: (ti, n))),
        scratch_shapes=[pltpu.VMEM((8, LANE), F32), pltpu.VMEM((8, LANE), F32)],
        compiler_params=_cp(("parallel", "arbitrary")),
        name="rg_fwd",
    )(xr_pre, gate_pre, rgp, w_a, w_i)


def rg_bwd(xr_pre, gate_pre, rgp, w_a, w_i, hs, dhg):
    t = xr_pre.shape[0]
    tr = _rt(t)
    nt = t // tr
    r8 = tr // 8

    def body(x_ref, xp_ref, g_ref, p_ref, wa_ref, wi_ref, hs_ref, hp_ref, dhg_ref,
             dx_ref, dg_ref, dp_ref, dwa_ref, dwi_ref, gcar, dnext):
        ti = pl.program_id(1)
        tt = nt - 1 - ti

        @pl.when(ti == 0)
        def _():
            gcar[...] = jnp.zeros_like(gcar)
            dnext[...] = jnp.zeros_like(dnext)
            dp_ref[...] = jnp.zeros_like(dp_ref)
            dwa_ref[...] = jnp.zeros_like(dwa_ref)
            dwi_ref[...] = jnp.zeros_like(dwi_ref)

        xv = x_ref[...]
        halo = jnp.where(tt > 0, xp_ref[...], 0.0)
        ext = jnp.concatenate([halo, xv], axis=0)
        xr = _conv_pre(ext, xv, p_ref, tr)
        rowmask = _row_mask(tt, tr, (tr, 1)).astype(F32)
        fn = lambda xr_, wa_, ba_, wi_, bi_, lam_: _rg_pw(xr_, wa_, ba_, wi_, bi_, lam_, rowmask)
        (a, _), vjp = jax.vjp(fn, xr, wa_ref[0], p_ref[5:6, :], wi_ref[0], p_ref[6:7, :], p_ref[7:8, :])
        gpre = g_ref[...]
        hsv = hs_ref[...]
        dhg_v = dhg_ref[...]
        dg_ref[...] = (dhg_v * hsv * _gelu_grad(gpre)).astype(dg_ref.dtype)
        row = lax.broadcasted_iota(jnp.int32, (tr, LANE), 0)
        d = dhg_v * _gelu(gpre) + jnp.where(row == tr - 1, gcar[0:1, :], 0.0)
        b = jnp.where(row < tr - 1, pltpu.roll(a, tr - 1, 0), 0.0)
        g = _scan_bwd(b, d)
        gcar[...] = jnp.broadcast_to(a[0:1, :] * g[0:1, :], (8, LANE))
        hlast = jnp.where(tt > 0, hp_ref[7:8, :], 0.0)
        hprev = jnp.where(row == 0, hlast, pltpu.roll(hsv, 1, 0))
        dxr, dwa, dba, dwi, dbi, dlam = vjp((g * hprev, g))
        dx, sums = _conv_bwd_parts(dxr, dnext[...], xv, ext, p_ref, tr)
        dx_ref[...] = dx.astype(dx_ref.dtype)
        dnext[...] = dxr[:8, :]
        dp_ref[...] += _rows_block(sums + [dba, dbi, dlam])
        dwa_ref[0] += dwa
        dwi_ref[0] += dwi

    tile = lambda off=0: pl.BlockSpec((tr, LANE), lambda n, ti: (nt - 1 - ti, off + n))
    halo = lambda off=0: pl.BlockSpec((8, LANE), lambda n, ti: (jnp.maximum((nt - 1 - ti) * r8 - 1, 0), off + n))
    par = pl.BlockSpec((8, LANE), lambda n, ti: (0, n))
    wspec = pl.BlockSpec((1, LANE, LANE), lambda n, ti: (n, 0, 0))
    return pl.pallas_call(
        body,
        out_shape=(_sds((t, LRU_WIDTH), BF), _sds((t, LRU_WIDTH), BF), _sds((8, LRU_WIDTH), F32),
                   _sds((LRU_BLOCKS, LANE, LANE), F32), _sds((LRU_BLOCKS, LANE, LANE), F32)),
        grid=(LRU_BLOCKS, nt),
        in_specs=[tile(), halo(), tile(), par, wspec, wspec, tile(), halo(), tile()],
        out_specs=(tile(), tile(), par, wspec, wspec),
        scratch_shapes=[pltpu.VMEM((8, LANE), F32), pltpu.VMEM((8, LANE), F32)],
        compiler_params=_cp(("parallel", "arbitrary")),
        name="rg_bwd",
    )(xr_pre, xr_pre, gate_pre, rgp, w_a, w_i, hs, hs, dhg)


PACK_W = 1024
MESH_ID = pl.DeviceIdType.MESH
ANY = pl.BlockSpec(memory_space=pl.ANY)


def _my_place():
    x, y, c = lax.axis_index("x"), lax.axis_index("y"), lax.axis_index("c")
    return x, y, c


def _lin(px, py, pc):
    return 4 * px + 2 * py + pc


class Exchange:
    def __init__(self, kind, arrs):
        self.kind, self.arrs, self.k = kind, list(arrs), len(arrs)

    def out_shapes(self):
        if self.kind == "gather":
            return [_sds((N_DEV,) + a.shape, a.dtype) for a in self.arrs]
        return [_sds(a.shape, a.dtype) for a in self.arrs]

    def scratch(self):
        k = self.k
        return [pltpu.SemaphoreType.DMA((k, 7)), pltpu.SemaphoreType.DMA((k, 7)), pltpu.SemaphoreType.DMA((k,))]

    def phases(self, ins, outs, sems):
        return (self._gather if self.kind == "gather" else self._scatter)(ins, outs, *sems)

    def _gather(self, ins, outs, send_sems, recv_sems, local_sems):
        k = self.k
        x, y, c = _my_place()
        me, sibling = (x, y, c), (x, y, 1 - c)
        chips = [(1 - x, y), (x, 1 - y), (1 - x, 1 - y)]

        def copy(a, sem, block, to, from_input=False):
            slab = outs[a].at[_lin(*block)]
            return pltpu.make_async_remote_copy(
                src_ref=ins[a] if from_input else slab, dst_ref=slab,
                send_sem=send_sems.at[a, sem], recv_sem=recv_sems.at[a, sem],
                device_id=to, device_id_type=MESH_ID)

        def mine():
            return [pltpu.make_async_copy(ins[a], outs[a].at[_lin(*me)], local_sems.at[a]) for a in range(k)]

        def first():
            out = []
            for a in range(k):
                out.append(copy(a, 0, me, sibling, True))
                out += [copy(a, 1 + j, me, (*chip, c), True) for j, chip in enumerate(chips)]
            return out

        def passed():
            return [copy(a, 4 + j, (*chip, c), sibling) for j, chip in enumerate(chips) for a in range(k)]

        def start():
            for cp in mine() + first():
                cp.start()

        def middle():
            onward = passed()
            for j, chip in enumerate(chips):
                for a in range(k):
                    copy(a, 1 + j, (*chip, c), me).wait_recv()
                    onward[j * k + a].start()

        def finish():
            for a in range(k):
                copy(a, 0, sibling, me).wait_recv()
                for j, chip in enumerate(chips):
                    copy(a, 4 + j, (*chip, 1 - c), me).wait_recv()
            for cp in first() + passed():
                cp.wait_send()
            for cp in mine():
                cp.wait()

        return start, middle, finish

    def _scatter(self, ins, outs, send_sems, recv_sems, local_sems):
        k = self.k
        x, y, c = _my_place()
        me = _lin(x, y, c)
        peers = [((1 - x) if r & 4 else x, (1 - y) if r & 2 else y, (1 - c) if r & 1 else c) for r in range(1, N_DEV)]

        def copy(a, r, src_slab, dst_slab, to):
            return pltpu.make_async_remote_copy(
                src_ref=ins[a].at[src_slab], dst_ref=outs[a].at[dst_slab],
                send_sem=send_sems.at[a, r], recv_sem=recv_sems.at[a, r],
                device_id=to, device_id_type=MESH_ID)

        def mine():
            return [pltpu.make_async_copy(ins[a].at[me], outs[a].at[me], local_sems.at[a]) for a in range(k)]

        def sends():
            return [copy(a, r, _lin(*peer), me, peer) for r, peer in enumerate(peers) for a in range(k)]

        def start():
            for cp in mine() + sends():
                cp.start()

        def middle():
            pass

        def finish():
            for r, peer in enumerate(peers):
                for a in range(k):
                    copy(a, r, me, _lin(*peer), peer).wait_recv()
            for cp in sends():
                cp.wait_send()
            for cp in mine():
                cp.wait()

        return start, middle, finish

    def run(self, name):
        k = self.k

        def body(*refs):
            start, middle, finish = self.phases(refs[:k], refs[k:2 * k], refs[2 * k:])
            start()
            middle()
            finish()

        return pl.pallas_call(
            body,
            out_shape=tuple(self.out_shapes()),
            in_specs=[ANY] * k,
            out_specs=tuple(ANY for _ in range(k)),
            scratch_shapes=self.scratch(),
            name=name,
        )(*self.arrs)


def all_gather(arrs, name):
    return Exchange("gather", arrs).run(name)


def all_to_all(arrs, name):
    return Exchange("scatter", arrs).run(name)


def slab_sum(a, name):
    _, r, w = a.shape
    tr = _pick(r, (256, 128, 64, 32, 16, 8))

    def body(a_ref, o_ref):
        acc = a_ref[0].astype(F32)
        for d in range(1, N_DEV):
            acc = acc + a_ref[d].astype(F32)
        o_ref[...] = acc

    return pl.pallas_call(
        body,
        out_shape=_sds((r, w), F32),
        grid=(r // tr,),
        in_specs=[pl.BlockSpec((N_DEV, tr, w), lambda i: (0, i, 0))],
        out_specs=pl.BlockSpec((tr, w), lambda i: (i, 0)),
        compiler_params=_cp(("parallel",)),
        name=name,
    )(a)


def _adam_update(w, g, m, v):
    nm = ADAM_B1 * m + (1.0 - ADAM_B1) * g
    nv = ADAM_B2 * v + (1.0 - ADAM_B2) * (g * g)
    m_hat = nm / (1.0 - ADAM_B1 ** ADAM_STEP)
    v_hat = nv / (1.0 - ADAM_B2 ** ADAM_STEP)
    return -ADAM_LR * (m_hat / (jnp.sqrt(v_hat) + ADAM_EPS) + ADAM_WD * w), nm, nv


def adamw_blocks(w, m, v, parts, name):
    nl, r, c = w.shape
    tr = next(t for t in (256, 160, 128, 64, 32, 16) if r % t == 0 and N_DEV * t * c * 2 <= 2 * 1024 * 1024)

    def body(w_ref, m_ref, v_ref, *rest):
        part_refs, (g_ref, d_ref, nm_ref, nv_ref) = rest[:nl], rest[nl:]
        layer = pl.program_id(0)
        for idx in range(nl):
            @pl.when(layer == idx)
            def _(idx=idx):
                g = part_refs[idx][0].astype(F32)
                for dev in range(1, N_DEV):
                    g = g + part_refs[idx][dev].astype(F32)
                g_ref[...] = g
                d_ref[...], nm_ref[...], nv_ref[...] = _adam_update(w_ref[...], g, m_ref[...], v_ref[...])

    spec = pl.BlockSpec((None, tr, c), lambda l, i: (l, i, 0))
    part_spec = lambda idx: pl.BlockSpec((N_DEV, tr, c), lambda l, i: (0, jnp.where(l == idx, i, 0), 0))
    return pl.pallas_call(
        body,
        out_shape=tuple(_sds((nl, r, c), F32) for _ in range(4)),
        grid=(nl, r // tr),
        in_specs=[spec] * 3 + [part_spec(idx) for idx in range(nl)],
        out_specs=(spec,) * 4,
        compiler_params=_cp(("arbitrary", "arbitrary")),
        name=name,
    )(w, m, v, *parts)


def adamw(w, g, m, v, name):
    r, c = w.shape
    tr = _pick(r, (256, 160, 128, 64, 32, 16, 8))

    def body(w_ref, g_ref, m_ref, v_ref, d_ref, nm_ref, nv_ref):
        d_ref[...], nm_ref[...], nv_ref[...] = _adam_update(w_ref[...], g_ref[...], m_ref[...], v_ref[...])

    spec = pl.BlockSpec((tr, c), lambda i: (i, 0))
    return pl.pallas_call(
        body,
        out_shape=tuple(_sds((r, c), F32) for _ in range(3)),
        grid=(r // tr,),
        in_specs=[spec] * 4,
        out_specs=(spec, spec, spec),
        compiler_params=_cp(("parallel",)),
        name=name,
    )(w, g, m, v)


def _relu2_epi(acc):
    r = jnp.maximum(acc, 0.0)
    return (r * r,)


def _drelu2_epi(acc, u):
    return (acc * (2.0 * jnp.sqrt(u.astype(F32))),)


def mlp_fwd(h, g_pre, g_post, w_up, w_down, hn=None, g_next=None):
    if hn is None:
        hn = norm_fwd(h, g_pre, BF, name="mlp_norm")
    u = matmul(hn, w_up, "nn", (BF,), epi=_relu2_epi, name="mlp_up")
    d = matmul(u, w_down, "nn", name="mlp_down")
    h2, hn_next = resadd_fwd(h, d, g_post, g_next, name="mlp_res")
    return h2, (h, hn, u, d), hn_next


def mlp_bwd(res, dh2, g_pre, g_post, w_up, w_down, post=None, then=None):
    h, hn, u, d = res
    dd, dg_post = post if post is not None else norm_bwd(d, g_post, dh2, mask_pad=True, out_dtype=BF, name="mlp_post_bwd")
    dw_down = matmul(u, dd, "tn", (BF,), name="mlp_dwdown").reshape(w_down.g8.shape)
    dp = matmul(dd, w_down, "nt", (BF,), epi=_drelu2_epi, extras=(u,), name="mlp_du")
    dw_up = matmul(hn, dp, "tn", (BF,), out_blocks=True, name="mlp_dwup")
    dhn = matmul(dp, w_up, "nt", name="mlp_dhn")
    dh, dg_pre, *below = norm_bwd(h, g_pre, dhn, dres=dh2, then=then, name="mlp_pre_bwd")
    return dh, dict(mlp_pre_g=dg_pre, mlp_post_g=dg_post, w_up=dw_up, w_down=dw_down), (tuple(below) or None)


def rg_layer_fwd(h, g_pre, g_post, w_x, w_y, rgp, w_a, w_i, w_out, hn=None, g_next=None):
    if hn is None:
        hn = norm_fwd(h, g_pre, BF, name="rg_norm")
    xr = matmul(hn, w_x, "nn", name="rg_in_x")
    gp = matmul(hn, w_y, "nn", name="rg_in_y")
    hg, hs = rg_fwd(xr, gp, rgp, w_a, w_i)
    m = matmul(hg, w_out, "nn", name="rg_out")
    h2, hn_next = resadd_fwd(h, m, g_post, g_next, name="rg_res")
    return h2, (h, hn, xr, gp, hg, hs, m), hn_next


def rg_layer_bwd(res, dh2, g_pre, g_post, w_x, w_y, rgp, w_a, w_i, w_out, post=None, then=None):
    h, hn, xr, gp, hg, hs, m = res
    dm, dg_post = post if post is not None else norm_bwd(m, g_post, dh2, mask_pad=True, out_dtype=BF, name="rg_post_bwd")
    dw_out = matmul(hg, dm, "tn", name="rg_dwout")
    dhg = matmul(dm, w_out, "nt", name="rg_dhg")
    dxr, dgp, drgp, dwa, dwi = rg_bwd(xr, gp, rgp, w_a, w_i, hs, dhg)
    dw_x = matmul(hn, dxr, "tn", name="rg_dwx")
    dw_y = matmul(hn, dgp, "tn", name="rg_dwy")
    dhn = matmul([dxr, dgp], [w_x, w_y], "nt", name="rg_dhn")
    dh, dg_pre, *below = norm_bwd(h, g_pre, dhn, dres=dh2, then=then, name="rg_pre_bwd")
    grads = dict(mix_pre_g=dg_pre, mix_post_g=dg_post, rg_w_x=dw_x, rg_w_y=dw_y,
                 rg_conv_w=drgp[0:4], rg_conv_b=drgp[4], rg_b_a=drgp[5], rg_b_i=drgp[6], rg_lambda=drgp[7],
                 rg_w_a=dwa, rg_w_i=dwi, rg_w_out=dw_out)
    return dh, grads, (tuple(below) or None)


def sm_layer_fwd(h, g_pre, g_post, w_in_p, convp, dt_bias, a_log, dskip, ssd_g, q_g, w_q_p, kv_g, w_kv_p, w_out, tabs,
                 carry=None, on_carried=None, hn=None, g_next=None):
    if hn is None:
        hn = norm_fwd(h, g_pre, BF, name="sm_norm")
    proj = matmul(hn, w_in_p, "nn", name="sm_in")
    xbc_c = conv_silu_fwd(proj, OFF_XBC, SSD_CONV_CH, convp, name="ssd_conv")
    y, hst = ssd_fwd(xbc_c, proj, dt_bias, a_log, dskip)
    y_ssd = gated_norm_fwd(y, proj, ssd_g)
    cqn = norm_fwd(proj, q_g, BF, col_blk=OFF_CQ // MLA_Q_RANK, width=MLA_Q_RANK, name="q_norm")
    q_raw = matmul(cqn, w_q_p, "nn", name="q_up")
    ckvn = norm_fwd(proj, kv_g, BF, col_blk=OFF_CKV // MLA_KV_RANK, width=MLA_KV_RANK, name="kv_norm")
    kv_raw = matmul(ckvn, w_kv_p, "nn", name="kv_up")
    q_cat, k_cat, v = rope_fwd(q_raw, kv_raw, proj, tabs)
    o, lse, carried = attn_fwd(q_cat, k_cat, v, carry)
    if on_carried is not None:
        on_carried(carried)
    w_out = w_out()
    half = w_out.shape[0] // 2
    m = matmul([y_ssd, o], [KBlock(w_out, half, 0), KBlock(w_out, half, 1)], "nn", name="sm_out")
    res = (h, hn, proj, xbc_c, y, hst, cqn, ckvn, q_cat, k_cat, v, o, lse, y_ssd, m)
    h2, hn_next = resadd_fwd(h, m, g_post, g_next, name="sm_res")
    return h2, res, hn_next


def sm_layer_bwd(res, dh2, g_pre, g_post, w_in_p, convp, dt_bias, a_log, dskip, ssd_g, q_g, w_q_p, kv_g, w_kv_p, w_out, tabs,
                 carry=None, post=None, then=None):
    h, hn, proj, xbc_c, y, hst, cqn, ckvn, q_cat, k_cat, v, o, lse, y_ssd, m = res
    w_out = w_out()
    dm, dg_post = post if post is not None else norm_bwd(m, g_post, dh2, mask_pad=True, out_dtype=BF, name="sm_post_bwd")
    dw_out = jnp.concatenate([matmul(y_ssd, dm, "tn", name="sm_dwout_ssd"), matmul(o, dm, "tn", name="sm_dwout_att")], axis=0)
    dyab = matmul(dm, w_out, "nt", name="sm_dyab")
    dq_cat, dk_cat, dv, carried = attn_bwd(q_cat, k_cat, v, o, lse, dyab, carry(dw_out) if carry is not None else None)
    dq_raw, dkr = rope_bwd(dq_cat, dk_cat, tabs)
    kw = MLA_HEADS * LANE
    dw_kv_p = jnp.concatenate([matmul(ckvn, dk_cat, "tn", name="kv_dw_k"), matmul(ckvn, dv, "tn", name="kv_dw_v")], axis=1)
    dckvn = matmul([dk_cat, dv], [KBlock(w_kv_p, kw, 0), KBlock(w_kv_p, kw // 2, 2)], "nt", name="kv_dx")
    dckv, dg_kv = norm_bwd(proj, kv_g, dckvn, out_dtype=BF, col_blk=OFF_CKV // MLA_KV_RANK, width=MLA_KV_RANK,
                           name="kv_norm_bwd")
    dw_q_p = matmul(cqn, dq_raw, "tn", name="q_dw")
    dcqn = matmul(dq_raw, w_q_p, "nt", name="q_dx")
    dcq, dg_q = norm_bwd(proj, q_g, dcqn, out_dtype=BF, col_blk=OFF_CQ // MLA_Q_RANK, width=MLA_Q_RANK, name="q_norm_bwd")
    dy, dz, dg_ssd = gated_norm_bwd(y, proj, ssd_g, dyab)
    dxbc_c, ddt, dpar = ssd_bwd(xbc_c, proj, dt_bias, a_log, dskip, hst, dy)
    dxbc, dconvp = conv_silu_bwd(proj, OFF_XBC, SSD_CONV_CH, convp, dxbc_c, name="ssd_conv_bwd")
    pieces = [dz, dxbc, dckv, ddt, dkr, dcq]
    dw_in_p = jnp.concatenate([matmul(hn, pc, "tn", (BF,), name="sm_dwin_%d" % i) for i, pc in enumerate(pieces)], axis=1)
    third = SSD_CONV_CH // 3
    a_terms = [dz] + [KBlock(dxbc, third, i) for i in range(3)] + [dckv, ddt, dkr, dcq]
    b_terms = ([KBlock(w_in_p, SSD_D_INNER, 0)] + [KBlock(w_in_p, third, OFF_XBC // third + i) for i in range(3)]
               + [KBlock(w_in_p, MLA_KV_RANK, OFF_CKV // MLA_KV_RANK), KBlock(w_in_p, LANE, OFF_DT // LANE),
                  KBlock(w_in_p, LANE, OFF_KR // LANE), KBlock(w_in_p, MLA_Q_RANK, OFF_CQ // MLA_Q_RANK)])
    dhn = matmul(a_terms, b_terms, "nt", name="sm_dhn")
    dh, dg_pre, *below = norm_bwd(h, g_pre, dhn, dres=dh2, then=then, name="sm_pre_bwd")
    grads = dict(mix_pre_g=dg_pre, mix_post_g=dg_post, w_in=w_in_cols_to_blocks(dw_in_p), ssd_conv_w=dconvp[0:4],
                 ssd_conv_b=dconvp[4], ssd_dt_bias=dpar[0, :SSD_HEADS], ssd_a_log=dpar[1, :SSD_HEADS],
                 ssd_d=dpar[2, :SSD_HEADS], ssd_norm_g=dg_ssd, mla_q_norm_g=dg_q, mla_w_q_up=_unpack_w_q(dw_q_p),
                 mla_kv_norm_g=dg_kv, mla_w_kv_up=_unpack_w_kv(dw_kv_p), w_out_ab=dw_out)
    return dh, grads, carried, (tuple(below) or None)


W_IN_COLS = 3248
W_IN_SHARD = W_IN_COLS // N_DEV
W_IN_WIRE = 512


def _w_in_tables():
    src = np.full((IN_W,), -1, np.int64)
    src[0:2560] = np.arange(2560)
    src[OFF_CKV:OFF_CKV + 256] = 2960 + np.arange(256)
    src[OFF_DT:OFF_DT + 16] = 2560 + np.arange(16)
    src[OFF_KR + 64:OFF_KR + 96] = 3216 + np.arange(32)
    src[OFF_CQ:OFF_CQ + 384] = 2576 + np.arange(384)
    dev = np.where(src >= 0, src // W_IN_SHARD, -1).astype(np.int32).reshape(1, IN_W)
    col = np.where(src >= 0, src % W_IN_SHARD, 0).astype(np.int32).reshape(1, IN_W)
    return dev, col


W_IN_TILE = 384


def _w_in_devices_of_tile(dev):
    return [sorted(set(dev[0, t * W_IN_TILE:(t + 1) * W_IN_TILE].tolist()) - {-1}) for t in range(IN_W // W_IN_TILE)]


def _any_of(index, values):
    cond = index == values[0]
    for v in values[1:]:
        cond = cond | (index == v)
    return cond


def w_in_blocks_to_cols(g8):
    _, k, wp = g8.shape
    tn = W_IN_TILE
    dev, col = _w_in_tables()
    holders = _w_in_devices_of_tile(dev)

    def body(g_ref, dev_ref, col_ref, o_ref):
        i = pl.program_id(0)
        row = lax.broadcasted_iota(jnp.int32, (wp, tn), 0)
        o_ref[...] = jnp.zeros_like(o_ref)
        for j in range(N_DEV):
            tiles = [t for t, devs in enumerate(holders) if j in devs]
            if tiles:
                @pl.when(_any_of(i, tiles))
                def _(j=j):
                    sel = ((row == col_ref[...]) & (dev_ref[...] == j)).astype(BF)
                    o_ref[...] += jnp.dot(g_ref[j], sel, preferred_element_type=F32).astype(o_ref.dtype)

    dev, col = jnp.asarray(dev), jnp.asarray(col)
    return pl.pallas_call(
        body,
        out_shape=_sds((k, IN_W), BF),
        grid=(IN_W // tn,),
        in_specs=[pl.BlockSpec((N_DEV, k, wp), lambda i: (0, 0, 0)), pl.BlockSpec((1, tn), lambda i: (0, i)),
                  pl.BlockSpec((1, tn), lambda i: (0, i))],
        out_specs=pl.BlockSpec((k, tn), lambda i: (0, i)),
        compiler_params=_cp(("parallel",)),
        name="w_in_cols",
    )(g8, dev, col)


def w_in_cols_to_blocks(dw):
    k = dw.shape[0]
    tn = W_IN_TILE
    dev, col = _w_in_tables()
    holders = _w_in_devices_of_tile(dev)

    def body(dw_ref, dev_ref, col_ref, o_ref):
        j = pl.program_id(0)
        row = lax.broadcasted_iota(jnp.int32, (W_IN_WIRE, tn), 0)
        o_ref[...] = jnp.zeros_like(o_ref)
        for t, devs in enumerate(holders):
            if devs:
                @pl.when(_any_of(j, devs))
                def _(t=t):
                    cols = slice(t * tn, (t + 1) * tn)
                    sel = ((row == col_ref[:, cols]) & (dev_ref[:, cols] == j)).astype(BF)
                    o_ref[0] += lax.dot_general(dw_ref[:, cols], sel, NT_DIMS,
                                                preferred_element_type=F32).astype(o_ref.dtype)

    dev, col = jnp.asarray(dev), jnp.asarray(col)
    return pl.pallas_call(
        body,
        out_shape=_sds((N_DEV, k, W_IN_WIRE), BF),
        grid=(N_DEV,),
        in_specs=[pl.BlockSpec((k, IN_W), lambda j: (0, 0)), pl.BlockSpec((1, IN_W), lambda j: (0, 0)),
                  pl.BlockSpec((1, IN_W), lambda j: (0, 0))],
        out_specs=pl.BlockSpec((1, k, W_IN_WIRE), lambda j: (j, 0, 0)),
        compiler_params=_cp(("parallel",)),
        name="w_in_blocks",
    )(dw, dev, col)


def _pack_w_q(w):
    w3 = w.reshape(w.shape[0], MLA_HEADS, MLA_NOPE + MLA_ROPE)
    return jnp.pad(w3, ((0, 0), (0, 0), (0, LANE - MLA_NOPE - MLA_ROPE))).reshape(w.shape[0], MLA_HEADS * LANE)


def _unpack_w_q(p):
    return p.reshape(p.shape[0], MLA_HEADS, LANE)[:, :, :MLA_NOPE + MLA_ROPE].reshape(p.shape[0], -1)


def _pack_w_kv(w):
    w3 = w.reshape(w.shape[0], MLA_HEADS, MLA_NOPE + MLA_V)
    k = jnp.pad(w3[:, :, :MLA_NOPE], ((0, 0), (0, 0), (0, LANE - MLA_NOPE))).reshape(w.shape[0], MLA_HEADS * LANE)
    return jnp.concatenate([k, w3[:, :, MLA_NOPE:].reshape(w.shape[0], MLA_HEADS * MLA_V)], axis=1)


def _unpack_w_kv(p):
    k = p[:, :MLA_HEADS * LANE].reshape(p.shape[0], MLA_HEADS, LANE)[:, :, :MLA_NOPE]
    v = p[:, MLA_HEADS * LANE:].reshape(p.shape[0], MLA_HEADS, MLA_V)
    return jnp.concatenate([k, v], axis=2).reshape(p.shape[0], -1)


def _rows8(rows, width):
    a = jnp.concatenate([r.reshape(-1, width) for r in rows], axis=0)
    return jnp.pad(a, ((0, 8 - a.shape[0]), (0, 0)))


SLAB_ROWS = 16


def _to_slab(flat_list, lead=()):
    cat = jnp.concatenate(flat_list, axis=-1)
    n = cat.shape[-1]
    unit = SLAB_ROWS * PACK_W
    total = -(-n // unit) * unit
    cat = jnp.pad(cat, [(0, 0)] * len(lead) + [(0, total - n)])
    return cat.reshape(lead + (total // PACK_W, PACK_W))


def _from_flat(flat, shapes):
    out, off = [], 0
    for s in shapes:
        n = int(np.prod(s))
        out.append(flat[off:off + n].reshape(s))
        off += n
    return out


def _gathered_full(g8, axis):
    moved = jnp.moveaxis(g8, 0, axis)
    shp = moved.shape
    return moved.reshape(shp[:axis] + (shp[axis] * shp[axis + 1],) + shp[axis + 2:])


def _per_device(full, axis):
    shp = full.shape
    split = full.reshape(shp[:axis] + (N_DEV, shp[axis] // N_DEV) + shp[axis + 1:])
    return jnp.moveaxis(split, axis, 0)


ARG_NAMES = ['x', 'meta_tokens', 'mix_pre_g', 'mix_post_g', 'mlp_pre_g', 'mlp_post_g', 'w_up', 'w_down', 'w_in',
             'ssd_conv_w', 'ssd_conv_b', 'ssd_dt_bias', 'ssd_a_log', 'ssd_d', 'ssd_norm_g', 'mla_q_norm_g',
             'mla_w_q_up', 'mla_kv_norm_g', 'mla_w_kv_up', 'w_out_ab', 'rg_w_x', 'rg_w_y', 'rg_conv_w', 'rg_conv_b',
             'rg_w_a', 'rg_b_a', 'rg_w_i', 'rg_b_i', 'rg_lambda', 'rg_w_out']
WEIGHTS = ARG_NAMES[1:]
BIG = {'w_up': 2, 'w_down': 1, 'w_in': 2, 'mla_w_q_up': 2, 'mla_w_kv_up': 2, 'w_out_ab': 1, 'rg_w_x': 2,
       'rg_w_y': 2, 'rg_w_out': 1}
SMALL = {'meta_tokens': 1, 'ssd_conv_w': 2, 'rg_conv_w': 2, 'rg_conv_b': 1, 'rg_b_a': 1, 'rg_b_i': 1, 'rg_lambda': 1}
REPL = [n for n in WEIGHTS if n not in BIG and n not in SMALL]
REPL_MEDIUM = ['rg_w_a', 'rg_w_i']
REPL_TINY = [n for n in REPL if n not in REPL_MEDIUM]


def _piece_axes():
    axes = {}
    for n, ax in BIG.items():
        for i in range(DEPTH if n in ('w_up', 'w_down') else DEPTH // 2):
            axes[(n, i)] = ax - 1
    return axes


PIECE_AXIS = _piece_axes()
AS_BLOCKS = ('w_up', 'w_down')
_RG = lambda i: [(n, i) for n in ('rg_w_x', 'rg_w_y', 'rg_w_out')]
_MLP = lambda l: [('w_up', l), ('w_down', l)]
_SM_IN = lambda i: [(n, i) for n in ('w_in', 'mla_w_q_up', 'mla_w_kv_up')]
GATHER_FIRST = _SM_IN(0)
GATHER_AT = {0: [('w_out_ab', 0)] + _MLP(0) + _RG(0) + _MLP(1) + _SM_IN(1), 2: [('w_out_ab', 1)] + _MLP(2) + _RG(1) + _MLP(3)}
SCATTER_AT = {2: _MLP(3) + _RG(1) + _MLP(2) + [('w_out_ab', 1)],
              0: _SM_IN(1) + _MLP(1) + _RG(0) + _MLP(0) + [('w_out_ab', 0)]}
SCATTER_LAST = _SM_IN(0)


def _wire_block(p, key):
    n, i = key
    blk = p[n][i]
    if n == 'w_in':
        blk = jnp.pad(blk, ((0, 0), (0, W_IN_WIRE - blk.shape[1])))
    return blk


def _step(p, moments):
    assert DEPTH == 4
    full = {n: [None] * p[n].shape[0] for n in BIG}
    full['w_in_g'] = [None] * p['w_in'].shape[0]

    def weight_blocks(group):
        return [_wire_block(p, k).astype(BF) for k in group]

    def take_weights(group, gathered):
        for (n, i), piece in zip(group, gathered):
            if n == 'w_in':
                full['w_in_g'][i] = piece
            elif n in AS_BLOCKS:
                full[n][i] = DevBlocks(piece, PIECE_AXIS[(n, i)])
            else:
                full[n][i] = _gathered_full(piece, PIECE_AXIS[(n, i)])

    def grad_blocks(group, gw):
        return [gw[k] if k[0] in AS_BLOCKS or k[0] == 'w_in' else _per_device(gw[k], PIECE_AXIS[k]).astype(BF)
                for k in group]

    parts = {}

    small_slab = _to_slab([p[n].reshape(-1) for n in SMALL])
    *first, small8 = all_gather(weight_blocks(GATHER_FIRST) + [small_slab], name="gather_first")
    take_weights(GATHER_FIRST, first)
    for n, piece in zip(SMALL, _from_flat_rows(small8, [p[n].shape for n in SMALL])):
        full[n] = _gathered_full(piece, SMALL[n])
    for n in REPL:
        full[n] = p[n]
    loss_local, grad_x, gw, gsmall_full, carried = _local_step(
        full, p['x'][0], p['loss_target'][0],
        fwd_carry=lambda layer: Exchange("gather", weight_blocks(GATHER_AT[layer])),
        on_fwd_carried=lambda layer, got: take_weights(GATHER_AT[layer], got),
        bwd_carry=lambda layer, gw_now, others: Exchange(
            "scatter", grad_blocks(SCATTER_AT[layer], gw_now)
            + ([jnp.stack(others[n], axis=0).reshape(N_DEV, -1, LANE) for n in REPL_MEDIUM] if layer == 0 else [])))

    for layer, group in SCATTER_AT.items():
        parts.update(zip(group, carried[layer]))
    rep_flat = jnp.concatenate([gsmall_full[n].reshape(-1) for n in REPL_TINY])
    rep_n = rep_flat.shape[0]
    rep_chunk = -(-rep_n // (N_DEV * PACK_W * 8)) * PACK_W * 8
    rep8 = jnp.pad(rep_flat, (0, N_DEV * rep_chunk - rep_n)).reshape(N_DEV, rep_chunk)
    gsmall = _to_slab([_per_device(gsmall_full[n], SMALL[n]).reshape(N_DEV, -1) for n in SMALL] + [rep8], lead=(N_DEV,))
    received = all_to_all(grad_blocks(SCATTER_LAST, gw) + [gsmall], name="scatter_last")
    n_last = len(SCATTER_LAST)
    parts.update(zip(SCATTER_LAST, received[:n_last]))
    ssmall = slab_sum(received[n_last], name="sum_small").reshape(-1)
    medium_mine = [slab_sum(r8, name="sum_" + n) for n, r8 in zip(REPL_MEDIUM, carried[0][len(SCATTER_AT[0]):])]
    g_loc = {'w_in': jnp.stack([slab_sum(parts[('w_in', i)], name="sum_w_in_%d" % i)[:, :W_IN_SHARD]
                                for i in range(p['w_in'].shape[0])], axis=0)}
    small_n = sum(int(np.prod(p[n].shape)) for n in SMALL)
    g_loc.update(zip(SMALL, _from_flat(ssmall, [p[n].shape for n in SMALL])))
    rep_mine = ssmall[small_n:small_n + rep_chunk].reshape(-1, PACK_W)
    rep_all, *medium_all = all_gather([rep_mine] + medium_mine, name="gather_replicated")
    g_loc.update(zip(REPL_TINY, _from_flat(rep_all.reshape(-1), [p[n].shape for n in REPL_TINY])))
    g_loc.update({n: g.reshape(p[n].shape) for n, g in zip(REPL_MEDIUM, medium_all)})

    out = {'loss': lax.psum(loss_local, ("x", "y", "c")), 'grad_x': grad_x[None]}
    small_names = list(SMALL) + REPL_TINY
    for n in list(BIG) + REPL_MEDIUM:
        shp = p[n].shape
        if n == 'w_in' or n in REPL_MEDIUM:
            v2 = lambda a: a.reshape(-1, shp[-1])
            d, nm, nv = adamw(v2(p[n]), v2(g_loc[n]), v2(moments['m_' + n]), v2(moments['v_' + n]), name="adamw_" + n)
            d, nm, nv = d.reshape(shp), nm.reshape(shp), nv.reshape(shp)
        else:
            g_loc[n], d, nm, nv = adamw_blocks(p[n], moments['m_' + n], moments['v_' + n],
                                               [parts[(n, i)] for i in range(shp[0])], name="adamw_" + n)
        out['delta_' + n], out['new_m_' + n], out['new_v_' + n] = d, nm, nv
    slab = lambda src: _to_slab([src(n).reshape(-1) for n in small_names])
    d, nm, nv = adamw(slab(lambda n: p[n]), slab(lambda n: g_loc[n]), slab(lambda n: moments['m_' + n]),
                      slab(lambda n: moments['v_' + n]), name="adamw_small")
    shapes = [p[n].shape for n in small_names]
    for key, flat in (('delta_', d), ('new_m_', nm), ('new_v_', nv)):
        for n, a in zip(small_names, _from_flat(flat.reshape(-1), shapes)):
            out[key + n] = a
    for n in WEIGHTS:
        out['grad_' + n] = g_loc[n]
    return out


def _local_step(full, x, target_rows, fwd_carry=None, on_fwd_carried=None, bwd_carry=None):
    t = PAD + N_META + x.shape[0]
    h = jnp.concatenate([jnp.zeros((PAD, D_MODEL), F32), full['meta_tokens'], x], axis=0)
    target = jnp.concatenate([jnp.zeros((PAD + N_META, D_MODEL), F32), target_rows], axis=0)
    tabs = rope_tables(t)

    def layer_args(layer):
        i = layer // 2
        if layer % 2 == 0:
            convp = _rows8([full['ssd_conv_w'][i], full['ssd_conv_b'][i]], SSD_CONV_CH)
            return (full['mix_pre_g'][layer], full['mix_post_g'][layer], w_in_blocks_to_cols(full['w_in_g'][i]), convp,
                    full['ssd_dt_bias'][i], full['ssd_a_log'][i], full['ssd_d'][i], full['ssd_norm_g'][i],
                    full['mla_q_norm_g'][i], _pack_w_q(full['mla_w_q_up'][i]), full['mla_kv_norm_g'][i],
                    _pack_w_kv(full['mla_w_kv_up'][i]), lambda: full['w_out_ab'][i], tabs)
        rgp = _rows8([full['rg_conv_w'][i], full['rg_conv_b'][i], full['rg_b_a'][i], full['rg_b_i'][i],
                      full['rg_lambda'][i]], LRU_WIDTH)
        return (full['mix_pre_g'][layer], full['mix_post_g'][layer], full['rg_w_x'][i], full['rg_w_y'][i], rgp,
                full['rg_w_a'][i], full['rg_w_i'][i], full['rg_w_out'][i])

    def mlp_args(layer):
        return (full['mlp_pre_g'][layer], full['mlp_post_g'][layer], full['w_up'][layer], full['w_down'][layer])

    saved = []
    hn = None
    for layer in range(DEPTH):
        la = layer_args(layer)
        to_mlp = dict(hn=hn, g_next=full['mlp_pre_g'][layer])
        if layer % 2 == 0:
            if fwd_carry is not None:
                h, res_mix, hn = sm_layer_fwd(h, *la, carry=fwd_carry(layer),
                                              on_carried=lambda got, layer=layer: on_fwd_carried(layer, got), **to_mlp)
            else:
                h, res_mix, hn = sm_layer_fwd(h, *la, **to_mlp)
        else:
            h, res_mix, hn = rg_layer_fwd(h, *la, **to_mlp)
        ma = mlp_args(layer)
        h, res_mlp, hn = mlp_fwd(h, *ma, hn=hn, g_next=full['mix_pre_g'][layer + 1] if layer + 1 < DEPTH else None)
        saved.append((la, ma, res_mix, res_mlp))
    loss_local, dh = loss_fwd_bwd(h, target)

    others = {n: [None] * len(full[n]) for n in WEIGHTS if n not in BIG and n != 'meta_tokens'}
    gw, carried = {}, {}
    post = None
    for layer in reversed(range(DEPTH)):
        la, ma, res_mix, res_mlp = saved[layer]
        dh, gm, post = mlp_bwd(res_mlp, dh, *ma, post=post, then=(res_mix[-1], la[1]))
        below = (saved[layer - 1][3][-1], saved[layer - 1][1][1]) if layer > 0 else None
        if layer % 2 == 0:
            for n in ('w_up', 'w_down'):
                gw[(n, layer)] = gm[n]
            carry = None
            if bwd_carry is not None:
                carry = lambda dw_out, layer=layer: bwd_carry(layer, {**gw, ('w_out_ab', layer // 2): dw_out}, others)
            dh, gx, carried[layer], post = sm_layer_bwd(res_mix, dh, *la, carry=carry, post=post, then=below)
        else:
            dh, gx, post = rg_layer_bwd(res_mix, dh, *la, post=post, then=below)
        for n, g in list(gm.items()) + list(gx.items()):
            i = layer if n in ('mix_pre_g', 'mix_post_g', 'mlp_pre_g', 'mlp_post_g', 'w_up', 'w_down') else layer // 2
            if n in BIG:
                gw[(n, i)] = g
            else:
                others[n][i] = g
    gothers = {n: jnp.stack(v, axis=0) for n, v in others.items()}
    gothers['meta_tokens'] = dh[PAD:PAD + N_META]
    return loss_local, dh[PAD + N_META:], gw, gothers, carried


def _from_flat_rows(g8, shapes):
    flat = g8.reshape(N_DEV, -1)
    out, off = [], 0
    for s in shapes:
        n = int(np.prod(s))
        out.append(flat[:, off:off + n].reshape((N_DEV,) + tuple(s)))
        off += n
    return out


def kernel(x, meta_tokens, mix_pre_g, mix_post_g, mlp_pre_g, mlp_post_g, w_up, w_down, w_in, ssd_conv_w, ssd_conv_b, ssd_dt_bias, ssd_a_log, ssd_d, ssd_norm_g, mla_q_norm_g, mla_w_q_up, mla_kv_norm_g, mla_w_kv_up, w_out_ab, rg_w_x, rg_w_y, rg_conv_w, rg_conv_b, rg_w_a, rg_b_a, rg_w_i, rg_b_i, rg_lambda, rg_w_out, loss_target, m_meta_tokens, m_mix_pre_g, m_mix_post_g, m_mlp_pre_g, m_mlp_post_g, m_w_up, m_w_down, m_w_in, m_ssd_conv_w, m_ssd_conv_b, m_ssd_dt_bias, m_ssd_a_log, m_ssd_d, m_ssd_norm_g, m_mla_q_norm_g, m_mla_w_q_up, m_mla_kv_norm_g, m_mla_w_kv_up, m_w_out_ab, m_rg_w_x, m_rg_w_y, m_rg_conv_w, m_rg_conv_b, m_rg_w_a, m_rg_b_a, m_rg_w_i, m_rg_b_i, m_rg_lambda, m_rg_w_out, v_meta_tokens, v_mix_pre_g, v_mix_post_g, v_mlp_pre_g, v_mlp_post_g, v_w_up, v_w_down, v_w_in, v_ssd_conv_w, v_ssd_conv_b, v_ssd_dt_bias, v_ssd_a_log, v_ssd_d, v_ssd_norm_g, v_mla_q_norm_g, v_mla_w_q_up, v_mla_kv_norm_g, v_mla_w_kv_up, v_w_out_ab, v_rg_w_x, v_rg_w_y, v_rg_conv_w, v_rg_conv_b, v_rg_w_a, v_rg_b_a, v_rg_w_i, v_rg_b_i, v_rg_lambda, v_rg_w_out):
    args = (x, meta_tokens, mix_pre_g, mix_post_g, mlp_pre_g, mlp_post_g, w_up, w_down, w_in, ssd_conv_w, ssd_conv_b, ssd_dt_bias, ssd_a_log, ssd_d, ssd_norm_g, mla_q_norm_g, mla_w_q_up, mla_kv_norm_g, mla_w_kv_up, w_out_ab, rg_w_x, rg_w_y, rg_conv_w, rg_conv_b, rg_w_a, rg_b_a, rg_w_i, rg_b_i, rg_lambda, rg_w_out, loss_target, m_meta_tokens, m_mix_pre_g, m_mix_post_g, m_mlp_pre_g, m_mlp_post_g, m_w_up, m_w_down, m_w_in, m_ssd_conv_w, m_ssd_conv_b, m_ssd_dt_bias, m_ssd_a_log, m_ssd_d, m_ssd_norm_g, m_mla_q_norm_g, m_mla_w_q_up, m_mla_kv_norm_g, m_mla_w_kv_up, m_w_out_ab, m_rg_w_x, m_rg_w_y, m_rg_conv_w, m_rg_conv_b, m_rg_w_a, m_rg_b_a, m_rg_w_i, m_rg_b_i, m_rg_lambda, m_rg_w_out, v_meta_tokens, v_mix_pre_g, v_mix_post_g, v_mlp_pre_g, v_mlp_post_g, v_w_up, v_w_down, v_w_in, v_ssd_conv_w, v_ssd_conv_b, v_ssd_dt_bias, v_ssd_a_log, v_ssd_d, v_ssd_norm_g, v_mla_q_norm_g, v_mla_w_q_up, v_mla_kv_norm_g, v_mla_w_kv_up, v_w_out_ab, v_rg_w_x, v_rg_w_y, v_rg_conv_w, v_rg_conv_b, v_rg_w_a, v_rg_b_a, v_rg_w_i, v_rg_b_i, v_rg_lambda, v_rg_w_out,)
    n_w = len(ARG_NAMES)
    p = dict(zip(ARG_NAMES, args[:n_w]))
    p['loss_target'] = args[n_w]
    moments = {}
    for i, n in enumerate(WEIGHTS):
        moments['m_' + n] = args[n_w + 1 + i]
        moments['v_' + n] = args[n_w + 1 + len(WEIGHTS) + i]
    out = _step(p, moments)
    res = [out['loss'], out['grad_x']]
    for prefix in ('grad_', 'delta_', 'new_m_', 'new_v_'):
        res += [out[prefix + n] for n in WEIGHTS]
    return tuple(res)
```

```python
import math

import numpy as np
import jax
import jax.numpy as jnp
from jax import lax
from jax.experimental import pallas as pl
from jax.experimental.pallas import tpu as pltpu

F32 = jnp.float32
BF = jnp.bfloat16
HI = lax.Precision.HIGHEST

D_MODEL = 1024
DEPTH = 4
N_META = 16
CHUNK = 128
PAD = CHUNK - N_META
EPS = 1e-6
SSD_HEADS = 16
SSD_HEAD_DIM = 64
SSD_D_INNER = 1024
SSD_STATE = 128
SSD_CONV_CH = 1536
MLA_HEADS = 16
MLA_NOPE = 64
MLA_ROPE = 32
MLA_V = 64
MLA_Q_RANK = 384
MLA_KV_RANK = 256
ROPE_BASE = 10000.0
LRU_WIDTH = 1280
LRU_BLOCKS = 10
LRU_C = 8.0
D_FF = 4096
N_DEV = 8
LANE = 128
IN_W = 3456
OFF_Z, OFF_XBC, OFF_CKV, OFF_DT, OFF_KR, OFF_CQ = 0, 1024, 2560, 2816, 2944, 3072

ADAM_LR = 0.001
ADAM_B1 = 0.9
ADAM_B2 = 0.999
ADAM_EPS = 1e-08
ADAM_WD = 0.01
ADAM_STEP = 10

VMEM_LIMIT = 56 * 1024 * 1024
NEG = -1e30


def _pick(n, cands):
    for c in cands:
        if n % c == 0:
            return c
    return n


def _cp(sem=None):
    return pltpu.CompilerParams(dimension_semantics=sem, vmem_limit_bytes=VMEM_LIMIT)


def _sds(shape, dtype):
    return jax.ShapeDtypeStruct(tuple(shape), dtype)


def _silu(x):
    return x * jax.nn.sigmoid(x)


def _softplus(x):
    return jnp.maximum(x, 0.0) + jnp.log(1.0 + jnp.exp(-jnp.abs(x)))


def _gelu(x):
    c = math.sqrt(2.0 / math.pi)
    return 0.5 * x * (1.0 + jnp.tanh(c * (x + 0.044715 * (x * x * x))))


def _row_mask(i, tr, shape, first_valid=PAD):
    row = i * tr + lax.broadcasted_iota(jnp.int32, shape, 0)
    return row >= first_valid


class KBlock:
    def __init__(self, arr, width, blk):
        self.arr, self.width, self.blk = arr, width, blk


class DevBlocks:
    def __init__(self, g8, axis):
        self.g8, self.axis = g8, axis
        _, r, c = g8.shape
        self.shape = (N_DEV * r, c) if axis == 0 else (r, N_DEV * c)


NN_DIMS = (((1,), (0,)), ((), ()))
MM_TALL_K = 1536
MM_WHOLE_K, MM_WHOLE_M = 1024, 4224


def matmul(a, b, mode, out_dtypes=(F32,), epi=None, extras=(), name="mm", tm=None, tn=None, out_blocks=False):
    a_terms = a if isinstance(a, (list, tuple)) else [a]
    b_terms = b if isinstance(b, (list, tuple)) else [b]
    assert len(a_terms) == len(b_terms) and (mode != "tn" or len(a_terms) == 1)
    arr_of = lambda t: t.arr if isinstance(t, KBlock) else t
    if mode == "tn":
        m, n = a_terms[0].shape[1], b_terms[0].shape[1]
    else:
        m = arr_of(a_terms[0]).shape[0]
        b0 = b_terms[0]
        n = (b0.shape if isinstance(b0, DevBlocks) else arr_of(b0).shape)[1 if mode == "nn" else 0]
    if mode == "tn":
        tm = _pick(m, (1024, 512, 384, 256, 128))
    else:
        k_all = sum(t.width if isinstance(t, KBlock) else t.shape[1] for t in a_terms)
        tall = (2112,) if k_all <= MM_TALL_K else ()
        if k_all <= MM_WHOLE_K and m <= MM_WHOLE_M and n % 256 == 0 and tm is None and tn is None:
            tm, tn = m, 256
        tm = tm or _pick(m, tall + (1056, 1024, 768, 640, 512, 384, 256, 128))
    tn = tn or _pick(n, (512, 640, 384, 256, 128))
    dims = {"nn": NN_DIMS, "nt": NT_DIMS, "tn": TN_DIMS}[mode]

    in_specs, args, plan = [], [], []
    for ta, tb in zip(a_terms, b_terms):
        if mode == "tn":
            k = ta.shape[0]
            in_specs += [pl.BlockSpec((k, tm), lambda i, j: (0, i)), pl.BlockSpec((k, tn), lambda i, j: (0, j))]
            args += [ta, tb]
            plan.append(None)
            continue
        if isinstance(ta, KBlock):
            kw, ka = ta.width, ta.blk
            in_specs.append(pl.BlockSpec((tm, kw), lambda i, j, ka=ka: (i, ka)))
        else:
            kw = ta.shape[1]
            in_specs.append(pl.BlockSpec((tm, kw), lambda i, j: (i, 0)))
        args.append(arr_of(ta))
        if isinstance(tb, DevBlocks):
            _, r, c = tb.g8.shape
            split_k = tb.axis == (0 if mode == "nn" else 1)
            if split_k:
                kd = r if mode == "nn" else c
                assert kw == N_DEV * kd
                blk = (N_DEV, kd, tn) if mode == "nn" else (N_DEV, tn, kd)
                in_specs.append(pl.BlockSpec(blk, (lambda i, j: (0, 0, j)) if mode == "nn" else (lambda i, j: (0, j, 0))))
                plan.append(kd)
            else:
                per = (c if mode == "nn" else r) // tn
                blk = (None, kw, tn) if mode == "nn" else (None, tn, kw)
                in_specs.append(pl.BlockSpec(blk, (lambda i, j, per=per: (j // per, 0, j % per)) if mode == "nn"
                                             else (lambda i, j, per=per: (j // per, j % per, 0))))
                plan.append(None)
            args.append(tb.g8)
        else:
            kb = tb.blk if isinstance(tb, KBlock) else 0
            assert (tb.width if isinstance(tb, KBlock) else tb.shape[0 if mode == "nn" else 1]) == kw
            in_specs.append(pl.BlockSpec((kw, tn), lambda i, j, kb=kb: (kb, j)) if mode == "nn"
                            else pl.BlockSpec((tn, kw), lambda i, j, kb=kb: (j, kb)))
            args.append(arr_of(tb))
            plan.append(None)
    n_terms, n_ex = len(plan), len(extras)

    def body(*refs):
        ex_refs, out_refs = refs[2 * n_terms:2 * n_terms + n_ex], refs[2 * n_terms + n_ex:]
        acc = None
        for t, kd in enumerate(plan):
            a_ref, b_ref = refs[2 * t], refs[2 * t + 1]
            if kd is None:
                parts = [lax.dot_general(a_ref[...].astype(BF), b_ref[...].astype(BF), dims, preferred_element_type=F32)]
            else:
                parts = [lax.dot_general(a_ref[:, d * kd:(d + 1) * kd].astype(BF), b_ref[d].astype(BF), dims,
                                         preferred_element_type=F32) for d in range(N_DEV)]
            for part in parts:
                acc = part if acc is None else acc + part
        outs = (acc,) if epi is None else epi(acc, *[r[...] for r in ex_refs])
        for r, o in zip(out_refs, outs):
            r[...] = o.astype(r.dtype)

    o_spec = pl.BlockSpec((tm, tn), lambda i, j: (i, j))
    if out_blocks:
        per = n // N_DEV // tn
        out_shape = tuple(_sds((N_DEV, m, n // N_DEV), dt) for dt in out_dtypes)
        out_specs = tuple(pl.BlockSpec((None, tm, tn), lambda i, j: (j // per, i, j % per)) for _ in out_dtypes)
    else:
        out_shape = tuple(_sds((m, n), dt) for dt in out_dtypes)
        out_specs = tuple(o_spec for _ in out_dtypes)
    outs = pl.pallas_call(
        body,
        out_shape=out_shape,
        grid=(m // tm, n // tn),
        in_specs=in_specs + [o_spec] * n_ex,
        out_specs=out_specs,
        compiler_params=_cp(("parallel", "parallel")),
        name=name,
    )(*args, *extras)
    return outs[0] if len(out_dtypes) == 1 else outs


def _rt(t):
    return _pick(t, (384, 256, 128))


def norm_fwd(x, g, out_dtype, col_blk=0, width=None, name="norm_fwd"):
    t = x.shape[0]
    w = width or x.shape[1]
    tr = _rt(t)

    def body(x_ref, g_ref, o_ref):
        xv = x_ref[...]
        r = lax.rsqrt(jnp.mean(xv * xv, axis=-1, keepdims=True) + EPS)
        o_ref[...] = (xv * r * g_ref[...]).astype(o_ref.dtype)

    return pl.pallas_call(
        body,
        out_shape=_sds((t, w), out_dtype),
        grid=(t // tr,),
        in_specs=[pl.BlockSpec((tr, w), lambda i: (i, col_blk)), pl.BlockSpec((1, w), lambda i: (0, 0))],
        out_specs=pl.BlockSpec((tr, w), lambda i: (i, 0)),
        compiler_params=_cp(("parallel",)),
        name=name,
    )(x, g.reshape(1, w))


def _rms_bwd(xv, gv, dyv):
    r = lax.rsqrt(jnp.mean(xv * xv, axis=-1, keepdims=True) + EPS)
    xh = xv * r
    dyg = dyv * gv
    dx = r * (dyg - xh * jnp.mean(dyg * xh, axis=-1, keepdims=True))
    return dx, jnp.sum(dyv * xh, axis=0, keepdims=True)


def norm_bwd(x, g, dy, dres=None, mask_pad=False, out_dtype=F32, col_blk=0, width=None, dy_col_blk=0, then=None,
             name="norm_bwd"):
    t = x.shape[0]
    w = width or x.shape[1]
    tr = _rt(t)
    has_res, has_then = dres is not None, then is not None

    def body(*refs):
        x_ref, g_ref, dy_ref = refs[:3]
        n_in = 3 + has_res + 2 * has_then
        dx_ref, dg_ref = refs[n_in:n_in + 2]
        i = pl.program_id(0)
        dyv = dy_ref[...].astype(F32)
        if mask_pad:
            dyv = jnp.where(_row_mask(i, tr, dyv.shape), dyv, 0.0)
        dx, dg = _rms_bwd(x_ref[...], g_ref[...], dyv)
        if has_res:
            dx = dx + refs[3][...]
        dx_ref[...] = dx.astype(dx_ref.dtype)

        @pl.when(i == 0)
        def _():
            for r in refs[n_in + 1::2]:
                r[...] = jnp.zeros_like(r)

        dg_ref[...] += dg
        if has_then:
            x2_ref, g2_ref = refs[3 + has_res:5 + has_res]
            dx2_ref, dg2_ref = refs[n_in + 2:]
            dx2, dg2 = _rms_bwd(x2_ref[...], g2_ref[...], jnp.where(_row_mask(i, tr, dx.shape), dx, 0.0))
            dx2_ref[...] = dx2.astype(dx2_ref.dtype)
            dg2_ref[...] += dg2

    row = pl.BlockSpec((tr, w), lambda i: (i, 0))
    vec = pl.BlockSpec((1, w), lambda i: (0, 0))
    in_specs = [pl.BlockSpec((tr, w), lambda i: (i, col_blk)), vec, pl.BlockSpec((tr, w), lambda i: (i, dy_col_blk))]
    args = [x, g.reshape(1, w), dy]
    out_shape, out_specs = [_sds((t, w), out_dtype), _sds((1, w), F32)], [row, vec]
    if has_res:
        in_specs.append(row)
        args.append(dres)
    if has_then:
        in_specs += [row, vec]
        args += [then[0], then[1].reshape(1, w)]
        out_shape += [_sds((t, w), BF), _sds((1, w), F32)]
        out_specs += [row, vec]
    outs = pl.pallas_call(
        body,
        out_shape=tuple(out_shape),
        grid=(t // tr,),
        in_specs=in_specs,
        out_specs=tuple(out_specs),
        compiler_params=_cp(("arbitrary",)),
        name=name,
    )(*args)
    if has_then:
        return outs[0], outs[1].reshape(w), outs[2], outs[3].reshape(w)
    return outs[0], outs[1].reshape(w)


def resadd_fwd(h, m, g, g_next=None, name="resadd"):
    t, w = h.shape
    tr = _rt(t)
    with_next = g_next is not None

    def body(h_ref, m_ref, g_ref, *rest):
        mv = m_ref[...]
        r = lax.rsqrt(jnp.mean(mv * mv, axis=-1, keepdims=True) + EPS)
        y = mv * r * g_ref[...]
        h2 = h_ref[...] + jnp.where(_row_mask(pl.program_id(0), tr, y.shape), y, 0.0)
        if with_next:
            gn_ref, o_ref, hn_ref = rest
            r2 = lax.rsqrt(jnp.mean(h2 * h2, axis=-1, keepdims=True) + EPS)
            hn_ref[...] = (h2 * r2 * gn_ref[...]).astype(hn_ref.dtype)
        else:
            (o_ref,) = rest
        o_ref[...] = h2

    row = pl.BlockSpec((tr, w), lambda i: (i, 0))
    vec = pl.BlockSpec((1, w), lambda i: (0, 0))
    outs = pl.pallas_call(
        body,
        out_shape=(_sds((t, w), F32),) + ((_sds((t, w), BF),) if with_next else ()),
        grid=(t // tr,),
        in_specs=[row, row, vec] + ([vec] if with_next else []),
        out_specs=(row,) + ((row,) if with_next else ()),
        compiler_params=_cp(("parallel",)),
        name=name,
    )(h, m, g.reshape(1, w), *((g_next.reshape(1, w),) if with_next else ()))
    return outs[0], (outs[1] if with_next else None)


def loss_fwd_bwd(h, target):
    t, w = h.shape
    tr = _rt(t)

    def body(h_ref, t_ref, s_ref, dh_ref):
        i = pl.program_id(0)
        err = h_ref[...] - t_ref[...]
        err = jnp.where(_row_mask(i, tr, err.shape, PAD + N_META), err, 0.0)
        dh_ref[...] = err * (1.0 / w)

        @pl.when(i == 0)
        def _():
            s_ref[...] = jnp.zeros_like(s_ref)

        s_ref[...] += jnp.sum(err * err).reshape(1, 1)

    s, dh = pl.pallas_call(
        body,
        out_shape=(_sds((1, LANE), F32), _sds((t, w), F32)),
        grid=(t // tr,),
        in_specs=[pl.BlockSpec((tr, w), lambda i: (i, 0)), pl.BlockSpec((tr, w), lambda i: (i, 0))],
        out_specs=(pl.BlockSpec((1, LANE), lambda i: (0, 0)), pl.BlockSpec((tr, w), lambda i: (i, 0))),
        compiler_params=_cp(("arbitrary",)),
        name="loss",
    )(h, target)
    return 0.5 * s[0, 0] / w, dh


def _shift_down(ext, k, n):
    return pltpu.roll(ext, k, 0)[8:]


def _conv_pre(ext, x, w_ref, n):
    return (w_ref[4:5, :] + w_ref[3:4, :] * x + w_ref[2:3, :] * _shift_down(ext, 1, n)
            + w_ref[1:2, :] * _shift_down(ext, 2, n) + w_ref[0:1, :] * _shift_down(ext, 3, n))


def _conv_bwd_parts(dpre, dnext, x, ext, w_ref, n):
    extd = jnp.concatenate([dpre, dnext], axis=0)
    ln = n + 8
    dx = (w_ref[3:4, :] * dpre + w_ref[2:3, :] * pltpu.roll(extd, ln - 1, 0)[:n]
          + w_ref[1:2, :] * pltpu.roll(extd, ln - 2, 0)[:n] + w_ref[0:1, :] * pltpu.roll(extd, ln - 3, 0)[:n])
    sums = [jnp.sum(dpre * _shift_down(ext, 3, n), axis=0, keepdims=True),
            jnp.sum(dpre * _shift_down(ext, 2, n), axis=0, keepdims=True),
            jnp.sum(dpre * _shift_down(ext, 1, n), axis=0, keepdims=True),
            jnp.sum(dpre * x, axis=0, keepdims=True),
            jnp.sum(dpre, axis=0, keepdims=True)]
    return dx, sums


def _rows_block(sums):
    w = sums[0].shape[1]
    row = lax.broadcasted_iota(jnp.int32, (8, w), 0)
    out = jnp.zeros((8, w), F32)
    for k, s in enumerate(sums):
        out = jnp.where(row == k, s, out)
    return out


CONV_BLOCK = 512


def conv_silu_fwd(x, col0, c, wb, name="conv_fwd"):
    t = x.shape[0]
    cw = _pick(c, (CONV_BLOCK, LANE))
    nblk, col0_blk = c // cw, col0 // cw
    assert col0 % cw == 0
    tr = _rt(t)

    def body(x_ref, w_ref, o_ref, prev):
        ti = pl.program_id(1)

        @pl.when(ti == 0)
        def _():
            prev[...] = jnp.zeros_like(prev)

        xv = x_ref[...]
        ext = jnp.concatenate([prev[...], xv], axis=0)
        o_ref[...] = _silu(_conv_pre(ext, xv, w_ref, tr))
        prev[...] = xv[tr - 8:, :]

    return pl.pallas_call(
        body,
        out_shape=_sds((t, c), F32),
        grid=(nblk, t // tr),
        in_specs=[pl.BlockSpec((tr, cw), lambda cb, ti: (ti, col0_blk + cb)),
                  pl.BlockSpec((8, cw), lambda cb, ti: (0, cb))],
        out_specs=pl.BlockSpec((tr, cw), lambda cb, ti: (ti, cb)),
        scratch_shapes=[pltpu.VMEM((8, cw), F32)],
        compiler_params=_cp(("parallel", "arbitrary")),
        name=name,
    )(x, wb)


def conv_silu_bwd(x, col0, c, wb, dout, name="conv_bwd"):
    t = x.shape[0]
    cw = _pick(c, (CONV_BLOCK, LANE))
    nblk, col0_blk = c // cw, col0 // cw
    assert col0 % cw == 0
    tr = _rt(t)
    nt = t // tr
    r8 = tr // 8

    def body(x_ref, xp_ref, w_ref, do_ref, dx_ref, dwb_ref, dnext):
        ti = pl.program_id(1)
        tt = nt - 1 - ti

        @pl.when(ti == 0)
        def _():
            dnext[...] = jnp.zeros_like(dnext)
            dwb_ref[...] = jnp.zeros_like(dwb_ref)

        xv = x_ref[...]
        halo = jnp.where(tt > 0, xp_ref[...], 0.0)
        ext = jnp.concatenate([halo, xv], axis=0)
        pre = _conv_pre(ext, xv, w_ref, tr)
        s = jax.nn.sigmoid(pre)
        dpre = do_ref[...] * (s + pre * s * (1.0 - s))
        dx, sums = _conv_bwd_parts(dpre, dnext[...], xv, ext, w_ref, tr)
        dx_ref[...] = dx.astype(dx_ref.dtype)
        dwb_ref[...] += _rows_block(sums)
        dnext[...] = dpre[:8, :]

    return pl.pallas_call(
        body,
        out_shape=(_sds((t, c), BF), _sds((8, c), F32)),
        grid=(nblk, nt),
        in_specs=[pl.BlockSpec((tr, cw), lambda cb, ti: (nt - 1 - ti, col0_blk + cb)),
                  pl.BlockSpec((8, cw), lambda cb, ti: (jnp.maximum((nt - 1 - ti) * r8 - 1, 0), col0_blk + cb)),
                  pl.BlockSpec((8, cw), lambda cb, ti: (0, cb)),
                  pl.BlockSpec((tr, cw), lambda cb, ti: (nt - 1 - ti, cb))],
        out_specs=(pl.BlockSpec((tr, cw), lambda cb, ti: (nt - 1 - ti, cb)),
                   pl.BlockSpec((8, cw), lambda cb, ti: (0, cb))),
        scratch_shapes=[pltpu.VMEM((8, cw), F32)],
        compiler_params=_cp(("parallel", "arbitrary")),
        name=name,
    )(x, x, wb, dout)


def gated_norm_fwd(y, proj, g, name="gnorm_fwd"):
    t, w = y.shape
    tr = _rt(t)

    def body(y_ref, z_ref, g_ref, o_ref):
        v = y_ref[...] * _silu(z_ref[...])
        r = lax.rsqrt(jnp.mean(v * v, axis=-1, keepdims=True) + EPS)
        o_ref[...] = (v * r * g_ref[...]).astype(o_ref.dtype)

    return pl.pallas_call(
        body,
        out_shape=_sds((t, w), BF),
        grid=(t // tr,),
        in_specs=[pl.BlockSpec((tr, w), lambda i: (i, 0)), pl.BlockSpec((tr, w), lambda i: (i, OFF_Z // w)),
                  pl.BlockSpec((1, w), lambda i: (0, 0))],
        out_specs=pl.BlockSpec((tr, w), lambda i: (i, 0)),
        compiler_params=_cp(("parallel",)),
        name=name,
    )(y, proj, g.reshape(1, w))


def gated_norm_bwd(y, proj, g, dyab, name="gnorm_bwd"):
    t, w = y.shape
    tr = _rt(t)

    def body(y_ref, z_ref, g_ref, do_ref, dy_ref, dz_ref, dg_ref):
        i = pl.program_id(0)
        yv, zv, dov = y_ref[...], z_ref[...], do_ref[...]
        s = jax.nn.sigmoid(zv)
        sz = zv * s
        v = yv * sz
        r = lax.rsqrt(jnp.mean(v * v, axis=-1, keepdims=True) + EPS)
        vh = v * r
        dvg = dov * g_ref[...]
        dv = r * (dvg - vh * jnp.mean(dvg * vh, axis=-1, keepdims=True))
        dy_ref[...] = dv * sz
        dz_ref[...] = (dv * yv * (s + sz * (1.0 - s))).astype(dz_ref.dtype)

        @pl.when(i == 0)
        def _():
            dg_ref[...] = jnp.zeros_like(dg_ref)

        dg_ref[...] += jnp.sum(dov * vh, axis=0, keepdims=True)

    dy, dz, dg = pl.pallas_call(
        body,
        out_shape=(_sds((t, w), F32), _sds((t, w), BF), _sds((1, w), F32)),
        grid=(t // tr,),
        in_specs=[pl.BlockSpec((tr, w), lambda i: (i, 0)), pl.BlockSpec((tr, w), lambda i: (i, OFF_Z // w)),
                  pl.BlockSpec((1, w), lambda i: (0, 0)), pl.BlockSpec((tr, w), lambda i: (i, 0))],
        out_specs=(pl.BlockSpec((tr, w), lambda i: (i, 0)), pl.BlockSpec((tr, w), lambda i: (i, 0)),
                   pl.BlockSpec((1, w), lambda i: (0, 0))),
        compiler_params=_cp(("arbitrary",)),
        name=name,
    )(y, proj, g.reshape(1, w), dyab)
    return dy, dz, dg.reshape(w)


def rope_tables(t):
    inv = ROPE_BASE ** (-jnp.arange(0, MLA_ROPE, 2, dtype=F32) / MLA_ROPE)
    pos = (jnp.arange(t, dtype=F32) - PAD)[:, None]
    ang = pos * inv[None, :]
    cos, sin = jnp.cos(ang), jnp.sin(ang)
    z16 = jnp.zeros((t, 16), F32)
    z32 = jnp.zeros((t, 32), F32)
    c = jnp.concatenate([jnp.ones((t, 64), F32), cos, cos, z32], axis=1)
    s1 = jnp.concatenate([jnp.zeros((t, 64), F32), z16, sin, z32], axis=1)
    s2 = jnp.concatenate([jnp.zeros((t, 64), F32), -sin, z16, z32], axis=1)
    return c, s1, s2


def _rope(x, c, s1, s2):
    return x * c + pltpu.roll(x, 16, 1) * s1 + pltpu.roll(x, LANE - 16, 1) * s2


def _rope_t(d, c, s1, s2):
    return d * c + pltpu.roll(d * s1, LANE - 16, 1) + pltpu.roll(d * s2, 16, 1)


def rope_fwd(q_raw, kv_raw, proj, tabs):
    t = q_raw.shape[0]
    tr = _rt(t)
    hw = MLA_HEADS * LANE

    def body(q_ref, k_ref, v_ref, kr_ref, c_ref, s1_ref, s2_ref, qo_ref, ko_ref, vo_ref):
        c, s1, s2 = c_ref[...], s1_ref[...], s2_ref[...]
        kr = _rope(kr_ref[...], c, s1, s2)
        for h in range(MLA_HEADS):
            sl = slice(h * LANE, (h + 1) * LANE)
            qo_ref[:, sl] = (_rope(q_ref[:, sl], c, s1, s2) * Q_PRESCALE).astype(BF)
            ko_ref[:, sl] = (k_ref[:, sl] + kr).astype(BF)
        vo_ref[...] = v_ref[...].astype(BF)

    tab_spec = pl.BlockSpec((tr, LANE), lambda i: (i, 0))
    return pl.pallas_call(
        body,
        out_shape=(_sds((t, hw), BF), _sds((t, hw), BF), _sds((t, 1024), BF)),
        grid=(t // tr,),
        in_specs=[pl.BlockSpec((tr, hw), lambda i: (i, 0)), pl.BlockSpec((tr, hw), lambda i: (i, 0)),
                  pl.BlockSpec((tr, 1024), lambda i: (i, 2)), pl.BlockSpec((tr, LANE), lambda i: (i, OFF_KR // LANE)),
                  tab_spec, tab_spec, tab_spec],
        out_specs=(pl.BlockSpec((tr, hw), lambda i: (i, 0)), pl.BlockSpec((tr, hw), lambda i: (i, 0)),
                   pl.BlockSpec((tr, 1024), lambda i: (i, 0))),
        compiler_params=_cp(("parallel",)),
        name="rope_fwd",
    )(q_raw, kv_raw, kv_raw, proj, *tabs)


def rope_bwd(dq_cat, dk_cat, tabs):
    t = dq_cat.shape[0]
    tr = _rt(t)
    hw = MLA_HEADS * LANE

    def body(dq_ref, dk_ref, c_ref, s1_ref, s2_ref, dqo_ref, dkr_ref):
        c, s1, s2 = c_ref[...], s1_ref[...], s2_ref[...]
        acc = jnp.zeros((tr, LANE), F32)
        for h in range(MLA_HEADS):
            sl = slice(h * LANE, (h + 1) * LANE)
            dqo_ref[:, sl] = _rope_t(dq_ref[:, sl] * ATT_SCALE, c, s1, s2).astype(BF)
            acc = acc + dk_ref[:, sl]
        lane = lax.broadcasted_iota(jnp.int32, (tr, LANE), 1)
        dkr_ref[...] = jnp.where((lane >= 64) & (lane < 96), _rope_t(acc, c, s1, s2), 0.0)

    tab_spec = pl.BlockSpec((tr, LANE), lambda i: (i, 0))
    return pl.pallas_call(
        body,
        out_shape=(_sds((t, hw), BF), _sds((t, LANE), F32)),
        grid=(t // tr,),
        in_specs=[pl.BlockSpec((tr, hw), lambda i: (i, 0)), pl.BlockSpec((tr, hw), lambda i: (i, 0)),
                  tab_spec, tab_spec, tab_spec],
        out_specs=(pl.BlockSpec((tr, hw), lambda i: (i, 0)), pl.BlockSpec((tr, LANE), lambda i: (i, 0))),
        compiler_params=_cp(("parallel",)),
        name="rope_bwd",
    )(dq_cat, dk_cat, *tabs)


ATT_SCALE = (MLA_NOPE + MLA_ROPE) ** -0.5
LOG2E = math.log2(math.e)
Q_PRESCALE = ATT_SCALE * LOG2E
CARRY_MIDDLE_PAIR = 6
NT_DIMS = (((1,), (1,)), ((), ()))
TN_DIMS = (((0,), (0,)), ((), ()))


def _att_mask(qi, ki, tq, tk):
    qpos = qi * tq + lax.broadcasted_iota(jnp.int32, (tq, tk), 0)
    kpos = ki * tk + lax.broadcasted_iota(jnp.int32, (tq, tk), 1)
    return (kpos <= qpos) & (kpos >= PAD)


def _half_masks(n):
    lane = lax.broadcasted_iota(jnp.int32, (n, LANE), 1)
    return lane < 64, lane >= 64


def _att_tile(t):
    return _pick(t, (384, 256, 128))


def _ds(i, n):
    return pl.ds(i * n, n) if isinstance(i, int) else pl.ds(pl.multiple_of(i * n, n), n)


FWD_PAIRS = 2


def attn_fwd(q_cat, k_cat, v, carry=None):
    t = q_cat.shape[0]
    tq = tk = _att_tile(t)
    nq = t // tq
    npair, nh = FWD_PAIRS, 2 * FWD_PAIRS
    n_grp = MLA_HEADS // nh
    nx = carry.k if carry else 0

    def body(*refs):
        q_ref, k_ref, v_ref = refs[:3]
        o_ref, lse_ref = refs[3 + nx:5 + nx]
        qi = pl.program_id(1)
        if carry:
            start, middle, finish = carry.phases(refs[3:3 + nx], refs[5 + nx:5 + 2 * nx], refs[5 + 2 * nx:])
            grp = pl.program_id(0)
            pl.when((grp == 0) & (qi == 0))(start)
            pl.when((grp == CARRY_MIDDLE_PAIR // npair) & (qi == 0))(middle)
        lo_q, _ = _half_masks(tq)
        halves = _half_masks(tk)

        def step(ki, state, masked):
            m_old, l_old, accs = state[0:nh], state[nh:2 * nh], state[2 * nh:]
            rows = _ds(ki, tk)
            ss = [lax.dot_general(q_ref[:, h * LANE:(h + 1) * LANE], k_ref[rows, h * LANE:(h + 1) * LANE], NT_DIMS,
                                  preferred_element_type=F32) for h in range(nh)]
            if masked:
                valid = _att_mask(qi, ki, tq, tk)
                ss = [jnp.where(valid, s, NEG) for s in ss]
            m_new = [jnp.maximum(m_old[h], jnp.max(ss[h], axis=-1, keepdims=True)) for h in range(nh)]
            ps = [jnp.exp2(ss[h] - m_new[h]) for h in range(nh)]
            alpha = [jnp.exp2(m_old[h] - m_new[h]) for h in range(nh)]
            l_new = [alpha[h] * l_old[h] + jnp.sum(ps[h], axis=-1, keepdims=True) for h in range(nh)]
            new_accs = []
            for pp in range(npair):
                vv = v_ref[rows, pp * LANE:(pp + 1) * LANE]
                pv = [jnp.dot(ps[2 * pp + hh].astype(BF), jnp.where(halves[hh], vv, jnp.zeros_like(vv)),
                              preferred_element_type=F32) for hh in range(2)]
                new_accs.append(accs[pp] * jnp.where(lo_q, alpha[2 * pp], alpha[2 * pp + 1]) + pv[0] + pv[1])
            return tuple(m_new) + tuple(l_new) + tuple(new_accs)

        neg, zero = jnp.full((tq, 1), NEG, F32), jnp.zeros((tq, 1), F32)
        state = step(0, (neg,) * nh + (zero,) * nh + (jnp.zeros((tq, LANE), F32),) * npair, True)
        state = lax.fori_loop(1, qi, lambda ki, st: step(ki, st, False), state)
        state = lax.cond(qi > 0, lambda st: step(qi, st, True), lambda st: st, state)
        for pp in range(npair):
            l = jnp.where(lo_q, state[nh + 2 * pp], state[nh + 2 * pp + 1])
            o_ref[:, pp * LANE:(pp + 1) * LANE] = (state[2 * nh + pp] / l).astype(o_ref.dtype)
            lse_ref[:, pp * LANE:(pp + 1) * LANE] = jnp.where(lo_q, state[2 * pp], state[2 * pp + 1]) + jnp.log2(l)
        if carry:
            pl.when((grp == n_grp - 1) & (qi == nq - 1))(finish)

    outs = pl.pallas_call(
        body,
        out_shape=(_sds((t, 1024), BF), _sds((t, 1024), F32)) + tuple(carry.out_shapes() if carry else ()),
        grid=(n_grp, nq),
        in_specs=[pl.BlockSpec((tq, nh * LANE), lambda g, qi: (qi, g)),
                  pl.BlockSpec((t, nh * LANE), lambda g, qi: (0, g)),
                  pl.BlockSpec((t, npair * LANE), lambda g, qi: (0, g))] + [ANY] * nx,
        out_specs=(pl.BlockSpec((tq, npair * LANE), lambda g, qi: (qi, g)),
                   pl.BlockSpec((tq, npair * LANE), lambda g, qi: (qi, g))) + (ANY,) * nx,
        scratch_shapes=carry.scratch() if carry else [],
        compiler_params=_cp(("arbitrary", "arbitrary") if carry else ("parallel", "parallel")),
        name="attn_fwd_carrying" if carry else "attn_fwd",
    )(q_cat, k_cat, v, *(carry.arrs if carry else ()))
    return outs[0], outs[1], list(outs[2:])


def attn_bwd(q_cat, k_cat, v, o, lse, dyab, carry=None):
    t = q_cat.shape[0]
    tq = tk = _att_tile(t)
    nq = t // tq
    n_pair = MLA_HEADS // 2
    nx = carry.k if carry else 0

    def body(*refs):
        q_ref, k_ref, v_ref, o_ref, lse_ref, do_ref = refs[:6]
        dq_ref, dk_ref, dv_ref = refs[6 + nx:9 + nx]
        ki = pl.program_id(1)
        if carry:
            start, middle, finish = carry.phases(refs[6:6 + nx], refs[9 + nx:9 + 2 * nx], refs[9 + 2 * nx:])
            pair = pl.program_id(0)
            pl.when((pair == 0) & (ki == 0))(start)
            pl.when((pair == CARRY_MIDDLE_PAIR) & (ki == 0))(middle)

        @pl.when(ki == 0)
        def _():
            dq_ref[...] = jnp.zeros_like(dq_ref)

        halves = _half_masks(tq)
        vv = v_ref[...]
        kk = [k_ref[:, hh * LANE:(hh + 1) * LANE] for hh in range(2)]

        def step(qi, acc, masked):
            rows = _ds(qi, tq)
            dov, ov, lse_v = do_ref[rows, :], o_ref[rows, :].astype(F32), lse_ref[rows, :]
            qh = [q_ref[rows, hh * LANE:(hh + 1) * LANE] for hh in range(2)]
            ss = [lax.dot_general(qh[hh], kk[hh], NT_DIMS, preferred_element_type=F32) for hh in range(2)]
            if masked:
                valid = _att_mask(qi, ki, tq, tk)
                ss = [jnp.where(valid, s, NEG) for s in ss]
            ps = [jnp.exp2(ss[hh] - lse_v[:, 64 * hh:64 * hh + 1]) for hh in range(2)]
            dom = [jnp.where(halves[hh], dov, 0.0) for hh in range(2)]
            delta = [jnp.sum(dom[hh] * ov, axis=-1, keepdims=True) for hh in range(2)]
            dom = [d.astype(BF) for d in dom]
            dp = [lax.dot_general(dom[hh], vv, NT_DIMS, preferred_element_type=F32) for hh in range(2)]
            ds = [(ps[hh] * (dp[hh] - delta[hh])).astype(BF) for hh in range(2)]
            pb = [p.astype(BF) for p in ps]
            dv = (acc[2] + lax.dot_general(pb[0], dom[0], TN_DIMS, preferred_element_type=F32)
                  + lax.dot_general(pb[1], dom[1], TN_DIMS, preferred_element_type=F32))
            dk = [acc[hh] + lax.dot_general(ds[hh], qh[hh], TN_DIMS, preferred_element_type=F32) for hh in range(2)]
            for hh in range(2):
                dq_ref[rows, hh * LANE:(hh + 1) * LANE] += jnp.dot(ds[hh], kk[hh], preferred_element_type=F32)
            return dk[0], dk[1], dv

        zero = jnp.zeros((tk, LANE), F32)
        acc = step(ki, (zero, zero, zero), True)
        acc = lax.fori_loop(ki + 1, jnp.where(ki == 0, nq, ki + 1), lambda qi, a: step(qi, a, True), acc)
        acc = lax.fori_loop(ki + 1, jnp.where(ki == 0, ki + 1, nq), lambda qi, a: step(qi, a, False), acc)
        dk_ref[:, 0:LANE] = acc[0] * (1.0 / LOG2E)
        dk_ref[:, LANE:2 * LANE] = acc[1] * (1.0 / LOG2E)
        dv_ref[...] = acc[2]
        if carry:
            pl.when((pair == n_pair - 1) & (ki == nq - 1))(finish)

    full = lambda w, off=0: pl.BlockSpec((t, w), lambda p, ki: (0, p + off))
    blk = lambda w: pl.BlockSpec((tk, w), lambda p, ki: (ki, p))
    outs = pl.pallas_call(
        body,
        out_shape=(_sds((t, 2048), F32), _sds((t, 2048), F32), _sds((t, 1024), F32))
        + tuple(carry.out_shapes() if carry else ()),
        grid=(n_pair, nq),
        in_specs=[full(2 * LANE), blk(2 * LANE), blk(LANE), full(LANE), full(LANE), full(LANE, 8)] + [ANY] * nx,
        out_specs=(full(2 * LANE), blk(2 * LANE), blk(LANE)) + (ANY,) * nx,
        scratch_shapes=carry.scratch() if carry else [],
        compiler_params=_cp(("arbitrary", "arbitrary") if carry else ("parallel", "arbitrary")),
        name="attn_bwd_carrying" if carry else "attn_bwd",
    )(q_cat, k_cat, v, o, lse, dyab, *(carry.arrs if carry else ()))
    return outs[0], outs[1], outs[2], list(outs[3:])


N_PAIR = SSD_HEADS // 2


def _hdot(a, b):
    return jnp.dot(a, b, precision=HI, preferred_element_type=F32)


def _ssd_chunk(xs, bg, cg, dtraw, hin, dt_bias, a_log, dskip, rowmask):
    ln = CHUNK
    causal = lax.broadcasted_iota(jnp.int32, (ln, ln), 0) >= lax.broadcasted_iota(jnp.int32, (ln, ln), 1)
    ltri = causal.astype(F32)
    lane = lax.broadcasted_iota(jnp.int32, (ln, LANE), 1)
    halves = (lane < 64, lane >= 64)
    low_row = lax.broadcasted_iota(jnp.int32, (1, LANE), 1) < 64
    head_lane = lax.broadcasted_iota(jnp.int32, (1, SSD_HEADS), 1)
    head_row = lax.broadcasted_iota(jnp.int32, (SSD_HEADS, 1), 0)

    def col(a, h):
        return jnp.sum(jnp.where(head_lane == h, a, 0.0), axis=1, keepdims=True)

    dt = _softplus(dtraw + dt_bias) * rowmask
    da = dt * (-jnp.exp(a_log))
    acs = _hdot(ltri, da)
    acs_t = lax.dot_general(da, ltri, (((0,), (1,)), ((), ())), precision=HI, preferred_element_type=F32)
    tot = jnp.sum(da, axis=0, keepdims=True)
    bm = [b * rowmask for b in bg]
    cm = [c * rowmask for c in cg]
    cb = [lax.dot_general(cm[g].astype(BF), bm[g].astype(BF), NT_DIMS, preferred_element_type=F32) for g in range(2)]
    ys, hout = [], []
    for p in range(N_PAIR):
        g = p // (N_PAIR // 2)
        h0, h1 = 2 * p, 2 * p + 1
        xdt = xs[p] * jnp.where(halves[0], col(dt, h0), col(dt, h1))
        a_cols = [col(acs, h0), col(acs, h1)]
        tot_cols = [col(tot, h0), col(tot, h1)]
        y = jnp.zeros((ln, LANE), F32)
        snew = jnp.zeros((ln, LANE), F32)
        for hh in range(2):
            a_row = jnp.sum(jnp.where(head_row == h0 + hh, acs_t, 0.0), axis=0, keepdims=True)
            dec = jnp.exp(jnp.where(causal, a_cols[hh] - a_row, NEG))
            xm = jnp.where(halves[hh], xdt, 0.0).astype(BF)
            y = y + jnp.dot((cb[g] * dec).astype(BF), xm, preferred_element_type=F32)
            bd = bm[g] * jnp.exp(tot_cols[hh] - a_cols[hh])
            snew = snew + lax.dot_general(bd.astype(BF), xm, TN_DIMS, preferred_element_type=F32)
        y_off = (jnp.dot(cm[g].astype(BF), hin[p].astype(BF), preferred_element_type=F32)
                 * jnp.where(halves[0], jnp.exp(a_cols[0]), jnp.exp(a_cols[1])))
        ys.append(y + y_off + jnp.where(low_row, col(dskip, h0), col(dskip, h1)) * xs[p])
        hout.append(jnp.where(low_row, jnp.exp(tot_cols[0]), jnp.exp(tot_cols[1])) * hin[p] + snew)
    return ys, hout


def _ssd_load(x_ref, dt_ref):
    xs = [x_ref[:, p * LANE:(p + 1) * LANE] for p in range(N_PAIR)]
    bg = [x_ref[:, SSD_D_INNER + g * LANE:SSD_D_INNER + (g + 1) * LANE] for g in range(2)]
    cg = [x_ref[:, SSD_D_INNER + (2 + g) * LANE:SSD_D_INNER + (3 + g) * LANE] for g in range(2)]
    return xs, bg, cg, dt_ref[:, 0:SSD_HEADS]


def _chunk_rowmask(c):
    return ((c * CHUNK + lax.broadcasted_iota(jnp.int32, (CHUNK, 1), 0)) >= PAD).astype(F32)


def ssd_fwd(xbc_c, proj, dt_bias, a_log, dskip):
    t = xbc_c.shape[0]
    nc = t // CHUNK

    def body(x_ref, dt_ref, dtb_ref, al_ref, d_ref, y_ref, hs_ref, h_s):
        c = pl.program_id(0)

        @pl.when(c == 0)
        def _():
            h_s[...] = jnp.zeros_like(h_s)

        xs, bg, cg, dtraw = _ssd_load(x_ref, dt_ref)
        hin = [h_s[p] for p in range(N_PAIR)]
        hs_ref[0] = h_s[...]
        ys, hout = _ssd_chunk(xs, bg, cg, dtraw, hin, dtb_ref[...], al_ref[...], d_ref[...], _chunk_rowmask(c))
        for p in range(N_PAIR):
            y_ref[:, p * LANE:(p + 1) * LANE] = ys[p]
            h_s[p] = hout[p]

    par = pl.BlockSpec((1, SSD_HEADS), lambda c: (0, 0))
    return pl.pallas_call(
        body,
        out_shape=(_sds((t, SSD_D_INNER), F32), _sds((nc, N_PAIR, CHUNK, LANE), F32)),
        grid=(nc,),
        in_specs=[pl.BlockSpec((CHUNK, SSD_CONV_CH), lambda c: (c, 0)),
                  pl.BlockSpec((CHUNK, LANE), lambda c: (c, OFF_DT // LANE)), par, par, par],
        out_specs=(pl.BlockSpec((CHUNK, SSD_D_INNER), lambda c: (c, 0)),
                   pl.BlockSpec((1, N_PAIR, CHUNK, LANE), lambda c: (c, 0, 0, 0))),
        scratch_shapes=[pltpu.VMEM((N_PAIR, CHUNK, LANE), F32)],
        compiler_params=_cp(("arbitrary",)),
        name="ssd_fwd",
    )(xbc_c, proj, dt_bias.reshape(1, -1), a_log.reshape(1, -1), dskip.reshape(1, -1))


def ssd_bwd(xbc_c, proj, dt_bias, a_log, dskip, hs, dy):
    t = xbc_c.shape[0]
    nc = t // CHUNK

    def body(x_ref, dt_ref, dtb_ref, al_ref, d_ref, hs_ref, dy_ref, dx_ref, ddt_ref, dpar_ref, dh_s):
        ci = pl.program_id(0)
        c = nc - 1 - ci

        @pl.when(ci == 0)
        def _():
            dh_s[...] = jnp.zeros_like(dh_s)
            dpar_ref[...] = jnp.zeros_like(dpar_ref)

        xs, bg, cg, dtraw = _ssd_load(x_ref, dt_ref)
        hin = [hs_ref[0, p] for p in range(N_PAIR)]
        rowmask = _chunk_rowmask(c)
        fn = lambda xs_, bg_, cg_, dtraw_, hin_, dtb_, al_, d_: _ssd_chunk(xs_, bg_, cg_, dtraw_, hin_, dtb_, al_, d_, rowmask)
        _, vjp = jax.vjp(fn, xs, bg, cg, dtraw, hin, dtb_ref[...], al_ref[...], d_ref[...])
        dys = [dy_ref[:, p * LANE:(p + 1) * LANE] for p in range(N_PAIR)]
        dhs = [dh_s[p] for p in range(N_PAIR)]
        dxs, dbg, dcg, ddtraw, dhin, ddtb, dal, dd = vjp((dys, dhs))
        for p in range(N_PAIR):
            dx_ref[:, p * LANE:(p + 1) * LANE] = dxs[p]
            dh_s[p] = dhin[p]
        for g in range(2):
            dx_ref[:, SSD_D_INNER + g * LANE:SSD_D_INNER + (g + 1) * LANE] = dbg[g]
            dx_ref[:, SSD_D_INNER + (2 + g) * LANE:SSD_D_INNER + (3 + g) * LANE] = dcg[g]
        ddt_ref[...] = jnp.zeros_like(ddt_ref)
        ddt_ref[:, 0:SSD_HEADS] = ddtraw
        dpar_ref[0:1, 0:SSD_HEADS] += ddtb
        dpar_ref[1:2, 0:SSD_HEADS] += dal
        dpar_ref[2:3, 0:SSD_HEADS] += dd

    par = pl.BlockSpec((1, SSD_HEADS), lambda ci: (0, 0))
    return pl.pallas_call(
        body,
        out_shape=(_sds((t, SSD_CONV_CH), F32), _sds((t, LANE), F32), _sds((8, LANE), F32)),
        grid=(nc,),
        in_specs=[pl.BlockSpec((CHUNK, SSD_CONV_CH), lambda ci: (nc - 1 - ci, 0)),
                  pl.BlockSpec((CHUNK, LANE), lambda ci: (nc - 1 - ci, OFF_DT // LANE)), par, par, par,
                  pl.BlockSpec((1, N_PAIR, CHUNK, LANE), lambda ci: (nc - 1 - ci, 0, 0, 0)),
                  pl.BlockSpec((CHUNK, SSD_D_INNER), lambda ci: (nc - 1 - ci, 0))],
        out_specs=(pl.BlockSpec((CHUNK, SSD_CONV_CH), lambda ci: (nc - 1 - ci, 0)),
                   pl.BlockSpec((CHUNK, LANE), lambda ci: (nc - 1 - ci, 0)),
                   pl.BlockSpec((8, LANE), lambda ci: (0, 0))),
        scratch_shapes=[pltpu.VMEM((N_PAIR, CHUNK, LANE), F32)],
        compiler_params=_cp(("arbitrary",)),
        name="ssd_bwd",
    )(xbc_c, proj, dt_bias.reshape(1, -1), a_log.reshape(1, -1), dskip.reshape(1, -1), hs, dy)


def _neg_expm1(y):
    series = -(y * (1.0 + y * (0.5 + y * (1.0 / 6.0 + y * (1.0 / 24.0 + y * (1.0 / 120.0))))))
    return jnp.where(y > -0.1, series, 1.0 - jnp.exp(y))


def _rg_pw(xr, wa, ba, wi, bi, lam, rowmask):
    xb = xr.astype(BF)
    r = jax.nn.sigmoid(jnp.dot(xb, wa.astype(BF), preferred_element_type=F32) + ba)
    i = jax.nn.sigmoid(jnp.dot(xb, wi.astype(BF), preferred_element_type=F32) + bi)
    log_a = -LRU_C * r * _softplus(-lam)
    a = jnp.exp(log_a)
    u = jnp.sqrt(_neg_expm1(2.0 * log_a)) * (i * xr) * rowmask
    return a, u


def _gelu_grad(x):
    c = math.sqrt(2.0 / math.pi)
    th = jnp.tanh(c * (x + 0.044715 * (x * x * x)))
    return 0.5 * (1.0 + th) + 0.5 * x * (1.0 - th * th) * c * (1.0 + 3.0 * 0.044715 * x * x)


def _scan_fwd(a, u):
    n = a.shape[0]
    row = lax.broadcasted_iota(jnp.int32, a.shape, 0)
    s = 1
    while s < n:
        a_s = jnp.where(row >= s, pltpu.roll(a, s, 0), 1.0)
        u_s = jnp.where(row >= s, pltpu.roll(u, s, 0), 0.0)
        u = u + a * u_s
        a = a * a_s
        s *= 2
    return a, u


def _scan_bwd(b, d):
    n = b.shape[0]
    row = lax.broadcasted_iota(jnp.int32, b.shape, 0)
    s = 1
    while s < n:
        b_s = jnp.where(row < n - s, pltpu.roll(b, n - s, 0), 1.0)
        d_s = jnp.where(row < n - s, pltpu.roll(d, n - s, 0), 0.0)
        d = d + b * d_s
        b = b * b_s
        s *= 2
    return d


def rg_fwd(xr_pre, gate_pre, rgp, w_a, w_i):
    t = xr_pre.shape[0]
    tr = _rt(t)

    def body(x_ref, g_ref, p_ref, wa_ref, wi_ref, hg_ref, hs_ref, prev, hcar):
        ti = pl.program_id(1)

        @pl.when(ti == 0)
        def _():
            prev[...] = jnp.zeros_like(prev)
            hcar[...] = jnp.zeros_like(hcar)

        xv = x_ref[...]
        ext = jnp.concatenate([prev[...], xv], axis=0)
        xr = _conv_pre(ext, xv, p_ref, tr)
        rowmask = _row_mask(ti, tr, (tr, 1)).astype(F32)
        a, u = _rg_pw(xr, wa_ref[0], p_ref[5:6, :], wi_ref[0], p_ref[6:7, :], p_ref[7:8, :], rowmask)
        a_cum, h_loc = _scan_fwd(a, u)
        hs = h_loc + a_cum * hcar[0:1, :]
        hs_ref[...] = hs
        hg_ref[...] = (hs * _gelu(g_ref[...])).astype(hg_ref.dtype)
        hcar[...] = jnp.broadcast_to(hs[tr - 1:tr, :], (8, LANE))
        prev[...] = xv[tr - 8:, :]

    return pl.pallas_call(
        body,
        out_shape=(_sds((t, LRU_WIDTH), BF), _sds((t, LRU_WIDTH), F32)),
        grid=(LRU_BLOCKS, t // tr),
        in_specs=[pl.BlockSpec((tr, LANE), lambda n, ti: (ti, n)),
                  pl.BlockSpec((tr, LANE), lambda n, ti: (ti, n)),
                  pl.BlockSpec((8, LANE), lambda n, ti: (0, n)),
                  pl.BlockSpec((1, LANE, LANE), lambda n, ti: (n, 0, 0)),
                  pl.BlockSpec((1, LANE, LANE), lambda n, ti: (n, 0, 0))],
        out_specs=(pl.BlockSpec((tr, LANE), lambda n, ti: (ti, n)), pl.BlockSpec((tr, LANE), lambda n, ti: (ti, n))),
        scratch_shapes=[pltpu.VMEM((8, LANE), F32), pltpu.VMEM((8, LANE), F32)],
        compiler_params=_cp(("parallel", "arbitrary")),
        name="rg_fwd",
    )(xr_pre, gate_pre, rgp, w_a, w_i)


def rg_bwd(xr_pre, gate_pre, rgp, w_a, w_i, hs, dhg):
    t = xr_pre.shape[0]
    tr = _rt(t)
    nt = t // tr
    r8 = tr // 8

    def body(x_ref, xp_ref, g_ref, p_ref, wa_ref, wi_ref, hs_ref, hp_ref, dhg_ref,
             dx_ref, dg_ref, dp_ref, dwa_ref, dwi_ref, gcar, dnext):
        ti = pl.program_id(1)
        tt = nt - 1 - ti

        @pl.when(ti == 0)
        def _():
            gcar[...] = jnp.zeros_like(gcar)
            dnext[...] = jnp.zeros_like(dnext)
            dp_ref[...] = jnp.zeros_like(dp_ref)
            dwa_ref[...] = jnp.zeros_like(dwa_ref)
            dwi_ref[...] = jnp.zeros_like(dwi_ref)

        xv = x_ref[...]
        halo = jnp.where(tt > 0, xp_ref[...], 0.0)
        ext = jnp.concatenate([halo, xv], axis=0)
        xr = _conv_pre(ext, xv, p_ref, tr)
        rowmask = _row_mask(tt, tr, (tr, 1)).astype(F32)
        fn = lambda xr_, wa_, ba_, wi_, bi_, lam_: _rg_pw(xr_, wa_, ba_, wi_, bi_, lam_, rowmask)
        (a, _), vjp = jax.vjp(fn, xr, wa_ref[0], p_ref[5:6, :], wi_ref[0], p_ref[6:7, :], p_ref[7:8, :])
        gpre = g_ref[...]
        hsv = hs_ref[...]
        dhg_v = dhg_ref[...]
        dg_ref[...] = (dhg_v * hsv * _gelu_grad(gpre)).astype(dg_ref.dtype)
        row = lax.broadcasted_iota(jnp.int32, (tr, LANE), 0)
        d = dhg_v * _gelu(gpre) + jnp.where(row == tr - 1, gcar[0:1, :], 0.0)
        b = jnp.where(row < tr - 1, pltpu.roll(a, tr - 1, 0), 0.0)
        g = _scan_bwd(b, d)
        gcar[...] = jnp.broadcast_to(a[0:1, :] * g[0:1, :], (8, LANE))
        hlast = jnp.where(tt > 0, hp_ref[7:8, :], 0.0)
        hprev = jnp.where(row == 0, hlast, pltpu.roll(hsv, 1, 0))
        dxr, dwa, dba, dwi, dbi, dlam = vjp((g * hprev, g))
        dx, sums = _conv_bwd_parts(dxr, dnext[...], xv, ext, p_ref, tr)
        dx_ref[...] = dx.astype(dx_ref.dtype)
        dnext[...] = dxr[:8, :]
        dp_ref[...] += _rows_block(sums + [dba, dbi, dlam])
        dwa_ref[0] += dwa
        dwi_ref[0] += dwi

    tile = lambda off=0: pl.BlockSpec((tr, LANE), lambda n, ti: (nt - 1 - ti, off + n))
    halo = lambda off=0: pl.BlockSpec((8, LANE), lambda n, ti: (jnp.maximum((nt - 1 - ti) * r8 - 1, 0), off + n))
    par = pl.BlockSpec((8, LANE), lambda n, ti: (0, n))
    wspec = pl.BlockSpec((1, LANE, LANE), lambda n, ti: (n, 0, 0))
    return pl.pallas_call(
        body,
        out_shape=(_sds((t, LRU_WIDTH), BF), _sds((t, LRU_WIDTH), BF), _sds((8, LRU_WIDTH), F32),
                   _sds((LRU_BLOCKS, LANE, LANE), F32), _sds((LRU_BLOCKS, LANE, LANE), F32)),
        grid=(LRU_BLOCKS, nt),
        in_specs=[tile(), halo(), tile(), par, wspec, wspec, tile(), halo(), tile()],
        out_specs=(tile(), tile(), par, wspec, wspec),
        scratch_shapes=[pltpu.VMEM((8, LANE), F32), pltpu.VMEM((8, LANE), F32)],
        compiler_params=_cp(("parallel", "arbitrary")),
        name="rg_bwd",
    )(xr_pre, xr_pre, gate_pre, rgp, w_a, w_i, hs, hs, dhg)


PACK_W = 1024
MESH_ID = pl.DeviceIdType.MESH
ANY = pl.BlockSpec(memory_space=pl.ANY)


def _my_place():
    x, y, c = lax.axis_index("x"), lax.axis_index("y"), lax.axis_index("c")
    return x, y, c


def _lin(px, py, pc):
    return 4 * px + 2 * py + pc


class Exchange:
    def __init__(self, kind, arrs):
        self.kind, self.arrs, self.k = kind, list(arrs), len(arrs)

    def out_shapes(self):
        if self.kind == "gather":
            return [_sds((N_DEV,) + a.shape, a.dtype) for a in self.arrs]
        return [_sds(a.shape, a.dtype) for a in self.arrs]

    def scratch(self):
        k = self.k
        return [pltpu.SemaphoreType.DMA((k, 7)), pltpu.SemaphoreType.DMA((k, 7)), pltpu.SemaphoreType.DMA((k,))]

    def phases(self, ins, outs, sems):
        return (self._gather if self.kind == "gather" else self._scatter)(ins, outs, *sems)

    def _gather(self, ins, outs, send_sems, recv_sems, local_sems):
        k = self.k
        x, y, c = _my_place()
        me, sibling = (x, y, c), (x, y, 1 - c)
        chips = [(1 - x, y), (x, 1 - y), (1 - x, 1 - y)]

        def copy(a, sem, block, to, from_input=False):
            slab = outs[a].at[_lin(*block)]
            return pltpu.make_async_remote_copy(
                src_ref=ins[a] if from_input else slab, dst_ref=slab,
                send_sem=send_sems.at[a, sem], recv_sem=recv_sems.at[a, sem],
                device_id=to, device_id_type=MESH_ID)

        def mine():
            return [pltpu.make_async_copy(ins[a], outs[a].at[_lin(*me)], local_sems.at[a]) for a in range(k)]

        def first():
            out = []
            for a in range(k):
                out.append(copy(a, 0, me, sibling, True))
                out += [copy(a, 1 + j, me, (*chip, c), True) for j, chip in enumerate(chips)]
            return out

        def passed():
            return [copy(a, 4 + j, (*chip, c), sibling) for j, chip in enumerate(chips) for a in range(k)]

        def start():
            for cp in mine() + first():
                cp.start()

        def middle():
            onward = passed()
            for j, chip in enumerate(chips):
                for a in range(k):
                    copy(a, 1 + j, (*chip, c), me).wait_recv()
                    onward[j * k + a].start()

        def finish():
            for a in range(k):
                copy(a, 0, sibling, me).wait_recv()
                for j, chip in enumerate(chips):
                    copy(a, 4 + j, (*chip, 1 - c), me).wait_recv()
            for cp in first() + passed():
                cp.wait_send()
            for cp in mine():
                cp.wait()

        return start, middle, finish

    def _scatter(self, ins, outs, send_sems, recv_sems, local_sems):
        k = self.k
        x, y, c = _my_place()
        me = _lin(x, y, c)
        peers = [((1 - x) if r & 4 else x, (1 - y) if r & 2 else y, (1 - c) if r & 1 else c) for r in range(1, N_DEV)]

        def copy(a, r, src_slab, dst_slab, to):
            return pltpu.make_async_remote_copy(
                src_ref=ins[a].at[src_slab], dst_ref=outs[a].at[dst_slab],
                send_sem=send_sems.at[a, r], recv_sem=recv_sems.at[a, r],
                device_id=to, device_id_type=MESH_ID)

        def mine():
            return [pltpu.make_async_copy(ins[a].at[me], outs[a].at[me], local_sems.at[a]) for a in range(k)]

        def sends():
            return [copy(a, r, _lin(*peer), me, peer) for r, peer in enumerate(peers) for a in range(k)]

        def start():
            for cp in mine() + sends():
                cp.start()

        def middle():
            pass

        def finish():
            for r, peer in enumerate(peers):
                for a in range(k):
                    copy(a, r, me, _lin(*peer), peer).wait_recv()
            for cp in sends():
                cp.wait_send()
            for cp in mine():
                cp.wait()

        return start, middle, finish

    def run(self, name):
        k = self.k

        def body(*refs):
            start, middle, finish = self.phases(refs[:k], refs[k:2 * k], refs[2 * k:])
            start()
            middle()
            finish()

        return pl.pallas_call(
            body,
            out_shape=tuple(self.out_shapes()),
            in_specs=[ANY] * k,
            out_specs=tuple(ANY for _ in range(k)),
            scratch_shapes=self.scratch(),
            name=name,
        )(*self.arrs)


def all_gather(arrs, name):
    return Exchange("gather", arrs).run(name)


def all_to_all(arrs, name):
    return Exchange("scatter", arrs).run(name)


def slab_sum(a, name):
    _, r, w = a.shape
    tr = _pick(r, (256, 128, 64, 32, 16, 8))

    def body(a_ref, o_ref):
        acc = a_ref[0].astype(F32)
        for d in range(1, N_DEV):
            acc = acc + a_ref[d].astype(F32)
        o_ref[...] = acc

    return pl.pallas_call(
        body,
        out_shape=_sds((r, w), F32),
        grid=(r // tr,),
        in_specs=[pl.BlockSpec((N_DEV, tr, w), lambda i: (0, i, 0))],
        out_specs=pl.BlockSpec((tr, w), lambda i: (i, 0)),
        compiler_params=_cp(("parallel",)),
        name=name,
    )(a)


def _adam_update(w, g, m, v):
    nm = ADAM_B1 * m + (1.0 - ADAM_B1) * g
    nv = ADAM_B2 * v + (1.0 - ADAM_B2) * (g * g)
    m_hat = nm / (1.0 - ADAM_B1 ** ADAM_STEP)
    v_hat = nv / (1.0 - ADAM_B2 ** ADAM_STEP)
    return -ADAM_LR * (m_hat / (jnp.sqrt(v_hat) + ADAM_EPS) + ADAM_WD * w), nm, nv


def adamw_blocks(w, m, v, parts, name):
    nl, r, c = w.shape
    tr = next(t for t in (256, 160, 128, 64, 32, 16) if r % t == 0 and N_DEV * t * c * 2 <= 2 * 1024 * 1024)

    def body(w_ref, m_ref, v_ref, *rest):
        part_refs, (g_ref, d_ref, nm_ref, nv_ref) = rest[:nl], rest[nl:]
        layer = pl.program_id(0)
        for idx in range(nl):
            @pl.when(layer == idx)
            def _(idx=idx):
                g = part_refs[idx][0].astype(F32)
                for dev in range(1, N_DEV):
                    g = g + part_refs[idx][dev].astype(F32)
                g_ref[...] = g
                d_ref[...], nm_ref[...], nv_ref[...] = _adam_update(w_ref[...], g, m_ref[...], v_ref[...])

    spec = pl.BlockSpec((None, tr, c), lambda l, i: (l, i, 0))
    part_spec = lambda idx: pl.BlockSpec((N_DEV, tr, c), lambda l, i: (0, jnp.where(l == idx, i, 0), 0))
    return pl.pallas_call(
        body,
        out_shape=tuple(_sds((nl, r, c), F32) for _ in range(4)),
        grid=(nl, r // tr),
        in_specs=[spec] * 3 + [part_spec(idx) for idx in range(nl)],
        out_specs=(spec,) * 4,
        compiler_params=_cp(("arbitrary", "arbitrary")),
        name=name,
    )(w, m, v, *parts)


def adamw(w, g, m, v, name):
    r, c = w.shape
    tr = _pick(r, (256, 160, 128, 64, 32, 16, 8))

    def body(w_ref, g_ref, m_ref, v_ref, d_ref, nm_ref, nv_ref):
        d_ref[...], nm_ref[...], nv_ref[...] = _adam_update(w_ref[...], g_ref[...], m_ref[...], v_ref[...])

    spec = pl.BlockSpec((tr, c), lambda i: (i, 0))
    return pl.pallas_call(
        body,
        out_shape=tuple(_sds((r, c), F32) for _ in range(3)),
        grid=(r // tr,),
        in_specs=[spec] * 4,
        out_specs=(spec, spec, spec),
        compiler_params=_cp(("parallel",)),
        name=name,
    )(w, g, m, v)


def _relu2_epi(acc):
    r = jnp.maximum(acc, 0.0)
    return r * r, r


def _drelu2_epi(acc, r):
    return (acc * (2.0 * r.astype(F32)),)


def mlp_fwd(h, g_pre, g_post, w_up, w_down, hn=None, g_next=None):
    if hn is None:
        hn = norm_fwd(h, g_pre, BF, name="mlp_norm")
    u, r = matmul(hn, w_up, "nn", (BF, BF), epi=_relu2_epi, name="mlp_up")
    d = matmul(u, w_down, "nn", name="mlp_down")
    h2, hn_next = resadd_fwd(h, d, g_post, g_next, name="mlp_res")
    return h2, (h, hn, u, r, d), hn_next


def mlp_bwd(res, dh2, g_pre, g_post, w_up, w_down, post=None, then=None):
    h, hn, u, r, d = res
    dd, dg_post = post if post is not None else norm_bwd(d, g_post, dh2, mask_pad=True, out_dtype=BF, name="mlp_post_bwd")
    dw_down = matmul(u, dd, "tn", (BF,), name="mlp_dwdown").reshape(w_down.g8.shape)
    dp = matmul(dd, w_down, "nt", (BF,), epi=_drelu2_epi, extras=(r,), name="mlp_du")
    dw_up = matmul(hn, dp, "tn", (BF,), out_blocks=True, name="mlp_dwup")
    dhn = matmul(dp, w_up, "nt", name="mlp_dhn")
    dh, dg_pre, *below = norm_bwd(h, g_pre, dhn, dres=dh2, then=then, name="mlp_pre_bwd")
    return dh, dict(mlp_pre_g=dg_pre, mlp_post_g=dg_post, w_up=dw_up, w_down=dw_down), (tuple(below) or None)


def rg_layer_fwd(h, g_pre, g_post, w_x, w_y, rgp, w_a, w_i, w_out, hn=None, g_next=None):
    if hn is None:
        hn = norm_fwd(h, g_pre, BF, name="rg_norm")
    xr = matmul(hn, w_x, "nn", name="rg_in_x")
    gp = matmul(hn, w_y, "nn", name="rg_in_y")
    hg, hs = rg_fwd(xr, gp, rgp, w_a, w_i)
    m = matmul(hg, w_out, "nn", name="rg_out")
    h2, hn_next = resadd_fwd(h, m, g_post, g_next, name="rg_res")
    return h2, (h, hn, xr, gp, hg, hs, m), hn_next


def rg_layer_bwd(res, dh2, g_pre, g_post, w_x, w_y, rgp, w_a, w_i, w_out, post=None, then=None):
    h, hn, xr, gp, hg, hs, m = res
    dm, dg_post = post if post is not None else norm_bwd(m, g_post, dh2, mask_pad=True, out_dtype=BF, name="rg_post_bwd")
    dw_out = matmul(hg, dm, "tn", name="rg_dwout")
    dhg = matmul(dm, w_out, "nt", name="rg_dhg")
    dxr, dgp, drgp, dwa, dwi = rg_bwd(xr, gp, rgp, w_a, w_i, hs, dhg)
    dw_x = matmul(hn, dxr, "tn", name="rg_dwx")
    dw_y = matmul(hn, dgp, "tn", name="rg_dwy")
    dhn = matmul([dxr, dgp], [w_x, w_y], "nt", name="rg_dhn")
    dh, dg_pre, *below = norm_bwd(h, g_pre, dhn, dres=dh2, then=then, name="rg_pre_bwd")
    grads = dict(mix_pre_g=dg_pre, mix_post_g=dg_post, rg_w_x=dw_x, rg_w_y=dw_y,
                 rg_conv_w=drgp[0:4], rg_conv_b=drgp[4], rg_b_a=drgp[5], rg_b_i=drgp[6], rg_lambda=drgp[7],
                 rg_w_a=dwa, rg_w_i=dwi, rg_w_out=dw_out)
    return dh, grads, (tuple(below) or None)


def sm_layer_fwd(h, g_pre, g_post, w_in_p, convp, dt_bias, a_log, dskip, ssd_g, q_g, w_q_p, kv_g, w_kv_p, w_out, tabs,
                 carry=None, on_carried=None, hn=None, g_next=None):
    if hn is None:
        hn = norm_fwd(h, g_pre, BF, name="sm_norm")
    proj = matmul(hn, w_in_p, "nn", name="sm_in")
    xbc_c = conv_silu_fwd(proj, OFF_XBC, SSD_CONV_CH, convp, name="ssd_conv")
    y, hst = ssd_fwd(xbc_c, proj, dt_bias, a_log, dskip)
    y_ssd = gated_norm_fwd(y, proj, ssd_g)
    cqn = norm_fwd(proj, q_g, BF, col_blk=OFF_CQ // MLA_Q_RANK, width=MLA_Q_RANK, name="q_norm")
    q_raw = matmul(cqn, w_q_p, "nn", name="q_up")
    ckvn = norm_fwd(proj, kv_g, BF, col_blk=OFF_CKV // MLA_KV_RANK, width=MLA_KV_RANK, name="kv_norm")
    kv_raw = matmul(ckvn, w_kv_p, "nn", name="kv_up")
    q_cat, k_cat, v = rope_fwd(q_raw, kv_raw, proj, tabs)
    o, lse, carried = attn_fwd(q_cat, k_cat, v, carry)
    if on_carried is not None:
        on_carried(carried)
    w_out = w_out()
    half = w_out.shape[0] // 2
    m = matmul([y_ssd, o], [KBlock(w_out, half, 0), KBlock(w_out, half, 1)], "nn", name="sm_out")
    res = (h, hn, proj, xbc_c, y, hst, cqn, ckvn, q_cat, k_cat, v, o, lse, y_ssd, m)
    h2, hn_next = resadd_fwd(h, m, g_post, g_next, name="sm_res")
    return h2, res, hn_next


def sm_layer_bwd(res, dh2, g_pre, g_post, w_in_p, convp, dt_bias, a_log, dskip, ssd_g, q_g, w_q_p, kv_g, w_kv_p, w_out, tabs,
                 carry=None, post=None, then=None):
    h, hn, proj, xbc_c, y, hst, cqn, ckvn, q_cat, k_cat, v, o, lse, y_ssd, m = res
    w_out = w_out()
    dm, dg_post = post if post is not None else norm_bwd(m, g_post, dh2, mask_pad=True, out_dtype=BF, name="sm_post_bwd")
    dw_out = jnp.concatenate([matmul(y_ssd, dm, "tn", name="sm_dwout_ssd"), matmul(o, dm, "tn", name="sm_dwout_att")], axis=0)
    dyab = matmul(dm, w_out, "nt", name="sm_dyab")
    dq_cat, dk_cat, dv, carried = attn_bwd(q_cat, k_cat, v, o, lse, dyab, carry(dw_out) if carry is not None else None)
    dq_raw, dkr = rope_bwd(dq_cat, dk_cat, tabs)
    kw = MLA_HEADS * LANE
    dw_kv_p = jnp.concatenate([matmul(ckvn, dk_cat, "tn", name="kv_dw_k"), matmul(ckvn, dv, "tn", name="kv_dw_v")], axis=1)
    dckvn = matmul([dk_cat, dv], [KBlock(w_kv_p, kw, 0), KBlock(w_kv_p, kw // 2, 2)], "nt", name="kv_dx")
    dckv, dg_kv = norm_bwd(proj, kv_g, dckvn, out_dtype=BF, col_blk=OFF_CKV // MLA_KV_RANK, width=MLA_KV_RANK,
                           name="kv_norm_bwd")
    dw_q_p = matmul(cqn, dq_raw, "tn", name="q_dw")
    dcqn = matmul(dq_raw, w_q_p, "nt", name="q_dx")
    dcq, dg_q = norm_bwd(proj, q_g, dcqn, out_dtype=BF, col_blk=OFF_CQ // MLA_Q_RANK, width=MLA_Q_RANK, name="q_norm_bwd")
    dy, dz, dg_ssd = gated_norm_bwd(y, proj, ssd_g, dyab)
    dxbc_c, ddt, dpar = ssd_bwd(xbc_c, proj, dt_bias, a_log, dskip, hst, dy)
    dxbc, dconvp = conv_silu_bwd(proj, OFF_XBC, SSD_CONV_CH, convp, dxbc_c, name="ssd_conv_bwd")
    pieces = [dz, dxbc, dckv, ddt, dkr, dcq]
    dw_in_p = jnp.concatenate([matmul(hn, pc, "tn", (BF,), name="sm_dwin_%d" % i) for i, pc in enumerate(pieces)], axis=1)
    third = SSD_CONV_CH // 3
    a_terms = [dz] + [KBlock(dxbc, third, i) for i in range(3)] + [dckv, ddt, dkr, dcq]
    b_terms = ([KBlock(w_in_p, SSD_D_INNER, 0)] + [KBlock(w_in_p, third, OFF_XBC // third + i) for i in range(3)]
               + [KBlock(w_in_p, MLA_KV_RANK, OFF_CKV // MLA_KV_RANK), KBlock(w_in_p, LANE, OFF_DT // LANE),
                  KBlock(w_in_p, LANE, OFF_KR // LANE), KBlock(w_in_p, MLA_Q_RANK, OFF_CQ // MLA_Q_RANK)])
    dhn = matmul(a_terms, b_terms, "nt", name="sm_dhn")
    dh, dg_pre, *below = norm_bwd(h, g_pre, dhn, dres=dh2, then=then, name="sm_pre_bwd")
    grads = dict(mix_pre_g=dg_pre, mix_post_g=dg_post, w_in=w_in_cols_to_blocks(dw_in_p), ssd_conv_w=dconvp[0:4],
                 ssd_conv_b=dconvp[4], ssd_dt_bias=dpar[0, :SSD_HEADS], ssd_a_log=dpar[1, :SSD_HEADS],
                 ssd_d=dpar[2, :SSD_HEADS], ssd_norm_g=dg_ssd, mla_q_norm_g=dg_q, mla_w_q_up=_unpack_w_q(dw_q_p),
                 mla_kv_norm_g=dg_kv, mla_w_kv_up=_unpack_w_kv(dw_kv_p), w_out_ab=dw_out)
    return dh, grads, carried, (tuple(below) or None)


W_IN_COLS = 3248
W_IN_SHARD = W_IN_COLS // N_DEV
W_IN_WIRE = 512


def _w_in_tables():
    src = np.full((IN_W,), -1, np.int64)
    src[0:2560] = np.arange(2560)
    src[OFF_CKV:OFF_CKV + 256] = 2960 + np.arange(256)
    src[OFF_DT:OFF_DT + 16] = 2560 + np.arange(16)
    src[OFF_KR + 64:OFF_KR + 96] = 3216 + np.arange(32)
    src[OFF_CQ:OFF_CQ + 384] = 2576 + np.arange(384)
    dev = np.where(src >= 0, src // W_IN_SHARD, -1).astype(np.int32).reshape(1, IN_W)
    col = np.where(src >= 0, src % W_IN_SHARD, 0).astype(np.int32).reshape(1, IN_W)
    return dev, col


W_IN_TILE = 384


def _w_in_devices_of_tile(dev):
    return [sorted(set(dev[0, t * W_IN_TILE:(t + 1) * W_IN_TILE].tolist()) - {-1}) for t in range(IN_W // W_IN_TILE)]


def _any_of(index, values):
    cond = index == values[0]
    for v in values[1:]:
        cond = cond | (index == v)
    return cond


def w_in_blocks_to_cols(g8):
    _, k, wp = g8.shape
    tn = W_IN_TILE
    dev, col = _w_in_tables()
    holders = _w_in_devices_of_tile(dev)

    def body(g_ref, dev_ref, col_ref, o_ref):
        i = pl.program_id(0)
        row = lax.broadcasted_iota(jnp.int32, (wp, tn), 0)
        o_ref[...] = jnp.zeros_like(o_ref)
        for j in range(N_DEV):
            tiles = [t for t, devs in enumerate(holders) if j in devs]
            if tiles:
                @pl.when(_any_of(i, tiles))
                def _(j=j):
                    sel = ((row == col_ref[...]) & (dev_ref[...] == j)).astype(BF)
                    o_ref[...] += jnp.dot(g_ref[j], sel, preferred_element_type=F32).astype(o_ref.dtype)

    dev, col = jnp.asarray(dev), jnp.asarray(col)
    return pl.pallas_call(
        body,
        out_shape=_sds((k, IN_W), BF),
        grid=(IN_W // tn,),
        in_specs=[pl.BlockSpec((N_DEV, k, wp), lambda i: (0, 0, 0)), pl.BlockSpec((1, tn), lambda i: (0, i)),
                  pl.BlockSpec((1, tn), lambda i: (0, i))],
        out_specs=pl.BlockSpec((k, tn), lambda i: (0, i)),
        compiler_params=_cp(("parallel",)),
        name="w_in_cols",
    )(g8, dev, col)


def w_in_cols_to_blocks(dw):
    k = dw.shape[0]
    tn = W_IN_TILE
    dev, col = _w_in_tables()
    holders = _w_in_devices_of_tile(dev)

    def body(dw_ref, dev_ref, col_ref, o_ref):
        j = pl.program_id(0)
        row = lax.broadcasted_iota(jnp.int32, (W_IN_WIRE, tn), 0)
        o_ref[...] = jnp.zeros_like(o_ref)
        for t, devs in enumerate(holders):
            if devs:
                @pl.when(_any_of(j, devs))
                def _(t=t):
                    cols = slice(t * tn, (t + 1) * tn)
                    sel = ((row == col_ref[:, cols]) & (dev_ref[:, cols] == j)).astype(BF)
                    o_ref[0] += lax.dot_general(dw_ref[:, cols], sel, NT_DIMS,
                                                preferred_element_type=F32).astype(o_ref.dtype)

    dev, col = jnp.asarray(dev), jnp.asarray(col)
    return pl.pallas_call(
        body,
        out_shape=_sds((N_DEV, k, W_IN_WIRE), BF),
        grid=(N_DEV,),
        in_specs=[pl.BlockSpec((k, IN_W), lambda j: (0, 0)), pl.BlockSpec((1, IN_W), lambda j: (0, 0)),
                  pl.BlockSpec((1, IN_W), lambda j: (0, 0))],
        out_specs=pl.BlockSpec((1, k, W_IN_WIRE), lambda j: (j, 0, 0)),
        compiler_params=_cp(("parallel",)),
        name="w_in_blocks",
    )(dw, dev, col)


def _pack_w_q(w):
    w3 = w.reshape(w.shape[0], MLA_HEADS, MLA_NOPE + MLA_ROPE)
    return jnp.pad(w3, ((0, 0), (0, 0), (0, LANE - MLA_NOPE - MLA_ROPE))).reshape(w.shape[0], MLA_HEADS * LANE)


def _unpack_w_q(p):
    return p.reshape(p.shape[0], MLA_HEADS, LANE)[:, :, :MLA_NOPE + MLA_ROPE].reshape(p.shape[0], -1)


def _pack_w_kv(w):
    w3 = w.reshape(w.shape[0], MLA_HEADS, MLA_NOPE + MLA_V)
    k = jnp.pad(w3[:, :, :MLA_NOPE], ((0, 0), (0, 0), (0, LANE - MLA_NOPE))).reshape(w.shape[0], MLA_HEADS * LANE)
    return jnp.concatenate([k, w3[:, :, MLA_NOPE:].reshape(w.shape[0], MLA_HEADS * MLA_V)], axis=1)


def _unpack_w_kv(p):
    k = p[:, :MLA_HEADS * LANE].reshape(p.shape[0], MLA_HEADS, LANE)[:, :, :MLA_NOPE]
    v = p[:, MLA_HEADS * LANE:].reshape(p.shape[0], MLA_HEADS, MLA_V)
    return jnp.concatenate([k, v], axis=2).reshape(p.shape[0], -1)


def _rows8(rows, width):
    a = jnp.concatenate([r.reshape(-1, width) for r in rows], axis=0)
    return jnp.pad(a, ((0, 8 - a.shape[0]), (0, 0)))


SLAB_ROWS = 16


def _to_slab(flat_list, lead=()):
    cat = jnp.concatenate(flat_list, axis=-1)
    n = cat.shape[-1]
    unit = SLAB_ROWS * PACK_W
    total = -(-n // unit) * unit
    cat = jnp.pad(cat, [(0, 0)] * len(lead) + [(0, total - n)])
    return cat.reshape(lead + (total // PACK_W, PACK_W))


def _from_flat(flat, shapes):
    out, off = [], 0
    for s in shapes:
        n = int(np.prod(s))
        out.append(flat[off:off + n].reshape(s))
        off += n
    return out


def _gathered_full(g8, axis):
    moved = jnp.moveaxis(g8, 0, axis)
    shp = moved.shape
    return moved.reshape(shp[:axis] + (shp[axis] * shp[axis + 1],) + shp[axis + 2:])


def _per_device(full, axis):
    shp = full.shape
    split = full.reshape(shp[:axis] + (N_DEV, shp[axis] // N_DEV) + shp[axis + 1:])
    return jnp.moveaxis(split, axis, 0)


ARG_NAMES = ['x', 'meta_tokens', 'mix_pre_g', 'mix_post_g', 'mlp_pre_g', 'mlp_post_g', 'w_up', 'w_down', 'w_in',
             'ssd_conv_w', 'ssd_conv_b', 'ssd_dt_bias', 'ssd_a_log', 'ssd_d', 'ssd_norm_g', 'mla_q_norm_g',
             'mla_w_q_up', 'mla_kv_norm_g', 'mla_w_kv_up', 'w_out_ab', 'rg_w_x', 'rg_w_y', 'rg_conv_w', 'rg_conv_b',
             'rg_w_a', 'rg_b_a', 'rg_w_i', 'rg_b_i', 'rg_lambda', 'rg_w_out']
WEIGHTS = ARG_NAMES[1:]
BIG = {'w_up': 2, 'w_down': 1, 'w_in': 2, 'mla_w_q_up': 2, 'mla_w_kv_up': 2, 'w_out_ab': 1, 'rg_w_x': 2,
       'rg_w_y': 2, 'rg_w_out': 1}
SMALL = {'meta_tokens': 1, 'ssd_conv_w': 2, 'rg_conv_w': 2, 'rg_conv_b': 1, 'rg_b_a': 1, 'rg_b_i': 1, 'rg_lambda': 1}
REPL = [n for n in WEIGHTS if n not in BIG and n not in SMALL]
REPL_MEDIUM = ['rg_w_a', 'rg_w_i']
REPL_TINY = [n for n in REPL if n not in REPL_MEDIUM]


def _piece_axes():
    axes = {}
    for n, ax in BIG.items():
        for i in range(DEPTH if n in ('w_up', 'w_down') else DEPTH // 2):
            axes[(n, i)] = ax - 1
    return axes


PIECE_AXIS = _piece_axes()
AS_BLOCKS = ('w_up', 'w_down')
_RG = lambda i: [(n, i) for n in ('rg_w_x', 'rg_w_y', 'rg_w_out')]
_MLP = lambda l: [('w_up', l), ('w_down', l)]
_SM_IN = lambda i: [(n, i) for n in ('w_in', 'mla_w_q_up', 'mla_w_kv_up')]
GATHER_FIRST = _SM_IN(0)
GATHER_AT = {0: [('w_out_ab', 0)] + _MLP(0) + _RG(0) + _MLP(1) + _SM_IN(1), 2: [('w_out_ab', 1)] + _MLP(2) + _RG(1) + _MLP(3)}
SCATTER_AT = {2: _MLP(3) + _RG(1) + _MLP(2) + [('w_out_ab', 1)],
              0: _SM_IN(1) + _MLP(1) + _RG(0) + _MLP(0) + [('w_out_ab', 0)]}
SCATTER_LAST = _SM_IN(0)


def _wire_block(p, key):
    n, i = key
    blk = p[n][i]
    if n == 'w_in':
        blk = jnp.pad(blk, ((0, 0), (0, W_IN_WIRE - blk.shape[1])))
    return blk


def _step(p, moments):
    assert DEPTH == 4
    full = {n: [None] * p[n].shape[0] for n in BIG}
    full['w_in_g'] = [None] * p['w_in'].shape[0]

    def weight_blocks(group):
        return [_wire_block(p, k).astype(BF) for k in group]

    def take_weights(group, gathered):
        for (n, i), piece in zip(group, gathered):
            if n == 'w_in':
                full['w_in_g'][i] = piece
            elif n in AS_BLOCKS:
                full[n][i] = DevBlocks(piece, PIECE_AXIS[(n, i)])
            else:
                full[n][i] = _gathered_full(piece, PIECE_AXIS[(n, i)])

    def grad_blocks(group, gw):
        return [gw[k] if k[0] in AS_BLOCKS or k[0] == 'w_in' else _per_device(gw[k], PIECE_AXIS[k]).astype(BF)
                for k in group]

    parts = {}

    small_slab = _to_slab([p[n].reshape(-1) for n in SMALL])
    *first, small8 = all_gather(weight_blocks(GATHER_FIRST) + [small_slab], name="gather_first")
    take_weights(GATHER_FIRST, first)
    for n, piece in zip(SMALL, _from_flat_rows(small8, [p[n].shape for n in SMALL])):
        full[n] = _gathered_full(piece, SMALL[n])
    for n in REPL:
        full[n] = p[n]
    loss_local, grad_x, gw, gsmall_full, carried = _local_step(
        full, p['x'][0], p['loss_target'][0],
        fwd_carry=lambda layer: Exchange("gather", weight_blocks(GATHER_AT[layer])),
        on_fwd_carried=lambda layer, got: take_weights(GATHER_AT[layer], got),
        bwd_carry=lambda layer, gw_now, others: Exchange(
            "scatter", grad_blocks(SCATTER_AT[layer], gw_now)
            + ([jnp.stack(others[n], axis=0).reshape(N_DEV, -1, LANE) for n in REPL_MEDIUM] if layer == 0 else [])))

    for layer, group in SCATTER_AT.items():
        parts.update(zip(group, carried[layer]))
    rep_flat = jnp.concatenate([gsmall_full[n].reshape(-1) for n in REPL_TINY])
    rep_n = rep_flat.shape[0]
    rep_chunk = -(-rep_n // (N_DEV * PACK_W * 8)) * PACK_W * 8
    rep8 = jnp.pad(rep_flat, (0, N_DEV * rep_chunk - rep_n)).reshape(N_DEV, rep_chunk)
    gsmall = _to_slab([_per_device(gsmall_full[n], SMALL[n]).reshape(N_DEV, -1) for n in SMALL] + [rep8], lead=(N_DEV,))
    received = all_to_all(grad_blocks(SCATTER_LAST, gw) + [gsmall], name="scatter_last")
    n_last = len(SCATTER_LAST)
    parts.update(zip(SCATTER_LAST, received[:n_last]))
    ssmall = slab_sum(received[n_last], name="sum_small").reshape(-1)
    medium_mine = [slab_sum(r8, name="sum_" + n) for n, r8 in zip(REPL_MEDIUM, carried[0][len(SCATTER_AT[0]):])]
    g_loc = {'w_in': jnp.stack([slab_sum(parts[('w_in', i)], name="sum_w_in_%d" % i)[:, :W_IN_SHARD]
                                for i in range(p['w_in'].shape[0])], axis=0)}
    small_n = sum(int(np.prod(p[n].shape)) for n in SMALL)
    g_loc.update(zip(SMALL, _from_flat(ssmall, [p[n].shape for n in SMALL])))
    rep_mine = ssmall[small_n:small_n + rep_chunk].reshape(-1, PACK_W)
    rep_all, *medium_all = all_gather([rep_mine] + medium_mine, name="gather_replicated")
    g_loc.update(zip(REPL_TINY, _from_flat(rep_all.reshape(-1), [p[n].shape for n in REPL_TINY])))
    g_loc.update({n: g.reshape(p[n].shape) for n, g in zip(REPL_MEDIUM, medium_all)})

    out = {'loss': lax.psum(loss_local, ("x", "y", "c")), 'grad_x': grad_x[None]}
    small_names = list(SMALL) + REPL_TINY
    for n in list(BIG) + REPL_MEDIUM:
        shp = p[n].shape
        if n == 'w_in' or n in REPL_MEDIUM:
            v2 = lambda a: a.reshape(-1, shp[-1])
            d, nm, nv = adamw(v2(p[n]), v2(g_loc[n]), v2(moments['m_' + n]), v2(moments['v_' + n]), name="adamw_" + n)
            d, nm, nv = d.reshape(shp), nm.reshape(shp), nv.reshape(shp)
        else:
            g_loc[n], d, nm, nv = adamw_blocks(p[n], moments['m_' + n], moments['v_' + n],
                                               [parts[(n, i)] for i in range(shp[0])], name="adamw_" + n)
        out['delta_' + n], out['new_m_' + n], out['new_v_' + n] = d, nm, nv
    slab = lambda src: _to_slab([src(n).reshape(-1) for n in small_names])
    d, nm, nv = adamw(slab(lambda n: p[n]), slab(lambda n: g_loc[n]), slab(lambda n: moments['m_' + n]),
                      slab(lambda n: moments['v_' + n]), name="adamw_small")
    shapes = [p[n].shape for n in small_names]
    for key, flat in (('delta_', d), ('new_m_', nm), ('new_v_', nv)):
        for n, a in zip(small_names, _from_flat(flat.reshape(-1), shapes)):
            out[key + n] = a
    for n in WEIGHTS:
        out['grad_' + n] = g_loc[n]
    return out


def _local_step(full, x, target_rows, fwd_carry=None, on_fwd_carried=None, bwd_carry=None):
    t = PAD + N_META + x.shape[0]
    h = jnp.concatenate([jnp.zeros((PAD, D_MODEL), F32), full['meta_tokens'], x], axis=0)
    target = jnp.concatenate([jnp.zeros((PAD + N_META, D_MODEL), F32), target_rows], axis=0)
    tabs = rope_tables(t)

    def layer_args(layer):
        i = layer // 2
        if layer % 2 == 0:
            convp = _rows8([full['ssd_conv_w'][i], full['ssd_conv_b'][i]], SSD_CONV_CH)
            return (full['mix_pre_g'][layer], full['mix_post_g'][layer], w_in_blocks_to_cols(full['w_in_g'][i]), convp,
                    full['ssd_dt_bias'][i], full['ssd_a_log'][i], full['ssd_d'][i], full['ssd_norm_g'][i],
                    full['mla_q_norm_g'][i], _pack_w_q(full['mla_w_q_up'][i]), full['mla_kv_norm_g'][i],
                    _pack_w_kv(full['mla_w_kv_up'][i]), lambda: full['w_out_ab'][i], tabs)
        rgp = _rows8([full['rg_conv_w'][i], full['rg_conv_b'][i], full['rg_b_a'][i], full['rg_b_i'][i],
                      full['rg_lambda'][i]], LRU_WIDTH)
        return (full['mix_pre_g'][layer], full['mix_post_g'][layer], full['rg_w_x'][i], full['rg_w_y'][i], rgp,
                full['rg_w_a'][i], full['rg_w_i'][i], full['rg_w_out'][i])

    def mlp_args(layer):
        return (full['mlp_pre_g'][layer], full['mlp_post_g'][layer], full['w_up'][layer], full['w_down'][layer])

    saved = []
    hn = None
    for layer in range(DEPTH):
        la = layer_args(layer)
        to_mlp = dict(hn=hn, g_next=full['mlp_pre_g'][layer])
        if layer % 2 == 0:
            if fwd_carry is not None:
                h, res_mix, hn = sm_layer_fwd(h, *la, carry=fwd_carry(layer),
                                              on_carried=lambda got, layer=layer: on_fwd_carried(layer, got), **to_mlp)
            else:
                h, res_mix, hn = sm_layer_fwd(h, *la, **to_mlp)
        else:
            h, res_mix, hn = rg_layer_fwd(h, *la, **to_mlp)
        ma = mlp_args(layer)
        h, res_mlp, hn = mlp_fwd(h, *ma, hn=hn, g_next=full['mix_pre_g'][layer + 1] if layer + 1 < DEPTH else None)
        saved.append((la, ma, res_mix, res_mlp))
    loss_local, dh = loss_fwd_bwd(h, target)

    others = {n: [None] * len(full[n]) for n in WEIGHTS if n not in BIG and n != 'meta_tokens'}
    gw, carried = {}, {}
    post = None
    for layer in reversed(range(DEPTH)):
        la, ma, res_mix, res_mlp = saved[layer]
        dh, gm, post = mlp_bwd(res_mlp, dh, *ma, post=post, then=(res_mix[-1], la[1]))
        below = (saved[layer - 1][3][-1], saved[layer - 1][1][1]) if layer > 0 else None
        if layer % 2 == 0:
            for n in ('w_up', 'w_down'):
                gw[(n, layer)] = gm[n]
            carry = None
            if bwd_carry is not None:
                carry = lambda dw_out, layer=layer: bwd_carry(layer, {**gw, ('w_out_ab', layer // 2): dw_out}, others)
            dh, gx, carried[layer], post = sm_layer_bwd(res_mix, dh, *la, carry=carry, post=post, then=below)
        else:
            dh, gx, post = rg_layer_bwd(res_mix, dh, *la, post=post, then=below)
        for n, g in list(gm.items()) + list(gx.items()):
            i = layer if n in ('mix_pre_g', 'mix_post_g', 'mlp_pre_g', 'mlp_post_g', 'w_up', 'w_down') else layer // 2
            if n in BIG:
                gw[(n, i)] = g
            else:
                others[n][i] = g
    gothers = {n: jnp.stack(v, axis=0) for n, v in others.items()}
    gothers['meta_tokens'] = dh[PAD:PAD + N_META]
    return loss_local, dh[PAD + N_META:], gw, gothers, carried


def _from_flat_rows(g8, shapes):
    flat = g8.reshape(N_DEV, -1)
    out, off = [], 0
    for s in shapes:
        n = int(np.prod(s))
        out.append(flat[:, off:off + n].reshape((N_DEV,) + tuple(s)))
        off += n
    return out


def kernel(x, meta_tokens, mix_pre_g, mix_post_g, mlp_pre_g, mlp_post_g, w_up, w_down, w_in, ssd_conv_w, ssd_conv_b, ssd_dt_bias, ssd_a_log, ssd_d, ssd_norm_g, mla_q_norm_g, mla_w_q_up, mla_kv_norm_g, mla_w_kv_up, w_out_ab, rg_w_x, rg_w_y, rg_conv_w, rg_conv_b, rg_w_a, rg_b_a, rg_w_i, rg_b_i, rg_lambda, rg_w_out, loss_target, m_meta_tokens, m_mix_pre_g, m_mix_post_g, m_mlp_pre_g, m_mlp_post_g, m_w_up, m_w_down, m_w_in, m_ssd_conv_w, m_ssd_conv_b, m_ssd_dt_bias, m_ssd_a_log, m_ssd_d, m_ssd_norm_g, m_mla_q_norm_g, m_mla_w_q_up, m_mla_kv_norm_g, m_mla_w_kv_up, m_w_out_ab, m_rg_w_x, m_rg_w_y, m_rg_conv_w, m_rg_conv_b, m_rg_w_a, m_rg_b_a, m_rg_w_i, m_rg_b_i, m_rg_lambda, m_rg_w_out, v_meta_tokens, v_mix_pre_g, v_mix_post_g, v_mlp_pre_g, v_mlp_post_g, v_w_up, v_w_down, v_w_in, v_ssd_conv_w, v_ssd_conv_b, v_ssd_dt_bias, v_ssd_a_log, v_ssd_d, v_ssd_norm_g, v_mla_q_norm_g, v_mla_w_q_up, v_mla_kv_norm_g, v_mla_w_kv_up, v_w_out_ab, v_rg_w_x, v_rg_w_y, v_rg_conv_w, v_rg_conv_b, v_rg_w_a, v_rg_b_a, v_rg_w_i, v_rg_b_i, v_rg_lambda, v_rg_w_out):
    args = (x, meta_tokens, mix_pre_g, mix_post_g, mlp_pre_g, mlp_post_g, w_up, w_down, w_in, ssd_conv_w, ssd_conv_b, ssd_dt_bias, ssd_a_log, ssd_d, ssd_norm_g, mla_q_norm_g, mla_w_q_up, mla_kv_norm_g, mla_w_kv_up, w_out_ab, rg_w_x, rg_w_y, rg_conv_w, rg_conv_b, rg_w_a, rg_b_a, rg_w_i, rg_b_i, rg_lambda, rg_w_out, loss_target, m_meta_tokens, m_mix_pre_g, m_mix_post_g, m_mlp_pre_g, m_mlp_post_g, m_w_up, m_w_down, m_w_in, m_ssd_conv_w, m_ssd_conv_b, m_ssd_dt_bias, m_ssd_a_log, m_ssd_d, m_ssd_norm_g, m_mla_q_norm_g, m_mla_w_q_up, m_mla_kv_norm_g, m_mla_w_kv_up, m_w_out_ab, m_rg_w_x, m_rg_w_y, m_rg_conv_w, m_rg_conv_b, m_rg_w_a, m_rg_b_a, m_rg_w_i, m_rg_b_i, m_rg_lambda, m_rg_w_out, v_meta_tokens, v_mix_pre_g, v_mix_post_g, v_mlp_pre_g, v_mlp_post_g, v_w_up, v_w_down, v_w_in, v_ssd_conv_w, v_ssd_conv_b, v_ssd_dt_bias, v_ssd_a_log, v_ssd_d, v_ssd_norm_g, v_mla_q_norm_g, v_mla_w_q_up, v_mla_kv_norm_g, v_mla_w_kv_up, v_w_out_ab, v_rg_w_x, v_rg_w_y, v_rg_conv_w, v_rg_conv_b, v_rg_w_a, v_rg_b_a, v_rg_w_i, v_rg_b_i, v_rg_lambda, v_rg_w_out,)
    n_w = len(ARG_NAMES)
    p = dict(zip(ARG_NAMES, args[:n_w]))
    p['loss_target'] = args[n_w]
    moments = {}
    for i, n in enumerate(WEIGHTS):
        moments['m_' + n] = args[n_w + 1 + i]
        moments['v_' + n] = args[n_w + 1 + len(WEIGHTS) + i]
    out = _step(p, moments)
    res = [out['loss'], out['grad_x']]
    for prefix in ('grad_', 'delta_', 'new_m_', 'new_v_'):
        res += [out[prefix + n] for n in WEIGHTS]
    return tuple(res)
```

```python
import math

import numpy as np
import jax
import jax.numpy as jnp
from jax import lax
from jax.experimental import pallas as pl
from jax.experimental.pallas import tpu as pltpu

F32 = jnp.float32
BF = jnp.bfloat16
HI = lax.Precision.HIGHEST

D_MODEL = 1024
DEPTH = 4
N_META = 16
CHUNK = 128
PAD = CHUNK - N_META
EPS = 1e-6
SSD_HEADS = 16
SSD_HEAD_DIM = 64
SSD_D_INNER = 1024
SSD_STATE = 128
SSD_CONV_CH = 1536
MLA_HEADS = 16
MLA_NOPE = 64
MLA_ROPE = 32
MLA_V = 64
MLA_Q_RANK = 384
MLA_KV_RANK = 256
ROPE_BASE = 10000.0
LRU_WIDTH = 1280
LRU_BLOCKS = 10
LRU_C = 8.0
D_FF = 4096
N_DEV = 8
LANE = 128
IN_W = 3456
OFF_Z, OFF_XBC, OFF_CKV, OFF_DT, OFF_KR, OFF_CQ = 0, 1024, 2560, 2816, 2944, 3072

ADAM_LR = 0.001
ADAM_B1 = 0.9
ADAM_B2 = 0.999
ADAM_EPS = 1e-08
ADAM_WD = 0.01
ADAM_STEP = 10

VMEM_LIMIT = 56 * 1024 * 1024
NEG = -1e30


def _pick(n, cands):
    for c in cands:
        if n % c == 0:
            return c
    return n


def _cp(sem=None):
    return pltpu.CompilerParams(dimension_semantics=sem, vmem_limit_bytes=VMEM_LIMIT)


def _sds(shape, dtype):
    return jax.ShapeDtypeStruct(tuple(shape), dtype)


def _silu(x):
    return x * jax.nn.sigmoid(x)


def _softplus(x):
    return jnp.maximum(x, 0.0) + jnp.log(1.0 + jnp.exp(-jnp.abs(x)))


def _gelu(x):
    c = math.sqrt(2.0 / math.pi)
    return 0.5 * x * (1.0 + jnp.tanh(c * (x + 0.044715 * (x * x * x))))


def _row_mask(i, tr, shape, first_valid=PAD):
    row = i * tr + lax.broadcasted_iota(jnp.int32, shape, 0)
    return row >= first_valid


class KBlock:
    def __init__(self, arr, width, blk):
        self.arr, self.width, self.blk = arr, width, blk


class DevBlocks:
    def __init__(self, g8, axis):
        self.g8, self.axis = g8, axis
        _, r, c = g8.shape
        self.shape = (N_DEV * r, c) if axis == 0 else (r, N_DEV * c)


NN_DIMS = (((1,), (0,)), ((), ()))
MM_TALL_K = 1536
MM_WHOLE_K, MM_WHOLE_M = 1024, 4224


def matmul(a, b, mode, out_dtypes=(F32,), epi=None, extras=(), name="mm", tm=None, tn=None, out_blocks=False):
    a_terms = a if isinstance(a, (list, tuple)) else [a]
    b_terms = b if isinstance(b, (list, tuple)) else [b]
    assert len(a_terms) == len(b_terms) and (mode != "tn" or len(a_terms) == 1)
    arr_of = lambda t: t.arr if isinstance(t, KBlock) else t
    if mode == "tn":
        m, n = a_terms[0].shape[1], b_terms[0].shape[1]
    else:
        m = arr_of(a_terms[0]).shape[0]
        b0 = b_terms[0]
        n = (b0.shape if isinstance(b0, DevBlocks) else arr_of(b0).shape)[1 if mode == "nn" else 0]
    if mode == "tn":
        tm = _pick(m, (1024, 512, 384, 256, 128))
    else:
        k_all = sum(t.width if isinstance(t, KBlock) else t.shape[1] for t in a_terms)
        tall = (2112,) if k_all <= MM_TALL_K else ()
        if k_all <= MM_WHOLE_K and m <= MM_WHOLE_M and n % 256 == 0 and tm is None and tn is None:
            tm, tn = m, 256
        tm = tm or _pick(m, tall + (1056, 1024, 768, 640, 512, 384, 256, 128))
    tn = tn or _pick(n, (512, 640, 384, 256, 128))
    dims = {"nn": NN_DIMS, "nt": NT_DIMS, "tn": TN_DIMS}[mode]

    in_specs, args, plan = [], [], []
    for ta, tb in zip(a_terms, b_terms):
        if mode == "tn":
            k = ta.shape[0]
            in_specs += [pl.BlockSpec((k, tm), lambda i, j: (0, i)), pl.BlockSpec((k, tn), lambda i, j: (0, j))]
            args += [ta, tb]
            plan.append(None)
            continue
        if isinstance(ta, KBlock):
            kw, ka = ta.width, ta.blk
            in_specs.append(pl.BlockSpec((tm, kw), lambda i, j, ka=ka: (i, ka)))
        else:
            kw = ta.shape[1]
            in_specs.append(pl.BlockSpec((tm, kw), lambda i, j: (i, 0)))
        args.append(arr_of(ta))
        if isinstance(tb, DevBlocks):
            _, r, c = tb.g8.shape
            split_k = tb.axis == (0 if mode == "nn" else 1)
            if split_k:
                kd = r if mode == "nn" else c
                assert kw == N_DEV * kd
                blk = (N_DEV, kd, tn) if mode == "nn" else (N_DEV, tn, kd)
                in_specs.append(pl.BlockSpec(blk, (lambda i, j: (0, 0, j)) if mode == "nn" else (lambda i, j: (0, j, 0))))
                plan.append(kd)
            else:
                per = (c if mode == "nn" else r) // tn
                blk = (None, kw, tn) if mode == "nn" else (None, tn, kw)
                in_specs.append(pl.BlockSpec(blk, (lambda i, j, per=per: (j // per, 0, j % per)) if mode == "nn"
                                             else (lambda i, j, per=per: (j // per, j % per, 0))))
                plan.append(None)
            args.append(tb.g8)
        else:
            kb = tb.blk if isinstance(tb, KBlock) else 0
            assert (tb.width if isinstance(tb, KBlock) else tb.shape[0 if mode == "nn" else 1]) == kw
            in_specs.append(pl.BlockSpec((kw, tn), lambda i, j, kb=kb: (kb, j)) if mode == "nn"
                            else pl.BlockSpec((tn, kw), lambda i, j, kb=kb: (j, kb)))
            args.append(arr_of(tb))
            plan.append(None)
    n_terms, n_ex = len(plan), len(extras)

    def body(*refs):
        ex_refs, out_refs = refs[2 * n_terms:2 * n_terms + n_ex], refs[2 * n_terms + n_ex:]
        acc = None
        for t, kd in enumerate(plan):
            a_ref, b_ref = refs[2 * t], refs[2 * t + 1]
            if kd is None:
                parts = [lax.dot_general(a_ref[...].astype(BF), b_ref[...].astype(BF), dims, preferred_element_type=F32)]
            else:
                parts = [lax.dot_general(a_ref[:, d * kd:(d + 1) * kd].astype(BF), b_ref[d].astype(BF), dims,
                                         preferred_element_type=F32) for d in range(N_DEV)]
            for part in parts:
                acc = part if acc is None else acc + part
        outs = (acc,) if epi is None else epi(acc, *[r[...] for r in ex_refs])
        for r, o in zip(out_refs, outs):
            r[...] = o.astype(r.dtype)

    o_spec = pl.BlockSpec((tm, tn), lambda i, j: (i, j))
    if out_blocks:
        per = n // N_DEV // tn
        out_shape = tuple(_sds((N_DEV, m, n // N_DEV), dt) for dt in out_dtypes)
        out_specs = tuple(pl.BlockSpec((None, tm, tn), lambda i, j: (j // per, i, j % per)) for _ in out_dtypes)
    else:
        out_shape = tuple(_sds((m, n), dt) for dt in out_dtypes)
        out_specs = tuple(o_spec for _ in out_dtypes)
    outs = pl.pallas_call(
        body,
        out_shape=out_shape,
        grid=(m // tm, n // tn),
        in_specs=in_specs + [o_spec] * n_ex,
        out_specs=out_specs,
        compiler_params=_cp(("parallel", "parallel")),
        name=name,
    )(*args, *extras)
    return outs[0] if len(out_dtypes) == 1 else outs


def _rt(t):
    return _pick(t, (384, 256, 128))


def norm_fwd(x, g, out_dtype, col_blk=0, width=None, name="norm_fwd"):
    t = x.shape[0]
    w = width or x.shape[1]
    tr = _rt(t)

    def body(x_ref, g_ref, o_ref):
        xv = x_ref[...]
        r = lax.rsqrt(jnp.mean(xv * xv, axis=-1, keepdims=True) + EPS)
        o_ref[...] = (xv * r * g_ref[...]).astype(o_ref.dtype)

    return pl.pallas_call(
        body,
        out_shape=_sds((t, w), out_dtype),
        grid=(t // tr,),
        in_specs=[pl.BlockSpec((tr, w), lambda i: (i, col_blk)), pl.BlockSpec((1, w), lambda i: (0, 0))],
        out_specs=pl.BlockSpec((tr, w), lambda i: (i, 0)),
        compiler_params=_cp(("parallel",)),
        name=name,
    )(x, g.reshape(1, w))


def _rms_bwd(xv, gv, dyv):
    r = lax.rsqrt(jnp.mean(xv * xv, axis=-1, keepdims=True) + EPS)
    xh = xv * r
    dyg = dyv * gv
    dx = r * (dyg - xh * jnp.mean(dyg * xh, axis=-1, keepdims=True))
    return dx, jnp.sum(dyv * xh, axis=0, keepdims=True)


def norm_bwd(x, g, dy, dres=None, mask_pad=False, out_dtype=F32, col_blk=0, width=None, dy_col_blk=0, then=None,
             name="norm_bwd"):
    t = x.shape[0]
    w = width or x.shape[1]
    tr = _rt(t)
    has_res, has_then = dres is not None, then is not None

    def body(*refs):
        x_ref, g_ref, dy_ref = refs[:3]
        n_in = 3 + has_res + 2 * has_then
        dx_ref, dg_ref = refs[n_in:n_in + 2]
        i = pl.program_id(0)
        dyv = dy_ref[...].astype(F32)
        if mask_pad:
            dyv = jnp.where(_row_mask(i, tr, dyv.shape), dyv, 0.0)
        dx, dg = _rms_bwd(x_ref[...], g_ref[...], dyv)
        if has_res:
            dx = dx + refs[3][...]
        dx_ref[...] = dx.astype(dx_ref.dtype)

        @pl.when(i == 0)
        def _():
            for r in refs[n_in + 1::2]:
                r[...] = jnp.zeros_like(r)

        dg_ref[...] += dg
        if has_then:
            x2_ref, g2_ref = refs[3 + has_res:5 + has_res]
            dx2_ref, dg2_ref = refs[n_in + 2:]
            dx2, dg2 = _rms_bwd(x2_ref[...], g2_ref[...], jnp.where(_row_mask(i, tr, dx.shape), dx, 0.0))
            dx2_ref[...] = dx2.astype(dx2_ref.dtype)
            dg2_ref[...] += dg2

    row = pl.BlockSpec((tr, w), lambda i: (i, 0))
    vec = pl.BlockSpec((1, w), lambda i: (0, 0))
    in_specs = [pl.BlockSpec((tr, w), lambda i: (i, col_blk)), vec, pl.BlockSpec((tr, w), lambda i: (i, dy_col_blk))]
    args = [x, g.reshape(1, w), dy]
    out_shape, out_specs = [_sds((t, w), out_dtype), _sds((1, w), F32)], [row, vec]
    if has_res:
        in_specs.append(row)
        args.append(dres)
    if has_then:
        in_specs += [row, vec]
        args += [then[0], then[1].reshape(1, w)]
        out_shape += [_sds((t, w), BF), _sds((1, w), F32)]
        out_specs += [row, vec]
    outs = pl.pallas_call(
        body,
        out_shape=tuple(out_shape),
        grid=(t // tr,),
        in_specs=in_specs,
        out_specs=tuple(out_specs),
        compiler_params=_cp(("arbitrary",)),
        name=name,
    )(*args)
    if has_then:
        return outs[0], outs[1].reshape(w), outs[2], outs[3].reshape(w)
    return outs[0], outs[1].reshape(w)


def resadd_fwd(h, m, g, g_next=None, name="resadd"):
    t, w = h.shape
    tr = _rt(t)
    with_next = g_next is not None

    def body(h_ref, m_ref, g_ref, *rest):
        mv = m_ref[...]
        r = lax.rsqrt(jnp.mean(mv * mv, axis=-1, keepdims=True) + EPS)
        y = mv * r * g_ref[...]
        h2 = h_ref[...] + jnp.where(_row_mask(pl.program_id(0), tr, y.shape), y, 0.0)
        if with_next:
            gn_ref, o_ref, hn_ref = rest
            r2 = lax.rsqrt(jnp.mean(h2 * h2, axis=-1, keepdims=True) + EPS)
            hn_ref[...] = (h2 * r2 * gn_ref[...]).astype(hn_ref.dtype)
        else:
            (o_ref,) = rest
        o_ref[...] = h2

    row = pl.BlockSpec((tr, w), lambda i: (i, 0))
    vec = pl.BlockSpec((1, w), lambda i: (0, 0))
    outs = pl.pallas_call(
        body,
        out_shape=(_sds((t, w), F32),) + ((_sds((t, w), BF),) if with_next else ()),
        grid=(t // tr,),
        in_specs=[row, row, vec] + ([vec] if with_next else []),
        out_specs=(row,) + ((row,) if with_next else ()),
        compiler_params=_cp(("parallel",)),
        name=name,
    )(h, m, g.reshape(1, w), *((g_next.reshape(1, w),) if with_next else ()))
    return outs[0], (outs[1] if with_next else None)


def loss_fwd_bwd(h, target):
    t, w = h.shape
    tr = _rt(t)

    def body(h_ref, t_ref, s_ref, dh_ref):
        i = pl.program_id(0)
        err = h_ref[...] - t_ref[...]
        err = jnp.where(_row_mask(i, tr, err.shape, PAD + N_META), err, 0.0)
        dh_ref[...] = err * (1.0 / w)

        @pl.when(i == 0)
        def _():
            s_ref[...] = jnp.zeros_like(s_ref)

        s_ref[...] += jnp.sum(err * err).reshape(1, 1)

    s, dh = pl.pallas_call(
        body,
        out_shape=(_sds((1, LANE), F32), _sds((t, w), F32)),
        grid=(t // tr,),
        in_specs=[pl.BlockSpec((tr, w), lambda i: (i, 0)), pl.BlockSpec((tr, w), lambda i: (i, 0))],
        out_specs=(pl.BlockSpec((1, LANE), lambda i: (0, 0)), pl.BlockSpec((tr, w), lambda i: (i, 0))),
        compiler_params=_cp(("arbitrary",)),
        name="loss",
    )(h, target)
    return 0.5 * s[0, 0] / w, dh


def _shift_down(ext, k, n):
    return pltpu.roll(ext, k, 0)[8:]


def _conv_pre(ext, x, w_ref, n):
    return (w_ref[4:5, :] + w_ref[3:4, :] * x + w_ref[2:3, :] * _shift_down(ext, 1, n)
            + w_ref[1:2, :] * _shift_down(ext, 2, n) + w_ref[0:1, :] * _shift_down(ext, 3, n))


def _conv_bwd_parts(dpre, dnext, x, ext, w_ref, n):
    extd = jnp.concatenate([dpre, dnext], axis=0)
    ln = n + 8
    dx = (w_ref[3:4, :] * dpre + w_ref[2:3, :] * pltpu.roll(extd, ln - 1, 0)[:n]
          + w_ref[1:2, :] * pltpu.roll(extd, ln - 2, 0)[:n] + w_ref[0:1, :] * pltpu.roll(extd, ln - 3, 0)[:n])
    sums = [jnp.sum(dpre * _shift_down(ext, 3, n), axis=0, keepdims=True),
            jnp.sum(dpre * _shift_down(ext, 2, n), axis=0, keepdims=True),
            jnp.sum(dpre * _shift_down(ext, 1, n), axis=0, keepdims=True),
            jnp.sum(dpre * x, axis=0, keepdims=True),
            jnp.sum(dpre, axis=0, keepdims=True)]
    return dx, sums


def _rows_block(sums):
    w = sums[0].shape[1]
    row = lax.broadcasted_iota(jnp.int32, (8, w), 0)
    out = jnp.zeros((8, w), F32)
    for k, s in enumerate(sums):
        out = jnp.where(row == k, s, out)
    return out


CONV_BLOCK = 512


def conv_silu_fwd(x, col0, c, wb, name="conv_fwd"):
    t = x.shape[0]
    cw = _pick(c, (CONV_BLOCK, LANE))
    nblk, col0_blk = c // cw, col0 // cw
    assert col0 % cw == 0
    tr = _rt(t)

    def body(x_ref, w_ref, o_ref, prev):
        ti = pl.program_id(1)

        @pl.when(ti == 0)
        def _():
            prev[...] = jnp.zeros_like(prev)

        xv = x_ref[...]
        ext = jnp.concatenate([prev[...], xv], axis=0)
        o_ref[...] = _silu(_conv_pre(ext, xv, w_ref, tr))
        prev[...] = xv[tr - 8:, :]

    return pl.pallas_call(
        body,
        out_shape=_sds((t, c), F32),
        grid=(nblk, t // tr),
        in_specs=[pl.BlockSpec((tr, cw), lambda cb, ti: (ti, col0_blk + cb)),
                  pl.BlockSpec((8, cw), lambda cb, ti: (0, cb))],
        out_specs=pl.BlockSpec((tr, cw), lambda cb, ti: (ti, cb)),
        scratch_shapes=[pltpu.VMEM((8, cw), F32)],
        compiler_params=_cp(("parallel", "arbitrary")),
        name=name,
    )(x, wb)


def conv_silu_bwd(x, col0, c, wb, dout, name="conv_bwd"):
    t = x.shape[0]
    cw = _pick(c, (CONV_BLOCK, LANE))
    nblk, col0_blk = c // cw, col0 // cw
    assert col0 % cw == 0
    tr = _rt(t)
    nt = t // tr
    r8 = tr // 8

    def body(x_ref, xp_ref, w_ref, do_ref, dx_ref, dwb_ref, dnext):
        ti = pl.program_id(1)
        tt = nt - 1 - ti

        @pl.when(ti == 0)
        def _():
            dnext[...] = jnp.zeros_like(dnext)
            dwb_ref[...] = jnp.zeros_like(dwb_ref)

        xv = x_ref[...]
        halo = jnp.where(tt > 0, xp_ref[...], 0.0)
        ext = jnp.concatenate([halo, xv], axis=0)
        pre = _conv_pre(ext, xv, w_ref, tr)
        s = jax.nn.sigmoid(pre)
        dpre = do_ref[...] * (s + pre * s * (1.0 - s))
        dx, sums = _conv_bwd_parts(dpre, dnext[...], xv, ext, w_ref, tr)
        dx_ref[...] = dx.astype(dx_ref.dtype)
        dwb_ref[...] += _rows_block(sums)
        dnext[...] = dpre[:8, :]

    return pl.pallas_call(
        body,
        out_shape=(_sds((t, c), BF), _sds((8, c), F32)),
        grid=(nblk, nt),
        in_specs=[pl.BlockSpec((tr, cw), lambda cb, ti: (nt - 1 - ti, col0_blk + cb)),
                  pl.BlockSpec((8, cw), lambda cb, ti: (jnp.maximum((nt - 1 - ti) * r8 - 1, 0), col0_blk + cb)),
                  pl.BlockSpec((8, cw), lambda cb, ti: (0, cb)),
                  pl.BlockSpec((tr, cw), lambda cb, ti: (nt - 1 - ti, cb))],
        out_specs=(pl.BlockSpec((tr, cw), lambda cb, ti: (nt - 1 - ti, cb)),
                   pl.BlockSpec((8, cw), lambda cb, ti: (0, cb))),
        scratch_shapes=[pltpu.VMEM((8, cw), F32)],
        compiler_params=_cp(("parallel", "arbitrary")),
        name=name,
    )(x, x, wb, dout)


def gated_norm_fwd(y, proj, g, name="gnorm_fwd"):
    t, w = y.shape
    tr = _rt(t)

    def body(y_ref, z_ref, g_ref, o_ref):
        v = y_ref[...] * _silu(z_ref[...])
        r = lax.rsqrt(jnp.mean(v * v, axis=-1, keepdims=True) + EPS)
        o_ref[...] = (v * r * g_ref[...]).astype(o_ref.dtype)

    return pl.pallas_call(
        body,
        out_shape=_sds((t, w), BF),
        grid=(t // tr,),
        in_specs=[pl.BlockSpec((tr, w), lambda i: (i, 0)), pl.BlockSpec((tr, w), lambda i: (i, OFF_Z // w)),
                  pl.BlockSpec((1, w), lambda i: (0, 0))],
        out_specs=pl.BlockSpec((tr, w), lambda i: (i, 0)),
        compiler_params=_cp(("parallel",)),
        name=name,
    )(y, proj, g.reshape(1, w))


def gated_norm_bwd(y, proj, g, dyab, name="gnorm_bwd"):
    t, w = y.shape
    tr = _rt(t)

    def body(y_ref, z_ref, g_ref, do_ref, dy_ref, dz_ref, dg_ref):
        i = pl.program_id(0)
        yv, zv, dov = y_ref[...], z_ref[...], do_ref[...]
        s = jax.nn.sigmoid(zv)
        sz = zv * s
        v = yv * sz
        r = lax.rsqrt(jnp.mean(v * v, axis=-1, keepdims=True) + EPS)
        vh = v * r
        dvg = dov * g_ref[...]
        dv = r * (dvg - vh * jnp.mean(dvg * vh, axis=-1, keepdims=True))
        dy_ref[...] = dv * sz
        dz_ref[...] = (dv * yv * (s + sz * (1.0 - s))).astype(dz_ref.dtype)

        @pl.when(i == 0)
        def _():
            dg_ref[...] = jnp.zeros_like(dg_ref)

        dg_ref[...] += jnp.sum(dov * vh, axis=0, keepdims=True)

    dy, dz, dg = pl.pallas_call(
        body,
        out_shape=(_sds((t, w), F32), _sds((t, w), BF), _sds((1, w), F32)),
        grid=(t // tr,),
        in_specs=[pl.BlockSpec((tr, w), lambda i: (i, 0)), pl.BlockSpec((tr, w), lambda i: (i, OFF_Z // w)),
                  pl.BlockSpec((1, w), lambda i: (0, 0)), pl.BlockSpec((tr, w), lambda i: (i, 0))],
        out_specs=(pl.BlockSpec((tr, w), lambda i: (i, 0)), pl.BlockSpec((tr, w), lambda i: (i, 0)),
                   pl.BlockSpec((1, w), lambda i: (0, 0))),
        compiler_params=_cp(("arbitrary",)),
        name=name,
    )(y, proj, g.reshape(1, w), dyab)
    return dy, dz, dg.reshape(w)


def rope_tables(t):
    inv = ROPE_BASE ** (-jnp.arange(0, MLA_ROPE, 2, dtype=F32) / MLA_ROPE)
    pos = (jnp.arange(t, dtype=F32) - PAD)[:, None]
    ang = pos * inv[None, :]
    cos, sin = jnp.cos(ang), jnp.sin(ang)
    z16 = jnp.zeros((t, 16), F32)
    z32 = jnp.zeros((t, 32), F32)
    c = jnp.concatenate([jnp.ones((t, 64), F32), cos, cos, z32], axis=1)
    s1 = jnp.concatenate([jnp.zeros((t, 64), F32), z16, sin, z32], axis=1)
    s2 = jnp.concatenate([jnp.zeros((t, 64), F32), -sin, z16, z32], axis=1)
    return c, s1, s2


def _rope(x, c, s1, s2):
    return x * c + pltpu.roll(x, 16, 1) * s1 + pltpu.roll(x, LANE - 16, 1) * s2


def _rope_t(d, c, s1, s2):
    return d * c + pltpu.roll(d * s1, LANE - 16, 1) + pltpu.roll(d * s2, 16, 1)


def rope_fwd(q_raw, kv_raw, proj, tabs):
    t = q_raw.shape[0]
    tr = _rt(t)
    hw = MLA_HEADS * LANE

    def body(q_ref, k_ref, v_ref, kr_ref, c_ref, s1_ref, s2_ref, qo_ref, ko_ref, vo_ref):
        c, s1, s2 = c_ref[...], s1_ref[...], s2_ref[...]
        kr = _rope(kr_ref[...], c, s1, s2)
        for h in range(MLA_HEADS):
            sl = slice(h * LANE, (h + 1) * LANE)
            qo_ref[:, sl] = (_rope(q_ref[:, sl], c, s1, s2) * Q_PRESCALE).astype(BF)
            ko_ref[:, sl] = (k_ref[:, sl] + kr).astype(BF)
        vo_ref[...] = v_ref[...].astype(BF)

    tab_spec = pl.BlockSpec((tr, LANE), lambda i: (i, 0))
    return pl.pallas_call(
        body,
        out_shape=(_sds((t, hw), BF), _sds((t, hw), BF), _sds((t, 1024), BF)),
        grid=(t // tr,),
        in_specs=[pl.BlockSpec((tr, hw), lambda i: (i, 0)), pl.BlockSpec((tr, hw), lambda i: (i, 0)),
                  pl.BlockSpec((tr, 1024), lambda i: (i, 2)), pl.BlockSpec((tr, LANE), lambda i: (i, OFF_KR // LANE)),
                  tab_spec, tab_spec, tab_spec],
        out_specs=(pl.BlockSpec((tr, hw), lambda i: (i, 0)), pl.BlockSpec((tr, hw), lambda i: (i, 0)),
                   pl.BlockSpec((tr, 1024), lambda i: (i, 0))),
        compiler_params=_cp(("parallel",)),
        name="rope_fwd",
    )(q_raw, kv_raw, kv_raw, proj, *tabs)


def rope_bwd(dq_cat, dk_cat, tabs):
    t = dq_cat.shape[0]
    tr = _rt(t)
    hw = MLA_HEADS * LANE

    def body(dq_ref, dk_ref, c_ref, s1_ref, s2_ref, dqo_ref, dkr_ref):
        c, s1, s2 = c_ref[...], s1_ref[...], s2_ref[...]
        acc = jnp.zeros((tr, LANE), F32)
        for h in range(MLA_HEADS):
            sl = slice(h * LANE, (h + 1) * LANE)
            dqo_ref[:, sl] = _rope_t(dq_ref[:, sl] * ATT_SCALE, c, s1, s2).astype(BF)
            acc = acc + dk_ref[:, sl]
        lane = lax.broadcasted_iota(jnp.int32, (tr, LANE), 1)
        dkr_ref[...] = jnp.where((lane >= 64) & (lane < 96), _rope_t(acc, c, s1, s2), 0.0)

    tab_spec = pl.BlockSpec((tr, LANE), lambda i: (i, 0))
    return pl.pallas_call(
        body,
        out_shape=(_sds((t, hw), BF), _sds((t, LANE), F32)),
        grid=(t // tr,),
        in_specs=[pl.BlockSpec((tr, hw), lambda i: (i, 0)), pl.BlockSpec((tr, hw), lambda i: (i, 0)),
                  tab_spec, tab_spec, tab_spec],
        out_specs=(pl.BlockSpec((tr, hw), lambda i: (i, 0)), pl.BlockSpec((tr, LANE), lambda i: (i, 0))),
        compiler_params=_cp(("parallel",)),
        name="rope_bwd",
    )(dq_cat, dk_cat, *tabs)


ATT_SCALE = (MLA_NOPE + MLA_ROPE) ** -0.5
LOG2E = math.log2(math.e)
Q_PRESCALE = ATT_SCALE * LOG2E
CARRY_MIDDLE_PAIR = 6
NT_DIMS = (((1,), (1,)), ((), ()))
TN_DIMS = (((0,), (0,)), ((), ()))


def _att_mask(qi, ki, tq, tk):
    qpos = qi * tq + lax.broadcasted_iota(jnp.int32, (tq, tk), 0)
    kpos = ki * tk + lax.broadcasted_iota(jnp.int32, (tq, tk), 1)
    return (kpos <= qpos) & (kpos >= PAD)


def _half_masks(n):
    lane = lax.broadcasted_iota(jnp.int32, (n, LANE), 1)
    return lane < 64, lane >= 64


def _att_tile(t):
    return _pick(t, (384, 256, 128))


def _ds(i, n):
    return pl.ds(i * n, n) if isinstance(i, int) else pl.ds(pl.multiple_of(i * n, n), n)


FWD_PAIRS = 4


def attn_fwd(q_cat, k_cat, v, carry=None):
    t = q_cat.shape[0]
    tq = tk = _att_tile(t)
    nq = t // tq
    npair, nh = FWD_PAIRS, 2 * FWD_PAIRS
    n_grp = MLA_HEADS // nh
    nx = carry.k if carry else 0

    def body(*refs):
        q_ref, k_ref, v_ref = refs[:3]
        o_ref, lse_ref = refs[3 + nx:5 + nx]
        qi = pl.program_id(1)
        if carry:
            start, middle, finish = carry.phases(refs[3:3 + nx], refs[5 + nx:5 + 2 * nx], refs[5 + 2 * nx:])
            grp = pl.program_id(0)
            pl.when((grp == 0) & (qi == 0))(start)
            pl.when(grp * nq + qi == (CARRY_MIDDLE_PAIR * n_grp * nq) // (MLA_HEADS // 2))(middle)
        lo_q, _ = _half_masks(tq)
        halves = _half_masks(tk)

        def step(ki, state, masked):
            m_old, l_old, accs = state[0:nh], state[nh:2 * nh], state[2 * nh:]
            rows = _ds(ki, tk)
            ss = [lax.dot_general(q_ref[:, h * LANE:(h + 1) * LANE], k_ref[rows, h * LANE:(h + 1) * LANE], NT_DIMS,
                                  preferred_element_type=F32) for h in range(nh)]
            if masked:
                valid = _att_mask(qi, ki, tq, tk)
                ss = [jnp.where(valid, s, NEG) for s in ss]
            m_new = [jnp.maximum(m_old[h], jnp.max(ss[h], axis=-1, keepdims=True)) for h in range(nh)]
            ps = [jnp.exp2(ss[h] - m_new[h]) for h in range(nh)]
            alpha = [jnp.exp2(m_old[h] - m_new[h]) for h in range(nh)]
            l_new = [alpha[h] * l_old[h] + jnp.sum(ps[h], axis=-1, keepdims=True) for h in range(nh)]
            new_accs = []
            for pp in range(npair):
                vv = v_ref[rows, pp * LANE:(pp + 1) * LANE]
                pv = [jnp.dot(ps[2 * pp + hh].astype(BF), jnp.where(halves[hh], vv, jnp.zeros_like(vv)),
                              preferred_element_type=F32) for hh in range(2)]
                new_accs.append(accs[pp] * jnp.where(lo_q, alpha[2 * pp], alpha[2 * pp + 1]) + pv[0] + pv[1])
            return tuple(m_new) + tuple(l_new) + tuple(new_accs)

        neg, zero = jnp.full((tq, 1), NEG, F32), jnp.zeros((tq, 1), F32)
        state = step(0, (neg,) * nh + (zero,) * nh + (jnp.zeros((tq, LANE), F32),) * npair, True)
        state = lax.fori_loop(1, qi, lambda ki, st: step(ki, st, False), state)
        state = lax.cond(qi > 0, lambda st: step(qi, st, True), lambda st: st, state)
        for pp in range(npair):
            l = jnp.where(lo_q, state[nh + 2 * pp], state[nh + 2 * pp + 1])
            o_ref[:, pp * LANE:(pp + 1) * LANE] = (state[2 * nh + pp] / l).astype(o_ref.dtype)
            lse_ref[:, pp * LANE:(pp + 1) * LANE] = jnp.where(lo_q, state[2 * pp], state[2 * pp + 1]) + jnp.log2(l)
        if carry:
            pl.when((grp == n_grp - 1) & (qi == nq - 1))(finish)

    outs = pl.pallas_call(
        body,
        out_shape=(_sds((t, 1024), BF), _sds((t, 1024), F32)) + tuple(carry.out_shapes() if carry else ()),
        grid=(n_grp, nq),
        in_specs=[pl.BlockSpec((tq, nh * LANE), lambda g, qi: (qi, g)),
                  pl.BlockSpec((t, nh * LANE), lambda g, qi: (0, g)),
                  pl.BlockSpec((t, npair * LANE), lambda g, qi: (0, g))] + [ANY] * nx,
        out_specs=(pl.BlockSpec((tq, npair * LANE), lambda g, qi: (qi, g)),
                   pl.BlockSpec((tq, npair * LANE), lambda g, qi: (qi, g))) + (ANY,) * nx,
        scratch_shapes=carry.scratch() if carry else [],
        compiler_params=_cp(("arbitrary", "arbitrary") if carry else ("parallel", "parallel")),
        name="attn_fwd_carrying" if carry else "attn_fwd",
    )(q_cat, k_cat, v, *(carry.arrs if carry else ()))
    return outs[0], outs[1], list(outs[2:])


def attn_bwd(q_cat, k_cat, v, o, lse, dyab, carry=None):
    t = q_cat.shape[0]
    tq = tk = _att_tile(t)
    nq = t // tq
    n_pair = MLA_HEADS // 2
    nx = carry.k if carry else 0

    def body(*refs):
        q_ref, k_ref, v_ref, o_ref, lse_ref, do_ref = refs[:6]
        dq_ref, dk_ref, dv_ref = refs[6 + nx:9 + nx]
        ki = pl.program_id(1)
        if carry:
            start, middle, finish = carry.phases(refs[6:6 + nx], refs[9 + nx:9 + 2 * nx], refs[9 + 2 * nx:])
            pair = pl.program_id(0)
            pl.when((pair == 0) & (ki == 0))(start)
            pl.when((pair == CARRY_MIDDLE_PAIR) & (ki == 0))(middle)

        @pl.when(ki == 0)
        def _():
            dq_ref[...] = jnp.zeros_like(dq_ref)

        halves = _half_masks(tq)
        vv = v_ref[...]
        kk = [k_ref[:, hh * LANE:(hh + 1) * LANE] for hh in range(2)]

        def step(qi, acc, masked):
            rows = _ds(qi, tq)
            dov, ov, lse_v = do_ref[rows, :], o_ref[rows, :].astype(F32), lse_ref[rows, :]
            qh = [q_ref[rows, hh * LANE:(hh + 1) * LANE] for hh in range(2)]
            ss = [lax.dot_general(qh[hh], kk[hh], NT_DIMS, preferred_element_type=F32) for hh in range(2)]
            if masked:
                valid = _att_mask(qi, ki, tq, tk)
                ss = [jnp.where(valid, s, NEG) for s in ss]
            ps = [jnp.exp2(ss[hh] - lse_v[:, 64 * hh:64 * hh + 1]) for hh in range(2)]
            dom = [jnp.where(halves[hh], dov, 0.0) for hh in range(2)]
            delta = [jnp.sum(dom[hh] * ov, axis=-1, keepdims=True) for hh in range(2)]
            dom = [d.astype(BF) for d in dom]
            dp = [lax.dot_general(dom[hh], vv, NT_DIMS, preferred_element_type=F32) for hh in range(2)]
            ds = [(ps[hh] * (dp[hh] - delta[hh])).astype(BF) for hh in range(2)]
            pb = [p.astype(BF) for p in ps]
            dv = (acc[2] + lax.dot_general(pb[0], dom[0], TN_DIMS, preferred_element_type=F32)
                  + lax.dot_general(pb[1], dom[1], TN_DIMS, preferred_element_type=F32))
            dk = [acc[hh] + lax.dot_general(ds[hh], qh[hh], TN_DIMS, preferred_element_type=F32) for hh in range(2)]
            for hh in range(2):
                dq_ref[rows, hh * LANE:(hh + 1) * LANE] += jnp.dot(ds[hh], kk[hh], preferred_element_type=F32)
            return dk[0], dk[1], dv

        zero = jnp.zeros((tk, LANE), F32)
        acc = step(ki, (zero, zero, zero), True)
        acc = lax.fori_loop(ki + 1, jnp.where(ki == 0, nq, ki + 1), lambda qi, a: step(qi, a, True), acc)
        acc = lax.fori_loop(ki + 1, jnp.where(ki == 0, ki + 1, nq), lambda qi, a: step(qi, a, False), acc)
        dk_ref[:, 0:LANE] = acc[0] * (1.0 / LOG2E)
        dk_ref[:, LANE:2 * LANE] = acc[1] * (1.0 / LOG2E)
        dv_ref[...] = acc[2]
        if carry:
            pl.when((pair == n_pair - 1) & (ki == nq - 1))(finish)

    full = lambda w, off=0: pl.BlockSpec((t, w), lambda p, ki: (0, p + off))
    blk = lambda w: pl.BlockSpec((tk, w), lambda p, ki: (ki, p))
    outs = pl.pallas_call(
        body,
        out_shape=(_sds((t, 2048), F32), _sds((t, 2048), F32), _sds((t, 1024), F32))
        + tuple(carry.out_shapes() if carry else ()),
        grid=(n_pair, nq),
        in_specs=[full(2 * LANE), blk(2 * LANE), blk(LANE), full(LANE), full(LANE), full(LANE, 8)] + [ANY] * nx,
        out_specs=(full(2 * LANE), blk(2 * LANE), blk(LANE)) + (ANY,) * nx,
        scratch_shapes=carry.scratch() if carry else [],
        compiler_params=_cp(("arbitrary", "arbitrary") if carry else ("parallel", "arbitrary")),
        name="attn_bwd_carrying" if carry else "attn_bwd",
    )(q_cat, k_cat, v, o, lse, dyab, *(carry.arrs if carry else ()))
    return outs[0], outs[1], outs[2], list(outs[3:])


N_PAIR = SSD_HEADS // 2


def _hdot(a, b):
    return jnp.dot(a, b, precision=HI, preferred_element_type=F32)


def _ssd_chunk(xs, bg, cg, dtraw, hin, dt_bias, a_log, dskip, rowmask):
    ln = CHUNK
    causal = lax.broadcasted_iota(jnp.int32, (ln, ln), 0) >= lax.broadcasted_iota(jnp.int32, (ln, ln), 1)
    ltri = causal.astype(F32)
    lane = lax.broadcasted_iota(jnp.int32, (ln, LANE), 1)
    halves = (lane < 64, lane >= 64)
    low_row = lax.broadcasted_iota(jnp.int32, (1, LANE), 1) < 64
    head_lane = lax.broadcasted_iota(jnp.int32, (1, SSD_HEADS), 1)
    head_row = lax.broadcasted_iota(jnp.int32, (SSD_HEADS, 1), 0)

    def col(a, h):
        return jnp.sum(jnp.where(head_lane == h, a, 0.0), axis=1, keepdims=True)

    dt = _softplus(dtraw + dt_bias) * rowmask
    da = dt * (-jnp.exp(a_log))
    acs = _hdot(ltri, da)
    acs_t = lax.dot_general(da, ltri, (((0,), (1,)), ((), ())), precision=HI, preferred_element_type=F32)
    tot = jnp.sum(da, axis=0, keepdims=True)
    bm = [b * rowmask for b in bg]
    cm = [c * rowmask for c in cg]
    cb = [lax.dot_general(cm[g].astype(BF), bm[g].astype(BF), NT_DIMS, preferred_element_type=F32) for g in range(2)]
    ys, hout = [], []
    for p in range(N_PAIR):
        g = p // (N_PAIR // 2)
        h0, h1 = 2 * p, 2 * p + 1
        xdt = xs[p] * jnp.where(halves[0], col(dt, h0), col(dt, h1))
        a_cols = [col(acs, h0), col(acs, h1)]
        tot_cols = [col(tot, h0), col(tot, h1)]
        y = jnp.zeros((ln, LANE), F32)
        snew = jnp.zeros((ln, LANE), F32)
        for hh in range(2):
            a_row = jnp.sum(jnp.where(head_row == h0 + hh, acs_t, 0.0), axis=0, keepdims=True)
            dec = jnp.exp(jnp.where(causal, a_cols[hh] - a_row, NEG))
            xm = jnp.where(halves[hh], xdt, 0.0).astype(BF)
            y = y + jnp.dot((cb[g] * dec).astype(BF), xm, preferred_element_type=F32)
            bd = bm[g] * jnp.exp(tot_cols[hh] - a_cols[hh])
            snew = snew + lax.dot_general(bd.astype(BF), xm, TN_DIMS, preferred_element_type=F32)
        y_off = (jnp.dot(cm[g].astype(BF), hin[p].astype(BF), preferred_element_type=F32)
                 * jnp.where(halves[0], jnp.exp(a_cols[0]), jnp.exp(a_cols[1])))
        ys.append(y + y_off + jnp.where(low_row, col(dskip, h0), col(dskip, h1)) * xs[p])
        hout.append(jnp.where(low_row, jnp.exp(tot_cols[0]), jnp.exp(tot_cols[1])) * hin[p] + snew)
    return ys, hout


def _ssd_load(x_ref, dt_ref):
    xs = [x_ref[:, p * LANE:(p + 1) * LANE] for p in range(N_PAIR)]
    bg = [x_ref[:, SSD_D_INNER + g * LANE:SSD_D_INNER + (g + 1) * LANE] for g in range(2)]
    cg = [x_ref[:, SSD_D_INNER + (2 + g) * LANE:SSD_D_INNER + (3 + g) * LANE] for g in range(2)]
    return xs, bg, cg, dt_ref[:, 0:SSD_HEADS]


def _chunk_rowmask(c):
    return ((c * CHUNK + lax.broadcasted_iota(jnp.int32, (CHUNK, 1), 0)) >= PAD).astype(F32)


def ssd_fwd(xbc_c, proj, dt_bias, a_log, dskip):
    t = xbc_c.shape[0]
    nc = t // CHUNK

    def body(x_ref, dt_ref, dtb_ref, al_ref, d_ref, y_ref, hs_ref, h_s):
        c = pl.program_id(0)

        @pl.when(c == 0)
        def _():
            h_s[...] = jnp.zeros_like(h_s)

        xs, bg, cg, dtraw = _ssd_load(x_ref, dt_ref)
        hin = [h_s[p] for p in range(N_PAIR)]
        hs_ref[0] = h_s[...]
        ys, hout = _ssd_chunk(xs, bg, cg, dtraw, hin, dtb_ref[...], al_ref[...], d_ref[...], _chunk_rowmask(c))
        for p in range(N_PAIR):
            y_ref[:, p * LANE:(p + 1) * LANE] = ys[p]
            h_s[p] = hout[p]

    par = pl.BlockSpec((1, SSD_HEADS), lambda c: (0, 0))
    return pl.pallas_call(
        body,
        out_shape=(_sds((t, SSD_D_INNER), F32), _sds((nc, N_PAIR, CHUNK, LANE), F32)),
        grid=(nc,),
        in_specs=[pl.BlockSpec((CHUNK, SSD_CONV_CH), lambda c: (c, 0)),
                  pl.BlockSpec((CHUNK, LANE), lambda c: (c, OFF_DT // LANE)), par, par, par],
        out_specs=(pl.BlockSpec((CHUNK, SSD_D_INNER), lambda c: (c, 0)),
                   pl.BlockSpec((1, N_PAIR, CHUNK, LANE), lambda c: (c, 0, 0, 0))),
        scratch_shapes=[pltpu.VMEM((N_PAIR, CHUNK, LANE), F32)],
        compiler_params=_cp(("arbitrary",)),
        name="ssd_fwd",
    )(xbc_c, proj, dt_bias.reshape(1, -1), a_log.reshape(1, -1), dskip.reshape(1, -1))


def ssd_bwd(xbc_c, proj, dt_bias, a_log, dskip, hs, dy):
    t = xbc_c.shape[0]
    nc = t // CHUNK

    def body(x_ref, dt_ref, dtb_ref, al_ref, d_ref, hs_ref, dy_ref, dx_ref, ddt_ref, dpar_ref, dh_s):
        ci = pl.program_id(0)
        c = nc - 1 - ci

        @pl.when(ci == 0)
        def _():
            dh_s[...] = jnp.zeros_like(dh_s)
            dpar_ref[...] = jnp.zeros_like(dpar_ref)

        xs, bg, cg, dtraw = _ssd_load(x_ref, dt_ref)
        hin = [hs_ref[0, p] for p in range(N_PAIR)]
        rowmask = _chunk_rowmask(c)
        fn = lambda xs_, bg_, cg_, dtraw_, hin_, dtb_, al_, d_: _ssd_chunk(xs_, bg_, cg_, dtraw_, hin_, dtb_, al_, d_, rowmask)
        _, vjp = jax.vjp(fn, xs, bg, cg, dtraw, hin, dtb_ref[...], al_ref[...], d_ref[...])
        dys = [dy_ref[:, p * LANE:(p + 1) * LANE] for p in range(N_PAIR)]
        dhs = [dh_s[p] for p in range(N_PAIR)]
        dxs, dbg, dcg, ddtraw, dhin, ddtb, dal, dd = vjp((dys, dhs))
        for p in range(N_PAIR):
            dx_ref[:, p * LANE:(p + 1) * LANE] = dxs[p]
            dh_s[p] = dhin[p]
        for g in range(2):
            dx_ref[:, SSD_D_INNER + g * LANE:SSD_D_INNER + (g + 1) * LANE] = dbg[g]
            dx_ref[:, SSD_D_INNER + (2 + g) * LANE:SSD_D_INNER + (3 + g) * LANE] = dcg[g]
        ddt_ref[...] = jnp.zeros_like(ddt_ref)
        ddt_ref[:, 0:SSD_HEADS] = ddtraw
        dpar_ref[0:1, 0:SSD_HEADS] += ddtb
        dpar_ref[1:2, 0:SSD_HEADS] += dal
        dpar_ref[2:3, 0:SSD_HEADS] += dd

    par = pl.BlockSpec((1, SSD_HEADS), lambda ci: (0, 0))
    return pl.pallas_call(
        body,
        out_shape=(_sds((t, SSD_CONV_CH), F32), _sds((t, LANE), F32), _sds((8, LANE), F32)),
        grid=(nc,),
        in_specs=[pl.BlockSpec((CHUNK, SSD_CONV_CH), lambda ci: (nc - 1 - ci, 0)),
                  pl.BlockSpec((CHUNK, LANE), lambda ci: (nc - 1 - ci, OFF_DT // LANE)), par, par, par,
                  pl.BlockSpec((1, N_PAIR, CHUNK, LANE), lambda ci: (nc - 1 - ci, 0, 0, 0)),
                  pl.BlockSpec((CHUNK, SSD_D_INNER), lambda ci: (nc - 1 - ci, 0))],
        out_specs=(pl.BlockSpec((CHUNK, SSD_CONV_CH), lambda ci: (nc - 1 - ci, 0)),
                   pl.BlockSpec((CHUNK, LANE), lambda ci: (nc - 1 - ci, 0)),
                   pl.BlockSpec((8, LANE), lambda ci: (0, 0))),
        scratch_shapes=[pltpu.VMEM((N_PAIR, CHUNK, LANE), F32)],
        compiler_params=_cp(("arbitrary",)),
        name="ssd_bwd",
    )(xbc_c, proj, dt_bias.reshape(1, -1), a_log.reshape(1, -1), dskip.reshape(1, -1), hs, dy)


def _neg_expm1(y):
    series = -(y * (1.0 + y * (0.5 + y * (1.0 / 6.0 + y * (1.0 / 24.0 + y * (1.0 / 120.0))))))
    return jnp.where(y > -0.1, series, 1.0 - jnp.exp(y))


def _rg_pw(xr, wa, ba, wi, bi, lam, rowmask):
    xb = xr.astype(BF)
    r = jax.nn.sigmoid(jnp.dot(xb, wa.astype(BF), preferred_element_type=F32) + ba)
    i = jax.nn.sigmoid(jnp.dot(xb, wi.astype(BF), preferred_element_type=F32) + bi)
    log_a = -LRU_C * r * _softplus(-lam)
    a = jnp.exp(log_a)
    u = jnp.sqrt(_neg_expm1(2.0 * log_a)) * (i * xr) * rowmask
    return a, u


def _gelu_grad(x):
    c = math.sqrt(2.0 / math.pi)
    th = jnp.tanh(c * (x + 0.044715 * (x * x * x)))
    return 0.5 * (1.0 + th) + 0.5 * x * (1.0 - th * th) * c * (1.0 + 3.0 * 0.044715 * x * x)


def _scan_fwd(a, u):
    n = a.shape[0]
    row = lax.broadcasted_iota(jnp.int32, a.shape, 0)
    s = 1
    while s < n:
        a_s = jnp.where(row >= s, pltpu.roll(a, s, 0), 1.0)
        u_s = jnp.where(row >= s, pltpu.roll(u, s, 0), 0.0)
        u = u + a * u_s
        a = a * a_s
        s *= 2
    return a, u


def _scan_bwd(b, d):
    n = b.shape[0]
    row = lax.broadcasted_iota(jnp.int32, b.shape, 0)
    s = 1
    while s < n:
        b_s = jnp.where(row < n - s, pltpu.roll(b, n - s, 0), 1.0)
        d_s = jnp.where(row < n - s, pltpu.roll(d, n - s, 0), 0.0)
        d = d + b * d_s
        b = b * b_s
        s *= 2
    return d


def rg_fwd(xr_pre, gate_pre, rgp, w_a, w_i):
    t = xr_pre.shape[0]
    tr = _rt(t)

    def body(x_ref, g_ref, p_ref, wa_ref, wi_ref, hg_ref, hs_ref, prev, hcar):
        ti = pl.program_id(1)

        @pl.when(ti == 0)
        def _():
            prev[...] = jnp.zeros_like(prev)
            hcar[...] = jnp.zeros_like(hcar)

        xv = x_ref[...]
        ext = jnp.concatenate([prev[...], xv], axis=0)
        xr = _conv_pre(ext, xv, p_ref, tr)
        rowmask = _row_mask(ti, tr, (tr, 1)).astype(F32)
        a, u = _rg_pw(xr, wa_ref[0], p_ref[5:6, :], wi_ref[0], p_ref[6:7, :], p_ref[7:8, :], rowmask)
        a_cum, h_loc = _scan_fwd(a, u)
        hs = h_loc + a_cum * hcar[0:1, :]
        hs_ref[...] = hs
        hg_ref[...] = (hs * _gelu(g_ref[...])).astype(hg_ref.dtype)
        hcar[...] = jnp.broadcast_to(hs[tr - 1:tr, :], (8, LANE))
        prev[...] = xv[tr - 8:, :]

    return pl.pallas_call(
        body,
        out_shape=(_sds((t, LRU_WIDTH), BF), _sds((t, LRU_WIDTH), F32)),
        grid=(LRU_BLOCKS, t // tr),
        in_specs=[pl.BlockSpec((tr, LANE), lambda n, ti: (ti, n)),
                  pl.BlockSpec((tr, LANE), lambda n, ti: (ti, n)),
                  pl.BlockSpec((8, LANE), lambda n, ti: (0, n)),
                  pl.BlockSpec((1, LANE, LANE), lambda n, ti: (n, 0, 0)),
                  pl.BlockSpec((1, LANE, LANE), lambda n, ti: (n, 0, 0))],
        out_specs=(pl.BlockSpec((tr, LANE), lambda n, ti: (ti, n)), pl.BlockSpec((tr, LANE), lambda n, ti: (ti, n))),
        scratch_shapes=[pltpu.VMEM((8, LANE), F32), pltpu.VMEM((8, LANE), F32)],
        compiler_params=_cp(("parallel", "arbitrary")),
        name="rg_fwd",
    )(xr_pre, gate_pre, rgp, w_a, w_i)


def rg_bwd(xr_pre, gate_pre, rgp, w_a, w_i, hs, dhg):
    t = xr_pre.shape[0]
    tr = _rt(t)
    nt = t // tr
    r8 = tr // 8

    def body(x_ref, xp_ref, g_ref, p_ref, wa_ref, wi_ref, hs_ref, hp_ref, dhg_ref,
             dx_ref, dg_ref, dp_ref, dwa_ref, dwi_ref, gcar, dnext):
        ti = pl.program_id(1)
        tt = nt - 1 - ti

        @pl.when(ti == 0)
        def _():
            gcar[...] = jnp.zeros_like(gcar)
            dnext[...] = jnp.zeros_like(dnext)
            dp_ref[...] = jnp.zeros_like(dp_ref)
            dwa_ref[...] = jnp.zeros_like(dwa_ref)
            dwi_ref[...] = jnp.zeros_like(dwi_ref)

        xv = x_ref[...]
        halo = jnp.where(tt > 0, xp_ref[...], 0.0)
        ext = jnp.concatenate([halo, xv], axis=0)
        xr = _conv_pre(ext, xv, p_ref, tr)
        rowmask = _row_mask(tt, tr, (tr, 1)).astype(F32)
        fn = lambda xr_, wa_, ba_, wi_, bi_, lam_: _rg_pw(xr_, wa_, ba_, wi_, bi_, lam_, rowmask)
        (a, _), vjp = jax.vjp(fn, xr, wa_ref[0], p_ref[5:6, :], wi_ref[0], p_ref[6:7, :], p_ref[7:8, :])
        gpre = g_ref[...]
        hsv = hs_ref[...]
        dhg_v = dhg_ref[...]
        dg_ref[...] = (dhg_v * hsv * _gelu_grad(gpre)).astype(dg_ref.dtype)
        row = lax.broadcasted_iota(jnp.int32, (tr, LANE), 0)
        d = dhg_v * _gelu(gpre) + jnp.where(row == tr - 1, gcar[0:1, :], 0.0)
        b = jnp.where(row < tr - 1, pltpu.roll(a, tr - 1, 0), 0.0)
        g = _scan_bwd(b, d)
        gcar[...] = jnp.broadcast_to(a[0:1, :] * g[0:1, :], (8, LANE))
        hlast = jnp.where(tt > 0, hp_ref[7:8, :], 0.0)
        hprev = jnp.where(row == 0, hlast, pltpu.roll(hsv, 1, 0))
        dxr, dwa, dba, dwi, dbi, dlam = vjp((g * hprev, g))
        dx, sums = _conv_bwd_parts(dxr, dnext[...], xv, ext, p_ref, tr)
        dx_ref[...] = dx.astype(dx_ref.dtype)
        dnext[...] = dxr[:8, :]
        dp_ref[...] += _rows_block(sums + [dba, dbi, dlam])
        dwa_ref[0] += dwa
        dwi_ref[0] += dwi

    tile = lambda off=0: pl.BlockSpec((tr, LANE), lambda n, ti: (nt - 1 - ti, off + n))
    halo = lambda off=0: pl.BlockSpec((8, LANE), lambda n, ti: (jnp.maximum((nt - 1 - ti) * r8 - 1, 0), off + n))
    par = pl.BlockSpec((8, LANE), lambda n, ti: (0, n))
    wspec = pl.BlockSpec((1, LANE, LANE), lambda n, ti: (n, 0, 0))
    return pl.pallas_call(
        body,
        out_shape=(_sds((t, LRU_WIDTH), BF), _sds((t, LRU_WIDTH), BF), _sds((8, LRU_WIDTH), F32),
                   _sds((LRU_BLOCKS, LANE, LANE), F32), _sds((LRU_BLOCKS, LANE, LANE), F32)),
        grid=(LRU_BLOCKS, nt),
        in_specs=[tile(), halo(), tile(), par, wspec, wspec, tile(), halo(), tile()],
        out_specs=(tile(), tile(), par, wspec, wspec),
        scratch_shapes=[pltpu.VMEM((8, LANE), F32), pltpu.VMEM((8, LANE), F32)],
        compiler_params=_cp(("parallel", "arbitrary")),
        name="rg_bwd",
    )(xr_pre, xr_pre, gate_pre, rgp, w_a, w_i, hs, hs, dhg)


PACK_W = 1024
MESH_ID = pl.DeviceIdType.MESH
ANY = pl.BlockSpec(memory_space=pl.ANY)


def _my_place():
    x, y, c = lax.axis_index("x"), lax.axis_index("y"), lax.axis_index("c")
    return x, y, c


def _lin(px, py, pc):
    return 4 * px + 2 * py + pc


class Exchange:
    def __init__(self, kind, arrs):
        self.kind, self.arrs, self.k = kind, list(arrs), len(arrs)

    def out_shapes(self):
        if self.kind == "gather":
            return [_sds((N_DEV,) + a.shape, a.dtype) for a in self.arrs]
        return [_sds(a.shape, a.dtype) for a in self.arrs]

    def scratch(self):
        k = self.k
        return [pltpu.SemaphoreType.DMA((k, 7)), pltpu.SemaphoreType.DMA((k, 7)), pltpu.SemaphoreType.DMA((k,))]

    def phases(self, ins, outs, sems):
        return (self._gather if self.kind == "gather" else self._scatter)(ins, outs, *sems)

    def _gather(self, ins, outs, send_sems, recv_sems, local_sems):
        k = self.k
        x, y, c = _my_place()
        me, sibling = (x, y, c), (x, y, 1 - c)
        chips = [(1 - x, y), (x, 1 - y), (1 - x, 1 - y)]

        def copy(a, sem, block, to, from_input=False):
            slab = outs[a].at[_lin(*block)]
            return pltpu.make_async_remote_copy(
                src_ref=ins[a] if from_input else slab, dst_ref=slab,
                send_sem=send_sems.at[a, sem], recv_sem=recv_sems.at[a, sem],
                device_id=to, device_id_type=MESH_ID)

        def mine():
            return [pltpu.make_async_copy(ins[a], outs[a].at[_lin(*me)], local_sems.at[a]) for a in range(k)]

        def first():
            out = []
            for a in range(k):
                out.append(copy(a, 0, me, sibling, True))
                out += [copy(a, 1 + j, me, (*chip, c), True) for j, chip in enumerate(chips)]
            return out

        def passed():
            return [copy(a, 4 + j, (*chip, c), sibling) for j, chip in enumerate(chips) for a in range(k)]

        def start():
            for cp in mine() + first():
                cp.start()

        def middle():
            onward = passed()
            for j, chip in enumerate(chips):
                for a in range(k):
                    copy(a, 1 + j, (*chip, c), me).wait_recv()
                    onward[j * k + a].start()

        def finish():
            for a in range(k):
                copy(a, 0, sibling, me).wait_recv()
                for j, chip in enumerate(chips):
                    copy(a, 4 + j, (*chip, 1 - c), me).wait_recv()
            for cp in first() + passed():
                cp.wait_send()
            for cp in mine():
                cp.wait()

        return start, middle, finish

    def _scatter(self, ins, outs, send_sems, recv_sems, local_sems):
        k = self.k
        x, y, c = _my_place()
        me = _lin(x, y, c)
        peers = [((1 - x) if r & 4 else x, (1 - y) if r & 2 else y, (1 - c) if r & 1 else c) for r in range(1, N_DEV)]

        def copy(a, r, src_slab, dst_slab, to):
            return pltpu.make_async_remote_copy(
                src_ref=ins[a].at[src_slab], dst_ref=outs[a].at[dst_slab],
                send_sem=send_sems.at[a, r], recv_sem=recv_sems.at[a, r],
                device_id=to, device_id_type=MESH_ID)

        def mine():
            return [pltpu.make_async_copy(ins[a].at[me], outs[a].at[me], local_sems.at[a]) for a in range(k)]

        def sends():
            return [copy(a, r, _lin(*peer), me, peer) for r, peer in enumerate(peers) for a in range(k)]

        def start():
            for cp in mine() + sends():
                cp.start()

        def middle():
            pass

        def finish():
            for r, peer in enumerate(peers):
                for a in range(k):
                    copy(a, r, me, _lin(*peer), peer).wait_recv()
            for cp in sends():
                cp.wait_send()
            for cp in mine():
                cp.wait()

        return start, middle, finish

    def run(self, name):
        k = self.k

        def body(*refs):
            start, middle, finish = self.phases(refs[:k], refs[k:2 * k], refs[2 * k:])
            start()
            middle()
            finish()

        return pl.pallas_call(
            body,
            out_shape=tuple(self.out_shapes()),
            in_specs=[ANY] * k,
            out_specs=tuple(ANY for _ in range(k)),
            scratch_shapes=self.scratch(),
            name=name,
        )(*self.arrs)


def all_gather(arrs, name):
    return Exchange("gather", arrs).run(name)


def all_to_all(arrs, name):
    return Exchange("scatter", arrs).run(name)


def slab_sum(a, name):
    _, r, w = a.shape
    tr = _pick(r, (256, 128, 64, 32, 16, 8))

    def body(a_ref, o_ref):
        acc = a_ref[0].astype(F32)
        for d in range(1, N_DEV):
            acc = acc + a_ref[d].astype(F32)
        o_ref[...] = acc

    return pl.pallas_call(
        body,
        out_shape=_sds((r, w), F32),
        grid=(r // tr,),
        in_specs=[pl.BlockSpec((N_DEV, tr, w), lambda i: (0, i, 0))],
        out_specs=pl.BlockSpec((tr, w), lambda i: (i, 0)),
        compiler_params=_cp(("parallel",)),
        name=name,
    )(a)


def _adam_update(w, g, m, v):
    nm = ADAM_B1 * m + (1.0 - ADAM_B1) * g
    nv = ADAM_B2 * v + (1.0 - ADAM_B2) * (g * g)
    m_hat = nm / (1.0 - ADAM_B1 ** ADAM_STEP)
    v_hat = nv / (1.0 - ADAM_B2 ** ADAM_STEP)
    return -ADAM_LR * (m_hat / (jnp.sqrt(v_hat) + ADAM_EPS) + ADAM_WD * w), nm, nv


def adamw_blocks(w, m, v, parts, name):
    nl, r, c = w.shape
    tr = next(t for t in (256, 160, 128, 64, 32, 16) if r % t == 0 and N_DEV * t * c * 2 <= 2 * 1024 * 1024)

    def body(w_ref, m_ref, v_ref, *rest):
        part_refs, (g_ref, d_ref, nm_ref, nv_ref) = rest[:nl], rest[nl:]
        layer = pl.program_id(0)
        for idx in range(nl):
            @pl.when(layer == idx)
            def _(idx=idx):
                g = part_refs[idx][0].astype(F32)
                for dev in range(1, N_DEV):
                    g = g + part_refs[idx][dev].astype(F32)
                g_ref[...] = g
                d_ref[...], nm_ref[...], nv_ref[...] = _adam_update(w_ref[...], g, m_ref[...], v_ref[...])

    spec = pl.BlockSpec((None, tr, c), lambda l, i: (l, i, 0))
    part_spec = lambda idx: pl.BlockSpec((N_DEV, tr, c), lambda l, i: (0, jnp.where(l == idx, i, 0), 0))
    return pl.pallas_call(
        body,
        out_shape=tuple(_sds((nl, r, c), F32) for _ in range(4)),
        grid=(nl, r // tr),
        in_specs=[spec] * 3 + [part_spec(idx) for idx in range(nl)],
        out_specs=(spec,) * 4,
        compiler_params=_cp(("arbitrary", "arbitrary")),
        name=name,
    )(w, m, v, *parts)


def adamw(w, g, m, v, name):
    r, c = w.shape
    tr = _pick(r, (256, 160, 128, 64, 32, 16, 8))

    def body(w_ref, g_ref, m_ref, v_ref, d_ref, nm_ref, nv_ref):
        d_ref[...], nm_ref[...], nv_ref[...] = _adam_update(w_ref[...], g_ref[...], m_ref[...], v_ref[...])

    spec = pl.BlockSpec((tr, c), lambda i: (i, 0))
    return pl.pallas_call(
        body,
        out_shape=tuple(_sds((r, c), F32) for _ in range(3)),
        grid=(r // tr,),
        in_specs=[spec] * 4,
        out_specs=(spec, spec, spec),
        compiler_params=_cp(("parallel",)),
        name=name,
    )(w, g, m, v)


def _relu2_epi(acc):
    r = jnp.maximum(acc, 0.0)
    return r * r, r


def _drelu2_epi(acc, r):
    return (acc * (2.0 * r.astype(F32)),)


def mlp_fwd(h, g_pre, g_post, w_up, w_down, hn=None, g_next=None):
    if hn is None:
        hn = norm_fwd(h, g_pre, BF, name="mlp_norm")
    u, r = matmul(hn, w_up, "nn", (BF, BF), epi=_relu2_epi, name="mlp_up")
    d = matmul(u, w_down, "nn", name="mlp_down")
    h2, hn_next = resadd_fwd(h, d, g_post, g_next, name="mlp_res")
    return h2, (h, hn, u, r, d), hn_next


def mlp_bwd(res, dh2, g_pre, g_post, w_up, w_down, post=None, then=None):
    h, hn, u, r, d = res
    dd, dg_post = post if post is not None else norm_bwd(d, g_post, dh2, mask_pad=True, out_dtype=BF, name="mlp_post_bwd")
    dw_down = matmul(u, dd, "tn", (BF,), name="mlp_dwdown").reshape(w_down.g8.shape)
    dp = matmul(dd, w_down, "nt", (BF,), epi=_drelu2_epi, extras=(r,), name="mlp_du")
    dw_up = matmul(hn, dp, "tn", (BF,), out_blocks=True, name="mlp_dwup")
    dhn = matmul(dp, w_up, "nt", name="mlp_dhn")
    dh, dg_pre, *below = norm_bwd(h, g_pre, dhn, dres=dh2, then=then, name="mlp_pre_bwd")
    return dh, dict(mlp_pre_g=dg_pre, mlp_post_g=dg_post, w_up=dw_up, w_down=dw_down), (tuple(below) or None)


def rg_layer_fwd(h, g_pre, g_post, w_x, w_y, rgp, w_a, w_i, w_out, hn=None, g_next=None):
    if hn is None:
        hn = norm_fwd(h, g_pre, BF, name="rg_norm")
    xr = matmul(hn, w_x, "nn", name="rg_in_x")
    gp = matmul(hn, w_y, "nn", name="rg_in_y")
    hg, hs = rg_fwd(xr, gp, rgp, w_a, w_i)
    m = matmul(hg, w_out, "nn", name="rg_out")
    h2, hn_next = resadd_fwd(h, m, g_post, g_next, name="rg_res")
    return h2, (h, hn, xr, gp, hg, hs, m), hn_next


def rg_layer_bwd(res, dh2, g_pre, g_post, w_x, w_y, rgp, w_a, w_i, w_out, post=None, then=None):
    h, hn, xr, gp, hg, hs, m = res
    dm, dg_post = post if post is not None else norm_bwd(m, g_post, dh2, mask_pad=True, out_dtype=BF, name="rg_post_bwd")
    dw_out = matmul(hg, dm, "tn", name="rg_dwout")
    dhg = matmul(dm, w_out, "nt", name="rg_dhg")
    dxr, dgp, drgp, dwa, dwi = rg_bwd(xr, gp, rgp, w_a, w_i, hs, dhg)
    dw_x = matmul(hn, dxr, "tn", name="rg_dwx")
    dw_y = matmul(hn, dgp, "tn", name="rg_dwy")
    dhn = matmul([dxr, dgp], [w_x, w_y], "nt", name="rg_dhn")
    dh, dg_pre, *below = norm_bwd(h, g_pre, dhn, dres=dh2, then=then, name="rg_pre_bwd")
    grads = dict(mix_pre_g=dg_pre, mix_post_g=dg_post, rg_w_x=dw_x, rg_w_y=dw_y,
                 rg_conv_w=drgp[0:4], rg_conv_b=drgp[4], rg_b_a=drgp[5], rg_b_i=drgp[6], rg_lambda=drgp[7],
                 rg_w_a=dwa, rg_w_i=dwi, rg_w_out=dw_out)
    return dh, grads, (tuple(below) or None)


def sm_layer_fwd(h, g_pre, g_post, w_in_p, convp, dt_bias, a_log, dskip, ssd_g, q_g, w_q_p, kv_g, w_kv_p, w_out, tabs,
                 carry=None, on_carried=None, hn=None, g_next=None):
    if hn is None:
        hn = norm_fwd(h, g_pre, BF, name="sm_norm")
    proj = matmul(hn, w_in_p, "nn", name="sm_in")
    xbc_c = conv_silu_fwd(proj, OFF_XBC, SSD_CONV_CH, convp, name="ssd_conv")
    y, hst = ssd_fwd(xbc_c, proj, dt_bias, a_log, dskip)
    y_ssd = gated_norm_fwd(y, proj, ssd_g)
    cqn = norm_fwd(proj, q_g, BF, col_blk=OFF_CQ // MLA_Q_RANK, width=MLA_Q_RANK, name="q_norm")
    q_raw = matmul(cqn, w_q_p, "nn", name="q_up")
    ckvn = norm_fwd(proj, kv_g, BF, col_blk=OFF_CKV // MLA_KV_RANK, width=MLA_KV_RANK, name="kv_norm")
    kv_raw = matmul(ckvn, w_kv_p, "nn", name="kv_up")
    q_cat, k_cat, v = rope_fwd(q_raw, kv_raw, proj, tabs)
    o, lse, carried = attn_fwd(q_cat, k_cat, v, carry)
    if on_carried is not None:
        on_carried(carried)
    w_out = w_out()
    half = w_out.shape[0] // 2
    m = matmul([y_ssd, o], [KBlock(w_out, half, 0), KBlock(w_out, half, 1)], "nn", name="sm_out")
    res = (h, hn, proj, xbc_c, y, hst, cqn, ckvn, q_cat, k_cat, v, o, lse, y_ssd, m)
    h2, hn_next = resadd_fwd(h, m, g_post, g_next, name="sm_res")
    return h2, res, hn_next


def sm_layer_bwd(res, dh2, g_pre, g_post, w_in_p, convp, dt_bias, a_log, dskip, ssd_g, q_g, w_q_p, kv_g, w_kv_p, w_out, tabs,
                 carry=None, post=None, then=None):
    h, hn, proj, xbc_c, y, hst, cqn, ckvn, q_cat, k_cat, v, o, lse, y_ssd, m = res
    w_out = w_out()
    dm, dg_post = post if post is not None else norm_bwd(m, g_post, dh2, mask_pad=True, out_dtype=BF, name="sm_post_bwd")
    dw_out = jnp.concatenate([matmul(y_ssd, dm, "tn", name="sm_dwout_ssd"), matmul(o, dm, "tn", name="sm_dwout_att")], axis=0)
    dyab = matmul(dm, w_out, "nt", name="sm_dyab")
    dq_cat, dk_cat, dv, carried = attn_bwd(q_cat, k_cat, v, o, lse, dyab, carry(dw_out) if carry is not None else None)
    dq_raw, dkr = rope_bwd(dq_cat, dk_cat, tabs)
    kw = MLA_HEADS * LANE
    dw_kv_p = jnp.concatenate([matmul(ckvn, dk_cat, "tn", name="kv_dw_k"), matmul(ckvn, dv, "tn", name="kv_dw_v")], axis=1)
    dckvn = matmul([dk_cat, dv], [KBlock(w_kv_p, kw, 0), KBlock(w_kv_p, kw // 2, 2)], "nt", name="kv_dx")
    dckv, dg_kv = norm_bwd(proj, kv_g, dckvn, out_dtype=BF, col_blk=OFF_CKV // MLA_KV_RANK, width=MLA_KV_RANK,
                           name="kv_norm_bwd")
    dw_q_p = matmul(cqn, dq_raw, "tn", name="q_dw")
    dcqn = matmul(dq_raw, w_q_p, "nt", name="q_dx")
    dcq, dg_q = norm_bwd(proj, q_g, dcqn, out_dtype=BF, col_blk=OFF_CQ // MLA_Q_RANK, width=MLA_Q_RANK, name="q_norm_bwd")
    dy, dz, dg_ssd = gated_norm_bwd(y, proj, ssd_g, dyab)
    dxbc_c, ddt, dpar = ssd_bwd(xbc_c, proj, dt_bias, a_log, dskip, hst, dy)
    dxbc, dconvp = conv_silu_bwd(proj, OFF_XBC, SSD_CONV_CH, convp, dxbc_c, name="ssd_conv_bwd")
    pieces = [dz, dxbc, dckv, ddt, dkr, dcq]
    dw_in_p = jnp.concatenate([matmul(hn, pc, "tn", (BF,), name="sm_dwin_%d" % i) for i, pc in enumerate(pieces)], axis=1)
    third = SSD_CONV_CH // 3
    a_terms = [dz] + [KBlock(dxbc, third, i) for i in range(3)] + [dckv, ddt, dkr, dcq]
    b_terms = ([KBlock(w_in_p, SSD_D_INNER, 0)] + [KBlock(w_in_p, third, OFF_XBC // third + i) for i in range(3)]
               + [KBlock(w_in_p, MLA_KV_RANK, OFF_CKV // MLA_KV_RANK), KBlock(w_in_p, LANE, OFF_DT // LANE),
                  KBlock(w_in_p, LANE, OFF_KR // LANE), KBlock(w_in_p, MLA_Q_RANK, OFF_CQ // MLA_Q_RANK)])
    dhn = matmul(a_terms, b_terms, "nt", name="sm_dhn")
    dh, dg_pre, *below = norm_bwd(h, g_pre, dhn, dres=dh2, then=then, name="sm_pre_bwd")
    grads = dict(mix_pre_g=dg_pre, mix_post_g=dg_post, w_in=w_in_cols_to_blocks(dw_in_p), ssd_conv_w=dconvp[0:4],
                 ssd_conv_b=dconvp[4], ssd_dt_bias=dpar[0, :SSD_HEADS], ssd_a_log=dpar[1, :SSD_HEADS],
                 ssd_d=dpar[2, :SSD_HEADS], ssd_norm_g=dg_ssd, mla_q_norm_g=dg_q, mla_w_q_up=_unpack_w_q(dw_q_p),
                 mla_kv_norm_g=dg_kv, mla_w_kv_up=_unpack_w_kv(dw_kv_p), w_out_ab=dw_out)
    return dh, grads, carried, (tuple(below) or None)


W_IN_COLS = 3248
W_IN_SHARD = W_IN_COLS // N_DEV
W_IN_WIRE = 512


def _w_in_tables():
    src = np.full((IN_W,), -1, np.int64)
    src[0:2560] = np.arange(2560)
    src[OFF_CKV:OFF_CKV + 256] = 2960 + np.arange(256)
    src[OFF_DT:OFF_DT + 16] = 2560 + np.arange(16)
    src[OFF_KR + 64:OFF_KR + 96] = 3216 + np.arange(32)
    src[OFF_CQ:OFF_CQ + 384] = 2576 + np.arange(384)
    dev = np.where(src >= 0, src // W_IN_SHARD, -1).astype(np.int32).reshape(1, IN_W)
    col = np.where(src >= 0, src % W_IN_SHARD, 0).astype(np.int32).reshape(1, IN_W)
    return dev, col


W_IN_TILE = 384


def _w_in_devices_of_tile(dev):
    return [sorted(set(dev[0, t * W_IN_TILE:(t + 1) * W_IN_TILE].tolist()) - {-1}) for t in range(IN_W // W_IN_TILE)]


def _any_of(index, values):
    cond = index == values[0]
    for v in values[1:]:
        cond = cond | (index == v)
    return cond


def w_in_blocks_to_cols(g8):
    _, k, wp = g8.shape
    tn = W_IN_TILE
    dev, col = _w_in_tables()
    holders = _w_in_devices_of_tile(dev)

    def body(g_ref, dev_ref, col_ref, o_ref):
        i = pl.program_id(0)
        row = lax.broadcasted_iota(jnp.int32, (wp, tn), 0)
        o_ref[...] = jnp.zeros_like(o_ref)
        for j in range(N_DEV):
            tiles = [t for t, devs in enumerate(holders) if j in devs]
            if tiles:
                @pl.when(_any_of(i, tiles))
                def _(j=j):
                    sel = ((row == col_ref[...]) & (dev_ref[...] == j)).astype(BF)
                    o_ref[...] += jnp.dot(g_ref[j], sel, preferred_element_type=F32).astype(o_ref.dtype)

    dev, col = jnp.asarray(dev), jnp.asarray(col)
    return pl.pallas_call(
        body,
        out_shape=_sds((k, IN_W), BF),
        grid=(IN_W // tn,),
        in_specs=[pl.BlockSpec((N_DEV, k, wp), lambda i: (0, 0, 0)), pl.BlockSpec((1, tn), lambda i: (0, i)),
                  pl.BlockSpec((1, tn), lambda i: (0, i))],
        out_specs=pl.BlockSpec((k, tn), lambda i: (0, i)),
        compiler_params=_cp(("parallel",)),
        name="w_in_cols",
    )(g8, dev, col)


def w_in_cols_to_blocks(dw):
    k = dw.shape[0]
    tn = W_IN_TILE
    dev, col = _w_in_tables()
    holders = _w_in_devices_of_tile(dev)

    def body(dw_ref, dev_ref, col_ref, o_ref):
        j = pl.program_id(0)
        row = lax.broadcasted_iota(jnp.int32, (W_IN_WIRE, tn), 0)
        o_ref[...] = jnp.zeros_like(o_ref)
        for t, devs in enumerate(holders):
            if devs:
                @pl.when(_any_of(j, devs))
                def _(t=t):
                    cols = slice(t * tn, (t + 1) * tn)
                    sel = ((row == col_ref[:, cols]) & (dev_ref[:, cols] == j)).astype(BF)
                    o_ref[0] += lax.dot_general(dw_ref[:, cols], sel, NT_DIMS,
                                                preferred_element_type=F32).astype(o_ref.dtype)

    dev, col = jnp.asarray(dev), jnp.asarray(col)
    return pl.pallas_call(
        body,
        out_shape=_sds((N_DEV, k, W_IN_WIRE), BF),
        grid=(N_DEV,),
        in_specs=[pl.BlockSpec((k, IN_W), lambda j: (0, 0)), pl.BlockSpec((1, IN_W), lambda j: (0, 0)),
                  pl.BlockSpec((1, IN_W), lambda j: (0, 0))],
        out_specs=pl.BlockSpec((1, k, W_IN_WIRE), lambda j: (j, 0, 0)),
        compiler_params=_cp(("parallel",)),
        name="w_in_blocks",
    )(dw, dev, col)


def _pack_w_q(w):
    w3 = w.reshape(w.shape[0], MLA_HEADS, MLA_NOPE + MLA_ROPE)
    return jnp.pad(w3, ((0, 0), (0, 0), (0, LANE - MLA_NOPE - MLA_ROPE))).reshape(w.shape[0], MLA_HEADS * LANE)


def _unpack_w_q(p):
    return p.reshape(p.shape[0], MLA_HEADS, LANE)[:, :, :MLA_NOPE + MLA_ROPE].reshape(p.shape[0], -1)


def _pack_w_kv(w):
    w3 = w.reshape(w.shape[0], MLA_HEADS, MLA_NOPE + MLA_V)
    k = jnp.pad(w3[:, :, :MLA_NOPE], ((0, 0), (0, 0), (0, LANE - MLA_NOPE))).reshape(w.shape[0], MLA_HEADS * LANE)
    return jnp.concatenate([k, w3[:, :, MLA_NOPE:].reshape(w.shape[0], MLA_HEADS * MLA_V)], axis=1)


def _unpack_w_kv(p):
    k = p[:, :MLA_HEADS * LANE].reshape(p.shape[0], MLA_HEADS, LANE)[:, :, :MLA_NOPE]
    v = p[:, MLA_HEADS * LANE:].reshape(p.shape[0], MLA_HEADS, MLA_V)
    return jnp.concatenate([k, v], axis=2).reshape(p.shape[0], -1)


def _rows8(rows, width):
    a = jnp.concatenate([r.reshape(-1, width) for r in rows], axis=0)
    return jnp.pad(a, ((0, 8 - a.shape[0]), (0, 0)))


SLAB_ROWS = 16


def _to_slab(flat_list, lead=()):
    cat = jnp.concatenate(flat_list, axis=-1)
    n = cat.shape[-1]
    unit = SLAB_ROWS * PACK_W
    total = -(-n // unit) * unit
    cat = jnp.pad(cat, [(0, 0)] * len(lead) + [(0, total - n)])
    return cat.reshape(lead + (total // PACK_W, PACK_W))


def _from_flat(flat, shapes):
    out, off = [], 0
    for s in shapes:
        n = int(np.prod(s))
        out.append(flat[off:off + n].reshape(s))
        off += n
    return out


def _gathered_full(g8, axis):
    moved = jnp.moveaxis(g8, 0, axis)
    shp = moved.shape
    return moved.reshape(shp[:axis] + (shp[axis] * shp[axis + 1],) + shp[axis + 2:])


def _per_device(full, axis):
    shp = full.shape
    split = full.reshape(shp[:axis] + (N_DEV, shp[axis] // N_DEV) + shp[axis + 1:])
    return jnp.moveaxis(split, axis, 0)


ARG_NAMES = ['x', 'meta_tokens', 'mix_pre_g', 'mix_post_g', 'mlp_pre_g', 'mlp_post_g', 'w_up', 'w_down', 'w_in',
             'ssd_conv_w', 'ssd_conv_b', 'ssd_dt_bias', 'ssd_a_log', 'ssd_d', 'ssd_norm_g', 'mla_q_norm_g',
             'mla_w_q_up', 'mla_kv_norm_g', 'mla_w_kv_up', 'w_out_ab', 'rg_w_x', 'rg_w_y', 'rg_conv_w', 'rg_conv_b',
             'rg_w_a', 'rg_b_a', 'rg_w_i', 'rg_b_i', 'rg_lambda', 'rg_w_out']
WEIGHTS = ARG_NAMES[1:]
BIG = {'w_up': 2, 'w_down': 1, 'w_in': 2, 'mla_w_q_up': 2, 'mla_w_kv_up': 2, 'w_out_ab': 1, 'rg_w_x': 2,
       'rg_w_y': 2, 'rg_w_out': 1}
SMALL = {'meta_tokens': 1, 'ssd_conv_w': 2, 'rg_conv_w': 2, 'rg_conv_b': 1, 'rg_b_a': 1, 'rg_b_i': 1, 'rg_lambda': 1}
REPL = [n for n in WEIGHTS if n not in BIG and n not in SMALL]
REPL_MEDIUM = ['rg_w_a', 'rg_w_i']
REPL_TINY = [n for n in REPL if n not in REPL_MEDIUM]


def _piece_axes():
    axes = {}
    for n, ax in BIG.items():
        for i in range(DEPTH if n in ('w_up', 'w_down') else DEPTH // 2):
            axes[(n, i)] = ax - 1
    return axes


PIECE_AXIS = _piece_axes()
AS_BLOCKS = ('w_up', 'w_down')
_RG = lambda i: [(n, i) for n in ('rg_w_x', 'rg_w_y', 'rg_w_out')]
_MLP = lambda l: [('w_up', l), ('w_down', l)]
_SM_IN = lambda i: [(n, i) for n in ('w_in', 'mla_w_q_up', 'mla_w_kv_up')]
GATHER_FIRST = _SM_IN(0)
GATHER_AT = {0: [('w_out_ab', 0)] + _MLP(0) + _RG(0) + _MLP(1) + _SM_IN(1), 2: [('w_out_ab', 1)] + _MLP(2) + _RG(1) + _MLP(3)}
SCATTER_AT = {2: _MLP(3) + _RG(1) + _MLP(2) + [('w_out_ab', 1)],
              0: _SM_IN(1) + _MLP(1) + _RG(0) + _MLP(0) + [('w_out_ab', 0)]}
SCATTER_LAST = _SM_IN(0)


def _wire_block(p, key):
    n, i = key
    blk = p[n][i]
    if n == 'w_in':
        blk = jnp.pad(blk, ((0, 0), (0, W_IN_WIRE - blk.shape[1])))
    return blk


def _step(p, moments):
    assert DEPTH == 4
    full = {n: [None] * p[n].shape[0] for n in BIG}
    full['w_in_g'] = [None] * p['w_in'].shape[0]

    def weight_blocks(group):
        return [_wire_block(p, k).astype(BF) for k in group]

    def take_weights(group, gathered):
        for (n, i), piece in zip(group, gathered):
            if n == 'w_in':
                full['w_in_g'][i] = piece
            elif n in AS_BLOCKS:
                full[n][i] = DevBlocks(piece, PIECE_AXIS[(n, i)])
            else:
                full[n][i] = _gathered_full(piece, PIECE_AXIS[(n, i)])

    def grad_blocks(group, gw):
        return [gw[k] if k[0] in AS_BLOCKS or k[0] == 'w_in' else _per_device(gw[k], PIECE_AXIS[k]).astype(BF)
                for k in group]

    parts = {}

    small_slab = _to_slab([p[n].reshape(-1) for n in SMALL])
    *first, small8 = all_gather(weight_blocks(GATHER_FIRST) + [small_slab], name="gather_first")
    take_weights(GATHER_FIRST, first)
    for n, piece in zip(SMALL, _from_flat_rows(small8, [p[n].shape for n in SMALL])):
        full[n] = _gathered_full(piece, SMALL[n])
    for n in REPL:
        full[n] = p[n]
    loss_local, grad_x, gw, gsmall_full, carried = _local_step(
        full, p['x'][0], p['loss_target'][0],
        fwd_carry=lambda layer: Exchange("gather", weight_blocks(GATHER_AT[layer])),
        on_fwd_carried=lambda layer, got: take_weights(GATHER_AT[layer], got),
        bwd_carry=lambda layer, gw_now, others: Exchange(
            "scatter", grad_blocks(SCATTER_AT[layer], gw_now)
            + ([jnp.stack(others[n], axis=0).reshape(N_DEV, -1, LANE) for n in REPL_MEDIUM] if layer == 0 else [])))

    for layer, group in SCATTER_AT.items():
        parts.update(zip(group, carried[layer]))
    rep_flat = jnp.concatenate([gsmall_full[n].reshape(-1) for n in REPL_TINY])
    rep_n = rep_flat.shape[0]
    rep_chunk = -(-rep_n // (N_DEV * PACK_W * 8)) * PACK_W * 8
    rep8 = jnp.pad(rep_flat, (0, N_DEV * rep_chunk - rep_n)).reshape(N_DEV, rep_chunk)
    gsmall = _to_slab([_per_device(gsmall_full[n], SMALL[n]).reshape(N_DEV, -1) for n in SMALL] + [rep8], lead=(N_DEV,))
    received = all_to_all(grad_blocks(SCATTER_LAST, gw) + [gsmall], name="scatter_last")
    n_last = len(SCATTER_LAST)
    parts.update(zip(SCATTER_LAST, received[:n_last]))
    ssmall = slab_sum(received[n_last], name="sum_small").reshape(-1)
    medium_mine = [slab_sum(r8, name="sum_" + n) for n, r8 in zip(REPL_MEDIUM, carried[0][len(SCATTER_AT[0]):])]
    g_loc = {'w_in': jnp.stack([slab_sum(parts[('w_in', i)], name="sum_w_in_%d" % i)[:, :W_IN_SHARD]
                                for i in range(p['w_in'].shape[0])], axis=0)}
    small_n = sum(int(np.prod(p[n].shape)) for n in SMALL)
    g_loc.update(zip(SMALL, _from_flat(ssmall, [p[n].shape for n in SMALL])))
    rep_mine = ssmall[small_n:small_n + rep_chunk].reshape(-1, PACK_W)
    rep_all, *medium_all = all_gather([rep_mine] + medium_mine, name="gather_replicated")
    g_loc.update(zip(REPL_TINY, _from_flat(rep_all.reshape(-1), [p[n].shape for n in REPL_TINY])))
    g_loc.update({n: g.reshape(p[n].shape) for n, g in zip(REPL_MEDIUM, medium_all)})

    out = {'loss': lax.psum(loss_local, ("x", "y", "c")), 'grad_x': grad_x[None]}
    small_names = list(SMALL) + REPL_TINY
    for n in list(BIG) + REPL_MEDIUM:
        shp = p[n].shape
        if n == 'w_in' or n in REPL_MEDIUM:
            v2 = lambda a: a.reshape(-1, shp[-1])
            d, nm, nv = adamw(v2(p[n]), v2(g_loc[n]), v2(moments['m_' + n]), v2(moments['v_' + n]), name="adamw_" + n)
            d, nm, nv = d.reshape(shp), nm.reshape(shp), nv.reshape(shp)
        else:
            g_loc[n], d, nm, nv = adamw_blocks(p[n], moments['m_' + n], moments['v_' + n],
                                               [parts[(n, i)] for i in range(shp[0])], name="adamw_" + n)
        out['delta_' + n], out['new_m_' + n], out['new_v_' + n] = d, nm, nv
    slab = lambda src: _to_slab([src(n).reshape(-1) for n in small_names])
    d, nm, nv = adamw(slab(lambda n: p[n]), slab(lambda n: g_loc[n]), slab(lambda n: moments['m_' + n]),
                      slab(lambda n: moments['v_' + n]), name="adamw_small")
    shapes = [p[n].shape for n in small_names]
    for key, flat in (('delta_', d), ('new_m_', nm), ('new_v_', nv)):
        for n, a in zip(small_names, _from_flat(flat.reshape(-1), shapes)):
            out[key + n] = a
    for n in WEIGHTS:
        out['grad_' + n] = g_loc[n]
    return out


def _local_step(full, x, target_rows, fwd_carry=None, on_fwd_carried=None, bwd_carry=None):
    t = PAD + N_META + x.shape[0]
    h = jnp.concatenate([jnp.zeros((PAD, D_MODEL), F32), full['meta_tokens'], x], axis=0)
    target = jnp.concatenate([jnp.zeros((PAD + N_META, D_MODEL), F32), target_rows], axis=0)
    tabs = rope_tables(t)

    def layer_args(layer):
        i = layer // 2
        if layer % 2 == 0:
            convp = _rows8([full['ssd_conv_w'][i], full['ssd_conv_b'][i]], SSD_CONV_CH)
            return (full['mix_pre_g'][layer], full['mix_post_g'][layer], w_in_blocks_to_cols(full['w_in_g'][i]), convp,
                    full['ssd_dt_bias'][i], full['ssd_a_log'][i], full['ssd_d'][i], full['ssd_norm_g'][i],
                    full['mla_q_norm_g'][i], _pack_w_q(full['mla_w_q_up'][i]), full['mla_kv_norm_g'][i],
                    _pack_w_kv(full['mla_w_kv_up'][i]), lambda: full['w_out_ab'][i], tabs)
        rgp = _rows8([full['rg_conv_w'][i], full['rg_conv_b'][i], full['rg_b_a'][i], full['rg_b_i'][i],
                      full['rg_lambda'][i]], LRU_WIDTH)
        return (full['mix_pre_g'][layer], full['mix_post_g'][layer], full['rg_w_x'][i], full['rg_w_y'][i], rgp,
                full['rg_w_a'][i], full['rg_w_i'][i], full['rg_w_out'][i])

    def mlp_args(layer):
        return (full['mlp_pre_g'][layer], full['mlp_post_g'][layer], full['w_up'][layer], full['w_down'][layer])

    saved = []
    hn = None
    for layer in range(DEPTH):
        la = layer_args(layer)
        to_mlp = dict(hn=hn, g_next=full['mlp_pre_g'][layer])
        if layer % 2 == 0:
            if fwd_carry is not None:
                h, res_mix, hn = sm_layer_fwd(h, *la, carry=fwd_carry(layer),
                                              on_carried=lambda got, layer=layer: on_fwd_carried(layer, got), **to_mlp)
            else:
                h, res_mix, hn = sm_layer_fwd(h, *la, **to_mlp)
        else:
            h, res_mix, hn = rg_layer_fwd(h, *la, **to_mlp)
        ma = mlp_args(layer)
        h, res_mlp, hn = mlp_fwd(h, *ma, hn=hn, g_next=full['mix_pre_g'][layer + 1] if layer + 1 < DEPTH else None)
        saved.append((la, ma, res_mix, res_mlp))
    loss_local, dh = loss_fwd_bwd(h, target)

    others = {n: [None] * len(full[n]) for n in WEIGHTS if n not in BIG and n != 'meta_tokens'}
    gw, carried = {}, {}
    post = None
    for layer in reversed(range(DEPTH)):
        la, ma, res_mix, res_mlp = saved[layer]
        dh, gm, post = mlp_bwd(res_mlp, dh, *ma, post=post, then=(res_mix[-1], la[1]))
        below = (saved[layer - 1][3][-1], saved[layer - 1][1][1]) if layer > 0 else None
        if layer % 2 == 0:
            for n in ('w_up', 'w_down'):
                gw[(n, layer)] = gm[n]
            carry = None
            if bwd_carry is not None:
                carry = lambda dw_out, layer=layer: bwd_carry(layer, {**gw, ('w_out_ab', layer // 2): dw_out}, others)
            dh, gx, carried[layer], post = sm_layer_bwd(res_mix, dh, *la, carry=carry, post=post, then=below)
        else:
            dh, gx, post = rg_layer_bwd(res_mix, dh, *la, post=post, then=below)
        for n, g in list(gm.items()) + list(gx.items()):
            i = layer if n in ('mix_pre_g', 'mix_post_g', 'mlp_pre_g', 'mlp_post_g', 'w_up', 'w_down') else layer // 2
            if n in BIG:
                gw[(n, i)] = g
            else:
                others[n][i] = g
    gothers = {n: jnp.stack(v, axis=0) for n, v in others.items()}
    gothers['meta_tokens'] = dh[PAD:PAD + N_META]
    return loss_local, dh[PAD + N_META:], gw, gothers, carried


def _from_flat_rows(g8, shapes):
    flat = g8.reshape(N_DEV, -1)
    out, off = [], 0
    for s in shapes:
        n = int(np.prod(s))
        out.append(flat[:, off:off + n].reshape((N_DEV,) + tuple(s)))
        off += n
    return out


def kernel(x, meta_tokens, mix_pre_g, mix_post_g, mlp_pre_g, mlp_post_g, w_up, w_down, w_in, ssd_conv_w, ssd_conv_b, ssd_dt_bias, ssd_a_log, ssd_d, ssd_norm_g, mla_q_norm_g, mla_w_q_up, mla_kv_norm_g, mla_w_kv_up, w_out_ab, rg_w_x, rg_w_y, rg_conv_w, rg_conv_b, rg_w_a, rg_b_a, rg_w_i, rg_b_i, rg_lambda, rg_w_out, loss_target, m_meta_tokens, m_mix_pre_g, m_mix_post_g, m_mlp_pre_g, m_mlp_post_g, m_w_up, m_w_down, m_w_in, m_ssd_conv_w, m_ssd_conv_b, m_ssd_dt_bias, m_ssd_a_log, m_ssd_d, m_ssd_norm_g, m_mla_q_norm_g, m_mla_w_q_up, m_mla_kv_norm_g, m_mla_w_kv_up, m_w_out_ab, m_rg_w_x, m_rg_w_y, m_rg_conv_w, m_rg_conv_b, m_rg_w_a, m_rg_b_a, m_rg_w_i, m_rg_b_i, m_rg_lambda, m_rg_w_out, v_meta_tokens, v_mix_pre_g, v_mix_post_g, v_mlp_pre_g, v_mlp_post_g, v_w_up, v_w_down, v_w_in, v_ssd_conv_w, v_ssd_conv_b, v_ssd_dt_bias, v_ssd_a_log, v_ssd_d, v_ssd_norm_g, v_mla_q_norm_g, v_mla_w_q_up, v_mla_kv_norm_g, v_mla_w_kv_up, v_w_out_ab, v_rg_w_x, v_rg_w_y, v_rg_conv_w, v_rg_conv_b, v_rg_w_a, v_rg_b_a, v_rg_w_i, v_rg_b_i, v_rg_lambda, v_rg_w_out):
    args = (x, meta_tokens, mix_pre_g, mix_post_g, mlp_pre_g, mlp_post_g, w_up, w_down, w_in, ssd_conv_w, ssd_conv_b, ssd_dt_bias, ssd_a_log, ssd_d, ssd_norm_g, mla_q_norm_g, mla_w_q_up, mla_kv_norm_g, mla_w_kv_up, w_out_ab, rg_w_x, rg_w_y, rg_conv_w, rg_conv_b, rg_w_a, rg_b_a, rg_w_i, rg_b_i, rg_lambda, rg_w_out, loss_target, m_meta_tokens, m_mix_pre_g, m_mix_post_g, m_mlp_pre_g, m_mlp_post_g, m_w_up, m_w_down, m_w_in, m_ssd_conv_w, m_ssd_conv_b, m_ssd_dt_bias, m_ssd_a_log, m_ssd_d, m_ssd_norm_g, m_mla_q_norm_g, m_mla_w_q_up, m_mla_kv_norm_g, m_mla_w_kv_up, m_w_out_ab, m_rg_w_x, m_rg_w_y, m_rg_conv_w, m_rg_conv_b, m_rg_w_a, m_rg_b_a, m_rg_w_i, m_rg_b_i, m_rg_lambda, m_rg_w_out, v_meta_tokens, v_mix_pre_g, v_mix_post_g, v_mlp_pre_g, v_mlp_post_g, v_w_up, v_w_down, v_w_in, v_ssd_conv_w, v_ssd_conv_b, v_ssd_dt_bias, v_ssd_a_log, v_ssd_d, v_ssd_norm_g, v_mla_q_norm_g, v_mla_w_q_up, v_mla_kv_norm_g, v_mla_w_kv_up, v_w_out_ab, v_rg_w_x, v_rg_w_y, v_rg_conv_w, v_rg_conv_b, v_rg_w_a, v_rg_b_a, v_rg_w_i, v_rg_b_i, v_rg_lambda, v_rg_w_out,)
    n_w = len(ARG_NAMES)
    p = dict(zip(ARG_NAMES, args[:n_w]))
    p['loss_target'] = args[n_w]
    moments = {}
    for i, n in enumerate(WEIGHTS):
        moments['m_' + n] = args[n_w + 1 + i]
        moments['v_' + n] = args[n_w + 1 + len(WEIGHTS) + i]
    out = _step(p, moments)
    res = [out['loss'], out['grad_x']]
    for prefix in ('grad_', 'delta_', 'new_m_', 'new_v_'):
        res += [out[prefix + n] for n in WEIGHTS]
    return tuple(res)
```

```python
import math

import numpy as np
import jax
import jax.numpy as jnp
from jax import lax
from jax.experimental import pallas as pl
from jax.experimental.pallas import tpu as pltpu

F32 = jnp.float32
BF = jnp.bfloat16
HI = lax.Precision.HIGHEST

D_MODEL = 1024
DEPTH = 4
N_META = 16
CHUNK = 128
PAD = CHUNK - N_META
EPS = 1e-6
SSD_HEADS = 16
SSD_HEAD_DIM = 64
SSD_D_INNER = 1024
SSD_STATE = 128
SSD_CONV_CH = 1536
MLA_HEADS = 16
MLA_NOPE = 64
MLA_ROPE = 32
MLA_V = 64
MLA_Q_RANK = 384
MLA_KV_RANK = 256
ROPE_BASE = 10000.0
LRU_WIDTH = 1280
LRU_BLOCKS = 10
LRU_C = 8.0
D_FF = 4096
N_DEV = 8
LANE = 128
IN_W = 3456
OFF_Z, OFF_XBC, OFF_CKV, OFF_DT, OFF_KR, OFF_CQ = 0, 1024, 2560, 2816, 2944, 3072

ADAM_LR = 0.001
ADAM_B1 = 0.9
ADAM_B2 = 0.999
ADAM_EPS = 1e-08
ADAM_WD = 0.01
ADAM_STEP = 10

VMEM_LIMIT = 56 * 1024 * 1024
NEG = -1e30


def _pick(n, cands):
    for c in cands:
        if n % c == 0:
            return c
    return n


def _cp(sem=None):
    return pltpu.CompilerParams(dimension_semantics=sem, vmem_limit_bytes=VMEM_LIMIT)


def _sds(shape, dtype):
    return jax.ShapeDtypeStruct(tuple(shape), dtype)


def _silu(x):
    return x * jax.nn.sigmoid(x)


def _softplus(x):
    return jnp.maximum(x, 0.0) + jnp.log(1.0 + jnp.exp(-jnp.abs(x)))


def _gelu(x):
    c = math.sqrt(2.0 / math.pi)
    return 0.5 * x * (1.0 + jnp.tanh(c * (x + 0.044715 * (x * x * x))))


def _row_mask(i, tr, shape, first_valid=PAD):
    row = i * tr + lax.broadcasted_iota(jnp.int32, shape, 0)
    return row >= first_valid


class KBlock:
    def __init__(self, arr, width, blk):
        self.arr, self.width, self.blk = arr, width, blk


class DevBlocks:
    def __init__(self, g8, axis):
        self.g8, self.axis = g8, axis
        _, r, c = g8.shape
        self.shape = (N_DEV * r, c) if axis == 0 else (r, N_DEV * c)


NN_DIMS = (((1,), (0,)), ((), ()))
MM_TALL_K = 1536
MM_WHOLE_K, MM_WHOLE_M = 1024, 4224


def matmul(a, b, mode, out_dtypes=(F32,), epi=None, extras=(), name="mm", tm=None, tn=None, out_blocks=False):
    a_terms = a if isinstance(a, (list, tuple)) else [a]
    b_terms = b if isinstance(b, (list, tuple)) else [b]
    assert len(a_terms) == len(b_terms) and (mode != "tn" or len(a_terms) == 1)
    arr_of = lambda t: t.arr if isinstance(t, KBlock) else t
    if mode == "tn":
        m, n = a_terms[0].shape[1], b_terms[0].shape[1]
    else:
        m = arr_of(a_terms[0]).shape[0]
        b0 = b_terms[0]
        n = (b0.shape if isinstance(b0, DevBlocks) else arr_of(b0).shape)[1 if mode == "nn" else 0]
    if mode == "tn":
        tm = _pick(m, (1024, 512, 384, 256, 128))
    else:
        k_all = sum(t.width if isinstance(t, KBlock) else t.shape[1] for t in a_terms)
        tall = (2112,) if k_all <= MM_TALL_K else ()
        if k_all <= MM_WHOLE_K and m <= MM_WHOLE_M and n % 256 == 0 and tm is None and tn is None:
            tm, tn = m, 256
        tm = tm or _pick(m, tall + (1056, 1024, 768, 640, 512, 384, 256, 128))
    tn = tn or _pick(n, (512, 640, 384, 256, 128))
    dims = {"nn": NN_DIMS, "nt": NT_DIMS, "tn": TN_DIMS}[mode]

    in_specs, args, plan = [], [], []
    for ta, tb in zip(a_terms, b_terms):
        if mode == "tn":
            k = ta.shape[0]
            in_specs += [pl.BlockSpec((k, tm), lambda i, j: (0, i)), pl.BlockSpec((k, tn), lambda i, j: (0, j))]
            args += [ta, tb]
            plan.append(None)
            continue
        if isinstance(ta, KBlock):
            kw, ka = ta.width, ta.blk
            in_specs.append(pl.BlockSpec((tm, kw), lambda i, j, ka=ka: (i, ka)))
        else:
            kw = ta.shape[1]
            in_specs.append(pl.BlockSpec((tm, kw), lambda i, j: (i, 0)))
        args.append(arr_of(ta))
        if isinstance(tb, DevBlocks):
            _, r, c = tb.g8.shape
            split_k = tb.axis == (0 if mode == "nn" else 1)
            if split_k:
                kd = r if mode == "nn" else c
                assert kw == N_DEV * kd
                blk = (N_DEV, kd, tn) if mode == "nn" else (N_DEV, tn, kd)
                in_specs.append(pl.BlockSpec(blk, (lambda i, j: (0, 0, j)) if mode == "nn" else (lambda i, j: (0, j, 0))))
                plan.append(kd)
            else:
                per = (c if mode == "nn" else r) // tn
                blk = (None, kw, tn) if mode == "nn" else (None, tn, kw)
                in_specs.append(pl.BlockSpec(blk, (lambda i, j, per=per: (j // per, 0, j % per)) if mode == "nn"
                                             else (lambda i, j, per=per: (j // per, j % per, 0))))
                plan.append(None)
            args.append(tb.g8)
        else:
            kb = tb.blk if isinstance(tb, KBlock) else 0
            assert (tb.width if isinstance(tb, KBlock) else tb.shape[0 if mode == "nn" else 1]) == kw
            in_specs.append(pl.BlockSpec((kw, tn), lambda i, j, kb=kb: (kb, j)) if mode == "nn"
                            else pl.BlockSpec((tn, kw), lambda i, j, kb=kb: (j, kb)))
            args.append(arr_of(tb))
            plan.append(None)
    n_terms, n_ex = len(plan), len(extras)

    def body(*refs):
        ex_refs, out_refs = refs[2 * n_terms:2 * n_terms + n_ex], refs[2 * n_terms + n_ex:]
        acc = None
        for t, kd in enumerate(plan):
            a_ref, b_ref = refs[2 * t], refs[2 * t + 1]
            if kd is None:
                parts = [lax.dot_general(a_ref[...].astype(BF), b_ref[...].astype(BF), dims, preferred_element_type=F32)]
            else:
                parts = [lax.dot_general(a_ref[:, d * kd:(d + 1) * kd].astype(BF), b_ref[d].astype(BF), dims,
                                         preferred_element_type=F32) for d in range(N_DEV)]
            for part in parts:
                acc = part if acc is None else acc + part
        outs = (acc,) if epi is None else epi(acc, *[r[...] for r in ex_refs])
        for r, o in zip(out_refs, outs):
            r[...] = o.astype(r.dtype)

    o_spec = pl.BlockSpec((tm, tn), lambda i, j: (i, j))
    if out_blocks:
        per = n // N_DEV // tn
        out_shape = tuple(_sds((N_DEV, m, n // N_DEV), dt) for dt in out_dtypes)
        out_specs = tuple(pl.BlockSpec((None, tm, tn), lambda i, j: (j // per, i, j % per)) for _ in out_dtypes)
    else:
        out_shape = tuple(_sds((m, n), dt) for dt in out_dtypes)
        out_specs = tuple(o_spec for _ in out_dtypes)
    outs = pl.pallas_call(
        body,
        out_shape=out_shape,
        grid=(m // tm, n // tn),
        in_specs=in_specs + [o_spec] * n_ex,
        out_specs=out_specs,
        compiler_params=_cp(("parallel", "parallel")),
        name=name,
    )(*args, *extras)
    return outs[0] if len(out_dtypes) == 1 else outs


def _rt(t):
    return _pick(t, (384, 256, 128))


def norm_fwd(x, g, out_dtype, col_blk=0, width=None, name="norm_fwd"):
    t = x.shape[0]
    w = width or x.shape[1]
    tr = _rt(t)

    def body(x_ref, g_ref, o_ref):
        xv = x_ref[...]
        r = lax.rsqrt(jnp.mean(xv * xv, axis=-1, keepdims=True) + EPS)
        o_ref[...] = (xv * r * g_ref[...]).astype(o_ref.dtype)

    return pl.pallas_call(
        body,
        out_shape=_sds((t, w), out_dtype),
        grid=(t // tr,),
        in_specs=[pl.BlockSpec((tr, w), lambda i: (i, col_blk)), pl.BlockSpec((1, w), lambda i: (0, 0))],
        out_specs=pl.BlockSpec((tr, w), lambda i: (i, 0)),
        compiler_params=_cp(("parallel",)),
        name=name,
    )(x, g.reshape(1, w))


def _rms_bwd(xv, gv, dyv):
    r = lax.rsqrt(jnp.mean(xv * xv, axis=-1, keepdims=True) + EPS)
    xh = xv * r
    dyg = dyv * gv
    dx = r * (dyg - xh * jnp.mean(dyg * xh, axis=-1, keepdims=True))
    return dx, jnp.sum(dyv * xh, axis=0, keepdims=True)


def norm_bwd(x, g, dy, dres=None, mask_pad=False, out_dtype=F32, col_blk=0, width=None, dy_col_blk=0, then=None,
             name="norm_bwd"):
    t = x.shape[0]
    w = width or x.shape[1]
    tr = _rt(t)
    has_res, has_then = dres is not None, then is not None

    def body(*refs):
        x_ref, g_ref, dy_ref = refs[:3]
        n_in = 3 + has_res + 2 * has_then
        dx_ref, dg_ref = refs[n_in:n_in + 2]
        i = pl.program_id(0)
        dyv = dy_ref[...].astype(F32)
        if mask_pad:
            dyv = jnp.where(_row_mask(i, tr, dyv.shape), dyv, 0.0)
        dx, dg = _rms_bwd(x_ref[...], g_ref[...], dyv)
        if has_res:
            dx = dx + refs[3][...]
        dx_ref[...] = dx.astype(dx_ref.dtype)

        @pl.when(i == 0)
        def _():
            for r in refs[n_in + 1::2]:
                r[...] = jnp.zeros_like(r)

        dg_ref[...] += dg
        if has_then:
            x2_ref, g2_ref = refs[3 + has_res:5 + has_res]
            dx2_ref, dg2_ref = refs[n_in + 2:]
            dx2, dg2 = _rms_bwd(x2_ref[...], g2_ref[...], jnp.where(_row_mask(i, tr, dx.shape), dx, 0.0))
            dx2_ref[...] = dx2.astype(dx2_ref.dtype)
            dg2_ref[...] += dg2

    row = pl.BlockSpec((tr, w), lambda i: (i, 0))
    vec = pl.BlockSpec((1, w), lambda i: (0, 0))
    in_specs = [pl.BlockSpec((tr, w), lambda i: (i, col_blk)), vec, pl.BlockSpec((tr, w), lambda i: (i, dy_col_blk))]
    args = [x, g.reshape(1, w), dy]
    out_shape, out_specs = [_sds((t, w), out_dtype), _sds((1, w), F32)], [row, vec]
    if has_res:
        in_specs.append(row)
        args.append(dres)
    if has_then:
        in_specs += [row, vec]
        args += [then[0], then[1].reshape(1, w)]
        out_shape += [_sds((t, w), BF), _sds((1, w), F32)]
        out_specs += [row, vec]
    outs = pl.pallas_call(
        body,
        out_shape=tuple(out_shape),
        grid=(t // tr,),
        in_specs=in_specs,
        out_specs=tuple(out_specs),
        compiler_params=_cp(("arbitrary",)),
        name=name,
    )(*args)
    if has_then:
        return outs[0], outs[1].reshape(w), outs[2], outs[3].reshape(w)
    return outs[0], outs[1].reshape(w)


def resadd_fwd(h, m, g, g_next=None, name="resadd"):
    t, w = h.shape
    tr = _rt(t)
    with_next = g_next is not None

    def body(h_ref, m_ref, g_ref, *rest):
        mv = m_ref[...]
        r = lax.rsqrt(jnp.mean(mv * mv, axis=-1, keepdims=True) + EPS)
        y = mv * r * g_ref[...]
        h2 = h_ref[...] + jnp.where(_row_mask(pl.program_id(0), tr, y.shape), y, 0.0)
        if with_next:
            gn_ref, o_ref, hn_ref = rest
            r2 = lax.rsqrt(jnp.mean(h2 * h2, axis=-1, keepdims=True) + EPS)
            hn_ref[...] = (h2 * r2 * gn_ref[...]).astype(hn_ref.dtype)
        else:
            (o_ref,) = rest
        o_ref[...] = h2

    row = pl.BlockSpec((tr, w), lambda i: (i, 0))
    vec = pl.BlockSpec((1, w), lambda i: (0, 0))
    outs = pl.pallas_call(
        body,
        out_shape=(_sds((t, w), F32),) + ((_sds((t, w), BF),) if with_next else ()),
        grid=(t // tr,),
        in_specs=[row, row, vec] + ([vec] if with_next else []),
        out_specs=(row,) + ((row,) if with_next else ()),
        compiler_params=_cp(("parallel",)),
        name=name,
    )(h, m, g.reshape(1, w), *((g_next.reshape(1, w),) if with_next else ()))
    return outs[0], (outs[1] if with_next else None)


def loss_fwd_bwd(h, target):
    t, w = h.shape
    tr = _rt(t)

    def body(h_ref, t_ref, s_ref, dh_ref):
        i = pl.program_id(0)
        err = h_ref[...] - t_ref[...]
        err = jnp.where(_row_mask(i, tr, err.shape, PAD + N_META), err, 0.0)
        dh_ref[...] = err * (1.0 / w)

        @pl.when(i == 0)
        def _():
            s_ref[...] = jnp.zeros_like(s_ref)

        s_ref[...] += jnp.sum(err * err).reshape(1, 1)

    s, dh = pl.pallas_call(
        body,
        out_shape=(_sds((1, LANE), F32), _sds((t, w), F32)),
        grid=(t // tr,),
        in_specs=[pl.BlockSpec((tr, w), lambda i: (i, 0)), pl.BlockSpec((tr, w), lambda i: (i, 0))],
        out_specs=(pl.BlockSpec((1, LANE), lambda i: (0, 0)), pl.BlockSpec((tr, w), lambda i: (i, 0))),
        compiler_params=_cp(("arbitrary",)),
        name="loss",
    )(h, target)
    return 0.5 * s[0, 0] / w, dh


def _shift_down(ext, k, n):
    return pltpu.roll(ext, k, 0)[8:]


def _conv_pre(ext, x, w_ref, n):
    return (w_ref[4:5, :] + w_ref[3:4, :] * x + w_ref[2:3, :] * _shift_down(ext, 1, n)
            + w_ref[1:2, :] * _shift_down(ext, 2, n) + w_ref[0:1, :] * _shift_down(ext, 3, n))


def _conv_bwd_parts(dpre, dnext, x, ext, w_ref, n):
    extd = jnp.concatenate([dpre, dnext], axis=0)
    ln = n + 8
    dx = (w_ref[3:4, :] * dpre + w_ref[2:3, :] * pltpu.roll(extd, ln - 1, 0)[:n]
          + w_ref[1:2, :] * pltpu.roll(extd, ln - 2, 0)[:n] + w_ref[0:1, :] * pltpu.roll(extd, ln - 3, 0)[:n])
    sums = [jnp.sum(dpre * _shift_down(ext, 3, n), axis=0, keepdims=True),
            jnp.sum(dpre * _shift_down(ext, 2, n), axis=0, keepdims=True),
            jnp.sum(dpre * _shift_down(ext, 1, n), axis=0, keepdims=True),
            jnp.sum(dpre * x, axis=0, keepdims=True),
            jnp.sum(dpre, axis=0, keepdims=True)]
    return dx, sums


def _rows_block(sums):
    w = sums[0].shape[1]
    row = lax.broadcasted_iota(jnp.int32, (8, w), 0)
    out = jnp.zeros((8, w), F32)
    for k, s in enumerate(sums):
        out = jnp.where(row == k, s, out)
    return out


CONV_BLOCK = 512


def conv_silu_fwd(x, col0, c, wb, name="conv_fwd"):
    t = x.shape[0]
    cw = _pick(c, (CONV_BLOCK, LANE))
    nblk, col0_blk = c // cw, col0 // cw
    assert col0 % cw == 0
    tr = _rt(t)

    def body(x_ref, w_ref, o_ref, prev):
        ti = pl.program_id(1)

        @pl.when(ti == 0)
        def _():
            prev[...] = jnp.zeros_like(prev)

        xv = x_ref[...]
        ext = jnp.concatenate([prev[...], xv], axis=0)
        o_ref[...] = _silu(_conv_pre(ext, xv, w_ref, tr))
        prev[...] = xv[tr - 8:, :]

    return pl.pallas_call(
        body,
        out_shape=_sds((t, c), F32),
        grid=(nblk, t // tr),
        in_specs=[pl.BlockSpec((tr, cw), lambda cb, ti: (ti, col0_blk + cb)),
                  pl.BlockSpec((8, cw), lambda cb, ti: (0, cb))],
        out_specs=pl.BlockSpec((tr, cw), lambda cb, ti: (ti, cb)),
        scratch_shapes=[pltpu.VMEM((8, cw), F32)],
        compiler_params=_cp(("parallel", "arbitrary")),
        name=name,
    )(x, wb)


def conv_silu_bwd(x, col0, c, wb, dout, name="conv_bwd"):
    t = x.shape[0]
    cw = _pick(c, (CONV_BLOCK, LANE))
    nblk, col0_blk = c // cw, col0 // cw
    assert col0 % cw == 0
    tr = _rt(t)
    nt = t // tr
    r8 = tr // 8

    def body(x_ref, xp_ref, w_ref, do_ref, dx_ref, dwb_ref, dnext):
        ti = pl.program_id(1)
        tt = nt - 1 - ti

        @pl.when(ti == 0)
        def _():
            dnext[...] = jnp.zeros_like(dnext)
            dwb_ref[...] = jnp.zeros_like(dwb_ref)

        xv = x_ref[...]
        halo = jnp.where(tt > 0, xp_ref[...], 0.0)
        ext = jnp.concatenate([halo, xv], axis=0)
        pre = _conv_pre(ext, xv, w_ref, tr)
        s = jax.nn.sigmoid(pre)
        dpre = do_ref[...] * (s + pre * s * (1.0 - s))
        dx, sums = _conv_bwd_parts(dpre, dnext[...], xv, ext, w_ref, tr)
        dx_ref[...] = dx.astype(dx_ref.dtype)
        dwb_ref[...] += _rows_block(sums)
        dnext[...] = dpre[:8, :]

    return pl.pallas_call(
        body,
        out_shape=(_sds((t, c), BF), _sds((8, c), F32)),
        grid=(nblk, nt),
        in_specs=[pl.BlockSpec((tr, cw), lambda cb, ti: (nt - 1 - ti, col0_blk + cb)),
                  pl.BlockSpec((8, cw), lambda cb, ti: (jnp.maximum((nt - 1 - ti) * r8 - 1, 0), col0_blk + cb)),
                  pl.BlockSpec((8, cw), lambda cb, ti: (0, cb)),
                  pl.BlockSpec((tr, cw), lambda cb, ti: (nt - 1 - ti, cb))],
        out_specs=(pl.BlockSpec((tr, cw), lambda cb, ti: (nt - 1 - ti, cb)),
                   pl.BlockSpec((8, cw), lambda cb, ti: (0, cb))),
        scratch_shapes=[pltpu.VMEM((8, cw), F32)],
        compiler_params=_cp(("parallel", "arbitrary")),
        name=name,
    )(x, x, wb, dout)


def gated_norm_fwd(y, proj, g, name="gnorm_fwd"):
    t, w = y.shape
    tr = _rt(t)

    def body(y_ref, z_ref, g_ref, o_ref):
        v = y_ref[...] * _silu(z_ref[...])
        r = lax.rsqrt(jnp.mean(v * v, axis=-1, keepdims=True) + EPS)
        o_ref[...] = (v * r * g_ref[...]).astype(o_ref.dtype)

    return pl.pallas_call(
        body,
        out_shape=_sds((t, w), BF),
        grid=(t // tr,),
        in_specs=[pl.BlockSpec((tr, w), lambda i: (i, 0)), pl.BlockSpec((tr, w), lambda i: (i, OFF_Z // w)),
                  pl.BlockSpec((1, w), lambda i: (0, 0))],
        out_specs=pl.BlockSpec((tr, w), lambda i: (i, 0)),
        compiler_params=_cp(("parallel",)),
        name=name,
    )(y, proj, g.reshape(1, w))


def gated_norm_bwd(y, proj, g, dyab, name="gnorm_bwd"):
    t, w = y.shape
    tr = _rt(t)

    def body(y_ref, z_ref, g_ref, do_ref, dy_ref, dz_ref, dg_ref):
        i = pl.program_id(0)
        yv, zv, dov = y_ref[...], z_ref[...], do_ref[...]
        s = jax.nn.sigmoid(zv)
        sz = zv * s
        v = yv * sz
        r = lax.rsqrt(jnp.mean(v * v, axis=-1, keepdims=True) + EPS)
        vh = v * r
        dvg = dov * g_ref[...]
        dv = r * (dvg - vh * jnp.mean(dvg * vh, axis=-1, keepdims=True))
        dy_ref[...] = dv * sz
        dz_ref[...] = (dv * yv * (s + sz * (1.0 - s))).astype(dz_ref.dtype)

        @pl.when(i == 0)
        def _():
            dg_ref[...] = jnp.zeros_like(dg_ref)

        dg_ref[...] += jnp.sum(dov * vh, axis=0, keepdims=True)

    dy, dz, dg = pl.pallas_call(
        body,
        out_shape=(_sds((t, w), F32), _sds((t, w), BF), _sds((1, w), F32)),
        grid=(t // tr,),
        in_specs=[pl.BlockSpec((tr, w), lambda i: (i, 0)), pl.BlockSpec((tr, w), lambda i: (i, OFF_Z // w)),
                  pl.BlockSpec((1, w), lambda i: (0, 0)), pl.BlockSpec((tr, w), lambda i: (i, 0))],
        out_specs=(pl.BlockSpec((tr, w), lambda i: (i, 0)), pl.BlockSpec((tr, w), lambda i: (i, 0)),
                   pl.BlockSpec((1, w), lambda i: (0, 0))),
        compiler_params=_cp(("arbitrary",)),
        name=name,
    )(y, proj, g.reshape(1, w), dyab)
    return dy, dz, dg.reshape(w)


def rope_tables(t):
    inv = ROPE_BASE ** (-jnp.arange(0, MLA_ROPE, 2, dtype=F32) / MLA_ROPE)
    pos = (jnp.arange(t, dtype=F32) - PAD)[:, None]
    ang = pos * inv[None, :]
    cos, sin = jnp.cos(ang), jnp.sin(ang)
    z16 = jnp.zeros((t, 16), F32)
    z32 = jnp.zeros((t, 32), F32)
    c = jnp.concatenate([jnp.ones((t, 64), F32), cos, cos, z32], axis=1)
    s1 = jnp.concatenate([jnp.zeros((t, 64), F32), z16, sin, z32], axis=1)
    s2 = jnp.concatenate([jnp.zeros((t, 64), F32), -sin, z16, z32], axis=1)
    return c, s1, s2


def _rope(x, c, s1, s2):
    return x * c + pltpu.roll(x, 16, 1) * s1 + pltpu.roll(x, LANE - 16, 1) * s2


def _rope_t(d, c, s1, s2):
    return d * c + pltpu.roll(d * s1, LANE - 16, 1) + pltpu.roll(d * s2, 16, 1)


def rope_fwd(q_raw, kv_raw, proj, tabs):
    t = q_raw.shape[0]
    tr = _rt(t)
    hw = MLA_HEADS * LANE

    def body(q_ref, k_ref, v_ref, kr_ref, c_ref, s1_ref, s2_ref, qo_ref, ko_ref, vo_ref):
        c, s1, s2 = c_ref[...], s1_ref[...], s2_ref[...]
        kr = _rope(kr_ref[...], c, s1, s2)
        for h in range(MLA_HEADS):
            sl = slice(h * LANE, (h + 1) * LANE)
            qo_ref[:, sl] = (_rope(q_ref[:, sl], c, s1, s2) * Q_PRESCALE).astype(BF)
            ko_ref[:, sl] = (k_ref[:, sl] + kr).astype(BF)
        vo_ref[...] = v_ref[...].astype(BF)

    tab_spec = pl.BlockSpec((tr, LANE), lambda i: (i, 0))
    return pl.pallas_call(
        body,
        out_shape=(_sds((t, hw), BF), _sds((t, hw), BF), _sds((t, 1024), BF)),
        grid=(t // tr,),
        in_specs=[pl.BlockSpec((tr, hw), lambda i: (i, 0)), pl.BlockSpec((tr, hw), lambda i: (i, 0)),
                  pl.BlockSpec((tr, 1024), lambda i: (i, 2)), pl.BlockSpec((tr, LANE), lambda i: (i, OFF_KR // LANE)),
                  tab_spec, tab_spec, tab_spec],
        out_specs=(pl.BlockSpec((tr, hw), lambda i: (i, 0)), pl.BlockSpec((tr, hw), lambda i: (i, 0)),
                   pl.BlockSpec((tr, 1024), lambda i: (i, 0))),
        compiler_params=_cp(("parallel",)),
        name="rope_fwd",
    )(q_raw, kv_raw, kv_raw, proj, *tabs)


def rope_bwd(dq_cat, dk_cat, tabs):
    t = dq_cat.shape[0]
    tr = _rt(t)
    hw = MLA_HEADS * LANE

    def body(dq_ref, dk_ref, c_ref, s1_ref, s2_ref, dqo_ref, dkr_ref):
        c, s1, s2 = c_ref[...], s1_ref[...], s2_ref[...]
        acc = jnp.zeros((tr, LANE), F32)
        for h in range(MLA_HEADS):
            sl = slice(h * LANE, (h + 1) * LANE)
            dqo_ref[:, sl] = _rope_t(dq_ref[:, sl] * ATT_SCALE, c, s1, s2).astype(BF)
            acc = acc + dk_ref[:, sl]
        lane = lax.broadcasted_iota(jnp.int32, (tr, LANE), 1)
        dkr_ref[...] = jnp.where((lane >= 64) & (lane < 96), _rope_t(acc, c, s1, s2), 0.0)

    tab_spec = pl.BlockSpec((tr, LANE), lambda i: (i, 0))
    return pl.pallas_call(
        body,
        out_shape=(_sds((t, hw), BF), _sds((t, LANE), F32)),
        grid=(t // tr,),
        in_specs=[pl.BlockSpec((tr, hw), lambda i: (i, 0)), pl.BlockSpec((tr, hw), lambda i: (i, 0)),
                  tab_spec, tab_spec, tab_spec],
        out_specs=(pl.BlockSpec((tr, hw), lambda i: (i, 0)), pl.BlockSpec((tr, LANE), lambda i: (i, 0))),
        compiler_params=_cp(("parallel",)),
        name="rope_bwd",
    )(dq_cat, dk_cat, *tabs)


ATT_SCALE = (MLA_NOPE + MLA_ROPE) ** -0.5
LOG2E = math.log2(math.e)
Q_PRESCALE = ATT_SCALE * LOG2E
CARRY_MIDDLE_PAIR = 6
NT_DIMS = (((1,), (1,)), ((), ()))
TN_DIMS = (((0,), (0,)), ((), ()))


def _att_mask(qi, ki, tq, tk):
    qpos = qi * tq + lax.broadcasted_iota(jnp.int32, (tq, tk), 0)
    kpos = ki * tk + lax.broadcasted_iota(jnp.int32, (tq, tk), 1)
    return (kpos <= qpos) & (kpos >= PAD)


def _half_masks(n):
    lane = lax.broadcasted_iota(jnp.int32, (n, LANE), 1)
    return lane < 64, lane >= 64


def _att_tile(t):
    return _pick(t, (384, 256, 128))


def _ds(i, n):
    return pl.ds(i * n, n) if isinstance(i, int) else pl.ds(pl.multiple_of(i * n, n), n)


FWD_PAIRS = 2


def attn_fwd(q_cat, k_cat, v, carry=None):
    t = q_cat.shape[0]
    tq = tk = _att_tile(t)
    nq = t // tq
    npair, nh = FWD_PAIRS, 2 * FWD_PAIRS
    n_grp = MLA_HEADS // nh
    nx = carry.k if carry else 0

    def body(*refs):
        q_ref, k_ref, v_ref = refs[:3]
        o_ref, lse_ref = refs[3 + nx:5 + nx]
        qi = pl.program_id(1)
        if carry:
            start, middle, finish = carry.phases(refs[3:3 + nx], refs[5 + nx:5 + 2 * nx], refs[5 + 2 * nx:])
            grp = pl.program_id(0)
            pl.when((grp == 0) & (qi == 0))(start)
            pl.when((grp == CARRY_MIDDLE_PAIR // npair) & (qi == 0))(middle)
        lo_q, _ = _half_masks(tq)
        halves = _half_masks(tk)

        def step(ki, state, masked):
            m_old, l_old, accs = state[0:nh], state[nh:2 * nh], state[2 * nh:]
            rows = _ds(ki, tk)
            ss = [lax.dot_general(q_ref[:, h * LANE:(h + 1) * LANE], k_ref[rows, h * LANE:(h + 1) * LANE], NT_DIMS,
                                  preferred_element_type=F32) for h in range(nh)]
            if masked:
                valid = _att_mask(qi, ki, tq, tk)
                ss = [jnp.where(valid, s, NEG) for s in ss]
            m_new = [jnp.maximum(m_old[h], jnp.max(ss[h], axis=-1, keepdims=True)) for h in range(nh)]
            ps = [jnp.exp2(ss[h] - m_new[h]) for h in range(nh)]
            alpha = [jnp.exp2(m_old[h] - m_new[h]) for h in range(nh)]
            l_new = [alpha[h] * l_old[h] + jnp.sum(ps[h], axis=-1, keepdims=True) for h in range(nh)]
            new_accs = []
            for pp in range(npair):
                vv = v_ref[rows, pp * LANE:(pp + 1) * LANE]
                pv = [jnp.dot(ps[2 * pp + hh].astype(BF), jnp.where(halves[hh], vv, jnp.zeros_like(vv)),
                              preferred_element_type=F32) for hh in range(2)]
                new_accs.append(accs[pp] * jnp.where(lo_q, alpha[2 * pp], alpha[2 * pp + 1]) + pv[0] + pv[1])
            return tuple(m_new) + tuple(l_new) + tuple(new_accs)

        neg, zero = jnp.full((tq, 1), NEG, F32), jnp.zeros((tq, 1), F32)
        state = step(0, (neg,) * nh + (zero,) * nh + (jnp.zeros((tq, LANE), F32),) * npair, True)
        state = lax.fori_loop(1, qi, lambda ki, st: step(ki, st, False), state)
        state = lax.cond(qi > 0, lambda st: step(qi, st, True), lambda st: st, state)
        for pp in range(npair):
            l = jnp.where(lo_q, state[nh + 2 * pp], state[nh + 2 * pp + 1])
            o_ref[:, pp * LANE:(pp + 1) * LANE] = (state[2 * nh + pp] / l).astype(o_ref.dtype)
            lse_ref[:, pp * LANE:(pp + 1) * LANE] = jnp.where(lo_q, state[2 * pp], state[2 * pp + 1]) + jnp.log2(l)
        if carry:
            pl.when((grp == n_grp - 1) & (qi == nq - 1))(finish)

    outs = pl.pallas_call(
        body,
        out_shape=(_sds((t, 1024), BF), _sds((t, 1024), F32)) + tuple(carry.out_shapes() if carry else ()),
        grid=(n_grp, nq),
        in_specs=[pl.BlockSpec((tq, nh * LANE), lambda g, qi: (qi, g)),
                  pl.BlockSpec((t, nh * LANE), lambda g, qi: (0, g)),
                  pl.BlockSpec((t, npair * LANE), lambda g, qi: (0, g))] + [ANY] * nx,
        out_specs=(pl.BlockSpec((tq, npair * LANE), lambda g, qi: (qi, g)),
                   pl.BlockSpec((tq, npair * LANE), lambda g, qi: (qi, g))) + (ANY,) * nx,
        scratch_shapes=carry.scratch() if carry else [],
        compiler_params=_cp(("arbitrary", "arbitrary") if carry else ("parallel", "parallel")),
        name="attn_fwd_carrying" if carry else "attn_fwd",
    )(q_cat, k_cat, v, *(carry.arrs if carry else ()))
    return outs[0], outs[1], list(outs[2:])


def attn_bwd(q_cat, k_cat, v, o, lse, dyab, carry=None):
    t = q_cat.shape[0]
    tq = tk = _att_tile(t)
    nq = t // tq
    n_pair = MLA_HEADS // 2
    nx = carry.k if carry else 0

    def body(*refs):
        q_ref, k_ref, v_ref, o_ref, lse_ref, do_ref = refs[:6]
        dq_ref, dk_ref, dv_ref = refs[6 + nx:9 + nx]
        ki = pl.program_id(1)
        if carry:
            start, middle, finish = carry.phases(refs[6:6 + nx], refs[9 + nx:9 + 2 * nx], refs[9 + 2 * nx:])
            pair = pl.program_id(0)
            pl.when((pair == 0) & (ki == 0))(start)
            pl.when((pair == CARRY_MIDDLE_PAIR) & (ki == 0))(middle)

        @pl.when(ki == 0)
        def _():
            dq_ref[...] = jnp.zeros_like(dq_ref)

        halves = _half_masks(tq)
        vv = v_ref[...]
        kk = [k_ref[:, hh * LANE:(hh + 1) * LANE] for hh in range(2)]

        def step(qi, acc, masked):
            rows = _ds(qi, tq)
            dov, ov, lse_v = do_ref[rows, :], o_ref[rows, :].astype(F32), lse_ref[rows, :]
            qh = [q_ref[rows, hh * LANE:(hh + 1) * LANE] for hh in range(2)]
            ss = [lax.dot_general(qh[hh], kk[hh], NT_DIMS, preferred_element_type=F32) for hh in range(2)]
            if masked:
                valid = _att_mask(qi, ki, tq, tk)
                ss = [jnp.where(valid, s, NEG) for s in ss]
            ps = [jnp.exp2(ss[hh] - lse_v[:, 64 * hh:64 * hh + 1]) for hh in range(2)]
            dom = [jnp.where(halves[hh], dov, 0.0) for hh in range(2)]
            delta = [jnp.sum(dom[hh] * ov, axis=-1, keepdims=True) for hh in range(2)]
            dom = [d.astype(BF) for d in dom]
            dp = [lax.dot_general(dom[hh], vv, NT_DIMS, preferred_element_type=F32) for hh in range(2)]
            ds = [(ps[hh] * (dp[hh] - delta[hh])).astype(BF) for hh in range(2)]
            pb = [p.astype(BF) for p in ps]
            dv = (acc[2] + lax.dot_general(pb[0], dom[0], TN_DIMS, preferred_element_type=F32)
                  + lax.dot_general(pb[1], dom[1], TN_DIMS, preferred_element_type=F32))
            dk = [acc[hh] + lax.dot_general(ds[hh], qh[hh], TN_DIMS, preferred_element_type=F32) for hh in range(2)]
            for hh in range(2):
                dq_ref[rows, hh * LANE:(hh + 1) * LANE] += jnp.dot(ds[hh], kk[hh], preferred_element_type=F32)
            return dk[0], dk[1], dv

        zero = jnp.zeros((tk, LANE), F32)
        acc = step(ki, (zero, zero, zero), True)
        acc = lax.fori_loop(ki + 1, jnp.where(ki == 0, nq, ki + 1), lambda qi, a: step(qi, a, True), acc)
        acc = lax.fori_loop(ki + 1, jnp.where(ki == 0, ki + 1, nq), lambda qi, a: step(qi, a, False), acc)
        dk_ref[:, 0:LANE] = acc[0] * (1.0 / LOG2E)
        dk_ref[:, LANE:2 * LANE] = acc[1] * (1.0 / LOG2E)
        dv_ref[...] = acc[2].astype(dv_ref.dtype)
        if carry:
            pl.when((pair == n_pair - 1) & (ki == nq - 1))(finish)

    full = lambda w, off=0: pl.BlockSpec((t, w), lambda p, ki: (0, p + off))
    blk = lambda w: pl.BlockSpec((tk, w), lambda p, ki: (ki, p))
    outs = pl.pallas_call(
        body,
        out_shape=(_sds((t, 2048), F32), _sds((t, 2048), F32), _sds((t, 1024), BF))
        + tuple(carry.out_shapes() if carry else ()),
        grid=(n_pair, nq),
        in_specs=[full(2 * LANE), blk(2 * LANE), blk(LANE), full(LANE), full(LANE), full(LANE, 8)] + [ANY] * nx,
        out_specs=(full(2 * LANE), blk(2 * LANE), blk(LANE)) + (ANY,) * nx,
        scratch_shapes=carry.scratch() if carry else [],
        compiler_params=_cp(("arbitrary", "arbitrary") if carry else ("parallel", "arbitrary")),
        name="attn_bwd_carrying" if carry else "attn_bwd",
    )(q_cat, k_cat, v, o, lse, dyab, *(carry.arrs if carry else ()))
    return outs[0], outs[1], outs[2], list(outs[3:])


N_PAIR = SSD_HEADS // 2


def _hdot(a, b):
    return jnp.dot(a, b, precision=HI, preferred_element_type=F32)


def _ssd_chunk(xs, bg, cg, dtraw, hin, dt_bias, a_log, dskip, rowmask):
    ln = CHUNK
    causal = lax.broadcasted_iota(jnp.int32, (ln, ln), 0) >= lax.broadcasted_iota(jnp.int32, (ln, ln), 1)
    ltri = causal.astype(F32)
    lane = lax.broadcasted_iota(jnp.int32, (ln, LANE), 1)
    halves = (lane < 64, lane >= 64)
    low_row = lax.broadcasted_iota(jnp.int32, (1, LANE), 1) < 64
    head_lane = lax.broadcasted_iota(jnp.int32, (1, SSD_HEADS), 1)
    head_row = lax.broadcasted_iota(jnp.int32, (SSD_HEADS, 1), 0)

    def col(a, h):
        return jnp.sum(jnp.where(head_lane == h, a, 0.0), axis=1, keepdims=True)

    dt = _softplus(dtraw + dt_bias) * rowmask
    da = dt * (-jnp.exp(a_log))
    acs = _hdot(ltri, da)
    acs_t = lax.dot_general(da, ltri, (((0,), (1,)), ((), ())), precision=HI, preferred_element_type=F32)
    tot = jnp.sum(da, axis=0, keepdims=True)
    bm = [b * rowmask for b in bg]
    cm = [c * rowmask for c in cg]
    cb = [lax.dot_general(cm[g].astype(BF), bm[g].astype(BF), NT_DIMS, preferred_element_type=F32) for g in range(2)]
    ys, hout = [], []
    for p in range(N_PAIR):
        g = p // (N_PAIR // 2)
        h0, h1 = 2 * p, 2 * p + 1
        xdt = xs[p] * jnp.where(halves[0], col(dt, h0), col(dt, h1))
        a_cols = [col(acs, h0), col(acs, h1)]
        tot_cols = [col(tot, h0), col(tot, h1)]
        y = jnp.zeros((ln, LANE), F32)
        snew = jnp.zeros((ln, LANE), F32)
        for hh in range(2):
            a_row = jnp.sum(jnp.where(head_row == h0 + hh, acs_t, 0.0), axis=0, keepdims=True)
            dec = jnp.exp(jnp.where(causal, a_cols[hh] - a_row, NEG))
            xm = jnp.where(halves[hh], xdt, 0.0).astype(BF)
            y = y + jnp.dot((cb[g] * dec).astype(BF), xm, preferred_element_type=F32)
            bd = bm[g] * jnp.exp(tot_cols[hh] - a_cols[hh])
            snew = snew + lax.dot_general(bd.astype(BF), xm, TN_DIMS, preferred_element_type=F32)
        y_off = (jnp.dot(cm[g].astype(BF), hin[p].astype(BF), preferred_element_type=F32)
                 * jnp.where(halves[0], jnp.exp(a_cols[0]), jnp.exp(a_cols[1])))
        ys.append(y + y_off + jnp.where(low_row, col(dskip, h0), col(dskip, h1)) * xs[p])
        hout.append(jnp.where(low_row, jnp.exp(tot_cols[0]), jnp.exp(tot_cols[1])) * hin[p] + snew)
    return ys, hout


def _ssd_load(x_ref, dt_ref):
    xs = [x_ref[:, p * LANE:(p + 1) * LANE] for p in range(N_PAIR)]
    bg = [x_ref[:, SSD_D_INNER + g * LANE:SSD_D_INNER + (g + 1) * LANE] for g in range(2)]
    cg = [x_ref[:, SSD_D_INNER + (2 + g) * LANE:SSD_D_INNER + (3 + g) * LANE] for g in range(2)]
    return xs, bg, cg, dt_ref[:, 0:SSD_HEADS]


def _chunk_rowmask(c):
    return ((c * CHUNK + lax.broadcasted_iota(jnp.int32, (CHUNK, 1), 0)) >= PAD).astype(F32)


def ssd_fwd(xbc_c, proj, dt_bias, a_log, dskip):
    t = xbc_c.shape[0]
    nc = t // CHUNK

    def body(x_ref, dt_ref, dtb_ref, al_ref, d_ref, y_ref, hs_ref, h_s):
        c = pl.program_id(0)

        @pl.when(c == 0)
        def _():
            h_s[...] = jnp.zeros_like(h_s)

        xs, bg, cg, dtraw = _ssd_load(x_ref, dt_ref)
        hin = [h_s[p] for p in range(N_PAIR)]
        hs_ref[0] = h_s[...]
        ys, hout = _ssd_chunk(xs, bg, cg, dtraw, hin, dtb_ref[...], al_ref[...], d_ref[...], _chunk_rowmask(c))
        for p in range(N_PAIR):
            y_ref[:, p * LANE:(p + 1) * LANE] = ys[p]
            h_s[p] = hout[p]

    par = pl.BlockSpec((1, SSD_HEADS), lambda c: (0, 0))
    return pl.pallas_call(
        body,
        out_shape=(_sds((t, SSD_D_INNER), F32), _sds((nc, N_PAIR, CHUNK, LANE), F32)),
        grid=(nc,),
        in_specs=[pl.BlockSpec((CHUNK, SSD_CONV_CH), lambda c: (c, 0)),
                  pl.BlockSpec((CHUNK, LANE), lambda c: (c, OFF_DT // LANE)), par, par, par],
        out_specs=(pl.BlockSpec((CHUNK, SSD_D_INNER), lambda c: (c, 0)),
                   pl.BlockSpec((1, N_PAIR, CHUNK, LANE), lambda c: (c, 0, 0, 0))),
        scratch_shapes=[pltpu.VMEM((N_PAIR, CHUNK, LANE), F32)],
        compiler_params=_cp(("arbitrary",)),
        name="ssd_fwd",
    )(xbc_c, proj, dt_bias.reshape(1, -1), a_log.reshape(1, -1), dskip.reshape(1, -1))


def ssd_bwd(xbc_c, proj, dt_bias, a_log, dskip, hs, dy):
    t = xbc_c.shape[0]
    nc = t // CHUNK

    def body(x_ref, dt_ref, dtb_ref, al_ref, d_ref, hs_ref, dy_ref, dx_ref, ddt_ref, dpar_ref, dh_s):
        ci = pl.program_id(0)
        c = nc - 1 - ci

        @pl.when(ci == 0)
        def _():
            dh_s[...] = jnp.zeros_like(dh_s)
            dpar_ref[...] = jnp.zeros_like(dpar_ref)

        xs, bg, cg, dtraw = _ssd_load(x_ref, dt_ref)
        hin = [hs_ref[0, p] for p in range(N_PAIR)]
        rowmask = _chunk_rowmask(c)
        fn = lambda xs_, bg_, cg_, dtraw_, hin_, dtb_, al_, d_: _ssd_chunk(xs_, bg_, cg_, dtraw_, hin_, dtb_, al_, d_, rowmask)
        _, vjp = jax.vjp(fn, xs, bg, cg, dtraw, hin, dtb_ref[...], al_ref[...], d_ref[...])
        dys = [dy_ref[:, p * LANE:(p + 1) * LANE] for p in range(N_PAIR)]
        dhs = [dh_s[p] for p in range(N_PAIR)]
        dxs, dbg, dcg, ddtraw, dhin, ddtb, dal, dd = vjp((dys, dhs))
        for p in range(N_PAIR):
            dx_ref[:, p * LANE:(p + 1) * LANE] = dxs[p]
            dh_s[p] = dhin[p]
        for g in range(2):
            dx_ref[:, SSD_D_INNER + g * LANE:SSD_D_INNER + (g + 1) * LANE] = dbg[g]
            dx_ref[:, SSD_D_INNER + (2 + g) * LANE:SSD_D_INNER + (3 + g) * LANE] = dcg[g]
        ddt_ref[...] = jnp.zeros_like(ddt_ref)
        ddt_ref[:, 0:SSD_HEADS] = ddtraw
        dpar_ref[0:1, 0:SSD_HEADS] += ddtb
        dpar_ref[1:2, 0:SSD_HEADS] += dal
        dpar_ref[2:3, 0:SSD_HEADS] += dd

    par = pl.BlockSpec((1, SSD_HEADS), lambda ci: (0, 0))
    return pl.pallas_call(
        body,
        out_shape=(_sds((t, SSD_CONV_CH), F32), _sds((t, LANE), F32), _sds((8, LANE), F32)),
        grid=(nc,),
        in_specs=[pl.BlockSpec((CHUNK, SSD_CONV_CH), lambda ci: (nc - 1 - ci, 0)),
                  pl.BlockSpec((CHUNK, LANE), lambda ci: (nc - 1 - ci, OFF_DT // LANE)), par, par, par,
                  pl.BlockSpec((1, N_PAIR, CHUNK, LANE), lambda ci: (nc - 1 - ci, 0, 0, 0)),
                  pl.BlockSpec((CHUNK, SSD_D_INNER), lambda ci: (nc - 1 - ci, 0))],
        out_specs=(pl.BlockSpec((CHUNK, SSD_CONV_CH), lambda ci: (nc - 1 - ci, 0)),
                   pl.BlockSpec((CHUNK, LANE), lambda ci: (nc - 1 - ci, 0)),
                   pl.BlockSpec((8, LANE), lambda ci: (0, 0))),
        scratch_shapes=[pltpu.VMEM((N_PAIR, CHUNK, LANE), F32)],
        compiler_params=_cp(("arbitrary",)),
        name="ssd_bwd",
    )(xbc_c, proj, dt_bias.reshape(1, -1), a_log.reshape(1, -1), dskip.reshape(1, -1), hs, dy)


def _neg_expm1(y):
    series = -(y * (1.0 + y * (0.5 + y * (1.0 / 6.0 + y * (1.0 / 24.0 + y * (1.0 / 120.0))))))
    return jnp.where(y > -0.1, series, 1.0 - jnp.exp(y))


def _rg_pw(xr, wa, ba, wi, bi, lam, rowmask):
    xb = xr.astype(BF)
    r = jax.nn.sigmoid(jnp.dot(xb, wa.astype(BF), preferred_element_type=F32) + ba)
    i = jax.nn.sigmoid(jnp.dot(xb, wi.astype(BF), preferred_element_type=F32) + bi)
    log_a = -LRU_C * r * _softplus(-lam)
    a = jnp.exp(log_a)
    u = jnp.sqrt(_neg_expm1(2.0 * log_a)) * (i * xr) * rowmask
    return a, u


def _gelu_grad(x):
    c = math.sqrt(2.0 / math.pi)
    th = jnp.tanh(c * (x + 0.044715 * (x * x * x)))
    return 0.5 * (1.0 + th) + 0.5 * x * (1.0 - th * th) * c * (1.0 + 3.0 * 0.044715 * x * x)


def _scan_fwd(a, u):
    n = a.shape[0]
    row = lax.broadcasted_iota(jnp.int32, a.shape, 0)
    s = 1
    while s < n:
        a_s = jnp.where(row >= s, pltpu.roll(a, s, 0), 1.0)
        u_s = jnp.where(row >= s, pltpu.roll(u, s, 0), 0.0)
        u = u + a * u_s
        a = a * a_s
        s *= 2
    return a, u


def _scan_bwd(b, d):
    n = b.shape[0]
    row = lax.broadcasted_iota(jnp.int32, b.shape, 0)
    s = 1
    while s < n:
        b_s = jnp.where(row < n - s, pltpu.roll(b, n - s, 0), 1.0)
        d_s = jnp.where(row < n - s, pltpu.roll(d, n - s, 0), 0.0)
        d = d + b * d_s
        b = b * b_s
        s *= 2
    return d


def rg_fwd(xr_pre, gate_pre, rgp, w_a, w_i):
    t = xr_pre.shape[0]
    tr = _rt(t)

    def body(x_ref, g_ref, p_ref, wa_ref, wi_ref, hg_ref, hs_ref, prev, hcar):
        ti = pl.program_id(1)

        @pl.when(ti == 0)
        def _():
            prev[...] = jnp.zeros_like(prev)
            hcar[...] = jnp.zeros_like(hcar)

        xv = x_ref[...]
        ext = jnp.concatenate([prev[...], xv], axis=0)
        xr = _conv_pre(ext, xv, p_ref, tr)
        rowmask = _row_mask(ti, tr, (tr, 1)).astype(F32)
        a, u = _rg_pw(xr, wa_ref[0], p_ref[5:6, :], wi_ref[0], p_ref[6:7, :], p_ref[7:8, :], rowmask)
        a_cum, h_loc = _scan_fwd(a, u)
        hs = h_loc + a_cum * hcar[0:1, :]
        hs_ref[...] = hs
        hg_ref[...] = (hs * _gelu(g_ref[...])).astype(hg_ref.dtype)
        hcar[...] = jnp.broadcast_to(hs[tr - 1:tr, :], (8, LANE))
        prev[...] = xv[tr - 8:, :]

    return pl.pallas_call(
        body,
        out_shape=(_sds((t, LRU_WIDTH), BF), _sds((t, LRU_WIDTH), F32)),
        grid=(LRU_BLOCKS, t // tr),
        in_specs=[pl.BlockSpec((tr, LANE), lambda n, ti: (ti, n)),
                  pl.BlockSpec((tr, LANE), lambda n, ti: (ti, n)),
                  pl.BlockSpec((8, LANE), lambda n, ti: (0, n)),
                  pl.BlockSpec((1, LANE, LANE), lambda n, ti: (n, 0, 0)),
                  pl.BlockSpec((1, LANE, LANE), lambda n, ti: (n, 0, 0))],
        out_specs=(pl.BlockSpec((tr, LANE), lambda n, ti: (ti, n)), pl.BlockSpec((tr, LANE), lambda n, ti: (ti, n))),
        scratch_shapes=[pltpu.VMEM((8, LANE), F32), pltpu.VMEM((8, LANE), F32)],
        compiler_params=_cp(("parallel", "arbitrary")),
        name="rg_fwd",
    )(xr_pre, gate_pre, rgp, w_a, w_i)


def rg_bwd(xr_pre, gate_pre, rgp, w_a, w_i, hs, dhg):
    t = xr_pre.shape[0]
    tr = _rt(t)
    nt = t // tr
    r8 = tr // 8

    def body(x_ref, xp_ref, g_ref, p_ref, wa_ref, wi_ref, hs_ref, hp_ref, dhg_ref,
             dx_ref, dg_ref, dp_ref, dwa_ref, dwi_ref, gcar, dnext):
        ti = pl.program_id(1)
        tt = nt - 1 - ti

        @pl.when(ti == 0)
        def _():
            gcar[...] = jnp.zeros_like(gcar)
            dnext[...] = jnp.zeros_like(dnext)
            dp_ref[...] = jnp.zeros_like(dp_ref)
            dwa_ref[...] = jnp.zeros_like(dwa_ref)
            dwi_ref[...] = jnp.zeros_like(dwi_ref)

        xv = x_ref[...]
        halo = jnp.where(tt > 0, xp_ref[...], 0.0)
        ext = jnp.concatenate([halo, xv], axis=0)
        xr = _conv_pre(ext, xv, p_ref, tr)
        rowmask = _row_mask(tt, tr, (tr, 1)).astype(F32)
        fn = lambda xr_, wa_, ba_, wi_, bi_, lam_: _rg_pw(xr_, wa_, ba_, wi_, bi_, lam_, rowmask)
        (a, _), vjp = jax.vjp(fn, xr, wa_ref[0], p_ref[5:6, :], wi_ref[0], p_ref[6:7, :], p_ref[7:8, :])
        gpre = g_ref[...]
        hsv = hs_ref[...]
        dhg_v = dhg_ref[...]
        dg_ref[...] = (dhg_v * hsv * _gelu_grad(gpre)).astype(dg_ref.dtype)
        row = lax.broadcasted_iota(jnp.int32, (tr, LANE), 0)
        d = dhg_v * _gelu(gpre) + jnp.where(row == tr - 1, gcar[0:1, :], 0.0)
        b = jnp.where(row < tr - 1, pltpu.roll(a, tr - 1, 0), 0.0)
        g = _scan_bwd(b, d)
        gcar[...] = jnp.broadcast_to(a[0:1, :] * g[0:1, :], (8, LANE))
        hlast = jnp.where(tt > 0, hp_ref[7:8, :], 0.0)
        hprev = jnp.where(row == 0, hlast, pltpu.roll(hsv, 1, 0))
        dxr, dwa, dba, dwi, dbi, dlam = vjp((g * hprev, g))
        dx, sums = _conv_bwd_parts(dxr, dnext[...], xv, ext, p_ref, tr)
        dx_ref[...] = dx.astype(dx_ref.dtype)
        dnext[...] = dxr[:8, :]
        dp_ref[...] += _rows_block(sums + [dba, dbi, dlam])
        dwa_ref[0] += dwa
        dwi_ref[0] += dwi

    tile = lambda off=0: pl.BlockSpec((tr, LANE), lambda n, ti: (nt - 1 - ti, off + n))
    halo = lambda off=0: pl.BlockSpec((8, LANE), lambda n, ti: (jnp.maximum((nt - 1 - ti) * r8 - 1, 0), off + n))
    par = pl.BlockSpec((8, LANE), lambda n, ti: (0, n))
    wspec = pl.BlockSpec((1, LANE, LANE), lambda n, ti: (n, 0, 0))
    return pl.pallas_call(
        body,
        out_shape=(_sds((t, LRU_WIDTH), BF), _sds((t, LRU_WIDTH), BF), _sds((8, LRU_WIDTH), F32),
                   _sds((LRU_BLOCKS, LANE, LANE), F32), _sds((LRU_BLOCKS, LANE, LANE), F32)),
        grid=(LRU_BLOCKS, nt),
        in_specs=[tile(), halo(), tile(), par, wspec, wspec, tile(), halo(), tile()],
        out_specs=(tile(), tile(), par, wspec, wspec),
        scratch_shapes=[pltpu.VMEM((8, LANE), F32), pltpu.VMEM((8, LANE), F32)],
        compiler_params=_cp(("parallel", "arbitrary")),
        name="rg_bwd",
    )(xr_pre, xr_pre, gate_pre, rgp, w_a, w_i, hs, hs, dhg)


PACK_W = 1024
MESH_ID = pl.DeviceIdType.MESH
ANY = pl.BlockSpec(memory_space=pl.ANY)


def _my_place():
    x, y, c = lax.axis_index("x"), lax.axis_index("y"), lax.axis_index("c")
    return x, y, c


def _lin(px, py, pc):
    return 4 * px + 2 * py + pc


class Exchange:
    def __init__(self, kind, arrs):
        self.kind, self.arrs, self.k = kind, list(arrs), len(arrs)

    def out_shapes(self):
        if self.kind == "gather":
            return [_sds((N_DEV,) + a.shape, a.dtype) for a in self.arrs]
        return [_sds(a.shape, a.dtype) for a in self.arrs]

    def scratch(self):
        k = self.k
        return [pltpu.SemaphoreType.DMA((k, 7)), pltpu.SemaphoreType.DMA((k, 7)), pltpu.SemaphoreType.DMA((k,))]

    def phases(self, ins, outs, sems):
        return (self._gather if self.kind == "gather" else self._scatter)(ins, outs, *sems)

    def _gather(self, ins, outs, send_sems, recv_sems, local_sems):
        k = self.k
        x, y, c = _my_place()
        me, sibling = (x, y, c), (x, y, 1 - c)
        chips = [(1 - x, y), (x, 1 - y), (1 - x, 1 - y)]

        def copy(a, sem, block, to, from_input=False):
            slab = outs[a].at[_lin(*block)]
            return pltpu.make_async_remote_copy(
                src_ref=ins[a] if from_input else slab, dst_ref=slab,
                send_sem=send_sems.at[a, sem], recv_sem=recv_sems.at[a, sem],
                device_id=to, device_id_type=MESH_ID)

        def mine():
            return [pltpu.make_async_copy(ins[a], outs[a].at[_lin(*me)], local_sems.at[a]) for a in range(k)]

        def first():
            out = []
            for a in range(k):
                out.append(copy(a, 0, me, sibling, True))
                out += [copy(a, 1 + j, me, (*chip, c), True) for j, chip in enumerate(chips)]
            return out

        def passed():
            return [copy(a, 4 + j, (*chip, c), sibling) for j, chip in enumerate(chips) for a in range(k)]

        def start():
            for cp in mine() + first():
                cp.start()

        def middle():
            onward = passed()
            for j, chip in enumerate(chips):
                for a in range(k):
                    copy(a, 1 + j, (*chip, c), me).wait_recv()
                    onward[j * k + a].start()

        def finish():
            for a in range(k):
                copy(a, 0, sibling, me).wait_recv()
                for j, chip in enumerate(chips):
                    copy(a, 4 + j, (*chip, 1 - c), me).wait_recv()
            for cp in first() + passed():
                cp.wait_send()
            for cp in mine():
                cp.wait()

        return start, middle, finish

    def _scatter(self, ins, outs, send_sems, recv_sems, local_sems):
        k = self.k
        x, y, c = _my_place()
        me = _lin(x, y, c)
        peers = [((1 - x) if r & 4 else x, (1 - y) if r & 2 else y, (1 - c) if r & 1 else c) for r in range(1, N_DEV)]

        def copy(a, r, src_slab, dst_slab, to):
            return pltpu.make_async_remote_copy(
                src_ref=ins[a].at[src_slab], dst_ref=outs[a].at[dst_slab],
                send_sem=send_sems.at[a, r], recv_sem=recv_sems.at[a, r],
                device_id=to, device_id_type=MESH_ID)

        def mine():
            return [pltpu.make_async_copy(ins[a].at[me], outs[a].at[me], local_sems.at[a]) for a in range(k)]

        def sends():
            return [copy(a, r, _lin(*peer), me, peer) for r, peer in enumerate(peers) for a in range(k)]

        def start():
            for cp in mine() + sends():
                cp.start()

        def middle():
            pass

        def finish():
            for r, peer in enumerate(peers):
                for a in range(k):
                    copy(a, r, me, _lin(*peer), peer).wait_recv()
            for cp in sends():
                cp.wait_send()
            for cp in mine():
                cp.wait()

        return start, middle, finish

    def run(self, name):
        k = self.k

        def body(*refs):
            start, middle, finish = self.phases(refs[:k], refs[k:2 * k], refs[2 * k:])
            start()
            middle()
            finish()

        return pl.pallas_call(
            body,
            out_shape=tuple(self.out_shapes()),
            in_specs=[ANY] * k,
            out_specs=tuple(ANY for _ in range(k)),
            scratch_shapes=self.scratch(),
            name=name,
        )(*self.arrs)


def all_gather(arrs, name):
    return Exchange("gather", arrs).run(name)


def all_to_all(arrs, name):
    return Exchange("scatter", arrs).run(name)


def slab_sum(a, name):
    _, r, w = a.shape
    tr = _pick(r, (256, 128, 64, 32, 16, 8))

    def body(a_ref, o_ref):
        acc = a_ref[0].astype(F32)
        for d in range(1, N_DEV):
            acc = acc + a_ref[d].astype(F32)
        o_ref[...] = acc

    return pl.pallas_call(
        body,
        out_shape=_sds((r, w), F32),
        grid=(r // tr,),
        in_specs=[pl.BlockSpec((N_DEV, tr, w), lambda i: (0, i, 0))],
        out_specs=pl.BlockSpec((tr, w), lambda i: (i, 0)),
        compiler_params=_cp(("parallel",)),
        name=name,
    )(a)


def _adam_update(w, g, m, v):
    nm = ADAM_B1 * m + (1.0 - ADAM_B1) * g
    nv = ADAM_B2 * v + (1.0 - ADAM_B2) * (g * g)
    m_hat = nm / (1.0 - ADAM_B1 ** ADAM_STEP)
    v_hat = nv / (1.0 - ADAM_B2 ** ADAM_STEP)
    return -ADAM_LR * (m_hat / (jnp.sqrt(v_hat) + ADAM_EPS) + ADAM_WD * w), nm, nv


def adamw_blocks(w, m, v, parts, name):
    nl, r, c = w.shape
    tr = next(t for t in (256, 160, 128, 64, 32, 16) if r % t == 0 and N_DEV * t * c * 2 <= 2 * 1024 * 1024)

    def body(w_ref, m_ref, v_ref, *rest):
        part_refs, (g_ref, d_ref, nm_ref, nv_ref) = rest[:nl], rest[nl:]
        layer = pl.program_id(0)
        for idx in range(nl):
            @pl.when(layer == idx)
            def _(idx=idx):
                g = part_refs[idx][0].astype(F32)
                for dev in range(1, N_DEV):
                    g = g + part_refs[idx][dev].astype(F32)
                g_ref[...] = g
                d_ref[...], nm_ref[...], nv_ref[...] = _adam_update(w_ref[...], g, m_ref[...], v_ref[...])

    spec = pl.BlockSpec((None, tr, c), lambda l, i: (l, i, 0))
    part_spec = lambda idx: pl.BlockSpec((N_DEV, tr, c), lambda l, i: (0, jnp.where(l == idx, i, 0), 0))
    return pl.pallas_call(
        body,
        out_shape=tuple(_sds((nl, r, c), F32) for _ in range(4)),
        grid=(nl, r // tr),
        in_specs=[spec] * 3 + [part_spec(idx) for idx in range(nl)],
        out_specs=(spec,) * 4,
        compiler_params=_cp(("arbitrary", "arbitrary")),
        name=name,
    )(w, m, v, *parts)


def adamw(w, g, m, v, name):
    r, c = w.shape
    tr = _pick(r, (256, 160, 128, 64, 32, 16, 8))

    def body(w_ref, g_ref, m_ref, v_ref, d_ref, nm_ref, nv_ref):
        d_ref[...], nm_ref[...], nv_ref[...] = _adam_update(w_ref[...], g_ref[...], m_ref[...], v_ref[...])

    spec = pl.BlockSpec((tr, c), lambda i: (i, 0))
    return pl.pallas_call(
        body,
        out_shape=tuple(_sds((r, c), F32) for _ in range(3)),
        grid=(r // tr,),
        in_specs=[spec] * 4,
        out_specs=(spec, spec, spec),
        compiler_params=_cp(("parallel",)),
        name=name,
    )(w, g, m, v)


def _relu2_epi(acc):
    r = jnp.maximum(acc, 0.0)
    return r * r, r


def _drelu2_epi(acc, r):
    return (acc * (2.0 * r.astype(F32)),)


def mlp_fwd(h, g_pre, g_post, w_up, w_down, hn=None, g_next=None):
    if hn is None:
        hn = norm_fwd(h, g_pre, BF, name="mlp_norm")
    u, r = matmul(hn, w_up, "nn", (BF, BF), epi=_relu2_epi, name="mlp_up")
    d = matmul(u, w_down, "nn", name="mlp_down")
    h2, hn_next = resadd_fwd(h, d, g_post, g_next, name="mlp_res")
    return h2, (h, hn, u, r, d), hn_next


def mlp_bwd(res, dh2, g_pre, g_post, w_up, w_down, post=None, then=None):
    h, hn, u, r, d = res
    dd, dg_post = post if post is not None else norm_bwd(d, g_post, dh2, mask_pad=True, out_dtype=BF, name="mlp_post_bwd")
    dw_down = matmul(u, dd, "tn", (BF,), name="mlp_dwdown").reshape(w_down.g8.shape)
    dp = matmul(dd, w_down, "nt", (BF,), epi=_drelu2_epi, extras=(r,), name="mlp_du")
    dw_up = matmul(hn, dp, "tn", (BF,), out_blocks=True, name="mlp_dwup")
    dhn = matmul(dp, w_up, "nt", name="mlp_dhn")
    dh, dg_pre, *below = norm_bwd(h, g_pre, dhn, dres=dh2, then=then, name="mlp_pre_bwd")
    return dh, dict(mlp_pre_g=dg_pre, mlp_post_g=dg_post, w_up=dw_up, w_down=dw_down), (tuple(below) or None)


def rg_layer_fwd(h, g_pre, g_post, w_x, w_y, rgp, w_a, w_i, w_out, hn=None, g_next=None):
    if hn is None:
        hn = norm_fwd(h, g_pre, BF, name="rg_norm")
    xr = matmul(hn, w_x, "nn", name="rg_in_x")
    gp = matmul(hn, w_y, "nn", name="rg_in_y")
    hg, hs = rg_fwd(xr, gp, rgp, w_a, w_i)
    m = matmul(hg, w_out, "nn", name="rg_out")
    h2, hn_next = resadd_fwd(h, m, g_post, g_next, name="rg_res")
    return h2, (h, hn, xr, gp, hg, hs, m), hn_next


def rg_layer_bwd(res, dh2, g_pre, g_post, w_x, w_y, rgp, w_a, w_i, w_out, post=None, then=None):
    h, hn, xr, gp, hg, hs, m = res
    dm, dg_post = post if post is not None else norm_bwd(m, g_post, dh2, mask_pad=True, out_dtype=BF, name="rg_post_bwd")
    dw_out = matmul(hg, dm, "tn", name="rg_dwout")
    dhg = matmul(dm, w_out, "nt", name="rg_dhg")
    dxr, dgp, drgp, dwa, dwi = rg_bwd(xr, gp, rgp, w_a, w_i, hs, dhg)
    dw_x = matmul(hn, dxr, "tn", name="rg_dwx")
    dw_y = matmul(hn, dgp, "tn", name="rg_dwy")
    dhn = matmul([dxr, dgp], [w_x, w_y], "nt", name="rg_dhn")
    dh, dg_pre, *below = norm_bwd(h, g_pre, dhn, dres=dh2, then=then, name="rg_pre_bwd")
    grads = dict(mix_pre_g=dg_pre, mix_post_g=dg_post, rg_w_x=dw_x, rg_w_y=dw_y,
                 rg_conv_w=drgp[0:4], rg_conv_b=drgp[4], rg_b_a=drgp[5], rg_b_i=drgp[6], rg_lambda=drgp[7],
                 rg_w_a=dwa, rg_w_i=dwi, rg_w_out=dw_out)
    return dh, grads, (tuple(below) or None)


def sm_layer_fwd(h, g_pre, g_post, w_in_p, convp, dt_bias, a_log, dskip, ssd_g, q_g, w_q_p, kv_g, w_kv_p, w_out, tabs,
                 carry=None, on_carried=None, hn=None, g_next=None):
    if hn is None:
        hn = norm_fwd(h, g_pre, BF, name="sm_norm")
    proj = matmul(hn, w_in_p, "nn", name="sm_in")
    xbc_c = conv_silu_fwd(proj, OFF_XBC, SSD_CONV_CH, convp, name="ssd_conv")
    y, hst = ssd_fwd(xbc_c, proj, dt_bias, a_log, dskip)
    y_ssd = gated_norm_fwd(y, proj, ssd_g)
    cqn = norm_fwd(proj, q_g, BF, col_blk=OFF_CQ // MLA_Q_RANK, width=MLA_Q_RANK, name="q_norm")
    q_raw = matmul(cqn, w_q_p, "nn", name="q_up")
    ckvn = norm_fwd(proj, kv_g, BF, col_blk=OFF_CKV // MLA_KV_RANK, width=MLA_KV_RANK, name="kv_norm")
    kv_raw = matmul(ckvn, w_kv_p, "nn", name="kv_up")
    q_cat, k_cat, v = rope_fwd(q_raw, kv_raw, proj, tabs)
    o, lse, carried = attn_fwd(q_cat, k_cat, v, carry)
    if on_carried is not None:
        on_carried(carried)
    w_out = w_out()
    half = w_out.shape[0] // 2
    m = matmul([y_ssd, o], [KBlock(w_out, half, 0), KBlock(w_out, half, 1)], "nn", name="sm_out")
    res = (h, hn, proj, xbc_c, y, hst, cqn, ckvn, q_cat, k_cat, v, o, lse, y_ssd, m)
    h2, hn_next = resadd_fwd(h, m, g_post, g_next, name="sm_res")
    return h2, res, hn_next


def sm_layer_bwd(res, dh2, g_pre, g_post, w_in_p, convp, dt_bias, a_log, dskip, ssd_g, q_g, w_q_p, kv_g, w_kv_p, w_out, tabs,
                 carry=None, post=None, then=None):
    h, hn, proj, xbc_c, y, hst, cqn, ckvn, q_cat, k_cat, v, o, lse, y_ssd, m = res
    w_out = w_out()
    dm, dg_post = post if post is not None else norm_bwd(m, g_post, dh2, mask_pad=True, out_dtype=BF, name="sm_post_bwd")
    dw_out = jnp.concatenate([matmul(y_ssd, dm, "tn", name="sm_dwout_ssd"), matmul(o, dm, "tn", name="sm_dwout_att")], axis=0)
    dyab = matmul(dm, w_out, "nt", name="sm_dyab")
    dq_cat, dk_cat, dv, carried = attn_bwd(q_cat, k_cat, v, o, lse, dyab, carry(dw_out) if carry is not None else None)
    dq_raw, dkr = rope_bwd(dq_cat, dk_cat, tabs)
    kw = MLA_HEADS * LANE
    dw_kv_p = jnp.concatenate([matmul(ckvn, dk_cat, "tn", name="kv_dw_k"), matmul(ckvn, dv, "tn", name="kv_dw_v")], axis=1)
    dckvn = matmul([dk_cat, dv], [KBlock(w_kv_p, kw, 0), KBlock(w_kv_p, kw // 2, 2)], "nt", name="kv_dx")
    dckv, dg_kv = norm_bwd(proj, kv_g, dckvn, out_dtype=BF, col_blk=OFF_CKV // MLA_KV_RANK, width=MLA_KV_RANK,
                           name="kv_norm_bwd")
    dw_q_p = matmul(cqn, dq_raw, "tn", name="q_dw")
    dcqn = matmul(dq_raw, w_q_p, "nt", name="q_dx")
    dcq, dg_q = norm_bwd(proj, q_g, dcqn, out_dtype=BF, col_blk=OFF_CQ // MLA_Q_RANK, width=MLA_Q_RANK, name="q_norm_bwd")
    dy, dz, dg_ssd = gated_norm_bwd(y, proj, ssd_g, dyab)
    dxbc_c, ddt, dpar = ssd_bwd(xbc_c, proj, dt_bias, a_log, dskip, hst, dy)
    dxbc, dconvp = conv_silu_bwd(proj, OFF_XBC, SSD_CONV_CH, convp, dxbc_c, name="ssd_conv_bwd")
    pieces = [dz, dxbc, dckv, ddt, dkr, dcq]
    dw_in_p = jnp.concatenate([matmul(hn, pc, "tn", (BF,), name="sm_dwin_%d" % i) for i, pc in enumerate(pieces)], axis=1)
    third = SSD_CONV_CH // 3
    a_terms = [dz] + [KBlock(dxbc, third, i) for i in range(3)] + [dckv, ddt, dkr, dcq]
    b_terms = ([KBlock(w_in_p, SSD_D_INNER, 0)] + [KBlock(w_in_p, third, OFF_XBC // third + i) for i in range(3)]
               + [KBlock(w_in_p, MLA_KV_RANK, OFF_CKV // MLA_KV_RANK), KBlock(w_in_p, LANE, OFF_DT // LANE),
                  KBlock(w_in_p, LANE, OFF_KR // LANE), KBlock(w_in_p, MLA_Q_RANK, OFF_CQ // MLA_Q_RANK)])
    dhn = matmul(a_terms, b_terms, "nt", name="sm_dhn")
    dh, dg_pre, *below = norm_bwd(h, g_pre, dhn, dres=dh2, then=then, name="sm_pre_bwd")
    grads = dict(mix_pre_g=dg_pre, mix_post_g=dg_post, w_in=w_in_cols_to_blocks(dw_in_p), ssd_conv_w=dconvp[0:4],
                 ssd_conv_b=dconvp[4], ssd_dt_bias=dpar[0, :SSD_HEADS], ssd_a_log=dpar[1, :SSD_HEADS],
                 ssd_d=dpar[2, :SSD_HEADS], ssd_norm_g=dg_ssd, mla_q_norm_g=dg_q, mla_w_q_up=_unpack_w_q(dw_q_p),
                 mla_kv_norm_g=dg_kv, mla_w_kv_up=_unpack_w_kv(dw_kv_p), w_out_ab=dw_out)
    return dh, grads, carried, (tuple(below) or None)


W_IN_COLS = 3248
W_IN_SHARD = W_IN_COLS // N_DEV
W_IN_WIRE = 512


def _w_in_tables():
    src = np.full((IN_W,), -1, np.int64)
    src[0:2560] = np.arange(2560)
    src[OFF_CKV:OFF_CKV + 256] = 2960 + np.arange(256)
    src[OFF_DT:OFF_DT + 16] = 2560 + np.arange(16)
    src[OFF_KR + 64:OFF_KR + 96] = 3216 + np.arange(32)
    src[OFF_CQ:OFF_CQ + 384] = 2576 + np.arange(384)
    dev = np.where(src >= 0, src // W_IN_SHARD, -1).astype(np.int32).reshape(1, IN_W)
    col = np.where(src >= 0, src % W_IN_SHARD, 0).astype(np.int32).reshape(1, IN_W)
    return dev, col


W_IN_TILE = 384


def _w_in_devices_of_tile(dev):
    return [sorted(set(dev[0, t * W_IN_TILE:(t + 1) * W_IN_TILE].tolist()) - {-1}) for t in range(IN_W // W_IN_TILE)]


def _any_of(index, values):
    cond = index == values[0]
    for v in values[1:]:
        cond = cond | (index == v)
    return cond


def w_in_blocks_to_cols(g8):
    _, k, wp = g8.shape
    tn = W_IN_TILE
    dev, col = _w_in_tables()
    holders = _w_in_devices_of_tile(dev)

    def body(g_ref, dev_ref, col_ref, o_ref):
        i = pl.program_id(0)
        row = lax.broadcasted_iota(jnp.int32, (wp, tn), 0)
        o_ref[...] = jnp.zeros_like(o_ref)
        for j in range(N_DEV):
            tiles = [t for t, devs in enumerate(holders) if j in devs]
            if tiles:
                @pl.when(_any_of(i, tiles))
                def _(j=j):
                    sel = ((row == col_ref[...]) & (dev_ref[...] == j)).astype(BF)
                    o_ref[...] += jnp.dot(g_ref[j], sel, preferred_element_type=F32).astype(o_ref.dtype)

    dev, col = jnp.asarray(dev), jnp.asarray(col)
    return pl.pallas_call(
        body,
        out_shape=_sds((k, IN_W), BF),
        grid=(IN_W // tn,),
        in_specs=[pl.BlockSpec((N_DEV, k, wp), lambda i: (0, 0, 0)), pl.BlockSpec((1, tn), lambda i: (0, i)),
                  pl.BlockSpec((1, tn), lambda i: (0, i))],
        out_specs=pl.BlockSpec((k, tn), lambda i: (0, i)),
        compiler_params=_cp(("parallel",)),
        name="w_in_cols",
    )(g8, dev, col)


def w_in_cols_to_blocks(dw):
    k = dw.shape[0]
    tn = W_IN_TILE
    dev, col = _w_in_tables()
    holders = _w_in_devices_of_tile(dev)

    def body(dw_ref, dev_ref, col_ref, o_ref):
        j = pl.program_id(0)
        row = lax.broadcasted_iota(jnp.int32, (W_IN_WIRE, tn), 0)
        o_ref[...] = jnp.zeros_like(o_ref)
        for t, devs in enumerate(holders):
            if devs:
                @pl.when(_any_of(j, devs))
                def _(t=t):
                    cols = slice(t * tn, (t + 1) * tn)
                    sel = ((row == col_ref[:, cols]) & (dev_ref[:, cols] == j)).astype(BF)
                    o_ref[0] += lax.dot_general(dw_ref[:, cols], sel, NT_DIMS,
                                                preferred_element_type=F32).astype(o_ref.dtype)

    dev, col = jnp.asarray(dev), jnp.asarray(col)
    return pl.pallas_call(
        body,
        out_shape=_sds((N_DEV, k, W_IN_WIRE), BF),
        grid=(N_DEV,),
        in_specs=[pl.BlockSpec((k, IN_W), lambda j: (0, 0)), pl.BlockSpec((1, IN_W), lambda j: (0, 0)),
                  pl.BlockSpec((1, IN_W), lambda j: (0, 0))],
        out_specs=pl.BlockSpec((1, k, W_IN_WIRE), lambda j: (j, 0, 0)),
        compiler_params=_cp(("parallel",)),
        name="w_in_blocks",
    )(dw, dev, col)


def _pack_w_q(w):
    w3 = w.reshape(w.shape[0], MLA_HEADS, MLA_NOPE + MLA_ROPE)
    return jnp.pad(w3, ((0, 0), (0, 0), (0, LANE - MLA_NOPE - MLA_ROPE))).reshape(w.shape[0], MLA_HEADS * LANE)


def _unpack_w_q(p):
    return p.reshape(p.shape[0], MLA_HEADS, LANE)[:, :, :MLA_NOPE + MLA_ROPE].reshape(p.shape[0], -1)


def _pack_w_kv(w):
    w3 = w.reshape(w.shape[0], MLA_HEADS, MLA_NOPE + MLA_V)
    k = jnp.pad(w3[:, :, :MLA_NOPE], ((0, 0), (0, 0), (0, LANE - MLA_NOPE))).reshape(w.shape[0], MLA_HEADS * LANE)
    return jnp.concatenate([k, w3[:, :, MLA_NOPE:].reshape(w.shape[0], MLA_HEADS * MLA_V)], axis=1)


def _unpack_w_kv(p):
    k = p[:, :MLA_HEADS * LANE].reshape(p.shape[0], MLA_HEADS, LANE)[:, :, :MLA_NOPE]
    v = p[:, MLA_HEADS * LANE:].reshape(p.shape[0], MLA_HEADS, MLA_V)
    return jnp.concatenate([k, v], axis=2).reshape(p.shape[0], -1)


def _rows8(rows, width):
    a = jnp.concatenate([r.reshape(-1, width) for r in rows], axis=0)
    return jnp.pad(a, ((0, 8 - a.shape[0]), (0, 0)))


SLAB_ROWS = 16


def _to_slab(flat_list, lead=()):
    cat = jnp.concatenate(flat_list, axis=-1)
    n = cat.shape[-1]
    unit = SLAB_ROWS * PACK_W
    total = -(-n // unit) * unit
    cat = jnp.pad(cat, [(0, 0)] * len(lead) + [(0, total - n)])
    return cat.reshape(lead + (total // PACK_W, PACK_W))


def _from_flat(flat, shapes):
    out, off = [], 0
    for s in shapes:
        n = int(np.prod(s))
        out.append(flat[off:off + n].reshape(s))
        off += n
    return out


def _gathered_full(g8, axis):
    moved = jnp.moveaxis(g8, 0, axis)
    shp = moved.shape
    return moved.reshape(shp[:axis] + (shp[axis] * shp[axis + 1],) + shp[axis + 2:])


def _per_device(full, axis):
    shp = full.shape
    split = full.reshape(shp[:axis] + (N_DEV, shp[axis] // N_DEV) + shp[axis + 1:])
    return jnp.moveaxis(split, axis, 0)


ARG_NAMES = ['x', 'meta_tokens', 'mix_pre_g', 'mix_post_g', 'mlp_pre_g', 'mlp_post_g', 'w_up', 'w_down', 'w_in',
             'ssd_conv_w', 'ssd_conv_b', 'ssd_dt_bias', 'ssd_a_log', 'ssd_d', 'ssd_norm_g', 'mla_q_norm_g',
             'mla_w_q_up', 'mla_kv_norm_g', 'mla_w_kv_up', 'w_out_ab', 'rg_w_x', 'rg_w_y', 'rg_conv_w', 'rg_conv_b',
             'rg_w_a', 'rg_b_a', 'rg_w_i', 'rg_b_i', 'rg_lambda', 'rg_w_out']
WEIGHTS = ARG_NAMES[1:]
BIG = {'w_up': 2, 'w_down': 1, 'w_in': 2, 'mla_w_q_up': 2, 'mla_w_kv_up': 2, 'w_out_ab': 1, 'rg_w_x': 2,
       'rg_w_y': 2, 'rg_w_out': 1}
SMALL = {'meta_tokens': 1, 'ssd_conv_w': 2, 'rg_conv_w': 2, 'rg_conv_b': 1, 'rg_b_a': 1, 'rg_b_i': 1, 'rg_lambda': 1}
REPL = [n for n in WEIGHTS if n not in BIG and n not in SMALL]
REPL_MEDIUM = ['rg_w_a', 'rg_w_i']
REPL_TINY = [n for n in REPL if n not in REPL_MEDIUM]


def _piece_axes():
    axes = {}
    for n, ax in BIG.items():
        for i in range(DEPTH if n in ('w_up', 'w_down') else DEPTH // 2):
            axes[(n, i)] = ax - 1
    return axes


PIECE_AXIS = _piece_axes()
AS_BLOCKS = ('w_up', 'w_down')
_RG = lambda i: [(n, i) for n in ('rg_w_x', 'rg_w_y', 'rg_w_out')]
_MLP = lambda l: [('w_up', l), ('w_down', l)]
_SM_IN = lambda i: [(n, i) for n in ('w_in', 'mla_w_q_up', 'mla_w_kv_up')]
GATHER_FIRST = _SM_IN(0)
GATHER_AT = {0: [('w_out_ab', 0)] + _MLP(0) + _RG(0) + _MLP(1) + _SM_IN(1), 2: [('w_out_ab', 1)] + _MLP(2) + _RG(1) + _MLP(3)}
SCATTER_AT = {2: _MLP(3) + _RG(1) + _MLP(2) + [('w_out_ab', 1)],
              0: _SM_IN(1) + _MLP(1) + _RG(0) + _MLP(0) + [('w_out_ab', 0)]}
SCATTER_LAST = _SM_IN(0)


def _wire_block(p, key):
    n, i = key
    blk = p[n][i]
    if n == 'w_in':
        blk = jnp.pad(blk, ((0, 0), (0, W_IN_WIRE - blk.shape[1])))
    return blk


def _step(p, moments):
    assert DEPTH == 4
    full = {n: [None] * p[n].shape[0] for n in BIG}
    full['w_in_g'] = [None] * p['w_in'].shape[0]

    def weight_blocks(group):
        return [_wire_block(p, k).astype(BF) for k in group]

    def take_weights(group, gathered):
        for (n, i), piece in zip(group, gathered):
            if n == 'w_in':
                full['w_in_g'][i] = piece
            elif n in AS_BLOCKS:
                full[n][i] = DevBlocks(piece, PIECE_AXIS[(n, i)])
            else:
                full[n][i] = _gathered_full(piece, PIECE_AXIS[(n, i)])

    def grad_blocks(group, gw):
        return [gw[k] if k[0] in AS_BLOCKS or k[0] == 'w_in' else _per_device(gw[k], PIECE_AXIS[k]).astype(BF)
                for k in group]

    parts = {}

    small_slab = _to_slab([p[n].reshape(-1) for n in SMALL])
    *first, small8 = all_gather(weight_blocks(GATHER_FIRST) + [small_slab], name="gather_first")
    take_weights(GATHER_FIRST, first)
    for n, piece in zip(SMALL, _from_flat_rows(small8, [p[n].shape for n in SMALL])):
        full[n] = _gathered_full(piece, SMALL[n])
    for n in REPL:
        full[n] = p[n]
    loss_local, grad_x, gw, gsmall_full, carried = _local_step(
        full, p['x'][0], p['loss_target'][0],
        fwd_carry=lambda layer: Exchange("gather", weight_blocks(GATHER_AT[layer])),
        on_fwd_carried=lambda layer, got: take_weights(GATHER_AT[layer], got),
        bwd_carry=lambda layer, gw_now, others: Exchange(
            "scatter", grad_blocks(SCATTER_AT[layer], gw_now)
            + ([jnp.stack(others[n], axis=0).reshape(N_DEV, -1, LANE) for n in REPL_MEDIUM] if layer == 0 else [])))

    for layer, group in SCATTER_AT.items():
        parts.update(zip(group, carried[layer]))
    rep_flat = jnp.concatenate([gsmall_full[n].reshape(-1) for n in REPL_TINY])
    rep_n = rep_flat.shape[0]
    rep_chunk = -(-rep_n // (N_DEV * PACK_W * 8)) * PACK_W * 8
    rep8 = jnp.pad(rep_flat, (0, N_DEV * rep_chunk - rep_n)).reshape(N_DEV, rep_chunk)
    gsmall = _to_slab([_per_device(gsmall_full[n], SMALL[n]).reshape(N_DEV, -1) for n in SMALL] + [rep8], lead=(N_DEV,))
    received = all_to_all(grad_blocks(SCATTER_LAST, gw) + [gsmall], name="scatter_last")
    n_last = len(SCATTER_LAST)
    parts.update(zip(SCATTER_LAST, received[:n_last]))
    ssmall = slab_sum(received[n_last], name="sum_small").reshape(-1)
    medium_mine = [slab_sum(r8, name="sum_" + n) for n, r8 in zip(REPL_MEDIUM, carried[0][len(SCATTER_AT[0]):])]
    g_loc = {'w_in': jnp.stack([slab_sum(parts[('w_in', i)], name="sum_w_in_%d" % i)[:, :W_IN_SHARD]
                                for i in range(p['w_in'].shape[0])], axis=0)}
    small_n = sum(int(np.prod(p[n].shape)) for n in SMALL)
    g_loc.update(zip(SMALL, _from_flat(ssmall, [p[n].shape for n in SMALL])))
    rep_mine = ssmall[small_n:small_n + rep_chunk].reshape(-1, PACK_W)
    rep_all, *medium_all = all_gather([rep_mine] + medium_mine, name="gather_replicated")
    g_loc.update(zip(REPL_TINY, _from_flat(rep_all.reshape(-1), [p[n].shape for n in REPL_TINY])))
    g_loc.update({n: g.reshape(p[n].shape) for n, g in zip(REPL_MEDIUM, medium_all)})

    out = {'loss': lax.psum(loss_local, ("x", "y", "c")), 'grad_x': grad_x[None]}
    small_names = list(SMALL) + REPL_TINY
    for n in list(BIG) + REPL_MEDIUM:
        shp = p[n].shape
        if n == 'w_in' or n in REPL_MEDIUM:
            v2 = lambda a: a.reshape(-1, shp[-1])
            d, nm, nv = adamw(v2(p[n]), v2(g_loc[n]), v2(moments['m_' + n]), v2(moments['v_' + n]), name="adamw_" + n)
            d, nm, nv = d.reshape(shp), nm.reshape(shp), nv.reshape(shp)
        else:
            g_loc[n], d, nm, nv = adamw_blocks(p[n], moments['m_' + n], moments['v_' + n],
                                               [parts[(n, i)] for i in range(shp[0])], name="adamw_" + n)
        out['delta_' + n], out['new_m_' + n], out['new_v_' + n] = d, nm, nv
    slab = lambda src: _to_slab([src(n).reshape(-1) for n in small_names])
    d, nm, nv = adamw(slab(lambda n: p[n]), slab(lambda n: g_loc[n]), slab(lambda n: moments['m_' + n]),
                      slab(lambda n: moments['v_' + n]), name="adamw_small")
    shapes = [p[n].shape for n in small_names]
    for key, flat in (('delta_', d), ('new_m_', nm), ('new_v_', nv)):
        for n, a in zip(small_names, _from_flat(flat.reshape(-1), shapes)):
            out[key + n] = a
    for n in WEIGHTS:
        out['grad_' + n] = g_loc[n]
    return out


def _local_step(full, x, target_rows, fwd_carry=None, on_fwd_carried=None, bwd_carry=None):
    t = PAD + N_META + x.shape[0]
    h = jnp.concatenate([jnp.zeros((PAD, D_MODEL), F32), full['meta_tokens'], x], axis=0)
    target = jnp.concatenate([jnp.zeros((PAD + N_META, D_MODEL), F32), target_rows], axis=0)
    tabs = rope_tables(t)

    def layer_args(layer):
        i = layer // 2
        if layer % 2 == 0:
            convp = _rows8([full['ssd_conv_w'][i], full['ssd_conv_b'][i]], SSD_CONV_CH)
            return (full['mix_pre_g'][layer], full['mix_post_g'][layer], w_in_blocks_to_cols(full['w_in_g'][i]), convp,
                    full['ssd_dt_bias'][i], full['ssd_a_log'][i], full['ssd_d'][i], full['ssd_norm_g'][i],
                    full['mla_q_norm_g'][i], _pack_w_q(full['mla_w_q_up'][i]), full['mla_kv_norm_g'][i],
                    _pack_w_kv(full['mla_w_kv_up'][i]), lambda: full['w_out_ab'][i], tabs)
        rgp = _rows8([full['rg_conv_w'][i], full['rg_conv_b'][i], full['rg_b_a'][i], full['rg_b_i'][i],
                      full['rg_lambda'][i]], LRU_WIDTH)
        return (full['mix_pre_g'][layer], full['mix_post_g'][layer], full['rg_w_x'][i], full['rg_w_y'][i], rgp,
                full['rg_w_a'][i], full['rg_w_i'][i], full['rg_w_out'][i])

    def mlp_args(layer):
        return (full['mlp_pre_g'][layer], full['mlp_post_g'][layer], full['w_up'][layer], full['w_down'][layer])

    saved = []
    hn = None
    for layer in range(DEPTH):
        la = layer_args(layer)
        to_mlp = dict(hn=hn, g_next=full['mlp_pre_g'][layer])
        if layer % 2 == 0:
            if fwd_carry is not None:
                h, res_mix, hn = sm_layer_fwd(h, *la, carry=fwd_carry(layer),
                                              on_carried=lambda got, layer=layer: on_fwd_carried(layer, got), **to_mlp)
            else:
                h, res_mix, hn = sm_layer_fwd(h, *la, **to_mlp)
        else:
            h, res_mix, hn = rg_layer_fwd(h, *la, **to_mlp)
        ma = mlp_args(layer)
        h, res_mlp, hn = mlp_fwd(h, *ma, hn=hn, g_next=full['mix_pre_g'][layer + 1] if layer + 1 < DEPTH else None)
        saved.append((la, ma, res_mix, res_mlp))
    loss_local, dh = loss_fwd_bwd(h, target)

    others = {n: [None] * len(full[n]) for n in WEIGHTS if n not in BIG and n != 'meta_tokens'}
    gw, carried = {}, {}
    post = None
    for layer in reversed(range(DEPTH)):
        la, ma, res_mix, res_mlp = saved[layer]
        dh, gm, post = mlp_bwd(res_mlp, dh, *ma, post=post, then=(res_mix[-1], la[1]))
        below = (saved[layer - 1][3][-1], saved[layer - 1][1][1]) if layer > 0 else None
        if layer % 2 == 0:
            for n in ('w_up', 'w_down'):
                gw[(n, layer)] = gm[n]
            carry = None
            if bwd_carry is not None:
                carry = lambda dw_out, layer=layer: bwd_carry(layer, {**gw, ('w_out_ab', layer // 2): dw_out}, others)
            dh, gx, carried[layer], post = sm_layer_bwd(res_mix, dh, *la, carry=carry, post=post, then=below)
        else:
            dh, gx, post = rg_layer_bwd(res_mix, dh, *la, post=post, then=below)
        for n, g in list(gm.items()) + list(gx.items()):
            i = layer if n in ('mix_pre_g', 'mix_post_g', 'mlp_pre_g', 'mlp_post_g', 'w_up', 'w_down') else layer // 2
            if n in BIG:
                gw[(n, i)] = g
            else:
                others[n][i] = g
    gothers = {n: jnp.stack(v, axis=0) for n, v in others.items()}
    gothers['meta_tokens'] = dh[PAD:PAD + N_META]
    return loss_local, dh[PAD + N_META:], gw, gothers, carried


def _from_flat_rows(g8, shapes):
    flat = g8.reshape(N_DEV, -1)
    out, off = [], 0
    for s in shapes:
        n = int(np.prod(s))
        out.append(flat[:, off:off + n].reshape((N_DEV,) + tuple(s)))
        off += n
    return out


def kernel(x, meta_tokens, mix_pre_g, mix_post_g, mlp_pre_g, mlp_post_g, w_up, w_down, w_in, ssd_conv_w, ssd_conv_b, ssd_dt_bias, ssd_a_log, ssd_d, ssd_norm_g, mla_q_norm_g, mla_w_q_up, mla_kv_norm_g, mla_w_kv_up, w_out_ab, rg_w_x, rg_w_y, rg_conv_w, rg_conv_b, rg_w_a, rg_b_a, rg_w_i, rg_b_i, rg_lambda, rg_w_out, loss_target, m_meta_tokens, m_mix_pre_g, m_mix_post_g, m_mlp_pre_g, m_mlp_post_g, m_w_up, m_w_down, m_w_in, m_ssd_conv_w, m_ssd_conv_b, m_ssd_dt_bias, m_ssd_a_log, m_ssd_d, m_ssd_norm_g, m_mla_q_norm_g, m_mla_w_q_up, m_mla_kv_norm_g, m_mla_w_kv_up, m_w_out_ab, m_rg_w_x, m_rg_w_y, m_rg_conv_w, m_rg_conv_b, m_rg_w_a, m_rg_b_a, m_rg_w_i, m_rg_b_i, m_rg_lambda, m_rg_w_out, v_meta_tokens, v_mix_pre_g, v_mix_post_g, v_mlp_pre_g, v_mlp_post_g, v_w_up, v_w_down, v_w_in, v_ssd_conv_w, v_ssd_conv_b, v_ssd_dt_bias, v_ssd_a_log, v_ssd_d, v_ssd_norm_g, v_mla_q_norm_g, v_mla_w_q_up, v_mla_kv_norm_g, v_mla_w_kv_up, v_w_out_ab, v_rg_w_x, v_rg_w_y, v_rg_conv_w, v_rg_conv_b, v_rg_w_a, v_rg_b_a, v_rg_w_i, v_rg_b_i, v_rg_lambda, v_rg_w_out):
    args = (x, meta_tokens, mix_pre_g, mix_post_g, mlp_pre_g, mlp_post_g, w_up, w_down, w_in, ssd_conv_w, ssd_conv_b, ssd_dt_bias, ssd_a_log, ssd_d, ssd_norm_g, mla_q_norm_g, mla_w_q_up, mla_kv_norm_g, mla_w_kv_up, w_out_ab, rg_w_x, rg_w_y, rg_conv_w, rg_conv_b, rg_w_a, rg_b_a, rg_w_i, rg_b_i, rg_lambda, rg_w_out, loss_target, m_meta_tokens, m_mix_pre_g, m_mix_post_g, m_mlp_pre_g, m_mlp_post_g, m_w_up, m_w_down, m_w_in, m_ssd_conv_w, m_ssd_conv_b, m_ssd_dt_bias, m_ssd_a_log, m_ssd_d, m_ssd_norm_g, m_mla_q_norm_g, m_mla_w_q_up, m_mla_kv_norm_g, m_mla_w_kv_up, m_w_out_ab, m_rg_w_x, m_rg_w_y, m_rg_conv_w, m_rg_conv_b, m_rg_w_a, m_rg_b_a, m_rg_w_i, m_rg_b_i, m_rg_lambda, m_rg_w_out, v_meta_tokens, v_mix_pre_g, v_mix_post_g, v_mlp_pre_g, v_mlp_post_g, v_w_up, v_w_down, v_w_in, v_ssd_conv_w, v_ssd_conv_b, v_ssd_dt_bias, v_ssd_a_log, v_ssd_d, v_ssd_norm_g, v_mla_q_norm_g, v_mla_w_q_up, v_mla_kv_norm_g, v_mla_w_kv_up, v_w_out_ab, v_rg_w_x, v_rg_w_y, v_rg_conv_w, v_rg_conv_b, v_rg_w_a, v_rg_b_a, v_rg_w_i, v_rg_b_i, v_rg_lambda, v_rg_w_out,)
    n_w = len(ARG_NAMES)
    p = dict(zip(ARG_NAMES, args[:n_w]))
    p['loss_target'] = args[n_w]
    moments = {}
    for i, n in enumerate(WEIGHTS):
        moments['m_' + n] = args[n_w + 1 + i]
        moments['v_' + n] = args[n_w + 1 + len(WEIGHTS) + i]
    out = _step(p, moments)
    res = [out['loss'], out['grad_x']]
    for prefix in ('grad_', 'delta_', 'new_m_', 'new_v_'):
        res += [out[prefix + n] for n in WEIGHTS]
    return tuple(res)
```

```python
import math

import numpy as np
import jax
import jax.numpy as jnp
from jax import lax
from jax.experimental import pallas as pl
from jax.experimental.pallas import tpu as pltpu

F32 = jnp.float32
BF = jnp.bfloat16
HI = lax.Precision.HIGHEST

D_MODEL = 1024
DEPTH = 4
N_META = 16
CHUNK = 128
PAD = CHUNK - N_META
EPS = 1e-6
SSD_HEADS = 16
SSD_HEAD_DIM = 64
SSD_D_INNER = 1024
SSD_STATE = 128
SSD_CONV_CH = 1536
MLA_HEADS = 16
MLA_NOPE = 64
MLA_ROPE = 32
MLA_V = 64
MLA_Q_RANK = 384
MLA_KV_RANK = 256
ROPE_BASE = 10000.0
LRU_WIDTH = 1280
LRU_BLOCKS = 10
LRU_C = 8.0
D_FF = 4096
N_DEV = 8
LANE = 128
IN_W = 3456
OFF_Z, OFF_XBC, OFF_CKV, OFF_DT, OFF_KR, OFF_CQ = 0, 1024, 2560, 2816, 2944, 3072

ADAM_LR = 0.001
ADAM_B1 = 0.9
ADAM_B2 = 0.999
ADAM_EPS = 1e-08
ADAM_WD = 0.01
ADAM_STEP = 10

VMEM_LIMIT = 56 * 1024 * 1024
NEG = -1e30


def _pick(n, cands):
    for c in cands:
        if n % c == 0:
            return c
    return n


def _cp(sem=None):
    return pltpu.CompilerParams(dimension_semantics=sem, vmem_limit_bytes=VMEM_LIMIT)


def _sds(shape, dtype):
    return jax.ShapeDtypeStruct(tuple(shape), dtype)


def _silu(x):
    return x * jax.nn.sigmoid(x)


def _softplus(x):
    return jnp.maximum(x, 0.0) + jnp.log(1.0 + jnp.exp(-jnp.abs(x)))


def _gelu(x):
    c = math.sqrt(2.0 / math.pi)
    return 0.5 * x * (1.0 + jnp.tanh(c * (x + 0.044715 * (x * x * x))))


def _row_mask(i, tr, shape, first_valid=PAD):
    row = i * tr + lax.broadcasted_iota(jnp.int32, shape, 0)
    return row >= first_valid


class KBlock:
    def __init__(self, arr, width, blk):
        self.arr, self.width, self.blk = arr, width, blk


class DevBlocks:
    def __init__(self, g8, axis):
        self.g8, self.axis = g8, axis
        _, r, c = g8.shape
        self.shape = (N_DEV * r, c) if axis == 0 else (r, N_DEV * c)


NN_DIMS = (((1,), (0,)), ((), ()))
MM_TALL_K = 1536
MM_WHOLE_K, MM_WHOLE_M = 1024, 4224
MM_TALL_A_BYTES = 36 * 1024 * 1024


def matmul(a, b, mode, out_dtypes=(F32,), epi=None, extras=(), name="mm", tm=None, tn=None, out_blocks=False):
    a_terms = a if isinstance(a, (list, tuple)) else [a]
    b_terms = b if isinstance(b, (list, tuple)) else [b]
    assert len(a_terms) == len(b_terms) and (mode != "tn" or len(a_terms) == 1)
    arr_of = lambda t: t.arr if isinstance(t, KBlock) else t
    if mode == "tn":
        m, n = a_terms[0].shape[1], b_terms[0].shape[1]
    else:
        m = arr_of(a_terms[0]).shape[0]
        b0 = b_terms[0]
        n = (b0.shape if isinstance(b0, DevBlocks) else arr_of(b0).shape)[1 if mode == "nn" else 0]
    if mode == "tn":
        tm = _pick(m, (1024, 512, 384, 256, 128))
    else:
        k_all = sum(t.width if isinstance(t, KBlock) else t.shape[1] for t in a_terms)
        tall = (2112,) if k_all <= MM_TALL_K else ()
        row_bytes = sum((t.width if isinstance(t, KBlock) else t.shape[1]) * arr_of(t).dtype.itemsize for t in a_terms)
        if k_all <= MM_WHOLE_K and m <= MM_WHOLE_M and n % 256 == 0 and tm is None and tn is None:
            tm, tn = m, 256
        elif (k_all > MM_TALL_K and m % 2112 == 0 and n % 256 == 0 and tm is None and tn is None
              and 2 * 2112 * row_bytes <= MM_TALL_A_BYTES):
            tm, tn = 2112, 256
        tm = tm or _pick(m, tall + (1056, 1024, 768, 640, 512, 384, 256, 128))
    tn = tn or _pick(n, (512, 640, 384, 256, 128))
    dims = {"nn": NN_DIMS, "nt": NT_DIMS, "tn": TN_DIMS}[mode]

    in_specs, args, plan = [], [], []
    for ta, tb in zip(a_terms, b_terms):
        if mode == "tn":
            k = ta.shape[0]
            in_specs += [pl.BlockSpec((k, tm), lambda i, j: (0, i)), pl.BlockSpec((k, tn), lambda i, j: (0, j))]
            args += [ta, tb]
            plan.append(None)
            continue
        if isinstance(ta, KBlock):
            kw, ka = ta.width, ta.blk
            in_specs.append(pl.BlockSpec((tm, kw), lambda i, j, ka=ka: (i, ka)))
        else:
            kw = ta.shape[1]
            in_specs.append(pl.BlockSpec((tm, kw), lambda i, j: (i, 0)))
        args.append(arr_of(ta))
        if isinstance(tb, DevBlocks):
            _, r, c = tb.g8.shape
            split_k = tb.axis == (0 if mode == "nn" else 1)
            if split_k:
                kd = r if mode == "nn" else c
                assert kw == N_DEV * kd
                blk = (N_DEV, kd, tn) if mode == "nn" else (N_DEV, tn, kd)
                in_specs.append(pl.BlockSpec(blk, (lambda i, j: (0, 0, j)) if mode == "nn" else (lambda i, j: (0, j, 0))))
                plan.append(kd)
            else:
                per = (c if mode == "nn" else r) // tn
                blk = (None, kw, tn) if mode == "nn" else (None, tn, kw)
                in_specs.append(pl.BlockSpec(blk, (lambda i, j, per=per: (j // per, 0, j % per)) if mode == "nn"
                                             else (lambda i, j, per=per: (j // per, j % per, 0))))
                plan.append(None)
            args.append(tb.g8)
        else:
            kb = tb.blk if isinstance(tb, KBlock) else 0
            assert (tb.width if isinstance(tb, KBlock) else tb.shape[0 if mode == "nn" else 1]) == kw
            in_specs.append(pl.BlockSpec((kw, tn), lambda i, j, kb=kb: (kb, j)) if mode == "nn"
                            else pl.BlockSpec((tn, kw), lambda i, j, kb=kb: (j, kb)))
            args.append(arr_of(tb))
            plan.append(None)
    n_terms, n_ex = len(plan), len(extras)

    def body(*refs):
        ex_refs, out_refs = refs[2 * n_terms:2 * n_terms + n_ex], refs[2 * n_terms + n_ex:]
        acc = None
        for t, kd in enumerate(plan):
            a_ref, b_ref = refs[2 * t], refs[2 * t + 1]
            if kd is None:
                parts = [lax.dot_general(a_ref[...].astype(BF), b_ref[...].astype(BF), dims, preferred_element_type=F32)]
            else:
                parts = [lax.dot_general(a_ref[:, d * kd:(d + 1) * kd].astype(BF), b_ref[d].astype(BF), dims,
                                         preferred_element_type=F32) for d in range(N_DEV)]
            for part in parts:
                acc = part if acc is None else acc + part
        outs = (acc,) if epi is None else epi(acc, *[r[...] for r in ex_refs])
        for r, o in zip(out_refs, outs):
            r[...] = o.astype(r.dtype)

    o_spec = pl.BlockSpec((tm, tn), lambda i, j: (i, j))
    if out_blocks:
        per = n // N_DEV // tn
        out_shape = tuple(_sds((N_DEV, m, n // N_DEV), dt) for dt in out_dtypes)
        out_specs = tuple(pl.BlockSpec((None, tm, tn), lambda i, j: (j // per, i, j % per)) for _ in out_dtypes)
    else:
        out_shape = tuple(_sds((m, n), dt) for dt in out_dtypes)
        out_specs = tuple(o_spec for _ in out_dtypes)
    outs = pl.pallas_call(
        body,
        out_shape=out_shape,
        grid=(m // tm, n // tn),
        in_specs=in_specs + [o_spec] * n_ex,
        out_specs=out_specs,
        compiler_params=_cp(("parallel", "parallel")),
        name=name,
    )(*args, *extras)
    return outs[0] if len(out_dtypes) == 1 else outs


def _rt(t):
    return _pick(t, (384, 256, 128))


def norm_fwd(x, g, out_dtype, col_blk=0, width=None, name="norm_fwd"):
    t = x.shape[0]
    w = width or x.shape[1]
    tr = _rt(t)

    def body(x_ref, g_ref, o_ref):
        xv = x_ref[...]
        r = lax.rsqrt(jnp.mean(xv * xv, axis=-1, keepdims=True) + EPS)
        o_ref[...] = (xv * r * g_ref[...]).astype(o_ref.dtype)

    return pl.pallas_call(
        body,
        out_shape=_sds((t, w), out_dtype),
        grid=(t // tr,),
        in_specs=[pl.BlockSpec((tr, w), lambda i: (i, col_blk)), pl.BlockSpec((1, w), lambda i: (0, 0))],
        out_specs=pl.BlockSpec((tr, w), lambda i: (i, 0)),
        compiler_params=_cp(("parallel",)),
        name=name,
    )(x, g.reshape(1, w))


def _rms_bwd(xv, gv, dyv):
    r = lax.rsqrt(jnp.mean(xv * xv, axis=-1, keepdims=True) + EPS)
    xh = xv * r
    dyg = dyv * gv
    dx = r * (dyg - xh * jnp.mean(dyg * xh, axis=-1, keepdims=True))
    return dx, jnp.sum(dyv * xh, axis=0, keepdims=True)


def norm_bwd(x, g, dy, dres=None, mask_pad=False, out_dtype=F32, col_blk=0, width=None, dy_col_blk=0, then=None,
             name="norm_bwd"):
    t = x.shape[0]
    w = width or x.shape[1]
    tr = _rt(t)
    has_res, has_then = dres is not None, then is not None

    def body(*refs):
        x_ref, g_ref, dy_ref = refs[:3]
        n_in = 3 + has_res + 2 * has_then
        dx_ref, dg_ref = refs[n_in:n_in + 2]
        i = pl.program_id(0)
        dyv = dy_ref[...].astype(F32)
        if mask_pad:
            dyv = jnp.where(_row_mask(i, tr, dyv.shape), dyv, 0.0)
        dx, dg = _rms_bwd(x_ref[...], g_ref[...], dyv)
        if has_res:
            dx = dx + refs[3][...]
        dx_ref[...] = dx.astype(dx_ref.dtype)

        @pl.when(i == 0)
        def _():
            for r in refs[n_in + 1::2]:
                r[...] = jnp.zeros_like(r)

        dg_ref[...] += dg
        if has_then:
            x2_ref, g2_ref = refs[3 + has_res:5 + has_res]
            dx2_ref, dg2_ref = refs[n_in + 2:]
            dx2, dg2 = _rms_bwd(x2_ref[...], g2_ref[...], jnp.where(_row_mask(i, tr, dx.shape), dx, 0.0))
            dx2_ref[...] = dx2.astype(dx2_ref.dtype)
            dg2_ref[...] += dg2

    row = pl.BlockSpec((tr, w), lambda i: (i, 0))
    vec = pl.BlockSpec((1, w), lambda i: (0, 0))
    in_specs = [pl.BlockSpec((tr, w), lambda i: (i, col_blk)), vec, pl.BlockSpec((tr, w), lambda i: (i, dy_col_blk))]
    args = [x, g.reshape(1, w), dy]
    out_shape, out_specs = [_sds((t, w), out_dtype), _sds((1, w), F32)], [row, vec]
    if has_res:
        in_specs.append(row)
        args.append(dres)
    if has_then:
        in_specs += [row, vec]
        args += [then[0], then[1].reshape(1, w)]
        out_shape += [_sds((t, w), BF), _sds((1, w), F32)]
        out_specs += [row, vec]
    outs = pl.pallas_call(
        body,
        out_shape=tuple(out_shape),
        grid=(t // tr,),
        in_specs=in_specs,
        out_specs=tuple(out_specs),
        compiler_params=_cp(("arbitrary",)),
        name=name,
    )(*args)
    if has_then:
        return outs[0], outs[1].reshape(w), outs[2], outs[3].reshape(w)
    return outs[0], outs[1].reshape(w)


def resadd_fwd(h, m, g, g_next=None, name="resadd"):
    t, w = h.shape
    tr = _rt(t)
    with_next = g_next is not None

    def body(h_ref, m_ref, g_ref, *rest):
        mv = m_ref[...]
        r = lax.rsqrt(jnp.mean(mv * mv, axis=-1, keepdims=True) + EPS)
        y = mv * r * g_ref[...]
        h2 = h_ref[...] + jnp.where(_row_mask(pl.program_id(0), tr, y.shape), y, 0.0)
        if with_next:
            gn_ref, o_ref, hn_ref = rest
            r2 = lax.rsqrt(jnp.mean(h2 * h2, axis=-1, keepdims=True) + EPS)
            hn_ref[...] = (h2 * r2 * gn_ref[...]).astype(hn_ref.dtype)
        else:
            (o_ref,) = rest
        o_ref[...] = h2

    row = pl.BlockSpec((tr, w), lambda i: (i, 0))
    vec = pl.BlockSpec((1, w), lambda i: (0, 0))
    outs = pl.pallas_call(
        body,
        out_shape=(_sds((t, w), F32),) + ((_sds((t, w), BF),) if with_next else ()),
        grid=(t // tr,),
        in_specs=[row, row, vec] + ([vec] if with_next else []),
        out_specs=(row,) + ((row,) if with_next else ()),
        compiler_params=_cp(("parallel",)),
        name=name,
    )(h, m, g.reshape(1, w), *((g_next.reshape(1, w),) if with_next else ()))
    return outs[0], (outs[1] if with_next else None)


def loss_fwd_bwd(h, target):
    t, w = h.shape
    tr = _rt(t)

    def body(h_ref, t_ref, s_ref, dh_ref):
        i = pl.program_id(0)
        err = h_ref[...] - t_ref[...]
        err = jnp.where(_row_mask(i, tr, err.shape, PAD + N_META), err, 0.0)
        dh_ref[...] = err * (1.0 / w)

        @pl.when(i == 0)
        def _():
            s_ref[...] = jnp.zeros_like(s_ref)

        s_ref[...] += jnp.sum(err * err).reshape(1, 1)

    s, dh = pl.pallas_call(
        body,
        out_shape=(_sds((1, LANE), F32), _sds((t, w), F32)),
        grid=(t // tr,),
        in_specs=[pl.BlockSpec((tr, w), lambda i: (i, 0)), pl.BlockSpec((tr, w), lambda i: (i, 0))],
        out_specs=(pl.BlockSpec((1, LANE), lambda i: (0, 0)), pl.BlockSpec((tr, w), lambda i: (i, 0))),
        compiler_params=_cp(("arbitrary",)),
        name="loss",
    )(h, target)
    return 0.5 * s[0, 0] / w, dh


def _shift_down(ext, k, n):
    return pltpu.roll(ext, k, 0)[8:]


def _conv_pre(ext, x, w_ref, n):
    return (w_ref[4:5, :] + w_ref[3:4, :] * x + w_ref[2:3, :] * _shift_down(ext, 1, n)
            + w_ref[1:2, :] * _shift_down(ext, 2, n) + w_ref[0:1, :] * _shift_down(ext, 3, n))


def _conv_bwd_parts(dpre, dnext, x, ext, w_ref, n):
    extd = jnp.concatenate([dpre, dnext], axis=0)
    ln = n + 8
    dx = (w_ref[3:4, :] * dpre + w_ref[2:3, :] * pltpu.roll(extd, ln - 1, 0)[:n]
          + w_ref[1:2, :] * pltpu.roll(extd, ln - 2, 0)[:n] + w_ref[0:1, :] * pltpu.roll(extd, ln - 3, 0)[:n])
    sums = [jnp.sum(dpre * _shift_down(ext, 3, n), axis=0, keepdims=True),
            jnp.sum(dpre * _shift_down(ext, 2, n), axis=0, keepdims=True),
            jnp.sum(dpre * _shift_down(ext, 1, n), axis=0, keepdims=True),
            jnp.sum(dpre * x, axis=0, keepdims=True),
            jnp.sum(dpre, axis=0, keepdims=True)]
    return dx, sums


def _rows_block(sums):
    w = sums[0].shape[1]
    row = lax.broadcasted_iota(jnp.int32, (8, w), 0)
    out = jnp.zeros((8, w), F32)
    for k, s in enumerate(sums):
        out = jnp.where(row == k, s, out)
    return out


CONV_BLOCK = 512


def conv_silu_fwd(x, col0, c, wb, name="conv_fwd"):
    t = x.shape[0]
    cw = _pick(c, (CONV_BLOCK, LANE))
    nblk, col0_blk = c // cw, col0 // cw
    assert col0 % cw == 0
    tr = _rt(t)

    def body(x_ref, w_ref, o_ref, prev):
        ti = pl.program_id(1)

        @pl.when(ti == 0)
        def _():
            prev[...] = jnp.zeros_like(prev)

        xv = x_ref[...]
        ext = jnp.concatenate([prev[...], xv], axis=0)
        o_ref[...] = _silu(_conv_pre(ext, xv, w_ref, tr))
        prev[...] = xv[tr - 8:, :]

    return pl.pallas_call(
        body,
        out_shape=_sds((t, c), F32),
        grid=(nblk, t // tr),
        in_specs=[pl.BlockSpec((tr, cw), lambda cb, ti: (ti, col0_blk + cb)),
                  pl.BlockSpec((8, cw), lambda cb, ti: (0, cb))],
        out_specs=pl.BlockSpec((tr, cw), lambda cb, ti: (ti, cb)),
        scratch_shapes=[pltpu.VMEM((8, cw), F32)],
        compiler_params=_cp(("parallel", "arbitrary")),
        name=name,
    )(x, wb)


def conv_silu_bwd(x, col0, c, wb, dout, name="conv_bwd"):
    t = x.shape[0]
    cw = _pick(c, (CONV_BLOCK, LANE))
    nblk, col0_blk = c // cw, col0 // cw
    assert col0 % cw == 0
    tr = _rt(t)
    nt = t // tr
    r8 = tr // 8

    def body(x_ref, xp_ref, w_ref, do_ref, dx_ref, dwb_ref, dnext):
        ti = pl.program_id(1)
        tt = nt - 1 - ti

        @pl.when(ti == 0)
        def _():
            dnext[...] = jnp.zeros_like(dnext)
            dwb_ref[...] = jnp.zeros_like(dwb_ref)

        xv = x_ref[...]
        halo = jnp.where(tt > 0, xp_ref[...], 0.0)
        ext = jnp.concatenate([halo, xv], axis=0)
        pre = _conv_pre(ext, xv, w_ref, tr)
        s = jax.nn.sigmoid(pre)
        dpre = do_ref[...] * (s + pre * s * (1.0 - s))
        dx, sums = _conv_bwd_parts(dpre, dnext[...], xv, ext, w_ref, tr)
        dx_ref[...] = dx.astype(dx_ref.dtype)
        dwb_ref[...] += _rows_block(sums)
        dnext[...] = dpre[:8, :]

    return pl.pallas_call(
        body,
        out_shape=(_sds((t, c), BF), _sds((8, c), F32)),
        grid=(nblk, nt),
        in_specs=[pl.BlockSpec((tr, cw), lambda cb, ti: (nt - 1 - ti, col0_blk + cb)),
                  pl.BlockSpec((8, cw), lambda cb, ti: (jnp.maximum((nt - 1 - ti) * r8 - 1, 0), col0_blk + cb)),
                  pl.BlockSpec((8, cw), lambda cb, ti: (0, cb)),
                  pl.BlockSpec((tr, cw), lambda cb, ti: (nt - 1 - ti, cb))],
        out_specs=(pl.BlockSpec((tr, cw), lambda cb, ti: (nt - 1 - ti, cb)),
                   pl.BlockSpec((8, cw), lambda cb, ti: (0, cb))),
        scratch_shapes=[pltpu.VMEM((8, cw), F32)],
        compiler_params=_cp(("parallel", "arbitrary")),
        name=name,
    )(x, x, wb, dout)


def gated_norm_fwd(y, proj, g, name="gnorm_fwd"):
    t, w = y.shape
    tr = _rt(t)

    def body(y_ref, z_ref, g_ref, o_ref):
        v = y_ref[...] * _silu(z_ref[...])
        r = lax.rsqrt(jnp.mean(v * v, axis=-1, keepdims=True) + EPS)
        o_ref[...] = (v * r * g_ref[...]).astype(o_ref.dtype)

    return pl.pallas_call(
        body,
        out_shape=_sds((t, w), BF),
        grid=(t // tr,),
        in_specs=[pl.BlockSpec((tr, w), lambda i: (i, 0)), pl.BlockSpec((tr, w), lambda i: (i, OFF_Z // w)),
                  pl.BlockSpec((1, w), lambda i: (0, 0))],
        out_specs=pl.BlockSpec((tr, w), lambda i: (i, 0)),
        compiler_params=_cp(("parallel",)),
        name=name,
    )(y, proj, g.reshape(1, w))


def gated_norm_bwd(y, proj, g, dyab, name="gnorm_bwd"):
    t, w = y.shape
    tr = _rt(t)

    def body(y_ref, z_ref, g_ref, do_ref, dy_ref, dz_ref, dg_ref):
        i = pl.program_id(0)
        yv, zv, dov = y_ref[...], z_ref[...], do_ref[...]
        s = jax.nn.sigmoid(zv)
        sz = zv * s
        v = yv * sz
        r = lax.rsqrt(jnp.mean(v * v, axis=-1, keepdims=True) + EPS)
        vh = v * r
        dvg = dov * g_ref[...]
        dv = r * (dvg - vh * jnp.mean(dvg * vh, axis=-1, keepdims=True))
        dy_ref[...] = dv * sz
        dz_ref[...] = (dv * yv * (s + sz * (1.0 - s))).astype(dz_ref.dtype)

        @pl.when(i == 0)
        def _():
            dg_ref[...] = jnp.zeros_like(dg_ref)

        dg_ref[...] += jnp.sum(dov * vh, axis=0, keepdims=True)

    dy, dz, dg = pl.pallas_call(
        body,
        out_shape=(_sds((t, w), F32), _sds((t, w), BF), _sds((1, w), F32)),
        grid=(t // tr,),
        in_specs=[pl.BlockSpec((tr, w), lambda i: (i, 0)), pl.BlockSpec((tr, w), lambda i: (i, OFF_Z // w)),
                  pl.BlockSpec((1, w), lambda i: (0, 0)), pl.BlockSpec((tr, w), lambda i: (i, 0))],
        out_specs=(pl.BlockSpec((tr, w), lambda i: (i, 0)), pl.BlockSpec((tr, w), lambda i: (i, 0)),
                   pl.BlockSpec((1, w), lambda i: (0, 0))),
        compiler_params=_cp(("arbitrary",)),
        name=name,
    )(y, proj, g.reshape(1, w), dyab)
    return dy, dz, dg.reshape(w)


def rope_tables(t):
    inv = ROPE_BASE ** (-jnp.arange(0, MLA_ROPE, 2, dtype=F32) / MLA_ROPE)
    pos = (jnp.arange(t, dtype=F32) - PAD)[:, None]
    ang = pos * inv[None, :]
    cos, sin = jnp.cos(ang), jnp.sin(ang)
    z16 = jnp.zeros((t, 16), F32)
    z32 = jnp.zeros((t, 32), F32)
    c = jnp.concatenate([jnp.ones((t, 64), F32), cos, cos, z32], axis=1)
    s1 = jnp.concatenate([jnp.zeros((t, 64), F32), z16, sin, z32], axis=1)
    s2 = jnp.concatenate([jnp.zeros((t, 64), F32), -sin, z16, z32], axis=1)
    return c, s1, s2


def _rope(x, c, s1, s2):
    return x * c + pltpu.roll(x, 16, 1) * s1 + pltpu.roll(x, LANE - 16, 1) * s2


def _rope_t(d, c, s1, s2):
    return d * c + pltpu.roll(d * s1, LANE - 16, 1) + pltpu.roll(d * s2, 16, 1)


def rope_fwd(q_raw, kv_raw, proj, tabs):
    t = q_raw.shape[0]
    tr = _rt(t)
    hw = MLA_HEADS * LANE

    def body(q_ref, k_ref, v_ref, kr_ref, c_ref, s1_ref, s2_ref, qo_ref, ko_ref, vo_ref):
        c, s1, s2 = c_ref[...], s1_ref[...], s2_ref[...]
        kr = _rope(kr_ref[...], c, s1, s2)
        for h in range(MLA_HEADS):
            sl = slice(h * LANE, (h + 1) * LANE)
            qo_ref[:, sl] = (_rope(q_ref[:, sl], c, s1, s2) * Q_PRESCALE).astype(BF)
            ko_ref[:, sl] = (k_ref[:, sl] + kr).astype(BF)
        vo_ref[...] = v_ref[...].astype(BF)

    tab_spec = pl.BlockSpec((tr, LANE), lambda i: (i, 0))
    return pl.pallas_call(
        body,
        out_shape=(_sds((t, hw), BF), _sds((t, hw), BF), _sds((t, 1024), BF)),
        grid=(t // tr,),
        in_specs=[pl.BlockSpec((tr, hw), lambda i: (i, 0)), pl.BlockSpec((tr, hw), lambda i: (i, 0)),
                  pl.BlockSpec((tr, 1024), lambda i: (i, 2)), pl.BlockSpec((tr, LANE), lambda i: (i, OFF_KR // LANE)),
                  tab_spec, tab_spec, tab_spec],
        out_specs=(pl.BlockSpec((tr, hw), lambda i: (i, 0)), pl.BlockSpec((tr, hw), lambda i: (i, 0)),
                   pl.BlockSpec((tr, 1024), lambda i: (i, 0))),
        compiler_params=_cp(("parallel",)),
        name="rope_fwd",
    )(q_raw, kv_raw, kv_raw, proj, *tabs)


def rope_bwd(dq_cat, dk_cat, tabs):
    t = dq_cat.shape[0]
    tr = _rt(t)
    hw = MLA_HEADS * LANE

    def body(dq_ref, dk_ref, c_ref, s1_ref, s2_ref, dqo_ref, dkr_ref):
        c, s1, s2 = c_ref[...], s1_ref[...], s2_ref[...]
        acc = jnp.zeros((tr, LANE), F32)
        for h in range(MLA_HEADS):
            sl = slice(h * LANE, (h + 1) * LANE)
            dqo_ref[:, sl] = _rope_t(dq_ref[:, sl] * ATT_SCALE, c, s1, s2).astype(BF)
            acc = acc + dk_ref[:, sl]
        lane = lax.broadcasted_iota(jnp.int32, (tr, LANE), 1)
        dkr_ref[...] = jnp.where((lane >= 64) & (lane < 96), _rope_t(acc, c, s1, s2), 0.0)

    tab_spec = pl.BlockSpec((tr, LANE), lambda i: (i, 0))
    return pl.pallas_call(
        body,
        out_shape=(_sds((t, hw), BF), _sds((t, LANE), F32)),
        grid=(t // tr,),
        in_specs=[pl.BlockSpec((tr, hw), lambda i: (i, 0)), pl.BlockSpec((tr, hw), lambda i: (i, 0)),
                  tab_spec, tab_spec, tab_spec],
        out_specs=(pl.BlockSpec((tr, hw), lambda i: (i, 0)), pl.BlockSpec((tr, LANE), lambda i: (i, 0))),
        compiler_params=_cp(("parallel",)),
        name="rope_bwd",
    )(dq_cat, dk_cat, *tabs)


ATT_SCALE = (MLA_NOPE + MLA_ROPE) ** -0.5
LOG2E = math.log2(math.e)
Q_PRESCALE = ATT_SCALE * LOG2E
CARRY_MIDDLE_PAIR = 6
NT_DIMS = (((1,), (1,)), ((), ()))
TN_DIMS = (((0,), (0,)), ((), ()))


def _att_mask(qi, ki, tq, tk):
    qpos = qi * tq + lax.broadcasted_iota(jnp.int32, (tq, tk), 0)
    kpos = ki * tk + lax.broadcasted_iota(jnp.int32, (tq, tk), 1)
    return (kpos <= qpos) & (kpos >= PAD)


def _half_masks(n):
    lane = lax.broadcasted_iota(jnp.int32, (n, LANE), 1)
    return lane < 64, lane >= 64


def _att_tile(t):
    return _pick(t, (384, 256, 128))


def _ds(i, n):
    return pl.ds(i * n, n) if isinstance(i, int) else pl.ds(pl.multiple_of(i * n, n), n)


FWD_PAIRS = 2


def attn_fwd(q_cat, k_cat, v, carry=None):
    t = q_cat.shape[0]
    tq = tk = _att_tile(t)
    nq = t // tq
    npair, nh = FWD_PAIRS, 2 * FWD_PAIRS
    n_grp = MLA_HEADS // nh
    nx = carry.k if carry else 0

    def body(*refs):
        q_ref, k_ref, v_ref = refs[:3]
        o_ref, lse_ref = refs[3 + nx:5 + nx]
        qi = pl.program_id(1)
        if carry:
            start, middle, finish = carry.phases(refs[3:3 + nx], refs[5 + nx:5 + 2 * nx], refs[5 + 2 * nx:])
            grp = pl.program_id(0)
            pl.when((grp == 0) & (qi == 0))(start)
            pl.when((grp == CARRY_MIDDLE_PAIR // npair) & (qi == 0))(middle)
        lo_q, _ = _half_masks(tq)
        halves = _half_masks(tk)

        def step(ki, state, masked):
            m_old, l_old, accs = state[0:nh], state[nh:2 * nh], state[2 * nh:]
            rows = _ds(ki, tk)
            ss = [lax.dot_general(q_ref[:, h * LANE:(h + 1) * LANE], k_ref[rows, h * LANE:(h + 1) * LANE], NT_DIMS,
                                  preferred_element_type=F32) for h in range(nh)]
            if masked:
                valid = _att_mask(qi, ki, tq, tk)
                ss = [jnp.where(valid, s, NEG) for s in ss]
            m_new = [jnp.maximum(m_old[h], jnp.max(ss[h], axis=-1, keepdims=True)) for h in range(nh)]
            ps = [jnp.exp2(ss[h] - m_new[h]) for h in range(nh)]
            alpha = [jnp.exp2(m_old[h] - m_new[h]) for h in range(nh)]
            l_new = [alpha[h] * l_old[h] + jnp.sum(ps[h], axis=-1, keepdims=True) for h in range(nh)]
            new_accs = []
            for pp in range(npair):
                vv = v_ref[rows, pp * LANE:(pp + 1) * LANE]
                pv = [jnp.dot(ps[2 * pp + hh].astype(BF), jnp.where(halves[hh], vv, jnp.zeros_like(vv)),
                              preferred_element_type=F32) for hh in range(2)]
                new_accs.append(accs[pp] * jnp.where(lo_q, alpha[2 * pp], alpha[2 * pp + 1]) + pv[0] + pv[1])
            return tuple(m_new) + tuple(l_new) + tuple(new_accs)

        neg, zero = jnp.full((tq, 1), NEG, F32), jnp.zeros((tq, 1), F32)
        state = step(0, (neg,) * nh + (zero,) * nh + (jnp.zeros((tq, LANE), F32),) * npair, True)
        state = lax.fori_loop(1, qi, lambda ki, st: step(ki, st, False), state)
        state = lax.cond(qi > 0, lambda st: step(qi, st, True), lambda st: st, state)
        for pp in range(npair):
            l = jnp.where(lo_q, state[nh + 2 * pp], state[nh + 2 * pp + 1])
            o_ref[:, pp * LANE:(pp + 1) * LANE] = (state[2 * nh + pp] / l).astype(o_ref.dtype)
            lse_ref[:, pp * LANE:(pp + 1) * LANE] = jnp.where(lo_q, state[2 * pp], state[2 * pp + 1]) + jnp.log2(l)
        if carry:
            pl.when((grp == n_grp - 1) & (qi == nq - 1))(finish)

    outs = pl.pallas_call(
        body,
        out_shape=(_sds((t, 1024), BF), _sds((t, 1024), F32)) + tuple(carry.out_shapes() if carry else ()),
        grid=(n_grp, nq),
        in_specs=[pl.BlockSpec((tq, nh * LANE), lambda g, qi: (qi, g)),
                  pl.BlockSpec((t, nh * LANE), lambda g, qi: (0, g)),
                  pl.BlockSpec((t, npair * LANE), lambda g, qi: (0, g))] + [ANY] * nx,
        out_specs=(pl.BlockSpec((tq, npair * LANE), lambda g, qi: (qi, g)),
                   pl.BlockSpec((tq, npair * LANE), lambda g, qi: (qi, g))) + (ANY,) * nx,
        scratch_shapes=carry.scratch() if carry else [],
        compiler_params=_cp(("arbitrary", "arbitrary") if carry else ("parallel", "parallel")),
        name="attn_fwd_carrying" if carry else "attn_fwd",
    )(q_cat, k_cat, v, *(carry.arrs if carry else ()))
    return outs[0], outs[1], list(outs[2:])


def attn_bwd(q_cat, k_cat, v, o, lse, dyab, carry=None):
    t = q_cat.shape[0]
    tq = tk = _att_tile(t)
    nq = t // tq
    n_pair = MLA_HEADS // 2
    nx = carry.k if carry else 0

    def body(*refs):
        q_ref, k_ref, v_ref, o_ref, lse_ref, do_ref = refs[:6]
        dq_ref, dk_ref, dv_ref = refs[6 + nx:9 + nx]
        ki = pl.program_id(1)
        if carry:
            start, middle, finish = carry.phases(refs[6:6 + nx], refs[9 + nx:9 + 2 * nx], refs[9 + 2 * nx:])
            pair = pl.program_id(0)
            pl.when((pair == 0) & (ki == 0))(start)
            pl.when((pair == CARRY_MIDDLE_PAIR) & (ki == 0))(middle)

        @pl.when(ki == 0)
        def _():
            dq_ref[...] = jnp.zeros_like(dq_ref)

        halves = _half_masks(tq)
        vv = v_ref[...]
        kk = [k_ref[:, hh * LANE:(hh + 1) * LANE] for hh in range(2)]

        def step(qi, acc, masked):
            rows = _ds(qi, tq)
            dov, ov, lse_v = do_ref[rows, :], o_ref[rows, :].astype(F32), lse_ref[rows, :]
            qh = [q_ref[rows, hh * LANE:(hh + 1) * LANE] for hh in range(2)]
            ss = [lax.dot_general(qh[hh], kk[hh], NT_DIMS, preferred_element_type=F32) for hh in range(2)]
            if masked:
                valid = _att_mask(qi, ki, tq, tk)
                ss = [jnp.where(valid, s, NEG) for s in ss]
            ps = [jnp.exp2(ss[hh] - lse_v[:, 64 * hh:64 * hh + 1]) for hh in range(2)]
            dom = [jnp.where(halves[hh], dov, 0.0) for hh in range(2)]
            delta = [jnp.sum(dom[hh] * ov, axis=-1, keepdims=True) for hh in range(2)]
            dom = [d.astype(BF) for d in dom]
            dp = [lax.dot_general(dom[hh], vv, NT_DIMS, preferred_element_type=F32) for hh in range(2)]
            ds = [(ps[hh] * (dp[hh] - delta[hh])).astype(BF) for hh in range(2)]
            pb = [p.astype(BF) for p in ps]
            dv = (acc[2] + lax.dot_general(pb[0], dom[0], TN_DIMS, preferred_element_type=F32)
                  + lax.dot_general(pb[1], dom[1], TN_DIMS, preferred_element_type=F32))
            dk = [acc[hh] + lax.dot_general(ds[hh], qh[hh], TN_DIMS, preferred_element_type=F32) for hh in range(2)]
            for hh in range(2):
                dq_ref[rows, hh * LANE:(hh + 1) * LANE] += jnp.dot(ds[hh], kk[hh], preferred_element_type=F32)
            return dk[0], dk[1], dv

        zero = jnp.zeros((tk, LANE), F32)
        acc = step(ki, (zero, zero, zero), True)
        acc = lax.fori_loop(ki + 1, jnp.where(ki == 0, nq, ki + 1), lambda qi, a: step(qi, a, True), acc)
        acc = lax.fori_loop(ki + 1, jnp.where(ki == 0, ki + 1, nq), lambda qi, a: step(qi, a, False), acc)
        dk_ref[:, 0:LANE] = acc[0] * (1.0 / LOG2E)
        dk_ref[:, LANE:2 * LANE] = acc[1] * (1.0 / LOG2E)
        dv_ref[...] = acc[2].astype(dv_ref.dtype)
        if carry:
            pl.when((pair == n_pair - 1) & (ki == nq - 1))(finish)

    full = lambda w, off=0: pl.BlockSpec((t, w), lambda p, ki: (0, p + off))
    blk = lambda w: pl.BlockSpec((tk, w), lambda p, ki: (ki, p))
    outs = pl.pallas_call(
        body,
        out_shape=(_sds((t, 2048), F32), _sds((t, 2048), F32), _sds((t, 1024), BF))
        + tuple(carry.out_shapes() if carry else ()),
        grid=(n_pair, nq),
        in_specs=[full(2 * LANE), blk(2 * LANE), blk(LANE), full(LANE), full(LANE), full(LANE, 8)] + [ANY] * nx,
        out_specs=(full(2 * LANE), blk(2 * LANE), blk(LANE)) + (ANY,) * nx,
        scratch_shapes=carry.scratch() if carry else [],
        compiler_params=_cp(("arbitrary", "arbitrary") if carry else ("parallel", "arbitrary")),
        name="attn_bwd_carrying" if carry else "attn_bwd",
    )(q_cat, k_cat, v, o, lse, dyab, *(carry.arrs if carry else ()))
    return outs[0], outs[1], outs[2], list(outs[3:])


N_PAIR = SSD_HEADS // 2


def _hdot(a, b):
    return jnp.dot(a, b, precision=HI, preferred_element_type=F32)


def _ssd_chunk(xs, bg, cg, dtraw, hin, dt_bias, a_log, dskip, rowmask):
    ln = CHUNK
    causal = lax.broadcasted_iota(jnp.int32, (ln, ln), 0) >= lax.broadcasted_iota(jnp.int32, (ln, ln), 1)
    ltri = causal.astype(F32)
    lane = lax.broadcasted_iota(jnp.int32, (ln, LANE), 1)
    halves = (lane < 64, lane >= 64)
    low_row = lax.broadcasted_iota(jnp.int32, (1, LANE), 1) < 64
    head_lane = lax.broadcasted_iota(jnp.int32, (1, SSD_HEADS), 1)
    head_row = lax.broadcasted_iota(jnp.int32, (SSD_HEADS, 1), 0)

    def col(a, h):
        return jnp.sum(jnp.where(head_lane == h, a, 0.0), axis=1, keepdims=True)

    dt = _softplus(dtraw + dt_bias) * rowmask
    da = dt * (-jnp.exp(a_log))
    acs = _hdot(ltri, da)
    acs_t = lax.dot_general(da, ltri, (((0,), (1,)), ((), ())), precision=HI, preferred_element_type=F32)
    tot = jnp.sum(da, axis=0, keepdims=True)
    bm = [b * rowmask for b in bg]
    cm = [c * rowmask for c in cg]
    cb = [lax.dot_general(cm[g].astype(BF), bm[g].astype(BF), NT_DIMS, preferred_element_type=F32) for g in range(2)]
    ys, hout = [], []
    for p in range(N_PAIR):
        g = p // (N_PAIR // 2)
        h0, h1 = 2 * p, 2 * p + 1
        xdt = xs[p] * jnp.where(halves[0], col(dt, h0), col(dt, h1))
        a_cols = [col(acs, h0), col(acs, h1)]
        tot_cols = [col(tot, h0), col(tot, h1)]
        y = jnp.zeros((ln, LANE), F32)
        snew = jnp.zeros((ln, LANE), F32)
        for hh in range(2):
            a_row = jnp.sum(jnp.where(head_row == h0 + hh, acs_t, 0.0), axis=0, keepdims=True)
            dec = jnp.exp(jnp.where(causal, a_cols[hh] - a_row, NEG))
            xm = jnp.where(halves[hh], xdt, 0.0).astype(BF)
            y = y + jnp.dot((cb[g] * dec).astype(BF), xm, preferred_element_type=F32)
            bd = bm[g] * jnp.exp(tot_cols[hh] - a_cols[hh])
            snew = snew + lax.dot_general(bd.astype(BF), xm, TN_DIMS, preferred_element_type=F32)
        y_off = (jnp.dot(cm[g].astype(BF), hin[p].astype(BF), preferred_element_type=F32)
                 * jnp.where(halves[0], jnp.exp(a_cols[0]), jnp.exp(a_cols[1])))
        ys.append(y + y_off + jnp.where(low_row, col(dskip, h0), col(dskip, h1)) * xs[p])
        hout.append(jnp.where(low_row, jnp.exp(tot_cols[0]), jnp.exp(tot_cols[1])) * hin[p] + snew)
    return ys, hout


def _ssd_load(x_ref, dt_ref):
    xs = [x_ref[:, p * LANE:(p + 1) * LANE] for p in range(N_PAIR)]
    bg = [x_ref[:, SSD_D_INNER + g * LANE:SSD_D_INNER + (g + 1) * LANE] for g in range(2)]
    cg = [x_ref[:, SSD_D_INNER + (2 + g) * LANE:SSD_D_INNER + (3 + g) * LANE] for g in range(2)]
    return xs, bg, cg, dt_ref[:, 0:SSD_HEADS]


def _chunk_rowmask(c):
    return ((c * CHUNK + lax.broadcasted_iota(jnp.int32, (CHUNK, 1), 0)) >= PAD).astype(F32)


def ssd_fwd(xbc_c, proj, dt_bias, a_log, dskip):
    t = xbc_c.shape[0]
    nc = t // CHUNK

    def body(x_ref, dt_ref, dtb_ref, al_ref, d_ref, y_ref, hs_ref, h_s):
        c = pl.program_id(0)

        @pl.when(c == 0)
        def _():
            h_s[...] = jnp.zeros_like(h_s)

        xs, bg, cg, dtraw = _ssd_load(x_ref, dt_ref)
        hin = [h_s[p] for p in range(N_PAIR)]
        hs_ref[0] = h_s[...]
        ys, hout = _ssd_chunk(xs, bg, cg, dtraw, hin, dtb_ref[...], al_ref[...], d_ref[...], _chunk_rowmask(c))
        for p in range(N_PAIR):
            y_ref[:, p * LANE:(p + 1) * LANE] = ys[p]
            h_s[p] = hout[p]

    par = pl.BlockSpec((1, SSD_HEADS), lambda c: (0, 0))
    return pl.pallas_call(
        body,
        out_shape=(_sds((t, SSD_D_INNER), F32), _sds((nc, N_PAIR, CHUNK, LANE), F32)),
        grid=(nc,),
        in_specs=[pl.BlockSpec((CHUNK, SSD_CONV_CH), lambda c: (c, 0)),
                  pl.BlockSpec((CHUNK, LANE), lambda c: (c, OFF_DT // LANE)), par, par, par],
        out_specs=(pl.BlockSpec((CHUNK, SSD_D_INNER), lambda c: (c, 0)),
                   pl.BlockSpec((1, N_PAIR, CHUNK, LANE), lambda c: (c, 0, 0, 0))),
        scratch_shapes=[pltpu.VMEM((N_PAIR, CHUNK, LANE), F32)],
        compiler_params=_cp(("arbitrary",)),
        name="ssd_fwd",
    )(xbc_c, proj, dt_bias.reshape(1, -1), a_log.reshape(1, -1), dskip.reshape(1, -1))


def ssd_bwd(xbc_c, proj, dt_bias, a_log, dskip, hs, dy):
    t = xbc_c.shape[0]
    nc = t // CHUNK

    def body(x_ref, dt_ref, dtb_ref, al_ref, d_ref, hs_ref, dy_ref, dx_ref, ddt_ref, dpar_ref, dh_s):
        ci = pl.program_id(0)
        c = nc - 1 - ci

        @pl.when(ci == 0)
        def _():
            dh_s[...] = jnp.zeros_like(dh_s)
            dpar_ref[...] = jnp.zeros_like(dpar_ref)

        xs, bg, cg, dtraw = _ssd_load(x_ref, dt_ref)
        hin = [hs_ref[0, p] for p in range(N_PAIR)]
        rowmask = _chunk_rowmask(c)
        fn = lambda xs_, bg_, cg_, dtraw_, hin_, dtb_, al_, d_: _ssd_chunk(xs_, bg_, cg_, dtraw_, hin_, dtb_, al_, d_, rowmask)
        _, vjp = jax.vjp(fn, xs, bg, cg, dtraw, hin, dtb_ref[...], al_ref[...], d_ref[...])
        dys = [dy_ref[:, p * LANE:(p + 1) * LANE] for p in range(N_PAIR)]
        dhs = [dh_s[p] for p in range(N_PAIR)]
        dxs, dbg, dcg, ddtraw, dhin, ddtb, dal, dd = vjp((dys, dhs))
        for p in range(N_PAIR):
            dx_ref[:, p * LANE:(p + 1) * LANE] = dxs[p]
            dh_s[p] = dhin[p]
        for g in range(2):
            dx_ref[:, SSD_D_INNER + g * LANE:SSD_D_INNER + (g + 1) * LANE] = dbg[g]
            dx_ref[:, SSD_D_INNER + (2 + g) * LANE:SSD_D_INNER + (3 + g) * LANE] = dcg[g]
        ddt_ref[...] = jnp.zeros_like(ddt_ref)
        ddt_ref[:, 0:SSD_HEADS] = ddtraw
        dpar_ref[0:1, 0:SSD_HEADS] += ddtb
        dpar_ref[1:2, 0:SSD_HEADS] += dal
        dpar_ref[2:3, 0:SSD_HEADS] += dd

    par = pl.BlockSpec((1, SSD_HEADS), lambda ci: (0, 0))
    return pl.pallas_call(
        body,
        out_shape=(_sds((t, SSD_CONV_CH), F32), _sds((t, LANE), F32), _sds((8, LANE), F32)),
        grid=(nc,),
        in_specs=[pl.BlockSpec((CHUNK, SSD_CONV_CH), lambda ci: (nc - 1 - ci, 0)),
                  pl.BlockSpec((CHUNK, LANE), lambda ci: (nc - 1 - ci, OFF_DT // LANE)), par, par, par,
                  pl.BlockSpec((1, N_PAIR, CHUNK, LANE), lambda ci: (nc - 1 - ci, 0, 0, 0)),
                  pl.BlockSpec((CHUNK, SSD_D_INNER), lambda ci: (nc - 1 - ci, 0))],
        out_specs=(pl.BlockSpec((CHUNK, SSD_CONV_CH), lambda ci: (nc - 1 - ci, 0)),
                   pl.BlockSpec((CHUNK, LANE), lambda ci: (nc - 1 - ci, 0)),
                   pl.BlockSpec((8, LANE), lambda ci: (0, 0))),
        scratch_shapes=[pltpu.VMEM((N_PAIR, CHUNK, LANE), F32)],
        compiler_params=_cp(("arbitrary",)),
        name="ssd_bwd",
    )(xbc_c, proj, dt_bias.reshape(1, -1), a_log.reshape(1, -1), dskip.reshape(1, -1), hs, dy)


def _neg_expm1(y):
    series = -(y * (1.0 + y * (0.5 + y * (1.0 / 6.0 + y * (1.0 / 24.0 + y * (1.0 / 120.0))))))
    return jnp.where(y > -0.1, series, 1.0 - jnp.exp(y))


def _rg_pw(xr, wa, ba, wi, bi, lam, rowmask):
    xb = xr.astype(BF)
    r = jax.nn.sigmoid(jnp.dot(xb, wa.astype(BF), preferred_element_type=F32) + ba)
    i = jax.nn.sigmoid(jnp.dot(xb, wi.astype(BF), preferred_element_type=F32) + bi)
    log_a = -LRU_C * r * _softplus(-lam)
    a = jnp.exp(log_a)
    u = jnp.sqrt(_neg_expm1(2.0 * log_a)) * (i * xr) * rowmask
    return a, u


def _gelu_grad(x):
    c = math.sqrt(2.0 / math.pi)
    th = jnp.tanh(c * (x + 0.044715 * (x * x * x)))
    return 0.5 * (1.0 + th) + 0.5 * x * (1.0 - th * th) * c * (1.0 + 3.0 * 0.044715 * x * x)


def _scan_fwd(a, u):
    n = a.shape[0]
    row = lax.broadcasted_iota(jnp.int32, a.shape, 0)
    s = 1
    while s < n:
        a_s = jnp.where(row >= s, pltpu.roll(a, s, 0), 1.0)
        u_s = jnp.where(row >= s, pltpu.roll(u, s, 0), 0.0)
        u = u + a * u_s
        a = a * a_s
        s *= 2
    return a, u


def _scan_bwd(b, d):
    n = b.shape[0]
    row = lax.broadcasted_iota(jnp.int32, b.shape, 0)
    s = 1
    while s < n:
        b_s = jnp.where(row < n - s, pltpu.roll(b, n - s, 0), 1.0)
        d_s = jnp.where(row < n - s, pltpu.roll(d, n - s, 0), 0.0)
        d = d + b * d_s
        b = b * b_s
        s *= 2
    return d


def rg_fwd(xr_pre, gate_pre, rgp, w_a, w_i):
    t = xr_pre.shape[0]
    tr = _rt(t)

    def body(x_ref, g_ref, p_ref, wa_ref, wi_ref, hg_ref, hs_ref, prev, hcar):
        ti = pl.program_id(1)

        @pl.when(ti == 0)
        def _():
            prev[...] = jnp.zeros_like(prev)
            hcar[...] = jnp.zeros_like(hcar)

        xv = x_ref[...]
        ext = jnp.concatenate([prev[...], xv], axis=0)
        xr = _conv_pre(ext, xv, p_ref, tr)
        rowmask = _row_mask(ti, tr, (tr, 1)).astype(F32)
        a, u = _rg_pw(xr, wa_ref[0], p_ref[5:6, :], wi_ref[0], p_ref[6:7, :], p_ref[7:8, :], rowmask)
        a_cum, h_loc = _scan_fwd(a, u)
        hs = h_loc + a_cum * hcar[0:1, :]
        hs_ref[...] = hs
        hg_ref[...] = (hs * _gelu(g_ref[...])).astype(hg_ref.dtype)
        hcar[...] = jnp.broadcast_to(hs[tr - 1:tr, :], (8, LANE))
        prev[...] = xv[tr - 8:, :]

    return pl.pallas_call(
        body,
        out_shape=(_sds((t, LRU_WIDTH), BF), _sds((t, LRU_WIDTH), F32)),
        grid=(LRU_BLOCKS, t // tr),
        in_specs=[pl.BlockSpec((tr, LANE), lambda n, ti: (ti, n)),
                  pl.BlockSpec((tr, LANE), lambda n, ti: (ti, n)),
                  pl.BlockSpec((8, LANE), lambda n, ti: (0, n)),
                  pl.BlockSpec((1, LANE, LANE), lambda n, ti: (n, 0, 0)),
                  pl.BlockSpec((1, LANE, LANE), lambda n, ti: (n, 0, 0))],
        out_specs=(pl.BlockSpec((tr, LANE), lambda n, ti: (ti, n)), pl.BlockSpec((tr, LANE), lambda n, ti: (ti, n))),
        scratch_shapes=[pltpu.VMEM((8, LANE), F32), pltpu.VMEM((8, LANE), F32)],
        compiler_params=_cp(("parallel", "arbitrary")),
        name="rg_fwd",
    )(xr_pre, gate_pre, rgp, w_a, w_i)


def rg_bwd(xr_pre, gate_pre, rgp, w_a, w_i, hs, dhg):
    t = xr_pre.shape[0]
    tr = _rt(t)
    nt = t // tr
    r8 = tr // 8

    def body(x_ref, xp_ref, g_ref, p_ref, wa_ref, wi_ref, hs_ref, hp_ref, dhg_ref,
             dx_ref, dg_ref, dp_ref, dwa_ref, dwi_ref, gcar, dnext):
        ti = pl.program_id(1)
        tt = nt - 1 - ti

        @pl.when(ti == 0)
        def _():
            gcar[...] = jnp.zeros_like(gcar)
            dnext[...] = jnp.zeros_like(dnext)
            dp_ref[...] = jnp.zeros_like(dp_ref)
            dwa_ref[...] = jnp.zeros_like(dwa_ref)
            dwi_ref[...] = jnp.zeros_like(dwi_ref)

        xv = x_ref[...]
        halo = jnp.where(tt > 0, xp_ref[...], 0.0)
        ext = jnp.concatenate([halo, xv], axis=0)
        xr = _conv_pre(ext, xv, p_ref, tr)
        rowmask = _row_mask(tt, tr, (tr, 1)).astype(F32)
        fn = lambda xr_, wa_, ba_, wi_, bi_, lam_: _rg_pw(xr_, wa_, ba_, wi_, bi_, lam_, rowmask)
        (a, _), vjp = jax.vjp(fn, xr, wa_ref[0], p_ref[5:6, :], wi_ref[0], p_ref[6:7, :], p_ref[7:8, :])
        gpre = g_ref[...]
        hsv = hs_ref[...]
        dhg_v = dhg_ref[...]
        dg_ref[...] = (dhg_v * hsv * _gelu_grad(gpre)).astype(dg_ref.dtype)
        row = lax.broadcasted_iota(jnp.int32, (tr, LANE), 0)
        d = dhg_v * _gelu(gpre) + jnp.where(row == tr - 1, gcar[0:1, :], 0.0)
        b = jnp.where(row < tr - 1, pltpu.roll(a, tr - 1, 0), 0.0)
        g = _scan_bwd(b, d)
        gcar[...] = jnp.broadcast_to(a[0:1, :] * g[0:1, :], (8, LANE))
        hlast = jnp.where(tt > 0, hp_ref[7:8, :], 0.0)
        hprev = jnp.where(row == 0, hlast, pltpu.roll(hsv, 1, 0))
        dxr, dwa, dba, dwi, dbi, dlam = vjp((g * hprev, g))
        dx, sums = _conv_bwd_parts(dxr, dnext[...], xv, ext, p_ref, tr)
        dx_ref[...] = dx.astype(dx_ref.dtype)
        dnext[...] = dxr[:8, :]
        dp_ref[...] += _rows_block(sums + [dba, dbi, dlam])
        dwa_ref[0] += dwa
        dwi_ref[0] += dwi

    tile = lambda off=0: pl.BlockSpec((tr, LANE), lambda n, ti: (nt - 1 - ti, off + n))
    halo = lambda off=0: pl.BlockSpec((8, LANE), lambda n, ti: (jnp.maximum((nt - 1 - ti) * r8 - 1, 0), off + n))
    par = pl.BlockSpec((8, LANE), lambda n, ti: (0, n))
    wspec = pl.BlockSpec((1, LANE, LANE), lambda n, ti: (n, 0, 0))
    return pl.pallas_call(
        body,
        out_shape=(_sds((t, LRU_WIDTH), BF), _sds((t, LRU_WIDTH), BF), _sds((8, LRU_WIDTH), F32),
                   _sds((LRU_BLOCKS, LANE, LANE), F32), _sds((LRU_BLOCKS, LANE, LANE), F32)),
        grid=(LRU_BLOCKS, nt),
        in_specs=[tile(), halo(), tile(), par, wspec, wspec, tile(), halo(), tile()],
        out_specs=(tile(), tile(), par, wspec, wspec),
        scratch_shapes=[pltpu.VMEM((8, LANE), F32), pltpu.VMEM((8, LANE), F32)],
        compiler_params=_cp(("parallel", "arbitrary")),
        name="rg_bwd",
    )(xr_pre, xr_pre, gate_pre, rgp, w_a, w_i, hs, hs, dhg)


PACK_W = 1024
MESH_ID = pl.DeviceIdType.MESH
ANY = pl.BlockSpec(memory_space=pl.ANY)


def _my_place():
    x, y, c = lax.axis_index("x"), lax.axis_index("y"), lax.axis_index("c")
    return x, y, c


def _lin(px, py, pc):
    return 4 * px + 2 * py + pc


class Exchange:
    def __init__(self, kind, arrs):
        self.kind, self.arrs, self.k = kind, list(arrs), len(arrs)

    def out_shapes(self):
        if self.kind == "gather":
            return [_sds((N_DEV,) + a.shape, a.dtype) for a in self.arrs]
        return [_sds(a.shape, a.dtype) for a in self.arrs]

    def scratch(self):
        k = self.k
        return [pltpu.SemaphoreType.DMA((k, 7)), pltpu.SemaphoreType.DMA((k, 7)), pltpu.SemaphoreType.DMA((k,))]

    def phases(self, ins, outs, sems):
        return (self._gather if self.kind == "gather" else self._scatter)(ins, outs, *sems)

    def _gather(self, ins, outs, send_sems, recv_sems, local_sems):
        k = self.k
        x, y, c = _my_place()
        me, sibling = (x, y, c), (x, y, 1 - c)
        chips = [(1 - x, y), (x, 1 - y), (1 - x, 1 - y)]

        def copy(a, sem, block, to, from_input=False):
            slab = outs[a].at[_lin(*block)]
            return pltpu.make_async_remote_copy(
                src_ref=ins[a] if from_input else slab, dst_ref=slab,
                send_sem=send_sems.at[a, sem], recv_sem=recv_sems.at[a, sem],
                device_id=to, device_id_type=MESH_ID)

        def mine():
            return [pltpu.make_async_copy(ins[a], outs[a].at[_lin(*me)], local_sems.at[a]) for a in range(k)]

        def first():
            out = []
            for a in range(k):
                out.append(copy(a, 0, me, sibling, True))
                out += [copy(a, 1 + j, me, (*chip, c), True) for j, chip in enumerate(chips)]
            return out

        def passed():
            return [copy(a, 4 + j, (*chip, c), sibling) for j, chip in enumerate(chips) for a in range(k)]

        def start():
            for cp in mine() + first():
                cp.start()

        def middle():
            onward = passed()
            for j, chip in enumerate(chips):
                for a in range(k):
                    copy(a, 1 + j, (*chip, c), me).wait_recv()
                    onward[j * k + a].start()

        def finish():
            for a in range(k):
                copy(a, 0, sibling, me).wait_recv()
                for j, chip in enumerate(chips):
                    copy(a, 4 + j, (*chip, 1 - c), me).wait_recv()
            for cp in first() + passed():
                cp.wait_send()
            for cp in mine():
                cp.wait()

        return start, middle, finish

    def _scatter(self, ins, outs, send_sems, recv_sems, local_sems):
        k = self.k
        x, y, c = _my_place()
        me = _lin(x, y, c)
        peers = [((1 - x) if r & 4 else x, (1 - y) if r & 2 else y, (1 - c) if r & 1 else c) for r in range(1, N_DEV)]

        def copy(a, r, src_slab, dst_slab, to):
            return pltpu.make_async_remote_copy(
                src_ref=ins[a].at[src_slab], dst_ref=outs[a].at[dst_slab],
                send_sem=send_sems.at[a, r], recv_sem=recv_sems.at[a, r],
                device_id=to, device_id_type=MESH_ID)

        def mine():
            return [pltpu.make_async_copy(ins[a].at[me], outs[a].at[me], local_sems.at[a]) for a in range(k)]

        def sends():
            return [copy(a, r, _lin(*peer), me, peer) for r, peer in enumerate(peers) for a in range(k)]

        def start():
            for cp in mine() + sends():
                cp.start()

        def middle():
            pass

        def finish():
            for r, peer in enumerate(peers):
                for a in range(k):
                    copy(a, r, me, _lin(*peer), peer).wait_recv()
            for cp in sends():
                cp.wait_send()
            for cp in mine():
                cp.wait()

        return start, middle, finish

    def run(self, name):
        k = self.k

        def body(*refs):
            start, middle, finish = self.phases(refs[:k], refs[k:2 * k], refs[2 * k:])
            start()
            middle()
            finish()

        return pl.pallas_call(
            body,
            out_shape=tuple(self.out_shapes()),
            in_specs=[ANY] * k,
            out_specs=tuple(ANY for _ in range(k)),
            scratch_shapes=self.scratch(),
            name=name,
        )(*self.arrs)


def all_gather(arrs, name):
    return Exchange("gather", arrs).run(name)


def all_to_all(arrs, name):
    return Exchange("scatter", arrs).run(name)


def slab_sum(a, name):
    _, r, w = a.shape
    tr = _pick(r, (256, 128, 64, 32, 16, 8))

    def body(a_ref, o_ref):
        acc = a_ref[0].astype(F32)
        for d in range(1, N_DEV):
            acc = acc + a_ref[d].astype(F32)
        o_ref[...] = acc

    return pl.pallas_call(
        body,
        out_shape=_sds((r, w), F32),
        grid=(r // tr,),
        in_specs=[pl.BlockSpec((N_DEV, tr, w), lambda i: (0, i, 0))],
        out_specs=pl.BlockSpec((tr, w), lambda i: (i, 0)),
        compiler_params=_cp(("parallel",)),
        name=name,
    )(a)


def _adam_update(w, g, m, v):
    nm = ADAM_B1 * m + (1.0 - ADAM_B1) * g
    nv = ADAM_B2 * v + (1.0 - ADAM_B2) * (g * g)
    m_hat = nm / (1.0 - ADAM_B1 ** ADAM_STEP)
    v_hat = nv / (1.0 - ADAM_B2 ** ADAM_STEP)
    return -ADAM_LR * (m_hat / (jnp.sqrt(v_hat) + ADAM_EPS) + ADAM_WD * w), nm, nv


def adamw_blocks(w, m, v, parts, name):
    nl, r, c = w.shape
    tr = next(t for t in (256, 160, 128, 64, 32, 16) if r % t == 0 and N_DEV * t * c * 2 <= 2 * 1024 * 1024)

    def body(w_ref, m_ref, v_ref, *rest):
        part_refs, (g_ref, d_ref, nm_ref, nv_ref) = rest[:nl], rest[nl:]
        layer = pl.program_id(0)
        for idx in range(nl):
            @pl.when(layer == idx)
            def _(idx=idx):
                g = part_refs[idx][0].astype(F32)
                for dev in range(1, N_DEV):
                    g = g + part_refs[idx][dev].astype(F32)
                g_ref[...] = g
                d_ref[...], nm_ref[...], nv_ref[...] = _adam_update(w_ref[...], g, m_ref[...], v_ref[...])

    spec = pl.BlockSpec((None, tr, c), lambda l, i: (l, i, 0))
    part_spec = lambda idx: pl.BlockSpec((N_DEV, tr, c), lambda l, i: (0, jnp.where(l == idx, i, 0), 0))
    return pl.pallas_call(
        body,
        out_shape=tuple(_sds((nl, r, c), F32) for _ in range(4)),
        grid=(nl, r // tr),
        in_specs=[spec] * 3 + [part_spec(idx) for idx in range(nl)],
        out_specs=(spec,) * 4,
        compiler_params=_cp(("arbitrary", "arbitrary")),
        name=name,
    )(w, m, v, *parts)


def adamw(w, g, m, v, name):
    r, c = w.shape
    tr = _pick(r, (256, 160, 128, 64, 32, 16, 8))

    def body(w_ref, g_ref, m_ref, v_ref, d_ref, nm_ref, nv_ref):
        d_ref[...], nm_ref[...], nv_ref[...] = _adam_update(w_ref[...], g_ref[...], m_ref[...], v_ref[...])

    spec = pl.BlockSpec((tr, c), lambda i: (i, 0))
    return pl.pallas_call(
        body,
        out_shape=tuple(_sds((r, c), F32) for _ in range(3)),
        grid=(r // tr,),
        in_specs=[spec] * 4,
        out_specs=(spec, spec, spec),
        compiler_params=_cp(("parallel",)),
        name=name,
    )(w, g, m, v)


def _relu2_epi(acc):
    r = jnp.maximum(acc, 0.0)
    return r * r, r


def _drelu2_epi(acc, r):
    return (acc * (2.0 * r.astype(F32)),)


def mlp_fwd(h, g_pre, g_post, w_up, w_down, hn=None, g_next=None):
    if hn is None:
        hn = norm_fwd(h, g_pre, BF, name="mlp_norm")
    u, r = matmul(hn, w_up, "nn", (BF, BF), epi=_relu2_epi, name="mlp_up")
    d = matmul(u, w_down, "nn", name="mlp_down")
    h2, hn_next = resadd_fwd(h, d, g_post, g_next, name="mlp_res")
    return h2, (h, hn, u, r, d), hn_next


def mlp_bwd(res, dh2, g_pre, g_post, w_up, w_down, post=None, then=None):
    h, hn, u, r, d = res
    dd, dg_post = post if post is not None else norm_bwd(d, g_post, dh2, mask_pad=True, out_dtype=BF, name="mlp_post_bwd")
    dw_down = matmul(u, dd, "tn", (BF,), name="mlp_dwdown").reshape(w_down.g8.shape)
    dp = matmul(dd, w_down, "nt", (BF,), epi=_drelu2_epi, extras=(r,), name="mlp_du")
    dw_up = matmul(hn, dp, "tn", (BF,), out_blocks=True, name="mlp_dwup")
    dhn = matmul(dp, w_up, "nt", name="mlp_dhn")
    dh, dg_pre, *below = norm_bwd(h, g_pre, dhn, dres=dh2, then=then, name="mlp_pre_bwd")
    return dh, dict(mlp_pre_g=dg_pre, mlp_post_g=dg_post, w_up=dw_up, w_down=dw_down), (tuple(below) or None)


def rg_layer_fwd(h, g_pre, g_post, w_x, w_y, rgp, w_a, w_i, w_out, hn=None, g_next=None):
    if hn is None:
        hn = norm_fwd(h, g_pre, BF, name="rg_norm")
    xr = matmul(hn, w_x, "nn", name="rg_in_x")
    gp = matmul(hn, w_y, "nn", name="rg_in_y")
    hg, hs = rg_fwd(xr, gp, rgp, w_a, w_i)
    m = matmul(hg, w_out, "nn", name="rg_out")
    h2, hn_next = resadd_fwd(h, m, g_post, g_next, name="rg_res")
    return h2, (h, hn, xr, gp, hg, hs, m), hn_next


def rg_layer_bwd(res, dh2, g_pre, g_post, w_x, w_y, rgp, w_a, w_i, w_out, post=None, then=None):
    h, hn, xr, gp, hg, hs, m = res
    dm, dg_post = post if post is not None else norm_bwd(m, g_post, dh2, mask_pad=True, out_dtype=BF, name="rg_post_bwd")
    dw_out = matmul(hg, dm, "tn", name="rg_dwout")
    dhg = matmul(dm, w_out, "nt", name="rg_dhg")
    dxr, dgp, drgp, dwa, dwi = rg_bwd(xr, gp, rgp, w_a, w_i, hs, dhg)
    dw_x = matmul(hn, dxr, "tn", name="rg_dwx")
    dw_y = matmul(hn, dgp, "tn", name="rg_dwy")
    dhn = matmul([dxr, dgp], [w_x, w_y], "nt", name="rg_dhn")
    dh, dg_pre, *below = norm_bwd(h, g_pre, dhn, dres=dh2, then=then, name="rg_pre_bwd")
    grads = dict(mix_pre_g=dg_pre, mix_post_g=dg_post, rg_w_x=dw_x, rg_w_y=dw_y,
                 rg_conv_w=drgp[0:4], rg_conv_b=drgp[4], rg_b_a=drgp[5], rg_b_i=drgp[6], rg_lambda=drgp[7],
                 rg_w_a=dwa, rg_w_i=dwi, rg_w_out=dw_out)
    return dh, grads, (tuple(below) or None)


def sm_layer_fwd(h, g_pre, g_post, w_in_p, convp, dt_bias, a_log, dskip, ssd_g, q_g, w_q_p, kv_g, w_kv_p, w_out, tabs,
                 carry=None, on_carried=None, hn=None, g_next=None):
    if hn is None:
        hn = norm_fwd(h, g_pre, BF, name="sm_norm")
    proj = matmul(hn, w_in_p, "nn", name="sm_in")
    xbc_c = conv_silu_fwd(proj, OFF_XBC, SSD_CONV_CH, convp, name="ssd_conv")
    y, hst = ssd_fwd(xbc_c, proj, dt_bias, a_log, dskip)
    y_ssd = gated_norm_fwd(y, proj, ssd_g)
    cqn = norm_fwd(proj, q_g, BF, col_blk=OFF_CQ // MLA_Q_RANK, width=MLA_Q_RANK, name="q_norm")
    q_raw = matmul(cqn, w_q_p, "nn", name="q_up")
    ckvn = norm_fwd(proj, kv_g, BF, col_blk=OFF_CKV // MLA_KV_RANK, width=MLA_KV_RANK, name="kv_norm")
    kv_raw = matmul(ckvn, w_kv_p, "nn", name="kv_up")
    q_cat, k_cat, v = rope_fwd(q_raw, kv_raw, proj, tabs)
    o, lse, carried = attn_fwd(q_cat, k_cat, v, carry)
    if on_carried is not None:
        on_carried(carried)
    w_out = w_out()
    half = w_out.shape[0] // 2
    m = matmul([y_ssd, o], [KBlock(w_out, half, 0), KBlock(w_out, half, 1)], "nn", name="sm_out")
    res = (h, hn, proj, xbc_c, y, hst, cqn, ckvn, q_cat, k_cat, v, o, lse, y_ssd, m)
    h2, hn_next = resadd_fwd(h, m, g_post, g_next, name="sm_res")
    return h2, res, hn_next


def sm_layer_bwd(res, dh2, g_pre, g_post, w_in_p, convp, dt_bias, a_log, dskip, ssd_g, q_g, w_q_p, kv_g, w_kv_p, w_out, tabs,
                 carry=None, post=None, then=None):
    h, hn, proj, xbc_c, y, hst, cqn, ckvn, q_cat, k_cat, v, o, lse, y_ssd, m = res
    w_out = w_out()
    dm, dg_post = post if post is not None else norm_bwd(m, g_post, dh2, mask_pad=True, out_dtype=BF, name="sm_post_bwd")
    dw_out = jnp.concatenate([matmul(y_ssd, dm, "tn", name="sm_dwout_ssd"), matmul(o, dm, "tn", name="sm_dwout_att")], axis=0)
    dyab = matmul(dm, w_out, "nt", name="sm_dyab")
    dq_cat, dk_cat, dv, carried = attn_bwd(q_cat, k_cat, v, o, lse, dyab, carry(dw_out) if carry is not None else None)
    dq_raw, dkr = rope_bwd(dq_cat, dk_cat, tabs)
    kw = MLA_HEADS * LANE
    dw_kv_p = jnp.concatenate([matmul(ckvn, dk_cat, "tn", name="kv_dw_k"), matmul(ckvn, dv, "tn", name="kv_dw_v")], axis=1)
    dckvn = matmul([dk_cat, dv], [KBlock(w_kv_p, kw, 0), KBlock(w_kv_p, kw // 2, 2)], "nt", name="kv_dx")
    dckv, dg_kv = norm_bwd(proj, kv_g, dckvn, out_dtype=BF, col_blk=OFF_CKV // MLA_KV_RANK, width=MLA_KV_RANK,
                           name="kv_norm_bwd")
    dw_q_p = matmul(cqn, dq_raw, "tn", name="q_dw")
    dcqn = matmul(dq_raw, w_q_p, "nt", name="q_dx")
    dcq, dg_q = norm_bwd(proj, q_g, dcqn, out_dtype=BF, col_blk=OFF_CQ // MLA_Q_RANK, width=MLA_Q_RANK, name="q_norm_bwd")
    dy, dz, dg_ssd = gated_norm_bwd(y, proj, ssd_g, dyab)
    dxbc_c, ddt, dpar = ssd_bwd(xbc_c, proj, dt_bias, a_log, dskip, hst, dy)
    dxbc, dconvp = conv_silu_bwd(proj, OFF_XBC, SSD_CONV_CH, convp, dxbc_c, name="ssd_conv_bwd")
    pieces = [dz, dxbc, dckv, ddt, dkr, dcq]
    dw_in_p = jnp.concatenate([matmul(hn, pc, "tn", (BF,), name="sm_dwin_%d" % i) for i, pc in enumerate(pieces)], axis=1)
    third = SSD_CONV_CH // 3
    a_terms = [dz] + [KBlock(dxbc, third, i) for i in range(3)] + [dckv, ddt, dkr, dcq]
    b_terms = ([KBlock(w_in_p, SSD_D_INNER, 0)] + [KBlock(w_in_p, third, OFF_XBC // third + i) for i in range(3)]
               + [KBlock(w_in_p, MLA_KV_RANK, OFF_CKV // MLA_KV_RANK), KBlock(w_in_p, LANE, OFF_DT // LANE),
                  KBlock(w_in_p, LANE, OFF_KR // LANE), KBlock(w_in_p, MLA_Q_RANK, OFF_CQ // MLA_Q_RANK)])
    dhn = matmul(a_terms, b_terms, "nt", name="sm_dhn")
    dh, dg_pre, *below = norm_bwd(h, g_pre, dhn, dres=dh2, then=then, name="sm_pre_bwd")
    grads = dict(mix_pre_g=dg_pre, mix_post_g=dg_post, w_in=w_in_cols_to_blocks(dw_in_p), ssd_conv_w=dconvp[0:4],
                 ssd_conv_b=dconvp[4], ssd_dt_bias=dpar[0, :SSD_HEADS], ssd_a_log=dpar[1, :SSD_HEADS],
                 ssd_d=dpar[2, :SSD_HEADS], ssd_norm_g=dg_ssd, mla_q_norm_g=dg_q, mla_w_q_up=_unpack_w_q(dw_q_p),
                 mla_kv_norm_g=dg_kv, mla_w_kv_up=_unpack_w_kv(dw_kv_p), w_out_ab=dw_out)
    return dh, grads, carried, (tuple(below) or None)


W_IN_COLS = 3248
W_IN_SHARD = W_IN_COLS // N_DEV
W_IN_WIRE = 512


def _w_in_tables():
    src = np.full((IN_W,), -1, np.int64)
    src[0:2560] = np.arange(2560)
    src[OFF_CKV:OFF_CKV + 256] = 2960 + np.arange(256)
    src[OFF_DT:OFF_DT + 16] = 2560 + np.arange(16)
    src[OFF_KR + 64:OFF_KR + 96] = 3216 + np.arange(32)
    src[OFF_CQ:OFF_CQ + 384] = 2576 + np.arange(384)
    dev = np.where(src >= 0, src // W_IN_SHARD, -1).astype(np.int32).reshape(1, IN_W)
    col = np.where(src >= 0, src % W_IN_SHARD, 0).astype(np.int32).reshape(1, IN_W)
    return dev, col


W_IN_TILE = 384


def _w_in_devices_of_tile(dev):
    return [sorted(set(dev[0, t * W_IN_TILE:(t + 1) * W_IN_TILE].tolist()) - {-1}) for t in range(IN_W // W_IN_TILE)]


def _any_of(index, values):
    cond = index == values[0]
    for v in values[1:]:
        cond = cond | (index == v)
    return cond


def w_in_blocks_to_cols(g8):
    _, k, wp = g8.shape
    tn = W_IN_TILE
    dev, col = _w_in_tables()
    holders = _w_in_devices_of_tile(dev)

    def body(g_ref, dev_ref, col_ref, o_ref):
        i = pl.program_id(0)
        row = lax.broadcasted_iota(jnp.int32, (wp, tn), 0)
        o_ref[...] = jnp.zeros_like(o_ref)
        for j in range(N_DEV):
            tiles = [t for t, devs in enumerate(holders) if j in devs]
            if tiles:
                @pl.when(_any_of(i, tiles))
                def _(j=j):
                    sel = ((row == col_ref[...]) & (dev_ref[...] == j)).astype(BF)
                    o_ref[...] += jnp.dot(g_ref[j], sel, preferred_element_type=F32).astype(o_ref.dtype)

    dev, col = jnp.asarray(dev), jnp.asarray(col)
    return pl.pallas_call(
        body,
        out_shape=_sds((k, IN_W), BF),
        grid=(IN_W // tn,),
        in_specs=[pl.BlockSpec((N_DEV, k, wp), lambda i: (0, 0, 0)), pl.BlockSpec((1, tn), lambda i: (0, i)),
                  pl.BlockSpec((1, tn), lambda i: (0, i))],
        out_specs=pl.BlockSpec((k, tn), lambda i: (0, i)),
        compiler_params=_cp(("parallel",)),
        name="w_in_cols",
    )(g8, dev, col)


def w_in_cols_to_blocks(dw):
    k = dw.shape[0]
    tn = W_IN_TILE
    dev, col = _w_in_tables()
    holders = _w_in_devices_of_tile(dev)

    def body(dw_ref, dev_ref, col_ref, o_ref):
        j = pl.program_id(0)
        row = lax.broadcasted_iota(jnp.int32, (W_IN_WIRE, tn), 0)
        o_ref[...] = jnp.zeros_like(o_ref)
        for t, devs in enumerate(holders):
            if devs:
                @pl.when(_any_of(j, devs))
                def _(t=t):
                    cols = slice(t * tn, (t + 1) * tn)
                    sel = ((row == col_ref[:, cols]) & (dev_ref[:, cols] == j)).astype(BF)
                    o_ref[0] += lax.dot_general(dw_ref[:, cols], sel, NT_DIMS,
                                                preferred_element_type=F32).astype(o_ref.dtype)

    dev, col = jnp.asarray(dev), jnp.asarray(col)
    return pl.pallas_call(
        body,
        out_shape=_sds((N_DEV, k, W_IN_WIRE), BF),
        grid=(N_DEV,),
        in_specs=[pl.BlockSpec((k, IN_W), lambda j: (0, 0)), pl.BlockSpec((1, IN_W), lambda j: (0, 0)),
                  pl.BlockSpec((1, IN_W), lambda j: (0, 0))],
        out_specs=pl.BlockSpec((1, k, W_IN_WIRE), lambda j: (j, 0, 0)),
        compiler_params=_cp(("parallel",)),
        name="w_in_blocks",
    )(dw, dev, col)


def _pack_w_q(w):
    w3 = w.reshape(w.shape[0], MLA_HEADS, MLA_NOPE + MLA_ROPE)
    return jnp.pad(w3, ((0, 0), (0, 0), (0, LANE - MLA_NOPE - MLA_ROPE))).reshape(w.shape[0], MLA_HEADS * LANE)


def _unpack_w_q(p):
    return p.reshape(p.shape[0], MLA_HEADS, LANE)[:, :, :MLA_NOPE + MLA_ROPE].reshape(p.shape[0], -1)


def _pack_w_kv(w):
    w3 = w.reshape(w.shape[0], MLA_HEADS, MLA_NOPE + MLA_V)
    k = jnp.pad(w3[:, :, :MLA_NOPE], ((0, 0), (0, 0), (0, LANE - MLA_NOPE))).reshape(w.shape[0], MLA_HEADS * LANE)
    return jnp.concatenate([k, w3[:, :, MLA_NOPE:].reshape(w.shape[0], MLA_HEADS * MLA_V)], axis=1)


def _unpack_w_kv(p):
    k = p[:, :MLA_HEADS * LANE].reshape(p.shape[0], MLA_HEADS, LANE)[:, :, :MLA_NOPE]
    v = p[:, MLA_HEADS * LANE:].reshape(p.shape[0], MLA_HEADS, MLA_V)
    return jnp.concatenate([k, v], axis=2).reshape(p.shape[0], -1)


def _rows8(rows, width):
    a = jnp.concatenate([r.reshape(-1, width) for r in rows], axis=0)
    return jnp.pad(a, ((0, 8 - a.shape[0]), (0, 0)))


SLAB_ROWS = 16


def _to_slab(flat_list, lead=()):
    cat = jnp.concatenate(flat_list, axis=-1)
    n = cat.shape[-1]
    unit = SLAB_ROWS * PACK_W
    total = -(-n // unit) * unit
    cat = jnp.pad(cat, [(0, 0)] * len(lead) + [(0, total - n)])
    return cat.reshape(lead + (total // PACK_W, PACK_W))


def _from_flat(flat, shapes):
    out, off = [], 0
    for s in shapes:
        n = int(np.prod(s))
        out.append(flat[off:off + n].reshape(s))
        off += n
    return out


def _gathered_full(g8, axis):
    moved = jnp.moveaxis(g8, 0, axis)
    shp = moved.shape
    return moved.reshape(shp[:axis] + (shp[axis] * shp[axis + 1],) + shp[axis + 2:])


def _per_device(full, axis):
    shp = full.shape
    split = full.reshape(shp[:axis] + (N_DEV, shp[axis] // N_DEV) + shp[axis + 1:])
    return jnp.moveaxis(split, axis, 0)


ARG_NAMES = ['x', 'meta_tokens', 'mix_pre_g', 'mix_post_g', 'mlp_pre_g', 'mlp_post_g', 'w_up', 'w_down', 'w_in',
             'ssd_conv_w', 'ssd_conv_b', 'ssd_dt_bias', 'ssd_a_log', 'ssd_d', 'ssd_norm_g', 'mla_q_norm_g',
             'mla_w_q_up', 'mla_kv_norm_g', 'mla_w_kv_up', 'w_out_ab', 'rg_w_x', 'rg_w_y', 'rg_conv_w', 'rg_conv_b',
             'rg_w_a', 'rg_b_a', 'rg_w_i', 'rg_b_i', 'rg_lambda', 'rg_w_out']
WEIGHTS = ARG_NAMES[1:]
BIG = {'w_up': 2, 'w_down': 1, 'w_in': 2, 'mla_w_q_up': 2, 'mla_w_kv_up': 2, 'w_out_ab': 1, 'rg_w_x': 2,
       'rg_w_y': 2, 'rg_w_out': 1}
SMALL = {'meta_tokens': 1, 'ssd_conv_w': 2, 'rg_conv_w': 2, 'rg_conv_b': 1, 'rg_b_a': 1, 'rg_b_i': 1, 'rg_lambda': 1}
REPL = [n for n in WEIGHTS if n not in BIG and n not in SMALL]
REPL_MEDIUM = ['rg_w_a', 'rg_w_i']
REPL_TINY = [n for n in REPL if n not in REPL_MEDIUM]


def _piece_axes():
    axes = {}
    for n, ax in BIG.items():
        for i in range(DEPTH if n in ('w_up', 'w_down') else DEPTH // 2):
            axes[(n, i)] = ax - 1
    return axes


PIECE_AXIS = _piece_axes()
AS_BLOCKS = ('w_up', 'w_down')
_RG = lambda i: [(n, i) for n in ('rg_w_x', 'rg_w_y', 'rg_w_out')]
_MLP = lambda l: [('w_up', l), ('w_down', l)]
_SM_IN = lambda i: [(n, i) for n in ('w_in', 'mla_w_q_up', 'mla_w_kv_up')]
GATHER_FIRST = _SM_IN(0)
GATHER_AT = {0: [('w_out_ab', 0)] + _MLP(0) + _RG(0) + _MLP(1) + _SM_IN(1), 2: [('w_out_ab', 1)] + _MLP(2) + _RG(1) + _MLP(3)}
SCATTER_AT = {2: _MLP(3) + _RG(1) + _MLP(2) + [('w_out_ab', 1)],
              0: _SM_IN(1) + _MLP(1) + _RG(0) + _MLP(0) + [('w_out_ab', 0)]}
SCATTER_LAST = _SM_IN(0)


def _wire_block(p, key):
    n, i = key
    blk = p[n][i]
    if n == 'w_in':
        blk = jnp.pad(blk, ((0, 0), (0, W_IN_WIRE - blk.shape[1])))
    return blk


def _step(p, moments):
    assert DEPTH == 4
    full = {n: [None] * p[n].shape[0] for n in BIG}
    full['w_in_g'] = [None] * p['w_in'].shape[0]

    def weight_blocks(group):
        return [_wire_block(p, k).astype(BF) for k in group]

    def take_weights(group, gathered):
        for (n, i), piece in zip(group, gathered):
            if n == 'w_in':
                full['w_in_g'][i] = piece
            elif n in AS_BLOCKS:
                full[n][i] = DevBlocks(piece, PIECE_AXIS[(n, i)])
            else:
                full[n][i] = _gathered_full(piece, PIECE_AXIS[(n, i)])

    def grad_blocks(group, gw):
        return [gw[k] if k[0] in AS_BLOCKS or k[0] == 'w_in' else _per_device(gw[k], PIECE_AXIS[k]).astype(BF)
                for k in group]

    parts = {}

    small_slab = _to_slab([p[n].reshape(-1) for n in SMALL])
    *first, small8 = all_gather(weight_blocks(GATHER_FIRST) + [small_slab], name="gather_first")
    take_weights(GATHER_FIRST, first)
    for n, piece in zip(SMALL, _from_flat_rows(small8, [p[n].shape for n in SMALL])):
        full[n] = _gathered_full(piece, SMALL[n])
    for n in REPL:
        full[n] = p[n]
    loss_local, grad_x, gw, gsmall_full, carried = _local_step(
        full, p['x'][0], p['loss_target'][0],
        fwd_carry=lambda layer: Exchange("gather", weight_blocks(GATHER_AT[layer])),
        on_fwd_carried=lambda layer, got: take_weights(GATHER_AT[layer], got),
        bwd_carry=lambda layer, gw_now, others: Exchange(
            "scatter", grad_blocks(SCATTER_AT[layer], gw_now)
            + ([jnp.stack(others[n], axis=0).reshape(N_DEV, -1, LANE) for n in REPL_MEDIUM] if layer == 0 else [])))

    for layer, group in SCATTER_AT.items():
        parts.update(zip(group, carried[layer]))
    rep_flat = jnp.concatenate([gsmall_full[n].reshape(-1) for n in REPL_TINY])
    rep_n = rep_flat.shape[0]
    rep_chunk = -(-rep_n // (N_DEV * PACK_W * 8)) * PACK_W * 8
    rep8 = jnp.pad(rep_flat, (0, N_DEV * rep_chunk - rep_n)).reshape(N_DEV, rep_chunk)
    gsmall = _to_slab([_per_device(gsmall_full[n], SMALL[n]).reshape(N_DEV, -1) for n in SMALL] + [rep8], lead=(N_DEV,))
    received = all_to_all(grad_blocks(SCATTER_LAST, gw) + [gsmall], name="scatter_last")
    n_last = len(SCATTER_LAST)
    parts.update(zip(SCATTER_LAST, received[:n_last]))
    ssmall = slab_sum(received[n_last], name="sum_small").reshape(-1)
    medium_mine = [slab_sum(r8, name="sum_" + n) for n, r8 in zip(REPL_MEDIUM, carried[0][len(SCATTER_AT[0]):])]
    g_loc = {'w_in': jnp.stack([slab_sum(parts[('w_in', i)], name="sum_w_in_%d" % i)[:, :W_IN_SHARD]
                                for i in range(p['w_in'].shape[0])], axis=0)}
    small_n = sum(int(np.prod(p[n].shape)) for n in SMALL)
    g_loc.update(zip(SMALL, _from_flat(ssmall, [p[n].shape for n in SMALL])))
    rep_mine = ssmall[small_n:small_n + rep_chunk].reshape(-1, PACK_W)
    rep_all, *medium_all = all_gather([rep_mine] + medium_mine, name="gather_replicated")
    g_loc.update(zip(REPL_TINY, _from_flat(rep_all.reshape(-1), [p[n].shape for n in REPL_TINY])))
    g_loc.update({n: g.reshape(p[n].shape) for n, g in zip(REPL_MEDIUM, medium_all)})

    out = {'loss': lax.psum(loss_local, ("x", "y", "c")), 'grad_x': grad_x[None]}
    small_names = list(SMALL) + REPL_TINY
    for n in list(BIG) + REPL_MEDIUM:
        shp = p[n].shape
        if n == 'w_in' or n in REPL_MEDIUM:
            v2 = lambda a: a.reshape(-1, shp[-1])
            d, nm, nv = adamw(v2(p[n]), v2(g_loc[n]), v2(moments['m_' + n]), v2(moments['v_' + n]), name="adamw_" + n)
            d, nm, nv = d.reshape(shp), nm.reshape(shp), nv.reshape(shp)
        else:
            g_loc[n], d, nm, nv = adamw_blocks(p[n], moments['m_' + n], moments['v_' + n],
                                               [parts[(n, i)] for i in range(shp[0])], name="adamw_" + n)
        out['delta_' + n], out['new_m_' + n], out['new_v_' + n] = d, nm, nv
    slab = lambda src: _to_slab([src(n).reshape(-1) for n in small_names])
    d, nm, nv = adamw(slab(lambda n: p[n]), slab(lambda n: g_loc[n]), slab(lambda n: moments['m_' + n]),
                      slab(lambda n: moments['v_' + n]), name="adamw_small")
    shapes = [p[n].shape for n in small_names]
    for key, flat in (('delta_', d), ('new_m_', nm), ('new_v_', nv)):
        for n, a in zip(small_names, _from_flat(flat.reshape(-1), shapes)):
            out[key + n] = a
    for n in WEIGHTS:
        out['grad_' + n] = g_loc[n]
    return out


def _local_step(full, x, target_rows, fwd_carry=None, on_fwd_carried=None, bwd_carry=None):
    t = PAD + N_META + x.shape[0]
    h = jnp.concatenate([jnp.zeros((PAD, D_MODEL), F32), full['meta_tokens'], x], axis=0)
    target = jnp.concatenate([jnp.zeros((PAD + N_META, D_MODEL), F32), target_rows], axis=0)
    tabs = rope_tables(t)

    def layer_args(layer):
        i = layer // 2
        if layer % 2 == 0:
            convp = _rows8([full['ssd_conv_w'][i], full['ssd_conv_b'][i]], SSD_CONV_CH)
            return (full['mix_pre_g'][layer], full['mix_post_g'][layer], w_in_blocks_to_cols(full['w_in_g'][i]), convp,
                    full['ssd_dt_bias'][i], full['ssd_a_log'][i], full['ssd_d'][i], full['ssd_norm_g'][i],
                    full['mla_q_norm_g'][i], _pack_w_q(full['mla_w_q_up'][i]), full['mla_kv_norm_g'][i],
                    _pack_w_kv(full['mla_w_kv_up'][i]), lambda: full['w_out_ab'][i], tabs)
        rgp = _rows8([full['rg_conv_w'][i], full['rg_conv_b'][i], full['rg_b_a'][i], full['rg_b_i'][i],
                      full['rg_lambda'][i]], LRU_WIDTH)
        return (full['mix_pre_g'][layer], full['mix_post_g'][layer], full['rg_w_x'][i], full['rg_w_y'][i], rgp,
                full['rg_w_a'][i], full['rg_w_i'][i], full['rg_w_out'][i])

    def mlp_args(layer):
        return (full['mlp_pre_g'][layer], full['mlp_post_g'][layer], full['w_up'][layer], full['w_down'][layer])

    saved = []
    hn = None
    for layer in range(DEPTH):
        la = layer_args(layer)
        to_mlp = dict(hn=hn, g_next=full['mlp_pre_g'][layer])
        if layer % 2 == 0:
            if fwd_carry is not None:
                h, res_mix, hn = sm_layer_fwd(h, *la, carry=fwd_carry(layer),
                                              on_carried=lambda got, layer=layer: on_fwd_carried(layer, got), **to_mlp)
            else:
                h, res_mix, hn = sm_layer_fwd(h, *la, **to_mlp)
        else:
            h, res_mix, hn = rg_layer_fwd(h, *la, **to_mlp)
        ma = mlp_args(layer)
        h, res_mlp, hn = mlp_fwd(h, *ma, hn=hn, g_next=full['mix_pre_g'][layer + 1] if layer + 1 < DEPTH else None)
        saved.append((la, ma, res_mix, res_mlp))
    loss_local, dh = loss_fwd_bwd(h, target)

    others = {n: [None] * len(full[n]) for n in WEIGHTS if n not in BIG and n != 'meta_tokens'}
    gw, carried = {}, {}
    post = None
    for layer in reversed(range(DEPTH)):
        la, ma, res_mix, res_mlp = saved[layer]
        dh, gm, post = mlp_bwd(res_mlp, dh, *ma, post=post, then=(res_mix[-1], la[1]))
        below = (saved[layer - 1][3][-1], saved[layer - 1][1][1]) if layer > 0 else None
        if layer % 2 == 0:
            for n in ('w_up', 'w_down'):
                gw[(n, layer)] = gm[n]
            carry = None
            if bwd_carry is not None:
                carry = lambda dw_out, layer=layer: bwd_carry(layer, {**gw, ('w_out_ab', layer // 2): dw_out}, others)
            dh, gx, carried[layer], post = sm_layer_bwd(res_mix, dh, *la, carry=carry, post=post, then=below)
        else:
            dh, gx, post = rg_layer_bwd(res_mix, dh, *la, post=post, then=below)
        for n, g in list(gm.items()) + list(gx.items()):
            i = layer if n in ('mix_pre_g', 'mix_post_g', 'mlp_pre_g', 'mlp_post_g', 'w_up', 'w_down') else layer // 2
            if n in BIG:
                gw[(n, i)] = g
            else:
                others[n][i] = g
    gothers = {n: jnp.stack(v, axis=0) for n, v in others.items()}
    gothers['meta_tokens'] = dh[PAD:PAD + N_META]
    return loss_local, dh[PAD + N_META:], gw, gothers, carried


def _from_flat_rows(g8, shapes):
    flat = g8.reshape(N_DEV, -1)
    out, off = [], 0
    for s in shapes:
        n = int(np.prod(s))
        out.append(flat[:, off:off + n].reshape((N_DEV,) + tuple(s)))
        off += n
    return out


def kernel(x, meta_tokens, mix_pre_g, mix_post_g, mlp_pre_g, mlp_post_g, w_up, w_down, w_in, ssd_conv_w, ssd_conv_b, ssd_dt_bias, ssd_a_log, ssd_d, ssd_norm_g, mla_q_norm_g, mla_w_q_up, mla_kv_norm_g, mla_w_kv_up, w_out_ab, rg_w_x, rg_w_y, rg_conv_w, rg_conv_b, rg_w_a, rg_b_a, rg_w_i, rg_b_i, rg_lambda, rg_w_out, loss_target, m_meta_tokens, m_mix_pre_g, m_mix_post_g, m_mlp_pre_g, m_mlp_post_g, m_w_up, m_w_down, m_w_in, m_ssd_conv_w, m_ssd_conv_b, m_ssd_dt_bias, m_ssd_a_log, m_ssd_d, m_ssd_norm_g, m_mla_q_norm_g, m_mla_w_q_up, m_mla_kv_norm_g, m_mla_w_kv_up, m_w_out_ab, m_rg_w_x, m_rg_w_y, m_rg_conv_w, m_rg_conv_b, m_rg_w_a, m_rg_b_a, m_rg_w_i, m_rg_b_i, m_rg_lambda, m_rg_w_out, v_meta_tokens, v_mix_pre_g, v_mix_post_g, v_mlp_pre_g, v_mlp_post_g, v_w_up, v_w_down, v_w_in, v_ssd_conv_w, v_ssd_conv_b, v_ssd_dt_bias, v_ssd_a_log, v_ssd_d, v_ssd_norm_g, v_mla_q_norm_g, v_mla_w_q_up, v_mla_kv_norm_g, v_mla_w_kv_up, v_w_out_ab, v_rg_w_x, v_rg_w_y, v_rg_conv_w, v_rg_conv_b, v_rg_w_a, v_rg_b_a, v_rg_w_i, v_rg_b_i, v_rg_lambda, v_rg_w_out):
    args = (x, meta_tokens, mix_pre_g, mix_post_g, mlp_pre_g, mlp_post_g, w_up, w_down, w_in, ssd_conv_w, ssd_conv_b, ssd_dt_bias, ssd_a_log, ssd_d, ssd_norm_g, mla_q_norm_g, mla_w_q_up, mla_kv_norm_g, mla_w_kv_up, w_out_ab, rg_w_x, rg_w_y, rg_conv_w, rg_conv_b, rg_w_a, rg_b_a, rg_w_i, rg_b_i, rg_lambda, rg_w_out, loss_target, m_meta_tokens, m_mix_pre_g, m_mix_post_g, m_mlp_pre_g, m_mlp_post_g, m_w_up, m_w_down, m_w_in, m_ssd_conv_w, m_ssd_conv_b, m_ssd_dt_bias, m_ssd_a_log, m_ssd_d, m_ssd_norm_g, m_mla_q_norm_g, m_mla_w_q_up, m_mla_kv_norm_g, m_mla_w_kv_up, m_w_out_ab, m_rg_w_x, m_rg_w_y, m_rg_conv_w, m_rg_conv_b, m_rg_w_a, m_rg_b_a, m_rg_w_i, m_rg_b_i, m_rg_lambda, m_rg_w_out, v_meta_tokens, v_mix_pre_g, v_mix_post_g, v_mlp_pre_g, v_mlp_post_g, v_w_up, v_w_down, v_w_in, v_ssd_conv_w, v_ssd_conv_b, v_ssd_dt_bias, v_ssd_a_log, v_ssd_d, v_ssd_norm_g, v_mla_q_norm_g, v_mla_w_q_up, v_mla_kv_norm_g, v_mla_w_kv_up, v_w_out_ab, v_rg_w_x, v_rg_w_y, v_rg_conv_w, v_rg_conv_b, v_rg_w_a, v_rg_b_a, v_rg_w_i, v_rg_b_i, v_rg_lambda, v_rg_w_out,)
    n_w = len(ARG_NAMES)
    p = dict(zip(ARG_NAMES, args[:n_w]))
    p['loss_target'] = args[n_w]
    moments = {}
    for i, n in enumerate(WEIGHTS):
        moments['m_' + n] = args[n_w + 1 + i]
        moments['v_' + n] = args[n_w + 1 + len(WEIGHTS) + i]
    out = _step(p, moments)
    res = [out['loss'], out['grad_x']]
    for prefix in ('grad_', 'delta_', 'new_m_', 'new_v_'):
        res += [out[prefix + n] for n in WEIGHTS]
    return tuple(res)
```

```python
import math

import numpy as np
import jax
import jax.numpy as jnp
from jax import lax
from jax.experimental import pallas as pl
from jax.experimental.pallas import tpu as pltpu

F32 = jnp.float32
BF = jnp.bfloat16
HI = lax.Precision.HIGHEST

D_MODEL = 1024
DEPTH = 4
N_META = 16
CHUNK = 128
PAD = CHUNK - N_META
EPS = 1e-6
SSD_HEADS = 16
SSD_HEAD_DIM = 64
SSD_D_INNER = 1024
SSD_STATE = 128
SSD_CONV_CH = 1536
MLA_HEADS = 16
MLA_NOPE = 64
MLA_ROPE = 32
MLA_V = 64
MLA_Q_RANK = 384
MLA_KV_RANK = 256
ROPE_BASE = 10000.0
LRU_WIDTH = 1280
LRU_BLOCKS = 10
LRU_C = 8.0
D_FF = 4096
N_DEV = 8
LANE = 128
IN_W = 3456
OFF_Z, OFF_XBC, OFF_CKV, OFF_DT, OFF_KR, OFF_CQ = 0, 1024, 2560, 2816, 2944, 3072

ADAM_LR = 0.001
ADAM_B1 = 0.9
ADAM_B2 = 0.999
ADAM_EPS = 1e-08
ADAM_WD = 0.01
ADAM_STEP = 10

VMEM_LIMIT = 56 * 1024 * 1024
NEG = -1e30


def _pick(n, cands):
    for c in cands:
        if n % c == 0:
            return c
    return n


def _cp(sem=None):
    return pltpu.CompilerParams(dimension_semantics=sem, vmem_limit_bytes=VMEM_LIMIT)


def _sds(shape, dtype):
    return jax.ShapeDtypeStruct(tuple(shape), dtype)


def _silu(x):
    return x * jax.nn.sigmoid(x)


def _softplus(x):
    return jnp.maximum(x, 0.0) + jnp.log(1.0 + jnp.exp(-jnp.abs(x)))


def _gelu(x):
    c = math.sqrt(2.0 / math.pi)
    return 0.5 * x * (1.0 + jnp.tanh(c * (x + 0.044715 * (x * x * x))))


def _row_mask(i, tr, shape, first_valid=PAD):
    row = i * tr + lax.broadcasted_iota(jnp.int32, shape, 0)
    return row >= first_valid


class KBlock:
    def __init__(self, arr, width, blk):
        self.arr, self.width, self.blk = arr, width, blk


class DevBlocks:
    def __init__(self, g8, axis):
        self.g8, self.axis = g8, axis
        _, r, c = g8.shape
        self.shape = (N_DEV * r, c) if axis == 0 else (r, N_DEV * c)


NN_DIMS = (((1,), (0,)), ((), ()))
MM_TALL_K = 1536
MM_WHOLE_K, MM_WHOLE_M = 1024, 4224


def matmul(a, b, mode, out_dtypes=(F32,), epi=None, extras=(), name="mm", tm=None, tn=None, out_blocks=False):
    a_terms = a if isinstance(a, (list, tuple)) else [a]
    b_terms = b if isinstance(b, (list, tuple)) else [b]
    assert len(a_terms) == len(b_terms) and (mode != "tn" or len(a_terms) == 1)
    arr_of = lambda t: t.arr if isinstance(t, KBlock) else t
    if mode == "tn":
        m, n = a_terms[0].shape[1], b_terms[0].shape[1]
    else:
        m = arr_of(a_terms[0]).shape[0]
        b0 = b_terms[0]
        n = (b0.shape if isinstance(b0, DevBlocks) else arr_of(b0).shape)[1 if mode == "nn" else 0]
    if mode == "tn":
        tm = _pick(m, (1024, 512, 384, 256, 128))
    else:
        k_all = sum(t.width if isinstance(t, KBlock) else t.shape[1] for t in a_terms)
        tall = (2112,) if k_all <= MM_TALL_K else ()
        if k_all <= MM_WHOLE_K and m <= MM_WHOLE_M and n % 256 == 0 and tm is None and tn is None:
            tm, tn = m, 256
        tm = tm or _pick(m, tall + (1056, 1024, 768, 640, 512, 384, 256, 128))
    tn = tn or _pick(n, (512, 640, 384, 256, 128))
    dims = {"nn": NN_DIMS, "nt": NT_DIMS, "tn": TN_DIMS}[mode]

    in_specs, args, plan = [], [], []
    for ta, tb in zip(a_terms, b_terms):
        if mode == "tn":
            k = ta.shape[0]
            in_specs += [pl.BlockSpec((k, tm), lambda i, j: (0, i)), pl.BlockSpec((k, tn), lambda i, j: (0, j))]
            args += [ta, tb]
            plan.append(None)
            continue
        if isinstance(ta, KBlock):
            kw, ka = ta.width, ta.blk
            in_specs.append(pl.BlockSpec((tm, kw), lambda i, j, ka=ka: (i, ka)))
        else:
            kw = ta.shape[1]
            in_specs.append(pl.BlockSpec((tm, kw), lambda i, j: (i, 0)))
        args.append(arr_of(ta))
        if isinstance(tb, DevBlocks):
            _, r, c = tb.g8.shape
            split_k = tb.axis == (0 if mode == "nn" else 1)
            if split_k:
                kd = r if mode == "nn" else c
                assert kw == N_DEV * kd
                blk = (N_DEV, kd, tn) if mode == "nn" else (N_DEV, tn, kd)
                in_specs.append(pl.BlockSpec(blk, (lambda i, j: (0, 0, j)) if mode == "nn" else (lambda i, j: (0, j, 0))))
                plan.append(kd)
            else:
                per = (c if mode == "nn" else r) // tn
                blk = (None, kw, tn) if mode == "nn" else (None, tn, kw)
                in_specs.append(pl.BlockSpec(blk, (lambda i, j, per=per: (j // per, 0, j % per)) if mode == "nn"
                                             else (lambda i, j, per=per: (j // per, j % per, 0))))
                plan.append(None)
            args.append(tb.g8)
        else:
            kb = tb.blk if isinstance(tb, KBlock) else 0
            assert (tb.width if isinstance(tb, KBlock) else tb.shape[0 if mode == "nn" else 1]) == kw
            in_specs.append(pl.BlockSpec((kw, tn), lambda i, j, kb=kb: (kb, j)) if mode == "nn"
                            else pl.BlockSpec((tn, kw), lambda i, j, kb=kb: (j, kb)))
            args.append(arr_of(tb))
            plan.append(None)
    n_terms, n_ex = len(plan), len(extras)

    def body(*refs):
        ex_refs, out_refs = refs[2 * n_terms:2 * n_terms + n_ex], refs[2 * n_terms + n_ex:]
        acc = None
        for t, kd in enumerate(plan):
            a_ref, b_ref = refs[2 * t], refs[2 * t + 1]
            if kd is None:
                parts = [lax.dot_general(a_ref[...].astype(BF), b_ref[...].astype(BF), dims, preferred_element_type=F32)]
            else:
                parts = [lax.dot_general(a_ref[:, d * kd:(d + 1) * kd].astype(BF), b_ref[d].astype(BF), dims,
                                         preferred_element_type=F32) for d in range(N_DEV)]
            for part in parts:
                acc = part if acc is None else acc + part
        outs = (acc,) if epi is None else epi(acc, *[r[...] for r in ex_refs])
        for r, o in zip(out_refs, outs):
            r[...] = o.astype(r.dtype)

    o_spec = pl.BlockSpec((tm, tn), lambda i, j: (i, j))
    if out_blocks:
        per = n // N_DEV // tn
        out_shape = tuple(_sds((N_DEV, m, n // N_DEV), dt) for dt in out_dtypes)
        out_specs = tuple(pl.BlockSpec((None, tm, tn), lambda i, j: (j // per, i, j % per)) for _ in out_dtypes)
    else:
        out_shape = tuple(_sds((m, n), dt) for dt in out_dtypes)
        out_specs = tuple(o_spec for _ in out_dtypes)
    outs = pl.pallas_call(
        body,
        out_shape=out_shape,
        grid=(m // tm, n // tn),
        in_specs=in_specs + [o_spec] * n_ex,
        out_specs=out_specs,
        compiler_params=_cp(("parallel", "parallel")),
        name=name,
    )(*args, *extras)
    return outs[0] if len(out_dtypes) == 1 else outs


def _rt(t):
    return _pick(t, (384, 256, 128))


def _rt_wide(t):
    return _pick(t, (704, 384, 256, 128))


def norm_fwd(x, g, out_dtype, col_blk=0, width=None, name="norm_fwd"):
    t = x.shape[0]
    w = width or x.shape[1]
    tr = _rt_wide(t)

    def body(x_ref, g_ref, o_ref):
        xv = x_ref[...]
        r = lax.rsqrt(jnp.mean(xv * xv, axis=-1, keepdims=True) + EPS)
        o_ref[...] = (xv * r * g_ref[...]).astype(o_ref.dtype)

    return pl.pallas_call(
        body,
        out_shape=_sds((t, w), out_dtype),
        grid=(t // tr,),
        in_specs=[pl.BlockSpec((tr, w), lambda i: (i, col_blk)), pl.BlockSpec((1, w), lambda i: (0, 0))],
        out_specs=pl.BlockSpec((tr, w), lambda i: (i, 0)),
        compiler_params=_cp(("parallel",)),
        name=name,
    )(x, g.reshape(1, w))


def _rms_bwd(xv, gv, dyv):
    r = lax.rsqrt(jnp.mean(xv * xv, axis=-1, keepdims=True) + EPS)
    xh = xv * r
    dyg = dyv * gv
    dx = r * (dyg - xh * jnp.mean(dyg * xh, axis=-1, keepdims=True))
    return dx, jnp.sum(dyv * xh, axis=0, keepdims=True)


def norm_bwd(x, g, dy, dres=None, mask_pad=False, out_dtype=F32, col_blk=0, width=None, dy_col_blk=0, then=None,
             name="norm_bwd"):
    t = x.shape[0]
    w = width or x.shape[1]
    tr = _rt_wide(t)
    has_res, has_then = dres is not None, then is not None

    def body(*refs):
        x_ref, g_ref, dy_ref = refs[:3]
        n_in = 3 + has_res + 2 * has_then
        dx_ref, dg_ref = refs[n_in:n_in + 2]
        i = pl.program_id(0)
        dyv = dy_ref[...].astype(F32)
        if mask_pad:
            dyv = jnp.where(_row_mask(i, tr, dyv.shape), dyv, 0.0)
        dx, dg = _rms_bwd(x_ref[...], g_ref[...], dyv)
        if has_res:
            dx = dx + refs[3][...]
        dx_ref[...] = dx.astype(dx_ref.dtype)

        @pl.when(i == 0)
        def _():
            for r in refs[n_in + 1::2]:
                r[...] = jnp.zeros_like(r)

        dg_ref[...] += dg
        if has_then:
            x2_ref, g2_ref = refs[3 + has_res:5 + has_res]
            dx2_ref, dg2_ref = refs[n_in + 2:]
            dx2, dg2 = _rms_bwd(x2_ref[...], g2_ref[...], jnp.where(_row_mask(i, tr, dx.shape), dx, 0.0))
            dx2_ref[...] = dx2.astype(dx2_ref.dtype)
            dg2_ref[...] += dg2

    row = pl.BlockSpec((tr, w), lambda i: (i, 0))
    vec = pl.BlockSpec((1, w), lambda i: (0, 0))
    in_specs = [pl.BlockSpec((tr, w), lambda i: (i, col_blk)), vec, pl.BlockSpec((tr, w), lambda i: (i, dy_col_blk))]
    args = [x, g.reshape(1, w), dy]
    out_shape, out_specs = [_sds((t, w), out_dtype), _sds((1, w), F32)], [row, vec]
    if has_res:
        in_specs.append(row)
        args.append(dres)
    if has_then:
        in_specs += [row, vec]
        args += [then[0], then[1].reshape(1, w)]
        out_shape += [_sds((t, w), BF), _sds((1, w), F32)]
        out_specs += [row, vec]
    outs = pl.pallas_call(
        body,
        out_shape=tuple(out_shape),
        grid=(t // tr,),
        in_specs=in_specs,
        out_specs=tuple(out_specs),
        compiler_params=_cp(("arbitrary",)),
        name=name,
    )(*args)
    if has_then:
        return outs[0], outs[1].reshape(w), outs[2], outs[3].reshape(w)
    return outs[0], outs[1].reshape(w)


def resadd_fwd(h, m, g, g_next=None, name="resadd"):
    t, w = h.shape
    tr = _rt_wide(t)
    with_next = g_next is not None

    def body(h_ref, m_ref, g_ref, *rest):
        mv = m_ref[...]
        r = lax.rsqrt(jnp.mean(mv * mv, axis=-1, keepdims=True) + EPS)
        y = mv * r * g_ref[...]
        h2 = h_ref[...] + jnp.where(_row_mask(pl.program_id(0), tr, y.shape), y, 0.0)
        if with_next:
            gn_ref, o_ref, hn_ref = rest
            r2 = lax.rsqrt(jnp.mean(h2 * h2, axis=-1, keepdims=True) + EPS)
            hn_ref[...] = (h2 * r2 * gn_ref[...]).astype(hn_ref.dtype)
        else:
            (o_ref,) = rest
        o_ref[...] = h2

    row = pl.BlockSpec((tr, w), lambda i: (i, 0))
    vec = pl.BlockSpec((1, w), lambda i: (0, 0))
    outs = pl.pallas_call(
        body,
        out_shape=(_sds((t, w), F32),) + ((_sds((t, w), BF),) if with_next else ()),
        grid=(t // tr,),
        in_specs=[row, row, vec] + ([vec] if with_next else []),
        out_specs=(row,) + ((row,) if with_next else ()),
        compiler_params=_cp(("parallel",)),
        name=name,
    )(h, m, g.reshape(1, w), *((g_next.reshape(1, w),) if with_next else ()))
    return outs[0], (outs[1] if with_next else None)


def loss_fwd_bwd(h, target):
    t, w = h.shape
    tr = _rt_wide(t)

    def body(h_ref, t_ref, s_ref, dh_ref):
        i = pl.program_id(0)
        err = h_ref[...] - t_ref[...]
        err = jnp.where(_row_mask(i, tr, err.shape, PAD + N_META), err, 0.0)
        dh_ref[...] = err * (1.0 / w)

        @pl.when(i == 0)
        def _():
            s_ref[...] = jnp.zeros_like(s_ref)

        s_ref[...] += jnp.sum(err * err).reshape(1, 1)

    s, dh = pl.pallas_call(
        body,
        out_shape=(_sds((1, LANE), F32), _sds((t, w), F32)),
        grid=(t // tr,),
        in_specs=[pl.BlockSpec((tr, w), lambda i: (i, 0)), pl.BlockSpec((tr, w), lambda i: (i, 0))],
        out_specs=(pl.BlockSpec((1, LANE), lambda i: (0, 0)), pl.BlockSpec((tr, w), lambda i: (i, 0))),
        compiler_params=_cp(("arbitrary",)),
        name="loss",
    )(h, target)
    return 0.5 * s[0, 0] / w, dh


def _shift_down(ext, k, n):
    return pltpu.roll(ext, k, 0)[8:]


def _conv_pre(ext, x, w_ref, n):
    return (w_ref[4:5, :] + w_ref[3:4, :] * x + w_ref[2:3, :] * _shift_down(ext, 1, n)
            + w_ref[1:2, :] * _shift_down(ext, 2, n) + w_ref[0:1, :] * _shift_down(ext, 3, n))


def _conv_bwd_parts(dpre, dnext, x, ext, w_ref, n):
    extd = jnp.concatenate([dpre, dnext], axis=0)
    ln = n + 8
    dx = (w_ref[3:4, :] * dpre + w_ref[2:3, :] * pltpu.roll(extd, ln - 1, 0)[:n]
          + w_ref[1:2, :] * pltpu.roll(extd, ln - 2, 0)[:n] + w_ref[0:1, :] * pltpu.roll(extd, ln - 3, 0)[:n])
    sums = [jnp.sum(dpre * _shift_down(ext, 3, n), axis=0, keepdims=True),
            jnp.sum(dpre * _shift_down(ext, 2, n), axis=0, keepdims=True),
            jnp.sum(dpre * _shift_down(ext, 1, n), axis=0, keepdims=True),
            jnp.sum(dpre * x, axis=0, keepdims=True),
            jnp.sum(dpre, axis=0, keepdims=True)]
    return dx, sums


def _rows_block(sums):
    w = sums[0].shape[1]
    row = lax.broadcasted_iota(jnp.int32, (8, w), 0)
    out = jnp.zeros((8, w), F32)
    for k, s in enumerate(sums):
        out = jnp.where(row == k, s, out)
    return out


CONV_BLOCK = 512


def conv_silu_fwd(x, col0, c, wb, name="conv_fwd"):
    t = x.shape[0]
    cw = _pick(c, (CONV_BLOCK, LANE))
    nblk, col0_blk = c // cw, col0 // cw
    assert col0 % cw == 0
    tr = _rt_wide(t)

    def body(x_ref, w_ref, o_ref, prev):
        ti = pl.program_id(1)

        @pl.when(ti == 0)
        def _():
            prev[...] = jnp.zeros_like(prev)

        xv = x_ref[...]
        ext = jnp.concatenate([prev[...], xv], axis=0)
        o_ref[...] = _silu(_conv_pre(ext, xv, w_ref, tr))
        prev[...] = xv[tr - 8:, :]

    return pl.pallas_call(
        body,
        out_shape=_sds((t, c), F32),
        grid=(nblk, t // tr),
        in_specs=[pl.BlockSpec((tr, cw), lambda cb, ti: (ti, col0_blk + cb)),
                  pl.BlockSpec((8, cw), lambda cb, ti: (0, cb))],
        out_specs=pl.BlockSpec((tr, cw), lambda cb, ti: (ti, cb)),
        scratch_shapes=[pltpu.VMEM((8, cw), F32)],
        compiler_params=_cp(("parallel", "arbitrary")),
        name=name,
    )(x, wb)


def conv_silu_bwd(x, col0, c, wb, dout, name="conv_bwd"):
    t = x.shape[0]
    cw = _pick(c, (CONV_BLOCK, LANE))
    nblk, col0_blk = c // cw, col0 // cw
    assert col0 % cw == 0
    tr = _rt_wide(t)
    nt = t // tr
    r8 = tr // 8

    def body(x_ref, xp_ref, w_ref, do_ref, dx_ref, dwb_ref, dnext):
        ti = pl.program_id(1)
        tt = nt - 1 - ti

        @pl.when(ti == 0)
        def _():
            dnext[...] = jnp.zeros_like(dnext)
            dwb_ref[...] = jnp.zeros_like(dwb_ref)

        xv = x_ref[...]
        halo = jnp.where(tt > 0, xp_ref[...], 0.0)
        ext = jnp.concatenate([halo, xv], axis=0)
        pre = _conv_pre(ext, xv, w_ref, tr)
        s = jax.nn.sigmoid(pre)
        dpre = do_ref[...] * (s + pre * s * (1.0 - s))
        dx, sums = _conv_bwd_parts(dpre, dnext[...], xv, ext, w_ref, tr)
        dx_ref[...] = dx.astype(dx_ref.dtype)
        dwb_ref[...] += _rows_block(sums)
        dnext[...] = dpre[:8, :]

    return pl.pallas_call(
        body,
        out_shape=(_sds((t, c), BF), _sds((8, c), F32)),
        grid=(nblk, nt),
        in_specs=[pl.BlockSpec((tr, cw), lambda cb, ti: (nt - 1 - ti, col0_blk + cb)),
                  pl.BlockSpec((8, cw), lambda cb, ti: (jnp.maximum((nt - 1 - ti) * r8 - 1, 0), col0_blk + cb)),
                  pl.BlockSpec((8, cw), lambda cb, ti: (0, cb)),
                  pl.BlockSpec((tr, cw), lambda cb, ti: (nt - 1 - ti, cb))],
        out_specs=(pl.BlockSpec((tr, cw), lambda cb, ti: (nt - 1 - ti, cb)),
                   pl.BlockSpec((8, cw), lambda cb, ti: (0, cb))),
        scratch_shapes=[pltpu.VMEM((8, cw), F32)],
        compiler_params=_cp(("parallel", "arbitrary")),
        name=name,
    )(x, x, wb, dout)


def gated_norm_fwd(y, proj, g, name="gnorm_fwd"):
    t, w = y.shape
    tr = _rt_wide(t)

    def body(y_ref, z_ref, g_ref, o_ref):
        v = y_ref[...] * _silu(z_ref[...])
        r = lax.rsqrt(jnp.mean(v * v, axis=-1, keepdims=True) + EPS)
        o_ref[...] = (v * r * g_ref[...]).astype(o_ref.dtype)

    return pl.pallas_call(
        body,
        out_shape=_sds((t, w), BF),
        grid=(t // tr,),
        in_specs=[pl.BlockSpec((tr, w), lambda i: (i, 0)), pl.BlockSpec((tr, w), lambda i: (i, OFF_Z // w)),
                  pl.BlockSpec((1, w), lambda i: (0, 0))],
        out_specs=pl.BlockSpec((tr, w), lambda i: (i, 0)),
        compiler_params=_cp(("parallel",)),
        name=name,
    )(y, proj, g.reshape(1, w))


def gated_norm_bwd(y, proj, g, dyab, name="gnorm_bwd"):
    t, w = y.shape
    tr = _rt_wide(t)

    def body(y_ref, z_ref, g_ref, do_ref, dy_ref, dz_ref, dg_ref):
        i = pl.program_id(0)
        yv, zv, dov = y_ref[...], z_ref[...], do_ref[...]
        s = jax.nn.sigmoid(zv)
        sz = zv * s
        v = yv * sz
        r = lax.rsqrt(jnp.mean(v * v, axis=-1, keepdims=True) + EPS)
        vh = v * r
        dvg = dov * g_ref[...]
        dv = r * (dvg - vh * jnp.mean(dvg * vh, axis=-1, keepdims=True))
        dy_ref[...] = dv * sz
        dz_ref[...] = (dv * yv * (s + sz * (1.0 - s))).astype(dz_ref.dtype)

        @pl.when(i == 0)
        def _():
            dg_ref[...] = jnp.zeros_like(dg_ref)

        dg_ref[...] += jnp.sum(dov * vh, axis=0, keepdims=True)

    dy, dz, dg = pl.pallas_call(
        body,
        out_shape=(_sds((t, w), F32), _sds((t, w), BF), _sds((1, w), F32)),
        grid=(t // tr,),
        in_specs=[pl.BlockSpec((tr, w), lambda i: (i, 0)), pl.BlockSpec((tr, w), lambda i: (i, OFF_Z // w)),
                  pl.BlockSpec((1, w), lambda i: (0, 0)), pl.BlockSpec((tr, w), lambda i: (i, 0))],
        out_specs=(pl.BlockSpec((tr, w), lambda i: (i, 0)), pl.BlockSpec((tr, w), lambda i: (i, 0)),
                   pl.BlockSpec((1, w), lambda i: (0, 0))),
        compiler_params=_cp(("arbitrary",)),
        name=name,
    )(y, proj, g.reshape(1, w), dyab)
    return dy, dz, dg.reshape(w)


def rope_tables(t):
    inv = ROPE_BASE ** (-jnp.arange(0, MLA_ROPE, 2, dtype=F32) / MLA_ROPE)
    pos = (jnp.arange(t, dtype=F32) - PAD)[:, None]
    ang = pos * inv[None, :]
    cos, sin = jnp.cos(ang), jnp.sin(ang)
    z16 = jnp.zeros((t, 16), F32)
    z32 = jnp.zeros((t, 32), F32)
    c = jnp.concatenate([jnp.ones((t, 64), F32), cos, cos, z32], axis=1)
    s1 = jnp.concatenate([jnp.zeros((t, 64), F32), z16, sin, z32], axis=1)
    s2 = jnp.concatenate([jnp.zeros((t, 64), F32), -sin, z16, z32], axis=1)
    return c, s1, s2


def _rope(x, c, s1, s2):
    return x * c + pltpu.roll(x, 16, 1) * s1 + pltpu.roll(x, LANE - 16, 1) * s2


def _rope_t(d, c, s1, s2):
    return d * c + pltpu.roll(d * s1, LANE - 16, 1) + pltpu.roll(d * s2, 16, 1)


def rope_fwd(q_raw, kv_raw, proj, tabs):
    t = q_raw.shape[0]
    tr = _rt_wide(t)
    hw = MLA_HEADS * LANE

    def body(q_ref, k_ref, v_ref, kr_ref, c_ref, s1_ref, s2_ref, qo_ref, ko_ref, vo_ref):
        c, s1, s2 = c_ref[...], s1_ref[...], s2_ref[...]
        kr = _rope(kr_ref[...], c, s1, s2)
        for h in range(MLA_HEADS):
            sl = slice(h * LANE, (h + 1) * LANE)
            qo_ref[:, sl] = (_rope(q_ref[:, sl], c, s1, s2) * Q_PRESCALE).astype(BF)
            ko_ref[:, sl] = (k_ref[:, sl] + kr).astype(BF)
        vo_ref[...] = v_ref[...].astype(BF)

    tab_spec = pl.BlockSpec((tr, LANE), lambda i: (i, 0))
    return pl.pallas_call(
        body,
        out_shape=(_sds((t, hw), BF), _sds((t, hw), BF), _sds((t, 1024), BF)),
        grid=(t // tr,),
        in_specs=[pl.BlockSpec((tr, hw), lambda i: (i, 0)), pl.BlockSpec((tr, hw), lambda i: (i, 0)),
                  pl.BlockSpec((tr, 1024), lambda i: (i, 2)), pl.BlockSpec((tr, LANE), lambda i: (i, OFF_KR // LANE)),
                  tab_spec, tab_spec, tab_spec],
        out_specs=(pl.BlockSpec((tr, hw), lambda i: (i, 0)), pl.BlockSpec((tr, hw), lambda i: (i, 0)),
                   pl.BlockSpec((tr, 1024), lambda i: (i, 0))),
        compiler_params=_cp(("parallel",)),
        name="rope_fwd",
    )(q_raw, kv_raw, kv_raw, proj, *tabs)


def rope_bwd(dq_cat, dk_cat, tabs):
    t = dq_cat.shape[0]
    tr = _rt_wide(t)
    hw = MLA_HEADS * LANE

    def body(dq_ref, dk_ref, c_ref, s1_ref, s2_ref, dqo_ref, dkr_ref):
        c, s1, s2 = c_ref[...], s1_ref[...], s2_ref[...]
        acc = jnp.zeros((tr, LANE), F32)
        for h in range(MLA_HEADS):
            sl = slice(h * LANE, (h + 1) * LANE)
            dqo_ref[:, sl] = _rope_t(dq_ref[:, sl] * ATT_SCALE, c, s1, s2).astype(BF)
            acc = acc + dk_ref[:, sl]
        lane = lax.broadcasted_iota(jnp.int32, (tr, LANE), 1)
        dkr_ref[...] = jnp.where((lane >= 64) & (lane < 96), _rope_t(acc, c, s1, s2), 0.0)

    tab_spec = pl.BlockSpec((tr, LANE), lambda i: (i, 0))
    return pl.pallas_call(
        body,
        out_shape=(_sds((t, hw), BF), _sds((t, LANE), F32)),
        grid=(t // tr,),
        in_specs=[pl.BlockSpec((tr, hw), lambda i: (i, 0)), pl.BlockSpec((tr, hw), lambda i: (i, 0)),
                  tab_spec, tab_spec, tab_spec],
        out_specs=(pl.BlockSpec((tr, hw), lambda i: (i, 0)), pl.BlockSpec((tr, LANE), lambda i: (i, 0))),
        compiler_params=_cp(("parallel",)),
        name="rope_bwd",
    )(dq_cat, dk_cat, *tabs)


ATT_SCALE = (MLA_NOPE + MLA_ROPE) ** -0.5
LOG2E = math.log2(math.e)
Q_PRESCALE = ATT_SCALE * LOG2E
CARRY_MIDDLE_PAIR = 6
NT_DIMS = (((1,), (1,)), ((), ()))
TN_DIMS = (((0,), (0,)), ((), ()))


def _att_mask(qi, ki, tq, tk):
    qpos = qi * tq + lax.broadcasted_iota(jnp.int32, (tq, tk), 0)
    kpos = ki * tk + lax.broadcasted_iota(jnp.int32, (tq, tk), 1)
    return (kpos <= qpos) & (kpos >= PAD)


def _half_masks(n):
    lane = lax.broadcasted_iota(jnp.int32, (n, LANE), 1)
    return lane < 64, lane >= 64


def _att_tile(t):
    return _pick(t, (384, 256, 128))


def _ds(i, n):
    return pl.ds(i * n, n) if isinstance(i, int) else pl.ds(pl.multiple_of(i * n, n), n)


FWD_PAIRS = 2


def attn_fwd(q_cat, k_cat, v, carry=None):
    t = q_cat.shape[0]
    tq = tk = _att_tile(t)
    nq = t // tq
    npair, nh = FWD_PAIRS, 2 * FWD_PAIRS
    n_grp = MLA_HEADS // nh
    nx = carry.k if carry else 0

    def body(*refs):
        q_ref, k_ref, v_ref = refs[:3]
        o_ref, lse_ref = refs[3 + nx:5 + nx]
        qi = pl.program_id(1)
        if carry:
            start, middle, finish = carry.phases(refs[3:3 + nx], refs[5 + nx:5 + 2 * nx], refs[5 + 2 * nx:])
            grp = pl.program_id(0)
            pl.when((grp == 0) & (qi == 0))(start)
            pl.when((grp == CARRY_MIDDLE_PAIR // npair) & (qi == 0))(middle)
        lo_q, _ = _half_masks(tq)
        halves = _half_masks(tk)

        def step(ki, state, masked):
            m_old, l_old, accs = state[0:nh], state[nh:2 * nh], state[2 * nh:]
            rows = _ds(ki, tk)
            ss = [lax.dot_general(q_ref[:, h * LANE:(h + 1) * LANE], k_ref[rows, h * LANE:(h + 1) * LANE], NT_DIMS,
                                  preferred_element_type=F32) for h in range(nh)]
            if masked:
                valid = _att_mask(qi, ki, tq, tk)
                ss = [jnp.where(valid, s, NEG) for s in ss]
            m_new = [jnp.maximum(m_old[h], jnp.max(ss[h], axis=-1, keepdims=True)) for h in range(nh)]
            ps = [jnp.exp2(ss[h] - m_new[h]) for h in range(nh)]
            alpha = [jnp.exp2(m_old[h] - m_new[h]) for h in range(nh)]
            l_new = [alpha[h] * l_old[h] + jnp.sum(ps[h], axis=-1, keepdims=True) for h in range(nh)]
            new_accs = []
            for pp in range(npair):
                vv = v_ref[rows, pp * LANE:(pp + 1) * LANE]
                pv = [jnp.dot(ps[2 * pp + hh].astype(BF), jnp.where(halves[hh], vv, jnp.zeros_like(vv)),
                              preferred_element_type=F32) for hh in range(2)]
                new_accs.append(accs[pp] * jnp.where(lo_q, alpha[2 * pp], alpha[2 * pp + 1]) + pv[0] + pv[1])
            return tuple(m_new) + tuple(l_new) + tuple(new_accs)

        neg, zero = jnp.full((tq, 1), NEG, F32), jnp.zeros((tq, 1), F32)
        state = step(0, (neg,) * nh + (zero,) * nh + (jnp.zeros((tq, LANE), F32),) * npair, True)
        state = lax.fori_loop(1, qi, lambda ki, st: step(ki, st, False), state)
        state = lax.cond(qi > 0, lambda st: step(qi, st, True), lambda st: st, state)
        for pp in range(npair):
            l = jnp.where(lo_q, state[nh + 2 * pp], state[nh + 2 * pp + 1])
            o_ref[:, pp * LANE:(pp + 1) * LANE] = (state[2 * nh + pp] / l).astype(o_ref.dtype)
            lse_ref[:, pp * LANE:(pp + 1) * LANE] = jnp.where(lo_q, state[2 * pp], state[2 * pp + 1]) + jnp.log2(l)
        if carry:
            pl.when((grp == n_grp - 1) & (qi == nq - 1))(finish)

    outs = pl.pallas_call(
        body,
        out_shape=(_sds((t, 1024), BF), _sds((t, 1024), F32)) + tuple(carry.out_shapes() if carry else ()),
        grid=(n_grp, nq),
        in_specs=[pl.BlockSpec((tq, nh * LANE), lambda g, qi: (qi, g)),
                  pl.BlockSpec((t, nh * LANE), lambda g, qi: (0, g)),
                  pl.BlockSpec((t, npair * LANE), lambda g, qi: (0, g))] + [ANY] * nx,
        out_specs=(pl.BlockSpec((tq, npair * LANE), lambda g, qi: (qi, g)),
                   pl.BlockSpec((tq, npair * LANE), lambda g, qi: (qi, g))) + (ANY,) * nx,
        scratch_shapes=carry.scratch() if carry else [],
        compiler_params=_cp(("arbitrary", "arbitrary") if carry else ("parallel", "parallel")),
        name="attn_fwd_carrying" if carry else "attn_fwd",
    )(q_cat, k_cat, v, *(carry.arrs if carry else ()))
    return outs[0], outs[1], list(outs[2:])


def attn_bwd(q_cat, k_cat, v, o, lse, dyab, carry=None):
    t = q_cat.shape[0]
    tq = tk = _att_tile(t)
    nq = t // tq
    n_pair = MLA_HEADS // 2
    nx = carry.k if carry else 0

    def body(*refs):
        q_ref, k_ref, v_ref, o_ref, lse_ref, do_ref = refs[:6]
        dq_ref, dk_ref, dv_ref = refs[6 + nx:9 + nx]
        ki = pl.program_id(1)
        if carry:
            start, middle, finish = carry.phases(refs[6:6 + nx], refs[9 + nx:9 + 2 * nx], refs[9 + 2 * nx:])
            pair = pl.program_id(0)
            pl.when((pair == 0) & (ki == 0))(start)
            pl.when((pair == CARRY_MIDDLE_PAIR) & (ki == 0))(middle)

        @pl.when(ki == 0)
        def _():
            dq_ref[...] = jnp.zeros_like(dq_ref)

        halves = _half_masks(tq)
        vv = v_ref[...]
        kk = [k_ref[:, hh * LANE:(hh + 1) * LANE] for hh in range(2)]

        def step(qi, acc, masked):
            rows = _ds(qi, tq)
            dov, ov, lse_v = do_ref[rows, :], o_ref[rows, :].astype(F32), lse_ref[rows, :]
            qh = [q_ref[rows, hh * LANE:(hh + 1) * LANE] for hh in range(2)]
            ss = [lax.dot_general(qh[hh], kk[hh], NT_DIMS, preferred_element_type=F32) for hh in range(2)]
            if masked:
                valid = _att_mask(qi, ki, tq, tk)
                ss = [jnp.where(valid, s, NEG) for s in ss]
            ps = [jnp.exp2(ss[hh] - lse_v[:, 64 * hh:64 * hh + 1]) for hh in range(2)]
            dom = [jnp.where(halves[hh], dov, 0.0) for hh in range(2)]
            delta = [jnp.sum(dom[hh] * ov, axis=-1, keepdims=True) for hh in range(2)]
            dom = [d.astype(BF) for d in dom]
            dp = [lax.dot_general(dom[hh], vv, NT_DIMS, preferred_element_type=F32) for hh in range(2)]
            ds = [(ps[hh] * (dp[hh] - delta[hh])).astype(BF) for hh in range(2)]
            pb = [p.astype(BF) for p in ps]
            dv = (acc[2] + lax.dot_general(pb[0], dom[0], TN_DIMS, preferred_element_type=F32)
                  + lax.dot_general(pb[1], dom[1], TN_DIMS, preferred_element_type=F32))
            dk = [acc[hh] + lax.dot_general(ds[hh], qh[hh], TN_DIMS, preferred_element_type=F32) for hh in range(2)]
            for hh in range(2):
                dq_ref[rows, hh * LANE:(hh + 1) * LANE] += jnp.dot(ds[hh], kk[hh], preferred_element_type=F32)
            return dk[0], dk[1], dv

        zero = jnp.zeros((tk, LANE), F32)
        acc = step(ki, (zero, zero, zero), True)
        acc = lax.fori_loop(ki + 1, jnp.where(ki == 0, nq, ki + 1), lambda qi, a: step(qi, a, True), acc)
        acc = lax.fori_loop(ki + 1, jnp.where(ki == 0, ki + 1, nq), lambda qi, a: step(qi, a, False), acc)
        dk_ref[:, 0:LANE] = acc[0] * (1.0 / LOG2E)
        dk_ref[:, LANE:2 * LANE] = acc[1] * (1.0 / LOG2E)
        dv_ref[...] = acc[2].astype(dv_ref.dtype)
        if carry:
            pl.when((pair == n_pair - 1) & (ki == nq - 1))(finish)

    full = lambda w, off=0: pl.BlockSpec((t, w), lambda p, ki: (0, p + off))
    blk = lambda w: pl.BlockSpec((tk, w), lambda p, ki: (ki, p))
    outs = pl.pallas_call(
        body,
        out_shape=(_sds((t, 2048), F32), _sds((t, 2048), F32), _sds((t, 1024), BF))
        + tuple(carry.out_shapes() if carry else ()),
        grid=(n_pair, nq),
        in_specs=[full(2 * LANE), blk(2 * LANE), blk(LANE), full(LANE), full(LANE), full(LANE, 8)] + [ANY] * nx,
        out_specs=(full(2 * LANE), blk(2 * LANE), blk(LANE)) + (ANY,) * nx,
        scratch_shapes=carry.scratch() if carry else [],
        compiler_params=_cp(("arbitrary", "arbitrary") if carry else ("parallel", "arbitrary")),
        name="attn_bwd_carrying" if carry else "attn_bwd",
    )(q_cat, k_cat, v, o, lse, dyab, *(carry.arrs if carry else ()))
    return outs[0], outs[1], outs[2], list(outs[3:])


N_PAIR = SSD_HEADS // 2


def _hdot(a, b):
    return jnp.dot(a, b, precision=HI, preferred_element_type=F32)


def _ssd_chunk(xs, bg, cg, dtraw, hin, dt_bias, a_log, dskip, rowmask):
    ln = CHUNK
    causal = lax.broadcasted_iota(jnp.int32, (ln, ln), 0) >= lax.broadcasted_iota(jnp.int32, (ln, ln), 1)
    ltri = causal.astype(F32)
    lane = lax.broadcasted_iota(jnp.int32, (ln, LANE), 1)
    halves = (lane < 64, lane >= 64)
    low_row = lax.broadcasted_iota(jnp.int32, (1, LANE), 1) < 64
    head_lane = lax.broadcasted_iota(jnp.int32, (1, SSD_HEADS), 1)
    head_row = lax.broadcasted_iota(jnp.int32, (SSD_HEADS, 1), 0)

    def col(a, h):
        return jnp.sum(jnp.where(head_lane == h, a, 0.0), axis=1, keepdims=True)

    dt = _softplus(dtraw + dt_bias) * rowmask
    da = dt * (-jnp.exp(a_log))
    acs = _hdot(ltri, da)
    acs_t = lax.dot_general(da, ltri, (((0,), (1,)), ((), ())), precision=HI, preferred_element_type=F32)
    tot = jnp.sum(da, axis=0, keepdims=True)
    bm = [b * rowmask for b in bg]
    cm = [c * rowmask for c in cg]
    cb = [lax.dot_general(cm[g].astype(BF), bm[g].astype(BF), NT_DIMS, preferred_element_type=F32) for g in range(2)]
    ys, hout = [], []
    for p in range(N_PAIR):
        g = p // (N_PAIR // 2)
        h0, h1 = 2 * p, 2 * p + 1
        xdt = xs[p] * jnp.where(halves[0], col(dt, h0), col(dt, h1))
        a_cols = [col(acs, h0), col(acs, h1)]
        tot_cols = [col(tot, h0), col(tot, h1)]
        y = jnp.zeros((ln, LANE), F32)
        snew = jnp.zeros((ln, LANE), F32)
        for hh in range(2):
            a_row = jnp.sum(jnp.where(head_row == h0 + hh, acs_t, 0.0), axis=0, keepdims=True)
            dec = jnp.exp(jnp.where(causal, a_cols[hh] - a_row, NEG))
            xm = jnp.where(halves[hh], xdt, 0.0).astype(BF)
            y = y + jnp.dot((cb[g] * dec).astype(BF), xm, preferred_element_type=F32)
            bd = bm[g] * jnp.exp(tot_cols[hh] - a_cols[hh])
            snew = snew + lax.dot_general(bd.astype(BF), xm, TN_DIMS, preferred_element_type=F32)
        y_off = (jnp.dot(cm[g].astype(BF), hin[p].astype(BF), preferred_element_type=F32)
                 * jnp.where(halves[0], jnp.exp(a_cols[0]), jnp.exp(a_cols[1])))
        ys.append(y + y_off + jnp.where(low_row, col(dskip, h0), col(dskip, h1)) * xs[p])
        hout.append(jnp.where(low_row, jnp.exp(tot_cols[0]), jnp.exp(tot_cols[1])) * hin[p] + snew)
    return ys, hout


def _ssd_load(x_ref, dt_ref):
    xs = [x_ref[:, p * LANE:(p + 1) * LANE] for p in range(N_PAIR)]
    bg = [x_ref[:, SSD_D_INNER + g * LANE:SSD_D_INNER + (g + 1) * LANE] for g in range(2)]
    cg = [x_ref[:, SSD_D_INNER + (2 + g) * LANE:SSD_D_INNER + (3 + g) * LANE] for g in range(2)]
    return xs, bg, cg, dt_ref[:, 0:SSD_HEADS]


def _chunk_rowmask(c):
    return ((c * CHUNK + lax.broadcasted_iota(jnp.int32, (CHUNK, 1), 0)) >= PAD).astype(F32)


def ssd_fwd(xbc_c, proj, dt_bias, a_log, dskip):
    t = xbc_c.shape[0]
    nc = t // CHUNK

    def body(x_ref, dt_ref, dtb_ref, al_ref, d_ref, y_ref, hs_ref, h_s):
        c = pl.program_id(0)

        @pl.when(c == 0)
        def _():
            h_s[...] = jnp.zeros_like(h_s)

        xs, bg, cg, dtraw = _ssd_load(x_ref, dt_ref)
        hin = [h_s[p] for p in range(N_PAIR)]
        hs_ref[0] = h_s[...]
        ys, hout = _ssd_chunk(xs, bg, cg, dtraw, hin, dtb_ref[...], al_ref[...], d_ref[...], _chunk_rowmask(c))
        for p in range(N_PAIR):
            y_ref[:, p * LANE:(p + 1) * LANE] = ys[p]
            h_s[p] = hout[p]

    par = pl.BlockSpec((1, SSD_HEADS), lambda c: (0, 0))
    return pl.pallas_call(
        body,
        out_shape=(_sds((t, SSD_D_INNER), F32), _sds((nc, N_PAIR, CHUNK, LANE), F32)),
        grid=(nc,),
        in_specs=[pl.BlockSpec((CHUNK, SSD_CONV_CH), lambda c: (c, 0)),
                  pl.BlockSpec((CHUNK, LANE), lambda c: (c, OFF_DT // LANE)), par, par, par],
        out_specs=(pl.BlockSpec((CHUNK, SSD_D_INNER), lambda c: (c, 0)),
                   pl.BlockSpec((1, N_PAIR, CHUNK, LANE), lambda c: (c, 0, 0, 0))),
        scratch_shapes=[pltpu.VMEM((N_PAIR, CHUNK, LANE), F32)],
        compiler_params=_cp(("arbitrary",)),
        name="ssd_fwd",
    )(xbc_c, proj, dt_bias.reshape(1, -1), a_log.reshape(1, -1), dskip.reshape(1, -1))


def ssd_bwd(xbc_c, proj, dt_bias, a_log, dskip, hs, dy):
    t = xbc_c.shape[0]
    nc = t // CHUNK

    def body(x_ref, dt_ref, dtb_ref, al_ref, d_ref, hs_ref, dy_ref, dx_ref, ddt_ref, dpar_ref, dh_s):
        ci = pl.program_id(0)
        c = nc - 1 - ci

        @pl.when(ci == 0)
        def _():
            dh_s[...] = jnp.zeros_like(dh_s)
            dpar_ref[...] = jnp.zeros_like(dpar_ref)

        xs, bg, cg, dtraw = _ssd_load(x_ref, dt_ref)
        hin = [hs_ref[0, p] for p in range(N_PAIR)]
        rowmask = _chunk_rowmask(c)
        fn = lambda xs_, bg_, cg_, dtraw_, hin_, dtb_, al_, d_: _ssd_chunk(xs_, bg_, cg_, dtraw_, hin_, dtb_, al_, d_, rowmask)
        _, vjp = jax.vjp(fn, xs, bg, cg, dtraw, hin, dtb_ref[...], al_ref[...], d_ref[...])
        dys = [dy_ref[:, p * LANE:(p + 1) * LANE] for p in range(N_PAIR)]
        dhs = [dh_s[p] for p in range(N_PAIR)]
        dxs, dbg, dcg, ddtraw, dhin, ddtb, dal, dd = vjp((dys, dhs))
        for p in range(N_PAIR):
            dx_ref[:, p * LANE:(p + 1) * LANE] = dxs[p]
            dh_s[p] = dhin[p]
        for g in range(2):
            dx_ref[:, SSD_D_INNER + g * LANE:SSD_D_INNER + (g + 1) * LANE] = dbg[g]
            dx_ref[:, SSD_D_INNER + (2 + g) * LANE:SSD_D_INNER + (3 + g) * LANE] = dcg[g]
        ddt_ref[...] = jnp.zeros_like(ddt_ref)
        ddt_ref[:, 0:SSD_HEADS] = ddtraw
        dpar_ref[0:1, 0:SSD_HEADS] += ddtb
        dpar_ref[1:2, 0:SSD_HEADS] += dal
        dpar_ref[2:3, 0:SSD_HEADS] += dd

    par = pl.BlockSpec((1, SSD_HEADS), lambda ci: (0, 0))
    return pl.pallas_call(
        body,
        out_shape=(_sds((t, SSD_CONV_CH), F32), _sds((t, LANE), F32), _sds((8, LANE), F32)),
        grid=(nc,),
        in_specs=[pl.BlockSpec((CHUNK, SSD_CONV_CH), lambda ci: (nc - 1 - ci, 0)),
                  pl.BlockSpec((CHUNK, LANE), lambda ci: (nc - 1 - ci, OFF_DT // LANE)), par, par, par,
                  pl.BlockSpec((1, N_PAIR, CHUNK, LANE), lambda ci: (nc - 1 - ci, 0, 0, 0)),
                  pl.BlockSpec((CHUNK, SSD_D_INNER), lambda ci: (nc - 1 - ci, 0))],
        out_specs=(pl.BlockSpec((CHUNK, SSD_CONV_CH), lambda ci: (nc - 1 - ci, 0)),
                   pl.BlockSpec((CHUNK, LANE), lambda ci: (nc - 1 - ci, 0)),
                   pl.BlockSpec((8, LANE), lambda ci: (0, 0))),
        scratch_shapes=[pltpu.VMEM((N_PAIR, CHUNK, LANE), F32)],
        compiler_params=_cp(("arbitrary",)),
        name="ssd_bwd",
    )(xbc_c, proj, dt_bias.reshape(1, -1), a_log.reshape(1, -1), dskip.reshape(1, -1), hs, dy)


def _neg_expm1(y):
    series = -(y * (1.0 + y * (0.5 + y * (1.0 / 6.0 + y * (1.0 / 24.0 + y * (1.0 / 120.0))))))
    return jnp.where(y > -0.1, series, 1.0 - jnp.exp(y))


def _rg_pw(xr, wa, ba, wi, bi, lam, rowmask):
    xb = xr.astype(BF)
    r = jax.nn.sigmoid(jnp.dot(xb, wa.astype(BF), preferred_element_type=F32) + ba)
    i = jax.nn.sigmoid(jnp.dot(xb, wi.astype(BF), preferred_element_type=F32) + bi)
    log_a = -LRU_C * r * _softplus(-lam)
    a = jnp.exp(log_a)
    u = jnp.sqrt(_neg_expm1(2.0 * log_a)) * (i * xr) * rowmask
    return a, u


def _gelu_grad(x):
    c = math.sqrt(2.0 / math.pi)
    th = jnp.tanh(c * (x + 0.044715 * (x * x * x)))
    return 0.5 * (1.0 + th) + 0.5 * x * (1.0 - th * th) * c * (1.0 + 3.0 * 0.044715 * x * x)


def _scan_fwd(a, u):
    n = a.shape[0]
    row = lax.broadcasted_iota(jnp.int32, a.shape, 0)
    s = 1
    while s < n:
        a_s = jnp.where(row >= s, pltpu.roll(a, s, 0), 1.0)
        u_s = jnp.where(row >= s, pltpu.roll(u, s, 0), 0.0)
        u = u + a * u_s
        a = a * a_s
        s *= 2
    return a, u


def _scan_bwd(b, d):
    n = b.shape[0]
    row = lax.broadcasted_iota(jnp.int32, b.shape, 0)
    s = 1
    while s < n:
        b_s = jnp.where(row < n - s, pltpu.roll(b, n - s, 0), 1.0)
        d_s = jnp.where(row < n - s, pltpu.roll(d, n - s, 0), 0.0)
        d = d + b * d_s
        b = b * b_s
        s *= 2
    return d


def rg_fwd(xr_pre, gate_pre, rgp, w_a, w_i):
    t = xr_pre.shape[0]
    tr = _rt(t)

    def body(x_ref, g_ref, p_ref, wa_ref, wi_ref, hg_ref, hs_ref, prev, hcar):
        ti = pl.program_id(1)

        @pl.when(ti == 0)
        def _():
            prev[...] = jnp.zeros_like(prev)
            hcar[...] = jnp.zeros_like(hcar)

        xv = x_ref[...]
        ext = jnp.concatenate([prev[...], xv], axis=0)
        xr = _conv_pre(ext, xv, p_ref, tr)
        rowmask = _row_mask(ti, tr, (tr, 1)).astype(F32)
        a, u = _rg_pw(xr, wa_ref[0], p_ref[5:6, :], wi_ref[0], p_ref[6:7, :], p_ref[7:8, :], rowmask)
        a_cum, h_loc = _scan_fwd(a, u)
        hs = h_loc + a_cum * hcar[0:1, :]
        hs_ref[...] = hs
        hg_ref[...] = (hs * _gelu(g_ref[...])).astype(hg_ref.dtype)
        hcar[...] = jnp.broadcast_to(hs[tr - 1:tr, :], (8, LANE))
        prev[...] = xv[tr - 8:, :]

    return pl.pallas_call(
        body,
        out_shape=(_sds((t, LRU_WIDTH), BF), _sds((t, LRU_WIDTH), F32)),
        grid=(LRU_BLOCKS, t // tr),
        in_specs=[pl.BlockSpec((tr, LANE), lambda n, ti: (ti, n)),
                  pl.BlockSpec((tr, LANE), lambda n, ti: (ti, n)),
                  pl.BlockSpec((8, LANE), lambda n, ti: (0, n)),
                  pl.BlockSpec((1, LANE, LANE), lambda n, ti: (n, 0, 0)),
                  pl.BlockSpec((1, LANE, LANE), lambda n, ti: (n, 0, 0))],
        out_specs=(pl.BlockSpec((tr, LANE), lambda n, ti: (ti, n)), pl.BlockSpec((tr, LANE), lambda n, ti: (ti, n))),
        scratch_shapes=[pltpu.VMEM((8, LANE), F32), pltpu.VMEM((8, LANE), F32)],
        compiler_params=_cp(("parallel", "arbitrary")),
        name="rg_fwd",
    )(xr_pre, gate_pre, rgp, w_a, w_i)


def rg_bwd(xr_pre, gate_pre, rgp, w_a, w_i, hs, dhg):
    t = xr_pre.shape[0]
    tr = _rt(t)
    nt = t // tr
    r8 = tr // 8

    def body(x_ref, xp_ref, g_ref, p_ref, wa_ref, wi_ref, hs_ref, hp_ref, dhg_ref,
             dx_ref, dg_ref, dp_ref, dwa_ref, dwi_ref, gcar, dnext):
        ti = pl.program_id(1)
        tt = nt - 1 - ti

        @pl.when(ti == 0)
        def _():
            gcar[...] = jnp.zeros_like(gcar)
            dnext[...] = jnp.zeros_like(dnext)
            dp_ref[...] = jnp.zeros_like(dp_ref)
            dwa_ref[...] = jnp.zeros_like(dwa_ref)
            dwi_ref[...] = jnp.zeros_like(dwi_ref)

        xv = x_ref[...]
        halo = jnp.where(tt > 0, xp_ref[...], 0.0)
        ext = jnp.concatenate([halo, xv], axis=0)
        xr = _conv_pre(ext, xv, p_ref, tr)
        rowmask = _row_mask(tt, tr, (tr, 1)).astype(F32)
        fn = lambda xr_, wa_, ba_, wi_, bi_, lam_: _rg_pw(xr_, wa_, ba_, wi_, bi_, lam_, rowmask)
        (a, _), vjp = jax.vjp(fn, xr, wa_ref[0], p_ref[5:6, :], wi_ref[0], p_ref[6:7, :], p_ref[7:8, :])
        gpre = g_ref[...]
        hsv = hs_ref[...]
        dhg_v = dhg_ref[...]
        dg_ref[...] = (dhg_v * hsv * _gelu_grad(gpre)).astype(dg_ref.dtype)
        row = lax.broadcasted_iota(jnp.int32, (tr, LANE), 0)
        d = dhg_v * _gelu(gpre) + jnp.where(row == tr - 1, gcar[0:1, :], 0.0)
        b = jnp.where(row < tr - 1, pltpu.roll(a, tr - 1, 0), 0.0)
        g = _scan_bwd(b, d)
        gcar[...] = jnp.broadcast_to(a[0:1, :] * g[0:1, :], (8, LANE))
        hlast = jnp.where(tt > 0, hp_ref[7:8, :], 0.0)
        hprev = jnp.where(row == 0, hlast, pltpu.roll(hsv, 1, 0))
        dxr, dwa, dba, dwi, dbi, dlam = vjp((g * hprev, g))
        dx, sums = _conv_bwd_parts(dxr, dnext[...], xv, ext, p_ref, tr)
        dx_ref[...] = dx.astype(dx_ref.dtype)
        dnext[...] = dxr[:8, :]
        dp_ref[...] += _rows_block(sums + [dba, dbi, dlam])
        dwa_ref[0] += dwa
        dwi_ref[0] += dwi

    tile = lambda off=0: pl.BlockSpec((tr, LANE), lambda n, ti: (nt - 1 - ti, off + n))
    halo = lambda off=0: pl.BlockSpec((8, LANE), lambda n, ti: (jnp.maximum((nt - 1 - ti) * r8 - 1, 0), off + n))
    par = pl.BlockSpec((8, LANE), lambda n, ti: (0, n))
    wspec = pl.BlockSpec((1, LANE, LANE), lambda n, ti: (n, 0, 0))
    return pl.pallas_call(
        body,
        out_shape=(_sds((t, LRU_WIDTH), BF), _sds((t, LRU_WIDTH), BF), _sds((8, LRU_WIDTH), F32),
                   _sds((LRU_BLOCKS, LANE, LANE), F32), _sds((LRU_BLOCKS, LANE, LANE), F32)),
        grid=(LRU_BLOCKS, nt),
        in_specs=[tile(), halo(), tile(), par, wspec, wspec, tile(), halo(), tile()],
        out_specs=(tile(), tile(), par, wspec, wspec),
        scratch_shapes=[pltpu.VMEM((8, LANE), F32), pltpu.VMEM((8, LANE), F32)],
        compiler_params=_cp(("parallel", "arbitrary")),
        name="rg_bwd",
    )(xr_pre, xr_pre, gate_pre, rgp, w_a, w_i, hs, hs, dhg)


PACK_W = 1024
MESH_ID = pl.DeviceIdType.MESH
ANY = pl.BlockSpec(memory_space=pl.ANY)


def _my_place():
    x, y, c = lax.axis_index("x"), lax.axis_index("y"), lax.axis_index("c")
    return x, y, c


def _lin(px, py, pc):
    return 4 * px + 2 * py + pc


class Exchange:
    def __init__(self, kind, arrs):
        self.kind, self.arrs, self.k = kind, list(arrs), len(arrs)

    def out_shapes(self):
        if self.kind == "gather":
            return [_sds((N_DEV,) + a.shape, a.dtype) for a in self.arrs]
        return [_sds(a.shape, a.dtype) for a in self.arrs]

    def scratch(self):
        k = self.k
        return [pltpu.SemaphoreType.DMA((k, 7)), pltpu.SemaphoreType.DMA((k, 7)), pltpu.SemaphoreType.DMA((k,))]

    def phases(self, ins, outs, sems):
        return (self._gather if self.kind == "gather" else self._scatter)(ins, outs, *sems)

    def _gather(self, ins, outs, send_sems, recv_sems, local_sems):
        k = self.k
        x, y, c = _my_place()
        me, sibling = (x, y, c), (x, y, 1 - c)
        chips = [(1 - x, y), (x, 1 - y), (1 - x, 1 - y)]

        def copy(a, sem, block, to, from_input=False):
            slab = outs[a].at[_lin(*block)]
            return pltpu.make_async_remote_copy(
                src_ref=ins[a] if from_input else slab, dst_ref=slab,
                send_sem=send_sems.at[a, sem], recv_sem=recv_sems.at[a, sem],
                device_id=to, device_id_type=MESH_ID)

        def mine():
            return [pltpu.make_async_copy(ins[a], outs[a].at[_lin(*me)], local_sems.at[a]) for a in range(k)]

        def first():
            out = []
            for a in range(k):
                out.append(copy(a, 0, me, sibling, True))
                out += [copy(a, 1 + j, me, (*chip, c), True) for j, chip in enumerate(chips)]
            return out

        def passed():
            return [copy(a, 4 + j, (*chip, c), sibling) for j, chip in enumerate(chips) for a in range(k)]

        def start():
            for cp in mine() + first():
                cp.start()

        def middle():
            onward = passed()
            for j, chip in enumerate(chips):
                for a in range(k):
                    copy(a, 1 + j, (*chip, c), me).wait_recv()
                    onward[j * k + a].start()

        def finish():
            for a in range(k):
                copy(a, 0, sibling, me).wait_recv()
                for j, chip in enumerate(chips):
                    copy(a, 4 + j, (*chip, 1 - c), me).wait_recv()
            for cp in first() + passed():
                cp.wait_send()
            for cp in mine():
                cp.wait()

        return start, middle, finish

    def _scatter(self, ins, outs, send_sems, recv_sems, local_sems):
        k = self.k
        x, y, c = _my_place()
        me = _lin(x, y, c)
        peers = [((1 - x) if r & 4 else x, (1 - y) if r & 2 else y, (1 - c) if r & 1 else c) for r in range(1, N_DEV)]

        def copy(a, r, src_slab, dst_slab, to):
            return pltpu.make_async_remote_copy(
                src_ref=ins[a].at[src_slab], dst_ref=outs[a].at[dst_slab],
                send_sem=send_sems.at[a, r], recv_sem=recv_sems.at[a, r],
                device_id=to, device_id_type=MESH_ID)

        def mine():
            return [pltpu.make_async_copy(ins[a].at[me], outs[a].at[me], local_sems.at[a]) for a in range(k)]

        def sends():
            return [copy(a, r, _lin(*peer), me, peer) for r, peer in enumerate(peers) for a in range(k)]

        def start():
            for cp in mine() + sends():
                cp.start()

        def middle():
            pass

        def finish():
            for r, peer in enumerate(peers):
                for a in range(k):
                    copy(a, r, me, _lin(*peer), peer).wait_recv()
            for cp in sends():
                cp.wait_send()
            for cp in mine():
                cp.wait()

        return start, middle, finish

    def run(self, name):
        k = self.k

        def body(*refs):
            start, middle, finish = self.phases(refs[:k], refs[k:2 * k], refs[2 * k:])
            start()
            middle()
            finish()

        return pl.pallas_call(
            body,
            out_shape=tuple(self.out_shapes()),
            in_specs=[ANY] * k,
            out_specs=tuple(ANY for _ in range(k)),
            scratch_shapes=self.scratch(),
            name=name,
        )(*self.arrs)


def all_gather(arrs, name):
    return Exchange("gather", arrs).run(name)


def all_to_all(arrs, name):
    return Exchange("scatter", arrs).run(name)


def slab_sum(a, name):
    _, r, w = a.shape
    tr = _pick(r, (256, 128, 64, 32, 16, 8))

    def body(a_ref, o_ref):
        acc = a_ref[0].astype(F32)
        for d in range(1, N_DEV):
            acc = acc + a_ref[d].astype(F32)
        o_ref[...] = acc

    return pl.pallas_call(
        body,
        out_shape=_sds((r, w), F32),
        grid=(r // tr,),
        in_specs=[pl.BlockSpec((N_DEV, tr, w), lambda i: (0, i, 0))],
        out_specs=pl.BlockSpec((tr, w), lambda i: (i, 0)),
        compiler_params=_cp(("parallel",)),
        name=name,
    )(a)


def _adam_update(w, g, m, v):
    nm = ADAM_B1 * m + (1.0 - ADAM_B1) * g
    nv = ADAM_B2 * v + (1.0 - ADAM_B2) * (g * g)
    m_hat = nm / (1.0 - ADAM_B1 ** ADAM_STEP)
    v_hat = nv / (1.0 - ADAM_B2 ** ADAM_STEP)
    return -ADAM_LR * (m_hat / (jnp.sqrt(v_hat) + ADAM_EPS) + ADAM_WD * w), nm, nv


def adamw_blocks(w, m, v, parts, name):
    nl, r, c = w.shape
    tr = next(t for t in (256, 160, 128, 64, 32, 16) if r % t == 0 and N_DEV * t * c * 2 <= 2 * 1024 * 1024)

    def body(w_ref, m_ref, v_ref, *rest):
        part_refs, (g_ref, d_ref, nm_ref, nv_ref) = rest[:nl], rest[nl:]
        layer = pl.program_id(0)
        for idx in range(nl):
            @pl.when(layer == idx)
            def _(idx=idx):
                g = part_refs[idx][0].astype(F32)
                for dev in range(1, N_DEV):
                    g = g + part_refs[idx][dev].astype(F32)
                g_ref[...] = g
                d_ref[...], nm_ref[...], nv_ref[...] = _adam_update(w_ref[...], g, m_ref[...], v_ref[...])

    spec = pl.BlockSpec((None, tr, c), lambda l, i: (l, i, 0))
    part_spec = lambda idx: pl.BlockSpec((N_DEV, tr, c), lambda l, i: (0, jnp.where(l == idx, i, 0), 0))
    return pl.pallas_call(
        body,
        out_shape=tuple(_sds((nl, r, c), F32) for _ in range(4)),
        grid=(nl, r // tr),
        in_specs=[spec] * 3 + [part_spec(idx) for idx in range(nl)],
        out_specs=(spec,) * 4,
        compiler_params=_cp(("arbitrary", "arbitrary")),
        name=name,
    )(w, m, v, *parts)


def adamw(w, g, m, v, name):
    r, c = w.shape
    tr = _pick(r, (256, 160, 128, 64, 32, 16, 8))

    def body(w_ref, g_ref, m_ref, v_ref, d_ref, nm_ref, nv_ref):
        d_ref[...], nm_ref[...], nv_ref[...] = _adam_update(w_ref[...], g_ref[...], m_ref[...], v_ref[...])

    spec = pl.BlockSpec((tr, c), lambda i: (i, 0))
    return pl.pallas_call(
        body,
        out_shape=tuple(_sds((r, c), F32) for _ in range(3)),
        grid=(r // tr,),
        in_specs=[spec] * 4,
        out_specs=(spec, spec, spec),
        compiler_params=_cp(("parallel",)),
        name=name,
    )(w, g, m, v)


def _relu2_epi(acc):
    r = jnp.maximum(acc, 0.0)
    return r * r, r


def _drelu2_epi(acc, r):
    return (acc * (2.0 * r.astype(F32)),)


def mlp_fwd(h, g_pre, g_post, w_up, w_down, hn=None, g_next=None):
    if hn is None:
        hn = norm_fwd(h, g_pre, BF, name="mlp_norm")
    u, r = matmul(hn, w_up, "nn", (BF, BF), epi=_relu2_epi, name="mlp_up")
    d = matmul(u, w_down, "nn", name="mlp_down")
    h2, hn_next = resadd_fwd(h, d, g_post, g_next, name="mlp_res")
    return h2, (h, hn, u, r, d), hn_next


def mlp_bwd(res, dh2, g_pre, g_post, w_up, w_down, post=None, then=None):
    h, hn, u, r, d = res
    dd, dg_post = post if post is not None else norm_bwd(d, g_post, dh2, mask_pad=True, out_dtype=BF, name="mlp_post_bwd")
    dw_down = matmul(u, dd, "tn", (BF,), name="mlp_dwdown").reshape(w_down.g8.shape)
    dp = matmul(dd, w_down, "nt", (BF,), epi=_drelu2_epi, extras=(r,), name="mlp_du")
    dw_up = matmul(hn, dp, "tn", (BF,), out_blocks=True, name="mlp_dwup")
    dhn = matmul(dp, w_up, "nt", name="mlp_dhn")
    dh, dg_pre, *below = norm_bwd(h, g_pre, dhn, dres=dh2, then=then, name="mlp_pre_bwd")
    return dh, dict(mlp_pre_g=dg_pre, mlp_post_g=dg_post, w_up=dw_up, w_down=dw_down), (tuple(below) or None)


def rg_layer_fwd(h, g_pre, g_post, w_x, w_y, rgp, w_a, w_i, w_out, hn=None, g_next=None):
    if hn is None:
        hn = norm_fwd(h, g_pre, BF, name="rg_norm")
    xr = matmul(hn, w_x, "nn", name="rg_in_x")
    gp = matmul(hn, w_y, "nn", name="rg_in_y")
    hg, hs = rg_fwd(xr, gp, rgp, w_a, w_i)
    m = matmul(hg, w_out, "nn", name="rg_out")
    h2, hn_next = resadd_fwd(h, m, g_post, g_next, name="rg_res")
    return h2, (h, hn, xr, gp, hg, hs, m), hn_next


def rg_layer_bwd(res, dh2, g_pre, g_post, w_x, w_y, rgp, w_a, w_i, w_out, post=None, then=None):
    h, hn, xr, gp, hg, hs, m = res
    dm, dg_post = post if post is not None else norm_bwd(m, g_post, dh2, mask_pad=True, out_dtype=BF, name="rg_post_bwd")
    dw_out = matmul(hg, dm, "tn", name="rg_dwout")
    dhg = matmul(dm, w_out, "nt", name="rg_dhg")
    dxr, dgp, drgp, dwa, dwi = rg_bwd(xr, gp, rgp, w_a, w_i, hs, dhg)
    dw_x = matmul(hn, dxr, "tn", name="rg_dwx")
    dw_y = matmul(hn, dgp, "tn", name="rg_dwy")
    dhn = matmul([dxr, dgp], [w_x, w_y], "nt", name="rg_dhn")
    dh, dg_pre, *below = norm_bwd(h, g_pre, dhn, dres=dh2, then=then, name="rg_pre_bwd")
    grads = dict(mix_pre_g=dg_pre, mix_post_g=dg_post, rg_w_x=dw_x, rg_w_y=dw_y,
                 rg_conv_w=drgp[0:4], rg_conv_b=drgp[4], rg_b_a=drgp[5], rg_b_i=drgp[6], rg_lambda=drgp[7],
                 rg_w_a=dwa, rg_w_i=dwi, rg_w_out=dw_out)
    return dh, grads, (tuple(below) or None)


def sm_layer_fwd(h, g_pre, g_post, w_in_p, convp, dt_bias, a_log, dskip, ssd_g, q_g, w_q_p, kv_g, w_kv_p, w_out, tabs,
                 carry=None, on_carried=None, hn=None, g_next=None):
    if hn is None:
        hn = norm_fwd(h, g_pre, BF, name="sm_norm")
    proj = matmul(hn, w_in_p, "nn", name="sm_in")
    xbc_c = conv_silu_fwd(proj, OFF_XBC, SSD_CONV_CH, convp, name="ssd_conv")
    y, hst = ssd_fwd(xbc_c, proj, dt_bias, a_log, dskip)
    y_ssd = gated_norm_fwd(y, proj, ssd_g)
    cqn = norm_fwd(proj, q_g, BF, col_blk=OFF_CQ // MLA_Q_RANK, width=MLA_Q_RANK, name="q_norm")
    q_raw = matmul(cqn, w_q_p, "nn", name="q_up")
    ckvn = norm_fwd(proj, kv_g, BF, col_blk=OFF_CKV // MLA_KV_RANK, width=MLA_KV_RANK, name="kv_norm")
    kv_raw = matmul(ckvn, w_kv_p, "nn", name="kv_up")
    q_cat, k_cat, v = rope_fwd(q_raw, kv_raw, proj, tabs)
    o, lse, carried = attn_fwd(q_cat, k_cat, v, carry)
    if on_carried is not None:
        on_carried(carried)
    w_out = w_out()
    half = w_out.shape[0] // 2
    m = matmul([y_ssd, o], [KBlock(w_out, half, 0), KBlock(w_out, half, 1)], "nn", name="sm_out")
    res = (h, hn, proj, xbc_c, y, hst, cqn, ckvn, q_cat, k_cat, v, o, lse, y_ssd, m)
    h2, hn_next = resadd_fwd(h, m, g_post, g_next, name="sm_res")
    return h2, res, hn_next


def sm_layer_bwd(res, dh2, g_pre, g_post, w_in_p, convp, dt_bias, a_log, dskip, ssd_g, q_g, w_q_p, kv_g, w_kv_p, w_out, tabs,
                 carry=None, post=None, then=None):
    h, hn, proj, xbc_c, y, hst, cqn, ckvn, q_cat, k_cat, v, o, lse, y_ssd, m = res
    w_out = w_out()
    dm, dg_post = post if post is not None else norm_bwd(m, g_post, dh2, mask_pad=True, out_dtype=BF, name="sm_post_bwd")
    dw_out = jnp.concatenate([matmul(y_ssd, dm, "tn", name="sm_dwout_ssd"), matmul(o, dm, "tn", name="sm_dwout_att")], axis=0)
    dyab = matmul(dm, w_out, "nt", name="sm_dyab")
    dq_cat, dk_cat, dv, carried = attn_bwd(q_cat, k_cat, v, o, lse, dyab, carry(dw_out) if carry is not None else None)
    dq_raw, dkr = rope_bwd(dq_cat, dk_cat, tabs)
    kw = MLA_HEADS * LANE
    dw_kv_p = jnp.concatenate([matmul(ckvn, dk_cat, "tn", name="kv_dw_k"), matmul(ckvn, dv, "tn", name="kv_dw_v")], axis=1)
    dckvn = matmul([dk_cat, dv], [KBlock(w_kv_p, kw, 0), KBlock(w_kv_p, kw // 2, 2)], "nt", name="kv_dx")
    dckv, dg_kv = norm_bwd(proj, kv_g, dckvn, out_dtype=BF, col_blk=OFF_CKV // MLA_KV_RANK, width=MLA_KV_RANK,
                           name="kv_norm_bwd")
    dw_q_p = matmul(cqn, dq_raw, "tn", name="q_dw")
    dcqn = matmul(dq_raw, w_q_p, "nt", name="q_dx")
    dcq, dg_q = norm_bwd(proj, q_g, dcqn, out_dtype=BF, col_blk=OFF_CQ // MLA_Q_RANK, width=MLA_Q_RANK, name="q_norm_bwd")
    dy, dz, dg_ssd = gated_norm_bwd(y, proj, ssd_g, dyab)
    dxbc_c, ddt, dpar = ssd_bwd(xbc_c, proj, dt_bias, a_log, dskip, hst, dy)
    dxbc, dconvp = conv_silu_bwd(proj, OFF_XBC, SSD_CONV_CH, convp, dxbc_c, name="ssd_conv_bwd")
    pieces = [dz, dxbc, dckv, ddt, dkr, dcq]
    dw_in_p = jnp.concatenate([matmul(hn, pc, "tn", (BF,), name="sm_dwin_%d" % i) for i, pc in enumerate(pieces)], axis=1)
    third = SSD_CONV_CH // 3
    a_terms = [dz] + [KBlock(dxbc, third, i) for i in range(3)] + [dckv, ddt, dkr, dcq]
    b_terms = ([KBlock(w_in_p, SSD_D_INNER, 0)] + [KBlock(w_in_p, third, OFF_XBC // third + i) for i in range(3)]
               + [KBlock(w_in_p, MLA_KV_RANK, OFF_CKV // MLA_KV_RANK), KBlock(w_in_p, LANE, OFF_DT // LANE),
                  KBlock(w_in_p, LANE, OFF_KR // LANE), KBlock(w_in_p, MLA_Q_RANK, OFF_CQ // MLA_Q_RANK)])
    dhn = matmul(a_terms, b_terms, "nt", name="sm_dhn")
    dh, dg_pre, *below = norm_bwd(h, g_pre, dhn, dres=dh2, then=then, name="sm_pre_bwd")
    grads = dict(mix_pre_g=dg_pre, mix_post_g=dg_post, w_in=w_in_cols_to_blocks(dw_in_p), ssd_conv_w=dconvp[0:4],
                 ssd_conv_b=dconvp[4], ssd_dt_bias=dpar[0, :SSD_HEADS], ssd_a_log=dpar[1, :SSD_HEADS],
                 ssd_d=dpar[2, :SSD_HEADS], ssd_norm_g=dg_ssd, mla_q_norm_g=dg_q, mla_w_q_up=_unpack_w_q(dw_q_p),
                 mla_kv_norm_g=dg_kv, mla_w_kv_up=_unpack_w_kv(dw_kv_p), w_out_ab=dw_out)
    return dh, grads, carried, (tuple(below) or None)


W_IN_COLS = 3248
W_IN_SHARD = W_IN_COLS // N_DEV
W_IN_WIRE = 512


def _w_in_tables():
    src = np.full((IN_W,), -1, np.int64)
    src[0:2560] = np.arange(2560)
    src[OFF_CKV:OFF_CKV + 256] = 2960 + np.arange(256)
    src[OFF_DT:OFF_DT + 16] = 2560 + np.arange(16)
    src[OFF_KR + 64:OFF_KR + 96] = 3216 + np.arange(32)
    src[OFF_CQ:OFF_CQ + 384] = 2576 + np.arange(384)
    dev = np.where(src >= 0, src // W_IN_SHARD, -1).astype(np.int32).reshape(1, IN_W)
    col = np.where(src >= 0, src % W_IN_SHARD, 0).astype(np.int32).reshape(1, IN_W)
    return dev, col


W_IN_TILE = 384


def _w_in_devices_of_tile(dev):
    return [sorted(set(dev[0, t * W_IN_TILE:(t + 1) * W_IN_TILE].tolist()) - {-1}) for t in range(IN_W // W_IN_TILE)]


def _any_of(index, values):
    cond = index == values[0]
    for v in values[1:]:
        cond = cond | (index == v)
    return cond


def w_in_blocks_to_cols(g8):
    _, k, wp = g8.shape
    tn = W_IN_TILE
    dev, col = _w_in_tables()
    holders = _w_in_devices_of_tile(dev)

    def body(g_ref, dev_ref, col_ref, o_ref):
        i = pl.program_id(0)
        row = lax.broadcasted_iota(jnp.int32, (wp, tn), 0)
        o_ref[...] = jnp.zeros_like(o_ref)
        for j in range(N_DEV):
            tiles = [t for t, devs in enumerate(holders) if j in devs]
            if tiles:
                @pl.when(_any_of(i, tiles))
                def _(j=j):
                    sel = ((row == col_ref[...]) & (dev_ref[...] == j)).astype(BF)
                    o_ref[...] += jnp.dot(g_ref[j], sel, preferred_element_type=F32).astype(o_ref.dtype)

    dev, col = jnp.asarray(dev), jnp.asarray(col)
    return pl.pallas_call(
        body,
        out_shape=_sds((k, IN_W), BF),
        grid=(IN_W // tn,),
        in_specs=[pl.BlockSpec((N_DEV, k, wp), lambda i: (0, 0, 0)), pl.BlockSpec((1, tn), lambda i: (0, i)),
                  pl.BlockSpec((1, tn), lambda i: (0, i))],
        out_specs=pl.BlockSpec((k, tn), lambda i: (0, i)),
        compiler_params=_cp(("parallel",)),
        name="w_in_cols",
    )(g8, dev, col)


def w_in_cols_to_blocks(dw):
    k = dw.shape[0]
    tn = W_IN_TILE
    dev, col = _w_in_tables()
    holders = _w_in_devices_of_tile(dev)

    def body(dw_ref, dev_ref, col_ref, o_ref):
        j = pl.program_id(0)
        row = lax.broadcasted_iota(jnp.int32, (W_IN_WIRE, tn), 0)
        o_ref[...] = jnp.zeros_like(o_ref)
        for t, devs in enumerate(holders):
            if devs:
                @pl.when(_any_of(j, devs))
                def _(t=t):
                    cols = slice(t * tn, (t + 1) * tn)
                    sel = ((row == col_ref[:, cols]) & (dev_ref[:, cols] == j)).astype(BF)
                    o_ref[0] += lax.dot_general(dw_ref[:, cols], sel, NT_DIMS,
                                                preferred_element_type=F32).astype(o_ref.dtype)

    dev, col = jnp.asarray(dev), jnp.asarray(col)
    return pl.pallas_call(
        body,
        out_shape=_sds((N_DEV, k, W_IN_WIRE), BF),
        grid=(N_DEV,),
        in_specs=[pl.BlockSpec((k, IN_W), lambda j: (0, 0)), pl.BlockSpec((1, IN_W), lambda j: (0, 0)),
                  pl.BlockSpec((1, IN_W), lambda j: (0, 0))],
        out_specs=pl.BlockSpec((1, k, W_IN_WIRE), lambda j: (j, 0, 0)),
        compiler_params=_cp(("parallel",)),
        name="w_in_blocks",
    )(dw, dev, col)


def _pack_w_q(w):
    w3 = w.reshape(w.shape[0], MLA_HEADS, MLA_NOPE + MLA_ROPE)
    return jnp.pad(w3, ((0, 0), (0, 0), (0, LANE - MLA_NOPE - MLA_ROPE))).reshape(w.shape[0], MLA_HEADS * LANE)


def _unpack_w_q(p):
    return p.reshape(p.shape[0], MLA_HEADS, LANE)[:, :, :MLA_NOPE + MLA_ROPE].reshape(p.shape[0], -1)


def _pack_w_kv(w):
    w3 = w.reshape(w.shape[0], MLA_HEADS, MLA_NOPE + MLA_V)
    k = jnp.pad(w3[:, :, :MLA_NOPE], ((0, 0), (0, 0), (0, LANE - MLA_NOPE))).reshape(w.shape[0], MLA_HEADS * LANE)
    return jnp.concatenate([k, w3[:, :, MLA_NOPE:].reshape(w.shape[0], MLA_HEADS * MLA_V)], axis=1)


def _unpack_w_kv(p):
    k = p[:, :MLA_HEADS * LANE].reshape(p.shape[0], MLA_HEADS, LANE)[:, :, :MLA_NOPE]
    v = p[:, MLA_HEADS * LANE:].reshape(p.shape[0], MLA_HEADS, MLA_V)
    return jnp.concatenate([k, v], axis=2).reshape(p.shape[0], -1)


def _rows8(rows, width):
    a = jnp.concatenate([r.reshape(-1, width) for r in rows], axis=0)
    return jnp.pad(a, ((0, 8 - a.shape[0]), (0, 0)))


SLAB_ROWS = 16


def _to_slab(flat_list, lead=()):
    cat = jnp.concatenate(flat_list, axis=-1)
    n = cat.shape[-1]
    unit = SLAB_ROWS * PACK_W
    total = -(-n // unit) * unit
    cat = jnp.pad(cat, [(0, 0)] * len(lead) + [(0, total - n)])
    return cat.reshape(lead + (total // PACK_W, PACK_W))


def _from_flat(flat, shapes):
    out, off = [], 0
    for s in shapes:
        n = int(np.prod(s))
        out.append(flat[off:off + n].reshape(s))
        off += n
    return out


def _gathered_full(g8, axis):
    moved = jnp.moveaxis(g8, 0, axis)
    shp = moved.shape
    return moved.reshape(shp[:axis] + (shp[axis] * shp[axis + 1],) + shp[axis + 2:])


def _per_device(full, axis):
    shp = full.shape
    split = full.reshape(shp[:axis] + (N_DEV, shp[axis] // N_DEV) + shp[axis + 1:])
    return jnp.moveaxis(split, axis, 0)


ARG_NAMES = ['x', 'meta_tokens', 'mix_pre_g', 'mix_post_g', 'mlp_pre_g', 'mlp_post_g', 'w_up', 'w_down', 'w_in',
             'ssd_conv_w', 'ssd_conv_b', 'ssd_dt_bias', 'ssd_a_log', 'ssd_d', 'ssd_norm_g', 'mla_q_norm_g',
             'mla_w_q_up', 'mla_kv_norm_g', 'mla_w_kv_up', 'w_out_ab', 'rg_w_x', 'rg_w_y', 'rg_conv_w', 'rg_conv_b',
             'rg_w_a', 'rg_b_a', 'rg_w_i', 'rg_b_i', 'rg_lambda', 'rg_w_out']
WEIGHTS = ARG_NAMES[1:]
BIG = {'w_up': 2, 'w_down': 1, 'w_in': 2, 'mla_w_q_up': 2, 'mla_w_kv_up': 2, 'w_out_ab': 1, 'rg_w_x': 2,
       'rg_w_y': 2, 'rg_w_out': 1}
SMALL = {'meta_tokens': 1, 'ssd_conv_w': 2, 'rg_conv_w': 2, 'rg_conv_b': 1, 'rg_b_a': 1, 'rg_b_i': 1, 'rg_lambda': 1}
REPL = [n for n in WEIGHTS if n not in BIG and n not in SMALL]
REPL_MEDIUM = ['rg_w_a', 'rg_w_i']
REPL_TINY = [n for n in REPL if n not in REPL_MEDIUM]


def _piece_axes():
    axes = {}
    for n, ax in BIG.items():
        for i in range(DEPTH if n in ('w_up', 'w_down') else DEPTH // 2):
            axes[(n, i)] = ax - 1
    return axes


PIECE_AXIS = _piece_axes()
AS_BLOCKS = ('w_up', 'w_down')
_RG = lambda i: [(n, i) for n in ('rg_w_x', 'rg_w_y', 'rg_w_out')]
_MLP = lambda l: [('w_up', l), ('w_down', l)]
_SM_IN = lambda i: [(n, i) for n in ('w_in', 'mla_w_q_up', 'mla_w_kv_up')]
GATHER_FIRST = _SM_IN(0)
GATHER_AT = {0: [('w_out_ab', 0)] + _MLP(0) + _RG(0) + _MLP(1) + _SM_IN(1), 2: [('w_out_ab', 1)] + _MLP(2) + _RG(1) + _MLP(3)}
SCATTER_AT = {2: _MLP(3) + _RG(1) + _MLP(2) + [('w_out_ab', 1)],
              0: _SM_IN(1) + _MLP(1) + _RG(0) + _MLP(0) + [('w_out_ab', 0)]}
SCATTER_LAST = _SM_IN(0)


def _wire_block(p, key):
    n, i = key
    blk = p[n][i]
    if n == 'w_in':
        blk = jnp.pad(blk, ((0, 0), (0, W_IN_WIRE - blk.shape[1])))
    return blk


def _step(p, moments):
    assert DEPTH == 4
    full = {n: [None] * p[n].shape[0] for n in BIG}
    full['w_in_g'] = [None] * p['w_in'].shape[0]

    def weight_blocks(group):
        return [_wire_block(p, k).astype(BF) for k in group]

    def take_weights(group, gathered):
        for (n, i), piece in zip(group, gathered):
            if n == 'w_in':
                full['w_in_g'][i] = piece
            elif n in AS_BLOCKS:
                full[n][i] = DevBlocks(piece, PIECE_AXIS[(n, i)])
            else:
                full[n][i] = _gathered_full(piece, PIECE_AXIS[(n, i)])

    def grad_blocks(group, gw):
        return [gw[k] if k[0] in AS_BLOCKS or k[0] == 'w_in' else _per_device(gw[k], PIECE_AXIS[k]).astype(BF)
                for k in group]

    parts = {}

    small_slab = _to_slab([p[n].reshape(-1) for n in SMALL])
    *first, small8 = all_gather(weight_blocks(GATHER_FIRST) + [small_slab], name="gather_first")
    take_weights(GATHER_FIRST, first)
    for n, piece in zip(SMALL, _from_flat_rows(small8, [p[n].shape for n in SMALL])):
        full[n] = _gathered_full(piece, SMALL[n])
    for n in REPL:
        full[n] = p[n]
    loss_local, grad_x, gw, gsmall_full, carried = _local_step(
        full, p['x'][0], p['loss_target'][0],
        fwd_carry=lambda layer: Exchange("gather", weight_blocks(GATHER_AT[layer])),
        on_fwd_carried=lambda layer, got: take_weights(GATHER_AT[layer], got),
        bwd_carry=lambda layer, gw_now, others: Exchange(
            "scatter", grad_blocks(SCATTER_AT[layer], gw_now)
            + ([jnp.stack(others[n], axis=0).reshape(N_DEV, -1, LANE) for n in REPL_MEDIUM] if layer == 0 else [])))

    for layer, group in SCATTER_AT.items():
        parts.update(zip(group, carried[layer]))
    rep_flat = jnp.concatenate([gsmall_full[n].reshape(-1) for n in REPL_TINY])
    rep_n = rep_flat.shape[0]
    rep_chunk = -(-rep_n // (N_DEV * PACK_W * 8)) * PACK_W * 8
    rep8 = jnp.pad(rep_flat, (0, N_DEV * rep_chunk - rep_n)).reshape(N_DEV, rep_chunk)
    gsmall = _to_slab([_per_device(gsmall_full[n], SMALL[n]).reshape(N_DEV, -1) for n in SMALL] + [rep8], lead=(N_DEV,))
    received = all_to_all(grad_blocks(SCATTER_LAST, gw) + [gsmall], name="scatter_last")
    n_last = len(SCATTER_LAST)
    parts.update(zip(SCATTER_LAST, received[:n_last]))
    ssmall = slab_sum(received[n_last], name="sum_small").reshape(-1)
    medium_mine = [slab_sum(r8, name="sum_" + n) for n, r8 in zip(REPL_MEDIUM, carried[0][len(SCATTER_AT[0]):])]
    g_loc = {'w_in': jnp.stack([slab_sum(parts[('w_in', i)], name="sum_w_in_%d" % i)[:, :W_IN_SHARD]
                                for i in range(p['w_in'].shape[0])], axis=0)}
    small_n = sum(int(np.prod(p[n].shape)) for n in SMALL)
    g_loc.update(zip(SMALL, _from_flat(ssmall, [p[n].shape for n in SMALL])))
    rep_mine = ssmall[small_n:small_n + rep_chunk].reshape(-1, PACK_W)
    rep_all, *medium_all = all_gather([rep_mine] + medium_mine, name="gather_replicated")
    g_loc.update(zip(REPL_TINY, _from_flat(rep_all.reshape(-1), [p[n].shape for n in REPL_TINY])))
    g_loc.update({n: g.reshape(p[n].shape) for n, g in zip(REPL_MEDIUM, medium_all)})

    out = {'loss': lax.psum(loss_local, ("x", "y", "c")), 'grad_x': grad_x[None]}
    small_names = list(SMALL) + REPL_TINY
    for n in list(BIG) + REPL_MEDIUM:
        shp = p[n].shape
        if n == 'w_in' or n in REPL_MEDIUM:
            v2 = lambda a: a.reshape(-1, shp[-1])
            d, nm, nv = adamw(v2(p[n]), v2(g_loc[n]), v2(moments['m_' + n]), v2(moments['v_' + n]), name="adamw_" + n)
            d, nm, nv = d.reshape(shp), nm.reshape(shp), nv.reshape(shp)
        else:
            g_loc[n], d, nm, nv = adamw_blocks(p[n], moments['m_' + n], moments['v_' + n],
                                               [parts[(n, i)] for i in range(shp[0])], name="adamw_" + n)
        out['delta_' + n], out['new_m_' + n], out['new_v_' + n] = d, nm, nv
    slab = lambda src: _to_slab([src(n).reshape(-1) for n in small_names])
    d, nm, nv = adamw(slab(lambda n: p[n]), slab(lambda n: g_loc[n]), slab(lambda n: moments['m_' + n]),
                      slab(lambda n: moments['v_' + n]), name="adamw_small")
    shapes = [p[n].shape for n in small_names]
    for key, flat in (('delta_', d), ('new_m_', nm), ('new_v_', nv)):
        for n, a in zip(small_names, _from_flat(flat.reshape(-1), shapes)):
            out[key + n] = a
    for n in WEIGHTS:
        out['grad_' + n] = g_loc[n]
    return out


def _local_step(full, x, target_rows, fwd_carry=None, on_fwd_carried=None, bwd_carry=None):
    t = PAD + N_META + x.shape[0]
    h = jnp.concatenate([jnp.zeros((PAD, D_MODEL), F32), full['meta_tokens'], x], axis=0)
    target = jnp.concatenate([jnp.zeros((PAD + N_META, D_MODEL), F32), target_rows], axis=0)
    tabs = rope_tables(t)

    def layer_args(layer):
        i = layer // 2
        if layer % 2 == 0:
            convp = _rows8([full['ssd_conv_w'][i], full['ssd_conv_b'][i]], SSD_CONV_CH)
            return (full['mix_pre_g'][layer], full['mix_post_g'][layer], w_in_blocks_to_cols(full['w_in_g'][i]), convp,
                    full['ssd_dt_bias'][i], full['ssd_a_log'][i], full['ssd_d'][i], full['ssd_norm_g'][i],
                    full['mla_q_norm_g'][i], _pack_w_q(full['mla_w_q_up'][i]), full['mla_kv_norm_g'][i],
                    _pack_w_kv(full['mla_w_kv_up'][i]), lambda: full['w_out_ab'][i], tabs)
        rgp = _rows8([full['rg_conv_w'][i], full['rg_conv_b'][i], full['rg_b_a'][i], full['rg_b_i'][i],
                      full['rg_lambda'][i]], LRU_WIDTH)
        return (full['mix_pre_g'][layer], full['mix_post_g'][layer], full['rg_w_x'][i], full['rg_w_y'][i], rgp,
                full['rg_w_a'][i], full['rg_w_i'][i], full['rg_w_out'][i])

    def mlp_args(layer):
        return (full['mlp_pre_g'][layer], full['mlp_post_g'][layer], full['w_up'][layer], full['w_down'][layer])

    saved = []
    hn = None
    for layer in range(DEPTH):
        la = layer_args(layer)
        to_mlp = dict(hn=hn, g_next=full['mlp_pre_g'][layer])
        if layer % 2 == 0:
            if fwd_carry is not None:
                h, res_mix, hn = sm_layer_fwd(h, *la, carry=fwd_carry(layer),
                                              on_carried=lambda got, layer=layer: on_fwd_carried(layer, got), **to_mlp)
            else:
                h, res_mix, hn = sm_layer_fwd(h, *la, **to_mlp)
        else:
            h, res_mix, hn = rg_layer_fwd(h, *la, **to_mlp)
        ma = mlp_args(layer)
        h, res_mlp, hn = mlp_fwd(h, *ma, hn=hn, g_next=full['mix_pre_g'][layer + 1] if layer + 1 < DEPTH else None)
        saved.append((la, ma, res_mix, res_mlp))
    loss_local, dh = loss_fwd_bwd(h, target)

    others = {n: [None] * len(full[n]) for n in WEIGHTS if n not in BIG and n != 'meta_tokens'}
    gw, carried = {}, {}
    post = None
    for layer in reversed(range(DEPTH)):
        la, ma, res_mix, res_mlp = saved[layer]
        dh, gm, post = mlp_bwd(res_mlp, dh, *ma, post=post, then=(res_mix[-1], la[1]))
        below = (saved[layer - 1][3][-1], saved[layer - 1][1][1]) if layer > 0 else None
        if layer % 2 == 0:
            for n in ('w_up', 'w_down'):
                gw[(n, layer)] = gm[n]
            carry = None
            if bwd_carry is not None:
                carry = lambda dw_out, layer=layer: bwd_carry(layer, {**gw, ('w_out_ab', layer // 2): dw_out}, others)
            dh, gx, carried[layer], post = sm_layer_bwd(res_mix, dh, *la, carry=carry, post=post, then=below)
        else:
            dh, gx, post = rg_layer_bwd(res_mix, dh, *la, post=post, then=below)
        for n, g in list(gm.items()) + list(gx.items()):
            i = layer if n in ('mix_pre_g', 'mix_post_g', 'mlp_pre_g', 'mlp_post_g', 'w_up', 'w_down') else layer // 2
            if n in BIG:
                gw[(n, i)] = g
            else:
                others[n][i] = g
    gothers = {n: jnp.stack(v, axis=0) for n, v in others.items()}
    gothers['meta_tokens'] = dh[PAD:PAD + N_META]
    return loss_local, dh[PAD + N_META:], gw, gothers, carried


def _from_flat_rows(g8, shapes):
    flat = g8.reshape(N_DEV, -1)
    out, off = [], 0
    for s in shapes:
        n = int(np.prod(s))
        out.append(flat[:, off:off + n].reshape((N_DEV,) + tuple(s)))
        off += n
    return out


def kernel(x, meta_tokens, mix_pre_g, mix_post_g, mlp_pre_g, mlp_post_g, w_up, w_down, w_in, ssd_conv_w, ssd_conv_b, ssd_dt_bias, ssd_a_log, ssd_d, ssd_norm_g, mla_q_norm_g, mla_w_q_up, mla_kv_norm_g, mla_w_kv_up, w_out_ab, rg_w_x, rg_w_y, rg_conv_w, rg_conv_b, rg_w_a, rg_b_a, rg_w_i, rg_b_i, rg_lambda, rg_w_out, loss_target, m_meta_tokens, m_mix_pre_g, m_mix_post_g, m_mlp_pre_g, m_mlp_post_g, m_w_up, m_w_down, m_w_in, m_ssd_conv_w, m_ssd_conv_b, m_ssd_dt_bias, m_ssd_a_log, m_ssd_d, m_ssd_norm_g, m_mla_q_norm_g, m_mla_w_q_up, m_mla_kv_norm_g, m_mla_w_kv_up, m_w_out_ab, m_rg_w_x, m_rg_w_y, m_rg_conv_w, m_rg_conv_b, m_rg_w_a, m_rg_b_a, m_rg_w_i, m_rg_b_i, m_rg_lambda, m_rg_w_out, v_meta_tokens, v_mix_pre_g, v_mix_post_g, v_mlp_pre_g, v_mlp_post_g, v_w_up, v_w_down, v_w_in, v_ssd_conv_w, v_ssd_conv_b, v_ssd_dt_bias, v_ssd_a_log, v_ssd_d, v_ssd_norm_g, v_mla_q_norm_g, v_mla_w_q_up, v_mla_kv_norm_g, v_mla_w_kv_up, v_w_out_ab, v_rg_w_x, v_rg_w_y, v_rg_conv_w, v_rg_conv_b, v_rg_w_a, v_rg_b_a, v_rg_w_i, v_rg_b_i, v_rg_lambda, v_rg_w_out):
    args = (x, meta_tokens, mix_pre_g, mix_post_g, mlp_pre_g, mlp_post_g, w_up, w_down, w_in, ssd_conv_w, ssd_conv_b, ssd_dt_bias, ssd_a_log, ssd_d, ssd_norm_g, mla_q_norm_g, mla_w_q_up, mla_kv_norm_g, mla_w_kv_up, w_out_ab, rg_w_x, rg_w_y, rg_conv_w, rg_conv_b, rg_w_a, rg_b_a, rg_w_i, rg_b_i, rg_lambda, rg_w_out, loss_target, m_meta_tokens, m_mix_pre_g, m_mix_post_g, m_mlp_pre_g, m_mlp_post_g, m_w_up, m_w_down, m_w_in, m_ssd_conv_w, m_ssd_conv_b, m_ssd_dt_bias, m_ssd_a_log, m_ssd_d, m_ssd_norm_g, m_mla_q_norm_g, m_mla_w_q_up, m_mla_kv_norm_g, m_mla_w_kv_up, m_w_out_ab, m_rg_w_x, m_rg_w_y, m_rg_conv_w, m_rg_conv_b, m_rg_w_a, m_rg_b_a, m_rg_w_i, m_rg_b_i, m_rg_lambda, m_rg_w_out, v_meta_tokens, v_mix_pre_g, v_mix_post_g, v_mlp_pre_g, v_mlp_post_g, v_w_up, v_w_down, v_w_in, v_ssd_conv_w, v_ssd_conv_b, v_ssd_dt_bias, v_ssd_a_log, v_ssd_d, v_ssd_norm_g, v_mla_q_norm_g, v_mla_w_q_up, v_mla_kv_norm_g, v_mla_w_kv_up, v_w_out_ab, v_rg_w_x, v_rg_w_y, v_rg_conv_w, v_rg_conv_b, v_rg_w_a, v_rg_b_a, v_rg_w_i, v_rg_b_i, v_rg_lambda, v_rg_w_out,)
    n_w = len(ARG_NAMES)
    p = dict(zip(ARG_NAMES, args[:n_w]))
    p['loss_target'] = args[n_w]
    moments = {}
    for i, n in enumerate(WEIGHTS):
        moments['m_' + n] = args[n_w + 1 + i]
        moments['v_' + n] = args[n_w + 1 + len(WEIGHTS) + i]
    out = _step(p, moments)
    res = [out['loss'], out['grad_x']]
    for prefix in ('grad_', 'delta_', 'new_m_', 'new_v_'):
        res += [out[prefix + n] for n in WEIGHTS]
    return tuple(res)
```

```python
import math

import numpy as np
import jax
import jax.numpy as jnp
from jax import lax
from jax.experimental import pallas as pl
from jax.experimental.pallas import tpu as pltpu

F32 = jnp.float32
BF = jnp.bfloat16
HI = lax.Precision.HIGHEST

D_MODEL = 1024
DEPTH = 4
N_META = 16
CHUNK = 128
PAD = CHUNK - N_META
EPS = 1e-6
SSD_HEADS = 16
SSD_HEAD_DIM = 64
SSD_D_INNER = 1024
SSD_STATE = 128
SSD_CONV_CH = 1536
MLA_HEADS = 16
MLA_NOPE = 64
MLA_ROPE = 32
MLA_V = 64
MLA_Q_RANK = 384
MLA_KV_RANK = 256
ROPE_BASE = 10000.0
LRU_WIDTH = 1280
LRU_BLOCKS = 10
LRU_C = 8.0
D_FF = 4096
N_DEV = 8
LANE = 128
IN_W = 3456
OFF_Z, OFF_XBC, OFF_CKV, OFF_DT, OFF_KR, OFF_CQ = 0, 1024, 2560, 2816, 2944, 3072

ADAM_LR = 0.001
ADAM_B1 = 0.9
ADAM_B2 = 0.999
ADAM_EPS = 1e-08
ADAM_WD = 0.01
ADAM_STEP = 10

VMEM_LIMIT = 56 * 1024 * 1024
NEG = -1e30


def _pick(n, cands):
    for c in cands:
        if n % c == 0:
            return c
    return n


def _cp(sem=None):
    return pltpu.CompilerParams(dimension_semantics=sem, vmem_limit_bytes=VMEM_LIMIT)


def _sds(shape, dtype):
    return jax.ShapeDtypeStruct(tuple(shape), dtype)


def _silu(x):
    return x * jax.nn.sigmoid(x)


def _softplus(x):
    return jnp.maximum(x, 0.0) + jnp.log(1.0 + jnp.exp(-jnp.abs(x)))


def _gelu(x):
    c = math.sqrt(2.0 / math.pi)
    return 0.5 * x * (1.0 + jnp.tanh(c * (x + 0.044715 * (x * x * x))))


def _row_mask(i, tr, shape, first_valid=PAD):
    row = i * tr + lax.broadcasted_iota(jnp.int32, shape, 0)
    return row >= first_valid


class KBlock:
    def __init__(self, arr, width, blk):
        self.arr, self.width, self.blk = arr, width, blk


class DevBlocks:
    def __init__(self, g8, axis):
        self.g8, self.axis = g8, axis
        _, r, c = g8.shape
        self.shape = (N_DEV * r, c) if axis == 0 else (r, N_DEV * c)


NN_DIMS = (((1,), (0,)), ((), ()))
MM_TALL_K = 1536
MM_WHOLE_K, MM_WHOLE_M = 1024, 4224


def matmul(a, b, mode, out_dtypes=(F32,), epi=None, extras=(), name="mm", tm=None, tn=None, out_blocks=False):
    a_terms = a if isinstance(a, (list, tuple)) else [a]
    b_terms = b if isinstance(b, (list, tuple)) else [b]
    assert len(a_terms) == len(b_terms) and (mode != "tn" or len(a_terms) == 1)
    arr_of = lambda t: t.arr if isinstance(t, KBlock) else t
    if mode == "tn":
        m, n = a_terms[0].shape[1], b_terms[0].shape[1]
    else:
        m = arr_of(a_terms[0]).shape[0]
        b0 = b_terms[0]
        n = (b0.shape if isinstance(b0, DevBlocks) else arr_of(b0).shape)[1 if mode == "nn" else 0]
    if mode == "tn":
        tm = _pick(m, (1024, 512, 384, 256, 128))
    else:
        k_all = sum(t.width if isinstance(t, KBlock) else t.shape[1] for t in a_terms)
        tall = (2112,) if k_all <= MM_TALL_K else ()
        if k_all <= MM_WHOLE_K and m <= MM_WHOLE_M and n % 256 == 0 and tm is None and tn is None:
            tm, tn = m, 256
        tm = tm or _pick(m, tall + (1056, 1024, 768, 640, 512, 384, 256, 128))
    tn = tn or _pick(n, (512, 640, 384, 256, 128))
    dims = {"nn": NN_DIMS, "nt": NT_DIMS, "tn": TN_DIMS}[mode]

    in_specs, args, plan = [], [], []
    for ta, tb in zip(a_terms, b_terms):
        if mode == "tn":
            k = ta.shape[0]
            in_specs += [pl.BlockSpec((k, tm), lambda i, j: (0, i)), pl.BlockSpec((k, tn), lambda i, j: (0, j))]
            args += [ta, tb]
            plan.append(None)
            continue
        if isinstance(ta, KBlock):
            kw, ka = ta.width, ta.blk
            in_specs.append(pl.BlockSpec((tm, kw), lambda i, j, ka=ka: (i, ka)))
        else:
            kw = ta.shape[1]
            in_specs.append(pl.BlockSpec((tm, kw), lambda i, j: (i, 0)))
        args.append(arr_of(ta))
        if isinstance(tb, DevBlocks):
            _, r, c = tb.g8.shape
            split_k = tb.axis == (0 if mode == "nn" else 1)
            if split_k:
                kd = r if mode == "nn" else c
                assert kw == N_DEV * kd
                blk = (N_DEV, kd, tn) if mode == "nn" else (N_DEV, tn, kd)
                in_specs.append(pl.BlockSpec(blk, (lambda i, j: (0, 0, j)) if mode == "nn" else (lambda i, j: (0, j, 0))))
                plan.append(kd)
            else:
                per = (c if mode == "nn" else r) // tn
                blk = (None, kw, tn) if mode == "nn" else (None, tn, kw)
                in_specs.append(pl.BlockSpec(blk, (lambda i, j, per=per: (j // per, 0, j % per)) if mode == "nn"
                                             else (lambda i, j, per=per: (j // per, j % per, 0))))
                plan.append(None)
            args.append(tb.g8)
        else:
            kb = tb.blk if isinstance(tb, KBlock) else 0
            assert (tb.width if isinstance(tb, KBlock) else tb.shape[0 if mode == "nn" else 1]) == kw
            in_specs.append(pl.BlockSpec((kw, tn), lambda i, j, kb=kb: (kb, j)) if mode == "nn"
                            else pl.BlockSpec((tn, kw), lambda i, j, kb=kb: (j, kb)))
            args.append(arr_of(tb))
            plan.append(None)
    n_terms, n_ex = len(plan), len(extras)

    def body(*refs):
        ex_refs, out_refs = refs[2 * n_terms:2 * n_terms + n_ex], refs[2 * n_terms + n_ex:]
        acc = None
        for t, kd in enumerate(plan):
            a_ref, b_ref = refs[2 * t], refs[2 * t + 1]
            if kd is None:
                parts = [lax.dot_general(a_ref[...].astype(BF), b_ref[...].astype(BF), dims, preferred_element_type=F32)]
            else:
                parts = [lax.dot_general(a_ref[:, d * kd:(d + 1) * kd].astype(BF), b_ref[d].astype(BF), dims,
                                         preferred_element_type=F32) for d in range(N_DEV)]
            for part in parts:
                acc = part if acc is None else acc + part
        outs = (acc,) if epi is None else epi(acc, *[r[...] for r in ex_refs])
        for r, o in zip(out_refs, outs):
            r[...] = o.astype(r.dtype)

    o_spec = pl.BlockSpec((tm, tn), lambda i, j: (i, j))
    if out_blocks:
        per = n // N_DEV // tn
        out_shape = tuple(_sds((N_DEV, m, n // N_DEV), dt) for dt in out_dtypes)
        out_specs = tuple(pl.BlockSpec((None, tm, tn), lambda i, j: (j // per, i, j % per)) for _ in out_dtypes)
    else:
        out_shape = tuple(_sds((m, n), dt) for dt in out_dtypes)
        out_specs = tuple(o_spec for _ in out_dtypes)
    outs = pl.pallas_call(
        body,
        out_shape=out_shape,
        grid=(m // tm, n // tn),
        in_specs=in_specs + [o_spec] * n_ex,
        out_specs=out_specs,
        compiler_params=_cp(("parallel", "parallel")),
        name=name,
    )(*args, *extras)
    return outs[0] if len(out_dtypes) == 1 else outs


def _rt(t):
    return _pick(t, (384, 256, 128))


def _rt_wide(t):
    return _pick(t, (704, 384, 256, 128))


def norm_fwd(x, g, out_dtype, col_blk=0, width=None, name="norm_fwd"):
    t = x.shape[0]
    w = width or x.shape[1]
    tr = _rt_wide(t)

    def body(x_ref, g_ref, o_ref):
        xv = x_ref[...]
        r = lax.rsqrt(jnp.mean(xv * xv, axis=-1, keepdims=True) + EPS)
        o_ref[...] = (xv * r * g_ref[...]).astype(o_ref.dtype)

    return pl.pallas_call(
        body,
        out_shape=_sds((t, w), out_dtype),
        grid=(t // tr,),
        in_specs=[pl.BlockSpec((tr, w), lambda i: (i, col_blk)), pl.BlockSpec((1, w), lambda i: (0, 0))],
        out_specs=pl.BlockSpec((tr, w), lambda i: (i, 0)),
        compiler_params=_cp(("parallel",)),
        name=name,
    )(x, g.reshape(1, w))


def _rms_bwd(xv, gv, dyv):
    r = lax.rsqrt(jnp.mean(xv * xv, axis=-1, keepdims=True) + EPS)
    xh = xv * r
    dyg = dyv * gv
    dx = r * (dyg - xh * jnp.mean(dyg * xh, axis=-1, keepdims=True))
    return dx, jnp.sum(dyv * xh, axis=0, keepdims=True)


def norm_bwd(x, g, dy, dres=None, mask_pad=False, out_dtype=F32, col_blk=0, width=None, dy_col_blk=0, then=None,
             name="norm_bwd"):
    t = x.shape[0]
    w = width or x.shape[1]
    tr = _rt_wide(t)
    has_res, has_then = dres is not None, then is not None

    def body(*refs):
        x_ref, g_ref, dy_ref = refs[:3]
        n_in = 3 + has_res + 2 * has_then
        dx_ref, dg_ref = refs[n_in:n_in + 2]
        i = pl.program_id(0)
        dyv = dy_ref[...].astype(F32)
        if mask_pad:
            dyv = jnp.where(_row_mask(i, tr, dyv.shape), dyv, 0.0)
        dx, dg = _rms_bwd(x_ref[...], g_ref[...], dyv)
        if has_res:
            dx = dx + refs[3][...]
        dx_ref[...] = dx.astype(dx_ref.dtype)

        @pl.when(i == 0)
        def _():
            for r in refs[n_in + 1::2]:
                r[...] = jnp.zeros_like(r)

        dg_ref[...] += dg
        if has_then:
            x2_ref, g2_ref = refs[3 + has_res:5 + has_res]
            dx2_ref, dg2_ref = refs[n_in + 2:]
            dx2, dg2 = _rms_bwd(x2_ref[...], g2_ref[...], jnp.where(_row_mask(i, tr, dx.shape), dx, 0.0))
            dx2_ref[...] = dx2.astype(dx2_ref.dtype)
            dg2_ref[...] += dg2

    row = pl.BlockSpec((tr, w), lambda i: (i, 0))
    vec = pl.BlockSpec((1, w), lambda i: (0, 0))
    in_specs = [pl.BlockSpec((tr, w), lambda i: (i, col_blk)), vec, pl.BlockSpec((tr, w), lambda i: (i, dy_col_blk))]
    args = [x, g.reshape(1, w), dy]
    out_shape, out_specs = [_sds((t, w), out_dtype), _sds((1, w), F32)], [row, vec]
    if has_res:
        in_specs.append(row)
        args.append(dres)
    if has_then:
        in_specs += [row, vec]
        args += [then[0], then[1].reshape(1, w)]
        out_shape += [_sds((t, w), BF), _sds((1, w), F32)]
        out_specs += [row, vec]
    outs = pl.pallas_call(
        body,
        out_shape=tuple(out_shape),
        grid=(t // tr,),
        in_specs=in_specs,
        out_specs=tuple(out_specs),
        compiler_params=_cp(("arbitrary",)),
        name=name,
    )(*args)
    if has_then:
        return outs[0], outs[1].reshape(w), outs[2], outs[3].reshape(w)
    return outs[0], outs[1].reshape(w)


def resadd_fwd(h, m, g, g_next=None, name="resadd"):
    t, w = h.shape
    tr = _rt_wide(t)
    with_next = g_next is not None

    def body(h_ref, m_ref, g_ref, *rest):
        mv = m_ref[...]
        r = lax.rsqrt(jnp.mean(mv * mv, axis=-1, keepdims=True) + EPS)
        y = mv * r * g_ref[...]
        h2 = h_ref[...] + jnp.where(_row_mask(pl.program_id(0), tr, y.shape), y, 0.0)
        if with_next:
            gn_ref, o_ref, hn_ref = rest
            r2 = lax.rsqrt(jnp.mean(h2 * h2, axis=-1, keepdims=True) + EPS)
            hn_ref[...] = (h2 * r2 * gn_ref[...]).astype(hn_ref.dtype)
        else:
            (o_ref,) = rest
        o_ref[...] = h2

    row = pl.BlockSpec((tr, w), lambda i: (i, 0))
    vec = pl.BlockSpec((1, w), lambda i: (0, 0))
    outs = pl.pallas_call(
        body,
        out_shape=(_sds((t, w), F32),) + ((_sds((t, w), BF),) if with_next else ()),
        grid=(t // tr,),
        in_specs=[row, row, vec] + ([vec] if with_next else []),
        out_specs=(row,) + ((row,) if with_next else ()),
        compiler_params=_cp(("parallel",)),
        name=name,
    )(h, m, g.reshape(1, w), *((g_next.reshape(1, w),) if with_next else ()))
    return outs[0], (outs[1] if with_next else None)


def loss_fwd_bwd(h, target):
    t, w = h.shape
    tr = _rt_wide(t)

    def body(h_ref, t_ref, s_ref, dh_ref):
        i = pl.program_id(0)
        err = h_ref[...] - t_ref[...]
        err = jnp.where(_row_mask(i, tr, err.shape, PAD + N_META), err, 0.0)
        dh_ref[...] = err * (1.0 / w)

        @pl.when(i == 0)
        def _():
            s_ref[...] = jnp.zeros_like(s_ref)

        s_ref[...] += jnp.sum(err * err).reshape(1, 1)

    s, dh = pl.pallas_call(
        body,
        out_shape=(_sds((1, LANE), F32), _sds((t, w), F32)),
        grid=(t // tr,),
        in_specs=[pl.BlockSpec((tr, w), lambda i: (i, 0)), pl.BlockSpec((tr, w), lambda i: (i, 0))],
        out_specs=(pl.BlockSpec((1, LANE), lambda i: (0, 0)), pl.BlockSpec((tr, w), lambda i: (i, 0))),
        compiler_params=_cp(("arbitrary",)),
        name="loss",
    )(h, target)
    return 0.5 * s[0, 0] / w, dh


def _shift_down(ext, k, n):
    return pltpu.roll(ext, k, 0)[8:]


def _conv_pre(ext, x, w_ref, n):
    return (w_ref[4:5, :] + w_ref[3:4, :] * x + w_ref[2:3, :] * _shift_down(ext, 1, n)
            + w_ref[1:2, :] * _shift_down(ext, 2, n) + w_ref[0:1, :] * _shift_down(ext, 3, n))


def _conv_bwd_parts(dpre, dnext, x, ext, w_ref, n):
    extd = jnp.concatenate([dpre, dnext], axis=0)
    ln = n + 8
    dx = (w_ref[3:4, :] * dpre + w_ref[2:3, :] * pltpu.roll(extd, ln - 1, 0)[:n]
          + w_ref[1:2, :] * pltpu.roll(extd, ln - 2, 0)[:n] + w_ref[0:1, :] * pltpu.roll(extd, ln - 3, 0)[:n])
    sums = [jnp.sum(dpre * _shift_down(ext, 3, n), axis=0, keepdims=True),
            jnp.sum(dpre * _shift_down(ext, 2, n), axis=0, keepdims=True),
            jnp.sum(dpre * _shift_down(ext, 1, n), axis=0, keepdims=True),
            jnp.sum(dpre * x, axis=0, keepdims=True),
            jnp.sum(dpre, axis=0, keepdims=True)]
    return dx, sums


def _rows_block(sums):
    w = sums[0].shape[1]
    row = lax.broadcasted_iota(jnp.int32, (8, w), 0)
    out = jnp.zeros((8, w), F32)
    for k, s in enumerate(sums):
        out = jnp.where(row == k, s, out)
    return out


CONV_BLOCK = 512


def conv_silu_fwd(x, col0, c, wb, name="conv_fwd"):
    t = x.shape[0]
    cw = _pick(c, (CONV_BLOCK, LANE))
    nblk, col0_blk = c // cw, col0 // cw
    assert col0 % cw == 0
    tr = _pick(t, (1408,) + (_rt_wide(t),))

    def body(x_ref, w_ref, o_ref, prev):
        ti = pl.program_id(1)

        @pl.when(ti == 0)
        def _():
            prev[...] = jnp.zeros_like(prev)

        xv = x_ref[...]
        ext = jnp.concatenate([prev[...], xv], axis=0)
        o_ref[...] = _silu(_conv_pre(ext, xv, w_ref, tr))
        prev[...] = xv[tr - 8:, :]

    return pl.pallas_call(
        body,
        out_shape=_sds((t, c), F32),
        grid=(nblk, t // tr),
        in_specs=[pl.BlockSpec((tr, cw), lambda cb, ti: (ti, col0_blk + cb)),
                  pl.BlockSpec((8, cw), lambda cb, ti: (0, cb))],
        out_specs=pl.BlockSpec((tr, cw), lambda cb, ti: (ti, cb)),
        scratch_shapes=[pltpu.VMEM((8, cw), F32)],
        compiler_params=_cp(("parallel", "arbitrary")),
        name=name,
    )(x, wb)


def conv_silu_bwd(x, col0, c, wb, dout, name="conv_bwd"):
    t = x.shape[0]
    cw = _pick(c, (CONV_BLOCK, LANE))
    nblk, col0_blk = c // cw, col0 // cw
    assert col0 % cw == 0
    tr = _pick(t, (1408,) + (_rt_wide(t),))
    nt = t // tr
    r8 = tr // 8

    def body(x_ref, xp_ref, w_ref, do_ref, dx_ref, dwb_ref, dnext):
        ti = pl.program_id(1)
        tt = nt - 1 - ti

        @pl.when(ti == 0)
        def _():
            dnext[...] = jnp.zeros_like(dnext)
            dwb_ref[...] = jnp.zeros_like(dwb_ref)

        xv = x_ref[...]
        halo = jnp.where(tt > 0, xp_ref[...], 0.0)
        ext = jnp.concatenate([halo, xv], axis=0)
        pre = _conv_pre(ext, xv, w_ref, tr)
        s = jax.nn.sigmoid(pre)
        dpre = do_ref[...] * (s + pre * s * (1.0 - s))
        dx, sums = _conv_bwd_parts(dpre, dnext[...], xv, ext, w_ref, tr)
        dx_ref[...] = dx.astype(dx_ref.dtype)
        dwb_ref[...] += _rows_block(sums)
        dnext[...] = dpre[:8, :]

    return pl.pallas_call(
        body,
        out_shape=(_sds((t, c), BF), _sds((8, c), F32)),
        grid=(nblk, nt),
        in_specs=[pl.BlockSpec((tr, cw), lambda cb, ti: (nt - 1 - ti, col0_blk + cb)),
                  pl.BlockSpec((8, cw), lambda cb, ti: (jnp.maximum((nt - 1 - ti) * r8 - 1, 0), col0_blk + cb)),
                  pl.BlockSpec((8, cw), lambda cb, ti: (0, cb)),
                  pl.BlockSpec((tr, cw), lambda cb, ti: (nt - 1 - ti, cb))],
        out_specs=(pl.BlockSpec((tr, cw), lambda cb, ti: (nt - 1 - ti, cb)),
                   pl.BlockSpec((8, cw), lambda cb, ti: (0, cb))),
        scratch_shapes=[pltpu.VMEM((8, cw), F32)],
        compiler_params=_cp(("parallel", "arbitrary")),
        name=name,
    )(x, x, wb, dout)


def gated_norm_fwd(y, proj, g, name="gnorm_fwd"):
    t, w = y.shape
    tr = _rt_wide(t)

    def body(y_ref, z_ref, g_ref, o_ref):
        v = y_ref[...] * _silu(z_ref[...])
        r = lax.rsqrt(jnp.mean(v * v, axis=-1, keepdims=True) + EPS)
        o_ref[...] = (v * r * g_ref[...]).astype(o_ref.dtype)

    return pl.pallas_call(
        body,
        out_shape=_sds((t, w), BF),
        grid=(t // tr,),
        in_specs=[pl.BlockSpec((tr, w), lambda i: (i, 0)), pl.BlockSpec((tr, w), lambda i: (i, OFF_Z // w)),
                  pl.BlockSpec((1, w), lambda i: (0, 0))],
        out_specs=pl.BlockSpec((tr, w), lambda i: (i, 0)),
        compiler_params=_cp(("parallel",)),
        name=name,
    )(y, proj, g.reshape(1, w))


def gated_norm_bwd(y, proj, g, dyab, name="gnorm_bwd"):
    t, w = y.shape
    tr = _rt_wide(t)

    def body(y_ref, z_ref, g_ref, do_ref, dy_ref, dz_ref, dg_ref):
        i = pl.program_id(0)
        yv, zv, dov = y_ref[...], z_ref[...], do_ref[...]
        s = jax.nn.sigmoid(zv)
        sz = zv * s
        v = yv * sz
        r = lax.rsqrt(jnp.mean(v * v, axis=-1, keepdims=True) + EPS)
        vh = v * r
        dvg = dov * g_ref[...]
        dv = r * (dvg - vh * jnp.mean(dvg * vh, axis=-1, keepdims=True))
        dy_ref[...] = dv * sz
        dz_ref[...] = (dv * yv * (s + sz * (1.0 - s))).astype(dz_ref.dtype)

        @pl.when(i == 0)
        def _():
            dg_ref[...] = jnp.zeros_like(dg_ref)

        dg_ref[...] += jnp.sum(dov * vh, axis=0, keepdims=True)

    dy, dz, dg = pl.pallas_call(
        body,
        out_shape=(_sds((t, w), F32), _sds((t, w), BF), _sds((1, w), F32)),
        grid=(t // tr,),
        in_specs=[pl.BlockSpec((tr, w), lambda i: (i, 0)), pl.BlockSpec((tr, w), lambda i: (i, OFF_Z // w)),
                  pl.BlockSpec((1, w), lambda i: (0, 0)), pl.BlockSpec((tr, w), lambda i: (i, 0))],
        out_specs=(pl.BlockSpec((tr, w), lambda i: (i, 0)), pl.BlockSpec((tr, w), lambda i: (i, 0)),
                   pl.BlockSpec((1, w), lambda i: (0, 0))),
        compiler_params=_cp(("arbitrary",)),
        name=name,
    )(y, proj, g.reshape(1, w), dyab)
    return dy, dz, dg.reshape(w)


def rope_tables(t):
    inv = ROPE_BASE ** (-jnp.arange(0, MLA_ROPE, 2, dtype=F32) / MLA_ROPE)
    pos = (jnp.arange(t, dtype=F32) - PAD)[:, None]
    ang = pos * inv[None, :]
    cos, sin = jnp.cos(ang), jnp.sin(ang)
    z16 = jnp.zeros((t, 16), F32)
    z32 = jnp.zeros((t, 32), F32)
    c = jnp.concatenate([jnp.ones((t, 64), F32), cos, cos, z32], axis=1)
    s1 = jnp.concatenate([jnp.zeros((t, 64), F32), z16, sin, z32], axis=1)
    s2 = jnp.concatenate([jnp.zeros((t, 64), F32), -sin, z16, z32], axis=1)
    return c, s1, s2


def _rope(x, c, s1, s2):
    return x * c + pltpu.roll(x, 16, 1) * s1 + pltpu.roll(x, LANE - 16, 1) * s2


def _rope_t(d, c, s1, s2):
    return d * c + pltpu.roll(d * s1, LANE - 16, 1) + pltpu.roll(d * s2, 16, 1)


def rope_fwd(q_raw, kv_raw, proj, tabs):
    t = q_raw.shape[0]
    tr = _rt_wide(t)
    hw = MLA_HEADS * LANE

    def body(q_ref, k_ref, v_ref, kr_ref, c_ref, s1_ref, s2_ref, qo_ref, ko_ref, vo_ref):
        c, s1, s2 = c_ref[...], s1_ref[...], s2_ref[...]
        kr = _rope(kr_ref[...], c, s1, s2)
        for h in range(MLA_HEADS):
            sl = slice(h * LANE, (h + 1) * LANE)
            qo_ref[:, sl] = (_rope(q_ref[:, sl], c, s1, s2) * Q_PRESCALE).astype(BF)
            ko_ref[:, sl] = (k_ref[:, sl] + kr).astype(BF)
        vo_ref[...] = v_ref[...].astype(BF)

    tab_spec = pl.BlockSpec((tr, LANE), lambda i: (i, 0))
    return pl.pallas_call(
        body,
        out_shape=(_sds((t, hw), BF), _sds((t, hw), BF), _sds((t, 1024), BF)),
        grid=(t // tr,),
        in_specs=[pl.BlockSpec((tr, hw), lambda i: (i, 0)), pl.BlockSpec((tr, hw), lambda i: (i, 0)),
                  pl.BlockSpec((tr, 1024), lambda i: (i, 2)), pl.BlockSpec((tr, LANE), lambda i: (i, OFF_KR // LANE)),
                  tab_spec, tab_spec, tab_spec],
        out_specs=(pl.BlockSpec((tr, hw), lambda i: (i, 0)), pl.BlockSpec((tr, hw), lambda i: (i, 0)),
                   pl.BlockSpec((tr, 1024), lambda i: (i, 0))),
        compiler_params=_cp(("parallel",)),
        name="rope_fwd",
    )(q_raw, kv_raw, kv_raw, proj, *tabs)


def rope_bwd(dq_cat, dk_cat, tabs):
    t = dq_cat.shape[0]
    tr = _rt_wide(t)
    hw = MLA_HEADS * LANE

    def body(dq_ref, dk_ref, c_ref, s1_ref, s2_ref, dqo_ref, dkr_ref):
        c, s1, s2 = c_ref[...], s1_ref[...], s2_ref[...]
        acc = jnp.zeros((tr, LANE), F32)
        for h in range(MLA_HEADS):
            sl = slice(h * LANE, (h + 1) * LANE)
            dqo_ref[:, sl] = _rope_t(dq_ref[:, sl] * ATT_SCALE, c, s1, s2).astype(BF)
            acc = acc + dk_ref[:, sl]
        lane = lax.broadcasted_iota(jnp.int32, (tr, LANE), 1)
        dkr_ref[...] = jnp.where((lane >= 64) & (lane < 96), _rope_t(acc, c, s1, s2), 0.0)

    tab_spec = pl.BlockSpec((tr, LANE), lambda i: (i, 0))
    return pl.pallas_call(
        body,
        out_shape=(_sds((t, hw), BF), _sds((t, LANE), F32)),
        grid=(t // tr,),
        in_specs=[pl.BlockSpec((tr, hw), lambda i: (i, 0)), pl.BlockSpec((tr, hw), lambda i: (i, 0)),
                  tab_spec, tab_spec, tab_spec],
        out_specs=(pl.BlockSpec((tr, hw), lambda i: (i, 0)), pl.BlockSpec((tr, LANE), lambda i: (i, 0))),
        compiler_params=_cp(("parallel",)),
        name="rope_bwd",
    )(dq_cat, dk_cat, *tabs)


ATT_SCALE = (MLA_NOPE + MLA_ROPE) ** -0.5
LOG2E = math.log2(math.e)
Q_PRESCALE = ATT_SCALE * LOG2E
CARRY_MIDDLE_PAIR = 6
NT_DIMS = (((1,), (1,)), ((), ()))
TN_DIMS = (((0,), (0,)), ((), ()))


def _att_mask(qi, ki, tq, tk):
    qpos = qi * tq + lax.broadcasted_iota(jnp.int32, (tq, tk), 0)
    kpos = ki * tk + lax.broadcasted_iota(jnp.int32, (tq, tk), 1)
    return (kpos <= qpos) & (kpos >= PAD)


def _half_masks(n):
    lane = lax.broadcasted_iota(jnp.int32, (n, LANE), 1)
    return lane < 64, lane >= 64


def _att_tile(t):
    return _pick(t, (384, 256, 128))


def _ds(i, n):
    return pl.ds(i * n, n) if isinstance(i, int) else pl.ds(pl.multiple_of(i * n, n), n)


FWD_PAIRS = 2


def attn_fwd(q_cat, k_cat, v, carry=None):
    t = q_cat.shape[0]
    tq = tk = _att_tile(t)
    nq = t // tq
    npair, nh = FWD_PAIRS, 2 * FWD_PAIRS
    n_grp = MLA_HEADS // nh
    nx = carry.k if carry else 0

    def body(*refs):
        q_ref, k_ref, v_ref = refs[:3]
        o_ref, lse_ref = refs[3 + nx:5 + nx]
        qi = pl.program_id(1)
        if carry:
            start, middle, finish = carry.phases(refs[3:3 + nx], refs[5 + nx:5 + 2 * nx], refs[5 + 2 * nx:])
            grp = pl.program_id(0)
            pl.when((grp == 0) & (qi == 0))(start)
            pl.when((grp == CARRY_MIDDLE_PAIR // npair) & (qi == 0))(middle)
        lo_q, _ = _half_masks(tq)
        halves = _half_masks(tk)

        def step(ki, state, masked):
            m_old, l_old, accs = state[0:nh], state[nh:2 * nh], state[2 * nh:]
            rows = _ds(ki, tk)
            ss = [lax.dot_general(q_ref[:, h * LANE:(h + 1) * LANE], k_ref[rows, h * LANE:(h + 1) * LANE], NT_DIMS,
                                  preferred_element_type=F32) for h in range(nh)]
            if masked:
                valid = _att_mask(qi, ki, tq, tk)
                ss = [jnp.where(valid, s, NEG) for s in ss]
            m_new = [jnp.maximum(m_old[h], jnp.max(ss[h], axis=-1, keepdims=True)) for h in range(nh)]
            ps = [jnp.exp2(ss[h] - m_new[h]) for h in range(nh)]
            alpha = [jnp.exp2(m_old[h] - m_new[h]) for h in range(nh)]
            l_new = [alpha[h] * l_old[h] + jnp.sum(ps[h], axis=-1, keepdims=True) for h in range(nh)]
            new_accs = []
            for pp in range(npair):
                vv = v_ref[rows, pp * LANE:(pp + 1) * LANE]
                pv = [jnp.dot(ps[2 * pp + hh].astype(BF), jnp.where(halves[hh], vv, jnp.zeros_like(vv)),
                              preferred_element_type=F32) for hh in range(2)]
                new_accs.append(accs[pp] * jnp.where(lo_q, alpha[2 * pp], alpha[2 * pp + 1]) + pv[0] + pv[1])
            return tuple(m_new) + tuple(l_new) + tuple(new_accs)

        neg, zero = jnp.full((tq, 1), NEG, F32), jnp.zeros((tq, 1), F32)
        state = step(0, (neg,) * nh + (zero,) * nh + (jnp.zeros((tq, LANE), F32),) * npair, True)
        state = lax.fori_loop(1, qi, lambda ki, st: step(ki, st, False), state)
        state = lax.cond(qi > 0, lambda st: step(qi, st, True), lambda st: st, state)
        for pp in range(npair):
            l = jnp.where(lo_q, state[nh + 2 * pp], state[nh + 2 * pp + 1])
            o_ref[:, pp * LANE:(pp + 1) * LANE] = (state[2 * nh + pp] / l).astype(o_ref.dtype)
            lse_ref[:, pp * LANE:(pp + 1) * LANE] = jnp.where(lo_q, state[2 * pp], state[2 * pp + 1]) + jnp.log2(l)
        if carry:
            pl.when((grp == n_grp - 1) & (qi == nq - 1))(finish)

    outs = pl.pallas_call(
        body,
        out_shape=(_sds((t, 1024), BF), _sds((t, 1024), F32)) + tuple(carry.out_shapes() if carry else ()),
        grid=(n_grp, nq),
        in_specs=[pl.BlockSpec((tq, nh * LANE), lambda g, qi: (qi, g)),
                  pl.BlockSpec((t, nh * LANE), lambda g, qi: (0, g)),
                  pl.BlockSpec((t, npair * LANE), lambda g, qi: (0, g))] + [ANY] * nx,
        out_specs=(pl.BlockSpec((tq, npair * LANE), lambda g, qi: (qi, g)),
                   pl.BlockSpec((tq, npair * LANE), lambda g, qi: (qi, g))) + (ANY,) * nx,
        scratch_shapes=carry.scratch() if carry else [],
        compiler_params=_cp(("arbitrary", "arbitrary") if carry else ("parallel", "parallel")),
        name="attn_fwd_carrying" if carry else "attn_fwd",
    )(q_cat, k_cat, v, *(carry.arrs if carry else ()))
    return outs[0], outs[1], list(outs[2:])


def attn_bwd(q_cat, k_cat, v, o, lse, dyab, carry=None):
    t = q_cat.shape[0]
    tq = tk = _att_tile(t)
    nq = t // tq
    n_pair = MLA_HEADS // 2
    nx = carry.k if carry else 0

    def body(*refs):
        q_ref, k_ref, v_ref, o_ref, lse_ref, do_ref = refs[:6]
        dq_ref, dk_ref, dv_ref = refs[6 + nx:9 + nx]
        ki = pl.program_id(1)
        if carry:
            start, middle, finish = carry.phases(refs[6:6 + nx], refs[9 + nx:9 + 2 * nx], refs[9 + 2 * nx:])
            pair = pl.program_id(0)
            pl.when((pair == 0) & (ki == 0))(start)
            pl.when((pair == CARRY_MIDDLE_PAIR) & (ki == 0))(middle)

        @pl.when(ki == 0)
        def _():
            dq_ref[...] = jnp.zeros_like(dq_ref)

        halves = _half_masks(tq)
        vv = v_ref[...]
        kk = [k_ref[:, hh * LANE:(hh + 1) * LANE] for hh in range(2)]

        def step(qi, acc, masked):
            rows = _ds(qi, tq)
            dov, ov, lse_v = do_ref[rows, :], o_ref[rows, :].astype(F32), lse_ref[rows, :]
            qh = [q_ref[rows, hh * LANE:(hh + 1) * LANE] for hh in range(2)]
            ss = [lax.dot_general(qh[hh], kk[hh], NT_DIMS, preferred_element_type=F32) for hh in range(2)]
            if masked:
                valid = _att_mask(qi, ki, tq, tk)
                ss = [jnp.where(valid, s, NEG) for s in ss]
            ps = [jnp.exp2(ss[hh] - lse_v[:, 64 * hh:64 * hh + 1]) for hh in range(2)]
            dom = [jnp.where(halves[hh], dov, 0.0) for hh in range(2)]
            delta = [jnp.sum(dom[hh] * ov, axis=-1, keepdims=True) for hh in range(2)]
            dom = [d.astype(BF) for d in dom]
            dp = [lax.dot_general(dom[hh], vv, NT_DIMS, preferred_element_type=F32) for hh in range(2)]
            ds = [(ps[hh] * (dp[hh] - delta[hh])).astype(BF) for hh in range(2)]
            pb = [p.astype(BF) for p in ps]
            dv = (acc[2] + lax.dot_general(pb[0], dom[0], TN_DIMS, preferred_element_type=F32)
                  + lax.dot_general(pb[1], dom[1], TN_DIMS, preferred_element_type=F32))
            dk = [acc[hh] + lax.dot_general(ds[hh], qh[hh], TN_DIMS, preferred_element_type=F32) for hh in range(2)]
            for hh in range(2):
                dq_ref[rows, hh * LANE:(hh + 1) * LANE] += jnp.dot(ds[hh], kk[hh], preferred_element_type=F32)
            return dk[0], dk[1], dv

        zero = jnp.zeros((tk, LANE), F32)
        acc = step(ki, (zero, zero, zero), True)
        acc = lax.fori_loop(ki + 1, jnp.where(ki == 0, nq, ki + 1), lambda qi, a: step(qi, a, True), acc)
        acc = lax.fori_loop(ki + 1, jnp.where(ki == 0, ki + 1, nq), lambda qi, a: step(qi, a, False), acc)
        dk_ref[:, 0:LANE] = acc[0] * (1.0 / LOG2E)
        dk_ref[:, LANE:2 * LANE] = acc[1] * (1.0 / LOG2E)
        dv_ref[...] = acc[2].astype(dv_ref.dtype)
        if carry:
            pl.when((pair == n_pair - 1) & (ki == nq - 1))(finish)

    full = lambda w, off=0: pl.BlockSpec((t, w), lambda p, ki: (0, p + off))
    blk = lambda w: pl.BlockSpec((tk, w), lambda p, ki: (ki, p))
    outs = pl.pallas_call(
        body,
        out_shape=(_sds((t, 2048), F32), _sds((t, 2048), F32), _sds((t, 1024), BF))
        + tuple(carry.out_shapes() if carry else ()),
        grid=(n_pair, nq),
        in_specs=[full(2 * LANE), blk(2 * LANE), blk(LANE), full(LANE), full(LANE), full(LANE, 8)] + [ANY] * nx,
        out_specs=(full(2 * LANE), blk(2 * LANE), blk(LANE)) + (ANY,) * nx,
        scratch_shapes=carry.scratch() if carry else [],
        compiler_params=_cp(("arbitrary", "arbitrary") if carry else ("parallel", "arbitrary")),
        name="attn_bwd_carrying" if carry else "attn_bwd",
    )(q_cat, k_cat, v, o, lse, dyab, *(carry.arrs if carry else ()))
    return outs[0], outs[1], outs[2], list(outs[3:])


N_PAIR = SSD_HEADS // 2


def _hdot(a, b):
    return jnp.dot(a, b, precision=HI, preferred_element_type=F32)


def _ssd_chunk(xs, bg, cg, dtraw, hin, dt_bias, a_log, dskip, rowmask):
    ln = CHUNK
    causal = lax.broadcasted_iota(jnp.int32, (ln, ln), 0) >= lax.broadcasted_iota(jnp.int32, (ln, ln), 1)
    ltri = causal.astype(F32)
    lane = lax.broadcasted_iota(jnp.int32, (ln, LANE), 1)
    halves = (lane < 64, lane >= 64)
    low_row = lax.broadcasted_iota(jnp.int32, (1, LANE), 1) < 64
    head_lane = lax.broadcasted_iota(jnp.int32, (1, SSD_HEADS), 1)
    head_row = lax.broadcasted_iota(jnp.int32, (SSD_HEADS, 1), 0)

    def col(a, h):
        return jnp.sum(jnp.where(head_lane == h, a, 0.0), axis=1, keepdims=True)

    dt = _softplus(dtraw + dt_bias) * rowmask
    da = dt * (-jnp.exp(a_log))
    acs = _hdot(ltri, da)
    acs_t = lax.dot_general(da, ltri, (((0,), (1,)), ((), ())), precision=HI, preferred_element_type=F32)
    tot = jnp.sum(da, axis=0, keepdims=True)
    bm = [b * rowmask for b in bg]
    cm = [c * rowmask for c in cg]
    cb = [lax.dot_general(cm[g].astype(BF), bm[g].astype(BF), NT_DIMS, preferred_element_type=F32) for g in range(2)]
    ys, hout = [], []
    for p in range(N_PAIR):
        g = p // (N_PAIR // 2)
        h0, h1 = 2 * p, 2 * p + 1
        xdt = xs[p] * jnp.where(halves[0], col(dt, h0), col(dt, h1))
        a_cols = [col(acs, h0), col(acs, h1)]
        tot_cols = [col(tot, h0), col(tot, h1)]
        y = jnp.zeros((ln, LANE), F32)
        snew = jnp.zeros((ln, LANE), F32)
        for hh in range(2):
            a_row = jnp.sum(jnp.where(head_row == h0 + hh, acs_t, 0.0), axis=0, keepdims=True)
            dec = jnp.exp(jnp.where(causal, a_cols[hh] - a_row, NEG))
            xm = jnp.where(halves[hh], xdt, 0.0).astype(BF)
            y = y + jnp.dot((cb[g] * dec).astype(BF), xm, preferred_element_type=F32)
            bd = bm[g] * jnp.exp(tot_cols[hh] - a_cols[hh])
            snew = snew + lax.dot_general(bd.astype(BF), xm, TN_DIMS, preferred_element_type=F32)
        y_off = (jnp.dot(cm[g].astype(BF), hin[p].astype(BF), preferred_element_type=F32)
                 * jnp.where(halves[0], jnp.exp(a_cols[0]), jnp.exp(a_cols[1])))
        ys.append(y + y_off + jnp.where(low_row, col(dskip, h0), col(dskip, h1)) * xs[p])
        hout.append(jnp.where(low_row, jnp.exp(tot_cols[0]), jnp.exp(tot_cols[1])) * hin[p] + snew)
    return ys, hout


def _ssd_load(x_ref, dt_ref):
    xs = [x_ref[:, p * LANE:(p + 1) * LANE] for p in range(N_PAIR)]
    bg = [x_ref[:, SSD_D_INNER + g * LANE:SSD_D_INNER + (g + 1) * LANE] for g in range(2)]
    cg = [x_ref[:, SSD_D_INNER + (2 + g) * LANE:SSD_D_INNER + (3 + g) * LANE] for g in range(2)]
    return xs, bg, cg, dt_ref[:, 0:SSD_HEADS]


def _chunk_rowmask(c):
    return ((c * CHUNK + lax.broadcasted_iota(jnp.int32, (CHUNK, 1), 0)) >= PAD).astype(F32)


def ssd_fwd(xbc_c, proj, dt_bias, a_log, dskip):
    t = xbc_c.shape[0]
    nc = t // CHUNK

    def body(x_ref, dt_ref, dtb_ref, al_ref, d_ref, y_ref, hs_ref, h_s):
        c = pl.program_id(0)

        @pl.when(c == 0)
        def _():
            h_s[...] = jnp.zeros_like(h_s)

        xs, bg, cg, dtraw = _ssd_load(x_ref, dt_ref)
        hin = [h_s[p] for p in range(N_PAIR)]
        hs_ref[0] = h_s[...]
        ys, hout = _ssd_chunk(xs, bg, cg, dtraw, hin, dtb_ref[...], al_ref[...], d_ref[...], _chunk_rowmask(c))
        for p in range(N_PAIR):
            y_ref[:, p * LANE:(p + 1) * LANE] = ys[p]
            h_s[p] = hout[p]

    par = pl.BlockSpec((1, SSD_HEADS), lambda c: (0, 0))
    return pl.pallas_call(
        body,
        out_shape=(_sds((t, SSD_D_INNER), F32), _sds((nc, N_PAIR, CHUNK, LANE), F32)),
        grid=(nc,),
        in_specs=[pl.BlockSpec((CHUNK, SSD_CONV_CH), lambda c: (c, 0)),
                  pl.BlockSpec((CHUNK, LANE), lambda c: (c, OFF_DT // LANE)), par, par, par],
        out_specs=(pl.BlockSpec((CHUNK, SSD_D_INNER), lambda c: (c, 0)),
                   pl.BlockSpec((1, N_PAIR, CHUNK, LANE), lambda c: (c, 0, 0, 0))),
        scratch_shapes=[pltpu.VMEM((N_PAIR, CHUNK, LANE), F32)],
        compiler_params=_cp(("arbitrary",)),
        name="ssd_fwd",
    )(xbc_c, proj, dt_bias.reshape(1, -1), a_log.reshape(1, -1), dskip.reshape(1, -1))


def ssd_bwd(xbc_c, proj, dt_bias, a_log, dskip, hs, dy):
    t = xbc_c.shape[0]
    nc = t // CHUNK

    def body(x_ref, dt_ref, dtb_ref, al_ref, d_ref, hs_ref, dy_ref, dx_ref, ddt_ref, dpar_ref, dh_s):
        ci = pl.program_id(0)
        c = nc - 1 - ci

        @pl.when(ci == 0)
        def _():
            dh_s[...] = jnp.zeros_like(dh_s)
            dpar_ref[...] = jnp.zeros_like(dpar_ref)

        xs, bg, cg, dtraw = _ssd_load(x_ref, dt_ref)
        hin = [hs_ref[0, p] for p in range(N_PAIR)]
        rowmask = _chunk_rowmask(c)
        fn = lambda xs_, bg_, cg_, dtraw_, hin_, dtb_, al_, d_: _ssd_chunk(xs_, bg_, cg_, dtraw_, hin_, dtb_, al_, d_, rowmask)
        _, vjp = jax.vjp(fn, xs, bg, cg, dtraw, hin, dtb_ref[...], al_ref[...], d_ref[...])
        dys = [dy_ref[:, p * LANE:(p + 1) * LANE] for p in range(N_PAIR)]
        dhs = [dh_s[p] for p in range(N_PAIR)]
        dxs, dbg, dcg, ddtraw, dhin, ddtb, dal, dd = vjp((dys, dhs))
        for p in range(N_PAIR):
            dx_ref[:, p * LANE:(p + 1) * LANE] = dxs[p]
            dh_s[p] = dhin[p]
        for g in range(2):
            dx_ref[:, SSD_D_INNER + g * LANE:SSD_D_INNER + (g + 1) * LANE] = dbg[g]
            dx_ref[:, SSD_D_INNER + (2 + g) * LANE:SSD_D_INNER + (3 + g) * LANE] = dcg[g]
        ddt_ref[...] = jnp.zeros_like(ddt_ref)
        ddt_ref[:, 0:SSD_HEADS] = ddtraw
        dpar_ref[0:1, 0:SSD_HEADS] += ddtb
        dpar_ref[1:2, 0:SSD_HEADS] += dal
        dpar_ref[2:3, 0:SSD_HEADS] += dd

    par = pl.BlockSpec((1, SSD_HEADS), lambda ci: (0, 0))
    return pl.pallas_call(
        body,
        out_shape=(_sds((t, SSD_CONV_CH), F32), _sds((t, LANE), F32), _sds((8, LANE), F32)),
        grid=(nc,),
        in_specs=[pl.BlockSpec((CHUNK, SSD_CONV_CH), lambda ci: (nc - 1 - ci, 0)),
                  pl.BlockSpec((CHUNK, LANE), lambda ci: (nc - 1 - ci, OFF_DT // LANE)), par, par, par,
                  pl.BlockSpec((1, N_PAIR, CHUNK, LANE), lambda ci: (nc - 1 - ci, 0, 0, 0)),
                  pl.BlockSpec((CHUNK, SSD_D_INNER), lambda ci: (nc - 1 - ci, 0))],
        out_specs=(pl.BlockSpec((CHUNK, SSD_CONV_CH), lambda ci: (nc - 1 - ci, 0)),
                   pl.BlockSpec((CHUNK, LANE), lambda ci: (nc - 1 - ci, 0)),
                   pl.BlockSpec((8, LANE), lambda ci: (0, 0))),
        scratch_shapes=[pltpu.VMEM((N_PAIR, CHUNK, LANE), F32)],
        compiler_params=_cp(("arbitrary",)),
        name="ssd_bwd",
    )(xbc_c, proj, dt_bias.reshape(1, -1), a_log.reshape(1, -1), dskip.reshape(1, -1), hs, dy)


def _neg_expm1(y):
    series = -(y * (1.0 + y * (0.5 + y * (1.0 / 6.0 + y * (1.0 / 24.0 + y * (1.0 / 120.0))))))
    return jnp.where(y > -0.1, series, 1.0 - jnp.exp(y))


def _rg_pw(xr, wa, ba, wi, bi, lam, rowmask):
    xb = xr.astype(BF)
    r = jax.nn.sigmoid(jnp.dot(xb, wa.astype(BF), preferred_element_type=F32) + ba)
    i = jax.nn.sigmoid(jnp.dot(xb, wi.astype(BF), preferred_element_type=F32) + bi)
    log_a = -LRU_C * r * _softplus(-lam)
    a = jnp.exp(log_a)
    u = jnp.sqrt(_neg_expm1(2.0 * log_a)) * (i * xr) * rowmask
    return a, u


def _gelu_grad(x):
    c = math.sqrt(2.0 / math.pi)
    th = jnp.tanh(c * (x + 0.044715 * (x * x * x)))
    return 0.5 * (1.0 + th) + 0.5 * x * (1.0 - th * th) * c * (1.0 + 3.0 * 0.044715 * x * x)


def _scan_fwd(a, u):
    n = a.shape[0]
    row = lax.broadcasted_iota(jnp.int32, a.shape, 0)
    s = 1
    while s < n:
        a_s = jnp.where(row >= s, pltpu.roll(a, s, 0), 1.0)
        u_s = jnp.where(row >= s, pltpu.roll(u, s, 0), 0.0)
        u = u + a * u_s
        a = a * a_s
        s *= 2
    return a, u


def _scan_bwd(b, d):
    n = b.shape[0]
    row = lax.broadcasted_iota(jnp.int32, b.shape, 0)
    s = 1
    while s < n:
        b_s = jnp.where(row < n - s, pltpu.roll(b, n - s, 0), 1.0)
        d_s = jnp.where(row < n - s, pltpu.roll(d, n - s, 0), 0.0)
        d = d + b * d_s
        b = b * b_s
        s *= 2
    return d


def rg_fwd(xr_pre, gate_pre, rgp, w_a, w_i):
    t = xr_pre.shape[0]
    tr = _rt(t)

    def body(x_ref, g_ref, p_ref, wa_ref, wi_ref, hg_ref, hs_ref, prev, hcar):
        ti = pl.program_id(1)

        @pl.when(ti == 0)
        def _():
            prev[...] = jnp.zeros_like(prev)
            hcar[...] = jnp.zeros_like(hcar)

        xv = x_ref[...]
        ext = jnp.concatenate([prev[...], xv], axis=0)
        xr = _conv_pre(ext, xv, p_ref, tr)
        rowmask = _row_mask(ti, tr, (tr, 1)).astype(F32)
        a, u = _rg_pw(xr, wa_ref[0], p_ref[5:6, :], wi_ref[0], p_ref[6:7, :], p_ref[7:8, :], rowmask)
        a_cum, h_loc = _scan_fwd(a, u)
        hs = h_loc + a_cum * hcar[0:1, :]
        hs_ref[...] = hs
        hg_ref[...] = (hs * _gelu(g_ref[...])).astype(hg_ref.dtype)
        hcar[...] = jnp.broadcast_to(hs[tr - 1:tr, :], (8, LANE))
        prev[...] = xv[tr - 8:, :]

    return pl.pallas_call(
        body,
        out_shape=(_sds((t, LRU_WIDTH), BF), _sds((t, LRU_WIDTH), F32)),
        grid=(LRU_BLOCKS, t // tr),
        in_specs=[pl.BlockSpec((tr, LANE), lambda n, ti: (ti, n)),
                  pl.BlockSpec((tr, LANE), lambda n, ti: (ti, n)),
                  pl.BlockSpec((8, LANE), lambda n, ti: (0, n)),
                  pl.BlockSpec((1, LANE, LANE), lambda n, ti: (n, 0, 0)),
                  pl.BlockSpec((1, LANE, LANE), lambda n, ti: (n, 0, 0))],
        out_specs=(pl.BlockSpec((tr, LANE), lambda n, ti: (ti, n)), pl.BlockSpec((tr, LANE), lambda n, ti: (ti, n))),
        scratch_shapes=[pltpu.VMEM((8, LANE), F32), pltpu.VMEM((8, LANE), F32)],
        compiler_params=_cp(("parallel", "arbitrary")),
        name="rg_fwd",
    )(xr_pre, gate_pre, rgp, w_a, w_i)


def rg_bwd(xr_pre, gate_pre, rgp, w_a, w_i, hs, dhg):
    t = xr_pre.shape[0]
    tr = _rt(t)
    nt = t // tr
    r8 = tr // 8

    def body(x_ref, xp_ref, g_ref, p_ref, wa_ref, wi_ref, hs_ref, hp_ref, dhg_ref,
             dx_ref, dg_ref, dp_ref, dwa_ref, dwi_ref, gcar, dnext):
        ti = pl.program_id(1)
        tt = nt - 1 - ti

        @pl.when(ti == 0)
        def _():
            gcar[...] = jnp.zeros_like(gcar)
            dnext[...] = jnp.zeros_like(dnext)
            dp_ref[...] = jnp.zeros_like(dp_ref)
            dwa_ref[...] = jnp.zeros_like(dwa_ref)
            dwi_ref[...] = jnp.zeros_like(dwi_ref)

        xv = x_ref[...]
        halo = jnp.where(tt > 0, xp_ref[...], 0.0)
        ext = jnp.concatenate([halo, xv], axis=0)
        xr = _conv_pre(ext, xv, p_ref, tr)
        rowmask = _row_mask(tt, tr, (tr, 1)).astype(F32)
        fn = lambda xr_, wa_, ba_, wi_, bi_, lam_: _rg_pw(xr_, wa_, ba_, wi_, bi_, lam_, rowmask)
        (a, _), vjp = jax.vjp(fn, xr, wa_ref[0], p_ref[5:6, :], wi_ref[0], p_ref[6:7, :], p_ref[7:8, :])
        gpre = g_ref[...]
        hsv = hs_ref[...]
        dhg_v = dhg_ref[...]
        dg_ref[...] = (dhg_v * hsv * _gelu_grad(gpre)).astype(dg_ref.dtype)
        row = lax.broadcasted_iota(jnp.int32, (tr, LANE), 0)
        d = dhg_v * _gelu(gpre) + jnp.where(row == tr - 1, gcar[0:1, :], 0.0)
        b = jnp.where(row < tr - 1, pltpu.roll(a, tr - 1, 0), 0.0)
        g = _scan_bwd(b, d)
        gcar[...] = jnp.broadcast_to(a[0:1, :] * g[0:1, :], (8, LANE))
        hlast = jnp.where(tt > 0, hp_ref[7:8, :], 0.0)
        hprev = jnp.where(row == 0, hlast, pltpu.roll(hsv, 1, 0))
        dxr, dwa, dba, dwi, dbi, dlam = vjp((g * hprev, g))
        dx, sums = _conv_bwd_parts(dxr, dnext[...], xv, ext, p_ref, tr)
        dx_ref[...] = dx.astype(dx_ref.dtype)
        dnext[...] = dxr[:8, :]
        dp_ref[...] += _rows_block(sums + [dba, dbi, dlam])
        dwa_ref[0] += dwa
        dwi_ref[0] += dwi

    tile = lambda off=0: pl.BlockSpec((tr, LANE), lambda n, ti: (nt - 1 - ti, off + n))
    halo = lambda off=0: pl.BlockSpec((8, LANE), lambda n, ti: (jnp.maximum((nt - 1 - ti) * r8 - 1, 0), off + n))
    par = pl.BlockSpec((8, LANE), lambda n, ti: (0, n))
    wspec = pl.BlockSpec((1, LANE, LANE), lambda n, ti: (n, 0, 0))
    return pl.pallas_call(
        body,
        out_shape=(_sds((t, LRU_WIDTH), BF), _sds((t, LRU_WIDTH), BF), _sds((8, LRU_WIDTH), F32),
                   _sds((LRU_BLOCKS, LANE, LANE), F32), _sds((LRU_BLOCKS, LANE, LANE), F32)),
        grid=(LRU_BLOCKS, nt),
        in_specs=[tile(), halo(), tile(), par, wspec, wspec, tile(), halo(), tile()],
        out_specs=(tile(), tile(), par, wspec, wspec),
        scratch_shapes=[pltpu.VMEM((8, LANE), F32), pltpu.VMEM((8, LANE), F32)],
        compiler_params=_cp(("parallel", "arbitrary")),
        name="rg_bwd",
    )(xr_pre, xr_pre, gate_pre, rgp, w_a, w_i, hs, hs, dhg)


PACK_W = 1024
MESH_ID = pl.DeviceIdType.MESH
ANY = pl.BlockSpec(memory_space=pl.ANY)


def _my_place():
    x, y, c = lax.axis_index("x"), lax.axis_index("y"), lax.axis_index("c")
    return x, y, c


def _lin(px, py, pc):
    return 4 * px + 2 * py + pc


class Exchange:
    def __init__(self, kind, arrs):
        self.kind, self.arrs, self.k = kind, list(arrs), len(arrs)

    def out_shapes(self):
        if self.kind == "gather":
            return [_sds((N_DEV,) + a.shape, a.dtype) for a in self.arrs]
        return [_sds(a.shape, a.dtype) for a in self.arrs]

    def scratch(self):
        k = self.k
        return [pltpu.SemaphoreType.DMA((k, 7)), pltpu.SemaphoreType.DMA((k, 7)), pltpu.SemaphoreType.DMA((k,))]

    def phases(self, ins, outs, sems):
        return (self._gather if self.kind == "gather" else self._scatter)(ins, outs, *sems)

    def _gather(self, ins, outs, send_sems, recv_sems, local_sems):
        k = self.k
        x, y, c = _my_place()
        me, sibling = (x, y, c), (x, y, 1 - c)
        chips = [(1 - x, y), (x, 1 - y), (1 - x, 1 - y)]

        def copy(a, sem, block, to, from_input=False):
            slab = outs[a].at[_lin(*block)]
            return pltpu.make_async_remote_copy(
                src_ref=ins[a] if from_input else slab, dst_ref=slab,
                send_sem=send_sems.at[a, sem], recv_sem=recv_sems.at[a, sem],
                device_id=to, device_id_type=MESH_ID)

        def mine():
            return [pltpu.make_async_copy(ins[a], outs[a].at[_lin(*me)], local_sems.at[a]) for a in range(k)]

        def first():
            out = []
            for a in range(k):
                out.append(copy(a, 0, me, sibling, True))
                out += [copy(a, 1 + j, me, (*chip, c), True) for j, chip in enumerate(chips)]
            return out

        def passed():
            return [copy(a, 4 + j, (*chip, c), sibling) for j, chip in enumerate(chips) for a in range(k)]

        def start():
            for cp in mine() + first():
                cp.start()

        def middle():
            onward = passed()
            for j, chip in enumerate(chips):
                for a in range(k):
                    copy(a, 1 + j, (*chip, c), me).wait_recv()
                    onward[j * k + a].start()

        def finish():
            for a in range(k):
                copy(a, 0, sibling, me).wait_recv()
                for j, chip in enumerate(chips):
                    copy(a, 4 + j, (*chip, 1 - c), me).wait_recv()
            for cp in first() + passed():
                cp.wait_send()
            for cp in mine():
                cp.wait()

        return start, middle, finish

    def _scatter(self, ins, outs, send_sems, recv_sems, local_sems):
        k = self.k
        x, y, c = _my_place()
        me = _lin(x, y, c)
        peers = [((1 - x) if r & 4 else x, (1 - y) if r & 2 else y, (1 - c) if r & 1 else c) for r in range(1, N_DEV)]

        def copy(a, r, src_slab, dst_slab, to):
            return pltpu.make_async_remote_copy(
                src_ref=ins[a].at[src_slab], dst_ref=outs[a].at[dst_slab],
                send_sem=send_sems.at[a, r], recv_sem=recv_sems.at[a, r],
                device_id=to, device_id_type=MESH_ID)

        def mine():
            return [pltpu.make_async_copy(ins[a].at[me], outs[a].at[me], local_sems.at[a]) for a in range(k)]

        def sends():
            return [copy(a, r, _lin(*peer), me, peer) for r, peer in enumerate(peers) for a in range(k)]

        def start():
            for cp in mine() + sends():
                cp.start()

        def middle():
            pass

        def finish():
            for r, peer in enumerate(peers):
                for a in range(k):
                    copy(a, r, me, _lin(*peer), peer).wait_recv()
            for cp in sends():
                cp.wait_send()
            for cp in mine():
                cp.wait()

        return start, middle, finish

    def run(self, name):
        k = self.k

        def body(*refs):
            start, middle, finish = self.phases(refs[:k], refs[k:2 * k], refs[2 * k:])
            start()
            middle()
            finish()

        return pl.pallas_call(
            body,
            out_shape=tuple(self.out_shapes()),
            in_specs=[ANY] * k,
            out_specs=tuple(ANY for _ in range(k)),
            scratch_shapes=self.scratch(),
            name=name,
        )(*self.arrs)


def all_gather(arrs, name):
    return Exchange("gather", arrs).run(name)


def all_to_all(arrs, name):
    return Exchange("scatter", arrs).run(name)


def slab_sum(a, name):
    _, r, w = a.shape
    tr = _pick(r, (256, 128, 64, 32, 16, 8))

    def body(a_ref, o_ref):
        acc = a_ref[0].astype(F32)
        for d in range(1, N_DEV):
            acc = acc + a_ref[d].astype(F32)
        o_ref[...] = acc

    return pl.pallas_call(
        body,
        out_shape=_sds((r, w), F32),
        grid=(r // tr,),
        in_specs=[pl.BlockSpec((N_DEV, tr, w), lambda i: (0, i, 0))],
        out_specs=pl.BlockSpec((tr, w), lambda i: (i, 0)),
        compiler_params=_cp(("parallel",)),
        name=name,
    )(a)


def _adam_update(w, g, m, v):
    nm = ADAM_B1 * m + (1.0 - ADAM_B1) * g
    nv = ADAM_B2 * v + (1.0 - ADAM_B2) * (g * g)
    m_hat = nm / (1.0 - ADAM_B1 ** ADAM_STEP)
    v_hat = nv / (1.0 - ADAM_B2 ** ADAM_STEP)
    return -ADAM_LR * (m_hat / (jnp.sqrt(v_hat) + ADAM_EPS) + ADAM_WD * w), nm, nv


def adamw_blocks(w, m, v, parts, name):
    nl, r, c = w.shape
    tr = next(t for t in (256, 160, 128, 64, 32, 16) if r % t == 0 and N_DEV * t * c * 2 <= 2 * 1024 * 1024)

    def body(w_ref, m_ref, v_ref, *rest):
        part_refs, (g_ref, d_ref, nm_ref, nv_ref) = rest[:nl], rest[nl:]
        layer = pl.program_id(0)
        for idx in range(nl):
            @pl.when(layer == idx)
            def _(idx=idx):
                g = part_refs[idx][0].astype(F32)
                for dev in range(1, N_DEV):
                    g = g + part_refs[idx][dev].astype(F32)
                g_ref[...] = g
                d_ref[...], nm_ref[...], nv_ref[...] = _adam_update(w_ref[...], g, m_ref[...], v_ref[...])

    spec = pl.BlockSpec((None, tr, c), lambda l, i: (l, i, 0))
    part_spec = lambda idx: pl.BlockSpec((N_DEV, tr, c), lambda l, i: (0, jnp.where(l == idx, i, 0), 0))
    return pl.pallas_call(
        body,
        out_shape=tuple(_sds((nl, r, c), F32) for _ in range(4)),
        grid=(nl, r // tr),
        in_specs=[spec] * 3 + [part_spec(idx) for idx in range(nl)],
        out_specs=(spec,) * 4,
        compiler_params=_cp(("arbitrary", "arbitrary")),
        name=name,
    )(w, m, v, *parts)


def adamw(w, g, m, v, name):
    r, c = w.shape
    tr = _pick(r, (256, 160, 128, 64, 32, 16, 8))

    def body(w_ref, g_ref, m_ref, v_ref, d_ref, nm_ref, nv_ref):
        d_ref[...], nm_ref[...], nv_ref[...] = _adam_update(w_ref[...], g_ref[...], m_ref[...], v_ref[...])

    spec = pl.BlockSpec((tr, c), lambda i: (i, 0))
    return pl.pallas_call(
        body,
        out_shape=tuple(_sds((r, c), F32) for _ in range(3)),
        grid=(r // tr,),
        in_specs=[spec] * 4,
        out_specs=(spec, spec, spec),
        compiler_params=_cp(("parallel",)),
        name=name,
    )(w, g, m, v)


def _relu2_epi(acc):
    r = jnp.maximum(acc, 0.0)
    return r * r, r


def _drelu2_epi(acc, r):
    return (acc * (2.0 * r.astype(F32)),)


def mlp_fwd(h, g_pre, g_post, w_up, w_down, hn=None, g_next=None):
    if hn is None:
        hn = norm_fwd(h, g_pre, BF, name="mlp_norm")
    u, r = matmul(hn, w_up, "nn", (BF, BF), epi=_relu2_epi, name="mlp_up")
    d = matmul(u, w_down, "nn", name="mlp_down")
    h2, hn_next = resadd_fwd(h, d, g_post, g_next, name="mlp_res")
    return h2, (h, hn, u, r, d), hn_next


def mlp_bwd(res, dh2, g_pre, g_post, w_up, w_down, post=None, then=None):
    h, hn, u, r, d = res
    dd, dg_post = post if post is not None else norm_bwd(d, g_post, dh2, mask_pad=True, out_dtype=BF, name="mlp_post_bwd")
    dw_down = matmul(u, dd, "tn", (BF,), name="mlp_dwdown").reshape(w_down.g8.shape)
    dp = matmul(dd, w_down, "nt", (BF,), epi=_drelu2_epi, extras=(r,), name="mlp_du")
    dw_up = matmul(hn, dp, "tn", (BF,), out_blocks=True, name="mlp_dwup")
    dhn = matmul(dp, w_up, "nt", name="mlp_dhn")
    dh, dg_pre, *below = norm_bwd(h, g_pre, dhn, dres=dh2, then=then, name="mlp_pre_bwd")
    return dh, dict(mlp_pre_g=dg_pre, mlp_post_g=dg_post, w_up=dw_up, w_down=dw_down), (tuple(below) or None)


def rg_layer_fwd(h, g_pre, g_post, w_x, w_y, rgp, w_a, w_i, w_out, hn=None, g_next=None):
    if hn is None:
        hn = norm_fwd(h, g_pre, BF, name="rg_norm")
    xr = matmul(hn, w_x, "nn", name="rg_in_x")
    gp = matmul(hn, w_y, "nn", name="rg_in_y")
    hg, hs = rg_fwd(xr, gp, rgp, w_a, w_i)
    m = matmul(hg, w_out, "nn", name="rg_out")
    h2, hn_next = resadd_fwd(h, m, g_post, g_next, name="rg_res")
    return h2, (h, hn, xr, gp, hg, hs, m), hn_next


def rg_layer_bwd(res, dh2, g_pre, g_post, w_x, w_y, rgp, w_a, w_i, w_out, post=None, then=None):
    h, hn, xr, gp, hg, hs, m = res
    dm, dg_post = post if post is not None else norm_bwd(m, g_post, dh2, mask_pad=True, out_dtype=BF, name="rg_post_bwd")
    dw_out = matmul(hg, dm, "tn", name="rg_dwout")
    dhg = matmul(dm, w_out, "nt", name="rg_dhg")
    dxr, dgp, drgp, dwa, dwi = rg_bwd(xr, gp, rgp, w_a, w_i, hs, dhg)
    dw_x = matmul(hn, dxr, "tn", name="rg_dwx")
    dw_y = matmul(hn, dgp, "tn", name="rg_dwy")
    dhn = matmul([dxr, dgp], [w_x, w_y], "nt", name="rg_dhn")
    dh, dg_pre, *below = norm_bwd(h, g_pre, dhn, dres=dh2, then=then, name="rg_pre_bwd")
    grads = dict(mix_pre_g=dg_pre, mix_post_g=dg_post, rg_w_x=dw_x, rg_w_y=dw_y,
                 rg_conv_w=drgp[0:4], rg_conv_b=drgp[4], rg_b_a=drgp[5], rg_b_i=drgp[6], rg_lambda=drgp[7],
                 rg_w_a=dwa, rg_w_i=dwi, rg_w_out=dw_out)
    return dh, grads, (tuple(below) or None)


def sm_layer_fwd(h, g_pre, g_post, w_in_p, convp, dt_bias, a_log, dskip, ssd_g, q_g, w_q_p, kv_g, w_kv_p, w_out, tabs,
                 carry=None, on_carried=None, hn=None, g_next=None):
    if hn is None:
        hn = norm_fwd(h, g_pre, BF, name="sm_norm")
    proj = matmul(hn, w_in_p, "nn", name="sm_in")
    xbc_c = conv_silu_fwd(proj, OFF_XBC, SSD_CONV_CH, convp, name="ssd_conv")
    y, hst = ssd_fwd(xbc_c, proj, dt_bias, a_log, dskip)
    y_ssd = gated_norm_fwd(y, proj, ssd_g)
    cqn = norm_fwd(proj, q_g, BF, col_blk=OFF_CQ // MLA_Q_RANK, width=MLA_Q_RANK, name="q_norm")
    q_raw = matmul(cqn, w_q_p, "nn", name="q_up")
    ckvn = norm_fwd(proj, kv_g, BF, col_blk=OFF_CKV // MLA_KV_RANK, width=MLA_KV_RANK, name="kv_norm")
    kv_raw = matmul(ckvn, w_kv_p, "nn", name="kv_up")
    q_cat, k_cat, v = rope_fwd(q_raw, kv_raw, proj, tabs)
    o, lse, carried = attn_fwd(q_cat, k_cat, v, carry)
    if on_carried is not None:
        on_carried(carried)
    w_out = w_out()
    half = w_out.shape[0] // 2
    m = matmul([y_ssd, o], [KBlock(w_out, half, 0), KBlock(w_out, half, 1)], "nn", name="sm_out")
    res = (h, hn, proj, xbc_c, y, hst, cqn, ckvn, q_cat, k_cat, v, o, lse, y_ssd, m)
    h2, hn_next = resadd_fwd(h, m, g_post, g_next, name="sm_res")
    return h2, res, hn_next


def sm_layer_bwd(res, dh2, g_pre, g_post, w_in_p, convp, dt_bias, a_log, dskip, ssd_g, q_g, w_q_p, kv_g, w_kv_p, w_out, tabs,
                 carry=None, post=None, then=None):
    h, hn, proj, xbc_c, y, hst, cqn, ckvn, q_cat, k_cat, v, o, lse, y_ssd, m = res
    w_out = w_out()
    dm, dg_post = post if post is not None else norm_bwd(m, g_post, dh2, mask_pad=True, out_dtype=BF, name="sm_post_bwd")
    dw_out = jnp.concatenate([matmul(y_ssd, dm, "tn", name="sm_dwout_ssd"), matmul(o, dm, "tn", name="sm_dwout_att")], axis=0)
    dyab = matmul(dm, w_out, "nt", name="sm_dyab")
    dq_cat, dk_cat, dv, carried = attn_bwd(q_cat, k_cat, v, o, lse, dyab, carry(dw_out) if carry is not None else None)
    dq_raw, dkr = rope_bwd(dq_cat, dk_cat, tabs)
    kw = MLA_HEADS * LANE
    dw_kv_p = jnp.concatenate([matmul(ckvn, dk_cat, "tn", name="kv_dw_k"), matmul(ckvn, dv, "tn", name="kv_dw_v")], axis=1)
    dckvn = matmul([dk_cat, dv], [KBlock(w_kv_p, kw, 0), KBlock(w_kv_p, kw // 2, 2)], "nt", name="kv_dx")
    dckv, dg_kv = norm_bwd(proj, kv_g, dckvn, out_dtype=BF, col_blk=OFF_CKV // MLA_KV_RANK, width=MLA_KV_RANK,
                           name="kv_norm_bwd")
    dw_q_p = matmul(cqn, dq_raw, "tn", name="q_dw")
    dcqn = matmul(dq_raw, w_q_p, "nt", name="q_dx")
    dcq, dg_q = norm_bwd(proj, q_g, dcqn, out_dtype=BF, col_blk=OFF_CQ // MLA_Q_RANK, width=MLA_Q_RANK, name="q_norm_bwd")
    dy, dz, dg_ssd = gated_norm_bwd(y, proj, ssd_g, dyab)
    dxbc_c, ddt, dpar = ssd_bwd(xbc_c, proj, dt_bias, a_log, dskip, hst, dy)
    dxbc, dconvp = conv_silu_bwd(proj, OFF_XBC, SSD_CONV_CH, convp, dxbc_c, name="ssd_conv_bwd")
    pieces = [dz, dxbc, dckv, ddt, dkr, dcq]
    dw_in_p = jnp.concatenate([matmul(hn, pc, "tn", (BF,), name="sm_dwin_%d" % i) for i, pc in enumerate(pieces)], axis=1)
    third = SSD_CONV_CH // 3
    a_terms = [dz] + [KBlock(dxbc, third, i) for i in range(3)] + [dckv, ddt, dkr, dcq]
    b_terms = ([KBlock(w_in_p, SSD_D_INNER, 0)] + [KBlock(w_in_p, third, OFF_XBC // third + i) for i in range(3)]
               + [KBlock(w_in_p, MLA_KV_RANK, OFF_CKV // MLA_KV_RANK), KBlock(w_in_p, LANE, OFF_DT // LANE),
                  KBlock(w_in_p, LANE, OFF_KR // LANE), KBlock(w_in_p, MLA_Q_RANK, OFF_CQ // MLA_Q_RANK)])
    dhn = matmul(a_terms, b_terms, "nt", name="sm_dhn")
    dh, dg_pre, *below = norm_bwd(h, g_pre, dhn, dres=dh2, then=then, name="sm_pre_bwd")
    grads = dict(mix_pre_g=dg_pre, mix_post_g=dg_post, w_in=w_in_cols_to_blocks(dw_in_p), ssd_conv_w=dconvp[0:4],
                 ssd_conv_b=dconvp[4], ssd_dt_bias=dpar[0, :SSD_HEADS], ssd_a_log=dpar[1, :SSD_HEADS],
                 ssd_d=dpar[2, :SSD_HEADS], ssd_norm_g=dg_ssd, mla_q_norm_g=dg_q, mla_w_q_up=_unpack_w_q(dw_q_p),
                 mla_kv_norm_g=dg_kv, mla_w_kv_up=_unpack_w_kv(dw_kv_p), w_out_ab=dw_out)
    return dh, grads, carried, (tuple(below) or None)


W_IN_COLS = 3248
W_IN_SHARD = W_IN_COLS // N_DEV
W_IN_WIRE = 512


def _w_in_tables():
    src = np.full((IN_W,), -1, np.int64)
    src[0:2560] = np.arange(2560)
    src[OFF_CKV:OFF_CKV + 256] = 2960 + np.arange(256)
    src[OFF_DT:OFF_DT + 16] = 2560 + np.arange(16)
    src[OFF_KR + 64:OFF_KR + 96] = 3216 + np.arange(32)
    src[OFF_CQ:OFF_CQ + 384] = 2576 + np.arange(384)
    dev = np.where(src >= 0, src // W_IN_SHARD, -1).astype(np.int32).reshape(1, IN_W)
    col = np.where(src >= 0, src % W_IN_SHARD, 0).astype(np.int32).reshape(1, IN_W)
    return dev, col


W_IN_TILE = 384


def _w_in_devices_of_tile(dev):
    return [sorted(set(dev[0, t * W_IN_TILE:(t + 1) * W_IN_TILE].tolist()) - {-1}) for t in range(IN_W // W_IN_TILE)]


def _any_of(index, values):
    cond = index == values[0]
    for v in values[1:]:
        cond = cond | (index == v)
    return cond


def w_in_blocks_to_cols(g8):
    _, k, wp = g8.shape
    tn = W_IN_TILE
    dev, col = _w_in_tables()
    holders = _w_in_devices_of_tile(dev)

    def body(g_ref, dev_ref, col_ref, o_ref):
        i = pl.program_id(0)
        row = lax.broadcasted_iota(jnp.int32, (wp, tn), 0)
        o_ref[...] = jnp.zeros_like(o_ref)
        for j in range(N_DEV):
            tiles = [t for t, devs in enumerate(holders) if j in devs]
            if tiles:
                @pl.when(_any_of(i, tiles))
                def _(j=j):
                    sel = ((row == col_ref[...]) & (dev_ref[...] == j)).astype(BF)
                    o_ref[...] += jnp.dot(g_ref[j], sel, preferred_element_type=F32).astype(o_ref.dtype)

    dev, col = jnp.asarray(dev), jnp.asarray(col)
    return pl.pallas_call(
        body,
        out_shape=_sds((k, IN_W), BF),
        grid=(IN_W // tn,),
        in_specs=[pl.BlockSpec((N_DEV, k, wp), lambda i: (0, 0, 0)), pl.BlockSpec((1, tn), lambda i: (0, i)),
                  pl.BlockSpec((1, tn), lambda i: (0, i))],
        out_specs=pl.BlockSpec((k, tn), lambda i: (0, i)),
        compiler_params=_cp(("parallel",)),
        name="w_in_cols",
    )(g8, dev, col)


def w_in_cols_to_blocks(dw):
    k = dw.shape[0]
    tn = W_IN_TILE
    dev, col = _w_in_tables()
    holders = _w_in_devices_of_tile(dev)

    def body(dw_ref, dev_ref, col_ref, o_ref):
        j = pl.program_id(0)
        row = lax.broadcasted_iota(jnp.int32, (W_IN_WIRE, tn), 0)
        o_ref[...] = jnp.zeros_like(o_ref)
        for t, devs in enumerate(holders):
            if devs:
                @pl.when(_any_of(j, devs))
                def _(t=t):
                    cols = slice(t * tn, (t + 1) * tn)
                    sel = ((row == col_ref[:, cols]) & (dev_ref[:, cols] == j)).astype(BF)
                    o_ref[0] += lax.dot_general(dw_ref[:, cols], sel, NT_DIMS,
                                                preferred_element_type=F32).astype(o_ref.dtype)

    dev, col = jnp.asarray(dev), jnp.asarray(col)
    return pl.pallas_call(
        body,
        out_shape=_sds((N_DEV, k, W_IN_WIRE), BF),
        grid=(N_DEV,),
        in_specs=[pl.BlockSpec((k, IN_W), lambda j: (0, 0)), pl.BlockSpec((1, IN_W), lambda j: (0, 0)),
                  pl.BlockSpec((1, IN_W), lambda j: (0, 0))],
        out_specs=pl.BlockSpec((1, k, W_IN_WIRE), lambda j: (j, 0, 0)),
        compiler_params=_cp(("parallel",)),
        name="w_in_blocks",
    )(dw, dev, col)


def _pack_w_q(w):
    w3 = w.reshape(w.shape[0], MLA_HEADS, MLA_NOPE + MLA_ROPE)
    return jnp.pad(w3, ((0, 0), (0, 0), (0, LANE - MLA_NOPE - MLA_ROPE))).reshape(w.shape[0], MLA_HEADS * LANE)


def _unpack_w_q(p):
    return p.reshape(p.shape[0], MLA_HEADS, LANE)[:, :, :MLA_NOPE + MLA_ROPE].reshape(p.shape[0], -1)


def _pack_w_kv(w):
    w3 = w.reshape(w.shape[0], MLA_HEADS, MLA_NOPE + MLA_V)
    k = jnp.pad(w3[:, :, :MLA_NOPE], ((0, 0), (0, 0), (0, LANE - MLA_NOPE))).reshape(w.shape[0], MLA_HEADS * LANE)
    return jnp.concatenate([k, w3[:, :, MLA_NOPE:].reshape(w.shape[0], MLA_HEADS * MLA_V)], axis=1)


def _unpack_w_kv(p):
    k = p[:, :MLA_HEADS * LANE].reshape(p.shape[0], MLA_HEADS, LANE)[:, :, :MLA_NOPE]
    v = p[:, MLA_HEADS * LANE:].reshape(p.shape[0], MLA_HEADS, MLA_V)
    return jnp.concatenate([k, v], axis=2).reshape(p.shape[0], -1)


def _rows8(rows, width):
    a = jnp.concatenate([r.reshape(-1, width) for r in rows], axis=0)
    return jnp.pad(a, ((0, 8 - a.shape[0]), (0, 0)))


SLAB_ROWS = 16


def _to_slab(flat_list, lead=()):
    cat = jnp.concatenate(flat_list, axis=-1)
    n = cat.shape[-1]
    unit = SLAB_ROWS * PACK_W
    total = -(-n // unit) * unit
    cat = jnp.pad(cat, [(0, 0)] * len(lead) + [(0, total - n)])
    return cat.reshape(lead + (total // PACK_W, PACK_W))


def _from_flat(flat, shapes):
    out, off = [], 0
    for s in shapes:
        n = int(np.prod(s))
        out.append(flat[off:off + n].reshape(s))
        off += n
    return out


def _gathered_full(g8, axis):
    moved = jnp.moveaxis(g8, 0, axis)
    shp = moved.shape
    return moved.reshape(shp[:axis] + (shp[axis] * shp[axis + 1],) + shp[axis + 2:])


def _per_device(full, axis):
    shp = full.shape
    split = full.reshape(shp[:axis] + (N_DEV, shp[axis] // N_DEV) + shp[axis + 1:])
    return jnp.moveaxis(split, axis, 0)


ARG_NAMES = ['x', 'meta_tokens', 'mix_pre_g', 'mix_post_g', 'mlp_pre_g', 'mlp_post_g', 'w_up', 'w_down', 'w_in',
             'ssd_conv_w', 'ssd_conv_b', 'ssd_dt_bias', 'ssd_a_log', 'ssd_d', 'ssd_norm_g', 'mla_q_norm_g',
             'mla_w_q_up', 'mla_kv_norm_g', 'mla_w_kv_up', 'w_out_ab', 'rg_w_x', 'rg_w_y', 'rg_conv_w', 'rg_conv_b',
             'rg_w_a', 'rg_b_a', 'rg_w_i', 'rg_b_i', 'rg_lambda', 'rg_w_out']
WEIGHTS = ARG_NAMES[1:]
BIG = {'w_up': 2, 'w_down': 1, 'w_in': 2, 'mla_w_q_up': 2, 'mla_w_kv_up': 2, 'w_out_ab': 1, 'rg_w_x': 2,
       'rg_w_y': 2, 'rg_w_out': 1}
SMALL = {'meta_tokens': 1, 'ssd_conv_w': 2, 'rg_conv_w': 2, 'rg_conv_b': 1, 'rg_b_a': 1, 'rg_b_i': 1, 'rg_lambda': 1}
REPL = [n for n in WEIGHTS if n not in BIG and n not in SMALL]
REPL_MEDIUM = ['rg_w_a', 'rg_w_i']
REPL_TINY = [n for n in REPL if n not in REPL_MEDIUM]


def _piece_axes():
    axes = {}
    for n, ax in BIG.items():
        for i in range(DEPTH if n in ('w_up', 'w_down') else DEPTH // 2):
            axes[(n, i)] = ax - 1
    return axes


PIECE_AXIS = _piece_axes()
AS_BLOCKS = ('w_up', 'w_down')
_RG = lambda i: [(n, i) for n in ('rg_w_x', 'rg_w_y', 'rg_w_out')]
_MLP = lambda l: [('w_up', l), ('w_down', l)]
_SM_IN = lambda i: [(n, i) for n in ('w_in', 'mla_w_q_up', 'mla_w_kv_up')]
GATHER_FIRST = _SM_IN(0)
GATHER_AT = {0: [('w_out_ab', 0)] + _MLP(0) + _RG(0) + _MLP(1) + _SM_IN(1), 2: [('w_out_ab', 1)] + _MLP(2) + _RG(1) + _MLP(3)}
SCATTER_AT = {2: _MLP(3) + _RG(1) + _MLP(2) + [('w_out_ab', 1)],
              0: _SM_IN(1) + _MLP(1) + _RG(0) + _MLP(0) + [('w_out_ab', 0)]}
SCATTER_LAST = _SM_IN(0)


def _wire_block(p, key):
    n, i = key
    blk = p[n][i]
    if n == 'w_in':
        blk = jnp.pad(blk, ((0, 0), (0, W_IN_WIRE - blk.shape[1])))
    return blk


def _step(p, moments):
    assert DEPTH == 4
    full = {n: [None] * p[n].shape[0] for n in BIG}
    full['w_in_g'] = [None] * p['w_in'].shape[0]

    def weight_blocks(group):
        return [_wire_block(p, k).astype(BF) for k in group]

    def take_weights(group, gathered):
        for (n, i), piece in zip(group, gathered):
            if n == 'w_in':
                full['w_in_g'][i] = piece
            elif n in AS_BLOCKS:
                full[n][i] = DevBlocks(piece, PIECE_AXIS[(n, i)])
            else:
                full[n][i] = _gathered_full(piece, PIECE_AXIS[(n, i)])

    def grad_blocks(group, gw):
        return [gw[k] if k[0] in AS_BLOCKS or k[0] == 'w_in' else _per_device(gw[k], PIECE_AXIS[k]).astype(BF)
                for k in group]

    parts = {}

    small_slab = _to_slab([p[n].reshape(-1) for n in SMALL])
    *first, small8 = all_gather(weight_blocks(GATHER_FIRST) + [small_slab], name="gather_first")
    take_weights(GATHER_FIRST, first)
    for n, piece in zip(SMALL, _from_flat_rows(small8, [p[n].shape for n in SMALL])):
        full[n] = _gathered_full(piece, SMALL[n])
    for n in REPL:
        full[n] = p[n]
    loss_local, grad_x, gw, gsmall_full, carried = _local_step(
        full, p['x'][0], p['loss_target'][0],
        fwd_carry=lambda layer: Exchange("gather", weight_blocks(GATHER_AT[layer])),
        on_fwd_carried=lambda layer, got: take_weights(GATHER_AT[layer], got),
        bwd_carry=lambda layer, gw_now, others: Exchange(
            "scatter", grad_blocks(SCATTER_AT[layer], gw_now)
            + ([jnp.stack(others[n], axis=0).reshape(N_DEV, -1, LANE) for n in REPL_MEDIUM] if layer == 0 else [])))

    for layer, group in SCATTER_AT.items():
        parts.update(zip(group, carried[layer]))
    rep_flat = jnp.concatenate([gsmall_full[n].reshape(-1) for n in REPL_TINY])
    rep_n = rep_flat.shape[0]
    rep_chunk = -(-rep_n // (N_DEV * PACK_W * 8)) * PACK_W * 8
    rep8 = jnp.pad(rep_flat, (0, N_DEV * rep_chunk - rep_n)).reshape(N_DEV, rep_chunk)
    gsmall = _to_slab([_per_device(gsmall_full[n], SMALL[n]).reshape(N_DEV, -1) for n in SMALL] + [rep8], lead=(N_DEV,))
    received = all_to_all(grad_blocks(SCATTER_LAST, gw) + [gsmall], name="scatter_last")
    n_last = len(SCATTER_LAST)
    parts.update(zip(SCATTER_LAST, received[:n_last]))
    ssmall = slab_sum(received[n_last], name="sum_small").reshape(-1)
    medium_mine = [slab_sum(r8, name="sum_" + n) for n, r8 in zip(REPL_MEDIUM, carried[0][len(SCATTER_AT[0]):])]
    g_loc = {'w_in': jnp.stack([slab_sum(parts[('w_in', i)], name="sum_w_in_%d" % i)[:, :W_IN_SHARD]
                                for i in range(p['w_in'].shape[0])], axis=0)}
    small_n = sum(int(np.prod(p[n].shape)) for n in SMALL)
    g_loc.update(zip(SMALL, _from_flat(ssmall, [p[n].shape for n in SMALL])))
    rep_mine = ssmall[small_n:small_n + rep_chunk].reshape(-1, PACK_W)
    rep_all, *medium_all = all_gather([rep_mine] + medium_mine, name="gather_replicated")
    g_loc.update(zip(REPL_TINY, _from_flat(rep_all.reshape(-1), [p[n].shape for n in REPL_TINY])))
    g_loc.update({n: g.reshape(p[n].shape) for n, g in zip(REPL_MEDIUM, medium_all)})

    out = {'loss': lax.psum(loss_local, ("x", "y", "c")), 'grad_x': grad_x[None]}
    small_names = list(SMALL) + REPL_TINY
    for n in list(BIG) + REPL_MEDIUM:
        shp = p[n].shape
        if n == 'w_in' or n in REPL_MEDIUM:
            v2 = lambda a: a.reshape(-1, shp[-1])
            d, nm, nv = adamw(v2(p[n]), v2(g_loc[n]), v2(moments['m_' + n]), v2(moments['v_' + n]), name="adamw_" + n)
            d, nm, nv = d.reshape(shp), nm.reshape(shp), nv.reshape(shp)
        else:
            g_loc[n], d, nm, nv = adamw_blocks(p[n], moments['m_' + n], moments['v_' + n],
                                               [parts[(n, i)] for i in range(shp[0])], name="adamw_" + n)
        out['delta_' + n], out['new_m_' + n], out['new_v_' + n] = d, nm, nv
    slab = lambda src: _to_slab([src(n).reshape(-1) for n in small_names])
    d, nm, nv = adamw(slab(lambda n: p[n]), slab(lambda n: g_loc[n]), slab(lambda n: moments['m_' + n]),
                      slab(lambda n: moments['v_' + n]), name="adamw_small")
    shapes = [p[n].shape for n in small_names]
    for key, flat in (('delta_', d), ('new_m_', nm), ('new_v_', nv)):
        for n, a in zip(small_names, _from_flat(flat.reshape(-1), shapes)):
            out[key + n] = a
    for n in WEIGHTS:
        out['grad_' + n] = g_loc[n]
    return out


def _local_step(full, x, target_rows, fwd_carry=None, on_fwd_carried=None, bwd_carry=None):
    t = PAD + N_META + x.shape[0]
    h = jnp.concatenate([jnp.zeros((PAD, D_MODEL), F32), full['meta_tokens'], x], axis=0)
    target = jnp.concatenate([jnp.zeros((PAD + N_META, D_MODEL), F32), target_rows], axis=0)
    tabs = rope_tables(t)

    def layer_args(layer):
        i = layer // 2
        if layer % 2 == 0:
            convp = _rows8([full['ssd_conv_w'][i], full['ssd_conv_b'][i]], SSD_CONV_CH)
            return (full['mix_pre_g'][layer], full['mix_post_g'][layer], w_in_blocks_to_cols(full['w_in_g'][i]), convp,
                    full['ssd_dt_bias'][i], full['ssd_a_log'][i], full['ssd_d'][i], full['ssd_norm_g'][i],
                    full['mla_q_norm_g'][i], _pack_w_q(full['mla_w_q_up'][i]), full['mla_kv_norm_g'][i],
                    _pack_w_kv(full['mla_w_kv_up'][i]), lambda: full['w_out_ab'][i], tabs)
        rgp = _rows8([full['rg_conv_w'][i], full['rg_conv_b'][i], full['rg_b_a'][i], full['rg_b_i'][i],
                      full['rg_lambda'][i]], LRU_WIDTH)
        return (full['mix_pre_g'][layer], full['mix_post_g'][layer], full['rg_w_x'][i], full['rg_w_y'][i], rgp,
                full['rg_w_a'][i], full['rg_w_i'][i], full['rg_w_out'][i])

    def mlp_args(layer):
        return (full['mlp_pre_g'][layer], full['mlp_post_g'][layer], full['w_up'][layer], full['w_down'][layer])

    saved = []
    hn = None
    for layer in range(DEPTH):
        la = layer_args(layer)
        to_mlp = dict(hn=hn, g_next=full['mlp_pre_g'][layer])
        if layer % 2 == 0:
            if fwd_carry is not None:
                h, res_mix, hn = sm_layer_fwd(h, *la, carry=fwd_carry(layer),
                                              on_carried=lambda got, layer=layer: on_fwd_carried(layer, got), **to_mlp)
            else:
                h, res_mix, hn = sm_layer_fwd(h, *la, **to_mlp)
        else:
            h, res_mix, hn = rg_layer_fwd(h, *la, **to_mlp)
        ma = mlp_args(layer)
        h, res_mlp, hn = mlp_fwd(h, *ma, hn=hn, g_next=full['mix_pre_g'][layer + 1] if layer + 1 < DEPTH else None)
        saved.append((la, ma, res_mix, res_mlp))
    loss_local, dh = loss_fwd_bwd(h, target)

    others = {n: [None] * len(full[n]) for n in WEIGHTS if n not in BIG and n != 'meta_tokens'}
    gw, carried = {}, {}
    post = None
    for layer in reversed(range(DEPTH)):
        la, ma, res_mix, res_mlp = saved[layer]
        dh, gm, post = mlp_bwd(res_mlp, dh, *ma, post=post, then=(res_mix[-1], la[1]))
        below = (saved[layer - 1][3][-1], saved[layer - 1][1][1]) if layer > 0 else None
        if layer % 2 == 0:
            for n in ('w_up', 'w_down'):
                gw[(n, layer)] = gm[n]
            carry = None
            if bwd_carry is not None:
                carry = lambda dw_out, layer=layer: bwd_carry(layer, {**gw, ('w_out_ab', layer // 2): dw_out}, others)
            dh, gx, carried[layer], post = sm_layer_bwd(res_mix, dh, *la, carry=carry, post=post, then=below)
        else:
            dh, gx, post = rg_layer_bwd(res_mix, dh, *la, post=post, then=below)
        for n, g in list(gm.items()) + list(gx.items()):
            i = layer if n in ('mix_pre_g', 'mix_post_g', 'mlp_pre_g', 'mlp_post_g', 'w_up', 'w_down') else layer // 2
            if n in BIG:
                gw[(n, i)] = g
            else:
                others[n][i] = g
    gothers = {n: jnp.stack(v, axis=0) for n, v in others.items()}
    gothers['meta_tokens'] = dh[PAD:PAD + N_META]
    return loss_local, dh[PAD + N_META:], gw, gothers, carried


def _from_flat_rows(g8, shapes):
    flat = g8.reshape(N_DEV, -1)
    out, off = [], 0
    for s in shapes:
        n = int(np.prod(s))
        out.append(flat[:, off:off + n].reshape((N_DEV,) + tuple(s)))
        off += n
    return out


def kernel(x, meta_tokens, mix_pre_g, mix_post_g, mlp_pre_g, mlp_post_g, w_up, w_down, w_in, ssd_conv_w, ssd_conv_b, ssd_dt_bias, ssd_a_log, ssd_d, ssd_norm_g, mla_q_norm_g, mla_w_q_up, mla_kv_norm_g, mla_w_kv_up, w_out_ab, rg_w_x, rg_w_y, rg_conv_w, rg_conv_b, rg_w_a, rg_b_a, rg_w_i, rg_b_i, rg_lambda, rg_w_out, loss_target, m_meta_tokens, m_mix_pre_g, m_mix_post_g, m_mlp_pre_g, m_mlp_post_g, m_w_up, m_w_down, m_w_in, m_ssd_conv_w, m_ssd_conv_b, m_ssd_dt_bias, m_ssd_a_log, m_ssd_d, m_ssd_norm_g, m_mla_q_norm_g, m_mla_w_q_up, m_mla_kv_norm_g, m_mla_w_kv_up, m_w_out_ab, m_rg_w_x, m_rg_w_y, m_rg_conv_w, m_rg_conv_b, m_rg_w_a, m_rg_b_a, m_rg_w_i, m_rg_b_i, m_rg_lambda, m_rg_w_out, v_meta_tokens, v_mix_pre_g, v_mix_post_g, v_mlp_pre_g, v_mlp_post_g, v_w_up, v_w_down, v_w_in, v_ssd_conv_w, v_ssd_conv_b, v_ssd_dt_bias, v_ssd_a_log, v_ssd_d, v_ssd_norm_g, v_mla_q_norm_g, v_mla_w_q_up, v_mla_kv_norm_g, v_mla_w_kv_up, v_w_out_ab, v_rg_w_x, v_rg_w_y, v_rg_conv_w, v_rg_conv_b, v_rg_w_a, v_rg_b_a, v_rg_w_i, v_rg_b_i, v_rg_lambda, v_rg_w_out):
    args = (x, meta_tokens, mix_pre_g, mix_post_g, mlp_pre_g, mlp_post_g, w_up, w_down, w_in, ssd_conv_w, ssd_conv_b, ssd_dt_bias, ssd_a_log, ssd_d, ssd_norm_g, mla_q_norm_g, mla_w_q_up, mla_kv_norm_g, mla_w_kv_up, w_out_ab, rg_w_x, rg_w_y, rg_conv_w, rg_conv_b, rg_w_a, rg_b_a, rg_w_i, rg_b_i, rg_lambda, rg_w_out, loss_target, m_meta_tokens, m_mix_pre_g, m_mix_post_g, m_mlp_pre_g, m_mlp_post_g, m_w_up, m_w_down, m_w_in, m_ssd_conv_w, m_ssd_conv_b, m_ssd_dt_bias, m_ssd_a_log, m_ssd_d, m_ssd_norm_g, m_mla_q_norm_g, m_mla_w_q_up, m_mla_kv_norm_g, m_mla_w_kv_up, m_w_out_ab, m_rg_w_x, m_rg_w_y, m_rg_conv_w, m_rg_conv_b, m_rg_w_a, m_rg_b_a, m_rg_w_i, m_rg_b_i, m_rg_lambda, m_rg_w_out, v_meta_tokens, v_mix_pre_g, v_mix_post_g, v_mlp_pre_g, v_mlp_post_g, v_w_up, v_w_down, v_w_in, v_ssd_conv_w, v_ssd_conv_b, v_ssd_dt_bias, v_ssd_a_log, v_ssd_d, v_ssd_norm_g, v_mla_q_norm_g, v_mla_w_q_up, v_mla_kv_norm_g, v_mla_w_kv_up, v_w_out_ab, v_rg_w_x, v_rg_w_y, v_rg_conv_w, v_rg_conv_b, v_rg_w_a, v_rg_b_a, v_rg_w_i, v_rg_b_i, v_rg_lambda, v_rg_w_out,)
    n_w = len(ARG_NAMES)
    p = dict(zip(ARG_NAMES, args[:n_w]))
    p['loss_target'] = args[n_w]
    moments = {}
    for i, n in enumerate(WEIGHTS):
        moments['m_' + n] = args[n_w + 1 + i]
        moments['v_' + n] = args[n_w + 1 + len(WEIGHTS) + i]
    out = _step(p, moments)
    res = [out['loss'], out['grad_x']]
    for prefix in ('grad_', 'delta_', 'new_m_', 'new_v_'):
        res += [out[prefix + n] for n in WEIGHTS]
    return tuple(res)
```
